```python
import math
import jax
import jax.numpy as jnp
from jax import lax

D_MODEL = 1024
BATCH = 8
SEQ = 4096
DEPTH = 4

MEM_LEN = 256
MAX_POS_OFFSET = 4096

DN_HEADS = 4
DN_HEAD_DIM = 128
DN_KEY_DIM = DN_HEADS * DN_HEAD_DIM
DN_QKV_DIM = 3 * DN_KEY_DIM
DN_CONV = 4
DN_CHUNK = 64

SW_HEADS = 8
SW_HEAD_DIM = 64
SW_DIM = SW_HEADS * SW_HEAD_DIM
SW_BRANCHES = ((128, 1), (512, 4), (2048, 16))
SW_BLOCK = 128
ROPE_THETA = 10000.0

IN_SPLITS = (DN_QKV_DIM, DN_KEY_DIM, DN_HEADS, DN_HEADS, SW_DIM, SW_DIM, SW_DIM)
HYB_IN = sum(IN_SPLITS)
HYB_MIX = DN_KEY_DIM + SW_DIM

S5_GROUP = 16
S5_GROUPS = D_MODEL // S5_GROUP
S5_STATE = 64

X_HEADS = 4
X_HEAD_DIM = D_MODEL // X_HEADS

FFN_HIDDEN = -(-8 * D_MODEL // (3 * 256)) * 256

N_EVEN = (DEPTH + 1) // 2
N_ODD = DEPTH // 2
DEEPNORM_ALPHA = (2 * DEPTH) ** 0.25
DEEPNORM_BETA = (8 * DEPTH) ** -0.25
LN_EPS = 1e-5
RMS_EPS = 1e-6

kernel_name = 'hybrid_deltanet_dilated_s5_block'


def _split_points(sizes):
    pts, acc = [], 0
    for s in sizes[:-1]:
        acc += s
        pts.append(acc)
    return pts


def layer_norm(x, g, b):
    xf = x.astype(jnp.float32)
    mu = xf.mean(-1, keepdims=True)
    var = jnp.square(xf - mu).mean(-1, keepdims=True)
    return (xf - mu) * lax.rsqrt(var + LN_EPS) * g.astype(jnp.float32) + b.astype(jnp.float32)


def post_norm(h, sub, g, b):
    return layer_norm(DEEPNORM_ALPHA * h.astype(jnp.float32) + sub.astype(jnp.float32), g, b).astype(h.dtype)


def rms_norm(x, g):
    xf = x.astype(jnp.float32)
    return xf * lax.rsqrt(jnp.mean(xf * xf, -1, keepdims=True) + RMS_EPS) * g.astype(jnp.float32)


def l2_normalize(x):
    xf = x.astype(jnp.float32)
    return xf * lax.rsqrt(jnp.sum(xf * xf, -1, keepdims=True) + RMS_EPS)


def causal_depthwise_conv(x, w):
    k, c = w.shape
    return lax.conv_general_dilated(
        x, w.astype(x.dtype)[:, None, :], window_strides=(1,), padding=[(k - 1, 0)],
        dimension_numbers=('NWC', 'WIO', 'NWC'), feature_group_count=c)


def rope(x, pos):
    d = x.shape[-1]
    inv_freq = ROPE_THETA ** (-jnp.arange(0, d, 2, dtype=jnp.float32) / d)
    ang = pos.astype(jnp.float32)[..., None] * inv_freq
    cos, sin = jnp.cos(ang)[:, :, None, :], jnp.sin(ang)[:, :, None, :]
    xf = x.astype(jnp.float32)
    x1, x2 = xf[..., : d // 2], xf[..., d // 2:]
    return jnp.concatenate([x1 * cos - x2 * sin, x2 * cos + x1 * sin], -1).astype(x.dtype)


def gated_delta_rule(q, k, v, g, beta):
    bsz, s, h, dk = q.shape
    dv = v.shape[-1]
    c = DN_CHUNK
    n = s // c

    def chunk(t):
        return t.reshape(bsz, n, c, h, -1).transpose(0, 3, 1, 2, 4)

    q = chunk(q) * (dk ** -0.5)
    k = chunk(k)
    v = chunk(v)
    g = jnp.cumsum(chunk(g[..., None])[..., 0], axis=-1)
    beta = chunk(beta[..., None])[..., 0]
    causal = jnp.tril(jnp.ones((c, c), dtype=bool))
    strict = jnp.tril(jnp.ones((c, c), dtype=bool), -1)
    decay = jnp.exp(jnp.where(causal, g[..., :, None] - g[..., None, :], -jnp.inf))
    k_beta = k * beta[..., None]
    lower = jnp.where(strict, jnp.einsum('bhnid,bhnjd->bhnij', k_beta, k), 0.0) * decay
    eye = jnp.eye(c, dtype=jnp.float32)
    rhs = jnp.concatenate([v * beta[..., None], k_beta * jnp.exp(g)[..., None]], axis=-1)
    sol = lax.linalg.triangular_solve(lower + eye, rhs, left_side=True, lower=True,
                                      unit_diagonal=True)
    u, w = sol[..., :dv], sol[..., dv:]
    intra = jnp.einsum('bhnid,bhnjd->bhnij', q, k) * decay
    q_dec = q * jnp.exp(g)[..., None]
    g_last = g[..., -1]
    k_dec = k * jnp.exp(g_last[..., None] - g)[..., None]

    def step(state, inp):
        q_c, w_c, u_c, k_c, a_c, gl = inp
        v_new = u_c - jnp.einsum('bhcd,bhde->bhce', w_c, state)
        out = (jnp.einsum('bhcd,bhde->bhce', q_c, state)
               + jnp.einsum('bhij,bhje->bhie', a_c, v_new))
        state = state * jnp.exp(gl)[..., None, None] + jnp.einsum('bhcd,bhce->bhde', k_c, v_new)
        return state, out

    xs = tuple(jnp.moveaxis(t, 2, 0) for t in (q_dec, w, u, k_dec, intra, g_last))
    state0 = jnp.zeros((bsz, h, dk, dv), jnp.float32)
    _, out = lax.scan(step, state0, xs)
    return out.transpose(1, 0, 3, 2, 4).reshape(bsz, s, h, dv)


def gated_deltanet(qkv, z, b_logit, a_logit, conv_w, a_log, dt_bias, norm_g):
    bsz, s, _ = qkv.shape
    qkv = jax.nn.silu(causal_depthwise_conv(qkv, conv_w))
    q, k, v = jnp.split(qkv, [DN_KEY_DIM, 2 * DN_KEY_DIM], axis=-1)
    heads = lambda t: t.reshape(bsz, s, DN_HEADS, DN_HEAD_DIM)
    q = l2_normalize(heads(q))
    k = l2_normalize(heads(k))
    v = heads(v).astype(jnp.float32)
    beta = jax.nn.sigmoid(b_logit.astype(jnp.float32))
    g = -jnp.exp(a_log.astype(jnp.float32)) * jax.nn.softplus(
        a_logit.astype(jnp.float32) + dt_bias.astype(jnp.float32))
    o = gated_delta_rule(q, k, v, g, beta)
    o = rms_norm(o, norm_g) * jax.nn.silu(heads(z).astype(jnp.float32))
    return o.reshape(bsz, s, DN_KEY_DIM)


def dilated_branch(q, k, v, window, dilation):
    bsz, s, h, d = q.shape
    steps = window // dilation
    span = dilation * SW_BLOCK
    s_pad = -(-s // span) * span
    length = s_pad // dilation
    nb = length // SW_BLOCK

    def to_blocks(t):
        t = jnp.pad(t, ((0, 0), (0, s_pad - s), (0, 0), (0, 0)))
        t = t.reshape(bsz, length, dilation, h, -1).transpose(0, 2, 1, 3, 4)
        return t.reshape(bsz, dilation, nb, SW_BLOCK, h, -1)

    def with_prev(t):
        prev = jnp.pad(t, ((0, 0), (0, 0), (1, 0), (0, 0), (0, 0), (0, 0)))[:, :, :-1]
        return jnp.concatenate([prev, t], axis=3)

    def from_blocks(t):
        t = t.reshape(bsz, dilation, length, h, -1).transpose(0, 2, 1, 3, 4)
        return t.reshape(bsz, s_pad, h, -1)[:, :s]

    qb = to_blocks(q)
    kw = with_prev(to_blocks(k))
    vw = with_prev(to_blocks(v))
    scores = jnp.einsum('brnqhd,brnkhd->brnhqk', qb, kw,
                        preferred_element_type=jnp.float32) * (d ** -0.5)
    qi = jnp.arange(SW_BLOCK)[:, None] + SW_BLOCK
    kj = jnp.arange(2 * SW_BLOCK)[None, :]
    dist = qi - kj
    blk = jnp.arange(nb)[:, None, None]
    valid = (dist >= 0) & (dist <= steps) & (blk * SW_BLOCK + kj - SW_BLOCK >= 0)
    scores = jnp.where(valid[None, None, :, None], scores, -jnp.inf)
    m = scores.max(-1, keepdims=True)
    p = jnp.exp(scores - m)
    l = p.sum(-1)
    o = jnp.einsum('brnhqk,brnkhd->brnqhd', p, vw.astype(jnp.float32))
    o = o / jnp.swapaxes(l, -1, -2)[..., None]
    lse = jnp.swapaxes(m[..., 0] + jnp.log(l), -1, -2)
    return from_blocks(o), from_blocks(lse[..., None])[..., 0]


def dilated_attention(q, k, v, pos):
    q = rope(q, pos)
    k = rope(k, pos)
    outs, lses = [], []
    for window, dilation in SW_BRANCHES:
        o, lse = dilated_branch(q, k, v, window, dilation)
        outs.append(o)
        lses.append(lse)
    wts = jax.nn.softmax(jnp.stack(lses), axis=0)[..., None]
    return jnp.sum(jnp.stack(outs) * wts, axis=0)


def delta_dilated_mixer(h, positions, w_in, conv_w, a_log, dt_bias, norm_g, w_out):
    bsz, s, _ = h.shape
    proj = h @ w_in
    dn_qkv, dn_z, dn_b, dn_a, sw_q, sw_k, sw_v = jnp.split(proj, _split_points(IN_SPLITS), axis=-1)
    a_out = gated_deltanet(dn_qkv, dn_z, dn_b, dn_a, conv_w, a_log, dt_bias, norm_g)
    heads = lambda t: t.reshape(bsz, s, SW_HEADS, SW_HEAD_DIM)
    b_out = dilated_attention(heads(sw_q), heads(sw_k), heads(sw_v), positions).reshape(bsz, s, SW_DIM)
    mixed = jnp.concatenate([a_out, b_out], axis=-1).astype(h.dtype)
    return mixed @ w_out


def _lin_rec_combine(e1, e2):
    a1, b1 = e1
    a2, b2 = e2
    return a1 * a2, a2 * b1 + b2


def s5_mixer(u, a_re, a_im, log_dt, b_re, b_im, c_re, c_im, d_skip, w_o, w_g):
    bsz, s, d = u.shape
    uf = u.astype(jnp.float32)
    ug = uf.reshape(bsz, s, S5_GROUPS, S5_GROUP)
    f32 = jnp.float32
    a = lax.complex(a_re.astype(f32), a_im.astype(f32))
    dt = jnp.exp(log_dt.astype(f32))[:, None]
    a_bar = jnp.exp(a * dt)
    b_bar = ((a_bar - 1.0) / a)[..., None] * lax.complex(b_re.astype(f32), b_im.astype(f32))
    bu = jnp.einsum('gph,bsgh->bsgp', b_bar, ug.astype(jnp.complex64))
    a_seq = jnp.broadcast_to(a_bar, (1, s) + a_bar.shape)
    _, states = lax.associative_scan(_lin_rec_combine, (a_seq, bu), axis=1)
    y = (jnp.einsum('ghp,bsgp->bsgh', c_re.astype(f32), states.real)
         - jnp.einsum('ghp,bsgp->bsgh', c_im.astype(f32), states.imag))
    y = y.reshape(bsz, s, d) + d_skip.astype(f32) * uf
    hid = jax.nn.gelu(y).astype(u.dtype)
    return (hid @ w_o) * jax.nn.sigmoid(hid @ w_g)


def memory_cross_attention(h, mem, wq, wk, wv, wo):
    bsz, s, _ = h.shape
    m = mem.shape[1]
    q = (h @ wq).reshape(bsz, s, X_HEADS, X_HEAD_DIM)
    k = (mem @ wk).reshape(bsz, m, X_HEADS, X_HEAD_DIM)
    v = (mem @ wv).reshape(bsz, m, X_HEADS, X_HEAD_DIM)
    scores = jnp.einsum('bqhd,bkhd->bhqk', q, k, preferred_element_type=jnp.float32) * (X_HEAD_DIM ** -0.5)
    p = jax.nn.softmax(scores, axis=-1)
    o = jnp.einsum('bhqk,bkhd->bqhd', p, v.astype(jnp.float32)).reshape(bsz, s, D_MODEL)
    return o.astype(h.dtype) @ wo


def swiglu(h, wg, wu, wd):
    return (jax.nn.silu(h @ wg) * (h @ wu)) @ wd


def _fwd_setup_inputs(seed: int = 0) -> dict:
    key = jax.random.key(seed)
    keys = iter(jax.random.split(key, 48))
    f32 = jnp.float32

    def nrm(shape, scale):
        return jax.random.normal(next(keys), shape, f32) * scale

    def uni(shape, lo, hi):
        return jax.random.uniform(next(keys), shape, f32, lo, hi)

    def gain(shape):
        return 1.0 + nrm(shape, 0.02)

    x = nrm((BATCH, SEQ, D_MODEL), 1.0)
    mem = nrm((BATCH, MEM_LEN, D_MODEL), 1.0)
    positions = (jax.random.randint(next(keys), (BATCH, 1), 0, MAX_POS_OFFSET, dtype=jnp.int32)
                 + jnp.arange(SEQ, dtype=jnp.int32)[None, :])

    hyb_w_in = nrm((N_EVEN, D_MODEL, HYB_IN), D_MODEL ** -0.5)
    dn_conv_w = nrm((N_EVEN, DN_CONV, DN_QKV_DIM), DN_CONV ** -0.5)
    dn_a_log = jnp.log(uni((N_EVEN, DN_HEADS), 1.0, 16.0))
    dt = jnp.exp(uni((N_EVEN, DN_HEADS), math.log(1e-3), math.log(1e-1)))
    dn_dt_bias = dt + jnp.log(-jnp.expm1(-dt))
    dn_norm_g = gain((N_EVEN, DN_HEAD_DIM))
    hyb_w_out = nrm((N_EVEN, HYB_MIX, D_MODEL), HYB_MIX ** -0.5 * DEEPNORM_BETA)

    s5_a_re = -0.5 + nrm((N_ODD, S5_GROUPS, S5_STATE), 0.01)
    s5_a_im = math.pi * jnp.arange(S5_STATE, dtype=f32) + nrm((N_ODD, S5_GROUPS, S5_STATE), 0.01)
    s5_log_dt = uni((N_ODD, S5_GROUPS), math.log(1e-3), math.log(1e-1))
    s5_b_re = nrm((N_ODD, S5_GROUPS, S5_STATE, S5_GROUP), (2 * S5_GROUP) ** -0.5)
    s5_b_im = nrm((N_ODD, S5_GROUPS, S5_STATE, S5_GROUP), (2 * S5_GROUP) ** -0.5)
    s5_c_re = nrm((N_ODD, S5_GROUPS, S5_GROUP, S5_STATE), 0.5 ** 0.5)
    s5_c_im = nrm((N_ODD, S5_GROUPS, S5_GROUP, S5_STATE), 0.5 ** 0.5)
    s5_d = nrm((N_ODD, D_MODEL), 1.0)
    s5_glu_wo = nrm((N_ODD, D_MODEL, D_MODEL), D_MODEL ** -0.5 * DEEPNORM_BETA)
    s5_glu_wg = nrm((N_ODD, D_MODEL, D_MODEL), D_MODEL ** -0.5)

    ln_mix_g = gain((DEPTH, D_MODEL))
    ln_mix_b = nrm((DEPTH, D_MODEL), 0.02)
    xq_w = nrm((DEPTH, D_MODEL, D_MODEL), D_MODEL ** -0.5)
    xk_w = nrm((DEPTH, D_MODEL, D_MODEL), D_MODEL ** -0.5)
    xv_w = nrm((DEPTH, D_MODEL, D_MODEL), D_MODEL ** -0.5)
    xo_w = nrm((DEPTH, D_MODEL, D_MODEL), D_MODEL ** -0.5 * DEEPNORM_BETA)
    ln_x_g = gain((DEPTH, D_MODEL))
    ln_x_b = nrm((DEPTH, D_MODEL), 0.02)
    ffn_wg = nrm((DEPTH, D_MODEL, FFN_HIDDEN), D_MODEL ** -0.5)
    ffn_wu = nrm((DEPTH, D_MODEL, FFN_HIDDEN), D_MODEL ** -0.5)
    ffn_wd = nrm((DEPTH, FFN_HIDDEN, D_MODEL), FFN_HIDDEN ** -0.5 * DEEPNORM_BETA)
    ln_ffn_g = gain((DEPTH, D_MODEL))
    ln_ffn_b = nrm((DEPTH, D_MODEL), 0.02)

    return {
        'x': x, 'mem': mem, 'positions': positions,
        'hyb_w_in': hyb_w_in, 'dn_conv_w': dn_conv_w, 'dn_a_log': dn_a_log,
        'dn_dt_bias': dn_dt_bias, 'dn_norm_g': dn_norm_g, 'hyb_w_out': hyb_w_out,
        's5_a_re': s5_a_re, 's5_a_im': s5_a_im, 's5_log_dt': s5_log_dt,
        's5_b_re': s5_b_re, 's5_b_im': s5_b_im, 's5_c_re': s5_c_re, 's5_c_im': s5_c_im,
        's5_d': s5_d, 's5_glu_wo': s5_glu_wo, 's5_glu_wg': s5_glu_wg,
        'ln_mix_g': ln_mix_g, 'ln_mix_b': ln_mix_b,
        'xq_w': xq_w, 'xk_w': xk_w, 'xv_w': xv_w, 'xo_w': xo_w,
        'ln_x_g': ln_x_g, 'ln_x_b': ln_x_b,
        'ffn_wg': ffn_wg, 'ffn_wu': ffn_wu, 'ffn_wd': ffn_wd,
        'ln_ffn_g': ln_ffn_g, 'ln_ffn_b': ln_ffn_b,
    }


def _fwd_reference(x, mem, positions,
              hyb_w_in, dn_conv_w, dn_a_log, dn_dt_bias, dn_norm_g, hyb_w_out,
              s5_a_re, s5_a_im, s5_log_dt, s5_b_re, s5_b_im, s5_c_re, s5_c_im,
              s5_d, s5_glu_wo, s5_glu_wg,
              ln_mix_g, ln_mix_b,
              xq_w, xk_w, xv_w, xo_w, ln_x_g, ln_x_b,
              ffn_wg, ffn_wu, ffn_wd, ln_ffn_g, ln_ffn_b):
    h = x
    for layer in range(DEPTH):
        i = layer // 2
        if layer % 2 == 0:
            mix = delta_dilated_mixer(h, positions, hyb_w_in[i], dn_conv_w[i], dn_a_log[i],
                                      dn_dt_bias[i], dn_norm_g[i], hyb_w_out[i])
        else:
            mix = s5_mixer(h, s5_a_re[i], s5_a_im[i], s5_log_dt[i], s5_b_re[i], s5_b_im[i],
                           s5_c_re[i], s5_c_im[i], s5_d[i], s5_glu_wo[i], s5_glu_wg[i])
        h = post_norm(h, mix, ln_mix_g[layer], ln_mix_b[layer])
        h = post_norm(h, memory_cross_attention(h, mem, xq_w[layer], xk_w[layer], xv_w[layer], xo_w[layer]),
                      ln_x_g[layer], ln_x_b[layer])
        h = post_norm(h, swiglu(h, ffn_wg[layer], ffn_wu[layer], ffn_wd[layer]),
                      ln_ffn_g[layer], ln_ffn_b[layer])
    return h


import jax as _jax
import jax.numpy as _jnp

TWIN_FORMAT = 'train_step'
FWD_PARAMS = ['x', 'mem', 'positions', 'hyb_w_in', 'dn_conv_w', 'dn_a_log', 'dn_dt_bias', 'dn_norm_g', 'hyb_w_out', 's5_a_re', 's5_a_im', 's5_log_dt', 's5_b_re', 's5_b_im', 's5_c_re', 's5_c_im', 's5_d', 's5_glu_wo', 's5_glu_wg', 'ln_mix_g', 'ln_mix_b', 'xq_w', 'xk_w', 'xv_w', 'xo_w', 'ln_x_g', 'ln_x_b', 'ffn_wg', 'ffn_wu', 'ffn_wd', 'ln_ffn_g', 'ln_ffn_b']
TWIN_WEIGHTS = ['hyb_w_in', 'dn_conv_w', 'dn_a_log', 'dn_dt_bias', 'dn_norm_g', 'hyb_w_out', 's5_a_re', 's5_a_im', 's5_log_dt', 's5_b_re', 's5_b_im', 's5_c_re', 's5_c_im', 's5_d', 's5_glu_wo', 's5_glu_wg', 'ln_mix_g', 'ln_mix_b', 'xq_w', 'xk_w', 'xv_w', 'xo_w', 'ln_x_g', 'ln_x_b', 'ffn_wg', 'ffn_wu', 'ffn_wd', 'ln_ffn_g', 'ln_ffn_b']
TWIN_DIFF_INPUT = 'x'
TWIN_INPUTS = ['x', 'mem', 'positions', 'hyb_w_in', 'dn_conv_w', 'dn_a_log', 'dn_dt_bias', 'dn_norm_g', 'hyb_w_out', 's5_a_re', 's5_a_im', 's5_log_dt', 's5_b_re', 's5_b_im', 's5_c_re', 's5_c_im', 's5_d', 's5_glu_wo', 's5_glu_wg', 'ln_mix_g', 'ln_mix_b', 'xq_w', 'xk_w', 'xv_w', 'xo_w', 'ln_x_g', 'ln_x_b', 'ffn_wg', 'ffn_wu', 'ffn_wd', 'ln_ffn_g', 'ln_ffn_b', 'loss_target', 'm_hyb_w_in', 'm_dn_conv_w', 'm_dn_a_log', 'm_dn_dt_bias', 'm_dn_norm_g', 'm_hyb_w_out', 'm_s5_a_re', 'm_s5_a_im', 'm_s5_log_dt', 'm_s5_b_re', 'm_s5_b_im', 'm_s5_c_re', 'm_s5_c_im', 'm_s5_d', 'm_s5_glu_wo', 'm_s5_glu_wg', 'm_ln_mix_g', 'm_ln_mix_b', 'm_xq_w', 'm_xk_w', 'm_xv_w', 'm_xo_w', 'm_ln_x_g', 'm_ln_x_b', 'm_ffn_wg', 'm_ffn_wu', 'm_ffn_wd', 'm_ln_ffn_g', 'm_ln_ffn_b', 'v_hyb_w_in', 'v_dn_conv_w', 'v_dn_a_log', 'v_dn_dt_bias', 'v_dn_norm_g', 'v_hyb_w_out', 'v_s5_a_re', 'v_s5_a_im', 'v_s5_log_dt', 'v_s5_b_re', 'v_s5_b_im', 'v_s5_c_re', 'v_s5_c_im', 'v_s5_d', 'v_s5_glu_wo', 'v_s5_glu_wg', 'v_ln_mix_g', 'v_ln_mix_b', 'v_xq_w', 'v_xk_w', 'v_xv_w', 'v_xo_w', 'v_ln_x_g', 'v_ln_x_b', 'v_ffn_wg', 'v_ffn_wu', 'v_ffn_wd', 'v_ln_ffn_g', 'v_ln_ffn_b']
TWIN_OUTPUTS = ['loss', 'grad_x', 'grad_hyb_w_in', 'grad_dn_conv_w', 'grad_dn_a_log', 'grad_dn_dt_bias', 'grad_dn_norm_g', 'grad_hyb_w_out', 'grad_s5_a_re', 'grad_s5_a_im', 'grad_s5_log_dt', 'grad_s5_b_re', 'grad_s5_b_im', 'grad_s5_c_re', 'grad_s5_c_im', 'grad_s5_d', 'grad_s5_glu_wo', 'grad_s5_glu_wg', 'grad_ln_mix_g', 'grad_ln_mix_b', 'grad_xq_w', 'grad_xk_w', 'grad_xv_w', 'grad_xo_w', 'grad_ln_x_g', 'grad_ln_x_b', 'grad_ffn_wg', 'grad_ffn_wu', 'grad_ffn_wd', 'grad_ln_ffn_g', 'grad_ln_ffn_b', 'delta_hyb_w_in', 'delta_dn_conv_w', 'delta_dn_a_log', 'delta_dn_dt_bias', 'delta_dn_norm_g', 'delta_hyb_w_out', 'delta_s5_a_re', 'delta_s5_a_im', 'delta_s5_log_dt', 'delta_s5_b_re', 'delta_s5_b_im', 'delta_s5_c_re', 'delta_s5_c_im', 'delta_s5_d', 'delta_s5_glu_wo', 'delta_s5_glu_wg', 'delta_ln_mix_g', 'delta_ln_mix_b', 'delta_xq_w', 'delta_xk_w', 'delta_xv_w', 'delta_xo_w', 'delta_ln_x_g', 'delta_ln_x_b', 'delta_ffn_wg', 'delta_ffn_wu', 'delta_ffn_wd', 'delta_ln_ffn_g', 'delta_ln_ffn_b', 'new_m_hyb_w_in', 'new_m_dn_conv_w', 'new_m_dn_a_log', 'new_m_dn_dt_bias', 'new_m_dn_norm_g', 'new_m_hyb_w_out', 'new_m_s5_a_re', 'new_m_s5_a_im', 'new_m_s5_log_dt', 'new_m_s5_b_re', 'new_m_s5_b_im', 'new_m_s5_c_re', 'new_m_s5_c_im', 'new_m_s5_d', 'new_m_s5_glu_wo', 'new_m_s5_glu_wg', 'new_m_ln_mix_g', 'new_m_ln_mix_b', 'new_m_xq_w', 'new_m_xk_w', 'new_m_xv_w', 'new_m_xo_w', 'new_m_ln_x_g', 'new_m_ln_x_b', 'new_m_ffn_wg', 'new_m_ffn_wu', 'new_m_ffn_wd', 'new_m_ln_ffn_g', 'new_m_ln_ffn_b', 'new_v_hyb_w_in', 'new_v_dn_conv_w', 'new_v_dn_a_log', 'new_v_dn_dt_bias', 'new_v_dn_norm_g', 'new_v_hyb_w_out', 'new_v_s5_a_re', 'new_v_s5_a_im', 'new_v_s5_log_dt', 'new_v_s5_b_re', 'new_v_s5_b_im', 'new_v_s5_c_re', 'new_v_s5_c_im', 'new_v_s5_d', 'new_v_s5_glu_wo', 'new_v_s5_glu_wg', 'new_v_ln_mix_g', 'new_v_ln_mix_b', 'new_v_xq_w', 'new_v_xk_w', 'new_v_xv_w', 'new_v_xo_w', 'new_v_ln_x_g', 'new_v_ln_x_b', 'new_v_ffn_wg', 'new_v_ffn_wu', 'new_v_ffn_wd', 'new_v_ln_ffn_g', 'new_v_ln_ffn_b']
TWIN_LEAF_KINDS = {'loss': 'loss', 'grad_x': 'grad_x', 'grad_hyb_w_in': 'grad_w', 'grad_dn_conv_w': 'grad_w', 'grad_dn_a_log': 'grad_w', 'grad_dn_dt_bias': 'grad_w', 'grad_dn_norm_g': 'grad_w', 'grad_hyb_w_out': 'grad_w', 'grad_s5_a_re': 'grad_w', 'grad_s5_a_im': 'grad_w', 'grad_s5_log_dt': 'grad_w', 'grad_s5_b_re': 'grad_w', 'grad_s5_b_im': 'grad_w', 'grad_s5_c_re': 'grad_w', 'grad_s5_c_im': 'grad_w', 'grad_s5_d': 'grad_w', 'grad_s5_glu_wo': 'grad_w', 'grad_s5_glu_wg': 'grad_w', 'grad_ln_mix_g': 'grad_w', 'grad_ln_mix_b': 'grad_w', 'grad_xq_w': 'grad_w', 'grad_xk_w': 'grad_w', 'grad_xv_w': 'grad_w', 'grad_xo_w': 'grad_w', 'grad_ln_x_g': 'grad_w', 'grad_ln_x_b': 'grad_w', 'grad_ffn_wg': 'grad_w', 'grad_ffn_wu': 'grad_w', 'grad_ffn_wd': 'grad_w', 'grad_ln_ffn_g': 'grad_w', 'grad_ln_ffn_b': 'grad_w', 'delta_hyb_w_in': 'delta_w', 'delta_dn_conv_w': 'delta_w', 'delta_dn_a_log': 'delta_w', 'delta_dn_dt_bias': 'delta_w', 'delta_dn_norm_g': 'delta_w', 'delta_hyb_w_out': 'delta_w', 'delta_s5_a_re': 'delta_w', 'delta_s5_a_im': 'delta_w', 'delta_s5_log_dt': 'delta_w', 'delta_s5_b_re': 'delta_w', 'delta_s5_b_im': 'delta_w', 'delta_s5_c_re': 'delta_w', 'delta_s5_c_im': 'delta_w', 'delta_s5_d': 'delta_w', 'delta_s5_glu_wo': 'delta_w', 'delta_s5_glu_wg': 'delta_w', 'delta_ln_mix_g': 'delta_w', 'delta_ln_mix_b': 'delta_w', 'delta_xq_w': 'delta_w', 'delta_xk_w': 'delta_w', 'delta_xv_w': 'delta_w', 'delta_xo_w': 'delta_w', 'delta_ln_x_g': 'delta_w', 'delta_ln_x_b': 'delta_w', 'delta_ffn_wg': 'delta_w', 'delta_ffn_wu': 'delta_w', 'delta_ffn_wd': 'delta_w', 'delta_ln_ffn_g': 'delta_w', 'delta_ln_ffn_b': 'delta_w', 'new_m_hyb_w_in': 'new_m', 'new_m_dn_conv_w': 'new_m', 'new_m_dn_a_log': 'new_m', 'new_m_dn_dt_bias': 'new_m', 'new_m_dn_norm_g': 'new_m', 'new_m_hyb_w_out': 'new_m', 'new_m_s5_a_re': 'new_m', 'new_m_s5_a_im': 'new_m', 'new_m_s5_log_dt': 'new_m', 'new_m_s5_b_re': 'new_m', 'new_m_s5_b_im': 'new_m', 'new_m_s5_c_re': 'new_m', 'new_m_s5_c_im': 'new_m', 'new_m_s5_d': 'new_m', 'new_m_s5_glu_wo': 'new_m', 'new_m_s5_glu_wg': 'new_m', 'new_m_ln_mix_g': 'new_m', 'new_m_ln_mix_b': 'new_m', 'new_m_xq_w': 'new_m', 'new_m_xk_w': 'new_m', 'new_m_xv_w': 'new_m', 'new_m_xo_w': 'new_m', 'new_m_ln_x_g': 'new_m', 'new_m_ln_x_b': 'new_m', 'new_m_ffn_wg': 'new_m', 'new_m_ffn_wu': 'new_m', 'new_m_ffn_wd': 'new_m', 'new_m_ln_ffn_g': 'new_m', 'new_m_ln_ffn_b': 'new_m', 'new_v_hyb_w_in': 'new_v', 'new_v_dn_conv_w': 'new_v', 'new_v_dn_a_log': 'new_v', 'new_v_dn_dt_bias': 'new_v', 'new_v_dn_norm_g': 'new_v', 'new_v_hyb_w_out': 'new_v', 'new_v_s5_a_re': 'new_v', 'new_v_s5_a_im': 'new_v', 'new_v_s5_log_dt': 'new_v', 'new_v_s5_b_re': 'new_v', 'new_v_s5_b_im': 'new_v', 'new_v_s5_c_re': 'new_v', 'new_v_s5_c_im': 'new_v', 'new_v_s5_d': 'new_v', 'new_v_s5_glu_wo': 'new_v', 'new_v_s5_glu_wg': 'new_v', 'new_v_ln_mix_g': 'new_v', 'new_v_ln_mix_b': 'new_v', 'new_v_xq_w': 'new_v', 'new_v_xk_w': 'new_v', 'new_v_xv_w': 'new_v', 'new_v_xo_w': 'new_v', 'new_v_ln_x_g': 'new_v', 'new_v_ln_x_b': 'new_v', 'new_v_ffn_wg': 'new_v', 'new_v_ffn_wu': 'new_v', 'new_v_ffn_wd': 'new_v', 'new_v_ln_ffn_g': 'new_v', 'new_v_ln_ffn_b': 'new_v'}


def _forward(args):
    return _fwd_reference(*[args[k] for k in FWD_PARAMS])


def _output_shape():
    out = _jax.eval_shape(lambda: _forward(_fwd_setup_inputs(0)))
    return out.shape, out.dtype

N_MICROBATCH = 1
ADAM_LR = 0.001
ADAM_B1 = 0.9
ADAM_B2 = 0.999
ADAM_EPS = 1e-08
ADAM_WD = 0.01
ADAM_STEP = 10
PER_EXAMPLE_BATCH_AXIS = {'x': 0, 'mem': 0, 'positions': 0, 'loss_target': 0}
SHARED_INPUTS = []
_WEIGHT_DTYPES = {'hyb_w_in': _jnp.float32, 'dn_conv_w': _jnp.float32, 'dn_a_log': _jnp.float32, 'dn_dt_bias': _jnp.float32, 'dn_norm_g': _jnp.float32, 'hyb_w_out': _jnp.float32, 's5_a_re': _jnp.float32, 's5_a_im': _jnp.float32, 's5_log_dt': _jnp.float32, 's5_b_re': _jnp.float32, 's5_b_im': _jnp.float32, 's5_c_re': _jnp.float32, 's5_c_im': _jnp.float32, 's5_d': _jnp.float32, 's5_glu_wo': _jnp.float32, 's5_glu_wg': _jnp.float32, 'ln_mix_g': _jnp.float32, 'ln_mix_b': _jnp.float32, 'xq_w': _jnp.float32, 'xk_w': _jnp.float32, 'xv_w': _jnp.float32, 'xo_w': _jnp.float32, 'ln_x_g': _jnp.float32, 'ln_x_b': _jnp.float32, 'ffn_wg': _jnp.float32, 'ffn_wu': _jnp.float32, 'ffn_wd': _jnp.float32, 'ln_ffn_g': _jnp.float32, 'ln_ffn_b': _jnp.float32}
MOMENT_SCALE = {'hyb_w_in': 1.804900e-02, 'dn_conv_w': 2.133039e-02, 'dn_a_log': 2.804136e-01, 'dn_dt_bias': 2.709452e-01, 'dn_norm_g': 6.280119e-02, 'hyb_w_out': 4.815969e-02, 's5_a_re': 8.711548e-03, 's5_a_im': 8.821377e-03, 's5_log_dt': 6.033372e+00, 's5_b_re': 4.957193e-03, 's5_b_im': 5.088381e-03, 's5_c_re': 1.282166e-03, 's5_c_im': 1.283562e-03, 's5_d': 2.257392e-02, 's5_glu_wo': 5.964488e-02, 's5_glu_wg': 6.759593e-03, 'ln_mix_g': 1.077693e+00, 'ln_mix_b': 4.528409e-01, 'xq_w': 4.802751e-03, 'xk_w': 4.815371e-03, 'xv_w': 5.525814e-03, 'xo_w': 1.311984e-02, 'ln_x_g': 1.081919e+00, 'ln_x_b': 4.527338e-01, 'ffn_wg': 1.663578e-02, 'ffn_wu': 1.612532e-02, 'ffn_wd': 6.361475e-02, 'ln_ffn_g': 1.612911e+01, 'ln_ffn_b': 1.402553e+00}


def _to_microbatches(a, axis):
    t = _jnp.moveaxis(a, axis, 0)
    t = t.reshape((N_MICROBATCH, t.shape[0] // N_MICROBATCH) + t.shape[1:])
    return _jnp.moveaxis(t, 1, axis + 1)


def setup_inputs(seed: int = 0) -> dict:
    inp = _fwd_setup_inputs(seed)
    key = _jax.random.fold_in(_jax.random.key(seed), 7919)
    shape, _ = _output_shape()
    out = dict(inp)
    out["loss_target"] = _jax.random.normal(_jax.random.fold_in(key, 0), shape, _jnp.float32)
    for i, name in enumerate(TWIN_WEIGHTS):
        w = inp[name].astype(_jnp.float32)
        if MOMENT_SCALE is None:
            s = _jnp.sqrt(_jnp.mean(_jnp.square(w)) + 1e-30)
        else:
            s = MOMENT_SCALE[name]
        km, kv = _jax.random.split(_jax.random.fold_in(key, i + 1))
        out[name] = w
        out["m_" + name] = s * _jax.random.normal(km, w.shape, _jnp.float32)
        out["v_" + name] = (s * s) * _jax.random.uniform(kv, w.shape, _jnp.float32, 0.5, 1.5)
    if N_MICROBATCH > 1:
        for name, axis in PER_EXAMPLE_BATCH_AXIS.items():
            out[name] = _to_microbatches(out[name], axis)
    return {'x': out['x'], 'mem': out['mem'], 'positions': out['positions'], 'hyb_w_in': out['hyb_w_in'], 'dn_conv_w': out['dn_conv_w'], 'dn_a_log': out['dn_a_log'], 'dn_dt_bias': out['dn_dt_bias'], 'dn_norm_g': out['dn_norm_g'], 'hyb_w_out': out['hyb_w_out'], 's5_a_re': out['s5_a_re'], 's5_a_im': out['s5_a_im'], 's5_log_dt': out['s5_log_dt'], 's5_b_re': out['s5_b_re'], 's5_b_im': out['s5_b_im'], 's5_c_re': out['s5_c_re'], 's5_c_im': out['s5_c_im'], 's5_d': out['s5_d'], 's5_glu_wo': out['s5_glu_wo'], 's5_glu_wg': out['s5_glu_wg'], 'ln_mix_g': out['ln_mix_g'], 'ln_mix_b': out['ln_mix_b'], 'xq_w': out['xq_w'], 'xk_w': out['xk_w'], 'xv_w': out['xv_w'], 'xo_w': out['xo_w'], 'ln_x_g': out['ln_x_g'], 'ln_x_b': out['ln_x_b'], 'ffn_wg': out['ffn_wg'], 'ffn_wu': out['ffn_wu'], 'ffn_wd': out['ffn_wd'], 'ln_ffn_g': out['ln_ffn_g'], 'ln_ffn_b': out['ln_ffn_b'], 'loss_target': out['loss_target'], 'm_hyb_w_in': out['m_hyb_w_in'], 'm_dn_conv_w': out['m_dn_conv_w'], 'm_dn_a_log': out['m_dn_a_log'], 'm_dn_dt_bias': out['m_dn_dt_bias'], 'm_dn_norm_g': out['m_dn_norm_g'], 'm_hyb_w_out': out['m_hyb_w_out'], 'm_s5_a_re': out['m_s5_a_re'], 'm_s5_a_im': out['m_s5_a_im'], 'm_s5_log_dt': out['m_s5_log_dt'], 'm_s5_b_re': out['m_s5_b_re'], 'm_s5_b_im': out['m_s5_b_im'], 'm_s5_c_re': out['m_s5_c_re'], 'm_s5_c_im': out['m_s5_c_im'], 'm_s5_d': out['m_s5_d'], 'm_s5_glu_wo': out['m_s5_glu_wo'], 'm_s5_glu_wg': out['m_s5_glu_wg'], 'm_ln_mix_g': out['m_ln_mix_g'], 'm_ln_mix_b': out['m_ln_mix_b'], 'm_xq_w': out['m_xq_w'], 'm_xk_w': out['m_xk_w'], 'm_xv_w': out['m_xv_w'], 'm_xo_w': out['m_xo_w'], 'm_ln_x_g': out['m_ln_x_g'], 'm_ln_x_b': out['m_ln_x_b'], 'm_ffn_wg': out['m_ffn_wg'], 'm_ffn_wu': out['m_ffn_wu'], 'm_ffn_wd': out['m_ffn_wd'], 'm_ln_ffn_g': out['m_ln_ffn_g'], 'm_ln_ffn_b': out['m_ln_ffn_b'], 'v_hyb_w_in': out['v_hyb_w_in'], 'v_dn_conv_w': out['v_dn_conv_w'], 'v_dn_a_log': out['v_dn_a_log'], 'v_dn_dt_bias': out['v_dn_dt_bias'], 'v_dn_norm_g': out['v_dn_norm_g'], 'v_hyb_w_out': out['v_hyb_w_out'], 'v_s5_a_re': out['v_s5_a_re'], 'v_s5_a_im': out['v_s5_a_im'], 'v_s5_log_dt': out['v_s5_log_dt'], 'v_s5_b_re': out['v_s5_b_re'], 'v_s5_b_im': out['v_s5_b_im'], 'v_s5_c_re': out['v_s5_c_re'], 'v_s5_c_im': out['v_s5_c_im'], 'v_s5_d': out['v_s5_d'], 'v_s5_glu_wo': out['v_s5_glu_wo'], 'v_s5_glu_wg': out['v_s5_glu_wg'], 'v_ln_mix_g': out['v_ln_mix_g'], 'v_ln_mix_b': out['v_ln_mix_b'], 'v_xq_w': out['v_xq_w'], 'v_xk_w': out['v_xk_w'], 'v_xv_w': out['v_xv_w'], 'v_xo_w': out['v_xo_w'], 'v_ln_x_g': out['v_ln_x_g'], 'v_ln_x_b': out['v_ln_x_b'], 'v_ffn_wg': out['v_ffn_wg'], 'v_ffn_wu': out['v_ffn_wu'], 'v_ffn_wd': out['v_ffn_wd'], 'v_ln_ffn_g': out['v_ln_ffn_g'], 'v_ln_ffn_b': out['v_ln_ffn_b']}


def _loss(weights, diff, rest, loss_target):
    with _jax.named_scope("forward"):
        args = {**rest, TWIN_DIFF_INPUT: diff, **{k: w.astype(_WEIGHT_DTYPES[k]) for k, w in weights.items()}}
        y = _forward(args)
    with _jax.named_scope("loss_head"):
        err = _jnp.square(y.astype(_jnp.float32) - loss_target)
        return 0.5 * _jnp.sum(_jnp.mean(err, axis=-1)) if err.ndim else 0.5 * err


def _adamw(w, g, m, v):
    m = ADAM_B1 * m + (1.0 - ADAM_B1) * g
    v = ADAM_B2 * v + (1.0 - ADAM_B2) * _jnp.square(g)
    m_hat = m / (1.0 - ADAM_B1 ** ADAM_STEP)
    v_hat = v / (1.0 - ADAM_B2 ** ADAM_STEP)
    delta = -ADAM_LR * (m_hat / (_jnp.sqrt(v_hat) + ADAM_EPS) + ADAM_WD * w)
    return delta, m, v


def reference(x, mem, positions, hyb_w_in, dn_conv_w, dn_a_log, dn_dt_bias, dn_norm_g, hyb_w_out, s5_a_re, s5_a_im, s5_log_dt, s5_b_re, s5_b_im, s5_c_re, s5_c_im, s5_d, s5_glu_wo, s5_glu_wg, ln_mix_g, ln_mix_b, xq_w, xk_w, xv_w, xo_w, ln_x_g, ln_x_b, ffn_wg, ffn_wu, ffn_wd, ln_ffn_g, ln_ffn_b, loss_target, m_hyb_w_in, m_dn_conv_w, m_dn_a_log, m_dn_dt_bias, m_dn_norm_g, m_hyb_w_out, m_s5_a_re, m_s5_a_im, m_s5_log_dt, m_s5_b_re, m_s5_b_im, m_s5_c_re, m_s5_c_im, m_s5_d, m_s5_glu_wo, m_s5_glu_wg, m_ln_mix_g, m_ln_mix_b, m_xq_w, m_xk_w, m_xv_w, m_xo_w, m_ln_x_g, m_ln_x_b, m_ffn_wg, m_ffn_wu, m_ffn_wd, m_ln_ffn_g, m_ln_ffn_b, v_hyb_w_in, v_dn_conv_w, v_dn_a_log, v_dn_dt_bias, v_dn_norm_g, v_hyb_w_out, v_s5_a_re, v_s5_a_im, v_s5_log_dt, v_s5_b_re, v_s5_b_im, v_s5_c_re, v_s5_c_im, v_s5_d, v_s5_glu_wo, v_s5_glu_wg, v_ln_mix_g, v_ln_mix_b, v_xq_w, v_xk_w, v_xv_w, v_xo_w, v_ln_x_g, v_ln_x_b, v_ffn_wg, v_ffn_wu, v_ffn_wd, v_ln_ffn_g, v_ln_ffn_b):
    given = dict(x=x, mem=mem, positions=positions, hyb_w_in=hyb_w_in, dn_conv_w=dn_conv_w, dn_a_log=dn_a_log, dn_dt_bias=dn_dt_bias, dn_norm_g=dn_norm_g, hyb_w_out=hyb_w_out, s5_a_re=s5_a_re, s5_a_im=s5_a_im, s5_log_dt=s5_log_dt, s5_b_re=s5_b_re, s5_b_im=s5_b_im, s5_c_re=s5_c_re, s5_c_im=s5_c_im, s5_d=s5_d, s5_glu_wo=s5_glu_wo, s5_glu_wg=s5_glu_wg, ln_mix_g=ln_mix_g, ln_mix_b=ln_mix_b, xq_w=xq_w, xk_w=xk_w, xv_w=xv_w, xo_w=xo_w, ln_x_g=ln_x_g, ln_x_b=ln_x_b, ffn_wg=ffn_wg, ffn_wu=ffn_wu, ffn_wd=ffn_wd, ln_ffn_g=ln_ffn_g, ln_ffn_b=ln_ffn_b, loss_target=loss_target, m_hyb_w_in=m_hyb_w_in, m_dn_conv_w=m_dn_conv_w, m_dn_a_log=m_dn_a_log, m_dn_dt_bias=m_dn_dt_bias, m_dn_norm_g=m_dn_norm_g, m_hyb_w_out=m_hyb_w_out, m_s5_a_re=m_s5_a_re, m_s5_a_im=m_s5_a_im, m_s5_log_dt=m_s5_log_dt, m_s5_b_re=m_s5_b_re, m_s5_b_im=m_s5_b_im, m_s5_c_re=m_s5_c_re, m_s5_c_im=m_s5_c_im, m_s5_d=m_s5_d, m_s5_glu_wo=m_s5_glu_wo, m_s5_glu_wg=m_s5_glu_wg, m_ln_mix_g=m_ln_mix_g, m_ln_mix_b=m_ln_mix_b, m_xq_w=m_xq_w, m_xk_w=m_xk_w, m_xv_w=m_xv_w, m_xo_w=m_xo_w, m_ln_x_g=m_ln_x_g, m_ln_x_b=m_ln_x_b, m_ffn_wg=m_ffn_wg, m_ffn_wu=m_ffn_wu, m_ffn_wd=m_ffn_wd, m_ln_ffn_g=m_ln_ffn_g, m_ln_ffn_b=m_ln_ffn_b, v_hyb_w_in=v_hyb_w_in, v_dn_conv_w=v_dn_conv_w, v_dn_a_log=v_dn_a_log, v_dn_dt_bias=v_dn_dt_bias, v_dn_norm_g=v_dn_norm_g, v_hyb_w_out=v_hyb_w_out, v_s5_a_re=v_s5_a_re, v_s5_a_im=v_s5_a_im, v_s5_log_dt=v_s5_log_dt, v_s5_b_re=v_s5_b_re, v_s5_b_im=v_s5_b_im, v_s5_c_re=v_s5_c_re, v_s5_c_im=v_s5_c_im, v_s5_d=v_s5_d, v_s5_glu_wo=v_s5_glu_wo, v_s5_glu_wg=v_s5_glu_wg, v_ln_mix_g=v_ln_mix_g, v_ln_mix_b=v_ln_mix_b, v_xq_w=v_xq_w, v_xk_w=v_xk_w, v_xv_w=v_xv_w, v_xo_w=v_xo_w, v_ln_x_g=v_ln_x_g, v_ln_x_b=v_ln_x_b, v_ffn_wg=v_ffn_wg, v_ffn_wu=v_ffn_wu, v_ffn_wd=v_ffn_wd, v_ln_ffn_g=v_ln_ffn_g, v_ln_ffn_b=v_ln_ffn_b)
    weights = {n: given[n] for n in TWIN_WEIGHTS}
    shared = {n: given[n] for n in SHARED_INPUTS}
    per_example = {n: given[n] for n in ['x', 'mem', 'positions']}
    grad_fn = _jax.value_and_grad(_loss, argnums=(0, 1))

    def one_microbatch(ex, loss_target):
        ex = dict(ex)
        diff = ex.pop(TWIN_DIFF_INPUT)
        return grad_fn(weights, diff, {**shared, **ex}, loss_target)

    if N_MICROBATCH == 1:
        loss, (grad_w, grad_x) = one_microbatch(per_example, given["loss_target"])
    else:
        def body(carry, xs):
            loss_sum, grad_sum = carry
            l_k, (gw_k, gx_k) = one_microbatch(xs[0], xs[1])
            with _jax.named_scope("update"):
                return (loss_sum + l_k, _jax.tree.map(_jnp.add, grad_sum, gw_k)), gx_k

        init = (_jnp.zeros((), _jnp.float32), _jax.tree.map(_jnp.zeros_like, weights))
        (loss, grad_w), grad_x = _jax.lax.scan(body, init, (per_example, given["loss_target"]))
    with _jax.named_scope("update"):
        delta_w, new_m, new_v = {}, {}, {}
        for n in TWIN_WEIGHTS:
            delta_w[n], new_m[n], new_v[n] = _adamw(weights[n], grad_w[n], given["m_" + n], given["v_" + n])
    return (loss, grad_x, *[grad_w[n] for n in TWIN_WEIGHTS], *[delta_w[n] for n in TWIN_WEIGHTS],
            *[new_m[n] for n in TWIN_WEIGHTS], *[new_v[n] for n in TWIN_WEIGHTS])
```

```python
import functools
import math

import jax
import jax.numpy as jnp
from jax import lax
from jax.experimental import pallas as pl
from jax.experimental.pallas import tpu as pltpu

f32 = jnp.float32
bf16 = jnp.bfloat16

D_MODEL = 1024
DEPTH = 4
DN_HEADS = 4
DN_HEAD_DIM = 128
DN_KEY_DIM = 512
DN_QKV_DIM = 1536
DN_CONV = 4
SW_HEADS = 8
SW_HEAD_DIM = 64
SW_DIM = 512
SW_DILATIONS = (1, 4, 16)
SW_BLOCK = 128
ROPE_THETA = 10000.0
S5_GROUP = 16
S5_GROUPS = 64
S5_STATE = 64
X_HEADS = 4
X_HEAD_DIM = 256
FFN_HIDDEN = 2816
ALPHA = (2 * DEPTH) ** 0.25
LN_EPS = 1e-5
RMS_EPS = 1e-6
ADAM_LR, ADAM_B1, ADAM_B2, ADAM_EPS, ADAM_WD, ADAM_STEP = 0.001, 0.9, 0.999, 1e-08, 0.01, 10

BA_PAD = 256
PROJ_COLS = DN_QKV_DIM + DN_KEY_DIM + 3 * SW_DIM + BA_PAD
COL_Z = DN_QKV_DIM
COL_SWQ = COL_Z + DN_KEY_DIM
COL_SWK = COL_SWQ + SW_DIM
COL_SWV = COL_SWK + SW_DIM
COL_BA = COL_SWV + SW_DIM

LANES = 128
SUBLANES = 8
VMEM_LIMIT = 56 * 1024 * 1024
DN_CHUNK = 128


def _cparams(*sem):
    return pltpu.CompilerParams(dimension_semantics=tuple(sem), vmem_limit_bytes=VMEM_LIMIT)


def _dg(x, y, cx, cy):
    return lax.dot_general(x, y, (((cx,), (cy,)), ((), ())), preferred_element_type=f32)


@functools.partial(jax.custom_vjp, nondiff_argnums=(2, 3))
def bdot(a, b, ca, cb):
    return _dg(a.astype(bf16), b.astype(bf16), ca, cb)


def _bdot_fwd(a, b, ca, cb):
    return bdot(a, b, ca, cb), (a, b)


def _bdot_bwd(ca, cb, res, g):
    a, b = res
    g16, a16, b16 = g.astype(bf16), a.astype(bf16), b.astype(bf16)
    da = _dg(g16, b16, 1, 1 - cb) if ca == 1 else _dg(b16, g16, 1 - cb, 1)
    db = _dg(a16, g16, 1 - ca, 0) if cb == 0 else _dg(g16, a16, 0, 1 - ca)
    return da.astype(a.dtype), db.astype(b.dtype)


bdot.defvjp(_bdot_fwd, _bdot_bwd)


def hdot(a, b):
    return jnp.dot(a, b, precision=lax.Precision.HIGHEST, preferred_element_type=f32)


def _iota2(shape, dim):
    return lax.broadcasted_iota(jnp.int32, shape, dim)


def _row_spec(r, tm):
    if isinstance(r, tuple):
        arr, width, blk = r
        return arr, pl.BlockSpec((tm, width), lambda i, _b=blk: (i, _b))
    return r, pl.BlockSpec((tm, r.shape[1]), lambda i: (i, 0))


def _par_spec(p):
    return pl.BlockSpec(p.shape, lambda i, _n=p.ndim: (0,) * _n)


def rowmap(name, fn, rows, params, out_cols, tm, out_dtypes=None):
    arrs, specs = zip(*[_row_spec(r, tm) for r in rows])
    s = arrs[0].shape[0]
    n_in = len(rows) + len(params)
    out_dtypes = out_dtypes or [f32] * len(out_cols)

    def body(*refs):
        outs = fn(*[r[...] for r in refs[:n_in]])
        for o_ref, o in zip(refs[n_in:], outs):
            o_ref[...] = o.astype(o_ref.dtype)

    return pl.pallas_call(
        body, grid=(s // tm,),
        in_specs=list(specs) + [_par_spec(p) for p in params],
        out_specs=[pl.BlockSpec((tm, c), lambda i: (i, 0)) for c in out_cols],
        out_shape=[jax.ShapeDtypeStruct((s, c), dt) for c, dt in zip(out_cols, out_dtypes)],
        compiler_params=_cparams("parallel"), name=name)(*arrs, *params)


def rowmap_bwd(name, fn, rows, params, cts, tm, row_mask=None, par_mask=None):
    arrs, specs = zip(*[_row_spec(r, tm) for r in rows])
    s = arrs[0].shape[0]
    ct_groups = [c if isinstance(c, list) else [c] for c in cts]
    ct_arrs, ct_specs = zip(*[_row_spec(a, tm) for grp in ct_groups for a in grp])
    cts = list(ct_arrs)
    nr, npar, nct = len(rows), len(params), len(cts)
    row_mask = row_mask or [True] * nr
    par_mask = par_mask or [True] * npar
    row_idx = [k for k in range(nr) if row_mask[k]]
    par_idx = [k for k in range(npar) if par_mask[k]]
    row_w = [specs[k].block_shape[1] for k in row_idx]

    def body(*refs):
        ins = [r[...] for r in refs[:nr + npar]]
        ct_refs = list(refs[nr + npar:nr + npar + nct])
        ctv = []
        for grp in ct_groups:
            acc = ct_refs.pop(0)[...]
            for _ in grp[1:]:
                acc = acc + ct_refs.pop(0)[...]
            ctv.append(acc)
        ctv = tuple(ctv)
        outs = refs[nr + npar + nct:]
        _, vjp = jax.vjp(fn, *ins)
        grads = vjp(ctv)
        for o_ref, k in zip(outs[:len(row_idx)], row_idx):
            o_ref[...] = grads[k].astype(o_ref.dtype)
        first = pl.program_id(0) == 0
        for o_ref, k in zip(outs[len(row_idx):], par_idx):
            g = grads[nr + k].astype(f32)

            @pl.when(first)
            def _(o_ref=o_ref, g=g):
                o_ref[...] = g

            @pl.when(jnp.logical_not(first))
            def _(o_ref=o_ref, g=g):
                o_ref[...] += g

    res = pl.pallas_call(
        body, grid=(s // tm,),
        in_specs=list(specs) + [_par_spec(p) for p in params]
        + list(ct_specs),
        out_specs=[pl.BlockSpec((tm, w), lambda i: (i, 0)) for w in row_w]
        + [_par_spec(params[k]) for k in par_idx],
        out_shape=[jax.ShapeDtypeStruct((s, w), f32) for w in row_w]
        + [jax.ShapeDtypeStruct(params[k].shape, f32) for k in par_idx],
        compiler_params=_cparams("arbitrary"), name=name)(*arrs, *params, *cts)
    return list(res[:len(row_idx)]), list(res[len(row_idx):])


def _pick(n, prefs):
    for t in prefs:
        if n % t == 0:
            return t
    return n


def mm_nn(name, a, b, out_dtype=f32):
    m, k = a.shape
    n = b.shape[1]
    tm, tn = _pick(m, (512, 256, 128)), _pick(n, (512, 256, 128))

    def body(a_ref, b_ref, o_ref):
        o_ref[...] = _dg(a_ref[...].astype(bf16), b_ref[...].astype(bf16), 1, 0).astype(o_ref.dtype)

    return pl.pallas_call(
        body, grid=(m // tm, n // tn),
        in_specs=[pl.BlockSpec((tm, k), lambda i, j: (i, 0)), pl.BlockSpec((k, tn), lambda i, j: (0, j))],
        out_specs=pl.BlockSpec((tm, tn), lambda i, j: (i, j)),
        out_shape=jax.ShapeDtypeStruct((m, n), out_dtype),
        compiler_params=_cparams("parallel", "parallel"), name=name)(a, b)


def mm_nt(name, a, b, out_dtype=f32):
    m, n = a.shape
    k = b.shape[0]
    tm, tk = _pick(m, (256, 128)), _pick(k, (512, 256, 128))

    def body(a_ref, b_ref, o_ref):
        o_ref[...] = _dg(a_ref[...].astype(bf16), b_ref[...].astype(bf16), 1, 1).astype(o_ref.dtype)

    return pl.pallas_call(
        body, grid=(m // tm, k // tk),
        in_specs=[pl.BlockSpec((tm, n), lambda i, j: (i, 0)), pl.BlockSpec((tk, n), lambda i, j: (j, 0))],
        out_specs=pl.BlockSpec((tm, tk), lambda i, j: (i, j)),
        out_shape=jax.ShapeDtypeStruct((m, k), out_dtype),
        compiler_params=_cparams("parallel", "parallel"), name=name)(a, b)


def mm_tn(name, a, b, out_dtype=f32):
    s, m = a.shape
    n = b.shape[1]
    tm, tn = _pick(m, (256, 128)), _pick(n, (256, 128))

    def body(a_ref, b_ref, o_ref):
        o_ref[...] = _dg(a_ref[...].astype(bf16), b_ref[...].astype(bf16), 0, 0).astype(o_ref.dtype)

    return pl.pallas_call(
        body, grid=(m // tm, n // tn),
        in_specs=[pl.BlockSpec((s, tm), lambda i, j: (0, i)), pl.BlockSpec((s, tn), lambda i, j: (0, j))],
        out_specs=pl.BlockSpec((tm, tn), lambda i, j: (i, j)),
        out_shape=jax.ShapeDtypeStruct((m, n), out_dtype),
        compiler_params=_cparams("parallel", "parallel"), name=name)(a, b)


def _postnorm_tile(h, sub, g, b):
    z = ALPHA * h + sub
    mu = jnp.mean(z, -1, keepdims=True)
    zc = z - mu
    var = jnp.mean(zc * zc, -1, keepdims=True)
    return (zc * lax.rsqrt(var + LN_EPS) * g + b,)


def _swiglu_tile(au):
    a, u = au[:, :FFN_HIDDEN], au[:, FFN_HIDDEN:]
    return (jax.nn.silu(a) * u,)


def _glu_tile(og):
    o, g = og[:, :D_MODEL], og[:, D_MODEL:]
    return (o * jax.nn.sigmoid(g),)


def _xattn_tile(q, kv):
    outs = []
    for h in range(X_HEADS):
        sl = slice(h * X_HEAD_DIM, (h + 1) * X_HEAD_DIM)
        s = bdot(q[:, sl], kv[:, sl], 1, 1) * (X_HEAD_DIM ** -0.5)
        m = lax.stop_gradient(jnp.max(s, -1, keepdims=True))
        p = jnp.exp(s - m)
        p = p / jnp.sum(p, -1, keepdims=True)
        outs.append(bdot(p, kv[:, D_MODEL + h * X_HEAD_DIM:D_MODEL + (h + 1) * X_HEAD_DIM], 1, 0))
    return (jnp.concatenate(outs, -1),)


TM_ROW = 256


def postnorm_fwd(tag, h, sub, g, b):
    return rowmap("postnorm_" + tag, _postnorm_tile, [h, sub], [g, b], [D_MODEL], TM_ROW)[0]


def postnorm_bwd(tag, h, sub, g, b, dy):
    (dh, dsub), (dg, db) = rowmap_bwd("postnorm_bwd_" + tag, _postnorm_tile, [h, sub], [g, b], [dy], TM_ROW)
    return dh, dsub, dg, db


def xattn_fwd(tag, h, mem, wq, wkv, wo):
    q = mm_nn("xq_" + tag, h, wq)
    kv = mm_nn("xkv_" + tag, mem, wkv)
    ao = rowmap("xattn_" + tag, _xattn_tile, [q], [kv], [D_MODEL], TM_ROW)[0]
    out = mm_nn("xo_" + tag, ao, wo)
    return out, (q, kv, ao)


def xattn_bwd(tag, h, mem, wq, wkv, wo, res, dout):
    q, kv, ao = res
    dwo = mm_tn("xo_dw_" + tag, ao, dout)
    dao = mm_nt("xo_dx_" + tag, dout, wo)
    (dq,), (dkv,) = rowmap_bwd("xattn_bwd_" + tag, _xattn_tile, [q], [kv], [dao], TM_ROW)
    dwq = mm_tn("xq_dw_" + tag, h, dq)
    dh = mm_nt("xq_dx_" + tag, dq, wq)
    dwkv = mm_tn("xkv_dw_" + tag, mem, dkv)
    return dh, dwq, dwkv, dwo


def ffn_fwd(tag, h, wgu, wd):
    au = mm_nn("ffn_gu_" + tag, h, wgu)
    hid = rowmap("ffn_act_" + tag, _swiglu_tile, [au], [], [FFN_HIDDEN], TM_ROW)[0]
    out = mm_nn("ffn_d_" + tag, hid, wd)
    return out, (au, hid)


def ffn_bwd(tag, h, wgu, wd, res, dout):
    au, hid = res
    dwd = mm_tn("ffn_d_dw_" + tag, hid, dout)
    dhid = mm_nt("ffn_d_dx_" + tag, dout, wd)
    (dau,), _ = rowmap_bwd("ffn_act_bwd_" + tag, _swiglu_tile, [au], [], [dhid], TM_ROW)
    dwgu = mm_tn("ffn_gu_dw_" + tag, h, dau)
    dh = mm_nt("ffn_gu_dx_" + tag, dau, wgu)
    return dh, dwgu, dwd


def loss_head(y, target):
    s, d = y.shape
    tm = TM_ROW

    def body(y_ref, t_ref, part_ref, dy_ref):
        e = y_ref[...] - t_ref[...]
        dy_ref[...] = e * (1.0 / d)
        p = jnp.sum(e * e, 0, keepdims=True) * (0.5 / d)

        @pl.when(pl.program_id(0) == 0)
        def _():
            part_ref[...] = p

        @pl.when(pl.program_id(0) != 0)
        def _():
            part_ref[...] += p

    return pl.pallas_call(
        body, grid=(s // tm,),
        in_specs=[pl.BlockSpec((tm, d), lambda i: (i, 0))] * 2,
        out_specs=[pl.BlockSpec((1, d), lambda i: (0, 0)), pl.BlockSpec((tm, d), lambda i: (i, 0))],
        out_shape=[jax.ShapeDtypeStruct((1, d), f32), jax.ShapeDtypeStruct((s, d), f32)],
        compiler_params=_cparams("arbitrary"), name="loss_head")(y, target)


TM_CONV = 512


def _conv_rows(xx, w_ref, n_rows):
    a = w_ref[3:4, :] * xx
    for k in (1, 2, 3):
        a = a + w_ref[3 - k:4 - k, :] * pltpu.roll(xx, k, 0)
    return a


def _dn_act(a, is_qk):
    s = jax.nn.silu(a)
    n = s * lax.rsqrt(jnp.sum(s * s, -1, keepdims=True) + RMS_EPS)
    return jnp.where(is_qk, n, s)


def dn_conv_fwd(tag, proj, cw):
    s = proj.shape[0]
    tm, hb = TM_CONV, TM_CONV // SUBLANES

    def body(xh_ref, x_ref, w_ref, o_ref):
        j, t = pl.program_id(0), pl.program_id(1)
        halo = jnp.where(t > 0, xh_ref[...], 0.0)
        xx = jnp.concatenate([halo, x_ref[...]], 0)
        a = _conv_rows(xx, w_ref, tm + SUBLANES)
        o_ref[...] = _dn_act(a, j < 2 * DN_HEADS)[SUBLANES:, :]

    return pl.pallas_call(
        body, grid=(DN_QKV_DIM // LANES, s // tm),
        in_specs=[pl.BlockSpec((SUBLANES, LANES), lambda j, t: (jnp.maximum(t * hb - 1, 0), j)),
                  pl.BlockSpec((tm, LANES), lambda j, t: (t, j)),
                  pl.BlockSpec((DN_CONV, LANES), lambda j, t: (0, j))],
        out_specs=pl.BlockSpec((tm, LANES), lambda j, t: (t, j)),
        out_shape=jax.ShapeDtypeStruct((s, DN_QKV_DIM), f32),
        compiler_params=_cparams("parallel", "parallel"), name="dn_conv_" + tag)(proj, proj, cw)


def dn_conv_bwd(tag, proj, cw, dy):
    s = proj.shape[0]
    tm, hb = TM_CONV, TM_CONV // SUBLANES
    nt = s // tm
    n_ext = tm + 2 * SUBLANES

    def body(xb_ref, x_ref, xa_ref, dy_ref, dya_ref, w_ref, dx_ref, dw_ref):
        j, t = pl.program_id(0), pl.program_id(1)
        xx = jnp.concatenate([jnp.where(t > 0, xb_ref[...], 0.0), x_ref[...],
                              jnp.where(t < nt - 1, xa_ref[...], 0.0)], 0)
        dyy = jnp.concatenate([jnp.zeros((SUBLANES, LANES), f32), dy_ref[...],
                               jnp.where(t < nt - 1, dya_ref[...], 0.0)], 0)
        a = _conv_rows(xx, w_ref, n_ext)
        _, vjp = jax.vjp(lambda v: _dn_act(v, j < 2 * DN_HEADS), a)
        da, = vjp(dyy)
        dx = w_ref[3:4, :] * da
        for k in (1, 2, 3):
            dx = dx + w_ref[3 - k:4 - k, :] * pltpu.roll(da, n_ext - k, 0)
        dx_ref[...] = dx[SUBLANES:SUBLANES + tm, :]
        row = _iota2((n_ext, LANES), 0)
        da_in = jnp.where((row >= SUBLANES) & (row < SUBLANES + tm), da, 0.0)
        r8 = _iota2((SUBLANES, LANES), 0)
        dw = jnp.zeros((SUBLANES, LANES), f32)
        for k in range(DN_CONV):
            xs = xx if k == 0 else pltpu.roll(xx, k, 0)
            dw = dw + jnp.where(r8 == 3 - k, jnp.sum(da_in * xs, 0, keepdims=True), 0.0)

        @pl.when(t == 0)
        def _():
            dw_ref[...] = dw

        @pl.when(t != 0)
        def _():
            dw_ref[...] += dw

    nb8 = s // SUBLANES
    return pl.pallas_call(
        body, grid=(DN_QKV_DIM // LANES, nt),
        in_specs=[pl.BlockSpec((SUBLANES, LANES), lambda j, t: (jnp.maximum(t * hb - 1, 0), j)),
                  pl.BlockSpec((tm, LANES), lambda j, t: (t, j)),
                  pl.BlockSpec((SUBLANES, LANES), lambda j, t: (jnp.minimum((t + 1) * hb, nb8 - 1), j)),
                  pl.BlockSpec((tm, LANES), lambda j, t: (t, j)),
                  pl.BlockSpec((SUBLANES, LANES), lambda j, t: (jnp.minimum((t + 1) * hb, nb8 - 1), j)),
                  pl.BlockSpec((DN_CONV, LANES), lambda j, t: (0, j))],
        out_specs=[pl.BlockSpec((tm, LANES), lambda j, t: (t, j)),
                   pl.BlockSpec((SUBLANES, LANES), lambda j, t: (0, j))],
        out_shape=[jax.ShapeDtypeStruct((s, DN_QKV_DIM), f32), jax.ShapeDtypeStruct((SUBLANES, DN_QKV_DIM), f32)],
        compiler_params=_cparams("parallel", "arbitrary"), name="dn_conv_bwd_" + tag)(proj, proj, proj, dy, dy, cw)


def _gate_tile(ba, eb, ea, alog, dtb):
    beta = jax.nn.sigmoid(hdot(ba, eb))
    g = -jnp.exp(alog) * jax.nn.softplus(hdot(ba, ea) + dtb)
    return beta, g


def _tri_inv_unit(lower):
    c = lower.shape[0]
    r, col = _iota2((c, c), 0), _iota2((c, c), 1)
    inv = jnp.where(r == col, 1.0, 0.0).astype(f32)
    sh = 0
    while (1 << sh) < c:
        same_2b = lax.shift_right_logical(r, sh + 1) == lax.shift_right_logical(col, sh + 1)
        diff_b = lax.shift_right_logical(r, sh) != lax.shift_right_logical(col, sh)
        off = jnp.where(same_2b & diff_b, lower, 0.0)
        inv = inv - hdot(hdot(inv, off), inv)
        sh += 1
    return inv


def _delta_chunk(q, k, v, gb, betab, state):
    c = DN_CHUNK
    r, col = _iota2((c, c), 0), _iota2((c, c), 1)
    causal, strict = r >= col, r > col
    gc = hdot(jnp.where(causal, 1.0, 0.0).astype(f32), gb)
    diff = gc - gc.T
    decay = jnp.where(causal, jnp.exp(jnp.where(causal, diff, 0.0)), 0.0)
    qs = q * (DN_HEAD_DIM ** -0.5)
    kb = k * betab
    lower = jnp.where(strict, bdot(kb, k, 1, 1), 0.0) * decay
    tinv = _tri_inv_unit(lower)
    eg = jnp.exp(gc)
    u = hdot(tinv, v * betab)
    w = hdot(tinv, kb * eg)
    intra = bdot(qs, k, 1, 1) * decay
    gl = jnp.sum(jnp.where(r == c - 1, gc, 0.0), 0, keepdims=True)
    k_dec = k * jnp.exp(gl - gc)
    v_new = u - bdot(w, state, 1, 0)
    out = bdot(qs * eg, state, 1, 0) + bdot(intra, v_new, 1, 0)
    new_state = state * jnp.exp(gl) + bdot(k_dec, v_new, 0, 0)
    return out, new_state


def delta_fwd(tag, qkv, gb, betab):
    s = qkv.shape[0]
    c, hd = DN_CHUNK, DN_HEAD_DIM
    n = s // c

    def body(q_ref, k_ref, v_ref, g_ref, b_ref, o_ref, st_ref, state):
        @pl.when(pl.program_id(1) == 0)
        def _():
            state[...] = jnp.zeros_like(state)

        st = state[...]
        st_ref[...] = st
        out, new = _delta_chunk(q_ref[...], k_ref[...], v_ref[...], g_ref[...], b_ref[...], st)
        o_ref[...] = out
        state[...] = new

    blk = lambda off: pl.BlockSpec((c, hd), lambda h, i, _o=off: (i, h + _o))
    return pl.pallas_call(
        body, grid=(DN_HEADS, n),
        in_specs=[blk(0), blk(DN_HEADS), blk(2 * DN_HEADS), blk(0), blk(0)],
        out_specs=[blk(0), pl.BlockSpec((None, None, hd, hd), lambda h, i: (h, i, 0, 0))],
        out_shape=[jax.ShapeDtypeStruct((s, DN_KEY_DIM), f32), jax.ShapeDtypeStruct((DN_HEADS, n, hd, hd), f32)],
        scratch_shapes=[pltpu.VMEM((hd, hd), f32)],
        compiler_params=_cparams("parallel", "arbitrary"), name="delta_" + tag)(qkv, qkv, qkv, gb, betab)


def delta_bwd(tag, qkv, gb, betab, states, do):
    s = qkv.shape[0]
    c, hd = DN_CHUNK, DN_HEAD_DIM
    n = s // c

    def body(q_ref, k_ref, v_ref, g_ref, b_ref, st_ref, do_ref, dq_ref, dk_ref, dv_ref, dg_ref, db_ref, dstate):
        @pl.when(pl.program_id(1) == 0)
        def _():
            dstate[...] = jnp.zeros_like(dstate)

        _, vjp = jax.vjp(_delta_chunk, q_ref[...], k_ref[...], v_ref[...], g_ref[...], b_ref[...], st_ref[...])
        dq, dk, dv, dg, db, dst = vjp((do_ref[...], dstate[...]))
        dq_ref[...], dk_ref[...], dv_ref[...], dg_ref[...], db_ref[...] = dq, dk, dv, dg, db
        dstate[...] = dst

    blk = lambda off: pl.BlockSpec((c, hd), lambda h, i, _o=off: (n - 1 - i, h + _o))
    return pl.pallas_call(
        body, grid=(DN_HEADS, n),
        in_specs=[blk(0), blk(DN_HEADS), blk(2 * DN_HEADS), blk(0), blk(0),
                  pl.BlockSpec((None, None, hd, hd), lambda h, i: (h, n - 1 - i, 0, 0)), blk(0)],
        out_specs=[blk(0)] * 5,
        out_shape=[jax.ShapeDtypeStruct((s, DN_KEY_DIM), f32)] * 5,
        scratch_shapes=[pltpu.VMEM((hd, hd), f32)],
        compiler_params=_cparams("parallel", "arbitrary"), name="delta_bwd_" + tag)(qkv, qkv, qkv, gb, betab, states, do)


def _dn_out_tile(o, z, ng):
    outs = []
    for h in range(DN_HEADS):
        sl = slice(h * DN_HEAD_DIM, (h + 1) * DN_HEAD_DIM)
        oh = o[:, sl]
        nrm = oh * lax.rsqrt(jnp.mean(oh * oh, -1, keepdims=True) + RMS_EPS) * ng[:, sl]
        outs.append(nrm * jax.nn.silu(z[:, sl]))
    return (jnp.concatenate(outs, -1),)


def _head_selectors():
    r, c = _iota2((BA_PAD, DN_KEY_DIM), 0), _iota2((BA_PAD, DN_KEY_DIM), 1) // DN_HEAD_DIM
    return (r == c).astype(f32), (r == c + DN_HEADS).astype(f32)


def dn_mixer_fwd(tag, proj, cw, alog_b, dtb_b, ng_b):
    eb, ea = _head_selectors()
    ba = (proj, BA_PAD, COL_BA // BA_PAD)
    qkv = dn_conv_fwd(tag, proj, cw)
    betab, gb = rowmap("dn_gate_" + tag, _gate_tile, [ba], [eb, ea, alog_b, dtb_b], [DN_KEY_DIM] * 2, TM_ROW)
    o, states = delta_fwd(tag, qkv, gb, betab)
    z = (proj, DN_KEY_DIM, COL_Z // DN_KEY_DIM)
    a_out = rowmap("dn_out_" + tag, _dn_out_tile, [o, z], [ng_b], [DN_KEY_DIM], TM_ROW)[0]
    return a_out, (qkv, betab, gb, o, states)


def dn_mixer_bwd(tag, proj, cw, alog_b, dtb_b, ng_b, res, da_out):
    qkv, betab, gb, o, states = res
    eb, ea = _head_selectors()
    ba = (proj, BA_PAD, COL_BA // BA_PAD)
    z = (proj, DN_KEY_DIM, COL_Z // DN_KEY_DIM)
    (do, dz), (dng,) = rowmap_bwd("dn_out_bwd_" + tag, _dn_out_tile, [o, z], [ng_b], [da_out], TM_ROW)
    dq, dk, dv, dgb, dbetab = delta_bwd(tag, qkv, gb, betab, states, do)
    dqkv_raw, dcw = dn_conv_bwd(tag, proj, cw, jnp.concatenate([dq, dk, dv], 1))
    (dba,), (dalog, ddtb) = rowmap_bwd("dn_gate_bwd_" + tag, _gate_tile, [ba], [eb, ea, alog_b, dtb_b],
                                       [dbetab, dgb], TM_ROW, par_mask=[False, False, True, True])
    return dqkv_raw, dz, dba, dcw[:DN_CONV], dalog, ddtb, dng


def _swap_halves(x):
    n = x.shape[1]
    first = (_iota2((1, n), 1) % SW_HEAD_DIM) < SW_HEAD_DIM // 2
    return jnp.where(first, pltpu.roll(x, n - SW_HEAD_DIM // 2, 1), pltpu.roll(x, SW_HEAD_DIM // 2, 1))


def _rope_apply(x, cos, sin_signed):
    return x * cos + _swap_halves(x) * sin_signed


def _rope_transpose(dy, cos, sin_signed):
    return dy * cos + _swap_halves(dy * sin_signed)


def rope_tables(positions, s):
    half = SW_HEAD_DIM // 2
    inv_freq = ROPE_THETA ** (-jnp.arange(0, SW_HEAD_DIM, 2, dtype=f32) / SW_HEAD_DIM)
    ang = positions.reshape(s, 1).astype(f32) * inv_freq[None, :]
    cos, sin = jnp.cos(ang), jnp.sin(ang)
    cos_t = jnp.tile(jnp.concatenate([cos, cos], 1), (1, SW_HEADS))
    sin_t = jnp.tile(jnp.concatenate([-sin, sin], 1), (1, SW_HEADS))
    assert cos_t.shape == (s, SW_DIM) and half * 2 == SW_HEAD_DIM
    return cos_t, sin_t


def rope_fwd(tag, proj, cos, sin):
    def fn(q, k, v, c, sg):
        return _rope_apply(q, c, sg), _rope_apply(k, c, sg), v

    rows = [(proj, SW_DIM, COL_SWQ // SW_DIM), (proj, SW_DIM, COL_SWK // SW_DIM), (proj, SW_DIM, COL_SWV // SW_DIM), cos, sin]
    return rowmap("rope_" + tag, fn, rows, [], [SW_DIM] * 3, TM_ROW)


def _swa_block(q, kp, kc, vp, vc, first):
    blk = SW_BLOCK
    kk = jnp.concatenate([kp, kc], 0)
    vv = jnp.concatenate([vp, vc], 0)
    dist = (_iota2((blk, 2 * blk), 0) + blk) - _iota2((blk, 2 * blk), 1)
    kj = _iota2((blk, 2 * blk), 1)
    valid = (dist >= 0) & (dist <= blk) & ((kj >= blk) | jnp.logical_not(first))
    lane_head = _iota2((1, LANES), 1) // SW_HEAD_DIM
    outs, lses = [], []
    for p in range(SW_DIM // LANES):
        sl = slice(p * LANES, (p + 1) * LANES)
        qp, kp_, vp_ = q[:, sl], kk[:, sl], vv[:, sl]
        o_pair = jnp.zeros((blk, LANES), f32)
        l_pair = jnp.zeros((blk, LANES), f32)
        for e in range(LANES // SW_HEAD_DIM):
            msk = lane_head == e
            sc = bdot(jnp.where(msk, qp, 0.0), kp_, 1, 1) * (SW_HEAD_DIM ** -0.5)
            sc = jnp.where(valid, sc, -1e30)
            m = lax.stop_gradient(jnp.max(sc, -1, keepdims=True))
            pe = jnp.exp(sc - m)
            l = jnp.sum(pe, -1, keepdims=True)
            o = bdot(pe, vp_, 1, 0) / l
            o_pair = o_pair + jnp.where(msk, o, 0.0)
            l_pair = l_pair + jnp.where(msk, m + jnp.log(l), 0.0)
        outs.append(o_pair)
        lses.append(l_pair)
    return jnp.concatenate(outs, -1), jnp.concatenate(lses, -1)


def _swa_specs(r):
    cur = pl.BlockSpec((SW_BLOCK, SW_DIM), lambda rho, n: (n, rho))
    prev = pl.BlockSpec((SW_BLOCK, SW_DIM), lambda rho, n: (jnp.maximum(n - 1, 0), rho))
    return cur, prev


def swa_fwd(tag, r, q, k, v):
    s = q.shape[0]
    ln = s // r
    q2, k2, v2 = (t.reshape(ln, r * SW_DIM) for t in (q, k, v))
    cur, prev = _swa_specs(r)

    def body(q_ref, kp_ref, kc_ref, vp_ref, vc_ref, o_ref, l_ref):
        o, l = _swa_block(q_ref[...], kp_ref[...], kc_ref[...], vp_ref[...], vc_ref[...], pl.program_id(1) == 0)
        o_ref[...] = o
        l_ref[...] = l

    o, l = pl.pallas_call(
        body, grid=(r, ln // SW_BLOCK),
        in_specs=[cur, prev, cur, prev, cur], out_specs=[cur, cur],
        out_shape=[jax.ShapeDtypeStruct((ln, r * SW_DIM), f32)] * 2,
        compiler_params=_cparams("parallel", "parallel"), name=f"swa{r}_{tag}")(q2, k2, k2, v2, v2)
    return o.reshape(s, SW_DIM), l.reshape(s, SW_DIM)


def swa_bwd(tag, r, q, k, v, do, dl):
    s = q.shape[0]
    ln = s // r
    q2, k2, v2, do2, dl2 = (t.reshape(ln, r * SW_DIM) for t in (q, k, v, do, dl))
    cur, prev = _swa_specs(r)

    def body(q_ref, kp_ref, kc_ref, vp_ref, vc_ref, do_ref, dl_ref, dq_ref, dka_ref, dkb_ref, dva_ref, dvb_ref):
        first = pl.program_id(1) == 0
        _, vjp = jax.vjp(lambda *a: _swa_block(*a, first), q_ref[...], kp_ref[...], kc_ref[...], vp_ref[...], vc_ref[...])
        dq_ref[...], dka_ref[...], dkb_ref[...], dva_ref[...], dvb_ref[...] = vjp((do_ref[...], dl_ref[...]))

    outs = pl.pallas_call(
        body, grid=(r, ln // SW_BLOCK),
        in_specs=[cur, prev, cur, prev, cur, cur, cur], out_specs=[cur] * 5,
        out_shape=[jax.ShapeDtypeStruct((ln, r * SW_DIM), f32)] * 5,
        compiler_params=_cparams("parallel", "parallel"), name=f"swa{r}_bwd_{tag}")(q2, k2, k2, v2, v2, do2, dl2)
    return [t.reshape(s, SW_DIM) for t in outs]


def _combine_tile(o1, l1, o2, l2, o3, l3):
    m = lax.stop_gradient(jnp.maximum(jnp.maximum(l1, l2), l3))
    e1, e2, e3 = jnp.exp(l1 - m), jnp.exp(l2 - m), jnp.exp(l3 - m)
    return ((o1 * e1 + o2 * e2 + o3 * e3) / (e1 + e2 + e3),)


def swa_merge_bwd(tag, grads, cos, sin):
    s = cos.shape[0]
    tm = SW_BLOCK
    nt = s // tm
    here = pl.BlockSpec((tm, SW_DIM), lambda i: (i, 0))
    arrs, specs = [], []
    for r, g in zip(SW_DILATIONS, grads):
        ahead = pl.BlockSpec((tm, SW_DIM), lambda i, _r=r: (jnp.minimum(i + _r, nt - 1), 0))
        arrs += g
        specs += [here, ahead, here, ahead, here]

    def body(*refs):
        i = pl.program_id(0)
        c_ref, s_ref = refs[15], refs[16]
        dq_ref, dk_ref, dv_ref = refs[17:]
        dq = jnp.zeros((tm, SW_DIM), f32)
        dk = jnp.zeros((tm, SW_DIM), f32)
        dv = jnp.zeros((tm, SW_DIM), f32)
        for b, r in enumerate(SW_DILATIONS):
            gq, gka, gkb, gva, gvb = refs[5 * b:5 * b + 5]
            inside = i + r < nt
            dq = dq + gq[...]
            dk = dk + gkb[...] + jnp.where(inside, gka[...], 0.0)
            dv = dv + gvb[...] + jnp.where(inside, gva[...], 0.0)
        dq_ref[...] = _rope_transpose(dq, c_ref[...], s_ref[...])
        dk_ref[...] = _rope_transpose(dk, c_ref[...], s_ref[...])
        dv_ref[...] = dv

    return pl.pallas_call(
        body, grid=(nt,), in_specs=specs + [here, here], out_specs=[here] * 3,
        out_shape=[jax.ShapeDtypeStruct((s, SW_DIM), f32)] * 3,
        compiler_params=_cparams("parallel"), name="swa_merge_bwd_" + tag)(*arrs, cos, sin)


def swa_mixer_fwd(tag, proj, cos, sin):
    q, k, v = rope_fwd(tag, proj, cos, sin)
    ols = []
    for r in SW_DILATIONS:
        ols += list(swa_fwd(tag, r, q, k, v))
    b_out = rowmap("swa_comb_" + tag, _combine_tile, ols, [], [SW_DIM], TM_ROW)[0]
    return b_out, (q, k, v, ols)


def swa_mixer_bwd(tag, cos, sin, res, db_out):
    q, k, v, ols = res
    dols, _ = rowmap_bwd("swa_comb_bwd_" + tag, _combine_tile, ols, [], [db_out], TM_ROW)
    grads = [swa_bwd(tag, r, q, k, v, dols[2 * b], dols[2 * b + 1]) for b, r in enumerate(SW_DILATIONS)]
    return swa_merge_bwd(tag, grads, cos, sin)


TM_S5 = 256
S5_LANES = 2 * S5_STATE
S5_GPB = LANES // S5_GROUP
S5_NBLK = D_MODEL // LANES
S5_BW = S5_GPB * S5_LANES
S5_WIDTH = S5_GROUPS * S5_LANES


def _swap_ri(x):
    n = x.shape[1]
    first = (_iota2((1, n), 1) % S5_LANES) < S5_STATE
    return jnp.where(first, pltpu.roll(x, n - S5_STATE, 1), pltpu.roll(x, S5_STATE, 1))


def _s5_disc_tile(a_re, a_im, log_dt, b_re, b_im, expand):
    dt = jnp.exp(log_dt)
    mag = jnp.exp(a_re * dt)
    abar_re, abar_im = mag * jnp.cos(a_im * dt), mag * jnp.sin(a_im * dt)
    n_re, n_im = abar_re - 1.0, abar_im
    den = a_re * a_re + a_im * a_im
    c_re = (n_re * a_re + n_im * a_im) / den
    c_im = (n_im * a_re - n_re * a_im) / den
    cx_re, cx_im = hdot(c_re, expand), hdot(c_im, expand)
    return abar_re, abar_im, cx_re * b_re - cx_im * b_im, cx_re * b_im + cx_im * b_re


def _s5_expand():
    return (_iota2((S5_STATE, S5_STATE * S5_GROUP), 1) // S5_GROUP == _iota2((S5_STATE, S5_STATE * S5_GROUP), 0)).astype(f32)


def _lane_layout(re, im):
    return jnp.concatenate([re, im], 1).reshape(1, S5_WIDTH)


def s5_tables(a_re, a_im, log_dt):
    dt = jnp.broadcast_to(log_dt.reshape(S5_GROUPS, 1), (S5_GROUPS, S5_STATE))
    are_l, aim_l, ldt_l = _lane_layout(a_re, a_re), _lane_layout(a_im, a_im), _lane_layout(dt, dt)
    t = TM_S5

    def body(are_ref, aim_ref, ldt_ref, a1_ref, a2_ref, a1r_ref, a2r_ref):
        dtv = jnp.exp(ldt_ref[...])
        lre, lim = are_ref[...] * dtv, aim_ref[...] * dtv
        sign = jnp.where((_iota2((1, S5_BW), 1) % S5_LANES) < S5_STATE, -1.0, 1.0)
        row = _iota2((t, S5_BW), 0)
        for asc, o1, o2 in ((True, a1_ref, a2_ref), (False, a1r_ref, a2r_ref)):
            n = (row + 1 if asc else t - row).astype(f32)
            mag = jnp.exp(n * lre)
            o1[...] = mag * jnp.cos(n * lim)
            o2[...] = sign * mag * jnp.sin(n * lim)

    lane = pl.BlockSpec((1, S5_BW), lambda j: (0, j))
    tab = pl.BlockSpec((t, S5_BW), lambda j: (0, j))
    return pl.pallas_call(
        body, grid=(S5_NBLK,), in_specs=[lane] * 3, out_specs=[tab] * 4,
        out_shape=[jax.ShapeDtypeStruct((t, S5_WIDTH), f32)] * 4,
        compiler_params=_cparams("parallel"), name="s5_tables")(are_l, aim_l, ldt_l)


def s5_pack_weights(bbar_re, bbar_im, c_re, c_im):
    eye = jnp.eye(S5_GPB, dtype=f32)
    bb = jnp.stack([bbar_re.reshape(S5_GROUPS, S5_STATE, S5_GROUP), bbar_im.reshape(S5_GROUPS, S5_STATE, S5_GROUP)], 1)
    bb = bb.transpose(0, 3, 1, 2).reshape(S5_NBLK, S5_GPB, S5_GROUP, S5_LANES)
    wb = (bb[:, :, :, None, :] * eye[None, :, None, :, None]).reshape(S5_NBLK, LANES, S5_BW)
    cc = jnp.stack([c_re, -c_im], 1)
    cc = cc.transpose(0, 1, 3, 2).reshape(S5_NBLK, S5_GPB, S5_LANES, S5_GROUP)
    wc = (cc[:, :, :, None, :] * eye[None, :, None, :, None]).reshape(S5_NBLK, S5_BW, LANES)
    return wb, wc


def s5_unpack_weight_grads(dwb, dwc):
    d5 = dwb.reshape(S5_NBLK, S5_GPB, S5_GROUP, S5_GPB, S5_LANES)
    dbb = jnp.stack([d5[:, gl, :, gl, :] for gl in range(S5_GPB)])
    dbb = dbb.transpose(1, 0, 2, 3).reshape(S5_GROUPS, S5_GROUP, 2, S5_STATE).transpose(0, 2, 3, 1)
    dbbar_re = dbb[:, 0].reshape(S5_GROUPS, S5_STATE * S5_GROUP)
    dbbar_im = dbb[:, 1].reshape(S5_GROUPS, S5_STATE * S5_GROUP)
    c5 = dwc.reshape(S5_NBLK, S5_GPB, S5_LANES, S5_GPB, S5_GROUP)
    dcc = jnp.stack([c5[:, gl, :, gl, :] for gl in range(S5_GPB)])
    dcc = dcc.transpose(1, 0, 2, 3).reshape(S5_GROUPS, 2, S5_STATE, S5_GROUP).transpose(0, 1, 3, 2)
    return dbbar_re, dbbar_im, dcc[:, 0], -dcc[:, 1]


def _s5_step_rows(t):
    d, out = 1, []
    while d < t:
        out.append(d)
        d *= 2
    return out


def s5_core_fwd(tag, u, wb, wc, a1, a2, dskip):
    s = u.shape[0]
    t = TM_S5

    def body(u_ref, wb_ref, wc_ref, a1_ref, a2_ref, d_ref, y_ref, x_ref, carry, carry_sw, tail_sw):
        @pl.when(pl.program_id(1) == 0)
        def _():
            carry[...] = jnp.zeros_like(carry)
            carry_sw[...] = jnp.zeros_like(carry_sw)

        uv = u_ref[...]
        x = bdot(uv, wb_ref[...], 1, 0)
        row = _iota2((t, S5_BW), 0)
        for d in _s5_step_rows(t):
            sh = jnp.where(row >= d, pltpu.roll(x, d, 0), 0.0)
            x = x + a1_ref[d - 1:d, :] * sh + a2_ref[d - 1:d, :] * _swap_ri(sh)
        x = x + a1_ref[...] * carry[...] + a2_ref[...] * carry_sw[...]
        x_ref[...] = x
        tail_sw[...] = _swap_ri(x[t - SUBLANES:, :])
        carry[...] = x_ref[t - 1:t, :]
        carry_sw[...] = tail_sw[SUBLANES - 1:SUBLANES, :]
        y_ref[...] = bdot(x, wc_ref[...], 1, 0) + d_ref[...] * uv

    return pl.pallas_call(
        body, grid=(S5_NBLK, s // t),
        in_specs=[pl.BlockSpec((t, LANES), lambda j, i: (i, j)),
                  pl.BlockSpec((None, LANES, S5_BW), lambda j, i: (j, 0, 0)),
                  pl.BlockSpec((None, S5_BW, LANES), lambda j, i: (j, 0, 0)),
                  pl.BlockSpec((t, S5_BW), lambda j, i: (0, j)),
                  pl.BlockSpec((t, S5_BW), lambda j, i: (0, j)),
                  pl.BlockSpec((1, LANES), lambda j, i: (0, j))],
        out_specs=[pl.BlockSpec((t, LANES), lambda j, i: (i, j)), pl.BlockSpec((t, S5_BW), lambda j, i: (i, j))],
        out_shape=[jax.ShapeDtypeStruct((s, D_MODEL), f32), jax.ShapeDtypeStruct((s, S5_WIDTH), f32)],
        scratch_shapes=[pltpu.VMEM((1, S5_BW), f32), pltpu.VMEM((1, S5_BW), f32), pltpu.VMEM((SUBLANES, S5_BW), f32)],
        compiler_params=_cparams("parallel", "arbitrary"), name="s5_core_" + tag)(u, wb, wc, a1, a2, dskip)


def s5_core_bwd(tag, u, x, wb, wc, a1, a2, a1r, a2r, dskip, dy):
    s = u.shape[0]
    t = TM_S5
    nt = s // t
    hb = t // SUBLANES

    def body(u_ref, dy_ref, x_ref, xh_ref, wb_ref, wc_ref, a1_ref, a2_ref, a1r_ref, a2r_ref, d_ref,
             du_ref, dwb_ref, dwc_ref, dd_ref, q1_ref, q2_ref, carry, carry_sw, lam_scr, head_sw):
        i = pl.program_id(1)
        tt = nt - 1 - i

        @pl.when(i == 0)
        def _():
            carry[...] = jnp.zeros_like(carry)
            carry_sw[...] = jnp.zeros_like(carry_sw)

        uv, dyv, xv = u_ref[...], dy_ref[...], x_ref[...]
        lam = bdot(dyv, wc_ref[...], 1, 1)
        row = _iota2((t, S5_BW), 0)
        for d in _s5_step_rows(t):
            sh = jnp.where(row < t - d, pltpu.roll(lam, t - d, 0), 0.0)
            lam = lam + a1_ref[d - 1:d, :] * sh - a2_ref[d - 1:d, :] * _swap_ri(sh)
        lam = lam + a1r_ref[...] * carry[...] - a2r_ref[...] * carry_sw[...]
        lam_scr[...] = lam
        head_sw[...] = _swap_ri(lam[:SUBLANES, :])
        carry[...] = lam_scr[0:1, :]
        carry_sw[...] = head_sw[0:1, :]
        du_ref[...] = bdot(lam, wb_ref[...], 1, 1) + d_ref[...] * dyv
        x_last = jnp.where(tt > 0, xh_ref[SUBLANES - 1:SUBLANES, :], 0.0)
        x_prev = jnp.where(row == 0, x_last, pltpu.roll(xv, 1, 0))
        p1, p2 = lam * x_prev, lam * _swap_ri(x_prev)
        q1 = p1[:SUBLANES, :]
        q2 = p2[:SUBLANES, :]
        for k in range(1, hb):
            q1 = q1 + p1[k * SUBLANES:(k + 1) * SUBLANES, :]
            q2 = q2 + p2[k * SUBLANES:(k + 1) * SUBLANES, :]
        upd = [(dwb_ref, bdot(uv, lam, 0, 0)), (dwc_ref, bdot(xv, dyv, 0, 0)),
               (dd_ref, jnp.sum(dyv * uv, 0, keepdims=True)), (q1_ref, q1), (q2_ref, q2)]

        @pl.when(i == 0)
        def _():
            for ref, val in upd:
                ref[...] = val

        @pl.when(i != 0)
        def _():
            for ref, val in upd:
                ref[...] += val

    nb8 = s // SUBLANES
    rev = lambda w: pl.BlockSpec((t, w), lambda j, i: (nt - 1 - i, j))
    tab = pl.BlockSpec((t, S5_BW), lambda j, i: (0, j))
    return pl.pallas_call(
        body, grid=(S5_NBLK, nt),
        in_specs=[rev(LANES), rev(LANES), rev(S5_BW),
                  pl.BlockSpec((SUBLANES, S5_BW), lambda j, i: (jnp.maximum((nt - 1 - i) * hb - 1, 0), j)),
                  pl.BlockSpec((None, LANES, S5_BW), lambda j, i: (j, 0, 0)),
                  pl.BlockSpec((None, S5_BW, LANES), lambda j, i: (j, 0, 0)),
                  tab, tab, tab, tab, pl.BlockSpec((1, LANES), lambda j, i: (0, j))],
        out_specs=[rev(LANES),
                   pl.BlockSpec((None, LANES, S5_BW), lambda j, i: (j, 0, 0)),
                   pl.BlockSpec((None, S5_BW, LANES), lambda j, i: (j, 0, 0)),
                   pl.BlockSpec((1, LANES), lambda j, i: (0, j)),
                   pl.BlockSpec((SUBLANES, S5_BW), lambda j, i: (0, j)),
                   pl.BlockSpec((SUBLANES, S5_BW), lambda j, i: (0, j))],
        out_shape=[jax.ShapeDtypeStruct((s, D_MODEL), f32),
                   jax.ShapeDtypeStruct((S5_NBLK, LANES, S5_BW), f32),
                   jax.ShapeDtypeStruct((S5_NBLK, S5_BW, LANES), f32),
                   jax.ShapeDtypeStruct((1, D_MODEL), f32),
                   jax.ShapeDtypeStruct((SUBLANES, S5_WIDTH), f32),
                   jax.ShapeDtypeStruct((SUBLANES, S5_WIDTH), f32)],
        scratch_shapes=[pltpu.VMEM((1, S5_BW), f32), pltpu.VMEM((1, S5_BW), f32),
                        pltpu.VMEM((t, S5_BW), f32), pltpu.VMEM((SUBLANES, S5_BW), f32)],
        compiler_params=_cparams("parallel", "arbitrary"),
        name="s5_core_bwd_" + tag)(u, dy, x, x, wb, wc, a1, a2, a1r, a2r, dskip)


def _gelu_tile(y):
    return (jax.nn.gelu(y),)


def s5_mixer_fwd(tag, u, prm, w_og):
    a_re, a_im, log_dt, b_re, b_im, c_re, c_im, dskip = prm
    disc_in = [a_re, a_im, log_dt.reshape(S5_GROUPS, 1), b_re.reshape(S5_GROUPS, -1), b_im.reshape(S5_GROUPS, -1)]
    abar_re, abar_im, bbar_re, bbar_im = rowmap("s5_disc_" + tag, _s5_disc_tile, disc_in, [_s5_expand()],
                                                [S5_STATE, S5_STATE, S5_STATE * S5_GROUP, S5_STATE * S5_GROUP], S5_GROUPS)
    del abar_re, abar_im
    a1, a2, a1r, a2r = s5_tables(a_re, a_im, log_dt)
    wb, wc = s5_pack_weights(bbar_re, bbar_im, c_re, c_im)
    wb, wc = wb.astype(bf16), wc.astype(bf16)
    y, x = s5_core_fwd(tag, u, wb, wc, a1, a2, dskip.reshape(1, D_MODEL))
    hid = rowmap("s5_gelu_" + tag, _gelu_tile, [y], [], [D_MODEL], TM_ROW)[0]
    og = mm_nn("s5_og_" + tag, hid, w_og)
    mix = rowmap("s5_glu_" + tag, _glu_tile, [og], [], [D_MODEL], TM_ROW)[0]
    return mix, (disc_in, a1, a2, a1r, a2r, wb, wc, x, y, hid, og)


def s5_mixer_bwd(tag, u, prm, w_og, res, dmix):
    a_re, a_im, log_dt, b_re, b_im, c_re, c_im, dskip = prm
    disc_in, a1, a2, a1r, a2r, wb, wc, x, y, hid, og = res
    (dog,), _ = rowmap_bwd("s5_glu_bwd_" + tag, _glu_tile, [og], [], [dmix], TM_ROW)
    dw_og = mm_tn("s5_og_dw_" + tag, hid, dog)
    dhid = mm_nt("s5_og_dx_" + tag, dog, w_og)
    (dy,), _ = rowmap_bwd("s5_gelu_bwd_" + tag, _gelu_tile, [y], [], [dhid], TM_ROW)
    du, dwb, dwc, ddskip, q1, q2 = s5_core_bwd(tag, u, x, wb, wc, a1, a2, a1r, a2r, dskip.reshape(1, D_MODEL), dy)
    dbbar_re, dbbar_im, dc_re, dc_im = s5_unpack_weight_grads(dwb, dwc)
    q1 = q1.sum(0).reshape(S5_GROUPS, 2, S5_STATE)
    q2 = q2.sum(0).reshape(S5_GROUPS, 2, S5_STATE)
    dabar_re, dabar_im = q1[:, 0] + q1[:, 1], q2[:, 1] - q2[:, 0]
    grads, _ = rowmap_bwd("s5_disc_bwd_" + tag, _s5_disc_tile, disc_in, [_s5_expand()],
                          [dabar_re, dabar_im, dbbar_re, dbbar_im], S5_GROUPS, par_mask=[False])
    da_re, da_im, dlog_dt, db_re, db_im = grads
    return du, (da_re, da_im, dlog_dt.reshape(S5_GROUPS), db_re.reshape(b_re.shape), db_im.reshape(b_im.shape),
                dc_re, dc_im, ddskip.reshape(D_MODEL)), dw_og


HYB_IN = 3592
_IN_B0, _IN_SW0 = 2048, 2056


def _canon_w_in(w):
    pad = jnp.zeros((D_MODEL, BA_PAD - 2 * DN_HEADS), w.dtype)
    return jnp.concatenate([w[:, :_IN_B0], w[:, _IN_SW0:], w[:, _IN_B0:_IN_SW0], pad], 1)


def _uncanon_w_in(g):
    return jnp.concatenate([g[:, :_IN_B0], g[:, COL_BA:COL_BA + 2 * DN_HEADS], g[:, _IN_B0:COL_BA]], 1)


def _add2(name, a, b):
    return rowmap(name, lambda p, q: (p + q,), [a, b], [], [a.shape[1]], _pick(a.shape[0], (256, 128, 64, 32, 16, 8)))[0]


def local_step(x, mem, positions, target, p):
    s = x.shape[0]
    cos, sin = rope_tables(positions, s)
    row = lambda v: v.reshape(1, -1).astype(f32)
    h = x
    tape = []
    for l in range(DEPTH):
        i, tag = l // 2, str(l)
        t = {"h0": h}
        if l % 2 == 0:
            t["w_in"] = _canon_w_in(p["hyb_w_in"][i]).astype(bf16)
            t["w_out"] = p["hyb_w_out"][i].astype(bf16)
            t["dn_prm"] = (p["dn_conv_w"][i].astype(f32), row(jnp.repeat(p["dn_a_log"][i], DN_HEAD_DIM)),
                           row(jnp.repeat(p["dn_dt_bias"][i], DN_HEAD_DIM)), row(jnp.tile(p["dn_norm_g"][i], DN_HEADS)))
            t["proj"] = mm_nn("hyb_in_" + tag, h, t["w_in"])
            a_out, t["dn"] = dn_mixer_fwd(tag, t["proj"], *t["dn_prm"])
            b_out, t["swa"] = swa_mixer_fwd(tag, t["proj"], cos, sin)
            t["mixed"] = jnp.concatenate([a_out, b_out], 1)
            mix = mm_nn("hyb_out_" + tag, t["mixed"], t["w_out"])
        else:
            t["s5_prm"] = tuple(p[n][i].astype(f32) for n in
                                ("s5_a_re", "s5_a_im", "s5_log_dt", "s5_b_re", "s5_b_im", "s5_c_re", "s5_c_im", "s5_d"))
            t["w_og"] = jnp.concatenate([p["s5_glu_wo"][i], p["s5_glu_wg"][i]], 1).astype(bf16)
            mix, t["s5"] = s5_mixer_fwd(tag, h, t["s5_prm"], t["w_og"])
        t["mix"] = mix
        t["ln"] = [(row(p[g][l]), row(p[b][l])) for g, b in
                   (("ln_mix_g", "ln_mix_b"), ("ln_x_g", "ln_x_b"), ("ln_ffn_g", "ln_ffn_b"))]
        t["h1"] = postnorm_fwd("mix" + tag, h, mix, *t["ln"][0])
        t["wq"], t["wo"] = p["xq_w"][l].astype(bf16), p["xo_w"][l].astype(bf16)
        t["wkv"] = jnp.concatenate([p["xk_w"][l], p["xv_w"][l]], 1).astype(bf16)
        t["xo"], t["xres"] = xattn_fwd(tag, t["h1"], mem, t["wq"], t["wkv"], t["wo"])
        t["h2"] = postnorm_fwd("x" + tag, t["h1"], t["xo"], *t["ln"][1])
        t["wgu"] = jnp.concatenate([p["ffn_wg"][l], p["ffn_wu"][l]], 1).astype(bf16)
        t["wd"] = p["ffn_wd"][l].astype(bf16)
        t["fo"], t["fres"] = ffn_fwd(tag, t["h2"], t["wgu"], t["wd"])
        h = postnorm_fwd("ffn" + tag, t["h2"], t["fo"], *t["ln"][2])
        tape.append(t)

    part, dh = loss_head(h, target)
    loss = jnp.sum(part)

    g = {n: [None] * v.shape[0] for n, v in p.items()}
    for l in reversed(range(DEPTH)):
        i, tag, t = l // 2, str(l), tape[l]
        dh2a, dfo, dg, db = postnorm_bwd("ffn" + tag, t["h2"], t["fo"], *t["ln"][2], dh)
        g["ln_ffn_g"][l], g["ln_ffn_b"][l] = dg[0], db[0]
        dh2b, dwgu, g["ffn_wd"][l] = ffn_bwd(tag, t["h2"], t["wgu"], t["wd"], t["fres"], dfo)
        g["ffn_wg"][l], g["ffn_wu"][l] = dwgu[:, :FFN_HIDDEN], dwgu[:, FFN_HIDDEN:]
        dh1a, dxo, dg, db = postnorm_bwd("x" + tag, t["h1"], t["xo"], *t["ln"][1], [dh2a, dh2b])
        g["ln_x_g"][l], g["ln_x_b"][l] = dg[0], db[0]
        dh1b, g["xq_w"][l], dwkv, g["xo_w"][l] = xattn_bwd(tag, t["h1"], mem, t["wq"], t["wkv"], t["wo"], t["xres"], dxo)
        g["xk_w"][l], g["xv_w"][l] = dwkv[:, :D_MODEL], dwkv[:, D_MODEL:]
        dh0a, dmix, dg, db = postnorm_bwd("mix" + tag, t["h0"], t["mix"], *t["ln"][0], [dh1a, dh1b])
        g["ln_mix_g"][l], g["ln_mix_b"][l] = dg[0], db[0]
        if l % 2 == 0:
            g["hyb_w_out"][i] = mm_tn("hyb_out_dw_" + tag, t["mixed"], dmix)
            dmixed = mm_nt("hyb_out_dx_" + tag, dmix, t["w_out"])
            dqkv, dz, dba, dcw, dalog, ddtb, dng = dn_mixer_bwd(tag, t["proj"], *t["dn_prm"], t["dn"], (dmixed, DN_KEY_DIM, 0))
            g["dn_conv_w"][i] = dcw
            g["dn_a_log"][i] = dalog.reshape(DN_HEADS, DN_HEAD_DIM).sum(1)
            g["dn_dt_bias"][i] = ddtb.reshape(DN_HEADS, DN_HEAD_DIM).sum(1)
            g["dn_norm_g"][i] = dng.reshape(DN_HEADS, DN_HEAD_DIM).sum(0)
            dq, dk, dv = swa_mixer_bwd(tag, cos, sin, t["swa"], (dmixed, SW_DIM, 1))
            dproj = jnp.concatenate([dqkv, dz, dq, dk, dv, dba], 1)
            g["hyb_w_in"][i] = _uncanon_w_in(mm_tn("hyb_in_dw_" + tag, t["h0"], dproj))
            dh0b = mm_nt("hyb_in_dx_" + tag, dproj, t["w_in"])
        else:
            dh0b, dprm, dw_og = s5_mixer_bwd(tag, t["h0"], t["s5_prm"], t["w_og"], t["s5"], dmix)
            for n, v in zip(("s5_a_re", "s5_a_im", "s5_log_dt", "s5_b_re", "s5_b_im", "s5_c_re", "s5_c_im", "s5_d"), dprm):
                g[n][i] = v
            g["s5_glu_wo"][i], g["s5_glu_wg"][i] = dw_og[:, :D_MODEL], dw_og[:, D_MODEL:]
        dh = [dh0a, dh0b]
    grad_x = _add2("grad_x", dh[0], dh[1])
    grads = {n: jnp.stack(v) for n, v in g.items()}
    return loss, grad_x, grads


WEIGHT_NAMES = ("hyb_w_in", "dn_conv_w", "dn_a_log", "dn_dt_bias", "dn_norm_g", "hyb_w_out", "s5_a_re", "s5_a_im",
                "s5_log_dt", "s5_b_re", "s5_b_im", "s5_c_re", "s5_c_im", "s5_d", "s5_glu_wo", "s5_glu_wg",
                "ln_mix_g", "ln_mix_b", "xq_w", "xk_w", "xv_w", "xo_w", "ln_x_g", "ln_x_b",
                "ffn_wg", "ffn_wu", "ffn_wd", "ln_ffn_g", "ln_ffn_b")
SHARD_AXIS = {"hyb_w_in": 2, "dn_conv_w": 2, "hyb_w_out": 1, "s5_d": 1, "s5_glu_wo": 1, "s5_glu_wg": 1,
              "xq_w": 1, "xk_w": 1, "xv_w": 1, "xo_w": 1, "ffn_wg": 2, "ffn_wu": 2, "ffn_wd": 1}
GATHER_F32 = ("dn_conv_w", "s5_d")
N_CHIPS = 4
PACK_COLS = 1024
_ANY = pl.BlockSpec(memory_space=pl.ANY)


def _pos():
    return lax.axis_index("x"), lax.axis_index("y"), lax.axis_index("c")


def _chip_peers(mx, my):
    return [(1 - mx, my), (mx, 1 - my), (1 - mx, 1 - my)]


def _rcopy(src, dst, ssem, rsem, dev):
    return pltpu.make_async_remote_copy(src_ref=src, dst_ref=dst, send_sem=ssem, recv_sem=rsem,
                                        device_id=dev, device_id_type=pl.DeviceIdType.MESH)


def comm_allgather4(name, x):
    def body(x_ref, o_ref, ssem, rsem, lsem):
        mx, my, mc = _pos()
        me = 2 * mx + my
        peers = _chip_peers(mx, my)
        loc = pltpu.make_async_copy(x_ref, o_ref.at[me], lsem)
        loc.start()
        sends = [_rcopy(x_ref, o_ref.at[me], ssem.at[k], rsem.at[k], (px, py, mc)) for k, (px, py) in enumerate(peers)]
        for cp in sends:
            cp.start()
        for k, (px, py) in enumerate(peers):
            _rcopy(x_ref, o_ref.at[2 * px + py], ssem.at[k], rsem.at[k], (px, py, mc)).wait_recv()
        for cp in sends:
            cp.wait_send()
        loc.wait()

    return pl.pallas_call(
        body, out_shape=jax.ShapeDtypeStruct((N_CHIPS,) + x.shape, x.dtype), in_specs=[_ANY], out_specs=_ANY,
        scratch_shapes=[pltpu.SemaphoreType.DMA((3,)), pltpu.SemaphoreType.DMA((3,)), pltpu.SemaphoreType.DMA],
        name=name)(x)


def comm_alltoall4(name, x):
    def body(x_ref, o_ref, ssem, rsem, lsem):
        mx, my, mc = _pos()
        me = 2 * mx + my
        peers = _chip_peers(mx, my)
        loc = pltpu.make_async_copy(x_ref.at[me], o_ref.at[me], lsem)
        loc.start()
        sends = [_rcopy(x_ref.at[2 * px + py], o_ref.at[me], ssem.at[k], rsem.at[k], (px, py, mc))
                 for k, (px, py) in enumerate(peers)]
        for cp in sends:
            cp.start()
        for k, (px, py) in enumerate(peers):
            _rcopy(x_ref.at[me], o_ref.at[2 * px + py], ssem.at[k], rsem.at[k], (px, py, mc)).wait_recv()
        for cp in sends:
            cp.wait_send()
        loc.wait()

    return pl.pallas_call(
        body, out_shape=jax.ShapeDtypeStruct(x.shape, x.dtype), in_specs=[_ANY], out_specs=_ANY,
        scratch_shapes=[pltpu.SemaphoreType.DMA((3,)), pltpu.SemaphoreType.DMA((3,)), pltpu.SemaphoreType.DMA],
        name=name)(x)


def comm_sibling_halves(name, g):
    def body(g_ref, o_ref, ssem, rsem):
        mx, my, mc = _pos()
        sib = (mx, my, 1 - mc)
        sends = [_rcopy(g_ref.at[j, 1 - mc], o_ref.at[j], ssem.at[j], rsem.at[j], sib) for j in range(N_CHIPS)]
        for cp in sends:
            cp.start()
        for j in range(N_CHIPS):
            _rcopy(g_ref.at[j, mc], o_ref.at[j], ssem.at[j], rsem.at[j], sib).wait_recv()
        for cp in sends:
            cp.wait_send()

    return pl.pallas_call(
        body, out_shape=jax.ShapeDtypeStruct((g.shape[0],) + g.shape[2:], g.dtype), in_specs=[_ANY], out_specs=_ANY,
        scratch_shapes=[pltpu.SemaphoreType.DMA((N_CHIPS,)), pltpu.SemaphoreType.DMA((N_CHIPS,))], name=name)(g)


def comm_sibling_swap(name, x):
    def body(x_ref, o_ref, ssem, rsem):
        mx, my, mc = _pos()
        cp = _rcopy(x_ref, o_ref, ssem, rsem, (mx, my, 1 - mc))
        cp.start()
        cp.wait_recv()
        cp.wait_send()

    return pl.pallas_call(
        body, out_shape=jax.ShapeDtypeStruct(x.shape, x.dtype), in_specs=[_ANY], out_specs=_ANY,
        scratch_shapes=[pltpu.SemaphoreType.DMA, pltpu.SemaphoreType.DMA], name=name)(x)


def comm_sibling_join(name, b):
    def body(b_ref, o_ref, ssem, rsem, lsem):
        mx, my, mc = _pos()
        loc = pltpu.make_async_copy(b_ref, o_ref.at[mc], lsem)
        loc.start()
        cp = _rcopy(b_ref, o_ref.at[mc], ssem, rsem, (mx, my, 1 - mc))
        cp.start()
        _rcopy(b_ref, o_ref.at[1 - mc], ssem, rsem, (mx, my, 1 - mc)).wait_recv()
        cp.wait_send()
        loc.wait()

    return pl.pallas_call(
        body, out_shape=jax.ShapeDtypeStruct((2,) + b.shape, b.dtype), in_specs=[_ANY], out_specs=_ANY,
        scratch_shapes=[pltpu.SemaphoreType.DMA, pltpu.SemaphoreType.DMA, pltpu.SemaphoreType.DMA], name=name)(b)


def _row_tile(r):
    return _pick(r, (256, 128, 64, 32, 16, 8))


def add_own_half(name, g, recv):
    r, c = g.shape[2:]
    tr = _row_tile(r)
    mc = lax.axis_index("c").astype(jnp.int32).reshape(1)

    def body(c_ref, g_ref, r_ref, o_ref):
        o_ref[...] = g_ref[...] + r_ref[...]

    grid_spec = pltpu.PrefetchScalarGridSpec(
        num_scalar_prefetch=1, grid=(N_CHIPS, r // tr),
        in_specs=[pl.BlockSpec((None, None, tr, c), lambda j, i, cr: (j, cr[0], i, 0)),
                  pl.BlockSpec((None, tr, c), lambda j, i, cr: (j, i, 0))],
        out_specs=pl.BlockSpec((None, tr, c), lambda j, i, cr: (j, i, 0)))
    return pl.pallas_call(body, grid_spec=grid_spec, out_shape=jax.ShapeDtypeStruct(recv.shape, f32),
                          compiler_params=_cparams("parallel", "parallel"), name=name)(mc, g, recv)


def sum_slots(name, x):
    r, c = x.shape[1:]
    tr = _row_tile(r)

    def body(x_ref, o_ref):
        o_ref[...] = (x_ref[0] + x_ref[1]) + (x_ref[2] + x_ref[3])

    return pl.pallas_call(
        body, grid=(r // tr,), in_specs=[pl.BlockSpec((N_CHIPS, tr, c), lambda i: (0, i, 0))],
        out_specs=pl.BlockSpec((tr, c), lambda i: (i, 0)), out_shape=jax.ShapeDtypeStruct((r, c), f32),
        compiler_params=_cparams("parallel"), name=name)(x)


def adamw(name, w, g, m, v):
    r, c = w.shape
    tr = _row_tile(r)

    def body(w_ref, g_ref, m_ref, v_ref, d_ref, nm_ref, nv_ref):
        gv = g_ref[...]
        nm = ADAM_B1 * m_ref[...] + (1.0 - ADAM_B1) * gv
        nv = ADAM_B2 * v_ref[...] + (1.0 - ADAM_B2) * (gv * gv)
        m_hat = nm / (1.0 - ADAM_B1 ** ADAM_STEP)
        v_hat = nv / (1.0 - ADAM_B2 ** ADAM_STEP)
        d_ref[...] = -ADAM_LR * (m_hat / (jnp.sqrt(v_hat) + ADAM_EPS) + ADAM_WD * w_ref[...])
        nm_ref[...] = nm
        nv_ref[...] = nv

    blk = pl.BlockSpec((tr, c), lambda i: (i, 0))
    return pl.pallas_call(
        body, grid=(r // tr,), in_specs=[blk] * 4, out_specs=[blk] * 3,
        out_shape=[jax.ShapeDtypeStruct((r, c), f32)] * 3,
        compiler_params=_cparams("parallel"), name=name)(w, g, m, v)


def _pack(arrs, dtype, row_multiple):
    flat = jnp.concatenate([a.astype(dtype).reshape(-1) for a in arrs])
    unit = PACK_COLS * row_multiple
    total = -(-flat.shape[0] // unit) * unit
    return jnp.pad(flat, (0, total - flat.shape[0])).reshape(-1, PACK_COLS)


def _unpack(packed, shapes):
    flat, out, off = packed.reshape(-1), [], 0
    for shp in shapes:
        n = math.prod(shp)
        out.append(flat[off:off + n].reshape(shp))
        off += n
    return out


def _gathered_to_full(g, axis):
    t = jnp.moveaxis(g, 0, axis)
    return t.reshape(t.shape[:axis] + (t.shape[axis] * t.shape[axis + 1],) + t.shape[axis + 2:])


def _full_to_shard_major(full, axis):
    shp = full.shape
    t = full.reshape(shp[:axis] + (N_CHIPS, shp[axis] // N_CHIPS) + shp[axis + 1:])
    return jnp.moveaxis(t, axis, 0)


GRAD_ROW_MULTIPLE = 256


def kernel(x, mem, positions, hyb_w_in, dn_conv_w, dn_a_log, dn_dt_bias, dn_norm_g, hyb_w_out, s5_a_re, s5_a_im, s5_log_dt, s5_b_re, s5_b_im, s5_c_re, s5_c_im, s5_d, s5_glu_wo, s5_glu_wg, ln_mix_g, ln_mix_b, xq_w, xk_w, xv_w, xo_w, ln_x_g, ln_x_b, ffn_wg, ffn_wu, ffn_wd, ln_ffn_g, ln_ffn_b, loss_target, m_hyb_w_in, m_dn_conv_w, m_dn_a_log, m_dn_dt_bias, m_dn_norm_g, m_hyb_w_out, m_s5_a_re, m_s5_a_im, m_s5_log_dt, m_s5_b_re, m_s5_b_im, m_s5_c_re, m_s5_c_im, m_s5_d, m_s5_glu_wo, m_s5_glu_wg, m_ln_mix_g, m_ln_mix_b, m_xq_w, m_xk_w, m_xv_w, m_xo_w, m_ln_x_g, m_ln_x_b, m_ffn_wg, m_ffn_wu, m_ffn_wd, m_ln_ffn_g, m_ln_ffn_b, v_hyb_w_in, v_dn_conv_w, v_dn_a_log, v_dn_dt_bias, v_dn_norm_g, v_hyb_w_out, v_s5_a_re, v_s5_a_im, v_s5_log_dt, v_s5_b_re, v_s5_b_im, v_s5_c_re, v_s5_c_im, v_s5_d, v_s5_glu_wo, v_s5_glu_wg, v_ln_mix_g, v_ln_mix_b, v_xq_w, v_xk_w, v_xv_w, v_xo_w, v_ln_x_g, v_ln_x_b, v_ffn_wg, v_ffn_wu, v_ffn_wd, v_ln_ffn_g, v_ln_ffn_b):
    a = dict(locals())
    sharded = [n for n in WEIGHT_NAMES if n in SHARD_AXIS]
    replicated = [n for n in WEIGHT_NAMES if n not in SHARD_AXIS]
    big = [n for n in sharded if n not in GATHER_F32]

    gath_big = comm_allgather4("gather_w_bf16", _pack([a[n] for n in big], bf16, 16))
    gath_small = comm_allgather4("gather_w_f32", _pack([a[n] for n in GATHER_F32], f32, 8))
    p = {n: a[n] for n in replicated}
    for names, gath in ((big, gath_big), (GATHER_F32, gath_small)):
        parts = _unpack_slots(gath, [a[n].shape for n in names])
        for n, g4 in zip(names, parts):
            p[n] = _gathered_to_full(g4, SHARD_AXIS[n])

    loss, grad_x, grads = local_step(x[0], mem[0], positions, loss_target[0], p)
    loss = lax.psum(loss, ("x", "y", "c"))

    gpack = jnp.stack([_pack([_full_to_shard_major(grads[n], SHARD_AXIS[n])[j] for n in sharded], f32, 2 * GRAD_ROW_MULTIPLE)
                       for j in range(N_CHIPS)])
    rh = gpack.shape[1] // 2
    gpack = gpack.reshape(N_CHIPS, 2, rh, PACK_COLS)
    pair = add_own_half("rs_add_sibling", gpack, comm_sibling_halves("rs_sibling_halves", gpack))
    mine = sum_slots("rs_sum_chips", comm_alltoall4("rs_alltoall", pair))
    g_sh = comm_sibling_join("rs_sibling_join", mine).reshape(2 * rh, PACK_COLS)

    rpack = _pack([grads[n] for n in replicated], f32, 8)
    rpair = _add2("ar_add_sibling", rpack, comm_sibling_swap("ar_sibling_swap", rpack))
    g_rep = sum_slots("ar_sum_chips", comm_allgather4("ar_allgather", rpair))

    outs = {}
    for names, gp, mult in ((sharded, g_sh, 2 * GRAD_ROW_MULTIPLE), (replicated, g_rep, 8)):
        shapes = [a[n].shape for n in names]
        packs = [_pack([a[pre + n] for n in names], f32, mult) for pre in ("", "m_", "v_")]
        delta, new_m, new_v = adamw("adamw_" + ("sharded" if names is sharded else "replicated"), packs[0], gp, *packs[1:])
        for kind, buf in (("grad", gp), ("delta", delta), ("new_m", new_m), ("new_v", new_v)):
            for n, val in zip(names, _unpack(buf, shapes)):
                outs[kind, n] = val
    res = [loss, grad_x[None]]
    for kind in ("grad", "delta", "new_m", "new_v"):
        res += [outs[kind, n] for n in WEIGHT_NAMES]
    return tuple(res)


def _unpack_slots(gathered, shapes):
    flat, out, off = gathered.reshape(N_CHIPS, -1), [], 0
    for shp in shapes:
        n = math.prod(shp)
        out.append(flat[:, off:off + n].reshape((N_CHIPS,) + tuple(shp)))
        off += n
    return out
```

```python
import functools
import math

import jax
import jax.numpy as jnp
from jax import lax
from jax.experimental import pallas as pl
from jax.experimental.pallas import tpu as pltpu

f32 = jnp.float32
bf16 = jnp.bfloat16

D_MODEL = 1024
DEPTH = 4
DN_HEADS = 4
DN_HEAD_DIM = 128
DN_KEY_DIM = 512
DN_QKV_DIM = 1536
DN_CONV = 4
SW_HEADS = 8
SW_HEAD_DIM = 64
SW_DIM = 512
SW_DILATIONS = (1, 4, 16)
SW_BLOCK = 128
ROPE_THETA = 10000.0
S5_GROUP = 16
S5_GROUPS = 64
S5_STATE = 64
X_HEADS = 4
X_HEAD_DIM = 256
FFN_HIDDEN = 2816
ALPHA = (2 * DEPTH) ** 0.25
LN_EPS = 1e-5
RMS_EPS = 1e-6
ADAM_LR, ADAM_B1, ADAM_B2, ADAM_EPS, ADAM_WD, ADAM_STEP = 0.001, 0.9, 0.999, 1e-08, 0.01, 10

BA_PAD = 256
PROJ_COLS = DN_QKV_DIM + DN_KEY_DIM + 3 * SW_DIM + BA_PAD
COL_Z = DN_QKV_DIM
COL_SWQ = COL_Z + DN_KEY_DIM
COL_SWK = COL_SWQ + SW_DIM
COL_SWV = COL_SWK + SW_DIM
COL_BA = COL_SWV + SW_DIM

LANES = 128
SUBLANES = 8
VMEM_LIMIT = 56 * 1024 * 1024
DN_CHUNK = 128


def _cparams(*sem):
    return pltpu.CompilerParams(dimension_semantics=tuple(sem), vmem_limit_bytes=VMEM_LIMIT)


def _dg(x, y, cx, cy):
    return lax.dot_general(x, y, (((cx,), (cy,)), ((), ())), preferred_element_type=f32)


@functools.partial(jax.custom_vjp, nondiff_argnums=(2, 3))
def bdot(a, b, ca, cb):
    return _dg(a.astype(bf16), b.astype(bf16), ca, cb)


def _bdot_fwd(a, b, ca, cb):
    return bdot(a, b, ca, cb), (a, b)


def _bdot_bwd(ca, cb, res, g):
    a, b = res
    g16, a16, b16 = g.astype(bf16), a.astype(bf16), b.astype(bf16)
    da = _dg(g16, b16, 1, 1 - cb) if ca == 1 else _dg(b16, g16, 1 - cb, 1)
    db = _dg(a16, g16, 1 - ca, 0) if cb == 0 else _dg(g16, a16, 0, 1 - ca)
    return da.astype(a.dtype), db.astype(b.dtype)


bdot.defvjp(_bdot_fwd, _bdot_bwd)


def hdot(a, b):
    return jnp.dot(a, b, precision=lax.Precision.HIGHEST, preferred_element_type=f32)


def _iota2(shape, dim):
    return lax.broadcasted_iota(jnp.int32, shape, dim)


def _row_spec(r, tm):
    if isinstance(r, tuple):
        arr, width, blk = r
        return arr, pl.BlockSpec((tm, width), lambda i, _b=blk: (i, _b))
    return r, pl.BlockSpec((tm, r.shape[1]), lambda i: (i, 0))


def _par_spec(p):
    return pl.BlockSpec(p.shape, lambda i, _n=p.ndim: (0,) * _n)


def rowmap(name, fn, rows, params, out_cols, tm, out_dtypes=None):
    arrs, specs = zip(*[_row_spec(r, tm) for r in rows])
    s = arrs[0].shape[0]
    n_in = len(rows) + len(params)
    out_dtypes = out_dtypes or [f32] * len(out_cols)

    def body(*refs):
        outs = fn(*[r[...] for r in refs[:n_in]])
        for o_ref, o in zip(refs[n_in:], outs):
            o_ref[...] = o.astype(o_ref.dtype)

    return pl.pallas_call(
        body, grid=(s // tm,),
        in_specs=list(specs) + [_par_spec(p) for p in params],
        out_specs=[pl.BlockSpec((tm, c), lambda i: (i, 0)) for c in out_cols],
        out_shape=[jax.ShapeDtypeStruct((s, c), dt) for c, dt in zip(out_cols, out_dtypes)],
        compiler_params=_cparams("parallel"), name=name)(*arrs, *params)


def rowmap_bwd(name, fn, rows, params, cts, tm, row_mask=None, par_mask=None):
    arrs, specs = zip(*[_row_spec(r, tm) for r in rows])
    s = arrs[0].shape[0]
    ct_groups = [c if isinstance(c, list) else [c] for c in cts]
    ct_arrs, ct_specs = zip(*[_row_spec(a, tm) for grp in ct_groups for a in grp])
    cts = list(ct_arrs)
    nr, npar, nct = len(rows), len(params), len(cts)
    row_mask = row_mask or [True] * nr
    par_mask = par_mask or [True] * npar
    row_idx = [k for k in range(nr) if row_mask[k]]
    par_idx = [k for k in range(npar) if par_mask[k]]
    row_w = [specs[k].block_shape[1] for k in row_idx]

    def body(*refs):
        ins = [r[...] for r in refs[:nr + npar]]
        ct_refs = list(refs[nr + npar:nr + npar + nct])
        ctv = []
        for grp in ct_groups:
            acc = ct_refs.pop(0)[...]
            for _ in grp[1:]:
                acc = acc + ct_refs.pop(0)[...]
            ctv.append(acc)
        ctv = tuple(ctv)
        outs = refs[nr + npar + nct:]
        _, vjp = jax.vjp(fn, *ins)
        grads = vjp(ctv)
        for o_ref, k in zip(outs[:len(row_idx)], row_idx):
            o_ref[...] = grads[k].astype(o_ref.dtype)
        first = pl.program_id(0) == 0
        for o_ref, k in zip(outs[len(row_idx):], par_idx):
            g = grads[nr + k].astype(f32)

            @pl.when(first)
            def _(o_ref=o_ref, g=g):
                o_ref[...] = g

            @pl.when(jnp.logical_not(first))
            def _(o_ref=o_ref, g=g):
                o_ref[...] += g

    res = pl.pallas_call(
        body, grid=(s // tm,),
        in_specs=list(specs) + [_par_spec(p) for p in params]
        + list(ct_specs),
        out_specs=[pl.BlockSpec((tm, w), lambda i: (i, 0)) for w in row_w]
        + [_par_spec(params[k]) for k in par_idx],
        out_shape=[jax.ShapeDtypeStruct((s, w), f32) for w in row_w]
        + [jax.ShapeDtypeStruct(params[k].shape, f32) for k in par_idx],
        compiler_params=_cparams("arbitrary"), name=name)(*arrs, *params, *cts)
    return list(res[:len(row_idx)]), list(res[len(row_idx):])


def _pick(n, prefs):
    for t in prefs:
        if n % t == 0:
            return t
    return n


def mm_nn(name, a, b, out_dtype=f32):
    m, k = a.shape
    n = b.shape[1]
    tm, tn = _pick(m, (512, 256, 128)), _pick(n, (512, 256, 128))

    def body(a_ref, b_ref, o_ref):
        o_ref[...] = _dg(a_ref[...].astype(bf16), b_ref[...].astype(bf16), 1, 0).astype(o_ref.dtype)

    return pl.pallas_call(
        body, grid=(m // tm, n // tn),
        in_specs=[pl.BlockSpec((tm, k), lambda i, j: (i, 0)), pl.BlockSpec((k, tn), lambda i, j: (0, j))],
        out_specs=pl.BlockSpec((tm, tn), lambda i, j: (i, j)),
        out_shape=jax.ShapeDtypeStruct((m, n), out_dtype),
        compiler_params=_cparams("parallel", "parallel"), name=name)(a, b)


def mm_nt(name, a, b, out_dtype=f32):
    m, n = a.shape
    k = b.shape[0]
    tm, tk = _pick(m, (256, 128)), _pick(k, (512, 256, 128))

    def body(a_ref, b_ref, o_ref):
        o_ref[...] = _dg(a_ref[...].astype(bf16), b_ref[...].astype(bf16), 1, 1).astype(o_ref.dtype)

    return pl.pallas_call(
        body, grid=(m // tm, k // tk),
        in_specs=[pl.BlockSpec((tm, n), lambda i, j: (i, 0)), pl.BlockSpec((tk, n), lambda i, j: (j, 0))],
        out_specs=pl.BlockSpec((tm, tk), lambda i, j: (i, j)),
        out_shape=jax.ShapeDtypeStruct((m, k), out_dtype),
        compiler_params=_cparams("parallel", "parallel"), name=name)(a, b)


def mm_tn(name, a, b, out_dtype=f32):
    s, m = a.shape
    n = b.shape[1]
    tm, tn = _pick(m, (256, 128)), _pick(n, (256, 128))

    def body(a_ref, b_ref, o_ref):
        o_ref[...] = _dg(a_ref[...].astype(bf16), b_ref[...].astype(bf16), 0, 0).astype(o_ref.dtype)

    return pl.pallas_call(
        body, grid=(m // tm, n // tn),
        in_specs=[pl.BlockSpec((s, tm), lambda i, j: (0, i)), pl.BlockSpec((s, tn), lambda i, j: (0, j))],
        out_specs=pl.BlockSpec((tm, tn), lambda i, j: (i, j)),
        out_shape=jax.ShapeDtypeStruct((m, n), out_dtype),
        compiler_params=_cparams("parallel", "parallel"), name=name)(a, b)


def _postnorm_tile(h, sub, g, b):
    z = ALPHA * h + sub
    mu = jnp.mean(z, -1, keepdims=True)
    zc = z - mu
    var = jnp.mean(zc * zc, -1, keepdims=True)
    return (zc * lax.rsqrt(var + LN_EPS) * g + b,)


def _swiglu_tile(au):
    a, u = au[:, :FFN_HIDDEN], au[:, FFN_HIDDEN:]
    return (jax.nn.silu(a) * u,)


def _glu_tile(og):
    o, g = og[:, :D_MODEL], og[:, D_MODEL:]
    return (o * jax.nn.sigmoid(g),)


def _xattn_tile(q, kv):
    outs = []
    for h in range(X_HEADS):
        sl = slice(h * X_HEAD_DIM, (h + 1) * X_HEAD_DIM)
        s = bdot(q[:, sl], kv[:, sl], 1, 1) * (X_HEAD_DIM ** -0.5)
        m = lax.stop_gradient(jnp.max(s, -1, keepdims=True))
        p = jnp.exp(s - m)
        p = p / jnp.sum(p, -1, keepdims=True)
        outs.append(bdot(p, kv[:, D_MODEL + h * X_HEAD_DIM:D_MODEL + (h + 1) * X_HEAD_DIM], 1, 0))
    return (jnp.concatenate(outs, -1),)


TM_ROW = 256


def postnorm_fwd(tag, h, sub, g, b):
    return rowmap("postnorm_" + tag, _postnorm_tile, [h, sub], [g, b], [D_MODEL], TM_ROW)[0]


def postnorm_bwd(tag, h, sub, g, b, dy):
    (dh, dsub), (dg, db) = rowmap_bwd("postnorm_bwd_" + tag, _postnorm_tile, [h, sub], [g, b], [dy], TM_ROW)
    return dh, dsub, dg, db


def xattn_fwd(tag, h, mem, wq, wkv, wo):
    q = mm_nn("xq_" + tag, h, wq)
    kv = mm_nn("xkv_" + tag, mem, wkv)
    ao = rowmap("xattn_" + tag, _xattn_tile, [q], [kv], [D_MODEL], TM_ROW)[0]
    out = mm_nn("xo_" + tag, ao, wo)
    return out, (q, kv, ao)


def xattn_bwd(tag, h, mem, wq, wkv, wo, res, dout):
    q, kv, ao = res
    dwo = mm_tn("xo_dw_" + tag, ao, dout)
    dao = mm_nt("xo_dx_" + tag, dout, wo)
    (dq,), (dkv,) = rowmap_bwd("xattn_bwd_" + tag, _xattn_tile, [q], [kv], [dao], TM_ROW)
    dwq = mm_tn("xq_dw_" + tag, h, dq)
    dh = mm_nt("xq_dx_" + tag, dq, wq)
    dwkv = mm_tn("xkv_dw_" + tag, mem, dkv)
    return dh, dwq, dwkv, dwo


FFN_SHARD = FFN_HIDDEN // 4
TM_FFN = 512


def _silu_mul(a, u):
    return jax.nn.silu(a) * u


def ffn_fwd(tag, layer, h, wg, wu, wd):
    s = h.shape[0]
    tm, fs = TM_FFN, FFN_SHARD
    w_in = pl.BlockSpec((None, None, D_MODEL, fs), lambda i, k: (k, layer, 0, 0))
    act = pl.BlockSpec((None, tm, fs), lambda i, k: (k, i, 0))

    def up_body(h_ref, wg_ref, wu_ref, a_ref, u_ref, hid_ref):
        hv = h_ref[...].astype(bf16)
        a, u = _dg(hv, wg_ref[...], 1, 0), _dg(hv, wu_ref[...], 1, 0)
        a_ref[...], u_ref[...] = a, u
        hid_ref[...] = _silu_mul(a, u).astype(bf16)

    a4, u4, hid4 = pl.pallas_call(
        up_body, grid=(s // tm, 4),
        in_specs=[pl.BlockSpec((tm, D_MODEL), lambda i, k: (i, 0)), w_in, w_in],
        out_specs=[act, act, act],
        out_shape=[jax.ShapeDtypeStruct((4, s, fs), f32)] * 2 + [jax.ShapeDtypeStruct((4, s, fs), bf16)],
        compiler_params=_cparams("parallel", "parallel"), name="ffn_up_" + tag)(h, wg, wu)

    def down_body(hid_ref, wd_ref, o_ref):
        part = _dg(hid_ref[...], wd_ref[...], 1, 0)

        @pl.when(pl.program_id(1) == 0)
        def _():
            o_ref[...] = part

        @pl.when(pl.program_id(1) != 0)
        def _():
            o_ref[...] += part

    out = pl.pallas_call(
        down_body, grid=(s // tm, 4),
        in_specs=[act, pl.BlockSpec((None, None, fs, D_MODEL), lambda i, k: (k, layer, 0, 0))],
        out_specs=pl.BlockSpec((tm, D_MODEL), lambda i, k: (i, 0)),
        out_shape=jax.ShapeDtypeStruct((s, D_MODEL), f32),
        compiler_params=_cparams("parallel", "arbitrary"), name="ffn_down_" + tag)(hid4, wd)
    return out, (a4, u4, hid4)


def ffn_bwd(tag, layer, h, wg, wu, wd, res, dout):
    a4, u4, hid4 = res
    s = h.shape[0]
    tm, fs = TM_FFN, FFN_SHARD
    act = pl.BlockSpec((None, tm, fs), lambda i, k: (k, i, 0))
    w_in = pl.BlockSpec((None, None, D_MODEL, fs), lambda i, k: (k, layer, 0, 0))

    def dact_body(do_ref, wd_ref, a_ref, u_ref, da_ref, du_ref):
        dhid = _dg(do_ref[...].astype(bf16), wd_ref[...], 1, 1)
        _, vjp = jax.vjp(_silu_mul, a_ref[...], u_ref[...])
        da, du = vjp(dhid)
        da_ref[...], du_ref[...] = da.astype(bf16), du.astype(bf16)

    da4, du4 = pl.pallas_call(
        dact_body, grid=(s // tm, 4),
        in_specs=[pl.BlockSpec((tm, D_MODEL), lambda i, k: (i, 0)),
                  pl.BlockSpec((None, None, fs, D_MODEL), lambda i, k: (k, layer, 0, 0)), act, act],
        out_specs=[act, act], out_shape=[jax.ShapeDtypeStruct((4, s, fs), bf16)] * 2,
        compiler_params=_cparams("parallel", "parallel"), name="ffn_dact_" + tag)(dout, wd, a4, u4)

    def dx_body(da_ref, du_ref, wg_ref, wu_ref, o_ref):
        part = _dg(da_ref[...], wg_ref[...], 1, 1) + _dg(du_ref[...], wu_ref[...], 1, 1)

        @pl.when(pl.program_id(1) == 0)
        def _():
            o_ref[...] = part

        @pl.when(pl.program_id(1) != 0)
        def _():
            o_ref[...] += part

    dh = pl.pallas_call(
        dx_body, grid=(s // tm, 4), in_specs=[act, act, w_in, w_in],
        out_specs=pl.BlockSpec((tm, D_MODEL), lambda i, k: (i, 0)),
        out_shape=jax.ShapeDtypeStruct((s, D_MODEL), f32),
        compiler_params=_cparams("parallel", "arbitrary"), name="ffn_dx_" + tag)(da4, du4, wg, wu)

    tn = 256
    whole = pl.BlockSpec((None, s, fs), lambda k, j: (k, 0, 0))

    def dwin_body(h_ref, da_ref, du_ref, dwg_ref, dwu_ref):
        hv = h_ref[...].astype(bf16)
        dwg_ref[...] = _dg(hv, da_ref[...], 0, 0)
        dwu_ref[...] = _dg(hv, du_ref[...], 0, 0)

    dwg, dwu = pl.pallas_call(
        dwin_body, grid=(4, D_MODEL // tn),
        in_specs=[pl.BlockSpec((s, tn), lambda k, j: (0, j)), whole, whole],
        out_specs=[pl.BlockSpec((None, tn, fs), lambda k, j: (k, j, 0))] * 2,
        out_shape=[jax.ShapeDtypeStruct((4, D_MODEL, fs), f32)] * 2,
        compiler_params=_cparams("parallel", "parallel"), name="ffn_dwin_" + tag)(h, da4, du4)

    def dwd_body(hid_ref, do_ref, dwd_ref):
        dwd_ref[...] = _dg(hid_ref[...], do_ref[...].astype(bf16), 0, 0)

    dwd = pl.pallas_call(
        dwd_body, grid=(4, D_MODEL // tn),
        in_specs=[whole, pl.BlockSpec((s, tn), lambda k, j: (0, j))],
        out_specs=pl.BlockSpec((None, fs, tn), lambda k, j: (k, 0, j)),
        out_shape=jax.ShapeDtypeStruct((4, fs, D_MODEL), f32),
        compiler_params=_cparams("parallel", "parallel"), name="ffn_dwd_" + tag)(hid4, dout)
    return dh, dwg, dwu, dwd


def loss_head(y, target):
    s, d = y.shape
    tm = TM_ROW

    def body(y_ref, t_ref, part_ref, dy_ref):
        e = y_ref[...] - t_ref[...]
        dy_ref[...] = e * (1.0 / d)
        p = jnp.sum(e * e, 0, keepdims=True) * (0.5 / d)

        @pl.when(pl.program_id(0) == 0)
        def _():
            part_ref[...] = p

        @pl.when(pl.program_id(0) != 0)
        def _():
            part_ref[...] += p

    return pl.pallas_call(
        body, grid=(s // tm,),
        in_specs=[pl.BlockSpec((tm, d), lambda i: (i, 0))] * 2,
        out_specs=[pl.BlockSpec((1, d), lambda i: (0, 0)), pl.BlockSpec((tm, d), lambda i: (i, 0))],
        out_shape=[jax.ShapeDtypeStruct((1, d), f32), jax.ShapeDtypeStruct((s, d), f32)],
        compiler_params=_cparams("arbitrary"), name="loss_head")(y, target)


TM_CONV = 512


def _conv_rows(xx, w_ref, n_rows):
    a = w_ref[3:4, :] * xx
    for k in (1, 2, 3):
        a = a + w_ref[3 - k:4 - k, :] * pltpu.roll(xx, k, 0)
    return a


def _dn_act(a, is_qk):
    s = jax.nn.silu(a)
    n = s * lax.rsqrt(jnp.sum(s * s, -1, keepdims=True) + RMS_EPS)
    return jnp.where(is_qk, n, s)


def dn_conv_fwd(tag, proj, cw):
    s = proj.shape[0]
    tm, hb = TM_CONV, TM_CONV // SUBLANES

    def body(xh_ref, x_ref, w_ref, o_ref):
        j, t = pl.program_id(0), pl.program_id(1)
        halo = jnp.where(t > 0, xh_ref[...], 0.0)
        xx = jnp.concatenate([halo, x_ref[...]], 0)
        a = _conv_rows(xx, w_ref, tm + SUBLANES)
        o_ref[...] = _dn_act(a, j < 2 * DN_HEADS)[SUBLANES:, :]

    return pl.pallas_call(
        body, grid=(DN_QKV_DIM // LANES, s // tm),
        in_specs=[pl.BlockSpec((SUBLANES, LANES), lambda j, t: (jnp.maximum(t * hb - 1, 0), j)),
                  pl.BlockSpec((tm, LANES), lambda j, t: (t, j)),
                  pl.BlockSpec((DN_CONV, LANES), lambda j, t: (0, j))],
        out_specs=pl.BlockSpec((tm, LANES), lambda j, t: (t, j)),
        out_shape=jax.ShapeDtypeStruct((s, DN_QKV_DIM), f32),
        compiler_params=_cparams("parallel", "parallel"), name="dn_conv_" + tag)(proj, proj, cw)


def dn_conv_bwd(tag, proj, cw, dy):
    s = proj.shape[0]
    tm, hb = TM_CONV, TM_CONV // SUBLANES
    nt = s // tm
    n_ext = tm + 2 * SUBLANES

    def body(xb_ref, x_ref, xa_ref, dy_ref, dya_ref, w_ref, dx_ref, dw_ref):
        j, t = pl.program_id(0), pl.program_id(1)
        xx = jnp.concatenate([jnp.where(t > 0, xb_ref[...], 0.0), x_ref[...],
                              jnp.where(t < nt - 1, xa_ref[...], 0.0)], 0)
        dyy = jnp.concatenate([jnp.zeros((SUBLANES, LANES), f32), dy_ref[...],
                               jnp.where(t < nt - 1, dya_ref[...], 0.0)], 0)
        a = _conv_rows(xx, w_ref, n_ext)
        _, vjp = jax.vjp(lambda v: _dn_act(v, j < 2 * DN_HEADS), a)
        da, = vjp(dyy)
        dx = w_ref[3:4, :] * da
        for k in (1, 2, 3):
            dx = dx + w_ref[3 - k:4 - k, :] * pltpu.roll(da, n_ext - k, 0)
        dx_ref[...] = dx[SUBLANES:SUBLANES + tm, :]
        row = _iota2((n_ext, LANES), 0)
        da_in = jnp.where((row >= SUBLANES) & (row < SUBLANES + tm), da, 0.0)
        r8 = _iota2((SUBLANES, LANES), 0)
        dw = jnp.zeros((SUBLANES, LANES), f32)
        for k in range(DN_CONV):
            xs = xx if k == 0 else pltpu.roll(xx, k, 0)
            dw = dw + jnp.where(r8 == 3 - k, jnp.sum(da_in * xs, 0, keepdims=True), 0.0)

        @pl.when(t == 0)
        def _():
            dw_ref[...] = dw

        @pl.when(t != 0)
        def _():
            dw_ref[...] += dw

    nb8 = s // SUBLANES
    return pl.pallas_call(
        body, grid=(DN_QKV_DIM // LANES, nt),
        in_specs=[pl.BlockSpec((SUBLANES, LANES), lambda j, t: (jnp.maximum(t * hb - 1, 0), j)),
                  pl.BlockSpec((tm, LANES), lambda j, t: (t, j)),
                  pl.BlockSpec((SUBLANES, LANES), lambda j, t: (jnp.minimum((t + 1) * hb, nb8 - 1), j)),
                  pl.BlockSpec((tm, LANES), lambda j, t: (t, j)),
                  pl.BlockSpec((SUBLANES, LANES), lambda j, t: (jnp.minimum((t + 1) * hb, nb8 - 1), j)),
                  pl.BlockSpec((DN_CONV, LANES), lambda j, t: (0, j))],
        out_specs=[pl.BlockSpec((tm, LANES), lambda j, t: (t, j)),
                   pl.BlockSpec((SUBLANES, LANES), lambda j, t: (0, j))],
        out_shape=[jax.ShapeDtypeStruct((s, DN_QKV_DIM), f32), jax.ShapeDtypeStruct((SUBLANES, DN_QKV_DIM), f32)],
        compiler_params=_cparams("parallel", "arbitrary"), name="dn_conv_bwd_" + tag)(proj, proj, proj, dy, dy, cw)


def _gate_tile(ba, eb, ea, alog, dtb):
    beta = jax.nn.sigmoid(hdot(ba, eb))
    g = -jnp.exp(alog) * jax.nn.softplus(hdot(ba, ea) + dtb)
    return beta, g


def _tri_inv_unit(lower):
    c = lower.shape[0]
    r, col = _iota2((c, c), 0), _iota2((c, c), 1)
    inv = jnp.where(r == col, 1.0, 0.0).astype(f32)
    sh = 0
    while (1 << sh) < c:
        same_2b = lax.shift_right_logical(r, sh + 1) == lax.shift_right_logical(col, sh + 1)
        diff_b = lax.shift_right_logical(r, sh) != lax.shift_right_logical(col, sh)
        off = jnp.where(same_2b & diff_b, lower, 0.0)
        inv = inv - hdot(hdot(inv, off), inv)
        sh += 1
    return inv


def _delta_chunk(q, k, v, gb, betab, state):
    c = DN_CHUNK
    r, col = _iota2((c, c), 0), _iota2((c, c), 1)
    causal, strict = r >= col, r > col
    gc = hdot(jnp.where(causal, 1.0, 0.0).astype(f32), gb)
    diff = gc - gc.T
    decay = jnp.where(causal, jnp.exp(jnp.where(causal, diff, 0.0)), 0.0)
    qs = q * (DN_HEAD_DIM ** -0.5)
    kb = k * betab
    lower = jnp.where(strict, bdot(kb, k, 1, 1), 0.0) * decay
    tinv = _tri_inv_unit(lower)
    eg = jnp.exp(gc)
    u = hdot(tinv, v * betab)
    w = hdot(tinv, kb * eg)
    intra = bdot(qs, k, 1, 1) * decay
    gl = jnp.sum(jnp.where(r == c - 1, gc, 0.0), 0, keepdims=True)
    k_dec = k * jnp.exp(gl - gc)
    v_new = u - bdot(w, state, 1, 0)
    out = bdot(qs * eg, state, 1, 0) + bdot(intra, v_new, 1, 0)
    new_state = state * jnp.exp(gl) + bdot(k_dec, v_new, 0, 0)
    return out, new_state


def delta_fwd(tag, qkv, gb, betab):
    s = qkv.shape[0]
    c, hd = DN_CHUNK, DN_HEAD_DIM
    n = s // c

    def body(q_ref, k_ref, v_ref, g_ref, b_ref, o_ref, st_ref, state):
        @pl.when(pl.program_id(1) == 0)
        def _():
            state[...] = jnp.zeros_like(state)

        st = state[...]
        st_ref[...] = st
        out, new = _delta_chunk(q_ref[...], k_ref[...], v_ref[...], g_ref[...], b_ref[...], st)
        o_ref[...] = out
        state[...] = new

    blk = lambda off: pl.BlockSpec((c, hd), lambda h, i, _o=off: (i, h + _o))
    return pl.pallas_call(
        body, grid=(DN_HEADS, n),
        in_specs=[blk(0), blk(DN_HEADS), blk(2 * DN_HEADS), blk(0), blk(0)],
        out_specs=[blk(0), pl.BlockSpec((None, None, hd, hd), lambda h, i: (h, i, 0, 0))],
        out_shape=[jax.ShapeDtypeStruct((s, DN_KEY_DIM), f32), jax.ShapeDtypeStruct((DN_HEADS, n, hd, hd), f32)],
        scratch_shapes=[pltpu.VMEM((hd, hd), f32)],
        compiler_params=_cparams("parallel", "arbitrary"), name="delta_" + tag)(qkv, qkv, qkv, gb, betab)


def delta_bwd(tag, qkv, gb, betab, states, do):
    s = qkv.shape[0]
    c, hd = DN_CHUNK, DN_HEAD_DIM
    n = s // c

    def body(q_ref, k_ref, v_ref, g_ref, b_ref, st_ref, do_ref, dq_ref, dk_ref, dv_ref, dg_ref, db_ref, dstate):
        @pl.when(pl.program_id(1) == 0)
        def _():
            dstate[...] = jnp.zeros_like(dstate)

        _, vjp = jax.vjp(_delta_chunk, q_ref[...], k_ref[...], v_ref[...], g_ref[...], b_ref[...], st_ref[...])
        dq, dk, dv, dg, db, dst = vjp((do_ref[...], dstate[...]))
        dq_ref[...], dk_ref[...], dv_ref[...], dg_ref[...], db_ref[...] = dq, dk, dv, dg, db
        dstate[...] = dst

    blk = lambda off: pl.BlockSpec((c, hd), lambda h, i, _o=off: (n - 1 - i, h + _o))
    return pl.pallas_call(
        body, grid=(DN_HEADS, n),
        in_specs=[blk(0), blk(DN_HEADS), blk(2 * DN_HEADS), blk(0), blk(0),
                  pl.BlockSpec((None, None, hd, hd), lambda h, i: (h, n - 1 - i, 0, 0)), blk(0)],
        out_specs=[blk(0)] * 5,
        out_shape=[jax.ShapeDtypeStruct((s, DN_KEY_DIM), f32)] * 5,
        scratch_shapes=[pltpu.VMEM((hd, hd), f32)],
        compiler_params=_cparams("parallel", "arbitrary"), name="delta_bwd_" + tag)(qkv, qkv, qkv, gb, betab, states, do)


def _dn_out_tile(o, z, ng):
    outs = []
    for h in range(DN_HEADS):
        sl = slice(h * DN_HEAD_DIM, (h + 1) * DN_HEAD_DIM)
        oh = o[:, sl]
        nrm = oh * lax.rsqrt(jnp.mean(oh * oh, -1, keepdims=True) + RMS_EPS) * ng[:, sl]
        outs.append(nrm * jax.nn.silu(z[:, sl]))
    return (jnp.concatenate(outs, -1),)


def _head_selectors():
    r, c = _iota2((BA_PAD, DN_KEY_DIM), 0), _iota2((BA_PAD, DN_KEY_DIM), 1) // DN_HEAD_DIM
    return (r == c).astype(f32), (r == c + DN_HEADS).astype(f32)


def dn_mixer_fwd(tag, proj, cw, alog_b, dtb_b, ng_b):
    eb, ea = _head_selectors()
    ba = (proj, BA_PAD, COL_BA // BA_PAD)
    qkv = dn_conv_fwd(tag, proj, cw)
    betab, gb = rowmap("dn_gate_" + tag, _gate_tile, [ba], [eb, ea, alog_b, dtb_b], [DN_KEY_DIM] * 2, TM_ROW)
    o, states = delta_fwd(tag, qkv, gb, betab)
    z = (proj, DN_KEY_DIM, COL_Z // DN_KEY_DIM)
    a_out = rowmap("dn_out_" + tag, _dn_out_tile, [o, z], [ng_b], [DN_KEY_DIM], TM_ROW)[0]
    return a_out, (qkv, betab, gb, o, states)


def dn_mixer_bwd(tag, proj, cw, alog_b, dtb_b, ng_b, res, da_out):
    qkv, betab, gb, o, states = res
    eb, ea = _head_selectors()
    ba = (proj, BA_PAD, COL_BA // BA_PAD)
    z = (proj, DN_KEY_DIM, COL_Z // DN_KEY_DIM)
    (do, dz), (dng,) = rowmap_bwd("dn_out_bwd_" + tag, _dn_out_tile, [o, z], [ng_b], [da_out], TM_ROW)
    dq, dk, dv, dgb, dbetab = delta_bwd(tag, qkv, gb, betab, states, do)
    dqkv_raw, dcw = dn_conv_bwd(tag, proj, cw, jnp.concatenate([dq, dk, dv], 1))
    (dba,), (dalog, ddtb) = rowmap_bwd("dn_gate_bwd_" + tag, _gate_tile, [ba], [eb, ea, alog_b, dtb_b],
                                       [dbetab, dgb], TM_ROW, par_mask=[False, False, True, True])
    return dqkv_raw, dz, dba, dcw[:DN_CONV], dalog, ddtb, dng


def _swap_halves(x):
    n = x.shape[1]
    first = (_iota2((1, n), 1) % SW_HEAD_DIM) < SW_HEAD_DIM // 2
    return jnp.where(first, pltpu.roll(x, n - SW_HEAD_DIM // 2, 1), pltpu.roll(x, SW_HEAD_DIM // 2, 1))


def _rope_apply(x, cos, sin_signed):
    return x * cos + _swap_halves(x) * sin_signed


def _rope_transpose(dy, cos, sin_signed):
    return dy * cos + _swap_halves(dy * sin_signed)


def rope_tables(positions, s):
    half = SW_HEAD_DIM // 2
    inv_freq = ROPE_THETA ** (-jnp.arange(0, SW_HEAD_DIM, 2, dtype=f32) / SW_HEAD_DIM)
    ang = positions.reshape(s, 1).astype(f32) * inv_freq[None, :]
    cos, sin = jnp.cos(ang), jnp.sin(ang)
    cos_t = jnp.tile(jnp.concatenate([cos, cos], 1), (1, SW_HEADS))
    sin_t = jnp.tile(jnp.concatenate([-sin, sin], 1), (1, SW_HEADS))
    assert cos_t.shape == (s, SW_DIM) and half * 2 == SW_HEAD_DIM
    return cos_t, sin_t


def rope_fwd(tag, proj, cos, sin):
    def fn(q, k, v, c, sg):
        return _rope_apply(q, c, sg), _rope_apply(k, c, sg), v

    rows = [(proj, SW_DIM, COL_SWQ // SW_DIM), (proj, SW_DIM, COL_SWK // SW_DIM), (proj, SW_DIM, COL_SWV // SW_DIM), cos, sin]
    return rowmap("rope_" + tag, fn, rows, [], [SW_DIM] * 3, TM_ROW)


def _swa_block(q, kp, kc, vp, vc, first):
    blk = SW_BLOCK
    kk = jnp.concatenate([kp, kc], 0)
    vv = jnp.concatenate([vp, vc], 0)
    dist = (_iota2((blk, 2 * blk), 0) + blk) - _iota2((blk, 2 * blk), 1)
    kj = _iota2((blk, 2 * blk), 1)
    valid = (dist >= 0) & (dist <= blk) & ((kj >= blk) | jnp.logical_not(first))
    lane_head = _iota2((1, LANES), 1) // SW_HEAD_DIM
    outs, lses = [], []
    for p in range(SW_DIM // LANES):
        sl = slice(p * LANES, (p + 1) * LANES)
        qp, kp_, vp_ = q[:, sl], kk[:, sl], vv[:, sl]
        o_pair = jnp.zeros((blk, LANES), f32)
        l_pair = jnp.zeros((blk, LANES), f32)
        for e in range(LANES // SW_HEAD_DIM):
            msk = lane_head == e
            sc = bdot(jnp.where(msk, qp, 0.0), kp_, 1, 1) * (SW_HEAD_DIM ** -0.5)
            sc = jnp.where(valid, sc, -1e30)
            m = lax.stop_gradient(jnp.max(sc, -1, keepdims=True))
            pe = jnp.exp(sc - m)
            l = jnp.sum(pe, -1, keepdims=True)
            o = bdot(pe, vp_, 1, 0) / l
            o_pair = o_pair + jnp.where(msk, o, 0.0)
            l_pair = l_pair + jnp.where(msk, m + jnp.log(l), 0.0)
        outs.append(o_pair)
        lses.append(l_pair)
    return jnp.concatenate(outs, -1), jnp.concatenate(lses, -1)


def _swa_specs(r):
    cur = pl.BlockSpec((SW_BLOCK, SW_DIM), lambda rho, n: (n, rho))
    prev = pl.BlockSpec((SW_BLOCK, SW_DIM), lambda rho, n: (jnp.maximum(n - 1, 0), rho))
    return cur, prev


def swa_fwd(tag, r, q, k, v):
    s = q.shape[0]
    ln = s // r
    q2, k2, v2 = (t.reshape(ln, r * SW_DIM) for t in (q, k, v))
    cur, prev = _swa_specs(r)

    def body(q_ref, kp_ref, kc_ref, vp_ref, vc_ref, o_ref, l_ref):
        o, l = _swa_block(q_ref[...], kp_ref[...], kc_ref[...], vp_ref[...], vc_ref[...], pl.program_id(1) == 0)
        o_ref[...] = o
        l_ref[...] = l

    o, l = pl.pallas_call(
        body, grid=(r, ln // SW_BLOCK),
        in_specs=[cur, prev, cur, prev, cur], out_specs=[cur, cur],
        out_shape=[jax.ShapeDtypeStruct((ln, r * SW_DIM), f32)] * 2,
        compiler_params=_cparams("parallel", "parallel"), name=f"swa{r}_{tag}")(q2, k2, k2, v2, v2)
    return o.reshape(s, SW_DIM), l.reshape(s, SW_DIM)


def swa_bwd(tag, r, q, k, v, do, dl):
    s = q.shape[0]
    ln = s // r
    q2, k2, v2, do2, dl2 = (t.reshape(ln, r * SW_DIM) for t in (q, k, v, do, dl))
    cur, prev = _swa_specs(r)

    def body(q_ref, kp_ref, kc_ref, vp_ref, vc_ref, do_ref, dl_ref, dq_ref, dka_ref, dkb_ref, dva_ref, dvb_ref):
        first = pl.program_id(1) == 0
        _, vjp = jax.vjp(lambda *a: _swa_block(*a, first), q_ref[...], kp_ref[...], kc_ref[...], vp_ref[...], vc_ref[...])
        dq_ref[...], dka_ref[...], dkb_ref[...], dva_ref[...], dvb_ref[...] = vjp((do_ref[...], dl_ref[...]))

    outs = pl.pallas_call(
        body, grid=(r, ln // SW_BLOCK),
        in_specs=[cur, prev, cur, prev, cur, cur, cur], out_specs=[cur] * 5,
        out_shape=[jax.ShapeDtypeStruct((ln, r * SW_DIM), f32)] * 5,
        compiler_params=_cparams("parallel", "parallel"), name=f"swa{r}_bwd_{tag}")(q2, k2, k2, v2, v2, do2, dl2)
    return [t.reshape(s, SW_DIM) for t in outs]


def _combine_tile(o1, l1, o2, l2, o3, l3):
    m = lax.stop_gradient(jnp.maximum(jnp.maximum(l1, l2), l3))
    e1, e2, e3 = jnp.exp(l1 - m), jnp.exp(l2 - m), jnp.exp(l3 - m)
    return ((o1 * e1 + o2 * e2 + o3 * e3) / (e1 + e2 + e3),)


def swa_merge_bwd(tag, grads, cos, sin):
    s = cos.shape[0]
    tm = SW_BLOCK
    nt = s // tm
    here = pl.BlockSpec((tm, SW_DIM), lambda i: (i, 0))
    arrs, specs = [], []
    for r, g in zip(SW_DILATIONS, grads):
        ahead = pl.BlockSpec((tm, SW_DIM), lambda i, _r=r: (jnp.minimum(i + _r, nt - 1), 0))
        arrs += g
        specs += [here, ahead, here, ahead, here]

    def body(*refs):
        i = pl.program_id(0)
        c_ref, s_ref = refs[15], refs[16]
        dq_ref, dk_ref, dv_ref = refs[17:]
        dq = jnp.zeros((tm, SW_DIM), f32)
        dk = jnp.zeros((tm, SW_DIM), f32)
        dv = jnp.zeros((tm, SW_DIM), f32)
        for b, r in enumerate(SW_DILATIONS):
            gq, gka, gkb, gva, gvb = refs[5 * b:5 * b + 5]
            inside = i + r < nt
            dq = dq + gq[...]
            dk = dk + gkb[...] + jnp.where(inside, gka[...], 0.0)
            dv = dv + gvb[...] + jnp.where(inside, gva[...], 0.0)
        dq_ref[...] = _rope_transpose(dq, c_ref[...], s_ref[...])
        dk_ref[...] = _rope_transpose(dk, c_ref[...], s_ref[...])
        dv_ref[...] = dv

    return pl.pallas_call(
        body, grid=(nt,), in_specs=specs + [here, here], out_specs=[here] * 3,
        out_shape=[jax.ShapeDtypeStruct((s, SW_DIM), f32)] * 3,
        compiler_params=_cparams("parallel"), name="swa_merge_bwd_" + tag)(*arrs, cos, sin)


def swa_mixer_fwd(tag, proj, cos, sin):
    q, k, v = rope_fwd(tag, proj, cos, sin)
    ols = []
    for r in SW_DILATIONS:
        ols += list(swa_fwd(tag, r, q, k, v))
    b_out = rowmap("swa_comb_" + tag, _combine_tile, ols, [], [SW_DIM], TM_ROW)[0]
    return b_out, (q, k, v, ols)


def swa_mixer_bwd(tag, cos, sin, res, db_out):
    q, k, v, ols = res
    dols, _ = rowmap_bwd("swa_comb_bwd_" + tag, _combine_tile, ols, [], [db_out], TM_ROW)
    grads = [swa_bwd(tag, r, q, k, v, dols[2 * b], dols[2 * b + 1]) for b, r in enumerate(SW_DILATIONS)]
    return swa_merge_bwd(tag, grads, cos, sin)


TM_S5 = 256
S5_LANES = 2 * S5_STATE
S5_GPB = LANES // S5_GROUP
S5_NBLK = D_MODEL // LANES
S5_BW = S5_GPB * S5_LANES
S5_WIDTH = S5_GROUPS * S5_LANES


def _swap_ri(x):
    n = x.shape[1]
    first = (_iota2((1, n), 1) % S5_LANES) < S5_STATE
    return jnp.where(first, pltpu.roll(x, n - S5_STATE, 1), pltpu.roll(x, S5_STATE, 1))


def _s5_disc_tile(a_re, a_im, log_dt, b_re, b_im, expand):
    dt = jnp.exp(log_dt)
    mag = jnp.exp(a_re * dt)
    abar_re, abar_im = mag * jnp.cos(a_im * dt), mag * jnp.sin(a_im * dt)
    n_re, n_im = abar_re - 1.0, abar_im
    den = a_re * a_re + a_im * a_im
    c_re = (n_re * a_re + n_im * a_im) / den
    c_im = (n_im * a_re - n_re * a_im) / den
    cx_re, cx_im = hdot(c_re, expand), hdot(c_im, expand)
    return abar_re, abar_im, cx_re * b_re - cx_im * b_im, cx_re * b_im + cx_im * b_re


def _s5_expand():
    return (_iota2((S5_STATE, S5_STATE * S5_GROUP), 1) // S5_GROUP == _iota2((S5_STATE, S5_STATE * S5_GROUP), 0)).astype(f32)


def _lane_layout(re, im):
    return jnp.concatenate([re, im], 1).reshape(1, S5_WIDTH)


def s5_tables(a_re, a_im, log_dt):
    dt = jnp.broadcast_to(log_dt.reshape(S5_GROUPS, 1), (S5_GROUPS, S5_STATE))
    are_l, aim_l, ldt_l = _lane_layout(a_re, a_re), _lane_layout(a_im, a_im), _lane_layout(dt, dt)
    t = TM_S5

    def body(are_ref, aim_ref, ldt_ref, a1_ref, a2_ref, a1r_ref, a2r_ref):
        dtv = jnp.exp(ldt_ref[...])
        lre, lim = are_ref[...] * dtv, aim_ref[...] * dtv
        sign = jnp.where((_iota2((1, S5_BW), 1) % S5_LANES) < S5_STATE, -1.0, 1.0)
        row = _iota2((t, S5_BW), 0)
        for asc, o1, o2 in ((True, a1_ref, a2_ref), (False, a1r_ref, a2r_ref)):
            n = (row + 1 if asc else t - row).astype(f32)
            mag = jnp.exp(n * lre)
            o1[...] = mag * jnp.cos(n * lim)
            o2[...] = sign * mag * jnp.sin(n * lim)

    lane = pl.BlockSpec((1, S5_BW), lambda j: (0, j))
    tab = pl.BlockSpec((t, S5_BW), lambda j: (0, j))
    return pl.pallas_call(
        body, grid=(S5_NBLK,), in_specs=[lane] * 3, out_specs=[tab] * 4,
        out_shape=[jax.ShapeDtypeStruct((t, S5_WIDTH), f32)] * 4,
        compiler_params=_cparams("parallel"), name="s5_tables")(are_l, aim_l, ldt_l)


def s5_pack_weights(bbar_re, bbar_im, c_re, c_im):
    eye = jnp.eye(S5_GPB, dtype=f32)
    bb = jnp.stack([bbar_re.reshape(S5_GROUPS, S5_STATE, S5_GROUP), bbar_im.reshape(S5_GROUPS, S5_STATE, S5_GROUP)], 1)
    bb = bb.transpose(0, 3, 1, 2).reshape(S5_NBLK, S5_GPB, S5_GROUP, S5_LANES)
    wb = (bb[:, :, :, None, :] * eye[None, :, None, :, None]).reshape(S5_NBLK, LANES, S5_BW)
    cc = jnp.stack([c_re, -c_im], 1)
    cc = cc.transpose(0, 1, 3, 2).reshape(S5_NBLK, S5_GPB, S5_LANES, S5_GROUP)
    wc = (cc[:, :, :, None, :] * eye[None, :, None, :, None]).reshape(S5_NBLK, S5_BW, LANES)
    return wb, wc


def s5_unpack_weight_grads(dwb, dwc):
    d5 = dwb.reshape(S5_NBLK, S5_GPB, S5_GROUP, S5_GPB, S5_LANES)
    dbb = jnp.stack([d5[:, gl, :, gl, :] for gl in range(S5_GPB)])
    dbb = dbb.transpose(1, 0, 2, 3).reshape(S5_GROUPS, S5_GROUP, 2, S5_STATE).transpose(0, 2, 3, 1)
    dbbar_re = dbb[:, 0].reshape(S5_GROUPS, S5_STATE * S5_GROUP)
    dbbar_im = dbb[:, 1].reshape(S5_GROUPS, S5_STATE * S5_GROUP)
    c5 = dwc.reshape(S5_NBLK, S5_GPB, S5_LANES, S5_GPB, S5_GROUP)
    dcc = jnp.stack([c5[:, gl, :, gl, :] for gl in range(S5_GPB)])
    dcc = dcc.transpose(1, 0, 2, 3).reshape(S5_GROUPS, 2, S5_STATE, S5_GROUP).transpose(0, 1, 3, 2)
    return dbbar_re, dbbar_im, dcc[:, 0], -dcc[:, 1]


def _s5_step_rows(t):
    d, out = 1, []
    while d < t:
        out.append(d)
        d *= 2
    return out


def s5_core_fwd(tag, u, wb, wc, a1, a2, dskip):
    s = u.shape[0]
    t = TM_S5

    def body(u_ref, wb_ref, wc_ref, a1_ref, a2_ref, d_ref, y_ref, x_ref, carry, carry_sw, tail_sw):
        @pl.when(pl.program_id(1) == 0)
        def _():
            carry[...] = jnp.zeros_like(carry)
            carry_sw[...] = jnp.zeros_like(carry_sw)

        uv = u_ref[...]
        x = bdot(uv, wb_ref[...], 1, 0)
        row = _iota2((t, S5_BW), 0)
        for d in _s5_step_rows(t):
            sh = jnp.where(row >= d, pltpu.roll(x, d, 0), 0.0)
            x = x + a1_ref[d - 1:d, :] * sh + a2_ref[d - 1:d, :] * _swap_ri(sh)
        x = x + a1_ref[...] * carry[...] + a2_ref[...] * carry_sw[...]
        x_ref[...] = x
        tail_sw[...] = _swap_ri(x[t - SUBLANES:, :])
        carry[...] = x_ref[t - 1:t, :]
        carry_sw[...] = tail_sw[SUBLANES - 1:SUBLANES, :]
        y_ref[...] = bdot(x, wc_ref[...], 1, 0) + d_ref[...] * uv

    return pl.pallas_call(
        body, grid=(S5_NBLK, s // t),
        in_specs=[pl.BlockSpec((t, LANES), lambda j, i: (i, j)),
                  pl.BlockSpec((None, LANES, S5_BW), lambda j, i: (j, 0, 0)),
                  pl.BlockSpec((None, S5_BW, LANES), lambda j, i: (j, 0, 0)),
                  pl.BlockSpec((t, S5_BW), lambda j, i: (0, j)),
                  pl.BlockSpec((t, S5_BW), lambda j, i: (0, j)),
                  pl.BlockSpec((1, LANES), lambda j, i: (0, j))],
        out_specs=[pl.BlockSpec((t, LANES), lambda j, i: (i, j)), pl.BlockSpec((t, S5_BW), lambda j, i: (i, j))],
        out_shape=[jax.ShapeDtypeStruct((s, D_MODEL), f32), jax.ShapeDtypeStruct((s, S5_WIDTH), f32)],
        scratch_shapes=[pltpu.VMEM((1, S5_BW), f32), pltpu.VMEM((1, S5_BW), f32), pltpu.VMEM((SUBLANES, S5_BW), f32)],
        compiler_params=_cparams("parallel", "arbitrary"), name="s5_core_" + tag)(u, wb, wc, a1, a2, dskip)


def s5_core_bwd(tag, u, x, wb, wc, a1, a2, a1r, a2r, dskip, dy):
    s = u.shape[0]
    t = TM_S5
    nt = s // t
    hb = t // SUBLANES

    def body(u_ref, dy_ref, x_ref, xh_ref, wb_ref, wc_ref, a1_ref, a2_ref, a1r_ref, a2r_ref, d_ref,
             du_ref, dwb_ref, dwc_ref, dd_ref, q1_ref, q2_ref, carry, carry_sw, lam_scr, head_sw):
        i = pl.program_id(1)
        tt = nt - 1 - i

        @pl.when(i == 0)
        def _():
            carry[...] = jnp.zeros_like(carry)
            carry_sw[...] = jnp.zeros_like(carry_sw)

        uv, dyv, xv = u_ref[...], dy_ref[...], x_ref[...]
        lam = bdot(dyv, wc_ref[...], 1, 1)
        row = _iota2((t, S5_BW), 0)
        for d in _s5_step_rows(t):
            sh = jnp.where(row < t - d, pltpu.roll(lam, t - d, 0), 0.0)
            lam = lam + a1_ref[d - 1:d, :] * sh - a2_ref[d - 1:d, :] * _swap_ri(sh)
        lam = lam + a1r_ref[...] * carry[...] - a2r_ref[...] * carry_sw[...]
        lam_scr[...] = lam
        head_sw[...] = _swap_ri(lam[:SUBLANES, :])
        carry[...] = lam_scr[0:1, :]
        carry_sw[...] = head_sw[0:1, :]
        du_ref[...] = bdot(lam, wb_ref[...], 1, 1) + d_ref[...] * dyv
        x_last = jnp.where(tt > 0, xh_ref[SUBLANES - 1:SUBLANES, :], 0.0)
        x_prev = jnp.where(row == 0, x_last, pltpu.roll(xv, 1, 0))
        p1, p2 = lam * x_prev, lam * _swap_ri(x_prev)
        q1 = p1[:SUBLANES, :]
        q2 = p2[:SUBLANES, :]
        for k in range(1, hb):
            q1 = q1 + p1[k * SUBLANES:(k + 1) * SUBLANES, :]
            q2 = q2 + p2[k * SUBLANES:(k + 1) * SUBLANES, :]
        upd = [(dwb_ref, bdot(uv, lam, 0, 0)), (dwc_ref, bdot(xv, dyv, 0, 0)),
               (dd_ref, jnp.sum(dyv * uv, 0, keepdims=True)), (q1_ref, q1), (q2_ref, q2)]

        @pl.when(i == 0)
        def _():
            for ref, val in upd:
                ref[...] = val

        @pl.when(i != 0)
        def _():
            for ref, val in upd:
                ref[...] += val

    nb8 = s // SUBLANES
    rev = lambda w: pl.BlockSpec((t, w), lambda j, i: (nt - 1 - i, j))
    tab = pl.BlockSpec((t, S5_BW), lambda j, i: (0, j))
    return pl.pallas_call(
        body, grid=(S5_NBLK, nt),
        in_specs=[rev(LANES), rev(LANES), rev(S5_BW),
                  pl.BlockSpec((SUBLANES, S5_BW), lambda j, i: (jnp.maximum((nt - 1 - i) * hb - 1, 0), j)),
                  pl.BlockSpec((None, LANES, S5_BW), lambda j, i: (j, 0, 0)),
                  pl.BlockSpec((None, S5_BW, LANES), lambda j, i: (j, 0, 0)),
                  tab, tab, tab, tab, pl.BlockSpec((1, LANES), lambda j, i: (0, j))],
        out_specs=[rev(LANES),
                   pl.BlockSpec((None, LANES, S5_BW), lambda j, i: (j, 0, 0)),
                   pl.BlockSpec((None, S5_BW, LANES), lambda j, i: (j, 0, 0)),
                   pl.BlockSpec((1, LANES), lambda j, i: (0, j)),
                   pl.BlockSpec((SUBLANES, S5_BW), lambda j, i: (0, j)),
                   pl.BlockSpec((SUBLANES, S5_BW), lambda j, i: (0, j))],
        out_shape=[jax.ShapeDtypeStruct((s, D_MODEL), f32),
                   jax.ShapeDtypeStruct((S5_NBLK, LANES, S5_BW), f32),
                   jax.ShapeDtypeStruct((S5_NBLK, S5_BW, LANES), f32),
                   jax.ShapeDtypeStruct((1, D_MODEL), f32),
                   jax.ShapeDtypeStruct((SUBLANES, S5_WIDTH), f32),
                   jax.ShapeDtypeStruct((SUBLANES, S5_WIDTH), f32)],
        scratch_shapes=[pltpu.VMEM((1, S5_BW), f32), pltpu.VMEM((1, S5_BW), f32),
                        pltpu.VMEM((t, S5_BW), f32), pltpu.VMEM((SUBLANES, S5_BW), f32)],
        compiler_params=_cparams("parallel", "arbitrary"),
        name="s5_core_bwd_" + tag)(u, dy, x, x, wb, wc, a1, a2, a1r, a2r, dskip)


def _gelu_tile(y):
    return (jax.nn.gelu(y),)


def s5_mixer_fwd(tag, u, prm, w_og):
    a_re, a_im, log_dt, b_re, b_im, c_re, c_im, dskip = prm
    disc_in = [a_re, a_im, log_dt.reshape(S5_GROUPS, 1), b_re.reshape(S5_GROUPS, -1), b_im.reshape(S5_GROUPS, -1)]
    abar_re, abar_im, bbar_re, bbar_im = rowmap("s5_disc_" + tag, _s5_disc_tile, disc_in, [_s5_expand()],
                                                [S5_STATE, S5_STATE, S5_STATE * S5_GROUP, S5_STATE * S5_GROUP], S5_GROUPS)
    del abar_re, abar_im
    a1, a2, a1r, a2r = s5_tables(a_re, a_im, log_dt)
    wb, wc = s5_pack_weights(bbar_re, bbar_im, c_re, c_im)
    wb, wc = wb.astype(bf16), wc.astype(bf16)
    y, x = s5_core_fwd(tag, u, wb, wc, a1, a2, dskip.reshape(1, D_MODEL))
    hid = rowmap("s5_gelu_" + tag, _gelu_tile, [y], [], [D_MODEL], TM_ROW)[0]
    og = mm_nn("s5_og_" + tag, hid, w_og)
    mix = rowmap("s5_glu_" + tag, _glu_tile, [og], [], [D_MODEL], TM_ROW)[0]
    return mix, (disc_in, a1, a2, a1r, a2r, wb, wc, x, y, hid, og)


def s5_mixer_bwd(tag, u, prm, w_og, res, dmix):
    a_re, a_im, log_dt, b_re, b_im, c_re, c_im, dskip = prm
    disc_in, a1, a2, a1r, a2r, wb, wc, x, y, hid, og = res
    (dog,), _ = rowmap_bwd("s5_glu_bwd_" + tag, _glu_tile, [og], [], [dmix], TM_ROW)
    dw_og = mm_tn("s5_og_dw_" + tag, hid, dog)
    dhid = mm_nt("s5_og_dx_" + tag, dog, w_og)
    (dy,), _ = rowmap_bwd("s5_gelu_bwd_" + tag, _gelu_tile, [y], [], [dhid], TM_ROW)
    du, dwb, dwc, ddskip, q1, q2 = s5_core_bwd(tag, u, x, wb, wc, a1, a2, a1r, a2r, dskip.reshape(1, D_MODEL), dy)
    dbbar_re, dbbar_im, dc_re, dc_im = s5_unpack_weight_grads(dwb, dwc)
    q1 = q1.sum(0).reshape(S5_GROUPS, 2, S5_STATE)
    q2 = q2.sum(0).reshape(S5_GROUPS, 2, S5_STATE)
    dabar_re, dabar_im = q1[:, 0] + q1[:, 1], q2[:, 1] - q2[:, 0]
    grads, _ = rowmap_bwd("s5_disc_bwd_" + tag, _s5_disc_tile, disc_in, [_s5_expand()],
                          [dabar_re, dabar_im, dbbar_re, dbbar_im], S5_GROUPS, par_mask=[False])
    da_re, da_im, dlog_dt, db_re, db_im = grads
    return du, (da_re, da_im, dlog_dt.reshape(S5_GROUPS), db_re.reshape(b_re.shape), db_im.reshape(b_im.shape),
                dc_re, dc_im, ddskip.reshape(D_MODEL)), dw_og


HYB_IN = 3592
_IN_B0, _IN_SW0 = 2048, 2056


IN_SHARD = HYB_IN // 4
SHARD_ORDER_GRADS = ("hyb_w_in", "ffn_wg", "ffn_wu", "ffn_wd")


def _w_in_pieces():
    runs = [(0, _IN_B0, 0), (_IN_B0, _IN_SW0, COL_BA), (_IN_SW0, HYB_IN, _IN_B0)]
    out = []
    for sh in range(4):
        lo, hi = sh * IN_SHARD, (sh + 1) * IN_SHARD
        for r_lo, r_hi, c_lo in runs:
            a, b = max(lo, r_lo), min(hi, r_hi)
            if a < b:
                out.append((sh, a - lo, b - lo, c_lo + a - r_lo))
    return out


def w_in_to_canonical(tag, layer, w4):
    tr = 128

    def body(w_ref, o_ref):
        o_ref[:, COL_BA:] = jnp.zeros((tr, BA_PAD), o_ref.dtype)
        for sh, a, b, c in _w_in_pieces():
            o_ref[:, c:c + b - a] = w_ref[sh, :, a:b]

    return pl.pallas_call(
        body, grid=(D_MODEL // tr,),
        in_specs=[pl.BlockSpec((4, None, tr, IN_SHARD), lambda i: (0, layer, i, 0))],
        out_specs=pl.BlockSpec((tr, PROJ_COLS), lambda i: (i, 0)),
        out_shape=jax.ShapeDtypeStruct((D_MODEL, PROJ_COLS), w4.dtype),
        compiler_params=_cparams("parallel"), name="w_in_canon_" + tag)(w4)


def w_in_grad_to_shards(tag, g):
    tr = 128

    def body(g_ref, o_ref):
        for sh, a, b, c in _w_in_pieces():
            o_ref[sh, :, a:b] = g_ref[:, c:c + b - a]

    return pl.pallas_call(
        body, grid=(D_MODEL // tr,),
        in_specs=[pl.BlockSpec((tr, PROJ_COLS), lambda i: (i, 0))],
        out_specs=pl.BlockSpec((4, tr, IN_SHARD), lambda i: (0, i, 0)),
        out_shape=jax.ShapeDtypeStruct((4, D_MODEL, IN_SHARD), f32),
        compiler_params=_cparams("parallel"), name="w_in_grad_shards_" + tag)(g)


def _add2(name, a, b):
    return rowmap(name, lambda p, q: (p + q,), [a, b], [], [a.shape[1]], _pick(a.shape[0], (256, 128, 64, 32, 16, 8)))[0]


def local_step(x, mem, positions, target, p):
    s = x.shape[0]
    cos, sin = rope_tables(positions, s)
    row = lambda v: v.reshape(1, -1).astype(f32)
    wg4, wu4, wd4 = (p[n].astype(bf16) for n in ("ffn_wg", "ffn_wu", "ffn_wd"))
    h = x
    tape = []
    for l in range(DEPTH):
        i, tag = l // 2, str(l)
        t = {"h0": h}
        if l % 2 == 0:
            t["w_in"] = w_in_to_canonical(tag, i, p["hyb_w_in"].astype(bf16))
            t["w_out"] = p["hyb_w_out"][i].astype(bf16)
            t["dn_prm"] = (p["dn_conv_w"][i].astype(f32), row(jnp.repeat(p["dn_a_log"][i], DN_HEAD_DIM)),
                           row(jnp.repeat(p["dn_dt_bias"][i], DN_HEAD_DIM)), row(jnp.tile(p["dn_norm_g"][i], DN_HEADS)))
            t["proj"] = mm_nn("hyb_in_" + tag, h, t["w_in"])
            a_out, t["dn"] = dn_mixer_fwd(tag, t["proj"], *t["dn_prm"])
            b_out, t["swa"] = swa_mixer_fwd(tag, t["proj"], cos, sin)
            t["mixed"] = jnp.concatenate([a_out, b_out], 1)
            mix = mm_nn("hyb_out_" + tag, t["mixed"], t["w_out"])
        else:
            t["s5_prm"] = tuple(p[n][i].astype(f32) for n in
                                ("s5_a_re", "s5_a_im", "s5_log_dt", "s5_b_re", "s5_b_im", "s5_c_re", "s5_c_im", "s5_d"))
            t["w_og"] = jnp.concatenate([p["s5_glu_wo"][i], p["s5_glu_wg"][i]], 1).astype(bf16)
            mix, t["s5"] = s5_mixer_fwd(tag, h, t["s5_prm"], t["w_og"])
        t["mix"] = mix
        t["ln"] = [(row(p[g][l]), row(p[b][l])) for g, b in
                   (("ln_mix_g", "ln_mix_b"), ("ln_x_g", "ln_x_b"), ("ln_ffn_g", "ln_ffn_b"))]
        t["h1"] = postnorm_fwd("mix" + tag, h, mix, *t["ln"][0])
        t["wq"], t["wo"] = p["xq_w"][l].astype(bf16), p["xo_w"][l].astype(bf16)
        t["wkv"] = jnp.concatenate([p["xk_w"][l], p["xv_w"][l]], 1).astype(bf16)
        t["xo"], t["xres"] = xattn_fwd(tag, t["h1"], mem, t["wq"], t["wkv"], t["wo"])
        t["h2"] = postnorm_fwd("x" + tag, t["h1"], t["xo"], *t["ln"][1])
        t["fo"], t["fres"] = ffn_fwd(tag, l, t["h2"], wg4, wu4, wd4)
        h = postnorm_fwd("ffn" + tag, t["h2"], t["fo"], *t["ln"][2])
        tape.append(t)

    part, dh = loss_head(h, target)
    loss = jnp.sum(part)

    g = {n: [None] * v.shape[1 if n in SHARD_ORDER_GRADS else 0] for n, v in p.items()}
    for l in reversed(range(DEPTH)):
        i, tag, t = l // 2, str(l), tape[l]
        dh2a, dfo, dg, db = postnorm_bwd("ffn" + tag, t["h2"], t["fo"], *t["ln"][2], dh)
        g["ln_ffn_g"][l], g["ln_ffn_b"][l] = dg[0], db[0]
        dh2b, g["ffn_wg"][l], g["ffn_wu"][l], g["ffn_wd"][l] = ffn_bwd(tag, l, t["h2"], wg4, wu4, wd4, t["fres"], dfo)
        dh1a, dxo, dg, db = postnorm_bwd("x" + tag, t["h1"], t["xo"], *t["ln"][1], [dh2a, dh2b])
        g["ln_x_g"][l], g["ln_x_b"][l] = dg[0], db[0]
        dh1b, g["xq_w"][l], dwkv, g["xo_w"][l] = xattn_bwd(tag, t["h1"], mem, t["wq"], t["wkv"], t["wo"], t["xres"], dxo)
        g["xk_w"][l], g["xv_w"][l] = dwkv[:, :D_MODEL], dwkv[:, D_MODEL:]
        dh0a, dmix, dg, db = postnorm_bwd("mix" + tag, t["h0"], t["mix"], *t["ln"][0], [dh1a, dh1b])
        g["ln_mix_g"][l], g["ln_mix_b"][l] = dg[0], db[0]
        if l % 2 == 0:
            g["hyb_w_out"][i] = mm_tn("hyb_out_dw_" + tag, t["mixed"], dmix)
            dmixed = mm_nt("hyb_out_dx_" + tag, dmix, t["w_out"])
            dqkv, dz, dba, dcw, dalog, ddtb, dng = dn_mixer_bwd(tag, t["proj"], *t["dn_prm"], t["dn"], (dmixed, DN_KEY_DIM, 0))
            g["dn_conv_w"][i] = dcw
            g["dn_a_log"][i] = dalog.reshape(DN_HEADS, DN_HEAD_DIM).sum(1)
            g["dn_dt_bias"][i] = ddtb.reshape(DN_HEADS, DN_HEAD_DIM).sum(1)
            g["dn_norm_g"][i] = dng.reshape(DN_HEADS, DN_HEAD_DIM).sum(0)
            dq, dk, dv = swa_mixer_bwd(tag, cos, sin, t["swa"], (dmixed, SW_DIM, 1))
            dproj = jnp.concatenate([dqkv, dz, dq, dk, dv, dba], 1)
            g["hyb_w_in"][i] = w_in_grad_to_shards(tag, mm_tn("hyb_in_dw_" + tag, t["h0"], dproj))
            dh0b = mm_nt("hyb_in_dx_" + tag, dproj, t["w_in"])
        else:
            dh0b, dprm, dw_og = s5_mixer_bwd(tag, t["h0"], t["s5_prm"], t["w_og"], t["s5"], dmix)
            for n, v in zip(("s5_a_re", "s5_a_im", "s5_log_dt", "s5_b_re", "s5_b_im", "s5_c_re", "s5_c_im", "s5_d"), dprm):
                g[n][i] = v
            g["s5_glu_wo"][i], g["s5_glu_wg"][i] = dw_og[:, :D_MODEL], dw_og[:, D_MODEL:]
        dh = [dh0a, dh0b]
    grad_x = _add2("grad_x", dh[0], dh[1])
    grads = {n: jnp.stack(v, 1 if n in SHARD_ORDER_GRADS else 0) for n, v in g.items()}
    return loss, grad_x, grads


WEIGHT_NAMES = ("hyb_w_in", "dn_conv_w", "dn_a_log", "dn_dt_bias", "dn_norm_g", "hyb_w_out", "s5_a_re", "s5_a_im",
                "s5_log_dt", "s5_b_re", "s5_b_im", "s5_c_re", "s5_c_im", "s5_d", "s5_glu_wo", "s5_glu_wg",
                "ln_mix_g", "ln_mix_b", "xq_w", "xk_w", "xv_w", "xo_w", "ln_x_g", "ln_x_b",
                "ffn_wg", "ffn_wu", "ffn_wd", "ln_ffn_g", "ln_ffn_b")
SHARD_AXIS = {"hyb_w_in": 2, "dn_conv_w": 2, "hyb_w_out": 1, "s5_d": 1, "s5_glu_wo": 1, "s5_glu_wg": 1,
              "xq_w": 1, "xk_w": 1, "xv_w": 1, "xo_w": 1, "ffn_wg": 2, "ffn_wu": 2, "ffn_wd": 1}
GATHER_F32 = ("dn_conv_w", "s5_d")
N_CHIPS = 4
PACK_COLS = 1024
_ANY = pl.BlockSpec(memory_space=pl.ANY)


def _pos():
    return lax.axis_index("x"), lax.axis_index("y"), lax.axis_index("c")


def _chip_peers(mx, my):
    return [(1 - mx, my), (mx, 1 - my), (1 - mx, 1 - my)]


def _rcopy(src, dst, ssem, rsem, dev):
    return pltpu.make_async_remote_copy(src_ref=src, dst_ref=dst, send_sem=ssem, recv_sem=rsem,
                                        device_id=dev, device_id_type=pl.DeviceIdType.MESH)


def comm_allgather4(name, x):
    def body(x_ref, o_ref, ssem, rsem, lsem):
        mx, my, mc = _pos()
        me = 2 * mx + my
        peers = _chip_peers(mx, my)
        loc = pltpu.make_async_copy(x_ref, o_ref.at[me], lsem)
        loc.start()
        sends = [_rcopy(x_ref, o_ref.at[me], ssem.at[k], rsem.at[k], (px, py, mc)) for k, (px, py) in enumerate(peers)]
        for cp in sends:
            cp.start()
        for k, (px, py) in enumerate(peers):
            _rcopy(x_ref, o_ref.at[2 * px + py], ssem.at[k], rsem.at[k], (px, py, mc)).wait_recv()
        for cp in sends:
            cp.wait_send()
        loc.wait()

    return pl.pallas_call(
        body, out_shape=jax.ShapeDtypeStruct((N_CHIPS,) + x.shape, x.dtype), in_specs=[_ANY], out_specs=_ANY,
        scratch_shapes=[pltpu.SemaphoreType.DMA((3,)), pltpu.SemaphoreType.DMA((3,)), pltpu.SemaphoreType.DMA],
        name=name)(x)


def _multi_call(name, body, ins, out_shapes, sems):
    return pl.pallas_call(
        body, out_shape=out_shapes, in_specs=[_ANY] * len(ins), out_specs=[_ANY] * len(out_shapes),
        scratch_shapes=sems, name=name)(*ins)


def comm_gather_weights(name, shards):
    n = len(shards)

    def body(*refs):
        xs, os_ = refs[:n], refs[n:2 * n]
        ssem, rsem, fssem, frsem, lsem = refs[2 * n:]
        mx, my, mc = _pos()
        me = 2 * mx + my
        peers = _chip_peers(mx, my)
        sib = (mx, my, 1 - mc)
        half = [x.shape[0] // 2 for x in xs]
        mine = [pl.ds(mc * h, h) for h in half]
        other = [pl.ds((1 - mc) * h, h) for h in half]
        locs = [pltpu.make_async_copy(xs[w], os_[w].at[me], lsem.at[w]) for w in range(n)]
        for cp in locs:
            cp.start()
        sends = [_rcopy(xs[w].at[mine[w]], os_[w].at[me, mine[w]], ssem.at[w, k], rsem.at[w, k], (px, py, mc))
                 for w in range(n) for k, (px, py) in enumerate(peers)]
        for cp in sends:
            cp.start()
        fwds = []
        for w in range(n):
            for k, (px, py) in enumerate(peers):
                landed = os_[w].at[2 * px + py, mine[w]]
                _rcopy(landed, landed, ssem.at[w, k], rsem.at[w, k], (px, py, mc)).wait_recv()
                fw = _rcopy(landed, landed, fssem.at[w, k], frsem.at[w, k], sib)
                fw.start()
                fwds.append(fw)
        for w in range(n):
            for k, (px, py) in enumerate(peers):
                theirs = os_[w].at[2 * px + py, other[w]]
                _rcopy(theirs, theirs, fssem.at[w, k], frsem.at[w, k], sib).wait_recv()
        for cp in sends + fwds:
            cp.wait_send()
        for cp in locs:
            cp.wait()

    dma = pltpu.SemaphoreType.DMA
    return _multi_call(name, body, shards, [jax.ShapeDtypeStruct((N_CHIPS,) + x.shape, x.dtype) for x in shards],
                       [dma((n, 3)), dma((n, 3)), dma((n, 3)), dma((n, 3)), dma((n,))])


def comm_sibling_halves(name, gs):
    n = len(gs)

    def body(*refs):
        xs, os_ = refs[:n], refs[n:2 * n]
        ssem, rsem = refs[2 * n:]
        mx, my, mc = _pos()
        sib = (mx, my, 1 - mc)
        sends = []
        for w in range(n):
            h = xs[w].shape[1] // 2
            for j in range(N_CHIPS):
                sends.append(_rcopy(xs[w].at[j, pl.ds((1 - mc) * h, h)], os_[w].at[j], ssem.at[w, j], rsem.at[w, j], sib))
        for cp in sends:
            cp.start()
        for w in range(n):
            for j in range(N_CHIPS):
                _rcopy(os_[w].at[j], os_[w].at[j], ssem.at[w, j], rsem.at[w, j], sib).wait_recv()
        for cp in sends:
            cp.wait_send()

    dma = pltpu.SemaphoreType.DMA
    return _multi_call(name, body, gs,
                       [jax.ShapeDtypeStruct((N_CHIPS, g.shape[1] // 2) + g.shape[2:], g.dtype) for g in gs],
                       [dma((n, N_CHIPS)), dma((n, N_CHIPS))])


def comm_alltoall4(name, xs):
    n = len(xs)

    def body(*refs):
        xr, os_ = refs[:n], refs[n:2 * n]
        ssem, rsem, lsem = refs[2 * n:]
        mx, my, mc = _pos()
        me = 2 * mx + my
        peers = _chip_peers(mx, my)
        locs = [pltpu.make_async_copy(xr[w].at[me], os_[w].at[me], lsem.at[w]) for w in range(n)]
        for cp in locs:
            cp.start()
        sends = [_rcopy(xr[w].at[2 * px + py], os_[w].at[me], ssem.at[w, k], rsem.at[w, k], (px, py, mc))
                 for w in range(n) for k, (px, py) in enumerate(peers)]
        for cp in sends:
            cp.start()
        for w in range(n):
            for k, (px, py) in enumerate(peers):
                dst = os_[w].at[2 * px + py]
                _rcopy(dst, dst, ssem.at[w, k], rsem.at[w, k], (px, py, mc)).wait_recv()
        for cp in sends:
            cp.wait_send()
        for cp in locs:
            cp.wait()

    dma = pltpu.SemaphoreType.DMA
    return _multi_call(name, body, xs, [jax.ShapeDtypeStruct(x.shape, x.dtype) for x in xs],
                       [dma((n, 3)), dma((n, 3)), dma((n,))])


def comm_sibling_join(name, bs):
    n = len(bs)

    def body(*refs):
        xs, os_ = refs[:n], refs[n:2 * n]
        ssem, rsem, lsem = refs[2 * n:]
        mx, my, mc = _pos()
        sib = (mx, my, 1 - mc)
        locs, sends = [], []
        for w in range(n):
            h = xs[w].shape[0]
            locs.append(pltpu.make_async_copy(xs[w], os_[w].at[pl.ds(mc * h, h)], lsem.at[w]))
            sends.append(_rcopy(xs[w], os_[w].at[pl.ds(mc * h, h)], ssem.at[w], rsem.at[w], sib))
        for cp in locs + sends:
            cp.start()
        for w in range(n):
            h = xs[w].shape[0]
            dst = os_[w].at[pl.ds((1 - mc) * h, h)]
            _rcopy(dst, dst, ssem.at[w], rsem.at[w], sib).wait_recv()
        for cp in sends:
            cp.wait_send()
        for cp in locs:
            cp.wait()

    dma = pltpu.SemaphoreType.DMA
    return _multi_call(name, body, bs, [jax.ShapeDtypeStruct((2 * b.shape[0],) + b.shape[1:], b.dtype) for b in bs],
                       [dma((n,)), dma((n,)), dma((n,))])


def comm_sibling_swap(name, x):
    def body(x_ref, o_ref, ssem, rsem):
        mx, my, mc = _pos()
        cp = _rcopy(x_ref, o_ref, ssem, rsem, (mx, my, 1 - mc))
        cp.start()
        cp.wait_recv()
        cp.wait_send()

    return pl.pallas_call(
        body, out_shape=jax.ShapeDtypeStruct(x.shape, x.dtype), in_specs=[_ANY], out_specs=_ANY,
        scratch_shapes=[pltpu.SemaphoreType.DMA, pltpu.SemaphoreType.DMA], name=name)(x)


def _row_tile(r):
    return _pick(r, (256, 128, 64, 32, 16, 8))


def add_own_half(name, g, recv, out_dtype):
    r, c = g.shape[2:]
    tr = _row_tile(r)
    mc = lax.axis_index("c").astype(jnp.int32).reshape(1)

    def body(c_ref, g_ref, r_ref, o_ref):
        o_ref[...] = (g_ref[...] + r_ref[...]).astype(o_ref.dtype)

    grid_spec = pltpu.PrefetchScalarGridSpec(
        num_scalar_prefetch=1, grid=(N_CHIPS, r // tr),
        in_specs=[pl.BlockSpec((None, None, tr, c), lambda j, i, cr: (j, cr[0], i, 0)),
                  pl.BlockSpec((None, tr, c), lambda j, i, cr: (j, i, 0))],
        out_specs=pl.BlockSpec((None, tr, c), lambda j, i, cr: (j, i, 0)))
    return pl.pallas_call(body, grid_spec=grid_spec, out_shape=jax.ShapeDtypeStruct(recv.shape, out_dtype),
                          compiler_params=_cparams("parallel", "parallel"), name=name)(mc, g, recv)


def sum_slots(name, x):
    r, c = x.shape[1:]
    tr = _row_tile(r)

    def body(x_ref, o_ref):
        o_ref[...] = (x_ref[0].astype(f32) + x_ref[1].astype(f32)) + (x_ref[2].astype(f32) + x_ref[3].astype(f32))

    return pl.pallas_call(
        body, grid=(r // tr,), in_specs=[pl.BlockSpec((N_CHIPS, tr, c), lambda i: (0, i, 0))],
        out_specs=pl.BlockSpec((tr, c), lambda i: (i, 0)), out_shape=jax.ShapeDtypeStruct((r, c), f32),
        compiler_params=_cparams("parallel"), name=name)(x)


def adamw(name, w, g, m, v):
    r, c = w.shape
    tr = _row_tile(r)

    def body(w_ref, g_ref, m_ref, v_ref, d_ref, nm_ref, nv_ref):
        gv = g_ref[...]
        nm = ADAM_B1 * m_ref[...] + (1.0 - ADAM_B1) * gv
        nv = ADAM_B2 * v_ref[...] + (1.0 - ADAM_B2) * (gv * gv)
        m_hat = nm / (1.0 - ADAM_B1 ** ADAM_STEP)
        v_hat = nv / (1.0 - ADAM_B2 ** ADAM_STEP)
        d_ref[...] = -ADAM_LR * (m_hat / (jnp.sqrt(v_hat) + ADAM_EPS) + ADAM_WD * w_ref[...])
        nm_ref[...] = nm
        nv_ref[...] = nv

    blk = pl.BlockSpec((tr, c), lambda i: (i, 0))
    return pl.pallas_call(
        body, grid=(r // tr,), in_specs=[blk] * 4, out_specs=[blk] * 3,
        out_shape=[jax.ShapeDtypeStruct((r, c), f32)] * 3,
        compiler_params=_cparams("parallel"), name=name)(w, g, m, v)


def _pack(arrs, dtype, row_multiple):
    flat = jnp.concatenate([a.astype(dtype).reshape(-1) for a in arrs])
    unit = PACK_COLS * row_multiple
    total = -(-flat.shape[0] // unit) * unit
    return jnp.pad(flat, (0, total - flat.shape[0])).reshape(-1, PACK_COLS)


def _unpack(packed, shapes):
    flat, out, off = packed.reshape(-1), [], 0
    for shp in shapes:
        n = math.prod(shp)
        out.append(flat[off:off + n].reshape(shp))
        off += n
    return out


def _gathered_to_full(g, axis):
    t = jnp.moveaxis(g, 0, axis)
    return t.reshape(t.shape[:axis] + (t.shape[axis] * t.shape[axis + 1],) + t.shape[axis + 2:])


def _full_to_shard_major(full, axis):
    shp = full.shape
    t = full.reshape(shp[:axis] + (N_CHIPS, shp[axis] // N_CHIPS) + shp[axis + 1:])
    return jnp.moveaxis(t, axis, 0)


GRAD_ROW_MULTIPLE = 256


def kernel(x, mem, positions, hyb_w_in, dn_conv_w, dn_a_log, dn_dt_bias, dn_norm_g, hyb_w_out, s5_a_re, s5_a_im, s5_log_dt, s5_b_re, s5_b_im, s5_c_re, s5_c_im, s5_d, s5_glu_wo, s5_glu_wg, ln_mix_g, ln_mix_b, xq_w, xk_w, xv_w, xo_w, ln_x_g, ln_x_b, ffn_wg, ffn_wu, ffn_wd, ln_ffn_g, ln_ffn_b, loss_target, m_hyb_w_in, m_dn_conv_w, m_dn_a_log, m_dn_dt_bias, m_dn_norm_g, m_hyb_w_out, m_s5_a_re, m_s5_a_im, m_s5_log_dt, m_s5_b_re, m_s5_b_im, m_s5_c_re, m_s5_c_im, m_s5_d, m_s5_glu_wo, m_s5_glu_wg, m_ln_mix_g, m_ln_mix_b, m_xq_w, m_xk_w, m_xv_w, m_xo_w, m_ln_x_g, m_ln_x_b, m_ffn_wg, m_ffn_wu, m_ffn_wd, m_ln_ffn_g, m_ln_ffn_b, v_hyb_w_in, v_dn_conv_w, v_dn_a_log, v_dn_dt_bias, v_dn_norm_g, v_hyb_w_out, v_s5_a_re, v_s5_a_im, v_s5_log_dt, v_s5_b_re, v_s5_b_im, v_s5_c_re, v_s5_c_im, v_s5_d, v_s5_glu_wo, v_s5_glu_wg, v_ln_mix_g, v_ln_mix_b, v_xq_w, v_xk_w, v_xv_w, v_xo_w, v_ln_x_g, v_ln_x_b, v_ffn_wg, v_ffn_wu, v_ffn_wd, v_ln_ffn_g, v_ln_ffn_b):
    a = dict(locals())
    big = [n for n in WEIGHT_NAMES if n in SHARD_AXIS and n not in GATHER_F32]
    small = [n for n in WEIGHT_NAMES if n not in big]
    chip = 2 * lax.axis_index("x") + lax.axis_index("y")

    gathered = comm_gather_weights("gather_w", [a[n].astype(bf16) for n in big])
    tiny4 = _unpack_slots(comm_allgather4("gather_w_tiny", _pack([a[n] for n in GATHER_F32], f32, 8)),
                          [a[n].shape for n in GATHER_F32])
    p = {n: a[n] for n in small if n not in GATHER_F32}
    for n, g4 in zip(GATHER_F32, tiny4):
        p[n] = _gathered_to_full(g4, SHARD_AXIS[n])
    for n, g4 in zip(big, gathered):
        p[n] = g4 if n in SHARD_ORDER_GRADS else _gathered_to_full(g4, SHARD_AXIS[n])

    loss, grad_x, grads = local_step(x[0], mem[0], positions, loss_target[0], p)
    loss = lax.psum(loss, ("x", "y", "c"))

    g4s = [grads[n] if n in SHARD_ORDER_GRADS else _full_to_shard_major(grads[n], SHARD_AXIS[n]) for n in big]
    recv = comm_sibling_halves("rs_sibling_halves", g4s)
    pairs = []
    for n, g4, r4 in zip(big, g4s, recv):
        lh, cols = g4.shape[1] // 2, g4.shape[-1]
        v4 = g4.reshape(N_CHIPS, 2, -1, cols)
        pairs.append(add_own_half("rs_add_" + n, v4, r4.reshape(N_CHIPS, -1, cols), bf16).reshape((N_CHIPS, lh) + g4.shape[2:]))
    arrived = comm_alltoall4("rs_alltoall", pairs)
    mine = [sum_slots("rs_sum_" + n, t.reshape(N_CHIPS, -1, t.shape[-1])).reshape(t.shape[1:]) for n, t in zip(big, arrived)]
    g_big = dict(zip(big, comm_sibling_join("rs_sibling_join", mine)))

    rpack = _pack([grads[n] for n in small], f32, 8)
    rpair = _add2("ar_add_sibling", rpack, comm_sibling_swap("ar_sibling_swap", rpack))
    g_small = _unpack(sum_slots("ar_sum_chips", comm_allgather4("ar_allgather", rpair)), [grads[n].shape for n in small])
    g_small = {n: (lax.dynamic_index_in_dim(_full_to_shard_major(g, SHARD_AXIS[n]), chip, 0, keepdims=False)
                   if n in SHARD_AXIS else g) for n, g in zip(small, g_small)}

    outs = {}
    for n in big:
        view = lambda t: t.reshape(-1, t.shape[-1])
        d, nm, nv = adamw("adamw_" + n, view(a[n]), view(g_big[n]), view(a["m_" + n]), view(a["v_" + n]))
        outs[n] = (g_big[n],) + tuple(t.reshape(a[n].shape) for t in (d, nm, nv))
    shapes = [a[n].shape for n in small]
    packs = [_pack([a[pre + n] for n in small], f32, 8) for pre in ("", "m_", "v_")]
    upd = adamw("adamw_small", packs[0], _pack([g_small[n] for n in small], f32, 8), packs[1], packs[2])
    for k, n in enumerate(small):
        outs[n] = (g_small[n],) + tuple(_unpack(buf, shapes)[k] for buf in upd)
    res = [loss, grad_x[None]]
    for kind in range(4):
        res += [outs[n][kind] for n in WEIGHT_NAMES]
    return tuple(res)


def _unpack_slots(gathered, shapes):
    flat, out, off = gathered.reshape(N_CHIPS, -1), [], 0
    for shp in shapes:
        n = math.prod(shp)
        out.append(flat[:, off:off + n].reshape((N_CHIPS,) + tuple(shp)))
        off += n
    return out
```

```python
import functools
import math

import jax
import jax.numpy as jnp
from jax import lax
from jax.experimental import pallas as pl
from jax.experimental.pallas import tpu as pltpu

f32 = jnp.float32
bf16 = jnp.bfloat16

D_MODEL = 1024
DEPTH = 4
DN_HEADS = 4
DN_HEAD_DIM = 128
DN_KEY_DIM = 512
DN_QKV_DIM = 1536
DN_CONV = 4
SW_HEADS = 8
SW_HEAD_DIM = 64
SW_DIM = 512
SW_DILATIONS = (1, 4, 16)
SW_BLOCK = 128
ROPE_THETA = 10000.0
S5_GROUP = 16
S5_GROUPS = 64
S5_STATE = 64
X_HEADS = 4
X_HEAD_DIM = 256
FFN_HIDDEN = 2816
ALPHA = (2 * DEPTH) ** 0.25
LN_EPS = 1e-5
RMS_EPS = 1e-6
ADAM_LR, ADAM_B1, ADAM_B2, ADAM_EPS, ADAM_WD, ADAM_STEP = 0.001, 0.9, 0.999, 1e-08, 0.01, 10

BA_PAD = 256
PROJ_COLS = DN_QKV_DIM + DN_KEY_DIM + 3 * SW_DIM + BA_PAD
COL_Z = DN_QKV_DIM
COL_SWQ = COL_Z + DN_KEY_DIM
COL_SWK = COL_SWQ + SW_DIM
COL_SWV = COL_SWK + SW_DIM
COL_BA = COL_SWV + SW_DIM

LANES = 128
SUBLANES = 8
VMEM_LIMIT = 56 * 1024 * 1024
DN_CHUNK = 128
DN_HEADS_PER_STEP = 4


def _cparams(*sem):
    return pltpu.CompilerParams(dimension_semantics=tuple(sem), vmem_limit_bytes=VMEM_LIMIT)


def _dg(x, y, cx, cy):
    return lax.dot_general(x, y, (((cx,), (cy,)), ((), ())), preferred_element_type=f32)


@functools.partial(jax.custom_vjp, nondiff_argnums=(2, 3))
def bdot(a, b, ca, cb):
    return _dg(a.astype(bf16), b.astype(bf16), ca, cb)


def _bdot_fwd(a, b, ca, cb):
    return bdot(a, b, ca, cb), (a, b)


def _bdot_bwd(ca, cb, res, g):
    a, b = res
    g16, a16, b16 = g.astype(bf16), a.astype(bf16), b.astype(bf16)
    da = _dg(g16, b16, 1, 1 - cb) if ca == 1 else _dg(b16, g16, 1 - cb, 1)
    db = _dg(a16, g16, 1 - ca, 0) if cb == 0 else _dg(g16, a16, 0, 1 - ca)
    return da.astype(a.dtype), db.astype(b.dtype)


bdot.defvjp(_bdot_fwd, _bdot_bwd)


def hdot(a, b):
    return jnp.dot(a, b, precision=lax.Precision.HIGHEST, preferred_element_type=f32)


def _iota2(shape, dim):
    return lax.broadcasted_iota(jnp.int32, shape, dim)


def _row_spec(r, tm):
    if isinstance(r, tuple):
        arr, width, blk = r
        return arr, pl.BlockSpec((tm, width), lambda i, _b=blk: (i, _b))
    return r, pl.BlockSpec((tm, r.shape[1]), lambda i: (i, 0))


def _par_spec(p):
    return pl.BlockSpec(p.shape, lambda i, _n=p.ndim: (0,) * _n)


def rowmap(name, fn, rows, params, out_cols, tm, out_dtypes=None):
    arrs, specs = zip(*[_row_spec(r, tm) for r in rows])
    s = arrs[0].shape[0]
    n_in = len(rows) + len(params)
    out_dtypes = out_dtypes or [f32] * len(out_cols)

    def body(*refs):
        outs = fn(*[r[...] for r in refs[:n_in]])
        for o_ref, o in zip(refs[n_in:], outs):
            o_ref[...] = o.astype(o_ref.dtype)

    return pl.pallas_call(
        body, grid=(s // tm,),
        in_specs=list(specs) + [_par_spec(p) for p in params],
        out_specs=[pl.BlockSpec((tm, c), lambda i: (i, 0)) for c in out_cols],
        out_shape=[jax.ShapeDtypeStruct((s, c), dt) for c, dt in zip(out_cols, out_dtypes)],
        compiler_params=_cparams("parallel"), name=name)(*arrs, *params)


def rowmap_bwd(name, fn, rows, params, cts, tm, row_mask=None, par_mask=None):
    arrs, specs = zip(*[_row_spec(r, tm) for r in rows])
    s = arrs[0].shape[0]
    ct_groups = [c if isinstance(c, list) else [c] for c in cts]
    ct_arrs, ct_specs = zip(*[_row_spec(a, tm) for grp in ct_groups for a in grp])
    cts = list(ct_arrs)
    nr, npar, nct = len(rows), len(params), len(cts)
    row_mask = row_mask or [True] * nr
    par_mask = par_mask or [True] * npar
    row_idx = [k for k in range(nr) if row_mask[k]]
    par_idx = [k for k in range(npar) if par_mask[k]]
    row_w = [specs[k].block_shape[1] for k in row_idx]

    def body(*refs):
        ins = [r[...] for r in refs[:nr + npar]]
        ct_refs = list(refs[nr + npar:nr + npar + nct])
        ctv = []
        for grp in ct_groups:
            acc = ct_refs.pop(0)[...]
            for _ in grp[1:]:
                acc = acc + ct_refs.pop(0)[...]
            ctv.append(acc)
        ctv = tuple(ctv)
        outs = refs[nr + npar + nct:]
        _, vjp = jax.vjp(fn, *ins)
        grads = vjp(ctv)
        for o_ref, k in zip(outs[:len(row_idx)], row_idx):
            o_ref[...] = grads[k].astype(o_ref.dtype)
        first = pl.program_id(0) == 0
        for o_ref, k in zip(outs[len(row_idx):], par_idx):
            g = grads[nr + k].astype(f32)

            @pl.when(first)
            def _(o_ref=o_ref, g=g):
                o_ref[...] = g

            @pl.when(jnp.logical_not(first))
            def _(o_ref=o_ref, g=g):
                o_ref[...] += g

    res = pl.pallas_call(
        body, grid=(s // tm,),
        in_specs=list(specs) + [_par_spec(p) for p in params]
        + list(ct_specs),
        out_specs=[pl.BlockSpec((tm, w), lambda i: (i, 0)) for w in row_w]
        + [_par_spec(params[k]) for k in par_idx],
        out_shape=[jax.ShapeDtypeStruct((s, w), f32) for w in row_w]
        + [jax.ShapeDtypeStruct(params[k].shape, f32) for k in par_idx],
        compiler_params=_cparams("arbitrary"), name=name)(*arrs, *params, *cts)
    return list(res[:len(row_idx)]), list(res[len(row_idx):])


def _pick(n, prefs):
    for t in prefs:
        if n % t == 0:
            return t
    return n


def mm_nn(name, a, b, out_dtype=f32):
    m, k = a.shape
    n = b.shape[1]
    tm, tn = _pick(m, (512, 256, 128)), _pick(n, (512, 256, 128))

    def body(a_ref, b_ref, o_ref):
        o_ref[...] = _dg(a_ref[...].astype(bf16), b_ref[...].astype(bf16), 1, 0).astype(o_ref.dtype)

    return pl.pallas_call(
        body, grid=(m // tm, n // tn),
        in_specs=[pl.BlockSpec((tm, k), lambda i, j: (i, 0)), pl.BlockSpec((k, tn), lambda i, j: (0, j))],
        out_specs=pl.BlockSpec((tm, tn), lambda i, j: (i, j)),
        out_shape=jax.ShapeDtypeStruct((m, n), out_dtype),
        compiler_params=_cparams("parallel", "parallel"), name=name)(a, b)


def mm_nt(name, a, b, out_dtype=f32):
    m, n = a.shape
    k = b.shape[0]
    tm, tk = _pick(m, (256, 128)), _pick(k, (512, 256, 128))

    def body(a_ref, b_ref, o_ref):
        o_ref[...] = _dg(a_ref[...].astype(bf16), b_ref[...].astype(bf16), 1, 1).astype(o_ref.dtype)

    return pl.pallas_call(
        body, grid=(m // tm, k // tk),
        in_specs=[pl.BlockSpec((tm, n), lambda i, j: (i, 0)), pl.BlockSpec((tk, n), lambda i, j: (j, 0))],
        out_specs=pl.BlockSpec((tm, tk), lambda i, j: (i, j)),
        out_shape=jax.ShapeDtypeStruct((m, k), out_dtype),
        compiler_params=_cparams("parallel", "parallel"), name=name)(a, b)


def mm_tn(name, a, b, out_dtype=f32):
    s, m = a.shape
    n = b.shape[1]
    tm, tn = _pick(m, (256, 128)), _pick(n, (256, 128))

    def body(a_ref, b_ref, o_ref):
        o_ref[...] = _dg(a_ref[...].astype(bf16), b_ref[...].astype(bf16), 0, 0).astype(o_ref.dtype)

    return pl.pallas_call(
        body, grid=(m // tm, n // tn),
        in_specs=[pl.BlockSpec((s, tm), lambda i, j: (0, i)), pl.BlockSpec((s, tn), lambda i, j: (0, j))],
        out_specs=pl.BlockSpec((tm, tn), lambda i, j: (i, j)),
        out_shape=jax.ShapeDtypeStruct((m, n), out_dtype),
        compiler_params=_cparams("parallel", "parallel"), name=name)(a, b)


def _postnorm_tile(h, sub, g, b):
    z = ALPHA * h + sub
    mu = jnp.mean(z, -1, keepdims=True)
    zc = z - mu
    var = jnp.mean(zc * zc, -1, keepdims=True)
    return (zc * lax.rsqrt(var + LN_EPS) * g + b,)


def _swiglu_tile(au):
    a, u = au[:, :FFN_HIDDEN], au[:, FFN_HIDDEN:]
    return (jax.nn.silu(a) * u,)


def _glu_tile(og):
    o, g = og[:, :D_MODEL], og[:, D_MODEL:]
    return (o * jax.nn.sigmoid(g),)


def _xattn_tile(q, kv):
    outs = []
    for h in range(X_HEADS):
        sl = slice(h * X_HEAD_DIM, (h + 1) * X_HEAD_DIM)
        s = bdot(q[:, sl], kv[:, sl], 1, 1) * (X_HEAD_DIM ** -0.5)
        m = lax.stop_gradient(jnp.max(s, -1, keepdims=True))
        p = jnp.exp(s - m)
        p = p / jnp.sum(p, -1, keepdims=True)
        outs.append(bdot(p, kv[:, D_MODEL + h * X_HEAD_DIM:D_MODEL + (h + 1) * X_HEAD_DIM], 1, 0))
    return (jnp.concatenate(outs, -1),)


TM_ROW = 256


def postnorm_fwd(tag, h, sub, g, b):
    return rowmap("postnorm_" + tag, _postnorm_tile, [h, sub], [g, b], [D_MODEL], TM_ROW)[0]


def postnorm_bwd(tag, h, sub, g, b, dy):
    (dh, dsub), (dg, db) = rowmap_bwd("postnorm_bwd_" + tag, _postnorm_tile, [h, sub], [g, b], [dy], TM_ROW)
    return dh, dsub, dg, db


def xattn_fwd(tag, h, mem, wq, wkv, wo):
    q = mm_nn("xq_" + tag, h, wq)
    kv = mm_nn("xkv_" + tag, mem, wkv)
    ao = rowmap("xattn_" + tag, _xattn_tile, [q], [kv], [D_MODEL], TM_ROW)[0]
    out = mm_nn("xo_" + tag, ao, wo)
    return out, (q, kv, ao)


def xattn_bwd(tag, h, mem, wq, wkv, wo, res, dout):
    q, kv, ao = res
    dwo = mm_tn("xo_dw_" + tag, ao, dout)
    dao = mm_nt("xo_dx_" + tag, dout, wo)
    (dq,), (dkv,) = rowmap_bwd("xattn_bwd_" + tag, _xattn_tile, [q], [kv], [dao], TM_ROW)
    dwq = mm_tn("xq_dw_" + tag, h, dq)
    dh = mm_nt("xq_dx_" + tag, dq, wq)
    dwkv = mm_tn("xkv_dw_" + tag, mem, dkv)
    return dh, dwq, dwkv, dwo


FFN_SHARD = FFN_HIDDEN // 4
TM_FFN = 512


def _silu_mul(a, u):
    return jax.nn.silu(a) * u


def ffn_fwd(tag, layer, h, wg, wu, wd):
    s = h.shape[0]
    tm, fs = TM_FFN, FFN_SHARD
    w_in = pl.BlockSpec((None, None, D_MODEL, fs), lambda i, k: (k, layer, 0, 0))
    act = pl.BlockSpec((None, tm, fs), lambda i, k: (k, i, 0))

    def up_body(h_ref, wg_ref, wu_ref, a_ref, u_ref, hid_ref):
        hv = h_ref[...].astype(bf16)
        a, u = _dg(hv, wg_ref[...], 1, 0), _dg(hv, wu_ref[...], 1, 0)
        a_ref[...], u_ref[...] = a, u
        hid_ref[...] = _silu_mul(a, u).astype(bf16)

    a4, u4, hid4 = pl.pallas_call(
        up_body, grid=(s // tm, 4),
        in_specs=[pl.BlockSpec((tm, D_MODEL), lambda i, k: (i, 0)), w_in, w_in],
        out_specs=[act, act, act],
        out_shape=[jax.ShapeDtypeStruct((4, s, fs), f32)] * 2 + [jax.ShapeDtypeStruct((4, s, fs), bf16)],
        compiler_params=_cparams("parallel", "parallel"), name="ffn_up_" + tag)(h, wg, wu)

    def down_body(hid_ref, wd_ref, o_ref):
        part = _dg(hid_ref[...], wd_ref[...], 1, 0)

        @pl.when(pl.program_id(1) == 0)
        def _():
            o_ref[...] = part

        @pl.when(pl.program_id(1) != 0)
        def _():
            o_ref[...] += part

    out = pl.pallas_call(
        down_body, grid=(s // tm, 4),
        in_specs=[act, pl.BlockSpec((None, None, fs, D_MODEL), lambda i, k: (k, layer, 0, 0))],
        out_specs=pl.BlockSpec((tm, D_MODEL), lambda i, k: (i, 0)),
        out_shape=jax.ShapeDtypeStruct((s, D_MODEL), f32),
        compiler_params=_cparams("parallel", "arbitrary"), name="ffn_down_" + tag)(hid4, wd)
    return out, (a4, u4, hid4)


def ffn_bwd(tag, layer, h, wg, wu, wd, res, dout):
    a4, u4, hid4 = res
    s = h.shape[0]
    tm, fs = TM_FFN, FFN_SHARD
    act = pl.BlockSpec((None, tm, fs), lambda i, k: (k, i, 0))
    w_in = pl.BlockSpec((None, None, D_MODEL, fs), lambda i, k: (k, layer, 0, 0))

    def dact_body(do_ref, wd_ref, a_ref, u_ref, da_ref, du_ref):
        dhid = _dg(do_ref[...].astype(bf16), wd_ref[...], 1, 1)
        _, vjp = jax.vjp(_silu_mul, a_ref[...], u_ref[...])
        da, du = vjp(dhid)
        da_ref[...], du_ref[...] = da.astype(bf16), du.astype(bf16)

    da4, du4 = pl.pallas_call(
        dact_body, grid=(s // tm, 4),
        in_specs=[pl.BlockSpec((tm, D_MODEL), lambda i, k: (i, 0)),
                  pl.BlockSpec((None, None, fs, D_MODEL), lambda i, k: (k, layer, 0, 0)), act, act],
        out_specs=[act, act], out_shape=[jax.ShapeDtypeStruct((4, s, fs), bf16)] * 2,
        compiler_params=_cparams("parallel", "parallel"), name="ffn_dact_" + tag)(dout, wd, a4, u4)

    def dx_body(da_ref, du_ref, wg_ref, wu_ref, o_ref):
        part = _dg(da_ref[...], wg_ref[...], 1, 1) + _dg(du_ref[...], wu_ref[...], 1, 1)

        @pl.when(pl.program_id(1) == 0)
        def _():
            o_ref[...] = part

        @pl.when(pl.program_id(1) != 0)
        def _():
            o_ref[...] += part

    dh = pl.pallas_call(
        dx_body, grid=(s // tm, 4), in_specs=[act, act, w_in, w_in],
        out_specs=pl.BlockSpec((tm, D_MODEL), lambda i, k: (i, 0)),
        out_shape=jax.ShapeDtypeStruct((s, D_MODEL), f32),
        compiler_params=_cparams("parallel", "arbitrary"), name="ffn_dx_" + tag)(da4, du4, wg, wu)

    tn = 256
    whole = pl.BlockSpec((None, s, fs), lambda k, j: (k, 0, 0))

    def dwin_body(h_ref, da_ref, du_ref, dwg_ref, dwu_ref):
        hv = h_ref[...].astype(bf16)
        dwg_ref[...] = _dg(hv, da_ref[...], 0, 0)
        dwu_ref[...] = _dg(hv, du_ref[...], 0, 0)

    dwg, dwu = pl.pallas_call(
        dwin_body, grid=(4, D_MODEL // tn),
        in_specs=[pl.BlockSpec((s, tn), lambda k, j: (0, j)), whole, whole],
        out_specs=[pl.BlockSpec((None, tn, fs), lambda k, j: (k, j, 0))] * 2,
        out_shape=[jax.ShapeDtypeStruct((4, D_MODEL, fs), f32)] * 2,
        compiler_params=_cparams("parallel", "parallel"), name="ffn_dwin_" + tag)(h, da4, du4)

    def dwd_body(hid_ref, do_ref, dwd_ref):
        dwd_ref[...] = _dg(hid_ref[...], do_ref[...].astype(bf16), 0, 0)

    dwd = pl.pallas_call(
        dwd_body, grid=(4, D_MODEL // tn),
        in_specs=[whole, pl.BlockSpec((s, tn), lambda k, j: (0, j))],
        out_specs=pl.BlockSpec((None, fs, tn), lambda k, j: (k, 0, j)),
        out_shape=jax.ShapeDtypeStruct((4, fs, D_MODEL), f32),
        compiler_params=_cparams("parallel", "parallel"), name="ffn_dwd_" + tag)(hid4, dout)
    return dh, dwg, dwu, dwd


def loss_head(y, target):
    s, d = y.shape
    tm = TM_ROW

    def body(y_ref, t_ref, part_ref, dy_ref):
        e = y_ref[...] - t_ref[...]
        dy_ref[...] = e * (1.0 / d)
        p = jnp.sum(e * e, 0, keepdims=True) * (0.5 / d)

        @pl.when(pl.program_id(0) == 0)
        def _():
            part_ref[...] = p

        @pl.when(pl.program_id(0) != 0)
        def _():
            part_ref[...] += p

    return pl.pallas_call(
        body, grid=(s // tm,),
        in_specs=[pl.BlockSpec((tm, d), lambda i: (i, 0))] * 2,
        out_specs=[pl.BlockSpec((1, d), lambda i: (0, 0)), pl.BlockSpec((tm, d), lambda i: (i, 0))],
        out_shape=[jax.ShapeDtypeStruct((1, d), f32), jax.ShapeDtypeStruct((s, d), f32)],
        compiler_params=_cparams("arbitrary"), name="loss_head")(y, target)


TM_CONV = 512


def _conv_rows(xx, w_ref, n_rows):
    a = w_ref[3:4, :] * xx
    for k in (1, 2, 3):
        a = a + w_ref[3 - k:4 - k, :] * pltpu.roll(xx, k, 0)
    return a


def _dn_act(a, is_qk):
    s = jax.nn.silu(a)
    n = s * lax.rsqrt(jnp.sum(s * s, -1, keepdims=True) + RMS_EPS)
    return jnp.where(is_qk, n, s)


def dn_conv_fwd(tag, proj, cw):
    s = proj.shape[0]
    tm, hb = TM_CONV, TM_CONV // SUBLANES

    def body(xh_ref, x_ref, w_ref, o_ref):
        j, t = pl.program_id(0), pl.program_id(1)
        halo = jnp.where(t > 0, xh_ref[...], 0.0)
        xx = jnp.concatenate([halo, x_ref[...]], 0)
        a = _conv_rows(xx, w_ref, tm + SUBLANES)
        o_ref[...] = _dn_act(a, j < 2 * DN_HEADS)[SUBLANES:, :]

    return pl.pallas_call(
        body, grid=(DN_QKV_DIM // LANES, s // tm),
        in_specs=[pl.BlockSpec((SUBLANES, LANES), lambda j, t: (jnp.maximum(t * hb - 1, 0), j)),
                  pl.BlockSpec((tm, LANES), lambda j, t: (t, j)),
                  pl.BlockSpec((DN_CONV, LANES), lambda j, t: (0, j))],
        out_specs=pl.BlockSpec((tm, LANES), lambda j, t: (t, j)),
        out_shape=jax.ShapeDtypeStruct((s, DN_QKV_DIM), f32),
        compiler_params=_cparams("parallel", "parallel"), name="dn_conv_" + tag)(proj, proj, cw)


def dn_conv_bwd(tag, proj, cw, dy):
    s = proj.shape[0]
    tm, hb = TM_CONV, TM_CONV // SUBLANES
    nt = s // tm
    n_ext = tm + 2 * SUBLANES

    def body(xb_ref, x_ref, xa_ref, dy_ref, dya_ref, w_ref, dx_ref, dw_ref):
        j, t = pl.program_id(0), pl.program_id(1)
        xx = jnp.concatenate([jnp.where(t > 0, xb_ref[...], 0.0), x_ref[...],
                              jnp.where(t < nt - 1, xa_ref[...], 0.0)], 0)
        dyy = jnp.concatenate([jnp.zeros((SUBLANES, LANES), f32), dy_ref[...],
                               jnp.where(t < nt - 1, dya_ref[...], 0.0)], 0)
        a = _conv_rows(xx, w_ref, n_ext)
        _, vjp = jax.vjp(lambda v: _dn_act(v, j < 2 * DN_HEADS), a)
        da, = vjp(dyy)
        dx = w_ref[3:4, :] * da
        for k in (1, 2, 3):
            dx = dx + w_ref[3 - k:4 - k, :] * pltpu.roll(da, n_ext - k, 0)
        dx_ref[...] = dx[SUBLANES:SUBLANES + tm, :]
        row = _iota2((n_ext, LANES), 0)
        da_in = jnp.where((row >= SUBLANES) & (row < SUBLANES + tm), da, 0.0)
        r8 = _iota2((SUBLANES, LANES), 0)
        dw = jnp.zeros((SUBLANES, LANES), f32)
        for k in range(DN_CONV):
            xs = xx if k == 0 else pltpu.roll(xx, k, 0)
            dw = dw + jnp.where(r8 == 3 - k, jnp.sum(da_in * xs, 0, keepdims=True), 0.0)

        @pl.when(t == 0)
        def _():
            dw_ref[...] = dw

        @pl.when(t != 0)
        def _():
            dw_ref[...] += dw

    nb8 = s // SUBLANES
    return pl.pallas_call(
        body, grid=(DN_QKV_DIM // LANES, nt),
        in_specs=[pl.BlockSpec((SUBLANES, LANES), lambda j, t: (jnp.maximum(t * hb - 1, 0), j)),
                  pl.BlockSpec((tm, LANES), lambda j, t: (t, j)),
                  pl.BlockSpec((SUBLANES, LANES), lambda j, t: (jnp.minimum((t + 1) * hb, nb8 - 1), j)),
                  pl.BlockSpec((tm, LANES), lambda j, t: (t, j)),
                  pl.BlockSpec((SUBLANES, LANES), lambda j, t: (jnp.minimum((t + 1) * hb, nb8 - 1), j)),
                  pl.BlockSpec((DN_CONV, LANES), lambda j, t: (0, j))],
        out_specs=[pl.BlockSpec((tm, LANES), lambda j, t: (t, j)),
                   pl.BlockSpec((SUBLANES, LANES), lambda j, t: (0, j))],
        out_shape=[jax.ShapeDtypeStruct((s, DN_QKV_DIM), f32), jax.ShapeDtypeStruct((SUBLANES, DN_QKV_DIM), f32)],
        compiler_params=_cparams("parallel", "arbitrary"), name="dn_conv_bwd_" + tag)(proj, proj, proj, dy, dy, cw)


def _gate_tile(ba, eb, ea, alog, dtb):
    beta = jax.nn.sigmoid(hdot(ba, eb))
    g = -jnp.exp(alog) * jax.nn.softplus(hdot(ba, ea) + dtb)
    return beta, g


def _tri_inv_unit(lower):
    c = lower.shape[0]
    r, col = _iota2((c, c), 0), _iota2((c, c), 1)
    inv = jnp.where(r == col, 1.0, 0.0).astype(f32)
    sh = 0
    while (1 << sh) < c:
        same_2b = lax.shift_right_logical(r, sh + 1) == lax.shift_right_logical(col, sh + 1)
        diff_b = lax.shift_right_logical(r, sh) != lax.shift_right_logical(col, sh)
        off = jnp.where(same_2b & diff_b, lower, 0.0)
        inv = inv - hdot(hdot(inv, off), inv)
        sh += 1
    return inv


def _delta_chunk(q, k, v, gb, betab, state):
    c = DN_CHUNK
    r, col = _iota2((c, c), 0), _iota2((c, c), 1)
    causal, strict = r >= col, r > col
    gc = hdot(jnp.where(causal, 1.0, 0.0).astype(f32), gb)
    diff = gc - gc.T
    decay = jnp.where(causal, jnp.exp(jnp.where(causal, diff, 0.0)), 0.0)
    qs = q * (DN_HEAD_DIM ** -0.5)
    kb = k * betab
    lower = jnp.where(strict, bdot(kb, k, 1, 1), 0.0) * decay
    tinv = _tri_inv_unit(lower)
    eg = jnp.exp(gc)
    u = hdot(tinv, v * betab)
    w = hdot(tinv, kb * eg)
    intra = bdot(qs, k, 1, 1) * decay
    gl = jnp.sum(jnp.where(r == c - 1, gc, 0.0), 0, keepdims=True)
    k_dec = k * jnp.exp(gl - gc)
    v_new = u - bdot(w, state, 1, 0)
    out = bdot(qs * eg, state, 1, 0) + bdot(intra, v_new, 1, 0)
    new_state = state * jnp.exp(gl) + bdot(k_dec, v_new, 0, 0)
    return out, new_state


def delta_fwd(tag, qkv, gb, betab):
    s = qkv.shape[0]
    c, hd = DN_CHUNK, DN_HEAD_DIM
    n = s // c

    hg, ng = DN_HEADS_PER_STEP, DN_HEADS // DN_HEADS_PER_STEP

    def body(q_ref, k_ref, v_ref, g_ref, b_ref, o_ref, st_ref, state):
        @pl.when(pl.program_id(1) == 0)
        def _():
            state[...] = jnp.zeros_like(state)

        for j in range(hg):
            sl = slice(j * hd, (j + 1) * hd)
            st = state[j]
            st_ref[j] = st
            out, new = _delta_chunk(q_ref[:, sl], k_ref[:, sl], v_ref[:, sl], g_ref[:, sl], b_ref[:, sl], st)
            o_ref[:, sl] = out
            state[j] = new

    blk = lambda off: pl.BlockSpec((c, hg * hd), lambda h, i, _o=off: (i, h + _o))
    return pl.pallas_call(
        body, grid=(ng, n),
        in_specs=[blk(0), blk(ng), blk(2 * ng), blk(0), blk(0)],
        out_specs=[blk(0), pl.BlockSpec((hg, None, hd, hd), lambda h, i: (h, i, 0, 0))],
        out_shape=[jax.ShapeDtypeStruct((s, DN_KEY_DIM), f32), jax.ShapeDtypeStruct((DN_HEADS, n, hd, hd), f32)],
        scratch_shapes=[pltpu.VMEM((hg, hd, hd), f32)],
        compiler_params=_cparams("parallel", "arbitrary"), name="delta_" + tag)(qkv, qkv, qkv, gb, betab)


def delta_bwd(tag, qkv, gb, betab, states, do):
    s = qkv.shape[0]
    c, hd = DN_CHUNK, DN_HEAD_DIM
    n = s // c

    hg, ng = DN_HEADS_PER_STEP, DN_HEADS // DN_HEADS_PER_STEP

    def body(q_ref, k_ref, v_ref, g_ref, b_ref, st_ref, do_ref, dq_ref, dk_ref, dv_ref, dg_ref, db_ref, dstate):
        @pl.when(pl.program_id(1) == 0)
        def _():
            dstate[...] = jnp.zeros_like(dstate)

        for j in range(hg):
            sl = slice(j * hd, (j + 1) * hd)
            _, vjp = jax.vjp(_delta_chunk, q_ref[:, sl], k_ref[:, sl], v_ref[:, sl], g_ref[:, sl], b_ref[:, sl], st_ref[j])
            dq, dk, dv, dg, db, dst = vjp((do_ref[:, sl], dstate[j]))
            dq_ref[:, sl], dk_ref[:, sl], dv_ref[:, sl], dg_ref[:, sl], db_ref[:, sl] = dq, dk, dv, dg, db
            dstate[j] = dst

    blk = lambda off: pl.BlockSpec((c, hg * hd), lambda h, i, _o=off: (n - 1 - i, h + _o))
    return pl.pallas_call(
        body, grid=(ng, n),
        in_specs=[blk(0), blk(ng), blk(2 * ng), blk(0), blk(0),
                  pl.BlockSpec((hg, None, hd, hd), lambda h, i: (h, n - 1 - i, 0, 0)), blk(0)],
        out_specs=[blk(0)] * 5,
        out_shape=[jax.ShapeDtypeStruct((s, DN_KEY_DIM), f32)] * 5,
        scratch_shapes=[pltpu.VMEM((hg, hd, hd), f32)],
        compiler_params=_cparams("parallel", "arbitrary"), name="delta_bwd_" + tag)(qkv, qkv, qkv, gb, betab, states, do)


def _dn_out_tile(o, z, ng):
    outs = []
    for h in range(DN_HEADS):
        sl = slice(h * DN_HEAD_DIM, (h + 1) * DN_HEAD_DIM)
        oh = o[:, sl]
        nrm = oh * lax.rsqrt(jnp.mean(oh * oh, -1, keepdims=True) + RMS_EPS) * ng[:, sl]
        outs.append(nrm * jax.nn.silu(z[:, sl]))
    return (jnp.concatenate(outs, -1),)


def _head_selectors():
    r, c = _iota2((BA_PAD, DN_KEY_DIM), 0), _iota2((BA_PAD, DN_KEY_DIM), 1) // DN_HEAD_DIM
    return (r == c).astype(f32), (r == c + DN_HEADS).astype(f32)


def dn_mixer_fwd(tag, proj, cw, alog_b, dtb_b, ng_b):
    eb, ea = _head_selectors()
    ba = (proj, BA_PAD, COL_BA // BA_PAD)
    qkv = dn_conv_fwd(tag, proj, cw)
    betab, gb = rowmap("dn_gate_" + tag, _gate_tile, [ba], [eb, ea, alog_b, dtb_b], [DN_KEY_DIM] * 2, TM_ROW)
    o, states = delta_fwd(tag, qkv, gb, betab)
    z = (proj, DN_KEY_DIM, COL_Z // DN_KEY_DIM)
    a_out = rowmap("dn_out_" + tag, _dn_out_tile, [o, z], [ng_b], [DN_KEY_DIM], TM_ROW)[0]
    return a_out, (qkv, betab, gb, o, states)


def dn_mixer_bwd(tag, proj, cw, alog_b, dtb_b, ng_b, res, da_out):
    qkv, betab, gb, o, states = res
    eb, ea = _head_selectors()
    ba = (proj, BA_PAD, COL_BA // BA_PAD)
    z = (proj, DN_KEY_DIM, COL_Z // DN_KEY_DIM)
    (do, dz), (dng,) = rowmap_bwd("dn_out_bwd_" + tag, _dn_out_tile, [o, z], [ng_b], [da_out], TM_ROW)
    dq, dk, dv, dgb, dbetab = delta_bwd(tag, qkv, gb, betab, states, do)
    dqkv_raw, dcw = dn_conv_bwd(tag, proj, cw, jnp.concatenate([dq, dk, dv], 1))
    (dba,), (dalog, ddtb) = rowmap_bwd("dn_gate_bwd_" + tag, _gate_tile, [ba], [eb, ea, alog_b, dtb_b],
                                       [dbetab, dgb], TM_ROW, par_mask=[False, False, True, True])
    return dqkv_raw, dz, dba, dcw[:DN_CONV], dalog, ddtb, dng


def _swap_halves(x):
    n = x.shape[1]
    first = (_iota2((1, n), 1) % SW_HEAD_DIM) < SW_HEAD_DIM // 2
    return jnp.where(first, pltpu.roll(x, n - SW_HEAD_DIM // 2, 1), pltpu.roll(x, SW_HEAD_DIM // 2, 1))


def _rope_apply(x, cos, sin_signed):
    return x * cos + _swap_halves(x) * sin_signed


def _rope_transpose(dy, cos, sin_signed):
    return dy * cos + _swap_halves(dy * sin_signed)


def rope_tables(positions, s):
    half = SW_HEAD_DIM // 2
    inv_freq = ROPE_THETA ** (-jnp.arange(0, SW_HEAD_DIM, 2, dtype=f32) / SW_HEAD_DIM)
    ang = positions.reshape(s, 1).astype(f32) * inv_freq[None, :]
    cos, sin = jnp.cos(ang), jnp.sin(ang)
    cos_t = jnp.tile(jnp.concatenate([cos, cos], 1), (1, SW_HEADS))
    sin_t = jnp.tile(jnp.concatenate([-sin, sin], 1), (1, SW_HEADS))
    assert cos_t.shape == (s, SW_DIM) and half * 2 == SW_HEAD_DIM
    return cos_t, sin_t


def rope_fwd(tag, proj, cos, sin):
    def fn(q, k, v, c, sg):
        return _rope_apply(q, c, sg), _rope_apply(k, c, sg), v

    rows = [(proj, SW_DIM, COL_SWQ // SW_DIM), (proj, SW_DIM, COL_SWK // SW_DIM), (proj, SW_DIM, COL_SWV // SW_DIM), cos, sin]
    return rowmap("rope_" + tag, fn, rows, [], [SW_DIM] * 3, TM_ROW)


def _swa_block(q, kp, kc, vp, vc, first):
    blk = SW_BLOCK
    kk = jnp.concatenate([kp, kc], 0)
    vv = jnp.concatenate([vp, vc], 0)
    dist = (_iota2((blk, 2 * blk), 0) + blk) - _iota2((blk, 2 * blk), 1)
    kj = _iota2((blk, 2 * blk), 1)
    valid = (dist >= 0) & (dist <= blk) & ((kj >= blk) | jnp.logical_not(first))
    lane_head = _iota2((1, LANES), 1) // SW_HEAD_DIM
    outs, lses = [], []
    for p in range(SW_DIM // LANES):
        sl = slice(p * LANES, (p + 1) * LANES)
        qp, kp_, vp_ = q[:, sl], kk[:, sl], vv[:, sl]
        o_pair = jnp.zeros((blk, LANES), f32)
        l_pair = jnp.zeros((blk, LANES), f32)
        for e in range(LANES // SW_HEAD_DIM):
            msk = lane_head == e
            sc = bdot(jnp.where(msk, qp, 0.0), kp_, 1, 1) * (SW_HEAD_DIM ** -0.5)
            sc = jnp.where(valid, sc, -1e30)
            m = lax.stop_gradient(jnp.max(sc, -1, keepdims=True))
            pe = jnp.exp(sc - m)
            l = jnp.sum(pe, -1, keepdims=True)
            o = bdot(pe, vp_, 1, 0) / l
            o_pair = o_pair + jnp.where(msk, o, 0.0)
            l_pair = l_pair + jnp.where(msk, m + jnp.log(l), 0.0)
        outs.append(o_pair)
        lses.append(l_pair)
    return jnp.concatenate(outs, -1), jnp.concatenate(lses, -1)


def _swa_specs(r):
    cur = pl.BlockSpec((SW_BLOCK, SW_DIM), lambda rho, n: (n, rho))
    prev = pl.BlockSpec((SW_BLOCK, SW_DIM), lambda rho, n: (jnp.maximum(n - 1, 0), rho))
    return cur, prev


def swa_fwd(tag, r, q, k, v):
    s = q.shape[0]
    ln = s // r
    q2, k2, v2 = (t.reshape(ln, r * SW_DIM) for t in (q, k, v))
    cur, prev = _swa_specs(r)

    def body(q_ref, kp_ref, kc_ref, vp_ref, vc_ref, o_ref, l_ref):
        o, l = _swa_block(q_ref[...], kp_ref[...], kc_ref[...], vp_ref[...], vc_ref[...], pl.program_id(1) == 0)
        o_ref[...] = o
        l_ref[...] = l

    o, l = pl.pallas_call(
        body, grid=(r, ln // SW_BLOCK),
        in_specs=[cur, prev, cur, prev, cur], out_specs=[cur, cur],
        out_shape=[jax.ShapeDtypeStruct((ln, r * SW_DIM), f32)] * 2,
        compiler_params=_cparams("parallel", "parallel"), name=f"swa{r}_{tag}")(q2, k2, k2, v2, v2)
    return o.reshape(s, SW_DIM), l.reshape(s, SW_DIM)


def swa_bwd(tag, r, q, k, v, do, dl):
    s = q.shape[0]
    ln = s // r
    q2, k2, v2, do2, dl2 = (t.reshape(ln, r * SW_DIM) for t in (q, k, v, do, dl))
    cur, prev = _swa_specs(r)

    def body(q_ref, kp_ref, kc_ref, vp_ref, vc_ref, do_ref, dl_ref, dq_ref, dka_ref, dkb_ref, dva_ref, dvb_ref):
        first = pl.program_id(1) == 0
        _, vjp = jax.vjp(lambda *a: _swa_block(*a, first), q_ref[...], kp_ref[...], kc_ref[...], vp_ref[...], vc_ref[...])
        dq_ref[...], dka_ref[...], dkb_ref[...], dva_ref[...], dvb_ref[...] = vjp((do_ref[...], dl_ref[...]))

    outs = pl.pallas_call(
        body, grid=(r, ln // SW_BLOCK),
        in_specs=[cur, prev, cur, prev, cur, cur, cur], out_specs=[cur] * 5,
        out_shape=[jax.ShapeDtypeStruct((ln, r * SW_DIM), f32)] * 5,
        compiler_params=_cparams("parallel", "parallel"), name=f"swa{r}_bwd_{tag}")(q2, k2, k2, v2, v2, do2, dl2)
    return [t.reshape(s, SW_DIM) for t in outs]


def _combine_tile(o1, l1, o2, l2, o3, l3):
    m = lax.stop_gradient(jnp.maximum(jnp.maximum(l1, l2), l3))
    e1, e2, e3 = jnp.exp(l1 - m), jnp.exp(l2 - m), jnp.exp(l3 - m)
    return ((o1 * e1 + o2 * e2 + o3 * e3) / (e1 + e2 + e3),)


def swa_merge_bwd(tag, grads, cos, sin):
    s = cos.shape[0]
    tm = SW_BLOCK
    nt = s // tm
    here = pl.BlockSpec((tm, SW_DIM), lambda i: (i, 0))
    arrs, specs = [], []
    for r, g in zip(SW_DILATIONS, grads):
        ahead = pl.BlockSpec((tm, SW_DIM), lambda i, _r=r: (jnp.minimum(i + _r, nt - 1), 0))
        arrs += g
        specs += [here, ahead, here, ahead, here]

    def body(*refs):
        i = pl.program_id(0)
        c_ref, s_ref = refs[15], refs[16]
        dq_ref, dk_ref, dv_ref = refs[17:]
        dq = jnp.zeros((tm, SW_DIM), f32)
        dk = jnp.zeros((tm, SW_DIM), f32)
        dv = jnp.zeros((tm, SW_DIM), f32)
        for b, r in enumerate(SW_DILATIONS):
            gq, gka, gkb, gva, gvb = refs[5 * b:5 * b + 5]
            inside = i + r < nt
            dq = dq + gq[...]
            dk = dk + gkb[...] + jnp.where(inside, gka[...], 0.0)
            dv = dv + gvb[...] + jnp.where(inside, gva[...], 0.0)
        dq_ref[...] = _rope_transpose(dq, c_ref[...], s_ref[...])
        dk_ref[...] = _rope_transpose(dk, c_ref[...], s_ref[...])
        dv_ref[...] = dv

    return pl.pallas_call(
        body, grid=(nt,), in_specs=specs + [here, here], out_specs=[here] * 3,
        out_shape=[jax.ShapeDtypeStruct((s, SW_DIM), f32)] * 3,
        compiler_params=_cparams("parallel"), name="swa_merge_bwd_" + tag)(*arrs, cos, sin)


def swa_mixer_fwd(tag, proj, cos, sin):
    q, k, v = rope_fwd(tag, proj, cos, sin)
    ols = []
    for r in SW_DILATIONS:
        ols += list(swa_fwd(tag, r, q, k, v))
    b_out = rowmap("swa_comb_" + tag, _combine_tile, ols, [], [SW_DIM], TM_ROW)[0]
    return b_out, (q, k, v, ols)


def swa_mixer_bwd(tag, cos, sin, res, db_out):
    q, k, v, ols = res
    dols, _ = rowmap_bwd("swa_comb_bwd_" + tag, _combine_tile, ols, [], [db_out], TM_ROW)
    grads = [swa_bwd(tag, r, q, k, v, dols[2 * b], dols[2 * b + 1]) for b, r in enumerate(SW_DILATIONS)]
    return swa_merge_bwd(tag, grads, cos, sin)


TM_S5 = 256
S5_LANES = 2 * S5_STATE
S5_GPB = LANES // S5_GROUP
S5_NBLK = D_MODEL // LANES
S5_BW = S5_GPB * S5_LANES
S5_WIDTH = S5_GROUPS * S5_LANES


def _swap_ri(x):
    n = x.shape[1]
    first = (_iota2((1, n), 1) % S5_LANES) < S5_STATE
    return jnp.where(first, pltpu.roll(x, n - S5_STATE, 1), pltpu.roll(x, S5_STATE, 1))


def _s5_disc_tile(a_re, a_im, log_dt, b_re, b_im, expand):
    dt = jnp.exp(log_dt)
    mag = jnp.exp(a_re * dt)
    abar_re, abar_im = mag * jnp.cos(a_im * dt), mag * jnp.sin(a_im * dt)
    n_re, n_im = abar_re - 1.0, abar_im
    den = a_re * a_re + a_im * a_im
    c_re = (n_re * a_re + n_im * a_im) / den
    c_im = (n_im * a_re - n_re * a_im) / den
    cx_re, cx_im = hdot(c_re, expand), hdot(c_im, expand)
    return abar_re, abar_im, cx_re * b_re - cx_im * b_im, cx_re * b_im + cx_im * b_re


def _s5_expand():
    return (_iota2((S5_STATE, S5_STATE * S5_GROUP), 1) // S5_GROUP == _iota2((S5_STATE, S5_STATE * S5_GROUP), 0)).astype(f32)


def _lane_layout(re, im):
    return jnp.concatenate([re, im], 1).reshape(1, S5_WIDTH)


def s5_tables(a_re, a_im, log_dt):
    dt = jnp.broadcast_to(log_dt.reshape(S5_GROUPS, 1), (S5_GROUPS, S5_STATE))
    are_l, aim_l, ldt_l = _lane_layout(a_re, a_re), _lane_layout(a_im, a_im), _lane_layout(dt, dt)
    t = TM_S5

    def body(are_ref, aim_ref, ldt_ref, a1_ref, a2_ref, a1r_ref, a2r_ref):
        dtv = jnp.exp(ldt_ref[...])
        lre, lim = are_ref[...] * dtv, aim_ref[...] * dtv
        sign = jnp.where((_iota2((1, S5_BW), 1) % S5_LANES) < S5_STATE, -1.0, 1.0)
        row = _iota2((t, S5_BW), 0)
        for asc, o1, o2 in ((True, a1_ref, a2_ref), (False, a1r_ref, a2r_ref)):
            n = (row + 1 if asc else t - row).astype(f32)
            mag = jnp.exp(n * lre)
            o1[...] = mag * jnp.cos(n * lim)
            o2[...] = sign * mag * jnp.sin(n * lim)

    lane = pl.BlockSpec((1, S5_BW), lambda j: (0, j))
    tab = pl.BlockSpec((t, S5_BW), lambda j: (0, j))
    return pl.pallas_call(
        body, grid=(S5_NBLK,), in_specs=[lane] * 3, out_specs=[tab] * 4,
        out_shape=[jax.ShapeDtypeStruct((t, S5_WIDTH), f32)] * 4,
        compiler_params=_cparams("parallel"), name="s5_tables")(are_l, aim_l, ldt_l)


def s5_pack_weights(bbar_re, bbar_im, c_re, c_im):
    eye = jnp.eye(S5_GPB, dtype=f32)
    bb = jnp.stack([bbar_re.reshape(S5_GROUPS, S5_STATE, S5_GROUP), bbar_im.reshape(S5_GROUPS, S5_STATE, S5_GROUP)], 1)
    bb = bb.transpose(0, 3, 1, 2).reshape(S5_NBLK, S5_GPB, S5_GROUP, S5_LANES)
    wb = (bb[:, :, :, None, :] * eye[None, :, None, :, None]).reshape(S5_NBLK, LANES, S5_BW)
    cc = jnp.stack([c_re, -c_im], 1)
    cc = cc.transpose(0, 1, 3, 2).reshape(S5_NBLK, S5_GPB, S5_LANES, S5_GROUP)
    wc = (cc[:, :, :, None, :] * eye[None, :, None, :, None]).reshape(S5_NBLK, S5_BW, LANES)
    return wb, wc


def s5_unpack_weight_grads(dwb, dwc):
    d5 = dwb.reshape(S5_NBLK, S5_GPB, S5_GROUP, S5_GPB, S5_LANES)
    dbb = jnp.stack([d5[:, gl, :, gl, :] for gl in range(S5_GPB)])
    dbb = dbb.transpose(1, 0, 2, 3).reshape(S5_GROUPS, S5_GROUP, 2, S5_STATE).transpose(0, 2, 3, 1)
    dbbar_re = dbb[:, 0].reshape(S5_GROUPS, S5_STATE * S5_GROUP)
    dbbar_im = dbb[:, 1].reshape(S5_GROUPS, S5_STATE * S5_GROUP)
    c5 = dwc.reshape(S5_NBLK, S5_GPB, S5_LANES, S5_GPB, S5_GROUP)
    dcc = jnp.stack([c5[:, gl, :, gl, :] for gl in range(S5_GPB)])
    dcc = dcc.transpose(1, 0, 2, 3).reshape(S5_GROUPS, 2, S5_STATE, S5_GROUP).transpose(0, 1, 3, 2)
    return dbbar_re, dbbar_im, dcc[:, 0], -dcc[:, 1]


def _s5_step_rows(t):
    d, out = 1, []
    while d < t:
        out.append(d)
        d *= 2
    return out


def s5_core_fwd(tag, u, wb, wc, a1, a2, dskip):
    s = u.shape[0]
    t = TM_S5

    def body(u_ref, wb_ref, wc_ref, a1_ref, a2_ref, d_ref, y_ref, x_ref, carry, carry_sw, tail_sw):
        @pl.when(pl.program_id(1) == 0)
        def _():
            carry[...] = jnp.zeros_like(carry)
            carry_sw[...] = jnp.zeros_like(carry_sw)

        uv = u_ref[...]
        x = bdot(uv, wb_ref[...], 1, 0)
        row = _iota2((t, S5_BW), 0)
        for d in _s5_step_rows(t):
            sh = jnp.where(row >= d, pltpu.roll(x, d, 0), 0.0)
            x = x + a1_ref[d - 1:d, :] * sh + a2_ref[d - 1:d, :] * _swap_ri(sh)
        x = x + a1_ref[...] * carry[...] + a2_ref[...] * carry_sw[...]
        x_ref[...] = x
        tail_sw[...] = _swap_ri(x[t - SUBLANES:, :])
        carry[...] = x_ref[t - 1:t, :]
        carry_sw[...] = tail_sw[SUBLANES - 1:SUBLANES, :]
        y_ref[...] = bdot(x, wc_ref[...], 1, 0) + d_ref[...] * uv

    return pl.pallas_call(
        body, grid=(S5_NBLK, s // t),
        in_specs=[pl.BlockSpec((t, LANES), lambda j, i: (i, j)),
                  pl.BlockSpec((None, LANES, S5_BW), lambda j, i: (j, 0, 0)),
                  pl.BlockSpec((None, S5_BW, LANES), lambda j, i: (j, 0, 0)),
                  pl.BlockSpec((t, S5_BW), lambda j, i: (0, j)),
                  pl.BlockSpec((t, S5_BW), lambda j, i: (0, j)),
                  pl.BlockSpec((1, LANES), lambda j, i: (0, j))],
        out_specs=[pl.BlockSpec((t, LANES), lambda j, i: (i, j)), pl.BlockSpec((t, S5_BW), lambda j, i: (i, j))],
        out_shape=[jax.ShapeDtypeStruct((s, D_MODEL), f32), jax.ShapeDtypeStruct((s, S5_WIDTH), f32)],
        scratch_shapes=[pltpu.VMEM((1, S5_BW), f32), pltpu.VMEM((1, S5_BW), f32), pltpu.VMEM((SUBLANES, S5_BW), f32)],
        compiler_params=_cparams("parallel", "arbitrary"), name="s5_core_" + tag)(u, wb, wc, a1, a2, dskip)


def s5_core_bwd(tag, u, x, wb, wc, a1, a2, a1r, a2r, dskip, dy):
    s = u.shape[0]
    t = TM_S5
    nt = s // t
    hb = t // SUBLANES

    def body(u_ref, dy_ref, x_ref, xh_ref, wb_ref, wc_ref, a1_ref, a2_ref, a1r_ref, a2r_ref, d_ref,
             du_ref, dwb_ref, dwc_ref, dd_ref, q1_ref, q2_ref, carry, carry_sw, lam_scr, head_sw):
        i = pl.program_id(1)
        tt = nt - 1 - i

        @pl.when(i == 0)
        def _():
            carry[...] = jnp.zeros_like(carry)
            carry_sw[...] = jnp.zeros_like(carry_sw)

        uv, dyv, xv = u_ref[...], dy_ref[...], x_ref[...]
        lam = bdot(dyv, wc_ref[...], 1, 1)
        row = _iota2((t, S5_BW), 0)
        for d in _s5_step_rows(t):
            sh = jnp.where(row < t - d, pltpu.roll(lam, t - d, 0), 0.0)
            lam = lam + a1_ref[d - 1:d, :] * sh - a2_ref[d - 1:d, :] * _swap_ri(sh)
        lam = lam + a1r_ref[...] * carry[...] - a2r_ref[...] * carry_sw[...]
        lam_scr[...] = lam
        head_sw[...] = _swap_ri(lam[:SUBLANES, :])
        carry[...] = lam_scr[0:1, :]
        carry_sw[...] = head_sw[0:1, :]
        du_ref[...] = bdot(lam, wb_ref[...], 1, 1) + d_ref[...] * dyv
        x_last = jnp.where(tt > 0, xh_ref[SUBLANES - 1:SUBLANES, :], 0.0)
        x_prev = jnp.where(row == 0, x_last, pltpu.roll(xv, 1, 0))
        p1, p2 = lam * x_prev, lam * _swap_ri(x_prev)
        q1 = p1[:SUBLANES, :]
        q2 = p2[:SUBLANES, :]
        for k in range(1, hb):
            q1 = q1 + p1[k * SUBLANES:(k + 1) * SUBLANES, :]
            q2 = q2 + p2[k * SUBLANES:(k + 1) * SUBLANES, :]
        upd = [(dwb_ref, bdot(uv, lam, 0, 0)), (dwc_ref, bdot(xv, dyv, 0, 0)),
               (dd_ref, jnp.sum(dyv * uv, 0, keepdims=True)), (q1_ref, q1), (q2_ref, q2)]

        @pl.when(i == 0)
        def _():
            for ref, val in upd:
                ref[...] = val

        @pl.when(i != 0)
        def _():
            for ref, val in upd:
                ref[...] += val

    nb8 = s // SUBLANES
    rev = lambda w: pl.BlockSpec((t, w), lambda j, i: (nt - 1 - i, j))
    tab = pl.BlockSpec((t, S5_BW), lambda j, i: (0, j))
    return pl.pallas_call(
        body, grid=(S5_NBLK, nt),
        in_specs=[rev(LANES), rev(LANES), rev(S5_BW),
                  pl.BlockSpec((SUBLANES, S5_BW), lambda j, i: (jnp.maximum((nt - 1 - i) * hb - 1, 0), j)),
                  pl.BlockSpec((None, LANES, S5_BW), lambda j, i: (j, 0, 0)),
                  pl.BlockSpec((None, S5_BW, LANES), lambda j, i: (j, 0, 0)),
                  tab, tab, tab, tab, pl.BlockSpec((1, LANES), lambda j, i: (0, j))],
        out_specs=[rev(LANES),
                   pl.BlockSpec((None, LANES, S5_BW), lambda j, i: (j, 0, 0)),
                   pl.BlockSpec((None, S5_BW, LANES), lambda j, i: (j, 0, 0)),
                   pl.BlockSpec((1, LANES), lambda j, i: (0, j)),
                   pl.BlockSpec((SUBLANES, S5_BW), lambda j, i: (0, j)),
                   pl.BlockSpec((SUBLANES, S5_BW), lambda j, i: (0, j))],
        out_shape=[jax.ShapeDtypeStruct((s, D_MODEL), f32),
                   jax.ShapeDtypeStruct((S5_NBLK, LANES, S5_BW), f32),
                   jax.ShapeDtypeStruct((S5_NBLK, S5_BW, LANES), f32),
                   jax.ShapeDtypeStruct((1, D_MODEL), f32),
                   jax.ShapeDtypeStruct((SUBLANES, S5_WIDTH), f32),
                   jax.ShapeDtypeStruct((SUBLANES, S5_WIDTH), f32)],
        scratch_shapes=[pltpu.VMEM((1, S5_BW), f32), pltpu.VMEM((1, S5_BW), f32),
                        pltpu.VMEM((t, S5_BW), f32), pltpu.VMEM((SUBLANES, S5_BW), f32)],
        compiler_params=_cparams("parallel", "arbitrary"),
        name="s5_core_bwd_" + tag)(u, dy, x, x, wb, wc, a1, a2, a1r, a2r, dskip)


def _gelu_tile(y):
    return (jax.nn.gelu(y),)


def s5_mixer_fwd(tag, u, prm, w_og):
    a_re, a_im, log_dt, b_re, b_im, c_re, c_im, dskip = prm
    disc_in = [a_re, a_im, log_dt.reshape(S5_GROUPS, 1), b_re.reshape(S5_GROUPS, -1), b_im.reshape(S5_GROUPS, -1)]
    abar_re, abar_im, bbar_re, bbar_im = rowmap("s5_disc_" + tag, _s5_disc_tile, disc_in, [_s5_expand()],
                                                [S5_STATE, S5_STATE, S5_STATE * S5_GROUP, S5_STATE * S5_GROUP], S5_GROUPS)
    del abar_re, abar_im
    a1, a2, a1r, a2r = s5_tables(a_re, a_im, log_dt)
    wb, wc = s5_pack_weights(bbar_re, bbar_im, c_re, c_im)
    wb, wc = wb.astype(bf16), wc.astype(bf16)
    y, x = s5_core_fwd(tag, u, wb, wc, a1, a2, dskip.reshape(1, D_MODEL))
    hid = rowmap("s5_gelu_" + tag, _gelu_tile, [y], [], [D_MODEL], TM_ROW)[0]
    og = mm_nn("s5_og_" + tag, hid, w_og)
    mix = rowmap("s5_glu_" + tag, _glu_tile, [og], [], [D_MODEL], TM_ROW)[0]
    return mix, (disc_in, a1, a2, a1r, a2r, wb, wc, x, y, hid, og)


def s5_mixer_bwd(tag, u, prm, w_og, res, dmix):
    a_re, a_im, log_dt, b_re, b_im, c_re, c_im, dskip = prm
    disc_in, a1, a2, a1r, a2r, wb, wc, x, y, hid, og = res
    (dog,), _ = rowmap_bwd("s5_glu_bwd_" + tag, _glu_tile, [og], [], [dmix], TM_ROW)
    dw_og = mm_tn("s5_og_dw_" + tag, hid, dog)
    dhid = mm_nt("s5_og_dx_" + tag, dog, w_og)
    (dy,), _ = rowmap_bwd("s5_gelu_bwd_" + tag, _gelu_tile, [y], [], [dhid], TM_ROW)
    du, dwb, dwc, ddskip, q1, q2 = s5_core_bwd(tag, u, x, wb, wc, a1, a2, a1r, a2r, dskip.reshape(1, D_MODEL), dy)
    dbbar_re, dbbar_im, dc_re, dc_im = s5_unpack_weight_grads(dwb, dwc)
    q1 = q1.sum(0).reshape(S5_GROUPS, 2, S5_STATE)
    q2 = q2.sum(0).reshape(S5_GROUPS, 2, S5_STATE)
    dabar_re, dabar_im = q1[:, 0] + q1[:, 1], q2[:, 1] - q2[:, 0]
    grads, _ = rowmap_bwd("s5_disc_bwd_" + tag, _s5_disc_tile, disc_in, [_s5_expand()],
                          [dabar_re, dabar_im, dbbar_re, dbbar_im], S5_GROUPS, par_mask=[False])
    da_re, da_im, dlog_dt, db_re, db_im = grads
    return du, (da_re, da_im, dlog_dt.reshape(S5_GROUPS), db_re.reshape(b_re.shape), db_im.reshape(b_im.shape),
                dc_re, dc_im, ddskip.reshape(D_MODEL)), dw_og


HYB_IN = 3592
_IN_B0, _IN_SW0 = 2048, 2056


IN_SHARD = HYB_IN // 4
SHARD_ORDER_GRADS = ("hyb_w_in", "ffn_wg", "ffn_wu", "ffn_wd")


def _w_in_pieces():
    runs = [(0, _IN_B0, 0), (_IN_B0, _IN_SW0, COL_BA), (_IN_SW0, HYB_IN, _IN_B0)]
    out = []
    for sh in range(4):
        lo, hi = sh * IN_SHARD, (sh + 1) * IN_SHARD
        for r_lo, r_hi, c_lo in runs:
            a, b = max(lo, r_lo), min(hi, r_hi)
            if a < b:
                out.append((sh, a - lo, b - lo, c_lo + a - r_lo))
    return out


def w_in_to_canonical(tag, layer, w4):
    tr = 128

    def body(w_ref, o_ref):
        o_ref[:, COL_BA:] = jnp.zeros((tr, BA_PAD), o_ref.dtype)
        for sh, a, b, c in _w_in_pieces():
            o_ref[:, c:c + b - a] = w_ref[sh, :, a:b]

    return pl.pallas_call(
        body, grid=(D_MODEL // tr,),
        in_specs=[pl.BlockSpec((4, None, tr, IN_SHARD), lambda i: (0, layer, i, 0))],
        out_specs=pl.BlockSpec((tr, PROJ_COLS), lambda i: (i, 0)),
        out_shape=jax.ShapeDtypeStruct((D_MODEL, PROJ_COLS), w4.dtype),
        compiler_params=_cparams("parallel"), name="w_in_canon_" + tag)(w4)


def w_in_grad_to_shards(tag, g):
    tr = 128

    def body(g_ref, o_ref):
        for sh, a, b, c in _w_in_pieces():
            o_ref[sh, :, a:b] = g_ref[:, c:c + b - a]

    return pl.pallas_call(
        body, grid=(D_MODEL // tr,),
        in_specs=[pl.BlockSpec((tr, PROJ_COLS), lambda i: (i, 0))],
        out_specs=pl.BlockSpec((4, tr, IN_SHARD), lambda i: (0, i, 0)),
        out_shape=jax.ShapeDtypeStruct((4, D_MODEL, IN_SHARD), f32),
        compiler_params=_cparams("parallel"), name="w_in_grad_shards_" + tag)(g)


def _add2(name, a, b):
    return rowmap(name, lambda p, q: (p + q,), [a, b], [], [a.shape[1]], _pick(a.shape[0], (256, 128, 64, 32, 16, 8)))[0]


def local_step(x, mem, positions, target, p):
    s = x.shape[0]
    cos, sin = rope_tables(positions, s)
    row = lambda v: v.reshape(1, -1).astype(f32)
    wg4, wu4, wd4 = (p[n].astype(bf16) for n in ("ffn_wg", "ffn_wu", "ffn_wd"))
    h = x
    tape = []
    for l in range(DEPTH):
        i, tag = l // 2, str(l)
        t = {"h0": h}
        if l % 2 == 0:
            t["w_in"] = w_in_to_canonical(tag, i, p["hyb_w_in"].astype(bf16))
            t["w_out"] = p["hyb_w_out"][i].astype(bf16)
            t["dn_prm"] = (p["dn_conv_w"][i].astype(f32), row(jnp.repeat(p["dn_a_log"][i], DN_HEAD_DIM)),
                           row(jnp.repeat(p["dn_dt_bias"][i], DN_HEAD_DIM)), row(jnp.tile(p["dn_norm_g"][i], DN_HEADS)))
            t["proj"] = mm_nn("hyb_in_" + tag, h, t["w_in"])
            a_out, t["dn"] = dn_mixer_fwd(tag, t["proj"], *t["dn_prm"])
            b_out, t["swa"] = swa_mixer_fwd(tag, t["proj"], cos, sin)
            t["mixed"] = jnp.concatenate([a_out, b_out], 1)
            mix = mm_nn("hyb_out_" + tag, t["mixed"], t["w_out"])
        else:
            t["s5_prm"] = tuple(p[n][i].astype(f32) for n in
                                ("s5_a_re", "s5_a_im", "s5_log_dt", "s5_b_re", "s5_b_im", "s5_c_re", "s5_c_im", "s5_d"))
            t["w_og"] = jnp.concatenate([p["s5_glu_wo"][i], p["s5_glu_wg"][i]], 1).astype(bf16)
            mix, t["s5"] = s5_mixer_fwd(tag, h, t["s5_prm"], t["w_og"])
        t["mix"] = mix
        t["ln"] = [(row(p[g][l]), row(p[b][l])) for g, b in
                   (("ln_mix_g", "ln_mix_b"), ("ln_x_g", "ln_x_b"), ("ln_ffn_g", "ln_ffn_b"))]
        t["h1"] = postnorm_fwd("mix" + tag, h, mix, *t["ln"][0])
        t["wq"], t["wo"] = p["xq_w"][l].astype(bf16), p["xo_w"][l].astype(bf16)
        t["wkv"] = jnp.concatenate([p["xk_w"][l], p["xv_w"][l]], 1).astype(bf16)
        t["xo"], t["xres"] = xattn_fwd(tag, t["h1"], mem, t["wq"], t["wkv"], t["wo"])
        t["h2"] = postnorm_fwd("x" + tag, t["h1"], t["xo"], *t["ln"][1])
        t["fo"], t["fres"] = ffn_fwd(tag, l, t["h2"], wg4, wu4, wd4)
        h = postnorm_fwd("ffn" + tag, t["h2"], t["fo"], *t["ln"][2])
        tape.append(t)

    part, dh = loss_head(h, target)
    loss = jnp.sum(part)

    g = {n: [None] * v.shape[1 if n in SHARD_ORDER_GRADS else 0] for n, v in p.items()}
    for l in reversed(range(DEPTH)):
        i, tag, t = l // 2, str(l), tape[l]
        dh2a, dfo, dg, db = postnorm_bwd("ffn" + tag, t["h2"], t["fo"], *t["ln"][2], dh)
        g["ln_ffn_g"][l], g["ln_ffn_b"][l] = dg[0], db[0]
        dh2b, g["ffn_wg"][l], g["ffn_wu"][l], g["ffn_wd"][l] = ffn_bwd(tag, l, t["h2"], wg4, wu4, wd4, t["fres"], dfo)
        dh1a, dxo, dg, db = postnorm_bwd("x" + tag, t["h1"], t["xo"], *t["ln"][1], [dh2a, dh2b])
        g["ln_x_g"][l], g["ln_x_b"][l] = dg[0], db[0]
        dh1b, g["xq_w"][l], dwkv, g["xo_w"][l] = xattn_bwd(tag, t["h1"], mem, t["wq"], t["wkv"], t["wo"], t["xres"], dxo)
        g["xk_w"][l], g["xv_w"][l] = dwkv[:, :D_MODEL], dwkv[:, D_MODEL:]
        dh0a, dmix, dg, db = postnorm_bwd("mix" + tag, t["h0"], t["mix"], *t["ln"][0], [dh1a, dh1b])
        g["ln_mix_g"][l], g["ln_mix_b"][l] = dg[0], db[0]
        if l % 2 == 0:
            g["hyb_w_out"][i] = mm_tn("hyb_out_dw_" + tag, t["mixed"], dmix)
            dmixed = mm_nt("hyb_out_dx_" + tag, dmix, t["w_out"])
            dqkv, dz, dba, dcw, dalog, ddtb, dng = dn_mixer_bwd(tag, t["proj"], *t["dn_prm"], t["dn"], (dmixed, DN_KEY_DIM, 0))
            g["dn_conv_w"][i] = dcw
            g["dn_a_log"][i] = dalog.reshape(DN_HEADS, DN_HEAD_DIM).sum(1)
            g["dn_dt_bias"][i] = ddtb.reshape(DN_HEADS, DN_HEAD_DIM).sum(1)
            g["dn_norm_g"][i] = dng.reshape(DN_HEADS, DN_HEAD_DIM).sum(0)
            dq, dk, dv = swa_mixer_bwd(tag, cos, sin, t["swa"], (dmixed, SW_DIM, 1))
            dproj = jnp.concatenate([dqkv, dz, dq, dk, dv, dba], 1)
            g["hyb_w_in"][i] = w_in_grad_to_shards(tag, mm_tn("hyb_in_dw_" + tag, t["h0"], dproj))
            dh0b = mm_nt("hyb_in_dx_" + tag, dproj, t["w_in"])
        else:
            dh0b, dprm, dw_og = s5_mixer_bwd(tag, t["h0"], t["s5_prm"], t["w_og"], t["s5"], dmix)
            for n, v in zip(("s5_a_re", "s5_a_im", "s5_log_dt", "s5_b_re", "s5_b_im", "s5_c_re", "s5_c_im", "s5_d"), dprm):
                g[n][i] = v
            g["s5_glu_wo"][i], g["s5_glu_wg"][i] = dw_og[:, :D_MODEL], dw_og[:, D_MODEL:]
        dh = [dh0a, dh0b]
    grad_x = _add2("grad_x", dh[0], dh[1])
    grads = {n: jnp.stack(v, 1 if n in SHARD_ORDER_GRADS else 0) for n, v in g.items()}
    return loss, grad_x, grads


WEIGHT_NAMES = ("hyb_w_in", "dn_conv_w", "dn_a_log", "dn_dt_bias", "dn_norm_g", "hyb_w_out", "s5_a_re", "s5_a_im",
                "s5_log_dt", "s5_b_re", "s5_b_im", "s5_c_re", "s5_c_im", "s5_d", "s5_glu_wo", "s5_glu_wg",
                "ln_mix_g", "ln_mix_b", "xq_w", "xk_w", "xv_w", "xo_w", "ln_x_g", "ln_x_b",
                "ffn_wg", "ffn_wu", "ffn_wd", "ln_ffn_g", "ln_ffn_b")
SHARD_AXIS = {"hyb_w_in": 2, "dn_conv_w": 2, "hyb_w_out": 1, "s5_d": 1, "s5_glu_wo": 1, "s5_glu_wg": 1,
              "xq_w": 1, "xk_w": 1, "xv_w": 1, "xo_w": 1, "ffn_wg": 2, "ffn_wu": 2, "ffn_wd": 1}
GATHER_F32 = ("dn_conv_w", "s5_d")
N_CHIPS = 4
PACK_COLS = 1024
_ANY = pl.BlockSpec(memory_space=pl.ANY)


def _pos():
    return lax.axis_index("x"), lax.axis_index("y"), lax.axis_index("c")


def _chip_peers(mx, my):
    return [(1 - mx, my), (mx, 1 - my), (1 - mx, 1 - my)]


def _rcopy(src, dst, ssem, rsem, dev):
    return pltpu.make_async_remote_copy(src_ref=src, dst_ref=dst, send_sem=ssem, recv_sem=rsem,
                                        device_id=dev, device_id_type=pl.DeviceIdType.MESH)


def comm_allgather4(name, x):
    def body(x_ref, o_ref, ssem, rsem, lsem):
        mx, my, mc = _pos()
        me = 2 * mx + my
        peers = _chip_peers(mx, my)
        loc = pltpu.make_async_copy(x_ref, o_ref.at[me], lsem)
        loc.start()
        sends = [_rcopy(x_ref, o_ref.at[me], ssem.at[k], rsem.at[k], (px, py, mc)) for k, (px, py) in enumerate(peers)]
        for cp in sends:
            cp.start()
        for k, (px, py) in enumerate(peers):
            _rcopy(x_ref, o_ref.at[2 * px + py], ssem.at[k], rsem.at[k], (px, py, mc)).wait_recv()
        for cp in sends:
            cp.wait_send()
        loc.wait()

    return pl.pallas_call(
        body, out_shape=jax.ShapeDtypeStruct((N_CHIPS,) + x.shape, x.dtype), in_specs=[_ANY], out_specs=_ANY,
        scratch_shapes=[pltpu.SemaphoreType.DMA((3,)), pltpu.SemaphoreType.DMA((3,)), pltpu.SemaphoreType.DMA],
        name=name)(x)


def _multi_call(name, body, ins, out_shapes, sems, in_place=False):
    return pl.pallas_call(
        body, out_shape=out_shapes, in_specs=[_ANY] * len(ins), out_specs=[_ANY] * len(out_shapes),
        scratch_shapes=sems, input_output_aliases={w: w for w in range(len(ins))} if in_place else {},
        name=name)(*ins)


def comm_gather_weights(name, slots):
    n = len(slots)

    def body(*refs):
        os_ = refs[n:2 * n]
        ssem, rsem, fssem, frsem = refs[2 * n:]
        mx, my, mc = _pos()
        me = 2 * mx + my
        peers = _chip_peers(mx, my)
        sib = (mx, my, 1 - mc)
        half = [o.shape[1] // 2 for o in os_]
        mine = [pl.ds(mc * h, h) for h in half]
        other = [pl.ds((1 - mc) * h, h) for h in half]
        sends = [_rcopy(os_[w].at[me, mine[w]], os_[w].at[me, mine[w]], ssem.at[w, k], rsem.at[w, k], (px, py, mc))
                 for w in range(n) for k, (px, py) in enumerate(peers)]
        for cp in sends:
            cp.start()
        fwds = []
        for w in range(n):
            for k, (px, py) in enumerate(peers):
                landed = os_[w].at[2 * px + py, mine[w]]
                _rcopy(landed, landed, ssem.at[w, k], rsem.at[w, k], (px, py, mc)).wait_recv()
                fw = _rcopy(landed, landed, fssem.at[w, k], frsem.at[w, k], sib)
                fw.start()
                fwds.append(fw)
        for w in range(n):
            for k, (px, py) in enumerate(peers):
                theirs = os_[w].at[2 * px + py, other[w]]
                _rcopy(theirs, theirs, fssem.at[w, k], frsem.at[w, k], sib).wait_recv()
        for cp in sends + fwds:
            cp.wait_send()

    dma = pltpu.SemaphoreType.DMA
    return _multi_call(name, body, slots, [jax.ShapeDtypeStruct(x.shape, x.dtype) for x in slots],
                       [dma((n, 3)), dma((n, 3)), dma((n, 3)), dma((n, 3))], in_place=True)


def comm_sibling_halves(name, gs):
    n = len(gs)

    def body(*refs):
        xs, os_ = refs[:n], refs[n:2 * n]
        ssem, rsem = refs[2 * n:]
        mx, my, mc = _pos()
        sib = (mx, my, 1 - mc)
        sends = []
        for w in range(n):
            h = xs[w].shape[1] // 2
            for j in range(N_CHIPS):
                sends.append(_rcopy(xs[w].at[j, pl.ds((1 - mc) * h, h)], os_[w].at[j], ssem.at[w, j], rsem.at[w, j], sib))
        for cp in sends:
            cp.start()
        for w in range(n):
            for j in range(N_CHIPS):
                _rcopy(os_[w].at[j], os_[w].at[j], ssem.at[w, j], rsem.at[w, j], sib).wait_recv()
        for cp in sends:
            cp.wait_send()

    dma = pltpu.SemaphoreType.DMA
    return _multi_call(name, body, gs,
                       [jax.ShapeDtypeStruct((N_CHIPS, g.shape[1] // 2) + g.shape[2:], g.dtype) for g in gs],
                       [dma((n, N_CHIPS)), dma((n, N_CHIPS))])


def comm_alltoall4(name, xs):
    n = len(xs)

    def body(*refs):
        xr, os_ = refs[:n], refs[n:2 * n]
        ssem, rsem = refs[2 * n:]
        mx, my, mc = _pos()
        me = 2 * mx + my
        peers = _chip_peers(mx, my)
        sends = [_rcopy(xr[w].at[2 * px + py], os_[w].at[me], ssem.at[w, k], rsem.at[w, k], (px, py, mc))
                 for w in range(n) for k, (px, py) in enumerate(peers)]
        for cp in sends:
            cp.start()
        for w in range(n):
            for k, (px, py) in enumerate(peers):
                dst = os_[w].at[2 * px + py]
                _rcopy(dst, dst, ssem.at[w, k], rsem.at[w, k], (px, py, mc)).wait_recv()
        for cp in sends:
            cp.wait_send()

    dma = pltpu.SemaphoreType.DMA
    return _multi_call(name, body, xs, [jax.ShapeDtypeStruct(x.shape, x.dtype) for x in xs], [dma((n, 3)), dma((n, 3))])


def comm_sibling_join(name, bs):
    n = len(bs)

    def body(*refs):
        os_ = refs[n:2 * n]
        ssem, rsem = refs[2 * n:]
        mx, my, mc = _pos()
        sib = (mx, my, 1 - mc)
        sends = [_rcopy(os_[w].at[mc], os_[w].at[mc], ssem.at[w], rsem.at[w], sib) for w in range(n)]
        for cp in sends:
            cp.start()
        for w in range(n):
            dst = os_[w].at[1 - mc]
            _rcopy(dst, dst, ssem.at[w], rsem.at[w], sib).wait_recv()
        for cp in sends:
            cp.wait_send()

    dma = pltpu.SemaphoreType.DMA
    return _multi_call(name, body, bs, [jax.ShapeDtypeStruct(b.shape, b.dtype) for b in bs], [dma((n,)), dma((n,))],
                       in_place=True)


def comm_sibling_swap(name, x):
    def body(x_ref, o_ref, ssem, rsem):
        mx, my, mc = _pos()
        cp = _rcopy(x_ref, o_ref, ssem, rsem, (mx, my, 1 - mc))
        cp.start()
        cp.wait_recv()
        cp.wait_send()

    return pl.pallas_call(
        body, out_shape=jax.ShapeDtypeStruct(x.shape, x.dtype), in_specs=[_ANY], out_specs=_ANY,
        scratch_shapes=[pltpu.SemaphoreType.DMA, pltpu.SemaphoreType.DMA], name=name)(x)


def _row_tile(r):
    return _pick(r, (256, 128, 64, 32, 16, 8))


def add_own_half(name, g, recv, out_dtype):
    r, c = g.shape[2:]
    tr = _row_tile(r)
    mc = lax.axis_index("c").astype(jnp.int32).reshape(1)

    def body(c_ref, g_ref, r_ref, o_ref):
        o_ref[...] = (g_ref[...] + r_ref[...]).astype(o_ref.dtype)

    grid_spec = pltpu.PrefetchScalarGridSpec(
        num_scalar_prefetch=1, grid=(N_CHIPS, r // tr),
        in_specs=[pl.BlockSpec((None, None, tr, c), lambda j, i, cr: (j, cr[0], i, 0)),
                  pl.BlockSpec((None, tr, c), lambda j, i, cr: (j, i, 0))],
        out_specs=pl.BlockSpec((None, tr, c), lambda j, i, cr: (j, i, 0)))
    return pl.pallas_call(body, grid_spec=grid_spec, out_shape=jax.ShapeDtypeStruct(recv.shape, out_dtype),
                          compiler_params=_cparams("parallel", "parallel"), name=name)(mc, g, recv)


def cast_into_slot(name, w, chip, dtype):
    r, c = w.shape
    tr = _row_tile(r)

    def body(c_ref, w_ref, o_ref):
        o_ref[...] = w_ref[...].astype(o_ref.dtype)

    grid_spec = pltpu.PrefetchScalarGridSpec(
        num_scalar_prefetch=1, grid=(r // tr,),
        in_specs=[pl.BlockSpec((tr, c), lambda i, cr: (i, 0))],
        out_specs=pl.BlockSpec((None, tr, c), lambda i, cr: (cr[0], i, 0)))
    return pl.pallas_call(body, grid_spec=grid_spec, out_shape=jax.ShapeDtypeStruct((N_CHIPS, r, c), dtype),
                          compiler_params=_cparams("parallel"), name=name)(chip.astype(jnp.int32).reshape(1), w)


def sum_chips_into_half(name, own, arrived, chip, mc):
    r, c = own.shape[1:]
    tr = _row_tile(r)

    def body(c_ref, own_ref, a_ref, b_ref, d_ref, o_ref):
        o_ref[...] = ((own_ref[...].astype(f32) + a_ref[...].astype(f32))
                      + (b_ref[...].astype(f32) + d_ref[...].astype(f32)))

    slot = lambda flip: pl.BlockSpec((None, tr, c), lambda i, cr, _f=flip: (cr[0] ^ _f, i, 0))
    grid_spec = pltpu.PrefetchScalarGridSpec(
        num_scalar_prefetch=1, grid=(r // tr,), in_specs=[slot(0), slot(2), slot(1), slot(3)],
        out_specs=pl.BlockSpec((None, tr, c), lambda i, cr: (cr[1], i, 0)))
    scal = jnp.stack([chip, mc]).astype(jnp.int32)
    return pl.pallas_call(body, grid_spec=grid_spec, out_shape=jax.ShapeDtypeStruct((2, r, c), f32),
                          compiler_params=_cparams("parallel"), name=name)(scal, own, arrived, arrived, arrived)


def sum_slots(name, x):
    r, c = x.shape[1:]
    tr = _row_tile(r)

    def body(x_ref, o_ref):
        o_ref[...] = (x_ref[0].astype(f32) + x_ref[1].astype(f32)) + (x_ref[2].astype(f32) + x_ref[3].astype(f32))

    return pl.pallas_call(
        body, grid=(r // tr,), in_specs=[pl.BlockSpec((N_CHIPS, tr, c), lambda i: (0, i, 0))],
        out_specs=pl.BlockSpec((tr, c), lambda i: (i, 0)), out_shape=jax.ShapeDtypeStruct((r, c), f32),
        compiler_params=_cparams("parallel"), name=name)(x)


def adamw(name, w, g, m, v):
    r, c = w.shape
    tr = _row_tile(r)

    def body(w_ref, g_ref, m_ref, v_ref, d_ref, nm_ref, nv_ref):
        gv = g_ref[...]
        nm = ADAM_B1 * m_ref[...] + (1.0 - ADAM_B1) * gv
        nv = ADAM_B2 * v_ref[...] + (1.0 - ADAM_B2) * (gv * gv)
        m_hat = nm / (1.0 - ADAM_B1 ** ADAM_STEP)
        v_hat = nv / (1.0 - ADAM_B2 ** ADAM_STEP)
        d_ref[...] = -ADAM_LR * (m_hat / (jnp.sqrt(v_hat) + ADAM_EPS) + ADAM_WD * w_ref[...])
        nm_ref[...] = nm
        nv_ref[...] = nv

    blk = pl.BlockSpec((tr, c), lambda i: (i, 0))
    return pl.pallas_call(
        body, grid=(r // tr,), in_specs=[blk] * 4, out_specs=[blk] * 3,
        out_shape=[jax.ShapeDtypeStruct((r, c), f32)] * 3,
        compiler_params=_cparams("parallel"), name=name)(w, g, m, v)


def _pack(arrs, dtype, row_multiple):
    flat = jnp.concatenate([a.astype(dtype).reshape(-1) for a in arrs])
    unit = PACK_COLS * row_multiple
    total = -(-flat.shape[0] // unit) * unit
    return jnp.pad(flat, (0, total - flat.shape[0])).reshape(-1, PACK_COLS)


def _unpack(packed, shapes):
    flat, out, off = packed.reshape(-1), [], 0
    for shp in shapes:
        n = math.prod(shp)
        out.append(flat[off:off + n].reshape(shp))
        off += n
    return out


def _gathered_to_full(g, axis):
    t = jnp.moveaxis(g, 0, axis)
    return t.reshape(t.shape[:axis] + (t.shape[axis] * t.shape[axis + 1],) + t.shape[axis + 2:])


def _full_to_shard_major(full, axis):
    shp = full.shape
    t = full.reshape(shp[:axis] + (N_CHIPS, shp[axis] // N_CHIPS) + shp[axis + 1:])
    return jnp.moveaxis(t, axis, 0)


GRAD_ROW_MULTIPLE = 256


def kernel(x, mem, positions, hyb_w_in, dn_conv_w, dn_a_log, dn_dt_bias, dn_norm_g, hyb_w_out, s5_a_re, s5_a_im, s5_log_dt, s5_b_re, s5_b_im, s5_c_re, s5_c_im, s5_d, s5_glu_wo, s5_glu_wg, ln_mix_g, ln_mix_b, xq_w, xk_w, xv_w, xo_w, ln_x_g, ln_x_b, ffn_wg, ffn_wu, ffn_wd, ln_ffn_g, ln_ffn_b, loss_target, m_hyb_w_in, m_dn_conv_w, m_dn_a_log, m_dn_dt_bias, m_dn_norm_g, m_hyb_w_out, m_s5_a_re, m_s5_a_im, m_s5_log_dt, m_s5_b_re, m_s5_b_im, m_s5_c_re, m_s5_c_im, m_s5_d, m_s5_glu_wo, m_s5_glu_wg, m_ln_mix_g, m_ln_mix_b, m_xq_w, m_xk_w, m_xv_w, m_xo_w, m_ln_x_g, m_ln_x_b, m_ffn_wg, m_ffn_wu, m_ffn_wd, m_ln_ffn_g, m_ln_ffn_b, v_hyb_w_in, v_dn_conv_w, v_dn_a_log, v_dn_dt_bias, v_dn_norm_g, v_hyb_w_out, v_s5_a_re, v_s5_a_im, v_s5_log_dt, v_s5_b_re, v_s5_b_im, v_s5_c_re, v_s5_c_im, v_s5_d, v_s5_glu_wo, v_s5_glu_wg, v_ln_mix_g, v_ln_mix_b, v_xq_w, v_xk_w, v_xv_w, v_xo_w, v_ln_x_g, v_ln_x_b, v_ffn_wg, v_ffn_wu, v_ffn_wd, v_ln_ffn_g, v_ln_ffn_b):
    a = dict(locals())
    big = [n for n in WEIGHT_NAMES if n in SHARD_AXIS and n not in GATHER_F32]
    small = [n for n in WEIGHT_NAMES if n not in big]
    chip = 2 * lax.axis_index("x") + lax.axis_index("y")

    mc = lax.axis_index("c")
    view2 = lambda t: t.reshape(-1, t.shape[-1])
    slots = [cast_into_slot("slot_" + n, view2(a[n]), chip, bf16).reshape((N_CHIPS,) + a[n].shape) for n in big]
    gathered = comm_gather_weights("gather_w", slots)
    tiny4 = _unpack_slots(comm_allgather4("gather_w_tiny", _pack([a[n] for n in GATHER_F32], f32, 8)),
                          [a[n].shape for n in GATHER_F32])
    p = {n: a[n] for n in small if n not in GATHER_F32}
    for n, g4 in zip(GATHER_F32, tiny4):
        p[n] = _gathered_to_full(g4, SHARD_AXIS[n])
    for n, g4 in zip(big, gathered):
        p[n] = g4 if n in SHARD_ORDER_GRADS else _gathered_to_full(g4, SHARD_AXIS[n])

    loss, grad_x, grads = local_step(x[0], mem[0], positions, loss_target[0], p)
    loss = lax.psum(loss, ("x", "y", "c"))

    g4s = [grads[n] if n in SHARD_ORDER_GRADS else _full_to_shard_major(grads[n], SHARD_AXIS[n]) for n in big]
    recv = comm_sibling_halves("rs_sibling_halves", g4s)
    pairs = []
    for n, g4, r4 in zip(big, g4s, recv):
        lh, cols = g4.shape[1] // 2, g4.shape[-1]
        v4 = g4.reshape(N_CHIPS, 2, -1, cols)
        pairs.append(add_own_half("rs_add_" + n, v4, r4.reshape(N_CHIPS, -1, cols), bf16).reshape((N_CHIPS, lh) + g4.shape[2:]))
    arrived = comm_alltoall4("rs_alltoall", pairs)
    slot3 = lambda t: t.reshape(N_CHIPS, -1, t.shape[-1])
    halves = [sum_chips_into_half("rs_sum_" + n, slot3(pr), slot3(ar), chip, mc) for n, pr, ar in zip(big, pairs, arrived)]
    g_big = {n: t.reshape(a[n].shape) for n, t in zip(big, comm_sibling_join("rs_sibling_join", halves))}

    rpack = _pack([grads[n] for n in small], f32, 8)
    rpair = _add2("ar_add_sibling", rpack, comm_sibling_swap("ar_sibling_swap", rpack))
    g_small = _unpack(sum_slots("ar_sum_chips", comm_allgather4("ar_allgather", rpair)), [grads[n].shape for n in small])
    g_small = {n: (lax.dynamic_index_in_dim(_full_to_shard_major(g, SHARD_AXIS[n]), chip, 0, keepdims=False)
                   if n in SHARD_AXIS else g) for n, g in zip(small, g_small)}

    outs = {}
    for n in big:
        view = lambda t: t.reshape(-1, t.shape[-1])
        d, nm, nv = adamw("adamw_" + n, view(a[n]), view(g_big[n]), view(a["m_" + n]), view(a["v_" + n]))
        outs[n] = (g_big[n],) + tuple(t.reshape(a[n].shape) for t in (d, nm, nv))
    shapes = [a[n].shape for n in small]
    packs = [_pack([a[pre + n] for n in small], f32, 8) for pre in ("", "m_", "v_")]
    upd = adamw("adamw_small", packs[0], _pack([g_small[n] for n in small], f32, 8), packs[1], packs[2])
    for k, n in enumerate(small):
        outs[n] = (g_small[n],) + tuple(_unpack(buf, shapes)[k] for buf in upd)
    res = [loss, grad_x[None]]
    for kind in range(4):
        res += [outs[n][kind] for n in WEIGHT_NAMES]
    return tuple(res)


def _unpack_slots(gathered, shapes):
    flat, out, off = gathered.reshape(N_CHIPS, -1), [], 0
    for shp in shapes:
        n = math.prod(shp)
        out.append(flat[:, off:off + n].reshape((N_CHIPS,) + tuple(shp)))
        off += n
    return out
```

```python
import functools
import math

import jax
import jax.numpy as jnp
from jax import lax
from jax.experimental import pallas as pl
from jax.experimental.pallas import tpu as pltpu

f32 = jnp.float32
bf16 = jnp.bfloat16

D_MODEL = 1024
DEPTH = 4
DN_HEADS = 4
DN_HEAD_DIM = 128
DN_KEY_DIM = 512
DN_QKV_DIM = 1536
DN_CONV = 4
SW_HEADS = 8
SW_HEAD_DIM = 64
SW_DIM = 512
SW_DILATIONS = (1, 4, 16)
SW_BLOCK = 128
ROPE_THETA = 10000.0
S5_GROUP = 16
S5_GROUPS = 64
S5_STATE = 64
X_HEADS = 4
X_HEAD_DIM = 256
FFN_HIDDEN = 2816
ALPHA = (2 * DEPTH) ** 0.25
LN_EPS = 1e-5
RMS_EPS = 1e-6
ADAM_LR, ADAM_B1, ADAM_B2, ADAM_EPS, ADAM_WD, ADAM_STEP = 0.001, 0.9, 0.999, 1e-08, 0.01, 10

BA_PAD = 256
PROJ_COLS = DN_QKV_DIM + DN_KEY_DIM + 3 * SW_DIM + BA_PAD
COL_Z = DN_QKV_DIM
COL_SWQ = COL_Z + DN_KEY_DIM
COL_SWK = COL_SWQ + SW_DIM
COL_SWV = COL_SWK + SW_DIM
COL_BA = COL_SWV + SW_DIM

LANES = 128
SUBLANES = 8
VMEM_LIMIT = 56 * 1024 * 1024
DN_CHUNK = 128
DN_HEADS_PER_STEP = 4


def _cparams(*sem):
    return pltpu.CompilerParams(dimension_semantics=tuple(sem), vmem_limit_bytes=VMEM_LIMIT)


def _dg(x, y, cx, cy):
    return lax.dot_general(x, y, (((cx,), (cy,)), ((), ())), preferred_element_type=f32)


@functools.partial(jax.custom_vjp, nondiff_argnums=(2, 3))
def bdot(a, b, ca, cb):
    return _dg(a.astype(bf16), b.astype(bf16), ca, cb)


def _bdot_fwd(a, b, ca, cb):
    return bdot(a, b, ca, cb), (a, b)


def _bdot_bwd(ca, cb, res, g):
    a, b = res
    g16, a16, b16 = g.astype(bf16), a.astype(bf16), b.astype(bf16)
    da = _dg(g16, b16, 1, 1 - cb) if ca == 1 else _dg(b16, g16, 1 - cb, 1)
    db = _dg(a16, g16, 1 - ca, 0) if cb == 0 else _dg(g16, a16, 0, 1 - ca)
    return da.astype(a.dtype), db.astype(b.dtype)


bdot.defvjp(_bdot_fwd, _bdot_bwd)


def _split_hi_lo(a):
    hi = a.astype(bf16)
    return hi, (a - hi.astype(f32)).astype(bf16)


def _dot3(a, b, ca, cb):
    a_hi, a_lo = _split_hi_lo(a)
    b_hi, b_lo = _split_hi_lo(b)
    return _dg(a_hi, b_hi, ca, cb) + (_dg(a_hi, b_lo, ca, cb) + _dg(a_lo, b_hi, ca, cb))


@jax.custom_vjp
def hdot3(a, b):
    return _dot3(a, b, 1, 0)


def _hdot3_fwd(a, b):
    return hdot3(a, b), (a, b)


def _hdot3_bwd(res, g):
    a, b = res
    return _dot3(g, b, 1, 1), _dot3(a, g, 0, 0)


hdot3.defvjp(_hdot3_fwd, _hdot3_bwd)


def hdot(a, b):
    return jnp.dot(a, b, precision=lax.Precision.HIGHEST, preferred_element_type=f32)


def _iota2(shape, dim):
    return lax.broadcasted_iota(jnp.int32, shape, dim)


def _row_spec(r, tm):
    if isinstance(r, tuple):
        arr, width, blk = r
        return arr, pl.BlockSpec((tm, width), lambda i, _b=blk: (i, _b))
    return r, pl.BlockSpec((tm, r.shape[1]), lambda i: (i, 0))


def _par_spec(p):
    return pl.BlockSpec(p.shape, lambda i, _n=p.ndim: (0,) * _n)


def rowmap(name, fn, rows, params, out_cols, tm, out_dtypes=None):
    arrs, specs = zip(*[_row_spec(r, tm) for r in rows])
    s = arrs[0].shape[0]
    n_in = len(rows) + len(params)
    out_dtypes = out_dtypes or [f32] * len(out_cols)

    def body(*refs):
        outs = fn(*[r[...] for r in refs[:n_in]])
        for o_ref, o in zip(refs[n_in:], outs):
            o_ref[...] = o.astype(o_ref.dtype)

    return pl.pallas_call(
        body, grid=(s // tm,),
        in_specs=list(specs) + [_par_spec(p) for p in params],
        out_specs=[pl.BlockSpec((tm, c), lambda i: (i, 0)) for c in out_cols],
        out_shape=[jax.ShapeDtypeStruct((s, c), dt) for c, dt in zip(out_cols, out_dtypes)],
        compiler_params=_cparams("parallel"), name=name)(*arrs, *params)


def rowmap_bwd(name, fn, rows, params, cts, tm, row_mask=None, par_mask=None):
    arrs, specs = zip(*[_row_spec(r, tm) for r in rows])
    s = arrs[0].shape[0]
    ct_groups = [c if isinstance(c, list) else [c] for c in cts]
    ct_arrs, ct_specs = zip(*[_row_spec(a, tm) for grp in ct_groups for a in grp])
    cts = list(ct_arrs)
    nr, npar, nct = len(rows), len(params), len(cts)
    row_mask = row_mask or [True] * nr
    par_mask = par_mask or [True] * npar
    row_idx = [k for k in range(nr) if row_mask[k]]
    par_idx = [k for k in range(npar) if par_mask[k]]
    row_w = [specs[k].block_shape[1] for k in row_idx]

    def body(*refs):
        ins = [r[...] for r in refs[:nr + npar]]
        ct_refs = list(refs[nr + npar:nr + npar + nct])
        ctv = []
        for grp in ct_groups:
            acc = ct_refs.pop(0)[...]
            for _ in grp[1:]:
                acc = acc + ct_refs.pop(0)[...]
            ctv.append(acc)
        ctv = tuple(ctv)
        outs = refs[nr + npar + nct:]
        _, vjp = jax.vjp(fn, *ins)
        grads = vjp(ctv)
        for o_ref, k in zip(outs[:len(row_idx)], row_idx):
            o_ref[...] = grads[k].astype(o_ref.dtype)
        first = pl.program_id(0) == 0
        for o_ref, k in zip(outs[len(row_idx):], par_idx):
            g = grads[nr + k].astype(f32)

            @pl.when(first)
            def _(o_ref=o_ref, g=g):
                o_ref[...] = g

            @pl.when(jnp.logical_not(first))
            def _(o_ref=o_ref, g=g):
                o_ref[...] += g

    res = pl.pallas_call(
        body, grid=(s // tm,),
        in_specs=list(specs) + [_par_spec(p) for p in params]
        + list(ct_specs),
        out_specs=[pl.BlockSpec((tm, w), lambda i: (i, 0)) for w in row_w]
        + [_par_spec(params[k]) for k in par_idx],
        out_shape=[jax.ShapeDtypeStruct((s, w), f32) for w in row_w]
        + [jax.ShapeDtypeStruct(params[k].shape, f32) for k in par_idx],
        compiler_params=_cparams("arbitrary"), name=name)(*arrs, *params, *cts)
    return list(res[:len(row_idx)]), list(res[len(row_idx):])


def _pick(n, prefs):
    for t in prefs:
        if n % t == 0:
            return t
    return n


def mm_nn(name, a, b, out_dtype=f32):
    m, k = a.shape
    n = b.shape[1]
    tm, tn = _pick(m, (512, 256, 128)), _pick(n, (512, 256, 128))

    def body(a_ref, b_ref, o_ref):
        o_ref[...] = _dg(a_ref[...].astype(bf16), b_ref[...].astype(bf16), 1, 0).astype(o_ref.dtype)

    return pl.pallas_call(
        body, grid=(m // tm, n // tn),
        in_specs=[pl.BlockSpec((tm, k), lambda i, j: (i, 0)), pl.BlockSpec((k, tn), lambda i, j: (0, j))],
        out_specs=pl.BlockSpec((tm, tn), lambda i, j: (i, j)),
        out_shape=jax.ShapeDtypeStruct((m, n), out_dtype),
        compiler_params=_cparams("parallel", "parallel"), name=name)(a, b)


def mm_nt(name, a, b, out_dtype=f32):
    m, n = a.shape
    k = b.shape[0]
    tm, tk = _pick(m, (256, 128)), _pick(k, (512, 256, 128))

    def body(a_ref, b_ref, o_ref):
        o_ref[...] = _dg(a_ref[...].astype(bf16), b_ref[...].astype(bf16), 1, 1).astype(o_ref.dtype)

    return pl.pallas_call(
        body, grid=(m // tm, k // tk),
        in_specs=[pl.BlockSpec((tm, n), lambda i, j: (i, 0)), pl.BlockSpec((tk, n), lambda i, j: (j, 0))],
        out_specs=pl.BlockSpec((tm, tk), lambda i, j: (i, j)),
        out_shape=jax.ShapeDtypeStruct((m, k), out_dtype),
        compiler_params=_cparams("parallel", "parallel"), name=name)(a, b)


def mm_tn(name, a, b, out_dtype=f32):
    s, m = a.shape
    n = b.shape[1]
    tm, tn = _pick(m, (256, 128)), _pick(n, (256, 128))

    def body(a_ref, b_ref, o_ref):
        o_ref[...] = _dg(a_ref[...].astype(bf16), b_ref[...].astype(bf16), 0, 0).astype(o_ref.dtype)

    return pl.pallas_call(
        body, grid=(m // tm, n // tn),
        in_specs=[pl.BlockSpec((s, tm), lambda i, j: (0, i)), pl.BlockSpec((s, tn), lambda i, j: (0, j))],
        out_specs=pl.BlockSpec((tm, tn), lambda i, j: (i, j)),
        out_shape=jax.ShapeDtypeStruct((m, n), out_dtype),
        compiler_params=_cparams("parallel", "parallel"), name=name)(a, b)


def _postnorm_tile(h, sub, g, b):
    z = ALPHA * h + sub
    mu = jnp.mean(z, -1, keepdims=True)
    zc = z - mu
    var = jnp.mean(zc * zc, -1, keepdims=True)
    return (zc * lax.rsqrt(var + LN_EPS) * g + b,)


def _swiglu_tile(au):
    a, u = au[:, :FFN_HIDDEN], au[:, FFN_HIDDEN:]
    return (jax.nn.silu(a) * u,)


def _glu_tile(og):
    o, g = og[:, :D_MODEL], og[:, D_MODEL:]
    return (o * jax.nn.sigmoid(g),)


def _xattn_tile(q, kv):
    outs = []
    for h in range(X_HEADS):
        sl = slice(h * X_HEAD_DIM, (h + 1) * X_HEAD_DIM)
        s = bdot(q[:, sl], kv[:, sl], 1, 1) * (X_HEAD_DIM ** -0.5)
        m = lax.stop_gradient(jnp.max(s, -1, keepdims=True))
        p = jnp.exp(s - m)
        p = p / jnp.sum(p, -1, keepdims=True)
        outs.append(bdot(p, kv[:, D_MODEL + h * X_HEAD_DIM:D_MODEL + (h + 1) * X_HEAD_DIM], 1, 0))
    return (jnp.concatenate(outs, -1),)


TM_ROW = 256


def postnorm_fwd(tag, h, sub, g, b):
    return rowmap("postnorm_" + tag, _postnorm_tile, [h, sub], [g, b], [D_MODEL], TM_ROW)[0]


def postnorm_bwd(tag, h, sub, g, b, dy):
    (dh, dsub), (dg, db) = rowmap_bwd("postnorm_bwd_" + tag, _postnorm_tile, [h, sub], [g, b], [dy], TM_ROW)
    return dh, dsub, dg, db


def xattn_fwd(tag, h, mem, wq, wkv, wo):
    q = mm_nn("xq_" + tag, h, wq)
    kv = mm_nn("xkv_" + tag, mem, wkv)
    ao = rowmap("xattn_" + tag, _xattn_tile, [q], [kv], [D_MODEL], TM_ROW)[0]
    out = mm_nn("xo_" + tag, ao, wo)
    return out, (q, kv, ao)


def xattn_bwd(tag, h, mem, wq, wkv, wo, res, dout):
    q, kv, ao = res
    dwo = mm_tn("xo_dw_" + tag, ao, dout)
    dao = mm_nt("xo_dx_" + tag, dout, wo)
    (dq,), (dkv,) = rowmap_bwd("xattn_bwd_" + tag, _xattn_tile, [q], [kv], [dao], TM_ROW)
    dwq = mm_tn("xq_dw_" + tag, h, dq)
    dh = mm_nt("xq_dx_" + tag, dq, wq)
    dwkv = mm_tn("xkv_dw_" + tag, mem, dkv)
    return dh, dwq, dwkv, dwo


FFN_SHARD = FFN_HIDDEN // 4
TM_FFN = 512


def _silu_mul(a, u):
    return jax.nn.silu(a) * u


def ffn_fwd(tag, layer, h, wg, wu, wd):
    s = h.shape[0]
    tm, fs = TM_FFN, FFN_SHARD
    w_in = pl.BlockSpec((None, None, D_MODEL, fs), lambda i, k: (k, layer, 0, 0))
    act = pl.BlockSpec((None, tm, fs), lambda i, k: (k, i, 0))

    def up_body(h_ref, wg_ref, wu_ref, a_ref, u_ref, hid_ref):
        hv = h_ref[...].astype(bf16)
        a, u = _dg(hv, wg_ref[...], 1, 0), _dg(hv, wu_ref[...], 1, 0)
        a_ref[...], u_ref[...] = a, u
        hid_ref[...] = _silu_mul(a, u).astype(bf16)

    a4, u4, hid4 = pl.pallas_call(
        up_body, grid=(s // tm, 4),
        in_specs=[pl.BlockSpec((tm, D_MODEL), lambda i, k: (i, 0)), w_in, w_in],
        out_specs=[act, act, act],
        out_shape=[jax.ShapeDtypeStruct((4, s, fs), f32)] * 2 + [jax.ShapeDtypeStruct((4, s, fs), bf16)],
        compiler_params=_cparams("parallel", "parallel"), name="ffn_up_" + tag)(h, wg, wu)

    def down_body(hid_ref, wd_ref, o_ref):
        part = _dg(hid_ref[...], wd_ref[...], 1, 0)

        @pl.when(pl.program_id(1) == 0)
        def _():
            o_ref[...] = part

        @pl.when(pl.program_id(1) != 0)
        def _():
            o_ref[...] += part

    out = pl.pallas_call(
        down_body, grid=(s // tm, 4),
        in_specs=[act, pl.BlockSpec((None, None, fs, D_MODEL), lambda i, k: (k, layer, 0, 0))],
        out_specs=pl.BlockSpec((tm, D_MODEL), lambda i, k: (i, 0)),
        out_shape=jax.ShapeDtypeStruct((s, D_MODEL), f32),
        compiler_params=_cparams("parallel", "arbitrary"), name="ffn_down_" + tag)(hid4, wd)
    return out, (a4, u4, hid4)


def ffn_bwd(tag, layer, h, wg, wu, wd, res, dout):
    a4, u4, hid4 = res
    s = h.shape[0]
    tm, fs = TM_FFN, FFN_SHARD
    act = pl.BlockSpec((None, tm, fs), lambda i, k: (k, i, 0))
    w_in = pl.BlockSpec((None, None, D_MODEL, fs), lambda i, k: (k, layer, 0, 0))

    def dact_body(do_ref, wd_ref, a_ref, u_ref, da_ref, du_ref):
        dhid = _dg(do_ref[...].astype(bf16), wd_ref[...], 1, 1)
        _, vjp = jax.vjp(_silu_mul, a_ref[...], u_ref[...])
        da, du = vjp(dhid)
        da_ref[...], du_ref[...] = da.astype(bf16), du.astype(bf16)

    da4, du4 = pl.pallas_call(
        dact_body, grid=(s // tm, 4),
        in_specs=[pl.BlockSpec((tm, D_MODEL), lambda i, k: (i, 0)),
                  pl.BlockSpec((None, None, fs, D_MODEL), lambda i, k: (k, layer, 0, 0)), act, act],
        out_specs=[act, act], out_shape=[jax.ShapeDtypeStruct((4, s, fs), bf16)] * 2,
        compiler_params=_cparams("parallel", "parallel"), name="ffn_dact_" + tag)(dout, wd, a4, u4)

    def dx_body(da_ref, du_ref, wg_ref, wu_ref, o_ref):
        part = _dg(da_ref[...], wg_ref[...], 1, 1) + _dg(du_ref[...], wu_ref[...], 1, 1)

        @pl.when(pl.program_id(1) == 0)
        def _():
            o_ref[...] = part

        @pl.when(pl.program_id(1) != 0)
        def _():
            o_ref[...] += part

    dh = pl.pallas_call(
        dx_body, grid=(s // tm, 4), in_specs=[act, act, w_in, w_in],
        out_specs=pl.BlockSpec((tm, D_MODEL), lambda i, k: (i, 0)),
        out_shape=jax.ShapeDtypeStruct((s, D_MODEL), f32),
        compiler_params=_cparams("parallel", "arbitrary"), name="ffn_dx_" + tag)(da4, du4, wg, wu)

    tn = 256
    whole = pl.BlockSpec((None, s, fs), lambda k, j: (k, 0, 0))

    def dwin_body(h_ref, da_ref, du_ref, dwg_ref, dwu_ref):
        hv = h_ref[...].astype(bf16)
        dwg_ref[...] = _dg(hv, da_ref[...], 0, 0)
        dwu_ref[...] = _dg(hv, du_ref[...], 0, 0)

    dwg, dwu = pl.pallas_call(
        dwin_body, grid=(4, D_MODEL // tn),
        in_specs=[pl.BlockSpec((s, tn), lambda k, j: (0, j)), whole, whole],
        out_specs=[pl.BlockSpec((None, tn, fs), lambda k, j: (k, j, 0))] * 2,
        out_shape=[jax.ShapeDtypeStruct((4, D_MODEL, fs), f32)] * 2,
        compiler_params=_cparams("parallel", "parallel"), name="ffn_dwin_" + tag)(h, da4, du4)

    def dwd_body(hid_ref, do_ref, dwd_ref):
        dwd_ref[...] = _dg(hid_ref[...], do_ref[...].astype(bf16), 0, 0)

    dwd = pl.pallas_call(
        dwd_body, grid=(4, D_MODEL // tn),
        in_specs=[whole, pl.BlockSpec((s, tn), lambda k, j: (0, j))],
        out_specs=pl.BlockSpec((None, fs, tn), lambda k, j: (k, 0, j)),
        out_shape=jax.ShapeDtypeStruct((4, fs, D_MODEL), f32),
        compiler_params=_cparams("parallel", "parallel"), name="ffn_dwd_" + tag)(hid4, dout)
    return dh, dwg, dwu, dwd


def loss_head(y, target):
    s, d = y.shape
    tm = TM_ROW

    def body(y_ref, t_ref, part_ref, dy_ref):
        e = y_ref[...] - t_ref[...]
        dy_ref[...] = e * (1.0 / d)
        p = jnp.sum(e * e, 0, keepdims=True) * (0.5 / d)

        @pl.when(pl.program_id(0) == 0)
        def _():
            part_ref[...] = p

        @pl.when(pl.program_id(0) != 0)
        def _():
            part_ref[...] += p

    return pl.pallas_call(
        body, grid=(s // tm,),
        in_specs=[pl.BlockSpec((tm, d), lambda i: (i, 0))] * 2,
        out_specs=[pl.BlockSpec((1, d), lambda i: (0, 0)), pl.BlockSpec((tm, d), lambda i: (i, 0))],
        out_shape=[jax.ShapeDtypeStruct((1, d), f32), jax.ShapeDtypeStruct((s, d), f32)],
        compiler_params=_cparams("arbitrary"), name="loss_head")(y, target)


TM_CONV = 512


def _conv_rows(xx, w_ref, n_rows):
    a = w_ref[3:4, :] * xx
    for k in (1, 2, 3):
        a = a + w_ref[3 - k:4 - k, :] * pltpu.roll(xx, k, 0)
    return a


def _dn_act(a, is_qk):
    s = jax.nn.silu(a)
    n = s * lax.rsqrt(jnp.sum(s * s, -1, keepdims=True) + RMS_EPS)
    return jnp.where(is_qk, n, s)


def dn_conv_fwd(tag, proj, cw):
    s = proj.shape[0]
    tm, hb = TM_CONV, TM_CONV // SUBLANES

    def body(xh_ref, x_ref, w_ref, o_ref):
        j, t = pl.program_id(0), pl.program_id(1)
        halo = jnp.where(t > 0, xh_ref[...], 0.0)
        xx = jnp.concatenate([halo, x_ref[...]], 0)
        a = _conv_rows(xx, w_ref, tm + SUBLANES)
        o_ref[...] = _dn_act(a, j < 2 * DN_HEADS)[SUBLANES:, :]

    return pl.pallas_call(
        body, grid=(DN_QKV_DIM // LANES, s // tm),
        in_specs=[pl.BlockSpec((SUBLANES, LANES), lambda j, t: (jnp.maximum(t * hb - 1, 0), j)),
                  pl.BlockSpec((tm, LANES), lambda j, t: (t, j)),
                  pl.BlockSpec((DN_CONV, LANES), lambda j, t: (0, j))],
        out_specs=pl.BlockSpec((tm, LANES), lambda j, t: (t, j)),
        out_shape=jax.ShapeDtypeStruct((s, DN_QKV_DIM), f32),
        compiler_params=_cparams("parallel", "parallel"), name="dn_conv_" + tag)(proj, proj, cw)


def dn_conv_bwd(tag, proj, cw, dy):
    s = proj.shape[0]
    tm, hb = TM_CONV, TM_CONV // SUBLANES
    nt = s // tm
    n_ext = tm + 2 * SUBLANES

    def body(xb_ref, x_ref, xa_ref, dy_ref, dya_ref, w_ref, dx_ref, dw_ref):
        j, t = pl.program_id(0), pl.program_id(1)
        xx = jnp.concatenate([jnp.where(t > 0, xb_ref[...], 0.0), x_ref[...],
                              jnp.where(t < nt - 1, xa_ref[...], 0.0)], 0)
        dyy = jnp.concatenate([jnp.zeros((SUBLANES, LANES), f32), dy_ref[...],
                               jnp.where(t < nt - 1, dya_ref[...], 0.0)], 0)
        a = _conv_rows(xx, w_ref, n_ext)
        _, vjp = jax.vjp(lambda v: _dn_act(v, j < 2 * DN_HEADS), a)
        da, = vjp(dyy)
        dx = w_ref[3:4, :] * da
        for k in (1, 2, 3):
            dx = dx + w_ref[3 - k:4 - k, :] * pltpu.roll(da, n_ext - k, 0)
        dx_ref[...] = dx[SUBLANES:SUBLANES + tm, :]
        row = _iota2((n_ext, LANES), 0)
        da_in = jnp.where((row >= SUBLANES) & (row < SUBLANES + tm), da, 0.0)
        r8 = _iota2((SUBLANES, LANES), 0)
        dw = jnp.zeros((SUBLANES, LANES), f32)
        for k in range(DN_CONV):
            xs = xx if k == 0 else pltpu.roll(xx, k, 0)
            dw = dw + jnp.where(r8 == 3 - k, jnp.sum(da_in * xs, 0, keepdims=True), 0.0)

        @pl.when(t == 0)
        def _():
            dw_ref[...] = dw

        @pl.when(t != 0)
        def _():
            dw_ref[...] += dw

    nb8 = s // SUBLANES
    return pl.pallas_call(
        body, grid=(DN_QKV_DIM // LANES, nt),
        in_specs=[pl.BlockSpec((SUBLANES, LANES), lambda j, t: (jnp.maximum(t * hb - 1, 0), j)),
                  pl.BlockSpec((tm, LANES), lambda j, t: (t, j)),
                  pl.BlockSpec((SUBLANES, LANES), lambda j, t: (jnp.minimum((t + 1) * hb, nb8 - 1), j)),
                  pl.BlockSpec((tm, LANES), lambda j, t: (t, j)),
                  pl.BlockSpec((SUBLANES, LANES), lambda j, t: (jnp.minimum((t + 1) * hb, nb8 - 1), j)),
                  pl.BlockSpec((DN_CONV, LANES), lambda j, t: (0, j))],
        out_specs=[pl.BlockSpec((tm, LANES), lambda j, t: (t, j)),
                   pl.BlockSpec((SUBLANES, LANES), lambda j, t: (0, j))],
        out_shape=[jax.ShapeDtypeStruct((s, DN_QKV_DIM), f32), jax.ShapeDtypeStruct((SUBLANES, DN_QKV_DIM), f32)],
        compiler_params=_cparams("parallel", "arbitrary"), name="dn_conv_bwd_" + tag)(proj, proj, proj, dy, dy, cw)


def _gate_tile(ba, eb, ea, alog, dtb):
    beta = jax.nn.sigmoid(hdot(ba, eb))
    g = -jnp.exp(alog) * jax.nn.softplus(hdot(ba, ea) + dtb)
    return beta, g


def _each(fn, *lists):
    return [fn(*args) for args in zip(*lists)]


@functools.partial(jax.custom_vjp, nondiff_argnums=(1,))
def _halves(x, axis):
    h = x.shape[axis] // 2
    return (x[:h], x[h:]) if axis == 0 else (x[:, :h], x[:, h:])


def _halves_fwd(x, axis):
    return _halves(x, axis), None


def _halves_bwd(axis, _, g):
    return (jnp.concatenate(g, axis),)


_halves.defvjp(_halves_fwd, _halves_bwd)


def _tri_inv_unit(lowers):
    c = lowers[0].shape[0]
    r, col = _iota2((c, c), 0), _iota2((c, c), 1)
    eye = jnp.where(r == col, 1.0, 0.0).astype(f32)
    invs = None
    sh = 0
    while (1 << sh) < c:
        same_2b = lax.shift_right_logical(r, sh + 1) == lax.shift_right_logical(col, sh + 1)
        diff_b = lax.shift_right_logical(r, sh) != lax.shift_right_logical(col, sh)
        offs = [jnp.where(same_2b & diff_b, low, 0.0) for low in lowers]
        if invs is None:
            invs = [eye - off for off in offs]
        else:
            part = _each(hdot, invs, offs)
            invs = _each(lambda inv, p: inv - hdot(p, inv), invs, part)
        sh += 1
    return invs


def _delta_chunk(q, k, v, gb, betab, state):
    c, hd = DN_CHUNK, DN_HEAD_DIM
    r, col = _iota2((c, c), 0), _iota2((c, c), 1)
    causal, strict = r >= col, r > col
    tril = jnp.where(causal, 1.0, 0.0).astype(f32)
    gc = _each(lambda g: hdot(tril, g), gb)
    decay = _each(lambda g: jnp.where(causal, jnp.exp(jnp.where(causal, g - g.T, 0.0)), 0.0), gc)
    qs = _each(lambda t: t * (DN_HEAD_DIM ** -0.5), q)
    kb = _each(lambda a, b: a * b, k, betab)
    kq = _each(lambda a, b, kk: _halves(bdot(jnp.concatenate([a, b], 0), kk, 1, 1), 0), kb, qs, k)
    lower = _each(lambda x, d: jnp.where(strict, x[0], 0.0) * d, kq, decay)
    intra = _each(lambda x, d: x[1] * d, kq, decay)
    tinv = _tri_inv_unit(lower)
    eg = _each(jnp.exp, gc)
    uw = _each(lambda t, vv, b, kb_, e: _halves(hdot(t, jnp.concatenate([vv * b, kb_ * e], 1)), 1),
               tinv, v, betab, kb, eg)
    gl = _each(lambda g: jnp.sum(jnp.where(r == c - 1, g, 0.0), 0, keepdims=True), gc)
    k_dec = _each(lambda kk, a, g: kk * jnp.exp(a - g), k, gl, gc)
    ws = _each(lambda x, t, e, st: _halves(bdot(jnp.concatenate([x[1], t * e], 0), st, 1, 0), 0), uw, qs, eg, state)
    v_new = _each(lambda x, y: x[0] - y[0], uw, ws)
    out = _each(lambda y, a, vn: y[1] + bdot(a, vn, 1, 0), ws, intra, v_new)
    new_state = _each(lambda st, a, kd, vn: st * jnp.exp(a) + bdot(kd, vn, 0, 0), state, gl, k_dec, v_new)
    return tuple(out), tuple(new_state)


def delta_fwd(tag, qkv, gb, betab):
    s = qkv.shape[0]
    c, hd = DN_CHUNK, DN_HEAD_DIM
    n = s // c

    hg, ng = DN_HEADS_PER_STEP, DN_HEADS // DN_HEADS_PER_STEP

    def body(q_ref, k_ref, v_ref, g_ref, b_ref, o_ref, st_ref, state):
        @pl.when(pl.program_id(1) == 0)
        def _():
            state[...] = jnp.zeros_like(state)

        heads = lambda ref: tuple(ref[:, j * hd:(j + 1) * hd] for j in range(hg))
        st = tuple(state[j] for j in range(hg))
        outs, news = _delta_chunk(heads(q_ref), heads(k_ref), heads(v_ref), heads(g_ref), heads(b_ref), st)
        for j in range(hg):
            st_ref[j] = st[j]
            o_ref[:, j * hd:(j + 1) * hd] = outs[j]
            state[j] = news[j]

    blk = lambda off: pl.BlockSpec((c, hg * hd), lambda h, i, _o=off: (i, h + _o))
    return pl.pallas_call(
        body, grid=(ng, n),
        in_specs=[blk(0), blk(ng), blk(2 * ng), blk(0), blk(0)],
        out_specs=[blk(0), pl.BlockSpec((hg, None, hd, hd), lambda h, i: (h, i, 0, 0))],
        out_shape=[jax.ShapeDtypeStruct((s, DN_KEY_DIM), f32), jax.ShapeDtypeStruct((DN_HEADS, n, hd, hd), f32)],
        scratch_shapes=[pltpu.VMEM((hg, hd, hd), f32)],
        compiler_params=_cparams("parallel", "arbitrary"), name="delta_" + tag)(qkv, qkv, qkv, gb, betab)


def delta_bwd(tag, qkv, gb, betab, states, do):
    s = qkv.shape[0]
    c, hd = DN_CHUNK, DN_HEAD_DIM
    n = s // c

    hg, ng = DN_HEADS_PER_STEP, DN_HEADS // DN_HEADS_PER_STEP

    def body(q_ref, k_ref, v_ref, g_ref, b_ref, st_ref, do_ref, dq_ref, dk_ref, dv_ref, dg_ref, db_ref, dstate):
        @pl.when(pl.program_id(1) == 0)
        def _():
            dstate[...] = jnp.zeros_like(dstate)

        heads = lambda ref: tuple(ref[:, j * hd:(j + 1) * hd] for j in range(hg))
        _, vjp = jax.vjp(_delta_chunk, heads(q_ref), heads(k_ref), heads(v_ref), heads(g_ref), heads(b_ref),
                         tuple(st_ref[j] for j in range(hg)))
        grads = vjp((heads(do_ref), tuple(dstate[j] for j in range(hg))))
        for ref, g in zip((dq_ref, dk_ref, dv_ref, dg_ref, db_ref), grads[:5]):
            for j in range(hg):
                ref[:, j * hd:(j + 1) * hd] = g[j]
        for j in range(hg):
            dstate[j] = grads[5][j]

    blk = lambda off: pl.BlockSpec((c, hg * hd), lambda h, i, _o=off: (n - 1 - i, h + _o))
    return pl.pallas_call(
        body, grid=(ng, n),
        in_specs=[blk(0), blk(ng), blk(2 * ng), blk(0), blk(0),
                  pl.BlockSpec((hg, None, hd, hd), lambda h, i: (h, n - 1 - i, 0, 0)), blk(0)],
        out_specs=[blk(0)] * 5,
        out_shape=[jax.ShapeDtypeStruct((s, DN_KEY_DIM), f32)] * 5,
        scratch_shapes=[pltpu.VMEM((hg, hd, hd), f32)],
        compiler_params=_cparams("parallel", "arbitrary"), name="delta_bwd_" + tag)(qkv, qkv, qkv, gb, betab, states, do)


def _dn_out_tile(o, z, ng):
    outs = []
    for h in range(DN_HEADS):
        sl = slice(h * DN_HEAD_DIM, (h + 1) * DN_HEAD_DIM)
        oh = o[:, sl]
        nrm = oh * lax.rsqrt(jnp.mean(oh * oh, -1, keepdims=True) + RMS_EPS) * ng[:, sl]
        outs.append(nrm * jax.nn.silu(z[:, sl]))
    return (jnp.concatenate(outs, -1),)


def _head_selectors():
    r, c = _iota2((BA_PAD, DN_KEY_DIM), 0), _iota2((BA_PAD, DN_KEY_DIM), 1) // DN_HEAD_DIM
    return (r == c).astype(f32), (r == c + DN_HEADS).astype(f32)


def dn_mixer_fwd(tag, proj, cw, alog_b, dtb_b, ng_b):
    eb, ea = _head_selectors()
    ba = (proj, BA_PAD, COL_BA // BA_PAD)
    qkv = dn_conv_fwd(tag, proj, cw)
    betab, gb = rowmap("dn_gate_" + tag, _gate_tile, [ba], [eb, ea, alog_b, dtb_b], [DN_KEY_DIM] * 2, TM_ROW)
    o, states = delta_fwd(tag, qkv, gb, betab)
    z = (proj, DN_KEY_DIM, COL_Z // DN_KEY_DIM)
    a_out = rowmap("dn_out_" + tag, _dn_out_tile, [o, z], [ng_b], [DN_KEY_DIM], TM_ROW)[0]
    return a_out, (qkv, betab, gb, o, states)


def dn_mixer_bwd(tag, proj, cw, alog_b, dtb_b, ng_b, res, da_out):
    qkv, betab, gb, o, states = res
    eb, ea = _head_selectors()
    ba = (proj, BA_PAD, COL_BA // BA_PAD)
    z = (proj, DN_KEY_DIM, COL_Z // DN_KEY_DIM)
    (do, dz), (dng,) = rowmap_bwd("dn_out_bwd_" + tag, _dn_out_tile, [o, z], [ng_b], [da_out], TM_ROW)
    dq, dk, dv, dgb, dbetab = delta_bwd(tag, qkv, gb, betab, states, do)
    dqkv_raw, dcw = dn_conv_bwd(tag, proj, cw, jnp.concatenate([dq, dk, dv], 1))
    (dba,), (dalog, ddtb) = rowmap_bwd("dn_gate_bwd_" + tag, _gate_tile, [ba], [eb, ea, alog_b, dtb_b],
                                       [dbetab, dgb], TM_ROW, par_mask=[False, False, True, True])
    return dqkv_raw, dz, dba, dcw[:DN_CONV], dalog, ddtb, dng


def _swap_halves(x):
    n = x.shape[1]
    first = (_iota2((1, n), 1) % SW_HEAD_DIM) < SW_HEAD_DIM // 2
    return jnp.where(first, pltpu.roll(x, n - SW_HEAD_DIM // 2, 1), pltpu.roll(x, SW_HEAD_DIM // 2, 1))


def _rope_apply(x, cos, sin_signed):
    return x * cos + _swap_halves(x) * sin_signed


def _rope_transpose(dy, cos, sin_signed):
    return dy * cos + _swap_halves(dy * sin_signed)


def rope_tables(positions, s):
    half = SW_HEAD_DIM // 2
    inv_freq = ROPE_THETA ** (-jnp.arange(0, SW_HEAD_DIM, 2, dtype=f32) / SW_HEAD_DIM)
    ang = positions.reshape(s, 1).astype(f32) * inv_freq[None, :]
    cos, sin = jnp.cos(ang), jnp.sin(ang)
    cos_t = jnp.tile(jnp.concatenate([cos, cos], 1), (1, SW_HEADS))
    sin_t = jnp.tile(jnp.concatenate([-sin, sin], 1), (1, SW_HEADS))
    assert cos_t.shape == (s, SW_DIM) and half * 2 == SW_HEAD_DIM
    return cos_t, sin_t


def rope_fwd(tag, proj, cos, sin):
    def fn(q, k, v, c, sg):
        return _rope_apply(q, c, sg), _rope_apply(k, c, sg), v

    rows = [(proj, SW_DIM, COL_SWQ // SW_DIM), (proj, SW_DIM, COL_SWK // SW_DIM), (proj, SW_DIM, COL_SWV // SW_DIM), cos, sin]
    return rowmap("rope_" + tag, fn, rows, [], [SW_DIM] * 3, TM_ROW)


def _swa_block(q, kp, kc, vp, vc, first):
    blk = SW_BLOCK
    kk = jnp.concatenate([kp, kc], 0)
    vv = jnp.concatenate([vp, vc], 0)
    dist = (_iota2((blk, 2 * blk), 0) + blk) - _iota2((blk, 2 * blk), 1)
    kj = _iota2((blk, 2 * blk), 1)
    valid = (dist >= 0) & (dist <= blk) & ((kj >= blk) | jnp.logical_not(first))
    lane_head = _iota2((1, LANES), 1) // SW_HEAD_DIM
    outs, lses = [], []
    for p in range(SW_DIM // LANES):
        sl = slice(p * LANES, (p + 1) * LANES)
        qp, kp_, vp_ = q[:, sl], kk[:, sl], vv[:, sl]
        o_pair = jnp.zeros((blk, LANES), f32)
        l_pair = jnp.zeros((blk, LANES), f32)
        for e in range(LANES // SW_HEAD_DIM):
            msk = lane_head == e
            sc = bdot(jnp.where(msk, qp, 0.0), kp_, 1, 1) * (SW_HEAD_DIM ** -0.5)
            sc = jnp.where(valid, sc, -1e30)
            m = lax.stop_gradient(jnp.max(sc, -1, keepdims=True))
            pe = jnp.exp(sc - m)
            l = jnp.sum(pe, -1, keepdims=True)
            o = bdot(pe, vp_, 1, 0) / l
            o_pair = o_pair + jnp.where(msk, o, 0.0)
            l_pair = l_pair + jnp.where(msk, m + jnp.log(l), 0.0)
        outs.append(o_pair)
        lses.append(l_pair)
    return jnp.concatenate(outs, -1), jnp.concatenate(lses, -1)


def _swa_specs(r):
    cur = pl.BlockSpec((SW_BLOCK, SW_DIM), lambda rho, n: (n, rho))
    prev = pl.BlockSpec((SW_BLOCK, SW_DIM), lambda rho, n: (jnp.maximum(n - 1, 0), rho))
    return cur, prev


def swa_fwd(tag, r, q, k, v):
    s = q.shape[0]
    ln = s // r
    q2, k2, v2 = (t.reshape(ln, r * SW_DIM) for t in (q, k, v))
    cur, prev = _swa_specs(r)

    def body(q_ref, kp_ref, kc_ref, vp_ref, vc_ref, o_ref, l_ref):
        o, l = _swa_block(q_ref[...], kp_ref[...], kc_ref[...], vp_ref[...], vc_ref[...], pl.program_id(1) == 0)
        o_ref[...] = o
        l_ref[...] = l

    o, l = pl.pallas_call(
        body, grid=(r, ln // SW_BLOCK),
        in_specs=[cur, prev, cur, prev, cur], out_specs=[cur, cur],
        out_shape=[jax.ShapeDtypeStruct((ln, r * SW_DIM), f32)] * 2,
        compiler_params=_cparams("parallel", "parallel"), name=f"swa{r}_{tag}")(q2, k2, k2, v2, v2)
    return o.reshape(s, SW_DIM), l.reshape(s, SW_DIM)


def swa_bwd(tag, r, q, k, v, do, dl):
    s = q.shape[0]
    ln = s // r
    q2, k2, v2, do2, dl2 = (t.reshape(ln, r * SW_DIM) for t in (q, k, v, do, dl))
    cur, prev = _swa_specs(r)

    def body(q_ref, kp_ref, kc_ref, vp_ref, vc_ref, do_ref, dl_ref, dq_ref, dka_ref, dkb_ref, dva_ref, dvb_ref):
        first = pl.program_id(1) == 0
        _, vjp = jax.vjp(lambda *a: _swa_block(*a, first), q_ref[...], kp_ref[...], kc_ref[...], vp_ref[...], vc_ref[...])
        dq_ref[...], dka_ref[...], dkb_ref[...], dva_ref[...], dvb_ref[...] = vjp((do_ref[...], dl_ref[...]))

    outs = pl.pallas_call(
        body, grid=(r, ln // SW_BLOCK),
        in_specs=[cur, prev, cur, prev, cur, cur, cur], out_specs=[cur] * 5,
        out_shape=[jax.ShapeDtypeStruct((ln, r * SW_DIM), f32)] * 5,
        compiler_params=_cparams("parallel", "parallel"), name=f"swa{r}_bwd_{tag}")(q2, k2, k2, v2, v2, do2, dl2)
    return [t.reshape(s, SW_DIM) for t in outs]


def _combine_tile(o1, l1, o2, l2, o3, l3):
    m = lax.stop_gradient(jnp.maximum(jnp.maximum(l1, l2), l3))
    e1, e2, e3 = jnp.exp(l1 - m), jnp.exp(l2 - m), jnp.exp(l3 - m)
    return ((o1 * e1 + o2 * e2 + o3 * e3) / (e1 + e2 + e3),)


def swa_merge_bwd(tag, grads, cos, sin):
    s = cos.shape[0]
    tm = SW_BLOCK
    nt = s // tm
    here = pl.BlockSpec((tm, SW_DIM), lambda i: (i, 0))
    arrs, specs = [], []
    for r, g in zip(SW_DILATIONS, grads):
        ahead = pl.BlockSpec((tm, SW_DIM), lambda i, _r=r: (jnp.minimum(i + _r, nt - 1), 0))
        arrs += g
        specs += [here, ahead, here, ahead, here]

    def body(*refs):
        i = pl.program_id(0)
        c_ref, s_ref = refs[15], refs[16]
        dq_ref, dk_ref, dv_ref = refs[17:]
        dq = jnp.zeros((tm, SW_DIM), f32)
        dk = jnp.zeros((tm, SW_DIM), f32)
        dv = jnp.zeros((tm, SW_DIM), f32)
        for b, r in enumerate(SW_DILATIONS):
            gq, gka, gkb, gva, gvb = refs[5 * b:5 * b + 5]
            inside = i + r < nt
            dq = dq + gq[...]
            dk = dk + gkb[...] + jnp.where(inside, gka[...], 0.0)
            dv = dv + gvb[...] + jnp.where(inside, gva[...], 0.0)
        dq_ref[...] = _rope_transpose(dq, c_ref[...], s_ref[...])
        dk_ref[...] = _rope_transpose(dk, c_ref[...], s_ref[...])
        dv_ref[...] = dv

    return pl.pallas_call(
        body, grid=(nt,), in_specs=specs + [here, here], out_specs=[here] * 3,
        out_shape=[jax.ShapeDtypeStruct((s, SW_DIM), f32)] * 3,
        compiler_params=_cparams("parallel"), name="swa_merge_bwd_" + tag)(*arrs, cos, sin)


def swa_mixer_fwd(tag, proj, cos, sin):
    q, k, v = rope_fwd(tag, proj, cos, sin)
    ols = []
    for r in SW_DILATIONS:
        ols += list(swa_fwd(tag, r, q, k, v))
    b_out = rowmap("swa_comb_" + tag, _combine_tile, ols, [], [SW_DIM], TM_ROW)[0]
    return b_out, (q, k, v, ols)


def swa_mixer_bwd(tag, cos, sin, res, db_out):
    q, k, v, ols = res
    dols, _ = rowmap_bwd("swa_comb_bwd_" + tag, _combine_tile, ols, [], [db_out], TM_ROW)
    grads = [swa_bwd(tag, r, q, k, v, dols[2 * b], dols[2 * b + 1]) for b, r in enumerate(SW_DILATIONS)]
    return swa_merge_bwd(tag, grads, cos, sin)


TM_S5 = 256
S5_GPB = LANES // S5_GROUP
S5_NBLK = D_MODEL // LANES
S5_HALF = S5_GPB * S5_STATE
S5_BW = 2 * S5_HALF
S5_WIDTH = S5_NBLK * S5_BW
S5_TABW = S5_NBLK * S5_HALF


def _s5_disc_tile(a_re, a_im, log_dt, b_re, b_im, expand):
    dt = jnp.exp(log_dt)
    mag = jnp.exp(a_re * dt)
    abar_re, abar_im = mag * jnp.cos(a_im * dt), mag * jnp.sin(a_im * dt)
    n_re, n_im = abar_re - 1.0, abar_im
    den = a_re * a_re + a_im * a_im
    c_re = (n_re * a_re + n_im * a_im) / den
    c_im = (n_im * a_re - n_re * a_im) / den
    cx_re, cx_im = hdot(c_re, expand), hdot(c_im, expand)
    return abar_re, abar_im, cx_re * b_re - cx_im * b_im, cx_re * b_im + cx_im * b_re


def _s5_expand():
    return (_iota2((S5_STATE, S5_STATE * S5_GROUP), 1) // S5_GROUP == _iota2((S5_STATE, S5_STATE * S5_GROUP), 0)).astype(f32)


def s5_tables(a_re, a_im, log_dt):
    lanes = lambda v: v.reshape(1, S5_TABW)
    dt = jnp.broadcast_to(log_dt.reshape(S5_GROUPS, 1), (S5_GROUPS, S5_STATE))
    t = TM_S5

    def body(are_ref, aim_ref, ldt_ref, ar_ref, ai_ref, arr_ref, air_ref):
        dtv = jnp.exp(ldt_ref[...])
        lre, lim = are_ref[...] * dtv, aim_ref[...] * dtv
        row = _iota2((t, S5_HALF), 0)
        for asc, o_re, o_im in ((True, ar_ref, ai_ref), (False, arr_ref, air_ref)):
            n = (row + 1 if asc else t - row).astype(f32)
            mag = jnp.exp(n * lre)
            o_re[...] = mag * jnp.cos(n * lim)
            o_im[...] = mag * jnp.sin(n * lim)

    lane = pl.BlockSpec((1, S5_HALF), lambda j: (0, j))
    tab = pl.BlockSpec((t, S5_HALF), lambda j: (0, j))
    return pl.pallas_call(
        body, grid=(S5_NBLK,), in_specs=[lane] * 3, out_specs=[tab] * 4,
        out_shape=[jax.ShapeDtypeStruct((t, S5_TABW), f32)] * 4,
        compiler_params=_cparams("parallel"), name="s5_tables")(lanes(a_re), lanes(a_im), lanes(dt))


def s5_pack_weights(bbar_re, bbar_im, c_re, c_im):
    eye = jnp.eye(S5_GPB, dtype=f32)
    bb = jnp.stack([bbar_re.reshape(S5_GROUPS, S5_STATE, S5_GROUP), bbar_im.reshape(S5_GROUPS, S5_STATE, S5_GROUP)], 1)
    bb = bb.transpose(0, 3, 1, 2).reshape(S5_NBLK, S5_GPB, S5_GROUP, 2, S5_STATE)
    wb = (bb[:, :, :, :, None, :] * eye[None, :, None, None, :, None]).reshape(S5_NBLK, LANES, S5_BW)
    cc = jnp.stack([c_re, -c_im], 1)
    cc = cc.reshape(S5_NBLK, S5_GPB, 2, S5_GROUP, S5_STATE).transpose(0, 2, 1, 4, 3)
    wc = (cc[:, :, :, :, None, :] * eye[None, None, :, None, :, None]).reshape(S5_NBLK, S5_BW, LANES)
    return wb, wc


def s5_unpack_weight_grads(dwb, dwc):
    d6 = dwb.reshape(S5_NBLK, S5_GPB, S5_GROUP, 2, S5_GPB, S5_STATE)
    dbb = jnp.stack([d6[:, gl, :, :, gl, :] for gl in range(S5_GPB)])
    dbb = dbb.transpose(1, 0, 3, 4, 2).reshape(S5_GROUPS, 2, S5_STATE * S5_GROUP)
    c6 = dwc.reshape(S5_NBLK, 2, S5_GPB, S5_STATE, S5_GPB, S5_GROUP)
    dcc = jnp.stack([c6[:, :, gl, :, gl, :] for gl in range(S5_GPB)])
    dcc = dcc.transpose(1, 0, 2, 4, 3).reshape(S5_GROUPS, 2, S5_GROUP, S5_STATE)
    return dbb[:, 0], dbb[:, 1], dcc[:, 0], -dcc[:, 1]


def _s5_step_rows(t):
    d, out = 1, []
    while d < t:
        out.append(d)
        d *= 2
    return out


def s5_core_fwd(tag, u, wb, wc, a1, a2, dskip):
    s = u.shape[0]
    t = TM_S5

    def body(u_ref, wb_ref, wc_ref, ar_ref, ai_ref, d_ref, y_ref, x_ref, carry):
        @pl.when(pl.program_id(1) == 0)
        def _():
            carry[...] = jnp.zeros_like(carry)

        uv = u_ref[...]
        bu = bdot(uv, wb_ref[...], 1, 0)
        xr, xi = bu[:, :S5_HALF], bu[:, S5_HALF:]
        row = _iota2((t, S5_HALF), 0)
        for d in _s5_step_rows(t):
            keep = row >= d
            sr = jnp.where(keep, pltpu.roll(xr, d, 0), 0.0)
            si = jnp.where(keep, pltpu.roll(xi, d, 0), 0.0)
            ar, ai = ar_ref[d - 1:d, :], ai_ref[d - 1:d, :]
            xr, xi = xr + ar * sr - ai * si, xi + ar * si + ai * sr
        cr, ci = carry[:, :S5_HALF], carry[:, S5_HALF:]
        ar, ai = ar_ref[...], ai_ref[...]
        x_ref[:, :S5_HALF] = xr + ar * cr - ai * ci
        x_ref[:, S5_HALF:] = xi + ar * ci + ai * cr
        carry[...] = x_ref[t - 1:t, :]
        y_ref[...] = bdot(x_ref[...], wc_ref[...], 1, 0) + d_ref[...] * uv

    tab = pl.BlockSpec((t, S5_HALF), lambda j, i: (0, j))
    return pl.pallas_call(
        body, grid=(S5_NBLK, s // t),
        in_specs=[pl.BlockSpec((t, LANES), lambda j, i: (i, j)),
                  pl.BlockSpec((None, LANES, S5_BW), lambda j, i: (j, 0, 0)),
                  pl.BlockSpec((None, S5_BW, LANES), lambda j, i: (j, 0, 0)),
                  tab, tab, pl.BlockSpec((1, LANES), lambda j, i: (0, j))],
        out_specs=[pl.BlockSpec((t, LANES), lambda j, i: (i, j)), pl.BlockSpec((t, S5_BW), lambda j, i: (i, j))],
        out_shape=[jax.ShapeDtypeStruct((s, D_MODEL), f32), jax.ShapeDtypeStruct((s, S5_WIDTH), f32)],
        scratch_shapes=[pltpu.VMEM((1, S5_BW), f32)],
        compiler_params=_cparams("parallel", "arbitrary"), name="s5_core_" + tag)(u, wb, wc, a1, a2, dskip)


def s5_core_bwd(tag, u, x, wb, wc, a1, a2, a1r, a2r, dskip, dy):
    s = u.shape[0]
    t = TM_S5
    nt = s // t
    hb = t // SUBLANES

    def body(u_ref, dy_ref, x_ref, xh_ref, wb_ref, wc_ref, ar_ref, ai_ref, arr_ref, air_ref, d_ref,
             du_ref, dwb_ref, dwc_ref, dd_ref, q1_ref, q2_ref, carry, lam_scr):
        i = pl.program_id(1)
        tt = nt - 1 - i

        @pl.when(i == 0)
        def _():
            carry[...] = jnp.zeros_like(carry)

        uv, dyv, xv = u_ref[...], dy_ref[...], x_ref[...]
        lam = bdot(dyv, wc_ref[...], 1, 1)
        lr, li = lam[:, :S5_HALF], lam[:, S5_HALF:]
        row = _iota2((t, S5_HALF), 0)
        for d in _s5_step_rows(t):
            keep = row < t - d
            sr = jnp.where(keep, pltpu.roll(lr, t - d, 0), 0.0)
            si = jnp.where(keep, pltpu.roll(li, t - d, 0), 0.0)
            ar, ai = ar_ref[d - 1:d, :], ai_ref[d - 1:d, :]
            lr, li = lr + ar * sr + ai * si, li + ar * si - ai * sr
        cr, ci = carry[:, :S5_HALF], carry[:, S5_HALF:]
        ar, ai = arr_ref[...], air_ref[...]
        lr, li = lr + ar * cr + ai * ci, li + ar * ci - ai * cr
        lam_scr[:, :S5_HALF] = lr
        lam_scr[:, S5_HALF:] = li
        carry[...] = lam_scr[0:1, :]
        lam = lam_scr[...]
        du_ref[...] = bdot(lam, wb_ref[...], 1, 1) + d_ref[...] * dyv
        x_last = jnp.where(tt > 0, xh_ref[SUBLANES - 1:SUBLANES, :], 0.0)
        x_prev = jnp.where(_iota2((t, S5_BW), 0) == 0, x_last, pltpu.roll(xv, 1, 0))
        pr, pi = x_prev[:, :S5_HALF], x_prev[:, S5_HALF:]
        p1, p2 = lr * pr + li * pi, li * pr - lr * pi
        q1 = p1[:SUBLANES, :]
        q2 = p2[:SUBLANES, :]
        for k in range(1, hb):
            q1 = q1 + p1[k * SUBLANES:(k + 1) * SUBLANES, :]
            q2 = q2 + p2[k * SUBLANES:(k + 1) * SUBLANES, :]
        upd = [(dwb_ref, bdot(uv, lam, 0, 0)), (dwc_ref, bdot(xv, dyv, 0, 0)),
               (dd_ref, jnp.sum(dyv * uv, 0, keepdims=True)), (q1_ref, q1), (q2_ref, q2)]

        @pl.when(i == 0)
        def _():
            for ref, val in upd:
                ref[...] = val

        @pl.when(i != 0)
        def _():
            for ref, val in upd:
                ref[...] += val

    nb8 = s // SUBLANES
    rev = lambda w: pl.BlockSpec((t, w), lambda j, i: (nt - 1 - i, j))
    tab = pl.BlockSpec((t, S5_HALF), lambda j, i: (0, j))
    return pl.pallas_call(
        body, grid=(S5_NBLK, nt),
        in_specs=[rev(LANES), rev(LANES), rev(S5_BW),
                  pl.BlockSpec((SUBLANES, S5_BW), lambda j, i: (jnp.maximum((nt - 1 - i) * hb - 1, 0), j)),
                  pl.BlockSpec((None, LANES, S5_BW), lambda j, i: (j, 0, 0)),
                  pl.BlockSpec((None, S5_BW, LANES), lambda j, i: (j, 0, 0)),
                  tab, tab, tab, tab, pl.BlockSpec((1, LANES), lambda j, i: (0, j))],
        out_specs=[rev(LANES),
                   pl.BlockSpec((None, LANES, S5_BW), lambda j, i: (j, 0, 0)),
                   pl.BlockSpec((None, S5_BW, LANES), lambda j, i: (j, 0, 0)),
                   pl.BlockSpec((1, LANES), lambda j, i: (0, j)),
                   pl.BlockSpec((SUBLANES, S5_HALF), lambda j, i: (0, j)),
                   pl.BlockSpec((SUBLANES, S5_HALF), lambda j, i: (0, j))],
        out_shape=[jax.ShapeDtypeStruct((s, D_MODEL), f32),
                   jax.ShapeDtypeStruct((S5_NBLK, LANES, S5_BW), f32),
                   jax.ShapeDtypeStruct((S5_NBLK, S5_BW, LANES), f32),
                   jax.ShapeDtypeStruct((1, D_MODEL), f32),
                   jax.ShapeDtypeStruct((SUBLANES, S5_TABW), f32),
                   jax.ShapeDtypeStruct((SUBLANES, S5_TABW), f32)],
        scratch_shapes=[pltpu.VMEM((1, S5_BW), f32), pltpu.VMEM((t, S5_BW), f32)],
        compiler_params=_cparams("parallel", "arbitrary"),
        name="s5_core_bwd_" + tag)(u, dy, x, x, wb, wc, a1, a2, a1r, a2r, dskip)


def _gelu_tile(y):
    return (jax.nn.gelu(y),)


def s5_mixer_fwd(tag, u, prm, w_og):
    a_re, a_im, log_dt, b_re, b_im, c_re, c_im, dskip = prm
    disc_in = [a_re, a_im, log_dt.reshape(S5_GROUPS, 1), b_re.reshape(S5_GROUPS, -1), b_im.reshape(S5_GROUPS, -1)]
    abar_re, abar_im, bbar_re, bbar_im = rowmap("s5_disc_" + tag, _s5_disc_tile, disc_in, [_s5_expand()],
                                                [S5_STATE, S5_STATE, S5_STATE * S5_GROUP, S5_STATE * S5_GROUP], S5_GROUPS)
    del abar_re, abar_im
    a1, a2, a1r, a2r = s5_tables(a_re, a_im, log_dt)
    wb, wc = s5_pack_weights(bbar_re, bbar_im, c_re, c_im)
    wb, wc = wb.astype(bf16), wc.astype(bf16)
    y, x = s5_core_fwd(tag, u, wb, wc, a1, a2, dskip.reshape(1, D_MODEL))
    hid = rowmap("s5_gelu_" + tag, _gelu_tile, [y], [], [D_MODEL], TM_ROW)[0]
    og = mm_nn("s5_og_" + tag, hid, w_og)
    mix = rowmap("s5_glu_" + tag, _glu_tile, [og], [], [D_MODEL], TM_ROW)[0]
    return mix, (disc_in, a1, a2, a1r, a2r, wb, wc, x, y, hid, og)


def s5_mixer_bwd(tag, u, prm, w_og, res, dmix):
    a_re, a_im, log_dt, b_re, b_im, c_re, c_im, dskip = prm
    disc_in, a1, a2, a1r, a2r, wb, wc, x, y, hid, og = res
    (dog,), _ = rowmap_bwd("s5_glu_bwd_" + tag, _glu_tile, [og], [], [dmix], TM_ROW)
    dw_og = mm_tn("s5_og_dw_" + tag, hid, dog)
    dhid = mm_nt("s5_og_dx_" + tag, dog, w_og)
    (dy,), _ = rowmap_bwd("s5_gelu_bwd_" + tag, _gelu_tile, [y], [], [dhid], TM_ROW)
    du, dwb, dwc, ddskip, q1, q2 = s5_core_bwd(tag, u, x, wb, wc, a1, a2, a1r, a2r, dskip.reshape(1, D_MODEL), dy)
    dbbar_re, dbbar_im, dc_re, dc_im = s5_unpack_weight_grads(dwb, dwc)
    dabar_re = q1.sum(0).reshape(S5_GROUPS, S5_STATE)
    dabar_im = q2.sum(0).reshape(S5_GROUPS, S5_STATE)
    grads, _ = rowmap_bwd("s5_disc_bwd_" + tag, _s5_disc_tile, disc_in, [_s5_expand()],
                          [dabar_re, dabar_im, dbbar_re, dbbar_im], S5_GROUPS, par_mask=[False])
    da_re, da_im, dlog_dt, db_re, db_im = grads
    return du, (da_re, da_im, dlog_dt.reshape(S5_GROUPS), db_re.reshape(b_re.shape), db_im.reshape(b_im.shape),
                dc_re, dc_im, ddskip.reshape(D_MODEL)), dw_og


HYB_IN = 3592
_IN_B0, _IN_SW0 = 2048, 2056


IN_SHARD = HYB_IN // 4
SHARD_ORDER_GRADS = ("hyb_w_in", "ffn_wg", "ffn_wu", "ffn_wd")


def _w_in_pieces():
    runs = [(0, _IN_B0, 0), (_IN_B0, _IN_SW0, COL_BA), (_IN_SW0, HYB_IN, _IN_B0)]
    out = []
    for sh in range(4):
        lo, hi = sh * IN_SHARD, (sh + 1) * IN_SHARD
        for r_lo, r_hi, c_lo in runs:
            a, b = max(lo, r_lo), min(hi, r_hi)
            if a < b:
                out.append((sh, a - lo, b - lo, c_lo + a - r_lo))
    return out


def w_in_to_canonical(tag, layer, w4):
    tr = 128

    def body(w_ref, o_ref):
        o_ref[:, COL_BA:] = jnp.zeros((tr, BA_PAD), o_ref.dtype)
        for sh, a, b, c in _w_in_pieces():
            o_ref[:, c:c + b - a] = w_ref[sh, :, a:b]

    return pl.pallas_call(
        body, grid=(D_MODEL // tr,),
        in_specs=[pl.BlockSpec((4, None, tr, IN_SHARD), lambda i: (0, layer, i, 0))],
        out_specs=pl.BlockSpec((tr, PROJ_COLS), lambda i: (i, 0)),
        out_shape=jax.ShapeDtypeStruct((D_MODEL, PROJ_COLS), w4.dtype),
        compiler_params=_cparams("parallel"), name="w_in_canon_" + tag)(w4)


def w_in_grad_to_shards(tag, g):
    tr = 128

    def body(g_ref, o_ref):
        for sh, a, b, c in _w_in_pieces():
            o_ref[sh, :, a:b] = g_ref[:, c:c + b - a]

    return pl.pallas_call(
        body, grid=(D_MODEL // tr,),
        in_specs=[pl.BlockSpec((tr, PROJ_COLS), lambda i: (i, 0))],
        out_specs=pl.BlockSpec((4, tr, IN_SHARD), lambda i: (0, i, 0)),
        out_shape=jax.ShapeDtypeStruct((4, D_MODEL, IN_SHARD), f32),
        compiler_params=_cparams("parallel"), name="w_in_grad_shards_" + tag)(g)


def _add2(name, a, b):
    return rowmap(name, lambda p, q: (p + q,), [a, b], [], [a.shape[1]], _pick(a.shape[0], (256, 128, 64, 32, 16, 8)))[0]


def local_step(x, mem, positions, target, p):
    s = x.shape[0]
    cos, sin = rope_tables(positions, s)
    row = lambda v: v.reshape(1, -1).astype(f32)
    wg4, wu4, wd4 = (p[n].astype(bf16) for n in ("ffn_wg", "ffn_wu", "ffn_wd"))
    h = x
    tape = []
    for l in range(DEPTH):
        i, tag = l // 2, str(l)
        t = {"h0": h}
        if l % 2 == 0:
            t["w_in"] = w_in_to_canonical(tag, i, p["hyb_w_in"].astype(bf16))
            t["w_out"] = p["hyb_w_out"][i].astype(bf16)
            t["dn_prm"] = (p["dn_conv_w"][i].astype(f32), row(jnp.repeat(p["dn_a_log"][i], DN_HEAD_DIM)),
                           row(jnp.repeat(p["dn_dt_bias"][i], DN_HEAD_DIM)), row(jnp.tile(p["dn_norm_g"][i], DN_HEADS)))
            t["proj"] = mm_nn("hyb_in_" + tag, h, t["w_in"])
            a_out, t["dn"] = dn_mixer_fwd(tag, t["proj"], *t["dn_prm"])
            b_out, t["swa"] = swa_mixer_fwd(tag, t["proj"], cos, sin)
            t["mixed"] = jnp.concatenate([a_out, b_out], 1)
            mix = mm_nn("hyb_out_" + tag, t["mixed"], t["w_out"])
        else:
            t["s5_prm"] = tuple(p[n][i].astype(f32) for n in
                                ("s5_a_re", "s5_a_im", "s5_log_dt", "s5_b_re", "s5_b_im", "s5_c_re", "s5_c_im", "s5_d"))
            t["w_og"] = jnp.concatenate([p["s5_glu_wo"][i], p["s5_glu_wg"][i]], 1).astype(bf16)
            mix, t["s5"] = s5_mixer_fwd(tag, h, t["s5_prm"], t["w_og"])
        t["mix"] = mix
        t["ln"] = [(row(p[g][l]), row(p[b][l])) for g, b in
                   (("ln_mix_g", "ln_mix_b"), ("ln_x_g", "ln_x_b"), ("ln_ffn_g", "ln_ffn_b"))]
        t["h1"] = postnorm_fwd("mix" + tag, h, mix, *t["ln"][0])
        t["wq"], t["wo"] = p["xq_w"][l].astype(bf16), p["xo_w"][l].astype(bf16)
        t["wkv"] = jnp.concatenate([p["xk_w"][l], p["xv_w"][l]], 1).astype(bf16)
        t["xo"], t["xres"] = xattn_fwd(tag, t["h1"], mem, t["wq"], t["wkv"], t["wo"])
        t["h2"] = postnorm_fwd("x" + tag, t["h1"], t["xo"], *t["ln"][1])
        t["fo"], t["fres"] = ffn_fwd(tag, l, t["h2"], wg4, wu4, wd4)
        h = postnorm_fwd("ffn" + tag, t["h2"], t["fo"], *t["ln"][2])
        tape.append(t)

    part, dh = loss_head(h, target)
    loss = jnp.sum(part)

    g = {n: [None] * v.shape[1 if n in SHARD_ORDER_GRADS else 0] for n, v in p.items()}
    for l in reversed(range(DEPTH)):
        i, tag, t = l // 2, str(l), tape[l]
        dh2a, dfo, dg, db = postnorm_bwd("ffn" + tag, t["h2"], t["fo"], *t["ln"][2], dh)
        g["ln_ffn_g"][l], g["ln_ffn_b"][l] = dg[0], db[0]
        dh2b, g["ffn_wg"][l], g["ffn_wu"][l], g["ffn_wd"][l] = ffn_bwd(tag, l, t["h2"], wg4, wu4, wd4, t["fres"], dfo)
        dh1a, dxo, dg, db = postnorm_bwd("x" + tag, t["h1"], t["xo"], *t["ln"][1], [dh2a, dh2b])
        g["ln_x_g"][l], g["ln_x_b"][l] = dg[0], db[0]
        dh1b, g["xq_w"][l], dwkv, g["xo_w"][l] = xattn_bwd(tag, t["h1"], mem, t["wq"], t["wkv"], t["wo"], t["xres"], dxo)
        g["xk_w"][l], g["xv_w"][l] = dwkv[:, :D_MODEL], dwkv[:, D_MODEL:]
        dh0a, dmix, dg, db = postnorm_bwd("mix" + tag, t["h0"], t["mix"], *t["ln"][0], [dh1a, dh1b])
        g["ln_mix_g"][l], g["ln_mix_b"][l] = dg[0], db[0]
        if l % 2 == 0:
            g["hyb_w_out"][i] = mm_tn("hyb_out_dw_" + tag, t["mixed"], dmix)
            dmixed = mm_nt("hyb_out_dx_" + tag, dmix, t["w_out"])
            dqkv, dz, dba, dcw, dalog, ddtb, dng = dn_mixer_bwd(tag, t["proj"], *t["dn_prm"], t["dn"], (dmixed, DN_KEY_DIM, 0))
            g["dn_conv_w"][i] = dcw
            g["dn_a_log"][i] = dalog.reshape(DN_HEADS, DN_HEAD_DIM).sum(1)
            g["dn_dt_bias"][i] = ddtb.reshape(DN_HEADS, DN_HEAD_DIM).sum(1)
            g["dn_norm_g"][i] = dng.reshape(DN_HEADS, DN_HEAD_DIM).sum(0)
            dq, dk, dv = swa_mixer_bwd(tag, cos, sin, t["swa"], (dmixed, SW_DIM, 1))
            dproj = jnp.concatenate([dqkv, dz, dq, dk, dv, dba], 1)
            g["hyb_w_in"][i] = w_in_grad_to_shards(tag, mm_tn("hyb_in_dw_" + tag, t["h0"], dproj))
            dh0b = mm_nt("hyb_in_dx_" + tag, dproj, t["w_in"])
        else:
            dh0b, dprm, dw_og = s5_mixer_bwd(tag, t["h0"], t["s5_prm"], t["w_og"], t["s5"], dmix)
            for n, v in zip(("s5_a_re", "s5_a_im", "s5_log_dt", "s5_b_re", "s5_b_im", "s5_c_re", "s5_c_im", "s5_d"), dprm):
                g[n][i] = v
            g["s5_glu_wo"][i], g["s5_glu_wg"][i] = dw_og[:, :D_MODEL], dw_og[:, D_MODEL:]
        dh = [dh0a, dh0b]
    grad_x = _add2("grad_x", dh[0], dh[1])
    grads = {n: jnp.stack(v, 1 if n in SHARD_ORDER_GRADS else 0) for n, v in g.items()}
    return loss, grad_x, grads


WEIGHT_NAMES = ("hyb_w_in", "dn_conv_w", "dn_a_log", "dn_dt_bias", "dn_norm_g", "hyb_w_out", "s5_a_re", "s5_a_im",
                "s5_log_dt", "s5_b_re", "s5_b_im", "s5_c_re", "s5_c_im", "s5_d", "s5_glu_wo", "s5_glu_wg",
                "ln_mix_g", "ln_mix_b", "xq_w", "xk_w", "xv_w", "xo_w", "ln_x_g", "ln_x_b",
                "ffn_wg", "ffn_wu", "ffn_wd", "ln_ffn_g", "ln_ffn_b")
SHARD_AXIS = {"hyb_w_in": 2, "dn_conv_w": 2, "hyb_w_out": 1, "s5_d": 1, "s5_glu_wo": 1, "s5_glu_wg": 1,
              "xq_w": 1, "xk_w": 1, "xv_w": 1, "xo_w": 1, "ffn_wg": 2, "ffn_wu": 2, "ffn_wd": 1}
GATHER_F32 = ("dn_conv_w", "s5_d")
N_CHIPS = 4
PACK_COLS = 1024
_ANY = pl.BlockSpec(memory_space=pl.ANY)


def _pos():
    return lax.axis_index("x"), lax.axis_index("y"), lax.axis_index("c")


def _chip_peers(mx, my):
    return [(1 - mx, my), (mx, 1 - my), (1 - mx, 1 - my)]


def _rcopy(src, dst, ssem, rsem, dev):
    return pltpu.make_async_remote_copy(src_ref=src, dst_ref=dst, send_sem=ssem, recv_sem=rsem,
                                        device_id=dev, device_id_type=pl.DeviceIdType.MESH)


def comm_allgather4(name, x):
    def body(x_ref, o_ref, ssem, rsem, lsem):
        mx, my, mc = _pos()
        me = 2 * mx + my
        peers = _chip_peers(mx, my)
        loc = pltpu.make_async_copy(x_ref, o_ref.at[me], lsem)
        loc.start()
        sends = [_rcopy(x_ref, o_ref.at[me], ssem.at[k], rsem.at[k], (px, py, mc)) for k, (px, py) in enumerate(peers)]
        for cp in sends:
            cp.start()
        for k, (px, py) in enumerate(peers):
            _rcopy(x_ref, o_ref.at[2 * px + py], ssem.at[k], rsem.at[k], (px, py, mc)).wait_recv()
        for cp in sends:
            cp.wait_send()
        loc.wait()

    return pl.pallas_call(
        body, out_shape=jax.ShapeDtypeStruct((N_CHIPS,) + x.shape, x.dtype), in_specs=[_ANY], out_specs=_ANY,
        scratch_shapes=[pltpu.SemaphoreType.DMA((3,)), pltpu.SemaphoreType.DMA((3,)), pltpu.SemaphoreType.DMA],
        name=name)(x)


def _multi_call(name, body, ins, out_shapes, sems, in_place=False):
    return pl.pallas_call(
        body, out_shape=out_shapes, in_specs=[_ANY] * len(ins), out_specs=[_ANY] * len(out_shapes),
        scratch_shapes=sems, input_output_aliases={w: w for w in range(len(ins))} if in_place else {},
        name=name)(*ins)


def comm_gather_weights(name, slots):
    n = len(slots)

    def body(*refs):
        os_ = refs[n:2 * n]
        ssem, rsem, fssem, frsem = refs[2 * n:]
        mx, my, mc = _pos()
        me = 2 * mx + my
        peers = _chip_peers(mx, my)
        sib = (mx, my, 1 - mc)
        half = [o.shape[1] // 2 for o in os_]
        mine = [pl.ds(mc * h, h) for h in half]
        other = [pl.ds((1 - mc) * h, h) for h in half]
        sends = [_rcopy(os_[w].at[me, mine[w]], os_[w].at[me, mine[w]], ssem.at[w, k], rsem.at[w, k], (px, py, mc))
                 for w in range(n) for k, (px, py) in enumerate(peers)]
        for cp in sends:
            cp.start()
        fwds = []
        for w in range(n):
            for k, (px, py) in enumerate(peers):
                landed = os_[w].at[2 * px + py, mine[w]]
                _rcopy(landed, landed, ssem.at[w, k], rsem.at[w, k], (px, py, mc)).wait_recv()
                fw = _rcopy(landed, landed, fssem.at[w, k], frsem.at[w, k], sib)
                fw.start()
                fwds.append(fw)
        for w in range(n):
            for k, (px, py) in enumerate(peers):
                theirs = os_[w].at[2 * px + py, other[w]]
                _rcopy(theirs, theirs, fssem.at[w, k], frsem.at[w, k], sib).wait_recv()
        for cp in sends + fwds:
            cp.wait_send()

    dma = pltpu.SemaphoreType.DMA
    return _multi_call(name, body, slots, [jax.ShapeDtypeStruct(x.shape, x.dtype) for x in slots],
                       [dma((n, 3)), dma((n, 3)), dma((n, 3)), dma((n, 3))], in_place=True)


def comm_sibling_halves(name, gs):
    n = len(gs)

    def body(*refs):
        xs, os_ = refs[:n], refs[n:2 * n]
        ssem, rsem = refs[2 * n:]
        mx, my, mc = _pos()
        sib = (mx, my, 1 - mc)
        sends = []
        for w in range(n):
            h = xs[w].shape[1] // 2
            for j in range(N_CHIPS):
                sends.append(_rcopy(xs[w].at[j, pl.ds((1 - mc) * h, h)], os_[w].at[j], ssem.at[w, j], rsem.at[w, j], sib))
        for cp in sends:
            cp.start()
        for w in range(n):
            for j in range(N_CHIPS):
                _rcopy(os_[w].at[j], os_[w].at[j], ssem.at[w, j], rsem.at[w, j], sib).wait_recv()
        for cp in sends:
            cp.wait_send()

    dma = pltpu.SemaphoreType.DMA
    return _multi_call(name, body, gs,
                       [jax.ShapeDtypeStruct((N_CHIPS, g.shape[1] // 2) + g.shape[2:], g.dtype) for g in gs],
                       [dma((n, N_CHIPS)), dma((n, N_CHIPS))])


def comm_alltoall4(name, xs):
    n = len(xs)

    def body(*refs):
        xr, os_ = refs[:n], refs[n:2 * n]
        ssem, rsem = refs[2 * n:]
        mx, my, mc = _pos()
        me = 2 * mx + my
        peers = _chip_peers(mx, my)
        sends = [_rcopy(xr[w].at[2 * px + py], os_[w].at[me], ssem.at[w, k], rsem.at[w, k], (px, py, mc))
                 for w in range(n) for k, (px, py) in enumerate(peers)]
        for cp in sends:
            cp.start()
        for w in range(n):
            for k, (px, py) in enumerate(peers):
                dst = os_[w].at[2 * px + py]
                _rcopy(dst, dst, ssem.at[w, k], rsem.at[w, k], (px, py, mc)).wait_recv()
        for cp in sends:
            cp.wait_send()

    dma = pltpu.SemaphoreType.DMA
    return _multi_call(name, body, xs, [jax.ShapeDtypeStruct(x.shape, x.dtype) for x in xs], [dma((n, 3)), dma((n, 3))])


def comm_sibling_join(name, bs):
    n = len(bs)

    def body(*refs):
        os_ = refs[n:2 * n]
        ssem, rsem = refs[2 * n:]
        mx, my, mc = _pos()
        sib = (mx, my, 1 - mc)
        sends = [_rcopy(os_[w].at[mc], os_[w].at[mc], ssem.at[w], rsem.at[w], sib) for w in range(n)]
        for cp in sends:
            cp.start()
        for w in range(n):
            dst = os_[w].at[1 - mc]
            _rcopy(dst, dst, ssem.at[w], rsem.at[w], sib).wait_recv()
        for cp in sends:
            cp.wait_send()

    dma = pltpu.SemaphoreType.DMA
    return _multi_call(name, body, bs, [jax.ShapeDtypeStruct(b.shape, b.dtype) for b in bs], [dma((n,)), dma((n,))],
                       in_place=True)


def comm_sibling_swap(name, x):
    def body(x_ref, o_ref, ssem, rsem):
        mx, my, mc = _pos()
        cp = _rcopy(x_ref, o_ref, ssem, rsem, (mx, my, 1 - mc))
        cp.start()
        cp.wait_recv()
        cp.wait_send()

    return pl.pallas_call(
        body, out_shape=jax.ShapeDtypeStruct(x.shape, x.dtype), in_specs=[_ANY], out_specs=_ANY,
        scratch_shapes=[pltpu.SemaphoreType.DMA, pltpu.SemaphoreType.DMA], name=name)(x)


def _row_tile(r):
    return _pick(r, (256, 128, 64, 32, 16, 8))


def add_own_half(name, g, recv, out_dtype):
    r, c = g.shape[2:]
    tr = _row_tile(r)
    mc = lax.axis_index("c").astype(jnp.int32).reshape(1)

    def body(c_ref, g_ref, r_ref, o_ref):
        o_ref[...] = (g_ref[...] + r_ref[...]).astype(o_ref.dtype)

    grid_spec = pltpu.PrefetchScalarGridSpec(
        num_scalar_prefetch=1, grid=(N_CHIPS, r // tr),
        in_specs=[pl.BlockSpec((None, None, tr, c), lambda j, i, cr: (j, cr[0], i, 0)),
                  pl.BlockSpec((None, tr, c), lambda j, i, cr: (j, i, 0))],
        out_specs=pl.BlockSpec((None, tr, c), lambda j, i, cr: (j, i, 0)))
    return pl.pallas_call(body, grid_spec=grid_spec, out_shape=jax.ShapeDtypeStruct(recv.shape, out_dtype),
                          compiler_params=_cparams("parallel", "parallel"), name=name)(mc, g, recv)


def cast_into_slot(name, w, chip, dtype):
    r, c = w.shape
    tr = _row_tile(r)

    def body(c_ref, w_ref, o_ref):
        o_ref[...] = w_ref[...].astype(o_ref.dtype)

    grid_spec = pltpu.PrefetchScalarGridSpec(
        num_scalar_prefetch=1, grid=(r // tr,),
        in_specs=[pl.BlockSpec((tr, c), lambda i, cr: (i, 0))],
        out_specs=pl.BlockSpec((None, tr, c), lambda i, cr: (cr[0], i, 0)))
    return pl.pallas_call(body, grid_spec=grid_spec, out_shape=jax.ShapeDtypeStruct((N_CHIPS, r, c), dtype),
                          compiler_params=_cparams("parallel"), name=name)(chip.astype(jnp.int32).reshape(1), w)


def sum_chips_into_half(name, own, arrived, chip, mc):
    r, c = own.shape[1:]
    tr = _row_tile(r)

    def body(s0, s1, s2, s3, s4, own_ref, a_ref, b_ref, d_ref, o_ref):
        o_ref[...] = ((own_ref[...].astype(f32) + a_ref[...].astype(f32))
                      + (b_ref[...].astype(f32) + d_ref[...].astype(f32)))

    slot = lambda k: pl.BlockSpec((None, tr, c), lambda i, *sc, _k=k: (sc[_k][0], i, 0))
    grid_spec = pltpu.PrefetchScalarGridSpec(
        num_scalar_prefetch=5, grid=(r // tr,), in_specs=[slot(0), slot(1), slot(2), slot(3)],
        out_specs=pl.BlockSpec((None, tr, c), lambda i, *sc: (sc[4][0], i, 0)))
    mx, my = lax.axis_index("x"), lax.axis_index("y")
    scal = [v.astype(jnp.int32).reshape(1) for v in
            (2 * mx + my, 2 * (1 - mx) + my, 2 * mx + (1 - my), 2 * (1 - mx) + (1 - my), mc)]
    return pl.pallas_call(body, grid_spec=grid_spec, out_shape=jax.ShapeDtypeStruct((2, r, c), f32),
                          compiler_params=_cparams("parallel"), name=name)(*scal, own, arrived, arrived, arrived)


def sum_slots(name, x):
    r, c = x.shape[1:]
    tr = _row_tile(r)

    def body(x_ref, o_ref):
        o_ref[...] = (x_ref[0].astype(f32) + x_ref[1].astype(f32)) + (x_ref[2].astype(f32) + x_ref[3].astype(f32))

    return pl.pallas_call(
        body, grid=(r // tr,), in_specs=[pl.BlockSpec((N_CHIPS, tr, c), lambda i: (0, i, 0))],
        out_specs=pl.BlockSpec((tr, c), lambda i: (i, 0)), out_shape=jax.ShapeDtypeStruct((r, c), f32),
        compiler_params=_cparams("parallel"), name=name)(x)


def adamw(name, w, g, m, v):
    r, c = w.shape
    tr = _row_tile(r)

    def body(w_ref, g_ref, m_ref, v_ref, d_ref, nm_ref, nv_ref):
        gv = g_ref[...]
        nm = ADAM_B1 * m_ref[...] + (1.0 - ADAM_B1) * gv
        nv = ADAM_B2 * v_ref[...] + (1.0 - ADAM_B2) * (gv * gv)
        m_hat = nm / (1.0 - ADAM_B1 ** ADAM_STEP)
        v_hat = nv / (1.0 - ADAM_B2 ** ADAM_STEP)
        d_ref[...] = -ADAM_LR * (m_hat / (jnp.sqrt(v_hat) + ADAM_EPS) + ADAM_WD * w_ref[...])
        nm_ref[...] = nm
        nv_ref[...] = nv

    blk = pl.BlockSpec((tr, c), lambda i: (i, 0))
    return pl.pallas_call(
        body, grid=(r // tr,), in_specs=[blk] * 4, out_specs=[blk] * 3,
        out_shape=[jax.ShapeDtypeStruct((r, c), f32)] * 3,
        compiler_params=_cparams("parallel"), name=name)(w, g, m, v)


def _pack(arrs, dtype, row_multiple):
    flat = jnp.concatenate([a.astype(dtype).reshape(-1) for a in arrs])
    unit = PACK_COLS * row_multiple
    total = -(-flat.shape[0] // unit) * unit
    return jnp.pad(flat, (0, total - flat.shape[0])).reshape(-1, PACK_COLS)


def _unpack(packed, shapes):
    flat, out, off = packed.reshape(-1), [], 0
    for shp in shapes:
        n = math.prod(shp)
        out.append(flat[off:off + n].reshape(shp))
        off += n
    return out


def _gathered_to_full(g, axis):
    t = jnp.moveaxis(g, 0, axis)
    return t.reshape(t.shape[:axis] + (t.shape[axis] * t.shape[axis + 1],) + t.shape[axis + 2:])


def _full_to_shard_major(full, axis):
    shp = full.shape
    t = full.reshape(shp[:axis] + (N_CHIPS, shp[axis] // N_CHIPS) + shp[axis + 1:])
    return jnp.moveaxis(t, axis, 0)


GRAD_ROW_MULTIPLE = 256


def kernel(x, mem, positions, hyb_w_in, dn_conv_w, dn_a_log, dn_dt_bias, dn_norm_g, hyb_w_out, s5_a_re, s5_a_im, s5_log_dt, s5_b_re, s5_b_im, s5_c_re, s5_c_im, s5_d, s5_glu_wo, s5_glu_wg, ln_mix_g, ln_mix_b, xq_w, xk_w, xv_w, xo_w, ln_x_g, ln_x_b, ffn_wg, ffn_wu, ffn_wd, ln_ffn_g, ln_ffn_b, loss_target, m_hyb_w_in, m_dn_conv_w, m_dn_a_log, m_dn_dt_bias, m_dn_norm_g, m_hyb_w_out, m_s5_a_re, m_s5_a_im, m_s5_log_dt, m_s5_b_re, m_s5_b_im, m_s5_c_re, m_s5_c_im, m_s5_d, m_s5_glu_wo, m_s5_glu_wg, m_ln_mix_g, m_ln_mix_b, m_xq_w, m_xk_w, m_xv_w, m_xo_w, m_ln_x_g, m_ln_x_b, m_ffn_wg, m_ffn_wu, m_ffn_wd, m_ln_ffn_g, m_ln_ffn_b, v_hyb_w_in, v_dn_conv_w, v_dn_a_log, v_dn_dt_bias, v_dn_norm_g, v_hyb_w_out, v_s5_a_re, v_s5_a_im, v_s5_log_dt, v_s5_b_re, v_s5_b_im, v_s5_c_re, v_s5_c_im, v_s5_d, v_s5_glu_wo, v_s5_glu_wg, v_ln_mix_g, v_ln_mix_b, v_xq_w, v_xk_w, v_xv_w, v_xo_w, v_ln_x_g, v_ln_x_b, v_ffn_wg, v_ffn_wu, v_ffn_wd, v_ln_ffn_g, v_ln_ffn_b):
    a = dict(locals())
    big = [n for n in WEIGHT_NAMES if n in SHARD_AXIS and n not in GATHER_F32]
    small = [n for n in WEIGHT_NAMES if n not in big]
    chip = 2 * lax.axis_index("x") + lax.axis_index("y")

    mc = lax.axis_index("c")
    view2 = lambda t: t.reshape(-1, t.shape[-1])
    slots = [cast_into_slot("slot_" + n, view2(a[n]), chip, bf16).reshape((N_CHIPS,) + a[n].shape) for n in big]
    gathered = comm_gather_weights("gather_w", slots)
    tiny4 = _unpack_slots(comm_allgather4("gather_w_tiny", _pack([a[n] for n in GATHER_F32], f32, 8)),
                          [a[n].shape for n in GATHER_F32])
    p = {n: a[n] for n in small if n not in GATHER_F32}
    for n, g4 in zip(GATHER_F32, tiny4):
        p[n] = _gathered_to_full(g4, SHARD_AXIS[n])
    for n, g4 in zip(big, gathered):
        p[n] = g4 if n in SHARD_ORDER_GRADS else _gathered_to_full(g4, SHARD_AXIS[n])

    loss, grad_x, grads = local_step(x[0], mem[0], positions, loss_target[0], p)
    loss = lax.psum(loss, ("x", "y", "c"))

    g4s = [grads[n] if n in SHARD_ORDER_GRADS else _full_to_shard_major(grads[n], SHARD_AXIS[n]) for n in big]
    recv = comm_sibling_halves("rs_sibling_halves", g4s)
    pairs = []
    for n, g4, r4 in zip(big, g4s, recv):
        lh, cols = g4.shape[1] // 2, g4.shape[-1]
        v4 = g4.reshape(N_CHIPS, 2, -1, cols)
        pairs.append(add_own_half("rs_add_" + n, v4, r4.reshape(N_CHIPS, -1, cols), bf16).reshape((N_CHIPS, lh) + g4.shape[2:]))
    arrived = comm_alltoall4("rs_alltoall", pairs)
    slot3 = lambda t: t.reshape(N_CHIPS, -1, t.shape[-1])
    halves = [sum_chips_into_half("rs_sum_" + n, slot3(pr), slot3(ar), chip, mc) for n, pr, ar in zip(big, pairs, arrived)]
    g_big = {n: t.reshape(a[n].shape) for n, t in zip(big, comm_sibling_join("rs_sibling_join", halves))}

    rpack = _pack([grads[n] for n in small], f32, 8)
    rpair = _add2("ar_add_sibling", rpack, comm_sibling_swap("ar_sibling_swap", rpack))
    g_small = _unpack(sum_slots("ar_sum_chips", comm_allgather4("ar_allgather", rpair)), [grads[n].shape for n in small])
    g_small = {n: (lax.dynamic_index_in_dim(_full_to_shard_major(g, SHARD_AXIS[n]), chip, 0, keepdims=False)
                   if n in SHARD_AXIS else g) for n, g in zip(small, g_small)}

    outs = {}
    for n in big:
        view = lambda t: t.reshape(-1, t.shape[-1])
        d, nm, nv = adamw("adamw_" + n, view(a[n]), view(g_big[n]), view(a["m_" + n]), view(a["v_" + n]))
        outs[n] = (g_big[n],) + tuple(t.reshape(a[n].shape) for t in (d, nm, nv))
    shapes = [a[n].shape for n in small]
    packs = [_pack([a[pre + n] for n in small], f32, 8) for pre in ("", "m_", "v_")]
    upd = adamw("adamw_small", packs[0], _pack([g_small[n] for n in small], f32, 8), packs[1], packs[2])
    for k, n in enumerate(small):
        outs[n] = (g_small[n],) + tuple(_unpack(buf, shapes)[k] for buf in upd)
    res = [loss, grad_x[None]]
    for kind in range(4):
        res += [outs[n][kind] for n in WEIGHT_NAMES]
    return tuple(res)


def _unpack_slots(gathered, shapes):
    flat, out, off = gathered.reshape(N_CHIPS, -1), [], 0
    for shp in shapes:
        n = math.prod(shp)
        out.append(flat[:, off:off + n].reshape((N_CHIPS,) + tuple(shp)))
        off += n
    return out
```

```python
import functools
import math

import jax
import jax.numpy as jnp
from jax import lax
from jax.experimental import pallas as pl
from jax.experimental.pallas import tpu as pltpu

f32 = jnp.float32
bf16 = jnp.bfloat16

D_MODEL = 1024
DEPTH = 4
DN_HEADS = 4
DN_HEAD_DIM = 128
DN_KEY_DIM = 512
DN_QKV_DIM = 1536
DN_CONV = 4
SW_HEADS = 8
SW_HEAD_DIM = 64
SW_DIM = 512
SW_DILATIONS = (1, 4, 16)
SW_BLOCK = 128
ROPE_THETA = 10000.0
S5_GROUP = 16
S5_GROUPS = 64
S5_STATE = 64
X_HEADS = 4
X_HEAD_DIM = 256
FFN_HIDDEN = 2816
ALPHA = (2 * DEPTH) ** 0.25
LN_EPS = 1e-5
RMS_EPS = 1e-6
ADAM_LR, ADAM_B1, ADAM_B2, ADAM_EPS, ADAM_WD, ADAM_STEP = 0.001, 0.9, 0.999, 1e-08, 0.01, 10

BA_PAD = 256
PROJ_COLS = DN_QKV_DIM + DN_KEY_DIM + 3 * SW_DIM + BA_PAD
COL_Z = DN_QKV_DIM
COL_SWQ = COL_Z + DN_KEY_DIM
COL_SWK = COL_SWQ + SW_DIM
COL_SWV = COL_SWK + SW_DIM
COL_BA = COL_SWV + SW_DIM

LANES = 128
SUBLANES = 8
VMEM_LIMIT = 56 * 1024 * 1024
DN_CHUNK = 128
DN_HEADS_PER_STEP = 4


def _cparams(*sem):
    return pltpu.CompilerParams(dimension_semantics=tuple(sem), vmem_limit_bytes=VMEM_LIMIT)


def _dg(x, y, cx, cy):
    return lax.dot_general(x, y, (((cx,), (cy,)), ((), ())), preferred_element_type=f32)


@functools.partial(jax.custom_vjp, nondiff_argnums=(2, 3))
def bdot(a, b, ca, cb):
    return _dg(a.astype(bf16), b.astype(bf16), ca, cb)


def _bdot_fwd(a, b, ca, cb):
    return bdot(a, b, ca, cb), (a, b)


def _bdot_bwd(ca, cb, res, g):
    a, b = res
    g16, a16, b16 = g.astype(bf16), a.astype(bf16), b.astype(bf16)
    da = _dg(g16, b16, 1, 1 - cb) if ca == 1 else _dg(b16, g16, 1 - cb, 1)
    db = _dg(a16, g16, 1 - ca, 0) if cb == 0 else _dg(g16, a16, 0, 1 - ca)
    return da.astype(a.dtype), db.astype(b.dtype)


bdot.defvjp(_bdot_fwd, _bdot_bwd)


def _split_hi_lo(a):
    hi = a.astype(bf16)
    return hi, (a - hi.astype(f32)).astype(bf16)


def _dot3(a, b, ca, cb):
    a_hi, a_lo = _split_hi_lo(a)
    b_hi, b_lo = _split_hi_lo(b)
    return _dg(a_hi, b_hi, ca, cb) + (_dg(a_hi, b_lo, ca, cb) + _dg(a_lo, b_hi, ca, cb))


@jax.custom_vjp
def hdot3(a, b):
    return _dot3(a, b, 1, 0)


def _hdot3_fwd(a, b):
    return hdot3(a, b), (a, b)


def _hdot3_bwd(res, g):
    a, b = res
    return _dot3(g, b, 1, 1), _dot3(a, g, 0, 0)


hdot3.defvjp(_hdot3_fwd, _hdot3_bwd)


def hdot(a, b):
    return jnp.dot(a, b, precision=lax.Precision.HIGHEST, preferred_element_type=f32)


def _iota2(shape, dim):
    return lax.broadcasted_iota(jnp.int32, shape, dim)


def _row_spec(r, tm):
    if isinstance(r, tuple):
        arr, width, blk = r
        return arr, pl.BlockSpec((tm, width), lambda i, _b=blk: (i, _b))
    return r, pl.BlockSpec((tm, r.shape[1]), lambda i: (i, 0))


def _par_spec(p):
    return pl.BlockSpec(p.shape, lambda i, _n=p.ndim: (0,) * _n)


def rowmap(name, fn, rows, params, out_cols, tm, out_dtypes=None):
    arrs, specs = zip(*[_row_spec(r, tm) for r in rows])
    s = arrs[0].shape[0]
    n_in = len(rows) + len(params)
    out_dtypes = out_dtypes or [f32] * len(out_cols)

    def body(*refs):
        outs = fn(*[r[...] for r in refs[:n_in]])
        for o_ref, o in zip(refs[n_in:], outs):
            o_ref[...] = o.astype(o_ref.dtype)

    return pl.pallas_call(
        body, grid=(s // tm,),
        in_specs=list(specs) + [_par_spec(p) for p in params],
        out_specs=[pl.BlockSpec((tm, c), lambda i: (i, 0)) for c in out_cols],
        out_shape=[jax.ShapeDtypeStruct((s, c), dt) for c, dt in zip(out_cols, out_dtypes)],
        compiler_params=_cparams("parallel"), name=name)(*arrs, *params)


def rowmap_bwd(name, fn, rows, params, cts, tm, row_mask=None, par_mask=None, row_dtypes=None):
    arrs, specs = zip(*[_row_spec(r, tm) for r in rows])
    s = arrs[0].shape[0]
    ct_groups = [c if isinstance(c, list) else [c] for c in cts]
    ct_arrs, ct_specs = zip(*[_row_spec(a, tm) for grp in ct_groups for a in grp])
    cts = list(ct_arrs)
    nr, npar, nct = len(rows), len(params), len(cts)
    row_mask = row_mask or [True] * nr
    par_mask = par_mask or [True] * npar
    row_idx = [k for k in range(nr) if row_mask[k]]
    par_idx = [k for k in range(npar) if par_mask[k]]
    row_w = [specs[k].block_shape[1] for k in row_idx]

    def body(*refs):
        ins = [r[...] for r in refs[:nr + npar]]
        ct_refs = list(refs[nr + npar:nr + npar + nct])
        ctv = []
        for grp in ct_groups:
            acc = ct_refs.pop(0)[...].astype(f32)
            for _ in grp[1:]:
                acc = acc + ct_refs.pop(0)[...].astype(f32)
            ctv.append(acc)
        ctv = tuple(ctv)
        outs = refs[nr + npar + nct:]
        _, vjp = jax.vjp(fn, *ins)
        grads = vjp(ctv)
        for o_ref, k in zip(outs[:len(row_idx)], row_idx):
            o_ref[...] = grads[k].astype(o_ref.dtype)
        first = pl.program_id(0) == 0
        for o_ref, k in zip(outs[len(row_idx):], par_idx):
            g = grads[nr + k].astype(f32)

            @pl.when(first)
            def _(o_ref=o_ref, g=g):
                o_ref[...] = g

            @pl.when(jnp.logical_not(first))
            def _(o_ref=o_ref, g=g):
                o_ref[...] += g

    res = pl.pallas_call(
        body, grid=(s // tm,),
        in_specs=list(specs) + [_par_spec(p) for p in params]
        + list(ct_specs),
        out_specs=[pl.BlockSpec((tm, w), lambda i: (i, 0)) for w in row_w]
        + [_par_spec(params[k]) for k in par_idx],
        out_shape=[jax.ShapeDtypeStruct((s, w), dt) for w, dt in zip(row_w, row_dtypes or [f32] * len(row_w))]
        + [jax.ShapeDtypeStruct(params[k].shape, f32) for k in par_idx],
        compiler_params=_cparams("arbitrary"), name=name)(*arrs, *params, *cts)
    return list(res[:len(row_idx)]), list(res[len(row_idx):])


def _pick(n, prefs):
    for t in prefs:
        if n % t == 0:
            return t
    return n


def mm_nn(name, a, b, out_dtype=f32):
    m, k = a.shape
    n = b.shape[1]
    tm, tn = _pick(m, (512, 256, 128)), _pick(n, (512, 256, 128))

    def body(a_ref, b_ref, o_ref):
        o_ref[...] = _dg(a_ref[...].astype(bf16), b_ref[...].astype(bf16), 1, 0).astype(o_ref.dtype)

    return pl.pallas_call(
        body, grid=(m // tm, n // tn),
        in_specs=[pl.BlockSpec((tm, k), lambda i, j: (i, 0)), pl.BlockSpec((k, tn), lambda i, j: (0, j))],
        out_specs=pl.BlockSpec((tm, tn), lambda i, j: (i, j)),
        out_shape=jax.ShapeDtypeStruct((m, n), out_dtype),
        compiler_params=_cparams("parallel", "parallel"), name=name)(a, b)


def mm_nt(name, a, b, out_dtype=f32):
    m, n = a.shape
    k = b.shape[0]
    tm, tk = _pick(m, (256, 128)), _pick(k, (512, 256, 128))

    def body(a_ref, b_ref, o_ref):
        o_ref[...] = _dg(a_ref[...].astype(bf16), b_ref[...].astype(bf16), 1, 1).astype(o_ref.dtype)

    return pl.pallas_call(
        body, grid=(m // tm, k // tk),
        in_specs=[pl.BlockSpec((tm, n), lambda i, j: (i, 0)), pl.BlockSpec((tk, n), lambda i, j: (j, 0))],
        out_specs=pl.BlockSpec((tm, tk), lambda i, j: (i, j)),
        out_shape=jax.ShapeDtypeStruct((m, k), out_dtype),
        compiler_params=_cparams("parallel", "parallel"), name=name)(a, b)


def mm_tn(name, a, b, out_dtype=f32):
    s, m = a.shape
    n = b.shape[1]
    tm, tn = _pick(m, (256, 128)), _pick(n, (256, 128))

    def body(a_ref, b_ref, o_ref):
        o_ref[...] = _dg(a_ref[...].astype(bf16), b_ref[...].astype(bf16), 0, 0).astype(o_ref.dtype)

    return pl.pallas_call(
        body, grid=(m // tm, n // tn),
        in_specs=[pl.BlockSpec((s, tm), lambda i, j: (0, i)), pl.BlockSpec((s, tn), lambda i, j: (0, j))],
        out_specs=pl.BlockSpec((tm, tn), lambda i, j: (i, j)),
        out_shape=jax.ShapeDtypeStruct((m, n), out_dtype),
        compiler_params=_cparams("parallel", "parallel"), name=name)(a, b)


def _postnorm_tile(h, sub, g, b):
    z = ALPHA * h + sub
    mu = jnp.mean(z, -1, keepdims=True)
    zc = z - mu
    var = jnp.mean(zc * zc, -1, keepdims=True)
    return (zc * lax.rsqrt(var + LN_EPS) * g + b,)


def _swiglu_tile(au):
    a, u = au[:, :FFN_HIDDEN], au[:, FFN_HIDDEN:]
    return (jax.nn.silu(a) * u,)


def _glu_tile(og):
    o, g = og[:, :D_MODEL], og[:, D_MODEL:]
    return (o * jax.nn.sigmoid(g),)


def _xattn_tile(q, kv):
    outs = []
    for h in range(X_HEADS):
        sl = slice(h * X_HEAD_DIM, (h + 1) * X_HEAD_DIM)
        s = bdot(q[:, sl], kv[:, sl], 1, 1) * (X_HEAD_DIM ** -0.5)
        m = lax.stop_gradient(jnp.max(s, -1, keepdims=True))
        p = jnp.exp(s - m)
        p = p / jnp.sum(p, -1, keepdims=True)
        outs.append(bdot(p, kv[:, D_MODEL + h * X_HEAD_DIM:D_MODEL + (h + 1) * X_HEAD_DIM], 1, 0))
    return (jnp.concatenate(outs, -1),)


TM_ROW = 256


def postnorm_fwd(tag, h, sub, g, b):
    return rowmap("postnorm_" + tag, lambda *a: _postnorm_tile(*a) * 2, [h, sub], [g, b], [D_MODEL] * 2, TM_ROW,
                  out_dtypes=[f32, bf16])


def postnorm_bwd(tag, h, sub, g, b, dy):
    (dh, dsub), (dg, db) = rowmap_bwd("postnorm_bwd_" + tag, _postnorm_tile, [h, sub], [g, b], [dy], TM_ROW,
                                      row_dtypes=[f32, bf16])
    return dh, dsub, dg, db


def xattn_fwd(tag, h, mem, wq, wkv, wo):
    q = mm_nn("xq_" + tag, h, wq, out_dtype=bf16)
    kv = mm_nn("xkv_" + tag, mem, wkv)
    ao = rowmap("xattn_" + tag, _xattn_tile, [q], [kv], [D_MODEL], TM_ROW, out_dtypes=[bf16])[0]
    out = mm_nn("xo_" + tag, ao, wo)
    return out, (q, kv, ao)


def xattn_bwd(tag, h, mem, wq, wkv, wo, res, dout):
    q, kv, ao = res
    dwo = mm_tn("xo_dw_" + tag, ao, dout)
    dao = mm_nt("xo_dx_" + tag, dout, wo)
    (dq,), (dkv,) = rowmap_bwd("xattn_bwd_" + tag, _xattn_tile, [q], [kv], [dao], TM_ROW, row_dtypes=[bf16])
    dwq = mm_tn("xq_dw_" + tag, h, dq)
    dh = mm_nt("xq_dx_" + tag, dq, wq)
    dwkv = mm_tn("xkv_dw_" + tag, mem, dkv)
    return dh, dwq, dwkv, dwo


FFN_SHARD = FFN_HIDDEN // 4
TM_FFN = 512


def _silu_mul(a, u):
    return jax.nn.silu(a) * u


def ffn_fwd(tag, layer, h, wg, wu, wd):
    s = h.shape[0]
    tm, fs = TM_FFN, FFN_SHARD
    w_in = pl.BlockSpec((None, None, fs, D_MODEL), lambda i, k: (k, layer, 0, 0))
    act = pl.BlockSpec((None, tm, fs), lambda i, k: (k, i, 0))

    def up_body(h_ref, wg_ref, wu_ref, a_ref, u_ref, hid_ref):
        hv = h_ref[...].astype(bf16)
        a, u = _dg(hv, wg_ref[...], 1, 1), _dg(hv, wu_ref[...], 1, 1)
        a_ref[...], u_ref[...] = a.astype(bf16), u.astype(bf16)
        hid_ref[...] = _silu_mul(a, u).astype(bf16)

    a4, u4, hid4 = pl.pallas_call(
        up_body, grid=(s // tm, 4),
        in_specs=[pl.BlockSpec((tm, D_MODEL), lambda i, k: (i, 0)), w_in, w_in],
        out_specs=[act, act, act],
        out_shape=[jax.ShapeDtypeStruct((4, s, fs), bf16)] * 3,
        compiler_params=_cparams("parallel", "parallel"), name="ffn_up_" + tag)(h, wg, wu)

    def down_body(hid_ref, wd_ref, o_ref):
        part = _dg(hid_ref[...], wd_ref[...], 1, 0)

        @pl.when(pl.program_id(1) == 0)
        def _():
            o_ref[...] = part

        @pl.when(pl.program_id(1) != 0)
        def _():
            o_ref[...] += part

    out = pl.pallas_call(
        down_body, grid=(s // tm, 4),
        in_specs=[act, pl.BlockSpec((None, None, fs, D_MODEL), lambda i, k: (k, layer, 0, 0))],
        out_specs=pl.BlockSpec((tm, D_MODEL), lambda i, k: (i, 0)),
        out_shape=jax.ShapeDtypeStruct((s, D_MODEL), f32),
        compiler_params=_cparams("parallel", "arbitrary"), name="ffn_down_" + tag)(hid4, wd)
    return out, (a4, u4, hid4)


def ffn_bwd(tag, layer, h, wg, wu, wd, res, dout):
    a4, u4, hid4 = res
    s = h.shape[0]
    tm, fs = TM_FFN, FFN_SHARD
    act = pl.BlockSpec((None, tm, fs), lambda i, k: (k, i, 0))
    w_in = pl.BlockSpec((None, None, fs, D_MODEL), lambda i, k: (k, layer, 0, 0))

    def dact_body(do_ref, wd_ref, a_ref, u_ref, da_ref, du_ref):
        dhid = _dg(do_ref[...].astype(bf16), wd_ref[...], 1, 1)
        _, vjp = jax.vjp(_silu_mul, a_ref[...].astype(f32), u_ref[...].astype(f32))
        da, du = vjp(dhid)
        da_ref[...], du_ref[...] = da.astype(bf16), du.astype(bf16)

    da4, du4 = pl.pallas_call(
        dact_body, grid=(s // tm, 4),
        in_specs=[pl.BlockSpec((tm, D_MODEL), lambda i, k: (i, 0)),
                  pl.BlockSpec((None, None, fs, D_MODEL), lambda i, k: (k, layer, 0, 0)), act, act],
        out_specs=[act, act], out_shape=[jax.ShapeDtypeStruct((4, s, fs), bf16)] * 2,
        compiler_params=_cparams("parallel", "parallel"), name="ffn_dact_" + tag)(dout, wd, a4, u4)

    def dx_body(da_ref, du_ref, wg_ref, wu_ref, o_ref):
        part = _dg(da_ref[...], wg_ref[...], 1, 0) + _dg(du_ref[...], wu_ref[...], 1, 0)

        @pl.when(pl.program_id(1) == 0)
        def _():
            o_ref[...] = part

        @pl.when(pl.program_id(1) != 0)
        def _():
            o_ref[...] += part

    dh = pl.pallas_call(
        dx_body, grid=(s // tm, 4), in_specs=[act, act, w_in, w_in],
        out_specs=pl.BlockSpec((tm, D_MODEL), lambda i, k: (i, 0)),
        out_shape=jax.ShapeDtypeStruct((s, D_MODEL), f32),
        compiler_params=_cparams("parallel", "arbitrary"), name="ffn_dx_" + tag)(da4, du4, wg, wu)

    tn = 256
    whole = pl.BlockSpec((None, s, fs), lambda k, j: (k, 0, 0))

    def dwin_body(h_ref, da_ref, du_ref, dwg_ref, dwu_ref):
        hv = h_ref[...].astype(bf16)
        dwg_ref[...] = _dg(da_ref[...], hv, 0, 0)
        dwu_ref[...] = _dg(du_ref[...], hv, 0, 0)

    dwg, dwu = pl.pallas_call(
        dwin_body, grid=(4, D_MODEL // tn),
        in_specs=[pl.BlockSpec((s, tn), lambda k, j: (0, j)), whole, whole],
        out_specs=[pl.BlockSpec((None, fs, tn), lambda k, j: (k, 0, j))] * 2,
        out_shape=[jax.ShapeDtypeStruct((4, fs, D_MODEL), f32)] * 2,
        compiler_params=_cparams("parallel", "parallel"), name="ffn_dwin_" + tag)(h, da4, du4)

    def dwd_body(hid_ref, do_ref, dwd_ref):
        dwd_ref[...] = _dg(hid_ref[...], do_ref[...].astype(bf16), 0, 0)

    dwd = pl.pallas_call(
        dwd_body, grid=(4, D_MODEL // tn),
        in_specs=[whole, pl.BlockSpec((s, tn), lambda k, j: (0, j))],
        out_specs=pl.BlockSpec((None, fs, tn), lambda k, j: (k, 0, j)),
        out_shape=jax.ShapeDtypeStruct((4, fs, D_MODEL), f32),
        compiler_params=_cparams("parallel", "parallel"), name="ffn_dwd_" + tag)(hid4, dout)
    return dh, dwg, dwu, dwd


def loss_head(y, target):
    s, d = y.shape
    tm = TM_ROW

    def body(y_ref, t_ref, part_ref, dy_ref):
        e = y_ref[...] - t_ref[...]
        dy_ref[...] = e * (1.0 / d)
        p = jnp.sum(e * e, 0, keepdims=True) * (0.5 / d)

        @pl.when(pl.program_id(0) == 0)
        def _():
            part_ref[...] = p

        @pl.when(pl.program_id(0) != 0)
        def _():
            part_ref[...] += p

    return pl.pallas_call(
        body, grid=(s // tm,),
        in_specs=[pl.BlockSpec((tm, d), lambda i: (i, 0))] * 2,
        out_specs=[pl.BlockSpec((1, d), lambda i: (0, 0)), pl.BlockSpec((tm, d), lambda i: (i, 0))],
        out_shape=[jax.ShapeDtypeStruct((1, d), f32), jax.ShapeDtypeStruct((s, d), f32)],
        compiler_params=_cparams("arbitrary"), name="loss_head")(y, target)


TM_CONV = 512


def _conv_rows(xx, w_ref, n_rows):
    a = w_ref[3:4, :] * xx
    for k in (1, 2, 3):
        a = a + w_ref[3 - k:4 - k, :] * pltpu.roll(xx, k, 0)
    return a


def _dn_act(a, is_qk):
    s = jax.nn.silu(a)
    n = s * lax.rsqrt(jnp.sum(s * s, -1, keepdims=True) + RMS_EPS)
    return jnp.where(is_qk, n, s)


def dn_conv_fwd(tag, proj, cw):
    s = proj.shape[0]
    tm, hb = TM_CONV, TM_CONV // SUBLANES

    def body(xh_ref, x_ref, w_ref, o_ref):
        j, t = pl.program_id(0), pl.program_id(1)
        halo = jnp.where(t > 0, xh_ref[...], 0.0)
        xx = jnp.concatenate([halo, x_ref[...]], 0)
        a = _conv_rows(xx, w_ref, tm + SUBLANES)
        o_ref[...] = _dn_act(a, j < 2 * DN_HEADS)[SUBLANES:, :]

    return pl.pallas_call(
        body, grid=(DN_QKV_DIM // LANES, s // tm),
        in_specs=[pl.BlockSpec((SUBLANES, LANES), lambda j, t: (jnp.maximum(t * hb - 1, 0), j)),
                  pl.BlockSpec((tm, LANES), lambda j, t: (t, j)),
                  pl.BlockSpec((DN_CONV, LANES), lambda j, t: (0, j))],
        out_specs=pl.BlockSpec((tm, LANES), lambda j, t: (t, j)),
        out_shape=jax.ShapeDtypeStruct((s, DN_QKV_DIM), f32),
        compiler_params=_cparams("parallel", "parallel"), name="dn_conv_" + tag)(proj, proj, cw)


def dn_conv_bwd(tag, proj, cw, dy):
    s = proj.shape[0]
    tm, hb = TM_CONV, TM_CONV // SUBLANES
    nt = s // tm
    n_ext = tm + 2 * SUBLANES

    def body(xb_ref, x_ref, xa_ref, dy_ref, dya_ref, w_ref, dx_ref, dw_ref):
        j, t = pl.program_id(0), pl.program_id(1)
        xx = jnp.concatenate([jnp.where(t > 0, xb_ref[...], 0.0), x_ref[...],
                              jnp.where(t < nt - 1, xa_ref[...], 0.0)], 0)
        dyy = jnp.concatenate([jnp.zeros((SUBLANES, LANES), f32), dy_ref[...],
                               jnp.where(t < nt - 1, dya_ref[...], 0.0)], 0)
        a = _conv_rows(xx, w_ref, n_ext)
        _, vjp = jax.vjp(lambda v: _dn_act(v, j < 2 * DN_HEADS), a)
        da, = vjp(dyy)
        dx = w_ref[3:4, :] * da
        for k in (1, 2, 3):
            dx = dx + w_ref[3 - k:4 - k, :] * pltpu.roll(da, n_ext - k, 0)
        dx_ref[...] = dx[SUBLANES:SUBLANES + tm, :]
        row = _iota2((n_ext, LANES), 0)
        da_in = jnp.where((row >= SUBLANES) & (row < SUBLANES + tm), da, 0.0)
        r8 = _iota2((SUBLANES, LANES), 0)
        dw = jnp.zeros((SUBLANES, LANES), f32)
        for k in range(DN_CONV):
            xs = xx if k == 0 else pltpu.roll(xx, k, 0)
            dw = dw + jnp.where(r8 == 3 - k, jnp.sum(da_in * xs, 0, keepdims=True), 0.0)

        @pl.when(t == 0)
        def _():
            dw_ref[...] = dw

        @pl.when(t != 0)
        def _():
            dw_ref[...] += dw

    nb8 = s // SUBLANES
    return pl.pallas_call(
        body, grid=(DN_QKV_DIM // LANES, nt),
        in_specs=[pl.BlockSpec((SUBLANES, LANES), lambda j, t: (jnp.maximum(t * hb - 1, 0), j)),
                  pl.BlockSpec((tm, LANES), lambda j, t: (t, j)),
                  pl.BlockSpec((SUBLANES, LANES), lambda j, t: (jnp.minimum((t + 1) * hb, nb8 - 1), j)),
                  pl.BlockSpec((tm, LANES), lambda j, t: (t, j)),
                  pl.BlockSpec((SUBLANES, LANES), lambda j, t: (jnp.minimum((t + 1) * hb, nb8 - 1), j)),
                  pl.BlockSpec((DN_CONV, LANES), lambda j, t: (0, j))],
        out_specs=[pl.BlockSpec((tm, LANES), lambda j, t: (t, j)),
                   pl.BlockSpec((SUBLANES, LANES), lambda j, t: (0, j))],
        out_shape=[jax.ShapeDtypeStruct((s, DN_QKV_DIM), f32), jax.ShapeDtypeStruct((SUBLANES, DN_QKV_DIM), f32)],
        compiler_params=_cparams("parallel", "arbitrary"), name="dn_conv_bwd_" + tag)(proj, proj, proj, dy, dy, cw)


def _gate_tile(ba, eb, ea, alog, dtb):
    beta = jax.nn.sigmoid(hdot(ba, eb))
    g = -jnp.exp(alog) * jax.nn.softplus(hdot(ba, ea) + dtb)
    return beta, g


def _each(fn, *lists):
    return [fn(*args) for args in zip(*lists)]


@functools.partial(jax.custom_vjp, nondiff_argnums=(1,))
def _halves(x, axis):
    h = x.shape[axis] // 2
    return (x[:h], x[h:]) if axis == 0 else (x[:, :h], x[:, h:])


def _halves_fwd(x, axis):
    return _halves(x, axis), None


def _halves_bwd(axis, _, g):
    return (jnp.concatenate(g, axis),)


_halves.defvjp(_halves_fwd, _halves_bwd)


def _tri_inv_unit(lowers):
    c = lowers[0].shape[0]
    r, col = _iota2((c, c), 0), _iota2((c, c), 1)
    eye = jnp.where(r == col, 1.0, 0.0).astype(f32)
    invs = None
    sh = 0
    while (1 << sh) < c:
        same_2b = lax.shift_right_logical(r, sh + 1) == lax.shift_right_logical(col, sh + 1)
        diff_b = lax.shift_right_logical(r, sh) != lax.shift_right_logical(col, sh)
        offs = [jnp.where(same_2b & diff_b, low, 0.0) for low in lowers]
        if invs is None:
            invs = [eye - off for off in offs]
        else:
            part = _each(hdot, invs, offs)
            invs = _each(lambda inv, p: inv - hdot(p, inv), invs, part)
        sh += 1
    return invs


def _delta_chunk(q, k, v, gb, betab, state):
    c, hd = DN_CHUNK, DN_HEAD_DIM
    r, col = _iota2((c, c), 0), _iota2((c, c), 1)
    causal, strict = r >= col, r > col
    tril = jnp.where(causal, 1.0, 0.0).astype(f32)
    gc = _each(lambda g: hdot(tril, g), gb)
    decay = _each(lambda g: jnp.where(causal, jnp.exp(jnp.where(causal, g - g.T, 0.0)), 0.0), gc)
    qs = _each(lambda t: t * (DN_HEAD_DIM ** -0.5), q)
    kb = _each(lambda a, b: a * b, k, betab)
    kq = _each(lambda a, b, kk: _halves(bdot(jnp.concatenate([a, b], 0), kk, 1, 1), 0), kb, qs, k)
    lower = _each(lambda x, d: jnp.where(strict, x[0], 0.0) * d, kq, decay)
    intra = _each(lambda x, d: x[1] * d, kq, decay)
    tinv = _tri_inv_unit(lower)
    eg = _each(jnp.exp, gc)
    uw = _each(lambda t, vv, b, kb_, e: _halves(hdot(t, jnp.concatenate([vv * b, kb_ * e], 1)), 1),
               tinv, v, betab, kb, eg)
    gl = _each(lambda g: jnp.sum(jnp.where(r == c - 1, g, 0.0), 0, keepdims=True), gc)
    k_dec = _each(lambda kk, a, g: kk * jnp.exp(a - g), k, gl, gc)
    ws = _each(lambda x, t, e, st: _halves(bdot(jnp.concatenate([x[1], t * e], 0), st, 1, 0), 0), uw, qs, eg, state)
    v_new = _each(lambda x, y: x[0] - y[0], uw, ws)
    out = _each(lambda y, a, vn: y[1] + bdot(a, vn, 1, 0), ws, intra, v_new)
    new_state = _each(lambda st, a, kd, vn: st * jnp.exp(a) + bdot(kd, vn, 0, 0), state, gl, k_dec, v_new)
    return tuple(out), tuple(new_state)


def delta_fwd(tag, qkv, gb, betab):
    s = qkv.shape[0]
    c, hd = DN_CHUNK, DN_HEAD_DIM
    n = s // c

    hg, ng = DN_HEADS_PER_STEP, DN_HEADS // DN_HEADS_PER_STEP

    def body(q_ref, k_ref, v_ref, g_ref, b_ref, o_ref, st_ref, state):
        @pl.when(pl.program_id(1) == 0)
        def _():
            state[...] = jnp.zeros_like(state)

        heads = lambda ref: tuple(ref[:, j * hd:(j + 1) * hd] for j in range(hg))
        st = tuple(state[j] for j in range(hg))
        outs, news = _delta_chunk(heads(q_ref), heads(k_ref), heads(v_ref), heads(g_ref), heads(b_ref), st)
        for j in range(hg):
            st_ref[j] = st[j]
            o_ref[:, j * hd:(j + 1) * hd] = outs[j]
            state[j] = news[j]

    blk = lambda off: pl.BlockSpec((c, hg * hd), lambda h, i, _o=off: (i, h + _o))
    return pl.pallas_call(
        body, grid=(ng, n),
        in_specs=[blk(0), blk(ng), blk(2 * ng), blk(0), blk(0)],
        out_specs=[blk(0), pl.BlockSpec((hg, None, hd, hd), lambda h, i: (h, i, 0, 0))],
        out_shape=[jax.ShapeDtypeStruct((s, DN_KEY_DIM), f32), jax.ShapeDtypeStruct((DN_HEADS, n, hd, hd), f32)],
        scratch_shapes=[pltpu.VMEM((hg, hd, hd), f32)],
        compiler_params=_cparams("parallel", "arbitrary"), name="delta_" + tag)(qkv, qkv, qkv, gb, betab)


def delta_bwd(tag, qkv, gb, betab, states, do):
    s = qkv.shape[0]
    c, hd = DN_CHUNK, DN_HEAD_DIM
    n = s // c

    hg, ng = DN_HEADS_PER_STEP, DN_HEADS // DN_HEADS_PER_STEP

    def body(q_ref, k_ref, v_ref, g_ref, b_ref, st_ref, do_ref, dq_ref, dk_ref, dv_ref, dg_ref, db_ref, dstate):
        @pl.when(pl.program_id(1) == 0)
        def _():
            dstate[...] = jnp.zeros_like(dstate)

        heads = lambda ref: tuple(ref[:, j * hd:(j + 1) * hd] for j in range(hg))
        _, vjp = jax.vjp(_delta_chunk, heads(q_ref), heads(k_ref), heads(v_ref), heads(g_ref), heads(b_ref),
                         tuple(st_ref[j] for j in range(hg)))
        grads = vjp((heads(do_ref), tuple(dstate[j] for j in range(hg))))
        for ref, g in zip((dq_ref, dk_ref, dv_ref, dg_ref, db_ref), grads[:5]):
            for j in range(hg):
                ref[:, j * hd:(j + 1) * hd] = g[j]
        for j in range(hg):
            dstate[j] = grads[5][j]

    blk = lambda off: pl.BlockSpec((c, hg * hd), lambda h, i, _o=off: (n - 1 - i, h + _o))
    return pl.pallas_call(
        body, grid=(ng, n),
        in_specs=[blk(0), blk(ng), blk(2 * ng), blk(0), blk(0),
                  pl.BlockSpec((hg, None, hd, hd), lambda h, i: (h, n - 1 - i, 0, 0)), blk(0)],
        out_specs=[blk(0)] * 5,
        out_shape=[jax.ShapeDtypeStruct((s, DN_KEY_DIM), f32)] * 5,
        scratch_shapes=[pltpu.VMEM((hg, hd, hd), f32)],
        compiler_params=_cparams("parallel", "arbitrary"), name="delta_bwd_" + tag)(qkv, qkv, qkv, gb, betab, states, do)


def _dn_out_tile(o, z, ng):
    outs = []
    for h in range(DN_HEADS):
        sl = slice(h * DN_HEAD_DIM, (h + 1) * DN_HEAD_DIM)
        oh = o[:, sl]
        nrm = oh * lax.rsqrt(jnp.mean(oh * oh, -1, keepdims=True) + RMS_EPS) * ng[:, sl]
        outs.append(nrm * jax.nn.silu(z[:, sl]))
    return (jnp.concatenate(outs, -1),)


def _head_selectors():
    r, c = _iota2((BA_PAD, DN_KEY_DIM), 0), _iota2((BA_PAD, DN_KEY_DIM), 1) // DN_HEAD_DIM
    return (r == c).astype(f32), (r == c + DN_HEADS).astype(f32)


def dn_mixer_fwd(tag, proj, cw, alog_b, dtb_b, ng_b):
    eb, ea = _head_selectors()
    ba = (proj, BA_PAD, COL_BA // BA_PAD)
    qkv = dn_conv_fwd(tag, proj, cw)
    betab, gb = rowmap("dn_gate_" + tag, _gate_tile, [ba], [eb, ea, alog_b, dtb_b], [DN_KEY_DIM] * 2, TM_ROW)
    o, states = delta_fwd(tag, qkv, gb, betab)
    z = (proj, DN_KEY_DIM, COL_Z // DN_KEY_DIM)
    a_out = rowmap("dn_out_" + tag, _dn_out_tile, [o, z], [ng_b], [DN_KEY_DIM], TM_ROW)[0]
    return a_out, (qkv, betab, gb, o, states)


def dn_mixer_bwd(tag, proj, cw, alog_b, dtb_b, ng_b, res, da_out):
    qkv, betab, gb, o, states = res
    eb, ea = _head_selectors()
    ba = (proj, BA_PAD, COL_BA // BA_PAD)
    z = (proj, DN_KEY_DIM, COL_Z // DN_KEY_DIM)
    (do, dz), (dng,) = rowmap_bwd("dn_out_bwd_" + tag, _dn_out_tile, [o, z], [ng_b], [da_out], TM_ROW)
    dq, dk, dv, dgb, dbetab = delta_bwd(tag, qkv, gb, betab, states, do)
    dqkv_raw, dcw = dn_conv_bwd(tag, proj, cw, jnp.concatenate([dq, dk, dv], 1))
    (dba,), (dalog, ddtb) = rowmap_bwd("dn_gate_bwd_" + tag, _gate_tile, [ba], [eb, ea, alog_b, dtb_b],
                                       [dbetab, dgb], TM_ROW, par_mask=[False, False, True, True])
    return dqkv_raw, dz, dba, dcw[:DN_CONV], dalog, ddtb, dng


def _swap_halves(x):
    n = x.shape[1]
    first = (_iota2((1, n), 1) % SW_HEAD_DIM) < SW_HEAD_DIM // 2
    return jnp.where(first, pltpu.roll(x, n - SW_HEAD_DIM // 2, 1), pltpu.roll(x, SW_HEAD_DIM // 2, 1))


def _rope_apply(x, cos, sin_signed):
    return x * cos + _swap_halves(x) * sin_signed


def _rope_transpose(dy, cos, sin_signed):
    return dy * cos + _swap_halves(dy * sin_signed)


def rope_tables(positions, s):
    half = SW_HEAD_DIM // 2
    inv_freq = ROPE_THETA ** (-jnp.arange(0, SW_HEAD_DIM, 2, dtype=f32) / SW_HEAD_DIM)
    ang = positions.reshape(s, 1).astype(f32) * inv_freq[None, :]
    cos, sin = jnp.cos(ang), jnp.sin(ang)
    cos_t = jnp.tile(jnp.concatenate([cos, cos], 1), (1, SW_HEADS))
    sin_t = jnp.tile(jnp.concatenate([-sin, sin], 1), (1, SW_HEADS))
    assert cos_t.shape == (s, SW_DIM) and half * 2 == SW_HEAD_DIM
    return cos_t, sin_t


def rope_fwd(tag, proj, cos, sin):
    def fn(q, k, v, c, sg):
        return _rope_apply(q, c, sg), _rope_apply(k, c, sg), v

    rows = [(proj, SW_DIM, COL_SWQ // SW_DIM), (proj, SW_DIM, COL_SWK // SW_DIM), (proj, SW_DIM, COL_SWV // SW_DIM), cos, sin]
    return rowmap("rope_" + tag, fn, rows, [], [SW_DIM] * 3, TM_ROW)


def _swa_block(q, kp, kc, vp, vc, first):
    blk = SW_BLOCK
    kk = jnp.concatenate([kp, kc], 0)
    vv = jnp.concatenate([vp, vc], 0)
    dist = (_iota2((blk, 2 * blk), 0) + blk) - _iota2((blk, 2 * blk), 1)
    kj = _iota2((blk, 2 * blk), 1)
    valid = (dist >= 0) & (dist <= blk) & ((kj >= blk) | jnp.logical_not(first))
    lane_head = _iota2((1, LANES), 1) // SW_HEAD_DIM
    outs, lses = [], []
    for p in range(SW_DIM // LANES):
        sl = slice(p * LANES, (p + 1) * LANES)
        qp, kp_, vp_ = q[:, sl], kk[:, sl], vv[:, sl]
        o_pair = jnp.zeros((blk, LANES), f32)
        l_pair = jnp.zeros((blk, LANES), f32)
        for e in range(LANES // SW_HEAD_DIM):
            msk = lane_head == e
            sc = bdot(jnp.where(msk, qp, 0.0), kp_, 1, 1) * (SW_HEAD_DIM ** -0.5)
            sc = jnp.where(valid, sc, -1e30)
            m = lax.stop_gradient(jnp.max(sc, -1, keepdims=True))
            pe = jnp.exp(sc - m)
            l = jnp.sum(pe, -1, keepdims=True)
            o = bdot(pe, vp_, 1, 0) / l
            o_pair = o_pair + jnp.where(msk, o, 0.0)
            l_pair = l_pair + jnp.where(msk, m + jnp.log(l), 0.0)
        outs.append(o_pair)
        lses.append(l_pair)
    return jnp.concatenate(outs, -1), jnp.concatenate(lses, -1)


def _swa_specs(r):
    cur = pl.BlockSpec((SW_BLOCK, SW_DIM), lambda rho, n: (n, rho))
    prev = pl.BlockSpec((SW_BLOCK, SW_DIM), lambda rho, n: (jnp.maximum(n - 1, 0), rho))
    return cur, prev


def swa_fwd(tag, r, q, k, v):
    s = q.shape[0]
    ln = s // r
    q2, k2, v2 = (t.reshape(ln, r * SW_DIM) for t in (q, k, v))
    cur, prev = _swa_specs(r)

    def body(q_ref, kp_ref, kc_ref, vp_ref, vc_ref, o_ref, l_ref):
        o, l = _swa_block(q_ref[...], kp_ref[...], kc_ref[...], vp_ref[...], vc_ref[...], pl.program_id(1) == 0)
        o_ref[...] = o
        l_ref[...] = l

    o, l = pl.pallas_call(
        body, grid=(r, ln // SW_BLOCK),
        in_specs=[cur, prev, cur, prev, cur], out_specs=[cur, cur],
        out_shape=[jax.ShapeDtypeStruct((ln, r * SW_DIM), f32)] * 2,
        compiler_params=_cparams("parallel", "parallel"), name=f"swa{r}_{tag}")(q2, k2, k2, v2, v2)
    return o.reshape(s, SW_DIM), l.reshape(s, SW_DIM)


def swa_bwd(tag, r, q, k, v, do, dl):
    s = q.shape[0]
    ln = s // r
    q2, k2, v2, do2, dl2 = (t.reshape(ln, r * SW_DIM) for t in (q, k, v, do, dl))
    cur, prev = _swa_specs(r)

    def body(q_ref, kp_ref, kc_ref, vp_ref, vc_ref, do_ref, dl_ref, dq_ref, dka_ref, dkb_ref, dva_ref, dvb_ref):
        first = pl.program_id(1) == 0
        _, vjp = jax.vjp(lambda *a: _swa_block(*a, first), q_ref[...], kp_ref[...], kc_ref[...], vp_ref[...], vc_ref[...])
        dq_ref[...], dka_ref[...], dkb_ref[...], dva_ref[...], dvb_ref[...] = vjp((do_ref[...], dl_ref[...]))

    outs = pl.pallas_call(
        body, grid=(r, ln // SW_BLOCK),
        in_specs=[cur, prev, cur, prev, cur, cur, cur], out_specs=[cur] * 5,
        out_shape=[jax.ShapeDtypeStruct((ln, r * SW_DIM), f32)] * 5,
        compiler_params=_cparams("parallel", "parallel"), name=f"swa{r}_bwd_{tag}")(q2, k2, k2, v2, v2, do2, dl2)
    return [t.reshape(s, SW_DIM) for t in outs]


def _combine_tile(o1, l1, o2, l2, o3, l3):
    m = lax.stop_gradient(jnp.maximum(jnp.maximum(l1, l2), l3))
    e1, e2, e3 = jnp.exp(l1 - m), jnp.exp(l2 - m), jnp.exp(l3 - m)
    return ((o1 * e1 + o2 * e2 + o3 * e3) / (e1 + e2 + e3),)


def swa_merge_bwd(tag, grads, cos, sin):
    s = cos.shape[0]
    tm = SW_BLOCK
    nt = s // tm
    here = pl.BlockSpec((tm, SW_DIM), lambda i: (i, 0))
    arrs, specs = [], []
    for r, g in zip(SW_DILATIONS, grads):
        ahead = pl.BlockSpec((tm, SW_DIM), lambda i, _r=r: (jnp.minimum(i + _r, nt - 1), 0))
        arrs += g
        specs += [here, ahead, here, ahead, here]

    def body(*refs):
        i = pl.program_id(0)
        c_ref, s_ref = refs[15], refs[16]
        dq_ref, dk_ref, dv_ref = refs[17:]
        dq = jnp.zeros((tm, SW_DIM), f32)
        dk = jnp.zeros((tm, SW_DIM), f32)
        dv = jnp.zeros((tm, SW_DIM), f32)
        for b, r in enumerate(SW_DILATIONS):
            gq, gka, gkb, gva, gvb = refs[5 * b:5 * b + 5]
            inside = i + r < nt
            dq = dq + gq[...]
            dk = dk + gkb[...] + jnp.where(inside, gka[...], 0.0)
            dv = dv + gvb[...] + jnp.where(inside, gva[...], 0.0)
        dq_ref[...] = _rope_transpose(dq, c_ref[...], s_ref[...])
        dk_ref[...] = _rope_transpose(dk, c_ref[...], s_ref[...])
        dv_ref[...] = dv

    return pl.pallas_call(
        body, grid=(nt,), in_specs=specs + [here, here], out_specs=[here] * 3,
        out_shape=[jax.ShapeDtypeStruct((s, SW_DIM), f32)] * 3,
        compiler_params=_cparams("parallel"), name="swa_merge_bwd_" + tag)(*arrs, cos, sin)


def swa_mixer_fwd(tag, proj, cos, sin):
    q, k, v = rope_fwd(tag, proj, cos, sin)
    ols = []
    for r in SW_DILATIONS:
        ols += list(swa_fwd(tag, r, q, k, v))
    b_out = rowmap("swa_comb_" + tag, _combine_tile, ols, [], [SW_DIM], TM_ROW)[0]
    return b_out, (q, k, v, ols)


def swa_mixer_bwd(tag, cos, sin, res, db_out):
    q, k, v, ols = res
    dols, _ = rowmap_bwd("swa_comb_bwd_" + tag, _combine_tile, ols, [], [db_out], TM_ROW)
    grads = [swa_bwd(tag, r, q, k, v, dols[2 * b], dols[2 * b + 1]) for b, r in enumerate(SW_DILATIONS)]
    return swa_merge_bwd(tag, grads, cos, sin)


TM_S5 = 256
S5_GPB = LANES // S5_GROUP
S5_NBLK = D_MODEL // LANES
S5_HALF = S5_GPB * S5_STATE
S5_BW = 2 * S5_HALF
S5_WIDTH = S5_NBLK * S5_BW
S5_TABW = S5_NBLK * S5_HALF


def _s5_disc_tile(a_re, a_im, log_dt, b_re, b_im, expand):
    dt = jnp.exp(log_dt)
    mag = jnp.exp(a_re * dt)
    abar_re, abar_im = mag * jnp.cos(a_im * dt), mag * jnp.sin(a_im * dt)
    n_re, n_im = abar_re - 1.0, abar_im
    den = a_re * a_re + a_im * a_im
    c_re = (n_re * a_re + n_im * a_im) / den
    c_im = (n_im * a_re - n_re * a_im) / den
    cx_re, cx_im = hdot(c_re, expand), hdot(c_im, expand)
    return abar_re, abar_im, cx_re * b_re - cx_im * b_im, cx_re * b_im + cx_im * b_re


def _s5_expand():
    return (_iota2((S5_STATE, S5_STATE * S5_GROUP), 1) // S5_GROUP == _iota2((S5_STATE, S5_STATE * S5_GROUP), 0)).astype(f32)


def s5_tables(a_re, a_im, log_dt):
    lanes = lambda v: v.reshape(1, S5_TABW)
    dt = jnp.broadcast_to(log_dt.reshape(S5_GROUPS, 1), (S5_GROUPS, S5_STATE))
    t = TM_S5

    def body(are_ref, aim_ref, ldt_ref, ar_ref, ai_ref, arr_ref, air_ref):
        dtv = jnp.exp(ldt_ref[...])
        lre, lim = are_ref[...] * dtv, aim_ref[...] * dtv
        row = _iota2((t, S5_HALF), 0)
        for asc, o_re, o_im in ((True, ar_ref, ai_ref), (False, arr_ref, air_ref)):
            n = (row + 1 if asc else t - row).astype(f32)
            mag = jnp.exp(n * lre)
            o_re[...] = mag * jnp.cos(n * lim)
            o_im[...] = mag * jnp.sin(n * lim)

    lane = pl.BlockSpec((1, S5_HALF), lambda j: (0, j))
    tab = pl.BlockSpec((t, S5_HALF), lambda j: (0, j))
    return pl.pallas_call(
        body, grid=(S5_NBLK,), in_specs=[lane] * 3, out_specs=[tab] * 4,
        out_shape=[jax.ShapeDtypeStruct((t, S5_TABW), f32)] * 4,
        compiler_params=_cparams("parallel"), name="s5_tables")(lanes(a_re), lanes(a_im), lanes(dt))


def s5_pack_weights(bbar_re, bbar_im, c_re, c_im):
    eye = jnp.eye(S5_GPB, dtype=f32)
    bb = jnp.stack([bbar_re.reshape(S5_GROUPS, S5_STATE, S5_GROUP), bbar_im.reshape(S5_GROUPS, S5_STATE, S5_GROUP)], 1)
    bb = bb.transpose(0, 3, 1, 2).reshape(S5_NBLK, S5_GPB, S5_GROUP, 2, S5_STATE)
    wb = (bb[:, :, :, :, None, :] * eye[None, :, None, None, :, None]).reshape(S5_NBLK, LANES, S5_BW)
    cc = jnp.stack([c_re, -c_im], 1)
    cc = cc.reshape(S5_NBLK, S5_GPB, 2, S5_GROUP, S5_STATE).transpose(0, 2, 1, 4, 3)
    wc = (cc[:, :, :, :, None, :] * eye[None, None, :, None, :, None]).reshape(S5_NBLK, S5_BW, LANES)
    return wb, wc


def s5_unpack_weight_grads(dwb, dwc):
    d6 = dwb.reshape(S5_NBLK, S5_GPB, S5_GROUP, 2, S5_GPB, S5_STATE)
    dbb = jnp.stack([d6[:, gl, :, :, gl, :] for gl in range(S5_GPB)])
    dbb = dbb.transpose(1, 0, 3, 4, 2).reshape(S5_GROUPS, 2, S5_STATE * S5_GROUP)
    c6 = dwc.reshape(S5_NBLK, 2, S5_GPB, S5_STATE, S5_GPB, S5_GROUP)
    dcc = jnp.stack([c6[:, :, gl, :, gl, :] for gl in range(S5_GPB)])
    dcc = dcc.transpose(1, 0, 2, 4, 3).reshape(S5_GROUPS, 2, S5_GROUP, S5_STATE)
    return dbb[:, 0], dbb[:, 1], dcc[:, 0], -dcc[:, 1]


def _s5_step_rows(t):
    d, out = 1, []
    while d < t:
        out.append(d)
        d *= 2
    return out


def s5_core_fwd(tag, u, wb, wc, a1, a2, dskip):
    s = u.shape[0]
    t = TM_S5

    def body(u_ref, wb_ref, wc_ref, ar_ref, ai_ref, d_ref, y_ref, x_ref, carry):
        @pl.when(pl.program_id(1) == 0)
        def _():
            carry[...] = jnp.zeros_like(carry)

        uv = u_ref[...]
        bu = bdot(uv, wb_ref[...], 1, 0)
        xr, xi = bu[:, :S5_HALF], bu[:, S5_HALF:]
        row = _iota2((t, S5_HALF), 0)
        for d in _s5_step_rows(t):
            keep = row >= d
            sr = jnp.where(keep, pltpu.roll(xr, d, 0), 0.0)
            si = jnp.where(keep, pltpu.roll(xi, d, 0), 0.0)
            ar, ai = ar_ref[d - 1:d, :], ai_ref[d - 1:d, :]
            xr, xi = xr + ar * sr - ai * si, xi + ar * si + ai * sr
        cr, ci = carry[:, :S5_HALF], carry[:, S5_HALF:]
        ar, ai = ar_ref[...], ai_ref[...]
        x_ref[:, :S5_HALF] = xr + ar * cr - ai * ci
        x_ref[:, S5_HALF:] = xi + ar * ci + ai * cr
        carry[...] = x_ref[t - 1:t, :]
        y_ref[...] = bdot(x_ref[...], wc_ref[...], 1, 0) + d_ref[...] * uv

    tab = pl.BlockSpec((t, S5_HALF), lambda j, i: (0, j))
    return pl.pallas_call(
        body, grid=(S5_NBLK, s // t),
        in_specs=[pl.BlockSpec((t, LANES), lambda j, i: (i, j)),
                  pl.BlockSpec((None, LANES, S5_BW), lambda j, i: (j, 0, 0)),
                  pl.BlockSpec((None, S5_BW, LANES), lambda j, i: (j, 0, 0)),
                  tab, tab, pl.BlockSpec((1, LANES), lambda j, i: (0, j))],
        out_specs=[pl.BlockSpec((t, LANES), lambda j, i: (i, j)), pl.BlockSpec((t, S5_BW), lambda j, i: (i, j))],
        out_shape=[jax.ShapeDtypeStruct((s, D_MODEL), f32), jax.ShapeDtypeStruct((s, S5_WIDTH), f32)],
        scratch_shapes=[pltpu.VMEM((1, S5_BW), f32)],
        compiler_params=_cparams("parallel", "arbitrary"), name="s5_core_" + tag)(u, wb, wc, a1, a2, dskip)


def s5_core_bwd(tag, u, x, wb, wc, a1, a2, a1r, a2r, dskip, dy):
    s = u.shape[0]
    t = TM_S5
    nt = s // t
    hb = t // SUBLANES

    def body(u_ref, dy_ref, x_ref, xh_ref, wb_ref, wc_ref, ar_ref, ai_ref, arr_ref, air_ref, d_ref,
             du_ref, dwb_ref, dwc_ref, dd_ref, q1_ref, q2_ref, carry, lam_scr):
        i = pl.program_id(1)
        tt = nt - 1 - i

        @pl.when(i == 0)
        def _():
            carry[...] = jnp.zeros_like(carry)

        uv, dyv, xv = u_ref[...], dy_ref[...], x_ref[...]
        lam = bdot(dyv, wc_ref[...], 1, 1)
        lr, li = lam[:, :S5_HALF], lam[:, S5_HALF:]
        row = _iota2((t, S5_HALF), 0)
        for d in _s5_step_rows(t):
            keep = row < t - d
            sr = jnp.where(keep, pltpu.roll(lr, t - d, 0), 0.0)
            si = jnp.where(keep, pltpu.roll(li, t - d, 0), 0.0)
            ar, ai = ar_ref[d - 1:d, :], ai_ref[d - 1:d, :]
            lr, li = lr + ar * sr + ai * si, li + ar * si - ai * sr
        cr, ci = carry[:, :S5_HALF], carry[:, S5_HALF:]
        ar, ai = arr_ref[...], air_ref[...]
        lr, li = lr + ar * cr + ai * ci, li + ar * ci - ai * cr
        lam_scr[:, :S5_HALF] = lr
        lam_scr[:, S5_HALF:] = li
        carry[...] = lam_scr[0:1, :]
        lam = lam_scr[...]
        du_ref[...] = bdot(lam, wb_ref[...], 1, 1) + d_ref[...] * dyv
        x_last = jnp.where(tt > 0, xh_ref[SUBLANES - 1:SUBLANES, :], 0.0)
        x_prev = jnp.where(_iota2((t, S5_BW), 0) == 0, x_last, pltpu.roll(xv, 1, 0))
        pr, pi = x_prev[:, :S5_HALF], x_prev[:, S5_HALF:]
        p1, p2 = lr * pr + li * pi, li * pr - lr * pi
        q1 = p1[:SUBLANES, :]
        q2 = p2[:SUBLANES, :]
        for k in range(1, hb):
            q1 = q1 + p1[k * SUBLANES:(k + 1) * SUBLANES, :]
            q2 = q2 + p2[k * SUBLANES:(k + 1) * SUBLANES, :]
        upd = [(dwb_ref, bdot(uv, lam, 0, 0)), (dwc_ref, bdot(xv, dyv, 0, 0)),
               (dd_ref, jnp.sum(dyv * uv, 0, keepdims=True)), (q1_ref, q1), (q2_ref, q2)]

        @pl.when(i == 0)
        def _():
            for ref, val in upd:
                ref[...] = val

        @pl.when(i != 0)
        def _():
            for ref, val in upd:
                ref[...] += val

    nb8 = s // SUBLANES
    rev = lambda w: pl.BlockSpec((t, w), lambda j, i: (nt - 1 - i, j))
    tab = pl.BlockSpec((t, S5_HALF), lambda j, i: (0, j))
    return pl.pallas_call(
        body, grid=(S5_NBLK, nt),
        in_specs=[rev(LANES), rev(LANES), rev(S5_BW),
                  pl.BlockSpec((SUBLANES, S5_BW), lambda j, i: (jnp.maximum((nt - 1 - i) * hb - 1, 0), j)),
                  pl.BlockSpec((None, LANES, S5_BW), lambda j, i: (j, 0, 0)),
                  pl.BlockSpec((None, S5_BW, LANES), lambda j, i: (j, 0, 0)),
                  tab, tab, tab, tab, pl.BlockSpec((1, LANES), lambda j, i: (0, j))],
        out_specs=[rev(LANES),
                   pl.BlockSpec((None, LANES, S5_BW), lambda j, i: (j, 0, 0)),
                   pl.BlockSpec((None, S5_BW, LANES), lambda j, i: (j, 0, 0)),
                   pl.BlockSpec((1, LANES), lambda j, i: (0, j)),
                   pl.BlockSpec((SUBLANES, S5_HALF), lambda j, i: (0, j)),
                   pl.BlockSpec((SUBLANES, S5_HALF), lambda j, i: (0, j))],
        out_shape=[jax.ShapeDtypeStruct((s, D_MODEL), f32),
                   jax.ShapeDtypeStruct((S5_NBLK, LANES, S5_BW), f32),
                   jax.ShapeDtypeStruct((S5_NBLK, S5_BW, LANES), f32),
                   jax.ShapeDtypeStruct((1, D_MODEL), f32),
                   jax.ShapeDtypeStruct((SUBLANES, S5_TABW), f32),
                   jax.ShapeDtypeStruct((SUBLANES, S5_TABW), f32)],
        scratch_shapes=[pltpu.VMEM((1, S5_BW), f32), pltpu.VMEM((t, S5_BW), f32)],
        compiler_params=_cparams("parallel", "arbitrary"),
        name="s5_core_bwd_" + tag)(u, dy, x, x, wb, wc, a1, a2, a1r, a2r, dskip)


def _gelu_tile(y):
    return (jax.nn.gelu(y),)


def s5_mixer_fwd(tag, u, prm, w_og):
    a_re, a_im, log_dt, b_re, b_im, c_re, c_im, dskip = prm
    disc_in = [a_re, a_im, log_dt.reshape(S5_GROUPS, 1), b_re.reshape(S5_GROUPS, -1), b_im.reshape(S5_GROUPS, -1)]
    abar_re, abar_im, bbar_re, bbar_im = rowmap("s5_disc_" + tag, _s5_disc_tile, disc_in, [_s5_expand()],
                                                [S5_STATE, S5_STATE, S5_STATE * S5_GROUP, S5_STATE * S5_GROUP], S5_GROUPS)
    del abar_re, abar_im
    a1, a2, a1r, a2r = s5_tables(a_re, a_im, log_dt)
    wb, wc = s5_pack_weights(bbar_re, bbar_im, c_re, c_im)
    wb, wc = wb.astype(bf16), wc.astype(bf16)
    y, x = s5_core_fwd(tag, u, wb, wc, a1, a2, dskip.reshape(1, D_MODEL))
    hid = rowmap("s5_gelu_" + tag, _gelu_tile, [y], [], [D_MODEL], TM_ROW, out_dtypes=[bf16])[0]
    og = mm_nn("s5_og_" + tag, hid, w_og)
    mix = rowmap("s5_glu_" + tag, _glu_tile, [og], [], [D_MODEL], TM_ROW)[0]
    return mix, (disc_in, a1, a2, a1r, a2r, wb, wc, x, y, hid, og)


def s5_mixer_bwd(tag, u, prm, w_og, res, dmix):
    a_re, a_im, log_dt, b_re, b_im, c_re, c_im, dskip = prm
    disc_in, a1, a2, a1r, a2r, wb, wc, x, y, hid, og = res
    (dog,), _ = rowmap_bwd("s5_glu_bwd_" + tag, _glu_tile, [og], [], [dmix], TM_ROW)
    dw_og = mm_tn("s5_og_dw_" + tag, hid, dog)
    dhid = mm_nt("s5_og_dx_" + tag, dog, w_og)
    (dy,), _ = rowmap_bwd("s5_gelu_bwd_" + tag, _gelu_tile, [y], [], [dhid], TM_ROW)
    du, dwb, dwc, ddskip, q1, q2 = s5_core_bwd(tag, u, x, wb, wc, a1, a2, a1r, a2r, dskip.reshape(1, D_MODEL), dy)
    dbbar_re, dbbar_im, dc_re, dc_im = s5_unpack_weight_grads(dwb, dwc)
    dabar_re = q1.sum(0).reshape(S5_GROUPS, S5_STATE)
    dabar_im = q2.sum(0).reshape(S5_GROUPS, S5_STATE)
    grads, _ = rowmap_bwd("s5_disc_bwd_" + tag, _s5_disc_tile, disc_in, [_s5_expand()],
                          [dabar_re, dabar_im, dbbar_re, dbbar_im], S5_GROUPS, par_mask=[False])
    da_re, da_im, dlog_dt, db_re, db_im = grads
    return du, (da_re, da_im, dlog_dt.reshape(S5_GROUPS), db_re.reshape(b_re.shape), db_im.reshape(b_im.shape),
                dc_re, dc_im, ddskip.reshape(D_MODEL)), dw_og


HYB_IN = 3592
_IN_B0, _IN_SW0 = 2048, 2056


IN_SHARD = HYB_IN // 4
SHARD_ORDER_GRADS = ("hyb_w_in", "ffn_wg", "ffn_wu", "ffn_wd")
FFN_TRANSPOSED = ("ffn_wg", "ffn_wu")


def _w_in_pieces():
    runs = [(0, _IN_B0, 0), (_IN_B0, _IN_SW0, COL_BA), (_IN_SW0, HYB_IN, _IN_B0)]
    out = []
    for sh in range(4):
        lo, hi = sh * IN_SHARD, (sh + 1) * IN_SHARD
        for r_lo, r_hi, c_lo in runs:
            a, b = max(lo, r_lo), min(hi, r_hi)
            if a < b:
                out.append((sh, a - lo, b - lo, c_lo + a - r_lo))
    return out


def w_in_to_canonical(tag, layer, w4):
    tr = 128

    def body(w_ref, o_ref):
        o_ref[:, COL_BA:] = jnp.zeros((tr, BA_PAD), o_ref.dtype)
        for sh, a, b, c in _w_in_pieces():
            o_ref[:, c:c + b - a] = w_ref[sh, :, a:b]

    return pl.pallas_call(
        body, grid=(D_MODEL // tr,),
        in_specs=[pl.BlockSpec((4, None, tr, IN_SHARD), lambda i: (0, layer, i, 0))],
        out_specs=pl.BlockSpec((tr, PROJ_COLS), lambda i: (i, 0)),
        out_shape=jax.ShapeDtypeStruct((D_MODEL, PROJ_COLS), w4.dtype),
        compiler_params=_cparams("parallel"), name="w_in_canon_" + tag)(w4)


def w_in_grad_to_shards(tag, g):
    tr = 128

    def body(g_ref, o_ref):
        for sh, a, b, c in _w_in_pieces():
            o_ref[sh, :, a:b] = g_ref[:, c:c + b - a]

    return pl.pallas_call(
        body, grid=(D_MODEL // tr,),
        in_specs=[pl.BlockSpec((tr, PROJ_COLS), lambda i: (i, 0))],
        out_specs=pl.BlockSpec((4, tr, IN_SHARD), lambda i: (0, i, 0)),
        out_shape=jax.ShapeDtypeStruct((4, D_MODEL, IN_SHARD), f32),
        compiler_params=_cparams("parallel"), name="w_in_grad_shards_" + tag)(g)


def _add2(name, a, b):
    return rowmap(name, lambda p, q: (p + q,), [a, b], [], [a.shape[1]], _pick(a.shape[0], (256, 128, 64, 32, 16, 8)))[0]


def local_step(x, mem, positions, target, p):
    s = x.shape[0]
    cos, sin = rope_tables(positions, s)
    row = lambda v: v.reshape(1, -1).astype(f32)
    wg4, wu4, wd4 = (p[n].astype(bf16) for n in ("ffn_wg", "ffn_wu", "ffn_wd"))
    h = h16 = x
    tape = []
    for l in range(DEPTH):
        i, tag = l // 2, str(l)
        t = {"h0": h, "h0_16": h16}
        if l % 2 == 0:
            t["w_in"] = w_in_to_canonical(tag, i, p["hyb_w_in"].astype(bf16))
            t["w_out"] = p["hyb_w_out"][i].astype(bf16)
            t["dn_prm"] = (p["dn_conv_w"][i].astype(f32), row(jnp.repeat(p["dn_a_log"][i], DN_HEAD_DIM)),
                           row(jnp.repeat(p["dn_dt_bias"][i], DN_HEAD_DIM)), row(jnp.tile(p["dn_norm_g"][i], DN_HEADS)))
            t["proj"] = mm_nn("hyb_in_" + tag, h16, t["w_in"])
            a_out, t["dn"] = dn_mixer_fwd(tag, t["proj"], *t["dn_prm"])
            b_out, t["swa"] = swa_mixer_fwd(tag, t["proj"], cos, sin)
            t["mixed"] = jnp.concatenate([a_out, b_out], 1)
            mix = mm_nn("hyb_out_" + tag, t["mixed"], t["w_out"])
        else:
            t["s5_prm"] = tuple(p[n][i].astype(f32) for n in
                                ("s5_a_re", "s5_a_im", "s5_log_dt", "s5_b_re", "s5_b_im", "s5_c_re", "s5_c_im", "s5_d"))
            t["w_og"] = jnp.concatenate([p["s5_glu_wo"][i], p["s5_glu_wg"][i]], 1).astype(bf16)
            mix, t["s5"] = s5_mixer_fwd(tag, h, t["s5_prm"], t["w_og"])
        t["mix"] = mix
        t["ln"] = [(row(p[g][l]), row(p[b][l])) for g, b in
                   (("ln_mix_g", "ln_mix_b"), ("ln_x_g", "ln_x_b"), ("ln_ffn_g", "ln_ffn_b"))]
        t["h1"], t["h1_16"] = postnorm_fwd("mix" + tag, h, mix, *t["ln"][0])
        t["wq"], t["wo"] = p["xq_w"][l].astype(bf16), p["xo_w"][l].astype(bf16)
        t["wkv"] = jnp.concatenate([p["xk_w"][l], p["xv_w"][l]], 1).astype(bf16)
        t["xo"], t["xres"] = xattn_fwd(tag, t["h1_16"], mem, t["wq"], t["wkv"], t["wo"])
        t["h2"], t["h2_16"] = postnorm_fwd("x" + tag, t["h1"], t["xo"], *t["ln"][1])
        t["fo"], t["fres"] = ffn_fwd(tag, l, t["h2_16"], wg4, wu4, wd4)
        h, h16 = postnorm_fwd("ffn" + tag, t["h2"], t["fo"], *t["ln"][2])
        tape.append(t)

    part, dh = loss_head(h, target)
    loss = jnp.sum(part)

    g = {n: [None] * v.shape[1 if n in SHARD_ORDER_GRADS else 0] for n, v in p.items()}
    for l in reversed(range(DEPTH)):
        i, tag, t = l // 2, str(l), tape[l]
        dh2a, dfo, dg, db = postnorm_bwd("ffn" + tag, t["h2"], t["fo"], *t["ln"][2], dh)
        g["ln_ffn_g"][l], g["ln_ffn_b"][l] = dg[0], db[0]
        dh2b, g["ffn_wg"][l], g["ffn_wu"][l], g["ffn_wd"][l] = ffn_bwd(tag, l, t["h2_16"], wg4, wu4, wd4, t["fres"], dfo)
        dh1a, dxo, dg, db = postnorm_bwd("x" + tag, t["h1"], t["xo"], *t["ln"][1], [dh2a, dh2b])
        g["ln_x_g"][l], g["ln_x_b"][l] = dg[0], db[0]
        dh1b, g["xq_w"][l], dwkv, g["xo_w"][l] = xattn_bwd(tag, t["h1_16"], mem, t["wq"], t["wkv"], t["wo"], t["xres"], dxo)
        g["xk_w"][l], g["xv_w"][l] = dwkv[:, :D_MODEL], dwkv[:, D_MODEL:]
        dh0a, dmix, dg, db = postnorm_bwd("mix" + tag, t["h0"], t["mix"], *t["ln"][0], [dh1a, dh1b])
        g["ln_mix_g"][l], g["ln_mix_b"][l] = dg[0], db[0]
        if l % 2 == 0:
            g["hyb_w_out"][i] = mm_tn("hyb_out_dw_" + tag, t["mixed"], dmix)
            dmixed = mm_nt("hyb_out_dx_" + tag, dmix, t["w_out"])
            dqkv, dz, dba, dcw, dalog, ddtb, dng = dn_mixer_bwd(tag, t["proj"], *t["dn_prm"], t["dn"], (dmixed, DN_KEY_DIM, 0))
            g["dn_conv_w"][i] = dcw
            g["dn_a_log"][i] = dalog.reshape(DN_HEADS, DN_HEAD_DIM).sum(1)
            g["dn_dt_bias"][i] = ddtb.reshape(DN_HEADS, DN_HEAD_DIM).sum(1)
            g["dn_norm_g"][i] = dng.reshape(DN_HEADS, DN_HEAD_DIM).sum(0)
            dq, dk, dv = swa_mixer_bwd(tag, cos, sin, t["swa"], (dmixed, SW_DIM, 1))
            dproj = jnp.concatenate([dqkv, dz, dq, dk, dv, dba], 1)
            g["hyb_w_in"][i] = w_in_grad_to_shards(tag, mm_tn("hyb_in_dw_" + tag, t["h0_16"], dproj))
            dh0b = mm_nt("hyb_in_dx_" + tag, dproj, t["w_in"])
        else:
            dh0b, dprm, dw_og = s5_mixer_bwd(tag, t["h0"], t["s5_prm"], t["w_og"], t["s5"], dmix)
            for n, v in zip(("s5_a_re", "s5_a_im", "s5_log_dt", "s5_b_re", "s5_b_im", "s5_c_re", "s5_c_im", "s5_d"), dprm):
                g[n][i] = v
            g["s5_glu_wo"][i], g["s5_glu_wg"][i] = dw_og[:, :D_MODEL], dw_og[:, D_MODEL:]
        dh = [dh0a, dh0b]
    grad_x = _add2("grad_x", dh[0], dh[1])
    grads = {n: jnp.stack(v, 1 if n in SHARD_ORDER_GRADS else 0) for n, v in g.items()}
    return loss, grad_x, grads


WEIGHT_NAMES = ("hyb_w_in", "dn_conv_w", "dn_a_log", "dn_dt_bias", "dn_norm_g", "hyb_w_out", "s5_a_re", "s5_a_im",
                "s5_log_dt", "s5_b_re", "s5_b_im", "s5_c_re", "s5_c_im", "s5_d", "s5_glu_wo", "s5_glu_wg",
                "ln_mix_g", "ln_mix_b", "xq_w", "xk_w", "xv_w", "xo_w", "ln_x_g", "ln_x_b",
                "ffn_wg", "ffn_wu", "ffn_wd", "ln_ffn_g", "ln_ffn_b")
SHARD_AXIS = {"hyb_w_in": 2, "dn_conv_w": 2, "hyb_w_out": 1, "s5_d": 1, "s5_glu_wo": 1, "s5_glu_wg": 1,
              "xq_w": 1, "xk_w": 1, "xv_w": 1, "xo_w": 1, "ffn_wg": 2, "ffn_wu": 2, "ffn_wd": 1}
GATHER_F32 = ("dn_conv_w", "s5_d")
N_CHIPS = 4
PACK_COLS = 1024
_ANY = pl.BlockSpec(memory_space=pl.ANY)


def _pos():
    return lax.axis_index("x"), lax.axis_index("y"), lax.axis_index("c")


def _chip_peers(mx, my):
    return [(1 - mx, my), (mx, 1 - my), (1 - mx, 1 - my)]


def _rcopy(src, dst, ssem, rsem, dev):
    return pltpu.make_async_remote_copy(src_ref=src, dst_ref=dst, send_sem=ssem, recv_sem=rsem,
                                        device_id=dev, device_id_type=pl.DeviceIdType.MESH)


def comm_allgather4(name, x):
    def body(x_ref, o_ref, ssem, rsem, lsem):
        mx, my, mc = _pos()
        me = 2 * mx + my
        peers = _chip_peers(mx, my)
        loc = pltpu.make_async_copy(x_ref, o_ref.at[me], lsem)
        loc.start()
        sends = [_rcopy(x_ref, o_ref.at[me], ssem.at[k], rsem.at[k], (px, py, mc)) for k, (px, py) in enumerate(peers)]
        for cp in sends:
            cp.start()
        for k, (px, py) in enumerate(peers):
            _rcopy(x_ref, o_ref.at[2 * px + py], ssem.at[k], rsem.at[k], (px, py, mc)).wait_recv()
        for cp in sends:
            cp.wait_send()
        loc.wait()

    return pl.pallas_call(
        body, out_shape=jax.ShapeDtypeStruct((N_CHIPS,) + x.shape, x.dtype), in_specs=[_ANY], out_specs=_ANY,
        scratch_shapes=[pltpu.SemaphoreType.DMA((3,)), pltpu.SemaphoreType.DMA((3,)), pltpu.SemaphoreType.DMA],
        name=name)(x)


def _multi_call(name, body, ins, out_shapes, sems, in_place=False):
    return pl.pallas_call(
        body, out_shape=out_shapes, in_specs=[_ANY] * len(ins), out_specs=[_ANY] * len(out_shapes),
        scratch_shapes=sems, input_output_aliases={w: w for w in range(len(ins))} if in_place else {},
        name=name)(*ins)


def comm_gather_weights(name, slots):
    n = len(slots)

    def body(*refs):
        os_ = refs[n:2 * n]
        ssem, rsem, fssem, frsem = refs[2 * n:]
        mx, my, mc = _pos()
        me = 2 * mx + my
        peers = _chip_peers(mx, my)
        sib = (mx, my, 1 - mc)
        half = [o.shape[1] // 2 for o in os_]
        mine = [pl.ds(mc * h, h) for h in half]
        other = [pl.ds((1 - mc) * h, h) for h in half]
        sends = [_rcopy(os_[w].at[me, mine[w]], os_[w].at[me, mine[w]], ssem.at[w, k], rsem.at[w, k], (px, py, mc))
                 for w in range(n) for k, (px, py) in enumerate(peers)]
        for cp in sends:
            cp.start()
        fwds = []
        for w in range(n):
            for k, (px, py) in enumerate(peers):
                landed = os_[w].at[2 * px + py, mine[w]]
                _rcopy(landed, landed, ssem.at[w, k], rsem.at[w, k], (px, py, mc)).wait_recv()
                fw = _rcopy(landed, landed, fssem.at[w, k], frsem.at[w, k], sib)
                fw.start()
                fwds.append(fw)
        for w in range(n):
            for k, (px, py) in enumerate(peers):
                theirs = os_[w].at[2 * px + py, other[w]]
                _rcopy(theirs, theirs, fssem.at[w, k], frsem.at[w, k], sib).wait_recv()
        for cp in sends + fwds:
            cp.wait_send()

    dma = pltpu.SemaphoreType.DMA
    return _multi_call(name, body, slots, [jax.ShapeDtypeStruct(x.shape, x.dtype) for x in slots],
                       [dma((n, 3)), dma((n, 3)), dma((n, 3)), dma((n, 3))], in_place=True)


def comm_sibling_halves(name, gs):
    n = len(gs)

    def body(*refs):
        xs, os_ = refs[:n], refs[n:2 * n]
        ssem, rsem = refs[2 * n:]
        mx, my, mc = _pos()
        sib = (mx, my, 1 - mc)
        sends = []
        for w in range(n):
            h = xs[w].shape[1] // 2
            for j in range(N_CHIPS):
                sends.append(_rcopy(xs[w].at[j, pl.ds((1 - mc) * h, h)], os_[w].at[j], ssem.at[w, j], rsem.at[w, j], sib))
        for cp in sends:
            cp.start()
        for w in range(n):
            for j in range(N_CHIPS):
                _rcopy(os_[w].at[j], os_[w].at[j], ssem.at[w, j], rsem.at[w, j], sib).wait_recv()
        for cp in sends:
            cp.wait_send()

    dma = pltpu.SemaphoreType.DMA
    return _multi_call(name, body, gs,
                       [jax.ShapeDtypeStruct((N_CHIPS, g.shape[1] // 2) + g.shape[2:], g.dtype) for g in gs],
                       [dma((n, N_CHIPS)), dma((n, N_CHIPS))])


def comm_alltoall4(name, xs):
    n = len(xs)

    def body(*refs):
        xr, os_ = refs[:n], refs[n:2 * n]
        ssem, rsem = refs[2 * n:]
        mx, my, mc = _pos()
        me = 2 * mx + my
        peers = _chip_peers(mx, my)
        sends = [_rcopy(xr[w].at[2 * px + py], os_[w].at[me], ssem.at[w, k], rsem.at[w, k], (px, py, mc))
                 for w in range(n) for k, (px, py) in enumerate(peers)]
        for cp in sends:
            cp.start()
        for w in range(n):
            for k, (px, py) in enumerate(peers):
                dst = os_[w].at[2 * px + py]
                _rcopy(dst, dst, ssem.at[w, k], rsem.at[w, k], (px, py, mc)).wait_recv()
        for cp in sends:
            cp.wait_send()

    dma = pltpu.SemaphoreType.DMA
    return _multi_call(name, body, xs, [jax.ShapeDtypeStruct(x.shape, x.dtype) for x in xs], [dma((n, 3)), dma((n, 3))])


def comm_sibling_join(name, bs):
    n = len(bs)

    def body(*refs):
        os_ = refs[n:2 * n]
        ssem, rsem = refs[2 * n:]
        mx, my, mc = _pos()
        sib = (mx, my, 1 - mc)
        sends = [_rcopy(os_[w].at[mc], os_[w].at[mc], ssem.at[w], rsem.at[w], sib) for w in range(n)]
        for cp in sends:
            cp.start()
        for w in range(n):
            dst = os_[w].at[1 - mc]
            _rcopy(dst, dst, ssem.at[w], rsem.at[w], sib).wait_recv()
        for cp in sends:
            cp.wait_send()

    dma = pltpu.SemaphoreType.DMA
    return _multi_call(name, body, bs, [jax.ShapeDtypeStruct(b.shape, b.dtype) for b in bs], [dma((n,)), dma((n,))],
                       in_place=True)


def comm_sibling_swap(name, x):
    def body(x_ref, o_ref, ssem, rsem):
        mx, my, mc = _pos()
        cp = _rcopy(x_ref, o_ref, ssem, rsem, (mx, my, 1 - mc))
        cp.start()
        cp.wait_recv()
        cp.wait_send()

    return pl.pallas_call(
        body, out_shape=jax.ShapeDtypeStruct(x.shape, x.dtype), in_specs=[_ANY], out_specs=_ANY,
        scratch_shapes=[pltpu.SemaphoreType.DMA, pltpu.SemaphoreType.DMA], name=name)(x)


def _row_tile(r):
    return _pick(r, (256, 128, 64, 32, 16, 8))


def add_own_half(name, g, recv, out_dtype):
    r, c = g.shape[2:]
    tr = _row_tile(r)
    mc = lax.axis_index("c").astype(jnp.int32).reshape(1)

    def body(c_ref, g_ref, r_ref, o_ref):
        o_ref[...] = (g_ref[...] + r_ref[...]).astype(o_ref.dtype)

    grid_spec = pltpu.PrefetchScalarGridSpec(
        num_scalar_prefetch=1, grid=(N_CHIPS, r // tr),
        in_specs=[pl.BlockSpec((None, None, tr, c), lambda j, i, cr: (j, cr[0], i, 0)),
                  pl.BlockSpec((None, tr, c), lambda j, i, cr: (j, i, 0))],
        out_specs=pl.BlockSpec((None, tr, c), lambda j, i, cr: (j, i, 0)))
    return pl.pallas_call(body, grid_spec=grid_spec, out_shape=jax.ShapeDtypeStruct(recv.shape, out_dtype),
                          compiler_params=_cparams("parallel", "parallel"), name=name)(mc, g, recv)


def cast_into_slot(name, w, chip, dtype):
    r, c = w.shape
    tr = _row_tile(r)

    def body(c_ref, w_ref, o_ref):
        o_ref[...] = w_ref[...].astype(o_ref.dtype)

    grid_spec = pltpu.PrefetchScalarGridSpec(
        num_scalar_prefetch=1, grid=(r // tr,),
        in_specs=[pl.BlockSpec((tr, c), lambda i, cr: (i, 0))],
        out_specs=pl.BlockSpec((None, tr, c), lambda i, cr: (cr[0], i, 0)))
    return pl.pallas_call(body, grid_spec=grid_spec, out_shape=jax.ShapeDtypeStruct((N_CHIPS, r, c), dtype),
                          compiler_params=_cparams("parallel"), name=name)(chip.astype(jnp.int32).reshape(1), w)


def sum_chips_into_half(name, own, arrived, chip, mc):
    r, c = own.shape[1:]
    tr = _row_tile(r)

    def body(s0, s1, s2, s3, s4, own_ref, a_ref, b_ref, d_ref, o_ref):
        o_ref[...] = ((own_ref[...].astype(f32) + a_ref[...].astype(f32))
                      + (b_ref[...].astype(f32) + d_ref[...].astype(f32)))

    slot = lambda k: pl.BlockSpec((None, tr, c), lambda i, *sc, _k=k: (sc[_k][0], i, 0))
    grid_spec = pltpu.PrefetchScalarGridSpec(
        num_scalar_prefetch=5, grid=(r // tr,), in_specs=[slot(0), slot(1), slot(2), slot(3)],
        out_specs=pl.BlockSpec((None, tr, c), lambda i, *sc: (sc[4][0], i, 0)))
    mx, my = lax.axis_index("x"), lax.axis_index("y")
    scal = [v.astype(jnp.int32).reshape(1) for v in
            (2 * mx + my, 2 * (1 - mx) + my, 2 * mx + (1 - my), 2 * (1 - mx) + (1 - my), mc)]
    return pl.pallas_call(body, grid_spec=grid_spec, out_shape=jax.ShapeDtypeStruct((2, r, c), f32),
                          compiler_params=_cparams("parallel"), name=name)(*scal, own, arrived, arrived, arrived)


def sum_slots(name, x):
    r, c = x.shape[1:]
    tr = _row_tile(r)

    def body(x_ref, o_ref):
        o_ref[...] = (x_ref[0].astype(f32) + x_ref[1].astype(f32)) + (x_ref[2].astype(f32) + x_ref[3].astype(f32))

    return pl.pallas_call(
        body, grid=(r // tr,), in_specs=[pl.BlockSpec((N_CHIPS, tr, c), lambda i: (0, i, 0))],
        out_specs=pl.BlockSpec((tr, c), lambda i: (i, 0)), out_shape=jax.ShapeDtypeStruct((r, c), f32),
        compiler_params=_cparams("parallel"), name=name)(x)


def adamw(name, w, g, m, v):
    r, c = w.shape
    tr = _row_tile(r)

    def body(w_ref, g_ref, m_ref, v_ref, d_ref, nm_ref, nv_ref):
        gv = g_ref[...]
        nm = ADAM_B1 * m_ref[...] + (1.0 - ADAM_B1) * gv
        nv = ADAM_B2 * v_ref[...] + (1.0 - ADAM_B2) * (gv * gv)
        m_hat = nm / (1.0 - ADAM_B1 ** ADAM_STEP)
        v_hat = nv / (1.0 - ADAM_B2 ** ADAM_STEP)
        d_ref[...] = -ADAM_LR * (m_hat / (jnp.sqrt(v_hat) + ADAM_EPS) + ADAM_WD * w_ref[...])
        nm_ref[...] = nm
        nv_ref[...] = nv

    blk = pl.BlockSpec((tr, c), lambda i: (i, 0))
    return pl.pallas_call(
        body, grid=(r // tr,), in_specs=[blk] * 4, out_specs=[blk] * 3,
        out_shape=[jax.ShapeDtypeStruct((r, c), f32)] * 3,
        compiler_params=_cparams("parallel"), name=name)(w, g, m, v)


def _pack_rows(n):
    return -(-n // PACK_COLS)


def _pack(arrs, dtype, row_multiple):
    segs = []
    for a in arrs:
        flat = a.astype(dtype).reshape(-1)
        k = _pack_rows(flat.shape[0])
        segs.append(jnp.pad(flat, (0, k * PACK_COLS - flat.shape[0])).reshape(k, PACK_COLS))
    rows = sum(s.shape[0] for s in segs)
    pad = -rows % row_multiple
    if pad:
        segs.append(jnp.zeros((pad, PACK_COLS), dtype))
    return jnp.concatenate(segs, 0)


def _unpack(packed, shapes):
    out, r = [], 0
    for shp in shapes:
        n = math.prod(shp)
        k = _pack_rows(n)
        out.append(packed[r:r + k].reshape(-1)[:n].reshape(shp))
        r += k
    return out


def _gathered_to_full(g, axis):
    t = jnp.moveaxis(g, 0, axis)
    return t.reshape(t.shape[:axis] + (t.shape[axis] * t.shape[axis + 1],) + t.shape[axis + 2:])


def _full_to_shard_major(full, axis):
    shp = full.shape
    t = full.reshape(shp[:axis] + (N_CHIPS, shp[axis] // N_CHIPS) + shp[axis + 1:])
    return jnp.moveaxis(t, axis, 0)


GRAD_ROW_MULTIPLE = 256


def kernel(x, mem, positions, hyb_w_in, dn_conv_w, dn_a_log, dn_dt_bias, dn_norm_g, hyb_w_out, s5_a_re, s5_a_im, s5_log_dt, s5_b_re, s5_b_im, s5_c_re, s5_c_im, s5_d, s5_glu_wo, s5_glu_wg, ln_mix_g, ln_mix_b, xq_w, xk_w, xv_w, xo_w, ln_x_g, ln_x_b, ffn_wg, ffn_wu, ffn_wd, ln_ffn_g, ln_ffn_b, loss_target, m_hyb_w_in, m_dn_conv_w, m_dn_a_log, m_dn_dt_bias, m_dn_norm_g, m_hyb_w_out, m_s5_a_re, m_s5_a_im, m_s5_log_dt, m_s5_b_re, m_s5_b_im, m_s5_c_re, m_s5_c_im, m_s5_d, m_s5_glu_wo, m_s5_glu_wg, m_ln_mix_g, m_ln_mix_b, m_xq_w, m_xk_w, m_xv_w, m_xo_w, m_ln_x_g, m_ln_x_b, m_ffn_wg, m_ffn_wu, m_ffn_wd, m_ln_ffn_g, m_ln_ffn_b, v_hyb_w_in, v_dn_conv_w, v_dn_a_log, v_dn_dt_bias, v_dn_norm_g, v_hyb_w_out, v_s5_a_re, v_s5_a_im, v_s5_log_dt, v_s5_b_re, v_s5_b_im, v_s5_c_re, v_s5_c_im, v_s5_d, v_s5_glu_wo, v_s5_glu_wg, v_ln_mix_g, v_ln_mix_b, v_xq_w, v_xk_w, v_xv_w, v_xo_w, v_ln_x_g, v_ln_x_b, v_ffn_wg, v_ffn_wu, v_ffn_wd, v_ln_ffn_g, v_ln_ffn_b):
    a = dict(locals())
    big = [n for n in WEIGHT_NAMES if n in SHARD_AXIS and n not in GATHER_F32]
    small = [n for n in WEIGHT_NAMES if n not in big]
    chip = 2 * lax.axis_index("x") + lax.axis_index("y")
    for n in FFN_TRANSPOSED:
        for pre in ("", "m_", "v_"):
            a[pre + n] = jnp.swapaxes(a[pre + n], 1, 2)

    mc = lax.axis_index("c")
    view2 = lambda t: t.reshape(-1, t.shape[-1])
    slots = [cast_into_slot("slot_" + n, view2(a[n]), chip, bf16).reshape((N_CHIPS,) + a[n].shape) for n in big]
    gathered = comm_gather_weights("gather_w", slots)
    tiny4 = _unpack_slots(comm_allgather4("gather_w_tiny", _pack([a[n] for n in GATHER_F32], f32, 8)),
                          [a[n].shape for n in GATHER_F32])
    p = {n: a[n] for n in small if n not in GATHER_F32}
    for n, g4 in zip(GATHER_F32, tiny4):
        p[n] = _gathered_to_full(g4, SHARD_AXIS[n])
    for n, g4 in zip(big, gathered):
        p[n] = g4 if n in SHARD_ORDER_GRADS else _gathered_to_full(g4, SHARD_AXIS[n])

    loss, grad_x, grads = local_step(x[0], mem[0], positions, loss_target[0], p)
    loss = lax.psum(loss, ("x", "y", "c"))

    g4s = [grads[n] if n in SHARD_ORDER_GRADS else _full_to_shard_major(grads[n], SHARD_AXIS[n]) for n in big]
    recv = comm_sibling_halves("rs_sibling_halves", g4s)
    pairs = []
    for n, g4, r4 in zip(big, g4s, recv):
        lh, cols = g4.shape[1] // 2, g4.shape[-1]
        v4 = g4.reshape(N_CHIPS, 2, -1, cols)
        pairs.append(add_own_half("rs_add_" + n, v4, r4.reshape(N_CHIPS, -1, cols), bf16).reshape((N_CHIPS, lh) + g4.shape[2:]))
    arrived = comm_alltoall4("rs_alltoall", pairs)
    slot3 = lambda t: t.reshape(N_CHIPS, -1, t.shape[-1])
    halves = [sum_chips_into_half("rs_sum_" + n, slot3(pr), slot3(ar), chip, mc) for n, pr, ar in zip(big, pairs, arrived)]
    g_big = {n: t.reshape(a[n].shape) for n, t in zip(big, comm_sibling_join("rs_sibling_join", halves))}

    rpack = _pack([grads[n] for n in small], f32, 8)
    rpair = _add2("ar_add_sibling", rpack, comm_sibling_swap("ar_sibling_swap", rpack))
    g_small = _unpack(sum_slots("ar_sum_chips", comm_allgather4("ar_allgather", rpair)), [grads[n].shape for n in small])
    g_small = {n: (lax.dynamic_index_in_dim(_full_to_shard_major(g, SHARD_AXIS[n]), chip, 0, keepdims=False)
                   if n in SHARD_AXIS else g) for n, g in zip(small, g_small)}

    outs = {}
    for n in big:
        view = lambda t: t.reshape(-1, t.shape[-1])
        d, nm, nv = adamw("adamw_" + n, view(a[n]), view(g_big[n]), view(a["m_" + n]), view(a["v_" + n]))
        outs[n] = (g_big[n],) + tuple(t.reshape(a[n].shape) for t in (d, nm, nv))
    shapes = [a[n].shape for n in small]
    packs = [_pack([a[pre + n] for n in small], f32, 8) for pre in ("", "m_", "v_")]
    upd = adamw("adamw_small", packs[0], _pack([g_small[n] for n in small], f32, 8), packs[1], packs[2])
    for k, n in enumerate(small):
        outs[n] = (g_small[n],) + tuple(_unpack(buf, shapes)[k] for buf in upd)
    for n in FFN_TRANSPOSED:
        outs[n] = tuple(jnp.swapaxes(t, 1, 2) for t in outs[n])
    res = [loss, grad_x[None]]
    for kind in range(4):
        res += [outs[n][kind] for n in WEIGHT_NAMES]
    return tuple(res)


def _unpack_slots(gathered, shapes):
    out, r = [], 0
    for shp in shapes:
        n = math.prod(shp)
        k = _pack_rows(n)
        out.append(gathered[:, r:r + k].reshape(N_CHIPS, -1)[:, :n].reshape((N_CHIPS,) + tuple(shp)))
        r += k
    return out
```

```python
import functools
import math

import jax
import jax.numpy as jnp
from jax import lax
from jax.experimental import pallas as pl
from jax.experimental.pallas import tpu as pltpu

f32 = jnp.float32
bf16 = jnp.bfloat16

D_MODEL = 1024
DEPTH = 4
DN_HEADS = 4
DN_HEAD_DIM = 128
DN_KEY_DIM = 512
DN_QKV_DIM = 1536
DN_CONV = 4
SW_HEADS = 8
SW_HEAD_DIM = 64
SW_DIM = 512
SW_DILATIONS = (1, 4, 16)
SW_BLOCK = 128
ROPE_THETA = 10000.0
S5_GROUP = 16
S5_GROUPS = 64
S5_STATE = 64
X_HEADS = 4
X_HEAD_DIM = 256
FFN_HIDDEN = 2816
ALPHA = (2 * DEPTH) ** 0.25
LN_EPS = 1e-5
RMS_EPS = 1e-6
ADAM_LR, ADAM_B1, ADAM_B2, ADAM_EPS, ADAM_WD, ADAM_STEP = 0.001, 0.9, 0.999, 1e-08, 0.01, 10

BA_PAD = 256
PROJ_COLS = DN_QKV_DIM + DN_KEY_DIM + 3 * SW_DIM + BA_PAD
COL_Z = DN_QKV_DIM
COL_SWQ = COL_Z + DN_KEY_DIM
COL_SWK = COL_SWQ + SW_DIM
COL_SWV = COL_SWK + SW_DIM
COL_BA = COL_SWV + SW_DIM

LANES = 128
SUBLANES = 8
VMEM_LIMIT = 56 * 1024 * 1024
DN_CHUNK = 128
DN_HEADS_PER_STEP = 4


def _cparams(*sem):
    return pltpu.CompilerParams(dimension_semantics=tuple(sem), vmem_limit_bytes=VMEM_LIMIT)


def _dg(x, y, cx, cy):
    return lax.dot_general(x, y, (((cx,), (cy,)), ((), ())), preferred_element_type=f32)


@functools.partial(jax.custom_vjp, nondiff_argnums=(2, 3))
def bdot(a, b, ca, cb):
    return _dg(a.astype(bf16), b.astype(bf16), ca, cb)


def _bdot_fwd(a, b, ca, cb):
    return bdot(a, b, ca, cb), (a, b)


def _bdot_bwd(ca, cb, res, g):
    a, b = res
    g16, a16, b16 = g.astype(bf16), a.astype(bf16), b.astype(bf16)
    da = _dg(g16, b16, 1, 1 - cb) if ca == 1 else _dg(b16, g16, 1 - cb, 1)
    db = _dg(a16, g16, 1 - ca, 0) if cb == 0 else _dg(g16, a16, 0, 1 - ca)
    return da.astype(a.dtype), db.astype(b.dtype)


bdot.defvjp(_bdot_fwd, _bdot_bwd)


def _split_hi_lo(a):
    hi = a.astype(bf16)
    return hi, (a - hi.astype(f32)).astype(bf16)


def _dot3(a, b, ca, cb):
    a_hi, a_lo = _split_hi_lo(a)
    b_hi, b_lo = _split_hi_lo(b)
    return _dg(a_hi, b_hi, ca, cb) + (_dg(a_hi, b_lo, ca, cb) + _dg(a_lo, b_hi, ca, cb))


@jax.custom_vjp
def hdot3(a, b):
    return _dot3(a, b, 1, 0)


def _hdot3_fwd(a, b):
    return hdot3(a, b), (a, b)


def _hdot3_bwd(res, g):
    a, b = res
    return _dot3(g, b, 1, 1), _dot3(a, g, 0, 0)


hdot3.defvjp(_hdot3_fwd, _hdot3_bwd)


def hdot(a, b):
    return jnp.dot(a, b, precision=lax.Precision.HIGHEST, preferred_element_type=f32)


def _iota2(shape, dim):
    return lax.broadcasted_iota(jnp.int32, shape, dim)


def _row_spec(r, tm):
    if isinstance(r, tuple):
        arr, width, blk = r
        return arr, pl.BlockSpec((tm, width), lambda i, _b=blk: (i, _b))
    return r, pl.BlockSpec((tm, r.shape[1]), lambda i: (i, 0))


def _par_spec(p):
    return pl.BlockSpec(p.shape, lambda i, _n=p.ndim: (0,) * _n)


def rowmap(name, fn, rows, params, out_cols, tm, out_dtypes=None):
    arrs, specs = zip(*[_row_spec(r, tm) for r in rows])
    s = arrs[0].shape[0]
    n_in = len(rows) + len(params)
    out_dtypes = out_dtypes or [f32] * len(out_cols)

    def body(*refs):
        outs = fn(*[r[...] for r in refs[:n_in]])
        for o_ref, o in zip(refs[n_in:], outs):
            o_ref[...] = o.astype(o_ref.dtype)

    return pl.pallas_call(
        body, grid=(s // tm,),
        in_specs=list(specs) + [_par_spec(p) for p in params],
        out_specs=[pl.BlockSpec((tm, c), lambda i: (i, 0)) for c in out_cols],
        out_shape=[jax.ShapeDtypeStruct((s, c), dt) for c, dt in zip(out_cols, out_dtypes)],
        compiler_params=_cparams("parallel"), name=name)(*arrs, *params)


def rowmap_bwd(name, fn, rows, params, cts, tm, row_mask=None, par_mask=None, row_dtypes=None):
    arrs, specs = zip(*[_row_spec(r, tm) for r in rows])
    s = arrs[0].shape[0]
    ct_groups = [c if isinstance(c, list) else [c] for c in cts]
    ct_arrs, ct_specs = zip(*[_row_spec(a, tm) for grp in ct_groups for a in grp])
    cts = list(ct_arrs)
    nr, npar, nct = len(rows), len(params), len(cts)
    row_mask = row_mask or [True] * nr
    par_mask = par_mask or [True] * npar
    row_idx = [k for k in range(nr) if row_mask[k]]
    par_idx = [k for k in range(npar) if par_mask[k]]
    row_w = [specs[k].block_shape[1] for k in row_idx]

    def body(*refs):
        ins = [r[...] for r in refs[:nr + npar]]
        ct_refs = list(refs[nr + npar:nr + npar + nct])
        ctv = []
        for grp in ct_groups:
            acc = ct_refs.pop(0)[...].astype(f32)
            for _ in grp[1:]:
                acc = acc + ct_refs.pop(0)[...].astype(f32)
            ctv.append(acc)
        ctv = tuple(ctv)
        outs = refs[nr + npar + nct:]
        _, vjp = jax.vjp(fn, *ins)
        grads = vjp(ctv)
        for o_ref, k in zip(outs[:len(row_idx)], row_idx):
            o_ref[...] = grads[k].astype(o_ref.dtype)
        first = pl.program_id(0) == 0
        for o_ref, k in zip(outs[len(row_idx):], par_idx):
            g = grads[nr + k].astype(f32)

            @pl.when(first)
            def _(o_ref=o_ref, g=g):
                o_ref[...] = g

            @pl.when(jnp.logical_not(first))
            def _(o_ref=o_ref, g=g):
                o_ref[...] += g

    res = pl.pallas_call(
        body, grid=(s // tm,),
        in_specs=list(specs) + [_par_spec(p) for p in params]
        + list(ct_specs),
        out_specs=[pl.BlockSpec((tm, w), lambda i: (i, 0)) for w in row_w]
        + [_par_spec(params[k]) for k in par_idx],
        out_shape=[jax.ShapeDtypeStruct((s, w), dt) for w, dt in zip(row_w, row_dtypes or [f32] * len(row_w))]
        + [jax.ShapeDtypeStruct(params[k].shape, f32) for k in par_idx],
        compiler_params=_cparams("arbitrary"), name=name)(*arrs, *params, *cts)
    return list(res[:len(row_idx)]), list(res[len(row_idx):])


def _pick(n, prefs):
    for t in prefs:
        if n % t == 0:
            return t
    return n


MM_CHUNK = 512


def mm_nn(name, a, b, out_dtype=f32):
    m, k = a.shape
    n = b.shape[1]
    tm = _pick(m, (512, 256, 128))
    cn = _pick(n, (MM_CHUNK, 256, 128))

    def body(a_ref, b_ref, o_ref):
        av = a_ref[...].astype(bf16)
        for c in range(n // cn):
            sl = slice(c * cn, (c + 1) * cn)
            o_ref[:, sl] = _dg(av, b_ref[:, sl].astype(bf16), 1, 0).astype(o_ref.dtype)

    return pl.pallas_call(
        body, grid=(m // tm,),
        in_specs=[pl.BlockSpec((tm, k), lambda i: (i, 0)), pl.BlockSpec((k, n), lambda i: (0, 0))],
        out_specs=pl.BlockSpec((tm, n), lambda i: (i, 0)),
        out_shape=jax.ShapeDtypeStruct((m, n), out_dtype),
        compiler_params=_cparams("parallel"), name=name)(a, b)


def mm_nt(name, a, b, out_dtype=f32):
    m, n = a.shape
    k = b.shape[0]
    tm = _pick(m, (512, 256, 128))
    ck = _pick(k, (MM_CHUNK, 256, 128))

    def body(a_ref, b_ref, o_ref):
        av = a_ref[...].astype(bf16)
        for c in range(k // ck):
            sl = slice(c * ck, (c + 1) * ck)
            o_ref[:, sl] = _dg(av, b_ref[sl, :].astype(bf16), 1, 1).astype(o_ref.dtype)

    return pl.pallas_call(
        body, grid=(m // tm,),
        in_specs=[pl.BlockSpec((tm, n), lambda i: (i, 0)), pl.BlockSpec((k, n), lambda i: (0, 0))],
        out_specs=pl.BlockSpec((tm, k), lambda i: (i, 0)),
        out_shape=jax.ShapeDtypeStruct((m, k), out_dtype),
        compiler_params=_cparams("parallel"), name=name)(a, b)


def mm_tn(name, a, b, out_dtype=f32):
    s, m = a.shape
    n = b.shape[1]
    tn = _pick(n, (256, 128))
    cm = _pick(m, (256, 128))

    def body(a_ref, b_ref, o_ref):
        bv = b_ref[...].astype(bf16)
        for c in range(m // cm):
            sl = slice(c * cm, (c + 1) * cm)
            o_ref[sl, :] = _dg(a_ref[:, sl].astype(bf16), bv, 0, 0).astype(o_ref.dtype)

    return pl.pallas_call(
        body, grid=(n // tn,),
        in_specs=[pl.BlockSpec((s, m), lambda j: (0, 0)), pl.BlockSpec((s, tn), lambda j: (0, j))],
        out_specs=pl.BlockSpec((m, tn), lambda j: (0, j)),
        out_shape=jax.ShapeDtypeStruct((m, n), out_dtype),
        compiler_params=_cparams("parallel"), name=name)(a, b)


def _postnorm_tile(h, sub, g, b):
    z = ALPHA * h + sub
    mu = jnp.mean(z, -1, keepdims=True)
    zc = z - mu
    var = jnp.mean(zc * zc, -1, keepdims=True)
    return (zc * lax.rsqrt(var + LN_EPS) * g + b,)


def _swiglu_tile(au):
    a, u = au[:, :FFN_HIDDEN], au[:, FFN_HIDDEN:]
    return (jax.nn.silu(a) * u,)


def _glu_tile(og):
    o, g = og[:, :D_MODEL], og[:, D_MODEL:]
    return (o * jax.nn.sigmoid(g),)


def _xattn_tile(q, kv):
    outs = []
    for h in range(X_HEADS):
        sl = slice(h * X_HEAD_DIM, (h + 1) * X_HEAD_DIM)
        s = bdot(q[:, sl], kv[:, sl], 1, 1) * (X_HEAD_DIM ** -0.5)
        m = lax.stop_gradient(jnp.max(s, -1, keepdims=True))
        p = jnp.exp(s - m)
        p = p / jnp.sum(p, -1, keepdims=True)
        outs.append(bdot(p, kv[:, D_MODEL + h * X_HEAD_DIM:D_MODEL + (h + 1) * X_HEAD_DIM], 1, 0))
    return (jnp.concatenate(outs, -1),)


TM_ROW = 256


def postnorm_fwd(tag, h, sub, g, b):
    return rowmap("postnorm_" + tag, lambda *a: _postnorm_tile(*a) * 2, [h, sub], [g, b], [D_MODEL] * 2, TM_ROW,
                  out_dtypes=[f32, bf16])


def postnorm_bwd(tag, h, sub, g, b, dy):
    (dh, dsub), (dg, db) = rowmap_bwd("postnorm_bwd_" + tag, _postnorm_tile, [h, sub], [g, b], [dy], TM_ROW,
                                      row_dtypes=[f32, bf16])
    return dh, dsub, dg, db


def xattn_fwd(tag, h, mem, wq, wkv, wo):
    q = mm_nn("xq_" + tag, h, wq, out_dtype=bf16)
    kv = mm_nn("xkv_" + tag, mem, wkv)
    ao = rowmap("xattn_" + tag, _xattn_tile, [q], [kv], [D_MODEL], TM_ROW, out_dtypes=[bf16])[0]
    out = mm_nn("xo_" + tag, ao, wo)
    return out, (q, kv, ao)


def xattn_bwd(tag, h, mem, wq, wkv, wo, res, dout):
    q, kv, ao = res
    dwo = mm_tn("xo_dw_" + tag, ao, dout)
    dao = mm_nt("xo_dx_" + tag, dout, wo)
    (dq,), (dkv,) = rowmap_bwd("xattn_bwd_" + tag, _xattn_tile, [q], [kv], [dao], TM_ROW, row_dtypes=[bf16])
    dwq = mm_tn("xq_dw_" + tag, h, dq)
    dh = mm_nt("xq_dx_" + tag, dq, wq)
    dwkv = mm_tn("xkv_dw_" + tag, mem, dkv)
    return dh, dwq, dwkv, dwo


FFN_SHARD = FFN_HIDDEN // 4
TM_FFN = 512


def _silu_mul(a, u):
    return jax.nn.silu(a) * u


def ffn_fwd(tag, layer, h, wg, wu, wd):
    s = h.shape[0]
    tm, fs = TM_FFN, FFN_SHARD
    w_in = pl.BlockSpec((None, None, fs, D_MODEL), lambda i, k: (k, layer, 0, 0))
    act = pl.BlockSpec((None, tm, fs), lambda i, k: (k, i, 0))

    def up_body(h_ref, wg_ref, wu_ref, a_ref, u_ref, hid_ref):
        hv = h_ref[...].astype(bf16)
        a, u = _dg(hv, wg_ref[...], 1, 1), _dg(hv, wu_ref[...], 1, 1)
        a_ref[...], u_ref[...] = a.astype(bf16), u.astype(bf16)
        hid_ref[...] = _silu_mul(a, u).astype(bf16)

    a4, u4, hid4 = pl.pallas_call(
        up_body, grid=(s // tm, 4),
        in_specs=[pl.BlockSpec((tm, D_MODEL), lambda i, k: (i, 0)), w_in, w_in],
        out_specs=[act, act, act],
        out_shape=[jax.ShapeDtypeStruct((4, s, fs), bf16)] * 3,
        compiler_params=_cparams("parallel", "parallel"), name="ffn_up_" + tag)(h, wg, wu)

    def down_body(hid_ref, wd_ref, o_ref):
        part = _dg(hid_ref[...], wd_ref[...], 1, 0)

        @pl.when(pl.program_id(1) == 0)
        def _():
            o_ref[...] = part

        @pl.when(pl.program_id(1) != 0)
        def _():
            o_ref[...] += part

    out = pl.pallas_call(
        down_body, grid=(s // tm, 4),
        in_specs=[act, pl.BlockSpec((None, None, fs, D_MODEL), lambda i, k: (k, layer, 0, 0))],
        out_specs=pl.BlockSpec((tm, D_MODEL), lambda i, k: (i, 0)),
        out_shape=jax.ShapeDtypeStruct((s, D_MODEL), f32),
        compiler_params=_cparams("parallel", "arbitrary"), name="ffn_down_" + tag)(hid4, wd)
    return out, (a4, u4, hid4)


def ffn_bwd(tag, layer, h, wg, wu, wd, res, dout):
    a4, u4, hid4 = res
    s = h.shape[0]
    tm, fs = TM_FFN, FFN_SHARD
    act = pl.BlockSpec((None, tm, fs), lambda i, k: (k, i, 0))
    w_in = pl.BlockSpec((None, None, fs, D_MODEL), lambda i, k: (k, layer, 0, 0))

    def dact_body(do_ref, wd_ref, a_ref, u_ref, da_ref, du_ref):
        dhid = _dg(do_ref[...].astype(bf16), wd_ref[...], 1, 1)
        _, vjp = jax.vjp(_silu_mul, a_ref[...].astype(f32), u_ref[...].astype(f32))
        da, du = vjp(dhid)
        da_ref[...], du_ref[...] = da.astype(bf16), du.astype(bf16)

    da4, du4 = pl.pallas_call(
        dact_body, grid=(s // tm, 4),
        in_specs=[pl.BlockSpec((tm, D_MODEL), lambda i, k: (i, 0)),
                  pl.BlockSpec((None, None, fs, D_MODEL), lambda i, k: (k, layer, 0, 0)), act, act],
        out_specs=[act, act], out_shape=[jax.ShapeDtypeStruct((4, s, fs), bf16)] * 2,
        compiler_params=_cparams("parallel", "parallel"), name="ffn_dact_" + tag)(dout, wd, a4, u4)

    def dx_body(da_ref, du_ref, wg_ref, wu_ref, o_ref):
        part = _dg(da_ref[...], wg_ref[...], 1, 0) + _dg(du_ref[...], wu_ref[...], 1, 0)

        @pl.when(pl.program_id(1) == 0)
        def _():
            o_ref[...] = part

        @pl.when(pl.program_id(1) != 0)
        def _():
            o_ref[...] += part

    dh = pl.pallas_call(
        dx_body, grid=(s // tm, 4), in_specs=[act, act, w_in, w_in],
        out_specs=pl.BlockSpec((tm, D_MODEL), lambda i, k: (i, 0)),
        out_shape=jax.ShapeDtypeStruct((s, D_MODEL), f32),
        compiler_params=_cparams("parallel", "arbitrary"), name="ffn_dx_" + tag)(da4, du4, wg, wu)

    tn = 256
    whole = pl.BlockSpec((None, s, fs), lambda k, j: (k, 0, 0))

    def dwin_body(h_ref, da_ref, du_ref, dwg_ref, dwu_ref):
        hv = h_ref[...].astype(bf16)
        dwg_ref[...] = _dg(da_ref[...], hv, 0, 0)
        dwu_ref[...] = _dg(du_ref[...], hv, 0, 0)

    dwg, dwu = pl.pallas_call(
        dwin_body, grid=(4, D_MODEL // tn),
        in_specs=[pl.BlockSpec((s, tn), lambda k, j: (0, j)), whole, whole],
        out_specs=[pl.BlockSpec((None, fs, tn), lambda k, j: (k, 0, j))] * 2,
        out_shape=[jax.ShapeDtypeStruct((4, fs, D_MODEL), f32)] * 2,
        compiler_params=_cparams("parallel", "parallel"), name="ffn_dwin_" + tag)(h, da4, du4)

    def dwd_body(hid_ref, do_ref, dwd_ref):
        dwd_ref[...] = _dg(hid_ref[...], do_ref[...].astype(bf16), 0, 0)

    dwd = pl.pallas_call(
        dwd_body, grid=(4, D_MODEL // tn),
        in_specs=[whole, pl.BlockSpec((s, tn), lambda k, j: (0, j))],
        out_specs=pl.BlockSpec((None, fs, tn), lambda k, j: (k, 0, j)),
        out_shape=jax.ShapeDtypeStruct((4, fs, D_MODEL), f32),
        compiler_params=_cparams("parallel", "parallel"), name="ffn_dwd_" + tag)(hid4, dout)
    return dh, dwg, dwu, dwd


def loss_head(y, target):
    s, d = y.shape
    tm = TM_ROW

    def body(y_ref, t_ref, part_ref, dy_ref):
        e = y_ref[...] - t_ref[...]
        dy_ref[...] = e * (1.0 / d)
        p = jnp.sum(e * e, 0, keepdims=True) * (0.5 / d)

        @pl.when(pl.program_id(0) == 0)
        def _():
            part_ref[...] = p

        @pl.when(pl.program_id(0) != 0)
        def _():
            part_ref[...] += p

    return pl.pallas_call(
        body, grid=(s // tm,),
        in_specs=[pl.BlockSpec((tm, d), lambda i: (i, 0))] * 2,
        out_specs=[pl.BlockSpec((1, d), lambda i: (0, 0)), pl.BlockSpec((tm, d), lambda i: (i, 0))],
        out_shape=[jax.ShapeDtypeStruct((1, d), f32), jax.ShapeDtypeStruct((s, d), f32)],
        compiler_params=_cparams("arbitrary"), name="loss_head")(y, target)


TM_CONV = 512


def _conv_rows(xx, w_ref, n_rows):
    a = w_ref[3:4, :] * xx
    for k in (1, 2, 3):
        a = a + w_ref[3 - k:4 - k, :] * pltpu.roll(xx, k, 0)
    return a


def _dn_act(a, is_qk):
    s = jax.nn.silu(a)
    n = s * lax.rsqrt(jnp.sum(s * s, -1, keepdims=True) + RMS_EPS)
    return jnp.where(is_qk, n, s)


def dn_conv_fwd(tag, proj, cw):
    s = proj.shape[0]
    tm, hb = TM_CONV, TM_CONV // SUBLANES

    def body(xh_ref, x_ref, w_ref, o_ref):
        j, t = pl.program_id(0), pl.program_id(1)
        halo = jnp.where(t > 0, xh_ref[...], 0.0)
        xx = jnp.concatenate([halo, x_ref[...]], 0)
        a = _conv_rows(xx, w_ref, tm + SUBLANES)
        o_ref[...] = _dn_act(a, j < 2 * DN_HEADS)[SUBLANES:, :]

    return pl.pallas_call(
        body, grid=(DN_QKV_DIM // LANES, s // tm),
        in_specs=[pl.BlockSpec((SUBLANES, LANES), lambda j, t: (jnp.maximum(t * hb - 1, 0), j)),
                  pl.BlockSpec((tm, LANES), lambda j, t: (t, j)),
                  pl.BlockSpec((DN_CONV, LANES), lambda j, t: (0, j))],
        out_specs=pl.BlockSpec((tm, LANES), lambda j, t: (t, j)),
        out_shape=jax.ShapeDtypeStruct((s, DN_QKV_DIM), f32),
        compiler_params=_cparams("parallel", "parallel"), name="dn_conv_" + tag)(proj, proj, cw)


def dn_conv_bwd(tag, proj, cw, dy):
    s = proj.shape[0]
    tm, hb = TM_CONV, TM_CONV // SUBLANES
    nt = s // tm
    n_ext = tm + 2 * SUBLANES

    def body(xb_ref, x_ref, xa_ref, dy_ref, dya_ref, w_ref, dx_ref, dw_ref):
        j, t = pl.program_id(0), pl.program_id(1)
        xx = jnp.concatenate([jnp.where(t > 0, xb_ref[...], 0.0), x_ref[...],
                              jnp.where(t < nt - 1, xa_ref[...], 0.0)], 0)
        dyy = jnp.concatenate([jnp.zeros((SUBLANES, LANES), f32), dy_ref[...],
                               jnp.where(t < nt - 1, dya_ref[...], 0.0)], 0)
        a = _conv_rows(xx, w_ref, n_ext)
        _, vjp = jax.vjp(lambda v: _dn_act(v, j < 2 * DN_HEADS), a)
        da, = vjp(dyy)
        dx = w_ref[3:4, :] * da
        for k in (1, 2, 3):
            dx = dx + w_ref[3 - k:4 - k, :] * pltpu.roll(da, n_ext - k, 0)
        dx_ref[...] = dx[SUBLANES:SUBLANES + tm, :]
        row = _iota2((n_ext, LANES), 0)
        da_in = jnp.where((row >= SUBLANES) & (row < SUBLANES + tm), da, 0.0)
        r8 = _iota2((SUBLANES, LANES), 0)
        dw = jnp.zeros((SUBLANES, LANES), f32)
        for k in range(DN_CONV):
            xs = xx if k == 0 else pltpu.roll(xx, k, 0)
            dw = dw + jnp.where(r8 == 3 - k, jnp.sum(da_in * xs, 0, keepdims=True), 0.0)

        @pl.when(t == 0)
        def _():
            dw_ref[...] = dw

        @pl.when(t != 0)
        def _():
            dw_ref[...] += dw

    nb8 = s // SUBLANES
    return pl.pallas_call(
        body, grid=(DN_QKV_DIM // LANES, nt),
        in_specs=[pl.BlockSpec((SUBLANES, LANES), lambda j, t: (jnp.maximum(t * hb - 1, 0), j)),
                  pl.BlockSpec((tm, LANES), lambda j, t: (t, j)),
                  pl.BlockSpec((SUBLANES, LANES), lambda j, t: (jnp.minimum((t + 1) * hb, nb8 - 1), j)),
                  pl.BlockSpec((tm, LANES), lambda j, t: (t, j)),
                  pl.BlockSpec((SUBLANES, LANES), lambda j, t: (jnp.minimum((t + 1) * hb, nb8 - 1), j)),
                  pl.BlockSpec((DN_CONV, LANES), lambda j, t: (0, j))],
        out_specs=[pl.BlockSpec((tm, LANES), lambda j, t: (t, j)),
                   pl.BlockSpec((SUBLANES, LANES), lambda j, t: (0, j))],
        out_shape=[jax.ShapeDtypeStruct((s, DN_QKV_DIM), f32), jax.ShapeDtypeStruct((SUBLANES, DN_QKV_DIM), f32)],
        compiler_params=_cparams("parallel", "arbitrary"), name="dn_conv_bwd_" + tag)(proj, proj, proj, dy, dy, cw)


def _gate_tile(ba, eb, ea, alog, dtb):
    beta = jax.nn.sigmoid(hdot(ba, eb))
    g = -jnp.exp(alog) * jax.nn.softplus(hdot(ba, ea) + dtb)
    return beta, g


def _each(fn, *lists):
    return [fn(*args) for args in zip(*lists)]


@functools.partial(jax.custom_vjp, nondiff_argnums=(1,))
def _halves(x, axis):
    h = x.shape[axis] // 2
    return (x[:h], x[h:]) if axis == 0 else (x[:, :h], x[:, h:])


def _halves_fwd(x, axis):
    return _halves(x, axis), None


def _halves_bwd(axis, _, g):
    return (jnp.concatenate(g, axis),)


_halves.defvjp(_halves_fwd, _halves_bwd)


def _tri_inv_unit(lowers):
    c = lowers[0].shape[0]
    r, col = _iota2((c, c), 0), _iota2((c, c), 1)
    eye = jnp.where(r == col, 1.0, 0.0).astype(f32)
    invs = None
    sh = 0
    while (1 << sh) < c:
        same_2b = lax.shift_right_logical(r, sh + 1) == lax.shift_right_logical(col, sh + 1)
        diff_b = lax.shift_right_logical(r, sh) != lax.shift_right_logical(col, sh)
        offs = [jnp.where(same_2b & diff_b, low, 0.0) for low in lowers]
        if invs is None:
            invs = [eye - off for off in offs]
        else:
            part = _each(hdot, invs, offs)
            invs = _each(lambda inv, p: inv - hdot(p, inv), invs, part)
        sh += 1
    return invs


def _delta_chunk(q, k, v, gb, betab, state):
    c, hd = DN_CHUNK, DN_HEAD_DIM
    r, col = _iota2((c, c), 0), _iota2((c, c), 1)
    causal, strict = r >= col, r > col
    tril = jnp.where(causal, 1.0, 0.0).astype(f32)
    gc = _each(lambda g: hdot(tril, g), gb)
    decay = _each(lambda g: jnp.where(causal, jnp.exp(jnp.where(causal, g - g.T, 0.0)), 0.0), gc)
    qs = _each(lambda t: t * (DN_HEAD_DIM ** -0.5), q)
    kb = _each(lambda a, b: a * b, k, betab)
    kq = _each(lambda a, b, kk: _halves(bdot(jnp.concatenate([a, b], 0), kk, 1, 1), 0), kb, qs, k)
    lower = _each(lambda x, d: jnp.where(strict, x[0], 0.0) * d, kq, decay)
    intra = _each(lambda x, d: x[1] * d, kq, decay)
    tinv = _tri_inv_unit(lower)
    eg = _each(jnp.exp, gc)
    uw = _each(lambda t, vv, b, kb_, e: _halves(hdot(t, jnp.concatenate([vv * b, kb_ * e], 1)), 1),
               tinv, v, betab, kb, eg)
    gl = _each(lambda g: jnp.sum(jnp.where(r == c - 1, g, 0.0), 0, keepdims=True), gc)
    k_dec = _each(lambda kk, a, g: kk * jnp.exp(a - g), k, gl, gc)
    ws = _each(lambda x, t, e, st: _halves(bdot(jnp.concatenate([x[1], t * e], 0), st, 1, 0), 0), uw, qs, eg, state)
    v_new = _each(lambda x, y: x[0] - y[0], uw, ws)
    out = _each(lambda y, a, vn: y[1] + bdot(a, vn, 1, 0), ws, intra, v_new)
    new_state = _each(lambda st, a, kd, vn: st * jnp.exp(a) + bdot(kd, vn, 0, 0), state, gl, k_dec, v_new)
    return tuple(out), tuple(new_state)


def delta_fwd(tag, qkv, gb, betab):
    s = qkv.shape[0]
    c, hd = DN_CHUNK, DN_HEAD_DIM
    n = s // c

    hg, ng = DN_HEADS_PER_STEP, DN_HEADS // DN_HEADS_PER_STEP

    def body(q_ref, k_ref, v_ref, g_ref, b_ref, o_ref, st_ref, state):
        @pl.when(pl.program_id(1) == 0)
        def _():
            state[...] = jnp.zeros_like(state)

        heads = lambda ref: tuple(ref[:, j * hd:(j + 1) * hd] for j in range(hg))
        st = tuple(state[j] for j in range(hg))
        outs, news = _delta_chunk(heads(q_ref), heads(k_ref), heads(v_ref), heads(g_ref), heads(b_ref), st)
        for j in range(hg):
            st_ref[j] = st[j]
            o_ref[:, j * hd:(j + 1) * hd] = outs[j]
            state[j] = news[j]

    blk = lambda off: pl.BlockSpec((c, hg * hd), lambda h, i, _o=off: (i, h + _o))
    return pl.pallas_call(
        body, grid=(ng, n),
        in_specs=[blk(0), blk(ng), blk(2 * ng), blk(0), blk(0)],
        out_specs=[blk(0), pl.BlockSpec((hg, None, hd, hd), lambda h, i: (h, i, 0, 0))],
        out_shape=[jax.ShapeDtypeStruct((s, DN_KEY_DIM), f32), jax.ShapeDtypeStruct((DN_HEADS, n, hd, hd), f32)],
        scratch_shapes=[pltpu.VMEM((hg, hd, hd), f32)],
        compiler_params=_cparams("parallel", "arbitrary"), name="delta_" + tag)(qkv, qkv, qkv, gb, betab)


def delta_bwd(tag, qkv, gb, betab, states, do):
    s = qkv.shape[0]
    c, hd = DN_CHUNK, DN_HEAD_DIM
    n = s // c

    hg, ng = DN_HEADS_PER_STEP, DN_HEADS // DN_HEADS_PER_STEP

    def body(q_ref, k_ref, v_ref, g_ref, b_ref, st_ref, do_ref, dq_ref, dk_ref, dv_ref, dg_ref, db_ref, dstate):
        @pl.when(pl.program_id(1) == 0)
        def _():
            dstate[...] = jnp.zeros_like(dstate)

        heads = lambda ref: tuple(ref[:, j * hd:(j + 1) * hd] for j in range(hg))
        _, vjp = jax.vjp(_delta_chunk, heads(q_ref), heads(k_ref), heads(v_ref), heads(g_ref), heads(b_ref),
                         tuple(st_ref[j] for j in range(hg)))
        grads = vjp((heads(do_ref), tuple(dstate[j] for j in range(hg))))
        for ref, g in zip((dq_ref, dk_ref, dv_ref, dg_ref, db_ref), grads[:5]):
            for j in range(hg):
                ref[:, j * hd:(j + 1) * hd] = g[j]
        for j in range(hg):
            dstate[j] = grads[5][j]

    blk = lambda off: pl.BlockSpec((c, hg * hd), lambda h, i, _o=off: (n - 1 - i, h + _o))
    return pl.pallas_call(
        body, grid=(ng, n),
        in_specs=[blk(0), blk(ng), blk(2 * ng), blk(0), blk(0),
                  pl.BlockSpec((hg, None, hd, hd), lambda h, i: (h, n - 1 - i, 0, 0)), blk(0)],
        out_specs=[blk(0)] * 5,
        out_shape=[jax.ShapeDtypeStruct((s, DN_KEY_DIM), f32)] * 5,
        scratch_shapes=[pltpu.VMEM((hg, hd, hd), f32)],
        compiler_params=_cparams("parallel", "arbitrary"), name="delta_bwd_" + tag)(qkv, qkv, qkv, gb, betab, states, do)


def _dn_out_tile(o, z, ng):
    outs = []
    for h in range(DN_HEADS):
        sl = slice(h * DN_HEAD_DIM, (h + 1) * DN_HEAD_DIM)
        oh = o[:, sl]
        nrm = oh * lax.rsqrt(jnp.mean(oh * oh, -1, keepdims=True) + RMS_EPS) * ng[:, sl]
        outs.append(nrm * jax.nn.silu(z[:, sl]))
    return (jnp.concatenate(outs, -1),)


def _head_selectors():
    r, c = _iota2((BA_PAD, DN_KEY_DIM), 0), _iota2((BA_PAD, DN_KEY_DIM), 1) // DN_HEAD_DIM
    return (r == c).astype(f32), (r == c + DN_HEADS).astype(f32)


def dn_mixer_fwd(tag, proj, cw, alog_b, dtb_b, ng_b):
    eb, ea = _head_selectors()
    ba = (proj, BA_PAD, COL_BA // BA_PAD)
    qkv = dn_conv_fwd(tag, proj, cw)
    betab, gb = rowmap("dn_gate_" + tag, _gate_tile, [ba], [eb, ea, alog_b, dtb_b], [DN_KEY_DIM] * 2, TM_ROW)
    o, states = delta_fwd(tag, qkv, gb, betab)
    z = (proj, DN_KEY_DIM, COL_Z // DN_KEY_DIM)
    a_out = rowmap("dn_out_" + tag, _dn_out_tile, [o, z], [ng_b], [DN_KEY_DIM], TM_ROW)[0]
    return a_out, (qkv, betab, gb, o, states)


def dn_mixer_bwd(tag, proj, cw, alog_b, dtb_b, ng_b, res, da_out):
    qkv, betab, gb, o, states = res
    eb, ea = _head_selectors()
    ba = (proj, BA_PAD, COL_BA // BA_PAD)
    z = (proj, DN_KEY_DIM, COL_Z // DN_KEY_DIM)
    (do, dz), (dng,) = rowmap_bwd("dn_out_bwd_" + tag, _dn_out_tile, [o, z], [ng_b], [da_out], TM_ROW)
    dq, dk, dv, dgb, dbetab = delta_bwd(tag, qkv, gb, betab, states, do)
    dqkv_raw, dcw = dn_conv_bwd(tag, proj, cw, jnp.concatenate([dq, dk, dv], 1))
    (dba,), (dalog, ddtb) = rowmap_bwd("dn_gate_bwd_" + tag, _gate_tile, [ba], [eb, ea, alog_b, dtb_b],
                                       [dbetab, dgb], TM_ROW, par_mask=[False, False, True, True])
    return dqkv_raw, dz, dba, dcw[:DN_CONV], dalog, ddtb, dng


def _swap_halves(x):
    n = x.shape[1]
    first = (_iota2((1, n), 1) % SW_HEAD_DIM) < SW_HEAD_DIM // 2
    return jnp.where(first, pltpu.roll(x, n - SW_HEAD_DIM // 2, 1), pltpu.roll(x, SW_HEAD_DIM // 2, 1))


def _rope_apply(x, cos, sin_signed):
    return x * cos + _swap_halves(x) * sin_signed


def _rope_transpose(dy, cos, sin_signed):
    return dy * cos + _swap_halves(dy * sin_signed)


def rope_tables(positions, s):
    half = SW_HEAD_DIM // 2
    inv_freq = ROPE_THETA ** (-jnp.arange(0, SW_HEAD_DIM, 2, dtype=f32) / SW_HEAD_DIM)
    ang = positions.reshape(s, 1).astype(f32) * inv_freq[None, :]
    cos, sin = jnp.cos(ang), jnp.sin(ang)
    cos_t = jnp.tile(jnp.concatenate([cos, cos], 1), (1, SW_HEADS))
    sin_t = jnp.tile(jnp.concatenate([-sin, sin], 1), (1, SW_HEADS))
    assert cos_t.shape == (s, SW_DIM) and half * 2 == SW_HEAD_DIM
    return cos_t, sin_t


def rope_fwd(tag, proj, cos, sin):
    def fn(q, k, v, c, sg):
        return _rope_apply(q, c, sg), _rope_apply(k, c, sg), v

    rows = [(proj, SW_DIM, COL_SWQ // SW_DIM), (proj, SW_DIM, COL_SWK // SW_DIM), (proj, SW_DIM, COL_SWV // SW_DIM), cos, sin]
    return rowmap("rope_" + tag, fn, rows, [], [SW_DIM] * 3, TM_ROW)


def _swa_block(q, kp, kc, vp, vc, first):
    blk = SW_BLOCK
    kk = jnp.concatenate([kp, kc], 0)
    vv = jnp.concatenate([vp, vc], 0)
    dist = (_iota2((blk, 2 * blk), 0) + blk) - _iota2((blk, 2 * blk), 1)
    kj = _iota2((blk, 2 * blk), 1)
    valid = (dist >= 0) & (dist <= blk) & ((kj >= blk) | jnp.logical_not(first))
    lane_head = _iota2((1, LANES), 1) // SW_HEAD_DIM
    outs, lses = [], []
    for p in range(SW_DIM // LANES):
        sl = slice(p * LANES, (p + 1) * LANES)
        qp, kp_, vp_ = q[:, sl], kk[:, sl], vv[:, sl]
        o_pair = jnp.zeros((blk, LANES), f32)
        l_pair = jnp.zeros((blk, LANES), f32)
        for e in range(LANES // SW_HEAD_DIM):
            msk = lane_head == e
            sc = bdot(jnp.where(msk, qp, 0.0), kp_, 1, 1) * (SW_HEAD_DIM ** -0.5)
            sc = jnp.where(valid, sc, -1e30)
            m = lax.stop_gradient(jnp.max(sc, -1, keepdims=True))
            pe = jnp.exp(sc - m)
            l = jnp.sum(pe, -1, keepdims=True)
            o = bdot(pe, vp_, 1, 0) / l
            o_pair = o_pair + jnp.where(msk, o, 0.0)
            l_pair = l_pair + jnp.where(msk, m + jnp.log(l), 0.0)
        outs.append(o_pair)
        lses.append(l_pair)
    return jnp.concatenate(outs, -1), jnp.concatenate(lses, -1)


def _swa_specs(r):
    cur = pl.BlockSpec((SW_BLOCK, SW_DIM), lambda rho, n: (n, rho))
    prev = pl.BlockSpec((SW_BLOCK, SW_DIM), lambda rho, n: (jnp.maximum(n - 1, 0), rho))
    return cur, prev


def swa_fwd(tag, r, q, k, v):
    s = q.shape[0]
    ln = s // r
    q2, k2, v2 = (t.reshape(ln, r * SW_DIM) for t in (q, k, v))
    cur, prev = _swa_specs(r)

    def body(q_ref, kp_ref, kc_ref, vp_ref, vc_ref, o_ref, l_ref):
        o, l = _swa_block(q_ref[...], kp_ref[...], kc_ref[...], vp_ref[...], vc_ref[...], pl.program_id(1) == 0)
        o_ref[...] = o
        l_ref[...] = l

    o, l = pl.pallas_call(
        body, grid=(r, ln // SW_BLOCK),
        in_specs=[cur, prev, cur, prev, cur], out_specs=[cur, cur],
        out_shape=[jax.ShapeDtypeStruct((ln, r * SW_DIM), f32)] * 2,
        compiler_params=_cparams("parallel", "parallel"), name=f"swa{r}_{tag}")(q2, k2, k2, v2, v2)
    return o.reshape(s, SW_DIM), l.reshape(s, SW_DIM)


def swa_bwd(tag, r, q, k, v, do, dl):
    s = q.shape[0]
    ln = s // r
    q2, k2, v2, do2, dl2 = (t.reshape(ln, r * SW_DIM) for t in (q, k, v, do, dl))
    cur, prev = _swa_specs(r)

    def body(q_ref, kp_ref, kc_ref, vp_ref, vc_ref, do_ref, dl_ref, dq_ref, dka_ref, dkb_ref, dva_ref, dvb_ref):
        first = pl.program_id(1) == 0
        _, vjp = jax.vjp(lambda *a: _swa_block(*a, first), q_ref[...], kp_ref[...], kc_ref[...], vp_ref[...], vc_ref[...])
        dq_ref[...], dka_ref[...], dkb_ref[...], dva_ref[...], dvb_ref[...] = vjp((do_ref[...], dl_ref[...]))

    outs = pl.pallas_call(
        body, grid=(r, ln // SW_BLOCK),
        in_specs=[cur, prev, cur, prev, cur, cur, cur], out_specs=[cur] * 5,
        out_shape=[jax.ShapeDtypeStruct((ln, r * SW_DIM), f32)] * 5,
        compiler_params=_cparams("parallel", "parallel"), name=f"swa{r}_bwd_{tag}")(q2, k2, k2, v2, v2, do2, dl2)
    return [t.reshape(s, SW_DIM) for t in outs]


def _combine_tile(o1, l1, o2, l2, o3, l3):
    m = lax.stop_gradient(jnp.maximum(jnp.maximum(l1, l2), l3))
    e1, e2, e3 = jnp.exp(l1 - m), jnp.exp(l2 - m), jnp.exp(l3 - m)
    return ((o1 * e1 + o2 * e2 + o3 * e3) / (e1 + e2 + e3),)


def swa_merge_bwd(tag, grads, cos, sin):
    s = cos.shape[0]
    tm = SW_BLOCK
    nt = s // tm
    here = pl.BlockSpec((tm, SW_DIM), lambda i: (i, 0))
    arrs, specs = [], []
    for r, g in zip(SW_DILATIONS, grads):
        ahead = pl.BlockSpec((tm, SW_DIM), lambda i, _r=r: (jnp.minimum(i + _r, nt - 1), 0))
        arrs += g
        specs += [here, ahead, here, ahead, here]

    def body(*refs):
        i = pl.program_id(0)
        c_ref, s_ref = refs[15], refs[16]
        dq_ref, dk_ref, dv_ref = refs[17:]
        dq = jnp.zeros((tm, SW_DIM), f32)
        dk = jnp.zeros((tm, SW_DIM), f32)
        dv = jnp.zeros((tm, SW_DIM), f32)
        for b, r in enumerate(SW_DILATIONS):
            gq, gka, gkb, gva, gvb = refs[5 * b:5 * b + 5]
            inside = i + r < nt
            dq = dq + gq[...]
            dk = dk + gkb[...] + jnp.where(inside, gka[...], 0.0)
            dv = dv + gvb[...] + jnp.where(inside, gva[...], 0.0)
        dq_ref[...] = _rope_transpose(dq, c_ref[...], s_ref[...])
        dk_ref[...] = _rope_transpose(dk, c_ref[...], s_ref[...])
        dv_ref[...] = dv

    return pl.pallas_call(
        body, grid=(nt,), in_specs=specs + [here, here], out_specs=[here] * 3,
        out_shape=[jax.ShapeDtypeStruct((s, SW_DIM), f32)] * 3,
        compiler_params=_cparams("parallel"), name="swa_merge_bwd_" + tag)(*arrs, cos, sin)


def swa_mixer_fwd(tag, proj, cos, sin):
    q, k, v = rope_fwd(tag, proj, cos, sin)
    ols = []
    for r in SW_DILATIONS:
        ols += list(swa_fwd(tag, r, q, k, v))
    b_out = rowmap("swa_comb_" + tag, _combine_tile, ols, [], [SW_DIM], TM_ROW)[0]
    return b_out, (q, k, v, ols)


def swa_mixer_bwd(tag, cos, sin, res, db_out):
    q, k, v, ols = res
    dols, _ = rowmap_bwd("swa_comb_bwd_" + tag, _combine_tile, ols, [], [db_out], TM_ROW)
    grads = [swa_bwd(tag, r, q, k, v, dols[2 * b], dols[2 * b + 1]) for b, r in enumerate(SW_DILATIONS)]
    return swa_merge_bwd(tag, grads, cos, sin)


TM_S5 = 256
S5_GPB = LANES // S5_GROUP
S5_NBLK = D_MODEL // LANES
S5_HALF = S5_GPB * S5_STATE
S5_BW = 2 * S5_HALF
S5_WIDTH = S5_NBLK * S5_BW
S5_TABW = S5_NBLK * S5_HALF


def _s5_disc_tile(a_re, a_im, log_dt, b_re, b_im, expand):
    dt = jnp.exp(log_dt)
    mag = jnp.exp(a_re * dt)
    abar_re, abar_im = mag * jnp.cos(a_im * dt), mag * jnp.sin(a_im * dt)
    n_re, n_im = abar_re - 1.0, abar_im
    den = a_re * a_re + a_im * a_im
    c_re = (n_re * a_re + n_im * a_im) / den
    c_im = (n_im * a_re - n_re * a_im) / den
    cx_re, cx_im = hdot(c_re, expand), hdot(c_im, expand)
    return abar_re, abar_im, cx_re * b_re - cx_im * b_im, cx_re * b_im + cx_im * b_re


def _s5_expand():
    return (_iota2((S5_STATE, S5_STATE * S5_GROUP), 1) // S5_GROUP == _iota2((S5_STATE, S5_STATE * S5_GROUP), 0)).astype(f32)


def s5_tables(a_re, a_im, log_dt):
    lanes = lambda v: v.reshape(1, S5_TABW)
    dt = jnp.broadcast_to(log_dt.reshape(S5_GROUPS, 1), (S5_GROUPS, S5_STATE))
    t = TM_S5

    def body(are_ref, aim_ref, ldt_ref, ar_ref, ai_ref, arr_ref, air_ref):
        dtv = jnp.exp(ldt_ref[...])
        lre, lim = are_ref[...] * dtv, aim_ref[...] * dtv
        row = _iota2((t, S5_HALF), 0)
        for asc, o_re, o_im in ((True, ar_ref, ai_ref), (False, arr_ref, air_ref)):
            n = (row + 1 if asc else t - row).astype(f32)
            mag = jnp.exp(n * lre)
            o_re[...] = mag * jnp.cos(n * lim)
            o_im[...] = mag * jnp.sin(n * lim)

    lane = pl.BlockSpec((1, S5_HALF), lambda j: (0, j))
    tab = pl.BlockSpec((t, S5_HALF), lambda j: (0, j))
    return pl.pallas_call(
        body, grid=(S5_NBLK,), in_specs=[lane] * 3, out_specs=[tab] * 4,
        out_shape=[jax.ShapeDtypeStruct((t, S5_TABW), f32)] * 4,
        compiler_params=_cparams("parallel"), name="s5_tables")(lanes(a_re), lanes(a_im), lanes(dt))


def s5_pack_weights(bbar_re, bbar_im, c_re, c_im):
    eye = jnp.eye(S5_GPB, dtype=f32)
    bb = jnp.stack([bbar_re.reshape(S5_GROUPS, S5_STATE, S5_GROUP), bbar_im.reshape(S5_GROUPS, S5_STATE, S5_GROUP)], 1)
    bb = bb.transpose(0, 3, 1, 2).reshape(S5_NBLK, S5_GPB, S5_GROUP, 2, S5_STATE)
    wb = (bb[:, :, :, :, None, :] * eye[None, :, None, None, :, None]).reshape(S5_NBLK, LANES, S5_BW)
    cc = jnp.stack([c_re, -c_im], 1)
    cc = cc.reshape(S5_NBLK, S5_GPB, 2, S5_GROUP, S5_STATE).transpose(0, 2, 1, 4, 3)
    wc = (cc[:, :, :, :, None, :] * eye[None, None, :, None, :, None]).reshape(S5_NBLK, S5_BW, LANES)
    return wb, wc


def s5_unpack_weight_grads(dwb, dwc):
    d6 = dwb.reshape(S5_NBLK, S5_GPB, S5_GROUP, 2, S5_GPB, S5_STATE)
    dbb = jnp.stack([d6[:, gl, :, :, gl, :] for gl in range(S5_GPB)])
    dbb = dbb.transpose(1, 0, 3, 4, 2).reshape(S5_GROUPS, 2, S5_STATE * S5_GROUP)
    c6 = dwc.reshape(S5_NBLK, 2, S5_GPB, S5_STATE, S5_GPB, S5_GROUP)
    dcc = jnp.stack([c6[:, :, gl, :, gl, :] for gl in range(S5_GPB)])
    dcc = dcc.transpose(1, 0, 2, 4, 3).reshape(S5_GROUPS, 2, S5_GROUP, S5_STATE)
    return dbb[:, 0], dbb[:, 1], dcc[:, 0], -dcc[:, 1]


def _s5_step_rows(t):
    d, out = 1, []
    while d < t:
        out.append(d)
        d *= 2
    return out


def s5_core_fwd(tag, u, wb, wc, a1, a2, dskip):
    s = u.shape[0]
    t = TM_S5

    def body(u_ref, wb_ref, wc_ref, ar_ref, ai_ref, d_ref, y_ref, x_ref, carry):
        @pl.when(pl.program_id(1) == 0)
        def _():
            carry[...] = jnp.zeros_like(carry)

        uv = u_ref[...]
        bu = bdot(uv, wb_ref[...], 1, 0)
        xr, xi = bu[:, :S5_HALF], bu[:, S5_HALF:]
        row = _iota2((t, S5_HALF), 0)
        for d in _s5_step_rows(t):
            keep = row >= d
            sr = jnp.where(keep, pltpu.roll(xr, d, 0), 0.0)
            si = jnp.where(keep, pltpu.roll(xi, d, 0), 0.0)
            ar, ai = ar_ref[d - 1:d, :], ai_ref[d - 1:d, :]
            xr, xi = xr + ar * sr - ai * si, xi + ar * si + ai * sr
        cr, ci = carry[:, :S5_HALF], carry[:, S5_HALF:]
        ar, ai = ar_ref[...], ai_ref[...]
        x_ref[:, :S5_HALF] = xr + ar * cr - ai * ci
        x_ref[:, S5_HALF:] = xi + ar * ci + ai * cr
        carry[...] = x_ref[t - 1:t, :]
        y_ref[...] = bdot(x_ref[...], wc_ref[...], 1, 0) + d_ref[...] * uv

    tab = pl.BlockSpec((t, S5_HALF), lambda j, i: (0, j))
    return pl.pallas_call(
        body, grid=(S5_NBLK, s // t),
        in_specs=[pl.BlockSpec((t, LANES), lambda j, i: (i, j)),
                  pl.BlockSpec((None, LANES, S5_BW), lambda j, i: (j, 0, 0)),
                  pl.BlockSpec((None, S5_BW, LANES), lambda j, i: (j, 0, 0)),
                  tab, tab, pl.BlockSpec((1, LANES), lambda j, i: (0, j))],
        out_specs=[pl.BlockSpec((t, LANES), lambda j, i: (i, j)), pl.BlockSpec((t, S5_BW), lambda j, i: (i, j))],
        out_shape=[jax.ShapeDtypeStruct((s, D_MODEL), f32), jax.ShapeDtypeStruct((s, S5_WIDTH), f32)],
        scratch_shapes=[pltpu.VMEM((1, S5_BW), f32)],
        compiler_params=_cparams("parallel", "arbitrary"), name="s5_core_" + tag)(u, wb, wc, a1, a2, dskip)


def s5_core_bwd(tag, u, x, wb, wc, a1, a2, a1r, a2r, dskip, dy):
    s = u.shape[0]
    t = TM_S5
    nt = s // t
    hb = t // SUBLANES

    def body(u_ref, dy_ref, x_ref, xh_ref, wb_ref, wc_ref, ar_ref, ai_ref, arr_ref, air_ref, d_ref,
             du_ref, dwb_ref, dwc_ref, dd_ref, q1_ref, q2_ref, carry, lam_scr):
        i = pl.program_id(1)
        tt = nt - 1 - i

        @pl.when(i == 0)
        def _():
            carry[...] = jnp.zeros_like(carry)

        uv, dyv, xv = u_ref[...], dy_ref[...], x_ref[...]
        lam = bdot(dyv, wc_ref[...], 1, 1)
        lr, li = lam[:, :S5_HALF], lam[:, S5_HALF:]
        row = _iota2((t, S5_HALF), 0)
        for d in _s5_step_rows(t):
            keep = row < t - d
            sr = jnp.where(keep, pltpu.roll(lr, t - d, 0), 0.0)
            si = jnp.where(keep, pltpu.roll(li, t - d, 0), 0.0)
            ar, ai = ar_ref[d - 1:d, :], ai_ref[d - 1:d, :]
            lr, li = lr + ar * sr + ai * si, li + ar * si - ai * sr
        cr, ci = carry[:, :S5_HALF], carry[:, S5_HALF:]
        ar, ai = arr_ref[...], air_ref[...]
        lr, li = lr + ar * cr + ai * ci, li + ar * ci - ai * cr
        lam_scr[:, :S5_HALF] = lr
        lam_scr[:, S5_HALF:] = li
        carry[...] = lam_scr[0:1, :]
        lam = lam_scr[...]
        du_ref[...] = bdot(lam, wb_ref[...], 1, 1) + d_ref[...] * dyv
        x_last = jnp.where(tt > 0, xh_ref[SUBLANES - 1:SUBLANES, :], 0.0)
        x_prev = jnp.where(_iota2((t, S5_BW), 0) == 0, x_last, pltpu.roll(xv, 1, 0))
        pr, pi = x_prev[:, :S5_HALF], x_prev[:, S5_HALF:]
        p1, p2 = lr * pr + li * pi, li * pr - lr * pi
        q1 = p1[:SUBLANES, :]
        q2 = p2[:SUBLANES, :]
        for k in range(1, hb):
            q1 = q1 + p1[k * SUBLANES:(k + 1) * SUBLANES, :]
            q2 = q2 + p2[k * SUBLANES:(k + 1) * SUBLANES, :]
        upd = [(dwb_ref, bdot(uv, lam, 0, 0)), (dwc_ref, bdot(xv, dyv, 0, 0)),
               (dd_ref, jnp.sum(dyv * uv, 0, keepdims=True)), (q1_ref, q1), (q2_ref, q2)]

        @pl.when(i == 0)
        def _():
            for ref, val in upd:
                ref[...] = val

        @pl.when(i != 0)
        def _():
            for ref, val in upd:
                ref[...] += val

    nb8 = s // SUBLANES
    rev = lambda w: pl.BlockSpec((t, w), lambda j, i: (nt - 1 - i, j))
    tab = pl.BlockSpec((t, S5_HALF), lambda j, i: (0, j))
    return pl.pallas_call(
        body, grid=(S5_NBLK, nt),
        in_specs=[rev(LANES), rev(LANES), rev(S5_BW),
                  pl.BlockSpec((SUBLANES, S5_BW), lambda j, i: (jnp.maximum((nt - 1 - i) * hb - 1, 0), j)),
                  pl.BlockSpec((None, LANES, S5_BW), lambda j, i: (j, 0, 0)),
                  pl.BlockSpec((None, S5_BW, LANES), lambda j, i: (j, 0, 0)),
                  tab, tab, tab, tab, pl.BlockSpec((1, LANES), lambda j, i: (0, j))],
        out_specs=[rev(LANES),
                   pl.BlockSpec((None, LANES, S5_BW), lambda j, i: (j, 0, 0)),
                   pl.BlockSpec((None, S5_BW, LANES), lambda j, i: (j, 0, 0)),
                   pl.BlockSpec((1, LANES), lambda j, i: (0, j)),
                   pl.BlockSpec((SUBLANES, S5_HALF), lambda j, i: (0, j)),
                   pl.BlockSpec((SUBLANES, S5_HALF), lambda j, i: (0, j))],
        out_shape=[jax.ShapeDtypeStruct((s, D_MODEL), f32),
                   jax.ShapeDtypeStruct((S5_NBLK, LANES, S5_BW), f32),
                   jax.ShapeDtypeStruct((S5_NBLK, S5_BW, LANES), f32),
                   jax.ShapeDtypeStruct((1, D_MODEL), f32),
                   jax.ShapeDtypeStruct((SUBLANES, S5_TABW), f32),
                   jax.ShapeDtypeStruct((SUBLANES, S5_TABW), f32)],
        scratch_shapes=[pltpu.VMEM((1, S5_BW), f32), pltpu.VMEM((t, S5_BW), f32)],
        compiler_params=_cparams("parallel", "arbitrary"),
        name="s5_core_bwd_" + tag)(u, dy, x, x, wb, wc, a1, a2, a1r, a2r, dskip)


def _gelu_tile(y):
    return (jax.nn.gelu(y),)


def s5_mixer_fwd(tag, u, prm, w_og):
    a_re, a_im, log_dt, b_re, b_im, c_re, c_im, dskip = prm
    disc_in = [a_re, a_im, log_dt.reshape(S5_GROUPS, 1), b_re.reshape(S5_GROUPS, -1), b_im.reshape(S5_GROUPS, -1)]
    abar_re, abar_im, bbar_re, bbar_im = rowmap("s5_disc_" + tag, _s5_disc_tile, disc_in, [_s5_expand()],
                                                [S5_STATE, S5_STATE, S5_STATE * S5_GROUP, S5_STATE * S5_GROUP], S5_GROUPS)
    del abar_re, abar_im
    a1, a2, a1r, a2r = s5_tables(a_re, a_im, log_dt)
    wb, wc = s5_pack_weights(bbar_re, bbar_im, c_re, c_im)
    wb, wc = wb.astype(bf16), wc.astype(bf16)
    y, x = s5_core_fwd(tag, u, wb, wc, a1, a2, dskip.reshape(1, D_MODEL))
    hid = rowmap("s5_gelu_" + tag, _gelu_tile, [y], [], [D_MODEL], TM_ROW, out_dtypes=[bf16])[0]
    og = mm_nn("s5_og_" + tag, hid, w_og)
    mix = rowmap("s5_glu_" + tag, _glu_tile, [og], [], [D_MODEL], TM_ROW)[0]
    return mix, (disc_in, a1, a2, a1r, a2r, wb, wc, x, y, hid, og)


def s5_mixer_bwd(tag, u, prm, w_og, res, dmix):
    a_re, a_im, log_dt, b_re, b_im, c_re, c_im, dskip = prm
    disc_in, a1, a2, a1r, a2r, wb, wc, x, y, hid, og = res
    (dog,), _ = rowmap_bwd("s5_glu_bwd_" + tag, _glu_tile, [og], [], [dmix], TM_ROW)
    dw_og = mm_tn("s5_og_dw_" + tag, hid, dog)
    dhid = mm_nt("s5_og_dx_" + tag, dog, w_og)
    (dy,), _ = rowmap_bwd("s5_gelu_bwd_" + tag, _gelu_tile, [y], [], [dhid], TM_ROW)
    du, dwb, dwc, ddskip, q1, q2 = s5_core_bwd(tag, u, x, wb, wc, a1, a2, a1r, a2r, dskip.reshape(1, D_MODEL), dy)
    dbbar_re, dbbar_im, dc_re, dc_im = s5_unpack_weight_grads(dwb, dwc)
    dabar_re = q1.sum(0).reshape(S5_GROUPS, S5_STATE)
    dabar_im = q2.sum(0).reshape(S5_GROUPS, S5_STATE)
    grads, _ = rowmap_bwd("s5_disc_bwd_" + tag, _s5_disc_tile, disc_in, [_s5_expand()],
                          [dabar_re, dabar_im, dbbar_re, dbbar_im], S5_GROUPS, par_mask=[False])
    da_re, da_im, dlog_dt, db_re, db_im = grads
    return du, (da_re, da_im, dlog_dt.reshape(S5_GROUPS), db_re.reshape(b_re.shape), db_im.reshape(b_im.shape),
                dc_re, dc_im, ddskip.reshape(D_MODEL)), dw_og


HYB_IN = 3592
_IN_B0, _IN_SW0 = 2048, 2056


IN_SHARD = HYB_IN // 4
SHARD_ORDER_GRADS = ("hyb_w_in", "ffn_wg", "ffn_wu", "ffn_wd")
FFN_TRANSPOSED = ("ffn_wg", "ffn_wu")


def _w_in_pieces():
    runs = [(0, _IN_B0, 0), (_IN_B0, _IN_SW0, COL_BA), (_IN_SW0, HYB_IN, _IN_B0)]
    out = []
    for sh in range(4):
        lo, hi = sh * IN_SHARD, (sh + 1) * IN_SHARD
        for r_lo, r_hi, c_lo in runs:
            a, b = max(lo, r_lo), min(hi, r_hi)
            if a < b:
                out.append((sh, a - lo, b - lo, c_lo + a - r_lo))
    return out


def w_in_to_canonical(tag, layer, w4):
    tr = 128

    def body(w_ref, o_ref):
        o_ref[:, COL_BA:] = jnp.zeros((tr, BA_PAD), o_ref.dtype)
        for sh, a, b, c in _w_in_pieces():
            o_ref[:, c:c + b - a] = w_ref[sh, :, a:b]

    return pl.pallas_call(
        body, grid=(D_MODEL // tr,),
        in_specs=[pl.BlockSpec((4, None, tr, IN_SHARD), lambda i: (0, layer, i, 0))],
        out_specs=pl.BlockSpec((tr, PROJ_COLS), lambda i: (i, 0)),
        out_shape=jax.ShapeDtypeStruct((D_MODEL, PROJ_COLS), w4.dtype),
        compiler_params=_cparams("parallel"), name="w_in_canon_" + tag)(w4)


def w_in_grad_to_shards(tag, g):
    tr = 128

    def body(g_ref, o_ref):
        for sh, a, b, c in _w_in_pieces():
            o_ref[sh, :, a:b] = g_ref[:, c:c + b - a]

    return pl.pallas_call(
        body, grid=(D_MODEL // tr,),
        in_specs=[pl.BlockSpec((tr, PROJ_COLS), lambda i: (i, 0))],
        out_specs=pl.BlockSpec((4, tr, IN_SHARD), lambda i: (0, i, 0)),
        out_shape=jax.ShapeDtypeStruct((4, D_MODEL, IN_SHARD), f32),
        compiler_params=_cparams("parallel"), name="w_in_grad_shards_" + tag)(g)


def _add2(name, a, b):
    return rowmap(name, lambda p, q: (p + q,), [a, b], [], [a.shape[1]], _pick(a.shape[0], (256, 128, 64, 32, 16, 8)))[0]


def local_step(x, mem, positions, target, p):
    s = x.shape[0]
    cos, sin = rope_tables(positions, s)
    row = lambda v: v.reshape(1, -1).astype(f32)
    wg4, wu4, wd4 = (p[n].astype(bf16) for n in ("ffn_wg", "ffn_wu", "ffn_wd"))
    h = h16 = x
    tape = []
    for l in range(DEPTH):
        i, tag = l // 2, str(l)
        t = {"h0": h, "h0_16": h16}
        if l % 2 == 0:
            t["w_in"] = w_in_to_canonical(tag, i, p["hyb_w_in"].astype(bf16))
            t["w_out"] = p["hyb_w_out"][i].astype(bf16)
            t["dn_prm"] = (p["dn_conv_w"][i].astype(f32), row(jnp.repeat(p["dn_a_log"][i], DN_HEAD_DIM)),
                           row(jnp.repeat(p["dn_dt_bias"][i], DN_HEAD_DIM)), row(jnp.tile(p["dn_norm_g"][i], DN_HEADS)))
            t["proj"] = mm_nn("hyb_in_" + tag, h16, t["w_in"])
            a_out, t["dn"] = dn_mixer_fwd(tag, t["proj"], *t["dn_prm"])
            b_out, t["swa"] = swa_mixer_fwd(tag, t["proj"], cos, sin)
            t["mixed"] = jnp.concatenate([a_out, b_out], 1)
            mix = mm_nn("hyb_out_" + tag, t["mixed"], t["w_out"])
        else:
            t["s5_prm"] = tuple(p[n][i].astype(f32) for n in
                                ("s5_a_re", "s5_a_im", "s5_log_dt", "s5_b_re", "s5_b_im", "s5_c_re", "s5_c_im", "s5_d"))
            t["w_og"] = jnp.concatenate([p["s5_glu_wo"][i], p["s5_glu_wg"][i]], 1).astype(bf16)
            mix, t["s5"] = s5_mixer_fwd(tag, h, t["s5_prm"], t["w_og"])
        t["mix"] = mix
        t["ln"] = [(row(p[g][l]), row(p[b][l])) for g, b in
                   (("ln_mix_g", "ln_mix_b"), ("ln_x_g", "ln_x_b"), ("ln_ffn_g", "ln_ffn_b"))]
        t["h1"], t["h1_16"] = postnorm_fwd("mix" + tag, h, mix, *t["ln"][0])
        t["wq"], t["wo"] = p["xq_w"][l].astype(bf16), p["xo_w"][l].astype(bf16)
        t["wkv"] = jnp.concatenate([p["xk_w"][l], p["xv_w"][l]], 1).astype(bf16)
        t["xo"], t["xres"] = xattn_fwd(tag, t["h1_16"], mem, t["wq"], t["wkv"], t["wo"])
        t["h2"], t["h2_16"] = postnorm_fwd("x" + tag, t["h1"], t["xo"], *t["ln"][1])
        t["fo"], t["fres"] = ffn_fwd(tag, l, t["h2_16"], wg4, wu4, wd4)
        h, h16 = postnorm_fwd("ffn" + tag, t["h2"], t["fo"], *t["ln"][2])
        tape.append(t)

    part, dh = loss_head(h, target)
    loss = jnp.sum(part)

    g = {n: [None] * v.shape[1 if n in SHARD_ORDER_GRADS else 0] for n, v in p.items()}
    for l in reversed(range(DEPTH)):
        i, tag, t = l // 2, str(l), tape[l]
        dh2a, dfo, dg, db = postnorm_bwd("ffn" + tag, t["h2"], t["fo"], *t["ln"][2], dh)
        g["ln_ffn_g"][l], g["ln_ffn_b"][l] = dg[0], db[0]
        dh2b, g["ffn_wg"][l], g["ffn_wu"][l], g["ffn_wd"][l] = ffn_bwd(tag, l, t["h2_16"], wg4, wu4, wd4, t["fres"], dfo)
        dh1a, dxo, dg, db = postnorm_bwd("x" + tag, t["h1"], t["xo"], *t["ln"][1], [dh2a, dh2b])
        g["ln_x_g"][l], g["ln_x_b"][l] = dg[0], db[0]
        dh1b, g["xq_w"][l], dwkv, g["xo_w"][l] = xattn_bwd(tag, t["h1_16"], mem, t["wq"], t["wkv"], t["wo"], t["xres"], dxo)
        g["xk_w"][l], g["xv_w"][l] = dwkv[:, :D_MODEL], dwkv[:, D_MODEL:]
        dh0a, dmix, dg, db = postnorm_bwd("mix" + tag, t["h0"], t["mix"], *t["ln"][0], [dh1a, dh1b])
        g["ln_mix_g"][l], g["ln_mix_b"][l] = dg[0], db[0]
        if l % 2 == 0:
            g["hyb_w_out"][i] = mm_tn("hyb_out_dw_" + tag, t["mixed"], dmix)
            dmixed = mm_nt("hyb_out_dx_" + tag, dmix, t["w_out"])
            dqkv, dz, dba, dcw, dalog, ddtb, dng = dn_mixer_bwd(tag, t["proj"], *t["dn_prm"], t["dn"], (dmixed, DN_KEY_DIM, 0))
            g["dn_conv_w"][i] = dcw
            g["dn_a_log"][i] = dalog.reshape(DN_HEADS, DN_HEAD_DIM).sum(1)
            g["dn_dt_bias"][i] = ddtb.reshape(DN_HEADS, DN_HEAD_DIM).sum(1)
            g["dn_norm_g"][i] = dng.reshape(DN_HEADS, DN_HEAD_DIM).sum(0)
            dq, dk, dv = swa_mixer_bwd(tag, cos, sin, t["swa"], (dmixed, SW_DIM, 1))
            dproj = jnp.concatenate([dqkv, dz, dq, dk, dv, dba], 1)
            g["hyb_w_in"][i] = w_in_grad_to_shards(tag, mm_tn("hyb_in_dw_" + tag, t["h0_16"], dproj))
            dh0b = mm_nt("hyb_in_dx_" + tag, dproj, t["w_in"])
        else:
            dh0b, dprm, dw_og = s5_mixer_bwd(tag, t["h0"], t["s5_prm"], t["w_og"], t["s5"], dmix)
            for n, v in zip(("s5_a_re", "s5_a_im", "s5_log_dt", "s5_b_re", "s5_b_im", "s5_c_re", "s5_c_im", "s5_d"), dprm):
                g[n][i] = v
            g["s5_glu_wo"][i], g["s5_glu_wg"][i] = dw_og[:, :D_MODEL], dw_og[:, D_MODEL:]
        dh = [dh0a, dh0b]
    grad_x = _add2("grad_x", dh[0], dh[1])
    grads = {n: jnp.stack(v, 1 if n in SHARD_ORDER_GRADS else 0) for n, v in g.items()}
    return loss, grad_x, grads


WEIGHT_NAMES = ("hyb_w_in", "dn_conv_w", "dn_a_log", "dn_dt_bias", "dn_norm_g", "hyb_w_out", "s5_a_re", "s5_a_im",
                "s5_log_dt", "s5_b_re", "s5_b_im", "s5_c_re", "s5_c_im", "s5_d", "s5_glu_wo", "s5_glu_wg",
                "ln_mix_g", "ln_mix_b", "xq_w", "xk_w", "xv_w", "xo_w", "ln_x_g", "ln_x_b",
                "ffn_wg", "ffn_wu", "ffn_wd", "ln_ffn_g", "ln_ffn_b")
SHARD_AXIS = {"hyb_w_in": 2, "dn_conv_w": 2, "hyb_w_out": 1, "s5_d": 1, "s5_glu_wo": 1, "s5_glu_wg": 1,
              "xq_w": 1, "xk_w": 1, "xv_w": 1, "xo_w": 1, "ffn_wg": 2, "ffn_wu": 2, "ffn_wd": 1}
GATHER_F32 = ("dn_conv_w", "s5_d")
N_CHIPS = 4
PACK_COLS = 1024
_ANY = pl.BlockSpec(memory_space=pl.ANY)


def _pos():
    return lax.axis_index("x"), lax.axis_index("y"), lax.axis_index("c")


def _chip_peers(mx, my):
    return [(1 - mx, my), (mx, 1 - my), (1 - mx, 1 - my)]


def _rcopy(src, dst, ssem, rsem, dev):
    return pltpu.make_async_remote_copy(src_ref=src, dst_ref=dst, send_sem=ssem, recv_sem=rsem,
                                        device_id=dev, device_id_type=pl.DeviceIdType.MESH)


def comm_allgather4(name, x):
    def body(x_ref, o_ref, ssem, rsem, lsem):
        mx, my, mc = _pos()
        me = 2 * mx + my
        peers = _chip_peers(mx, my)
        loc = pltpu.make_async_copy(x_ref, o_ref.at[me], lsem)
        loc.start()
        sends = [_rcopy(x_ref, o_ref.at[me], ssem.at[k], rsem.at[k], (px, py, mc)) for k, (px, py) in enumerate(peers)]
        for cp in sends:
            cp.start()
        for k, (px, py) in enumerate(peers):
            _rcopy(x_ref, o_ref.at[2 * px + py], ssem.at[k], rsem.at[k], (px, py, mc)).wait_recv()
        for cp in sends:
            cp.wait_send()
        loc.wait()

    return pl.pallas_call(
        body, out_shape=jax.ShapeDtypeStruct((N_CHIPS,) + x.shape, x.dtype), in_specs=[_ANY], out_specs=_ANY,
        scratch_shapes=[pltpu.SemaphoreType.DMA((3,)), pltpu.SemaphoreType.DMA((3,)), pltpu.SemaphoreType.DMA],
        name=name)(x)


def _multi_call(name, body, ins, out_shapes, sems, in_place=False):
    return pl.pallas_call(
        body, out_shape=out_shapes, in_specs=[_ANY] * len(ins), out_specs=[_ANY] * len(out_shapes),
        scratch_shapes=sems, input_output_aliases={w: w for w in range(len(ins))} if in_place else {},
        name=name)(*ins)


def comm_gather_weights(name, slots):
    n = len(slots)

    def body(*refs):
        os_ = refs[n:2 * n]
        ssem, rsem, fssem, frsem = refs[2 * n:]
        mx, my, mc = _pos()
        me = 2 * mx + my
        peers = _chip_peers(mx, my)
        sib = (mx, my, 1 - mc)
        half = [o.shape[1] // 2 for o in os_]
        mine = [pl.ds(mc * h, h) for h in half]
        other = [pl.ds((1 - mc) * h, h) for h in half]
        sends = [_rcopy(os_[w].at[me, mine[w]], os_[w].at[me, mine[w]], ssem.at[w, k], rsem.at[w, k], (px, py, mc))
                 for w in range(n) for k, (px, py) in enumerate(peers)]
        for cp in sends:
            cp.start()
        fwds = []
        for w in range(n):
            for k, (px, py) in enumerate(peers):
                landed = os_[w].at[2 * px + py, mine[w]]
                _rcopy(landed, landed, ssem.at[w, k], rsem.at[w, k], (px, py, mc)).wait_recv()
                fw = _rcopy(landed, landed, fssem.at[w, k], frsem.at[w, k], sib)
                fw.start()
                fwds.append(fw)
        for w in range(n):
            for k, (px, py) in enumerate(peers):
                theirs = os_[w].at[2 * px + py, other[w]]
                _rcopy(theirs, theirs, fssem.at[w, k], frsem.at[w, k], sib).wait_recv()
        for cp in sends + fwds:
            cp.wait_send()

    dma = pltpu.SemaphoreType.DMA
    return _multi_call(name, body, slots, [jax.ShapeDtypeStruct(x.shape, x.dtype) for x in slots],
                       [dma((n, 3)), dma((n, 3)), dma((n, 3)), dma((n, 3))], in_place=True)


def comm_sibling_halves(name, gs):
    n = len(gs)

    def body(*refs):
        xs, os_ = refs[:n], refs[n:2 * n]
        ssem, rsem = refs[2 * n:]
        mx, my, mc = _pos()
        sib = (mx, my, 1 - mc)
        sends = []
        for w in range(n):
            h = xs[w].shape[1] // 2
            for j in range(N_CHIPS):
                sends.append(_rcopy(xs[w].at[j, pl.ds((1 - mc) * h, h)], os_[w].at[j], ssem.at[w, j], rsem.at[w, j], sib))
        for cp in sends:
            cp.start()
        for w in range(n):
            for j in range(N_CHIPS):
                _rcopy(os_[w].at[j], os_[w].at[j], ssem.at[w, j], rsem.at[w, j], sib).wait_recv()
        for cp in sends:
            cp.wait_send()

    dma = pltpu.SemaphoreType.DMA
    return _multi_call(name, body, gs,
                       [jax.ShapeDtypeStruct((N_CHIPS, g.shape[1] // 2) + g.shape[2:], g.dtype) for g in gs],
                       [dma((n, N_CHIPS)), dma((n, N_CHIPS))])


def comm_alltoall4(name, xs):
    n = len(xs)

    def body(*refs):
        xr, os_ = refs[:n], refs[n:2 * n]
        ssem, rsem = refs[2 * n:]
        mx, my, mc = _pos()
        me = 2 * mx + my
        peers = _chip_peers(mx, my)
        sends = [_rcopy(xr[w].at[2 * px + py], os_[w].at[me], ssem.at[w, k], rsem.at[w, k], (px, py, mc))
                 for w in range(n) for k, (px, py) in enumerate(peers)]
        for cp in sends:
            cp.start()
        for w in range(n):
            for k, (px, py) in enumerate(peers):
                dst = os_[w].at[2 * px + py]
                _rcopy(dst, dst, ssem.at[w, k], rsem.at[w, k], (px, py, mc)).wait_recv()
        for cp in sends:
            cp.wait_send()

    dma = pltpu.SemaphoreType.DMA
    return _multi_call(name, body, xs, [jax.ShapeDtypeStruct(x.shape, x.dtype) for x in xs], [dma((n, 3)), dma((n, 3))])


def comm_sibling_join(name, bs):
    n = len(bs)

    def body(*refs):
        os_ = refs[n:2 * n]
        ssem, rsem = refs[2 * n:]
        mx, my, mc = _pos()
        sib = (mx, my, 1 - mc)
        sends = [_rcopy(os_[w].at[mc], os_[w].at[mc], ssem.at[w], rsem.at[w], sib) for w in range(n)]
        for cp in sends:
            cp.start()
        for w in range(n):
            dst = os_[w].at[1 - mc]
            _rcopy(dst, dst, ssem.at[w], rsem.at[w], sib).wait_recv()
        for cp in sends:
            cp.wait_send()

    dma = pltpu.SemaphoreType.DMA
    return _multi_call(name, body, bs, [jax.ShapeDtypeStruct(b.shape, b.dtype) for b in bs], [dma((n,)), dma((n,))],
                       in_place=True)


def comm_sibling_swap(name, x):
    def body(x_ref, o_ref, ssem, rsem):
        mx, my, mc = _pos()
        cp = _rcopy(x_ref, o_ref, ssem, rsem, (mx, my, 1 - mc))
        cp.start()
        cp.wait_recv()
        cp.wait_send()

    return pl.pallas_call(
        body, out_shape=jax.ShapeDtypeStruct(x.shape, x.dtype), in_specs=[_ANY], out_specs=_ANY,
        scratch_shapes=[pltpu.SemaphoreType.DMA, pltpu.SemaphoreType.DMA], name=name)(x)


def _row_tile(r):
    return _pick(r, (256, 128, 64, 32, 16, 8))


def add_own_half(name, g, recv, out_dtype):
    r, c = g.shape[2:]
    tr = _row_tile(r)
    mc = lax.axis_index("c").astype(jnp.int32).reshape(1)

    def body(c_ref, g_ref, r_ref, o_ref):
        o_ref[...] = (g_ref[...] + r_ref[...]).astype(o_ref.dtype)

    grid_spec = pltpu.PrefetchScalarGridSpec(
        num_scalar_prefetch=1, grid=(N_CHIPS, r // tr),
        in_specs=[pl.BlockSpec((None, None, tr, c), lambda j, i, cr: (j, cr[0], i, 0)),
                  pl.BlockSpec((None, tr, c), lambda j, i, cr: (j, i, 0))],
        out_specs=pl.BlockSpec((None, tr, c), lambda j, i, cr: (j, i, 0)))
    return pl.pallas_call(body, grid_spec=grid_spec, out_shape=jax.ShapeDtypeStruct(recv.shape, out_dtype),
                          compiler_params=_cparams("parallel", "parallel"), name=name)(mc, g, recv)


def cast_into_slot(name, w, chip, dtype):
    r, c = w.shape
    tr = _row_tile(r)

    def body(c_ref, w_ref, o_ref):
        o_ref[...] = w_ref[...].astype(o_ref.dtype)

    grid_spec = pltpu.PrefetchScalarGridSpec(
        num_scalar_prefetch=1, grid=(r // tr,),
        in_specs=[pl.BlockSpec((tr, c), lambda i, cr: (i, 0))],
        out_specs=pl.BlockSpec((None, tr, c), lambda i, cr: (cr[0], i, 0)))
    return pl.pallas_call(body, grid_spec=grid_spec, out_shape=jax.ShapeDtypeStruct((N_CHIPS, r, c), dtype),
                          compiler_params=_cparams("parallel"), name=name)(chip.astype(jnp.int32).reshape(1), w)


def sum_chips_into_half(name, own, arrived, chip, mc):
    r, c = own.shape[1:]
    tr = _row_tile(r)

    def body(s0, s1, s2, s3, s4, own_ref, a_ref, b_ref, d_ref, o_ref):
        o_ref[...] = ((own_ref[...].astype(f32) + a_ref[...].astype(f32))
                      + (b_ref[...].astype(f32) + d_ref[...].astype(f32)))

    slot = lambda k: pl.BlockSpec((None, tr, c), lambda i, *sc, _k=k: (sc[_k][0], i, 0))
    grid_spec = pltpu.PrefetchScalarGridSpec(
        num_scalar_prefetch=5, grid=(r // tr,), in_specs=[slot(0), slot(1), slot(2), slot(3)],
        out_specs=pl.BlockSpec((None, tr, c), lambda i, *sc: (sc[4][0], i, 0)))
    mx, my = lax.axis_index("x"), lax.axis_index("y")
    scal = [v.astype(jnp.int32).reshape(1) for v in
            (2 * mx + my, 2 * (1 - mx) + my, 2 * mx + (1 - my), 2 * (1 - mx) + (1 - my), mc)]
    return pl.pallas_call(body, grid_spec=grid_spec, out_shape=jax.ShapeDtypeStruct((2, r, c), f32),
                          compiler_params=_cparams("parallel"), name=name)(*scal, own, arrived, arrived, arrived)


def sum_slots(name, x):
    r, c = x.shape[1:]
    tr = _row_tile(r)

    def body(x_ref, o_ref):
        o_ref[...] = (x_ref[0].astype(f32) + x_ref[1].astype(f32)) + (x_ref[2].astype(f32) + x_ref[3].astype(f32))

    return pl.pallas_call(
        body, grid=(r // tr,), in_specs=[pl.BlockSpec((N_CHIPS, tr, c), lambda i: (0, i, 0))],
        out_specs=pl.BlockSpec((tr, c), lambda i: (i, 0)), out_shape=jax.ShapeDtypeStruct((r, c), f32),
        compiler_params=_cparams("parallel"), name=name)(x)


def adamw(name, w, g, m, v):
    r, c = w.shape
    tr = _row_tile(r)

    def body(w_ref, g_ref, m_ref, v_ref, d_ref, nm_ref, nv_ref):
        gv = g_ref[...]
        nm = ADAM_B1 * m_ref[...] + (1.0 - ADAM_B1) * gv
        nv = ADAM_B2 * v_ref[...] + (1.0 - ADAM_B2) * (gv * gv)
        m_hat = nm / (1.0 - ADAM_B1 ** ADAM_STEP)
        v_hat = nv / (1.0 - ADAM_B2 ** ADAM_STEP)
        d_ref[...] = -ADAM_LR * (m_hat / (jnp.sqrt(v_hat) + ADAM_EPS) + ADAM_WD * w_ref[...])
        nm_ref[...] = nm
        nv_ref[...] = nv

    blk = pl.BlockSpec((tr, c), lambda i: (i, 0))
    return pl.pallas_call(
        body, grid=(r // tr,), in_specs=[blk] * 4, out_specs=[blk] * 3,
        out_shape=[jax.ShapeDtypeStruct((r, c), f32)] * 3,
        compiler_params=_cparams("parallel"), name=name)(w, g, m, v)


def _pack_rows(n):
    return -(-n // PACK_COLS)


def _pack(arrs, dtype, row_multiple):
    segs = []
    for a in arrs:
        flat = a.astype(dtype).reshape(-1)
        k = _pack_rows(flat.shape[0])
        segs.append(jnp.pad(flat, (0, k * PACK_COLS - flat.shape[0])).reshape(k, PACK_COLS))
    rows = sum(s.shape[0] for s in segs)
    pad = -rows % row_multiple
    if pad:
        segs.append(jnp.zeros((pad, PACK_COLS), dtype))
    return jnp.concatenate(segs, 0)


def _unpack(packed, shapes):
    out, r = [], 0
    for shp in shapes:
        n = math.prod(shp)
        k = _pack_rows(n)
        out.append(packed[r:r + k].reshape(-1)[:n].reshape(shp))
        r += k
    return out


def _gathered_to_full(g, axis):
    t = jnp.moveaxis(g, 0, axis)
    return t.reshape(t.shape[:axis] + (t.shape[axis] * t.shape[axis + 1],) + t.shape[axis + 2:])


def _full_to_shard_major(full, axis):
    shp = full.shape
    t = full.reshape(shp[:axis] + (N_CHIPS, shp[axis] // N_CHIPS) + shp[axis + 1:])
    return jnp.moveaxis(t, axis, 0)


GRAD_ROW_MULTIPLE = 256


def kernel(x, mem, positions, hyb_w_in, dn_conv_w, dn_a_log, dn_dt_bias, dn_norm_g, hyb_w_out, s5_a_re, s5_a_im, s5_log_dt, s5_b_re, s5_b_im, s5_c_re, s5_c_im, s5_d, s5_glu_wo, s5_glu_wg, ln_mix_g, ln_mix_b, xq_w, xk_w, xv_w, xo_w, ln_x_g, ln_x_b, ffn_wg, ffn_wu, ffn_wd, ln_ffn_g, ln_ffn_b, loss_target, m_hyb_w_in, m_dn_conv_w, m_dn_a_log, m_dn_dt_bias, m_dn_norm_g, m_hyb_w_out, m_s5_a_re, m_s5_a_im, m_s5_log_dt, m_s5_b_re, m_s5_b_im, m_s5_c_re, m_s5_c_im, m_s5_d, m_s5_glu_wo, m_s5_glu_wg, m_ln_mix_g, m_ln_mix_b, m_xq_w, m_xk_w, m_xv_w, m_xo_w, m_ln_x_g, m_ln_x_b, m_ffn_wg, m_ffn_wu, m_ffn_wd, m_ln_ffn_g, m_ln_ffn_b, v_hyb_w_in, v_dn_conv_w, v_dn_a_log, v_dn_dt_bias, v_dn_norm_g, v_hyb_w_out, v_s5_a_re, v_s5_a_im, v_s5_log_dt, v_s5_b_re, v_s5_b_im, v_s5_c_re, v_s5_c_im, v_s5_d, v_s5_glu_wo, v_s5_glu_wg, v_ln_mix_g, v_ln_mix_b, v_xq_w, v_xk_w, v_xv_w, v_xo_w, v_ln_x_g, v_ln_x_b, v_ffn_wg, v_ffn_wu, v_ffn_wd, v_ln_ffn_g, v_ln_ffn_b):
    a = dict(locals())
    big = [n for n in WEIGHT_NAMES if n in SHARD_AXIS and n not in GATHER_F32]
    small = [n for n in WEIGHT_NAMES if n not in big]
    chip = 2 * lax.axis_index("x") + lax.axis_index("y")
    for n in FFN_TRANSPOSED:
        for pre in ("", "m_", "v_"):
            a[pre + n] = jnp.swapaxes(a[pre + n], 1, 2)

    mc = lax.axis_index("c")
    view2 = lambda t: t.reshape(-1, t.shape[-1])
    slots = [cast_into_slot("slot_" + n, view2(a[n]), chip, bf16).reshape((N_CHIPS,) + a[n].shape) for n in big]
    gathered = comm_gather_weights("gather_w", slots)
    tiny4 = _unpack_slots(comm_allgather4("gather_w_tiny", _pack([a[n] for n in GATHER_F32], f32, 8)),
                          [a[n].shape for n in GATHER_F32])
    p = {n: a[n] for n in small if n not in GATHER_F32}
    for n, g4 in zip(GATHER_F32, tiny4):
        p[n] = _gathered_to_full(g4, SHARD_AXIS[n])
    for n, g4 in zip(big, gathered):
        p[n] = g4 if n in SHARD_ORDER_GRADS else _gathered_to_full(g4, SHARD_AXIS[n])

    loss, grad_x, grads = local_step(x[0], mem[0], positions, loss_target[0], p)
    loss = lax.psum(loss, ("x", "y", "c"))

    g4s = [grads[n] if n in SHARD_ORDER_GRADS else _full_to_shard_major(grads[n], SHARD_AXIS[n]) for n in big]
    recv = comm_sibling_halves("rs_sibling_halves", g4s)
    pairs = []
    for n, g4, r4 in zip(big, g4s, recv):
        lh, cols = g4.shape[1] // 2, g4.shape[-1]
        v4 = g4.reshape(N_CHIPS, 2, -1, cols)
        pairs.append(add_own_half("rs_add_" + n, v4, r4.reshape(N_CHIPS, -1, cols), bf16).reshape((N_CHIPS, lh) + g4.shape[2:]))
    arrived = comm_alltoall4("rs_alltoall", pairs)
    slot3 = lambda t: t.reshape(N_CHIPS, -1, t.shape[-1])
    halves = [sum_chips_into_half("rs_sum_" + n, slot3(pr), slot3(ar), chip, mc) for n, pr, ar in zip(big, pairs, arrived)]
    g_big = {n: t.reshape(a[n].shape) for n, t in zip(big, comm_sibling_join("rs_sibling_join", halves))}

    rpack = _pack([grads[n] for n in small], f32, 8)
    rpair = _add2("ar_add_sibling", rpack, comm_sibling_swap("ar_sibling_swap", rpack))
    g_small = _unpack(sum_slots("ar_sum_chips", comm_allgather4("ar_allgather", rpair)), [grads[n].shape for n in small])
    g_small = {n: (lax.dynamic_index_in_dim(_full_to_shard_major(g, SHARD_AXIS[n]), chip, 0, keepdims=False)
                   if n in SHARD_AXIS else g) for n, g in zip(small, g_small)}

    outs = {}
    for n in big:
        view = lambda t: t.reshape(-1, t.shape[-1])
        d, nm, nv = adamw("adamw_" + n, view(a[n]), view(g_big[n]), view(a["m_" + n]), view(a["v_" + n]))
        outs[n] = (g_big[n],) + tuple(t.reshape(a[n].shape) for t in (d, nm, nv))
    shapes = [a[n].shape for n in small]
    packs = [_pack([a[pre + n] for n in small], f32, 8) for pre in ("", "m_", "v_")]
    upd = adamw("adamw_small", packs[0], _pack([g_small[n] for n in small], f32, 8), packs[1], packs[2])
    for k, n in enumerate(small):
        outs[n] = (g_small[n],) + tuple(_unpack(buf, shapes)[k] for buf in upd)
    for n in FFN_TRANSPOSED:
        outs[n] = tuple(jnp.swapaxes(t, 1, 2) for t in outs[n])
    res = [loss, grad_x[None]]
    for kind in range(4):
        res += [outs[n][kind] for n in WEIGHT_NAMES]
    return tuple(res)


def _unpack_slots(gathered, shapes):
    out, r = [], 0
    for shp in shapes:
        n = math.prod(shp)
        k = _pack_rows(n)
        out.append(gathered[:, r:r + k].reshape(N_CHIPS, -1)[:, :n].reshape((N_CHIPS,) + tuple(shp)))
        r += k
    return out
```

```python
import functools
import math

import jax
import jax.numpy as jnp
from jax import lax
from jax.experimental import pallas as pl
from jax.experimental.pallas import tpu as pltpu

f32 = jnp.float32
bf16 = jnp.bfloat16

D_MODEL = 1024
DEPTH = 4
DN_HEADS = 4
DN_HEAD_DIM = 128
DN_KEY_DIM = 512
DN_QKV_DIM = 1536
DN_CONV = 4
SW_HEADS = 8
SW_HEAD_DIM = 64
SW_DIM = 512
SW_DILATIONS = (1, 4, 16)
SW_BLOCK = 128
ROPE_THETA = 10000.0
S5_GROUP = 16
S5_GROUPS = 64
S5_STATE = 64
X_HEADS = 4
X_HEAD_DIM = 256
FFN_HIDDEN = 2816
ALPHA = (2 * DEPTH) ** 0.25
LN_EPS = 1e-5
RMS_EPS = 1e-6
ADAM_LR, ADAM_B1, ADAM_B2, ADAM_EPS, ADAM_WD, ADAM_STEP = 0.001, 0.9, 0.999, 1e-08, 0.01, 10

BA_PAD = 256
PROJ_COLS = DN_QKV_DIM + DN_KEY_DIM + 3 * SW_DIM + BA_PAD
COL_Z = DN_QKV_DIM
COL_SWQ = COL_Z + DN_KEY_DIM
COL_SWK = COL_SWQ + SW_DIM
COL_SWV = COL_SWK + SW_DIM
COL_BA = COL_SWV + SW_DIM

LANES = 128
SUBLANES = 8
VMEM_LIMIT = 56 * 1024 * 1024
DN_CHUNK = 128
DN_HEADS_PER_STEP = 4


def _cparams(*sem):
    return pltpu.CompilerParams(dimension_semantics=tuple(sem), vmem_limit_bytes=VMEM_LIMIT)


def _dg(x, y, cx, cy):
    return lax.dot_general(x, y, (((cx,), (cy,)), ((), ())), preferred_element_type=f32)


@functools.partial(jax.custom_vjp, nondiff_argnums=(2, 3))
def bdot(a, b, ca, cb):
    return _dg(a.astype(bf16), b.astype(bf16), ca, cb)


def _bdot_fwd(a, b, ca, cb):
    return bdot(a, b, ca, cb), (a, b)


def _bdot_bwd(ca, cb, res, g):
    a, b = res
    g16, a16, b16 = g.astype(bf16), a.astype(bf16), b.astype(bf16)
    da = _dg(g16, b16, 1, 1 - cb) if ca == 1 else _dg(b16, g16, 1 - cb, 1)
    db = _dg(a16, g16, 1 - ca, 0) if cb == 0 else _dg(g16, a16, 0, 1 - ca)
    return da.astype(a.dtype), db.astype(b.dtype)


bdot.defvjp(_bdot_fwd, _bdot_bwd)


def _split_hi_lo(a):
    hi = a.astype(bf16)
    return hi, (a - hi.astype(f32)).astype(bf16)


def _dot3(a, b, ca, cb):
    a_hi, a_lo = _split_hi_lo(a)
    b_hi, b_lo = _split_hi_lo(b)
    return _dg(a_hi, b_hi, ca, cb) + (_dg(a_hi, b_lo, ca, cb) + _dg(a_lo, b_hi, ca, cb))


@jax.custom_vjp
def hdot3(a, b):
    return _dot3(a, b, 1, 0)


def _hdot3_fwd(a, b):
    return hdot3(a, b), (a, b)


def _hdot3_bwd(res, g):
    a, b = res
    return _dot3(g, b, 1, 1), _dot3(a, g, 0, 0)


hdot3.defvjp(_hdot3_fwd, _hdot3_bwd)


def hdot(a, b):
    return jnp.dot(a, b, precision=lax.Precision.HIGHEST, preferred_element_type=f32)


def _iota2(shape, dim):
    return lax.broadcasted_iota(jnp.int32, shape, dim)


def _row_spec(r, tm):
    if isinstance(r, tuple):
        arr, width, blk = r
        return arr, pl.BlockSpec((tm, width), lambda i, _b=blk: (i, _b))
    return r, pl.BlockSpec((tm, r.shape[1]), lambda i: (i, 0))


def _par_spec(p):
    return pl.BlockSpec(p.shape, lambda i, _n=p.ndim: (0,) * _n)


def rowmap(name, fn, rows, params, out_cols, tm, out_dtypes=None):
    arrs, specs = zip(*[_row_spec(r, tm) for r in rows])
    s = arrs[0].shape[0]
    n_in = len(rows) + len(params)
    out_dtypes = out_dtypes or [f32] * len(out_cols)

    def body(*refs):
        outs = fn(*[r[...] for r in refs[:n_in]])
        for o_ref, o in zip(refs[n_in:], outs):
            o_ref[...] = o.astype(o_ref.dtype)

    return pl.pallas_call(
        body, grid=(s // tm,),
        in_specs=list(specs) + [_par_spec(p) for p in params],
        out_specs=[pl.BlockSpec((tm, c), lambda i: (i, 0)) for c in out_cols],
        out_shape=[jax.ShapeDtypeStruct((s, c), dt) for c, dt in zip(out_cols, out_dtypes)],
        compiler_params=_cparams("parallel"), name=name)(*arrs, *params)


def rowmap_bwd(name, fn, rows, params, cts, tm, row_mask=None, par_mask=None, row_dtypes=None):
    arrs, specs = zip(*[_row_spec(r, tm) for r in rows])
    s = arrs[0].shape[0]
    ct_groups = [c if isinstance(c, list) else [c] for c in cts]
    ct_arrs, ct_specs = zip(*[_row_spec(a, tm) for grp in ct_groups for a in grp])
    cts = list(ct_arrs)
    nr, npar, nct = len(rows), len(params), len(cts)
    row_mask = row_mask or [True] * nr
    par_mask = par_mask or [True] * npar
    row_idx = [k for k in range(nr) if row_mask[k]]
    par_idx = [k for k in range(npar) if par_mask[k]]
    row_w = [specs[k].block_shape[1] for k in row_idx]

    def body(*refs):
        ins = [r[...] for r in refs[:nr + npar]]
        ct_refs = list(refs[nr + npar:nr + npar + nct])
        ctv = []
        for grp in ct_groups:
            acc = ct_refs.pop(0)[...].astype(f32)
            for _ in grp[1:]:
                acc = acc + ct_refs.pop(0)[...].astype(f32)
            ctv.append(acc)
        ctv = tuple(ctv)
        outs = refs[nr + npar + nct:]
        _, vjp = jax.vjp(fn, *ins)
        grads = vjp(ctv)
        for o_ref, k in zip(outs[:len(row_idx)], row_idx):
            o_ref[...] = grads[k].astype(o_ref.dtype)
        first = pl.program_id(0) == 0
        for o_ref, k in zip(outs[len(row_idx):], par_idx):
            g = grads[nr + k].astype(f32)

            @pl.when(first)
            def _(o_ref=o_ref, g=g):
                o_ref[...] = g

            @pl.when(jnp.logical_not(first))
            def _(o_ref=o_ref, g=g):
                o_ref[...] += g

    res = pl.pallas_call(
        body, grid=(s // tm,),
        in_specs=list(specs) + [_par_spec(p) for p in params]
        + list(ct_specs),
        out_specs=[pl.BlockSpec((tm, w), lambda i: (i, 0)) for w in row_w]
        + [_par_spec(params[k]) for k in par_idx],
        out_shape=[jax.ShapeDtypeStruct((s, w), dt) for w, dt in zip(row_w, row_dtypes or [f32] * len(row_w))]
        + [jax.ShapeDtypeStruct(params[k].shape, f32) for k in par_idx],
        compiler_params=_cparams("arbitrary"), name=name)(*arrs, *params, *cts)
    return list(res[:len(row_idx)]), list(res[len(row_idx):])


def _pick(n, prefs):
    for t in prefs:
        if n % t == 0:
            return t
    return n


MM_CHUNK = 512


def mm_nn(name, a, b, out_dtype=f32):
    m, k = a.shape
    n = b.shape[1]
    tm = _pick(m, (512, 256, 128))
    cn = _pick(n, (MM_CHUNK, 256, 128))

    def body(a_ref, b_ref, o_ref):
        av = a_ref[...].astype(bf16)
        for c in range(n // cn):
            sl = slice(c * cn, (c + 1) * cn)
            o_ref[:, sl] = _dg(av, b_ref[:, sl].astype(bf16), 1, 0).astype(o_ref.dtype)

    return pl.pallas_call(
        body, grid=(m // tm,),
        in_specs=[pl.BlockSpec((tm, k), lambda i: (i, 0)), pl.BlockSpec((k, n), lambda i: (0, 0))],
        out_specs=pl.BlockSpec((tm, n), lambda i: (i, 0)),
        out_shape=jax.ShapeDtypeStruct((m, n), out_dtype),
        compiler_params=_cparams("parallel"), name=name)(a, b)


def mm_nt(name, a, b, out_dtype=f32):
    m, n = a.shape
    k = b.shape[0]
    tm = _pick(m, (512, 256, 128))
    ck = _pick(k, (MM_CHUNK, 256, 128))

    def body(a_ref, b_ref, o_ref):
        av = a_ref[...].astype(bf16)
        for c in range(k // ck):
            sl = slice(c * ck, (c + 1) * ck)
            o_ref[:, sl] = _dg(av, b_ref[sl, :].astype(bf16), 1, 1).astype(o_ref.dtype)

    return pl.pallas_call(
        body, grid=(m // tm,),
        in_specs=[pl.BlockSpec((tm, n), lambda i: (i, 0)), pl.BlockSpec((k, n), lambda i: (0, 0))],
        out_specs=pl.BlockSpec((tm, k), lambda i: (i, 0)),
        out_shape=jax.ShapeDtypeStruct((m, k), out_dtype),
        compiler_params=_cparams("parallel"), name=name)(a, b)


def mm_tn(name, a, b, out_dtype=f32):
    s, m = a.shape
    n = b.shape[1]
    tn = _pick(n, (256, 128))
    cm = _pick(m, (256, 128))

    def body(a_ref, b_ref, o_ref):
        bv = b_ref[...].astype(bf16)
        for c in range(m // cm):
            sl = slice(c * cm, (c + 1) * cm)
            o_ref[sl, :] = _dg(a_ref[:, sl].astype(bf16), bv, 0, 0).astype(o_ref.dtype)

    return pl.pallas_call(
        body, grid=(n // tn,),
        in_specs=[pl.BlockSpec((s, m), lambda j: (0, 0)), pl.BlockSpec((s, tn), lambda j: (0, j))],
        out_specs=pl.BlockSpec((m, tn), lambda j: (0, j)),
        out_shape=jax.ShapeDtypeStruct((m, n), out_dtype),
        compiler_params=_cparams("parallel"), name=name)(a, b)


def _postnorm_tile(h, sub, g, b):
    z = ALPHA * h + sub
    mu = jnp.mean(z, -1, keepdims=True)
    zc = z - mu
    var = jnp.mean(zc * zc, -1, keepdims=True)
    return (zc * lax.rsqrt(var + LN_EPS) * g + b,)


def _swiglu_tile(au):
    a, u = au[:, :FFN_HIDDEN], au[:, FFN_HIDDEN:]
    return (jax.nn.silu(a) * u,)


def _glu_tile(og):
    o, g = og[:, :D_MODEL], og[:, D_MODEL:]
    return (o * jax.nn.sigmoid(g),)


def _xattn_tile(q, kv):
    outs = []
    for h in range(X_HEADS):
        sl = slice(h * X_HEAD_DIM, (h + 1) * X_HEAD_DIM)
        s = bdot(q[:, sl], kv[:, sl], 1, 1) * (X_HEAD_DIM ** -0.5)
        m = lax.stop_gradient(jnp.max(s, -1, keepdims=True))
        p = jnp.exp(s - m)
        p = p / jnp.sum(p, -1, keepdims=True)
        outs.append(bdot(p, kv[:, D_MODEL + h * X_HEAD_DIM:D_MODEL + (h + 1) * X_HEAD_DIM], 1, 0))
    return (jnp.concatenate(outs, -1),)


TM_ROW = 256


def postnorm_fwd(tag, h, sub, g, b):
    return rowmap("postnorm_" + tag, lambda *a: _postnorm_tile(*a) * 2, [h, sub], [g, b], [D_MODEL] * 2, TM_ROW,
                  out_dtypes=[f32, bf16])


def postnorm_bwd(tag, h, sub, g, b, dy):
    (dh, dsub), (dg, db) = rowmap_bwd("postnorm_bwd_" + tag, _postnorm_tile, [h, sub], [g, b], [dy], TM_ROW,
                                      row_dtypes=[f32, bf16])
    return dh, dsub, dg, db


def xattn_fwd(tag, h, mem, wq, wkv, wo):
    q = mm_nn("xq_" + tag, h, wq, out_dtype=bf16)
    kv = mm_nn("xkv_" + tag, mem, wkv)
    ao = rowmap("xattn_" + tag, _xattn_tile, [q], [kv], [D_MODEL], TM_ROW, out_dtypes=[bf16])[0]
    out = mm_nn("xo_" + tag, ao, wo)
    return out, (q, kv, ao)


def xattn_bwd(tag, h, mem, wq, wkv, wo, res, dout):
    q, kv, ao = res
    dwo = mm_tn("xo_dw_" + tag, ao, dout)
    dao = mm_nt("xo_dx_" + tag, dout, wo)
    (dq,), (dkv,) = rowmap_bwd("xattn_bwd_" + tag, _xattn_tile, [q], [kv], [dao], TM_ROW, row_dtypes=[bf16])
    dwq = mm_tn("xq_dw_" + tag, h, dq)
    dh = mm_nt("xq_dx_" + tag, dq, wq)
    dwkv = mm_tn("xkv_dw_" + tag, mem, dkv)
    return dh, dwq, dwkv, dwo


FFN_SHARD = FFN_HIDDEN // 4
TM_FFN = 512


def _silu_mul(a, u):
    return jax.nn.silu(a) * u


def ffn_fwd(tag, layer, h, wg, wu, wd):
    s = h.shape[0]
    tm, fs = TM_FFN, FFN_SHARD
    w_in = pl.BlockSpec((None, None, fs, D_MODEL), lambda k, i: (k, layer, 0, 0))
    act = pl.BlockSpec((None, tm, fs), lambda k, i: (k, i, 0))

    def up_body(h_ref, wg_ref, wu_ref, a_ref, u_ref, hid_ref):
        hv = h_ref[...].astype(bf16)
        a, u = _dg(hv, wg_ref[...], 1, 1), _dg(hv, wu_ref[...], 1, 1)
        a_ref[...], u_ref[...] = a.astype(bf16), u.astype(bf16)
        hid_ref[...] = _silu_mul(a, u).astype(bf16)

    a4, u4, hid4 = pl.pallas_call(
        up_body, grid=(4, s // tm),
        in_specs=[pl.BlockSpec((tm, D_MODEL), lambda k, i: (i, 0)), w_in, w_in],
        out_specs=[act, act, act],
        out_shape=[jax.ShapeDtypeStruct((4, s, fs), bf16)] * 3,
        compiler_params=_cparams("parallel", "parallel"), name="ffn_up_" + tag)(h, wg, wu)

    all_act = pl.BlockSpec((4, tm, fs), lambda i: (0, i, 0))
    all_w = pl.BlockSpec((4, None, fs, D_MODEL), lambda i: (0, layer, 0, 0))

    def down_body(hid_ref, wd_ref, o_ref):
        acc = _dg(hid_ref[0], wd_ref[0], 1, 0)
        for k in range(1, 4):
            acc = acc + _dg(hid_ref[k], wd_ref[k], 1, 0)
        o_ref[...] = acc

    out = pl.pallas_call(
        down_body, grid=(s // tm,), in_specs=[all_act, all_w],
        out_specs=pl.BlockSpec((tm, D_MODEL), lambda i: (i, 0)),
        out_shape=jax.ShapeDtypeStruct((s, D_MODEL), f32),
        compiler_params=_cparams("parallel"), name="ffn_down_" + tag)(hid4, wd)
    return out, (a4, u4, hid4)


def ffn_bwd(tag, layer, h, wg, wu, wd, res, dout):
    a4, u4, hid4 = res
    s = h.shape[0]
    tm, fs = TM_FFN, FFN_SHARD
    act = pl.BlockSpec((None, tm, fs), lambda k, i: (k, i, 0))

    def dact_body(do_ref, wd_ref, a_ref, u_ref, da_ref, du_ref):
        dhid = _dg(do_ref[...].astype(bf16), wd_ref[...], 1, 1)
        _, vjp = jax.vjp(_silu_mul, a_ref[...].astype(f32), u_ref[...].astype(f32))
        da, du = vjp(dhid)
        da_ref[...], du_ref[...] = da.astype(bf16), du.astype(bf16)

    da4, du4 = pl.pallas_call(
        dact_body, grid=(4, s // tm),
        in_specs=[pl.BlockSpec((tm, D_MODEL), lambda k, i: (i, 0)),
                  pl.BlockSpec((None, None, fs, D_MODEL), lambda k, i: (k, layer, 0, 0)), act, act],
        out_specs=[act, act], out_shape=[jax.ShapeDtypeStruct((4, s, fs), bf16)] * 2,
        compiler_params=_cparams("parallel", "parallel"), name="ffn_dact_" + tag)(dout, wd, a4, u4)

    all_act = pl.BlockSpec((4, tm, fs), lambda i: (0, i, 0))
    all_w = pl.BlockSpec((4, None, fs, D_MODEL), lambda i: (0, layer, 0, 0))

    def dx_body(da_ref, du_ref, wg_ref, wu_ref, o_ref):
        acc = _dg(da_ref[0], wg_ref[0], 1, 0) + _dg(du_ref[0], wu_ref[0], 1, 0)
        for k in range(1, 4):
            acc = acc + (_dg(da_ref[k], wg_ref[k], 1, 0) + _dg(du_ref[k], wu_ref[k], 1, 0))
        o_ref[...] = acc

    dh = pl.pallas_call(
        dx_body, grid=(s // tm,), in_specs=[all_act, all_act, all_w, all_w],
        out_specs=pl.BlockSpec((tm, D_MODEL), lambda i: (i, 0)),
        out_shape=jax.ShapeDtypeStruct((s, D_MODEL), f32),
        compiler_params=_cparams("parallel"), name="ffn_dx_" + tag)(da4, du4, wg, wu)

    tn = 256
    whole = pl.BlockSpec((None, s, fs), lambda k, j: (k, 0, 0))

    def dwin_body(h_ref, da_ref, du_ref, dwg_ref, dwu_ref):
        hv = h_ref[...].astype(bf16)
        dwg_ref[...] = _dg(da_ref[...], hv, 0, 0)
        dwu_ref[...] = _dg(du_ref[...], hv, 0, 0)

    dwg, dwu = pl.pallas_call(
        dwin_body, grid=(4, D_MODEL // tn),
        in_specs=[pl.BlockSpec((s, tn), lambda k, j: (0, j)), whole, whole],
        out_specs=[pl.BlockSpec((None, fs, tn), lambda k, j: (k, 0, j))] * 2,
        out_shape=[jax.ShapeDtypeStruct((4, fs, D_MODEL), f32)] * 2,
        compiler_params=_cparams("parallel", "parallel"), name="ffn_dwin_" + tag)(h, da4, du4)

    def dwd_body(hid_ref, do_ref, dwd_ref):
        dwd_ref[...] = _dg(hid_ref[...], do_ref[...].astype(bf16), 0, 0)

    dwd = pl.pallas_call(
        dwd_body, grid=(4, D_MODEL // tn),
        in_specs=[whole, pl.BlockSpec((s, tn), lambda k, j: (0, j))],
        out_specs=pl.BlockSpec((None, fs, tn), lambda k, j: (k, 0, j)),
        out_shape=jax.ShapeDtypeStruct((4, fs, D_MODEL), f32),
        compiler_params=_cparams("parallel", "parallel"), name="ffn_dwd_" + tag)(hid4, dout)
    return dh, dwg, dwu, dwd


def loss_head(y, target):
    s, d = y.shape
    tm = TM_ROW

    def body(y_ref, t_ref, part_ref, dy_ref):
        e = y_ref[...] - t_ref[...]
        dy_ref[...] = e * (1.0 / d)
        p = jnp.sum(e * e, 0, keepdims=True) * (0.5 / d)

        @pl.when(pl.program_id(0) == 0)
        def _():
            part_ref[...] = p

        @pl.when(pl.program_id(0) != 0)
        def _():
            part_ref[...] += p

    return pl.pallas_call(
        body, grid=(s // tm,),
        in_specs=[pl.BlockSpec((tm, d), lambda i: (i, 0))] * 2,
        out_specs=[pl.BlockSpec((1, d), lambda i: (0, 0)), pl.BlockSpec((tm, d), lambda i: (i, 0))],
        out_shape=[jax.ShapeDtypeStruct((1, d), f32), jax.ShapeDtypeStruct((s, d), f32)],
        compiler_params=_cparams("arbitrary"), name="loss_head")(y, target)


TM_CONV = 512


def _conv_rows(xx, w_ref, n_rows):
    a = w_ref[3:4, :] * xx
    for k in (1, 2, 3):
        a = a + w_ref[3 - k:4 - k, :] * pltpu.roll(xx, k, 0)
    return a


def _dn_act(a, is_qk):
    s = jax.nn.silu(a)
    n = s * lax.rsqrt(jnp.sum(s * s, -1, keepdims=True) + RMS_EPS)
    return jnp.where(is_qk, n, s)


def dn_conv_fwd(tag, proj, cw):
    s = proj.shape[0]
    tm, hb = TM_CONV, TM_CONV // SUBLANES

    def body(xh_ref, x_ref, w_ref, o_ref):
        j, t = pl.program_id(0), pl.program_id(1)
        halo = jnp.where(t > 0, xh_ref[...], 0.0)
        xx = jnp.concatenate([halo, x_ref[...]], 0)
        a = _conv_rows(xx, w_ref, tm + SUBLANES)
        o_ref[...] = _dn_act(a, j < 2 * DN_HEADS)[SUBLANES:, :]

    return pl.pallas_call(
        body, grid=(DN_QKV_DIM // LANES, s // tm),
        in_specs=[pl.BlockSpec((SUBLANES, LANES), lambda j, t: (jnp.maximum(t * hb - 1, 0), j)),
                  pl.BlockSpec((tm, LANES), lambda j, t: (t, j)),
                  pl.BlockSpec((DN_CONV, LANES), lambda j, t: (0, j))],
        out_specs=pl.BlockSpec((tm, LANES), lambda j, t: (t, j)),
        out_shape=jax.ShapeDtypeStruct((s, DN_QKV_DIM), f32),
        compiler_params=_cparams("parallel", "parallel"), name="dn_conv_" + tag)(proj, proj, cw)


def dn_conv_bwd(tag, proj, cw, dy):
    s = proj.shape[0]
    tm, hb = TM_CONV, TM_CONV // SUBLANES
    nt = s // tm
    n_ext = tm + 2 * SUBLANES

    def body(xb_ref, x_ref, xa_ref, dy_ref, dya_ref, w_ref, dx_ref, dw_ref):
        j, t = pl.program_id(0), pl.program_id(1)
        xx = jnp.concatenate([jnp.where(t > 0, xb_ref[...], 0.0), x_ref[...],
                              jnp.where(t < nt - 1, xa_ref[...], 0.0)], 0)
        dyy = jnp.concatenate([jnp.zeros((SUBLANES, LANES), f32), dy_ref[...],
                               jnp.where(t < nt - 1, dya_ref[...], 0.0)], 0)
        a = _conv_rows(xx, w_ref, n_ext)
        _, vjp = jax.vjp(lambda v: _dn_act(v, j < 2 * DN_HEADS), a)
        da, = vjp(dyy)
        dx = w_ref[3:4, :] * da
        for k in (1, 2, 3):
            dx = dx + w_ref[3 - k:4 - k, :] * pltpu.roll(da, n_ext - k, 0)
        dx_ref[...] = dx[SUBLANES:SUBLANES + tm, :]
        row = _iota2((n_ext, LANES), 0)
        da_in = jnp.where((row >= SUBLANES) & (row < SUBLANES + tm), da, 0.0)
        r8 = _iota2((SUBLANES, LANES), 0)
        dw = jnp.zeros((SUBLANES, LANES), f32)
        for k in range(DN_CONV):
            xs = xx if k == 0 else pltpu.roll(xx, k, 0)
            dw = dw + jnp.where(r8 == 3 - k, jnp.sum(da_in * xs, 0, keepdims=True), 0.0)

        @pl.when(t == 0)
        def _():
            dw_ref[...] = dw

        @pl.when(t != 0)
        def _():
            dw_ref[...] += dw

    nb8 = s // SUBLANES
    return pl.pallas_call(
        body, grid=(DN_QKV_DIM // LANES, nt),
        in_specs=[pl.BlockSpec((SUBLANES, LANES), lambda j, t: (jnp.maximum(t * hb - 1, 0), j)),
                  pl.BlockSpec((tm, LANES), lambda j, t: (t, j)),
                  pl.BlockSpec((SUBLANES, LANES), lambda j, t: (jnp.minimum((t + 1) * hb, nb8 - 1), j)),
                  pl.BlockSpec((tm, LANES), lambda j, t: (t, j)),
                  pl.BlockSpec((SUBLANES, LANES), lambda j, t: (jnp.minimum((t + 1) * hb, nb8 - 1), j)),
                  pl.BlockSpec((DN_CONV, LANES), lambda j, t: (0, j))],
        out_specs=[pl.BlockSpec((tm, LANES), lambda j, t: (t, j)),
                   pl.BlockSpec((SUBLANES, LANES), lambda j, t: (0, j))],
        out_shape=[jax.ShapeDtypeStruct((s, DN_QKV_DIM), f32), jax.ShapeDtypeStruct((SUBLANES, DN_QKV_DIM), f32)],
        compiler_params=_cparams("parallel", "arbitrary"), name="dn_conv_bwd_" + tag)(proj, proj, proj, dy, dy, cw)


def _gate_tile(ba, eb, ea, alog, dtb):
    beta = jax.nn.sigmoid(hdot(ba, eb))
    g = -jnp.exp(alog) * jax.nn.softplus(hdot(ba, ea) + dtb)
    return beta, g


def _each(fn, *lists):
    return [fn(*args) for args in zip(*lists)]


@functools.partial(jax.custom_vjp, nondiff_argnums=(1,))
def _halves(x, axis):
    h = x.shape[axis] // 2
    return (x[:h], x[h:]) if axis == 0 else (x[:, :h], x[:, h:])


def _halves_fwd(x, axis):
    return _halves(x, axis), None


def _halves_bwd(axis, _, g):
    return (jnp.concatenate(g, axis),)


_halves.defvjp(_halves_fwd, _halves_bwd)


def _tri_inv_unit(lowers):
    c = lowers[0].shape[0]
    r, col = _iota2((c, c), 0), _iota2((c, c), 1)
    eye = jnp.where(r == col, 1.0, 0.0).astype(f32)
    invs = None
    sh = 0
    while (1 << sh) < c:
        same_2b = lax.shift_right_logical(r, sh + 1) == lax.shift_right_logical(col, sh + 1)
        diff_b = lax.shift_right_logical(r, sh) != lax.shift_right_logical(col, sh)
        offs = [jnp.where(same_2b & diff_b, low, 0.0) for low in lowers]
        if invs is None:
            invs = [eye - off for off in offs]
        else:
            part = _each(hdot, invs, offs)
            invs = _each(lambda inv, p: inv - hdot(p, inv), invs, part)
        sh += 1
    return invs


@jax.custom_vjp
def _known_inverse(lower, tinv):
    return tinv


def _known_inverse_fwd(lower, tinv):
    return tinv, tinv


def _known_inverse_bwd(tinv, g):
    tt = tinv.T
    return -hdot(hdot(tt, g), tt), jnp.zeros_like(tinv)


_known_inverse.defvjp(_known_inverse_fwd, _known_inverse_bwd)


def _delta_chunk(q, k, v, gb, betab, state, tinv_known=None):
    c, hd = DN_CHUNK, DN_HEAD_DIM
    r, col = _iota2((c, c), 0), _iota2((c, c), 1)
    causal, strict = r >= col, r > col
    tril = jnp.where(causal, 1.0, 0.0).astype(f32)
    gc = _each(lambda g: hdot(tril, g), gb)
    decay = _each(lambda g: jnp.where(causal, jnp.exp(jnp.where(causal, g - g.T, 0.0)), 0.0), gc)
    qs = _each(lambda t: t * (DN_HEAD_DIM ** -0.5), q)
    kb = _each(lambda a, b: a * b, k, betab)
    kq = _each(lambda a, b, kk: _halves(bdot(jnp.concatenate([a, b], 0), kk, 1, 1), 0), kb, qs, k)
    lower = _each(lambda x, d: jnp.where(strict, x[0], 0.0) * d, kq, decay)
    intra = _each(lambda x, d: x[1] * d, kq, decay)
    tinv = _tri_inv_unit(lower) if tinv_known is None else _each(_known_inverse, lower, tinv_known)
    eg = _each(jnp.exp, gc)
    uw = _each(lambda t, vv, b, kb_, e: _halves(hdot(t, jnp.concatenate([vv * b, kb_ * e], 1)), 1),
               tinv, v, betab, kb, eg)
    gl = _each(lambda g: jnp.sum(jnp.where(r == c - 1, g, 0.0), 0, keepdims=True), gc)
    k_dec = _each(lambda kk, a, g: kk * jnp.exp(a - g), k, gl, gc)
    ws = _each(lambda x, t, e, st: _halves(bdot(jnp.concatenate([x[1], t * e], 0), st, 1, 0), 0), uw, qs, eg, state)
    v_new = _each(lambda x, y: x[0] - y[0], uw, ws)
    out = _each(lambda y, a, vn: y[1] + bdot(a, vn, 1, 0), ws, intra, v_new)
    new_state = _each(lambda st, a, kd, vn: st * jnp.exp(a) + bdot(kd, vn, 0, 0), state, gl, k_dec, v_new)
    return tuple(out), tuple(new_state), tuple(tinv)


def delta_fwd(tag, qkv, gb, betab):
    s = qkv.shape[0]
    c, hd = DN_CHUNK, DN_HEAD_DIM
    n = s // c

    hg, ng = DN_HEADS_PER_STEP, DN_HEADS // DN_HEADS_PER_STEP

    def body(q_ref, k_ref, v_ref, g_ref, b_ref, o_ref, st_ref, ti_ref, state):
        @pl.when(pl.program_id(1) == 0)
        def _():
            state[...] = jnp.zeros_like(state)

        heads = lambda ref: tuple(ref[:, j * hd:(j + 1) * hd] for j in range(hg))
        st = tuple(state[j] for j in range(hg))
        outs, news, tinv = _delta_chunk(heads(q_ref), heads(k_ref), heads(v_ref), heads(g_ref), heads(b_ref), st)
        for j in range(hg):
            st_ref[j] = st[j]
            ti_ref[j] = tinv[j]
            o_ref[:, j * hd:(j + 1) * hd] = outs[j]
            state[j] = news[j]

    blk = lambda off: pl.BlockSpec((c, hg * hd), lambda h, i, _o=off: (i, h + _o))
    per_chunk = pl.BlockSpec((hg, None, hd, hd), lambda h, i: (h, i, 0, 0))
    return pl.pallas_call(
        body, grid=(ng, n),
        in_specs=[blk(0), blk(ng), blk(2 * ng), blk(0), blk(0)],
        out_specs=[blk(0), per_chunk, per_chunk],
        out_shape=[jax.ShapeDtypeStruct((s, DN_KEY_DIM), f32)] + [jax.ShapeDtypeStruct((DN_HEADS, n, hd, hd), f32)] * 2,
        scratch_shapes=[pltpu.VMEM((hg, hd, hd), f32)],
        compiler_params=_cparams("parallel", "arbitrary"), name="delta_" + tag)(qkv, qkv, qkv, gb, betab)


def delta_bwd(tag, qkv, gb, betab, states, tinvs, do):
    s = qkv.shape[0]
    c, hd = DN_CHUNK, DN_HEAD_DIM
    n = s // c

    hg, ng = DN_HEADS_PER_STEP, DN_HEADS // DN_HEADS_PER_STEP

    def body(q_ref, k_ref, v_ref, g_ref, b_ref, st_ref, ti_ref, do_ref, dq_ref, dk_ref, dv_ref, dg_ref, db_ref, dstate):
        @pl.when(pl.program_id(1) == 0)
        def _():
            dstate[...] = jnp.zeros_like(dstate)

        heads = lambda ref: tuple(ref[:, j * hd:(j + 1) * hd] for j in range(hg))
        tinv = tuple(ti_ref[j] for j in range(hg))
        _, vjp = jax.vjp(lambda *args: _delta_chunk(*args, tinv_known=tinv)[:2],
                         heads(q_ref), heads(k_ref), heads(v_ref), heads(g_ref), heads(b_ref),
                         tuple(st_ref[j] for j in range(hg)))
        grads = vjp((heads(do_ref), tuple(dstate[j] for j in range(hg))))
        for ref, g in zip((dq_ref, dk_ref, dv_ref, dg_ref, db_ref), grads[:5]):
            for j in range(hg):
                ref[:, j * hd:(j + 1) * hd] = g[j]
        for j in range(hg):
            dstate[j] = grads[5][j]

    blk = lambda off: pl.BlockSpec((c, hg * hd), lambda h, i, _o=off: (n - 1 - i, h + _o))
    return pl.pallas_call(
        body, grid=(ng, n),
        in_specs=[blk(0), blk(ng), blk(2 * ng), blk(0), blk(0)]
        + [pl.BlockSpec((hg, None, hd, hd), lambda h, i: (h, n - 1 - i, 0, 0))] * 2 + [blk(0)],
        out_specs=[blk(0)] * 5,
        out_shape=[jax.ShapeDtypeStruct((s, DN_KEY_DIM), f32)] * 5,
        scratch_shapes=[pltpu.VMEM((hg, hd, hd), f32)],
        compiler_params=_cparams("parallel", "arbitrary"),
        name="delta_bwd_" + tag)(qkv, qkv, qkv, gb, betab, states, tinvs, do)


def _dn_out_tile(o, z, ng):
    outs = []
    for h in range(DN_HEADS):
        sl = slice(h * DN_HEAD_DIM, (h + 1) * DN_HEAD_DIM)
        oh = o[:, sl]
        nrm = oh * lax.rsqrt(jnp.mean(oh * oh, -1, keepdims=True) + RMS_EPS) * ng[:, sl]
        outs.append(nrm * jax.nn.silu(z[:, sl]))
    return (jnp.concatenate(outs, -1),)


def _head_selectors():
    r, c = _iota2((BA_PAD, DN_KEY_DIM), 0), _iota2((BA_PAD, DN_KEY_DIM), 1) // DN_HEAD_DIM
    return (r == c).astype(f32), (r == c + DN_HEADS).astype(f32)


def dn_mixer_fwd(tag, proj, cw, alog_b, dtb_b, ng_b):
    eb, ea = _head_selectors()
    ba = (proj, BA_PAD, COL_BA // BA_PAD)
    qkv = dn_conv_fwd(tag, proj, cw)
    betab, gb = rowmap("dn_gate_" + tag, _gate_tile, [ba], [eb, ea, alog_b, dtb_b], [DN_KEY_DIM] * 2, TM_ROW)
    o, states, tinvs = delta_fwd(tag, qkv, gb, betab)
    z = (proj, DN_KEY_DIM, COL_Z // DN_KEY_DIM)
    a_out = rowmap("dn_out_" + tag, _dn_out_tile, [o, z], [ng_b], [DN_KEY_DIM], TM_ROW)[0]
    return a_out, (qkv, betab, gb, o, states, tinvs)


def dn_mixer_bwd(tag, proj, cw, alog_b, dtb_b, ng_b, res, da_out):
    qkv, betab, gb, o, states, tinvs = res
    eb, ea = _head_selectors()
    ba = (proj, BA_PAD, COL_BA // BA_PAD)
    z = (proj, DN_KEY_DIM, COL_Z // DN_KEY_DIM)
    (do, dz), (dng,) = rowmap_bwd("dn_out_bwd_" + tag, _dn_out_tile, [o, z], [ng_b], [da_out], TM_ROW)
    dq, dk, dv, dgb, dbetab = delta_bwd(tag, qkv, gb, betab, states, tinvs, do)
    dqkv_raw, dcw = dn_conv_bwd(tag, proj, cw, jnp.concatenate([dq, dk, dv], 1))
    (dba,), (dalog, ddtb) = rowmap_bwd("dn_gate_bwd_" + tag, _gate_tile, [ba], [eb, ea, alog_b, dtb_b],
                                       [dbetab, dgb], TM_ROW, par_mask=[False, False, True, True])
    return dqkv_raw, dz, dba, dcw[:DN_CONV], dalog, ddtb, dng


def _swap_halves(x):
    n = x.shape[1]
    first = (_iota2((1, n), 1) % SW_HEAD_DIM) < SW_HEAD_DIM // 2
    return jnp.where(first, pltpu.roll(x, n - SW_HEAD_DIM // 2, 1), pltpu.roll(x, SW_HEAD_DIM // 2, 1))


def _rope_apply(x, cos, sin_signed):
    return x * cos + _swap_halves(x) * sin_signed


def _rope_transpose(dy, cos, sin_signed):
    return dy * cos + _swap_halves(dy * sin_signed)


def rope_tables(positions, s):
    half = SW_HEAD_DIM // 2
    inv_freq = ROPE_THETA ** (-jnp.arange(0, SW_HEAD_DIM, 2, dtype=f32) / SW_HEAD_DIM)
    ang = positions.reshape(s, 1).astype(f32) * inv_freq[None, :]
    cos, sin = jnp.cos(ang), jnp.sin(ang)
    cos_t = jnp.tile(jnp.concatenate([cos, cos], 1), (1, SW_HEADS))
    sin_t = jnp.tile(jnp.concatenate([-sin, sin], 1), (1, SW_HEADS))
    assert cos_t.shape == (s, SW_DIM) and half * 2 == SW_HEAD_DIM
    return cos_t, sin_t


def rope_fwd(tag, proj, cos, sin):
    def fn(q, k, v, c, sg):
        return _rope_apply(q, c, sg), _rope_apply(k, c, sg), v

    rows = [(proj, SW_DIM, COL_SWQ // SW_DIM), (proj, SW_DIM, COL_SWK // SW_DIM), (proj, SW_DIM, COL_SWV // SW_DIM), cos, sin]
    return rowmap("rope_" + tag, fn, rows, [], [SW_DIM] * 3, TM_ROW)


def _swa_block(q, kp, kc, vp, vc, first):
    blk = SW_BLOCK
    kk = jnp.concatenate([kp, kc], 0)
    vv = jnp.concatenate([vp, vc], 0)
    dist = (_iota2((blk, 2 * blk), 0) + blk) - _iota2((blk, 2 * blk), 1)
    kj = _iota2((blk, 2 * blk), 1)
    valid = (dist >= 0) & (dist <= blk) & ((kj >= blk) | jnp.logical_not(first))
    lane_head = _iota2((1, LANES), 1) // SW_HEAD_DIM
    outs, lses = [], []
    for p in range(SW_DIM // LANES):
        sl = slice(p * LANES, (p + 1) * LANES)
        qp, kp_, vp_ = q[:, sl], kk[:, sl], vv[:, sl]
        o_pair = jnp.zeros((blk, LANES), f32)
        l_pair = jnp.zeros((blk, LANES), f32)
        for e in range(LANES // SW_HEAD_DIM):
            msk = lane_head == e
            sc = bdot(jnp.where(msk, qp, 0.0), kp_, 1, 1) * (SW_HEAD_DIM ** -0.5)
            sc = jnp.where(valid, sc, -1e30)
            m = lax.stop_gradient(jnp.max(sc, -1, keepdims=True))
            pe = jnp.exp(sc - m)
            l = jnp.sum(pe, -1, keepdims=True)
            o = bdot(pe, vp_, 1, 0) / l
            o_pair = o_pair + jnp.where(msk, o, 0.0)
            l_pair = l_pair + jnp.where(msk, m + jnp.log(l), 0.0)
        outs.append(o_pair)
        lses.append(l_pair)
    return jnp.concatenate(outs, -1), jnp.concatenate(lses, -1)


def _swa_specs(r):
    cur = pl.BlockSpec((SW_BLOCK, SW_DIM), lambda rho, n: (n, rho))
    prev = pl.BlockSpec((SW_BLOCK, SW_DIM), lambda rho, n: (jnp.maximum(n - 1, 0), rho))
    return cur, prev


def swa_fwd(tag, r, q, k, v):
    s = q.shape[0]
    ln = s // r
    q2, k2, v2 = (t.reshape(ln, r * SW_DIM) for t in (q, k, v))
    cur, prev = _swa_specs(r)

    def body(q_ref, kp_ref, kc_ref, vp_ref, vc_ref, o_ref, l_ref):
        o, l = _swa_block(q_ref[...], kp_ref[...], kc_ref[...], vp_ref[...], vc_ref[...], pl.program_id(1) == 0)
        o_ref[...] = o
        l_ref[...] = l

    o, l = pl.pallas_call(
        body, grid=(r, ln // SW_BLOCK),
        in_specs=[cur, prev, cur, prev, cur], out_specs=[cur, cur],
        out_shape=[jax.ShapeDtypeStruct((ln, r * SW_DIM), f32)] * 2,
        compiler_params=_cparams("parallel", "parallel"), name=f"swa{r}_{tag}")(q2, k2, k2, v2, v2)
    return o.reshape(s, SW_DIM), l.reshape(s, SW_DIM)


def swa_bwd(tag, r, q, k, v, do, dl):
    s = q.shape[0]
    ln = s // r
    q2, k2, v2, do2, dl2 = (t.reshape(ln, r * SW_DIM) for t in (q, k, v, do, dl))
    cur, prev = _swa_specs(r)

    def body(q_ref, kp_ref, kc_ref, vp_ref, vc_ref, do_ref, dl_ref, dq_ref, dka_ref, dkb_ref, dva_ref, dvb_ref):
        first = pl.program_id(1) == 0
        _, vjp = jax.vjp(lambda *a: _swa_block(*a, first), q_ref[...], kp_ref[...], kc_ref[...], vp_ref[...], vc_ref[...])
        dq_ref[...], dka_ref[...], dkb_ref[...], dva_ref[...], dvb_ref[...] = vjp((do_ref[...], dl_ref[...]))

    outs = pl.pallas_call(
        body, grid=(r, ln // SW_BLOCK),
        in_specs=[cur, prev, cur, prev, cur, cur, cur], out_specs=[cur] * 5,
        out_shape=[jax.ShapeDtypeStruct((ln, r * SW_DIM), f32)] * 5,
        compiler_params=_cparams("parallel", "parallel"), name=f"swa{r}_bwd_{tag}")(q2, k2, k2, v2, v2, do2, dl2)
    return [t.reshape(s, SW_DIM) for t in outs]


def _combine_tile(o1, l1, o2, l2, o3, l3):
    m = lax.stop_gradient(jnp.maximum(jnp.maximum(l1, l2), l3))
    e1, e2, e3 = jnp.exp(l1 - m), jnp.exp(l2 - m), jnp.exp(l3 - m)
    return ((o1 * e1 + o2 * e2 + o3 * e3) / (e1 + e2 + e3),)


def swa_merge_bwd(tag, grads, cos, sin):
    s = cos.shape[0]
    tm = SW_BLOCK
    nt = s // tm
    here = pl.BlockSpec((tm, SW_DIM), lambda i: (i, 0))
    arrs, specs = [], []
    for r, g in zip(SW_DILATIONS, grads):
        ahead = pl.BlockSpec((tm, SW_DIM), lambda i, _r=r: (jnp.minimum(i + _r, nt - 1), 0))
        arrs += g
        specs += [here, ahead, here, ahead, here]

    def body(*refs):
        i = pl.program_id(0)
        c_ref, s_ref = refs[15], refs[16]
        dq_ref, dk_ref, dv_ref = refs[17:]
        dq = jnp.zeros((tm, SW_DIM), f32)
        dk = jnp.zeros((tm, SW_DIM), f32)
        dv = jnp.zeros((tm, SW_DIM), f32)
        for b, r in enumerate(SW_DILATIONS):
            gq, gka, gkb, gva, gvb = refs[5 * b:5 * b + 5]
            inside = i + r < nt
            dq = dq + gq[...]
            dk = dk + gkb[...] + jnp.where(inside, gka[...], 0.0)
            dv = dv + gvb[...] + jnp.where(inside, gva[...], 0.0)
        dq_ref[...] = _rope_transpose(dq, c_ref[...], s_ref[...])
        dk_ref[...] = _rope_transpose(dk, c_ref[...], s_ref[...])
        dv_ref[...] = dv

    return pl.pallas_call(
        body, grid=(nt,), in_specs=specs + [here, here], out_specs=[here] * 3,
        out_shape=[jax.ShapeDtypeStruct((s, SW_DIM), f32)] * 3,
        compiler_params=_cparams("parallel"), name="swa_merge_bwd_" + tag)(*arrs, cos, sin)


def swa_mixer_fwd(tag, proj, cos, sin):
    q, k, v = rope_fwd(tag, proj, cos, sin)
    ols = []
    for r in SW_DILATIONS:
        ols += list(swa_fwd(tag, r, q, k, v))
    b_out = rowmap("swa_comb_" + tag, _combine_tile, ols, [], [SW_DIM], TM_ROW)[0]
    return b_out, (q, k, v, ols)


def swa_mixer_bwd(tag, cos, sin, res, db_out):
    q, k, v, ols = res
    dols, _ = rowmap_bwd("swa_comb_bwd_" + tag, _combine_tile, ols, [], [db_out], TM_ROW)
    grads = [swa_bwd(tag, r, q, k, v, dols[2 * b], dols[2 * b + 1]) for b, r in enumerate(SW_DILATIONS)]
    return swa_merge_bwd(tag, grads, cos, sin)


TM_S5 = 256
S5_GPB = LANES // S5_GROUP
S5_NBLK = D_MODEL // LANES
S5_HALF = S5_GPB * S5_STATE
S5_BW = 2 * S5_HALF
S5_WIDTH = S5_NBLK * S5_BW
S5_TABW = S5_NBLK * S5_HALF


def _s5_disc_tile(a_re, a_im, log_dt, b_re, b_im, expand):
    dt = jnp.exp(log_dt)
    mag = jnp.exp(a_re * dt)
    abar_re, abar_im = mag * jnp.cos(a_im * dt), mag * jnp.sin(a_im * dt)
    n_re, n_im = abar_re - 1.0, abar_im
    den = a_re * a_re + a_im * a_im
    c_re = (n_re * a_re + n_im * a_im) / den
    c_im = (n_im * a_re - n_re * a_im) / den
    cx_re, cx_im = hdot(c_re, expand), hdot(c_im, expand)
    return abar_re, abar_im, cx_re * b_re - cx_im * b_im, cx_re * b_im + cx_im * b_re


def _s5_expand():
    return (_iota2((S5_STATE, S5_STATE * S5_GROUP), 1) // S5_GROUP == _iota2((S5_STATE, S5_STATE * S5_GROUP), 0)).astype(f32)


def s5_tables(a_re, a_im, log_dt):
    lanes = lambda v: v.reshape(1, S5_TABW)
    dt = jnp.broadcast_to(log_dt.reshape(S5_GROUPS, 1), (S5_GROUPS, S5_STATE))
    t = TM_S5

    def body(are_ref, aim_ref, ldt_ref, ar_ref, ai_ref, arr_ref, air_ref):
        dtv = jnp.exp(ldt_ref[...])
        lre, lim = are_ref[...] * dtv, aim_ref[...] * dtv
        row = _iota2((t, S5_HALF), 0)
        for asc, o_re, o_im in ((True, ar_ref, ai_ref), (False, arr_ref, air_ref)):
            n = (row + 1 if asc else t - row).astype(f32)
            mag = jnp.exp(n * lre)
            o_re[...] = mag * jnp.cos(n * lim)
            o_im[...] = mag * jnp.sin(n * lim)

    lane = pl.BlockSpec((1, S5_HALF), lambda j: (0, j))
    tab = pl.BlockSpec((t, S5_HALF), lambda j: (0, j))
    return pl.pallas_call(
        body, grid=(S5_NBLK,), in_specs=[lane] * 3, out_specs=[tab] * 4,
        out_shape=[jax.ShapeDtypeStruct((t, S5_TABW), f32)] * 4,
        compiler_params=_cparams("parallel"), name="s5_tables")(lanes(a_re), lanes(a_im), lanes(dt))


def s5_pack_weights(bbar_re, bbar_im, c_re, c_im):
    eye = jnp.eye(S5_GPB, dtype=f32)
    bb = jnp.stack([bbar_re.reshape(S5_GROUPS, S5_STATE, S5_GROUP), bbar_im.reshape(S5_GROUPS, S5_STATE, S5_GROUP)], 1)
    bb = bb.transpose(0, 3, 1, 2).reshape(S5_NBLK, S5_GPB, S5_GROUP, 2, S5_STATE)
    wb = (bb[:, :, :, :, None, :] * eye[None, :, None, None, :, None]).reshape(S5_NBLK, LANES, S5_BW)
    cc = jnp.stack([c_re, -c_im], 1)
    cc = cc.reshape(S5_NBLK, S5_GPB, 2, S5_GROUP, S5_STATE).transpose(0, 2, 1, 4, 3)
    wc = (cc[:, :, :, :, None, :] * eye[None, None, :, None, :, None]).reshape(S5_NBLK, S5_BW, LANES)
    return wb, wc


def s5_unpack_weight_grads(dwb, dwc):
    d6 = dwb.reshape(S5_NBLK, S5_GPB, S5_GROUP, 2, S5_GPB, S5_STATE)
    dbb = jnp.stack([d6[:, gl, :, :, gl, :] for gl in range(S5_GPB)])
    dbb = dbb.transpose(1, 0, 3, 4, 2).reshape(S5_GROUPS, 2, S5_STATE * S5_GROUP)
    c6 = dwc.reshape(S5_NBLK, 2, S5_GPB, S5_STATE, S5_GPB, S5_GROUP)
    dcc = jnp.stack([c6[:, :, gl, :, gl, :] for gl in range(S5_GPB)])
    dcc = dcc.transpose(1, 0, 2, 4, 3).reshape(S5_GROUPS, 2, S5_GROUP, S5_STATE)
    return dbb[:, 0], dbb[:, 1], dcc[:, 0], -dcc[:, 1]


def _s5_step_rows(t):
    d, out = 1, []
    while d < t:
        out.append(d)
        d *= 2
    return out


def s5_core_fwd(tag, u, wb, wc, a1, a2, dskip):
    s = u.shape[0]
    t = TM_S5

    def body(u_ref, wb_ref, wc_ref, ar_ref, ai_ref, d_ref, y_ref, x_ref, carry):
        @pl.when(pl.program_id(1) == 0)
        def _():
            carry[...] = jnp.zeros_like(carry)

        uv = u_ref[...]
        bu = bdot(uv, wb_ref[...], 1, 0)
        xr, xi = bu[:, :S5_HALF], bu[:, S5_HALF:]
        row = _iota2((t, S5_HALF), 0)
        for d in _s5_step_rows(t):
            keep = row >= d
            sr = jnp.where(keep, pltpu.roll(xr, d, 0), 0.0)
            si = jnp.where(keep, pltpu.roll(xi, d, 0), 0.0)
            ar, ai = ar_ref[d - 1:d, :], ai_ref[d - 1:d, :]
            xr, xi = xr + ar * sr - ai * si, xi + ar * si + ai * sr
        cr, ci = carry[:, :S5_HALF], carry[:, S5_HALF:]
        ar, ai = ar_ref[...], ai_ref[...]
        x_ref[:, :S5_HALF] = xr + ar * cr - ai * ci
        x_ref[:, S5_HALF:] = xi + ar * ci + ai * cr
        carry[...] = x_ref[t - 1:t, :]
        y_ref[...] = bdot(x_ref[...], wc_ref[...], 1, 0) + d_ref[...] * uv

    tab = pl.BlockSpec((t, S5_HALF), lambda j, i: (0, j))
    return pl.pallas_call(
        body, grid=(S5_NBLK, s // t),
        in_specs=[pl.BlockSpec((t, LANES), lambda j, i: (i, j)),
                  pl.BlockSpec((None, LANES, S5_BW), lambda j, i: (j, 0, 0)),
                  pl.BlockSpec((None, S5_BW, LANES), lambda j, i: (j, 0, 0)),
                  tab, tab, pl.BlockSpec((1, LANES), lambda j, i: (0, j))],
        out_specs=[pl.BlockSpec((t, LANES), lambda j, i: (i, j)), pl.BlockSpec((t, S5_BW), lambda j, i: (i, j))],
        out_shape=[jax.ShapeDtypeStruct((s, D_MODEL), f32), jax.ShapeDtypeStruct((s, S5_WIDTH), f32)],
        scratch_shapes=[pltpu.VMEM((1, S5_BW), f32)],
        compiler_params=_cparams("parallel", "arbitrary"), name="s5_core_" + tag)(u, wb, wc, a1, a2, dskip)


def s5_core_bwd(tag, u, x, wb, wc, a1, a2, a1r, a2r, dskip, dy):
    s = u.shape[0]
    t = TM_S5
    nt = s // t
    hb = t // SUBLANES

    def body(u_ref, dy_ref, x_ref, xh_ref, wb_ref, wc_ref, ar_ref, ai_ref, arr_ref, air_ref, d_ref,
             du_ref, dwb_ref, dwc_ref, dd_ref, q1_ref, q2_ref, carry, lam_scr):
        i = pl.program_id(1)
        tt = nt - 1 - i

        @pl.when(i == 0)
        def _():
            carry[...] = jnp.zeros_like(carry)

        uv, dyv, xv = u_ref[...], dy_ref[...], x_ref[...]
        lam = bdot(dyv, wc_ref[...], 1, 1)
        lr, li = lam[:, :S5_HALF], lam[:, S5_HALF:]
        row = _iota2((t, S5_HALF), 0)
        for d in _s5_step_rows(t):
            keep = row < t - d
            sr = jnp.where(keep, pltpu.roll(lr, t - d, 0), 0.0)
            si = jnp.where(keep, pltpu.roll(li, t - d, 0), 0.0)
            ar, ai = ar_ref[d - 1:d, :], ai_ref[d - 1:d, :]
            lr, li = lr + ar * sr + ai * si, li + ar * si - ai * sr
        cr, ci = carry[:, :S5_HALF], carry[:, S5_HALF:]
        ar, ai = arr_ref[...], air_ref[...]
        lr, li = lr + ar * cr + ai * ci, li + ar * ci - ai * cr
        lam_scr[:, :S5_HALF] = lr
        lam_scr[:, S5_HALF:] = li
        carry[...] = lam_scr[0:1, :]
        lam = lam_scr[...]
        du_ref[...] = bdot(lam, wb_ref[...], 1, 1) + d_ref[...] * dyv
        x_last = jnp.where(tt > 0, xh_ref[SUBLANES - 1:SUBLANES, :], 0.0)
        x_prev = jnp.where(_iota2((t, S5_BW), 0) == 0, x_last, pltpu.roll(xv, 1, 0))
        pr, pi = x_prev[:, :S5_HALF], x_prev[:, S5_HALF:]
        p1, p2 = lr * pr + li * pi, li * pr - lr * pi
        q1 = p1[:SUBLANES, :]
        q2 = p2[:SUBLANES, :]
        for k in range(1, hb):
            q1 = q1 + p1[k * SUBLANES:(k + 1) * SUBLANES, :]
            q2 = q2 + p2[k * SUBLANES:(k + 1) * SUBLANES, :]
        upd = [(dwb_ref, bdot(uv, lam, 0, 0)), (dwc_ref, bdot(xv, dyv, 0, 0)),
               (dd_ref, jnp.sum(dyv * uv, 0, keepdims=True)), (q1_ref, q1), (q2_ref, q2)]

        @pl.when(i == 0)
        def _():
            for ref, val in upd:
                ref[...] = val

        @pl.when(i != 0)
        def _():
            for ref, val in upd:
                ref[...] += val

    nb8 = s // SUBLANES
    rev = lambda w: pl.BlockSpec((t, w), lambda j, i: (nt - 1 - i, j))
    tab = pl.BlockSpec((t, S5_HALF), lambda j, i: (0, j))
    return pl.pallas_call(
        body, grid=(S5_NBLK, nt),
        in_specs=[rev(LANES), rev(LANES), rev(S5_BW),
                  pl.BlockSpec((SUBLANES, S5_BW), lambda j, i: (jnp.maximum((nt - 1 - i) * hb - 1, 0), j)),
                  pl.BlockSpec((None, LANES, S5_BW), lambda j, i: (j, 0, 0)),
                  pl.BlockSpec((None, S5_BW, LANES), lambda j, i: (j, 0, 0)),
                  tab, tab, tab, tab, pl.BlockSpec((1, LANES), lambda j, i: (0, j))],
        out_specs=[rev(LANES),
                   pl.BlockSpec((None, LANES, S5_BW), lambda j, i: (j, 0, 0)),
                   pl.BlockSpec((None, S5_BW, LANES), lambda j, i: (j, 0, 0)),
                   pl.BlockSpec((1, LANES), lambda j, i: (0, j)),
                   pl.BlockSpec((SUBLANES, S5_HALF), lambda j, i: (0, j)),
                   pl.BlockSpec((SUBLANES, S5_HALF), lambda j, i: (0, j))],
        out_shape=[jax.ShapeDtypeStruct((s, D_MODEL), f32),
                   jax.ShapeDtypeStruct((S5_NBLK, LANES, S5_BW), f32),
                   jax.ShapeDtypeStruct((S5_NBLK, S5_BW, LANES), f32),
                   jax.ShapeDtypeStruct((1, D_MODEL), f32),
                   jax.ShapeDtypeStruct((SUBLANES, S5_TABW), f32),
                   jax.ShapeDtypeStruct((SUBLANES, S5_TABW), f32)],
        scratch_shapes=[pltpu.VMEM((1, S5_BW), f32), pltpu.VMEM((t, S5_BW), f32)],
        compiler_params=_cparams("parallel", "arbitrary"),
        name="s5_core_bwd_" + tag)(u, dy, x, x, wb, wc, a1, a2, a1r, a2r, dskip)


def _gelu_tile(y):
    return (jax.nn.gelu(y),)


def s5_mixer_fwd(tag, u, prm, w_og):
    a_re, a_im, log_dt, b_re, b_im, c_re, c_im, dskip = prm
    disc_in = [a_re, a_im, log_dt.reshape(S5_GROUPS, 1), b_re.reshape(S5_GROUPS, -1), b_im.reshape(S5_GROUPS, -1)]
    abar_re, abar_im, bbar_re, bbar_im = rowmap("s5_disc_" + tag, _s5_disc_tile, disc_in, [_s5_expand()],
                                                [S5_STATE, S5_STATE, S5_STATE * S5_GROUP, S5_STATE * S5_GROUP], S5_GROUPS)
    del abar_re, abar_im
    a1, a2, a1r, a2r = s5_tables(a_re, a_im, log_dt)
    wb, wc = s5_pack_weights(bbar_re, bbar_im, c_re, c_im)
    wb, wc = wb.astype(bf16), wc.astype(bf16)
    y, x = s5_core_fwd(tag, u, wb, wc, a1, a2, dskip.reshape(1, D_MODEL))
    hid = rowmap("s5_gelu_" + tag, _gelu_tile, [y], [], [D_MODEL], TM_ROW, out_dtypes=[bf16])[0]
    og = mm_nn("s5_og_" + tag, hid, w_og)
    mix = rowmap("s5_glu_" + tag, _glu_tile, [og], [], [D_MODEL], TM_ROW)[0]
    return mix, (disc_in, a1, a2, a1r, a2r, wb, wc, x, y, hid, og)


def s5_mixer_bwd(tag, u, prm, w_og, res, dmix):
    a_re, a_im, log_dt, b_re, b_im, c_re, c_im, dskip = prm
    disc_in, a1, a2, a1r, a2r, wb, wc, x, y, hid, og = res
    (dog,), _ = rowmap_bwd("s5_glu_bwd_" + tag, _glu_tile, [og], [], [dmix], TM_ROW)
    dw_og = mm_tn("s5_og_dw_" + tag, hid, dog)
    dhid = mm_nt("s5_og_dx_" + tag, dog, w_og)
    (dy,), _ = rowmap_bwd("s5_gelu_bwd_" + tag, _gelu_tile, [y], [], [dhid], TM_ROW)
    du, dwb, dwc, ddskip, q1, q2 = s5_core_bwd(tag, u, x, wb, wc, a1, a2, a1r, a2r, dskip.reshape(1, D_MODEL), dy)
    dbbar_re, dbbar_im, dc_re, dc_im = s5_unpack_weight_grads(dwb, dwc)
    dabar_re = q1.sum(0).reshape(S5_GROUPS, S5_STATE)
    dabar_im = q2.sum(0).reshape(S5_GROUPS, S5_STATE)
    grads, _ = rowmap_bwd("s5_disc_bwd_" + tag, _s5_disc_tile, disc_in, [_s5_expand()],
                          [dabar_re, dabar_im, dbbar_re, dbbar_im], S5_GROUPS, par_mask=[False])
    da_re, da_im, dlog_dt, db_re, db_im = grads
    return du, (da_re, da_im, dlog_dt.reshape(S5_GROUPS), db_re.reshape(b_re.shape), db_im.reshape(b_im.shape),
                dc_re, dc_im, ddskip.reshape(D_MODEL)), dw_og


HYB_IN = 3592
_IN_B0, _IN_SW0 = 2048, 2056


IN_SHARD = HYB_IN // 4
SHARD_ORDER_GRADS = ("hyb_w_in", "ffn_wg", "ffn_wu", "ffn_wd")
FFN_TRANSPOSED = ("ffn_wg", "ffn_wu")


def _w_in_pieces():
    runs = [(0, _IN_B0, 0), (_IN_B0, _IN_SW0, COL_BA), (_IN_SW0, HYB_IN, _IN_B0)]
    out = []
    for sh in range(4):
        lo, hi = sh * IN_SHARD, (sh + 1) * IN_SHARD
        for r_lo, r_hi, c_lo in runs:
            a, b = max(lo, r_lo), min(hi, r_hi)
            if a < b:
                out.append((sh, a - lo, b - lo, c_lo + a - r_lo))
    return out


def w_in_to_canonical(tag, layer, w4):
    tr = 128

    def body(w_ref, o_ref):
        o_ref[:, COL_BA:] = jnp.zeros((tr, BA_PAD), o_ref.dtype)
        for sh, a, b, c in _w_in_pieces():
            o_ref[:, c:c + b - a] = w_ref[sh, :, a:b]

    return pl.pallas_call(
        body, grid=(D_MODEL // tr,),
        in_specs=[pl.BlockSpec((4, None, tr, IN_SHARD), lambda i: (0, layer, i, 0))],
        out_specs=pl.BlockSpec((tr, PROJ_COLS), lambda i: (i, 0)),
        out_shape=jax.ShapeDtypeStruct((D_MODEL, PROJ_COLS), w4.dtype),
        compiler_params=_cparams("parallel"), name="w_in_canon_" + tag)(w4)


def w_in_grad_to_shards(tag, g):
    tr = 128

    def body(g_ref, o_ref):
        for sh, a, b, c in _w_in_pieces():
            o_ref[sh, :, a:b] = g_ref[:, c:c + b - a]

    return pl.pallas_call(
        body, grid=(D_MODEL // tr,),
        in_specs=[pl.BlockSpec((tr, PROJ_COLS), lambda i: (i, 0))],
        out_specs=pl.BlockSpec((4, tr, IN_SHARD), lambda i: (0, i, 0)),
        out_shape=jax.ShapeDtypeStruct((4, D_MODEL, IN_SHARD), f32),
        compiler_params=_cparams("parallel"), name="w_in_grad_shards_" + tag)(g)


def _add2(name, a, b):
    return rowmap(name, lambda p, q: (p + q,), [a, b], [], [a.shape[1]], _pick(a.shape[0], (256, 128, 64, 32, 16, 8)))[0]


def local_step(x, mem, positions, target, p):
    s = x.shape[0]
    cos, sin = rope_tables(positions, s)
    row = lambda v: v.reshape(1, -1).astype(f32)
    wg4, wu4, wd4 = (p[n].astype(bf16) for n in ("ffn_wg", "ffn_wu", "ffn_wd"))
    h = h16 = x
    tape = []
    for l in range(DEPTH):
        i, tag = l // 2, str(l)
        t = {"h0": h, "h0_16": h16}
        if l % 2 == 0:
            t["w_in"] = w_in_to_canonical(tag, i, p["hyb_w_in"].astype(bf16))
            t["w_out"] = p["hyb_w_out"][i].astype(bf16)
            t["dn_prm"] = (p["dn_conv_w"][i].astype(f32), row(jnp.repeat(p["dn_a_log"][i], DN_HEAD_DIM)),
                           row(jnp.repeat(p["dn_dt_bias"][i], DN_HEAD_DIM)), row(jnp.tile(p["dn_norm_g"][i], DN_HEADS)))
            t["proj"] = mm_nn("hyb_in_" + tag, h16, t["w_in"])
            a_out, t["dn"] = dn_mixer_fwd(tag, t["proj"], *t["dn_prm"])
            b_out, t["swa"] = swa_mixer_fwd(tag, t["proj"], cos, sin)
            t["mixed"] = jnp.concatenate([a_out, b_out], 1)
            mix = mm_nn("hyb_out_" + tag, t["mixed"], t["w_out"])
        else:
            t["s5_prm"] = tuple(p[n][i].astype(f32) for n in
                                ("s5_a_re", "s5_a_im", "s5_log_dt", "s5_b_re", "s5_b_im", "s5_c_re", "s5_c_im", "s5_d"))
            t["w_og"] = jnp.concatenate([p["s5_glu_wo"][i], p["s5_glu_wg"][i]], 1).astype(bf16)
            mix, t["s5"] = s5_mixer_fwd(tag, h, t["s5_prm"], t["w_og"])
        t["mix"] = mix
        t["ln"] = [(row(p[g][l]), row(p[b][l])) for g, b in
                   (("ln_mix_g", "ln_mix_b"), ("ln_x_g", "ln_x_b"), ("ln_ffn_g", "ln_ffn_b"))]
        t["h1"], t["h1_16"] = postnorm_fwd("mix" + tag, h, mix, *t["ln"][0])
        t["wq"], t["wo"] = p["xq_w"][l].astype(bf16), p["xo_w"][l].astype(bf16)
        t["wkv"] = jnp.concatenate([p["xk_w"][l], p["xv_w"][l]], 1).astype(bf16)
        t["xo"], t["xres"] = xattn_fwd(tag, t["h1_16"], mem, t["wq"], t["wkv"], t["wo"])
        t["h2"], t["h2_16"] = postnorm_fwd("x" + tag, t["h1"], t["xo"], *t["ln"][1])
        t["fo"], t["fres"] = ffn_fwd(tag, l, t["h2_16"], wg4, wu4, wd4)
        h, h16 = postnorm_fwd("ffn" + tag, t["h2"], t["fo"], *t["ln"][2])
        tape.append(t)

    part, dh = loss_head(h, target)
    loss = jnp.sum(part)

    g = {n: [None] * v.shape[1 if n in SHARD_ORDER_GRADS else 0] for n, v in p.items()}
    for l in reversed(range(DEPTH)):
        i, tag, t = l // 2, str(l), tape[l]
        dh2a, dfo, dg, db = postnorm_bwd("ffn" + tag, t["h2"], t["fo"], *t["ln"][2], dh)
        g["ln_ffn_g"][l], g["ln_ffn_b"][l] = dg[0], db[0]
        dh2b, g["ffn_wg"][l], g["ffn_wu"][l], g["ffn_wd"][l] = ffn_bwd(tag, l, t["h2_16"], wg4, wu4, wd4, t["fres"], dfo)
        dh1a, dxo, dg, db = postnorm_bwd("x" + tag, t["h1"], t["xo"], *t["ln"][1], [dh2a, dh2b])
        g["ln_x_g"][l], g["ln_x_b"][l] = dg[0], db[0]
        dh1b, g["xq_w"][l], dwkv, g["xo_w"][l] = xattn_bwd(tag, t["h1_16"], mem, t["wq"], t["wkv"], t["wo"], t["xres"], dxo)
        g["xk_w"][l], g["xv_w"][l] = dwkv[:, :D_MODEL], dwkv[:, D_MODEL:]
        dh0a, dmix, dg, db = postnorm_bwd("mix" + tag, t["h0"], t["mix"], *t["ln"][0], [dh1a, dh1b])
        g["ln_mix_g"][l], g["ln_mix_b"][l] = dg[0], db[0]
        if l % 2 == 0:
            g["hyb_w_out"][i] = mm_tn("hyb_out_dw_" + tag, t["mixed"], dmix)
            dmixed = mm_nt("hyb_out_dx_" + tag, dmix, t["w_out"])
            dqkv, dz, dba, dcw, dalog, ddtb, dng = dn_mixer_bwd(tag, t["proj"], *t["dn_prm"], t["dn"], (dmixed, DN_KEY_DIM, 0))
            g["dn_conv_w"][i] = dcw
            g["dn_a_log"][i] = dalog.reshape(DN_HEADS, DN_HEAD_DIM).sum(1)
            g["dn_dt_bias"][i] = ddtb.reshape(DN_HEADS, DN_HEAD_DIM).sum(1)
            g["dn_norm_g"][i] = dng.reshape(DN_HEADS, DN_HEAD_DIM).sum(0)
            dq, dk, dv = swa_mixer_bwd(tag, cos, sin, t["swa"], (dmixed, SW_DIM, 1))
            dproj = jnp.concatenate([dqkv, dz, dq, dk, dv, dba], 1)
            g["hyb_w_in"][i] = w_in_grad_to_shards(tag, mm_tn("hyb_in_dw_" + tag, t["h0_16"], dproj))
            dh0b = mm_nt("hyb_in_dx_" + tag, dproj, t["w_in"])
        else:
            dh0b, dprm, dw_og = s5_mixer_bwd(tag, t["h0"], t["s5_prm"], t["w_og"], t["s5"], dmix)
            for n, v in zip(("s5_a_re", "s5_a_im", "s5_log_dt", "s5_b_re", "s5_b_im", "s5_c_re", "s5_c_im", "s5_d"), dprm):
                g[n][i] = v
            g["s5_glu_wo"][i], g["s5_glu_wg"][i] = dw_og[:, :D_MODEL], dw_og[:, D_MODEL:]
        dh = [dh0a, dh0b]
    grad_x = _add2("grad_x", dh[0], dh[1])
    grads = {n: jnp.stack(v, 1 if n in SHARD_ORDER_GRADS else 0) for n, v in g.items()}
    return loss, grad_x, grads


WEIGHT_NAMES = ("hyb_w_in", "dn_conv_w", "dn_a_log", "dn_dt_bias", "dn_norm_g", "hyb_w_out", "s5_a_re", "s5_a_im",
                "s5_log_dt", "s5_b_re", "s5_b_im", "s5_c_re", "s5_c_im", "s5_d", "s5_glu_wo", "s5_glu_wg",
                "ln_mix_g", "ln_mix_b", "xq_w", "xk_w", "xv_w", "xo_w", "ln_x_g", "ln_x_b",
                "ffn_wg", "ffn_wu", "ffn_wd", "ln_ffn_g", "ln_ffn_b")
SHARD_AXIS = {"hyb_w_in": 2, "dn_conv_w": 2, "hyb_w_out": 1, "s5_d": 1, "s5_glu_wo": 1, "s5_glu_wg": 1,
              "xq_w": 1, "xk_w": 1, "xv_w": 1, "xo_w": 1, "ffn_wg": 2, "ffn_wu": 2, "ffn_wd": 1}
GATHER_F32 = ("dn_conv_w", "s5_d")
N_CHIPS = 4
PACK_COLS = 1024
_ANY = pl.BlockSpec(memory_space=pl.ANY)


def _pos():
    return lax.axis_index("x"), lax.axis_index("y"), lax.axis_index("c")


def _chip_peers(mx, my):
    return [(1 - mx, my), (mx, 1 - my), (1 - mx, 1 - my)]


def _rcopy(src, dst, ssem, rsem, dev):
    return pltpu.make_async_remote_copy(src_ref=src, dst_ref=dst, send_sem=ssem, recv_sem=rsem,
                                        device_id=dev, device_id_type=pl.DeviceIdType.MESH)


def comm_allgather4(name, x):
    def body(x_ref, o_ref, ssem, rsem, lsem):
        mx, my, mc = _pos()
        me = 2 * mx + my
        peers = _chip_peers(mx, my)
        loc = pltpu.make_async_copy(x_ref, o_ref.at[me], lsem)
        loc.start()
        sends = [_rcopy(x_ref, o_ref.at[me], ssem.at[k], rsem.at[k], (px, py, mc)) for k, (px, py) in enumerate(peers)]
        for cp in sends:
            cp.start()
        for k, (px, py) in enumerate(peers):
            _rcopy(x_ref, o_ref.at[2 * px + py], ssem.at[k], rsem.at[k], (px, py, mc)).wait_recv()
        for cp in sends:
            cp.wait_send()
        loc.wait()

    return pl.pallas_call(
        body, out_shape=jax.ShapeDtypeStruct((N_CHIPS,) + x.shape, x.dtype), in_specs=[_ANY], out_specs=_ANY,
        scratch_shapes=[pltpu.SemaphoreType.DMA((3,)), pltpu.SemaphoreType.DMA((3,)), pltpu.SemaphoreType.DMA],
        name=name)(x)


def _multi_call(name, body, ins, out_shapes, sems, in_place=False):
    return pl.pallas_call(
        body, out_shape=out_shapes, in_specs=[_ANY] * len(ins), out_specs=[_ANY] * len(out_shapes),
        scratch_shapes=sems, input_output_aliases={w: w for w in range(len(ins))} if in_place else {},
        name=name)(*ins)


def comm_gather_weights(name, slots):
    n = len(slots)

    def body(*refs):
        os_ = refs[n:2 * n]
        ssem, rsem, fssem, frsem = refs[2 * n:]
        mx, my, mc = _pos()
        me = 2 * mx + my
        peers = _chip_peers(mx, my)
        sib = (mx, my, 1 - mc)
        half = [o.shape[1] // 2 for o in os_]
        mine = [pl.ds(mc * h, h) for h in half]
        other = [pl.ds((1 - mc) * h, h) for h in half]
        sends = [_rcopy(os_[w].at[me, mine[w]], os_[w].at[me, mine[w]], ssem.at[w, k], rsem.at[w, k], (px, py, mc))
                 for w in range(n) for k, (px, py) in enumerate(peers)]
        for cp in sends:
            cp.start()
        fwds = []
        for w in range(n):
            for k, (px, py) in enumerate(peers):
                landed = os_[w].at[2 * px + py, mine[w]]
                _rcopy(landed, landed, ssem.at[w, k], rsem.at[w, k], (px, py, mc)).wait_recv()
                fw = _rcopy(landed, landed, fssem.at[w, k], frsem.at[w, k], sib)
                fw.start()
                fwds.append(fw)
        for w in range(n):
            for k, (px, py) in enumerate(peers):
                theirs = os_[w].at[2 * px + py, other[w]]
                _rcopy(theirs, theirs, fssem.at[w, k], frsem.at[w, k], sib).wait_recv()
        for cp in sends + fwds:
            cp.wait_send()

    dma = pltpu.SemaphoreType.DMA
    return _multi_call(name, body, slots, [jax.ShapeDtypeStruct(x.shape, x.dtype) for x in slots],
                       [dma((n, 3)), dma((n, 3)), dma((n, 3)), dma((n, 3))], in_place=True)


def comm_sibling_halves(name, gs):
    n = len(gs)

    def body(*refs):
        xs, os_ = refs[:n], refs[n:2 * n]
        ssem, rsem = refs[2 * n:]
        mx, my, mc = _pos()
        sib = (mx, my, 1 - mc)
        sends = []
        for w in range(n):
            h = xs[w].shape[1] // 2
            for j in range(N_CHIPS):
                sends.append(_rcopy(xs[w].at[j, pl.ds((1 - mc) * h, h)], os_[w].at[j], ssem.at[w, j], rsem.at[w, j], sib))
        for cp in sends:
            cp.start()
        for w in range(n):
            for j in range(N_CHIPS):
                _rcopy(os_[w].at[j], os_[w].at[j], ssem.at[w, j], rsem.at[w, j], sib).wait_recv()
        for cp in sends:
            cp.wait_send()

    dma = pltpu.SemaphoreType.DMA
    return _multi_call(name, body, gs,
                       [jax.ShapeDtypeStruct((N_CHIPS, g.shape[1] // 2) + g.shape[2:], g.dtype) for g in gs],
                       [dma((n, N_CHIPS)), dma((n, N_CHIPS))])


def comm_alltoall4(name, xs):
    n = len(xs)

    def body(*refs):
        xr, os_ = refs[:n], refs[n:2 * n]
        ssem, rsem = refs[2 * n:]
        mx, my, mc = _pos()
        me = 2 * mx + my
        peers = _chip_peers(mx, my)
        sends = [_rcopy(xr[w].at[2 * px + py], os_[w].at[me], ssem.at[w, k], rsem.at[w, k], (px, py, mc))
                 for w in range(n) for k, (px, py) in enumerate(peers)]
        for cp in sends:
            cp.start()
        for w in range(n):
            for k, (px, py) in enumerate(peers):
                dst = os_[w].at[2 * px + py]
                _rcopy(dst, dst, ssem.at[w, k], rsem.at[w, k], (px, py, mc)).wait_recv()
        for cp in sends:
            cp.wait_send()

    dma = pltpu.SemaphoreType.DMA
    return _multi_call(name, body, xs, [jax.ShapeDtypeStruct(x.shape, x.dtype) for x in xs], [dma((n, 3)), dma((n, 3))])


def comm_sibling_join(name, bs):
    n = len(bs)

    def body(*refs):
        os_ = refs[n:2 * n]
        ssem, rsem = refs[2 * n:]
        mx, my, mc = _pos()
        sib = (mx, my, 1 - mc)
        sends = [_rcopy(os_[w].at[mc], os_[w].at[mc], ssem.at[w], rsem.at[w], sib) for w in range(n)]
        for cp in sends:
            cp.start()
        for w in range(n):
            dst = os_[w].at[1 - mc]
            _rcopy(dst, dst, ssem.at[w], rsem.at[w], sib).wait_recv()
        for cp in sends:
            cp.wait_send()

    dma = pltpu.SemaphoreType.DMA
    return _multi_call(name, body, bs, [jax.ShapeDtypeStruct(b.shape, b.dtype) for b in bs], [dma((n,)), dma((n,))],
                       in_place=True)


def comm_sibling_swap(name, x):
    def body(x_ref, o_ref, ssem, rsem):
        mx, my, mc = _pos()
        cp = _rcopy(x_ref, o_ref, ssem, rsem, (mx, my, 1 - mc))
        cp.start()
        cp.wait_recv()
        cp.wait_send()

    return pl.pallas_call(
        body, out_shape=jax.ShapeDtypeStruct(x.shape, x.dtype), in_specs=[_ANY], out_specs=_ANY,
        scratch_shapes=[pltpu.SemaphoreType.DMA, pltpu.SemaphoreType.DMA], name=name)(x)


def _row_tile(r):
    return _pick(r, (256, 128, 64, 32, 16, 8))


def add_own_half(name, g, recv, out_dtype):
    r, c = g.shape[2:]
    tr = _row_tile(r)
    mc = lax.axis_index("c").astype(jnp.int32).reshape(1)

    def body(c_ref, g_ref, r_ref, o_ref):
        o_ref[...] = (g_ref[...] + r_ref[...]).astype(o_ref.dtype)

    grid_spec = pltpu.PrefetchScalarGridSpec(
        num_scalar_prefetch=1, grid=(N_CHIPS, r // tr),
        in_specs=[pl.BlockSpec((None, None, tr, c), lambda j, i, cr: (j, cr[0], i, 0)),
                  pl.BlockSpec((None, tr, c), lambda j, i, cr: (j, i, 0))],
        out_specs=pl.BlockSpec((None, tr, c), lambda j, i, cr: (j, i, 0)))
    return pl.pallas_call(body, grid_spec=grid_spec, out_shape=jax.ShapeDtypeStruct(recv.shape, out_dtype),
                          compiler_params=_cparams("parallel", "parallel"), name=name)(mc, g, recv)


def cast_into_slot(name, w, chip, dtype):
    r, c = w.shape
    tr = _row_tile(r)

    def body(c_ref, w_ref, o_ref):
        o_ref[...] = w_ref[...].astype(o_ref.dtype)

    grid_spec = pltpu.PrefetchScalarGridSpec(
        num_scalar_prefetch=1, grid=(r // tr,),
        in_specs=[pl.BlockSpec((tr, c), lambda i, cr: (i, 0))],
        out_specs=pl.BlockSpec((None, tr, c), lambda i, cr: (cr[0], i, 0)))
    return pl.pallas_call(body, grid_spec=grid_spec, out_shape=jax.ShapeDtypeStruct((N_CHIPS, r, c), dtype),
                          compiler_params=_cparams("parallel"), name=name)(chip.astype(jnp.int32).reshape(1), w)


def sum_chips_into_half(name, own, arrived, chip, mc):
    r, c = own.shape[1:]
    tr = _row_tile(r)

    def body(s0, s1, s2, s3, s4, own_ref, a_ref, b_ref, d_ref, o_ref):
        o_ref[...] = ((own_ref[...].astype(f32) + a_ref[...].astype(f32))
                      + (b_ref[...].astype(f32) + d_ref[...].astype(f32)))

    slot = lambda k: pl.BlockSpec((None, tr, c), lambda i, *sc, _k=k: (sc[_k][0], i, 0))
    grid_spec = pltpu.PrefetchScalarGridSpec(
        num_scalar_prefetch=5, grid=(r // tr,), in_specs=[slot(0), slot(1), slot(2), slot(3)],
        out_specs=pl.BlockSpec((None, tr, c), lambda i, *sc: (sc[4][0], i, 0)))
    mx, my = lax.axis_index("x"), lax.axis_index("y")
    scal = [v.astype(jnp.int32).reshape(1) for v in
            (2 * mx + my, 2 * (1 - mx) + my, 2 * mx + (1 - my), 2 * (1 - mx) + (1 - my), mc)]
    return pl.pallas_call(body, grid_spec=grid_spec, out_shape=jax.ShapeDtypeStruct((2, r, c), f32),
                          compiler_params=_cparams("parallel"), name=name)(*scal, own, arrived, arrived, arrived)


def sum_slots(name, x):
    r, c = x.shape[1:]
    tr = _row_tile(r)

    def body(x_ref, o_ref):
        o_ref[...] = (x_ref[0].astype(f32) + x_ref[1].astype(f32)) + (x_ref[2].astype(f32) + x_ref[3].astype(f32))

    return pl.pallas_call(
        body, grid=(r // tr,), in_specs=[pl.BlockSpec((N_CHIPS, tr, c), lambda i: (0, i, 0))],
        out_specs=pl.BlockSpec((tr, c), lambda i: (i, 0)), out_shape=jax.ShapeDtypeStruct((r, c), f32),
        compiler_params=_cparams("parallel"), name=name)(x)


def adamw(name, w, g, m, v):
    r, c = w.shape
    tr = _row_tile(r)

    def body(w_ref, g_ref, m_ref, v_ref, d_ref, nm_ref, nv_ref):
        gv = g_ref[...]
        nm = ADAM_B1 * m_ref[...] + (1.0 - ADAM_B1) * gv
        nv = ADAM_B2 * v_ref[...] + (1.0 - ADAM_B2) * (gv * gv)
        m_hat = nm / (1.0 - ADAM_B1 ** ADAM_STEP)
        v_hat = nv / (1.0 - ADAM_B2 ** ADAM_STEP)
        d_ref[...] = -ADAM_LR * (m_hat / (jnp.sqrt(v_hat) + ADAM_EPS) + ADAM_WD * w_ref[...])
        nm_ref[...] = nm
        nv_ref[...] = nv

    blk = pl.BlockSpec((tr, c), lambda i: (i, 0))
    return pl.pallas_call(
        body, grid=(r // tr,), in_specs=[blk] * 4, out_specs=[blk] * 3,
        out_shape=[jax.ShapeDtypeStruct((r, c), f32)] * 3,
        compiler_params=_cparams("parallel"), name=name)(w, g, m, v)


def _pack_rows(n):
    return -(-n // PACK_COLS)


def _pack(arrs, dtype, row_multiple):
    segs = []
    for a in arrs:
        flat = a.astype(dtype).reshape(-1)
        k = _pack_rows(flat.shape[0])
        segs.append(jnp.pad(flat, (0, k * PACK_COLS - flat.shape[0])).reshape(k, PACK_COLS))
    rows = sum(s.shape[0] for s in segs)
    pad = -rows % row_multiple
    if pad:
        segs.append(jnp.zeros((pad, PACK_COLS), dtype))
    return jnp.concatenate(segs, 0)


def _unpack(packed, shapes):
    out, r = [], 0
    for shp in shapes:
        n = math.prod(shp)
        k = _pack_rows(n)
        out.append(packed[r:r + k].reshape(-1)[:n].reshape(shp))
        r += k
    return out


def _gathered_to_full(g, axis):
    t = jnp.moveaxis(g, 0, axis)
    return t.reshape(t.shape[:axis] + (t.shape[axis] * t.shape[axis + 1],) + t.shape[axis + 2:])


def _full_to_shard_major(full, axis):
    shp = full.shape
    t = full.reshape(shp[:axis] + (N_CHIPS, shp[axis] // N_CHIPS) + shp[axis + 1:])
    return jnp.moveaxis(t, axis, 0)


GRAD_ROW_MULTIPLE = 256


def kernel(x, mem, positions, hyb_w_in, dn_conv_w, dn_a_log, dn_dt_bias, dn_norm_g, hyb_w_out, s5_a_re, s5_a_im, s5_log_dt, s5_b_re, s5_b_im, s5_c_re, s5_c_im, s5_d, s5_glu_wo, s5_glu_wg, ln_mix_g, ln_mix_b, xq_w, xk_w, xv_w, xo_w, ln_x_g, ln_x_b, ffn_wg, ffn_wu, ffn_wd, ln_ffn_g, ln_ffn_b, loss_target, m_hyb_w_in, m_dn_conv_w, m_dn_a_log, m_dn_dt_bias, m_dn_norm_g, m_hyb_w_out, m_s5_a_re, m_s5_a_im, m_s5_log_dt, m_s5_b_re, m_s5_b_im, m_s5_c_re, m_s5_c_im, m_s5_d, m_s5_glu_wo, m_s5_glu_wg, m_ln_mix_g, m_ln_mix_b, m_xq_w, m_xk_w, m_xv_w, m_xo_w, m_ln_x_g, m_ln_x_b, m_ffn_wg, m_ffn_wu, m_ffn_wd, m_ln_ffn_g, m_ln_ffn_b, v_hyb_w_in, v_dn_conv_w, v_dn_a_log, v_dn_dt_bias, v_dn_norm_g, v_hyb_w_out, v_s5_a_re, v_s5_a_im, v_s5_log_dt, v_s5_b_re, v_s5_b_im, v_s5_c_re, v_s5_c_im, v_s5_d, v_s5_glu_wo, v_s5_glu_wg, v_ln_mix_g, v_ln_mix_b, v_xq_w, v_xk_w, v_xv_w, v_xo_w, v_ln_x_g, v_ln_x_b, v_ffn_wg, v_ffn_wu, v_ffn_wd, v_ln_ffn_g, v_ln_ffn_b):
    a = dict(locals())
    big = [n for n in WEIGHT_NAMES if n in SHARD_AXIS and n not in GATHER_F32]
    small = [n for n in WEIGHT_NAMES if n not in big]
    chip = 2 * lax.axis_index("x") + lax.axis_index("y")
    for n in FFN_TRANSPOSED:
        for pre in ("", "m_", "v_"):
            a[pre + n] = jnp.swapaxes(a[pre + n], 1, 2)

    mc = lax.axis_index("c")
    view2 = lambda t: t.reshape(-1, t.shape[-1])
    slots = [cast_into_slot("slot_" + n, view2(a[n]), chip, bf16).reshape((N_CHIPS,) + a[n].shape) for n in big]
    gathered = comm_gather_weights("gather_w", slots)
    tiny4 = _unpack_slots(comm_allgather4("gather_w_tiny", _pack([a[n] for n in GATHER_F32], f32, 8)),
                          [a[n].shape for n in GATHER_F32])
    p = {n: a[n] for n in small if n not in GATHER_F32}
    for n, g4 in zip(GATHER_F32, tiny4):
        p[n] = _gathered_to_full(g4, SHARD_AXIS[n])
    for n, g4 in zip(big, gathered):
        p[n] = g4 if n in SHARD_ORDER_GRADS else _gathered_to_full(g4, SHARD_AXIS[n])

    loss, grad_x, grads = local_step(x[0], mem[0], positions, loss_target[0], p)
    loss = lax.psum(loss, ("x", "y", "c"))

    g4s = [grads[n] if n in SHARD_ORDER_GRADS else _full_to_shard_major(grads[n], SHARD_AXIS[n]) for n in big]
    recv = comm_sibling_halves("rs_sibling_halves", g4s)
    pairs = []
    for n, g4, r4 in zip(big, g4s, recv):
        lh, cols = g4.shape[1] // 2, g4.shape[-1]
        v4 = g4.reshape(N_CHIPS, 2, -1, cols)
        pairs.append(add_own_half("rs_add_" + n, v4, r4.reshape(N_CHIPS, -1, cols), bf16).reshape((N_CHIPS, lh) + g4.shape[2:]))
    arrived = comm_alltoall4("rs_alltoall", pairs)
    slot3 = lambda t: t.reshape(N_CHIPS, -1, t.shape[-1])
    halves = [sum_chips_into_half("rs_sum_" + n, slot3(pr), slot3(ar), chip, mc) for n, pr, ar in zip(big, pairs, arrived)]
    g_big = {n: t.reshape(a[n].shape) for n, t in zip(big, comm_sibling_join("rs_sibling_join", halves))}

    rpack = _pack([grads[n] for n in small], f32, 8)
    rpair = _add2("ar_add_sibling", rpack, comm_sibling_swap("ar_sibling_swap", rpack))
    g_small = _unpack(sum_slots("ar_sum_chips", comm_allgather4("ar_allgather", rpair)), [grads[n].shape for n in small])
    g_small = {n: (lax.dynamic_index_in_dim(_full_to_shard_major(g, SHARD_AXIS[n]), chip, 0, keepdims=False)
                   if n in SHARD_AXIS else g) for n, g in zip(small, g_small)}

    outs = {}
    for n in big:
        view = lambda t: t.reshape(-1, t.shape[-1])
        d, nm, nv = adamw("adamw_" + n, view(a[n]), view(g_big[n]), view(a["m_" + n]), view(a["v_" + n]))
        outs[n] = (g_big[n],) + tuple(t.reshape(a[n].shape) for t in (d, nm, nv))
    shapes = [a[n].shape for n in small]
    packs = [_pack([a[pre + n] for n in small], f32, 8) for pre in ("", "m_", "v_")]
    upd = adamw("adamw_small", packs[0], _pack([g_small[n] for n in small], f32, 8), packs[1], packs[2])
    for k, n in enumerate(small):
        outs[n] = (g_small[n],) + tuple(_unpack(buf, shapes)[k] for buf in upd)
    for n in FFN_TRANSPOSED:
        outs[n] = tuple(jnp.swapaxes(t, 1, 2) for t in outs[n])
    res = [loss, grad_x[None]]
    for kind in range(4):
        res += [outs[n][kind] for n in WEIGHT_NAMES]
    return tuple(res)


def _unpack_slots(gathered, shapes):
    out, r = [], 0
    for shp in shapes:
        n = math.prod(shp)
        k = _pack_rows(n)
        out.append(gathered[:, r:r + k].reshape(N_CHIPS, -1)[:, :n].reshape((N_CHIPS,) + tuple(shp)))
        r += k
    return out
```

```python
import functools
import math

import jax
import jax.numpy as jnp
from jax import lax
from jax.experimental import pallas as pl
from jax.experimental.pallas import tpu as pltpu

f32 = jnp.float32
bf16 = jnp.bfloat16

D_MODEL = 1024
DEPTH = 4
DN_HEADS = 4
DN_HEAD_DIM = 128
DN_KEY_DIM = 512
DN_QKV_DIM = 1536
DN_CONV = 4
SW_HEADS = 8
SW_HEAD_DIM = 64
SW_DIM = 512
SW_DILATIONS = (1, 4, 16)
SW_BLOCK = 128
ROPE_THETA = 10000.0
S5_GROUP = 16
S5_GROUPS = 64
S5_STATE = 64
X_HEADS = 4
X_HEAD_DIM = 256
FFN_HIDDEN = 2816
ALPHA = (2 * DEPTH) ** 0.25
LN_EPS = 1e-5
RMS_EPS = 1e-6
ADAM_LR, ADAM_B1, ADAM_B2, ADAM_EPS, ADAM_WD, ADAM_STEP = 0.001, 0.9, 0.999, 1e-08, 0.01, 10

BA_PAD = 256
PROJ_COLS = DN_QKV_DIM + DN_KEY_DIM + 3 * SW_DIM + BA_PAD
COL_Z = DN_QKV_DIM
COL_SWQ = COL_Z + DN_KEY_DIM
COL_SWK = COL_SWQ + SW_DIM
COL_SWV = COL_SWK + SW_DIM
COL_BA = COL_SWV + SW_DIM

LANES = 128
SUBLANES = 8
VMEM_LIMIT = 56 * 1024 * 1024
DN_CHUNK = 128
DN_HEADS_PER_STEP = 4


def _cparams(*sem):
    return pltpu.CompilerParams(dimension_semantics=tuple(sem), vmem_limit_bytes=VMEM_LIMIT)


def _dg(x, y, cx, cy):
    return lax.dot_general(x, y, (((cx,), (cy,)), ((), ())), preferred_element_type=f32)


@functools.partial(jax.custom_vjp, nondiff_argnums=(2, 3))
def bdot(a, b, ca, cb):
    return _dg(a.astype(bf16), b.astype(bf16), ca, cb)


def _bdot_fwd(a, b, ca, cb):
    return bdot(a, b, ca, cb), (a, b)


def _bdot_bwd(ca, cb, res, g):
    a, b = res
    g16, a16, b16 = g.astype(bf16), a.astype(bf16), b.astype(bf16)
    da = _dg(g16, b16, 1, 1 - cb) if ca == 1 else _dg(b16, g16, 1 - cb, 1)
    db = _dg(a16, g16, 1 - ca, 0) if cb == 0 else _dg(g16, a16, 0, 1 - ca)
    return da.astype(a.dtype), db.astype(b.dtype)


bdot.defvjp(_bdot_fwd, _bdot_bwd)


def _split_hi_lo(a):
    hi = a.astype(bf16)
    return hi, (a - hi.astype(f32)).astype(bf16)


def _dot3(a, b, ca, cb):
    a_hi, a_lo = _split_hi_lo(a)
    b_hi, b_lo = _split_hi_lo(b)
    return _dg(a_hi, b_hi, ca, cb) + (_dg(a_hi, b_lo, ca, cb) + _dg(a_lo, b_hi, ca, cb))


@jax.custom_vjp
def hdot3(a, b):
    return _dot3(a, b, 1, 0)


def _hdot3_fwd(a, b):
    return hdot3(a, b), (a, b)


def _hdot3_bwd(res, g):
    a, b = res
    return _dot3(g, b, 1, 1), _dot3(a, g, 0, 0)


hdot3.defvjp(_hdot3_fwd, _hdot3_bwd)


def hdot(a, b):
    return jnp.dot(a, b, precision=lax.Precision.HIGHEST, preferred_element_type=f32)


def _iota2(shape, dim):
    return lax.broadcasted_iota(jnp.int32, shape, dim)


def _row_spec(r, tm):
    if isinstance(r, tuple):
        arr, width, blk = r
        return arr, pl.BlockSpec((tm, width), lambda i, _b=blk: (i, _b))
    return r, pl.BlockSpec((tm, r.shape[1]), lambda i: (i, 0))


def _par_spec(p):
    return pl.BlockSpec(p.shape, lambda i, _n=p.ndim: (0,) * _n)


def rowmap(name, fn, rows, params, out_cols, tm, out_dtypes=None):
    arrs, specs = zip(*[_row_spec(r, tm) for r in rows])
    s = arrs[0].shape[0]
    n_in = len(rows) + len(params)
    out_dtypes = out_dtypes or [f32] * len(out_cols)

    def body(*refs):
        outs = fn(*[r[...] for r in refs[:n_in]])
        for o_ref, o in zip(refs[n_in:], outs):
            o_ref[...] = o.astype(o_ref.dtype)

    return pl.pallas_call(
        body, grid=(s // tm,),
        in_specs=list(specs) + [_par_spec(p) for p in params],
        out_specs=[pl.BlockSpec((tm, c), lambda i: (i, 0)) for c in out_cols],
        out_shape=[jax.ShapeDtypeStruct((s, c), dt) for c, dt in zip(out_cols, out_dtypes)],
        compiler_params=_cparams("parallel"), name=name)(*arrs, *params)


def rowmap_bwd(name, fn, rows, params, cts, tm, row_mask=None, par_mask=None, row_dtypes=None):
    arrs, specs = zip(*[_row_spec(r, tm) for r in rows])
    s = arrs[0].shape[0]
    ct_groups = [c if isinstance(c, list) else [c] for c in cts]
    ct_arrs, ct_specs = zip(*[_row_spec(a, tm) for grp in ct_groups for a in grp])
    cts = list(ct_arrs)
    nr, npar, nct = len(rows), len(params), len(cts)
    row_mask = row_mask or [True] * nr
    par_mask = par_mask or [True] * npar
    row_idx = [k for k in range(nr) if row_mask[k]]
    par_idx = [k for k in range(npar) if par_mask[k]]
    row_w = [specs[k].block_shape[1] for k in row_idx]

    def body(*refs):
        ins = [r[...] for r in refs[:nr + npar]]
        ct_refs = list(refs[nr + npar:nr + npar + nct])
        ctv = []
        for grp in ct_groups:
            acc = ct_refs.pop(0)[...].astype(f32)
            for _ in grp[1:]:
                acc = acc + ct_refs.pop(0)[...].astype(f32)
            ctv.append(acc)
        ctv = tuple(ctv)
        outs = refs[nr + npar + nct:]
        _, vjp = jax.vjp(fn, *ins)
        grads = vjp(ctv)
        for o_ref, k in zip(outs[:len(row_idx)], row_idx):
            o_ref[...] = grads[k].astype(o_ref.dtype)
        first = pl.program_id(0) == 0
        for o_ref, k in zip(outs[len(row_idx):], par_idx):
            g = grads[nr + k].astype(f32)

            @pl.when(first)
            def _(o_ref=o_ref, g=g):
                o_ref[...] = g

            @pl.when(jnp.logical_not(first))
            def _(o_ref=o_ref, g=g):
                o_ref[...] += g

    res = pl.pallas_call(
        body, grid=(s // tm,),
        in_specs=list(specs) + [_par_spec(p) for p in params]
        + list(ct_specs),
        out_specs=[pl.BlockSpec((tm, w), lambda i: (i, 0)) for w in row_w]
        + [_par_spec(params[k]) for k in par_idx],
        out_shape=[jax.ShapeDtypeStruct((s, w), dt) for w, dt in zip(row_w, row_dtypes or [f32] * len(row_w))]
        + [jax.ShapeDtypeStruct(params[k].shape, f32) for k in par_idx],
        compiler_params=_cparams("arbitrary"), name=name)(*arrs, *params, *cts)
    return list(res[:len(row_idx)]), list(res[len(row_idx):])


def _pick(n, prefs):
    for t in prefs:
        if n % t == 0:
            return t
    return n


MM_CHUNK = 512


def mm_nn(name, a, b, out_dtype=f32):
    m, k = a.shape
    n = b.shape[1]
    tm = _pick(m, (512, 256, 128))
    cn = _pick(n, (MM_CHUNK, 256, 128))

    def body(a_ref, b_ref, o_ref):
        av = a_ref[...].astype(bf16)
        for c in range(n // cn):
            sl = slice(c * cn, (c + 1) * cn)
            o_ref[:, sl] = _dg(av, b_ref[:, sl].astype(bf16), 1, 0).astype(o_ref.dtype)

    return pl.pallas_call(
        body, grid=(m // tm,),
        in_specs=[pl.BlockSpec((tm, k), lambda i: (i, 0)), pl.BlockSpec((k, n), lambda i: (0, 0))],
        out_specs=pl.BlockSpec((tm, n), lambda i: (i, 0)),
        out_shape=jax.ShapeDtypeStruct((m, n), out_dtype),
        compiler_params=_cparams("parallel"), name=name)(a, b)


def mm_nt(name, a, b, out_dtype=f32):
    m, n = a.shape
    k = b.shape[0]
    tm = _pick(m, (512, 256, 128))
    ck = _pick(k, (MM_CHUNK, 256, 128))

    def body(a_ref, b_ref, o_ref):
        av = a_ref[...].astype(bf16)
        for c in range(k // ck):
            sl = slice(c * ck, (c + 1) * ck)
            o_ref[:, sl] = _dg(av, b_ref[sl, :].astype(bf16), 1, 1).astype(o_ref.dtype)

    return pl.pallas_call(
        body, grid=(m // tm,),
        in_specs=[pl.BlockSpec((tm, n), lambda i: (i, 0)), pl.BlockSpec((k, n), lambda i: (0, 0))],
        out_specs=pl.BlockSpec((tm, k), lambda i: (i, 0)),
        out_shape=jax.ShapeDtypeStruct((m, k), out_dtype),
        compiler_params=_cparams("parallel"), name=name)(a, b)


def mm_tn(name, a, b, out_dtype=f32):
    s, m = a.shape
    n = b.shape[1]
    tn = _pick(n, (256, 128))
    cm = _pick(m, (256, 128))

    def body(a_ref, b_ref, o_ref):
        bv = b_ref[...].astype(bf16)
        for c in range(m // cm):
            sl = slice(c * cm, (c + 1) * cm)
            o_ref[sl, :] = _dg(a_ref[:, sl].astype(bf16), bv, 0, 0).astype(o_ref.dtype)

    return pl.pallas_call(
        body, grid=(n // tn,),
        in_specs=[pl.BlockSpec((s, m), lambda j: (0, 0)), pl.BlockSpec((s, tn), lambda j: (0, j))],
        out_specs=pl.BlockSpec((m, tn), lambda j: (0, j)),
        out_shape=jax.ShapeDtypeStruct((m, n), out_dtype),
        compiler_params=_cparams("parallel"), name=name)(a, b)


def _postnorm_tile(h, sub, g, b):
    z = ALPHA * h + sub
    mu = jnp.mean(z, -1, keepdims=True)
    zc = z - mu
    var = jnp.mean(zc * zc, -1, keepdims=True)
    return (zc * lax.rsqrt(var + LN_EPS) * g + b,)


def _swiglu_tile(au):
    a, u = au[:, :FFN_HIDDEN], au[:, FFN_HIDDEN:]
    return (jax.nn.silu(a) * u,)


def _glu_tile(og):
    o, g = og[:, :D_MODEL], og[:, D_MODEL:]
    return (o * jax.nn.sigmoid(g),)


def _xattn_tile(q, kv):
    outs = []
    for h in range(X_HEADS):
        sl = slice(h * X_HEAD_DIM, (h + 1) * X_HEAD_DIM)
        s = bdot(q[:, sl], kv[:, sl], 1, 1) * (X_HEAD_DIM ** -0.5)
        m = lax.stop_gradient(jnp.max(s, -1, keepdims=True))
        p = jnp.exp(s - m)
        p = p / jnp.sum(p, -1, keepdims=True)
        outs.append(bdot(p, kv[:, D_MODEL + h * X_HEAD_DIM:D_MODEL + (h + 1) * X_HEAD_DIM], 1, 0))
    return (jnp.concatenate(outs, -1),)


TM_ROW = 256


def postnorm_fwd(tag, h, sub, g, b):
    return rowmap("postnorm_" + tag, lambda *a: _postnorm_tile(*a) * 2, [h, sub], [g, b], [D_MODEL] * 2, TM_ROW,
                  out_dtypes=[f32, bf16])


def postnorm_bwd(tag, h, sub, g, b, dy):
    (dh, dsub), (dg, db) = rowmap_bwd("postnorm_bwd_" + tag, _postnorm_tile, [h, sub], [g, b], [dy], TM_ROW,
                                      row_dtypes=[f32, bf16])
    return dh, dsub, dg, db


def xattn_fwd(tag, h, mem, wq, wkv, wo):
    q = mm_nn("xq_" + tag, h, wq, out_dtype=bf16)
    kv = mm_nn("xkv_" + tag, mem, wkv)
    ao = rowmap("xattn_" + tag, _xattn_tile, [q], [kv], [D_MODEL], TM_ROW, out_dtypes=[bf16])[0]
    out = mm_nn("xo_" + tag, ao, wo)
    return out, (q, kv, ao)


def xattn_bwd(tag, h, mem, wq, wkv, wo, res, dout):
    q, kv, ao = res
    dwo = mm_tn("xo_dw_" + tag, ao, dout)
    dao = mm_nt("xo_dx_" + tag, dout, wo)
    (dq,), (dkv,) = rowmap_bwd("xattn_bwd_" + tag, _xattn_tile, [q], [kv], [dao], TM_ROW, row_dtypes=[bf16])
    dwq = mm_tn("xq_dw_" + tag, h, dq)
    dh = mm_nt("xq_dx_" + tag, dq, wq)
    dwkv = mm_tn("xkv_dw_" + tag, mem, dkv)
    return dh, dwq, dwkv, dwo


FFN_SHARD = FFN_HIDDEN // 4
TM_FFN = 512


def _silu_mul(a, u):
    return jax.nn.silu(a) * u


def ffn_fwd(tag, layer, h, wg, wu, wd):
    s = h.shape[0]
    tm, fs = TM_FFN, FFN_SHARD
    w_in = pl.BlockSpec((None, None, fs, D_MODEL), lambda k, i: (k, layer, 0, 0))
    act = pl.BlockSpec((None, tm, fs), lambda k, i: (k, i, 0))

    def up_body(h_ref, wg_ref, wu_ref, a_ref, u_ref, hid_ref):
        hv = h_ref[...].astype(bf16)
        a, u = _dg(hv, wg_ref[...], 1, 1), _dg(hv, wu_ref[...], 1, 1)
        a_ref[...], u_ref[...] = a.astype(bf16), u.astype(bf16)
        hid_ref[...] = _silu_mul(a, u).astype(bf16)

    a4, u4, hid4 = pl.pallas_call(
        up_body, grid=(4, s // tm),
        in_specs=[pl.BlockSpec((tm, D_MODEL), lambda k, i: (i, 0)), w_in, w_in],
        out_specs=[act, act, act],
        out_shape=[jax.ShapeDtypeStruct((4, s, fs), bf16)] * 3,
        compiler_params=_cparams("parallel", "parallel"), name="ffn_up_" + tag)(h, wg, wu)

    all_act = pl.BlockSpec((4, tm, fs), lambda i: (0, i, 0))
    all_w = pl.BlockSpec((4, None, fs, D_MODEL), lambda i: (0, layer, 0, 0))

    def down_body(hid_ref, wd_ref, o_ref):
        acc = _dg(hid_ref[0], wd_ref[0], 1, 0)
        for k in range(1, 4):
            acc = acc + _dg(hid_ref[k], wd_ref[k], 1, 0)
        o_ref[...] = acc

    out = pl.pallas_call(
        down_body, grid=(s // tm,), in_specs=[all_act, all_w],
        out_specs=pl.BlockSpec((tm, D_MODEL), lambda i: (i, 0)),
        out_shape=jax.ShapeDtypeStruct((s, D_MODEL), f32),
        compiler_params=_cparams("parallel"), name="ffn_down_" + tag)(hid4, wd)
    return out, (a4, u4, hid4)


def ffn_bwd(tag, layer, h, wg, wu, wd, res, dout):
    a4, u4, hid4 = res
    s = h.shape[0]
    tm, fs = TM_FFN, FFN_SHARD
    act = pl.BlockSpec((None, tm, fs), lambda k, i: (k, i, 0))

    def dact_body(do_ref, wd_ref, a_ref, u_ref, da_ref, du_ref):
        dhid = _dg(do_ref[...].astype(bf16), wd_ref[...], 1, 1)
        _, vjp = jax.vjp(_silu_mul, a_ref[...].astype(f32), u_ref[...].astype(f32))
        da, du = vjp(dhid)
        da_ref[...], du_ref[...] = da.astype(bf16), du.astype(bf16)

    da4, du4 = pl.pallas_call(
        dact_body, grid=(4, s // tm),
        in_specs=[pl.BlockSpec((tm, D_MODEL), lambda k, i: (i, 0)),
                  pl.BlockSpec((None, None, fs, D_MODEL), lambda k, i: (k, layer, 0, 0)), act, act],
        out_specs=[act, act], out_shape=[jax.ShapeDtypeStruct((4, s, fs), bf16)] * 2,
        compiler_params=_cparams("parallel", "parallel"), name="ffn_dact_" + tag)(dout, wd, a4, u4)

    all_act = pl.BlockSpec((4, tm, fs), lambda i: (0, i, 0))
    all_w = pl.BlockSpec((4, None, fs, D_MODEL), lambda i: (0, layer, 0, 0))

    def dx_body(da_ref, du_ref, wg_ref, wu_ref, o_ref):
        acc = _dg(da_ref[0], wg_ref[0], 1, 0) + _dg(du_ref[0], wu_ref[0], 1, 0)
        for k in range(1, 4):
            acc = acc + (_dg(da_ref[k], wg_ref[k], 1, 0) + _dg(du_ref[k], wu_ref[k], 1, 0))
        o_ref[...] = acc

    dh = pl.pallas_call(
        dx_body, grid=(s // tm,), in_specs=[all_act, all_act, all_w, all_w],
        out_specs=pl.BlockSpec((tm, D_MODEL), lambda i: (i, 0)),
        out_shape=jax.ShapeDtypeStruct((s, D_MODEL), f32),
        compiler_params=_cparams("parallel"), name="ffn_dx_" + tag)(da4, du4, wg, wu)

    tn = 256
    whole = pl.BlockSpec((None, s, fs), lambda k, j: (k, 0, 0))

    def dwin_body(h_ref, da_ref, du_ref, dwg_ref, dwu_ref):
        hv = h_ref[...].astype(bf16)
        dwg_ref[...] = _dg(da_ref[...], hv, 0, 0)
        dwu_ref[...] = _dg(du_ref[...], hv, 0, 0)

    dwg, dwu = pl.pallas_call(
        dwin_body, grid=(4, D_MODEL // tn),
        in_specs=[pl.BlockSpec((s, tn), lambda k, j: (0, j)), whole, whole],
        out_specs=[pl.BlockSpec((None, fs, tn), lambda k, j: (k, 0, j))] * 2,
        out_shape=[jax.ShapeDtypeStruct((4, fs, D_MODEL), f32)] * 2,
        compiler_params=_cparams("parallel", "parallel"), name="ffn_dwin_" + tag)(h, da4, du4)

    def dwd_body(hid_ref, do_ref, dwd_ref):
        dwd_ref[...] = _dg(hid_ref[...], do_ref[...].astype(bf16), 0, 0)

    dwd = pl.pallas_call(
        dwd_body, grid=(4, D_MODEL // tn),
        in_specs=[whole, pl.BlockSpec((s, tn), lambda k, j: (0, j))],
        out_specs=pl.BlockSpec((None, fs, tn), lambda k, j: (k, 0, j)),
        out_shape=jax.ShapeDtypeStruct((4, fs, D_MODEL), f32),
        compiler_params=_cparams("parallel", "parallel"), name="ffn_dwd_" + tag)(hid4, dout)
    return dh, dwg, dwu, dwd


def loss_head(y, target):
    s, d = y.shape
    tm = TM_ROW

    def body(y_ref, t_ref, part_ref, dy_ref):
        e = y_ref[...] - t_ref[...]
        dy_ref[...] = e * (1.0 / d)
        p = jnp.sum(e * e, 0, keepdims=True) * (0.5 / d)

        @pl.when(pl.program_id(0) == 0)
        def _():
            part_ref[...] = p

        @pl.when(pl.program_id(0) != 0)
        def _():
            part_ref[...] += p

    return pl.pallas_call(
        body, grid=(s // tm,),
        in_specs=[pl.BlockSpec((tm, d), lambda i: (i, 0))] * 2,
        out_specs=[pl.BlockSpec((1, d), lambda i: (0, 0)), pl.BlockSpec((tm, d), lambda i: (i, 0))],
        out_shape=[jax.ShapeDtypeStruct((1, d), f32), jax.ShapeDtypeStruct((s, d), f32)],
        compiler_params=_cparams("arbitrary"), name="loss_head")(y, target)


TM_CONV = 512


def _conv_rows(xx, w_ref, n_rows):
    a = w_ref[3:4, :] * xx
    for k in (1, 2, 3):
        a = a + w_ref[3 - k:4 - k, :] * pltpu.roll(xx, k, 0)
    return a


def _dn_act(a, is_qk):
    s = jax.nn.silu(a)
    n = s * lax.rsqrt(jnp.sum(s * s, -1, keepdims=True) + RMS_EPS)
    return jnp.where(is_qk, n, s)


def dn_conv_fwd(tag, proj, cw):
    s = proj.shape[0]
    tm, hb = TM_CONV, TM_CONV // SUBLANES

    def body(xh_ref, x_ref, w_ref, o_ref):
        j, t = pl.program_id(0), pl.program_id(1)
        halo = jnp.where(t > 0, xh_ref[...], 0.0)
        xx = jnp.concatenate([halo, x_ref[...]], 0)
        a = _conv_rows(xx, w_ref, tm + SUBLANES)
        o_ref[...] = _dn_act(a, j < 2 * DN_HEADS)[SUBLANES:, :]

    return pl.pallas_call(
        body, grid=(DN_QKV_DIM // LANES, s // tm),
        in_specs=[pl.BlockSpec((SUBLANES, LANES), lambda j, t: (jnp.maximum(t * hb - 1, 0), j)),
                  pl.BlockSpec((tm, LANES), lambda j, t: (t, j)),
                  pl.BlockSpec((DN_CONV, LANES), lambda j, t: (0, j))],
        out_specs=pl.BlockSpec((tm, LANES), lambda j, t: (t, j)),
        out_shape=jax.ShapeDtypeStruct((s, DN_QKV_DIM), f32),
        compiler_params=_cparams("parallel", "parallel"), name="dn_conv_" + tag)(proj, proj, cw)


def dn_conv_bwd(tag, proj, cw, dy):
    s = proj.shape[0]
    tm, hb = TM_CONV, TM_CONV // SUBLANES
    nt = s // tm
    n_ext = tm + 2 * SUBLANES

    def body(xb_ref, x_ref, xa_ref, dy_ref, dya_ref, w_ref, dx_ref, dw_ref):
        j, t = pl.program_id(0), pl.program_id(1)
        xx = jnp.concatenate([jnp.where(t > 0, xb_ref[...], 0.0), x_ref[...],
                              jnp.where(t < nt - 1, xa_ref[...], 0.0)], 0)
        dyy = jnp.concatenate([jnp.zeros((SUBLANES, LANES), f32), dy_ref[...],
                               jnp.where(t < nt - 1, dya_ref[...], 0.0)], 0)
        a = _conv_rows(xx, w_ref, n_ext)
        _, vjp = jax.vjp(lambda v: _dn_act(v, j < 2 * DN_HEADS), a)
        da, = vjp(dyy)
        dx = w_ref[3:4, :] * da
        for k in (1, 2, 3):
            dx = dx + w_ref[3 - k:4 - k, :] * pltpu.roll(da, n_ext - k, 0)
        dx_ref[...] = dx[SUBLANES:SUBLANES + tm, :]
        row = _iota2((n_ext, LANES), 0)
        da_in = jnp.where((row >= SUBLANES) & (row < SUBLANES + tm), da, 0.0)
        r8 = _iota2((SUBLANES, LANES), 0)
        dw = jnp.zeros((SUBLANES, LANES), f32)
        for k in range(DN_CONV):
            xs = xx if k == 0 else pltpu.roll(xx, k, 0)
            dw = dw + jnp.where(r8 == 3 - k, jnp.sum(da_in * xs, 0, keepdims=True), 0.0)

        @pl.when(t == 0)
        def _():
            dw_ref[...] = dw

        @pl.when(t != 0)
        def _():
            dw_ref[...] += dw

    nb8 = s // SUBLANES
    return pl.pallas_call(
        body, grid=(DN_QKV_DIM // LANES, nt),
        in_specs=[pl.BlockSpec((SUBLANES, LANES), lambda j, t: (jnp.maximum(t * hb - 1, 0), j)),
                  pl.BlockSpec((tm, LANES), lambda j, t: (t, j)),
                  pl.BlockSpec((SUBLANES, LANES), lambda j, t: (jnp.minimum((t + 1) * hb, nb8 - 1), j)),
                  pl.BlockSpec((tm, LANES), lambda j, t: (t, j)),
                  pl.BlockSpec((SUBLANES, LANES), lambda j, t: (jnp.minimum((t + 1) * hb, nb8 - 1), j)),
                  pl.BlockSpec((DN_CONV, LANES), lambda j, t: (0, j))],
        out_specs=[pl.BlockSpec((tm, LANES), lambda j, t: (t, j)),
                   pl.BlockSpec((SUBLANES, LANES), lambda j, t: (0, j))],
        out_shape=[jax.ShapeDtypeStruct((s, DN_QKV_DIM), f32), jax.ShapeDtypeStruct((SUBLANES, DN_QKV_DIM), f32)],
        compiler_params=_cparams("parallel", "arbitrary"), name="dn_conv_bwd_" + tag)(proj, proj, proj, dy, dy, cw)


def _gate_tile(ba, eb, ea, alog, dtb):
    beta = jax.nn.sigmoid(hdot(ba, eb))
    g = -jnp.exp(alog) * jax.nn.softplus(hdot(ba, ea) + dtb)
    return beta, g


def _each(fn, *lists):
    return [fn(*args) for args in zip(*lists)]


@functools.partial(jax.custom_vjp, nondiff_argnums=(1,))
def _halves(x, axis):
    h = x.shape[axis] // 2
    return (x[:h], x[h:]) if axis == 0 else (x[:, :h], x[:, h:])


def _halves_fwd(x, axis):
    return _halves(x, axis), None


def _halves_bwd(axis, _, g):
    return (jnp.concatenate(g, axis),)


_halves.defvjp(_halves_fwd, _halves_bwd)


def _tri_inv_unit(lowers):
    c = lowers[0].shape[0]
    r, col = _iota2((c, c), 0), _iota2((c, c), 1)
    eye = jnp.where(r == col, 1.0, 0.0).astype(f32)
    invs = None
    sh = 0
    while (1 << sh) < c:
        same_2b = lax.shift_right_logical(r, sh + 1) == lax.shift_right_logical(col, sh + 1)
        diff_b = lax.shift_right_logical(r, sh) != lax.shift_right_logical(col, sh)
        offs = [jnp.where(same_2b & diff_b, low, 0.0) for low in lowers]
        if invs is None:
            invs = [eye - off for off in offs]
        else:
            part = _each(hdot, invs, offs)
            invs = _each(lambda inv, p: inv - hdot(p, inv), invs, part)
        sh += 1
    return invs


@jax.custom_vjp
def _known_inverse(lower, tinv):
    return tinv


def _known_inverse_fwd(lower, tinv):
    return tinv, tinv


def _known_inverse_bwd(tinv, g):
    tt = tinv.T
    return -hdot(hdot(tt, g), tt), jnp.zeros_like(tinv)


_known_inverse.defvjp(_known_inverse_fwd, _known_inverse_bwd)


def _delta_chunk(q, k, v, gb, betab, state, tinv_known=None):
    c, hd = DN_CHUNK, DN_HEAD_DIM
    r, col = _iota2((c, c), 0), _iota2((c, c), 1)
    causal, strict = r >= col, r > col
    tril = jnp.where(causal, 1.0, 0.0).astype(f32)
    gc = _each(lambda g: hdot(tril, g), gb)
    decay = _each(lambda g: jnp.where(causal, jnp.exp(jnp.where(causal, g - g.T, 0.0)), 0.0), gc)
    qs = _each(lambda t: t * (DN_HEAD_DIM ** -0.5), q)
    kb = _each(lambda a, b: a * b, k, betab)
    kq = _each(lambda a, b, kk: _halves(bdot(jnp.concatenate([a, b], 0), kk, 1, 1), 0), kb, qs, k)
    lower = _each(lambda x, d: jnp.where(strict, x[0], 0.0) * d, kq, decay)
    intra = _each(lambda x, d: x[1] * d, kq, decay)
    tinv = _tri_inv_unit(lower) if tinv_known is None else _each(_known_inverse, lower, tinv_known)
    eg = _each(jnp.exp, gc)
    uw = _each(lambda t, vv, b, kb_, e: _halves(hdot(t, jnp.concatenate([vv * b, kb_ * e], 1)), 1),
               tinv, v, betab, kb, eg)
    gl = _each(lambda g: jnp.sum(jnp.where(r == c - 1, g, 0.0), 0, keepdims=True), gc)
    k_dec = _each(lambda kk, a, g: kk * jnp.exp(a - g), k, gl, gc)
    ws = _each(lambda x, t, e, st: _halves(bdot(jnp.concatenate([x[1], t * e], 0), st, 1, 0), 0), uw, qs, eg, state)
    v_new = _each(lambda x, y: x[0] - y[0], uw, ws)
    out = _each(lambda y, a, vn: y[1] + bdot(a, vn, 1, 0), ws, intra, v_new)
    new_state = _each(lambda st, a, kd, vn: st * jnp.exp(a) + bdot(kd, vn, 0, 0), state, gl, k_dec, v_new)
    return tuple(out), tuple(new_state), tuple(tinv)


def delta_fwd(tag, qkv, gb, betab):
    s = qkv.shape[0]
    c, hd = DN_CHUNK, DN_HEAD_DIM
    n = s // c

    hg, ng = DN_HEADS_PER_STEP, DN_HEADS // DN_HEADS_PER_STEP

    def body(q_ref, k_ref, v_ref, g_ref, b_ref, o_ref, st_ref, ti_ref, state):
        @pl.when(pl.program_id(1) == 0)
        def _():
            state[...] = jnp.zeros_like(state)

        heads = lambda ref: tuple(ref[:, j * hd:(j + 1) * hd] for j in range(hg))
        st = tuple(state[j] for j in range(hg))
        outs, news, tinv = _delta_chunk(heads(q_ref), heads(k_ref), heads(v_ref), heads(g_ref), heads(b_ref), st)
        for j in range(hg):
            st_ref[j] = st[j]
            ti_ref[j] = tinv[j]
            o_ref[:, j * hd:(j + 1) * hd] = outs[j]
            state[j] = news[j]

    blk = lambda off: pl.BlockSpec((c, hg * hd), lambda h, i, _o=off: (i, h + _o))
    per_chunk = pl.BlockSpec((hg, None, hd, hd), lambda h, i: (h, i, 0, 0))
    return pl.pallas_call(
        body, grid=(ng, n),
        in_specs=[blk(0), blk(ng), blk(2 * ng), blk(0), blk(0)],
        out_specs=[blk(0), per_chunk, per_chunk],
        out_shape=[jax.ShapeDtypeStruct((s, DN_KEY_DIM), f32)] + [jax.ShapeDtypeStruct((DN_HEADS, n, hd, hd), f32)] * 2,
        scratch_shapes=[pltpu.VMEM((hg, hd, hd), f32)],
        compiler_params=_cparams("parallel", "arbitrary"), name="delta_" + tag)(qkv, qkv, qkv, gb, betab)


def delta_bwd(tag, qkv, gb, betab, states, tinvs, do):
    s = qkv.shape[0]
    c, hd = DN_CHUNK, DN_HEAD_DIM
    n = s // c

    hg, ng = DN_HEADS_PER_STEP, DN_HEADS // DN_HEADS_PER_STEP

    def body(q_ref, k_ref, v_ref, g_ref, b_ref, st_ref, ti_ref, do_ref, dq_ref, dk_ref, dv_ref, dg_ref, db_ref, dstate):
        @pl.when(pl.program_id(1) == 0)
        def _():
            dstate[...] = jnp.zeros_like(dstate)

        heads = lambda ref: tuple(ref[:, j * hd:(j + 1) * hd] for j in range(hg))
        tinv = tuple(ti_ref[j] for j in range(hg))
        _, vjp = jax.vjp(lambda *args: _delta_chunk(*args, tinv_known=tinv)[:2],
                         heads(q_ref), heads(k_ref), heads(v_ref), heads(g_ref), heads(b_ref),
                         tuple(st_ref[j] for j in range(hg)))
        grads = vjp((heads(do_ref), tuple(dstate[j] for j in range(hg))))
        for ref, g in zip((dq_ref, dk_ref, dv_ref, dg_ref, db_ref), grads[:5]):
            for j in range(hg):
                ref[:, j * hd:(j + 1) * hd] = g[j]
        for j in range(hg):
            dstate[j] = grads[5][j]

    blk = lambda off: pl.BlockSpec((c, hg * hd), lambda h, i, _o=off: (n - 1 - i, h + _o))
    return pl.pallas_call(
        body, grid=(ng, n),
        in_specs=[blk(0), blk(ng), blk(2 * ng), blk(0), blk(0)]
        + [pl.BlockSpec((hg, None, hd, hd), lambda h, i: (h, n - 1 - i, 0, 0))] * 2 + [blk(0)],
        out_specs=[blk(0)] * 5,
        out_shape=[jax.ShapeDtypeStruct((s, DN_KEY_DIM), f32)] * 5,
        scratch_shapes=[pltpu.VMEM((hg, hd, hd), f32)],
        compiler_params=_cparams("parallel", "arbitrary"),
        name="delta_bwd_" + tag)(qkv, qkv, qkv, gb, betab, states, tinvs, do)


def _dn_out_tile(o, z, ng):
    outs = []
    for h in range(DN_HEADS):
        sl = slice(h * DN_HEAD_DIM, (h + 1) * DN_HEAD_DIM)
        oh = o[:, sl]
        nrm = oh * lax.rsqrt(jnp.mean(oh * oh, -1, keepdims=True) + RMS_EPS) * ng[:, sl]
        outs.append(nrm * jax.nn.silu(z[:, sl]))
    return (jnp.concatenate(outs, -1),)


def _head_selectors():
    r, c = _iota2((BA_PAD, DN_KEY_DIM), 0), _iota2((BA_PAD, DN_KEY_DIM), 1) // DN_HEAD_DIM
    return (r == c).astype(f32), (r == c + DN_HEADS).astype(f32)


def dn_mixer_fwd(tag, proj, cw, alog_b, dtb_b, ng_b):
    eb, ea = _head_selectors()
    ba = (proj, BA_PAD, COL_BA // BA_PAD)
    qkv = dn_conv_fwd(tag, proj, cw)
    betab, gb = rowmap("dn_gate_" + tag, _gate_tile, [ba], [eb, ea, alog_b, dtb_b], [DN_KEY_DIM] * 2, TM_ROW)
    o, states, tinvs = delta_fwd(tag, qkv, gb, betab)
    z = (proj, DN_KEY_DIM, COL_Z // DN_KEY_DIM)
    a_out = rowmap("dn_out_" + tag, _dn_out_tile, [o, z], [ng_b], [DN_KEY_DIM], TM_ROW)[0]
    return a_out, (qkv, betab, gb, o, states, tinvs)


def dn_mixer_bwd(tag, proj, cw, alog_b, dtb_b, ng_b, res, da_out):
    qkv, betab, gb, o, states, tinvs = res
    eb, ea = _head_selectors()
    ba = (proj, BA_PAD, COL_BA // BA_PAD)
    z = (proj, DN_KEY_DIM, COL_Z // DN_KEY_DIM)
    (do, dz), (dng,) = rowmap_bwd("dn_out_bwd_" + tag, _dn_out_tile, [o, z], [ng_b], [da_out], TM_ROW)
    dq, dk, dv, dgb, dbetab = delta_bwd(tag, qkv, gb, betab, states, tinvs, do)
    dqkv_raw, dcw = dn_conv_bwd(tag, proj, cw, jnp.concatenate([dq, dk, dv], 1))
    (dba,), (dalog, ddtb) = rowmap_bwd("dn_gate_bwd_" + tag, _gate_tile, [ba], [eb, ea, alog_b, dtb_b],
                                       [dbetab, dgb], TM_ROW, par_mask=[False, False, True, True])
    return dqkv_raw, dz, dba, dcw[:DN_CONV], dalog, ddtb, dng


def _swap_halves(x):
    n = x.shape[1]
    first = (_iota2((1, n), 1) % SW_HEAD_DIM) < SW_HEAD_DIM // 2
    return jnp.where(first, pltpu.roll(x, n - SW_HEAD_DIM // 2, 1), pltpu.roll(x, SW_HEAD_DIM // 2, 1))


def _rope_apply(x, cos, sin_signed):
    return x * cos + _swap_halves(x) * sin_signed


def _rope_transpose(dy, cos, sin_signed):
    return dy * cos + _swap_halves(dy * sin_signed)


def rope_tables(positions, s):
    half = SW_HEAD_DIM // 2
    inv_freq = ROPE_THETA ** (-jnp.arange(0, SW_HEAD_DIM, 2, dtype=f32) / SW_HEAD_DIM)
    ang = positions.reshape(s, 1).astype(f32) * inv_freq[None, :]
    cos, sin = jnp.cos(ang), jnp.sin(ang)
    cos_t = jnp.tile(jnp.concatenate([cos, cos], 1), (1, SW_HEADS))
    sin_t = jnp.tile(jnp.concatenate([-sin, sin], 1), (1, SW_HEADS))
    assert cos_t.shape == (s, SW_DIM) and half * 2 == SW_HEAD_DIM
    return cos_t, sin_t


def rope_fwd(tag, proj, cos, sin):
    def fn(q, k, v, c, sg):
        return _rope_apply(q, c, sg), _rope_apply(k, c, sg), v

    rows = [(proj, SW_DIM, COL_SWQ // SW_DIM), (proj, SW_DIM, COL_SWK // SW_DIM), (proj, SW_DIM, COL_SWV // SW_DIM), cos, sin]
    return rowmap("rope_" + tag, fn, rows, [], [SW_DIM] * 3, TM_ROW, out_dtypes=[bf16] * 3)


def _swa_block(q, kp, kc, vp, vc, first):
    blk = SW_BLOCK
    kk = jnp.concatenate([kp, kc], 0)
    vv = jnp.concatenate([vp, vc], 0)
    dist = (_iota2((blk, 2 * blk), 0) + blk) - _iota2((blk, 2 * blk), 1)
    kj = _iota2((blk, 2 * blk), 1)
    valid = (dist >= 0) & (dist <= blk) & ((kj >= blk) | jnp.logical_not(first))
    lane_head = _iota2((1, LANES), 1) // SW_HEAD_DIM
    outs, lses = [], []
    for p in range(SW_DIM // LANES):
        sl = slice(p * LANES, (p + 1) * LANES)
        qp, kp_, vp_ = q[:, sl], kk[:, sl], vv[:, sl]
        o_pair = jnp.zeros((blk, LANES), f32)
        l_pair = jnp.zeros((blk, LANES), f32)
        for e in range(LANES // SW_HEAD_DIM):
            msk = lane_head == e
            sc = bdot(jnp.where(msk, qp, 0.0), kp_, 1, 1) * (SW_HEAD_DIM ** -0.5)
            sc = jnp.where(valid, sc, -1e30)
            m = lax.stop_gradient(jnp.max(sc, -1, keepdims=True))
            pe = jnp.exp(sc - m)
            l = jnp.sum(pe, -1, keepdims=True)
            o = bdot(pe, vp_, 1, 0) / l
            o_pair = o_pair + jnp.where(msk, o, 0.0)
            l_pair = l_pair + jnp.where(msk, m + jnp.log(l), 0.0)
        outs.append(o_pair)
        lses.append(l_pair)
    return jnp.concatenate(outs, -1), jnp.concatenate(lses, -1)


def _swa_specs(r):
    cur = pl.BlockSpec((SW_BLOCK, SW_DIM), lambda rho, n: (n, rho))
    prev = pl.BlockSpec((SW_BLOCK, SW_DIM), lambda rho, n: (jnp.maximum(n - 1, 0), rho))
    return cur, prev


def swa_fwd(tag, r, q, k, v):
    s = q.shape[0]
    ln = s // r
    q2, k2, v2 = (t.reshape(ln, r * SW_DIM) for t in (q, k, v))
    cur, prev = _swa_specs(r)

    def body(q_ref, kp_ref, kc_ref, vp_ref, vc_ref, o_ref, l_ref):
        ins = [r[...].astype(f32) for r in (q_ref, kp_ref, kc_ref, vp_ref, vc_ref)]
        o, l = _swa_block(*ins, pl.program_id(1) == 0)
        o_ref[...] = o
        l_ref[...] = l

    o, l = pl.pallas_call(
        body, grid=(r, ln // SW_BLOCK),
        in_specs=[cur, prev, cur, prev, cur], out_specs=[cur, cur],
        out_shape=[jax.ShapeDtypeStruct((ln, r * SW_DIM), f32)] * 2,
        compiler_params=_cparams("parallel", "parallel"), name=f"swa{r}_{tag}")(q2, k2, k2, v2, v2)
    return o.reshape(s, SW_DIM), l.reshape(s, SW_DIM)


def swa_bwd(tag, r, q, k, v, do, dl):
    s = q.shape[0]
    ln = s // r
    q2, k2, v2, do2, dl2 = (t.reshape(ln, r * SW_DIM) for t in (q, k, v, do, dl))
    cur, prev = _swa_specs(r)

    def body(q_ref, kp_ref, kc_ref, vp_ref, vc_ref, do_ref, dl_ref, dq_ref, dka_ref, dkb_ref, dva_ref, dvb_ref):
        first = pl.program_id(1) == 0
        ins = [r[...].astype(f32) for r in (q_ref, kp_ref, kc_ref, vp_ref, vc_ref)]
        _, vjp = jax.vjp(lambda *a: _swa_block(*a, first), *ins)
        dq_ref[...], dka_ref[...], dkb_ref[...], dva_ref[...], dvb_ref[...] = vjp((do_ref[...], dl_ref[...]))

    outs = pl.pallas_call(
        body, grid=(r, ln // SW_BLOCK),
        in_specs=[cur, prev, cur, prev, cur, cur, cur], out_specs=[cur] * 5,
        out_shape=[jax.ShapeDtypeStruct((ln, r * SW_DIM), f32)] * 5,
        compiler_params=_cparams("parallel", "parallel"), name=f"swa{r}_bwd_{tag}")(q2, k2, k2, v2, v2, do2, dl2)
    return [t.reshape(s, SW_DIM) for t in outs]


def _combine_tile(o1, l1, o2, l2, o3, l3):
    m = lax.stop_gradient(jnp.maximum(jnp.maximum(l1, l2), l3))
    e1, e2, e3 = jnp.exp(l1 - m), jnp.exp(l2 - m), jnp.exp(l3 - m)
    return ((o1 * e1 + o2 * e2 + o3 * e3) / (e1 + e2 + e3),)


def swa_merge_bwd(tag, grads, cos, sin):
    s = cos.shape[0]
    tm = SW_BLOCK
    nt = s // tm
    here = pl.BlockSpec((tm, SW_DIM), lambda i: (i, 0))
    arrs, specs = [], []
    for r, g in zip(SW_DILATIONS, grads):
        ahead = pl.BlockSpec((tm, SW_DIM), lambda i, _r=r: (jnp.minimum(i + _r, nt - 1), 0))
        arrs += g
        specs += [here, ahead, here, ahead, here]

    def body(*refs):
        i = pl.program_id(0)
        c_ref, s_ref = refs[15], refs[16]
        dq_ref, dk_ref, dv_ref = refs[17:]
        dq = jnp.zeros((tm, SW_DIM), f32)
        dk = jnp.zeros((tm, SW_DIM), f32)
        dv = jnp.zeros((tm, SW_DIM), f32)
        for b, r in enumerate(SW_DILATIONS):
            gq, gka, gkb, gva, gvb = refs[5 * b:5 * b + 5]
            inside = i + r < nt
            dq = dq + gq[...]
            dk = dk + gkb[...] + jnp.where(inside, gka[...], 0.0)
            dv = dv + gvb[...] + jnp.where(inside, gva[...], 0.0)
        dq_ref[...] = _rope_transpose(dq, c_ref[...], s_ref[...])
        dk_ref[...] = _rope_transpose(dk, c_ref[...], s_ref[...])
        dv_ref[...] = dv

    return pl.pallas_call(
        body, grid=(nt,), in_specs=specs + [here, here], out_specs=[here] * 3,
        out_shape=[jax.ShapeDtypeStruct((s, SW_DIM), f32)] * 3,
        compiler_params=_cparams("parallel"), name="swa_merge_bwd_" + tag)(*arrs, cos, sin)


def swa_mixer_fwd(tag, proj, cos, sin):
    q, k, v = rope_fwd(tag, proj, cos, sin)
    ols = []
    for r in SW_DILATIONS:
        ols += list(swa_fwd(tag, r, q, k, v))
    b_out = rowmap("swa_comb_" + tag, _combine_tile, ols, [], [SW_DIM], TM_ROW)[0]
    return b_out, (q, k, v, ols)


def swa_mixer_bwd(tag, cos, sin, res, db_out):
    q, k, v, ols = res
    dols, _ = rowmap_bwd("swa_comb_bwd_" + tag, _combine_tile, ols, [], [db_out], TM_ROW)
    grads = [swa_bwd(tag, r, q, k, v, dols[2 * b], dols[2 * b + 1]) for b, r in enumerate(SW_DILATIONS)]
    return swa_merge_bwd(tag, grads, cos, sin)


TM_S5 = 256
S5_GPB = LANES // S5_GROUP
S5_NBLK = D_MODEL // LANES
S5_HALF = S5_GPB * S5_STATE
S5_BW = 2 * S5_HALF
S5_WIDTH = S5_NBLK * S5_BW
S5_TABW = S5_NBLK * S5_HALF


def _s5_disc_tile(a_re, a_im, log_dt, b_re, b_im, expand):
    dt = jnp.exp(log_dt)
    mag = jnp.exp(a_re * dt)
    abar_re, abar_im = mag * jnp.cos(a_im * dt), mag * jnp.sin(a_im * dt)
    n_re, n_im = abar_re - 1.0, abar_im
    den = a_re * a_re + a_im * a_im
    c_re = (n_re * a_re + n_im * a_im) / den
    c_im = (n_im * a_re - n_re * a_im) / den
    cx_re, cx_im = hdot(c_re, expand), hdot(c_im, expand)
    return abar_re, abar_im, cx_re * b_re - cx_im * b_im, cx_re * b_im + cx_im * b_re


def _s5_expand():
    return (_iota2((S5_STATE, S5_STATE * S5_GROUP), 1) // S5_GROUP == _iota2((S5_STATE, S5_STATE * S5_GROUP), 0)).astype(f32)


def s5_tables(a_re, a_im, log_dt):
    lanes = lambda v: v.reshape(1, S5_TABW)
    dt = jnp.broadcast_to(log_dt.reshape(S5_GROUPS, 1), (S5_GROUPS, S5_STATE))
    t = TM_S5

    def body(are_ref, aim_ref, ldt_ref, ar_ref, ai_ref, arr_ref, air_ref):
        dtv = jnp.exp(ldt_ref[...])
        lre, lim = are_ref[...] * dtv, aim_ref[...] * dtv
        row = _iota2((t, S5_HALF), 0)
        for asc, o_re, o_im in ((True, ar_ref, ai_ref), (False, arr_ref, air_ref)):
            n = (row + 1 if asc else t - row).astype(f32)
            mag = jnp.exp(n * lre)
            o_re[...] = mag * jnp.cos(n * lim)
            o_im[...] = mag * jnp.sin(n * lim)

    lane = pl.BlockSpec((1, S5_HALF), lambda j: (0, j))
    tab = pl.BlockSpec((t, S5_HALF), lambda j: (0, j))
    return pl.pallas_call(
        body, grid=(S5_NBLK,), in_specs=[lane] * 3, out_specs=[tab] * 4,
        out_shape=[jax.ShapeDtypeStruct((t, S5_TABW), f32)] * 4,
        compiler_params=_cparams("parallel"), name="s5_tables")(lanes(a_re), lanes(a_im), lanes(dt))


def s5_pack_weights(bbar_re, bbar_im, c_re, c_im):
    eye = jnp.eye(S5_GPB, dtype=f32)
    bb = jnp.stack([bbar_re.reshape(S5_GROUPS, S5_STATE, S5_GROUP), bbar_im.reshape(S5_GROUPS, S5_STATE, S5_GROUP)], 1)
    bb = bb.transpose(0, 3, 1, 2).reshape(S5_NBLK, S5_GPB, S5_GROUP, 2, S5_STATE)
    wb = (bb[:, :, :, :, None, :] * eye[None, :, None, None, :, None]).reshape(S5_NBLK, LANES, S5_BW)
    cc = jnp.stack([c_re, -c_im], 1)
    cc = cc.reshape(S5_NBLK, S5_GPB, 2, S5_GROUP, S5_STATE).transpose(0, 2, 1, 4, 3)
    wc = (cc[:, :, :, :, None, :] * eye[None, None, :, None, :, None]).reshape(S5_NBLK, S5_BW, LANES)
    return wb, wc


def s5_unpack_weight_grads(dwb, dwc):
    d6 = dwb.reshape(S5_NBLK, S5_GPB, S5_GROUP, 2, S5_GPB, S5_STATE)
    dbb = jnp.stack([d6[:, gl, :, :, gl, :] for gl in range(S5_GPB)])
    dbb = dbb.transpose(1, 0, 3, 4, 2).reshape(S5_GROUPS, 2, S5_STATE * S5_GROUP)
    c6 = dwc.reshape(S5_NBLK, 2, S5_GPB, S5_STATE, S5_GPB, S5_GROUP)
    dcc = jnp.stack([c6[:, :, gl, :, gl, :] for gl in range(S5_GPB)])
    dcc = dcc.transpose(1, 0, 2, 4, 3).reshape(S5_GROUPS, 2, S5_GROUP, S5_STATE)
    return dbb[:, 0], dbb[:, 1], dcc[:, 0], -dcc[:, 1]


def _s5_step_rows(t):
    d, out = 1, []
    while d < t:
        out.append(d)
        d *= 2
    return out


def s5_core_fwd(tag, u, wb, wc, a1, a2, dskip):
    s = u.shape[0]
    t = TM_S5

    def body(u_ref, wb_ref, wc_ref, ar_ref, ai_ref, d_ref, y_ref, x_ref, carry):
        @pl.when(pl.program_id(1) == 0)
        def _():
            carry[...] = jnp.zeros_like(carry)

        uv = u_ref[...]
        bu = bdot(uv, wb_ref[...], 1, 0)
        xr, xi = bu[:, :S5_HALF], bu[:, S5_HALF:]
        row = _iota2((t, S5_HALF), 0)
        for d in _s5_step_rows(t):
            ar, ai = ar_ref[d - 1:d, :], ai_ref[d - 1:d, :]
            if d % SUBLANES:
                keep = row >= d
                sr = jnp.where(keep, pltpu.roll(xr, d, 0), 0.0)
                si = jnp.where(keep, pltpu.roll(xi, d, 0), 0.0)
                xr, xi = xr + ar * sr - ai * si, xi + ar * si + ai * sr
            else:
                sr, si = xr[:t - d], xi[:t - d]
                xr = jnp.concatenate([xr[:d], xr[d:] + (ar * sr - ai * si)], 0)
                xi = jnp.concatenate([xi[:d], xi[d:] + (ar * si + ai * sr)], 0)
        cr, ci = carry[:, :S5_HALF], carry[:, S5_HALF:]
        ar, ai = ar_ref[...], ai_ref[...]
        x_ref[:, :S5_HALF] = xr + ar * cr - ai * ci
        x_ref[:, S5_HALF:] = xi + ar * ci + ai * cr
        carry[...] = x_ref[t - 1:t, :]
        y_ref[...] = bdot(x_ref[...], wc_ref[...], 1, 0) + d_ref[...] * uv

    tab = pl.BlockSpec((t, S5_HALF), lambda j, i: (0, j))
    return pl.pallas_call(
        body, grid=(S5_NBLK, s // t),
        in_specs=[pl.BlockSpec((t, LANES), lambda j, i: (i, j)),
                  pl.BlockSpec((None, LANES, S5_BW), lambda j, i: (j, 0, 0)),
                  pl.BlockSpec((None, S5_BW, LANES), lambda j, i: (j, 0, 0)),
                  tab, tab, pl.BlockSpec((1, LANES), lambda j, i: (0, j))],
        out_specs=[pl.BlockSpec((t, LANES), lambda j, i: (i, j)), pl.BlockSpec((t, S5_BW), lambda j, i: (i, j))],
        out_shape=[jax.ShapeDtypeStruct((s, D_MODEL), f32), jax.ShapeDtypeStruct((s, S5_WIDTH), f32)],
        scratch_shapes=[pltpu.VMEM((1, S5_BW), f32)],
        compiler_params=_cparams("parallel", "arbitrary"), name="s5_core_" + tag)(u, wb, wc, a1, a2, dskip)


def s5_core_bwd(tag, u, x, wb, wc, a1, a2, a1r, a2r, dskip, dy):
    s = u.shape[0]
    t = TM_S5
    nt = s // t
    hb = t // SUBLANES

    def body(u_ref, dy_ref, x_ref, xh_ref, wb_ref, wc_ref, ar_ref, ai_ref, arr_ref, air_ref, d_ref,
             du_ref, dwb_ref, dwc_ref, dd_ref, q1_ref, q2_ref, carry, lam_scr):
        i = pl.program_id(1)
        tt = nt - 1 - i

        @pl.when(i == 0)
        def _():
            carry[...] = jnp.zeros_like(carry)

        uv, dyv, xv = u_ref[...], dy_ref[...], x_ref[...]
        lam = bdot(dyv, wc_ref[...], 1, 1)
        lr, li = lam[:, :S5_HALF], lam[:, S5_HALF:]
        row = _iota2((t, S5_HALF), 0)
        for d in _s5_step_rows(t):
            ar, ai = ar_ref[d - 1:d, :], ai_ref[d - 1:d, :]
            if d % SUBLANES:
                keep = row < t - d
                sr = jnp.where(keep, pltpu.roll(lr, t - d, 0), 0.0)
                si = jnp.where(keep, pltpu.roll(li, t - d, 0), 0.0)
                lr, li = lr + ar * sr + ai * si, li + ar * si - ai * sr
            else:
                sr, si = lr[d:], li[d:]
                lr = jnp.concatenate([lr[:t - d] + (ar * sr + ai * si), lr[t - d:]], 0)
                li = jnp.concatenate([li[:t - d] + (ar * si - ai * sr), li[t - d:]], 0)
        cr, ci = carry[:, :S5_HALF], carry[:, S5_HALF:]
        ar, ai = arr_ref[...], air_ref[...]
        lr, li = lr + ar * cr + ai * ci, li + ar * ci - ai * cr
        lam_scr[:, :S5_HALF] = lr
        lam_scr[:, S5_HALF:] = li
        carry[...] = lam_scr[0:1, :]
        lam = lam_scr[...]
        du_ref[...] = bdot(lam, wb_ref[...], 1, 1) + d_ref[...] * dyv
        x_last = jnp.where(tt > 0, xh_ref[SUBLANES - 1:SUBLANES, :], 0.0)
        x_prev = jnp.where(_iota2((t, S5_BW), 0) == 0, x_last, pltpu.roll(xv, 1, 0))
        pr, pi = x_prev[:, :S5_HALF], x_prev[:, S5_HALF:]
        p1, p2 = lr * pr + li * pi, li * pr - lr * pi
        q1 = p1[:SUBLANES, :]
        q2 = p2[:SUBLANES, :]
        for k in range(1, hb):
            q1 = q1 + p1[k * SUBLANES:(k + 1) * SUBLANES, :]
            q2 = q2 + p2[k * SUBLANES:(k + 1) * SUBLANES, :]
        upd = [(dwb_ref, bdot(uv, lam, 0, 0)), (dwc_ref, bdot(xv, dyv, 0, 0)),
               (dd_ref, jnp.sum(dyv * uv, 0, keepdims=True)), (q1_ref, q1), (q2_ref, q2)]

        @pl.when(i == 0)
        def _():
            for ref, val in upd:
                ref[...] = val

        @pl.when(i != 0)
        def _():
            for ref, val in upd:
                ref[...] += val

    nb8 = s // SUBLANES
    rev = lambda w: pl.BlockSpec((t, w), lambda j, i: (nt - 1 - i, j))
    tab = pl.BlockSpec((t, S5_HALF), lambda j, i: (0, j))
    return pl.pallas_call(
        body, grid=(S5_NBLK, nt),
        in_specs=[rev(LANES), rev(LANES), rev(S5_BW),
                  pl.BlockSpec((SUBLANES, S5_BW), lambda j, i: (jnp.maximum((nt - 1 - i) * hb - 1, 0), j)),
                  pl.BlockSpec((None, LANES, S5_BW), lambda j, i: (j, 0, 0)),
                  pl.BlockSpec((None, S5_BW, LANES), lambda j, i: (j, 0, 0)),
                  tab, tab, tab, tab, pl.BlockSpec((1, LANES), lambda j, i: (0, j))],
        out_specs=[rev(LANES),
                   pl.BlockSpec((None, LANES, S5_BW), lambda j, i: (j, 0, 0)),
                   pl.BlockSpec((None, S5_BW, LANES), lambda j, i: (j, 0, 0)),
                   pl.BlockSpec((1, LANES), lambda j, i: (0, j)),
                   pl.BlockSpec((SUBLANES, S5_HALF), lambda j, i: (0, j)),
                   pl.BlockSpec((SUBLANES, S5_HALF), lambda j, i: (0, j))],
        out_shape=[jax.ShapeDtypeStruct((s, D_MODEL), f32),
                   jax.ShapeDtypeStruct((S5_NBLK, LANES, S5_BW), f32),
                   jax.ShapeDtypeStruct((S5_NBLK, S5_BW, LANES), f32),
                   jax.ShapeDtypeStruct((1, D_MODEL), f32),
                   jax.ShapeDtypeStruct((SUBLANES, S5_TABW), f32),
                   jax.ShapeDtypeStruct((SUBLANES, S5_TABW), f32)],
        scratch_shapes=[pltpu.VMEM((1, S5_BW), f32), pltpu.VMEM((t, S5_BW), f32)],
        compiler_params=_cparams("parallel", "arbitrary"),
        name="s5_core_bwd_" + tag)(u, dy, x, x, wb, wc, a1, a2, a1r, a2r, dskip)


def _gelu_tile(y):
    return (jax.nn.gelu(y),)


def s5_mixer_fwd(tag, u, prm, w_og):
    a_re, a_im, log_dt, b_re, b_im, c_re, c_im, dskip = prm
    disc_in = [a_re, a_im, log_dt.reshape(S5_GROUPS, 1), b_re.reshape(S5_GROUPS, -1), b_im.reshape(S5_GROUPS, -1)]
    abar_re, abar_im, bbar_re, bbar_im = rowmap("s5_disc_" + tag, _s5_disc_tile, disc_in, [_s5_expand()],
                                                [S5_STATE, S5_STATE, S5_STATE * S5_GROUP, S5_STATE * S5_GROUP], S5_GROUPS)
    del abar_re, abar_im
    a1, a2, a1r, a2r = s5_tables(a_re, a_im, log_dt)
    wb, wc = s5_pack_weights(bbar_re, bbar_im, c_re, c_im)
    wb, wc = wb.astype(bf16), wc.astype(bf16)
    y, x = s5_core_fwd(tag, u, wb, wc, a1, a2, dskip.reshape(1, D_MODEL))
    hid = rowmap("s5_gelu_" + tag, _gelu_tile, [y], [], [D_MODEL], TM_ROW, out_dtypes=[bf16])[0]
    og = mm_nn("s5_og_" + tag, hid, w_og)
    mix = rowmap("s5_glu_" + tag, _glu_tile, [og], [], [D_MODEL], TM_ROW)[0]
    return mix, (disc_in, a1, a2, a1r, a2r, wb, wc, x, y, hid, og)


def s5_mixer_bwd(tag, u, prm, w_og, res, dmix):
    a_re, a_im, log_dt, b_re, b_im, c_re, c_im, dskip = prm
    disc_in, a1, a2, a1r, a2r, wb, wc, x, y, hid, og = res
    (dog,), _ = rowmap_bwd("s5_glu_bwd_" + tag, _glu_tile, [og], [], [dmix], TM_ROW)
    dw_og = mm_tn("s5_og_dw_" + tag, hid, dog)
    dhid = mm_nt("s5_og_dx_" + tag, dog, w_og)
    (dy,), _ = rowmap_bwd("s5_gelu_bwd_" + tag, _gelu_tile, [y], [], [dhid], TM_ROW)
    du, dwb, dwc, ddskip, q1, q2 = s5_core_bwd(tag, u, x, wb, wc, a1, a2, a1r, a2r, dskip.reshape(1, D_MODEL), dy)
    dbbar_re, dbbar_im, dc_re, dc_im = s5_unpack_weight_grads(dwb, dwc)
    dabar_re = q1.sum(0).reshape(S5_GROUPS, S5_STATE)
    dabar_im = q2.sum(0).reshape(S5_GROUPS, S5_STATE)
    grads, _ = rowmap_bwd("s5_disc_bwd_" + tag, _s5_disc_tile, disc_in, [_s5_expand()],
                          [dabar_re, dabar_im, dbbar_re, dbbar_im], S5_GROUPS, par_mask=[False])
    da_re, da_im, dlog_dt, db_re, db_im = grads
    return du, (da_re, da_im, dlog_dt.reshape(S5_GROUPS), db_re.reshape(b_re.shape), db_im.reshape(b_im.shape),
                dc_re, dc_im, ddskip.reshape(D_MODEL)), dw_og


HYB_IN = 3592
_IN_B0, _IN_SW0 = 2048, 2056


IN_SHARD = HYB_IN // 4
SHARD_ORDER_GRADS = ("hyb_w_in", "ffn_wg", "ffn_wu", "ffn_wd")
FFN_TRANSPOSED = ("ffn_wg", "ffn_wu")


def _w_in_pieces():
    runs = [(0, _IN_B0, 0), (_IN_B0, _IN_SW0, COL_BA), (_IN_SW0, HYB_IN, _IN_B0)]
    out = []
    for sh in range(4):
        lo, hi = sh * IN_SHARD, (sh + 1) * IN_SHARD
        for r_lo, r_hi, c_lo in runs:
            a, b = max(lo, r_lo), min(hi, r_hi)
            if a < b:
                out.append((sh, a - lo, b - lo, c_lo + a - r_lo))
    return out


def w_in_to_canonical(tag, layer, w4):
    tr = 128

    def body(w_ref, o_ref):
        o_ref[:, COL_BA:] = jnp.zeros((tr, BA_PAD), o_ref.dtype)
        for sh, a, b, c in _w_in_pieces():
            o_ref[:, c:c + b - a] = w_ref[sh, :, a:b]

    return pl.pallas_call(
        body, grid=(D_MODEL // tr,),
        in_specs=[pl.BlockSpec((4, None, tr, IN_SHARD), lambda i: (0, layer, i, 0))],
        out_specs=pl.BlockSpec((tr, PROJ_COLS), lambda i: (i, 0)),
        out_shape=jax.ShapeDtypeStruct((D_MODEL, PROJ_COLS), w4.dtype),
        compiler_params=_cparams("parallel"), name="w_in_canon_" + tag)(w4)


def w_in_grad_to_shards(tag, g):
    tr = 128

    def body(g_ref, o_ref):
        for sh, a, b, c in _w_in_pieces():
            o_ref[sh, :, a:b] = g_ref[:, c:c + b - a]

    return pl.pallas_call(
        body, grid=(D_MODEL // tr,),
        in_specs=[pl.BlockSpec((tr, PROJ_COLS), lambda i: (i, 0))],
        out_specs=pl.BlockSpec((4, tr, IN_SHARD), lambda i: (0, i, 0)),
        out_shape=jax.ShapeDtypeStruct((4, D_MODEL, IN_SHARD), f32),
        compiler_params=_cparams("parallel"), name="w_in_grad_shards_" + tag)(g)


def _add2(name, a, b):
    return rowmap(name, lambda p, q: (p + q,), [a, b], [], [a.shape[1]], _pick(a.shape[0], (256, 128, 64, 32, 16, 8)))[0]


def local_step(x, mem, positions, target, p):
    s = x.shape[0]
    cos, sin = rope_tables(positions, s)
    row = lambda v: v.reshape(1, -1).astype(f32)
    wg4, wu4, wd4 = (p[n].astype(bf16) for n in ("ffn_wg", "ffn_wu", "ffn_wd"))
    h = h16 = x
    tape = []
    for l in range(DEPTH):
        i, tag = l // 2, str(l)
        t = {"h0": h, "h0_16": h16}
        if l % 2 == 0:
            t["w_in"] = w_in_to_canonical(tag, i, p["hyb_w_in"].astype(bf16))
            t["w_out"] = p["hyb_w_out"][i].astype(bf16)
            t["dn_prm"] = (p["dn_conv_w"][i].astype(f32), row(jnp.repeat(p["dn_a_log"][i], DN_HEAD_DIM)),
                           row(jnp.repeat(p["dn_dt_bias"][i], DN_HEAD_DIM)), row(jnp.tile(p["dn_norm_g"][i], DN_HEADS)))
            t["proj"] = mm_nn("hyb_in_" + tag, h16, t["w_in"])
            a_out, t["dn"] = dn_mixer_fwd(tag, t["proj"], *t["dn_prm"])
            b_out, t["swa"] = swa_mixer_fwd(tag, t["proj"], cos, sin)
            t["mixed"] = jnp.concatenate([a_out, b_out], 1)
            mix = mm_nn("hyb_out_" + tag, t["mixed"], t["w_out"])
        else:
            t["s5_prm"] = tuple(p[n][i].astype(f32) for n in
                                ("s5_a_re", "s5_a_im", "s5_log_dt", "s5_b_re", "s5_b_im", "s5_c_re", "s5_c_im", "s5_d"))
            t["w_og"] = jnp.concatenate([p["s5_glu_wo"][i], p["s5_glu_wg"][i]], 1).astype(bf16)
            mix, t["s5"] = s5_mixer_fwd(tag, h, t["s5_prm"], t["w_og"])
        t["mix"] = mix
        t["ln"] = [(row(p[g][l]), row(p[b][l])) for g, b in
                   (("ln_mix_g", "ln_mix_b"), ("ln_x_g", "ln_x_b"), ("ln_ffn_g", "ln_ffn_b"))]
        t["h1"], t["h1_16"] = postnorm_fwd("mix" + tag, h, mix, *t["ln"][0])
        t["wq"], t["wo"] = p["xq_w"][l].astype(bf16), p["xo_w"][l].astype(bf16)
        t["wkv"] = jnp.concatenate([p["xk_w"][l], p["xv_w"][l]], 1).astype(bf16)
        t["xo"], t["xres"] = xattn_fwd(tag, t["h1_16"], mem, t["wq"], t["wkv"], t["wo"])
        t["h2"], t["h2_16"] = postnorm_fwd("x" + tag, t["h1"], t["xo"], *t["ln"][1])
        t["fo"], t["fres"] = ffn_fwd(tag, l, t["h2_16"], wg4, wu4, wd4)
        h, h16 = postnorm_fwd("ffn" + tag, t["h2"], t["fo"], *t["ln"][2])
        tape.append(t)

    part, dh = loss_head(h, target)
    loss = jnp.sum(part)

    g = {n: [None] * v.shape[1 if n in SHARD_ORDER_GRADS else 0] for n, v in p.items()}
    for l in reversed(range(DEPTH)):
        i, tag, t = l // 2, str(l), tape[l]
        dh2a, dfo, dg, db = postnorm_bwd("ffn" + tag, t["h2"], t["fo"], *t["ln"][2], dh)
        g["ln_ffn_g"][l], g["ln_ffn_b"][l] = dg[0], db[0]
        dh2b, g["ffn_wg"][l], g["ffn_wu"][l], g["ffn_wd"][l] = ffn_bwd(tag, l, t["h2_16"], wg4, wu4, wd4, t["fres"], dfo)
        dh1a, dxo, dg, db = postnorm_bwd("x" + tag, t["h1"], t["xo"], *t["ln"][1], [dh2a, dh2b])
        g["ln_x_g"][l], g["ln_x_b"][l] = dg[0], db[0]
        dh1b, g["xq_w"][l], dwkv, g["xo_w"][l] = xattn_bwd(tag, t["h1_16"], mem, t["wq"], t["wkv"], t["wo"], t["xres"], dxo)
        g["xk_w"][l], g["xv_w"][l] = dwkv[:, :D_MODEL], dwkv[:, D_MODEL:]
        dh0a, dmix, dg, db = postnorm_bwd("mix" + tag, t["h0"], t["mix"], *t["ln"][0], [dh1a, dh1b])
        g["ln_mix_g"][l], g["ln_mix_b"][l] = dg[0], db[0]
        if l % 2 == 0:
            g["hyb_w_out"][i] = mm_tn("hyb_out_dw_" + tag, t["mixed"], dmix)
            dmixed = mm_nt("hyb_out_dx_" + tag, dmix, t["w_out"])
            dqkv, dz, dba, dcw, dalog, ddtb, dng = dn_mixer_bwd(tag, t["proj"], *t["dn_prm"], t["dn"], (dmixed, DN_KEY_DIM, 0))
            g["dn_conv_w"][i] = dcw
            g["dn_a_log"][i] = dalog.reshape(DN_HEADS, DN_HEAD_DIM).sum(1)
            g["dn_dt_bias"][i] = ddtb.reshape(DN_HEADS, DN_HEAD_DIM).sum(1)
            g["dn_norm_g"][i] = dng.reshape(DN_HEADS, DN_HEAD_DIM).sum(0)
            dq, dk, dv = swa_mixer_bwd(tag, cos, sin, t["swa"], (dmixed, SW_DIM, 1))
            dproj = jnp.concatenate([dqkv, dz, dq, dk, dv, dba], 1)
            g["hyb_w_in"][i] = w_in_grad_to_shards(tag, mm_tn("hyb_in_dw_" + tag, t["h0_16"], dproj))
            dh0b = mm_nt("hyb_in_dx_" + tag, dproj, t["w_in"])
        else:
            dh0b, dprm, dw_og = s5_mixer_bwd(tag, t["h0"], t["s5_prm"], t["w_og"], t["s5"], dmix)
            for n, v in zip(("s5_a_re", "s5_a_im", "s5_log_dt", "s5_b_re", "s5_b_im", "s5_c_re", "s5_c_im", "s5_d"), dprm):
                g[n][i] = v
            g["s5_glu_wo"][i], g["s5_glu_wg"][i] = dw_og[:, :D_MODEL], dw_og[:, D_MODEL:]
        dh = [dh0a, dh0b]
    grad_x = _add2("grad_x", dh[0], dh[1])
    grads = {n: jnp.stack(v, 1 if n in SHARD_ORDER_GRADS else 0) for n, v in g.items()}
    return loss, grad_x, grads


WEIGHT_NAMES = ("hyb_w_in", "dn_conv_w", "dn_a_log", "dn_dt_bias", "dn_norm_g", "hyb_w_out", "s5_a_re", "s5_a_im",
                "s5_log_dt", "s5_b_re", "s5_b_im", "s5_c_re", "s5_c_im", "s5_d", "s5_glu_wo", "s5_glu_wg",
                "ln_mix_g", "ln_mix_b", "xq_w", "xk_w", "xv_w", "xo_w", "ln_x_g", "ln_x_b",
                "ffn_wg", "ffn_wu", "ffn_wd", "ln_ffn_g", "ln_ffn_b")
SHARD_AXIS = {"hyb_w_in": 2, "dn_conv_w": 2, "hyb_w_out": 1, "s5_d": 1, "s5_glu_wo": 1, "s5_glu_wg": 1,
              "xq_w": 1, "xk_w": 1, "xv_w": 1, "xo_w": 1, "ffn_wg": 2, "ffn_wu": 2, "ffn_wd": 1}
GATHER_F32 = ("dn_conv_w", "s5_d")
N_CHIPS = 4
PACK_COLS = 1024
_ANY = pl.BlockSpec(memory_space=pl.ANY)


def _pos():
    return lax.axis_index("x"), lax.axis_index("y"), lax.axis_index("c")


def _chip_peers(mx, my):
    return [(1 - mx, my), (mx, 1 - my), (1 - mx, 1 - my)]


def _rcopy(src, dst, ssem, rsem, dev):
    return pltpu.make_async_remote_copy(src_ref=src, dst_ref=dst, send_sem=ssem, recv_sem=rsem,
                                        device_id=dev, device_id_type=pl.DeviceIdType.MESH)


def comm_allgather4(name, x):
    def body(x_ref, o_ref, ssem, rsem, lsem):
        mx, my, mc = _pos()
        me = 2 * mx + my
        peers = _chip_peers(mx, my)
        loc = pltpu.make_async_copy(x_ref, o_ref.at[me], lsem)
        loc.start()
        sends = [_rcopy(x_ref, o_ref.at[me], ssem.at[k], rsem.at[k], (px, py, mc)) for k, (px, py) in enumerate(peers)]
        for cp in sends:
            cp.start()
        for k, (px, py) in enumerate(peers):
            _rcopy(x_ref, o_ref.at[2 * px + py], ssem.at[k], rsem.at[k], (px, py, mc)).wait_recv()
        for cp in sends:
            cp.wait_send()
        loc.wait()

    return pl.pallas_call(
        body, out_shape=jax.ShapeDtypeStruct((N_CHIPS,) + x.shape, x.dtype), in_specs=[_ANY], out_specs=_ANY,
        scratch_shapes=[pltpu.SemaphoreType.DMA((3,)), pltpu.SemaphoreType.DMA((3,)), pltpu.SemaphoreType.DMA],
        name=name)(x)


def _multi_call(name, body, ins, out_shapes, sems, in_place=False):
    return pl.pallas_call(
        body, out_shape=out_shapes, in_specs=[_ANY] * len(ins), out_specs=[_ANY] * len(out_shapes),
        scratch_shapes=sems, input_output_aliases={w: w for w in range(len(ins))} if in_place else {},
        name=name)(*ins)


def comm_gather_weights(name, slots):
    n = len(slots)

    def body(*refs):
        os_ = refs[n:2 * n]
        ssem, rsem, fssem, frsem = refs[2 * n:]
        mx, my, mc = _pos()
        me = 2 * mx + my
        peers = _chip_peers(mx, my)
        sib = (mx, my, 1 - mc)
        half = [o.shape[1] // 2 for o in os_]
        mine = [pl.ds(mc * h, h) for h in half]
        other = [pl.ds((1 - mc) * h, h) for h in half]
        sends = [_rcopy(os_[w].at[me, mine[w]], os_[w].at[me, mine[w]], ssem.at[w, k], rsem.at[w, k], (px, py, mc))
                 for w in range(n) for k, (px, py) in enumerate(peers)]
        for cp in sends:
            cp.start()
        fwds = []
        for w in range(n):
            for k, (px, py) in enumerate(peers):
                landed = os_[w].at[2 * px + py, mine[w]]
                _rcopy(landed, landed, ssem.at[w, k], rsem.at[w, k], (px, py, mc)).wait_recv()
                fw = _rcopy(landed, landed, fssem.at[w, k], frsem.at[w, k], sib)
                fw.start()
                fwds.append(fw)
        for w in range(n):
            for k, (px, py) in enumerate(peers):
                theirs = os_[w].at[2 * px + py, other[w]]
                _rcopy(theirs, theirs, fssem.at[w, k], frsem.at[w, k], sib).wait_recv()
        for cp in sends + fwds:
            cp.wait_send()

    dma = pltpu.SemaphoreType.DMA
    return _multi_call(name, body, slots, [jax.ShapeDtypeStruct(x.shape, x.dtype) for x in slots],
                       [dma((n, 3)), dma((n, 3)), dma((n, 3)), dma((n, 3))], in_place=True)


def comm_sibling_halves(name, gs):
    n = len(gs)

    def body(*refs):
        xs, os_ = refs[:n], refs[n:2 * n]
        ssem, rsem = refs[2 * n:]
        mx, my, mc = _pos()
        sib = (mx, my, 1 - mc)
        sends = []
        for w in range(n):
            h = xs[w].shape[1] // 2
            for j in range(N_CHIPS):
                sends.append(_rcopy(xs[w].at[j, pl.ds((1 - mc) * h, h)], os_[w].at[j], ssem.at[w, j], rsem.at[w, j], sib))
        for cp in sends:
            cp.start()
        for w in range(n):
            for j in range(N_CHIPS):
                _rcopy(os_[w].at[j], os_[w].at[j], ssem.at[w, j], rsem.at[w, j], sib).wait_recv()
        for cp in sends:
            cp.wait_send()

    dma = pltpu.SemaphoreType.DMA
    return _multi_call(name, body, gs,
                       [jax.ShapeDtypeStruct((N_CHIPS, g.shape[1] // 2) + g.shape[2:], g.dtype) for g in gs],
                       [dma((n, N_CHIPS)), dma((n, N_CHIPS))])


def comm_alltoall4(name, xs):
    n = len(xs)

    def body(*refs):
        xr, os_ = refs[:n], refs[n:2 * n]
        ssem, rsem = refs[2 * n:]
        mx, my, mc = _pos()
        me = 2 * mx + my
        peers = _chip_peers(mx, my)
        sends = [_rcopy(xr[w].at[2 * px + py], os_[w].at[me], ssem.at[w, k], rsem.at[w, k], (px, py, mc))
                 for w in range(n) for k, (px, py) in enumerate(peers)]
        for cp in sends:
            cp.start()
        for w in range(n):
            for k, (px, py) in enumerate(peers):
                dst = os_[w].at[2 * px + py]
                _rcopy(dst, dst, ssem.at[w, k], rsem.at[w, k], (px, py, mc)).wait_recv()
        for cp in sends:
            cp.wait_send()

    dma = pltpu.SemaphoreType.DMA
    return _multi_call(name, body, xs, [jax.ShapeDtypeStruct(x.shape, x.dtype) for x in xs], [dma((n, 3)), dma((n, 3))])


def comm_sibling_join(name, bs):
    n = len(bs)

    def body(*refs):
        os_ = refs[n:2 * n]
        ssem, rsem = refs[2 * n:]
        mx, my, mc = _pos()
        sib = (mx, my, 1 - mc)
        sends = [_rcopy(os_[w].at[mc], os_[w].at[mc], ssem.at[w], rsem.at[w], sib) for w in range(n)]
        for cp in sends:
            cp.start()
        for w in range(n):
            dst = os_[w].at[1 - mc]
            _rcopy(dst, dst, ssem.at[w], rsem.at[w], sib).wait_recv()
        for cp in sends:
            cp.wait_send()

    dma = pltpu.SemaphoreType.DMA
    return _multi_call(name, body, bs, [jax.ShapeDtypeStruct(b.shape, b.dtype) for b in bs], [dma((n,)), dma((n,))],
                       in_place=True)


def comm_sibling_swap(name, x):
    def body(x_ref, o_ref, ssem, rsem):
        mx, my, mc = _pos()
        cp = _rcopy(x_ref, o_ref, ssem, rsem, (mx, my, 1 - mc))
        cp.start()
        cp.wait_recv()
        cp.wait_send()

    return pl.pallas_call(
        body, out_shape=jax.ShapeDtypeStruct(x.shape, x.dtype), in_specs=[_ANY], out_specs=_ANY,
        scratch_shapes=[pltpu.SemaphoreType.DMA, pltpu.SemaphoreType.DMA], name=name)(x)


def _row_tile(r):
    return _pick(r, (256, 128, 64, 32, 16, 8))


def add_own_half(name, g, recv, out_dtype):
    r, c = g.shape[2:]
    tr = _row_tile(r)
    mc = lax.axis_index("c").astype(jnp.int32).reshape(1)

    def body(c_ref, g_ref, r_ref, o_ref):
        o_ref[...] = (g_ref[...] + r_ref[...]).astype(o_ref.dtype)

    grid_spec = pltpu.PrefetchScalarGridSpec(
        num_scalar_prefetch=1, grid=(N_CHIPS, r // tr),
        in_specs=[pl.BlockSpec((None, None, tr, c), lambda j, i, cr: (j, cr[0], i, 0)),
                  pl.BlockSpec((None, tr, c), lambda j, i, cr: (j, i, 0))],
        out_specs=pl.BlockSpec((None, tr, c), lambda j, i, cr: (j, i, 0)))
    return pl.pallas_call(body, grid_spec=grid_spec, out_shape=jax.ShapeDtypeStruct(recv.shape, out_dtype),
                          compiler_params=_cparams("parallel", "parallel"), name=name)(mc, g, recv)


def cast_into_slot(name, w, chip, dtype):
    r, c = w.shape
    tr = _row_tile(r)

    def body(c_ref, w_ref, o_ref):
        o_ref[...] = w_ref[...].astype(o_ref.dtype)

    grid_spec = pltpu.PrefetchScalarGridSpec(
        num_scalar_prefetch=1, grid=(r // tr,),
        in_specs=[pl.BlockSpec((tr, c), lambda i, cr: (i, 0))],
        out_specs=pl.BlockSpec((None, tr, c), lambda i, cr: (cr[0], i, 0)))
    return pl.pallas_call(body, grid_spec=grid_spec, out_shape=jax.ShapeDtypeStruct((N_CHIPS, r, c), dtype),
                          compiler_params=_cparams("parallel"), name=name)(chip.astype(jnp.int32).reshape(1), w)


def sum_chips_into_half(name, own, arrived, chip, mc):
    r, c = own.shape[1:]
    tr = _row_tile(r)

    def body(s0, s1, s2, s3, s4, own_ref, a_ref, b_ref, d_ref, o_ref):
        o_ref[...] = ((own_ref[...].astype(f32) + a_ref[...].astype(f32))
                      + (b_ref[...].astype(f32) + d_ref[...].astype(f32)))

    slot = lambda k: pl.BlockSpec((None, tr, c), lambda i, *sc, _k=k: (sc[_k][0], i, 0))
    grid_spec = pltpu.PrefetchScalarGridSpec(
        num_scalar_prefetch=5, grid=(r // tr,), in_specs=[slot(0), slot(1), slot(2), slot(3)],
        out_specs=pl.BlockSpec((None, tr, c), lambda i, *sc: (sc[4][0], i, 0)))
    mx, my = lax.axis_index("x"), lax.axis_index("y")
    scal = [v.astype(jnp.int32).reshape(1) for v in
            (2 * mx + my, 2 * (1 - mx) + my, 2 * mx + (1 - my), 2 * (1 - mx) + (1 - my), mc)]
    return pl.pallas_call(body, grid_spec=grid_spec, out_shape=jax.ShapeDtypeStruct((2, r, c), f32),
                          compiler_params=_cparams("parallel"), name=name)(*scal, own, arrived, arrived, arrived)


def sum_slots(name, x):
    r, c = x.shape[1:]
    tr = _row_tile(r)

    def body(x_ref, o_ref):
        o_ref[...] = (x_ref[0].astype(f32) + x_ref[1].astype(f32)) + (x_ref[2].astype(f32) + x_ref[3].astype(f32))

    return pl.pallas_call(
        body, grid=(r // tr,), in_specs=[pl.BlockSpec((N_CHIPS, tr, c), lambda i: (0, i, 0))],
        out_specs=pl.BlockSpec((tr, c), lambda i: (i, 0)), out_shape=jax.ShapeDtypeStruct((r, c), f32),
        compiler_params=_cparams("parallel"), name=name)(x)


def adamw(name, w, g, m, v):
    r, c = w.shape
    tr = _row_tile(r)

    def body(w_ref, g_ref, m_ref, v_ref, d_ref, nm_ref, nv_ref):
        gv = g_ref[...]
        nm = ADAM_B1 * m_ref[...] + (1.0 - ADAM_B1) * gv
        nv = ADAM_B2 * v_ref[...] + (1.0 - ADAM_B2) * (gv * gv)
        m_hat = nm / (1.0 - ADAM_B1 ** ADAM_STEP)
        v_hat = nv / (1.0 - ADAM_B2 ** ADAM_STEP)
        d_ref[...] = -ADAM_LR * (m_hat / (jnp.sqrt(v_hat) + ADAM_EPS) + ADAM_WD * w_ref[...])
        nm_ref[...] = nm
        nv_ref[...] = nv

    blk = pl.BlockSpec((tr, c), lambda i: (i, 0))
    return pl.pallas_call(
        body, grid=(r // tr,), in_specs=[blk] * 4, out_specs=[blk] * 3,
        out_shape=[jax.ShapeDtypeStruct((r, c), f32)] * 3,
        compiler_params=_cparams("parallel"), name=name)(w, g, m, v)


def _pack_rows(n):
    return -(-n // PACK_COLS)


def _pack(arrs, dtype, row_multiple):
    segs = []
    for a in arrs:
        flat = a.astype(dtype).reshape(-1)
        k = _pack_rows(flat.shape[0])
        segs.append(jnp.pad(flat, (0, k * PACK_COLS - flat.shape[0])).reshape(k, PACK_COLS))
    rows = sum(s.shape[0] for s in segs)
    pad = -rows % row_multiple
    if pad:
        segs.append(jnp.zeros((pad, PACK_COLS), dtype))
    return jnp.concatenate(segs, 0)


def _unpack(packed, shapes):
    out, r = [], 0
    for shp in shapes:
        n = math.prod(shp)
        k = _pack_rows(n)
        out.append(packed[r:r + k].reshape(-1)[:n].reshape(shp))
        r += k
    return out


def _gathered_to_full(g, axis):
    t = jnp.moveaxis(g, 0, axis)
    return t.reshape(t.shape[:axis] + (t.shape[axis] * t.shape[axis + 1],) + t.shape[axis + 2:])


def _full_to_shard_major(full, axis):
    shp = full.shape
    t = full.reshape(shp[:axis] + (N_CHIPS, shp[axis] // N_CHIPS) + shp[axis + 1:])
    return jnp.moveaxis(t, axis, 0)


GRAD_ROW_MULTIPLE = 256


def kernel(x, mem, positions, hyb_w_in, dn_conv_w, dn_a_log, dn_dt_bias, dn_norm_g, hyb_w_out, s5_a_re, s5_a_im, s5_log_dt, s5_b_re, s5_b_im, s5_c_re, s5_c_im, s5_d, s5_glu_wo, s5_glu_wg, ln_mix_g, ln_mix_b, xq_w, xk_w, xv_w, xo_w, ln_x_g, ln_x_b, ffn_wg, ffn_wu, ffn_wd, ln_ffn_g, ln_ffn_b, loss_target, m_hyb_w_in, m_dn_conv_w, m_dn_a_log, m_dn_dt_bias, m_dn_norm_g, m_hyb_w_out, m_s5_a_re, m_s5_a_im, m_s5_log_dt, m_s5_b_re, m_s5_b_im, m_s5_c_re, m_s5_c_im, m_s5_d, m_s5_glu_wo, m_s5_glu_wg, m_ln_mix_g, m_ln_mix_b, m_xq_w, m_xk_w, m_xv_w, m_xo_w, m_ln_x_g, m_ln_x_b, m_ffn_wg, m_ffn_wu, m_ffn_wd, m_ln_ffn_g, m_ln_ffn_b, v_hyb_w_in, v_dn_conv_w, v_dn_a_log, v_dn_dt_bias, v_dn_norm_g, v_hyb_w_out, v_s5_a_re, v_s5_a_im, v_s5_log_dt, v_s5_b_re, v_s5_b_im, v_s5_c_re, v_s5_c_im, v_s5_d, v_s5_glu_wo, v_s5_glu_wg, v_ln_mix_g, v_ln_mix_b, v_xq_w, v_xk_w, v_xv_w, v_xo_w, v_ln_x_g, v_ln_x_b, v_ffn_wg, v_ffn_wu, v_ffn_wd, v_ln_ffn_g, v_ln_ffn_b):
    a = dict(locals())
    big = [n for n in WEIGHT_NAMES if n in SHARD_AXIS and n not in GATHER_F32]
    small = [n for n in WEIGHT_NAMES if n not in big]
    chip = 2 * lax.axis_index("x") + lax.axis_index("y")
    for n in FFN_TRANSPOSED:
        for pre in ("", "m_", "v_"):
            a[pre + n] = jnp.swapaxes(a[pre + n], 1, 2)

    mc = lax.axis_index("c")
    view2 = lambda t: t.reshape(-1, t.shape[-1])
    slots = [cast_into_slot("slot_" + n, view2(a[n]), chip, bf16).reshape((N_CHIPS,) + a[n].shape) for n in big]
    gathered = comm_gather_weights("gather_w", slots)
    tiny4 = _unpack_slots(comm_allgather4("gather_w_tiny", _pack([a[n] for n in GATHER_F32], f32, 8)),
                          [a[n].shape for n in GATHER_F32])
    p = {n: a[n] for n in small if n not in GATHER_F32}
    for n, g4 in zip(GATHER_F32, tiny4):
        p[n] = _gathered_to_full(g4, SHARD_AXIS[n])
    for n, g4 in zip(big, gathered):
        p[n] = g4 if n in SHARD_ORDER_GRADS else _gathered_to_full(g4, SHARD_AXIS[n])

    loss, grad_x, grads = local_step(x[0], mem[0], positions, loss_target[0], p)
    loss = lax.psum(loss, ("x", "y", "c"))

    g4s = [grads[n] if n in SHARD_ORDER_GRADS else _full_to_shard_major(grads[n], SHARD_AXIS[n]) for n in big]
    recv = comm_sibling_halves("rs_sibling_halves", g4s)
    pairs = []
    for n, g4, r4 in zip(big, g4s, recv):
        lh, cols = g4.shape[1] // 2, g4.shape[-1]
        v4 = g4.reshape(N_CHIPS, 2, -1, cols)
        pairs.append(add_own_half("rs_add_" + n, v4, r4.reshape(N_CHIPS, -1, cols), bf16).reshape((N_CHIPS, lh) + g4.shape[2:]))
    arrived = comm_alltoall4("rs_alltoall", pairs)
    slot3 = lambda t: t.reshape(N_CHIPS, -1, t.shape[-1])
    halves = [sum_chips_into_half("rs_sum_" + n, slot3(pr), slot3(ar), chip, mc) for n, pr, ar in zip(big, pairs, arrived)]
    g_big = {n: t.reshape(a[n].shape) for n, t in zip(big, comm_sibling_join("rs_sibling_join", halves))}

    rpack = _pack([grads[n] for n in small], f32, 8)
    rpair = _add2("ar_add_sibling", rpack, comm_sibling_swap("ar_sibling_swap", rpack))
    g_small = _unpack(sum_slots("ar_sum_chips", comm_allgather4("ar_allgather", rpair)), [grads[n].shape for n in small])
    g_small = {n: (lax.dynamic_index_in_dim(_full_to_shard_major(g, SHARD_AXIS[n]), chip, 0, keepdims=False)
                   if n in SHARD_AXIS else g) for n, g in zip(small, g_small)}

    outs = {}
    for n in big:
        view = lambda t: t.reshape(-1, t.shape[-1])
        d, nm, nv = adamw("adamw_" + n, view(a[n]), view(g_big[n]), view(a["m_" + n]), view(a["v_" + n]))
        outs[n] = (g_big[n],) + tuple(t.reshape(a[n].shape) for t in (d, nm, nv))
    shapes = [a[n].shape for n in small]
    packs = [_pack([a[pre + n] for n in small], f32, 8) for pre in ("", "m_", "v_")]
    upd = adamw("adamw_small", packs[0], _pack([g_small[n] for n in small], f32, 8), packs[1], packs[2])
    for k, n in enumerate(small):
        outs[n] = (g_small[n],) + tuple(_unpack(buf, shapes)[k] for buf in upd)
    for n in FFN_TRANSPOSED:
        outs[n] = tuple(jnp.swapaxes(t, 1, 2) for t in outs[n])
    res = [loss, grad_x[None]]
    for kind in range(4):
        res += [outs[n][kind] for n in WEIGHT_NAMES]
    return tuple(res)


def _unpack_slots(gathered, shapes):
    out, r = [], 0
    for shp in shapes:
        n = math.prod(shp)
        k = _pack_rows(n)
        out.append(gathered[:, r:r + k].reshape(N_CHIPS, -1)[:, :n].reshape((N_CHIPS,) + tuple(shp)))
        r += k
    return out
```

```python
import functools
import math

import jax
import jax.numpy as jnp
from jax import lax
from jax.experimental import pallas as pl
from jax.experimental.pallas import tpu as pltpu

f32 = jnp.float32
bf16 = jnp.bfloat16

D_MODEL = 1024
DEPTH = 4
DN_HEADS = 4
DN_HEAD_DIM = 128
DN_KEY_DIM = 512
DN_QKV_DIM = 1536
DN_CONV = 4
SW_HEADS = 8
SW_HEAD_DIM = 64
SW_DIM = 512
SW_DILATIONS = (1, 4, 16)
SW_BLOCK = 128
ROPE_THETA = 10000.0
S5_GROUP = 16
S5_GROUPS = 64
S5_STATE = 64
X_HEADS = 4
X_HEAD_DIM = 256
FFN_HIDDEN = 2816
ALPHA = (2 * DEPTH) ** 0.25
LN_EPS = 1e-5
RMS_EPS = 1e-6
ADAM_LR, ADAM_B1, ADAM_B2, ADAM_EPS, ADAM_WD, ADAM_STEP = 0.001, 0.9, 0.999, 1e-08, 0.01, 10

BA_PAD = 256
PROJ_COLS = DN_QKV_DIM + DN_KEY_DIM + 3 * SW_DIM + BA_PAD
COL_Z = DN_QKV_DIM
COL_SWQ = COL_Z + DN_KEY_DIM
COL_SWK = COL_SWQ + SW_DIM
COL_SWV = COL_SWK + SW_DIM
COL_BA = COL_SWV + SW_DIM

LANES = 128
SUBLANES = 8
VMEM_LIMIT = 56 * 1024 * 1024
DN_CHUNK = 128
DN_HEADS_PER_STEP = 4


def _cparams(*sem):
    return pltpu.CompilerParams(dimension_semantics=tuple(sem), vmem_limit_bytes=VMEM_LIMIT)


def _dg(x, y, cx, cy):
    return lax.dot_general(x, y, (((cx,), (cy,)), ((), ())), preferred_element_type=f32)


@functools.partial(jax.custom_vjp, nondiff_argnums=(2, 3))
def bdot(a, b, ca, cb):
    return _dg(a.astype(bf16), b.astype(bf16), ca, cb)


def _bdot_fwd(a, b, ca, cb):
    return bdot(a, b, ca, cb), (a, b)


def _bdot_bwd(ca, cb, res, g):
    a, b = res
    g16, a16, b16 = g.astype(bf16), a.astype(bf16), b.astype(bf16)
    da = _dg(g16, b16, 1, 1 - cb) if ca == 1 else _dg(b16, g16, 1 - cb, 1)
    db = _dg(a16, g16, 1 - ca, 0) if cb == 0 else _dg(g16, a16, 0, 1 - ca)
    return da.astype(a.dtype), db.astype(b.dtype)


bdot.defvjp(_bdot_fwd, _bdot_bwd)


def _split_hi_lo(a):
    hi = a.astype(bf16)
    return hi, (a - hi.astype(f32)).astype(bf16)


def _dot3(a, b, ca, cb):
    a_hi, a_lo = _split_hi_lo(a)
    b_hi, b_lo = _split_hi_lo(b)
    return _dg(a_hi, b_hi, ca, cb) + (_dg(a_hi, b_lo, ca, cb) + _dg(a_lo, b_hi, ca, cb))


@jax.custom_vjp
def hdot3(a, b):
    return _dot3(a, b, 1, 0)


def _hdot3_fwd(a, b):
    return hdot3(a, b), (a, b)


def _hdot3_bwd(res, g):
    a, b = res
    return _dot3(g, b, 1, 1), _dot3(a, g, 0, 0)


hdot3.defvjp(_hdot3_fwd, _hdot3_bwd)


def hdot(a, b):
    return jnp.dot(a, b, precision=lax.Precision.HIGHEST, preferred_element_type=f32)


def _iota2(shape, dim):
    return lax.broadcasted_iota(jnp.int32, shape, dim)


def _row_spec(r, tm):
    if isinstance(r, tuple):
        arr, width, blk = r
        return arr, pl.BlockSpec((tm, width), lambda i, _b=blk: (i, _b))
    return r, pl.BlockSpec((tm, r.shape[1]), lambda i: (i, 0))


def _par_spec(p):
    return pl.BlockSpec(p.shape, lambda i, _n=p.ndim: (0,) * _n)


def rowmap(name, fn, rows, params, out_cols, tm, out_dtypes=None):
    arrs, specs = zip(*[_row_spec(r, tm) for r in rows])
    s = arrs[0].shape[0]
    n_in = len(rows) + len(params)
    out_dtypes = out_dtypes or [f32] * len(out_cols)

    def body(*refs):
        outs = fn(*[r[...] for r in refs[:n_in]])
        for o_ref, o in zip(refs[n_in:], outs):
            o_ref[...] = o.astype(o_ref.dtype)

    return pl.pallas_call(
        body, grid=(s // tm,),
        in_specs=list(specs) + [_par_spec(p) for p in params],
        out_specs=[pl.BlockSpec((tm, c), lambda i: (i, 0)) for c in out_cols],
        out_shape=[jax.ShapeDtypeStruct((s, c), dt) for c, dt in zip(out_cols, out_dtypes)],
        compiler_params=_cparams("parallel"), name=name)(*arrs, *params)


def rowmap_bwd(name, fn, rows, params, cts, tm, row_mask=None, par_mask=None, row_dtypes=None):
    arrs, specs = zip(*[_row_spec(r, tm) for r in rows])
    s = arrs[0].shape[0]
    ct_groups = [c if isinstance(c, list) else [c] for c in cts]
    ct_arrs, ct_specs = zip(*[_row_spec(a, tm) for grp in ct_groups for a in grp])
    cts = list(ct_arrs)
    nr, npar, nct = len(rows), len(params), len(cts)
    row_mask = row_mask or [True] * nr
    par_mask = par_mask or [True] * npar
    row_idx = [k for k in range(nr) if row_mask[k]]
    par_idx = [k for k in range(npar) if par_mask[k]]
    row_w = [specs[k].block_shape[1] for k in row_idx]

    def body(*refs):
        ins = [r[...] for r in refs[:nr + npar]]
        ct_refs = list(refs[nr + npar:nr + npar + nct])
        ctv = []
        for grp in ct_groups:
            acc = ct_refs.pop(0)[...].astype(f32)
            for _ in grp[1:]:
                acc = acc + ct_refs.pop(0)[...].astype(f32)
            ctv.append(acc)
        ctv = tuple(ctv)
        outs = refs[nr + npar + nct:]
        _, vjp = jax.vjp(fn, *ins)
        grads = vjp(ctv)
        for o_ref, k in zip(outs[:len(row_idx)], row_idx):
            o_ref[...] = grads[k].astype(o_ref.dtype)
        first = pl.program_id(0) == 0
        for o_ref, k in zip(outs[len(row_idx):], par_idx):
            g = grads[nr + k].astype(f32)

            @pl.when(first)
            def _(o_ref=o_ref, g=g):
                o_ref[...] = g

            @pl.when(jnp.logical_not(first))
            def _(o_ref=o_ref, g=g):
                o_ref[...] += g

    res = pl.pallas_call(
        body, grid=(s // tm,),
        in_specs=list(specs) + [_par_spec(p) for p in params]
        + list(ct_specs),
        out_specs=[pl.BlockSpec((tm, w), lambda i: (i, 0)) for w in row_w]
        + [_par_spec(params[k]) for k in par_idx],
        out_shape=[jax.ShapeDtypeStruct((s, w), dt) for w, dt in zip(row_w, row_dtypes or [f32] * len(row_w))]
        + [jax.ShapeDtypeStruct(params[k].shape, f32) for k in par_idx],
        compiler_params=_cparams("arbitrary"), name=name)(*arrs, *params, *cts)
    return list(res[:len(row_idx)]), list(res[len(row_idx):])


def _pick(n, prefs):
    for t in prefs:
        if n % t == 0:
            return t
    return n


MM_CHUNK = 512


def mm_nn(name, a, b, out_dtype=f32):
    m, k = a.shape
    n = b.shape[1]
    tm = _pick(m, (512, 256, 128))
    cn = _pick(n, (MM_CHUNK, 256, 128))

    def body(a_ref, b_ref, o_ref):
        av = a_ref[...].astype(bf16)
        for c in range(n // cn):
            sl = slice(c * cn, (c + 1) * cn)
            o_ref[:, sl] = _dg(av, b_ref[:, sl].astype(bf16), 1, 0).astype(o_ref.dtype)

    return pl.pallas_call(
        body, grid=(m // tm,),
        in_specs=[pl.BlockSpec((tm, k), lambda i: (i, 0)), pl.BlockSpec((k, n), lambda i: (0, 0))],
        out_specs=pl.BlockSpec((tm, n), lambda i: (i, 0)),
        out_shape=jax.ShapeDtypeStruct((m, n), out_dtype),
        compiler_params=_cparams("parallel"), name=name)(a, b)


def mm_nt(name, a, b, out_dtype=f32):
    m, n = a.shape
    k = b.shape[0]
    tm = _pick(m, (512, 256, 128))
    ck = _pick(k, (MM_CHUNK, 256, 128))

    def body(a_ref, b_ref, o_ref):
        av = a_ref[...].astype(bf16)
        for c in range(k // ck):
            sl = slice(c * ck, (c + 1) * ck)
            o_ref[:, sl] = _dg(av, b_ref[sl, :].astype(bf16), 1, 1).astype(o_ref.dtype)

    return pl.pallas_call(
        body, grid=(m // tm,),
        in_specs=[pl.BlockSpec((tm, n), lambda i: (i, 0)), pl.BlockSpec((k, n), lambda i: (0, 0))],
        out_specs=pl.BlockSpec((tm, k), lambda i: (i, 0)),
        out_shape=jax.ShapeDtypeStruct((m, k), out_dtype),
        compiler_params=_cparams("parallel"), name=name)(a, b)


def mm_tn(name, a, b, out_dtype=f32):
    s, m = a.shape
    n = b.shape[1]
    tn = _pick(n, (256, 128))
    cm = _pick(m, (256, 128))

    def body(a_ref, b_ref, o_ref):
        bv = b_ref[...].astype(bf16)
        for c in range(m // cm):
            sl = slice(c * cm, (c + 1) * cm)
            o_ref[sl, :] = _dg(a_ref[:, sl].astype(bf16), bv, 0, 0).astype(o_ref.dtype)

    return pl.pallas_call(
        body, grid=(n // tn,),
        in_specs=[pl.BlockSpec((s, m), lambda j: (0, 0)), pl.BlockSpec((s, tn), lambda j: (0, j))],
        out_specs=pl.BlockSpec((m, tn), lambda j: (0, j)),
        out_shape=jax.ShapeDtypeStruct((m, n), out_dtype),
        compiler_params=_cparams("parallel"), name=name)(a, b)


def _postnorm_tile(h, sub, g, b):
    z = ALPHA * h + sub
    mu = jnp.mean(z, -1, keepdims=True)
    zc = z - mu
    var = jnp.mean(zc * zc, -1, keepdims=True)
    return (zc * lax.rsqrt(var + LN_EPS) * g + b,)


def _swiglu_tile(au):
    a, u = au[:, :FFN_HIDDEN], au[:, FFN_HIDDEN:]
    return (jax.nn.silu(a) * u,)


def _glu_tile(og):
    o, g = og[:, :D_MODEL], og[:, D_MODEL:]
    return (o * jax.nn.sigmoid(g),)


def _xattn_tile(q, kv):
    outs = []
    for h in range(X_HEADS):
        sl = slice(h * X_HEAD_DIM, (h + 1) * X_HEAD_DIM)
        s = bdot(q[:, sl], kv[:, sl], 1, 1) * (X_HEAD_DIM ** -0.5)
        m = lax.stop_gradient(jnp.max(s, -1, keepdims=True))
        p = jnp.exp(s - m)
        p = p / jnp.sum(p, -1, keepdims=True)
        outs.append(bdot(p, kv[:, D_MODEL + h * X_HEAD_DIM:D_MODEL + (h + 1) * X_HEAD_DIM], 1, 0))
    return (jnp.concatenate(outs, -1),)


TM_ROW = 256


def postnorm_fwd(tag, h, sub, g, b):
    return rowmap("postnorm_" + tag, lambda *a: _postnorm_tile(*a) * 2, [h, sub], [g, b], [D_MODEL] * 2, TM_ROW,
                  out_dtypes=[f32, bf16])


def postnorm_bwd(tag, h, sub, g, b, dy):
    (dh, dsub), (dg, db) = rowmap_bwd("postnorm_bwd_" + tag, _postnorm_tile, [h, sub], [g, b], [dy], TM_ROW,
                                      row_dtypes=[f32, bf16])
    return dh, dsub, dg, db


def xattn_fwd(tag, h, mem, wq, wkv, wo):
    q = mm_nn("xq_" + tag, h, wq, out_dtype=bf16)
    kv = mm_nn("xkv_" + tag, mem, wkv)
    ao = rowmap("xattn_" + tag, _xattn_tile, [q], [kv], [D_MODEL], TM_ROW, out_dtypes=[bf16])[0]
    out = mm_nn("xo_" + tag, ao, wo)
    return out, (q, kv, ao)


def xattn_bwd(tag, h, mem, wq, wkv, wo, res, dout):
    q, kv, ao = res
    dwo = mm_tn("xo_dw_" + tag, ao, dout)
    dao = mm_nt("xo_dx_" + tag, dout, wo)
    (dq,), (dkv,) = rowmap_bwd("xattn_bwd_" + tag, _xattn_tile, [q], [kv], [dao], TM_ROW, row_dtypes=[bf16])
    dwq = mm_tn("xq_dw_" + tag, h, dq)
    dh = mm_nt("xq_dx_" + tag, dq, wq)
    dwkv = mm_tn("xkv_dw_" + tag, mem, dkv)
    return dh, dwq, dwkv, dwo


FFN_SHARD = FFN_HIDDEN // 4
TM_FFN = 512


def _silu_mul(a, u):
    return jax.nn.silu(a) * u


def ffn_fwd(tag, layer, h, wg, wu, wd):
    s = h.shape[0]
    tm, fs = TM_FFN, FFN_SHARD
    w_in = pl.BlockSpec((None, None, fs, D_MODEL), lambda k, i: (k, layer, 0, 0))
    act = pl.BlockSpec((None, tm, fs), lambda k, i: (k, i, 0))

    def up_body(h_ref, wg_ref, wu_ref, a_ref, u_ref, hid_ref):
        hv = h_ref[...].astype(bf16)
        a, u = _dg(hv, wg_ref[...], 1, 1), _dg(hv, wu_ref[...], 1, 1)
        a_ref[...], u_ref[...] = a.astype(bf16), u.astype(bf16)
        hid_ref[...] = _silu_mul(a, u).astype(bf16)

    a4, u4, hid4 = pl.pallas_call(
        up_body, grid=(4, s // tm),
        in_specs=[pl.BlockSpec((tm, D_MODEL), lambda k, i: (i, 0)), w_in, w_in],
        out_specs=[act, act, act],
        out_shape=[jax.ShapeDtypeStruct((4, s, fs), bf16)] * 3,
        compiler_params=_cparams("parallel", "parallel"), name="ffn_up_" + tag)(h, wg, wu)

    all_act = pl.BlockSpec((4, tm, fs), lambda i: (0, i, 0))
    all_w = pl.BlockSpec((4, None, fs, D_MODEL), lambda i: (0, layer, 0, 0))

    def down_body(hid_ref, wd_ref, o_ref):
        acc = _dg(hid_ref[0], wd_ref[0], 1, 0)
        for k in range(1, 4):
            acc = acc + _dg(hid_ref[k], wd_ref[k], 1, 0)
        o_ref[...] = acc

    out = pl.pallas_call(
        down_body, grid=(s // tm,), in_specs=[all_act, all_w],
        out_specs=pl.BlockSpec((tm, D_MODEL), lambda i: (i, 0)),
        out_shape=jax.ShapeDtypeStruct((s, D_MODEL), f32),
        compiler_params=_cparams("parallel"), name="ffn_down_" + tag)(hid4, wd)
    return out, (a4, u4, hid4)


def ffn_bwd(tag, layer, h, wg, wu, wd, res, dout):
    a4, u4, hid4 = res
    s = h.shape[0]
    tm, fs = TM_FFN, FFN_SHARD
    act = pl.BlockSpec((None, tm, fs), lambda k, i: (k, i, 0))

    def dact_body(do_ref, wd_ref, a_ref, u_ref, da_ref, du_ref):
        dhid = _dg(do_ref[...].astype(bf16), wd_ref[...], 1, 1)
        _, vjp = jax.vjp(_silu_mul, a_ref[...].astype(f32), u_ref[...].astype(f32))
        da, du = vjp(dhid)
        da_ref[...], du_ref[...] = da.astype(bf16), du.astype(bf16)

    da4, du4 = pl.pallas_call(
        dact_body, grid=(4, s // tm),
        in_specs=[pl.BlockSpec((tm, D_MODEL), lambda k, i: (i, 0)),
                  pl.BlockSpec((None, None, fs, D_MODEL), lambda k, i: (k, layer, 0, 0)), act, act],
        out_specs=[act, act], out_shape=[jax.ShapeDtypeStruct((4, s, fs), bf16)] * 2,
        compiler_params=_cparams("parallel", "parallel"), name="ffn_dact_" + tag)(dout, wd, a4, u4)

    all_act = pl.BlockSpec((4, tm, fs), lambda i: (0, i, 0))
    all_w = pl.BlockSpec((4, None, fs, D_MODEL), lambda i: (0, layer, 0, 0))

    def dx_body(da_ref, du_ref, wg_ref, wu_ref, o_ref):
        acc = _dg(da_ref[0], wg_ref[0], 1, 0) + _dg(du_ref[0], wu_ref[0], 1, 0)
        for k in range(1, 4):
            acc = acc + (_dg(da_ref[k], wg_ref[k], 1, 0) + _dg(du_ref[k], wu_ref[k], 1, 0))
        o_ref[...] = acc

    dh = pl.pallas_call(
        dx_body, grid=(s // tm,), in_specs=[all_act, all_act, all_w, all_w],
        out_specs=pl.BlockSpec((tm, D_MODEL), lambda i: (i, 0)),
        out_shape=jax.ShapeDtypeStruct((s, D_MODEL), f32),
        compiler_params=_cparams("parallel"), name="ffn_dx_" + tag)(da4, du4, wg, wu)

    tn = 256
    whole = pl.BlockSpec((None, s, fs), lambda k, j: (k, 0, 0))

    def dwin_body(h_ref, da_ref, du_ref, dwg_ref, dwu_ref):
        hv = h_ref[...].astype(bf16)
        dwg_ref[...] = _dg(da_ref[...], hv, 0, 0)
        dwu_ref[...] = _dg(du_ref[...], hv, 0, 0)

    dwg, dwu = pl.pallas_call(
        dwin_body, grid=(4, D_MODEL // tn),
        in_specs=[pl.BlockSpec((s, tn), lambda k, j: (0, j)), whole, whole],
        out_specs=[pl.BlockSpec((None, fs, tn), lambda k, j: (k, 0, j))] * 2,
        out_shape=[jax.ShapeDtypeStruct((4, fs, D_MODEL), f32)] * 2,
        compiler_params=_cparams("parallel", "parallel"), name="ffn_dwin_" + tag)(h, da4, du4)

    def dwd_body(hid_ref, do_ref, dwd_ref):
        dwd_ref[...] = _dg(hid_ref[...], do_ref[...].astype(bf16), 0, 0)

    dwd = pl.pallas_call(
        dwd_body, grid=(4, D_MODEL // tn),
        in_specs=[whole, pl.BlockSpec((s, tn), lambda k, j: (0, j))],
        out_specs=pl.BlockSpec((None, fs, tn), lambda k, j: (k, 0, j)),
        out_shape=jax.ShapeDtypeStruct((4, fs, D_MODEL), f32),
        compiler_params=_cparams("parallel", "parallel"), name="ffn_dwd_" + tag)(hid4, dout)
    return dh, dwg, dwu, dwd


def loss_head(y, target):
    s, d = y.shape
    tm = TM_ROW

    def body(y_ref, t_ref, part_ref, dy_ref):
        e = y_ref[...] - t_ref[...]
        dy_ref[...] = e * (1.0 / d)
        p = jnp.sum(e * e, 0, keepdims=True) * (0.5 / d)

        @pl.when(pl.program_id(0) == 0)
        def _():
            part_ref[...] = p

        @pl.when(pl.program_id(0) != 0)
        def _():
            part_ref[...] += p

    return pl.pallas_call(
        body, grid=(s // tm,),
        in_specs=[pl.BlockSpec((tm, d), lambda i: (i, 0))] * 2,
        out_specs=[pl.BlockSpec((1, d), lambda i: (0, 0)), pl.BlockSpec((tm, d), lambda i: (i, 0))],
        out_shape=[jax.ShapeDtypeStruct((1, d), f32), jax.ShapeDtypeStruct((s, d), f32)],
        compiler_params=_cparams("arbitrary"), name="loss_head")(y, target)


TM_CONV = 512


def _conv_rows(xx, w_ref, n_rows):
    a = w_ref[3:4, :] * xx
    for k in (1, 2, 3):
        a = a + w_ref[3 - k:4 - k, :] * pltpu.roll(xx, k, 0)
    return a


def _dn_act(a, is_qk):
    s = jax.nn.silu(a)
    n = s * lax.rsqrt(jnp.sum(s * s, -1, keepdims=True) + RMS_EPS)
    return jnp.where(is_qk, n, s)


def dn_conv_fwd(tag, proj, cw):
    s = proj.shape[0]
    tm, hb = TM_CONV, TM_CONV // SUBLANES

    def body(xh_ref, x_ref, w_ref, o_ref):
        j, t = pl.program_id(0), pl.program_id(1)
        halo = jnp.where(t > 0, xh_ref[...], 0.0)
        xx = jnp.concatenate([halo, x_ref[...]], 0)
        a = _conv_rows(xx, w_ref, tm + SUBLANES)
        o_ref[...] = _dn_act(a, j < 2 * DN_HEADS)[SUBLANES:, :]

    return pl.pallas_call(
        body, grid=(DN_QKV_DIM // LANES, s // tm),
        in_specs=[pl.BlockSpec((SUBLANES, LANES), lambda j, t: (jnp.maximum(t * hb - 1, 0), j)),
                  pl.BlockSpec((tm, LANES), lambda j, t: (t, j)),
                  pl.BlockSpec((DN_CONV, LANES), lambda j, t: (0, j))],
        out_specs=pl.BlockSpec((tm, LANES), lambda j, t: (t, j)),
        out_shape=jax.ShapeDtypeStruct((s, DN_QKV_DIM), f32),
        compiler_params=_cparams("parallel", "parallel"), name="dn_conv_" + tag)(proj, proj, cw)


def dn_conv_bwd(tag, proj, cw, dy):
    s = proj.shape[0]
    tm, hb = TM_CONV, TM_CONV // SUBLANES
    nt = s // tm
    n_ext = tm + 2 * SUBLANES

    def body(xb_ref, x_ref, xa_ref, dy_ref, dya_ref, w_ref, dx_ref, dw_ref):
        j, t = pl.program_id(0), pl.program_id(1)
        xx = jnp.concatenate([jnp.where(t > 0, xb_ref[...], 0.0), x_ref[...],
                              jnp.where(t < nt - 1, xa_ref[...], 0.0)], 0)
        dyy = jnp.concatenate([jnp.zeros((SUBLANES, LANES), f32), dy_ref[...],
                               jnp.where(t < nt - 1, dya_ref[...], 0.0)], 0)
        a = _conv_rows(xx, w_ref, n_ext)
        _, vjp = jax.vjp(lambda v: _dn_act(v, j < 2 * DN_HEADS), a)
        da, = vjp(dyy)
        dx = w_ref[3:4, :] * da
        for k in (1, 2, 3):
            dx = dx + w_ref[3 - k:4 - k, :] * pltpu.roll(da, n_ext - k, 0)
        dx_ref[...] = dx[SUBLANES:SUBLANES + tm, :]
        row = _iota2((n_ext, LANES), 0)
        da_in = jnp.where((row >= SUBLANES) & (row < SUBLANES + tm), da, 0.0)
        r8 = _iota2((SUBLANES, LANES), 0)
        dw = jnp.zeros((SUBLANES, LANES), f32)
        for k in range(DN_CONV):
            xs = xx if k == 0 else pltpu.roll(xx, k, 0)
            dw = dw + jnp.where(r8 == 3 - k, jnp.sum(da_in * xs, 0, keepdims=True), 0.0)

        @pl.when(t == 0)
        def _():
            dw_ref[...] = dw

        @pl.when(t != 0)
        def _():
            dw_ref[...] += dw

    nb8 = s // SUBLANES
    return pl.pallas_call(
        body, grid=(DN_QKV_DIM // LANES, nt),
        in_specs=[pl.BlockSpec((SUBLANES, LANES), lambda j, t: (jnp.maximum(t * hb - 1, 0), j)),
                  pl.BlockSpec((tm, LANES), lambda j, t: (t, j)),
                  pl.BlockSpec((SUBLANES, LANES), lambda j, t: (jnp.minimum((t + 1) * hb, nb8 - 1), j)),
                  pl.BlockSpec((tm, LANES), lambda j, t: (t, j)),
                  pl.BlockSpec((SUBLANES, LANES), lambda j, t: (jnp.minimum((t + 1) * hb, nb8 - 1), j)),
                  pl.BlockSpec((DN_CONV, LANES), lambda j, t: (0, j))],
        out_specs=[pl.BlockSpec((tm, LANES), lambda j, t: (t, j)),
                   pl.BlockSpec((SUBLANES, LANES), lambda j, t: (0, j))],
        out_shape=[jax.ShapeDtypeStruct((s, DN_QKV_DIM), f32), jax.ShapeDtypeStruct((SUBLANES, DN_QKV_DIM), f32)],
        compiler_params=_cparams("parallel", "arbitrary"), name="dn_conv_bwd_" + tag)(proj, proj, proj, dy, dy, cw)


def _gate_tile(ba, eb, ea, alog, dtb):
    beta = jax.nn.sigmoid(hdot(ba, eb))
    g = -jnp.exp(alog) * jax.nn.softplus(hdot(ba, ea) + dtb)
    return beta, g


def _each(fn, *lists):
    return [fn(*args) for args in zip(*lists)]


@functools.partial(jax.custom_vjp, nondiff_argnums=(1,))
def _halves(x, axis):
    h = x.shape[axis] // 2
    return (x[:h], x[h:]) if axis == 0 else (x[:, :h], x[:, h:])


def _halves_fwd(x, axis):
    return _halves(x, axis), None


def _halves_bwd(axis, _, g):
    return (jnp.concatenate(g, axis),)


_halves.defvjp(_halves_fwd, _halves_bwd)


def _tri_inv_unit(lowers):
    c = lowers[0].shape[0]
    r, col = _iota2((c, c), 0), _iota2((c, c), 1)
    eye = jnp.where(r == col, 1.0, 0.0).astype(f32)
    invs = None
    sh = 0
    while (1 << sh) < c:
        same_2b = lax.shift_right_logical(r, sh + 1) == lax.shift_right_logical(col, sh + 1)
        diff_b = lax.shift_right_logical(r, sh) != lax.shift_right_logical(col, sh)
        offs = [jnp.where(same_2b & diff_b, low, 0.0) for low in lowers]
        if invs is None:
            invs = [eye - off for off in offs]
        else:
            part = _each(hdot, invs, offs)
            invs = _each(lambda inv, p: inv - hdot(p, inv), invs, part)
        sh += 1
    return invs


@jax.custom_vjp
def _known_inverse(lower, tinv):
    return tinv


def _known_inverse_fwd(lower, tinv):
    return tinv, tinv


def _known_inverse_bwd(tinv, g):
    tt = tinv.T
    return -hdot(hdot(tt, g), tt), jnp.zeros_like(tinv)


_known_inverse.defvjp(_known_inverse_fwd, _known_inverse_bwd)


def _delta_chunk(q, k, v, gb, betab, state, tinv_known=None):
    c, hd = DN_CHUNK, DN_HEAD_DIM
    r, col = _iota2((c, c), 0), _iota2((c, c), 1)
    causal, strict = r >= col, r > col
    tril = jnp.where(causal, 1.0, 0.0).astype(f32)
    gc = _each(lambda g: hdot(tril, g), gb)
    decay = _each(lambda g: jnp.where(causal, jnp.exp(jnp.where(causal, g - g.T, 0.0)), 0.0), gc)
    qs = _each(lambda t: t * (DN_HEAD_DIM ** -0.5), q)
    kb = _each(lambda a, b: a * b, k, betab)
    kq = _each(lambda a, b, kk: _halves(bdot(jnp.concatenate([a, b], 0), kk, 1, 1), 0), kb, qs, k)
    lower = _each(lambda x, d: jnp.where(strict, x[0], 0.0) * d, kq, decay)
    intra = _each(lambda x, d: x[1] * d, kq, decay)
    tinv = _tri_inv_unit(lower) if tinv_known is None else _each(_known_inverse, lower, tinv_known)
    eg = _each(jnp.exp, gc)
    uw = _each(lambda t, vv, b, kb_, e: _halves(hdot(t, jnp.concatenate([vv * b, kb_ * e], 1)), 1),
               tinv, v, betab, kb, eg)
    gl = _each(lambda g: jnp.sum(jnp.where(r == c - 1, g, 0.0), 0, keepdims=True), gc)
    k_dec = _each(lambda kk, a, g: kk * jnp.exp(a - g), k, gl, gc)
    ws = _each(lambda x, t, e, st: _halves(bdot(jnp.concatenate([x[1], t * e], 0), st, 1, 0), 0), uw, qs, eg, state)
    v_new = _each(lambda x, y: x[0] - y[0], uw, ws)
    out = _each(lambda y, a, vn: y[1] + bdot(a, vn, 1, 0), ws, intra, v_new)
    new_state = _each(lambda st, a, kd, vn: st * jnp.exp(a) + bdot(kd, vn, 0, 0), state, gl, k_dec, v_new)
    return tuple(out), tuple(new_state), tuple(tinv)


def delta_fwd(tag, qkv, gb, betab):
    s = qkv.shape[0]
    c, hd = DN_CHUNK, DN_HEAD_DIM
    n = s // c

    hg, ng = DN_HEADS_PER_STEP, DN_HEADS // DN_HEADS_PER_STEP

    def body(q_ref, k_ref, v_ref, g_ref, b_ref, o_ref, st_ref, ti_ref, state):
        @pl.when(pl.program_id(1) == 0)
        def _():
            state[...] = jnp.zeros_like(state)

        heads = lambda ref: tuple(ref[:, j * hd:(j + 1) * hd] for j in range(hg))
        st = tuple(state[j] for j in range(hg))
        outs, news, tinv = _delta_chunk(heads(q_ref), heads(k_ref), heads(v_ref), heads(g_ref), heads(b_ref), st)
        for j in range(hg):
            st_ref[j] = st[j]
            ti_ref[j] = tinv[j]
            o_ref[:, j * hd:(j + 1) * hd] = outs[j]
            state[j] = news[j]

    blk = lambda off: pl.BlockSpec((c, hg * hd), lambda h, i, _o=off: (i, h + _o))
    per_chunk = pl.BlockSpec((hg, None, hd, hd), lambda h, i: (h, i, 0, 0))
    return pl.pallas_call(
        body, grid=(ng, n),
        in_specs=[blk(0), blk(ng), blk(2 * ng), blk(0), blk(0)],
        out_specs=[blk(0), per_chunk, per_chunk],
        out_shape=[jax.ShapeDtypeStruct((s, DN_KEY_DIM), f32)] + [jax.ShapeDtypeStruct((DN_HEADS, n, hd, hd), f32)] * 2,
        scratch_shapes=[pltpu.VMEM((hg, hd, hd), f32)],
        compiler_params=_cparams("parallel", "arbitrary"), name="delta_" + tag)(qkv, qkv, qkv, gb, betab)


def delta_bwd(tag, qkv, gb, betab, states, tinvs, do):
    s = qkv.shape[0]
    c, hd = DN_CHUNK, DN_HEAD_DIM
    n = s // c

    hg, ng = DN_HEADS_PER_STEP, DN_HEADS // DN_HEADS_PER_STEP

    def body(q_ref, k_ref, v_ref, g_ref, b_ref, st_ref, ti_ref, do_ref, dq_ref, dk_ref, dv_ref, dg_ref, db_ref, dstate):
        @pl.when(pl.program_id(1) == 0)
        def _():
            dstate[...] = jnp.zeros_like(dstate)

        heads = lambda ref: tuple(ref[:, j * hd:(j + 1) * hd] for j in range(hg))
        tinv = tuple(ti_ref[j] for j in range(hg))
        _, vjp = jax.vjp(lambda *args: _delta_chunk(*args, tinv_known=tinv)[:2],
                         heads(q_ref), heads(k_ref), heads(v_ref), heads(g_ref), heads(b_ref),
                         tuple(st_ref[j] for j in range(hg)))
        grads = vjp((heads(do_ref), tuple(dstate[j] for j in range(hg))))
        for ref, g in zip((dq_ref, dk_ref, dv_ref, dg_ref, db_ref), grads[:5]):
            for j in range(hg):
                ref[:, j * hd:(j + 1) * hd] = g[j]
        for j in range(hg):
            dstate[j] = grads[5][j]

    blk = lambda off: pl.BlockSpec((c, hg * hd), lambda h, i, _o=off: (n - 1 - i, h + _o))
    return pl.pallas_call(
        body, grid=(ng, n),
        in_specs=[blk(0), blk(ng), blk(2 * ng), blk(0), blk(0)]
        + [pl.BlockSpec((hg, None, hd, hd), lambda h, i: (h, n - 1 - i, 0, 0))] * 2 + [blk(0)],
        out_specs=[blk(0)] * 5,
        out_shape=[jax.ShapeDtypeStruct((s, DN_KEY_DIM), f32)] * 5,
        scratch_shapes=[pltpu.VMEM((hg, hd, hd), f32)],
        compiler_params=_cparams("parallel", "arbitrary"),
        name="delta_bwd_" + tag)(qkv, qkv, qkv, gb, betab, states, tinvs, do)


def _dn_out_tile(o, z, ng):
    outs = []
    for h in range(DN_HEADS):
        sl = slice(h * DN_HEAD_DIM, (h + 1) * DN_HEAD_DIM)
        oh = o[:, sl]
        nrm = oh * lax.rsqrt(jnp.mean(oh * oh, -1, keepdims=True) + RMS_EPS) * ng[:, sl]
        outs.append(nrm * jax.nn.silu(z[:, sl]))
    return (jnp.concatenate(outs, -1),)


def _head_selectors():
    r, c = _iota2((BA_PAD, DN_KEY_DIM), 0), _iota2((BA_PAD, DN_KEY_DIM), 1) // DN_HEAD_DIM
    return (r == c).astype(f32), (r == c + DN_HEADS).astype(f32)


def dn_mixer_fwd(tag, proj, cw, alog_b, dtb_b, ng_b):
    eb, ea = _head_selectors()
    ba = (proj, BA_PAD, COL_BA // BA_PAD)
    qkv = dn_conv_fwd(tag, proj, cw)
    betab, gb = rowmap("dn_gate_" + tag, _gate_tile, [ba], [eb, ea, alog_b, dtb_b], [DN_KEY_DIM] * 2, TM_ROW)
    o, states, tinvs = delta_fwd(tag, qkv, gb, betab)
    z = (proj, DN_KEY_DIM, COL_Z // DN_KEY_DIM)
    a_out = rowmap("dn_out_" + tag, _dn_out_tile, [o, z], [ng_b], [DN_KEY_DIM], TM_ROW)[0]
    return a_out, (qkv, betab, gb, o, states, tinvs)


def dn_mixer_bwd(tag, proj, cw, alog_b, dtb_b, ng_b, res, da_out):
    qkv, betab, gb, o, states, tinvs = res
    eb, ea = _head_selectors()
    ba = (proj, BA_PAD, COL_BA // BA_PAD)
    z = (proj, DN_KEY_DIM, COL_Z // DN_KEY_DIM)
    (do, dz), (dng,) = rowmap_bwd("dn_out_bwd_" + tag, _dn_out_tile, [o, z], [ng_b], [da_out], TM_ROW)
    dq, dk, dv, dgb, dbetab = delta_bwd(tag, qkv, gb, betab, states, tinvs, do)
    dqkv_raw, dcw = dn_conv_bwd(tag, proj, cw, jnp.concatenate([dq, dk, dv], 1))
    (dba,), (dalog, ddtb) = rowmap_bwd("dn_gate_bwd_" + tag, _gate_tile, [ba], [eb, ea, alog_b, dtb_b],
                                       [dbetab, dgb], TM_ROW, par_mask=[False, False, True, True])
    return dqkv_raw, dz, dba, dcw[:DN_CONV], dalog, ddtb, dng


def _swap_halves(x):
    n = x.shape[1]
    first = (_iota2((1, n), 1) % SW_HEAD_DIM) < SW_HEAD_DIM // 2
    return jnp.where(first, pltpu.roll(x, n - SW_HEAD_DIM // 2, 1), pltpu.roll(x, SW_HEAD_DIM // 2, 1))


def _rope_apply(x, cos, sin_signed):
    return x * cos + _swap_halves(x) * sin_signed


def _rope_transpose(dy, cos, sin_signed):
    return dy * cos + _swap_halves(dy * sin_signed)


def rope_tables(positions, s):
    half = SW_HEAD_DIM // 2
    inv_freq = ROPE_THETA ** (-jnp.arange(0, SW_HEAD_DIM, 2, dtype=f32) / SW_HEAD_DIM)
    ang = positions.reshape(s, 1).astype(f32) * inv_freq[None, :]
    cos, sin = jnp.cos(ang), jnp.sin(ang)
    cos_t = jnp.tile(jnp.concatenate([cos, cos], 1), (1, SW_HEADS))
    sin_t = jnp.tile(jnp.concatenate([-sin, sin], 1), (1, SW_HEADS))
    assert cos_t.shape == (s, SW_DIM) and half * 2 == SW_HEAD_DIM
    return cos_t, sin_t


def rope_fwd(tag, proj, cos, sin):
    def fn(q, k, v, c, sg):
        return _rope_apply(q, c, sg), _rope_apply(k, c, sg), v

    rows = [(proj, SW_DIM, COL_SWQ // SW_DIM), (proj, SW_DIM, COL_SWK // SW_DIM), (proj, SW_DIM, COL_SWV // SW_DIM), cos, sin]
    return rowmap("rope_" + tag, fn, rows, [], [SW_DIM] * 3, TM_ROW, out_dtypes=[bf16] * 3)


def _swa_block(q, kp, kc, vp, vc, first):
    blk = SW_BLOCK
    kk = jnp.concatenate([kp, kc], 0)
    vv = jnp.concatenate([vp, vc], 0)
    dist = (_iota2((blk, 2 * blk), 0) + blk) - _iota2((blk, 2 * blk), 1)
    kj = _iota2((blk, 2 * blk), 1)
    valid = (dist >= 0) & (dist <= blk) & ((kj >= blk) | jnp.logical_not(first))
    lane_head = _iota2((1, LANES), 1) // SW_HEAD_DIM
    outs, lses = [], []
    for p in range(SW_DIM // LANES):
        sl = slice(p * LANES, (p + 1) * LANES)
        qp, kp_, vp_ = q[:, sl], kk[:, sl], vv[:, sl]
        o_pair = jnp.zeros((blk, LANES), f32)
        l_pair = jnp.zeros((blk, LANES), f32)
        for e in range(LANES // SW_HEAD_DIM):
            msk = lane_head == e
            sc = bdot(jnp.where(msk, qp, 0.0), kp_, 1, 1) * (SW_HEAD_DIM ** -0.5)
            sc = jnp.where(valid, sc, -1e30)
            m = lax.stop_gradient(jnp.max(sc, -1, keepdims=True))
            pe = jnp.exp(sc - m)
            l = jnp.sum(pe, -1, keepdims=True)
            o = bdot(pe, vp_, 1, 0) / l
            o_pair = o_pair + jnp.where(msk, o, 0.0)
            l_pair = l_pair + jnp.where(msk, m + jnp.log(l), 0.0)
        outs.append(o_pair)
        lses.append(l_pair)
    return jnp.concatenate(outs, -1), jnp.concatenate(lses, -1)


def _swa_specs(r):
    cur = pl.BlockSpec((SW_BLOCK, SW_DIM), lambda rho, n: (n, rho))
    prev = pl.BlockSpec((SW_BLOCK, SW_DIM), lambda rho, n: (jnp.maximum(n - 1, 0), rho))
    return cur, prev


def swa_fwd(tag, r, q, k, v):
    s = q.shape[0]
    ln = s // r
    q2, k2, v2 = (t.reshape(ln, r * SW_DIM) for t in (q, k, v))
    cur, prev = _swa_specs(r)

    def body(q_ref, kp_ref, kc_ref, vp_ref, vc_ref, o_ref, l_ref):
        ins = [r[...].astype(f32) for r in (q_ref, kp_ref, kc_ref, vp_ref, vc_ref)]
        o, l = _swa_block(*ins, pl.program_id(1) == 0)
        o_ref[...] = o
        l_ref[...] = l

    o, l = pl.pallas_call(
        body, grid=(r, ln // SW_BLOCK),
        in_specs=[cur, prev, cur, prev, cur], out_specs=[cur, cur],
        out_shape=[jax.ShapeDtypeStruct((ln, r * SW_DIM), f32)] * 2,
        compiler_params=_cparams("parallel", "parallel"), name=f"swa{r}_{tag}")(q2, k2, k2, v2, v2)
    return o.reshape(s, SW_DIM), l.reshape(s, SW_DIM)


def swa_bwd(tag, r, q, k, v, do, dl):
    s = q.shape[0]
    ln = s // r
    q2, k2, v2, do2, dl2 = (t.reshape(ln, r * SW_DIM) for t in (q, k, v, do, dl))
    cur, prev = _swa_specs(r)

    def body(q_ref, kp_ref, kc_ref, vp_ref, vc_ref, do_ref, dl_ref, dq_ref, dka_ref, dkb_ref, dva_ref, dvb_ref):
        first = pl.program_id(1) == 0
        ins = [r[...].astype(f32) for r in (q_ref, kp_ref, kc_ref, vp_ref, vc_ref)]
        _, vjp = jax.vjp(lambda *a: _swa_block(*a, first), *ins)
        dq_ref[...], dka_ref[...], dkb_ref[...], dva_ref[...], dvb_ref[...] = vjp((do_ref[...], dl_ref[...]))

    outs = pl.pallas_call(
        body, grid=(r, ln // SW_BLOCK),
        in_specs=[cur, prev, cur, prev, cur, cur, cur], out_specs=[cur] * 5,
        out_shape=[jax.ShapeDtypeStruct((ln, r * SW_DIM), f32)] * 5,
        compiler_params=_cparams("parallel", "parallel"), name=f"swa{r}_bwd_{tag}")(q2, k2, k2, v2, v2, do2, dl2)
    return [t.reshape(s, SW_DIM) for t in outs]


def _combine_tile(o1, l1, o2, l2, o3, l3):
    m = lax.stop_gradient(jnp.maximum(jnp.maximum(l1, l2), l3))
    e1, e2, e3 = jnp.exp(l1 - m), jnp.exp(l2 - m), jnp.exp(l3 - m)
    return ((o1 * e1 + o2 * e2 + o3 * e3) / (e1 + e2 + e3),)


def swa_merge_bwd(tag, grads, cos, sin):
    s = cos.shape[0]
    tm = SW_BLOCK
    nt = s // tm
    here = pl.BlockSpec((tm, SW_DIM), lambda i: (i, 0))
    arrs, specs = [], []
    for r, g in zip(SW_DILATIONS, grads):
        ahead = pl.BlockSpec((tm, SW_DIM), lambda i, _r=r: (jnp.minimum(i + _r, nt - 1), 0))
        arrs += g
        specs += [here, ahead, here, ahead, here]

    def body(*refs):
        i = pl.program_id(0)
        c_ref, s_ref = refs[15], refs[16]
        dq_ref, dk_ref, dv_ref = refs[17:]
        dq = jnp.zeros((tm, SW_DIM), f32)
        dk = jnp.zeros((tm, SW_DIM), f32)
        dv = jnp.zeros((tm, SW_DIM), f32)
        for b, r in enumerate(SW_DILATIONS):
            gq, gka, gkb, gva, gvb = refs[5 * b:5 * b + 5]
            inside = i + r < nt
            dq = dq + gq[...]
            dk = dk + gkb[...] + jnp.where(inside, gka[...], 0.0)
            dv = dv + gvb[...] + jnp.where(inside, gva[...], 0.0)
        dq_ref[...] = _rope_transpose(dq, c_ref[...], s_ref[...])
        dk_ref[...] = _rope_transpose(dk, c_ref[...], s_ref[...])
        dv_ref[...] = dv

    return pl.pallas_call(
        body, grid=(nt,), in_specs=specs + [here, here], out_specs=[here] * 3,
        out_shape=[jax.ShapeDtypeStruct((s, SW_DIM), f32)] * 3,
        compiler_params=_cparams("parallel"), name="swa_merge_bwd_" + tag)(*arrs, cos, sin)


def swa_mixer_fwd(tag, proj, cos, sin):
    q, k, v = rope_fwd(tag, proj, cos, sin)
    ols = []
    for r in SW_DILATIONS:
        ols += list(swa_fwd(tag, r, q, k, v))
    b_out = rowmap("swa_comb_" + tag, _combine_tile, ols, [], [SW_DIM], TM_ROW)[0]
    return b_out, (q, k, v, ols)


def swa_mixer_bwd(tag, cos, sin, res, db_out):
    q, k, v, ols = res
    dols, _ = rowmap_bwd("swa_comb_bwd_" + tag, _combine_tile, ols, [], [db_out], TM_ROW)
    grads = [swa_bwd(tag, r, q, k, v, dols[2 * b], dols[2 * b + 1]) for b, r in enumerate(SW_DILATIONS)]
    return swa_merge_bwd(tag, grads, cos, sin)


TM_S5 = 128
S5_GPB = LANES // S5_GROUP
S5_NBLK = D_MODEL // LANES
S5_HALF = S5_GPB * S5_STATE
S5_BW = 2 * S5_HALF
S5_WIDTH = S5_NBLK * S5_BW
S5_TABW = S5_NBLK * S5_HALF


def _s5_disc_tile(a_re, a_im, log_dt, b_re, b_im, expand):
    dt = jnp.exp(log_dt)
    mag = jnp.exp(a_re * dt)
    abar_re, abar_im = mag * jnp.cos(a_im * dt), mag * jnp.sin(a_im * dt)
    n_re, n_im = abar_re - 1.0, abar_im
    den = a_re * a_re + a_im * a_im
    c_re = (n_re * a_re + n_im * a_im) / den
    c_im = (n_im * a_re - n_re * a_im) / den
    cx_re, cx_im = hdot(c_re, expand), hdot(c_im, expand)
    return abar_re, abar_im, cx_re * b_re - cx_im * b_im, cx_re * b_im + cx_im * b_re


def _s5_expand():
    return (_iota2((S5_STATE, S5_STATE * S5_GROUP), 1) // S5_GROUP == _iota2((S5_STATE, S5_STATE * S5_GROUP), 0)).astype(f32)


def s5_tables(a_re, a_im, log_dt):
    lanes = lambda v: v.reshape(1, S5_TABW)
    dt = jnp.broadcast_to(log_dt.reshape(S5_GROUPS, 1), (S5_GROUPS, S5_STATE))
    t = TM_S5

    def body(are_ref, aim_ref, ldt_ref, ar_ref, ai_ref, arr_ref, air_ref):
        dtv = jnp.exp(ldt_ref[...])
        lre, lim = are_ref[...] * dtv, aim_ref[...] * dtv
        row = _iota2((t, S5_HALF), 0)
        for asc, o_re, o_im in ((True, ar_ref, ai_ref), (False, arr_ref, air_ref)):
            n = (row + 1 if asc else t - row).astype(f32)
            mag = jnp.exp(n * lre)
            o_re[...] = mag * jnp.cos(n * lim)
            o_im[...] = mag * jnp.sin(n * lim)

    lane = pl.BlockSpec((1, S5_HALF), lambda j: (0, j))
    tab = pl.BlockSpec((t, S5_HALF), lambda j: (0, j))
    return pl.pallas_call(
        body, grid=(S5_NBLK,), in_specs=[lane] * 3, out_specs=[tab] * 4,
        out_shape=[jax.ShapeDtypeStruct((t, S5_TABW), f32)] * 4,
        compiler_params=_cparams("parallel"), name="s5_tables")(lanes(a_re), lanes(a_im), lanes(dt))


def s5_pack_weights(bbar_re, bbar_im, c_re, c_im):
    eye = jnp.eye(S5_GPB, dtype=f32)
    bb = jnp.stack([bbar_re.reshape(S5_GROUPS, S5_STATE, S5_GROUP), bbar_im.reshape(S5_GROUPS, S5_STATE, S5_GROUP)], 1)
    bb = bb.transpose(0, 3, 1, 2).reshape(S5_NBLK, S5_GPB, S5_GROUP, 2, S5_STATE)
    wb = (bb[:, :, :, :, None, :] * eye[None, :, None, None, :, None]).reshape(S5_NBLK, LANES, S5_BW)
    cc = jnp.stack([c_re, -c_im], 1)
    cc = cc.reshape(S5_NBLK, S5_GPB, 2, S5_GROUP, S5_STATE).transpose(0, 2, 1, 4, 3)
    wc = (cc[:, :, :, :, None, :] * eye[None, None, :, None, :, None]).reshape(S5_NBLK, S5_BW, LANES)
    return wb, wc


def s5_unpack_weight_grads(dwb, dwc):
    d6 = dwb.reshape(S5_NBLK, S5_GPB, S5_GROUP, 2, S5_GPB, S5_STATE)
    dbb = jnp.stack([d6[:, gl, :, :, gl, :] for gl in range(S5_GPB)])
    dbb = dbb.transpose(1, 0, 3, 4, 2).reshape(S5_GROUPS, 2, S5_STATE * S5_GROUP)
    c6 = dwc.reshape(S5_NBLK, 2, S5_GPB, S5_STATE, S5_GPB, S5_GROUP)
    dcc = jnp.stack([c6[:, :, gl, :, gl, :] for gl in range(S5_GPB)])
    dcc = dcc.transpose(1, 0, 2, 4, 3).reshape(S5_GROUPS, 2, S5_GROUP, S5_STATE)
    return dbb[:, 0], dbb[:, 1], dcc[:, 0], -dcc[:, 1]


def _s5_step_rows(t):
    d, out = 1, []
    while d < t:
        out.append(d)
        d *= 2
    return out


def s5_core_fwd(tag, u, wb, wc, a1, a2, dskip):
    s = u.shape[0]
    t = TM_S5

    def body(u_ref, wb_ref, wc_ref, ar_ref, ai_ref, d_ref, y_ref, x_ref, carry):
        @pl.when(pl.program_id(1) == 0)
        def _():
            carry[...] = jnp.zeros_like(carry)

        uv = u_ref[...]
        bu = bdot(uv, wb_ref[...], 1, 0)
        row = _iota2((t, LANES), 0)
        for c in range(S5_HALF // LANES):
            re, im = slice(c * LANES, (c + 1) * LANES), slice(S5_HALF + c * LANES, S5_HALF + (c + 1) * LANES)
            xr, xi = bu[:, re], bu[:, im]
            for d in _s5_step_rows(t):
                ar, ai = ar_ref[d - 1:d, re], ai_ref[d - 1:d, re]
                if d % SUBLANES:
                    keep = row >= d
                    sr = jnp.where(keep, pltpu.roll(xr, d, 0), 0.0)
                    si = jnp.where(keep, pltpu.roll(xi, d, 0), 0.0)
                    xr, xi = xr + ar * sr - ai * si, xi + ar * si + ai * sr
                else:
                    sr, si = xr[:t - d], xi[:t - d]
                    xr = jnp.concatenate([xr[:d], xr[d:] + (ar * sr - ai * si)], 0)
                    xi = jnp.concatenate([xi[:d], xi[d:] + (ar * si + ai * sr)], 0)
            cr, ci = carry[:, re], carry[:, im]
            ar, ai = ar_ref[:, re], ai_ref[:, re]
            x_ref[:, re] = xr + ar * cr - ai * ci
            x_ref[:, im] = xi + ar * ci + ai * cr
        carry[...] = x_ref[t - 1:t, :]
        y_ref[...] = bdot(x_ref[...], wc_ref[...], 1, 0) + d_ref[...] * uv

    tab = pl.BlockSpec((t, S5_HALF), lambda j, i: (0, j))
    return pl.pallas_call(
        body, grid=(S5_NBLK, s // t),
        in_specs=[pl.BlockSpec((t, LANES), lambda j, i: (i, j)),
                  pl.BlockSpec((None, LANES, S5_BW), lambda j, i: (j, 0, 0)),
                  pl.BlockSpec((None, S5_BW, LANES), lambda j, i: (j, 0, 0)),
                  tab, tab, pl.BlockSpec((1, LANES), lambda j, i: (0, j))],
        out_specs=[pl.BlockSpec((t, LANES), lambda j, i: (i, j)), pl.BlockSpec((t, S5_BW), lambda j, i: (i, j))],
        out_shape=[jax.ShapeDtypeStruct((s, D_MODEL), f32), jax.ShapeDtypeStruct((s, S5_WIDTH), f32)],
        scratch_shapes=[pltpu.VMEM((1, S5_BW), f32)],
        compiler_params=_cparams("parallel", "arbitrary"), name="s5_core_" + tag)(u, wb, wc, a1, a2, dskip)


def s5_core_bwd(tag, u, x, wb, wc, a1, a2, a1r, a2r, dskip, dy):
    s = u.shape[0]
    t = TM_S5
    nt = s // t
    hb = t // SUBLANES

    def body(u_ref, dy_ref, x_ref, xh_ref, wb_ref, wc_ref, ar_ref, ai_ref, arr_ref, air_ref, d_ref,
             du_ref, dwb_ref, dwc_ref, dd_ref, q1_ref, q2_ref, carry, lam_scr):
        i = pl.program_id(1)
        tt = nt - 1 - i

        @pl.when(i == 0)
        def _():
            carry[...] = jnp.zeros_like(carry)

        uv, dyv, xv = u_ref[...], dy_ref[...], x_ref[...]
        lam = bdot(dyv, wc_ref[...], 1, 1)
        row = _iota2((t, LANES), 0)
        x_last = jnp.where(tt > 0, xh_ref[SUBLANES - 1:SUBLANES, :], 0.0)
        q1s, q2s = [], []
        for c in range(S5_HALF // LANES):
            re, im = slice(c * LANES, (c + 1) * LANES), slice(S5_HALF + c * LANES, S5_HALF + (c + 1) * LANES)
            lr, li = lam[:, re], lam[:, im]
            for d in _s5_step_rows(t):
                ar, ai = ar_ref[d - 1:d, re], ai_ref[d - 1:d, re]
                if d % SUBLANES:
                    keep = row < t - d
                    sr = jnp.where(keep, pltpu.roll(lr, t - d, 0), 0.0)
                    si = jnp.where(keep, pltpu.roll(li, t - d, 0), 0.0)
                    lr, li = lr + ar * sr + ai * si, li + ar * si - ai * sr
                else:
                    sr, si = lr[d:], li[d:]
                    lr = jnp.concatenate([lr[:t - d] + (ar * sr + ai * si), lr[t - d:]], 0)
                    li = jnp.concatenate([li[:t - d] + (ar * si - ai * sr), li[t - d:]], 0)
            cr, ci = carry[:, re], carry[:, im]
            ar, ai = arr_ref[:, re], air_ref[:, re]
            lr, li = lr + ar * cr + ai * ci, li + ar * ci - ai * cr
            lam_scr[:, re] = lr
            lam_scr[:, im] = li
            pr = jnp.where(row == 0, x_last[:, re], pltpu.roll(xv[:, re], 1, 0))
            pi = jnp.where(row == 0, x_last[:, im], pltpu.roll(xv[:, im], 1, 0))
            p1, p2 = lr * pr + li * pi, li * pr - lr * pi
            q1, q2 = p1[:SUBLANES, :], p2[:SUBLANES, :]
            for k in range(1, hb):
                q1 = q1 + p1[k * SUBLANES:(k + 1) * SUBLANES, :]
                q2 = q2 + p2[k * SUBLANES:(k + 1) * SUBLANES, :]
            q1s.append(q1)
            q2s.append(q2)
        carry[...] = lam_scr[0:1, :]
        lam = lam_scr[...]
        du_ref[...] = bdot(lam, wb_ref[...], 1, 1) + d_ref[...] * dyv
        upd = [(dwb_ref, bdot(uv, lam, 0, 0)), (dwc_ref, bdot(xv, dyv, 0, 0)),
               (dd_ref, jnp.sum(dyv * uv, 0, keepdims=True)),
               (q1_ref, jnp.concatenate(q1s, 1)), (q2_ref, jnp.concatenate(q2s, 1))]

        @pl.when(i == 0)
        def _():
            for ref, val in upd:
                ref[...] = val

        @pl.when(i != 0)
        def _():
            for ref, val in upd:
                ref[...] += val

    nb8 = s // SUBLANES
    rev = lambda w: pl.BlockSpec((t, w), lambda j, i: (nt - 1 - i, j))
    tab = pl.BlockSpec((t, S5_HALF), lambda j, i: (0, j))
    return pl.pallas_call(
        body, grid=(S5_NBLK, nt),
        in_specs=[rev(LANES), rev(LANES), rev(S5_BW),
                  pl.BlockSpec((SUBLANES, S5_BW), lambda j, i: (jnp.maximum((nt - 1 - i) * hb - 1, 0), j)),
                  pl.BlockSpec((None, LANES, S5_BW), lambda j, i: (j, 0, 0)),
                  pl.BlockSpec((None, S5_BW, LANES), lambda j, i: (j, 0, 0)),
                  tab, tab, tab, tab, pl.BlockSpec((1, LANES), lambda j, i: (0, j))],
        out_specs=[rev(LANES),
                   pl.BlockSpec((None, LANES, S5_BW), lambda j, i: (j, 0, 0)),
                   pl.BlockSpec((None, S5_BW, LANES), lambda j, i: (j, 0, 0)),
                   pl.BlockSpec((1, LANES), lambda j, i: (0, j)),
                   pl.BlockSpec((SUBLANES, S5_HALF), lambda j, i: (0, j)),
                   pl.BlockSpec((SUBLANES, S5_HALF), lambda j, i: (0, j))],
        out_shape=[jax.ShapeDtypeStruct((s, D_MODEL), f32),
                   jax.ShapeDtypeStruct((S5_NBLK, LANES, S5_BW), f32),
                   jax.ShapeDtypeStruct((S5_NBLK, S5_BW, LANES), f32),
                   jax.ShapeDtypeStruct((1, D_MODEL), f32),
                   jax.ShapeDtypeStruct((SUBLANES, S5_TABW), f32),
                   jax.ShapeDtypeStruct((SUBLANES, S5_TABW), f32)],
        scratch_shapes=[pltpu.VMEM((1, S5_BW), f32), pltpu.VMEM((t, S5_BW), f32)],
        compiler_params=_cparams("parallel", "arbitrary"),
        name="s5_core_bwd_" + tag)(u, dy, x, x, wb, wc, a1, a2, a1r, a2r, dskip)


def _gelu_tile(y):
    return (jax.nn.gelu(y),)


def s5_mixer_fwd(tag, u, prm, w_og):
    a_re, a_im, log_dt, b_re, b_im, c_re, c_im, dskip = prm
    disc_in = [a_re, a_im, log_dt.reshape(S5_GROUPS, 1), b_re.reshape(S5_GROUPS, -1), b_im.reshape(S5_GROUPS, -1)]
    abar_re, abar_im, bbar_re, bbar_im = rowmap("s5_disc_" + tag, _s5_disc_tile, disc_in, [_s5_expand()],
                                                [S5_STATE, S5_STATE, S5_STATE * S5_GROUP, S5_STATE * S5_GROUP], S5_GROUPS)
    del abar_re, abar_im
    a1, a2, a1r, a2r = s5_tables(a_re, a_im, log_dt)
    wb, wc = s5_pack_weights(bbar_re, bbar_im, c_re, c_im)
    wb, wc = wb.astype(bf16), wc.astype(bf16)
    y, x = s5_core_fwd(tag, u, wb, wc, a1, a2, dskip.reshape(1, D_MODEL))
    hid = rowmap("s5_gelu_" + tag, _gelu_tile, [y], [], [D_MODEL], TM_ROW, out_dtypes=[bf16])[0]
    og = mm_nn("s5_og_" + tag, hid, w_og)
    mix = rowmap("s5_glu_" + tag, _glu_tile, [og], [], [D_MODEL], TM_ROW)[0]
    return mix, (disc_in, a1, a2, a1r, a2r, wb, wc, x, y, hid, og)


def s5_mixer_bwd(tag, u, prm, w_og, res, dmix):
    a_re, a_im, log_dt, b_re, b_im, c_re, c_im, dskip = prm
    disc_in, a1, a2, a1r, a2r, wb, wc, x, y, hid, og = res
    (dog,), _ = rowmap_bwd("s5_glu_bwd_" + tag, _glu_tile, [og], [], [dmix], TM_ROW)
    dw_og = mm_tn("s5_og_dw_" + tag, hid, dog)
    dhid = mm_nt("s5_og_dx_" + tag, dog, w_og)
    (dy,), _ = rowmap_bwd("s5_gelu_bwd_" + tag, _gelu_tile, [y], [], [dhid], TM_ROW)
    du, dwb, dwc, ddskip, q1, q2 = s5_core_bwd(tag, u, x, wb, wc, a1, a2, a1r, a2r, dskip.reshape(1, D_MODEL), dy)
    dbbar_re, dbbar_im, dc_re, dc_im = s5_unpack_weight_grads(dwb, dwc)
    dabar_re = q1.sum(0).reshape(S5_GROUPS, S5_STATE)
    dabar_im = q2.sum(0).reshape(S5_GROUPS, S5_STATE)
    grads, _ = rowmap_bwd("s5_disc_bwd_" + tag, _s5_disc_tile, disc_in, [_s5_expand()],
                          [dabar_re, dabar_im, dbbar_re, dbbar_im], S5_GROUPS, par_mask=[False])
    da_re, da_im, dlog_dt, db_re, db_im = grads
    return du, (da_re, da_im, dlog_dt.reshape(S5_GROUPS), db_re.reshape(b_re.shape), db_im.reshape(b_im.shape),
                dc_re, dc_im, ddskip.reshape(D_MODEL)), dw_og


HYB_IN = 3592
_IN_B0, _IN_SW0 = 2048, 2056


IN_SHARD = HYB_IN // 4
SHARD_ORDER_GRADS = ("hyb_w_in", "ffn_wg", "ffn_wu", "ffn_wd")
FFN_TRANSPOSED = ("ffn_wg", "ffn_wu")


def _w_in_pieces():
    runs = [(0, _IN_B0, 0), (_IN_B0, _IN_SW0, COL_BA), (_IN_SW0, HYB_IN, _IN_B0)]
    out = []
    for sh in range(4):
        lo, hi = sh * IN_SHARD, (sh + 1) * IN_SHARD
        for r_lo, r_hi, c_lo in runs:
            a, b = max(lo, r_lo), min(hi, r_hi)
            if a < b:
                out.append((sh, a - lo, b - lo, c_lo + a - r_lo))
    return out


def w_in_to_canonical(tag, layer, w4):
    tr = 128

    def body(w_ref, o_ref):
        o_ref[:, COL_BA:] = jnp.zeros((tr, BA_PAD), o_ref.dtype)
        for sh, a, b, c in _w_in_pieces():
            o_ref[:, c:c + b - a] = w_ref[sh, :, a:b]

    return pl.pallas_call(
        body, grid=(D_MODEL // tr,),
        in_specs=[pl.BlockSpec((4, None, tr, IN_SHARD), lambda i: (0, layer, i, 0))],
        out_specs=pl.BlockSpec((tr, PROJ_COLS), lambda i: (i, 0)),
        out_shape=jax.ShapeDtypeStruct((D_MODEL, PROJ_COLS), w4.dtype),
        compiler_params=_cparams("parallel"), name="w_in_canon_" + tag)(w4)


def w_in_grad_to_shards(tag, g):
    tr = 128

    def body(g_ref, o_ref):
        for sh, a, b, c in _w_in_pieces():
            o_ref[sh, :, a:b] = g_ref[:, c:c + b - a]

    return pl.pallas_call(
        body, grid=(D_MODEL // tr,),
        in_specs=[pl.BlockSpec((tr, PROJ_COLS), lambda i: (i, 0))],
        out_specs=pl.BlockSpec((4, tr, IN_SHARD), lambda i: (0, i, 0)),
        out_shape=jax.ShapeDtypeStruct((4, D_MODEL, IN_SHARD), f32),
        compiler_params=_cparams("parallel"), name="w_in_grad_shards_" + tag)(g)


def _add2(name, a, b):
    return rowmap(name, lambda p, q: (p + q,), [a, b], [], [a.shape[1]], _pick(a.shape[0], (256, 128, 64, 32, 16, 8)))[0]


def local_step(x, mem, positions, target, p):
    s = x.shape[0]
    cos, sin = rope_tables(positions, s)
    row = lambda v: v.reshape(1, -1).astype(f32)
    wg4, wu4, wd4 = (p[n].astype(bf16) for n in ("ffn_wg", "ffn_wu", "ffn_wd"))
    h = h16 = x
    tape = []
    for l in range(DEPTH):
        i, tag = l // 2, str(l)
        t = {"h0": h, "h0_16": h16}
        if l % 2 == 0:
            t["w_in"] = w_in_to_canonical(tag, i, p["hyb_w_in"].astype(bf16))
            t["w_out"] = p["hyb_w_out"][i].astype(bf16)
            t["dn_prm"] = (p["dn_conv_w"][i].astype(f32), row(jnp.repeat(p["dn_a_log"][i], DN_HEAD_DIM)),
                           row(jnp.repeat(p["dn_dt_bias"][i], DN_HEAD_DIM)), row(jnp.tile(p["dn_norm_g"][i], DN_HEADS)))
            t["proj"] = mm_nn("hyb_in_" + tag, h16, t["w_in"])
            a_out, t["dn"] = dn_mixer_fwd(tag, t["proj"], *t["dn_prm"])
            b_out, t["swa"] = swa_mixer_fwd(tag, t["proj"], cos, sin)
            t["mixed"] = jnp.concatenate([a_out, b_out], 1)
            mix = mm_nn("hyb_out_" + tag, t["mixed"], t["w_out"])
        else:
            t["s5_prm"] = tuple(p[n][i].astype(f32) for n in
                                ("s5_a_re", "s5_a_im", "s5_log_dt", "s5_b_re", "s5_b_im", "s5_c_re", "s5_c_im", "s5_d"))
            t["w_og"] = jnp.concatenate([p["s5_glu_wo"][i], p["s5_glu_wg"][i]], 1).astype(bf16)
            mix, t["s5"] = s5_mixer_fwd(tag, h, t["s5_prm"], t["w_og"])
        t["mix"] = mix
        t["ln"] = [(row(p[g][l]), row(p[b][l])) for g, b in
                   (("ln_mix_g", "ln_mix_b"), ("ln_x_g", "ln_x_b"), ("ln_ffn_g", "ln_ffn_b"))]
        t["h1"], t["h1_16"] = postnorm_fwd("mix" + tag, h, mix, *t["ln"][0])
        t["wq"], t["wo"] = p["xq_w"][l].astype(bf16), p["xo_w"][l].astype(bf16)
        t["wkv"] = jnp.concatenate([p["xk_w"][l], p["xv_w"][l]], 1).astype(bf16)
        t["xo"], t["xres"] = xattn_fwd(tag, t["h1_16"], mem, t["wq"], t["wkv"], t["wo"])
        t["h2"], t["h2_16"] = postnorm_fwd("x" + tag, t["h1"], t["xo"], *t["ln"][1])
        t["fo"], t["fres"] = ffn_fwd(tag, l, t["h2_16"], wg4, wu4, wd4)
        h, h16 = postnorm_fwd("ffn" + tag, t["h2"], t["fo"], *t["ln"][2])
        tape.append(t)

    part, dh = loss_head(h, target)
    loss = jnp.sum(part)

    g = {n: [None] * v.shape[1 if n in SHARD_ORDER_GRADS else 0] for n, v in p.items()}
    for l in reversed(range(DEPTH)):
        i, tag, t = l // 2, str(l), tape[l]
        dh2a, dfo, dg, db = postnorm_bwd("ffn" + tag, t["h2"], t["fo"], *t["ln"][2], dh)
        g["ln_ffn_g"][l], g["ln_ffn_b"][l] = dg[0], db[0]
        dh2b, g["ffn_wg"][l], g["ffn_wu"][l], g["ffn_wd"][l] = ffn_bwd(tag, l, t["h2_16"], wg4, wu4, wd4, t["fres"], dfo)
        dh1a, dxo, dg, db = postnorm_bwd("x" + tag, t["h1"], t["xo"], *t["ln"][1], [dh2a, dh2b])
        g["ln_x_g"][l], g["ln_x_b"][l] = dg[0], db[0]
        dh1b, g["xq_w"][l], dwkv, g["xo_w"][l] = xattn_bwd(tag, t["h1_16"], mem, t["wq"], t["wkv"], t["wo"], t["xres"], dxo)
        g["xk_w"][l], g["xv_w"][l] = dwkv[:, :D_MODEL], dwkv[:, D_MODEL:]
        dh0a, dmix, dg, db = postnorm_bwd("mix" + tag, t["h0"], t["mix"], *t["ln"][0], [dh1a, dh1b])
        g["ln_mix_g"][l], g["ln_mix_b"][l] = dg[0], db[0]
        if l % 2 == 0:
            g["hyb_w_out"][i] = mm_tn("hyb_out_dw_" + tag, t["mixed"], dmix)
            dmixed = mm_nt("hyb_out_dx_" + tag, dmix, t["w_out"])
            dqkv, dz, dba, dcw, dalog, ddtb, dng = dn_mixer_bwd(tag, t["proj"], *t["dn_prm"], t["dn"], (dmixed, DN_KEY_DIM, 0))
            g["dn_conv_w"][i] = dcw
            g["dn_a_log"][i] = dalog.reshape(DN_HEADS, DN_HEAD_DIM).sum(1)
            g["dn_dt_bias"][i] = ddtb.reshape(DN_HEADS, DN_HEAD_DIM).sum(1)
            g["dn_norm_g"][i] = dng.reshape(DN_HEADS, DN_HEAD_DIM).sum(0)
            dq, dk, dv = swa_mixer_bwd(tag, cos, sin, t["swa"], (dmixed, SW_DIM, 1))
            dproj = jnp.concatenate([dqkv, dz, dq, dk, dv, dba], 1)
            g["hyb_w_in"][i] = w_in_grad_to_shards(tag, mm_tn("hyb_in_dw_" + tag, t["h0_16"], dproj))
            dh0b = mm_nt("hyb_in_dx_" + tag, dproj, t["w_in"])
        else:
            dh0b, dprm, dw_og = s5_mixer_bwd(tag, t["h0"], t["s5_prm"], t["w_og"], t["s5"], dmix)
            for n, v in zip(("s5_a_re", "s5_a_im", "s5_log_dt", "s5_b_re", "s5_b_im", "s5_c_re", "s5_c_im", "s5_d"), dprm):
                g[n][i] = v
            g["s5_glu_wo"][i], g["s5_glu_wg"][i] = dw_og[:, :D_MODEL], dw_og[:, D_MODEL:]
        dh = [dh0a, dh0b]
    grad_x = _add2("grad_x", dh[0], dh[1])
    grads = {n: jnp.stack(v, 1 if n in SHARD_ORDER_GRADS else 0) for n, v in g.items()}
    return loss, grad_x, grads


WEIGHT_NAMES = ("hyb_w_in", "dn_conv_w", "dn_a_log", "dn_dt_bias", "dn_norm_g", "hyb_w_out", "s5_a_re", "s5_a_im",
                "s5_log_dt", "s5_b_re", "s5_b_im", "s5_c_re", "s5_c_im", "s5_d", "s5_glu_wo", "s5_glu_wg",
                "ln_mix_g", "ln_mix_b", "xq_w", "xk_w", "xv_w", "xo_w", "ln_x_g", "ln_x_b",
                "ffn_wg", "ffn_wu", "ffn_wd", "ln_ffn_g", "ln_ffn_b")
SHARD_AXIS = {"hyb_w_in": 2, "dn_conv_w": 2, "hyb_w_out": 1, "s5_d": 1, "s5_glu_wo": 1, "s5_glu_wg": 1,
              "xq_w": 1, "xk_w": 1, "xv_w": 1, "xo_w": 1, "ffn_wg": 2, "ffn_wu": 2, "ffn_wd": 1}
GATHER_F32 = ("dn_conv_w", "s5_d")
N_CHIPS = 4
PACK_COLS = 1024
_ANY = pl.BlockSpec(memory_space=pl.ANY)


def _pos():
    return lax.axis_index("x"), lax.axis_index("y"), lax.axis_index("c")


def _chip_peers(mx, my):
    return [(1 - mx, my), (mx, 1 - my), (1 - mx, 1 - my)]


def _rcopy(src, dst, ssem, rsem, dev):
    return pltpu.make_async_remote_copy(src_ref=src, dst_ref=dst, send_sem=ssem, recv_sem=rsem,
                                        device_id=dev, device_id_type=pl.DeviceIdType.MESH)


def comm_allgather4(name, x):
    def body(x_ref, o_ref, ssem, rsem, lsem):
        mx, my, mc = _pos()
        me = 2 * mx + my
        peers = _chip_peers(mx, my)
        loc = pltpu.make_async_copy(x_ref, o_ref.at[me], lsem)
        loc.start()
        sends = [_rcopy(x_ref, o_ref.at[me], ssem.at[k], rsem.at[k], (px, py, mc)) for k, (px, py) in enumerate(peers)]
        for cp in sends:
            cp.start()
        for k, (px, py) in enumerate(peers):
            _rcopy(x_ref, o_ref.at[2 * px + py], ssem.at[k], rsem.at[k], (px, py, mc)).wait_recv()
        for cp in sends:
            cp.wait_send()
        loc.wait()

    return pl.pallas_call(
        body, out_shape=jax.ShapeDtypeStruct((N_CHIPS,) + x.shape, x.dtype), in_specs=[_ANY], out_specs=_ANY,
        scratch_shapes=[pltpu.SemaphoreType.DMA((3,)), pltpu.SemaphoreType.DMA((3,)), pltpu.SemaphoreType.DMA],
        name=name)(x)


def _multi_call(name, body, ins, out_shapes, sems, in_place=False):
    return pl.pallas_call(
        body, out_shape=out_shapes, in_specs=[_ANY] * len(ins), out_specs=[_ANY] * len(out_shapes),
        scratch_shapes=sems, input_output_aliases={w: w for w in range(len(ins))} if in_place else {},
        name=name)(*ins)


def comm_gather_weights(name, slots):
    n = len(slots)

    def body(*refs):
        os_ = refs[n:2 * n]
        ssem, rsem, fssem, frsem = refs[2 * n:]
        mx, my, mc = _pos()
        me = 2 * mx + my
        peers = _chip_peers(mx, my)
        sib = (mx, my, 1 - mc)
        half = [o.shape[1] // 2 for o in os_]
        mine = [pl.ds(mc * h, h) for h in half]
        other = [pl.ds((1 - mc) * h, h) for h in half]
        sends = [_rcopy(os_[w].at[me, mine[w]], os_[w].at[me, mine[w]], ssem.at[w, k], rsem.at[w, k], (px, py, mc))
                 for w in range(n) for k, (px, py) in enumerate(peers)]
        for cp in sends:
            cp.start()
        fwds = []
        for w in range(n):
            for k, (px, py) in enumerate(peers):
                landed = os_[w].at[2 * px + py, mine[w]]
                _rcopy(landed, landed, ssem.at[w, k], rsem.at[w, k], (px, py, mc)).wait_recv()
                fw = _rcopy(landed, landed, fssem.at[w, k], frsem.at[w, k], sib)
                fw.start()
                fwds.append(fw)
        for w in range(n):
            for k, (px, py) in enumerate(peers):
                theirs = os_[w].at[2 * px + py, other[w]]
                _rcopy(theirs, theirs, fssem.at[w, k], frsem.at[w, k], sib).wait_recv()
        for cp in sends + fwds:
            cp.wait_send()

    dma = pltpu.SemaphoreType.DMA
    return _multi_call(name, body, slots, [jax.ShapeDtypeStruct(x.shape, x.dtype) for x in slots],
                       [dma((n, 3)), dma((n, 3)), dma((n, 3)), dma((n, 3))], in_place=True)


def comm_sibling_halves(name, gs):
    n = len(gs)

    def body(*refs):
        xs, os_ = refs[:n], refs[n:2 * n]
        ssem, rsem = refs[2 * n:]
        mx, my, mc = _pos()
        sib = (mx, my, 1 - mc)
        sends = []
        for w in range(n):
            h = xs[w].shape[1] // 2
            for j in range(N_CHIPS):
                sends.append(_rcopy(xs[w].at[j, pl.ds((1 - mc) * h, h)], os_[w].at[j], ssem.at[w, j], rsem.at[w, j], sib))
        for cp in sends:
            cp.start()
        for w in range(n):
            for j in range(N_CHIPS):
                _rcopy(os_[w].at[j], os_[w].at[j], ssem.at[w, j], rsem.at[w, j], sib).wait_recv()
        for cp in sends:
            cp.wait_send()

    dma = pltpu.SemaphoreType.DMA
    return _multi_call(name, body, gs,
                       [jax.ShapeDtypeStruct((N_CHIPS, g.shape[1] // 2) + g.shape[2:], g.dtype) for g in gs],
                       [dma((n, N_CHIPS)), dma((n, N_CHIPS))])


def comm_alltoall4(name, xs):
    n = len(xs)

    def body(*refs):
        xr, os_ = refs[:n], refs[n:2 * n]
        ssem, rsem = refs[2 * n:]
        mx, my, mc = _pos()
        me = 2 * mx + my
        peers = _chip_peers(mx, my)
        sends = [_rcopy(xr[w].at[2 * px + py], os_[w].at[me], ssem.at[w, k], rsem.at[w, k], (px, py, mc))
                 for w in range(n) for k, (px, py) in enumerate(peers)]
        for cp in sends:
            cp.start()
        for w in range(n):
            for k, (px, py) in enumerate(peers):
                dst = os_[w].at[2 * px + py]
                _rcopy(dst, dst, ssem.at[w, k], rsem.at[w, k], (px, py, mc)).wait_recv()
        for cp in sends:
            cp.wait_send()

    dma = pltpu.SemaphoreType.DMA
    return _multi_call(name, body, xs, [jax.ShapeDtypeStruct(x.shape, x.dtype) for x in xs], [dma((n, 3)), dma((n, 3))])


def comm_sibling_join(name, bs):
    n = len(bs)

    def body(*refs):
        os_ = refs[n:2 * n]
        ssem, rsem = refs[2 * n:]
        mx, my, mc = _pos()
        sib = (mx, my, 1 - mc)
        sends = [_rcopy(os_[w].at[mc], os_[w].at[mc], ssem.at[w], rsem.at[w], sib) for w in range(n)]
        for cp in sends:
            cp.start()
        for w in range(n):
            dst = os_[w].at[1 - mc]
            _rcopy(dst, dst, ssem.at[w], rsem.at[w], sib).wait_recv()
        for cp in sends:
            cp.wait_send()

    dma = pltpu.SemaphoreType.DMA
    return _multi_call(name, body, bs, [jax.ShapeDtypeStruct(b.shape, b.dtype) for b in bs], [dma((n,)), dma((n,))],
                       in_place=True)


def comm_sibling_swap(name, x):
    def body(x_ref, o_ref, ssem, rsem):
        mx, my, mc = _pos()
        cp = _rcopy(x_ref, o_ref, ssem, rsem, (mx, my, 1 - mc))
        cp.start()
        cp.wait_recv()
        cp.wait_send()

    return pl.pallas_call(
        body, out_shape=jax.ShapeDtypeStruct(x.shape, x.dtype), in_specs=[_ANY], out_specs=_ANY,
        scratch_shapes=[pltpu.SemaphoreType.DMA, pltpu.SemaphoreType.DMA], name=name)(x)


def _row_tile(r):
    return _pick(r, (256, 128, 64, 32, 16, 8))


def add_own_half(name, g, recv, out_dtype):
    r, c = g.shape[2:]
    tr = _row_tile(r)
    mc = lax.axis_index("c").astype(jnp.int32).reshape(1)

    def body(c_ref, g_ref, r_ref, o_ref):
        o_ref[...] = (g_ref[...] + r_ref[...]).astype(o_ref.dtype)

    grid_spec = pltpu.PrefetchScalarGridSpec(
        num_scalar_prefetch=1, grid=(N_CHIPS, r // tr),
        in_specs=[pl.BlockSpec((None, None, tr, c), lambda j, i, cr: (j, cr[0], i, 0)),
                  pl.BlockSpec((None, tr, c), lambda j, i, cr: (j, i, 0))],
        out_specs=pl.BlockSpec((None, tr, c), lambda j, i, cr: (j, i, 0)))
    return pl.pallas_call(body, grid_spec=grid_spec, out_shape=jax.ShapeDtypeStruct(recv.shape, out_dtype),
                          compiler_params=_cparams("parallel", "parallel"), name=name)(mc, g, recv)


def cast_into_slot(name, w, chip, dtype):
    r, c = w.shape
    tr = _row_tile(r)

    def body(c_ref, w_ref, o_ref):
        o_ref[...] = w_ref[...].astype(o_ref.dtype)

    grid_spec = pltpu.PrefetchScalarGridSpec(
        num_scalar_prefetch=1, grid=(r // tr,),
        in_specs=[pl.BlockSpec((tr, c), lambda i, cr: (i, 0))],
        out_specs=pl.BlockSpec((None, tr, c), lambda i, cr: (cr[0], i, 0)))
    return pl.pallas_call(body, grid_spec=grid_spec, out_shape=jax.ShapeDtypeStruct((N_CHIPS, r, c), dtype),
                          compiler_params=_cparams("parallel"), name=name)(chip.astype(jnp.int32).reshape(1), w)


def sum_chips_into_half(name, own, arrived, chip, mc):
    r, c = own.shape[1:]
    tr = _row_tile(r)

    def body(s0, s1, s2, s3, s4, own_ref, a_ref, b_ref, d_ref, o_ref):
        o_ref[...] = ((own_ref[...].astype(f32) + a_ref[...].astype(f32))
                      + (b_ref[...].astype(f32) + d_ref[...].astype(f32)))

    slot = lambda k: pl.BlockSpec((None, tr, c), lambda i, *sc, _k=k: (sc[_k][0], i, 0))
    grid_spec = pltpu.PrefetchScalarGridSpec(
        num_scalar_prefetch=5, grid=(r // tr,), in_specs=[slot(0), slot(1), slot(2), slot(3)],
        out_specs=pl.BlockSpec((None, tr, c), lambda i, *sc: (sc[4][0], i, 0)))
    mx, my = lax.axis_index("x"), lax.axis_index("y")
    scal = [v.astype(jnp.int32).reshape(1) for v in
            (2 * mx + my, 2 * (1 - mx) + my, 2 * mx + (1 - my), 2 * (1 - mx) + (1 - my), mc)]
    return pl.pallas_call(body, grid_spec=grid_spec, out_shape=jax.ShapeDtypeStruct((2, r, c), f32),
                          compiler_params=_cparams("parallel"), name=name)(*scal, own, arrived, arrived, arrived)


def sum_slots(name, x):
    r, c = x.shape[1:]
    tr = _row_tile(r)

    def body(x_ref, o_ref):
        o_ref[...] = (x_ref[0].astype(f32) + x_ref[1].astype(f32)) + (x_ref[2].astype(f32) + x_ref[3].astype(f32))

    return pl.pallas_call(
        body, grid=(r // tr,), in_specs=[pl.BlockSpec((N_CHIPS, tr, c), lambda i: (0, i, 0))],
        out_specs=pl.BlockSpec((tr, c), lambda i: (i, 0)), out_shape=jax.ShapeDtypeStruct((r, c), f32),
        compiler_params=_cparams("parallel"), name=name)(x)


def adamw(name, w, g, m, v):
    r, c = w.shape
    tr = _row_tile(r)

    def body(w_ref, g_ref, m_ref, v_ref, d_ref, nm_ref, nv_ref):
        gv = g_ref[...]
        nm = ADAM_B1 * m_ref[...] + (1.0 - ADAM_B1) * gv
        nv = ADAM_B2 * v_ref[...] + (1.0 - ADAM_B2) * (gv * gv)
        m_hat = nm / (1.0 - ADAM_B1 ** ADAM_STEP)
        v_hat = nv / (1.0 - ADAM_B2 ** ADAM_STEP)
        d_ref[...] = -ADAM_LR * (m_hat / (jnp.sqrt(v_hat) + ADAM_EPS) + ADAM_WD * w_ref[...])
        nm_ref[...] = nm
        nv_ref[...] = nv

    blk = pl.BlockSpec((tr, c), lambda i: (i, 0))
    return pl.pallas_call(
        body, grid=(r // tr,), in_specs=[blk] * 4, out_specs=[blk] * 3,
        out_shape=[jax.ShapeDtypeStruct((r, c), f32)] * 3,
        compiler_params=_cparams("parallel"), name=name)(w, g, m, v)


def _pack_rows(n):
    return -(-n // PACK_COLS)


def _pack(arrs, dtype, row_multiple):
    segs = []
    for a in arrs:
        flat = a.astype(dtype).reshape(-1)
        k = _pack_rows(flat.shape[0])
        segs.append(jnp.pad(flat, (0, k * PACK_COLS - flat.shape[0])).reshape(k, PACK_COLS))
    rows = sum(s.shape[0] for s in segs)
    pad = -rows % row_multiple
    if pad:
        segs.append(jnp.zeros((pad, PACK_COLS), dtype))
    return jnp.concatenate(segs, 0)


def _unpack(packed, shapes):
    out, r = [], 0
    for shp in shapes:
        n = math.prod(shp)
        k = _pack_rows(n)
        out.append(packed[r:r + k].reshape(-1)[:n].reshape(shp))
        r += k
    return out


def _gathered_to_full(g, axis):
    t = jnp.moveaxis(g, 0, axis)
    return t.reshape(t.shape[:axis] + (t.shape[axis] * t.shape[axis + 1],) + t.shape[axis + 2:])


def _full_to_shard_major(full, axis):
    shp = full.shape
    t = full.reshape(shp[:axis] + (N_CHIPS, shp[axis] // N_CHIPS) + shp[axis + 1:])
    return jnp.moveaxis(t, axis, 0)


GRAD_ROW_MULTIPLE = 256


def kernel(x, mem, positions, hyb_w_in, dn_conv_w, dn_a_log, dn_dt_bias, dn_norm_g, hyb_w_out, s5_a_re, s5_a_im, s5_log_dt, s5_b_re, s5_b_im, s5_c_re, s5_c_im, s5_d, s5_glu_wo, s5_glu_wg, ln_mix_g, ln_mix_b, xq_w, xk_w, xv_w, xo_w, ln_x_g, ln_x_b, ffn_wg, ffn_wu, ffn_wd, ln_ffn_g, ln_ffn_b, loss_target, m_hyb_w_in, m_dn_conv_w, m_dn_a_log, m_dn_dt_bias, m_dn_norm_g, m_hyb_w_out, m_s5_a_re, m_s5_a_im, m_s5_log_dt, m_s5_b_re, m_s5_b_im, m_s5_c_re, m_s5_c_im, m_s5_d, m_s5_glu_wo, m_s5_glu_wg, m_ln_mix_g, m_ln_mix_b, m_xq_w, m_xk_w, m_xv_w, m_xo_w, m_ln_x_g, m_ln_x_b, m_ffn_wg, m_ffn_wu, m_ffn_wd, m_ln_ffn_g, m_ln_ffn_b, v_hyb_w_in, v_dn_conv_w, v_dn_a_log, v_dn_dt_bias, v_dn_norm_g, v_hyb_w_out, v_s5_a_re, v_s5_a_im, v_s5_log_dt, v_s5_b_re, v_s5_b_im, v_s5_c_re, v_s5_c_im, v_s5_d, v_s5_glu_wo, v_s5_glu_wg, v_ln_mix_g, v_ln_mix_b, v_xq_w, v_xk_w, v_xv_w, v_xo_w, v_ln_x_g, v_ln_x_b, v_ffn_wg, v_ffn_wu, v_ffn_wd, v_ln_ffn_g, v_ln_ffn_b):
    a = dict(locals())
    big = [n for n in WEIGHT_NAMES if n in SHARD_AXIS and n not in GATHER_F32]
    small = [n for n in WEIGHT_NAMES if n not in big]
    chip = 2 * lax.axis_index("x") + lax.axis_index("y")
    for n in FFN_TRANSPOSED:
        for pre in ("", "m_", "v_"):
            a[pre + n] = jnp.swapaxes(a[pre + n], 1, 2)

    mc = lax.axis_index("c")
    view2 = lambda t: t.reshape(-1, t.shape[-1])
    slots = [cast_into_slot("slot_" + n, view2(a[n]), chip, bf16).reshape((N_CHIPS,) + a[n].shape) for n in big]
    gathered = comm_gather_weights("gather_w", slots)
    tiny4 = _unpack_slots(comm_allgather4("gather_w_tiny", _pack([a[n] for n in GATHER_F32], f32, 8)),
                          [a[n].shape for n in GATHER_F32])
    p = {n: a[n] for n in small if n not in GATHER_F32}
    for n, g4 in zip(GATHER_F32, tiny4):
        p[n] = _gathered_to_full(g4, SHARD_AXIS[n])
    for n, g4 in zip(big, gathered):
        p[n] = g4 if n in SHARD_ORDER_GRADS else _gathered_to_full(g4, SHARD_AXIS[n])

    loss, grad_x, grads = local_step(x[0], mem[0], positions, loss_target[0], p)
    loss = lax.psum(loss, ("x", "y", "c"))

    g4s = [grads[n] if n in SHARD_ORDER_GRADS else _full_to_shard_major(grads[n], SHARD_AXIS[n]) for n in big]
    recv = comm_sibling_halves("rs_sibling_halves", g4s)
    pairs = []
    for n, g4, r4 in zip(big, g4s, recv):
        lh, cols = g4.shape[1] // 2, g4.shape[-1]
        v4 = g4.reshape(N_CHIPS, 2, -1, cols)
        pairs.append(add_own_half("rs_add_" + n, v4, r4.reshape(N_CHIPS, -1, cols), bf16).reshape((N_CHIPS, lh) + g4.shape[2:]))
    arrived = comm_alltoall4("rs_alltoall", pairs)
    slot3 = lambda t: t.reshape(N_CHIPS, -1, t.shape[-1])
    halves = [sum_chips_into_half("rs_sum_" + n, slot3(pr), slot3(ar), chip, mc) for n, pr, ar in zip(big, pairs, arrived)]
    g_big = {n: t.reshape(a[n].shape) for n, t in zip(big, comm_sibling_join("rs_sibling_join", halves))}

    rpack = _pack([grads[n] for n in small], f32, 8)
    rpair = _add2("ar_add_sibling", rpack, comm_sibling_swap("ar_sibling_swap", rpack))
    g_small = _unpack(sum_slots("ar_sum_chips", comm_allgather4("ar_allgather", rpair)), [grads[n].shape for n in small])
    g_small = {n: (lax.dynamic_index_in_dim(_full_to_shard_major(g, SHARD_AXIS[n]), chip, 0, keepdims=False)
                   if n in SHARD_AXIS else g) for n, g in zip(small, g_small)}

    outs = {}
    for n in big:
        view = lambda t: t.reshape(-1, t.shape[-1])
        d, nm, nv = adamw("adamw_" + n, view(a[n]), view(g_big[n]), view(a["m_" + n]), view(a["v_" + n]))
        outs[n] = (g_big[n],) + tuple(t.reshape(a[n].shape) for t in (d, nm, nv))
    shapes = [a[n].shape for n in small]
    packs = [_pack([a[pre + n] for n in small], f32, 8) for pre in ("", "m_", "v_")]
    upd = adamw("adamw_small", packs[0], _pack([g_small[n] for n in small], f32, 8), packs[1], packs[2])
    for k, n in enumerate(small):
        outs[n] = (g_small[n],) + tuple(_unpack(buf, shapes)[k] for buf in upd)
    for n in FFN_TRANSPOSED:
        outs[n] = tuple(jnp.swapaxes(t, 1, 2) for t in outs[n])
    res = [loss, grad_x[None]]
    for kind in range(4):
        res += [outs[n][kind] for n in WEIGHT_NAMES]
    return tuple(res)


def _unpack_slots(gathered, shapes):
    out, r = [], 0
    for shp in shapes:
        n = math.prod(shp)
        k = _pack_rows(n)
        out.append(gathered[:, r:r + k].reshape(N_CHIPS, -1)[:, :n].reshape((N_CHIPS,) + tuple(shp)))
        r += k
    return out
```

```python
import functools
import math

import jax
import jax.numpy as jnp
from jax import lax
from jax.experimental import pallas as pl
from jax.experimental.pallas import tpu as pltpu

f32 = jnp.float32
bf16 = jnp.bfloat16

D_MODEL = 1024
DEPTH = 4
DN_HEADS = 4
DN_HEAD_DIM = 128
DN_KEY_DIM = 512
DN_QKV_DIM = 1536
DN_CONV = 4
SW_HEADS = 8
SW_HEAD_DIM = 64
SW_DIM = 512
SW_DILATIONS = (1, 4, 16)
SW_BLOCK = 128
ROPE_THETA = 10000.0
S5_GROUP = 16
S5_GROUPS = 64
S5_STATE = 64
X_HEADS = 4
X_HEAD_DIM = 256
FFN_HIDDEN = 2816
ALPHA = (2 * DEPTH) ** 0.25
LN_EPS = 1e-5
RMS_EPS = 1e-6
ADAM_LR, ADAM_B1, ADAM_B2, ADAM_EPS, ADAM_WD, ADAM_STEP = 0.001, 0.9, 0.999, 1e-08, 0.01, 10

BA_PAD = 256
PROJ_COLS = DN_QKV_DIM + DN_KEY_DIM + 3 * SW_DIM + BA_PAD
COL_Z = DN_QKV_DIM
COL_SWQ = COL_Z + DN_KEY_DIM
COL_SWK = COL_SWQ + SW_DIM
COL_SWV = COL_SWK + SW_DIM
COL_BA = COL_SWV + SW_DIM

LANES = 128
SUBLANES = 8
VMEM_LIMIT = 56 * 1024 * 1024
DN_CHUNK = 128
DN_HEADS_PER_STEP = 4


def _cparams(*sem):
    return pltpu.CompilerParams(dimension_semantics=tuple(sem), vmem_limit_bytes=VMEM_LIMIT)


def _dg(x, y, cx, cy):
    return lax.dot_general(x, y, (((cx,), (cy,)), ((), ())), preferred_element_type=f32)


@functools.partial(jax.custom_vjp, nondiff_argnums=(2, 3))
def bdot(a, b, ca, cb):
    return _dg(a.astype(bf16), b.astype(bf16), ca, cb)


def _bdot_fwd(a, b, ca, cb):
    return bdot(a, b, ca, cb), (a, b)


def _bdot_bwd(ca, cb, res, g):
    a, b = res
    g16, a16, b16 = g.astype(bf16), a.astype(bf16), b.astype(bf16)
    da = _dg(g16, b16, 1, 1 - cb) if ca == 1 else _dg(b16, g16, 1 - cb, 1)
    db = _dg(a16, g16, 1 - ca, 0) if cb == 0 else _dg(g16, a16, 0, 1 - ca)
    return da.astype(a.dtype), db.astype(b.dtype)


bdot.defvjp(_bdot_fwd, _bdot_bwd)


def _split_hi_lo(a):
    hi = a.astype(bf16)
    return hi, (a - hi.astype(f32)).astype(bf16)


def _dot3(a, b, ca, cb):
    a_hi, a_lo = _split_hi_lo(a)
    b_hi, b_lo = _split_hi_lo(b)
    return _dg(a_hi, b_hi, ca, cb) + (_dg(a_hi, b_lo, ca, cb) + _dg(a_lo, b_hi, ca, cb))


def hdot(a, b):
    return jnp.dot(a, b, precision=lax.Precision.HIGHEST, preferred_element_type=f32)


def _iota2(shape, dim):
    return lax.broadcasted_iota(jnp.int32, shape, dim)


def _row_spec(r, tm):
    if isinstance(r, tuple):
        arr, width, blk = r
        return arr, pl.BlockSpec((tm, width), lambda i, _b=blk: (i, _b))
    return r, pl.BlockSpec((tm, r.shape[1]), lambda i: (i, 0))


def _par_spec(p):
    return pl.BlockSpec(p.shape, lambda i, _n=p.ndim: (0,) * _n)


def rowmap(name, fn, rows, params, out_cols, tm, out_dtypes=None):
    arrs, specs = zip(*[_row_spec(r, tm) for r in rows])
    s = arrs[0].shape[0]
    n_in = len(rows) + len(params)
    out_dtypes = out_dtypes or [f32] * len(out_cols)

    def body(*refs):
        outs = fn(*[r[...] for r in refs[:n_in]])
        for o_ref, o in zip(refs[n_in:], outs):
            o_ref[...] = o.astype(o_ref.dtype)

    return pl.pallas_call(
        body, grid=(s // tm,),
        in_specs=list(specs) + [_par_spec(p) for p in params],
        out_specs=[pl.BlockSpec((tm, c), lambda i: (i, 0)) for c in out_cols],
        out_shape=[jax.ShapeDtypeStruct((s, c), dt) for c, dt in zip(out_cols, out_dtypes)],
        compiler_params=_cparams("parallel"), name=name)(*arrs, *params)


def rowmap_bwd(name, fn, rows, params, cts, tm, row_mask=None, par_mask=None, row_dtypes=None):
    arrs, specs = zip(*[_row_spec(r, tm) for r in rows])
    s = arrs[0].shape[0]
    ct_groups = [c if isinstance(c, list) else [c] for c in cts]
    ct_arrs, ct_specs = zip(*[_row_spec(a, tm) for grp in ct_groups for a in grp])
    cts = list(ct_arrs)
    nr, npar, nct = len(rows), len(params), len(cts)
    row_mask = row_mask or [True] * nr
    par_mask = par_mask or [True] * npar
    row_idx = [k for k in range(nr) if row_mask[k]]
    par_idx = [k for k in range(npar) if par_mask[k]]
    row_w = [specs[k].block_shape[1] for k in row_idx]

    def body(*refs):
        ins = [r[...] for r in refs[:nr + npar]]
        ct_refs = list(refs[nr + npar:nr + npar + nct])
        ctv = []
        for grp in ct_groups:
            acc = ct_refs.pop(0)[...].astype(f32)
            for _ in grp[1:]:
                acc = acc + ct_refs.pop(0)[...].astype(f32)
            ctv.append(acc)
        ctv = tuple(ctv)
        outs = refs[nr + npar + nct:]
        _, vjp = jax.vjp(fn, *ins)
        grads = vjp(ctv)
        for o_ref, k in zip(outs[:len(row_idx)], row_idx):
            o_ref[...] = grads[k].astype(o_ref.dtype)
        first = pl.program_id(0) == 0
        for o_ref, k in zip(outs[len(row_idx):], par_idx):
            g = grads[nr + k].astype(f32)

            @pl.when(first)
            def _(o_ref=o_ref, g=g):
                o_ref[...] = g

            @pl.when(jnp.logical_not(first))
            def _(o_ref=o_ref, g=g):
                o_ref[...] += g

    res = pl.pallas_call(
        body, grid=(s // tm,),
        in_specs=list(specs) + [_par_spec(p) for p in params]
        + list(ct_specs),
        out_specs=[pl.BlockSpec((tm, w), lambda i: (i, 0)) for w in row_w]
        + [_par_spec(params[k]) for k in par_idx],
        out_shape=[jax.ShapeDtypeStruct((s, w), dt) for w, dt in zip(row_w, row_dtypes or [f32] * len(row_w))]
        + [jax.ShapeDtypeStruct(params[k].shape, f32) for k in par_idx],
        compiler_params=_cparams("arbitrary"), name=name)(*arrs, *params, *cts)
    return list(res[:len(row_idx)]), list(res[len(row_idx):])


def _pick(n, prefs):
    for t in prefs:
        if n % t == 0:
            return t
    return n


MM_CHUNK = 512


def mm_nn(name, a, b, out_dtype=f32):
    m, k = a.shape
    n = b.shape[1]
    tm = _pick(m, (512, 256, 128))
    cn = _pick(n, (MM_CHUNK, 256, 128))

    def body(a_ref, b_ref, o_ref):
        av = a_ref[...].astype(bf16)
        for c in range(n // cn):
            sl = slice(c * cn, (c + 1) * cn)
            o_ref[:, sl] = _dg(av, b_ref[:, sl].astype(bf16), 1, 0).astype(o_ref.dtype)

    return pl.pallas_call(
        body, grid=(m // tm,),
        in_specs=[pl.BlockSpec((tm, k), lambda i: (i, 0)), pl.BlockSpec((k, n), lambda i: (0, 0))],
        out_specs=pl.BlockSpec((tm, n), lambda i: (i, 0)),
        out_shape=jax.ShapeDtypeStruct((m, n), out_dtype),
        compiler_params=_cparams("parallel"), name=name)(a, b)


def mm_nt(name, a, b, out_dtype=f32):
    m, n = a.shape
    k = b.shape[0]
    tm = _pick(m, (512, 256, 128))
    ck = _pick(k, (MM_CHUNK, 256, 128))

    def body(a_ref, b_ref, o_ref):
        av = a_ref[...].astype(bf16)
        for c in range(k // ck):
            sl = slice(c * ck, (c + 1) * ck)
            o_ref[:, sl] = _dg(av, b_ref[sl, :].astype(bf16), 1, 1).astype(o_ref.dtype)

    return pl.pallas_call(
        body, grid=(m // tm,),
        in_specs=[pl.BlockSpec((tm, n), lambda i: (i, 0)), pl.BlockSpec((k, n), lambda i: (0, 0))],
        out_specs=pl.BlockSpec((tm, k), lambda i: (i, 0)),
        out_shape=jax.ShapeDtypeStruct((m, k), out_dtype),
        compiler_params=_cparams("parallel"), name=name)(a, b)


def mm_tn(name, a, b, out_dtype=f32):
    s, m = a.shape
    n = b.shape[1]
    tn = _pick(n, (256, 128))
    cm = _pick(m, (256, 128))

    def body(a_ref, b_ref, o_ref):
        bv = b_ref[...].astype(bf16)
        for c in range(m // cm):
            sl = slice(c * cm, (c + 1) * cm)
            o_ref[sl, :] = _dg(a_ref[:, sl].astype(bf16), bv, 0, 0).astype(o_ref.dtype)

    return pl.pallas_call(
        body, grid=(n // tn,),
        in_specs=[pl.BlockSpec((s, m), lambda j: (0, 0)), pl.BlockSpec((s, tn), lambda j: (0, j))],
        out_specs=pl.BlockSpec((m, tn), lambda j: (0, j)),
        out_shape=jax.ShapeDtypeStruct((m, n), out_dtype),
        compiler_params=_cparams("parallel"), name=name)(a, b)


def _postnorm_tile(h, sub, g, b):
    z = ALPHA * h + sub
    mu = jnp.mean(z, -1, keepdims=True)
    zc = z - mu
    var = jnp.mean(zc * zc, -1, keepdims=True)
    return (zc * lax.rsqrt(var + LN_EPS) * g + b,)


def _swiglu_tile(au):
    a, u = au[:, :FFN_HIDDEN], au[:, FFN_HIDDEN:]
    return (jax.nn.silu(a) * u,)


def _glu_tile(og):
    o, g = og[:, :D_MODEL], og[:, D_MODEL:]
    return (o * jax.nn.sigmoid(g),)


def _xattn_tile(q, kv):
    outs = []
    for h in range(X_HEADS):
        sl = slice(h * X_HEAD_DIM, (h + 1) * X_HEAD_DIM)
        s = bdot(q[:, sl], kv[:, sl], 1, 1) * (X_HEAD_DIM ** -0.5)
        m = lax.stop_gradient(jnp.max(s, -1, keepdims=True))
        p = jnp.exp(s - m)
        p = p / jnp.sum(p, -1, keepdims=True)
        outs.append(bdot(p, kv[:, D_MODEL + h * X_HEAD_DIM:D_MODEL + (h + 1) * X_HEAD_DIM], 1, 0))
    return (jnp.concatenate(outs, -1),)


TM_ROW = 256


def postnorm_fwd(tag, h, sub, g, b):
    return rowmap("postnorm_" + tag, lambda *a: _postnorm_tile(*a) * 2, [h, sub], [g, b], [D_MODEL] * 2, TM_ROW,
                  out_dtypes=[f32, bf16])


def postnorm_bwd(tag, h, sub, g, b, dy):
    (dh, dsub), (dg, db) = rowmap_bwd("postnorm_bwd_" + tag, _postnorm_tile, [h, sub], [g, b], [dy], TM_ROW,
                                      row_dtypes=[f32, bf16])
    return dh, dsub, dg, db


def xattn_fwd(tag, h, mem, wq, wkv, wo):
    q = mm_nn("xq_" + tag, h, wq, out_dtype=bf16)
    kv = mm_nn("xkv_" + tag, mem, wkv)
    ao = rowmap("xattn_" + tag, _xattn_tile, [q], [kv], [D_MODEL], TM_ROW, out_dtypes=[bf16])[0]
    out = mm_nn("xo_" + tag, ao, wo)
    return out, (q, kv, ao)


def xattn_bwd(tag, h, mem, wq, wkv, wo, res, dout):
    q, kv, ao = res
    dwo = mm_tn("xo_dw_" + tag, ao, dout)
    dao = mm_nt("xo_dx_" + tag, dout, wo)
    (dq,), (dkv,) = rowmap_bwd("xattn_bwd_" + tag, _xattn_tile, [q], [kv], [dao], TM_ROW, row_dtypes=[bf16])
    dwq = mm_tn("xq_dw_" + tag, h, dq)
    dh = mm_nt("xq_dx_" + tag, dq, wq)
    dwkv = mm_tn("xkv_dw_" + tag, mem, dkv)
    return dh, dwq, dwkv, dwo


FFN_SHARD = FFN_HIDDEN // 4
TM_FFN = 512


def _silu_mul(a, u):
    return jax.nn.silu(a) * u


def ffn_fwd(tag, layer, h, wg, wu, wd):
    s = h.shape[0]
    tm, fs = TM_FFN, FFN_SHARD
    w_in = pl.BlockSpec((None, None, fs, D_MODEL), lambda k, i: (k, layer, 0, 0))
    act = pl.BlockSpec((None, tm, fs), lambda k, i: (k, i, 0))

    def up_body(h_ref, wg_ref, wu_ref, a_ref, u_ref, hid_ref):
        hv = h_ref[...].astype(bf16)
        a, u = _dg(hv, wg_ref[...], 1, 1), _dg(hv, wu_ref[...], 1, 1)
        a_ref[...], u_ref[...] = a.astype(bf16), u.astype(bf16)
        hid_ref[...] = _silu_mul(a, u).astype(bf16)

    a4, u4, hid4 = pl.pallas_call(
        up_body, grid=(4, s // tm),
        in_specs=[pl.BlockSpec((tm, D_MODEL), lambda k, i: (i, 0)), w_in, w_in],
        out_specs=[act, act, act],
        out_shape=[jax.ShapeDtypeStruct((4, s, fs), bf16)] * 3,
        compiler_params=_cparams("parallel", "parallel"), name="ffn_up_" + tag)(h, wg, wu)

    all_act = pl.BlockSpec((4, tm, fs), lambda i: (0, i, 0))
    all_w = pl.BlockSpec((4, None, fs, D_MODEL), lambda i: (0, layer, 0, 0))

    def down_body(hid_ref, wd_ref, o_ref):
        acc = _dg(hid_ref[0], wd_ref[0], 1, 0)
        for k in range(1, 4):
            acc = acc + _dg(hid_ref[k], wd_ref[k], 1, 0)
        o_ref[...] = acc

    out = pl.pallas_call(
        down_body, grid=(s // tm,), in_specs=[all_act, all_w],
        out_specs=pl.BlockSpec((tm, D_MODEL), lambda i: (i, 0)),
        out_shape=jax.ShapeDtypeStruct((s, D_MODEL), f32),
        compiler_params=_cparams("parallel"), name="ffn_down_" + tag)(hid4, wd)
    return out, (a4, u4, hid4)


def ffn_bwd(tag, layer, h, wg, wu, wd, res, dout):
    a4, u4, hid4 = res
    s = h.shape[0]
    tm, fs = TM_FFN, FFN_SHARD
    act = pl.BlockSpec((None, tm, fs), lambda k, i: (k, i, 0))

    def dact_body(do_ref, wd_ref, a_ref, u_ref, da_ref, du_ref):
        dhid = _dg(do_ref[...].astype(bf16), wd_ref[...], 1, 1)
        _, vjp = jax.vjp(_silu_mul, a_ref[...].astype(f32), u_ref[...].astype(f32))
        da, du = vjp(dhid)
        da_ref[...], du_ref[...] = da.astype(bf16), du.astype(bf16)

    da4, du4 = pl.pallas_call(
        dact_body, grid=(4, s // tm),
        in_specs=[pl.BlockSpec((tm, D_MODEL), lambda k, i: (i, 0)),
                  pl.BlockSpec((None, None, fs, D_MODEL), lambda k, i: (k, layer, 0, 0)), act, act],
        out_specs=[act, act], out_shape=[jax.ShapeDtypeStruct((4, s, fs), bf16)] * 2,
        compiler_params=_cparams("parallel", "parallel"), name="ffn_dact_" + tag)(dout, wd, a4, u4)

    all_act = pl.BlockSpec((4, tm, fs), lambda i: (0, i, 0))
    all_w = pl.BlockSpec((4, None, fs, D_MODEL), lambda i: (0, layer, 0, 0))

    def dx_body(da_ref, du_ref, wg_ref, wu_ref, o_ref):
        acc = _dg(da_ref[0], wg_ref[0], 1, 0) + _dg(du_ref[0], wu_ref[0], 1, 0)
        for k in range(1, 4):
            acc = acc + (_dg(da_ref[k], wg_ref[k], 1, 0) + _dg(du_ref[k], wu_ref[k], 1, 0))
        o_ref[...] = acc

    dh = pl.pallas_call(
        dx_body, grid=(s // tm,), in_specs=[all_act, all_act, all_w, all_w],
        out_specs=pl.BlockSpec((tm, D_MODEL), lambda i: (i, 0)),
        out_shape=jax.ShapeDtypeStruct((s, D_MODEL), f32),
        compiler_params=_cparams("parallel"), name="ffn_dx_" + tag)(da4, du4, wg, wu)

    tn = 256
    whole = pl.BlockSpec((None, s, fs), lambda k, j: (k, 0, 0))

    def dwin_body(h_ref, da_ref, du_ref, dwg_ref, dwu_ref):
        hv = h_ref[...].astype(bf16)
        dwg_ref[...] = _dg(da_ref[...], hv, 0, 0)
        dwu_ref[...] = _dg(du_ref[...], hv, 0, 0)

    dwg, dwu = pl.pallas_call(
        dwin_body, grid=(4, D_MODEL // tn),
        in_specs=[pl.BlockSpec((s, tn), lambda k, j: (0, j)), whole, whole],
        out_specs=[pl.BlockSpec((None, fs, tn), lambda k, j: (k, 0, j))] * 2,
        out_shape=[jax.ShapeDtypeStruct((4, fs, D_MODEL), f32)] * 2,
        compiler_params=_cparams("parallel", "parallel"), name="ffn_dwin_" + tag)(h, da4, du4)

    def dwd_body(hid_ref, do_ref, dwd_ref):
        dwd_ref[...] = _dg(hid_ref[...], do_ref[...].astype(bf16), 0, 0)

    dwd = pl.pallas_call(
        dwd_body, grid=(4, D_MODEL // tn),
        in_specs=[whole, pl.BlockSpec((s, tn), lambda k, j: (0, j))],
        out_specs=pl.BlockSpec((None, fs, tn), lambda k, j: (k, 0, j)),
        out_shape=jax.ShapeDtypeStruct((4, fs, D_MODEL), f32),
        compiler_params=_cparams("parallel", "parallel"), name="ffn_dwd_" + tag)(hid4, dout)
    return dh, dwg, dwu, dwd


def loss_head(y, target):
    s, d = y.shape
    tm = TM_ROW

    def body(y_ref, t_ref, part_ref, dy_ref):
        e = y_ref[...] - t_ref[...]
        dy_ref[...] = e * (1.0 / d)
        p = jnp.sum(e * e, 0, keepdims=True) * (0.5 / d)

        @pl.when(pl.program_id(0) == 0)
        def _():
            part_ref[...] = p

        @pl.when(pl.program_id(0) != 0)
        def _():
            part_ref[...] += p

    return pl.pallas_call(
        body, grid=(s // tm,),
        in_specs=[pl.BlockSpec((tm, d), lambda i: (i, 0))] * 2,
        out_specs=[pl.BlockSpec((1, d), lambda i: (0, 0)), pl.BlockSpec((tm, d), lambda i: (i, 0))],
        out_shape=[jax.ShapeDtypeStruct((1, d), f32), jax.ShapeDtypeStruct((s, d), f32)],
        compiler_params=_cparams("arbitrary"), name="loss_head")(y, target)


TM_CONV = 512


def _conv_rows(xx, w_ref, n_rows):
    a = w_ref[3:4, :] * xx
    for k in (1, 2, 3):
        a = a + w_ref[3 - k:4 - k, :] * pltpu.roll(xx, k, 0)
    return a


def _dn_act(a, is_qk):
    s = jax.nn.silu(a)
    n = s * lax.rsqrt(jnp.sum(s * s, -1, keepdims=True) + RMS_EPS)
    return jnp.where(is_qk, n, s)


def dn_conv_fwd(tag, proj, cw):
    s = proj.shape[0]
    tm, hb = TM_CONV, TM_CONV // SUBLANES

    def body(xh_ref, x_ref, w_ref, o_ref):
        j, t = pl.program_id(0), pl.program_id(1)
        halo = jnp.where(t > 0, xh_ref[...], 0.0)
        xx = jnp.concatenate([halo, x_ref[...]], 0)
        a = _conv_rows(xx, w_ref, tm + SUBLANES)
        o_ref[...] = _dn_act(a, j < 2 * DN_HEADS)[SUBLANES:, :]

    return pl.pallas_call(
        body, grid=(DN_QKV_DIM // LANES, s // tm),
        in_specs=[pl.BlockSpec((SUBLANES, LANES), lambda j, t: (jnp.maximum(t * hb - 1, 0), j)),
                  pl.BlockSpec((tm, LANES), lambda j, t: (t, j)),
                  pl.BlockSpec((DN_CONV, LANES), lambda j, t: (0, j))],
        out_specs=pl.BlockSpec((tm, LANES), lambda j, t: (t, j)),
        out_shape=jax.ShapeDtypeStruct((s, DN_QKV_DIM), f32),
        compiler_params=_cparams("parallel", "parallel"), name="dn_conv_" + tag)(proj, proj, cw)


def dn_conv_bwd(tag, proj, cw, dy):
    s = proj.shape[0]
    tm, hb = TM_CONV, TM_CONV // SUBLANES
    nt = s // tm
    n_ext = tm + 2 * SUBLANES

    def body(xb_ref, x_ref, xa_ref, dy_ref, dya_ref, w_ref, dx_ref, dw_ref):
        j, t = pl.program_id(0), pl.program_id(1)
        xx = jnp.concatenate([jnp.where(t > 0, xb_ref[...], 0.0), x_ref[...],
                              jnp.where(t < nt - 1, xa_ref[...], 0.0)], 0)
        dyy = jnp.concatenate([jnp.zeros((SUBLANES, LANES), f32), dy_ref[...],
                               jnp.where(t < nt - 1, dya_ref[...], 0.0)], 0)
        a = _conv_rows(xx, w_ref, n_ext)
        _, vjp = jax.vjp(lambda v: _dn_act(v, j < 2 * DN_HEADS), a)
        da, = vjp(dyy)
        dx = w_ref[3:4, :] * da
        for k in (1, 2, 3):
            dx = dx + w_ref[3 - k:4 - k, :] * pltpu.roll(da, n_ext - k, 0)
        dx_ref[...] = dx[SUBLANES:SUBLANES + tm, :]
        row = _iota2((n_ext, LANES), 0)
        da_in = jnp.where((row >= SUBLANES) & (row < SUBLANES + tm), da, 0.0)
        r8 = _iota2((SUBLANES, LANES), 0)
        dw = jnp.zeros((SUBLANES, LANES), f32)
        for k in range(DN_CONV):
            xs = xx if k == 0 else pltpu.roll(xx, k, 0)
            dw = dw + jnp.where(r8 == 3 - k, jnp.sum(da_in * xs, 0, keepdims=True), 0.0)

        @pl.when(t == 0)
        def _():
            dw_ref[...] = dw

        @pl.when(t != 0)
        def _():
            dw_ref[...] += dw

    nb8 = s // SUBLANES
    return pl.pallas_call(
        body, grid=(DN_QKV_DIM // LANES, nt),
        in_specs=[pl.BlockSpec((SUBLANES, LANES), lambda j, t: (jnp.maximum(t * hb - 1, 0), j)),
                  pl.BlockSpec((tm, LANES), lambda j, t: (t, j)),
                  pl.BlockSpec((SUBLANES, LANES), lambda j, t: (jnp.minimum((t + 1) * hb, nb8 - 1), j)),
                  pl.BlockSpec((tm, LANES), lambda j, t: (t, j)),
                  pl.BlockSpec((SUBLANES, LANES), lambda j, t: (jnp.minimum((t + 1) * hb, nb8 - 1), j)),
                  pl.BlockSpec((DN_CONV, LANES), lambda j, t: (0, j))],
        out_specs=[pl.BlockSpec((tm, LANES), lambda j, t: (t, j)),
                   pl.BlockSpec((SUBLANES, LANES), lambda j, t: (0, j))],
        out_shape=[jax.ShapeDtypeStruct((s, DN_QKV_DIM), f32), jax.ShapeDtypeStruct((SUBLANES, DN_QKV_DIM), f32)],
        compiler_params=_cparams("parallel", "arbitrary"), name="dn_conv_bwd_" + tag)(proj, proj, proj, dy, dy, cw)


def _gate_tile(ba, eb, ea, alog, dtb):
    beta = jax.nn.sigmoid(hdot(ba, eb))
    g = -jnp.exp(alog) * jax.nn.softplus(hdot(ba, ea) + dtb)
    return beta, g


def _each(fn, *lists):
    return [fn(*args) for args in zip(*lists)]


@functools.partial(jax.custom_vjp, nondiff_argnums=(1,))
def _halves(x, axis):
    h = x.shape[axis] // 2
    return (x[:h], x[h:]) if axis == 0 else (x[:, :h], x[:, h:])


def _halves_fwd(x, axis):
    return _halves(x, axis), None


def _halves_bwd(axis, _, g):
    return (jnp.concatenate(g, axis),)


_halves.defvjp(_halves_fwd, _halves_bwd)


def _tri_inv_unit(lowers):
    c = lowers[0].shape[0]
    r, col = _iota2((c, c), 0), _iota2((c, c), 1)
    eye = jnp.where(r == col, 1.0, 0.0).astype(f32)
    invs = None
    sh = 0
    while (1 << sh) < c:
        same_2b = lax.shift_right_logical(r, sh + 1) == lax.shift_right_logical(col, sh + 1)
        diff_b = lax.shift_right_logical(r, sh) != lax.shift_right_logical(col, sh)
        offs = [jnp.where(same_2b & diff_b, low, 0.0) for low in lowers]
        if invs is None:
            invs = [eye - off for off in offs]
        else:
            part = _each(lambda inv, off: _dot3(inv, off, 1, 0), invs, offs)
            invs = _each(lambda inv, p: inv - _dot3(p, inv, 1, 0), invs, part)
        sh += 1
    return invs


@jax.custom_vjp
def _known_inverse(lower, tinv):
    return tinv


def _known_inverse_fwd(lower, tinv):
    return tinv, tinv


def _known_inverse_bwd(tinv, g):
    tt = tinv.T
    return -hdot(hdot(tt, g), tt), jnp.zeros_like(tinv)


_known_inverse.defvjp(_known_inverse_fwd, _known_inverse_bwd)


def _delta_chunk(q, k, v, gb, betab, state, tinv_known=None):
    c, hd = DN_CHUNK, DN_HEAD_DIM
    r, col = _iota2((c, c), 0), _iota2((c, c), 1)
    causal, strict = r >= col, r > col
    tril = jnp.where(causal, 1.0, 0.0).astype(f32)
    gc = _each(lambda g: hdot(tril, g), gb)
    decay = _each(lambda g: jnp.where(causal, jnp.exp(jnp.where(causal, g - g.T, 0.0)), 0.0), gc)
    qs = _each(lambda t: t * (DN_HEAD_DIM ** -0.5), q)
    kb = _each(lambda a, b: a * b, k, betab)
    kq = _each(lambda a, b, kk: _halves(bdot(jnp.concatenate([a, b], 0), kk, 1, 1), 0), kb, qs, k)
    lower = _each(lambda x, d: jnp.where(strict, x[0], 0.0) * d, kq, decay)
    intra = _each(lambda x, d: x[1] * d, kq, decay)
    tinv = _tri_inv_unit(lower) if tinv_known is None else _each(_known_inverse, lower, tinv_known)
    eg = _each(jnp.exp, gc)
    uw = _each(lambda t, vv, b, kb_, e: _halves(hdot(t, jnp.concatenate([vv * b, kb_ * e], 1)), 1),
               tinv, v, betab, kb, eg)
    gl = _each(lambda g: jnp.sum(jnp.where(r == c - 1, g, 0.0), 0, keepdims=True), gc)
    k_dec = _each(lambda kk, a, g: kk * jnp.exp(a - g), k, gl, gc)
    ws = _each(lambda x, t, e, st: _halves(bdot(jnp.concatenate([x[1], t * e], 0), st, 1, 0), 0), uw, qs, eg, state)
    v_new = _each(lambda x, y: x[0] - y[0], uw, ws)
    out = _each(lambda y, a, vn: y[1] + bdot(a, vn, 1, 0), ws, intra, v_new)
    new_state = _each(lambda st, a, kd, vn: st * jnp.exp(a) + bdot(kd, vn, 0, 0), state, gl, k_dec, v_new)
    return tuple(out), tuple(new_state), tuple(tinv)


def delta_fwd(tag, qkv, gb, betab):
    s = qkv.shape[0]
    c, hd = DN_CHUNK, DN_HEAD_DIM
    n = s // c

    hg, ng = DN_HEADS_PER_STEP, DN_HEADS // DN_HEADS_PER_STEP

    def body(q_ref, k_ref, v_ref, g_ref, b_ref, o_ref, st_ref, ti_ref, state):
        @pl.when(pl.program_id(1) == 0)
        def _():
            state[...] = jnp.zeros_like(state)

        heads = lambda ref: tuple(ref[:, j * hd:(j + 1) * hd] for j in range(hg))
        st = tuple(state[j] for j in range(hg))
        outs, news, tinv = _delta_chunk(heads(q_ref), heads(k_ref), heads(v_ref), heads(g_ref), heads(b_ref), st)
        for j in range(hg):
            st_ref[j] = st[j]
            ti_ref[j] = tinv[j]
            o_ref[:, j * hd:(j + 1) * hd] = outs[j]
            state[j] = news[j]

    blk = lambda off: pl.BlockSpec((c, hg * hd), lambda h, i, _o=off: (i, h + _o))
    per_chunk = pl.BlockSpec((hg, None, hd, hd), lambda h, i: (h, i, 0, 0))
    return pl.pallas_call(
        body, grid=(ng, n),
        in_specs=[blk(0), blk(ng), blk(2 * ng), blk(0), blk(0)],
        out_specs=[blk(0), per_chunk, per_chunk],
        out_shape=[jax.ShapeDtypeStruct((s, DN_KEY_DIM), f32)] + [jax.ShapeDtypeStruct((DN_HEADS, n, hd, hd), f32)] * 2,
        scratch_shapes=[pltpu.VMEM((hg, hd, hd), f32)],
        compiler_params=_cparams("parallel", "arbitrary"), name="delta_" + tag)(qkv, qkv, qkv, gb, betab)


def delta_bwd(tag, qkv, gb, betab, states, tinvs, do):
    s = qkv.shape[0]
    c, hd = DN_CHUNK, DN_HEAD_DIM
    n = s // c

    hg, ng = DN_HEADS_PER_STEP, DN_HEADS // DN_HEADS_PER_STEP

    def body(q_ref, k_ref, v_ref, g_ref, b_ref, st_ref, ti_ref, do_ref, dq_ref, dk_ref, dv_ref, dg_ref, db_ref, dstate):
        @pl.when(pl.program_id(1) == 0)
        def _():
            dstate[...] = jnp.zeros_like(dstate)

        heads = lambda ref: tuple(ref[:, j * hd:(j + 1) * hd] for j in range(hg))
        tinv = tuple(ti_ref[j] for j in range(hg))
        _, vjp = jax.vjp(lambda *args: _delta_chunk(*args, tinv_known=tinv)[:2],
                         heads(q_ref), heads(k_ref), heads(v_ref), heads(g_ref), heads(b_ref),
                         tuple(st_ref[j] for j in range(hg)))
        grads = vjp((heads(do_ref), tuple(dstate[j] for j in range(hg))))
        for ref, g in zip((dq_ref, dk_ref, dv_ref, dg_ref, db_ref), grads[:5]):
            for j in range(hg):
                ref[:, j * hd:(j + 1) * hd] = g[j]
        for j in range(hg):
            dstate[j] = grads[5][j]

    blk = lambda off: pl.BlockSpec((c, hg * hd), lambda h, i, _o=off: (n - 1 - i, h + _o))
    return pl.pallas_call(
        body, grid=(ng, n),
        in_specs=[blk(0), blk(ng), blk(2 * ng), blk(0), blk(0)]
        + [pl.BlockSpec((hg, None, hd, hd), lambda h, i: (h, n - 1 - i, 0, 0))] * 2 + [blk(0)],
        out_specs=[blk(0)] * 5,
        out_shape=[jax.ShapeDtypeStruct((s, DN_KEY_DIM), f32)] * 5,
        scratch_shapes=[pltpu.VMEM((hg, hd, hd), f32)],
        compiler_params=_cparams("parallel", "arbitrary"),
        name="delta_bwd_" + tag)(qkv, qkv, qkv, gb, betab, states, tinvs, do)


def _dn_out_tile(o, z, ng):
    outs = []
    for h in range(DN_HEADS):
        sl = slice(h * DN_HEAD_DIM, (h + 1) * DN_HEAD_DIM)
        oh = o[:, sl]
        nrm = oh * lax.rsqrt(jnp.mean(oh * oh, -1, keepdims=True) + RMS_EPS) * ng[:, sl]
        outs.append(nrm * jax.nn.silu(z[:, sl]))
    return (jnp.concatenate(outs, -1),)


def _head_selectors():
    r, c = _iota2((BA_PAD, DN_KEY_DIM), 0), _iota2((BA_PAD, DN_KEY_DIM), 1) // DN_HEAD_DIM
    return (r == c).astype(f32), (r == c + DN_HEADS).astype(f32)


def dn_mixer_fwd(tag, proj, cw, alog_b, dtb_b, ng_b):
    eb, ea = _head_selectors()
    ba = (proj, BA_PAD, COL_BA // BA_PAD)
    qkv = dn_conv_fwd(tag, proj, cw)
    betab, gb = rowmap("dn_gate_" + tag, _gate_tile, [ba], [eb, ea, alog_b, dtb_b], [DN_KEY_DIM] * 2, TM_ROW)
    o, states, tinvs = delta_fwd(tag, qkv, gb, betab)
    z = (proj, DN_KEY_DIM, COL_Z // DN_KEY_DIM)
    a_out = rowmap("dn_out_" + tag, _dn_out_tile, [o, z], [ng_b], [DN_KEY_DIM], TM_ROW)[0]
    return a_out, (qkv, betab, gb, o, states, tinvs)


def dn_mixer_bwd(tag, proj, cw, alog_b, dtb_b, ng_b, res, da_out):
    qkv, betab, gb, o, states, tinvs = res
    eb, ea = _head_selectors()
    ba = (proj, BA_PAD, COL_BA // BA_PAD)
    z = (proj, DN_KEY_DIM, COL_Z // DN_KEY_DIM)
    (do, dz), (dng,) = rowmap_bwd("dn_out_bwd_" + tag, _dn_out_tile, [o, z], [ng_b], [da_out], TM_ROW)
    dq, dk, dv, dgb, dbetab = delta_bwd(tag, qkv, gb, betab, states, tinvs, do)
    dqkv_raw, dcw = dn_conv_bwd(tag, proj, cw, jnp.concatenate([dq, dk, dv], 1))
    (dba,), (dalog, ddtb) = rowmap_bwd("dn_gate_bwd_" + tag, _gate_tile, [ba], [eb, ea, alog_b, dtb_b],
                                       [dbetab, dgb], TM_ROW, par_mask=[False, False, True, True])
    return dqkv_raw, dz, dba, dcw[:DN_CONV], dalog, ddtb, dng


def _swap_halves(x):
    n = x.shape[1]
    first = (_iota2((1, n), 1) % SW_HEAD_DIM) < SW_HEAD_DIM // 2
    return jnp.where(first, pltpu.roll(x, n - SW_HEAD_DIM // 2, 1), pltpu.roll(x, SW_HEAD_DIM // 2, 1))


def _rope_apply(x, cos, sin_signed):
    return x * cos + _swap_halves(x) * sin_signed


def _rope_transpose(dy, cos, sin_signed):
    return dy * cos + _swap_halves(dy * sin_signed)


def rope_tables(positions, s):
    half = SW_HEAD_DIM // 2
    inv_freq = ROPE_THETA ** (-jnp.arange(0, SW_HEAD_DIM, 2, dtype=f32) / SW_HEAD_DIM)
    ang = positions.reshape(s, 1).astype(f32) * inv_freq[None, :]
    cos, sin = jnp.cos(ang), jnp.sin(ang)
    cos_t = jnp.tile(jnp.concatenate([cos, cos], 1), (1, SW_HEADS))
    sin_t = jnp.tile(jnp.concatenate([-sin, sin], 1), (1, SW_HEADS))
    assert cos_t.shape == (s, SW_DIM) and half * 2 == SW_HEAD_DIM
    return cos_t, sin_t


def rope_fwd(tag, proj, cos, sin):
    def fn(q, k, v, c, sg):
        return _rope_apply(q, c, sg), _rope_apply(k, c, sg), v

    rows = [(proj, SW_DIM, COL_SWQ // SW_DIM), (proj, SW_DIM, COL_SWK // SW_DIM), (proj, SW_DIM, COL_SWV // SW_DIM), cos, sin]
    return rowmap("rope_" + tag, fn, rows, [], [SW_DIM] * 3, TM_ROW, out_dtypes=[bf16] * 3)


def _swa_block(q, kp, kc, vp, vc, first):
    blk = SW_BLOCK
    kk = jnp.concatenate([kp, kc], 0)
    vv = jnp.concatenate([vp, vc], 0)
    dist = (_iota2((blk, 2 * blk), 0) + blk) - _iota2((blk, 2 * blk), 1)
    kj = _iota2((blk, 2 * blk), 1)
    valid = (dist >= 0) & (dist <= blk) & ((kj >= blk) | jnp.logical_not(first))
    lane_head = _iota2((1, LANES), 1) // SW_HEAD_DIM
    outs, lses = [], []
    for p in range(SW_DIM // LANES):
        sl = slice(p * LANES, (p + 1) * LANES)
        qp, kp_, vp_ = q[:, sl], kk[:, sl], vv[:, sl]
        o_pair = jnp.zeros((blk, LANES), f32)
        l_pair = jnp.zeros((blk, LANES), f32)
        for e in range(LANES // SW_HEAD_DIM):
            msk = lane_head == e
            sc = bdot(jnp.where(msk, qp, 0.0), kp_, 1, 1) * (SW_HEAD_DIM ** -0.5)
            sc = jnp.where(valid, sc, -1e30)
            m = lax.stop_gradient(jnp.max(sc, -1, keepdims=True))
            pe = jnp.exp(sc - m)
            l = jnp.sum(pe, -1, keepdims=True)
            o = bdot(pe, vp_, 1, 0) / l
            o_pair = o_pair + jnp.where(msk, o, 0.0)
            l_pair = l_pair + jnp.where(msk, m + jnp.log(l), 0.0)
        outs.append(o_pair)
        lses.append(l_pair)
    return jnp.concatenate(outs, -1), jnp.concatenate(lses, -1)


def _swa_specs(r):
    cur = pl.BlockSpec((SW_BLOCK, SW_DIM), lambda rho, n: (n, rho))
    prev = pl.BlockSpec((SW_BLOCK, SW_DIM), lambda rho, n: (jnp.maximum(n - 1, 0), rho))
    return cur, prev


def swa_fwd(tag, r, q, k, v):
    s = q.shape[0]
    ln = s // r
    q2, k2, v2 = (t.reshape(ln, r * SW_DIM) for t in (q, k, v))
    cur, prev = _swa_specs(r)

    def body(q_ref, kp_ref, kc_ref, vp_ref, vc_ref, o_ref, l_ref):
        ins = [r[...].astype(f32) for r in (q_ref, kp_ref, kc_ref, vp_ref, vc_ref)]
        o, l = _swa_block(*ins, pl.program_id(1) == 0)
        o_ref[...] = o
        l_ref[...] = l

    o, l = pl.pallas_call(
        body, grid=(r, ln // SW_BLOCK),
        in_specs=[cur, prev, cur, prev, cur], out_specs=[cur, cur],
        out_shape=[jax.ShapeDtypeStruct((ln, r * SW_DIM), f32)] * 2,
        compiler_params=_cparams("parallel", "parallel"), name=f"swa{r}_{tag}")(q2, k2, k2, v2, v2)
    return o.reshape(s, SW_DIM), l.reshape(s, SW_DIM)


def swa_bwd(tag, r, q, k, v, do, dl):
    s = q.shape[0]
    ln = s // r
    q2, k2, v2, do2, dl2 = (t.reshape(ln, r * SW_DIM) for t in (q, k, v, do, dl))
    cur, prev = _swa_specs(r)

    def body(q_ref, kp_ref, kc_ref, vp_ref, vc_ref, do_ref, dl_ref, dq_ref, dka_ref, dkb_ref, dva_ref, dvb_ref):
        first = pl.program_id(1) == 0
        ins = [r[...].astype(f32) for r in (q_ref, kp_ref, kc_ref, vp_ref, vc_ref)]
        _, vjp = jax.vjp(lambda *a: _swa_block(*a, first), *ins)
        dq_ref[...], dka_ref[...], dkb_ref[...], dva_ref[...], dvb_ref[...] = vjp((do_ref[...], dl_ref[...]))

    outs = pl.pallas_call(
        body, grid=(r, ln // SW_BLOCK),
        in_specs=[cur, prev, cur, prev, cur, cur, cur], out_specs=[cur] * 5,
        out_shape=[jax.ShapeDtypeStruct((ln, r * SW_DIM), f32)] * 5,
        compiler_params=_cparams("parallel", "parallel"), name=f"swa{r}_bwd_{tag}")(q2, k2, k2, v2, v2, do2, dl2)
    return [t.reshape(s, SW_DIM) for t in outs]


def _combine_tile(o1, l1, o2, l2, o3, l3):
    m = lax.stop_gradient(jnp.maximum(jnp.maximum(l1, l2), l3))
    e1, e2, e3 = jnp.exp(l1 - m), jnp.exp(l2 - m), jnp.exp(l3 - m)
    return ((o1 * e1 + o2 * e2 + o3 * e3) / (e1 + e2 + e3),)


def swa_merge_bwd(tag, grads, cos, sin):
    s = cos.shape[0]
    tm = SW_BLOCK
    nt = s // tm
    here = pl.BlockSpec((tm, SW_DIM), lambda i: (i, 0))
    arrs, specs = [], []
    for r, g in zip(SW_DILATIONS, grads):
        ahead = pl.BlockSpec((tm, SW_DIM), lambda i, _r=r: (jnp.minimum(i + _r, nt - 1), 0))
        arrs += g
        specs += [here, ahead, here, ahead, here]

    def body(*refs):
        i = pl.program_id(0)
        c_ref, s_ref = refs[15], refs[16]
        dq_ref, dk_ref, dv_ref = refs[17:]
        dq = jnp.zeros((tm, SW_DIM), f32)
        dk = jnp.zeros((tm, SW_DIM), f32)
        dv = jnp.zeros((tm, SW_DIM), f32)
        for b, r in enumerate(SW_DILATIONS):
            gq, gka, gkb, gva, gvb = refs[5 * b:5 * b + 5]
            inside = i + r < nt
            dq = dq + gq[...]
            dk = dk + gkb[...] + jnp.where(inside, gka[...], 0.0)
            dv = dv + gvb[...] + jnp.where(inside, gva[...], 0.0)
        dq_ref[...] = _rope_transpose(dq, c_ref[...], s_ref[...])
        dk_ref[...] = _rope_transpose(dk, c_ref[...], s_ref[...])
        dv_ref[...] = dv

    return pl.pallas_call(
        body, grid=(nt,), in_specs=specs + [here, here], out_specs=[here] * 3,
        out_shape=[jax.ShapeDtypeStruct((s, SW_DIM), f32)] * 3,
        compiler_params=_cparams("parallel"), name="swa_merge_bwd_" + tag)(*arrs, cos, sin)


def swa_mixer_fwd(tag, proj, cos, sin):
    q, k, v = rope_fwd(tag, proj, cos, sin)
    ols = []
    for r in SW_DILATIONS:
        ols += list(swa_fwd(tag, r, q, k, v))
    b_out = rowmap("swa_comb_" + tag, _combine_tile, ols, [], [SW_DIM], TM_ROW)[0]
    return b_out, (q, k, v, ols)


def swa_mixer_bwd(tag, cos, sin, res, db_out):
    q, k, v, ols = res
    dols, _ = rowmap_bwd("swa_comb_bwd_" + tag, _combine_tile, ols, [], [db_out], TM_ROW)
    grads = [swa_bwd(tag, r, q, k, v, dols[2 * b], dols[2 * b + 1]) for b, r in enumerate(SW_DILATIONS)]
    return swa_merge_bwd(tag, grads, cos, sin)


TM_S5 = 256
S5_GPB = LANES // S5_GROUP
S5_NBLK = D_MODEL // LANES
S5_HALF = S5_GPB * S5_STATE
S5_BW = 2 * S5_HALF
S5_WIDTH = S5_NBLK * S5_BW
S5_TABW = S5_NBLK * S5_HALF


def _s5_disc_tile(a_re, a_im, log_dt, b_re, b_im, expand):
    dt = jnp.exp(log_dt)
    mag = jnp.exp(a_re * dt)
    abar_re, abar_im = mag * jnp.cos(a_im * dt), mag * jnp.sin(a_im * dt)
    n_re, n_im = abar_re - 1.0, abar_im
    den = a_re * a_re + a_im * a_im
    c_re = (n_re * a_re + n_im * a_im) / den
    c_im = (n_im * a_re - n_re * a_im) / den
    cx_re, cx_im = hdot(c_re, expand), hdot(c_im, expand)
    return abar_re, abar_im, cx_re * b_re - cx_im * b_im, cx_re * b_im + cx_im * b_re


def _s5_expand():
    return (_iota2((S5_STATE, S5_STATE * S5_GROUP), 1) // S5_GROUP == _iota2((S5_STATE, S5_STATE * S5_GROUP), 0)).astype(f32)


def s5_tables(a_re, a_im, log_dt):
    lanes = lambda v: v.reshape(1, S5_TABW)
    dt = jnp.broadcast_to(log_dt.reshape(S5_GROUPS, 1), (S5_GROUPS, S5_STATE))
    t = TM_S5

    def body(are_ref, aim_ref, ldt_ref, ar_ref, ai_ref, arr_ref, air_ref):
        dtv = jnp.exp(ldt_ref[...])
        lre, lim = are_ref[...] * dtv, aim_ref[...] * dtv
        row = _iota2((t, S5_HALF), 0)
        for asc, o_re, o_im in ((True, ar_ref, ai_ref), (False, arr_ref, air_ref)):
            n = (row + 1 if asc else t - row).astype(f32)
            mag = jnp.exp(n * lre)
            o_re[...] = mag * jnp.cos(n * lim)
            o_im[...] = mag * jnp.sin(n * lim)

    lane = pl.BlockSpec((1, S5_HALF), lambda j: (0, j))
    tab = pl.BlockSpec((t, S5_HALF), lambda j: (0, j))
    return pl.pallas_call(
        body, grid=(S5_NBLK,), in_specs=[lane] * 3, out_specs=[tab] * 4,
        out_shape=[jax.ShapeDtypeStruct((t, S5_TABW), f32)] * 4,
        compiler_params=_cparams("parallel"), name="s5_tables")(lanes(a_re), lanes(a_im), lanes(dt))


def s5_pack_weights(bbar_re, bbar_im, c_re, c_im):
    eye = jnp.eye(S5_GPB, dtype=f32)
    bb = jnp.stack([bbar_re.reshape(S5_GROUPS, S5_STATE, S5_GROUP), bbar_im.reshape(S5_GROUPS, S5_STATE, S5_GROUP)], 1)
    bb = bb.transpose(0, 3, 1, 2).reshape(S5_NBLK, S5_GPB, S5_GROUP, 2, S5_STATE)
    wb = (bb[:, :, :, :, None, :] * eye[None, :, None, None, :, None]).reshape(S5_NBLK, LANES, S5_BW)
    cc = jnp.stack([c_re, -c_im], 1)
    cc = cc.reshape(S5_NBLK, S5_GPB, 2, S5_GROUP, S5_STATE).transpose(0, 2, 1, 4, 3)
    wc = (cc[:, :, :, :, None, :] * eye[None, None, :, None, :, None]).reshape(S5_NBLK, S5_BW, LANES)
    return wb, wc


def s5_unpack_weight_grads(dwb, dwc):
    d6 = dwb.reshape(S5_NBLK, S5_GPB, S5_GROUP, 2, S5_GPB, S5_STATE)
    dbb = jnp.stack([d6[:, gl, :, :, gl, :] for gl in range(S5_GPB)])
    dbb = dbb.transpose(1, 0, 3, 4, 2).reshape(S5_GROUPS, 2, S5_STATE * S5_GROUP)
    c6 = dwc.reshape(S5_NBLK, 2, S5_GPB, S5_STATE, S5_GPB, S5_GROUP)
    dcc = jnp.stack([c6[:, :, gl, :, gl, :] for gl in range(S5_GPB)])
    dcc = dcc.transpose(1, 0, 2, 4, 3).reshape(S5_GROUPS, 2, S5_GROUP, S5_STATE)
    return dbb[:, 0], dbb[:, 1], dcc[:, 0], -dcc[:, 1]


def _s5_step_rows(t):
    d, out = 1, []
    while d < t:
        out.append(d)
        d *= 2
    return out


def s5_core_fwd(tag, u, wb, wc, a1, a2, dskip):
    s = u.shape[0]
    t = TM_S5

    def body(u_ref, wb_ref, wc_ref, ar_ref, ai_ref, d_ref, y_ref, x_ref, carry):
        @pl.when(pl.program_id(1) == 0)
        def _():
            carry[...] = jnp.zeros_like(carry)

        uv = u_ref[...]
        bu = bdot(uv, wb_ref[...], 1, 0)
        row = _iota2((t, LANES), 0)
        for c in range(S5_HALF // LANES):
            re, im = slice(c * LANES, (c + 1) * LANES), slice(S5_HALF + c * LANES, S5_HALF + (c + 1) * LANES)
            xr, xi = bu[:, re], bu[:, im]
            for d in _s5_step_rows(t):
                ar, ai = ar_ref[d - 1:d, re], ai_ref[d - 1:d, re]
                if d % SUBLANES:
                    keep = row >= d
                    sr = jnp.where(keep, pltpu.roll(xr, d, 0), 0.0)
                    si = jnp.where(keep, pltpu.roll(xi, d, 0), 0.0)
                    xr, xi = xr + ar * sr - ai * si, xi + ar * si + ai * sr
                else:
                    sr, si = xr[:t - d], xi[:t - d]
                    xr = jnp.concatenate([xr[:d], xr[d:] + (ar * sr - ai * si)], 0)
                    xi = jnp.concatenate([xi[:d], xi[d:] + (ar * si + ai * sr)], 0)
            cr, ci = carry[:, re], carry[:, im]
            ar, ai = ar_ref[:, re], ai_ref[:, re]
            x_ref[:, re] = xr + ar * cr - ai * ci
            x_ref[:, im] = xi + ar * ci + ai * cr
        carry[...] = x_ref[t - 1:t, :]
        y_ref[...] = bdot(x_ref[...], wc_ref[...], 1, 0) + d_ref[...] * uv

    tab = pl.BlockSpec((t, S5_HALF), lambda j, i: (0, j))
    return pl.pallas_call(
        body, grid=(S5_NBLK, s // t),
        in_specs=[pl.BlockSpec((t, LANES), lambda j, i: (i, j)),
                  pl.BlockSpec((None, LANES, S5_BW), lambda j, i: (j, 0, 0)),
                  pl.BlockSpec((None, S5_BW, LANES), lambda j, i: (j, 0, 0)),
                  tab, tab, pl.BlockSpec((1, LANES), lambda j, i: (0, j))],
        out_specs=[pl.BlockSpec((t, LANES), lambda j, i: (i, j)), pl.BlockSpec((t, S5_BW), lambda j, i: (i, j))],
        out_shape=[jax.ShapeDtypeStruct((s, D_MODEL), f32), jax.ShapeDtypeStruct((s, S5_WIDTH), f32)],
        scratch_shapes=[pltpu.VMEM((1, S5_BW), f32)],
        compiler_params=_cparams("parallel", "arbitrary"), name="s5_core_" + tag)(u, wb, wc, a1, a2, dskip)


def s5_core_bwd(tag, u, x, wb, wc, a1, a2, a1r, a2r, dskip, dy):
    s = u.shape[0]
    t = TM_S5
    nt = s // t
    hb = t // SUBLANES

    def body(u_ref, dy_ref, x_ref, xh_ref, wb_ref, wc_ref, ar_ref, ai_ref, arr_ref, air_ref, d_ref,
             du_ref, dwb_ref, dwc_ref, dd_ref, q1_ref, q2_ref, carry, lam_scr):
        i = pl.program_id(1)
        tt = nt - 1 - i

        @pl.when(i == 0)
        def _():
            carry[...] = jnp.zeros_like(carry)

        uv, dyv, xv = u_ref[...], dy_ref[...], x_ref[...]
        lam = bdot(dyv, wc_ref[...], 1, 1)
        row = _iota2((t, LANES), 0)
        x_last = jnp.where(tt > 0, xh_ref[SUBLANES - 1:SUBLANES, :], 0.0)
        q1s, q2s = [], []
        for c in range(S5_HALF // LANES):
            re, im = slice(c * LANES, (c + 1) * LANES), slice(S5_HALF + c * LANES, S5_HALF + (c + 1) * LANES)
            lr, li = lam[:, re], lam[:, im]
            for d in _s5_step_rows(t):
                ar, ai = ar_ref[d - 1:d, re], ai_ref[d - 1:d, re]
                if d % SUBLANES:
                    keep = row < t - d
                    sr = jnp.where(keep, pltpu.roll(lr, t - d, 0), 0.0)
                    si = jnp.where(keep, pltpu.roll(li, t - d, 0), 0.0)
                    lr, li = lr + ar * sr + ai * si, li + ar * si - ai * sr
                else:
                    sr, si = lr[d:], li[d:]
                    lr = jnp.concatenate([lr[:t - d] + (ar * sr + ai * si), lr[t - d:]], 0)
                    li = jnp.concatenate([li[:t - d] + (ar * si - ai * sr), li[t - d:]], 0)
            cr, ci = carry[:, re], carry[:, im]
            ar, ai = arr_ref[:, re], air_ref[:, re]
            lr, li = lr + ar * cr + ai * ci, li + ar * ci - ai * cr
            lam_scr[:, re] = lr
            lam_scr[:, im] = li
            pr = jnp.where(row == 0, x_last[:, re], pltpu.roll(xv[:, re], 1, 0))
            pi = jnp.where(row == 0, x_last[:, im], pltpu.roll(xv[:, im], 1, 0))
            p1, p2 = lr * pr + li * pi, li * pr - lr * pi
            q1, q2 = p1[:SUBLANES, :], p2[:SUBLANES, :]
            for k in range(1, hb):
                q1 = q1 + p1[k * SUBLANES:(k + 1) * SUBLANES, :]
                q2 = q2 + p2[k * SUBLANES:(k + 1) * SUBLANES, :]
            q1s.append(q1)
            q2s.append(q2)
        carry[...] = lam_scr[0:1, :]
        lam = lam_scr[...]
        du_ref[...] = bdot(lam, wb_ref[...], 1, 1) + d_ref[...] * dyv
        upd = [(dwb_ref, bdot(uv, lam, 0, 0)), (dwc_ref, bdot(xv, dyv, 0, 0)),
               (dd_ref, jnp.sum(dyv * uv, 0, keepdims=True)),
               (q1_ref, jnp.concatenate(q1s, 1)), (q2_ref, jnp.concatenate(q2s, 1))]

        @pl.when(i == 0)
        def _():
            for ref, val in upd:
                ref[...] = val

        @pl.when(i != 0)
        def _():
            for ref, val in upd:
                ref[...] += val

    nb8 = s // SUBLANES
    rev = lambda w: pl.BlockSpec((t, w), lambda j, i: (nt - 1 - i, j))
    tab = pl.BlockSpec((t, S5_HALF), lambda j, i: (0, j))
    return pl.pallas_call(
        body, grid=(S5_NBLK, nt),
        in_specs=[rev(LANES), rev(LANES), rev(S5_BW),
                  pl.BlockSpec((SUBLANES, S5_BW), lambda j, i: (jnp.maximum((nt - 1 - i) * hb - 1, 0), j)),
                  pl.BlockSpec((None, LANES, S5_BW), lambda j, i: (j, 0, 0)),
                  pl.BlockSpec((None, S5_BW, LANES), lambda j, i: (j, 0, 0)),
                  tab, tab, tab, tab, pl.BlockSpec((1, LANES), lambda j, i: (0, j))],
        out_specs=[rev(LANES),
                   pl.BlockSpec((None, LANES, S5_BW), lambda j, i: (j, 0, 0)),
                   pl.BlockSpec((None, S5_BW, LANES), lambda j, i: (j, 0, 0)),
                   pl.BlockSpec((1, LANES), lambda j, i: (0, j)),
                   pl.BlockSpec((SUBLANES, S5_HALF), lambda j, i: (0, j)),
                   pl.BlockSpec((SUBLANES, S5_HALF), lambda j, i: (0, j))],
        out_shape=[jax.ShapeDtypeStruct((s, D_MODEL), f32),
                   jax.ShapeDtypeStruct((S5_NBLK, LANES, S5_BW), f32),
                   jax.ShapeDtypeStruct((S5_NBLK, S5_BW, LANES), f32),
                   jax.ShapeDtypeStruct((1, D_MODEL), f32),
                   jax.ShapeDtypeStruct((SUBLANES, S5_TABW), f32),
                   jax.ShapeDtypeStruct((SUBLANES, S5_TABW), f32)],
        scratch_shapes=[pltpu.VMEM((1, S5_BW), f32), pltpu.VMEM((t, S5_BW), f32)],
        compiler_params=_cparams("parallel", "arbitrary"),
        name="s5_core_bwd_" + tag)(u, dy, x, x, wb, wc, a1, a2, a1r, a2r, dskip)


def _gelu_tile(y):
    return (jax.nn.gelu(y),)


def s5_mixer_fwd(tag, u, prm, w_og):
    a_re, a_im, log_dt, b_re, b_im, c_re, c_im, dskip = prm
    disc_in = [a_re, a_im, log_dt.reshape(S5_GROUPS, 1), b_re.reshape(S5_GROUPS, -1), b_im.reshape(S5_GROUPS, -1)]
    abar_re, abar_im, bbar_re, bbar_im = rowmap("s5_disc_" + tag, _s5_disc_tile, disc_in, [_s5_expand()],
                                                [S5_STATE, S5_STATE, S5_STATE * S5_GROUP, S5_STATE * S5_GROUP], S5_GROUPS)
    del abar_re, abar_im
    a1, a2, a1r, a2r = s5_tables(a_re, a_im, log_dt)
    wb, wc = s5_pack_weights(bbar_re, bbar_im, c_re, c_im)
    wb, wc = wb.astype(bf16), wc.astype(bf16)
    y, x = s5_core_fwd(tag, u, wb, wc, a1, a2, dskip.reshape(1, D_MODEL))
    hid = rowmap("s5_gelu_" + tag, _gelu_tile, [y], [], [D_MODEL], TM_ROW, out_dtypes=[bf16])[0]
    og = mm_nn("s5_og_" + tag, hid, w_og)
    mix = rowmap("s5_glu_" + tag, _glu_tile, [og], [], [D_MODEL], TM_ROW)[0]
    return mix, (disc_in, a1, a2, a1r, a2r, wb, wc, x, y, hid, og)


def s5_mixer_bwd(tag, u, prm, w_og, res, dmix):
    a_re, a_im, log_dt, b_re, b_im, c_re, c_im, dskip = prm
    disc_in, a1, a2, a1r, a2r, wb, wc, x, y, hid, og = res
    (dog,), _ = rowmap_bwd("s5_glu_bwd_" + tag, _glu_tile, [og], [], [dmix], TM_ROW)
    dw_og = mm_tn("s5_og_dw_" + tag, hid, dog)
    dhid = mm_nt("s5_og_dx_" + tag, dog, w_og)
    (dy,), _ = rowmap_bwd("s5_gelu_bwd_" + tag, _gelu_tile, [y], [], [dhid], TM_ROW)
    du, dwb, dwc, ddskip, q1, q2 = s5_core_bwd(tag, u, x, wb, wc, a1, a2, a1r, a2r, dskip.reshape(1, D_MODEL), dy)
    dbbar_re, dbbar_im, dc_re, dc_im = s5_unpack_weight_grads(dwb, dwc)
    dabar_re = q1.sum(0).reshape(S5_GROUPS, S5_STATE)
    dabar_im = q2.sum(0).reshape(S5_GROUPS, S5_STATE)
    grads, _ = rowmap_bwd("s5_disc_bwd_" + tag, _s5_disc_tile, disc_in, [_s5_expand()],
                          [dabar_re, dabar_im, dbbar_re, dbbar_im], S5_GROUPS, par_mask=[False])
    da_re, da_im, dlog_dt, db_re, db_im = grads
    return du, (da_re, da_im, dlog_dt.reshape(S5_GROUPS), db_re.reshape(b_re.shape), db_im.reshape(b_im.shape),
                dc_re, dc_im, ddskip.reshape(D_MODEL)), dw_og


HYB_IN = 3592
_IN_B0, _IN_SW0 = 2048, 2056


IN_SHARD = HYB_IN // 4
SHARD_ORDER_GRADS = ("hyb_w_in", "ffn_wg", "ffn_wu", "ffn_wd")
FFN_TRANSPOSED = ("ffn_wg", "ffn_wu")


def _w_in_pieces():
    runs = [(0, _IN_B0, 0), (_IN_B0, _IN_SW0, COL_BA), (_IN_SW0, HYB_IN, _IN_B0)]
    out = []
    for sh in range(4):
        lo, hi = sh * IN_SHARD, (sh + 1) * IN_SHARD
        for r_lo, r_hi, c_lo in runs:
            a, b = max(lo, r_lo), min(hi, r_hi)
            if a < b:
                out.append((sh, a - lo, b - lo, c_lo + a - r_lo))
    return out


def w_in_to_canonical(tag, layer, w4):
    tr = 128

    def body(w_ref, o_ref):
        o_ref[:, COL_BA:] = jnp.zeros((tr, BA_PAD), o_ref.dtype)
        for sh, a, b, c in _w_in_pieces():
            o_ref[:, c:c + b - a] = w_ref[sh, :, a:b]

    return pl.pallas_call(
        body, grid=(D_MODEL // tr,),
        in_specs=[pl.BlockSpec((4, None, tr, IN_SHARD), lambda i: (0, layer, i, 0))],
        out_specs=pl.BlockSpec((tr, PROJ_COLS), lambda i: (i, 0)),
        out_shape=jax.ShapeDtypeStruct((D_MODEL, PROJ_COLS), w4.dtype),
        compiler_params=_cparams("parallel"), name="w_in_canon_" + tag)(w4)


def w_in_grad_to_shards(tag, g):
    tr = 128

    def body(g_ref, o_ref):
        for sh, a, b, c in _w_in_pieces():
            o_ref[sh, :, a:b] = g_ref[:, c:c + b - a]

    return pl.pallas_call(
        body, grid=(D_MODEL // tr,),
        in_specs=[pl.BlockSpec((tr, PROJ_COLS), lambda i: (i, 0))],
        out_specs=pl.BlockSpec((4, tr, IN_SHARD), lambda i: (0, i, 0)),
        out_shape=jax.ShapeDtypeStruct((4, D_MODEL, IN_SHARD), f32),
        compiler_params=_cparams("parallel"), name="w_in_grad_shards_" + tag)(g)


def _add2(name, a, b):
    return rowmap(name, lambda p, q: (p + q,), [a, b], [], [a.shape[1]], _pick(a.shape[0], (256, 128, 64, 32, 16, 8)))[0]


def local_step(x, mem, positions, target, p):
    s = x.shape[0]
    cos, sin = rope_tables(positions, s)
    row = lambda v: v.reshape(1, -1).astype(f32)
    wg4, wu4, wd4 = (p[n].astype(bf16) for n in ("ffn_wg", "ffn_wu", "ffn_wd"))
    h = h16 = x
    tape = []
    for l in range(DEPTH):
        i, tag = l // 2, str(l)
        t = {"h0": h, "h0_16": h16}
        if l % 2 == 0:
            t["w_in"] = w_in_to_canonical(tag, i, p["hyb_w_in"].astype(bf16))
            t["w_out"] = p["hyb_w_out"][i].astype(bf16)
            t["dn_prm"] = (p["dn_conv_w"][i].astype(f32), row(jnp.repeat(p["dn_a_log"][i], DN_HEAD_DIM)),
                           row(jnp.repeat(p["dn_dt_bias"][i], DN_HEAD_DIM)), row(jnp.tile(p["dn_norm_g"][i], DN_HEADS)))
            t["proj"] = mm_nn("hyb_in_" + tag, h16, t["w_in"])
            a_out, t["dn"] = dn_mixer_fwd(tag, t["proj"], *t["dn_prm"])
            b_out, t["swa"] = swa_mixer_fwd(tag, t["proj"], cos, sin)
            t["mixed"] = jnp.concatenate([a_out, b_out], 1)
            mix = mm_nn("hyb_out_" + tag, t["mixed"], t["w_out"])
        else:
            t["s5_prm"] = tuple(p[n][i].astype(f32) for n in
                                ("s5_a_re", "s5_a_im", "s5_log_dt", "s5_b_re", "s5_b_im", "s5_c_re", "s5_c_im", "s5_d"))
            t["w_og"] = jnp.concatenate([p["s5_glu_wo"][i], p["s5_glu_wg"][i]], 1).astype(bf16)
            mix, t["s5"] = s5_mixer_fwd(tag, h, t["s5_prm"], t["w_og"])
        t["mix"] = mix
        t["ln"] = [(row(p[g][l]), row(p[b][l])) for g, b in
                   (("ln_mix_g", "ln_mix_b"), ("ln_x_g", "ln_x_b"), ("ln_ffn_g", "ln_ffn_b"))]
        t["h1"], t["h1_16"] = postnorm_fwd("mix" + tag, h, mix, *t["ln"][0])
        t["wq"], t["wo"] = p["xq_w"][l].astype(bf16), p["xo_w"][l].astype(bf16)
        t["wkv"] = jnp.concatenate([p["xk_w"][l], p["xv_w"][l]], 1).astype(bf16)
        t["xo"], t["xres"] = xattn_fwd(tag, t["h1_16"], mem, t["wq"], t["wkv"], t["wo"])
        t["h2"], t["h2_16"] = postnorm_fwd("x" + tag, t["h1"], t["xo"], *t["ln"][1])
        t["fo"], t["fres"] = ffn_fwd(tag, l, t["h2_16"], wg4, wu4, wd4)
        h, h16 = postnorm_fwd("ffn" + tag, t["h2"], t["fo"], *t["ln"][2])
        tape.append(t)

    part, dh = loss_head(h, target)
    loss = jnp.sum(part)

    g = {n: [None] * v.shape[1 if n in SHARD_ORDER_GRADS else 0] for n, v in p.items()}
    for l in reversed(range(DEPTH)):
        i, tag, t = l // 2, str(l), tape[l]
        dh2a, dfo, dg, db = postnorm_bwd("ffn" + tag, t["h2"], t["fo"], *t["ln"][2], dh)
        g["ln_ffn_g"][l], g["ln_ffn_b"][l] = dg[0], db[0]
        dh2b, g["ffn_wg"][l], g["ffn_wu"][l], g["ffn_wd"][l] = ffn_bwd(tag, l, t["h2_16"], wg4, wu4, wd4, t["fres"], dfo)
        dh1a, dxo, dg, db = postnorm_bwd("x" + tag, t["h1"], t["xo"], *t["ln"][1], [dh2a, dh2b])
        g["ln_x_g"][l], g["ln_x_b"][l] = dg[0], db[0]
        dh1b, g["xq_w"][l], dwkv, g["xo_w"][l] = xattn_bwd(tag, t["h1_16"], mem, t["wq"], t["wkv"], t["wo"], t["xres"], dxo)
        g["xk_w"][l], g["xv_w"][l] = dwkv[:, :D_MODEL], dwkv[:, D_MODEL:]
        dh0a, dmix, dg, db = postnorm_bwd("mix" + tag, t["h0"], t["mix"], *t["ln"][0], [dh1a, dh1b])
        g["ln_mix_g"][l], g["ln_mix_b"][l] = dg[0], db[0]
        if l % 2 == 0:
            g["hyb_w_out"][i] = mm_tn("hyb_out_dw_" + tag, t["mixed"], dmix)
            dmixed = mm_nt("hyb_out_dx_" + tag, dmix, t["w_out"])
            dqkv, dz, dba, dcw, dalog, ddtb, dng = dn_mixer_bwd(tag, t["proj"], *t["dn_prm"], t["dn"], (dmixed, DN_KEY_DIM, 0))
            g["dn_conv_w"][i] = dcw
            g["dn_a_log"][i] = dalog.reshape(DN_HEADS, DN_HEAD_DIM).sum(1)
            g["dn_dt_bias"][i] = ddtb.reshape(DN_HEADS, DN_HEAD_DIM).sum(1)
            g["dn_norm_g"][i] = dng.reshape(DN_HEADS, DN_HEAD_DIM).sum(0)
            dq, dk, dv = swa_mixer_bwd(tag, cos, sin, t["swa"], (dmixed, SW_DIM, 1))
            dproj = jnp.concatenate([dqkv, dz, dq, dk, dv, dba], 1)
            g["hyb_w_in"][i] = w_in_grad_to_shards(tag, mm_tn("hyb_in_dw_" + tag, t["h0_16"], dproj))
            dh0b = mm_nt("hyb_in_dx_" + tag, dproj, t["w_in"])
        else:
            dh0b, dprm, dw_og = s5_mixer_bwd(tag, t["h0"], t["s5_prm"], t["w_og"], t["s5"], dmix)
            for n, v in zip(("s5_a_re", "s5_a_im", "s5_log_dt", "s5_b_re", "s5_b_im", "s5_c_re", "s5_c_im", "s5_d"), dprm):
                g[n][i] = v
            g["s5_glu_wo"][i], g["s5_glu_wg"][i] = dw_og[:, :D_MODEL], dw_og[:, D_MODEL:]
        dh = [dh0a, dh0b]
    grad_x = _add2("grad_x", dh[0], dh[1])
    grads = {n: jnp.stack(v, 1 if n in SHARD_ORDER_GRADS else 0) for n, v in g.items()}
    return loss, grad_x, grads


WEIGHT_NAMES = ("hyb_w_in", "dn_conv_w", "dn_a_log", "dn_dt_bias", "dn_norm_g", "hyb_w_out", "s5_a_re", "s5_a_im",
                "s5_log_dt", "s5_b_re", "s5_b_im", "s5_c_re", "s5_c_im", "s5_d", "s5_glu_wo", "s5_glu_wg",
                "ln_mix_g", "ln_mix_b", "xq_w", "xk_w", "xv_w", "xo_w", "ln_x_g", "ln_x_b",
                "ffn_wg", "ffn_wu", "ffn_wd", "ln_ffn_g", "ln_ffn_b")
SHARD_AXIS = {"hyb_w_in": 2, "dn_conv_w": 2, "hyb_w_out": 1, "s5_d": 1, "s5_glu_wo": 1, "s5_glu_wg": 1,
              "xq_w": 1, "xk_w": 1, "xv_w": 1, "xo_w": 1, "ffn_wg": 2, "ffn_wu": 2, "ffn_wd": 1}
GATHER_F32 = ("dn_conv_w", "s5_d")
N_CHIPS = 4
PACK_COLS = 1024
_ANY = pl.BlockSpec(memory_space=pl.ANY)


def _pos():
    return lax.axis_index("x"), lax.axis_index("y"), lax.axis_index("c")


def _chip_peers(mx, my):
    return [(1 - mx, my), (mx, 1 - my), (1 - mx, 1 - my)]


def _rcopy(src, dst, ssem, rsem, dev):
    return pltpu.make_async_remote_copy(src_ref=src, dst_ref=dst, send_sem=ssem, recv_sem=rsem,
                                        device_id=dev, device_id_type=pl.DeviceIdType.MESH)


def comm_allgather4(name, x):
    def body(x_ref, o_ref, ssem, rsem, lsem):
        mx, my, mc = _pos()
        me = 2 * mx + my
        peers = _chip_peers(mx, my)
        loc = pltpu.make_async_copy(x_ref, o_ref.at[me], lsem)
        loc.start()
        sends = [_rcopy(x_ref, o_ref.at[me], ssem.at[k], rsem.at[k], (px, py, mc)) for k, (px, py) in enumerate(peers)]
        for cp in sends:
            cp.start()
        for k, (px, py) in enumerate(peers):
            _rcopy(x_ref, o_ref.at[2 * px + py], ssem.at[k], rsem.at[k], (px, py, mc)).wait_recv()
        for cp in sends:
            cp.wait_send()
        loc.wait()

    return pl.pallas_call(
        body, out_shape=jax.ShapeDtypeStruct((N_CHIPS,) + x.shape, x.dtype), in_specs=[_ANY], out_specs=_ANY,
        scratch_shapes=[pltpu.SemaphoreType.DMA((3,)), pltpu.SemaphoreType.DMA((3,)), pltpu.SemaphoreType.DMA],
        name=name)(x)


def _multi_call(name, body, ins, out_shapes, sems, in_place=False):
    return pl.pallas_call(
        body, out_shape=out_shapes, in_specs=[_ANY] * len(ins), out_specs=[_ANY] * len(out_shapes),
        scratch_shapes=sems, input_output_aliases={w: w for w in range(len(ins))} if in_place else {},
        name=name)(*ins)


def comm_gather_weights(name, slots):
    n = len(slots)

    def body(*refs):
        os_ = refs[n:2 * n]
        ssem, rsem, fssem, frsem = refs[2 * n:]
        mx, my, mc = _pos()
        me = 2 * mx + my
        peers = _chip_peers(mx, my)
        sib = (mx, my, 1 - mc)
        half = [o.shape[1] // 2 for o in os_]
        mine = [pl.ds(mc * h, h) for h in half]
        other = [pl.ds((1 - mc) * h, h) for h in half]
        sends = [_rcopy(os_[w].at[me, mine[w]], os_[w].at[me, mine[w]], ssem.at[w, k], rsem.at[w, k], (px, py, mc))
                 for w in range(n) for k, (px, py) in enumerate(peers)]
        for cp in sends:
            cp.start()
        fwds = []
        for w in range(n):
            for k, (px, py) in enumerate(peers):
                landed = os_[w].at[2 * px + py, mine[w]]
                _rcopy(landed, landed, ssem.at[w, k], rsem.at[w, k], (px, py, mc)).wait_recv()
                fw = _rcopy(landed, landed, fssem.at[w, k], frsem.at[w, k], sib)
                fw.start()
                fwds.append(fw)
        for w in range(n):
            for k, (px, py) in enumerate(peers):
                theirs = os_[w].at[2 * px + py, other[w]]
                _rcopy(theirs, theirs, fssem.at[w, k], frsem.at[w, k], sib).wait_recv()
        for cp in sends + fwds:
            cp.wait_send()

    dma = pltpu.SemaphoreType.DMA
    return _multi_call(name, body, slots, [jax.ShapeDtypeStruct(x.shape, x.dtype) for x in slots],
                       [dma((n, 3)), dma((n, 3)), dma((n, 3)), dma((n, 3))], in_place=True)


def comm_sibling_halves(name, gs):
    n = len(gs)

    def body(*refs):
        xs, os_ = refs[:n], refs[n:2 * n]
        ssem, rsem = refs[2 * n:]
        mx, my, mc = _pos()
        sib = (mx, my, 1 - mc)
        sends = []
        for w in range(n):
            h = xs[w].shape[1] // 2
            for j in range(N_CHIPS):
                sends.append(_rcopy(xs[w].at[j, pl.ds((1 - mc) * h, h)], os_[w].at[j], ssem.at[w, j], rsem.at[w, j], sib))
        for cp in sends:
            cp.start()
        for w in range(n):
            for j in range(N_CHIPS):
                _rcopy(os_[w].at[j], os_[w].at[j], ssem.at[w, j], rsem.at[w, j], sib).wait_recv()
        for cp in sends:
            cp.wait_send()

    dma = pltpu.SemaphoreType.DMA
    return _multi_call(name, body, gs,
                       [jax.ShapeDtypeStruct((N_CHIPS, g.shape[1] // 2) + g.shape[2:], g.dtype) for g in gs],
                       [dma((n, N_CHIPS)), dma((n, N_CHIPS))])


def comm_alltoall4(name, xs):
    n = len(xs)

    def body(*refs):
        xr, os_ = refs[:n], refs[n:2 * n]
        ssem, rsem = refs[2 * n:]
        mx, my, mc = _pos()
        me = 2 * mx + my
        peers = _chip_peers(mx, my)
        sends = [_rcopy(xr[w].at[2 * px + py], os_[w].at[me], ssem.at[w, k], rsem.at[w, k], (px, py, mc))
                 for w in range(n) for k, (px, py) in enumerate(peers)]
        for cp in sends:
            cp.start()
        for w in range(n):
            for k, (px, py) in enumerate(peers):
                dst = os_[w].at[2 * px + py]
                _rcopy(dst, dst, ssem.at[w, k], rsem.at[w, k], (px, py, mc)).wait_recv()
        for cp in sends:
            cp.wait_send()

    dma = pltpu.SemaphoreType.DMA
    return _multi_call(name, body, xs, [jax.ShapeDtypeStruct(x.shape, x.dtype) for x in xs], [dma((n, 3)), dma((n, 3))])


def comm_sibling_join(name, bs):
    n = len(bs)

    def body(*refs):
        os_ = refs[n:2 * n]
        ssem, rsem = refs[2 * n:]
        mx, my, mc = _pos()
        sib = (mx, my, 1 - mc)
        sends = [_rcopy(os_[w].at[mc], os_[w].at[mc], ssem.at[w], rsem.at[w], sib) for w in range(n)]
        for cp in sends:
            cp.start()
        for w in range(n):
            dst = os_[w].at[1 - mc]
            _rcopy(dst, dst, ssem.at[w], rsem.at[w], sib).wait_recv()
        for cp in sends:
            cp.wait_send()

    dma = pltpu.SemaphoreType.DMA
    return _multi_call(name, body, bs, [jax.ShapeDtypeStruct(b.shape, b.dtype) for b in bs], [dma((n,)), dma((n,))],
                       in_place=True)


def comm_sibling_swap(name, x):
    def body(x_ref, o_ref, ssem, rsem):
        mx, my, mc = _pos()
        cp = _rcopy(x_ref, o_ref, ssem, rsem, (mx, my, 1 - mc))
        cp.start()
        cp.wait_recv()
        cp.wait_send()

    return pl.pallas_call(
        body, out_shape=jax.ShapeDtypeStruct(x.shape, x.dtype), in_specs=[_ANY], out_specs=_ANY,
        scratch_shapes=[pltpu.SemaphoreType.DMA, pltpu.SemaphoreType.DMA], name=name)(x)


def _row_tile(r):
    return _pick(r, (256, 128, 64, 32, 16, 8))


def add_own_half(name, g, recv, out_dtype):
    r, c = g.shape[2:]
    tr = _row_tile(r)
    mc = lax.axis_index("c").astype(jnp.int32).reshape(1)

    def body(c_ref, g_ref, r_ref, o_ref):
        o_ref[...] = (g_ref[...] + r_ref[...]).astype(o_ref.dtype)

    grid_spec = pltpu.PrefetchScalarGridSpec(
        num_scalar_prefetch=1, grid=(N_CHIPS, r // tr),
        in_specs=[pl.BlockSpec((None, None, tr, c), lambda j, i, cr: (j, cr[0], i, 0)),
                  pl.BlockSpec((None, tr, c), lambda j, i, cr: (j, i, 0))],
        out_specs=pl.BlockSpec((None, tr, c), lambda j, i, cr: (j, i, 0)))
    return pl.pallas_call(body, grid_spec=grid_spec, out_shape=jax.ShapeDtypeStruct(recv.shape, out_dtype),
                          compiler_params=_cparams("parallel", "parallel"), name=name)(mc, g, recv)


def cast_into_slot(name, w, chip, dtype):
    r, c = w.shape
    tr = _row_tile(r)

    def body(c_ref, w_ref, o_ref):
        o_ref[...] = w_ref[...].astype(o_ref.dtype)

    grid_spec = pltpu.PrefetchScalarGridSpec(
        num_scalar_prefetch=1, grid=(r // tr,),
        in_specs=[pl.BlockSpec((tr, c), lambda i, cr: (i, 0))],
        out_specs=pl.BlockSpec((None, tr, c), lambda i, cr: (cr[0], i, 0)))
    return pl.pallas_call(body, grid_spec=grid_spec, out_shape=jax.ShapeDtypeStruct((N_CHIPS, r, c), dtype),
                          compiler_params=_cparams("parallel"), name=name)(chip.astype(jnp.int32).reshape(1), w)


def sum_chips_into_half(name, own, arrived, chip, mc):
    r, c = own.shape[1:]
    tr = _row_tile(r)

    def body(s0, s1, s2, s3, s4, own_ref, a_ref, b_ref, d_ref, o_ref):
        o_ref[...] = ((own_ref[...].astype(f32) + a_ref[...].astype(f32))
                      + (b_ref[...].astype(f32) + d_ref[...].astype(f32)))

    slot = lambda k: pl.BlockSpec((None, tr, c), lambda i, *sc, _k=k: (sc[_k][0], i, 0))
    grid_spec = pltpu.PrefetchScalarGridSpec(
        num_scalar_prefetch=5, grid=(r // tr,), in_specs=[slot(0), slot(1), slot(2), slot(3)],
        out_specs=pl.BlockSpec((None, tr, c), lambda i, *sc: (sc[4][0], i, 0)))
    mx, my = lax.axis_index("x"), lax.axis_index("y")
    scal = [v.astype(jnp.int32).reshape(1) for v in
            (2 * mx + my, 2 * (1 - mx) + my, 2 * mx + (1 - my), 2 * (1 - mx) + (1 - my), mc)]
    return pl.pallas_call(body, grid_spec=grid_spec, out_shape=jax.ShapeDtypeStruct((2, r, c), f32),
                          compiler_params=_cparams("parallel"), name=name)(*scal, own, arrived, arrived, arrived)


def sum_slots(name, x):
    r, c = x.shape[1:]
    tr = _row_tile(r)

    def body(x_ref, o_ref):
        o_ref[...] = (x_ref[0].astype(f32) + x_ref[1].astype(f32)) + (x_ref[2].astype(f32) + x_ref[3].astype(f32))

    return pl.pallas_call(
        body, grid=(r // tr,), in_specs=[pl.BlockSpec((N_CHIPS, tr, c), lambda i: (0, i, 0))],
        out_specs=pl.BlockSpec((tr, c), lambda i: (i, 0)), out_shape=jax.ShapeDtypeStruct((r, c), f32),
        compiler_params=_cparams("parallel"), name=name)(x)


def adamw(name, w, g, m, v):
    r, c = w.shape
    tr = _row_tile(r)

    def body(w_ref, g_ref, m_ref, v_ref, d_ref, nm_ref, nv_ref):
        gv = g_ref[...]
        nm = ADAM_B1 * m_ref[...] + (1.0 - ADAM_B1) * gv
        nv = ADAM_B2 * v_ref[...] + (1.0 - ADAM_B2) * (gv * gv)
        m_hat = nm / (1.0 - ADAM_B1 ** ADAM_STEP)
        v_hat = nv / (1.0 - ADAM_B2 ** ADAM_STEP)
        d_ref[...] = -ADAM_LR * (m_hat / (jnp.sqrt(v_hat) + ADAM_EPS) + ADAM_WD * w_ref[...])
        nm_ref[...] = nm
        nv_ref[...] = nv

    blk = pl.BlockSpec((tr, c), lambda i: (i, 0))
    return pl.pallas_call(
        body, grid=(r // tr,), in_specs=[blk] * 4, out_specs=[blk] * 3,
        out_shape=[jax.ShapeDtypeStruct((r, c), f32)] * 3,
        compiler_params=_cparams("parallel"), name=name)(w, g, m, v)


def _pack_rows(n):
    return -(-n // PACK_COLS)


def _pack(arrs, dtype, row_multiple):
    segs = []
    for a in arrs:
        flat = a.astype(dtype).reshape(-1)
        k = _pack_rows(flat.shape[0])
        segs.append(jnp.pad(flat, (0, k * PACK_COLS - flat.shape[0])).reshape(k, PACK_COLS))
    rows = sum(s.shape[0] for s in segs)
    pad = -rows % row_multiple
    if pad:
        segs.append(jnp.zeros((pad, PACK_COLS), dtype))
    return jnp.concatenate(segs, 0)


def _unpack(packed, shapes):
    out, r = [], 0
    for shp in shapes:
        n = math.prod(shp)
        k = _pack_rows(n)
        out.append(packed[r:r + k].reshape(-1)[:n].reshape(shp))
        r += k
    return out


def _gathered_to_full(g, axis):
    t = jnp.moveaxis(g, 0, axis)
    return t.reshape(t.shape[:axis] + (t.shape[axis] * t.shape[axis + 1],) + t.shape[axis + 2:])


def _full_to_shard_major(full, axis):
    shp = full.shape
    t = full.reshape(shp[:axis] + (N_CHIPS, shp[axis] // N_CHIPS) + shp[axis + 1:])
    return jnp.moveaxis(t, axis, 0)


GRAD_ROW_MULTIPLE = 256


def kernel(x, mem, positions, hyb_w_in, dn_conv_w, dn_a_log, dn_dt_bias, dn_norm_g, hyb_w_out, s5_a_re, s5_a_im, s5_log_dt, s5_b_re, s5_b_im, s5_c_re, s5_c_im, s5_d, s5_glu_wo, s5_glu_wg, ln_mix_g, ln_mix_b, xq_w, xk_w, xv_w, xo_w, ln_x_g, ln_x_b, ffn_wg, ffn_wu, ffn_wd, ln_ffn_g, ln_ffn_b, loss_target, m_hyb_w_in, m_dn_conv_w, m_dn_a_log, m_dn_dt_bias, m_dn_norm_g, m_hyb_w_out, m_s5_a_re, m_s5_a_im, m_s5_log_dt, m_s5_b_re, m_s5_b_im, m_s5_c_re, m_s5_c_im, m_s5_d, m_s5_glu_wo, m_s5_glu_wg, m_ln_mix_g, m_ln_mix_b, m_xq_w, m_xk_w, m_xv_w, m_xo_w, m_ln_x_g, m_ln_x_b, m_ffn_wg, m_ffn_wu, m_ffn_wd, m_ln_ffn_g, m_ln_ffn_b, v_hyb_w_in, v_dn_conv_w, v_dn_a_log, v_dn_dt_bias, v_dn_norm_g, v_hyb_w_out, v_s5_a_re, v_s5_a_im, v_s5_log_dt, v_s5_b_re, v_s5_b_im, v_s5_c_re, v_s5_c_im, v_s5_d, v_s5_glu_wo, v_s5_glu_wg, v_ln_mix_g, v_ln_mix_b, v_xq_w, v_xk_w, v_xv_w, v_xo_w, v_ln_x_g, v_ln_x_b, v_ffn_wg, v_ffn_wu, v_ffn_wd, v_ln_ffn_g, v_ln_ffn_b):
    a = dict(locals())
    big = [n for n in WEIGHT_NAMES if n in SHARD_AXIS and n not in GATHER_F32]
    small = [n for n in WEIGHT_NAMES if n not in big]
    chip = 2 * lax.axis_index("x") + lax.axis_index("y")
    for n in FFN_TRANSPOSED:
        for pre in ("", "m_", "v_"):
            a[pre + n] = jnp.swapaxes(a[pre + n], 1, 2)

    mc = lax.axis_index("c")
    view2 = lambda t: t.reshape(-1, t.shape[-1])
    slots = [cast_into_slot("slot_" + n, view2(a[n]), chip, bf16).reshape((N_CHIPS,) + a[n].shape) for n in big]
    gathered = comm_gather_weights("gather_w", slots)
    tiny4 = _unpack_slots(comm_allgather4("gather_w_tiny", _pack([a[n] for n in GATHER_F32], f32, 8)),
                          [a[n].shape for n in GATHER_F32])
    p = {n: a[n] for n in small if n not in GATHER_F32}
    for n, g4 in zip(GATHER_F32, tiny4):
        p[n] = _gathered_to_full(g4, SHARD_AXIS[n])
    for n, g4 in zip(big, gathered):
        p[n] = g4 if n in SHARD_ORDER_GRADS else _gathered_to_full(g4, SHARD_AXIS[n])

    loss, grad_x, grads = local_step(x[0], mem[0], positions, loss_target[0], p)
    loss = lax.psum(loss, ("x", "y", "c"))

    g4s = [grads[n] if n in SHARD_ORDER_GRADS else _full_to_shard_major(grads[n], SHARD_AXIS[n]) for n in big]
    recv = comm_sibling_halves("rs_sibling_halves", g4s)
    pairs = []
    for n, g4, r4 in zip(big, g4s, recv):
        lh, cols = g4.shape[1] // 2, g4.shape[-1]
        v4 = g4.reshape(N_CHIPS, 2, -1, cols)
        pairs.append(add_own_half("rs_add_" + n, v4, r4.reshape(N_CHIPS, -1, cols), bf16).reshape((N_CHIPS, lh) + g4.shape[2:]))
    arrived = comm_alltoall4("rs_alltoall", pairs)
    slot3 = lambda t: t.reshape(N_CHIPS, -1, t.shape[-1])
    halves = [sum_chips_into_half("rs_sum_" + n, slot3(pr), slot3(ar), chip, mc) for n, pr, ar in zip(big, pairs, arrived)]
    g_big = {n: t.reshape(a[n].shape) for n, t in zip(big, comm_sibling_join("rs_sibling_join", halves))}

    rpack = _pack([grads[n] for n in small], f32, 8)
    rpair = _add2("ar_add_sibling", rpack, comm_sibling_swap("ar_sibling_swap", rpack))
    g_small = _unpack(sum_slots("ar_sum_chips", comm_allgather4("ar_allgather", rpair)), [grads[n].shape for n in small])
    g_small = {n: (lax.dynamic_index_in_dim(_full_to_shard_major(g, SHARD_AXIS[n]), chip, 0, keepdims=False)
                   if n in SHARD_AXIS else g) for n, g in zip(small, g_small)}

    outs = {}
    for n in big:
        view = lambda t: t.reshape(-1, t.shape[-1])
        d, nm, nv = adamw("adamw_" + n, view(a[n]), view(g_big[n]), view(a["m_" + n]), view(a["v_" + n]))
        outs[n] = (g_big[n],) + tuple(t.reshape(a[n].shape) for t in (d, nm, nv))
    shapes = [a[n].shape for n in small]
    packs = [_pack([a[pre + n] for n in small], f32, 8) for pre in ("", "m_", "v_")]
    upd = adamw("adamw_small", packs[0], _pack([g_small[n] for n in small], f32, 8), packs[1], packs[2])
    for k, n in enumerate(small):
        outs[n] = (g_small[n],) + tuple(_unpack(buf, shapes)[k] for buf in upd)
    for n in FFN_TRANSPOSED:
        outs[n] = tuple(jnp.swapaxes(t, 1, 2) for t in outs[n])
    res = [loss, grad_x[None]]
    for kind in range(4):
        res += [outs[n][kind] for n in WEIGHT_NAMES]
    return tuple(res)


def _unpack_slots(gathered, shapes):
    out, r = [], 0
    for shp in shapes:
        n = math.prod(shp)
        k = _pack_rows(n)
        out.append(gathered[:, r:r + k].reshape(N_CHIPS, -1)[:, :n].reshape((N_CHIPS,) + tuple(shp)))
        r += k
    return out
```

```python
import functools
import math

import jax
import jax.numpy as jnp
from jax import lax
from jax.experimental import pallas as pl
from jax.experimental.pallas import tpu as pltpu

f32 = jnp.float32
bf16 = jnp.bfloat16

D_MODEL = 1024
DEPTH = 4
DN_HEADS = 4
DN_HEAD_DIM = 128
DN_KEY_DIM = 512
DN_QKV_DIM = 1536
DN_CONV = 4
SW_HEADS = 8
SW_HEAD_DIM = 64
SW_DIM = 512
SW_DILATIONS = (1, 4, 16)
SW_BLOCK = 128
ROPE_THETA = 10000.0
S5_GROUP = 16
S5_GROUPS = 64
S5_STATE = 64
X_HEADS = 4
X_HEAD_DIM = 256
FFN_HIDDEN = 2816
ALPHA = (2 * DEPTH) ** 0.25
LN_EPS = 1e-5
RMS_EPS = 1e-6
ADAM_LR, ADAM_B1, ADAM_B2, ADAM_EPS, ADAM_WD, ADAM_STEP = 0.001, 0.9, 0.999, 1e-08, 0.01, 10

BA_PAD = 256
PROJ_COLS = DN_QKV_DIM + DN_KEY_DIM + 3 * SW_DIM + BA_PAD
COL_Z = DN_QKV_DIM
COL_SWQ = COL_Z + DN_KEY_DIM
COL_SWK = COL_SWQ + SW_DIM
COL_SWV = COL_SWK + SW_DIM
COL_BA = COL_SWV + SW_DIM

LANES = 128
SUBLANES = 8
VMEM_LIMIT = 56 * 1024 * 1024
DN_CHUNK = 128
DN_HEADS_PER_STEP = 4


def _cparams(*sem):
    return pltpu.CompilerParams(dimension_semantics=tuple(sem), vmem_limit_bytes=VMEM_LIMIT)


def _dg(x, y, cx, cy):
    return lax.dot_general(x, y, (((cx,), (cy,)), ((), ())), preferred_element_type=f32)


@functools.partial(jax.custom_vjp, nondiff_argnums=(2, 3))
def bdot(a, b, ca, cb):
    return _dg(a.astype(bf16), b.astype(bf16), ca, cb)


def _bdot_fwd(a, b, ca, cb):
    return bdot(a, b, ca, cb), (a, b)


def _bdot_bwd(ca, cb, res, g):
    a, b = res
    g16, a16, b16 = g.astype(bf16), a.astype(bf16), b.astype(bf16)
    da = _dg(g16, b16, 1, 1 - cb) if ca == 1 else _dg(b16, g16, 1 - cb, 1)
    db = _dg(a16, g16, 1 - ca, 0) if cb == 0 else _dg(g16, a16, 0, 1 - ca)
    return da.astype(a.dtype), db.astype(b.dtype)


bdot.defvjp(_bdot_fwd, _bdot_bwd)


def _split_hi_lo(a):
    hi = a.astype(bf16)
    return hi, (a - hi.astype(f32)).astype(bf16)


def _dot3(a, b, ca, cb):
    a_hi, a_lo = _split_hi_lo(a)
    b_hi, b_lo = _split_hi_lo(b)
    return _dg(a_hi, b_hi, ca, cb) + (_dg(a_hi, b_lo, ca, cb) + _dg(a_lo, b_hi, ca, cb))


def hdot(a, b):
    return jnp.dot(a, b, precision=lax.Precision.HIGHEST, preferred_element_type=f32)


def _iota2(shape, dim):
    return lax.broadcasted_iota(jnp.int32, shape, dim)


def _row_spec(r, tm):
    if isinstance(r, tuple):
        arr, width, blk = r
        return arr, pl.BlockSpec((tm, width), lambda i, _b=blk: (i, _b))
    return r, pl.BlockSpec((tm, r.shape[1]), lambda i: (i, 0))


def _par_spec(p):
    return pl.BlockSpec(p.shape, lambda i, _n=p.ndim: (0,) * _n)


def rowmap(name, fn, rows, params, out_cols, tm, out_dtypes=None):
    arrs, specs = zip(*[_row_spec(r, tm) for r in rows])
    s = arrs[0].shape[0]
    n_in = len(rows) + len(params)
    out_dtypes = out_dtypes or [f32] * len(out_cols)

    def body(*refs):
        outs = fn(*[r[...] for r in refs[:n_in]])
        for o_ref, o in zip(refs[n_in:], outs):
            o_ref[...] = o.astype(o_ref.dtype)

    return pl.pallas_call(
        body, grid=(s // tm,),
        in_specs=list(specs) + [_par_spec(p) for p in params],
        out_specs=[pl.BlockSpec((tm, c), lambda i: (i, 0)) for c in out_cols],
        out_shape=[jax.ShapeDtypeStruct((s, c), dt) for c, dt in zip(out_cols, out_dtypes)],
        compiler_params=_cparams("parallel"), name=name)(*arrs, *params)


def rowmap_bwd(name, fn, rows, params, cts, tm, row_mask=None, par_mask=None, row_dtypes=None):
    arrs, specs = zip(*[_row_spec(r, tm) for r in rows])
    s = arrs[0].shape[0]
    ct_groups = [c if isinstance(c, list) else [c] for c in cts]
    ct_arrs, ct_specs = zip(*[_row_spec(a, tm) for grp in ct_groups for a in grp])
    cts = list(ct_arrs)
    nr, npar, nct = len(rows), len(params), len(cts)
    row_mask = row_mask or [True] * nr
    par_mask = par_mask or [True] * npar
    row_idx = [k for k in range(nr) if row_mask[k]]
    par_idx = [k for k in range(npar) if par_mask[k]]
    row_w = [specs[k].block_shape[1] for k in row_idx]

    def body(*refs):
        ins = [r[...] for r in refs[:nr + npar]]
        ct_refs = list(refs[nr + npar:nr + npar + nct])
        ctv = []
        for grp in ct_groups:
            acc = ct_refs.pop(0)[...].astype(f32)
            for _ in grp[1:]:
                acc = acc + ct_refs.pop(0)[...].astype(f32)
            ctv.append(acc)
        ctv = tuple(ctv)
        outs = refs[nr + npar + nct:]
        _, vjp = jax.vjp(fn, *ins)
        grads = vjp(ctv)
        for o_ref, k in zip(outs[:len(row_idx)], row_idx):
            o_ref[...] = grads[k].astype(o_ref.dtype)
        first = pl.program_id(0) == 0
        for o_ref, k in zip(outs[len(row_idx):], par_idx):
            g = grads[nr + k].astype(f32)

            @pl.when(first)
            def _(o_ref=o_ref, g=g):
                o_ref[...] = g

            @pl.when(jnp.logical_not(first))
            def _(o_ref=o_ref, g=g):
                o_ref[...] += g

    res = pl.pallas_call(
        body, grid=(s // tm,),
        in_specs=list(specs) + [_par_spec(p) for p in params]
        + list(ct_specs),
        out_specs=[pl.BlockSpec((tm, w), lambda i: (i, 0)) for w in row_w]
        + [_par_spec(params[k]) for k in par_idx],
        out_shape=[jax.ShapeDtypeStruct((s, w), dt) for w, dt in zip(row_w, row_dtypes or [f32] * len(row_w))]
        + [jax.ShapeDtypeStruct(params[k].shape, f32) for k in par_idx],
        compiler_params=_cparams("arbitrary"), name=name)(*arrs, *params, *cts)
    return list(res[:len(row_idx)]), list(res[len(row_idx):])


def _pick(n, prefs):
    for t in prefs:
        if n % t == 0:
            return t
    return n


MM_CHUNK = 512


def mm_nn(name, a, b, out_dtype=f32):
    m, k = a.shape
    n = b.shape[1]
    tm = _pick(m, (512, 256, 128))
    cn = _pick(n, (MM_CHUNK, 256, 128))

    def body(a_ref, b_ref, o_ref):
        av = a_ref[...].astype(bf16)
        for c in range(n // cn):
            sl = slice(c * cn, (c + 1) * cn)
            o_ref[:, sl] = _dg(av, b_ref[:, sl].astype(bf16), 1, 0).astype(o_ref.dtype)

    return pl.pallas_call(
        body, grid=(m // tm,),
        in_specs=[pl.BlockSpec((tm, k), lambda i: (i, 0)), pl.BlockSpec((k, n), lambda i: (0, 0))],
        out_specs=pl.BlockSpec((tm, n), lambda i: (i, 0)),
        out_shape=jax.ShapeDtypeStruct((m, n), out_dtype),
        compiler_params=_cparams("parallel"), name=name)(a, b)


def mm_nt(name, a, b, out_dtype=f32):
    m, n = a.shape
    k = b.shape[0]
    tm = _pick(m, (512, 256, 128))
    ck = _pick(k, (MM_CHUNK, 256, 128))

    def body(a_ref, b_ref, o_ref):
        av = a_ref[...].astype(bf16)
        for c in range(k // ck):
            sl = slice(c * ck, (c + 1) * ck)
            o_ref[:, sl] = _dg(av, b_ref[sl, :].astype(bf16), 1, 1).astype(o_ref.dtype)

    return pl.pallas_call(
        body, grid=(m // tm,),
        in_specs=[pl.BlockSpec((tm, n), lambda i: (i, 0)), pl.BlockSpec((k, n), lambda i: (0, 0))],
        out_specs=pl.BlockSpec((tm, k), lambda i: (i, 0)),
        out_shape=jax.ShapeDtypeStruct((m, k), out_dtype),
        compiler_params=_cparams("parallel"), name=name)(a, b)


def _stacked(buf):
    if buf is None:
        return [], [], {}
    return [buf], [pl.BlockSpec(memory_space=pl.ANY)], None


def mm_tn(name, a, b, out_dtype=f32, b_col0=0, n_cols=None, stack=None):
    s, m = a.shape
    n = n_cols or b.shape[1]
    tn = _pick(n, (256, 128))
    cm = _pick(m, (256, 128))
    col0 = b_col0 // tn
    in_specs = [pl.BlockSpec((s, m), lambda j: (0, 0)), pl.BlockSpec((s, tn), lambda j: (0, j + col0))]

    if stack is None:
        def body(a_ref, b_ref, o_ref):
            bv = b_ref[...].astype(bf16)
            for c in range(m // cm):
                sl = slice(c * cm, (c + 1) * cm)
                o_ref[sl, :] = _dg(a_ref[:, sl].astype(bf16), bv, 0, 0).astype(o_ref.dtype)

        return pl.pallas_call(
            body, grid=(n // tn,), in_specs=in_specs, out_specs=pl.BlockSpec((m, tn), lambda j: (0, j)),
            out_shape=jax.ShapeDtypeStruct((m, n), out_dtype),
            compiler_params=_cparams("parallel"), name=name)(a, b)

    buf, layer, n_layers = stack
    assert cm * N_CHIPS == m
    extra, extra_specs, _ = _stacked(buf)

    def body_stacked(a_ref, b_ref, *rest):
        o_ref = rest[-1]
        bv = b_ref[...].astype(bf16)
        for c in range(N_CHIPS):
            o_ref[c] = _dg(a_ref[:, c * cm:(c + 1) * cm].astype(bf16), bv, 0, 0).astype(o_ref.dtype)

    return pl.pallas_call(
        body_stacked, grid=(n // tn,), in_specs=in_specs + extra_specs,
        out_specs=pl.BlockSpec((N_CHIPS, None, cm, tn), lambda j: (0, layer, 0, j)),
        out_shape=jax.ShapeDtypeStruct((N_CHIPS, n_layers, cm, n), out_dtype),
        input_output_aliases={2: 0} if extra else {},
        compiler_params=_cparams("parallel"), name=name)(a, b, *extra)


def _postnorm_tile(h, sub, g, b):
    z = ALPHA * h + sub
    mu = jnp.mean(z, -1, keepdims=True)
    zc = z - mu
    var = jnp.mean(zc * zc, -1, keepdims=True)
    return (zc * lax.rsqrt(var + LN_EPS) * g + b,)


def _swiglu_tile(au):
    a, u = au[:, :FFN_HIDDEN], au[:, FFN_HIDDEN:]
    return (jax.nn.silu(a) * u,)


def _glu_tile(og):
    o, g = og[:, :D_MODEL], og[:, D_MODEL:]
    return (o * jax.nn.sigmoid(g),)


def _xattn_tile(q, kv):
    outs = []
    for h in range(X_HEADS):
        sl = slice(h * X_HEAD_DIM, (h + 1) * X_HEAD_DIM)
        s = bdot(q[:, sl], kv[:, sl], 1, 1) * (X_HEAD_DIM ** -0.5)
        m = lax.stop_gradient(jnp.max(s, -1, keepdims=True))
        p = jnp.exp(s - m)
        p = p / jnp.sum(p, -1, keepdims=True)
        outs.append(bdot(p, kv[:, D_MODEL + h * X_HEAD_DIM:D_MODEL + (h + 1) * X_HEAD_DIM], 1, 0))
    return (jnp.concatenate(outs, -1),)


TM_ROW = 256


def postnorm_fwd(tag, h, sub, g, b):
    return rowmap("postnorm_" + tag, lambda *a: _postnorm_tile(*a) * 2, [h, sub], [g, b], [D_MODEL] * 2, TM_ROW,
                  out_dtypes=[f32, bf16])


def postnorm_bwd(tag, h, sub, g, b, dy):
    (dh, dsub), (dg, db) = rowmap_bwd("postnorm_bwd_" + tag, _postnorm_tile, [h, sub], [g, b], [dy], TM_ROW,
                                      row_dtypes=[f32, bf16])
    return dh, dsub, dg, db


def xattn_fwd(tag, h, mem, wq, wkv, wo):
    q = mm_nn("xq_" + tag, h, wq, out_dtype=bf16)
    kv = mm_nn("xkv_" + tag, mem, wkv)
    ao = rowmap("xattn_" + tag, _xattn_tile, [q], [kv], [D_MODEL], TM_ROW, out_dtypes=[bf16])[0]
    out = mm_nn("xo_" + tag, ao, wo)
    return out, (q, kv, ao)


def xattn_bwd(tag, layer, h, mem, wq, wkv, wo, res, dout, stacks):
    q, kv, ao = res
    sq, sk, sv, so = stacks
    so = mm_tn("xo_dw_" + tag, ao, dout, stack=(so, layer, DEPTH))
    dao = mm_nt("xo_dx_" + tag, dout, wo)
    (dq,), (dkv,) = rowmap_bwd("xattn_bwd_" + tag, _xattn_tile, [q], [kv], [dao], TM_ROW, row_dtypes=[bf16])
    sq = mm_tn("xq_dw_" + tag, h, dq, stack=(sq, layer, DEPTH))
    dh = mm_nt("xq_dx_" + tag, dq, wq)
    sk = mm_tn("xk_dw_" + tag, mem, dkv, n_cols=D_MODEL, stack=(sk, layer, DEPTH))
    sv = mm_tn("xv_dw_" + tag, mem, dkv, b_col0=D_MODEL, n_cols=D_MODEL, stack=(sv, layer, DEPTH))
    return dh, (sq, sk, sv, so)


FFN_SHARD = FFN_HIDDEN // 4
TM_FFN = 512


def _silu_mul(a, u):
    return jax.nn.silu(a) * u


def ffn_fwd(tag, layer, h, wg, wu, wd):
    s = h.shape[0]
    tm, fs = TM_FFN, FFN_SHARD
    w_in = pl.BlockSpec((None, None, fs, D_MODEL), lambda k, i: (k, layer, 0, 0))
    act = pl.BlockSpec((None, tm, fs), lambda k, i: (k, i, 0))

    def up_body(h_ref, wg_ref, wu_ref, a_ref, u_ref, hid_ref):
        hv = h_ref[...].astype(bf16)
        a, u = _dg(hv, wg_ref[...], 1, 1), _dg(hv, wu_ref[...], 1, 1)
        a_ref[...], u_ref[...] = a.astype(bf16), u.astype(bf16)
        hid_ref[...] = _silu_mul(a, u).astype(bf16)

    a4, u4, hid4 = pl.pallas_call(
        up_body, grid=(4, s // tm),
        in_specs=[pl.BlockSpec((tm, D_MODEL), lambda k, i: (i, 0)), w_in, w_in],
        out_specs=[act, act, act],
        out_shape=[jax.ShapeDtypeStruct((4, s, fs), bf16)] * 3,
        compiler_params=_cparams("parallel", "parallel"), name="ffn_up_" + tag)(h, wg, wu)

    all_act = pl.BlockSpec((4, tm, fs), lambda i: (0, i, 0))
    all_w = pl.BlockSpec((4, None, fs, D_MODEL), lambda i: (0, layer, 0, 0))

    def down_body(hid_ref, wd_ref, o_ref):
        acc = _dg(hid_ref[0], wd_ref[0], 1, 0)
        for k in range(1, 4):
            acc = acc + _dg(hid_ref[k], wd_ref[k], 1, 0)
        o_ref[...] = acc

    out = pl.pallas_call(
        down_body, grid=(s // tm,), in_specs=[all_act, all_w],
        out_specs=pl.BlockSpec((tm, D_MODEL), lambda i: (i, 0)),
        out_shape=jax.ShapeDtypeStruct((s, D_MODEL), f32),
        compiler_params=_cparams("parallel"), name="ffn_down_" + tag)(hid4, wd)
    return out, (a4, u4, hid4)


def ffn_bwd(tag, layer, h, wg, wu, wd, res, dout, stacks=None):
    a4, u4, hid4 = res
    s = h.shape[0]
    tm, fs = TM_FFN, FFN_SHARD
    act = pl.BlockSpec((None, tm, fs), lambda k, i: (k, i, 0))

    def dact_body(do_ref, wd_ref, a_ref, u_ref, da_ref, du_ref):
        dhid = _dg(do_ref[...].astype(bf16), wd_ref[...], 1, 1)
        _, vjp = jax.vjp(_silu_mul, a_ref[...].astype(f32), u_ref[...].astype(f32))
        da, du = vjp(dhid)
        da_ref[...], du_ref[...] = da.astype(bf16), du.astype(bf16)

    da4, du4 = pl.pallas_call(
        dact_body, grid=(4, s // tm),
        in_specs=[pl.BlockSpec((tm, D_MODEL), lambda k, i: (i, 0)),
                  pl.BlockSpec((None, None, fs, D_MODEL), lambda k, i: (k, layer, 0, 0)), act, act],
        out_specs=[act, act], out_shape=[jax.ShapeDtypeStruct((4, s, fs), bf16)] * 2,
        compiler_params=_cparams("parallel", "parallel"), name="ffn_dact_" + tag)(dout, wd, a4, u4)

    all_act = pl.BlockSpec((4, tm, fs), lambda i: (0, i, 0))
    all_w = pl.BlockSpec((4, None, fs, D_MODEL), lambda i: (0, layer, 0, 0))

    def dx_body(da_ref, du_ref, wg_ref, wu_ref, o_ref):
        acc = _dg(da_ref[0], wg_ref[0], 1, 0) + _dg(du_ref[0], wu_ref[0], 1, 0)
        for k in range(1, 4):
            acc = acc + (_dg(da_ref[k], wg_ref[k], 1, 0) + _dg(du_ref[k], wu_ref[k], 1, 0))
        o_ref[...] = acc

    dh = pl.pallas_call(
        dx_body, grid=(s // tm,), in_specs=[all_act, all_act, all_w, all_w],
        out_specs=pl.BlockSpec((tm, D_MODEL), lambda i: (i, 0)),
        out_shape=jax.ShapeDtypeStruct((s, D_MODEL), f32),
        compiler_params=_cparams("parallel"), name="ffn_dx_" + tag)(da4, du4, wg, wu)

    tn = 256
    whole = pl.BlockSpec((None, s, fs), lambda k, j: (k, 0, 0))

    def dwin_body(h_ref, da_ref, du_ref, *rest):
        dwg_ref, dwu_ref = rest[-2:]
        hv = h_ref[...].astype(bf16)
        dwg_ref[...] = _dg(da_ref[...], hv, 0, 0)
        dwu_ref[...] = _dg(du_ref[...], hv, 0, 0)

    n_layers = wd.shape[1]
    layer_out = pl.BlockSpec((None, None, fs, tn), lambda k, j: (k, layer, 0, j))
    stack_shape = jax.ShapeDtypeStruct((4, n_layers, fs, D_MODEL), f32)
    prev = [] if stacks is None else list(stacks)
    any_spec = [pl.BlockSpec(memory_space=pl.ANY)]

    dwg, dwu = pl.pallas_call(
        dwin_body, grid=(4, D_MODEL // tn),
        in_specs=[pl.BlockSpec((s, tn), lambda k, j: (0, j)), whole, whole] + any_spec * len(prev[:2]),
        out_specs=[layer_out] * 2, out_shape=[stack_shape] * 2,
        input_output_aliases={3: 0, 4: 1} if prev else {},
        compiler_params=_cparams("parallel", "parallel"), name="ffn_dwin_" + tag)(h, da4, du4, *prev[:2])

    def dwd_body(hid_ref, do_ref, *rest):
        rest[-1][...] = _dg(hid_ref[...], do_ref[...].astype(bf16), 0, 0)

    dwd = pl.pallas_call(
        dwd_body, grid=(4, D_MODEL // tn),
        in_specs=[whole, pl.BlockSpec((s, tn), lambda k, j: (0, j))] + any_spec * len(prev[2:]),
        out_specs=layer_out, out_shape=stack_shape,
        input_output_aliases={2: 0} if prev else {},
        compiler_params=_cparams("parallel", "parallel"), name="ffn_dwd_" + tag)(hid4, dout, *prev[2:])
    return dh, (dwg, dwu, dwd)


def loss_head(y, target):
    s, d = y.shape
    tm = TM_ROW

    def body(y_ref, t_ref, part_ref, dy_ref):
        e = y_ref[...] - t_ref[...]
        dy_ref[...] = e * (1.0 / d)
        p = jnp.sum(e * e, 0, keepdims=True) * (0.5 / d)

        @pl.when(pl.program_id(0) == 0)
        def _():
            part_ref[...] = p

        @pl.when(pl.program_id(0) != 0)
        def _():
            part_ref[...] += p

    return pl.pallas_call(
        body, grid=(s // tm,),
        in_specs=[pl.BlockSpec((tm, d), lambda i: (i, 0))] * 2,
        out_specs=[pl.BlockSpec((1, d), lambda i: (0, 0)), pl.BlockSpec((tm, d), lambda i: (i, 0))],
        out_shape=[jax.ShapeDtypeStruct((1, d), f32), jax.ShapeDtypeStruct((s, d), f32)],
        compiler_params=_cparams("arbitrary"), name="loss_head")(y, target)


TM_CONV = 512


def _conv_rows(xx, w_ref, n_rows):
    a = w_ref[3:4, :] * xx
    for k in (1, 2, 3):
        a = a + w_ref[3 - k:4 - k, :] * pltpu.roll(xx, k, 0)
    return a


def _dn_act(a, is_qk):
    s = jax.nn.silu(a)
    n = s * lax.rsqrt(jnp.sum(s * s, -1, keepdims=True) + RMS_EPS)
    return jnp.where(is_qk, n, s)


def dn_conv_fwd(tag, proj, cw):
    s = proj.shape[0]
    tm, hb = TM_CONV, TM_CONV // SUBLANES

    def body(xh_ref, x_ref, w_ref, o_ref):
        j, t = pl.program_id(0), pl.program_id(1)
        halo = jnp.where(t > 0, xh_ref[...], 0.0)
        xx = jnp.concatenate([halo, x_ref[...]], 0)
        a = _conv_rows(xx, w_ref, tm + SUBLANES)
        o_ref[...] = _dn_act(a, j < 2 * DN_HEADS)[SUBLANES:, :]

    return pl.pallas_call(
        body, grid=(DN_QKV_DIM // LANES, s // tm),
        in_specs=[pl.BlockSpec((SUBLANES, LANES), lambda j, t: (jnp.maximum(t * hb - 1, 0), j)),
                  pl.BlockSpec((tm, LANES), lambda j, t: (t, j)),
                  pl.BlockSpec((DN_CONV, LANES), lambda j, t: (0, j))],
        out_specs=pl.BlockSpec((tm, LANES), lambda j, t: (t, j)),
        out_shape=jax.ShapeDtypeStruct((s, DN_QKV_DIM), f32),
        compiler_params=_cparams("parallel", "parallel"), name="dn_conv_" + tag)(proj, proj, cw)


def dn_conv_bwd(tag, proj, cw, dy):
    s = proj.shape[0]
    tm, hb = TM_CONV, TM_CONV // SUBLANES
    nt = s // tm
    n_ext = tm + 2 * SUBLANES

    def body(xb_ref, x_ref, xa_ref, dy_ref, dya_ref, w_ref, dx_ref, dw_ref):
        j, t = pl.program_id(0), pl.program_id(1)
        xx = jnp.concatenate([jnp.where(t > 0, xb_ref[...], 0.0), x_ref[...],
                              jnp.where(t < nt - 1, xa_ref[...], 0.0)], 0)
        dyy = jnp.concatenate([jnp.zeros((SUBLANES, LANES), f32), dy_ref[...],
                               jnp.where(t < nt - 1, dya_ref[...], 0.0)], 0)
        a = _conv_rows(xx, w_ref, n_ext)
        _, vjp = jax.vjp(lambda v: _dn_act(v, j < 2 * DN_HEADS), a)
        da, = vjp(dyy)
        dx = w_ref[3:4, :] * da
        for k in (1, 2, 3):
            dx = dx + w_ref[3 - k:4 - k, :] * pltpu.roll(da, n_ext - k, 0)
        dx_ref[...] = dx[SUBLANES:SUBLANES + tm, :]
        row = _iota2((n_ext, LANES), 0)
        da_in = jnp.where((row >= SUBLANES) & (row < SUBLANES + tm), da, 0.0)
        r8 = _iota2((SUBLANES, LANES), 0)
        dw = jnp.zeros((SUBLANES, LANES), f32)
        for k in range(DN_CONV):
            xs = xx if k == 0 else pltpu.roll(xx, k, 0)
            dw = dw + jnp.where(r8 == 3 - k, jnp.sum(da_in * xs, 0, keepdims=True), 0.0)

        @pl.when(t == 0)
        def _():
            dw_ref[...] = dw

        @pl.when(t != 0)
        def _():
            dw_ref[...] += dw

    nb8 = s // SUBLANES
    return pl.pallas_call(
        body, grid=(DN_QKV_DIM // LANES, nt),
        in_specs=[pl.BlockSpec((SUBLANES, LANES), lambda j, t: (jnp.maximum(t * hb - 1, 0), j)),
                  pl.BlockSpec((tm, LANES), lambda j, t: (t, j)),
                  pl.BlockSpec((SUBLANES, LANES), lambda j, t: (jnp.minimum((t + 1) * hb, nb8 - 1), j)),
                  pl.BlockSpec((tm, LANES), lambda j, t: (t, j)),
                  pl.BlockSpec((SUBLANES, LANES), lambda j, t: (jnp.minimum((t + 1) * hb, nb8 - 1), j)),
                  pl.BlockSpec((DN_CONV, LANES), lambda j, t: (0, j))],
        out_specs=[pl.BlockSpec((tm, LANES), lambda j, t: (t, j)),
                   pl.BlockSpec((SUBLANES, LANES), lambda j, t: (0, j))],
        out_shape=[jax.ShapeDtypeStruct((s, DN_QKV_DIM), f32), jax.ShapeDtypeStruct((SUBLANES, DN_QKV_DIM), f32)],
        compiler_params=_cparams("parallel", "arbitrary"), name="dn_conv_bwd_" + tag)(proj, proj, proj, dy, dy, cw)


def _gate_tile(ba, eb, ea, alog, dtb):
    beta = jax.nn.sigmoid(hdot(ba, eb))
    g = -jnp.exp(alog) * jax.nn.softplus(hdot(ba, ea) + dtb)
    return beta, g


def _each(fn, *lists):
    return [fn(*args) for args in zip(*lists)]


@functools.partial(jax.custom_vjp, nondiff_argnums=(1,))
def _halves(x, axis):
    h = x.shape[axis] // 2
    return (x[:h], x[h:]) if axis == 0 else (x[:, :h], x[:, h:])


def _halves_fwd(x, axis):
    return _halves(x, axis), None


def _halves_bwd(axis, _, g):
    return (jnp.concatenate(g, axis),)


_halves.defvjp(_halves_fwd, _halves_bwd)


def _tri_inv_unit(lowers):
    c = lowers[0].shape[0]
    r, col = _iota2((c, c), 0), _iota2((c, c), 1)
    eye = jnp.where(r == col, 1.0, 0.0).astype(f32)
    invs = None
    sh = 0
    while (1 << sh) < c:
        same_2b = lax.shift_right_logical(r, sh + 1) == lax.shift_right_logical(col, sh + 1)
        diff_b = lax.shift_right_logical(r, sh) != lax.shift_right_logical(col, sh)
        offs = [jnp.where(same_2b & diff_b, low, 0.0) for low in lowers]
        if invs is None:
            invs = [eye - off for off in offs]
        else:
            part = _each(lambda inv, off: _dot3(inv, off, 1, 0), invs, offs)
            invs = _each(lambda inv, p: inv - _dot3(p, inv, 1, 0), invs, part)
        sh += 1
    return invs


@jax.custom_vjp
def _known_inverse(lower, tinv):
    return tinv


def _known_inverse_fwd(lower, tinv):
    return tinv, tinv


def _known_inverse_bwd(tinv, g):
    tt = tinv.T
    return -hdot(hdot(tt, g), tt), jnp.zeros_like(tinv)


_known_inverse.defvjp(_known_inverse_fwd, _known_inverse_bwd)


def _delta_chunk(q, k, v, gb, betab, state, tinv_known=None):
    c, hd = DN_CHUNK, DN_HEAD_DIM
    r, col = _iota2((c, c), 0), _iota2((c, c), 1)
    causal, strict = r >= col, r > col
    tril = jnp.where(causal, 1.0, 0.0).astype(f32)
    gc = _each(lambda g: hdot(tril, g), gb)
    decay = _each(lambda g: jnp.where(causal, jnp.exp(jnp.where(causal, g - g.T, 0.0)), 0.0), gc)
    qs = _each(lambda t: t * (DN_HEAD_DIM ** -0.5), q)
    kb = _each(lambda a, b: a * b, k, betab)
    kq = _each(lambda a, b, kk: _halves(bdot(jnp.concatenate([a, b], 0), kk, 1, 1), 0), kb, qs, k)
    lower = _each(lambda x, d: jnp.where(strict, x[0], 0.0) * d, kq, decay)
    intra = _each(lambda x, d: x[1] * d, kq, decay)
    tinv = _tri_inv_unit(lower) if tinv_known is None else _each(_known_inverse, lower, tinv_known)
    eg = _each(jnp.exp, gc)
    uw = _each(lambda t, vv, b, kb_, e: _halves(hdot(t, jnp.concatenate([vv * b, kb_ * e], 1)), 1),
               tinv, v, betab, kb, eg)
    gl = _each(lambda g: jnp.sum(jnp.where(r == c - 1, g, 0.0), 0, keepdims=True), gc)
    k_dec = _each(lambda kk, a, g: kk * jnp.exp(a - g), k, gl, gc)
    ws = _each(lambda x, t, e, st: _halves(bdot(jnp.concatenate([x[1], t * e], 0), st, 1, 0), 0), uw, qs, eg, state)
    v_new = _each(lambda x, y: x[0] - y[0], uw, ws)
    out = _each(lambda y, a, vn: y[1] + bdot(a, vn, 1, 0), ws, intra, v_new)
    new_state = _each(lambda st, a, kd, vn: st * jnp.exp(a) + bdot(kd, vn, 0, 0), state, gl, k_dec, v_new)
    return tuple(out), tuple(new_state), tuple(tinv)


def delta_fwd(tag, qkv, gb, betab):
    s = qkv.shape[0]
    c, hd = DN_CHUNK, DN_HEAD_DIM
    n = s // c

    hg, ng = DN_HEADS_PER_STEP, DN_HEADS // DN_HEADS_PER_STEP

    def body(q_ref, k_ref, v_ref, g_ref, b_ref, o_ref, st_ref, ti_ref, state):
        @pl.when(pl.program_id(1) == 0)
        def _():
            state[...] = jnp.zeros_like(state)

        heads = lambda ref: tuple(ref[:, j * hd:(j + 1) * hd] for j in range(hg))
        st = tuple(state[j] for j in range(hg))
        outs, news, tinv = _delta_chunk(heads(q_ref), heads(k_ref), heads(v_ref), heads(g_ref), heads(b_ref), st)
        for j in range(hg):
            st_ref[j] = st[j]
            ti_ref[j] = tinv[j]
            o_ref[:, j * hd:(j + 1) * hd] = outs[j]
            state[j] = news[j]

    blk = lambda off: pl.BlockSpec((c, hg * hd), lambda h, i, _o=off: (i, h + _o))
    per_chunk = pl.BlockSpec((hg, None, hd, hd), lambda h, i: (h, i, 0, 0))
    return pl.pallas_call(
        body, grid=(ng, n),
        in_specs=[blk(0), blk(ng), blk(2 * ng), blk(0), blk(0)],
        out_specs=[blk(0), per_chunk, per_chunk],
        out_shape=[jax.ShapeDtypeStruct((s, DN_KEY_DIM), f32)] + [jax.ShapeDtypeStruct((DN_HEADS, n, hd, hd), f32)] * 2,
        scratch_shapes=[pltpu.VMEM((hg, hd, hd), f32)],
        compiler_params=_cparams("parallel", "arbitrary"), name="delta_" + tag)(qkv, qkv, qkv, gb, betab)


def delta_bwd(tag, qkv, gb, betab, states, tinvs, do):
    s = qkv.shape[0]
    c, hd = DN_CHUNK, DN_HEAD_DIM
    n = s // c

    hg, ng = DN_HEADS_PER_STEP, DN_HEADS // DN_HEADS_PER_STEP

    def body(q_ref, k_ref, v_ref, g_ref, b_ref, st_ref, ti_ref, do_ref, dq_ref, dk_ref, dv_ref, dg_ref, db_ref, dstate):
        @pl.when(pl.program_id(1) == 0)
        def _():
            dstate[...] = jnp.zeros_like(dstate)

        heads = lambda ref: tuple(ref[:, j * hd:(j + 1) * hd] for j in range(hg))
        tinv = tuple(ti_ref[j] for j in range(hg))
        _, vjp = jax.vjp(lambda *args: _delta_chunk(*args, tinv_known=tinv)[:2],
                         heads(q_ref), heads(k_ref), heads(v_ref), heads(g_ref), heads(b_ref),
                         tuple(st_ref[j] for j in range(hg)))
        grads = vjp((heads(do_ref), tuple(dstate[j] for j in range(hg))))
        for ref, g in zip((dq_ref, dk_ref, dv_ref, dg_ref, db_ref), grads[:5]):
            for j in range(hg):
                ref[:, j * hd:(j + 1) * hd] = g[j]
        for j in range(hg):
            dstate[j] = grads[5][j]

    blk = lambda off: pl.BlockSpec((c, hg * hd), lambda h, i, _o=off: (n - 1 - i, h + _o))
    return pl.pallas_call(
        body, grid=(ng, n),
        in_specs=[blk(0), blk(ng), blk(2 * ng), blk(0), blk(0)]
        + [pl.BlockSpec((hg, None, hd, hd), lambda h, i: (h, n - 1 - i, 0, 0))] * 2 + [blk(0)],
        out_specs=[blk(0)] * 5,
        out_shape=[jax.ShapeDtypeStruct((s, DN_KEY_DIM), f32)] * 5,
        scratch_shapes=[pltpu.VMEM((hg, hd, hd), f32)],
        compiler_params=_cparams("parallel", "arbitrary"),
        name="delta_bwd_" + tag)(qkv, qkv, qkv, gb, betab, states, tinvs, do)


def _dn_out_tile(o, z, ng):
    outs = []
    for h in range(DN_HEADS):
        sl = slice(h * DN_HEAD_DIM, (h + 1) * DN_HEAD_DIM)
        oh = o[:, sl]
        nrm = oh * lax.rsqrt(jnp.mean(oh * oh, -1, keepdims=True) + RMS_EPS) * ng[:, sl]
        outs.append(nrm * jax.nn.silu(z[:, sl]))
    return (jnp.concatenate(outs, -1),)


def _head_selectors():
    r, c = _iota2((BA_PAD, DN_KEY_DIM), 0), _iota2((BA_PAD, DN_KEY_DIM), 1) // DN_HEAD_DIM
    return (r == c).astype(f32), (r == c + DN_HEADS).astype(f32)


def dn_mixer_fwd(tag, proj, cw, alog_b, dtb_b, ng_b):
    eb, ea = _head_selectors()
    ba = (proj, BA_PAD, COL_BA // BA_PAD)
    qkv = dn_conv_fwd(tag, proj, cw)
    betab, gb = rowmap("dn_gate_" + tag, _gate_tile, [ba], [eb, ea, alog_b, dtb_b], [DN_KEY_DIM] * 2, TM_ROW)
    o, states, tinvs = delta_fwd(tag, qkv, gb, betab)
    z = (proj, DN_KEY_DIM, COL_Z // DN_KEY_DIM)
    a_out = rowmap("dn_out_" + tag, _dn_out_tile, [o, z], [ng_b], [DN_KEY_DIM], TM_ROW)[0]
    return a_out, (qkv, betab, gb, o, states, tinvs)


def dn_mixer_bwd(tag, proj, cw, alog_b, dtb_b, ng_b, res, da_out):
    qkv, betab, gb, o, states, tinvs = res
    eb, ea = _head_selectors()
    ba = (proj, BA_PAD, COL_BA // BA_PAD)
    z = (proj, DN_KEY_DIM, COL_Z // DN_KEY_DIM)
    (do, dz), (dng,) = rowmap_bwd("dn_out_bwd_" + tag, _dn_out_tile, [o, z], [ng_b], [da_out], TM_ROW)
    dq, dk, dv, dgb, dbetab = delta_bwd(tag, qkv, gb, betab, states, tinvs, do)
    dqkv_raw, dcw = dn_conv_bwd(tag, proj, cw, jnp.concatenate([dq, dk, dv], 1))
    (dba,), (dalog, ddtb) = rowmap_bwd("dn_gate_bwd_" + tag, _gate_tile, [ba], [eb, ea, alog_b, dtb_b],
                                       [dbetab, dgb], TM_ROW, par_mask=[False, False, True, True])
    return dqkv_raw, dz, dba, dcw[:DN_CONV], dalog, ddtb, dng


def _swap_halves(x):
    n = x.shape[1]
    first = (_iota2((1, n), 1) % SW_HEAD_DIM) < SW_HEAD_DIM // 2
    return jnp.where(first, pltpu.roll(x, n - SW_HEAD_DIM // 2, 1), pltpu.roll(x, SW_HEAD_DIM // 2, 1))


def _rope_apply(x, cos, sin_signed):
    return x * cos + _swap_halves(x) * sin_signed


def _rope_transpose(dy, cos, sin_signed):
    return dy * cos + _swap_halves(dy * sin_signed)


def rope_tables(positions, s):
    half = SW_HEAD_DIM // 2
    inv_freq = ROPE_THETA ** (-jnp.arange(0, SW_HEAD_DIM, 2, dtype=f32) / SW_HEAD_DIM)
    ang = positions.reshape(s, 1).astype(f32) * inv_freq[None, :]
    cos, sin = jnp.cos(ang), jnp.sin(ang)
    cos_t = jnp.tile(jnp.concatenate([cos, cos], 1), (1, SW_HEADS))
    sin_t = jnp.tile(jnp.concatenate([-sin, sin], 1), (1, SW_HEADS))
    assert cos_t.shape == (s, SW_DIM) and half * 2 == SW_HEAD_DIM
    return cos_t, sin_t


def rope_fwd(tag, proj, cos, sin):
    def fn(q, k, v, c, sg):
        return _rope_apply(q, c, sg), _rope_apply(k, c, sg), v

    rows = [(proj, SW_DIM, COL_SWQ // SW_DIM), (proj, SW_DIM, COL_SWK // SW_DIM), (proj, SW_DIM, COL_SWV // SW_DIM), cos, sin]
    return rowmap("rope_" + tag, fn, rows, [], [SW_DIM] * 3, TM_ROW, out_dtypes=[bf16] * 3)


def _swa_block(q, kp, kc, vp, vc, first):
    blk = SW_BLOCK
    kk = jnp.concatenate([kp, kc], 0)
    vv = jnp.concatenate([vp, vc], 0)
    dist = (_iota2((blk, 2 * blk), 0) + blk) - _iota2((blk, 2 * blk), 1)
    kj = _iota2((blk, 2 * blk), 1)
    valid = (dist >= 0) & (dist <= blk) & ((kj >= blk) | jnp.logical_not(first))
    lane_head = _iota2((1, LANES), 1) // SW_HEAD_DIM
    outs, lses = [], []
    for p in range(SW_DIM // LANES):
        sl = slice(p * LANES, (p + 1) * LANES)
        qp, kp_, vp_ = q[:, sl], kk[:, sl], vv[:, sl]
        o_pair = jnp.zeros((blk, LANES), f32)
        l_pair = jnp.zeros((blk, LANES), f32)
        for e in range(LANES // SW_HEAD_DIM):
            msk = lane_head == e
            sc = bdot(jnp.where(msk, qp, 0.0), kp_, 1, 1) * (SW_HEAD_DIM ** -0.5)
            sc = jnp.where(valid, sc, -1e30)
            m = lax.stop_gradient(jnp.max(sc, -1, keepdims=True))
            pe = jnp.exp(sc - m)
            l = jnp.sum(pe, -1, keepdims=True)
            o = bdot(pe, vp_, 1, 0) / l
            o_pair = o_pair + jnp.where(msk, o, 0.0)
            l_pair = l_pair + jnp.where(msk, m + jnp.log(l), 0.0)
        outs.append(o_pair)
        lses.append(l_pair)
    return jnp.concatenate(outs, -1), jnp.concatenate(lses, -1)


def _swa_specs(r):
    cur = pl.BlockSpec((SW_BLOCK, SW_DIM), lambda rho, n: (n, rho))
    prev = pl.BlockSpec((SW_BLOCK, SW_DIM), lambda rho, n: (jnp.maximum(n - 1, 0), rho))
    return cur, prev


def swa_fwd(tag, r, q, k, v):
    s = q.shape[0]
    ln = s // r
    q2, k2, v2 = (t.reshape(ln, r * SW_DIM) for t in (q, k, v))
    cur, prev = _swa_specs(r)

    def body(q_ref, kp_ref, kc_ref, vp_ref, vc_ref, o_ref, l_ref):
        ins = [r[...].astype(f32) for r in (q_ref, kp_ref, kc_ref, vp_ref, vc_ref)]
        o, l = _swa_block(*ins, pl.program_id(1) == 0)
        o_ref[...] = o
        l_ref[...] = l

    o, l = pl.pallas_call(
        body, grid=(r, ln // SW_BLOCK),
        in_specs=[cur, prev, cur, prev, cur], out_specs=[cur, cur],
        out_shape=[jax.ShapeDtypeStruct((ln, r * SW_DIM), f32)] * 2,
        compiler_params=_cparams("parallel", "parallel"), name=f"swa{r}_{tag}")(q2, k2, k2, v2, v2)
    return o.reshape(s, SW_DIM), l.reshape(s, SW_DIM)


def swa_bwd(tag, r, q, k, v, do, dl):
    s = q.shape[0]
    ln = s // r
    q2, k2, v2, do2, dl2 = (t.reshape(ln, r * SW_DIM) for t in (q, k, v, do, dl))
    cur, prev = _swa_specs(r)

    def body(q_ref, kp_ref, kc_ref, vp_ref, vc_ref, do_ref, dl_ref, dq_ref, dka_ref, dkb_ref, dva_ref, dvb_ref):
        first = pl.program_id(1) == 0
        ins = [r[...].astype(f32) for r in (q_ref, kp_ref, kc_ref, vp_ref, vc_ref)]
        _, vjp = jax.vjp(lambda *a: _swa_block(*a, first), *ins)
        dq_ref[...], dka_ref[...], dkb_ref[...], dva_ref[...], dvb_ref[...] = vjp((do_ref[...], dl_ref[...]))

    outs = pl.pallas_call(
        body, grid=(r, ln // SW_BLOCK),
        in_specs=[cur, prev, cur, prev, cur, cur, cur], out_specs=[cur] * 5,
        out_shape=[jax.ShapeDtypeStruct((ln, r * SW_DIM), f32)] * 5,
        compiler_params=_cparams("parallel", "parallel"), name=f"swa{r}_bwd_{tag}")(q2, k2, k2, v2, v2, do2, dl2)
    return [t.reshape(s, SW_DIM) for t in outs]


def _combine_tile(o1, l1, o2, l2, o3, l3):
    m = lax.stop_gradient(jnp.maximum(jnp.maximum(l1, l2), l3))
    e1, e2, e3 = jnp.exp(l1 - m), jnp.exp(l2 - m), jnp.exp(l3 - m)
    return ((o1 * e1 + o2 * e2 + o3 * e3) / (e1 + e2 + e3),)


def swa_merge_bwd(tag, grads, cos, sin):
    s = cos.shape[0]
    tm = SW_BLOCK
    nt = s // tm
    here = pl.BlockSpec((tm, SW_DIM), lambda i: (i, 0))
    arrs, specs = [], []
    for r, g in zip(SW_DILATIONS, grads):
        ahead = pl.BlockSpec((tm, SW_DIM), lambda i, _r=r: (jnp.minimum(i + _r, nt - 1), 0))
        arrs += g
        specs += [here, ahead, here, ahead, here]

    def body(*refs):
        i = pl.program_id(0)
        c_ref, s_ref = refs[15], refs[16]
        dq_ref, dk_ref, dv_ref = refs[17:]
        dq = jnp.zeros((tm, SW_DIM), f32)
        dk = jnp.zeros((tm, SW_DIM), f32)
        dv = jnp.zeros((tm, SW_DIM), f32)
        for b, r in enumerate(SW_DILATIONS):
            gq, gka, gkb, gva, gvb = refs[5 * b:5 * b + 5]
            inside = i + r < nt
            dq = dq + gq[...]
            dk = dk + gkb[...] + jnp.where(inside, gka[...], 0.0)
            dv = dv + gvb[...] + jnp.where(inside, gva[...], 0.0)
        dq_ref[...] = _rope_transpose(dq, c_ref[...], s_ref[...])
        dk_ref[...] = _rope_transpose(dk, c_ref[...], s_ref[...])
        dv_ref[...] = dv

    return pl.pallas_call(
        body, grid=(nt,), in_specs=specs + [here, here], out_specs=[here] * 3,
        out_shape=[jax.ShapeDtypeStruct((s, SW_DIM), f32)] * 3,
        compiler_params=_cparams("parallel"), name="swa_merge_bwd_" + tag)(*arrs, cos, sin)


def swa_mixer_fwd(tag, proj, cos, sin):
    q, k, v = rope_fwd(tag, proj, cos, sin)
    ols = []
    for r in SW_DILATIONS:
        ols += list(swa_fwd(tag, r, q, k, v))
    b_out = rowmap("swa_comb_" + tag, _combine_tile, ols, [], [SW_DIM], TM_ROW)[0]
    return b_out, (q, k, v, ols)


def swa_mixer_bwd(tag, cos, sin, res, db_out):
    q, k, v, ols = res
    dols, _ = rowmap_bwd("swa_comb_bwd_" + tag, _combine_tile, ols, [], [db_out], TM_ROW)
    grads = [swa_bwd(tag, r, q, k, v, dols[2 * b], dols[2 * b + 1]) for b, r in enumerate(SW_DILATIONS)]
    return swa_merge_bwd(tag, grads, cos, sin)


TM_S5 = 256
S5_GPB = LANES // S5_GROUP
S5_NBLK = D_MODEL // LANES
S5_HALF = S5_GPB * S5_STATE
S5_BW = 2 * S5_HALF
S5_WIDTH = S5_NBLK * S5_BW
S5_TABW = S5_NBLK * S5_HALF


def _s5_disc_tile(a_re, a_im, log_dt, b_re, b_im, expand):
    dt = jnp.exp(log_dt)
    mag = jnp.exp(a_re * dt)
    abar_re, abar_im = mag * jnp.cos(a_im * dt), mag * jnp.sin(a_im * dt)
    n_re, n_im = abar_re - 1.0, abar_im
    den = a_re * a_re + a_im * a_im
    c_re = (n_re * a_re + n_im * a_im) / den
    c_im = (n_im * a_re - n_re * a_im) / den
    cx_re, cx_im = hdot(c_re, expand), hdot(c_im, expand)
    return abar_re, abar_im, cx_re * b_re - cx_im * b_im, cx_re * b_im + cx_im * b_re


def _s5_expand():
    return (_iota2((S5_STATE, S5_STATE * S5_GROUP), 1) // S5_GROUP == _iota2((S5_STATE, S5_STATE * S5_GROUP), 0)).astype(f32)


def s5_tables(a_re, a_im, log_dt):
    lanes = lambda v: v.reshape(1, S5_TABW)
    dt = jnp.broadcast_to(log_dt.reshape(S5_GROUPS, 1), (S5_GROUPS, S5_STATE))
    t = TM_S5

    def body(are_ref, aim_ref, ldt_ref, ar_ref, ai_ref, arr_ref, air_ref):
        dtv = jnp.exp(ldt_ref[...])
        lre, lim = are_ref[...] * dtv, aim_ref[...] * dtv
        row = _iota2((t, S5_HALF), 0)
        for asc, o_re, o_im in ((True, ar_ref, ai_ref), (False, arr_ref, air_ref)):
            n = (row + 1 if asc else t - row).astype(f32)
            mag = jnp.exp(n * lre)
            o_re[...] = mag * jnp.cos(n * lim)
            o_im[...] = mag * jnp.sin(n * lim)

    lane = pl.BlockSpec((1, S5_HALF), lambda j: (0, j))
    tab = pl.BlockSpec((t, S5_HALF), lambda j: (0, j))
    return pl.pallas_call(
        body, grid=(S5_NBLK,), in_specs=[lane] * 3, out_specs=[tab] * 4,
        out_shape=[jax.ShapeDtypeStruct((t, S5_TABW), f32)] * 4,
        compiler_params=_cparams("parallel"), name="s5_tables")(lanes(a_re), lanes(a_im), lanes(dt))


def s5_pack_weights(bbar_re, bbar_im, c_re, c_im):
    eye = jnp.eye(S5_GPB, dtype=f32)
    bb = jnp.stack([bbar_re.reshape(S5_GROUPS, S5_STATE, S5_GROUP), bbar_im.reshape(S5_GROUPS, S5_STATE, S5_GROUP)], 1)
    bb = bb.transpose(0, 3, 1, 2).reshape(S5_NBLK, S5_GPB, S5_GROUP, 2, S5_STATE)
    wb = (bb[:, :, :, :, None, :] * eye[None, :, None, None, :, None]).reshape(S5_NBLK, LANES, S5_BW)
    cc = jnp.stack([c_re, -c_im], 1)
    cc = cc.reshape(S5_NBLK, S5_GPB, 2, S5_GROUP, S5_STATE).transpose(0, 2, 1, 4, 3)
    wc = (cc[:, :, :, :, None, :] * eye[None, None, :, None, :, None]).reshape(S5_NBLK, S5_BW, LANES)
    return wb, wc


def s5_unpack_weight_grads(dwb, dwc):
    d6 = dwb.reshape(S5_NBLK, S5_GPB, S5_GROUP, 2, S5_GPB, S5_STATE)
    dbb = jnp.stack([d6[:, gl, :, :, gl, :] for gl in range(S5_GPB)])
    dbb = dbb.transpose(1, 0, 3, 4, 2).reshape(S5_GROUPS, 2, S5_STATE * S5_GROUP)
    c6 = dwc.reshape(S5_NBLK, 2, S5_GPB, S5_STATE, S5_GPB, S5_GROUP)
    dcc = jnp.stack([c6[:, :, gl, :, gl, :] for gl in range(S5_GPB)])
    dcc = dcc.transpose(1, 0, 2, 4, 3).reshape(S5_GROUPS, 2, S5_GROUP, S5_STATE)
    return dbb[:, 0], dbb[:, 1], dcc[:, 0], -dcc[:, 1]


def _s5_step_rows(t):
    d, out = 1, []
    while d < t:
        out.append(d)
        d *= 2
    return out


def s5_core_fwd(tag, u, wb, wc, a1, a2, dskip):
    s = u.shape[0]
    t = TM_S5

    def body(u_ref, wb_ref, wc_ref, ar_ref, ai_ref, d_ref, y_ref, x_ref, carry):
        @pl.when(pl.program_id(1) == 0)
        def _():
            carry[...] = jnp.zeros_like(carry)

        uv = u_ref[...]
        bu = bdot(uv, wb_ref[...], 1, 0)
        row = _iota2((t, LANES), 0)
        for c in range(S5_HALF // LANES):
            re, im = slice(c * LANES, (c + 1) * LANES), slice(S5_HALF + c * LANES, S5_HALF + (c + 1) * LANES)
            xr, xi = bu[:, re], bu[:, im]
            for d in _s5_step_rows(t):
                ar, ai = ar_ref[d - 1:d, re], ai_ref[d - 1:d, re]
                if d % SUBLANES:
                    keep = row >= d
                    sr = jnp.where(keep, pltpu.roll(xr, d, 0), 0.0)
                    si = jnp.where(keep, pltpu.roll(xi, d, 0), 0.0)
                    xr, xi = xr + ar * sr - ai * si, xi + ar * si + ai * sr
                else:
                    sr, si = xr[:t - d], xi[:t - d]
                    xr = jnp.concatenate([xr[:d], xr[d:] + (ar * sr - ai * si)], 0)
                    xi = jnp.concatenate([xi[:d], xi[d:] + (ar * si + ai * sr)], 0)
            cr, ci = carry[:, re], carry[:, im]
            ar, ai = ar_ref[:, re], ai_ref[:, re]
            x_ref[:, re] = xr + ar * cr - ai * ci
            x_ref[:, im] = xi + ar * ci + ai * cr
        carry[...] = x_ref[t - 1:t, :]
        y_ref[...] = bdot(x_ref[...], wc_ref[...], 1, 0) + d_ref[...] * uv

    tab = pl.BlockSpec((t, S5_HALF), lambda j, i: (0, j))
    return pl.pallas_call(
        body, grid=(S5_NBLK, s // t),
        in_specs=[pl.BlockSpec((t, LANES), lambda j, i: (i, j)),
                  pl.BlockSpec((None, LANES, S5_BW), lambda j, i: (j, 0, 0)),
                  pl.BlockSpec((None, S5_BW, LANES), lambda j, i: (j, 0, 0)),
                  tab, tab, pl.BlockSpec((1, LANES), lambda j, i: (0, j))],
        out_specs=[pl.BlockSpec((t, LANES), lambda j, i: (i, j)), pl.BlockSpec((t, S5_BW), lambda j, i: (i, j))],
        out_shape=[jax.ShapeDtypeStruct((s, D_MODEL), f32), jax.ShapeDtypeStruct((s, S5_WIDTH), f32)],
        scratch_shapes=[pltpu.VMEM((1, S5_BW), f32)],
        compiler_params=_cparams("parallel", "arbitrary"), name="s5_core_" + tag)(u, wb, wc, a1, a2, dskip)


def s5_core_bwd(tag, u, x, wb, wc, a1, a2, a1r, a2r, dskip, dy):
    s = u.shape[0]
    t = TM_S5
    nt = s // t
    hb = t // SUBLANES

    def body(u_ref, dy_ref, x_ref, xh_ref, wb_ref, wc_ref, ar_ref, ai_ref, arr_ref, air_ref, d_ref,
             du_ref, dwb_ref, dwc_ref, dd_ref, q1_ref, q2_ref, carry, lam_scr):
        i = pl.program_id(1)
        tt = nt - 1 - i

        @pl.when(i == 0)
        def _():
            carry[...] = jnp.zeros_like(carry)

        uv, dyv, xv = u_ref[...], dy_ref[...], x_ref[...]
        lam = bdot(dyv, wc_ref[...], 1, 1)
        row = _iota2((t, LANES), 0)
        x_last = jnp.where(tt > 0, xh_ref[SUBLANES - 1:SUBLANES, :], 0.0)
        q1s, q2s = [], []
        for c in range(S5_HALF // LANES):
            re, im = slice(c * LANES, (c + 1) * LANES), slice(S5_HALF + c * LANES, S5_HALF + (c + 1) * LANES)
            lr, li = lam[:, re], lam[:, im]
            for d in _s5_step_rows(t):
                ar, ai = ar_ref[d - 1:d, re], ai_ref[d - 1:d, re]
                if d % SUBLANES:
                    keep = row < t - d
                    sr = jnp.where(keep, pltpu.roll(lr, t - d, 0), 0.0)
                    si = jnp.where(keep, pltpu.roll(li, t - d, 0), 0.0)
                    lr, li = lr + ar * sr + ai * si, li + ar * si - ai * sr
                else:
                    sr, si = lr[d:], li[d:]
                    lr = jnp.concatenate([lr[:t - d] + (ar * sr + ai * si), lr[t - d:]], 0)
                    li = jnp.concatenate([li[:t - d] + (ar * si - ai * sr), li[t - d:]], 0)
            cr, ci = carry[:, re], carry[:, im]
            ar, ai = arr_ref[:, re], air_ref[:, re]
            lr, li = lr + ar * cr + ai * ci, li + ar * ci - ai * cr
            lam_scr[:, re] = lr
            lam_scr[:, im] = li
            pr = jnp.where(row == 0, x_last[:, re], pltpu.roll(xv[:, re], 1, 0))
            pi = jnp.where(row == 0, x_last[:, im], pltpu.roll(xv[:, im], 1, 0))
            p1, p2 = lr * pr + li * pi, li * pr - lr * pi
            q1, q2 = p1[:SUBLANES, :], p2[:SUBLANES, :]
            for k in range(1, hb):
                q1 = q1 + p1[k * SUBLANES:(k + 1) * SUBLANES, :]
                q2 = q2 + p2[k * SUBLANES:(k + 1) * SUBLANES, :]
            q1s.append(q1)
            q2s.append(q2)
        carry[...] = lam_scr[0:1, :]
        lam = lam_scr[...]
        du_ref[...] = bdot(lam, wb_ref[...], 1, 1) + d_ref[...] * dyv
        upd = [(dwb_ref, bdot(uv, lam, 0, 0)), (dwc_ref, bdot(xv, dyv, 0, 0)),
               (dd_ref, jnp.sum(dyv * uv, 0, keepdims=True)),
               (q1_ref, jnp.concatenate(q1s, 1)), (q2_ref, jnp.concatenate(q2s, 1))]

        @pl.when(i == 0)
        def _():
            for ref, val in upd:
                ref[...] = val

        @pl.when(i != 0)
        def _():
            for ref, val in upd:
                ref[...] += val

    nb8 = s // SUBLANES
    rev = lambda w: pl.BlockSpec((t, w), lambda j, i: (nt - 1 - i, j))
    tab = pl.BlockSpec((t, S5_HALF), lambda j, i: (0, j))
    return pl.pallas_call(
        body, grid=(S5_NBLK, nt),
        in_specs=[rev(LANES), rev(LANES), rev(S5_BW),
                  pl.BlockSpec((SUBLANES, S5_BW), lambda j, i: (jnp.maximum((nt - 1 - i) * hb - 1, 0), j)),
                  pl.BlockSpec((None, LANES, S5_BW), lambda j, i: (j, 0, 0)),
                  pl.BlockSpec((None, S5_BW, LANES), lambda j, i: (j, 0, 0)),
                  tab, tab, tab, tab, pl.BlockSpec((1, LANES), lambda j, i: (0, j))],
        out_specs=[rev(LANES),
                   pl.BlockSpec((None, LANES, S5_BW), lambda j, i: (j, 0, 0)),
                   pl.BlockSpec((None, S5_BW, LANES), lambda j, i: (j, 0, 0)),
                   pl.BlockSpec((1, LANES), lambda j, i: (0, j)),
                   pl.BlockSpec((SUBLANES, S5_HALF), lambda j, i: (0, j)),
                   pl.BlockSpec((SUBLANES, S5_HALF), lambda j, i: (0, j))],
        out_shape=[jax.ShapeDtypeStruct((s, D_MODEL), f32),
                   jax.ShapeDtypeStruct((S5_NBLK, LANES, S5_BW), f32),
                   jax.ShapeDtypeStruct((S5_NBLK, S5_BW, LANES), f32),
                   jax.ShapeDtypeStruct((1, D_MODEL), f32),
                   jax.ShapeDtypeStruct((SUBLANES, S5_TABW), f32),
                   jax.ShapeDtypeStruct((SUBLANES, S5_TABW), f32)],
        scratch_shapes=[pltpu.VMEM((1, S5_BW), f32), pltpu.VMEM((t, S5_BW), f32)],
        compiler_params=_cparams("parallel", "arbitrary"),
        name="s5_core_bwd_" + tag)(u, dy, x, x, wb, wc, a1, a2, a1r, a2r, dskip)


def _gelu_tile(y):
    return (jax.nn.gelu(y),)


def s5_mixer_fwd(tag, u, prm, w_og):
    a_re, a_im, log_dt, b_re, b_im, c_re, c_im, dskip = prm
    disc_in = [a_re, a_im, log_dt.reshape(S5_GROUPS, 1), b_re.reshape(S5_GROUPS, -1), b_im.reshape(S5_GROUPS, -1)]
    abar_re, abar_im, bbar_re, bbar_im = rowmap("s5_disc_" + tag, _s5_disc_tile, disc_in, [_s5_expand()],
                                                [S5_STATE, S5_STATE, S5_STATE * S5_GROUP, S5_STATE * S5_GROUP], S5_GROUPS)
    del abar_re, abar_im
    a1, a2, a1r, a2r = s5_tables(a_re, a_im, log_dt)
    wb, wc = s5_pack_weights(bbar_re, bbar_im, c_re, c_im)
    wb, wc = wb.astype(bf16), wc.astype(bf16)
    y, x = s5_core_fwd(tag, u, wb, wc, a1, a2, dskip.reshape(1, D_MODEL))
    hid = rowmap("s5_gelu_" + tag, _gelu_tile, [y], [], [D_MODEL], TM_ROW, out_dtypes=[bf16])[0]
    og = mm_nn("s5_og_" + tag, hid, w_og)
    mix = rowmap("s5_glu_" + tag, _glu_tile, [og], [], [D_MODEL], TM_ROW)[0]
    return mix, (disc_in, a1, a2, a1r, a2r, wb, wc, x, y, hid, og)


def s5_mixer_bwd(tag, idx, u, prm, w_og, res, dmix, stacks):
    a_re, a_im, log_dt, b_re, b_im, c_re, c_im, dskip = prm
    disc_in, a1, a2, a1r, a2r, wb, wc, x, y, hid, og = res
    (dog,), _ = rowmap_bwd("s5_glu_bwd_" + tag, _glu_tile, [og], [], [dmix], TM_ROW)
    n_odd = DEPTH // 2
    dw_og = (mm_tn("s5_wo_dw_" + tag, hid, dog, n_cols=D_MODEL, stack=(stacks[0], idx, n_odd)),
             mm_tn("s5_wg_dw_" + tag, hid, dog, b_col0=D_MODEL, n_cols=D_MODEL, stack=(stacks[1], idx, n_odd)))
    dhid = mm_nt("s5_og_dx_" + tag, dog, w_og)
    (dy,), _ = rowmap_bwd("s5_gelu_bwd_" + tag, _gelu_tile, [y], [], [dhid], TM_ROW)
    du, dwb, dwc, ddskip, q1, q2 = s5_core_bwd(tag, u, x, wb, wc, a1, a2, a1r, a2r, dskip.reshape(1, D_MODEL), dy)
    dbbar_re, dbbar_im, dc_re, dc_im = s5_unpack_weight_grads(dwb, dwc)
    dabar_re = q1.sum(0).reshape(S5_GROUPS, S5_STATE)
    dabar_im = q2.sum(0).reshape(S5_GROUPS, S5_STATE)
    grads, _ = rowmap_bwd("s5_disc_bwd_" + tag, _s5_disc_tile, disc_in, [_s5_expand()],
                          [dabar_re, dabar_im, dbbar_re, dbbar_im], S5_GROUPS, par_mask=[False])
    da_re, da_im, dlog_dt, db_re, db_im = grads
    return du, (da_re, da_im, dlog_dt.reshape(S5_GROUPS), db_re.reshape(b_re.shape), db_im.reshape(b_im.shape),
                dc_re, dc_im, ddskip.reshape(D_MODEL)), dw_og


HYB_IN = 3592
_IN_B0, _IN_SW0 = 2048, 2056


IN_SHARD = HYB_IN // 4
SHARD_ORDER_GRADS = ("hyb_w_in", "ffn_wg", "ffn_wu", "ffn_wd")
FFN_TRANSPOSED = ("ffn_wg", "ffn_wu")
BIG_SHARDED = ("hyb_w_in", "hyb_w_out", "s5_glu_wo", "s5_glu_wg", "xq_w", "xk_w", "xv_w", "xo_w", "ffn_wg", "ffn_wu", "ffn_wd")


def _w_in_pieces():
    runs = [(0, _IN_B0, 0), (_IN_B0, _IN_SW0, COL_BA), (_IN_SW0, HYB_IN, _IN_B0)]
    out = []
    for sh in range(4):
        lo, hi = sh * IN_SHARD, (sh + 1) * IN_SHARD
        for r_lo, r_hi, c_lo in runs:
            a, b = max(lo, r_lo), min(hi, r_hi)
            if a < b:
                out.append((sh, a - lo, b - lo, c_lo + a - r_lo))
    return out


def w_in_to_canonical(tag, layer, w4):
    tr = 128

    def body(w_ref, o_ref):
        o_ref[:, COL_BA:] = jnp.zeros((tr, BA_PAD), o_ref.dtype)
        for sh, a, b, c in _w_in_pieces():
            o_ref[:, c:c + b - a] = w_ref[sh, :, a:b]

    return pl.pallas_call(
        body, grid=(D_MODEL // tr,),
        in_specs=[pl.BlockSpec((4, None, tr, IN_SHARD), lambda i: (0, layer, i, 0))],
        out_specs=pl.BlockSpec((tr, PROJ_COLS), lambda i: (i, 0)),
        out_shape=jax.ShapeDtypeStruct((D_MODEL, PROJ_COLS), w4.dtype),
        compiler_params=_cparams("parallel"), name="w_in_canon_" + tag)(w4)


def w_in_grad_to_shards(tag, layer, g, stack, n_layers):
    tr = 128
    extra, extra_specs, _ = _stacked(stack)

    def body(g_ref, *rest):
        o_ref = rest[-1]
        for sh, a, b, c in _w_in_pieces():
            o_ref[sh, :, a:b] = g_ref[:, c:c + b - a]

    return pl.pallas_call(
        body, grid=(D_MODEL // tr,),
        in_specs=[pl.BlockSpec((tr, PROJ_COLS), lambda i: (i, 0))] + extra_specs,
        out_specs=pl.BlockSpec((4, None, tr, IN_SHARD), lambda i: (0, layer, i, 0)),
        out_shape=jax.ShapeDtypeStruct((4, n_layers, D_MODEL, IN_SHARD), f32),
        input_output_aliases={1: 0} if extra else {},
        compiler_params=_cparams("parallel"), name="w_in_grad_shards_" + tag)(g, *extra)


def _add2(name, a, b):
    return rowmap(name, lambda p, q: (p + q,), [a, b], [], [a.shape[1]], _pick(a.shape[0], (256, 128, 64, 32, 16, 8)))[0]


def local_step(x, mem, positions, target, p):
    s = x.shape[0]
    cos, sin = rope_tables(positions, s)
    row = lambda v: v.reshape(1, -1).astype(f32)
    wg4, wu4, wd4 = (p[n].astype(bf16) for n in ("ffn_wg", "ffn_wu", "ffn_wd"))
    h = h16 = x
    tape = []
    for l in range(DEPTH):
        i, tag = l // 2, str(l)
        t = {"h0": h, "h0_16": h16}
        if l % 2 == 0:
            t["w_in"] = w_in_to_canonical(tag, i, p["hyb_w_in"].astype(bf16))
            t["w_out"] = p["hyb_w_out"][i].astype(bf16)
            t["dn_prm"] = (p["dn_conv_w"][i].astype(f32), row(jnp.repeat(p["dn_a_log"][i], DN_HEAD_DIM)),
                           row(jnp.repeat(p["dn_dt_bias"][i], DN_HEAD_DIM)), row(jnp.tile(p["dn_norm_g"][i], DN_HEADS)))
            t["proj"] = mm_nn("hyb_in_" + tag, h16, t["w_in"])
            a_out, t["dn"] = dn_mixer_fwd(tag, t["proj"], *t["dn_prm"])
            b_out, t["swa"] = swa_mixer_fwd(tag, t["proj"], cos, sin)
            t["mixed"] = jnp.concatenate([a_out, b_out], 1)
            mix = mm_nn("hyb_out_" + tag, t["mixed"], t["w_out"])
        else:
            t["s5_prm"] = tuple(p[n][i].astype(f32) for n in
                                ("s5_a_re", "s5_a_im", "s5_log_dt", "s5_b_re", "s5_b_im", "s5_c_re", "s5_c_im", "s5_d"))
            t["w_og"] = jnp.concatenate([p["s5_glu_wo"][i], p["s5_glu_wg"][i]], 1).astype(bf16)
            mix, t["s5"] = s5_mixer_fwd(tag, h, t["s5_prm"], t["w_og"])
        t["mix"] = mix
        t["ln"] = [(row(p[g][l]), row(p[b][l])) for g, b in
                   (("ln_mix_g", "ln_mix_b"), ("ln_x_g", "ln_x_b"), ("ln_ffn_g", "ln_ffn_b"))]
        t["h1"], t["h1_16"] = postnorm_fwd("mix" + tag, h, mix, *t["ln"][0])
        t["wq"], t["wo"] = p["xq_w"][l].astype(bf16), p["xo_w"][l].astype(bf16)
        t["wkv"] = jnp.concatenate([p["xk_w"][l], p["xv_w"][l]], 1).astype(bf16)
        t["xo"], t["xres"] = xattn_fwd(tag, t["h1_16"], mem, t["wq"], t["wkv"], t["wo"])
        t["h2"], t["h2_16"] = postnorm_fwd("x" + tag, t["h1"], t["xo"], *t["ln"][1])
        t["fo"], t["fres"] = ffn_fwd(tag, l, t["h2_16"], wg4, wu4, wd4)
        h, h16 = postnorm_fwd("ffn" + tag, t["h2"], t["fo"], *t["ln"][2])
        tape.append(t)

    part, dh = loss_head(h, target)
    loss = jnp.sum(part)

    g = {n: [None] * v.shape[0] for n, v in p.items() if n not in BIG_SHARDED}
    st = {n: None for n in BIG_SHARDED}
    for l in reversed(range(DEPTH)):
        i, tag, t = l // 2, str(l), tape[l]
        dh2a, dfo, dg, db = postnorm_bwd("ffn" + tag, t["h2"], t["fo"], *t["ln"][2], dh)
        g["ln_ffn_g"][l], g["ln_ffn_b"][l] = dg[0], db[0]
        ffn_names = ("ffn_wg", "ffn_wu", "ffn_wd")
        prev = None if st["ffn_wd"] is None else [st[n] for n in ffn_names]
        dh2b, new = ffn_bwd(tag, l, t["h2_16"], wg4, wu4, wd4, t["fres"], dfo, prev)
        st.update(zip(ffn_names, new))
        dh1a, dxo, dg, db = postnorm_bwd("x" + tag, t["h1"], t["xo"], *t["ln"][1], [dh2a, dh2b])
        g["ln_x_g"][l], g["ln_x_b"][l] = dg[0], db[0]
        x_names = ("xq_w", "xk_w", "xv_w", "xo_w")
        dh1b, new = xattn_bwd(tag, l, t["h1_16"], mem, t["wq"], t["wkv"], t["wo"], t["xres"], dxo, [st[n] for n in x_names])
        st.update(zip(x_names, new))
        dh0a, dmix, dg, db = postnorm_bwd("mix" + tag, t["h0"], t["mix"], *t["ln"][0], [dh1a, dh1b])
        g["ln_mix_g"][l], g["ln_mix_b"][l] = dg[0], db[0]
        if l % 2 == 0:
            st["hyb_w_out"] = mm_tn("hyb_out_dw_" + tag, t["mixed"], dmix, stack=(st["hyb_w_out"], i, DEPTH // 2))
            dmixed = mm_nt("hyb_out_dx_" + tag, dmix, t["w_out"])
            dqkv, dz, dba, dcw, dalog, ddtb, dng = dn_mixer_bwd(tag, t["proj"], *t["dn_prm"], t["dn"], (dmixed, DN_KEY_DIM, 0))
            g["dn_conv_w"][i] = dcw
            g["dn_a_log"][i] = dalog.reshape(DN_HEADS, DN_HEAD_DIM).sum(1)
            g["dn_dt_bias"][i] = ddtb.reshape(DN_HEADS, DN_HEAD_DIM).sum(1)
            g["dn_norm_g"][i] = dng.reshape(DN_HEADS, DN_HEAD_DIM).sum(0)
            dq, dk, dv = swa_mixer_bwd(tag, cos, sin, t["swa"], (dmixed, SW_DIM, 1))
            dproj = jnp.concatenate([dqkv, dz, dq, dk, dv, dba], 1)
            st["hyb_w_in"] = w_in_grad_to_shards(tag, i, mm_tn("hyb_in_dw_" + tag, t["h0_16"], dproj), st["hyb_w_in"], DEPTH // 2)
            dh0b = mm_nt("hyb_in_dx_" + tag, dproj, t["w_in"])
        else:
            dh0b, dprm, (st["s5_glu_wo"], st["s5_glu_wg"]) = s5_mixer_bwd(
                tag, i, t["h0"], t["s5_prm"], t["w_og"], t["s5"], dmix, (st["s5_glu_wo"], st["s5_glu_wg"]))
            for n, v in zip(("s5_a_re", "s5_a_im", "s5_log_dt", "s5_b_re", "s5_b_im", "s5_c_re", "s5_c_im", "s5_d"), dprm):
                g[n][i] = v
        dh = [dh0a, dh0b]
    grad_x = _add2("grad_x", dh[0], dh[1])
    grads = {n: jnp.stack(v) for n, v in g.items()}
    grads.update(st)
    return loss, grad_x, grads


WEIGHT_NAMES = ("hyb_w_in", "dn_conv_w", "dn_a_log", "dn_dt_bias", "dn_norm_g", "hyb_w_out", "s5_a_re", "s5_a_im",
                "s5_log_dt", "s5_b_re", "s5_b_im", "s5_c_re", "s5_c_im", "s5_d", "s5_glu_wo", "s5_glu_wg",
                "ln_mix_g", "ln_mix_b", "xq_w", "xk_w", "xv_w", "xo_w", "ln_x_g", "ln_x_b",
                "ffn_wg", "ffn_wu", "ffn_wd", "ln_ffn_g", "ln_ffn_b")
SHARD_AXIS = {"hyb_w_in": 2, "dn_conv_w": 2, "hyb_w_out": 1, "s5_d": 1, "s5_glu_wo": 1, "s5_glu_wg": 1,
              "xq_w": 1, "xk_w": 1, "xv_w": 1, "xo_w": 1, "ffn_wg": 2, "ffn_wu": 2, "ffn_wd": 1}
GATHER_F32 = ("dn_conv_w", "s5_d")
N_CHIPS = 4
PACK_COLS = 1024
_ANY = pl.BlockSpec(memory_space=pl.ANY)


def _pos():
    return lax.axis_index("x"), lax.axis_index("y"), lax.axis_index("c")


def _chip_peers(mx, my):
    return [(1 - mx, my), (mx, 1 - my), (1 - mx, 1 - my)]


def _rcopy(src, dst, ssem, rsem, dev):
    return pltpu.make_async_remote_copy(src_ref=src, dst_ref=dst, send_sem=ssem, recv_sem=rsem,
                                        device_id=dev, device_id_type=pl.DeviceIdType.MESH)


def comm_allgather4(name, x):
    def body(x_ref, o_ref, ssem, rsem, lsem):
        mx, my, mc = _pos()
        me = 2 * mx + my
        peers = _chip_peers(mx, my)
        loc = pltpu.make_async_copy(x_ref, o_ref.at[me], lsem)
        loc.start()
        sends = [_rcopy(x_ref, o_ref.at[me], ssem.at[k], rsem.at[k], (px, py, mc)) for k, (px, py) in enumerate(peers)]
        for cp in sends:
            cp.start()
        for k, (px, py) in enumerate(peers):
            _rcopy(x_ref, o_ref.at[2 * px + py], ssem.at[k], rsem.at[k], (px, py, mc)).wait_recv()
        for cp in sends:
            cp.wait_send()
        loc.wait()

    return pl.pallas_call(
        body, out_shape=jax.ShapeDtypeStruct((N_CHIPS,) + x.shape, x.dtype), in_specs=[_ANY], out_specs=_ANY,
        scratch_shapes=[pltpu.SemaphoreType.DMA((3,)), pltpu.SemaphoreType.DMA((3,)), pltpu.SemaphoreType.DMA],
        name=name)(x)


def _multi_call(name, body, ins, out_shapes, sems, in_place=False):
    return pl.pallas_call(
        body, out_shape=out_shapes, in_specs=[_ANY] * len(ins), out_specs=[_ANY] * len(out_shapes),
        scratch_shapes=sems, input_output_aliases={w: w for w in range(len(ins))} if in_place else {},
        name=name)(*ins)


def comm_gather_weights(name, slots):
    n = len(slots)

    def body(*refs):
        os_ = refs[n:2 * n]
        ssem, rsem, fssem, frsem = refs[2 * n:]
        mx, my, mc = _pos()
        me = 2 * mx + my
        peers = _chip_peers(mx, my)
        sib = (mx, my, 1 - mc)
        half = [o.shape[1] // 2 for o in os_]
        mine = [pl.ds(mc * h, h) for h in half]
        other = [pl.ds((1 - mc) * h, h) for h in half]
        sends = [_rcopy(os_[w].at[me, mine[w]], os_[w].at[me, mine[w]], ssem.at[w, k], rsem.at[w, k], (px, py, mc))
                 for w in range(n) for k, (px, py) in enumerate(peers)]
        for cp in sends:
            cp.start()
        fwds = []
        for w in range(n):
            for k, (px, py) in enumerate(peers):
                landed = os_[w].at[2 * px + py, mine[w]]
                _rcopy(landed, landed, ssem.at[w, k], rsem.at[w, k], (px, py, mc)).wait_recv()
                fw = _rcopy(landed, landed, fssem.at[w, k], frsem.at[w, k], sib)
                fw.start()
                fwds.append(fw)
        for w in range(n):
            for k, (px, py) in enumerate(peers):
                theirs = os_[w].at[2 * px + py, other[w]]
                _rcopy(theirs, theirs, fssem.at[w, k], frsem.at[w, k], sib).wait_recv()
        for cp in sends + fwds:
            cp.wait_send()

    dma = pltpu.SemaphoreType.DMA
    return _multi_call(name, body, slots, [jax.ShapeDtypeStruct(x.shape, x.dtype) for x in slots],
                       [dma((n, 3)), dma((n, 3)), dma((n, 3)), dma((n, 3))], in_place=True)


def comm_sibling_halves(name, gs):
    n = len(gs)

    def body(*refs):
        xs, os_ = refs[:n], refs[n:2 * n]
        ssem, rsem = refs[2 * n:]
        mx, my, mc = _pos()
        sib = (mx, my, 1 - mc)
        sends = []
        for w in range(n):
            h = xs[w].shape[1] // 2
            for j in range(N_CHIPS):
                sends.append(_rcopy(xs[w].at[j, pl.ds((1 - mc) * h, h)], os_[w].at[j], ssem.at[w, j], rsem.at[w, j], sib))
        for cp in sends:
            cp.start()
        for w in range(n):
            for j in range(N_CHIPS):
                _rcopy(os_[w].at[j], os_[w].at[j], ssem.at[w, j], rsem.at[w, j], sib).wait_recv()
        for cp in sends:
            cp.wait_send()

    dma = pltpu.SemaphoreType.DMA
    return _multi_call(name, body, gs,
                       [jax.ShapeDtypeStruct((N_CHIPS, g.shape[1] // 2) + g.shape[2:], g.dtype) for g in gs],
                       [dma((n, N_CHIPS)), dma((n, N_CHIPS))])


def comm_alltoall4(name, xs):
    n = len(xs)

    def body(*refs):
        xr, os_ = refs[:n], refs[n:2 * n]
        ssem, rsem = refs[2 * n:]
        mx, my, mc = _pos()
        me = 2 * mx + my
        peers = _chip_peers(mx, my)
        sends = [_rcopy(xr[w].at[2 * px + py], os_[w].at[me], ssem.at[w, k], rsem.at[w, k], (px, py, mc))
                 for w in range(n) for k, (px, py) in enumerate(peers)]
        for cp in sends:
            cp.start()
        for w in range(n):
            for k, (px, py) in enumerate(peers):
                dst = os_[w].at[2 * px + py]
                _rcopy(dst, dst, ssem.at[w, k], rsem.at[w, k], (px, py, mc)).wait_recv()
        for cp in sends:
            cp.wait_send()

    dma = pltpu.SemaphoreType.DMA
    return _multi_call(name, body, xs, [jax.ShapeDtypeStruct(x.shape, x.dtype) for x in xs], [dma((n, 3)), dma((n, 3))])


def comm_sibling_join(name, bs):
    n = len(bs)

    def body(*refs):
        os_ = refs[n:2 * n]
        ssem, rsem = refs[2 * n:]
        mx, my, mc = _pos()
        sib = (mx, my, 1 - mc)
        sends = [_rcopy(os_[w].at[mc], os_[w].at[mc], ssem.at[w], rsem.at[w], sib) for w in range(n)]
        for cp in sends:
            cp.start()
        for w in range(n):
            dst = os_[w].at[1 - mc]
            _rcopy(dst, dst, ssem.at[w], rsem.at[w], sib).wait_recv()
        for cp in sends:
            cp.wait_send()

    dma = pltpu.SemaphoreType.DMA
    return _multi_call(name, body, bs, [jax.ShapeDtypeStruct(b.shape, b.dtype) for b in bs], [dma((n,)), dma((n,))],
                       in_place=True)


def comm_sibling_swap(name, x):
    def body(x_ref, o_ref, ssem, rsem):
        mx, my, mc = _pos()
        cp = _rcopy(x_ref, o_ref, ssem, rsem, (mx, my, 1 - mc))
        cp.start()
        cp.wait_recv()
        cp.wait_send()

    return pl.pallas_call(
        body, out_shape=jax.ShapeDtypeStruct(x.shape, x.dtype), in_specs=[_ANY], out_specs=_ANY,
        scratch_shapes=[pltpu.SemaphoreType.DMA, pltpu.SemaphoreType.DMA], name=name)(x)


def _row_tile(r):
    return _pick(r, (256, 128, 64, 32, 16, 8))


def add_own_half(name, g, recv, out_dtype):
    r, c = g.shape[2:]
    tr = _row_tile(r)
    mc = lax.axis_index("c").astype(jnp.int32).reshape(1)

    def body(c_ref, g_ref, r_ref, o_ref):
        o_ref[...] = (g_ref[...] + r_ref[...]).astype(o_ref.dtype)

    grid_spec = pltpu.PrefetchScalarGridSpec(
        num_scalar_prefetch=1, grid=(N_CHIPS, r // tr),
        in_specs=[pl.BlockSpec((None, None, tr, c), lambda j, i, cr: (j, cr[0], i, 0)),
                  pl.BlockSpec((None, tr, c), lambda j, i, cr: (j, i, 0))],
        out_specs=pl.BlockSpec((None, tr, c), lambda j, i, cr: (j, i, 0)))
    return pl.pallas_call(body, grid_spec=grid_spec, out_shape=jax.ShapeDtypeStruct(recv.shape, out_dtype),
                          compiler_params=_cparams("parallel", "parallel"), name=name)(mc, g, recv)


def cast_into_slot(name, w, chip, dtype):
    r, c = w.shape
    tr = _row_tile(r)

    def body(c_ref, w_ref, o_ref):
        o_ref[...] = w_ref[...].astype(o_ref.dtype)

    grid_spec = pltpu.PrefetchScalarGridSpec(
        num_scalar_prefetch=1, grid=(r // tr,),
        in_specs=[pl.BlockSpec((tr, c), lambda i, cr: (i, 0))],
        out_specs=pl.BlockSpec((None, tr, c), lambda i, cr: (cr[0], i, 0)))
    return pl.pallas_call(body, grid_spec=grid_spec, out_shape=jax.ShapeDtypeStruct((N_CHIPS, r, c), dtype),
                          compiler_params=_cparams("parallel"), name=name)(chip.astype(jnp.int32).reshape(1), w)


def sum_chips_into_half(name, own, arrived, chip, mc):
    r, c = own.shape[1:]
    tr = _row_tile(r)

    def body(s0, s1, s2, s3, s4, own_ref, a_ref, b_ref, d_ref, o_ref):
        o_ref[...] = ((own_ref[...].astype(f32) + a_ref[...].astype(f32))
                      + (b_ref[...].astype(f32) + d_ref[...].astype(f32)))

    slot = lambda k: pl.BlockSpec((None, tr, c), lambda i, *sc, _k=k: (sc[_k][0], i, 0))
    grid_spec = pltpu.PrefetchScalarGridSpec(
        num_scalar_prefetch=5, grid=(r // tr,), in_specs=[slot(0), slot(1), slot(2), slot(3)],
        out_specs=pl.BlockSpec((None, tr, c), lambda i, *sc: (sc[4][0], i, 0)))
    mx, my = lax.axis_index("x"), lax.axis_index("y")
    scal = [v.astype(jnp.int32).reshape(1) for v in
            (2 * mx + my, 2 * (1 - mx) + my, 2 * mx + (1 - my), 2 * (1 - mx) + (1 - my), mc)]
    return pl.pallas_call(body, grid_spec=grid_spec, out_shape=jax.ShapeDtypeStruct((2, r, c), f32),
                          compiler_params=_cparams("parallel"), name=name)(*scal, own, arrived, arrived, arrived)


def sum_slots(name, x):
    r, c = x.shape[1:]
    tr = _row_tile(r)

    def body(x_ref, o_ref):
        o_ref[...] = (x_ref[0].astype(f32) + x_ref[1].astype(f32)) + (x_ref[2].astype(f32) + x_ref[3].astype(f32))

    return pl.pallas_call(
        body, grid=(r // tr,), in_specs=[pl.BlockSpec((N_CHIPS, tr, c), lambda i: (0, i, 0))],
        out_specs=pl.BlockSpec((tr, c), lambda i: (i, 0)), out_shape=jax.ShapeDtypeStruct((r, c), f32),
        compiler_params=_cparams("parallel"), name=name)(x)


def adamw(name, w, g, m, v):
    r, c = w.shape
    tr = _row_tile(r)

    def body(w_ref, g_ref, m_ref, v_ref, d_ref, nm_ref, nv_ref):
        gv = g_ref[...]
        nm = ADAM_B1 * m_ref[...] + (1.0 - ADAM_B1) * gv
        nv = ADAM_B2 * v_ref[...] + (1.0 - ADAM_B2) * (gv * gv)
        m_hat = nm / (1.0 - ADAM_B1 ** ADAM_STEP)
        v_hat = nv / (1.0 - ADAM_B2 ** ADAM_STEP)
        d_ref[...] = -ADAM_LR * (m_hat / (jnp.sqrt(v_hat) + ADAM_EPS) + ADAM_WD * w_ref[...])
        nm_ref[...] = nm
        nv_ref[...] = nv

    blk = pl.BlockSpec((tr, c), lambda i: (i, 0))
    return pl.pallas_call(
        body, grid=(r // tr,), in_specs=[blk] * 4, out_specs=[blk] * 3,
        out_shape=[jax.ShapeDtypeStruct((r, c), f32)] * 3,
        compiler_params=_cparams("parallel"), name=name)(w, g, m, v)


def _pack_rows(n):
    return -(-n // PACK_COLS)


def _pack(arrs, dtype, row_multiple):
    segs = []
    for a in arrs:
        flat = a.astype(dtype).reshape(-1)
        k = _pack_rows(flat.shape[0])
        segs.append(jnp.pad(flat, (0, k * PACK_COLS - flat.shape[0])).reshape(k, PACK_COLS))
    rows = sum(s.shape[0] for s in segs)
    pad = -rows % row_multiple
    if pad:
        segs.append(jnp.zeros((pad, PACK_COLS), dtype))
    return jnp.concatenate(segs, 0)


def _unpack(packed, shapes):
    out, r = [], 0
    for shp in shapes:
        n = math.prod(shp)
        k = _pack_rows(n)
        out.append(packed[r:r + k].reshape(-1)[:n].reshape(shp))
        r += k
    return out


def _gathered_to_full(g, axis):
    t = jnp.moveaxis(g, 0, axis)
    return t.reshape(t.shape[:axis] + (t.shape[axis] * t.shape[axis + 1],) + t.shape[axis + 2:])


def _full_to_shard_major(full, axis):
    shp = full.shape
    t = full.reshape(shp[:axis] + (N_CHIPS, shp[axis] // N_CHIPS) + shp[axis + 1:])
    return jnp.moveaxis(t, axis, 0)


GRAD_ROW_MULTIPLE = 256


def kernel(x, mem, positions, hyb_w_in, dn_conv_w, dn_a_log, dn_dt_bias, dn_norm_g, hyb_w_out, s5_a_re, s5_a_im, s5_log_dt, s5_b_re, s5_b_im, s5_c_re, s5_c_im, s5_d, s5_glu_wo, s5_glu_wg, ln_mix_g, ln_mix_b, xq_w, xk_w, xv_w, xo_w, ln_x_g, ln_x_b, ffn_wg, ffn_wu, ffn_wd, ln_ffn_g, ln_ffn_b, loss_target, m_hyb_w_in, m_dn_conv_w, m_dn_a_log, m_dn_dt_bias, m_dn_norm_g, m_hyb_w_out, m_s5_a_re, m_s5_a_im, m_s5_log_dt, m_s5_b_re, m_s5_b_im, m_s5_c_re, m_s5_c_im, m_s5_d, m_s5_glu_wo, m_s5_glu_wg, m_ln_mix_g, m_ln_mix_b, m_xq_w, m_xk_w, m_xv_w, m_xo_w, m_ln_x_g, m_ln_x_b, m_ffn_wg, m_ffn_wu, m_ffn_wd, m_ln_ffn_g, m_ln_ffn_b, v_hyb_w_in, v_dn_conv_w, v_dn_a_log, v_dn_dt_bias, v_dn_norm_g, v_hyb_w_out, v_s5_a_re, v_s5_a_im, v_s5_log_dt, v_s5_b_re, v_s5_b_im, v_s5_c_re, v_s5_c_im, v_s5_d, v_s5_glu_wo, v_s5_glu_wg, v_ln_mix_g, v_ln_mix_b, v_xq_w, v_xk_w, v_xv_w, v_xo_w, v_ln_x_g, v_ln_x_b, v_ffn_wg, v_ffn_wu, v_ffn_wd, v_ln_ffn_g, v_ln_ffn_b):
    a = dict(locals())
    big = [n for n in WEIGHT_NAMES if n in SHARD_AXIS and n not in GATHER_F32]
    small = [n for n in WEIGHT_NAMES if n not in big]
    chip = 2 * lax.axis_index("x") + lax.axis_index("y")
    for n in FFN_TRANSPOSED:
        for pre in ("", "m_", "v_"):
            a[pre + n] = jnp.swapaxes(a[pre + n], 1, 2)

    mc = lax.axis_index("c")
    view2 = lambda t: t.reshape(-1, t.shape[-1])
    slots = [cast_into_slot("slot_" + n, view2(a[n]), chip, bf16).reshape((N_CHIPS,) + a[n].shape) for n in big]
    gathered = comm_gather_weights("gather_w", slots)
    tiny4 = _unpack_slots(comm_allgather4("gather_w_tiny", _pack([a[n] for n in GATHER_F32], f32, 8)),
                          [a[n].shape for n in GATHER_F32])
    p = {n: a[n] for n in small if n not in GATHER_F32}
    for n, g4 in zip(GATHER_F32, tiny4):
        p[n] = _gathered_to_full(g4, SHARD_AXIS[n])
    for n, g4 in zip(big, gathered):
        p[n] = g4 if n in SHARD_ORDER_GRADS else _gathered_to_full(g4, SHARD_AXIS[n])

    loss, grad_x, grads = local_step(x[0], mem[0], positions, loss_target[0], p)
    loss = lax.psum(loss, ("x", "y", "c"))

    g4s = [grads[n] for n in big]
    recv = comm_sibling_halves("rs_sibling_halves", g4s)
    pairs = []
    for n, g4, r4 in zip(big, g4s, recv):
        lh, cols = g4.shape[1] // 2, g4.shape[-1]
        v4 = g4.reshape(N_CHIPS, 2, -1, cols)
        pairs.append(add_own_half("rs_add_" + n, v4, r4.reshape(N_CHIPS, -1, cols), bf16).reshape((N_CHIPS, lh) + g4.shape[2:]))
    arrived = comm_alltoall4("rs_alltoall", pairs)
    slot3 = lambda t: t.reshape(N_CHIPS, -1, t.shape[-1])
    halves = [sum_chips_into_half("rs_sum_" + n, slot3(pr), slot3(ar), chip, mc) for n, pr, ar in zip(big, pairs, arrived)]
    g_big = {n: t.reshape(a[n].shape) for n, t in zip(big, comm_sibling_join("rs_sibling_join", halves))}

    rpack = _pack([grads[n] for n in small], f32, 8)
    rpair = _add2("ar_add_sibling", rpack, comm_sibling_swap("ar_sibling_swap", rpack))
    g_small = _unpack(sum_slots("ar_sum_chips", comm_allgather4("ar_allgather", rpair)), [grads[n].shape for n in small])
    g_small = {n: (lax.dynamic_index_in_dim(_full_to_shard_major(g, SHARD_AXIS[n]), chip, 0, keepdims=False)
                   if n in SHARD_AXIS else g) for n, g in zip(small, g_small)}

    outs = {}
    for n in big:
        view = lambda t: t.reshape(-1, t.shape[-1])
        d, nm, nv = adamw("adamw_" + n, view(a[n]), view(g_big[n]), view(a["m_" + n]), view(a["v_" + n]))
        outs[n] = (g_big[n],) + tuple(t.reshape(a[n].shape) for t in (d, nm, nv))
    shapes = [a[n].shape for n in small]
    packs = [_pack([a[pre + n] for n in small], f32, 8) for pre in ("", "m_", "v_")]
    upd = adamw("adamw_small", packs[0], _pack([g_small[n] for n in small], f32, 8), packs[1], packs[2])
    for k, n in enumerate(small):
        outs[n] = (g_small[n],) + tuple(_unpack(buf, shapes)[k] for buf in upd)
    for n in FFN_TRANSPOSED:
        outs[n] = tuple(jnp.swapaxes(t, 1, 2) for t in outs[n])
    res = [loss, grad_x[None]]
    for kind in range(4):
        res += [outs[n][kind] for n in WEIGHT_NAMES]
    return tuple(res)


def _unpack_slots(gathered, shapes):
    out, r = [], 0
    for shp in shapes:
        n = math.prod(shp)
        k = _pack_rows(n)
        out.append(gathered[:, r:r + k].reshape(N_CHIPS, -1)[:, :n].reshape((N_CHIPS,) + tuple(shp)))
        r += k
    return out
```

```python
import functools
import math

import jax
import jax.numpy as jnp
from jax import lax
from jax.experimental import pallas as pl
from jax.experimental.pallas import tpu as pltpu

f32 = jnp.float32
bf16 = jnp.bfloat16

D_MODEL = 1024
DEPTH = 4
DN_HEADS = 4
DN_HEAD_DIM = 128
DN_KEY_DIM = 512
DN_QKV_DIM = 1536
DN_CONV = 4
SW_HEADS = 8
SW_HEAD_DIM = 64
SW_DIM = 512
SW_DILATIONS = (1, 4, 16)
SW_BLOCK = 128
ROPE_THETA = 10000.0
S5_GROUP = 16
S5_GROUPS = 64
S5_STATE = 64
X_HEADS = 4
X_HEAD_DIM = 256
FFN_HIDDEN = 2816
ALPHA = (2 * DEPTH) ** 0.25
LN_EPS = 1e-5
RMS_EPS = 1e-6
ADAM_LR, ADAM_B1, ADAM_B2, ADAM_EPS, ADAM_WD, ADAM_STEP = 0.001, 0.9, 0.999, 1e-08, 0.01, 10

BA_PAD = 256
PROJ_COLS = DN_QKV_DIM + DN_KEY_DIM + 3 * SW_DIM + BA_PAD
COL_Z = DN_QKV_DIM
COL_SWQ = COL_Z + DN_KEY_DIM
COL_SWK = COL_SWQ + SW_DIM
COL_SWV = COL_SWK + SW_DIM
COL_BA = COL_SWV + SW_DIM

LANES = 128
SUBLANES = 8
VMEM_LIMIT = 56 * 1024 * 1024
DN_CHUNK = 128
DN_HEADS_PER_STEP = 4


def _cparams(*sem):
    return pltpu.CompilerParams(dimension_semantics=tuple(sem), vmem_limit_bytes=VMEM_LIMIT)


def _dg(x, y, cx, cy):
    return lax.dot_general(x, y, (((cx,), (cy,)), ((), ())), preferred_element_type=f32)


@functools.partial(jax.custom_vjp, nondiff_argnums=(2, 3))
def bdot(a, b, ca, cb):
    return _dg(a.astype(bf16), b.astype(bf16), ca, cb)


def _bdot_fwd(a, b, ca, cb):
    return bdot(a, b, ca, cb), (a, b)


def _bdot_bwd(ca, cb, res, g):
    a, b = res
    g16, a16, b16 = g.astype(bf16), a.astype(bf16), b.astype(bf16)
    da = _dg(g16, b16, 1, 1 - cb) if ca == 1 else _dg(b16, g16, 1 - cb, 1)
    db = _dg(a16, g16, 1 - ca, 0) if cb == 0 else _dg(g16, a16, 0, 1 - ca)
    return da.astype(a.dtype), db.astype(b.dtype)


bdot.defvjp(_bdot_fwd, _bdot_bwd)


def _split_hi_lo(a):
    hi = a.astype(bf16)
    return hi, (a - hi.astype(f32)).astype(bf16)


def _dot3(a, b, ca, cb):
    a_hi, a_lo = _split_hi_lo(a)
    b_hi, b_lo = _split_hi_lo(b)
    return _dg(a_hi, b_hi, ca, cb) + (_dg(a_hi, b_lo, ca, cb) + _dg(a_lo, b_hi, ca, cb))


def hdot(a, b):
    return jnp.dot(a, b, precision=lax.Precision.HIGHEST, preferred_element_type=f32)


def _iota2(shape, dim):
    return lax.broadcasted_iota(jnp.int32, shape, dim)


def _row_spec(r, tm):
    if isinstance(r, tuple):
        arr, width, blk = r
        return arr, pl.BlockSpec((tm, width), lambda i, _b=blk: (i, _b))
    return r, pl.BlockSpec((tm, r.shape[1]), lambda i: (i, 0))


def _par_spec(p):
    return pl.BlockSpec(p.shape, lambda i, _n=p.ndim: (0,) * _n)


def rowmap(name, fn, rows, params, out_cols, tm, out_dtypes=None):
    arrs, specs = zip(*[_row_spec(r, tm) for r in rows])
    s = arrs[0].shape[0]
    n_in = len(rows) + len(params)
    out_dtypes = out_dtypes or [f32] * len(out_cols)

    def body(*refs):
        outs = fn(*[r[...] for r in refs[:n_in]])
        for o_ref, o in zip(refs[n_in:], outs):
            o_ref[...] = o.astype(o_ref.dtype)

    return pl.pallas_call(
        body, grid=(s // tm,),
        in_specs=list(specs) + [_par_spec(p) for p in params],
        out_specs=[pl.BlockSpec((tm, c), lambda i: (i, 0)) for c in out_cols],
        out_shape=[jax.ShapeDtypeStruct((s, c), dt) for c, dt in zip(out_cols, out_dtypes)],
        compiler_params=_cparams("parallel"), name=name)(*arrs, *params)


def rowmap_bwd(name, fn, rows, params, cts, tm, row_mask=None, par_mask=None, row_dtypes=None):
    arrs, specs = zip(*[_row_spec(r, tm) for r in rows])
    s = arrs[0].shape[0]
    ct_groups = [c if isinstance(c, list) else [c] for c in cts]
    ct_arrs, ct_specs = zip(*[_row_spec(a, tm) for grp in ct_groups for a in grp])
    cts = list(ct_arrs)
    nr, npar, nct = len(rows), len(params), len(cts)
    row_mask = row_mask or [True] * nr
    par_mask = par_mask or [True] * npar
    row_idx = [k for k in range(nr) if row_mask[k]]
    par_idx = [k for k in range(npar) if par_mask[k]]
    row_w = [specs[k].block_shape[1] for k in row_idx]

    def body(*refs):
        ins = [r[...] for r in refs[:nr + npar]]
        ct_refs = list(refs[nr + npar:nr + npar + nct])
        ctv = []
        for grp in ct_groups:
            acc = ct_refs.pop(0)[...].astype(f32)
            for _ in grp[1:]:
                acc = acc + ct_refs.pop(0)[...].astype(f32)
            ctv.append(acc)
        ctv = tuple(ctv)
        outs = refs[nr + npar + nct:]
        _, vjp = jax.vjp(fn, *ins)
        grads = vjp(ctv)
        for o_ref, k in zip(outs[:len(row_idx)], row_idx):
            o_ref[...] = grads[k].astype(o_ref.dtype)
        first = pl.program_id(0) == 0
        for o_ref, k in zip(outs[len(row_idx):], par_idx):
            g = grads[nr + k].astype(f32)

            @pl.when(first)
            def _(o_ref=o_ref, g=g):
                o_ref[...] = g

            @pl.when(jnp.logical_not(first))
            def _(o_ref=o_ref, g=g):
                o_ref[...] += g

    res = pl.pallas_call(
        body, grid=(s // tm,),
        in_specs=list(specs) + [_par_spec(p) for p in params]
        + list(ct_specs),
        out_specs=[pl.BlockSpec((tm, w), lambda i: (i, 0)) for w in row_w]
        + [_par_spec(params[k]) for k in par_idx],
        out_shape=[jax.ShapeDtypeStruct((s, w), dt) for w, dt in zip(row_w, row_dtypes or [f32] * len(row_w))]
        + [jax.ShapeDtypeStruct(params[k].shape, f32) for k in par_idx],
        compiler_params=_cparams("arbitrary"), name=name)(*arrs, *params, *cts)
    return list(res[:len(row_idx)]), list(res[len(row_idx):])


def _pick(n, prefs):
    for t in prefs:
        if n % t == 0:
            return t
    return n


MM_CHUNK = 512


def mm_nn(name, a, b, out_dtype=f32):
    m, k = a.shape
    n = b.shape[1]
    tm = _pick(m, (512, 256, 128))
    cn = _pick(n, (MM_CHUNK, 256, 128))

    def body(a_ref, b_ref, o_ref):
        av = a_ref[...].astype(bf16)
        for c in range(n // cn):
            sl = slice(c * cn, (c + 1) * cn)
            o_ref[:, sl] = _dg(av, b_ref[:, sl].astype(bf16), 1, 0).astype(o_ref.dtype)

    return pl.pallas_call(
        body, grid=(m // tm,),
        in_specs=[pl.BlockSpec((tm, k), lambda i: (i, 0)), pl.BlockSpec((k, n), lambda i: (0, 0))],
        out_specs=pl.BlockSpec((tm, n), lambda i: (i, 0)),
        out_shape=jax.ShapeDtypeStruct((m, n), out_dtype),
        compiler_params=_cparams("parallel"), name=name)(a, b)


def mm_nt(name, a, b, out_dtype=f32):
    m, n = a.shape
    k = b.shape[0]
    tm = _pick(m, (512, 256, 128))
    ck = _pick(k, (MM_CHUNK, 256, 128))

    def body(a_ref, b_ref, o_ref):
        av = a_ref[...].astype(bf16)
        for c in range(k // ck):
            sl = slice(c * ck, (c + 1) * ck)
            o_ref[:, sl] = _dg(av, b_ref[sl, :].astype(bf16), 1, 1).astype(o_ref.dtype)

    return pl.pallas_call(
        body, grid=(m // tm,),
        in_specs=[pl.BlockSpec((tm, n), lambda i: (i, 0)), pl.BlockSpec((k, n), lambda i: (0, 0))],
        out_specs=pl.BlockSpec((tm, k), lambda i: (i, 0)),
        out_shape=jax.ShapeDtypeStruct((m, k), out_dtype),
        compiler_params=_cparams("parallel"), name=name)(a, b)


def _stacked(buf):
    if buf is None:
        return [], [], {}
    return [buf], [pl.BlockSpec(memory_space=pl.ANY)], None


def mm_tn(name, a, b, out_dtype=f32, b_col0=0, n_cols=None, stack=None):
    s, m = a.shape
    n = n_cols or b.shape[1]
    tn = _pick(n, (256, 128))
    cm = _pick(m, (256, 128))
    col0 = b_col0 // tn
    in_specs = [pl.BlockSpec((s, m), lambda j: (0, 0)), pl.BlockSpec((s, tn), lambda j: (0, j + col0))]

    if stack is None:
        def body(a_ref, b_ref, o_ref):
            bv = b_ref[...].astype(bf16)
            for c in range(m // cm):
                sl = slice(c * cm, (c + 1) * cm)
                o_ref[sl, :] = _dg(a_ref[:, sl].astype(bf16), bv, 0, 0).astype(o_ref.dtype)

        return pl.pallas_call(
            body, grid=(n // tn,), in_specs=in_specs, out_specs=pl.BlockSpec((m, tn), lambda j: (0, j)),
            out_shape=jax.ShapeDtypeStruct((m, n), out_dtype),
            compiler_params=_cparams("parallel"), name=name)(a, b)

    buf, layer, n_layers = stack
    assert cm * N_CHIPS == m
    extra, extra_specs, _ = _stacked(buf)

    def body_stacked(a_ref, b_ref, *rest):
        o_ref = rest[-1]
        bv = b_ref[...].astype(bf16)
        for c in range(N_CHIPS):
            o_ref[c] = _dg(a_ref[:, c * cm:(c + 1) * cm].astype(bf16), bv, 0, 0).astype(o_ref.dtype)

    return pl.pallas_call(
        body_stacked, grid=(n // tn,), in_specs=in_specs + extra_specs,
        out_specs=pl.BlockSpec((N_CHIPS, None, cm, tn), lambda j: (0, layer, 0, j)),
        out_shape=jax.ShapeDtypeStruct((N_CHIPS, n_layers, cm, n), out_dtype),
        input_output_aliases={2: 0} if extra else {},
        compiler_params=_cparams("parallel"), name=name)(a, b, *extra)


def _postnorm_tile(h, sub, g, b):
    z = ALPHA * h + sub
    mu = jnp.mean(z, -1, keepdims=True)
    zc = z - mu
    var = jnp.mean(zc * zc, -1, keepdims=True)
    return (zc * lax.rsqrt(var + LN_EPS) * g + b,)


def _glu_tile(og):
    o, g = og[:, :D_MODEL], og[:, D_MODEL:]
    return (o * jax.nn.sigmoid(g),)


def _xattn_tile(q, kv):
    outs = []
    for h in range(X_HEADS):
        sl = slice(h * X_HEAD_DIM, (h + 1) * X_HEAD_DIM)
        s = bdot(q[:, sl], kv[:, sl], 1, 1) * (X_HEAD_DIM ** -0.5)
        m = lax.stop_gradient(jnp.max(s, -1, keepdims=True))
        p = jnp.exp(s - m)
        p = p / jnp.sum(p, -1, keepdims=True)
        outs.append(bdot(p, kv[:, D_MODEL + h * X_HEAD_DIM:D_MODEL + (h + 1) * X_HEAD_DIM], 1, 0))
    return (jnp.concatenate(outs, -1),)


TM_ROW = 256


def postnorm_fwd(tag, h, sub, g, b):
    return rowmap("postnorm_" + tag, lambda *a: _postnorm_tile(*a) * 2, [h, sub], [g, b], [D_MODEL] * 2, TM_ROW,
                  out_dtypes=[f32, bf16])


def postnorm_bwd(tag, h, sub, g, b, dy):
    (dh, dsub), (dg, db) = rowmap_bwd("postnorm_bwd_" + tag, _postnorm_tile, [h, sub], [g, b], [dy], TM_ROW,
                                      row_dtypes=[f32, bf16])
    return dh, dsub, dg, db


def xattn_fwd(tag, h, mem, wq, wkv, wo):
    q = mm_nn("xq_" + tag, h, wq, out_dtype=bf16)
    kv = mm_nn("xkv_" + tag, mem, wkv)
    ao = rowmap("xattn_" + tag, _xattn_tile, [q], [kv], [D_MODEL], TM_ROW, out_dtypes=[bf16])[0]
    out = mm_nn("xo_" + tag, ao, wo)
    return out, (q, kv, ao)


def xattn_bwd(tag, layer, h, mem, wq, wkv, wo, res, dout, stacks):
    q, kv, ao = res
    sq, sk, sv, so = stacks
    so = mm_tn("xo_dw_" + tag, ao, dout, stack=(so, layer, DEPTH))
    dao = mm_nt("xo_dx_" + tag, dout, wo)
    (dq,), (dkv,) = rowmap_bwd("xattn_bwd_" + tag, _xattn_tile, [q], [kv], [dao], TM_ROW, row_dtypes=[bf16])
    sq = mm_tn("xq_dw_" + tag, h, dq, stack=(sq, layer, DEPTH))
    dh = mm_nt("xq_dx_" + tag, dq, wq)
    sk = mm_tn("xk_dw_" + tag, mem, dkv, n_cols=D_MODEL, stack=(sk, layer, DEPTH))
    sv = mm_tn("xv_dw_" + tag, mem, dkv, b_col0=D_MODEL, n_cols=D_MODEL, stack=(sv, layer, DEPTH))
    return dh, (sq, sk, sv, so)


FFN_SHARD = FFN_HIDDEN // 4
TM_FFN = 512


def _silu_mul(a, u):
    return jax.nn.silu(a) * u


def ffn_fwd(tag, layer, h, wg, wu, wd):
    s = h.shape[0]
    tm, fs = TM_FFN, FFN_SHARD
    w_in = pl.BlockSpec((None, None, fs, D_MODEL), lambda k, i: (k, layer, 0, 0))
    act = pl.BlockSpec((None, tm, fs), lambda k, i: (k, i, 0))

    def up_body(h_ref, wg_ref, wu_ref, a_ref, u_ref, hid_ref):
        hv = h_ref[...].astype(bf16)
        a, u = _dg(hv, wg_ref[...], 1, 1), _dg(hv, wu_ref[...], 1, 1)
        a_ref[...], u_ref[...] = a.astype(bf16), u.astype(bf16)
        hid_ref[...] = _silu_mul(a, u).astype(bf16)

    a4, u4, hid4 = pl.pallas_call(
        up_body, grid=(4, s // tm),
        in_specs=[pl.BlockSpec((tm, D_MODEL), lambda k, i: (i, 0)), w_in, w_in],
        out_specs=[act, act, act],
        out_shape=[jax.ShapeDtypeStruct((4, s, fs), bf16)] * 3,
        compiler_params=_cparams("parallel", "parallel"), name="ffn_up_" + tag)(h, wg, wu)

    all_act = pl.BlockSpec((4, tm, fs), lambda i: (0, i, 0))
    all_w = pl.BlockSpec((4, None, fs, D_MODEL), lambda i: (0, layer, 0, 0))

    def down_body(hid_ref, wd_ref, o_ref):
        acc = _dg(hid_ref[0], wd_ref[0], 1, 0)
        for k in range(1, 4):
            acc = acc + _dg(hid_ref[k], wd_ref[k], 1, 0)
        o_ref[...] = acc

    out = pl.pallas_call(
        down_body, grid=(s // tm,), in_specs=[all_act, all_w],
        out_specs=pl.BlockSpec((tm, D_MODEL), lambda i: (i, 0)),
        out_shape=jax.ShapeDtypeStruct((s, D_MODEL), f32),
        compiler_params=_cparams("parallel"), name="ffn_down_" + tag)(hid4, wd)
    return out, (a4, u4, hid4)


def ffn_bwd(tag, layer, h, wg, wu, wd, res, dout, stacks=None):
    a4, u4, hid4 = res
    s = h.shape[0]
    tm, fs = TM_FFN, FFN_SHARD
    act = pl.BlockSpec((None, tm, fs), lambda k, i: (k, i, 0))

    def dact_body(do_ref, wd_ref, a_ref, u_ref, da_ref, du_ref):
        dhid = _dg(do_ref[...].astype(bf16), wd_ref[...], 1, 1)
        _, vjp = jax.vjp(_silu_mul, a_ref[...].astype(f32), u_ref[...].astype(f32))
        da, du = vjp(dhid)
        da_ref[...], du_ref[...] = da.astype(bf16), du.astype(bf16)

    da4, du4 = pl.pallas_call(
        dact_body, grid=(4, s // tm),
        in_specs=[pl.BlockSpec((tm, D_MODEL), lambda k, i: (i, 0)),
                  pl.BlockSpec((None, None, fs, D_MODEL), lambda k, i: (k, layer, 0, 0)), act, act],
        out_specs=[act, act], out_shape=[jax.ShapeDtypeStruct((4, s, fs), bf16)] * 2,
        compiler_params=_cparams("parallel", "parallel"), name="ffn_dact_" + tag)(dout, wd, a4, u4)

    all_act = pl.BlockSpec((4, tm, fs), lambda i: (0, i, 0))
    all_w = pl.BlockSpec((4, None, fs, D_MODEL), lambda i: (0, layer, 0, 0))

    def dx_body(da_ref, du_ref, wg_ref, wu_ref, o_ref):
        acc = _dg(da_ref[0], wg_ref[0], 1, 0) + _dg(du_ref[0], wu_ref[0], 1, 0)
        for k in range(1, 4):
            acc = acc + (_dg(da_ref[k], wg_ref[k], 1, 0) + _dg(du_ref[k], wu_ref[k], 1, 0))
        o_ref[...] = acc

    dh = pl.pallas_call(
        dx_body, grid=(s // tm,), in_specs=[all_act, all_act, all_w, all_w],
        out_specs=pl.BlockSpec((tm, D_MODEL), lambda i: (i, 0)),
        out_shape=jax.ShapeDtypeStruct((s, D_MODEL), f32),
        compiler_params=_cparams("parallel"), name="ffn_dx_" + tag)(da4, du4, wg, wu)

    tn = 256
    whole = pl.BlockSpec((None, s, fs), lambda k: (k, 0, 0))
    resident = pl.BlockSpec((s, D_MODEL), lambda k: (0, 0), pipeline_mode=pl.Buffered(1))

    def dwin_body(h_ref, da_ref, du_ref, *rest):
        dwg_ref, dwu_ref = rest[-2:]
        da, du = da_ref[...], du_ref[...]
        for c in range(D_MODEL // tn):
            sl = slice(c * tn, (c + 1) * tn)
            hv = h_ref[:, sl].astype(bf16)
            dwg_ref[:, sl] = _dg(da, hv, 0, 0)
            dwu_ref[:, sl] = _dg(du, hv, 0, 0)

    n_layers = wd.shape[1]
    layer_out = pl.BlockSpec((None, None, fs, D_MODEL), lambda k: (k, layer, 0, 0))
    stack_shape = jax.ShapeDtypeStruct((4, n_layers, fs, D_MODEL), f32)
    prev = [] if stacks is None else list(stacks)
    any_spec = [pl.BlockSpec(memory_space=pl.ANY)]

    dwg, dwu = pl.pallas_call(
        dwin_body, grid=(4,),
        in_specs=[resident, whole, whole] + any_spec * len(prev[:2]),
        out_specs=[layer_out] * 2, out_shape=[stack_shape] * 2,
        input_output_aliases={3: 0, 4: 1} if prev else {},
        compiler_params=_cparams("parallel"), name="ffn_dwin_" + tag)(h, da4, du4, *prev[:2])

    def dwd_body(hid_ref, do_ref, *rest):
        hid = hid_ref[...]
        for c in range(D_MODEL // tn):
            sl = slice(c * tn, (c + 1) * tn)
            rest[-1][:, sl] = _dg(hid, do_ref[:, sl].astype(bf16), 0, 0)

    dwd = pl.pallas_call(
        dwd_body, grid=(4,),
        in_specs=[whole, resident] + any_spec * len(prev[2:]),
        out_specs=layer_out, out_shape=stack_shape,
        input_output_aliases={2: 0} if prev else {},
        compiler_params=_cparams("parallel"), name="ffn_dwd_" + tag)(hid4, dout, *prev[2:])
    return dh, (dwg, dwu, dwd)


def loss_head(y, target):
    s, d = y.shape
    tm = TM_ROW

    def body(y_ref, t_ref, part_ref, dy_ref):
        e = y_ref[...] - t_ref[...]
        dy_ref[...] = e * (1.0 / d)
        p = jnp.sum(e * e, 0, keepdims=True) * (0.5 / d)

        @pl.when(pl.program_id(0) == 0)
        def _():
            part_ref[...] = p

        @pl.when(pl.program_id(0) != 0)
        def _():
            part_ref[...] += p

    return pl.pallas_call(
        body, grid=(s // tm,),
        in_specs=[pl.BlockSpec((tm, d), lambda i: (i, 0))] * 2,
        out_specs=[pl.BlockSpec((1, d), lambda i: (0, 0)), pl.BlockSpec((tm, d), lambda i: (i, 0))],
        out_shape=[jax.ShapeDtypeStruct((1, d), f32), jax.ShapeDtypeStruct((s, d), f32)],
        compiler_params=_cparams("arbitrary"), name="loss_head")(y, target)


TM_CONV = 512


def _conv_rows(xx, w_ref, n_rows):
    a = w_ref[3:4, :] * xx
    for k in (1, 2, 3):
        a = a + w_ref[3 - k:4 - k, :] * pltpu.roll(xx, k, 0)
    return a


def _dn_act(a, is_qk):
    s = jax.nn.silu(a)
    n = s * lax.rsqrt(jnp.sum(s * s, -1, keepdims=True) + RMS_EPS)
    return jnp.where(is_qk, n, s)


def dn_conv_fwd(tag, proj, cw):
    s = proj.shape[0]
    tm, hb = TM_CONV, TM_CONV // SUBLANES

    def body(xh_ref, x_ref, w_ref, o_ref):
        j, t = pl.program_id(0), pl.program_id(1)
        halo = jnp.where(t > 0, xh_ref[...], 0.0)
        xx = jnp.concatenate([halo, x_ref[...]], 0)
        a = _conv_rows(xx, w_ref, tm + SUBLANES)
        o_ref[...] = _dn_act(a, j < 2 * DN_HEADS)[SUBLANES:, :]

    return pl.pallas_call(
        body, grid=(DN_QKV_DIM // LANES, s // tm),
        in_specs=[pl.BlockSpec((SUBLANES, LANES), lambda j, t: (jnp.maximum(t * hb - 1, 0), j)),
                  pl.BlockSpec((tm, LANES), lambda j, t: (t, j)),
                  pl.BlockSpec((DN_CONV, LANES), lambda j, t: (0, j))],
        out_specs=pl.BlockSpec((tm, LANES), lambda j, t: (t, j)),
        out_shape=jax.ShapeDtypeStruct((s, DN_QKV_DIM), f32),
        compiler_params=_cparams("parallel", "parallel"), name="dn_conv_" + tag)(proj, proj, cw)


def dn_conv_bwd(tag, proj, cw, dy):
    s = proj.shape[0]
    tm, hb = TM_CONV, TM_CONV // SUBLANES
    nt = s // tm
    n_ext = tm + 2 * SUBLANES

    def body(xb_ref, x_ref, xa_ref, dy_ref, dya_ref, w_ref, dx_ref, dw_ref):
        j, t = pl.program_id(0), pl.program_id(1)
        xx = jnp.concatenate([jnp.where(t > 0, xb_ref[...], 0.0), x_ref[...],
                              jnp.where(t < nt - 1, xa_ref[...], 0.0)], 0)
        dyy = jnp.concatenate([jnp.zeros((SUBLANES, LANES), f32), dy_ref[...],
                               jnp.where(t < nt - 1, dya_ref[...], 0.0)], 0)
        a = _conv_rows(xx, w_ref, n_ext)
        _, vjp = jax.vjp(lambda v: _dn_act(v, j < 2 * DN_HEADS), a)
        da, = vjp(dyy)
        dx = w_ref[3:4, :] * da
        for k in (1, 2, 3):
            dx = dx + w_ref[3 - k:4 - k, :] * pltpu.roll(da, n_ext - k, 0)
        dx_ref[...] = dx[SUBLANES:SUBLANES + tm, :]
        row = _iota2((n_ext, LANES), 0)
        da_in = jnp.where((row >= SUBLANES) & (row < SUBLANES + tm), da, 0.0)
        r8 = _iota2((SUBLANES, LANES), 0)
        dw = jnp.zeros((SUBLANES, LANES), f32)
        for k in range(DN_CONV):
            xs = xx if k == 0 else pltpu.roll(xx, k, 0)
            dw = dw + jnp.where(r8 == 3 - k, jnp.sum(da_in * xs, 0, keepdims=True), 0.0)

        @pl.when(t == 0)
        def _():
            dw_ref[...] = dw

        @pl.when(t != 0)
        def _():
            dw_ref[...] += dw

    nb8 = s // SUBLANES
    return pl.pallas_call(
        body, grid=(DN_QKV_DIM // LANES, nt),
        in_specs=[pl.BlockSpec((SUBLANES, LANES), lambda j, t: (jnp.maximum(t * hb - 1, 0), j)),
                  pl.BlockSpec((tm, LANES), lambda j, t: (t, j)),
                  pl.BlockSpec((SUBLANES, LANES), lambda j, t: (jnp.minimum((t + 1) * hb, nb8 - 1), j)),
                  pl.BlockSpec((tm, LANES), lambda j, t: (t, j)),
                  pl.BlockSpec((SUBLANES, LANES), lambda j, t: (jnp.minimum((t + 1) * hb, nb8 - 1), j)),
                  pl.BlockSpec((DN_CONV, LANES), lambda j, t: (0, j))],
        out_specs=[pl.BlockSpec((tm, LANES), lambda j, t: (t, j)),
                   pl.BlockSpec((SUBLANES, LANES), lambda j, t: (0, j))],
        out_shape=[jax.ShapeDtypeStruct((s, DN_QKV_DIM), f32), jax.ShapeDtypeStruct((SUBLANES, DN_QKV_DIM), f32)],
        compiler_params=_cparams("parallel", "arbitrary"), name="dn_conv_bwd_" + tag)(proj, proj, proj, dy, dy, cw)


def _gate_tile(ba, eb, ea, alog, dtb):
    beta = jax.nn.sigmoid(hdot(ba, eb))
    g = -jnp.exp(alog) * jax.nn.softplus(hdot(ba, ea) + dtb)
    return beta, g


def _each(fn, *lists):
    return [fn(*args) for args in zip(*lists)]


@functools.partial(jax.custom_vjp, nondiff_argnums=(1,))
def _halves(x, axis):
    h = x.shape[axis] // 2
    return (x[:h], x[h:]) if axis == 0 else (x[:, :h], x[:, h:])


def _halves_fwd(x, axis):
    return _halves(x, axis), None


def _halves_bwd(axis, _, g):
    return (jnp.concatenate(g, axis),)


_halves.defvjp(_halves_fwd, _halves_bwd)


def _tri_inv_unit(lowers):
    c = lowers[0].shape[0]
    r, col = _iota2((c, c), 0), _iota2((c, c), 1)
    eye = jnp.where(r == col, 1.0, 0.0).astype(f32)
    invs = None
    sh = 0
    while (1 << sh) < c:
        same_2b = lax.shift_right_logical(r, sh + 1) == lax.shift_right_logical(col, sh + 1)
        diff_b = lax.shift_right_logical(r, sh) != lax.shift_right_logical(col, sh)
        offs = [jnp.where(same_2b & diff_b, low, 0.0) for low in lowers]
        if invs is None:
            invs = [eye - off for off in offs]
        else:
            part = _each(lambda inv, off: _dot3(inv, off, 1, 0), invs, offs)
            invs = _each(lambda inv, p: inv - _dot3(p, inv, 1, 0), invs, part)
        sh += 1
    return invs


@jax.custom_vjp
def _known_inverse(lower, tinv):
    return tinv


def _known_inverse_fwd(lower, tinv):
    return tinv, tinv


def _known_inverse_bwd(tinv, g):
    tt = tinv.T
    return -hdot(hdot(tt, g), tt), jnp.zeros_like(tinv)


_known_inverse.defvjp(_known_inverse_fwd, _known_inverse_bwd)


def _delta_chunk(q, k, v, gb, betab, state, tinv_known=None):
    c, hd = DN_CHUNK, DN_HEAD_DIM
    r, col = _iota2((c, c), 0), _iota2((c, c), 1)
    causal, strict = r >= col, r > col
    tril = jnp.where(causal, 1.0, 0.0).astype(f32)
    gc = _each(lambda g: hdot(tril, g), gb)
    decay = _each(lambda g: jnp.where(causal, jnp.exp(jnp.where(causal, g - g.T, 0.0)), 0.0), gc)
    qs = _each(lambda t: t * (DN_HEAD_DIM ** -0.5), q)
    kb = _each(lambda a, b: a * b, k, betab)
    kq = _each(lambda a, b, kk: _halves(bdot(jnp.concatenate([a, b], 0), kk, 1, 1), 0), kb, qs, k)
    lower = _each(lambda x, d: jnp.where(strict, x[0], 0.0) * d, kq, decay)
    intra = _each(lambda x, d: x[1] * d, kq, decay)
    tinv = _tri_inv_unit(lower) if tinv_known is None else _each(_known_inverse, lower, tinv_known)
    eg = _each(jnp.exp, gc)
    uw = _each(lambda t, vv, b, kb_, e: _halves(hdot(t, jnp.concatenate([vv * b, kb_ * e], 1)), 1),
               tinv, v, betab, kb, eg)
    gl = _each(lambda g: jnp.sum(jnp.where(r == c - 1, g, 0.0), 0, keepdims=True), gc)
    k_dec = _each(lambda kk, a, g: kk * jnp.exp(a - g), k, gl, gc)
    ws = _each(lambda x, t, e, st: _halves(bdot(jnp.concatenate([x[1], t * e], 0), st, 1, 0), 0), uw, qs, eg, state)
    v_new = _each(lambda x, y: x[0] - y[0], uw, ws)
    out = _each(lambda y, a, vn: y[1] + bdot(a, vn, 1, 0), ws, intra, v_new)
    new_state = _each(lambda st, a, kd, vn: st * jnp.exp(a) + bdot(kd, vn, 0, 0), state, gl, k_dec, v_new)
    return tuple(out), tuple(new_state), tuple(tinv)


def delta_fwd(tag, qkv, gb, betab):
    s = qkv.shape[0]
    c, hd = DN_CHUNK, DN_HEAD_DIM
    n = s // c

    hg, ng = DN_HEADS_PER_STEP, DN_HEADS // DN_HEADS_PER_STEP

    def body(q_ref, k_ref, v_ref, g_ref, b_ref, o_ref, st_ref, ti_ref, state):
        @pl.when(pl.program_id(1) == 0)
        def _():
            state[...] = jnp.zeros_like(state)

        heads = lambda ref: tuple(ref[:, j * hd:(j + 1) * hd] for j in range(hg))
        st = tuple(state[j] for j in range(hg))
        outs, news, tinv = _delta_chunk(heads(q_ref), heads(k_ref), heads(v_ref), heads(g_ref), heads(b_ref), st)
        for j in range(hg):
            st_ref[j] = st[j]
            ti_ref[j] = tinv[j]
            o_ref[:, j * hd:(j + 1) * hd] = outs[j]
            state[j] = news[j]

    blk = lambda off: pl.BlockSpec((c, hg * hd), lambda h, i, _o=off: (i, h + _o))
    per_chunk = pl.BlockSpec((hg, None, hd, hd), lambda h, i: (h, i, 0, 0))
    return pl.pallas_call(
        body, grid=(ng, n),
        in_specs=[blk(0), blk(ng), blk(2 * ng), blk(0), blk(0)],
        out_specs=[blk(0), per_chunk, per_chunk],
        out_shape=[jax.ShapeDtypeStruct((s, DN_KEY_DIM), f32)] + [jax.ShapeDtypeStruct((DN_HEADS, n, hd, hd), f32)] * 2,
        scratch_shapes=[pltpu.VMEM((hg, hd, hd), f32)],
        compiler_params=_cparams("parallel", "arbitrary"), name="delta_" + tag)(qkv, qkv, qkv, gb, betab)


def delta_bwd(tag, qkv, gb, betab, states, tinvs, do):
    s = qkv.shape[0]
    c, hd = DN_CHUNK, DN_HEAD_DIM
    n = s // c

    hg, ng = DN_HEADS_PER_STEP, DN_HEADS // DN_HEADS_PER_STEP

    def body(q_ref, k_ref, v_ref, g_ref, b_ref, st_ref, ti_ref, do_ref, dqkv_ref, dg_ref, db_ref, dstate):
        @pl.when(pl.program_id(1) == 0)
        def _():
            dstate[...] = jnp.zeros_like(dstate)

        heads = lambda ref: tuple(ref[:, j * hd:(j + 1) * hd] for j in range(hg))
        tinv = tuple(ti_ref[j] for j in range(hg))
        _, vjp = jax.vjp(lambda *args: _delta_chunk(*args, tinv_known=tinv)[:2],
                         heads(q_ref), heads(k_ref), heads(v_ref), heads(g_ref), heads(b_ref),
                         tuple(st_ref[j] for j in range(hg)))
        grads = vjp((heads(do_ref), tuple(dstate[j] for j in range(hg))))
        for part, g in enumerate(grads[:3]):
            for j in range(hg):
                dqkv_ref[:, part * DN_KEY_DIM + j * hd:part * DN_KEY_DIM + (j + 1) * hd] = g[j]
        for ref, g in zip((dg_ref, db_ref), grads[3:5]):
            for j in range(hg):
                ref[:, j * hd:(j + 1) * hd] = g[j]
        for j in range(hg):
            dstate[j] = grads[5][j]

    assert ng == 1
    blk = lambda off: pl.BlockSpec((c, hg * hd), lambda h, i, _o=off: (n - 1 - i, h + _o))
    return pl.pallas_call(
        body, grid=(ng, n),
        in_specs=[blk(0), blk(ng), blk(2 * ng), blk(0), blk(0)]
        + [pl.BlockSpec((hg, None, hd, hd), lambda h, i: (h, n - 1 - i, 0, 0))] * 2 + [blk(0)],
        out_specs=[pl.BlockSpec((c, DN_QKV_DIM), lambda h, i: (n - 1 - i, 0)), blk(0), blk(0)],
        out_shape=[jax.ShapeDtypeStruct((s, DN_QKV_DIM), f32)] + [jax.ShapeDtypeStruct((s, DN_KEY_DIM), f32)] * 2,
        scratch_shapes=[pltpu.VMEM((hg, hd, hd), f32)],
        compiler_params=_cparams("parallel", "arbitrary"),
        name="delta_bwd_" + tag)(qkv, qkv, qkv, gb, betab, states, tinvs, do)


def _dn_out_tile(o, z, ng):
    outs = []
    for h in range(DN_HEADS):
        sl = slice(h * DN_HEAD_DIM, (h + 1) * DN_HEAD_DIM)
        oh = o[:, sl]
        nrm = oh * lax.rsqrt(jnp.mean(oh * oh, -1, keepdims=True) + RMS_EPS) * ng[:, sl]
        outs.append(nrm * jax.nn.silu(z[:, sl]))
    return (jnp.concatenate(outs, -1),)


def _head_selectors():
    r, c = _iota2((BA_PAD, DN_KEY_DIM), 0), _iota2((BA_PAD, DN_KEY_DIM), 1) // DN_HEAD_DIM
    return (r == c).astype(f32), (r == c + DN_HEADS).astype(f32)


def dn_mixer_fwd(tag, proj, cw, alog_b, dtb_b, ng_b):
    eb, ea = _head_selectors()
    ba = (proj, BA_PAD, COL_BA // BA_PAD)
    qkv = dn_conv_fwd(tag, proj, cw)
    betab, gb = rowmap("dn_gate_" + tag, _gate_tile, [ba], [eb, ea, alog_b, dtb_b], [DN_KEY_DIM] * 2, TM_ROW)
    o, states, tinvs = delta_fwd(tag, qkv, gb, betab)
    z = (proj, DN_KEY_DIM, COL_Z // DN_KEY_DIM)
    a_out = rowmap("dn_out_" + tag, _dn_out_tile, [o, z], [ng_b], [DN_KEY_DIM], TM_ROW)[0]
    return a_out, (qkv, betab, gb, o, states, tinvs)


def dn_mixer_bwd(tag, proj, cw, alog_b, dtb_b, ng_b, res, da_out):
    qkv, betab, gb, o, states, tinvs = res
    eb, ea = _head_selectors()
    ba = (proj, BA_PAD, COL_BA // BA_PAD)
    z = (proj, DN_KEY_DIM, COL_Z // DN_KEY_DIM)
    (do, dz), (dng,) = rowmap_bwd("dn_out_bwd_" + tag, _dn_out_tile, [o, z], [ng_b], [da_out], TM_ROW)
    dqkv, dgb, dbetab = delta_bwd(tag, qkv, gb, betab, states, tinvs, do)
    dqkv_raw, dcw = dn_conv_bwd(tag, proj, cw, dqkv)
    (dba,), (dalog, ddtb) = rowmap_bwd("dn_gate_bwd_" + tag, _gate_tile, [ba], [eb, ea, alog_b, dtb_b],
                                       [dbetab, dgb], TM_ROW, par_mask=[False, False, True, True])
    return dqkv_raw, dz, dba, dcw[:DN_CONV], dalog, ddtb, dng


def _swap_halves(x):
    n = x.shape[1]
    first = (_iota2((1, n), 1) % SW_HEAD_DIM) < SW_HEAD_DIM // 2
    return jnp.where(first, pltpu.roll(x, n - SW_HEAD_DIM // 2, 1), pltpu.roll(x, SW_HEAD_DIM // 2, 1))


def _rope_apply(x, cos, sin_signed):
    return x * cos + _swap_halves(x) * sin_signed


def _rope_transpose(dy, cos, sin_signed):
    return dy * cos + _swap_halves(dy * sin_signed)


def rope_tables(positions, s):
    half = SW_HEAD_DIM // 2
    inv_freq = ROPE_THETA ** (-jnp.arange(0, SW_HEAD_DIM, 2, dtype=f32) / SW_HEAD_DIM)
    ang = positions.reshape(s, 1).astype(f32) * inv_freq[None, :]
    cos, sin = jnp.cos(ang), jnp.sin(ang)
    cos_t = jnp.tile(jnp.concatenate([cos, cos], 1), (1, SW_HEADS))
    sin_t = jnp.tile(jnp.concatenate([-sin, sin], 1), (1, SW_HEADS))
    assert cos_t.shape == (s, SW_DIM) and half * 2 == SW_HEAD_DIM
    return cos_t, sin_t


def rope_fwd(tag, proj, cos, sin):
    def fn(q, k, v, c, sg):
        return _rope_apply(q, c, sg), _rope_apply(k, c, sg), v

    rows = [(proj, SW_DIM, COL_SWQ // SW_DIM), (proj, SW_DIM, COL_SWK // SW_DIM), (proj, SW_DIM, COL_SWV // SW_DIM), cos, sin]
    return rowmap("rope_" + tag, fn, rows, [], [SW_DIM] * 3, TM_ROW, out_dtypes=[bf16] * 3)


def _swa_block(q, kp, kc, vp, vc, first):
    blk = SW_BLOCK
    kk = jnp.concatenate([kp, kc], 0)
    vv = jnp.concatenate([vp, vc], 0)
    dist = (_iota2((blk, 2 * blk), 0) + blk) - _iota2((blk, 2 * blk), 1)
    kj = _iota2((blk, 2 * blk), 1)
    valid = (dist >= 0) & (dist <= blk) & ((kj >= blk) | jnp.logical_not(first))
    lane_head = _iota2((1, LANES), 1) // SW_HEAD_DIM
    outs, lses = [], []
    for p in range(SW_DIM // LANES):
        sl = slice(p * LANES, (p + 1) * LANES)
        qp, kp_, vp_ = q[:, sl], kk[:, sl], vv[:, sl]
        o_pair = jnp.zeros((blk, LANES), f32)
        l_pair = jnp.zeros((blk, LANES), f32)
        for e in range(LANES // SW_HEAD_DIM):
            msk = lane_head == e
            sc = bdot(jnp.where(msk, qp, 0.0), kp_, 1, 1) * (SW_HEAD_DIM ** -0.5)
            sc = jnp.where(valid, sc, -1e30)
            m = lax.stop_gradient(jnp.max(sc, -1, keepdims=True))
            pe = jnp.exp(sc - m)
            l = jnp.sum(pe, -1, keepdims=True)
            o = bdot(pe, vp_, 1, 0) / l
            o_pair = o_pair + jnp.where(msk, o, 0.0)
            l_pair = l_pair + jnp.where(msk, m + jnp.log(l), 0.0)
        outs.append(o_pair)
        lses.append(l_pair)
    return jnp.concatenate(outs, -1), jnp.concatenate(lses, -1)


def _swa_specs(r):
    cur = pl.BlockSpec((SW_BLOCK, SW_DIM), lambda rho, n: (n, rho))
    prev = pl.BlockSpec((SW_BLOCK, SW_DIM), lambda rho, n: (jnp.maximum(n - 1, 0), rho))
    return cur, prev


def swa_fwd(tag, r, q, k, v):
    s = q.shape[0]
    ln = s // r
    q2, k2, v2 = (t.reshape(ln, r * SW_DIM) for t in (q, k, v))
    cur, prev = _swa_specs(r)

    def body(q_ref, kp_ref, kc_ref, vp_ref, vc_ref, o_ref, l_ref):
        ins = [r[...].astype(f32) for r in (q_ref, kp_ref, kc_ref, vp_ref, vc_ref)]
        o, l = _swa_block(*ins, pl.program_id(1) == 0)
        o_ref[...] = o
        l_ref[...] = l

    o, l = pl.pallas_call(
        body, grid=(r, ln // SW_BLOCK),
        in_specs=[cur, prev, cur, prev, cur], out_specs=[cur, cur],
        out_shape=[jax.ShapeDtypeStruct((ln, r * SW_DIM), f32)] * 2,
        compiler_params=_cparams("parallel", "parallel"), name=f"swa{r}_{tag}")(q2, k2, k2, v2, v2)
    return o.reshape(s, SW_DIM), l.reshape(s, SW_DIM)


def swa_bwd(tag, r, q, k, v, do, dl):
    s = q.shape[0]
    ln = s // r
    q2, k2, v2, do2, dl2 = (t.reshape(ln, r * SW_DIM) for t in (q, k, v, do, dl))
    cur, prev = _swa_specs(r)

    def body(q_ref, kp_ref, kc_ref, vp_ref, vc_ref, do_ref, dl_ref, dq_ref, dka_ref, dkb_ref, dva_ref, dvb_ref):
        first = pl.program_id(1) == 0
        ins = [r[...].astype(f32) for r in (q_ref, kp_ref, kc_ref, vp_ref, vc_ref)]
        _, vjp = jax.vjp(lambda *a: _swa_block(*a, first), *ins)
        dq_ref[...], dka_ref[...], dkb_ref[...], dva_ref[...], dvb_ref[...] = vjp((do_ref[...], dl_ref[...]))

    outs = pl.pallas_call(
        body, grid=(r, ln // SW_BLOCK),
        in_specs=[cur, prev, cur, prev, cur, cur, cur], out_specs=[cur] * 5,
        out_shape=[jax.ShapeDtypeStruct((ln, r * SW_DIM), f32)] * 5,
        compiler_params=_cparams("parallel", "parallel"), name=f"swa{r}_bwd_{tag}")(q2, k2, k2, v2, v2, do2, dl2)
    return [t.reshape(s, SW_DIM) for t in outs]


def _combine_tile(o1, l1, o2, l2, o3, l3):
    m = lax.stop_gradient(jnp.maximum(jnp.maximum(l1, l2), l3))
    e1, e2, e3 = jnp.exp(l1 - m), jnp.exp(l2 - m), jnp.exp(l3 - m)
    return ((o1 * e1 + o2 * e2 + o3 * e3) / (e1 + e2 + e3),)


def swa_merge_bwd(tag, grads, cos, sin):
    s = cos.shape[0]
    tm = SW_BLOCK
    nt = s // tm
    here = pl.BlockSpec((tm, SW_DIM), lambda i: (i, 0))
    arrs, specs = [], []
    for r, g in zip(SW_DILATIONS, grads):
        ahead = pl.BlockSpec((tm, SW_DIM), lambda i, _r=r: (jnp.minimum(i + _r, nt - 1), 0))
        arrs += g
        specs += [here, ahead, here, ahead, here]

    def body(*refs):
        i = pl.program_id(0)
        c_ref, s_ref = refs[15], refs[16]
        dq_ref, dk_ref, dv_ref = refs[17:]
        dq = jnp.zeros((tm, SW_DIM), f32)
        dk = jnp.zeros((tm, SW_DIM), f32)
        dv = jnp.zeros((tm, SW_DIM), f32)
        for b, r in enumerate(SW_DILATIONS):
            gq, gka, gkb, gva, gvb = refs[5 * b:5 * b + 5]
            inside = i + r < nt
            dq = dq + gq[...]
            dk = dk + gkb[...] + jnp.where(inside, gka[...], 0.0)
            dv = dv + gvb[...] + jnp.where(inside, gva[...], 0.0)
        dq_ref[...] = _rope_transpose(dq, c_ref[...], s_ref[...])
        dk_ref[...] = _rope_transpose(dk, c_ref[...], s_ref[...])
        dv_ref[...] = dv

    return pl.pallas_call(
        body, grid=(nt,), in_specs=specs + [here, here], out_specs=[here] * 3,
        out_shape=[jax.ShapeDtypeStruct((s, SW_DIM), f32)] * 3,
        compiler_params=_cparams("parallel"), name="swa_merge_bwd_" + tag)(*arrs, cos, sin)


def swa_mixer_fwd(tag, proj, cos, sin):
    q, k, v = rope_fwd(tag, proj, cos, sin)
    ols = []
    for r in SW_DILATIONS:
        ols += list(swa_fwd(tag, r, q, k, v))
    b_out = rowmap("swa_comb_" + tag, _combine_tile, ols, [], [SW_DIM], TM_ROW)[0]
    return b_out, (q, k, v, ols)


def swa_mixer_bwd(tag, cos, sin, res, db_out):
    q, k, v, ols = res
    dols, _ = rowmap_bwd("swa_comb_bwd_" + tag, _combine_tile, ols, [], [db_out], TM_ROW)
    grads = [swa_bwd(tag, r, q, k, v, dols[2 * b], dols[2 * b + 1]) for b, r in enumerate(SW_DILATIONS)]
    return swa_merge_bwd(tag, grads, cos, sin)


TM_S5 = 256
S5_GPB = LANES // S5_GROUP
S5_NBLK = D_MODEL // LANES
S5_HALF = S5_GPB * S5_STATE
S5_BW = 2 * S5_HALF
S5_WIDTH = S5_NBLK * S5_BW
S5_TABW = S5_NBLK * S5_HALF


def _s5_disc_tile(a_re, a_im, log_dt, b_re, b_im, expand):
    dt = jnp.exp(log_dt)
    mag = jnp.exp(a_re * dt)
    abar_re, abar_im = mag * jnp.cos(a_im * dt), mag * jnp.sin(a_im * dt)
    n_re, n_im = abar_re - 1.0, abar_im
    den = a_re * a_re + a_im * a_im
    c_re = (n_re * a_re + n_im * a_im) / den
    c_im = (n_im * a_re - n_re * a_im) / den
    cx_re, cx_im = hdot(c_re, expand), hdot(c_im, expand)
    return abar_re, abar_im, cx_re * b_re - cx_im * b_im, cx_re * b_im + cx_im * b_re


def _s5_expand():
    return (_iota2((S5_STATE, S5_STATE * S5_GROUP), 1) // S5_GROUP == _iota2((S5_STATE, S5_STATE * S5_GROUP), 0)).astype(f32)


def s5_tables(a_re, a_im, log_dt):
    lanes = lambda v: v.reshape(1, S5_TABW)
    dt = jnp.broadcast_to(log_dt.reshape(S5_GROUPS, 1), (S5_GROUPS, S5_STATE))
    t = TM_S5

    def body(are_ref, aim_ref, ldt_ref, ar_ref, ai_ref, arr_ref, air_ref):
        dtv = jnp.exp(ldt_ref[...])
        lre, lim = are_ref[...] * dtv, aim_ref[...] * dtv
        row = _iota2((t, S5_HALF), 0)
        for asc, o_re, o_im in ((True, ar_ref, ai_ref), (False, arr_ref, air_ref)):
            n = (row + 1 if asc else t - row).astype(f32)
            mag = jnp.exp(n * lre)
            o_re[...] = mag * jnp.cos(n * lim)
            o_im[...] = mag * jnp.sin(n * lim)

    lane = pl.BlockSpec((1, S5_HALF), lambda j: (0, j))
    tab = pl.BlockSpec((t, S5_HALF), lambda j: (0, j))
    return pl.pallas_call(
        body, grid=(S5_NBLK,), in_specs=[lane] * 3, out_specs=[tab] * 4,
        out_shape=[jax.ShapeDtypeStruct((t, S5_TABW), f32)] * 4,
        compiler_params=_cparams("parallel"), name="s5_tables")(lanes(a_re), lanes(a_im), lanes(dt))


def s5_pack_weights(bbar_re, bbar_im, c_re, c_im):
    eye = jnp.eye(S5_GPB, dtype=f32)
    bb = jnp.stack([bbar_re.reshape(S5_GROUPS, S5_STATE, S5_GROUP), bbar_im.reshape(S5_GROUPS, S5_STATE, S5_GROUP)], 1)
    bb = bb.transpose(0, 3, 1, 2).reshape(S5_NBLK, S5_GPB, S5_GROUP, 2, S5_STATE)
    wb = (bb[:, :, :, :, None, :] * eye[None, :, None, None, :, None]).reshape(S5_NBLK, LANES, S5_BW)
    cc = jnp.stack([c_re, -c_im], 1)
    cc = cc.reshape(S5_NBLK, S5_GPB, 2, S5_GROUP, S5_STATE).transpose(0, 2, 1, 4, 3)
    wc = (cc[:, :, :, :, None, :] * eye[None, None, :, None, :, None]).reshape(S5_NBLK, S5_BW, LANES)
    return wb, wc


def s5_unpack_weight_grads(dwb, dwc):
    d6 = dwb.reshape(S5_NBLK, S5_GPB, S5_GROUP, 2, S5_GPB, S5_STATE)
    dbb = jnp.stack([d6[:, gl, :, :, gl, :] for gl in range(S5_GPB)])
    dbb = dbb.transpose(1, 0, 3, 4, 2).reshape(S5_GROUPS, 2, S5_STATE * S5_GROUP)
    c6 = dwc.reshape(S5_NBLK, 2, S5_GPB, S5_STATE, S5_GPB, S5_GROUP)
    dcc = jnp.stack([c6[:, :, gl, :, gl, :] for gl in range(S5_GPB)])
    dcc = dcc.transpose(1, 0, 2, 4, 3).reshape(S5_GROUPS, 2, S5_GROUP, S5_STATE)
    return dbb[:, 0], dbb[:, 1], dcc[:, 0], -dcc[:, 1]


def _s5_step_rows(t):
    d, out = 1, []
    while d < t:
        out.append(d)
        d *= 2
    return out


def s5_core_fwd(tag, u, wb, wc, a1, a2, dskip):
    s = u.shape[0]
    t = TM_S5

    def body(u_ref, wb_ref, wc_ref, ar_ref, ai_ref, d_ref, y_ref, x_ref, carry):
        @pl.when(pl.program_id(1) == 0)
        def _():
            carry[...] = jnp.zeros_like(carry)

        uv = u_ref[...]
        bu = bdot(uv, wb_ref[...], 1, 0)
        row = _iota2((t, LANES), 0)
        for c in range(S5_HALF // LANES):
            re, im = slice(c * LANES, (c + 1) * LANES), slice(S5_HALF + c * LANES, S5_HALF + (c + 1) * LANES)
            xr, xi = bu[:, re], bu[:, im]
            for d in _s5_step_rows(t):
                ar, ai = ar_ref[d - 1:d, re], ai_ref[d - 1:d, re]
                if d % SUBLANES:
                    keep = row >= d
                    sr = jnp.where(keep, pltpu.roll(xr, d, 0), 0.0)
                    si = jnp.where(keep, pltpu.roll(xi, d, 0), 0.0)
                    xr, xi = xr + ar * sr - ai * si, xi + ar * si + ai * sr
                else:
                    sr, si = xr[:t - d], xi[:t - d]
                    xr = jnp.concatenate([xr[:d], xr[d:] + (ar * sr - ai * si)], 0)
                    xi = jnp.concatenate([xi[:d], xi[d:] + (ar * si + ai * sr)], 0)
            cr, ci = carry[:, re], carry[:, im]
            ar, ai = ar_ref[:, re], ai_ref[:, re]
            x_ref[:, re] = xr + ar * cr - ai * ci
            x_ref[:, im] = xi + ar * ci + ai * cr
        carry[...] = x_ref[t - 1:t, :]
        y_ref[...] = bdot(x_ref[...], wc_ref[...], 1, 0) + d_ref[...] * uv

    tab = pl.BlockSpec((t, S5_HALF), lambda j, i: (0, j))
    return pl.pallas_call(
        body, grid=(S5_NBLK, s // t),
        in_specs=[pl.BlockSpec((t, LANES), lambda j, i: (i, j)),
                  pl.BlockSpec((None, LANES, S5_BW), lambda j, i: (j, 0, 0)),
                  pl.BlockSpec((None, S5_BW, LANES), lambda j, i: (j, 0, 0)),
                  tab, tab, pl.BlockSpec((1, LANES), lambda j, i: (0, j))],
        out_specs=[pl.BlockSpec((t, LANES), lambda j, i: (i, j)), pl.BlockSpec((t, S5_BW), lambda j, i: (i, j))],
        out_shape=[jax.ShapeDtypeStruct((s, D_MODEL), f32), jax.ShapeDtypeStruct((s, S5_WIDTH), f32)],
        scratch_shapes=[pltpu.VMEM((1, S5_BW), f32)],
        compiler_params=_cparams("parallel", "arbitrary"), name="s5_core_" + tag)(u, wb, wc, a1, a2, dskip)


def s5_core_bwd(tag, u, x, wb, wc, a1, a2, a1r, a2r, dskip, dy):
    s = u.shape[0]
    t = TM_S5
    nt = s // t
    hb = t // SUBLANES

    def body(u_ref, dy_ref, x_ref, xh_ref, wb_ref, wc_ref, ar_ref, ai_ref, arr_ref, air_ref, d_ref,
             du_ref, dwb_ref, dwc_ref, dd_ref, q1_ref, q2_ref, carry, lam_scr):
        i = pl.program_id(1)
        tt = nt - 1 - i

        @pl.when(i == 0)
        def _():
            carry[...] = jnp.zeros_like(carry)

        uv, dyv, xv = u_ref[...], dy_ref[...], x_ref[...]
        lam = bdot(dyv, wc_ref[...], 1, 1)
        row = _iota2((t, LANES), 0)
        x_last = jnp.where(tt > 0, xh_ref[SUBLANES - 1:SUBLANES, :], 0.0)
        q1s, q2s = [], []
        for c in range(S5_HALF // LANES):
            re, im = slice(c * LANES, (c + 1) * LANES), slice(S5_HALF + c * LANES, S5_HALF + (c + 1) * LANES)
            lr, li = lam[:, re], lam[:, im]
            for d in _s5_step_rows(t):
                ar, ai = ar_ref[d - 1:d, re], ai_ref[d - 1:d, re]
                if d % SUBLANES:
                    keep = row < t - d
                    sr = jnp.where(keep, pltpu.roll(lr, t - d, 0), 0.0)
                    si = jnp.where(keep, pltpu.roll(li, t - d, 0), 0.0)
                    lr, li = lr + ar * sr + ai * si, li + ar * si - ai * sr
                else:
                    sr, si = lr[d:], li[d:]
                    lr = jnp.concatenate([lr[:t - d] + (ar * sr + ai * si), lr[t - d:]], 0)
                    li = jnp.concatenate([li[:t - d] + (ar * si - ai * sr), li[t - d:]], 0)
            cr, ci = carry[:, re], carry[:, im]
            ar, ai = arr_ref[:, re], air_ref[:, re]
            lr, li = lr + ar * cr + ai * ci, li + ar * ci - ai * cr
            lam_scr[:, re] = lr
            lam_scr[:, im] = li
            pr = jnp.where(row == 0, x_last[:, re], pltpu.roll(xv[:, re], 1, 0))
            pi = jnp.where(row == 0, x_last[:, im], pltpu.roll(xv[:, im], 1, 0))
            p1, p2 = lr * pr + li * pi, li * pr - lr * pi
            q1, q2 = p1[:SUBLANES, :], p2[:SUBLANES, :]
            for k in range(1, hb):
                q1 = q1 + p1[k * SUBLANES:(k + 1) * SUBLANES, :]
                q2 = q2 + p2[k * SUBLANES:(k + 1) * SUBLANES, :]
            q1s.append(q1)
            q2s.append(q2)
        carry[...] = lam_scr[0:1, :]
        lam = lam_scr[...]
        du_ref[...] = bdot(lam, wb_ref[...], 1, 1) + d_ref[...] * dyv
        upd = [(dwb_ref, bdot(uv, lam, 0, 0)), (dwc_ref, bdot(xv, dyv, 0, 0)),
               (dd_ref, jnp.sum(dyv * uv, 0, keepdims=True)),
               (q1_ref, jnp.concatenate(q1s, 1)), (q2_ref, jnp.concatenate(q2s, 1))]

        @pl.when(i == 0)
        def _():
            for ref, val in upd:
                ref[...] = val

        @pl.when(i != 0)
        def _():
            for ref, val in upd:
                ref[...] += val

    nb8 = s // SUBLANES
    rev = lambda w: pl.BlockSpec((t, w), lambda j, i: (nt - 1 - i, j))
    tab = pl.BlockSpec((t, S5_HALF), lambda j, i: (0, j))
    return pl.pallas_call(
        body, grid=(S5_NBLK, nt),
        in_specs=[rev(LANES), rev(LANES), rev(S5_BW),
                  pl.BlockSpec((SUBLANES, S5_BW), lambda j, i: (jnp.maximum((nt - 1 - i) * hb - 1, 0), j)),
                  pl.BlockSpec((None, LANES, S5_BW), lambda j, i: (j, 0, 0)),
                  pl.BlockSpec((None, S5_BW, LANES), lambda j, i: (j, 0, 0)),
                  tab, tab, tab, tab, pl.BlockSpec((1, LANES), lambda j, i: (0, j))],
        out_specs=[rev(LANES),
                   pl.BlockSpec((None, LANES, S5_BW), lambda j, i: (j, 0, 0)),
                   pl.BlockSpec((None, S5_BW, LANES), lambda j, i: (j, 0, 0)),
                   pl.BlockSpec((1, LANES), lambda j, i: (0, j)),
                   pl.BlockSpec((SUBLANES, S5_HALF), lambda j, i: (0, j)),
                   pl.BlockSpec((SUBLANES, S5_HALF), lambda j, i: (0, j))],
        out_shape=[jax.ShapeDtypeStruct((s, D_MODEL), f32),
                   jax.ShapeDtypeStruct((S5_NBLK, LANES, S5_BW), f32),
                   jax.ShapeDtypeStruct((S5_NBLK, S5_BW, LANES), f32),
                   jax.ShapeDtypeStruct((1, D_MODEL), f32),
                   jax.ShapeDtypeStruct((SUBLANES, S5_TABW), f32),
                   jax.ShapeDtypeStruct((SUBLANES, S5_TABW), f32)],
        scratch_shapes=[pltpu.VMEM((1, S5_BW), f32), pltpu.VMEM((t, S5_BW), f32)],
        compiler_params=_cparams("parallel", "arbitrary"),
        name="s5_core_bwd_" + tag)(u, dy, x, x, wb, wc, a1, a2, a1r, a2r, dskip)


def _gelu_tile(y):
    return (jax.nn.gelu(y),)


def s5_mixer_fwd(tag, u, prm, w_og):
    a_re, a_im, log_dt, b_re, b_im, c_re, c_im, dskip = prm
    disc_in = [a_re, a_im, log_dt.reshape(S5_GROUPS, 1), b_re.reshape(S5_GROUPS, -1), b_im.reshape(S5_GROUPS, -1)]
    abar_re, abar_im, bbar_re, bbar_im = rowmap("s5_disc_" + tag, _s5_disc_tile, disc_in, [_s5_expand()],
                                                [S5_STATE, S5_STATE, S5_STATE * S5_GROUP, S5_STATE * S5_GROUP], S5_GROUPS)
    del abar_re, abar_im
    a1, a2, a1r, a2r = s5_tables(a_re, a_im, log_dt)
    wb, wc = s5_pack_weights(bbar_re, bbar_im, c_re, c_im)
    wb, wc = wb.astype(bf16), wc.astype(bf16)
    y, x = s5_core_fwd(tag, u, wb, wc, a1, a2, dskip.reshape(1, D_MODEL))
    hid = rowmap("s5_gelu_" + tag, _gelu_tile, [y], [], [D_MODEL], TM_ROW, out_dtypes=[bf16])[0]
    og = mm_nn("s5_og_" + tag, hid, w_og)
    mix = rowmap("s5_glu_" + tag, _glu_tile, [og], [], [D_MODEL], TM_ROW)[0]
    return mix, (disc_in, a1, a2, a1r, a2r, wb, wc, x, y, hid, og)


def s5_mixer_bwd(tag, idx, u, prm, w_og, res, dmix, stacks):
    a_re, a_im, log_dt, b_re, b_im, c_re, c_im, dskip = prm
    disc_in, a1, a2, a1r, a2r, wb, wc, x, y, hid, og = res
    (dog,), _ = rowmap_bwd("s5_glu_bwd_" + tag, _glu_tile, [og], [], [dmix], TM_ROW)
    n_odd = DEPTH // 2
    dw_og = (mm_tn("s5_wo_dw_" + tag, hid, dog, n_cols=D_MODEL, stack=(stacks[0], idx, n_odd)),
             mm_tn("s5_wg_dw_" + tag, hid, dog, b_col0=D_MODEL, n_cols=D_MODEL, stack=(stacks[1], idx, n_odd)))
    dhid = mm_nt("s5_og_dx_" + tag, dog, w_og)
    (dy,), _ = rowmap_bwd("s5_gelu_bwd_" + tag, _gelu_tile, [y], [], [dhid], TM_ROW)
    du, dwb, dwc, ddskip, q1, q2 = s5_core_bwd(tag, u, x, wb, wc, a1, a2, a1r, a2r, dskip.reshape(1, D_MODEL), dy)
    dbbar_re, dbbar_im, dc_re, dc_im = s5_unpack_weight_grads(dwb, dwc)
    dabar_re = q1.sum(0).reshape(S5_GROUPS, S5_STATE)
    dabar_im = q2.sum(0).reshape(S5_GROUPS, S5_STATE)
    grads, _ = rowmap_bwd("s5_disc_bwd_" + tag, _s5_disc_tile, disc_in, [_s5_expand()],
                          [dabar_re, dabar_im, dbbar_re, dbbar_im], S5_GROUPS, par_mask=[False])
    da_re, da_im, dlog_dt, db_re, db_im = grads
    return du, (da_re, da_im, dlog_dt.reshape(S5_GROUPS), db_re.reshape(b_re.shape), db_im.reshape(b_im.shape),
                dc_re, dc_im, ddskip.reshape(D_MODEL)), dw_og


HYB_IN = 3592
_IN_B0, _IN_SW0 = 2048, 2056


IN_SHARD = HYB_IN // 4
SHARD_ORDER_GRADS = ("hyb_w_in", "ffn_wg", "ffn_wu", "ffn_wd")
FFN_TRANSPOSED = ("ffn_wg", "ffn_wu")
BIG_SHARDED = ("hyb_w_in", "hyb_w_out", "s5_glu_wo", "s5_glu_wg", "xq_w", "xk_w", "xv_w", "xo_w", "ffn_wg", "ffn_wu", "ffn_wd")


def _w_in_pieces():
    runs = [(0, _IN_B0, 0), (_IN_B0, _IN_SW0, COL_BA), (_IN_SW0, HYB_IN, _IN_B0)]
    out = []
    for sh in range(4):
        lo, hi = sh * IN_SHARD, (sh + 1) * IN_SHARD
        for r_lo, r_hi, c_lo in runs:
            a, b = max(lo, r_lo), min(hi, r_hi)
            if a < b:
                out.append((sh, a - lo, b - lo, c_lo + a - r_lo))
    return out


def w_in_to_canonical(tag, layer, w4):
    tr = 128

    def body(w_ref, o_ref):
        o_ref[:, COL_BA:] = jnp.zeros((tr, BA_PAD), o_ref.dtype)
        for sh, a, b, c in _w_in_pieces():
            o_ref[:, c:c + b - a] = w_ref[sh, :, a:b]

    return pl.pallas_call(
        body, grid=(D_MODEL // tr,),
        in_specs=[pl.BlockSpec((4, None, tr, IN_SHARD), lambda i: (0, layer, i, 0))],
        out_specs=pl.BlockSpec((tr, PROJ_COLS), lambda i: (i, 0)),
        out_shape=jax.ShapeDtypeStruct((D_MODEL, PROJ_COLS), w4.dtype),
        compiler_params=_cparams("parallel"), name="w_in_canon_" + tag)(w4)


def w_in_grad_to_shards(tag, layer, g, stack, n_layers):
    tr = 128
    extra, extra_specs, _ = _stacked(stack)

    def body(g_ref, *rest):
        o_ref = rest[-1]
        for sh, a, b, c in _w_in_pieces():
            o_ref[sh, :, a:b] = g_ref[:, c:c + b - a]

    return pl.pallas_call(
        body, grid=(D_MODEL // tr,),
        in_specs=[pl.BlockSpec((tr, PROJ_COLS), lambda i: (i, 0))] + extra_specs,
        out_specs=pl.BlockSpec((4, None, tr, IN_SHARD), lambda i: (0, layer, i, 0)),
        out_shape=jax.ShapeDtypeStruct((4, n_layers, D_MODEL, IN_SHARD), f32),
        input_output_aliases={1: 0} if extra else {},
        compiler_params=_cparams("parallel"), name="w_in_grad_shards_" + tag)(g, *extra)


def _add2(name, a, b):
    return rowmap(name, lambda p, q: (p + q,), [a, b], [], [a.shape[1]], _pick(a.shape[0], (256, 128, 64, 32, 16, 8)))[0]


def local_step(x, mem, positions, target, p):
    s = x.shape[0]
    cos, sin = rope_tables(positions, s)
    row = lambda v: v.reshape(1, -1).astype(f32)
    wg4, wu4, wd4 = (p[n].astype(bf16) for n in ("ffn_wg", "ffn_wu", "ffn_wd"))
    h = h16 = x
    tape = []
    for l in range(DEPTH):
        i, tag = l // 2, str(l)
        t = {"h0": h, "h0_16": h16}
        if l % 2 == 0:
            t["w_in"] = w_in_to_canonical(tag, i, p["hyb_w_in"].astype(bf16))
            t["w_out"] = p["hyb_w_out"][i].astype(bf16)
            t["dn_prm"] = (p["dn_conv_w"][i].astype(f32), row(jnp.repeat(p["dn_a_log"][i], DN_HEAD_DIM)),
                           row(jnp.repeat(p["dn_dt_bias"][i], DN_HEAD_DIM)), row(jnp.tile(p["dn_norm_g"][i], DN_HEADS)))
            t["proj"] = mm_nn("hyb_in_" + tag, h16, t["w_in"])
            a_out, t["dn"] = dn_mixer_fwd(tag, t["proj"], *t["dn_prm"])
            b_out, t["swa"] = swa_mixer_fwd(tag, t["proj"], cos, sin)
            t["mixed"] = jnp.concatenate([a_out, b_out], 1)
            mix = mm_nn("hyb_out_" + tag, t["mixed"], t["w_out"])
        else:
            t["s5_prm"] = tuple(p[n][i].astype(f32) for n in
                                ("s5_a_re", "s5_a_im", "s5_log_dt", "s5_b_re", "s5_b_im", "s5_c_re", "s5_c_im", "s5_d"))
            t["w_og"] = jnp.concatenate([p["s5_glu_wo"][i], p["s5_glu_wg"][i]], 1).astype(bf16)
            mix, t["s5"] = s5_mixer_fwd(tag, h, t["s5_prm"], t["w_og"])
        t["mix"] = mix
        t["ln"] = [(row(p[g][l]), row(p[b][l])) for g, b in
                   (("ln_mix_g", "ln_mix_b"), ("ln_x_g", "ln_x_b"), ("ln_ffn_g", "ln_ffn_b"))]
        t["h1"], t["h1_16"] = postnorm_fwd("mix" + tag, h, mix, *t["ln"][0])
        t["wq"], t["wo"] = p["xq_w"][l].astype(bf16), p["xo_w"][l].astype(bf16)
        t["wkv"] = jnp.concatenate([p["xk_w"][l], p["xv_w"][l]], 1).astype(bf16)
        t["xo"], t["xres"] = xattn_fwd(tag, t["h1_16"], mem, t["wq"], t["wkv"], t["wo"])
        t["h2"], t["h2_16"] = postnorm_fwd("x" + tag, t["h1"], t["xo"], *t["ln"][1])
        t["fo"], t["fres"] = ffn_fwd(tag, l, t["h2_16"], wg4, wu4, wd4)
        h, h16 = postnorm_fwd("ffn" + tag, t["h2"], t["fo"], *t["ln"][2])
        tape.append(t)

    part, dh = loss_head(h, target)
    loss = jnp.sum(part)

    g = {n: [None] * v.shape[0] for n, v in p.items() if n not in BIG_SHARDED}
    st = {n: None for n in BIG_SHARDED}
    for l in reversed(range(DEPTH)):
        i, tag, t = l // 2, str(l), tape[l]
        dh2a, dfo, dg, db = postnorm_bwd("ffn" + tag, t["h2"], t["fo"], *t["ln"][2], dh)
        g["ln_ffn_g"][l], g["ln_ffn_b"][l] = dg[0], db[0]
        ffn_names = ("ffn_wg", "ffn_wu", "ffn_wd")
        prev = None if st["ffn_wd"] is None else [st[n] for n in ffn_names]
        dh2b, new = ffn_bwd(tag, l, t["h2_16"], wg4, wu4, wd4, t["fres"], dfo, prev)
        st.update(zip(ffn_names, new))
        dh1a, dxo, dg, db = postnorm_bwd("x" + tag, t["h1"], t["xo"], *t["ln"][1], [dh2a, dh2b])
        g["ln_x_g"][l], g["ln_x_b"][l] = dg[0], db[0]
        x_names = ("xq_w", "xk_w", "xv_w", "xo_w")
        dh1b, new = xattn_bwd(tag, l, t["h1_16"], mem, t["wq"], t["wkv"], t["wo"], t["xres"], dxo, [st[n] for n in x_names])
        st.update(zip(x_names, new))
        dh0a, dmix, dg, db = postnorm_bwd("mix" + tag, t["h0"], t["mix"], *t["ln"][0], [dh1a, dh1b])
        g["ln_mix_g"][l], g["ln_mix_b"][l] = dg[0], db[0]
        if l % 2 == 0:
            st["hyb_w_out"] = mm_tn("hyb_out_dw_" + tag, t["mixed"], dmix, stack=(st["hyb_w_out"], i, DEPTH // 2))
            dmixed = mm_nt("hyb_out_dx_" + tag, dmix, t["w_out"])
            dqkv, dz, dba, dcw, dalog, ddtb, dng = dn_mixer_bwd(tag, t["proj"], *t["dn_prm"], t["dn"], (dmixed, DN_KEY_DIM, 0))
            g["dn_conv_w"][i] = dcw
            g["dn_a_log"][i] = dalog.reshape(DN_HEADS, DN_HEAD_DIM).sum(1)
            g["dn_dt_bias"][i] = ddtb.reshape(DN_HEADS, DN_HEAD_DIM).sum(1)
            g["dn_norm_g"][i] = dng.reshape(DN_HEADS, DN_HEAD_DIM).sum(0)
            dq, dk, dv = swa_mixer_bwd(tag, cos, sin, t["swa"], (dmixed, SW_DIM, 1))
            dproj = jnp.concatenate([dqkv, dz, dq, dk, dv, dba], 1)
            st["hyb_w_in"] = w_in_grad_to_shards(tag, i, mm_tn("hyb_in_dw_" + tag, t["h0_16"], dproj), st["hyb_w_in"], DEPTH // 2)
            dh0b = mm_nt("hyb_in_dx_" + tag, dproj, t["w_in"])
        else:
            dh0b, dprm, (st["s5_glu_wo"], st["s5_glu_wg"]) = s5_mixer_bwd(
                tag, i, t["h0"], t["s5_prm"], t["w_og"], t["s5"], dmix, (st["s5_glu_wo"], st["s5_glu_wg"]))
            for n, v in zip(("s5_a_re", "s5_a_im", "s5_log_dt", "s5_b_re", "s5_b_im", "s5_c_re", "s5_c_im", "s5_d"), dprm):
                g[n][i] = v
        dh = [dh0a, dh0b]
    grad_x = _add2("grad_x", dh[0], dh[1])
    grads = {n: jnp.stack(v) for n, v in g.items()}
    grads.update(st)
    return loss, grad_x, grads


WEIGHT_NAMES = ("hyb_w_in", "dn_conv_w", "dn_a_log", "dn_dt_bias", "dn_norm_g", "hyb_w_out", "s5_a_re", "s5_a_im",
                "s5_log_dt", "s5_b_re", "s5_b_im", "s5_c_re", "s5_c_im", "s5_d", "s5_glu_wo", "s5_glu_wg",
                "ln_mix_g", "ln_mix_b", "xq_w", "xk_w", "xv_w", "xo_w", "ln_x_g", "ln_x_b",
                "ffn_wg", "ffn_wu", "ffn_wd", "ln_ffn_g", "ln_ffn_b")
SHARD_AXIS = {"hyb_w_in": 2, "dn_conv_w": 2, "hyb_w_out": 1, "s5_d": 1, "s5_glu_wo": 1, "s5_glu_wg": 1,
              "xq_w": 1, "xk_w": 1, "xv_w": 1, "xo_w": 1, "ffn_wg": 2, "ffn_wu": 2, "ffn_wd": 1}
GATHER_F32 = ("dn_conv_w", "s5_d")
N_CHIPS = 4
PACK_COLS = 1024
_ANY = pl.BlockSpec(memory_space=pl.ANY)


def _pos():
    return lax.axis_index("x"), lax.axis_index("y"), lax.axis_index("c")


def _chip_peers(mx, my):
    return [(1 - mx, my), (mx, 1 - my), (1 - mx, 1 - my)]


def _rcopy(src, dst, ssem, rsem, dev):
    return pltpu.make_async_remote_copy(src_ref=src, dst_ref=dst, send_sem=ssem, recv_sem=rsem,
                                        device_id=dev, device_id_type=pl.DeviceIdType.MESH)


def comm_allgather4(name, x):
    def body(x_ref, o_ref, ssem, rsem, lsem):
        mx, my, mc = _pos()
        me = 2 * mx + my
        peers = _chip_peers(mx, my)
        loc = pltpu.make_async_copy(x_ref, o_ref.at[me], lsem)
        loc.start()
        sends = [_rcopy(x_ref, o_ref.at[me], ssem.at[k], rsem.at[k], (px, py, mc)) for k, (px, py) in enumerate(peers)]
        for cp in sends:
            cp.start()
        for k, (px, py) in enumerate(peers):
            _rcopy(x_ref, o_ref.at[2 * px + py], ssem.at[k], rsem.at[k], (px, py, mc)).wait_recv()
        for cp in sends:
            cp.wait_send()
        loc.wait()

    return pl.pallas_call(
        body, out_shape=jax.ShapeDtypeStruct((N_CHIPS,) + x.shape, x.dtype), in_specs=[_ANY], out_specs=_ANY,
        scratch_shapes=[pltpu.SemaphoreType.DMA((3,)), pltpu.SemaphoreType.DMA((3,)), pltpu.SemaphoreType.DMA],
        name=name)(x)


def _multi_call(name, body, ins, out_shapes, sems, in_place=False):
    return pl.pallas_call(
        body, out_shape=out_shapes, in_specs=[_ANY] * len(ins), out_specs=[_ANY] * len(out_shapes),
        scratch_shapes=sems, input_output_aliases={w: w for w in range(len(ins))} if in_place else {},
        name=name)(*ins)


def comm_gather_weights(name, slots):
    n = len(slots)

    def body(*refs):
        os_ = refs[n:2 * n]
        ssem, rsem, fssem, frsem = refs[2 * n:]
        mx, my, mc = _pos()
        me = 2 * mx + my
        peers = _chip_peers(mx, my)
        sib = (mx, my, 1 - mc)
        half = [o.shape[1] // 2 for o in os_]
        mine = [pl.ds(mc * h, h) for h in half]
        other = [pl.ds((1 - mc) * h, h) for h in half]
        sends = [_rcopy(os_[w].at[me, mine[w]], os_[w].at[me, mine[w]], ssem.at[w, k], rsem.at[w, k], (px, py, mc))
                 for w in range(n) for k, (px, py) in enumerate(peers)]
        for cp in sends:
            cp.start()
        fwds = []
        for w in range(n):
            for k, (px, py) in enumerate(peers):
                landed = os_[w].at[2 * px + py, mine[w]]
                _rcopy(landed, landed, ssem.at[w, k], rsem.at[w, k], (px, py, mc)).wait_recv()
                fw = _rcopy(landed, landed, fssem.at[w, k], frsem.at[w, k], sib)
                fw.start()
                fwds.append(fw)
        for w in range(n):
            for k, (px, py) in enumerate(peers):
                theirs = os_[w].at[2 * px + py, other[w]]
                _rcopy(theirs, theirs, fssem.at[w, k], frsem.at[w, k], sib).wait_recv()
        for cp in sends + fwds:
            cp.wait_send()

    dma = pltpu.SemaphoreType.DMA
    return _multi_call(name, body, slots, [jax.ShapeDtypeStruct(x.shape, x.dtype) for x in slots],
                       [dma((n, 3)), dma((n, 3)), dma((n, 3)), dma((n, 3))], in_place=True)


def comm_sibling_halves(name, gs):
    n = len(gs)

    def body(*refs):
        xs, os_ = refs[:n], refs[n:2 * n]
        ssem, rsem = refs[2 * n:]
        mx, my, mc = _pos()
        sib = (mx, my, 1 - mc)
        sends = []
        for w in range(n):
            h = xs[w].shape[1] // 2
            for j in range(N_CHIPS):
                sends.append(_rcopy(xs[w].at[j, pl.ds((1 - mc) * h, h)], os_[w].at[j], ssem.at[w, j], rsem.at[w, j], sib))
        for cp in sends:
            cp.start()
        for w in range(n):
            for j in range(N_CHIPS):
                _rcopy(os_[w].at[j], os_[w].at[j], ssem.at[w, j], rsem.at[w, j], sib).wait_recv()
        for cp in sends:
            cp.wait_send()

    dma = pltpu.SemaphoreType.DMA
    return _multi_call(name, body, gs,
                       [jax.ShapeDtypeStruct((N_CHIPS, g.shape[1] // 2) + g.shape[2:], g.dtype) for g in gs],
                       [dma((n, N_CHIPS)), dma((n, N_CHIPS))])


def comm_alltoall4(name, xs):
    n = len(xs)

    def body(*refs):
        xr, os_ = refs[:n], refs[n:2 * n]
        ssem, rsem = refs[2 * n:]
        mx, my, mc = _pos()
        me = 2 * mx + my
        peers = _chip_peers(mx, my)
        sends = [_rcopy(xr[w].at[2 * px + py], os_[w].at[me], ssem.at[w, k], rsem.at[w, k], (px, py, mc))
                 for w in range(n) for k, (px, py) in enumerate(peers)]
        for cp in sends:
            cp.start()
        for w in range(n):
            for k, (px, py) in enumerate(peers):
                dst = os_[w].at[2 * px + py]
                _rcopy(dst, dst, ssem.at[w, k], rsem.at[w, k], (px, py, mc)).wait_recv()
        for cp in sends:
            cp.wait_send()

    dma = pltpu.SemaphoreType.DMA
    return _multi_call(name, body, xs, [jax.ShapeDtypeStruct(x.shape, x.dtype) for x in xs], [dma((n, 3)), dma((n, 3))])


def comm_sibling_join(name, bs):
    n = len(bs)

    def body(*refs):
        os_ = refs[n:2 * n]
        ssem, rsem = refs[2 * n:]
        mx, my, mc = _pos()
        sib = (mx, my, 1 - mc)
        sends = [_rcopy(os_[w].at[mc], os_[w].at[mc], ssem.at[w], rsem.at[w], sib) for w in range(n)]
        for cp in sends:
            cp.start()
        for w in range(n):
            dst = os_[w].at[1 - mc]
            _rcopy(dst, dst, ssem.at[w], rsem.at[w], sib).wait_recv()
        for cp in sends:
            cp.wait_send()

    dma = pltpu.SemaphoreType.DMA
    return _multi_call(name, body, bs, [jax.ShapeDtypeStruct(b.shape, b.dtype) for b in bs], [dma((n,)), dma((n,))],
                       in_place=True)


def comm_sibling_swap(name, x):
    def body(x_ref, o_ref, ssem, rsem):
        mx, my, mc = _pos()
        cp = _rcopy(x_ref, o_ref, ssem, rsem, (mx, my, 1 - mc))
        cp.start()
        cp.wait_recv()
        cp.wait_send()

    return pl.pallas_call(
        body, out_shape=jax.ShapeDtypeStruct(x.shape, x.dtype), in_specs=[_ANY], out_specs=_ANY,
        scratch_shapes=[pltpu.SemaphoreType.DMA, pltpu.SemaphoreType.DMA], name=name)(x)


def _row_tile(r):
    return _pick(r, (256, 128, 64, 32, 16, 8))


def add_own_half(name, g, recv, out_dtype):
    r, c = g.shape[2:]
    tr = _row_tile(r)
    mc = lax.axis_index("c").astype(jnp.int32).reshape(1)

    def body(c_ref, g_ref, r_ref, o_ref):
        o_ref[...] = (g_ref[...] + r_ref[...]).astype(o_ref.dtype)

    grid_spec = pltpu.PrefetchScalarGridSpec(
        num_scalar_prefetch=1, grid=(N_CHIPS, r // tr),
        in_specs=[pl.BlockSpec((None, None, tr, c), lambda j, i, cr: (j, cr[0], i, 0)),
                  pl.BlockSpec((None, tr, c), lambda j, i, cr: (j, i, 0))],
        out_specs=pl.BlockSpec((None, tr, c), lambda j, i, cr: (j, i, 0)))
    return pl.pallas_call(body, grid_spec=grid_spec, out_shape=jax.ShapeDtypeStruct(recv.shape, out_dtype),
                          compiler_params=_cparams("parallel", "parallel"), name=name)(mc, g, recv)


def cast_into_slot(name, w, chip, dtype):
    r, c = w.shape
    tr = _row_tile(r)

    def body(c_ref, w_ref, o_ref):
        o_ref[...] = w_ref[...].astype(o_ref.dtype)

    grid_spec = pltpu.PrefetchScalarGridSpec(
        num_scalar_prefetch=1, grid=(r // tr,),
        in_specs=[pl.BlockSpec((tr, c), lambda i, cr: (i, 0))],
        out_specs=pl.BlockSpec((None, tr, c), lambda i, cr: (cr[0], i, 0)))
    return pl.pallas_call(body, grid_spec=grid_spec, out_shape=jax.ShapeDtypeStruct((N_CHIPS, r, c), dtype),
                          compiler_params=_cparams("parallel"), name=name)(chip.astype(jnp.int32).reshape(1), w)


def sum_chips_into_half(name, own, arrived, chip, mc):
    r, c = own.shape[1:]
    tr = _row_tile(r)

    def body(s0, s1, s2, s3, s4, own_ref, a_ref, b_ref, d_ref, o_ref):
        o_ref[...] = ((own_ref[...].astype(f32) + a_ref[...].astype(f32))
                      + (b_ref[...].astype(f32) + d_ref[...].astype(f32)))

    slot = lambda k: pl.BlockSpec((None, tr, c), lambda i, *sc, _k=k: (sc[_k][0], i, 0))
    grid_spec = pltpu.PrefetchScalarGridSpec(
        num_scalar_prefetch=5, grid=(r // tr,), in_specs=[slot(0), slot(1), slot(2), slot(3)],
        out_specs=pl.BlockSpec((None, tr, c), lambda i, *sc: (sc[4][0], i, 0)))
    mx, my = lax.axis_index("x"), lax.axis_index("y")
    scal = [v.astype(jnp.int32).reshape(1) for v in
            (2 * mx + my, 2 * (1 - mx) + my, 2 * mx + (1 - my), 2 * (1 - mx) + (1 - my), mc)]
    return pl.pallas_call(body, grid_spec=grid_spec, out_shape=jax.ShapeDtypeStruct((2, r, c), f32),
                          compiler_params=_cparams("parallel"), name=name)(*scal, own, arrived, arrived, arrived)


def sum_slots(name, x):
    r, c = x.shape[1:]
    tr = _row_tile(r)

    def body(x_ref, o_ref):
        o_ref[...] = (x_ref[0].astype(f32) + x_ref[1].astype(f32)) + (x_ref[2].astype(f32) + x_ref[3].astype(f32))

    return pl.pallas_call(
        body, grid=(r // tr,), in_specs=[pl.BlockSpec((N_CHIPS, tr, c), lambda i: (0, i, 0))],
        out_specs=pl.BlockSpec((tr, c), lambda i: (i, 0)), out_shape=jax.ShapeDtypeStruct((r, c), f32),
        compiler_params=_cparams("parallel"), name=name)(x)


def adamw(name, w, g, m, v):
    r, c = w.shape
    tr = _row_tile(r)

    def body(w_ref, g_ref, m_ref, v_ref, d_ref, nm_ref, nv_ref):
        gv = g_ref[...]
        nm = ADAM_B1 * m_ref[...] + (1.0 - ADAM_B1) * gv
        nv = ADAM_B2 * v_ref[...] + (1.0 - ADAM_B2) * (gv * gv)
        m_hat = nm / (1.0 - ADAM_B1 ** ADAM_STEP)
        v_hat = nv / (1.0 - ADAM_B2 ** ADAM_STEP)
        d_ref[...] = -ADAM_LR * (m_hat / (jnp.sqrt(v_hat) + ADAM_EPS) + ADAM_WD * w_ref[...])
        nm_ref[...] = nm
        nv_ref[...] = nv

    blk = pl.BlockSpec((tr, c), lambda i: (i, 0))
    return pl.pallas_call(
        body, grid=(r // tr,), in_specs=[blk] * 4, out_specs=[blk] * 3,
        out_shape=[jax.ShapeDtypeStruct((r, c), f32)] * 3,
        compiler_params=_cparams("parallel"), name=name)(w, g, m, v)


def _pack_rows(n):
    return -(-n // PACK_COLS)


def _pack(arrs, dtype, row_multiple):
    segs = []
    for a in arrs:
        flat = a.astype(dtype).reshape(-1)
        k = _pack_rows(flat.shape[0])
        segs.append(jnp.pad(flat, (0, k * PACK_COLS - flat.shape[0])).reshape(k, PACK_COLS))
    rows = sum(s.shape[0] for s in segs)
    pad = -rows % row_multiple
    if pad:
        segs.append(jnp.zeros((pad, PACK_COLS), dtype))
    return jnp.concatenate(segs, 0)


def _unpack(packed, shapes):
    out, r = [], 0
    for shp in shapes:
        n = math.prod(shp)
        k = _pack_rows(n)
        out.append(packed[r:r + k].reshape(-1)[:n].reshape(shp))
        r += k
    return out


def _gathered_to_full(g, axis):
    t = jnp.moveaxis(g, 0, axis)
    return t.reshape(t.shape[:axis] + (t.shape[axis] * t.shape[axis + 1],) + t.shape[axis + 2:])


def _full_to_shard_major(full, axis):
    shp = full.shape
    t = full.reshape(shp[:axis] + (N_CHIPS, shp[axis] // N_CHIPS) + shp[axis + 1:])
    return jnp.moveaxis(t, axis, 0)


def kernel(x, mem, positions, hyb_w_in, dn_conv_w, dn_a_log, dn_dt_bias, dn_norm_g, hyb_w_out, s5_a_re, s5_a_im, s5_log_dt, s5_b_re, s5_b_im, s5_c_re, s5_c_im, s5_d, s5_glu_wo, s5_glu_wg, ln_mix_g, ln_mix_b, xq_w, xk_w, xv_w, xo_w, ln_x_g, ln_x_b, ffn_wg, ffn_wu, ffn_wd, ln_ffn_g, ln_ffn_b, loss_target, m_hyb_w_in, m_dn_conv_w, m_dn_a_log, m_dn_dt_bias, m_dn_norm_g, m_hyb_w_out, m_s5_a_re, m_s5_a_im, m_s5_log_dt, m_s5_b_re, m_s5_b_im, m_s5_c_re, m_s5_c_im, m_s5_d, m_s5_glu_wo, m_s5_glu_wg, m_ln_mix_g, m_ln_mix_b, m_xq_w, m_xk_w, m_xv_w, m_xo_w, m_ln_x_g, m_ln_x_b, m_ffn_wg, m_ffn_wu, m_ffn_wd, m_ln_ffn_g, m_ln_ffn_b, v_hyb_w_in, v_dn_conv_w, v_dn_a_log, v_dn_dt_bias, v_dn_norm_g, v_hyb_w_out, v_s5_a_re, v_s5_a_im, v_s5_log_dt, v_s5_b_re, v_s5_b_im, v_s5_c_re, v_s5_c_im, v_s5_d, v_s5_glu_wo, v_s5_glu_wg, v_ln_mix_g, v_ln_mix_b, v_xq_w, v_xk_w, v_xv_w, v_xo_w, v_ln_x_g, v_ln_x_b, v_ffn_wg, v_ffn_wu, v_ffn_wd, v_ln_ffn_g, v_ln_ffn_b):
    a = dict(locals())
    big = [n for n in WEIGHT_NAMES if n in SHARD_AXIS and n not in GATHER_F32]
    small = [n for n in WEIGHT_NAMES if n not in big]
    chip = 2 * lax.axis_index("x") + lax.axis_index("y")
    for n in FFN_TRANSPOSED:
        for pre in ("", "m_", "v_"):
            a[pre + n] = jnp.swapaxes(a[pre + n], 1, 2)

    mc = lax.axis_index("c")
    view2 = lambda t: t.reshape(-1, t.shape[-1])
    slots = [cast_into_slot("slot_" + n, view2(a[n]), chip, bf16).reshape((N_CHIPS,) + a[n].shape) for n in big]
    gathered = comm_gather_weights("gather_w", slots)
    tiny4 = _unpack_slots(comm_allgather4("gather_w_tiny", _pack([a[n] for n in GATHER_F32], f32, 8)),
                          [a[n].shape for n in GATHER_F32])
    p = {n: a[n] for n in small if n not in GATHER_F32}
    for n, g4 in zip(GATHER_F32, tiny4):
        p[n] = _gathered_to_full(g4, SHARD_AXIS[n])
    for n, g4 in zip(big, gathered):
        p[n] = g4 if n in SHARD_ORDER_GRADS else _gathered_to_full(g4, SHARD_AXIS[n])

    loss, grad_x, grads = local_step(x[0], mem[0], positions, loss_target[0], p)
    loss = lax.psum(loss, ("x", "y", "c"))

    g4s = [grads[n] for n in big]
    recv = comm_sibling_halves("rs_sibling_halves", g4s)
    pairs = []
    for n, g4, r4 in zip(big, g4s, recv):
        lh, cols = g4.shape[1] // 2, g4.shape[-1]
        v4 = g4.reshape(N_CHIPS, 2, -1, cols)
        pairs.append(add_own_half("rs_add_" + n, v4, r4.reshape(N_CHIPS, -1, cols), bf16).reshape((N_CHIPS, lh) + g4.shape[2:]))
    arrived = comm_alltoall4("rs_alltoall", pairs)
    slot3 = lambda t: t.reshape(N_CHIPS, -1, t.shape[-1])
    halves = [sum_chips_into_half("rs_sum_" + n, slot3(pr), slot3(ar), chip, mc) for n, pr, ar in zip(big, pairs, arrived)]
    g_big = {n: t.reshape(a[n].shape) for n, t in zip(big, comm_sibling_join("rs_sibling_join", halves))}

    rpack = _pack([grads[n] for n in small], f32, 8)
    rpair = _add2("ar_add_sibling", rpack, comm_sibling_swap("ar_sibling_swap", rpack))
    g_small = _unpack(sum_slots("ar_sum_chips", comm_allgather4("ar_allgather", rpair)), [grads[n].shape for n in small])
    g_small = {n: (lax.dynamic_index_in_dim(_full_to_shard_major(g, SHARD_AXIS[n]), chip, 0, keepdims=False)
                   if n in SHARD_AXIS else g) for n, g in zip(small, g_small)}

    outs = {}
    for n in big:
        view = lambda t: t.reshape(-1, t.shape[-1])
        d, nm, nv = adamw("adamw_" + n, view(a[n]), view(g_big[n]), view(a["m_" + n]), view(a["v_" + n]))
        outs[n] = (g_big[n],) + tuple(t.reshape(a[n].shape) for t in (d, nm, nv))
    shapes = [a[n].shape for n in small]
    packs = [_pack([a[pre + n] for n in small], f32, 8) for pre in ("", "m_", "v_")]
    upd = adamw("adamw_small", packs[0], _pack([g_small[n] for n in small], f32, 8), packs[1], packs[2])
    for k, n in enumerate(small):
        outs[n] = (g_small[n],) + tuple(_unpack(buf, shapes)[k] for buf in upd)
    for n in FFN_TRANSPOSED:
        outs[n] = tuple(jnp.swapaxes(t, 1, 2) for t in outs[n])
    res = [loss, grad_x[None]]
    for kind in range(4):
        res += [outs[n][kind] for n in WEIGHT_NAMES]
    return tuple(res)


def _unpack_slots(gathered, shapes):
    out, r = [], 0
    for shp in shapes:
        n = math.prod(shp)
        k = _pack_rows(n)
        out.append(gathered[:, r:r + k].reshape(N_CHIPS, -1)[:, :n].reshape((N_CHIPS,) + tuple(shp)))
        r += k
    return out
```

```python
import functools
import math

import jax
import jax.numpy as jnp
from jax import lax
from jax.experimental import pallas as pl
from jax.experimental.pallas import tpu as pltpu

f32 = jnp.float32
bf16 = jnp.bfloat16

D_MODEL = 1024
DEPTH = 4
DN_HEADS = 4
DN_HEAD_DIM = 128
DN_KEY_DIM = 512
DN_QKV_DIM = 1536
DN_CONV = 4
SW_HEADS = 8
SW_HEAD_DIM = 64
SW_DIM = 512
SW_DILATIONS = (1, 4, 16)
SW_BLOCK = 128
ROPE_THETA = 10000.0
S5_GROUP = 16
S5_GROUPS = 64
S5_STATE = 64
X_HEADS = 4
X_HEAD_DIM = 256
FFN_HIDDEN = 2816
ALPHA = (2 * DEPTH) ** 0.25
LN_EPS = 1e-5
RMS_EPS = 1e-6
ADAM_LR, ADAM_B1, ADAM_B2, ADAM_EPS, ADAM_WD, ADAM_STEP = 0.001, 0.9, 0.999, 1e-08, 0.01, 10

BA_PAD = 256
PROJ_COLS = DN_QKV_DIM + DN_KEY_DIM + 3 * SW_DIM + BA_PAD
COL_Z = DN_QKV_DIM
COL_SWQ = COL_Z + DN_KEY_DIM
COL_SWK = COL_SWQ + SW_DIM
COL_SWV = COL_SWK + SW_DIM
COL_BA = COL_SWV + SW_DIM

LANES = 128
SUBLANES = 8
VMEM_LIMIT = 56 * 1024 * 1024
DN_CHUNK = 128
DN_HEADS_PER_STEP = 4


def _cparams(*sem):
    return pltpu.CompilerParams(dimension_semantics=tuple(sem), vmem_limit_bytes=VMEM_LIMIT)


def _dg(x, y, cx, cy):
    return lax.dot_general(x, y, (((cx,), (cy,)), ((), ())), preferred_element_type=f32)


@functools.partial(jax.custom_vjp, nondiff_argnums=(2, 3))
def bdot(a, b, ca, cb):
    return _dg(a.astype(bf16), b.astype(bf16), ca, cb)


def _bdot_fwd(a, b, ca, cb):
    return bdot(a, b, ca, cb), (a, b)


def _bdot_bwd(ca, cb, res, g):
    a, b = res
    g16, a16, b16 = g.astype(bf16), a.astype(bf16), b.astype(bf16)
    da = _dg(g16, b16, 1, 1 - cb) if ca == 1 else _dg(b16, g16, 1 - cb, 1)
    db = _dg(a16, g16, 1 - ca, 0) if cb == 0 else _dg(g16, a16, 0, 1 - ca)
    return da.astype(a.dtype), db.astype(b.dtype)


bdot.defvjp(_bdot_fwd, _bdot_bwd)


def _split_hi_lo(a):
    hi = a.astype(bf16)
    return hi, (a - hi.astype(f32)).astype(bf16)


def _dot3(a, b, ca, cb):
    a_hi, a_lo = _split_hi_lo(a)
    b_hi, b_lo = _split_hi_lo(b)
    return _dg(a_hi, b_hi, ca, cb) + (_dg(a_hi, b_lo, ca, cb) + _dg(a_lo, b_hi, ca, cb))


def hdot(a, b):
    return jnp.dot(a, b, precision=lax.Precision.HIGHEST, preferred_element_type=f32)


def _iota2(shape, dim):
    return lax.broadcasted_iota(jnp.int32, shape, dim)


def _row_spec(r, tm):
    if isinstance(r, tuple):
        arr, width, blk = r
        return arr, pl.BlockSpec((tm, width), lambda i, _b=blk: (i, _b))
    return r, pl.BlockSpec((tm, r.shape[1]), lambda i: (i, 0))


def _par_spec(p):
    return pl.BlockSpec(p.shape, lambda i, _n=p.ndim: (0,) * _n)


def rowmap(name, fn, rows, params, out_cols, tm, out_dtypes=None):
    arrs, specs = zip(*[_row_spec(r, tm) for r in rows])
    s = arrs[0].shape[0]
    n_in = len(rows) + len(params)
    out_dtypes = out_dtypes or [f32] * len(out_cols)

    def body(*refs):
        outs = fn(*[r[...] for r in refs[:n_in]])
        for o_ref, o in zip(refs[n_in:], outs):
            o_ref[...] = o.astype(o_ref.dtype)

    return pl.pallas_call(
        body, grid=(s // tm,),
        in_specs=list(specs) + [_par_spec(p) for p in params],
        out_specs=[pl.BlockSpec((tm, c), lambda i: (i, 0)) for c in out_cols],
        out_shape=[jax.ShapeDtypeStruct((s, c), dt) for c, dt in zip(out_cols, out_dtypes)],
        compiler_params=_cparams("parallel"), name=name)(*arrs, *params)


def rowmap_bwd(name, fn, rows, params, cts, tm, row_mask=None, par_mask=None, row_dtypes=None):
    arrs, specs = zip(*[_row_spec(r, tm) for r in rows])
    s = arrs[0].shape[0]
    ct_groups = [c if isinstance(c, list) else [c] for c in cts]
    ct_arrs, ct_specs = zip(*[_row_spec(a, tm) for grp in ct_groups for a in grp])
    cts = list(ct_arrs)
    nr, npar, nct = len(rows), len(params), len(cts)
    row_mask = row_mask or [True] * nr
    par_mask = par_mask or [True] * npar
    row_idx = [k for k in range(nr) if row_mask[k]]
    par_idx = [k for k in range(npar) if par_mask[k]]
    row_w = [specs[k].block_shape[1] for k in row_idx]

    def body(*refs):
        ins = [r[...] for r in refs[:nr + npar]]
        ct_refs = list(refs[nr + npar:nr + npar + nct])
        ctv = []
        for grp in ct_groups:
            acc = ct_refs.pop(0)[...].astype(f32)
            for _ in grp[1:]:
                acc = acc + ct_refs.pop(0)[...].astype(f32)
            ctv.append(acc)
        ctv = tuple(ctv)
        outs = refs[nr + npar + nct:]
        _, vjp = jax.vjp(fn, *ins)
        grads = vjp(ctv)
        for o_ref, k in zip(outs[:len(row_idx)], row_idx):
            o_ref[...] = grads[k].astype(o_ref.dtype)
        first = pl.program_id(0) == 0
        for o_ref, k in zip(outs[len(row_idx):], par_idx):
            g = grads[nr + k].astype(f32)

            @pl.when(first)
            def _(o_ref=o_ref, g=g):
                o_ref[...] = g

            @pl.when(jnp.logical_not(first))
            def _(o_ref=o_ref, g=g):
                o_ref[...] += g

    res = pl.pallas_call(
        body, grid=(s // tm,),
        in_specs=list(specs) + [_par_spec(p) for p in params]
        + list(ct_specs),
        out_specs=[pl.BlockSpec((tm, w), lambda i: (i, 0)) for w in row_w]
        + [_par_spec(params[k]) for k in par_idx],
        out_shape=[jax.ShapeDtypeStruct((s, w), dt) for w, dt in zip(row_w, row_dtypes or [f32] * len(row_w))]
        + [jax.ShapeDtypeStruct(params[k].shape, f32) for k in par_idx],
        compiler_params=_cparams("arbitrary"), name=name)(*arrs, *params, *cts)
    return list(res[:len(row_idx)]), list(res[len(row_idx):])


def _pick(n, prefs):
    for t in prefs:
        if n % t == 0:
            return t
    return n


MM_CHUNK = 512


def mm_nn(name, a, b, out_dtype=f32):
    m, k = a.shape
    n = b.shape[1]
    tm = _pick(m, (512, 256, 128))
    cn = _pick(n, (MM_CHUNK, 256, 128))

    def body(a_ref, b_ref, o_ref):
        av = a_ref[...].astype(bf16)
        for c in range(n // cn):
            sl = slice(c * cn, (c + 1) * cn)
            o_ref[:, sl] = _dg(av, b_ref[:, sl].astype(bf16), 1, 0).astype(o_ref.dtype)

    return pl.pallas_call(
        body, grid=(m // tm,),
        in_specs=[pl.BlockSpec((tm, k), lambda i: (i, 0)), pl.BlockSpec((k, n), lambda i: (0, 0))],
        out_specs=pl.BlockSpec((tm, n), lambda i: (i, 0)),
        out_shape=jax.ShapeDtypeStruct((m, n), out_dtype),
        compiler_params=_cparams("parallel"), name=name)(a, b)


def mm_nt(name, a, b, out_dtype=f32):
    m, n = a.shape
    k = b.shape[0]
    tm = _pick(m, (512, 256, 128))
    ck = _pick(k, (MM_CHUNK, 256, 128))

    def body(a_ref, b_ref, o_ref):
        av = a_ref[...].astype(bf16)
        for c in range(k // ck):
            sl = slice(c * ck, (c + 1) * ck)
            o_ref[:, sl] = _dg(av, b_ref[sl, :].astype(bf16), 1, 1).astype(o_ref.dtype)

    return pl.pallas_call(
        body, grid=(m // tm,),
        in_specs=[pl.BlockSpec((tm, n), lambda i: (i, 0)), pl.BlockSpec((k, n), lambda i: (0, 0))],
        out_specs=pl.BlockSpec((tm, k), lambda i: (i, 0)),
        out_shape=jax.ShapeDtypeStruct((m, k), out_dtype),
        compiler_params=_cparams("parallel"), name=name)(a, b)


def _stacked(buf):
    if buf is None:
        return [], [], {}
    return [buf], [pl.BlockSpec(memory_space=pl.ANY)], None


def mm_tn(name, a, b, out_dtype=f32, b_col0=0, n_cols=None, stack=None):
    s, m = a.shape
    n = n_cols or b.shape[1]
    tn = _pick(n, (256, 128))
    cm = _pick(m, (256, 128))
    col0 = b_col0 // tn
    in_specs = [pl.BlockSpec((s, m), lambda j: (0, 0)), pl.BlockSpec((s, tn), lambda j: (0, j + col0))]

    if stack is None:
        def body(a_ref, b_ref, o_ref):
            bv = b_ref[...].astype(bf16)
            for c in range(m // cm):
                sl = slice(c * cm, (c + 1) * cm)
                o_ref[sl, :] = _dg(a_ref[:, sl].astype(bf16), bv, 0, 0).astype(o_ref.dtype)

        return pl.pallas_call(
            body, grid=(n // tn,), in_specs=in_specs, out_specs=pl.BlockSpec((m, tn), lambda j: (0, j)),
            out_shape=jax.ShapeDtypeStruct((m, n), out_dtype),
            compiler_params=_cparams("parallel"), name=name)(a, b)

    buf, layer, n_layers = stack
    assert cm * N_CHIPS == m
    extra, extra_specs, _ = _stacked(buf)

    def body_stacked(a_ref, b_ref, *rest):
        o_ref = rest[-1]
        bv = b_ref[...].astype(bf16)
        for c in range(N_CHIPS):
            o_ref[c] = _dg(a_ref[:, c * cm:(c + 1) * cm].astype(bf16), bv, 0, 0).astype(o_ref.dtype)

    return pl.pallas_call(
        body_stacked, grid=(n // tn,), in_specs=in_specs + extra_specs,
        out_specs=pl.BlockSpec((N_CHIPS, None, cm, tn), lambda j: (0, layer, 0, j)),
        out_shape=jax.ShapeDtypeStruct((N_CHIPS, n_layers, cm, n), out_dtype),
        input_output_aliases={2: 0} if extra else {},
        compiler_params=_cparams("parallel"), name=name)(a, b, *extra)


def _postnorm_tile(h, sub, g, b):
    z = ALPHA * h + sub
    mu = jnp.mean(z, -1, keepdims=True)
    zc = z - mu
    var = jnp.mean(zc * zc, -1, keepdims=True)
    return (zc * lax.rsqrt(var + LN_EPS) * g + b,)


def _glu_tile(og):
    o, g = og[:, :D_MODEL], og[:, D_MODEL:]
    return (o * jax.nn.sigmoid(g),)


def _xattn_tile(q, kv):
    outs = []
    for h in range(X_HEADS):
        sl = slice(h * X_HEAD_DIM, (h + 1) * X_HEAD_DIM)
        s = bdot(q[:, sl], kv[:, sl], 1, 1) * (X_HEAD_DIM ** -0.5)
        m = lax.stop_gradient(jnp.max(s, -1, keepdims=True))
        p = jnp.exp(s - m)
        p = p / jnp.sum(p, -1, keepdims=True)
        outs.append(bdot(p, kv[:, D_MODEL + h * X_HEAD_DIM:D_MODEL + (h + 1) * X_HEAD_DIM], 1, 0))
    return (jnp.concatenate(outs, -1),)


TM_ROW = 512


def postnorm_fwd(tag, h, sub, g, b):
    return rowmap("postnorm_" + tag, lambda *a: _postnorm_tile(*a) * 2, [h, sub], [g, b], [D_MODEL] * 2, TM_ROW,
                  out_dtypes=[f32, bf16])


def postnorm_bwd(tag, h, sub, g, b, dy):
    (dh, dsub), (dg, db) = rowmap_bwd("postnorm_bwd_" + tag, _postnorm_tile, [h, sub], [g, b], [dy], TM_ROW,
                                      row_dtypes=[f32, bf16])
    return dh, dsub, dg, db


def xattn_fwd(tag, h, mem, wq, wkv, wo):
    q = mm_nn("xq_" + tag, h, wq, out_dtype=bf16)
    kv = mm_nn("xkv_" + tag, mem, wkv)
    ao = rowmap("xattn_" + tag, _xattn_tile, [q], [kv], [D_MODEL], TM_ROW, out_dtypes=[bf16])[0]
    out = mm_nn("xo_" + tag, ao, wo)
    return out, (q, kv, ao)


def xattn_bwd(tag, layer, h, mem, wq, wkv, wo, res, dout, stacks):
    q, kv, ao = res
    sq, sk, sv, so = stacks
    so = mm_tn("xo_dw_" + tag, ao, dout, stack=(so, layer, DEPTH))
    dao = mm_nt("xo_dx_" + tag, dout, wo)
    (dq,), (dkv,) = rowmap_bwd("xattn_bwd_" + tag, _xattn_tile, [q], [kv], [dao], TM_ROW, row_dtypes=[bf16])
    sq = mm_tn("xq_dw_" + tag, h, dq, stack=(sq, layer, DEPTH))
    dh = mm_nt("xq_dx_" + tag, dq, wq)
    sk = mm_tn("xk_dw_" + tag, mem, dkv, n_cols=D_MODEL, stack=(sk, layer, DEPTH))
    sv = mm_tn("xv_dw_" + tag, mem, dkv, b_col0=D_MODEL, n_cols=D_MODEL, stack=(sv, layer, DEPTH))
    return dh, (sq, sk, sv, so)


FFN_SHARD = FFN_HIDDEN // 4
TM_FFN = 512


def _silu_mul(a, u):
    return jax.nn.silu(a) * u


def ffn_fwd(tag, layer, h, wg, wu, wd):
    s = h.shape[0]
    tm, fs = TM_FFN, FFN_SHARD
    w_in = pl.BlockSpec((None, None, fs, D_MODEL), lambda k, i: (k, layer, 0, 0))
    act = pl.BlockSpec((None, tm, fs), lambda k, i: (k, i, 0))

    def up_body(h_ref, wg_ref, wu_ref, a_ref, u_ref, hid_ref):
        hv = h_ref[...].astype(bf16)
        a, u = _dg(hv, wg_ref[...], 1, 1), _dg(hv, wu_ref[...], 1, 1)
        a_ref[...], u_ref[...] = a.astype(bf16), u.astype(bf16)
        hid_ref[...] = _silu_mul(a, u).astype(bf16)

    a4, u4, hid4 = pl.pallas_call(
        up_body, grid=(4, s // tm),
        in_specs=[pl.BlockSpec((tm, D_MODEL), lambda k, i: (i, 0)), w_in, w_in],
        out_specs=[act, act, act],
        out_shape=[jax.ShapeDtypeStruct((4, s, fs), bf16)] * 3,
        compiler_params=_cparams("parallel", "parallel"), name="ffn_up_" + tag)(h, wg, wu)

    all_act = pl.BlockSpec((4, tm, fs), lambda i: (0, i, 0))
    all_w = pl.BlockSpec((4, None, fs, D_MODEL), lambda i: (0, layer, 0, 0))

    def down_body(hid_ref, wd_ref, o_ref):
        acc = _dg(hid_ref[0], wd_ref[0], 1, 0)
        for k in range(1, 4):
            acc = acc + _dg(hid_ref[k], wd_ref[k], 1, 0)
        o_ref[...] = acc

    out = pl.pallas_call(
        down_body, grid=(s // tm,), in_specs=[all_act, all_w],
        out_specs=pl.BlockSpec((tm, D_MODEL), lambda i: (i, 0)),
        out_shape=jax.ShapeDtypeStruct((s, D_MODEL), f32),
        compiler_params=_cparams("parallel"), name="ffn_down_" + tag)(hid4, wd)
    return out, (a4, u4, hid4)


def ffn_bwd(tag, layer, h, wg, wu, wd, res, dout, stacks=None):
    a4, u4, hid4 = res
    s = h.shape[0]
    tm, fs = TM_FFN, FFN_SHARD
    act = pl.BlockSpec((None, tm, fs), lambda k, i: (k, i, 0))

    def dact_body(do_ref, wd_ref, a_ref, u_ref, da_ref, du_ref):
        dhid = _dg(do_ref[...].astype(bf16), wd_ref[...], 1, 1)
        _, vjp = jax.vjp(_silu_mul, a_ref[...].astype(f32), u_ref[...].astype(f32))
        da, du = vjp(dhid)
        da_ref[...], du_ref[...] = da.astype(bf16), du.astype(bf16)

    da4, du4 = pl.pallas_call(
        dact_body, grid=(4, s // tm),
        in_specs=[pl.BlockSpec((tm, D_MODEL), lambda k, i: (i, 0)),
                  pl.BlockSpec((None, None, fs, D_MODEL), lambda k, i: (k, layer, 0, 0)), act, act],
        out_specs=[act, act], out_shape=[jax.ShapeDtypeStruct((4, s, fs), bf16)] * 2,
        compiler_params=_cparams("parallel", "parallel"), name="ffn_dact_" + tag)(dout, wd, a4, u4)

    all_act = pl.BlockSpec((4, tm, fs), lambda i: (0, i, 0))
    all_w = pl.BlockSpec((4, None, fs, D_MODEL), lambda i: (0, layer, 0, 0))

    def dx_body(da_ref, du_ref, wg_ref, wu_ref, o_ref):
        acc = _dg(da_ref[0], wg_ref[0], 1, 0) + _dg(du_ref[0], wu_ref[0], 1, 0)
        for k in range(1, 4):
            acc = acc + (_dg(da_ref[k], wg_ref[k], 1, 0) + _dg(du_ref[k], wu_ref[k], 1, 0))
        o_ref[...] = acc

    dh = pl.pallas_call(
        dx_body, grid=(s // tm,), in_specs=[all_act, all_act, all_w, all_w],
        out_specs=pl.BlockSpec((tm, D_MODEL), lambda i: (i, 0)),
        out_shape=jax.ShapeDtypeStruct((s, D_MODEL), f32),
        compiler_params=_cparams("parallel"), name="ffn_dx_" + tag)(da4, du4, wg, wu)

    tn = 256
    whole = pl.BlockSpec((None, s, fs), lambda k: (k, 0, 0))
    resident = pl.BlockSpec((s, D_MODEL), lambda k: (0, 0), pipeline_mode=pl.Buffered(1))

    def dwin_body(h_ref, da_ref, du_ref, *rest):
        dwg_ref, dwu_ref = rest[-2:]
        da, du = da_ref[...], du_ref[...]
        for c in range(D_MODEL // tn):
            sl = slice(c * tn, (c + 1) * tn)
            hv = h_ref[:, sl].astype(bf16)
            dwg_ref[:, sl] = _dg(da, hv, 0, 0)
            dwu_ref[:, sl] = _dg(du, hv, 0, 0)

    n_layers = wd.shape[1]
    layer_out = pl.BlockSpec((None, None, fs, D_MODEL), lambda k: (k, layer, 0, 0))
    stack_shape = jax.ShapeDtypeStruct((4, n_layers, fs, D_MODEL), f32)
    prev = [] if stacks is None else list(stacks)
    any_spec = [pl.BlockSpec(memory_space=pl.ANY)]

    dwg, dwu = pl.pallas_call(
        dwin_body, grid=(4,),
        in_specs=[resident, whole, whole] + any_spec * len(prev[:2]),
        out_specs=[layer_out] * 2, out_shape=[stack_shape] * 2,
        input_output_aliases={3: 0, 4: 1} if prev else {},
        compiler_params=_cparams("parallel"), name="ffn_dwin_" + tag)(h, da4, du4, *prev[:2])

    def dwd_body(hid_ref, do_ref, *rest):
        hid = hid_ref[...]
        for c in range(D_MODEL // tn):
            sl = slice(c * tn, (c + 1) * tn)
            rest[-1][:, sl] = _dg(hid, do_ref[:, sl].astype(bf16), 0, 0)

    dwd = pl.pallas_call(
        dwd_body, grid=(4,),
        in_specs=[whole, resident] + any_spec * len(prev[2:]),
        out_specs=layer_out, out_shape=stack_shape,
        input_output_aliases={2: 0} if prev else {},
        compiler_params=_cparams("parallel"), name="ffn_dwd_" + tag)(hid4, dout, *prev[2:])
    return dh, (dwg, dwu, dwd)


def loss_head(y, target):
    s, d = y.shape
    tm = TM_ROW

    def body(y_ref, t_ref, part_ref, dy_ref):
        e = y_ref[...] - t_ref[...]
        dy_ref[...] = e * (1.0 / d)
        p = jnp.sum(e * e, 0, keepdims=True) * (0.5 / d)

        @pl.when(pl.program_id(0) == 0)
        def _():
            part_ref[...] = p

        @pl.when(pl.program_id(0) != 0)
        def _():
            part_ref[...] += p

    return pl.pallas_call(
        body, grid=(s // tm,),
        in_specs=[pl.BlockSpec((tm, d), lambda i: (i, 0))] * 2,
        out_specs=[pl.BlockSpec((1, d), lambda i: (0, 0)), pl.BlockSpec((tm, d), lambda i: (i, 0))],
        out_shape=[jax.ShapeDtypeStruct((1, d), f32), jax.ShapeDtypeStruct((s, d), f32)],
        compiler_params=_cparams("arbitrary"), name="loss_head")(y, target)


TM_CONV = 512


def _conv_rows(xx, w_ref, n_rows):
    a = w_ref[3:4, :] * xx
    for k in (1, 2, 3):
        a = a + w_ref[3 - k:4 - k, :] * pltpu.roll(xx, k, 0)
    return a


def _dn_act(a, is_qk):
    s = jax.nn.silu(a)
    n = s * lax.rsqrt(jnp.sum(s * s, -1, keepdims=True) + RMS_EPS)
    return jnp.where(is_qk, n, s)


def dn_conv_fwd(tag, proj, cw):
    s = proj.shape[0]
    tm, hb = TM_CONV, TM_CONV // SUBLANES

    def body(xh_ref, x_ref, w_ref, o_ref):
        j, t = pl.program_id(0), pl.program_id(1)
        halo = jnp.where(t > 0, xh_ref[...], 0.0)
        xx = jnp.concatenate([halo, x_ref[...]], 0)
        a = _conv_rows(xx, w_ref, tm + SUBLANES)
        o_ref[...] = _dn_act(a, j < 2 * DN_HEADS)[SUBLANES:, :]

    return pl.pallas_call(
        body, grid=(DN_QKV_DIM // LANES, s // tm),
        in_specs=[pl.BlockSpec((SUBLANES, LANES), lambda j, t: (jnp.maximum(t * hb - 1, 0), j)),
                  pl.BlockSpec((tm, LANES), lambda j, t: (t, j)),
                  pl.BlockSpec((DN_CONV, LANES), lambda j, t: (0, j))],
        out_specs=pl.BlockSpec((tm, LANES), lambda j, t: (t, j)),
        out_shape=jax.ShapeDtypeStruct((s, DN_QKV_DIM), f32),
        compiler_params=_cparams("parallel", "parallel"), name="dn_conv_" + tag)(proj, proj, cw)


def dn_conv_bwd(tag, proj, cw, dy):
    s = proj.shape[0]
    tm, hb = TM_CONV, TM_CONV // SUBLANES
    nt = s // tm
    n_ext = tm + 2 * SUBLANES

    def body(xb_ref, x_ref, xa_ref, dy_ref, dya_ref, w_ref, dx_ref, dw_ref):
        j, t = pl.program_id(0), pl.program_id(1)
        xx = jnp.concatenate([jnp.where(t > 0, xb_ref[...], 0.0), x_ref[...],
                              jnp.where(t < nt - 1, xa_ref[...], 0.0)], 0)
        dyy = jnp.concatenate([jnp.zeros((SUBLANES, LANES), f32), dy_ref[...],
                               jnp.where(t < nt - 1, dya_ref[...], 0.0)], 0)
        a = _conv_rows(xx, w_ref, n_ext)
        _, vjp = jax.vjp(lambda v: _dn_act(v, j < 2 * DN_HEADS), a)
        da, = vjp(dyy)
        dx = w_ref[3:4, :] * da
        for k in (1, 2, 3):
            dx = dx + w_ref[3 - k:4 - k, :] * pltpu.roll(da, n_ext - k, 0)
        dx_ref[...] = dx[SUBLANES:SUBLANES + tm, :]
        row = _iota2((n_ext, LANES), 0)
        da_in = jnp.where((row >= SUBLANES) & (row < SUBLANES + tm), da, 0.0)
        r8 = _iota2((SUBLANES, LANES), 0)
        dw = jnp.zeros((SUBLANES, LANES), f32)
        for k in range(DN_CONV):
            xs = xx if k == 0 else pltpu.roll(xx, k, 0)
            dw = dw + jnp.where(r8 == 3 - k, jnp.sum(da_in * xs, 0, keepdims=True), 0.0)

        @pl.when(t == 0)
        def _():
            dw_ref[...] = dw

        @pl.when(t != 0)
        def _():
            dw_ref[...] += dw

    nb8 = s // SUBLANES
    return pl.pallas_call(
        body, grid=(DN_QKV_DIM // LANES, nt),
        in_specs=[pl.BlockSpec((SUBLANES, LANES), lambda j, t: (jnp.maximum(t * hb - 1, 0), j)),
                  pl.BlockSpec((tm, LANES), lambda j, t: (t, j)),
                  pl.BlockSpec((SUBLANES, LANES), lambda j, t: (jnp.minimum((t + 1) * hb, nb8 - 1), j)),
                  pl.BlockSpec((tm, LANES), lambda j, t: (t, j)),
                  pl.BlockSpec((SUBLANES, LANES), lambda j, t: (jnp.minimum((t + 1) * hb, nb8 - 1), j)),
                  pl.BlockSpec((DN_CONV, LANES), lambda j, t: (0, j))],
        out_specs=[pl.BlockSpec((tm, LANES), lambda j, t: (t, j)),
                   pl.BlockSpec((SUBLANES, LANES), lambda j, t: (0, j))],
        out_shape=[jax.ShapeDtypeStruct((s, DN_QKV_DIM), f32), jax.ShapeDtypeStruct((SUBLANES, DN_QKV_DIM), f32)],
        compiler_params=_cparams("parallel", "arbitrary"), name="dn_conv_bwd_" + tag)(proj, proj, proj, dy, dy, cw)


def _gate_tile(ba, eb, ea, alog, dtb):
    beta = jax.nn.sigmoid(hdot(ba, eb))
    g = -jnp.exp(alog) * jax.nn.softplus(hdot(ba, ea) + dtb)
    return beta, g


def _each(fn, *lists):
    return [fn(*args) for args in zip(*lists)]


@functools.partial(jax.custom_vjp, nondiff_argnums=(1,))
def _halves(x, axis):
    h = x.shape[axis] // 2
    return (x[:h], x[h:]) if axis == 0 else (x[:, :h], x[:, h:])


def _halves_fwd(x, axis):
    return _halves(x, axis), None


def _halves_bwd(axis, _, g):
    return (jnp.concatenate(g, axis),)


_halves.defvjp(_halves_fwd, _halves_bwd)


def _tri_inv_unit(lowers):
    c = lowers[0].shape[0]
    r, col = _iota2((c, c), 0), _iota2((c, c), 1)
    eye = jnp.where(r == col, 1.0, 0.0).astype(f32)
    invs = None
    sh = 0
    while (1 << sh) < c:
        same_2b = lax.shift_right_logical(r, sh + 1) == lax.shift_right_logical(col, sh + 1)
        diff_b = lax.shift_right_logical(r, sh) != lax.shift_right_logical(col, sh)
        offs = [jnp.where(same_2b & diff_b, low, 0.0) for low in lowers]
        if invs is None:
            invs = [eye - off for off in offs]
        else:
            part = _each(lambda inv, off: _dot3(inv, off, 1, 0), invs, offs)
            invs = _each(lambda inv, p: inv - _dot3(p, inv, 1, 0), invs, part)
        sh += 1
    return invs


@jax.custom_vjp
def _known_inverse(lower, tinv):
    return tinv


def _known_inverse_fwd(lower, tinv):
    return tinv, tinv


def _known_inverse_bwd(tinv, g):
    tt = tinv.T
    return -hdot(hdot(tt, g), tt), jnp.zeros_like(tinv)


_known_inverse.defvjp(_known_inverse_fwd, _known_inverse_bwd)


def _delta_chunk(q, k, v, gb, betab, state, tinv_known=None):
    c, hd = DN_CHUNK, DN_HEAD_DIM
    r, col = _iota2((c, c), 0), _iota2((c, c), 1)
    causal, strict = r >= col, r > col
    tril = jnp.where(causal, 1.0, 0.0).astype(f32)
    gc = _each(lambda g: hdot(tril, g), gb)
    decay = _each(lambda g: jnp.where(causal, jnp.exp(jnp.where(causal, g - g.T, 0.0)), 0.0), gc)
    qs = _each(lambda t: t * (DN_HEAD_DIM ** -0.5), q)
    kb = _each(lambda a, b: a * b, k, betab)
    kq = _each(lambda a, b, kk: _halves(bdot(jnp.concatenate([a, b], 0), kk, 1, 1), 0), kb, qs, k)
    lower = _each(lambda x, d: jnp.where(strict, x[0], 0.0) * d, kq, decay)
    intra = _each(lambda x, d: x[1] * d, kq, decay)
    tinv = _tri_inv_unit(lower) if tinv_known is None else _each(_known_inverse, lower, tinv_known)
    eg = _each(jnp.exp, gc)
    uw = _each(lambda t, vv, b, kb_, e: _halves(hdot(t, jnp.concatenate([vv * b, kb_ * e], 1)), 1),
               tinv, v, betab, kb, eg)
    gl = _each(lambda g: jnp.sum(jnp.where(r == c - 1, g, 0.0), 0, keepdims=True), gc)
    k_dec = _each(lambda kk, a, g: kk * jnp.exp(a - g), k, gl, gc)
    ws = _each(lambda x, t, e, st: _halves(bdot(jnp.concatenate([x[1], t * e], 0), st, 1, 0), 0), uw, qs, eg, state)
    v_new = _each(lambda x, y: x[0] - y[0], uw, ws)
    out = _each(lambda y, a, vn: y[1] + bdot(a, vn, 1, 0), ws, intra, v_new)
    new_state = _each(lambda st, a, kd, vn: st * jnp.exp(a) + bdot(kd, vn, 0, 0), state, gl, k_dec, v_new)
    return tuple(out), tuple(new_state), tuple(tinv)


def delta_fwd(tag, qkv, gb, betab):
    s = qkv.shape[0]
    c, hd = DN_CHUNK, DN_HEAD_DIM
    n = s // c

    hg, ng = DN_HEADS_PER_STEP, DN_HEADS // DN_HEADS_PER_STEP

    def body(q_ref, k_ref, v_ref, g_ref, b_ref, o_ref, st_ref, ti_ref, state):
        @pl.when(pl.program_id(1) == 0)
        def _():
            state[...] = jnp.zeros_like(state)

        heads = lambda ref: tuple(ref[:, j * hd:(j + 1) * hd] for j in range(hg))
        st = tuple(state[j] for j in range(hg))
        outs, news, tinv = _delta_chunk(heads(q_ref), heads(k_ref), heads(v_ref), heads(g_ref), heads(b_ref), st)
        for j in range(hg):
            st_ref[j] = st[j]
            ti_ref[j] = tinv[j]
            o_ref[:, j * hd:(j + 1) * hd] = outs[j]
            state[j] = news[j]

    blk = lambda off: pl.BlockSpec((c, hg * hd), lambda h, i, _o=off: (i, h + _o))
    per_chunk = pl.BlockSpec((hg, None, hd, hd), lambda h, i: (h, i, 0, 0))
    return pl.pallas_call(
        body, grid=(ng, n),
        in_specs=[blk(0), blk(ng), blk(2 * ng), blk(0), blk(0)],
        out_specs=[blk(0), per_chunk, per_chunk],
        out_shape=[jax.ShapeDtypeStruct((s, DN_KEY_DIM), f32)] + [jax.ShapeDtypeStruct((DN_HEADS, n, hd, hd), f32)] * 2,
        scratch_shapes=[pltpu.VMEM((hg, hd, hd), f32)],
        compiler_params=_cparams("parallel", "arbitrary"), name="delta_" + tag)(qkv, qkv, qkv, gb, betab)


def delta_bwd(tag, qkv, gb, betab, states, tinvs, do):
    s = qkv.shape[0]
    c, hd = DN_CHUNK, DN_HEAD_DIM
    n = s // c

    hg, ng = DN_HEADS_PER_STEP, DN_HEADS // DN_HEADS_PER_STEP

    def body(q_ref, k_ref, v_ref, g_ref, b_ref, st_ref, ti_ref, do_ref, dqkv_ref, dg_ref, db_ref, dstate):
        @pl.when(pl.program_id(1) == 0)
        def _():
            dstate[...] = jnp.zeros_like(dstate)

        heads = lambda ref: tuple(ref[:, j * hd:(j + 1) * hd] for j in range(hg))
        tinv = tuple(ti_ref[j] for j in range(hg))
        _, vjp = jax.vjp(lambda *args: _delta_chunk(*args, tinv_known=tinv)[:2],
                         heads(q_ref), heads(k_ref), heads(v_ref), heads(g_ref), heads(b_ref),
                         tuple(st_ref[j] for j in range(hg)))
        grads = vjp((heads(do_ref), tuple(dstate[j] for j in range(hg))))
        for part, g in enumerate(grads[:3]):
            for j in range(hg):
                dqkv_ref[:, part * DN_KEY_DIM + j * hd:part * DN_KEY_DIM + (j + 1) * hd] = g[j]
        for ref, g in zip((dg_ref, db_ref), grads[3:5]):
            for j in range(hg):
                ref[:, j * hd:(j + 1) * hd] = g[j]
        for j in range(hg):
            dstate[j] = grads[5][j]

    assert ng == 1
    blk = lambda off: pl.BlockSpec((c, hg * hd), lambda h, i, _o=off: (n - 1 - i, h + _o))
    return pl.pallas_call(
        body, grid=(ng, n),
        in_specs=[blk(0), blk(ng), blk(2 * ng), blk(0), blk(0)]
        + [pl.BlockSpec((hg, None, hd, hd), lambda h, i: (h, n - 1 - i, 0, 0))] * 2 + [blk(0)],
        out_specs=[pl.BlockSpec((c, DN_QKV_DIM), lambda h, i: (n - 1 - i, 0)), blk(0), blk(0)],
        out_shape=[jax.ShapeDtypeStruct((s, DN_QKV_DIM), f32)] + [jax.ShapeDtypeStruct((s, DN_KEY_DIM), f32)] * 2,
        scratch_shapes=[pltpu.VMEM((hg, hd, hd), f32)],
        compiler_params=_cparams("parallel", "arbitrary"),
        name="delta_bwd_" + tag)(qkv, qkv, qkv, gb, betab, states, tinvs, do)


def _dn_out_tile(o, z, ng):
    outs = []
    for h in range(DN_HEADS):
        sl = slice(h * DN_HEAD_DIM, (h + 1) * DN_HEAD_DIM)
        oh = o[:, sl]
        nrm = oh * lax.rsqrt(jnp.mean(oh * oh, -1, keepdims=True) + RMS_EPS) * ng[:, sl]
        outs.append(nrm * jax.nn.silu(z[:, sl]))
    return (jnp.concatenate(outs, -1),)


def _head_selectors():
    r, c = _iota2((BA_PAD, DN_KEY_DIM), 0), _iota2((BA_PAD, DN_KEY_DIM), 1) // DN_HEAD_DIM
    return (r == c).astype(f32), (r == c + DN_HEADS).astype(f32)


def dn_mixer_fwd(tag, proj, cw, alog_b, dtb_b, ng_b):
    eb, ea = _head_selectors()
    ba = (proj, BA_PAD, COL_BA // BA_PAD)
    qkv = dn_conv_fwd(tag, proj, cw)
    betab, gb = rowmap("dn_gate_" + tag, _gate_tile, [ba], [eb, ea, alog_b, dtb_b], [DN_KEY_DIM] * 2, TM_ROW)
    o, states, tinvs = delta_fwd(tag, qkv, gb, betab)
    z = (proj, DN_KEY_DIM, COL_Z // DN_KEY_DIM)
    a_out = rowmap("dn_out_" + tag, _dn_out_tile, [o, z], [ng_b], [DN_KEY_DIM], TM_ROW)[0]
    return a_out, (qkv, betab, gb, o, states, tinvs)


def dn_mixer_bwd(tag, proj, cw, alog_b, dtb_b, ng_b, res, da_out):
    qkv, betab, gb, o, states, tinvs = res
    eb, ea = _head_selectors()
    ba = (proj, BA_PAD, COL_BA // BA_PAD)
    z = (proj, DN_KEY_DIM, COL_Z // DN_KEY_DIM)
    (do, dz), (dng,) = rowmap_bwd("dn_out_bwd_" + tag, _dn_out_tile, [o, z], [ng_b], [da_out], TM_ROW)
    dqkv, dgb, dbetab = delta_bwd(tag, qkv, gb, betab, states, tinvs, do)
    dqkv_raw, dcw = dn_conv_bwd(tag, proj, cw, dqkv)
    (dba,), (dalog, ddtb) = rowmap_bwd("dn_gate_bwd_" + tag, _gate_tile, [ba], [eb, ea, alog_b, dtb_b],
                                       [dbetab, dgb], TM_ROW, par_mask=[False, False, True, True])
    return dqkv_raw, dz, dba, dcw[:DN_CONV], dalog, ddtb, dng


def _swap_halves(x):
    n = x.shape[1]
    first = (_iota2((1, n), 1) % SW_HEAD_DIM) < SW_HEAD_DIM // 2
    return jnp.where(first, pltpu.roll(x, n - SW_HEAD_DIM // 2, 1), pltpu.roll(x, SW_HEAD_DIM // 2, 1))


def _rope_apply(x, cos, sin_signed):
    return x * cos + _swap_halves(x) * sin_signed


def _rope_transpose(dy, cos, sin_signed):
    return dy * cos + _swap_halves(dy * sin_signed)


def rope_tables(positions, s):
    half = SW_HEAD_DIM // 2
    inv_freq = ROPE_THETA ** (-jnp.arange(0, SW_HEAD_DIM, 2, dtype=f32) / SW_HEAD_DIM)
    ang = positions.reshape(s, 1).astype(f32) * inv_freq[None, :]
    cos, sin = jnp.cos(ang), jnp.sin(ang)
    cos_t = jnp.tile(jnp.concatenate([cos, cos], 1), (1, SW_HEADS))
    sin_t = jnp.tile(jnp.concatenate([-sin, sin], 1), (1, SW_HEADS))
    assert cos_t.shape == (s, SW_DIM) and half * 2 == SW_HEAD_DIM
    return cos_t, sin_t


def rope_fwd(tag, proj, cos, sin):
    def fn(q, k, v, c, sg):
        return _rope_apply(q, c, sg), _rope_apply(k, c, sg), v

    rows = [(proj, SW_DIM, COL_SWQ // SW_DIM), (proj, SW_DIM, COL_SWK // SW_DIM), (proj, SW_DIM, COL_SWV // SW_DIM), cos, sin]
    return rowmap("rope_" + tag, fn, rows, [], [SW_DIM] * 3, TM_ROW, out_dtypes=[bf16] * 3)


def _swa_block(q, kp, kc, vp, vc, first):
    blk = SW_BLOCK
    kk = jnp.concatenate([kp, kc], 0)
    vv = jnp.concatenate([vp, vc], 0)
    dist = (_iota2((blk, 2 * blk), 0) + blk) - _iota2((blk, 2 * blk), 1)
    kj = _iota2((blk, 2 * blk), 1)
    valid = (dist >= 0) & (dist <= blk) & ((kj >= blk) | jnp.logical_not(first))
    lane_head = _iota2((1, LANES), 1) // SW_HEAD_DIM
    outs, lses = [], []
    for p in range(SW_DIM // LANES):
        sl = slice(p * LANES, (p + 1) * LANES)
        qp, kp_, vp_ = q[:, sl], kk[:, sl], vv[:, sl]
        o_pair = jnp.zeros((blk, LANES), f32)
        l_pair = jnp.zeros((blk, LANES), f32)
        for e in range(LANES // SW_HEAD_DIM):
            msk = lane_head == e
            sc = bdot(jnp.where(msk, qp, 0.0), kp_, 1, 1) * (SW_HEAD_DIM ** -0.5)
            sc = jnp.where(valid, sc, -1e30)
            m = lax.stop_gradient(jnp.max(sc, -1, keepdims=True))
            pe = jnp.exp(sc - m)
            l = jnp.sum(pe, -1, keepdims=True)
            o = bdot(pe, vp_, 1, 0) / l
            o_pair = o_pair + jnp.where(msk, o, 0.0)
            l_pair = l_pair + jnp.where(msk, m + jnp.log(l), 0.0)
        outs.append(o_pair)
        lses.append(l_pair)
    return jnp.concatenate(outs, -1), jnp.concatenate(lses, -1)


def _swa_specs(r):
    cur = pl.BlockSpec((SW_BLOCK, SW_DIM), lambda rho, n: (n, rho))
    prev = pl.BlockSpec((SW_BLOCK, SW_DIM), lambda rho, n: (jnp.maximum(n - 1, 0), rho))
    return cur, prev


def swa_fwd(tag, r, q, k, v):
    s = q.shape[0]
    ln = s // r
    q2, k2, v2 = (t.reshape(ln, r * SW_DIM) for t in (q, k, v))
    cur, prev = _swa_specs(r)

    def body(q_ref, kp_ref, kc_ref, vp_ref, vc_ref, o_ref, l_ref):
        ins = [r[...].astype(f32) for r in (q_ref, kp_ref, kc_ref, vp_ref, vc_ref)]
        o, l = _swa_block(*ins, pl.program_id(1) == 0)
        o_ref[...] = o
        l_ref[...] = l

    o, l = pl.pallas_call(
        body, grid=(r, ln // SW_BLOCK),
        in_specs=[cur, prev, cur, prev, cur], out_specs=[cur, cur],
        out_shape=[jax.ShapeDtypeStruct((ln, r * SW_DIM), f32)] * 2,
        compiler_params=_cparams("parallel", "parallel"), name=f"swa{r}_{tag}")(q2, k2, k2, v2, v2)
    return o.reshape(s, SW_DIM), l.reshape(s, SW_DIM)


def swa_bwd(tag, r, q, k, v, do, dl):
    s = q.shape[0]
    ln = s // r
    q2, k2, v2, do2, dl2 = (t.reshape(ln, r * SW_DIM) for t in (q, k, v, do, dl))
    cur, prev = _swa_specs(r)

    def body(q_ref, kp_ref, kc_ref, vp_ref, vc_ref, do_ref, dl_ref, dq_ref, dka_ref, dkb_ref, dva_ref, dvb_ref):
        first = pl.program_id(1) == 0
        ins = [r[...].astype(f32) for r in (q_ref, kp_ref, kc_ref, vp_ref, vc_ref)]
        _, vjp = jax.vjp(lambda *a: _swa_block(*a, first), *ins)
        dq_ref[...], dka_ref[...], dkb_ref[...], dva_ref[...], dvb_ref[...] = vjp((do_ref[...], dl_ref[...]))

    outs = pl.pallas_call(
        body, grid=(r, ln // SW_BLOCK),
        in_specs=[cur, prev, cur, prev, cur, cur, cur], out_specs=[cur] * 5,
        out_shape=[jax.ShapeDtypeStruct((ln, r * SW_DIM), f32)] * 5,
        compiler_params=_cparams("parallel", "parallel"), name=f"swa{r}_bwd_{tag}")(q2, k2, k2, v2, v2, do2, dl2)
    return [t.reshape(s, SW_DIM) for t in outs]


def _combine_tile(o1, l1, o2, l2, o3, l3):
    m = lax.stop_gradient(jnp.maximum(jnp.maximum(l1, l2), l3))
    e1, e2, e3 = jnp.exp(l1 - m), jnp.exp(l2 - m), jnp.exp(l3 - m)
    return ((o1 * e1 + o2 * e2 + o3 * e3) / (e1 + e2 + e3),)


def swa_merge_bwd(tag, grads, cos, sin):
    s = cos.shape[0]
    tm = SW_BLOCK
    nt = s // tm
    here = pl.BlockSpec((tm, SW_DIM), lambda i: (i, 0))
    arrs, specs = [], []
    for r, g in zip(SW_DILATIONS, grads):
        ahead = pl.BlockSpec((tm, SW_DIM), lambda i, _r=r: (jnp.minimum(i + _r, nt - 1), 0))
        arrs += g
        specs += [here, ahead, here, ahead, here]

    def body(*refs):
        i = pl.program_id(0)
        c_ref, s_ref = refs[15], refs[16]
        dq_ref, dk_ref, dv_ref = refs[17:]
        dq = jnp.zeros((tm, SW_DIM), f32)
        dk = jnp.zeros((tm, SW_DIM), f32)
        dv = jnp.zeros((tm, SW_DIM), f32)
        for b, r in enumerate(SW_DILATIONS):
            gq, gka, gkb, gva, gvb = refs[5 * b:5 * b + 5]
            inside = i + r < nt
            dq = dq + gq[...]
            dk = dk + gkb[...] + jnp.where(inside, gka[...], 0.0)
            dv = dv + gvb[...] + jnp.where(inside, gva[...], 0.0)
        dq_ref[...] = _rope_transpose(dq, c_ref[...], s_ref[...])
        dk_ref[...] = _rope_transpose(dk, c_ref[...], s_ref[...])
        dv_ref[...] = dv

    return pl.pallas_call(
        body, grid=(nt,), in_specs=specs + [here, here], out_specs=[here] * 3,
        out_shape=[jax.ShapeDtypeStruct((s, SW_DIM), f32)] * 3,
        compiler_params=_cparams("parallel"), name="swa_merge_bwd_" + tag)(*arrs, cos, sin)


def swa_mixer_fwd(tag, proj, cos, sin):
    q, k, v = rope_fwd(tag, proj, cos, sin)
    ols = []
    for r in SW_DILATIONS:
        ols += list(swa_fwd(tag, r, q, k, v))
    b_out = rowmap("swa_comb_" + tag, _combine_tile, ols, [], [SW_DIM], TM_ROW)[0]
    return b_out, (q, k, v, ols)


def swa_mixer_bwd(tag, cos, sin, res, db_out):
    q, k, v, ols = res
    dols, _ = rowmap_bwd("swa_comb_bwd_" + tag, _combine_tile, ols, [], [db_out], TM_ROW)
    grads = [swa_bwd(tag, r, q, k, v, dols[2 * b], dols[2 * b + 1]) for b, r in enumerate(SW_DILATIONS)]
    return swa_merge_bwd(tag, grads, cos, sin)


TM_S5 = 256
S5_GPB = LANES // S5_GROUP
S5_NBLK = D_MODEL // LANES
S5_HALF = S5_GPB * S5_STATE
S5_BW = 2 * S5_HALF
S5_WIDTH = S5_NBLK * S5_BW
S5_TABW = S5_NBLK * S5_HALF


def _s5_disc_tile(a_re, a_im, log_dt, b_re, b_im, expand):
    dt = jnp.exp(log_dt)
    mag = jnp.exp(a_re * dt)
    abar_re, abar_im = mag * jnp.cos(a_im * dt), mag * jnp.sin(a_im * dt)
    n_re, n_im = abar_re - 1.0, abar_im
    den = a_re * a_re + a_im * a_im
    c_re = (n_re * a_re + n_im * a_im) / den
    c_im = (n_im * a_re - n_re * a_im) / den
    cx_re, cx_im = hdot(c_re, expand), hdot(c_im, expand)
    return abar_re, abar_im, cx_re * b_re - cx_im * b_im, cx_re * b_im + cx_im * b_re


def _s5_expand():
    return (_iota2((S5_STATE, S5_STATE * S5_GROUP), 1) // S5_GROUP == _iota2((S5_STATE, S5_STATE * S5_GROUP), 0)).astype(f32)


def s5_tables(a_re, a_im, log_dt):
    lanes = lambda v: v.reshape(1, S5_TABW)
    dt = jnp.broadcast_to(log_dt.reshape(S5_GROUPS, 1), (S5_GROUPS, S5_STATE))
    t = TM_S5

    def body(are_ref, aim_ref, ldt_ref, ar_ref, ai_ref, arr_ref, air_ref):
        dtv = jnp.exp(ldt_ref[...])
        lre, lim = are_ref[...] * dtv, aim_ref[...] * dtv
        row = _iota2((t, S5_HALF), 0)
        for asc, o_re, o_im in ((True, ar_ref, ai_ref), (False, arr_ref, air_ref)):
            n = (row + 1 if asc else t - row).astype(f32)
            mag = jnp.exp(n * lre)
            o_re[...] = mag * jnp.cos(n * lim)
            o_im[...] = mag * jnp.sin(n * lim)

    lane = pl.BlockSpec((1, S5_HALF), lambda j: (0, j))
    tab = pl.BlockSpec((t, S5_HALF), lambda j: (0, j))
    return pl.pallas_call(
        body, grid=(S5_NBLK,), in_specs=[lane] * 3, out_specs=[tab] * 4,
        out_shape=[jax.ShapeDtypeStruct((t, S5_TABW), f32)] * 4,
        compiler_params=_cparams("parallel"), name="s5_tables")(lanes(a_re), lanes(a_im), lanes(dt))


def s5_pack_weights(bbar_re, bbar_im, c_re, c_im):
    eye = jnp.eye(S5_GPB, dtype=f32)
    bb = jnp.stack([bbar_re.reshape(S5_GROUPS, S5_STATE, S5_GROUP), bbar_im.reshape(S5_GROUPS, S5_STATE, S5_GROUP)], 1)
    bb = bb.transpose(0, 3, 1, 2).reshape(S5_NBLK, S5_GPB, S5_GROUP, 2, S5_STATE)
    wb = (bb[:, :, :, :, None, :] * eye[None, :, None, None, :, None]).reshape(S5_NBLK, LANES, S5_BW)
    cc = jnp.stack([c_re, -c_im], 1)
    cc = cc.reshape(S5_NBLK, S5_GPB, 2, S5_GROUP, S5_STATE).transpose(0, 2, 1, 4, 3)
    wc = (cc[:, :, :, :, None, :] * eye[None, None, :, None, :, None]).reshape(S5_NBLK, S5_BW, LANES)
    return wb, wc


def s5_unpack_weight_grads(dwb, dwc):
    d6 = dwb.reshape(S5_NBLK, S5_GPB, S5_GROUP, 2, S5_GPB, S5_STATE)
    dbb = jnp.stack([d6[:, gl, :, :, gl, :] for gl in range(S5_GPB)])
    dbb = dbb.transpose(1, 0, 3, 4, 2).reshape(S5_GROUPS, 2, S5_STATE * S5_GROUP)
    c6 = dwc.reshape(S5_NBLK, 2, S5_GPB, S5_STATE, S5_GPB, S5_GROUP)
    dcc = jnp.stack([c6[:, :, gl, :, gl, :] for gl in range(S5_GPB)])
    dcc = dcc.transpose(1, 0, 2, 4, 3).reshape(S5_GROUPS, 2, S5_GROUP, S5_STATE)
    return dbb[:, 0], dbb[:, 1], dcc[:, 0], -dcc[:, 1]


def _s5_step_rows(t):
    d, out = 1, []
    while d < t:
        out.append(d)
        d *= 2
    return out


def s5_core_fwd(tag, u, wb, wc, a1, a2, dskip):
    s = u.shape[0]
    t = TM_S5

    def body(u_ref, wb_ref, wc_ref, ar_ref, ai_ref, d_ref, y_ref, x_ref, carry):
        @pl.when(pl.program_id(1) == 0)
        def _():
            carry[...] = jnp.zeros_like(carry)

        uv = u_ref[...]
        bu = bdot(uv, wb_ref[...], 1, 0)
        row = _iota2((t, LANES), 0)
        for c in range(S5_HALF // LANES):
            re, im = slice(c * LANES, (c + 1) * LANES), slice(S5_HALF + c * LANES, S5_HALF + (c + 1) * LANES)
            xr, xi = bu[:, re], bu[:, im]
            for d in _s5_step_rows(t):
                ar, ai = ar_ref[d - 1:d, re], ai_ref[d - 1:d, re]
                if d % SUBLANES:
                    keep = row >= d
                    sr = jnp.where(keep, pltpu.roll(xr, d, 0), 0.0)
                    si = jnp.where(keep, pltpu.roll(xi, d, 0), 0.0)
                    xr, xi = xr + ar * sr - ai * si, xi + ar * si + ai * sr
                else:
                    sr, si = xr[:t - d], xi[:t - d]
                    xr = jnp.concatenate([xr[:d], xr[d:] + (ar * sr - ai * si)], 0)
                    xi = jnp.concatenate([xi[:d], xi[d:] + (ar * si + ai * sr)], 0)
            cr, ci = carry[:, re], carry[:, im]
            ar, ai = ar_ref[:, re], ai_ref[:, re]
            x_ref[:, re] = xr + ar * cr - ai * ci
            x_ref[:, im] = xi + ar * ci + ai * cr
        carry[...] = x_ref[t - 1:t, :]
        y_ref[...] = bdot(x_ref[...], wc_ref[...], 1, 0) + d_ref[...] * uv

    tab = pl.BlockSpec((t, S5_HALF), lambda j, i: (0, j))
    return pl.pallas_call(
        body, grid=(S5_NBLK, s // t),
        in_specs=[pl.BlockSpec((t, LANES), lambda j, i: (i, j)),
                  pl.BlockSpec((None, LANES, S5_BW), lambda j, i: (j, 0, 0)),
                  pl.BlockSpec((None, S5_BW, LANES), lambda j, i: (j, 0, 0)),
                  tab, tab, pl.BlockSpec((1, LANES), lambda j, i: (0, j))],
        out_specs=[pl.BlockSpec((t, LANES), lambda j, i: (i, j)), pl.BlockSpec((t, S5_BW), lambda j, i: (i, j))],
        out_shape=[jax.ShapeDtypeStruct((s, D_MODEL), f32), jax.ShapeDtypeStruct((s, S5_WIDTH), f32)],
        scratch_shapes=[pltpu.VMEM((1, S5_BW), f32)],
        compiler_params=_cparams("parallel", "arbitrary"), name="s5_core_" + tag)(u, wb, wc, a1, a2, dskip)


def s5_core_bwd(tag, u, x, wb, wc, a1, a2, a1r, a2r, dskip, dy):
    s = u.shape[0]
    t = TM_S5
    nt = s // t
    hb = t // SUBLANES

    def body(u_ref, dy_ref, x_ref, xh_ref, wb_ref, wc_ref, ar_ref, ai_ref, arr_ref, air_ref, d_ref,
             du_ref, dwb_ref, dwc_ref, dd_ref, q1_ref, q2_ref, carry, lam_scr):
        i = pl.program_id(1)
        tt = nt - 1 - i

        @pl.when(i == 0)
        def _():
            carry[...] = jnp.zeros_like(carry)

        uv, dyv, xv = u_ref[...], dy_ref[...], x_ref[...]
        lam = bdot(dyv, wc_ref[...], 1, 1)
        row = _iota2((t, LANES), 0)
        x_last = jnp.where(tt > 0, xh_ref[SUBLANES - 1:SUBLANES, :], 0.0)
        q1s, q2s = [], []
        for c in range(S5_HALF // LANES):
            re, im = slice(c * LANES, (c + 1) * LANES), slice(S5_HALF + c * LANES, S5_HALF + (c + 1) * LANES)
            lr, li = lam[:, re], lam[:, im]
            for d in _s5_step_rows(t):
                ar, ai = ar_ref[d - 1:d, re], ai_ref[d - 1:d, re]
                if d % SUBLANES:
                    keep = row < t - d
                    sr = jnp.where(keep, pltpu.roll(lr, t - d, 0), 0.0)
                    si = jnp.where(keep, pltpu.roll(li, t - d, 0), 0.0)
                    lr, li = lr + ar * sr + ai * si, li + ar * si - ai * sr
                else:
                    sr, si = lr[d:], li[d:]
                    lr = jnp.concatenate([lr[:t - d] + (ar * sr + ai * si), lr[t - d:]], 0)
                    li = jnp.concatenate([li[:t - d] + (ar * si - ai * sr), li[t - d:]], 0)
            cr, ci = carry[:, re], carry[:, im]
            ar, ai = arr_ref[:, re], air_ref[:, re]
            lr, li = lr + ar * cr + ai * ci, li + ar * ci - ai * cr
            lam_scr[:, re] = lr
            lam_scr[:, im] = li
            pr = jnp.where(row == 0, x_last[:, re], pltpu.roll(xv[:, re], 1, 0))
            pi = jnp.where(row == 0, x_last[:, im], pltpu.roll(xv[:, im], 1, 0))
            p1, p2 = lr * pr + li * pi, li * pr - lr * pi
            q1, q2 = p1[:SUBLANES, :], p2[:SUBLANES, :]
            for k in range(1, hb):
                q1 = q1 + p1[k * SUBLANES:(k + 1) * SUBLANES, :]
                q2 = q2 + p2[k * SUBLANES:(k + 1) * SUBLANES, :]
            q1s.append(q1)
            q2s.append(q2)
        carry[...] = lam_scr[0:1, :]
        lam = lam_scr[...]
        du_ref[...] = bdot(lam, wb_ref[...], 1, 1) + d_ref[...] * dyv
        upd = [(dwb_ref, bdot(uv, lam, 0, 0)), (dwc_ref, bdot(xv, dyv, 0, 0)),
               (dd_ref, jnp.sum(dyv * uv, 0, keepdims=True)),
               (q1_ref, jnp.concatenate(q1s, 1)), (q2_ref, jnp.concatenate(q2s, 1))]

        @pl.when(i == 0)
        def _():
            for ref, val in upd:
                ref[...] = val

        @pl.when(i != 0)
        def _():
            for ref, val in upd:
                ref[...] += val

    nb8 = s // SUBLANES
    rev = lambda w: pl.BlockSpec((t, w), lambda j, i: (nt - 1 - i, j))
    tab = pl.BlockSpec((t, S5_HALF), lambda j, i: (0, j))
    return pl.pallas_call(
        body, grid=(S5_NBLK, nt),
        in_specs=[rev(LANES), rev(LANES), rev(S5_BW),
                  pl.BlockSpec((SUBLANES, S5_BW), lambda j, i: (jnp.maximum((nt - 1 - i) * hb - 1, 0), j)),
                  pl.BlockSpec((None, LANES, S5_BW), lambda j, i: (j, 0, 0)),
                  pl.BlockSpec((None, S5_BW, LANES), lambda j, i: (j, 0, 0)),
                  tab, tab, tab, tab, pl.BlockSpec((1, LANES), lambda j, i: (0, j))],
        out_specs=[rev(LANES),
                   pl.BlockSpec((None, LANES, S5_BW), lambda j, i: (j, 0, 0)),
                   pl.BlockSpec((None, S5_BW, LANES), lambda j, i: (j, 0, 0)),
                   pl.BlockSpec((1, LANES), lambda j, i: (0, j)),
                   pl.BlockSpec((SUBLANES, S5_HALF), lambda j, i: (0, j)),
                   pl.BlockSpec((SUBLANES, S5_HALF), lambda j, i: (0, j))],
        out_shape=[jax.ShapeDtypeStruct((s, D_MODEL), f32),
                   jax.ShapeDtypeStruct((S5_NBLK, LANES, S5_BW), f32),
                   jax.ShapeDtypeStruct((S5_NBLK, S5_BW, LANES), f32),
                   jax.ShapeDtypeStruct((1, D_MODEL), f32),
                   jax.ShapeDtypeStruct((SUBLANES, S5_TABW), f32),
                   jax.ShapeDtypeStruct((SUBLANES, S5_TABW), f32)],
        scratch_shapes=[pltpu.VMEM((1, S5_BW), f32), pltpu.VMEM((t, S5_BW), f32)],
        compiler_params=_cparams("parallel", "arbitrary"),
        name="s5_core_bwd_" + tag)(u, dy, x, x, wb, wc, a1, a2, a1r, a2r, dskip)


def _gelu_tile(y):
    return (jax.nn.gelu(y),)


def s5_mixer_fwd(tag, u, prm, w_og):
    a_re, a_im, log_dt, b_re, b_im, c_re, c_im, dskip = prm
    disc_in = [a_re, a_im, log_dt.reshape(S5_GROUPS, 1), b_re.reshape(S5_GROUPS, -1), b_im.reshape(S5_GROUPS, -1)]
    abar_re, abar_im, bbar_re, bbar_im = rowmap("s5_disc_" + tag, _s5_disc_tile, disc_in, [_s5_expand()],
                                                [S5_STATE, S5_STATE, S5_STATE * S5_GROUP, S5_STATE * S5_GROUP], S5_GROUPS)
    del abar_re, abar_im
    a1, a2, a1r, a2r = s5_tables(a_re, a_im, log_dt)
    wb, wc = s5_pack_weights(bbar_re, bbar_im, c_re, c_im)
    wb, wc = wb.astype(bf16), wc.astype(bf16)
    y, x = s5_core_fwd(tag, u, wb, wc, a1, a2, dskip.reshape(1, D_MODEL))
    hid = rowmap("s5_gelu_" + tag, _gelu_tile, [y], [], [D_MODEL], TM_ROW, out_dtypes=[bf16])[0]
    og = mm_nn("s5_og_" + tag, hid, w_og)
    mix = rowmap("s5_glu_" + tag, _glu_tile, [og], [], [D_MODEL], TM_ROW)[0]
    return mix, (disc_in, a1, a2, a1r, a2r, wb, wc, x, y, hid, og)


def s5_mixer_bwd(tag, idx, u, prm, w_og, res, dmix, stacks):
    a_re, a_im, log_dt, b_re, b_im, c_re, c_im, dskip = prm
    disc_in, a1, a2, a1r, a2r, wb, wc, x, y, hid, og = res
    (dog,), _ = rowmap_bwd("s5_glu_bwd_" + tag, _glu_tile, [og], [], [dmix], TM_ROW)
    n_odd = DEPTH // 2
    dw_og = (mm_tn("s5_wo_dw_" + tag, hid, dog, n_cols=D_MODEL, stack=(stacks[0], idx, n_odd)),
             mm_tn("s5_wg_dw_" + tag, hid, dog, b_col0=D_MODEL, n_cols=D_MODEL, stack=(stacks[1], idx, n_odd)))
    dhid = mm_nt("s5_og_dx_" + tag, dog, w_og)
    (dy,), _ = rowmap_bwd("s5_gelu_bwd_" + tag, _gelu_tile, [y], [], [dhid], TM_ROW)
    du, dwb, dwc, ddskip, q1, q2 = s5_core_bwd(tag, u, x, wb, wc, a1, a2, a1r, a2r, dskip.reshape(1, D_MODEL), dy)
    dbbar_re, dbbar_im, dc_re, dc_im = s5_unpack_weight_grads(dwb, dwc)
    dabar_re = q1.sum(0).reshape(S5_GROUPS, S5_STATE)
    dabar_im = q2.sum(0).reshape(S5_GROUPS, S5_STATE)
    grads, _ = rowmap_bwd("s5_disc_bwd_" + tag, _s5_disc_tile, disc_in, [_s5_expand()],
                          [dabar_re, dabar_im, dbbar_re, dbbar_im], S5_GROUPS, par_mask=[False])
    da_re, da_im, dlog_dt, db_re, db_im = grads
    return du, (da_re, da_im, dlog_dt.reshape(S5_GROUPS), db_re.reshape(b_re.shape), db_im.reshape(b_im.shape),
                dc_re, dc_im, ddskip.reshape(D_MODEL)), dw_og


HYB_IN = 3592
_IN_B0, _IN_SW0 = 2048, 2056


IN_SHARD = HYB_IN // 4
SHARD_ORDER_GRADS = ("hyb_w_in", "ffn_wg", "ffn_wu", "ffn_wd")
FFN_TRANSPOSED = ("ffn_wg", "ffn_wu")
BIG_SHARDED = ("hyb_w_in", "hyb_w_out", "s5_glu_wo", "s5_glu_wg", "xq_w", "xk_w", "xv_w", "xo_w", "ffn_wg", "ffn_wu", "ffn_wd")


def _w_in_pieces():
    runs = [(0, _IN_B0, 0), (_IN_B0, _IN_SW0, COL_BA), (_IN_SW0, HYB_IN, _IN_B0)]
    out = []
    for sh in range(4):
        lo, hi = sh * IN_SHARD, (sh + 1) * IN_SHARD
        for r_lo, r_hi, c_lo in runs:
            a, b = max(lo, r_lo), min(hi, r_hi)
            if a < b:
                out.append((sh, a - lo, b - lo, c_lo + a - r_lo))
    return out


def w_in_to_canonical(tag, layer, w4):
    tr = 128

    def body(w_ref, o_ref):
        o_ref[:, COL_BA:] = jnp.zeros((tr, BA_PAD), o_ref.dtype)
        for sh, a, b, c in _w_in_pieces():
            o_ref[:, c:c + b - a] = w_ref[sh, :, a:b]

    return pl.pallas_call(
        body, grid=(D_MODEL // tr,),
        in_specs=[pl.BlockSpec((4, None, tr, IN_SHARD), lambda i: (0, layer, i, 0))],
        out_specs=pl.BlockSpec((tr, PROJ_COLS), lambda i: (i, 0)),
        out_shape=jax.ShapeDtypeStruct((D_MODEL, PROJ_COLS), w4.dtype),
        compiler_params=_cparams("parallel"), name="w_in_canon_" + tag)(w4)


def w_in_grad_to_shards(tag, layer, g, stack, n_layers):
    tr = 128
    extra, extra_specs, _ = _stacked(stack)

    def body(g_ref, *rest):
        o_ref = rest[-1]
        for sh, a, b, c in _w_in_pieces():
            o_ref[sh, :, a:b] = g_ref[:, c:c + b - a]

    return pl.pallas_call(
        body, grid=(D_MODEL // tr,),
        in_specs=[pl.BlockSpec((tr, PROJ_COLS), lambda i: (i, 0))] + extra_specs,
        out_specs=pl.BlockSpec((4, None, tr, IN_SHARD), lambda i: (0, layer, i, 0)),
        out_shape=jax.ShapeDtypeStruct((4, n_layers, D_MODEL, IN_SHARD), f32),
        input_output_aliases={1: 0} if extra else {},
        compiler_params=_cparams("parallel"), name="w_in_grad_shards_" + tag)(g, *extra)


def _add2(name, a, b):
    return rowmap(name, lambda p, q: (p + q,), [a, b], [], [a.shape[1]], _pick(a.shape[0], (256, 128, 64, 32, 16, 8)))[0]


def local_step(x, mem, positions, target, p):
    s = x.shape[0]
    cos, sin = rope_tables(positions, s)
    row = lambda v: v.reshape(1, -1).astype(f32)
    wg4, wu4, wd4 = (p[n].astype(bf16) for n in ("ffn_wg", "ffn_wu", "ffn_wd"))
    h = h16 = x
    tape = []
    for l in range(DEPTH):
        i, tag = l // 2, str(l)
        t = {"h0": h, "h0_16": h16}
        if l % 2 == 0:
            t["w_in"] = w_in_to_canonical(tag, i, p["hyb_w_in"].astype(bf16))
            t["w_out"] = p["hyb_w_out"][i].astype(bf16)
            t["dn_prm"] = (p["dn_conv_w"][i].astype(f32), row(jnp.repeat(p["dn_a_log"][i], DN_HEAD_DIM)),
                           row(jnp.repeat(p["dn_dt_bias"][i], DN_HEAD_DIM)), row(jnp.tile(p["dn_norm_g"][i], DN_HEADS)))
            t["proj"] = mm_nn("hyb_in_" + tag, h16, t["w_in"])
            a_out, t["dn"] = dn_mixer_fwd(tag, t["proj"], *t["dn_prm"])
            b_out, t["swa"] = swa_mixer_fwd(tag, t["proj"], cos, sin)
            t["mixed"] = jnp.concatenate([a_out, b_out], 1)
            mix = mm_nn("hyb_out_" + tag, t["mixed"], t["w_out"])
        else:
            t["s5_prm"] = tuple(p[n][i].astype(f32) for n in
                                ("s5_a_re", "s5_a_im", "s5_log_dt", "s5_b_re", "s5_b_im", "s5_c_re", "s5_c_im", "s5_d"))
            t["w_og"] = jnp.concatenate([p["s5_glu_wo"][i], p["s5_glu_wg"][i]], 1).astype(bf16)
            mix, t["s5"] = s5_mixer_fwd(tag, h, t["s5_prm"], t["w_og"])
        t["mix"] = mix
        t["ln"] = [(row(p[g][l]), row(p[b][l])) for g, b in
                   (("ln_mix_g", "ln_mix_b"), ("ln_x_g", "ln_x_b"), ("ln_ffn_g", "ln_ffn_b"))]
        t["h1"], t["h1_16"] = postnorm_fwd("mix" + tag, h, mix, *t["ln"][0])
        t["wq"], t["wo"] = p["xq_w"][l].astype(bf16), p["xo_w"][l].astype(bf16)
        t["wkv"] = jnp.concatenate([p["xk_w"][l], p["xv_w"][l]], 1).astype(bf16)
        t["xo"], t["xres"] = xattn_fwd(tag, t["h1_16"], mem, t["wq"], t["wkv"], t["wo"])
        t["h2"], t["h2_16"] = postnorm_fwd("x" + tag, t["h1"], t["xo"], *t["ln"][1])
        t["fo"], t["fres"] = ffn_fwd(tag, l, t["h2_16"], wg4, wu4, wd4)
        h, h16 = postnorm_fwd("ffn" + tag, t["h2"], t["fo"], *t["ln"][2])
        tape.append(t)

    part, dh = loss_head(h, target)
    loss = jnp.sum(part)

    g = {n: [None] * v.shape[0] for n, v in p.items() if n not in BIG_SHARDED}
    st = {n: None for n in BIG_SHARDED}
    for l in reversed(range(DEPTH)):
        i, tag, t = l // 2, str(l), tape[l]
        dh2a, dfo, dg, db = postnorm_bwd("ffn" + tag, t["h2"], t["fo"], *t["ln"][2], dh)
        g["ln_ffn_g"][l], g["ln_ffn_b"][l] = dg[0], db[0]
        ffn_names = ("ffn_wg", "ffn_wu", "ffn_wd")
        prev = None if st["ffn_wd"] is None else [st[n] for n in ffn_names]
        dh2b, new = ffn_bwd(tag, l, t["h2_16"], wg4, wu4, wd4, t["fres"], dfo, prev)
        st.update(zip(ffn_names, new))
        dh1a, dxo, dg, db = postnorm_bwd("x" + tag, t["h1"], t["xo"], *t["ln"][1], [dh2a, dh2b])
        g["ln_x_g"][l], g["ln_x_b"][l] = dg[0], db[0]
        x_names = ("xq_w", "xk_w", "xv_w", "xo_w")
        dh1b, new = xattn_bwd(tag, l, t["h1_16"], mem, t["wq"], t["wkv"], t["wo"], t["xres"], dxo, [st[n] for n in x_names])
        st.update(zip(x_names, new))
        dh0a, dmix, dg, db = postnorm_bwd("mix" + tag, t["h0"], t["mix"], *t["ln"][0], [dh1a, dh1b])
        g["ln_mix_g"][l], g["ln_mix_b"][l] = dg[0], db[0]
        if l % 2 == 0:
            st["hyb_w_out"] = mm_tn("hyb_out_dw_" + tag, t["mixed"], dmix, stack=(st["hyb_w_out"], i, DEPTH // 2))
            dmixed = mm_nt("hyb_out_dx_" + tag, dmix, t["w_out"])
            dqkv, dz, dba, dcw, dalog, ddtb, dng = dn_mixer_bwd(tag, t["proj"], *t["dn_prm"], t["dn"], (dmixed, DN_KEY_DIM, 0))
            g["dn_conv_w"][i] = dcw
            g["dn_a_log"][i] = dalog.reshape(DN_HEADS, DN_HEAD_DIM).sum(1)
            g["dn_dt_bias"][i] = ddtb.reshape(DN_HEADS, DN_HEAD_DIM).sum(1)
            g["dn_norm_g"][i] = dng.reshape(DN_HEADS, DN_HEAD_DIM).sum(0)
            dq, dk, dv = swa_mixer_bwd(tag, cos, sin, t["swa"], (dmixed, SW_DIM, 1))
            dproj = jnp.concatenate([dqkv, dz, dq, dk, dv, dba], 1)
            st["hyb_w_in"] = w_in_grad_to_shards(tag, i, mm_tn("hyb_in_dw_" + tag, t["h0_16"], dproj), st["hyb_w_in"], DEPTH // 2)
            dh0b = mm_nt("hyb_in_dx_" + tag, dproj, t["w_in"])
        else:
            dh0b, dprm, (st["s5_glu_wo"], st["s5_glu_wg"]) = s5_mixer_bwd(
                tag, i, t["h0"], t["s5_prm"], t["w_og"], t["s5"], dmix, (st["s5_glu_wo"], st["s5_glu_wg"]))
            for n, v in zip(("s5_a_re", "s5_a_im", "s5_log_dt", "s5_b_re", "s5_b_im", "s5_c_re", "s5_c_im", "s5_d"), dprm):
                g[n][i] = v
        dh = [dh0a, dh0b]
    grad_x = _add2("grad_x", dh[0], dh[1])
    grads = {n: jnp.stack(v) for n, v in g.items()}
    grads.update(st)
    return loss, grad_x, grads


WEIGHT_NAMES = ("hyb_w_in", "dn_conv_w", "dn_a_log", "dn_dt_bias", "dn_norm_g", "hyb_w_out", "s5_a_re", "s5_a_im",
                "s5_log_dt", "s5_b_re", "s5_b_im", "s5_c_re", "s5_c_im", "s5_d", "s5_glu_wo", "s5_glu_wg",
                "ln_mix_g", "ln_mix_b", "xq_w", "xk_w", "xv_w", "xo_w", "ln_x_g", "ln_x_b",
                "ffn_wg", "ffn_wu", "ffn_wd", "ln_ffn_g", "ln_ffn_b")
SHARD_AXIS = {"hyb_w_in": 2, "dn_conv_w": 2, "hyb_w_out": 1, "s5_d": 1, "s5_glu_wo": 1, "s5_glu_wg": 1,
              "xq_w": 1, "xk_w": 1, "xv_w": 1, "xo_w": 1, "ffn_wg": 2, "ffn_wu": 2, "ffn_wd": 1}
GATHER_F32 = ("dn_conv_w", "s5_d")
N_CHIPS = 4
PACK_COLS = 1024
_ANY = pl.BlockSpec(memory_space=pl.ANY)


def _pos():
    return lax.axis_index("x"), lax.axis_index("y"), lax.axis_index("c")


def _chip_peers(mx, my):
    return [(1 - mx, my), (mx, 1 - my), (1 - mx, 1 - my)]


def _rcopy(src, dst, ssem, rsem, dev):
    return pltpu.make_async_remote_copy(src_ref=src, dst_ref=dst, send_sem=ssem, recv_sem=rsem,
                                        device_id=dev, device_id_type=pl.DeviceIdType.MESH)


def comm_allgather4(name, x):
    def body(x_ref, o_ref, ssem, rsem, lsem):
        mx, my, mc = _pos()
        me = 2 * mx + my
        peers = _chip_peers(mx, my)
        loc = pltpu.make_async_copy(x_ref, o_ref.at[me], lsem)
        loc.start()
        sends = [_rcopy(x_ref, o_ref.at[me], ssem.at[k], rsem.at[k], (px, py, mc)) for k, (px, py) in enumerate(peers)]
        for cp in sends:
            cp.start()
        for k, (px, py) in enumerate(peers):
            _rcopy(x_ref, o_ref.at[2 * px + py], ssem.at[k], rsem.at[k], (px, py, mc)).wait_recv()
        for cp in sends:
            cp.wait_send()
        loc.wait()

    return pl.pallas_call(
        body, out_shape=jax.ShapeDtypeStruct((N_CHIPS,) + x.shape, x.dtype), in_specs=[_ANY], out_specs=_ANY,
        scratch_shapes=[pltpu.SemaphoreType.DMA((3,)), pltpu.SemaphoreType.DMA((3,)), pltpu.SemaphoreType.DMA],
        name=name)(x)


def _multi_call(name, body, ins, out_shapes, sems, in_place=False):
    return pl.pallas_call(
        body, out_shape=out_shapes, in_specs=[_ANY] * len(ins), out_specs=[_ANY] * len(out_shapes),
        scratch_shapes=sems, input_output_aliases={w: w for w in range(len(ins))} if in_place else {},
        name=name)(*ins)


def comm_gather_weights(name, slots):
    n = len(slots)

    def body(*refs):
        os_ = refs[n:2 * n]
        ssem, rsem, fssem, frsem = refs[2 * n:]
        mx, my, mc = _pos()
        me = 2 * mx + my
        peers = _chip_peers(mx, my)
        sib = (mx, my, 1 - mc)
        half = [o.shape[1] // 2 for o in os_]
        mine = [pl.ds(mc * h, h) for h in half]
        other = [pl.ds((1 - mc) * h, h) for h in half]
        sends = [_rcopy(os_[w].at[me, mine[w]], os_[w].at[me, mine[w]], ssem.at[w, k], rsem.at[w, k], (px, py, mc))
                 for w in range(n) for k, (px, py) in enumerate(peers)]
        for cp in sends:
            cp.start()
        fwds = []
        for w in range(n):
            for k, (px, py) in enumerate(peers):
                landed = os_[w].at[2 * px + py, mine[w]]
                _rcopy(landed, landed, ssem.at[w, k], rsem.at[w, k], (px, py, mc)).wait_recv()
                fw = _rcopy(landed, landed, fssem.at[w, k], frsem.at[w, k], sib)
                fw.start()
                fwds.append(fw)
        for w in range(n):
            for k, (px, py) in enumerate(peers):
                theirs = os_[w].at[2 * px + py, other[w]]
                _rcopy(theirs, theirs, fssem.at[w, k], frsem.at[w, k], sib).wait_recv()
        for cp in sends + fwds:
            cp.wait_send()

    dma = pltpu.SemaphoreType.DMA
    return _multi_call(name, body, slots, [jax.ShapeDtypeStruct(x.shape, x.dtype) for x in slots],
                       [dma((n, 3)), dma((n, 3)), dma((n, 3)), dma((n, 3))], in_place=True)


def comm_sibling_halves(name, gs):
    n = len(gs)

    def body(*refs):
        xs, os_ = refs[:n], refs[n:2 * n]
        ssem, rsem = refs[2 * n:]
        mx, my, mc = _pos()
        sib = (mx, my, 1 - mc)
        sends = []
        for w in range(n):
            h = xs[w].shape[1] // 2
            for j in range(N_CHIPS):
                sends.append(_rcopy(xs[w].at[j, pl.ds((1 - mc) * h, h)], os_[w].at[j], ssem.at[w, j], rsem.at[w, j], sib))
        for cp in sends:
            cp.start()
        for w in range(n):
            for j in range(N_CHIPS):
                _rcopy(os_[w].at[j], os_[w].at[j], ssem.at[w, j], rsem.at[w, j], sib).wait_recv()
        for cp in sends:
            cp.wait_send()

    dma = pltpu.SemaphoreType.DMA
    return _multi_call(name, body, gs,
                       [jax.ShapeDtypeStruct((N_CHIPS, g.shape[1] // 2) + g.shape[2:], g.dtype) for g in gs],
                       [dma((n, N_CHIPS)), dma((n, N_CHIPS))])


def comm_alltoall4(name, xs):
    n = len(xs)

    def body(*refs):
        xr, os_ = refs[:n], refs[n:2 * n]
        ssem, rsem = refs[2 * n:]
        mx, my, mc = _pos()
        me = 2 * mx + my
        peers = _chip_peers(mx, my)
        sends = [_rcopy(xr[w].at[2 * px + py], os_[w].at[me], ssem.at[w, k], rsem.at[w, k], (px, py, mc))
                 for w in range(n) for k, (px, py) in enumerate(peers)]
        for cp in sends:
            cp.start()
        for w in range(n):
            for k, (px, py) in enumerate(peers):
                dst = os_[w].at[2 * px + py]
                _rcopy(dst, dst, ssem.at[w, k], rsem.at[w, k], (px, py, mc)).wait_recv()
        for cp in sends:
            cp.wait_send()

    dma = pltpu.SemaphoreType.DMA
    return _multi_call(name, body, xs, [jax.ShapeDtypeStruct(x.shape, x.dtype) for x in xs], [dma((n, 3)), dma((n, 3))])


def comm_sibling_join(name, bs):
    n = len(bs)

    def body(*refs):
        os_ = refs[n:2 * n]
        ssem, rsem = refs[2 * n:]
        mx, my, mc = _pos()
        sib = (mx, my, 1 - mc)
        sends = [_rcopy(os_[w].at[mc], os_[w].at[mc], ssem.at[w], rsem.at[w], sib) for w in range(n)]
        for cp in sends:
            cp.start()
        for w in range(n):
            dst = os_[w].at[1 - mc]
            _rcopy(dst, dst, ssem.at[w], rsem.at[w], sib).wait_recv()
        for cp in sends:
            cp.wait_send()

    dma = pltpu.SemaphoreType.DMA
    return _multi_call(name, body, bs, [jax.ShapeDtypeStruct(b.shape, b.dtype) for b in bs], [dma((n,)), dma((n,))],
                       in_place=True)


def comm_sibling_swap(name, x):
    def body(x_ref, o_ref, ssem, rsem):
        mx, my, mc = _pos()
        cp = _rcopy(x_ref, o_ref, ssem, rsem, (mx, my, 1 - mc))
        cp.start()
        cp.wait_recv()
        cp.wait_send()

    return pl.pallas_call(
        body, out_shape=jax.ShapeDtypeStruct(x.shape, x.dtype), in_specs=[_ANY], out_specs=_ANY,
        scratch_shapes=[pltpu.SemaphoreType.DMA, pltpu.SemaphoreType.DMA], name=name)(x)


def _row_tile(r):
    return _pick(r, (256, 128, 64, 32, 16, 8))


def add_own_half(name, g, recv, out_dtype):
    r, c = g.shape[2:]
    tr = _row_tile(r)
    mc = lax.axis_index("c").astype(jnp.int32).reshape(1)

    def body(c_ref, g_ref, r_ref, o_ref):
        o_ref[...] = (g_ref[...] + r_ref[...]).astype(o_ref.dtype)

    grid_spec = pltpu.PrefetchScalarGridSpec(
        num_scalar_prefetch=1, grid=(N_CHIPS, r // tr),
        in_specs=[pl.BlockSpec((None, None, tr, c), lambda j, i, cr: (j, cr[0], i, 0)),
                  pl.BlockSpec((None, tr, c), lambda j, i, cr: (j, i, 0))],
        out_specs=pl.BlockSpec((None, tr, c), lambda j, i, cr: (j, i, 0)))
    return pl.pallas_call(body, grid_spec=grid_spec, out_shape=jax.ShapeDtypeStruct(recv.shape, out_dtype),
                          compiler_params=_cparams("parallel", "parallel"), name=name)(mc, g, recv)


def cast_into_slot(name, w, chip, dtype):
    r, c = w.shape
    tr = _row_tile(r)

    def body(c_ref, w_ref, o_ref):
        o_ref[...] = w_ref[...].astype(o_ref.dtype)

    grid_spec = pltpu.PrefetchScalarGridSpec(
        num_scalar_prefetch=1, grid=(r // tr,),
        in_specs=[pl.BlockSpec((tr, c), lambda i, cr: (i, 0))],
        out_specs=pl.BlockSpec((None, tr, c), lambda i, cr: (cr[0], i, 0)))
    return pl.pallas_call(body, grid_spec=grid_spec, out_shape=jax.ShapeDtypeStruct((N_CHIPS, r, c), dtype),
                          compiler_params=_cparams("parallel"), name=name)(chip.astype(jnp.int32).reshape(1), w)


def sum_chips_into_half(name, own, arrived, chip, mc):
    r, c = own.shape[1:]
    tr = _row_tile(r)

    def body(s0, s1, s2, s3, s4, own_ref, a_ref, b_ref, d_ref, o_ref):
        o_ref[...] = ((own_ref[...].astype(f32) + a_ref[...].astype(f32))
                      + (b_ref[...].astype(f32) + d_ref[...].astype(f32)))

    slot = lambda k: pl.BlockSpec((None, tr, c), lambda i, *sc, _k=k: (sc[_k][0], i, 0))
    grid_spec = pltpu.PrefetchScalarGridSpec(
        num_scalar_prefetch=5, grid=(r // tr,), in_specs=[slot(0), slot(1), slot(2), slot(3)],
        out_specs=pl.BlockSpec((None, tr, c), lambda i, *sc: (sc[4][0], i, 0)))
    mx, my = lax.axis_index("x"), lax.axis_index("y")
    scal = [v.astype(jnp.int32).reshape(1) for v in
            (2 * mx + my, 2 * (1 - mx) + my, 2 * mx + (1 - my), 2 * (1 - mx) + (1 - my), mc)]
    return pl.pallas_call(body, grid_spec=grid_spec, out_shape=jax.ShapeDtypeStruct((2, r, c), f32),
                          compiler_params=_cparams("parallel"), name=name)(*scal, own, arrived, arrived, arrived)


def sum_slots(name, x):
    r, c = x.shape[1:]
    tr = _row_tile(r)

    def body(x_ref, o_ref):
        o_ref[...] = (x_ref[0].astype(f32) + x_ref[1].astype(f32)) + (x_ref[2].astype(f32) + x_ref[3].astype(f32))

    return pl.pallas_call(
        body, grid=(r // tr,), in_specs=[pl.BlockSpec((N_CHIPS, tr, c), lambda i: (0, i, 0))],
        out_specs=pl.BlockSpec((tr, c), lambda i: (i, 0)), out_shape=jax.ShapeDtypeStruct((r, c), f32),
        compiler_params=_cparams("parallel"), name=name)(x)


def adamw(name, w, g, m, v):
    r, c = w.shape
    tr = _row_tile(r)

    def body(w_ref, g_ref, m_ref, v_ref, d_ref, nm_ref, nv_ref):
        gv = g_ref[...]
        nm = ADAM_B1 * m_ref[...] + (1.0 - ADAM_B1) * gv
        nv = ADAM_B2 * v_ref[...] + (1.0 - ADAM_B2) * (gv * gv)
        m_hat = nm / (1.0 - ADAM_B1 ** ADAM_STEP)
        v_hat = nv / (1.0 - ADAM_B2 ** ADAM_STEP)
        d_ref[...] = -ADAM_LR * (m_hat / (jnp.sqrt(v_hat) + ADAM_EPS) + ADAM_WD * w_ref[...])
        nm_ref[...] = nm
        nv_ref[...] = nv

    blk = pl.BlockSpec((tr, c), lambda i: (i, 0))
    return pl.pallas_call(
        body, grid=(r // tr,), in_specs=[blk] * 4, out_specs=[blk] * 3,
        out_shape=[jax.ShapeDtypeStruct((r, c), f32)] * 3,
        compiler_params=_cparams("parallel"), name=name)(w, g, m, v)


def _pack_rows(n):
    return -(-n // PACK_COLS)


def _pack(arrs, dtype, row_multiple):
    segs = []
    for a in arrs:
        flat = a.astype(dtype).reshape(-1)
        k = _pack_rows(flat.shape[0])
        segs.append(jnp.pad(flat, (0, k * PACK_COLS - flat.shape[0])).reshape(k, PACK_COLS))
    rows = sum(s.shape[0] for s in segs)
    pad = -rows % row_multiple
    if pad:
        segs.append(jnp.zeros((pad, PACK_COLS), dtype))
    return jnp.concatenate(segs, 0)


def _unpack(packed, shapes):
    out, r = [], 0
    for shp in shapes:
        n = math.prod(shp)
        k = _pack_rows(n)
        out.append(packed[r:r + k].reshape(-1)[:n].reshape(shp))
        r += k
    return out


def _gathered_to_full(g, axis):
    t = jnp.moveaxis(g, 0, axis)
    return t.reshape(t.shape[:axis] + (t.shape[axis] * t.shape[axis + 1],) + t.shape[axis + 2:])


def _full_to_shard_major(full, axis):
    shp = full.shape
    t = full.reshape(shp[:axis] + (N_CHIPS, shp[axis] // N_CHIPS) + shp[axis + 1:])
    return jnp.moveaxis(t, axis, 0)


def kernel(x, mem, positions, hyb_w_in, dn_conv_w, dn_a_log, dn_dt_bias, dn_norm_g, hyb_w_out, s5_a_re, s5_a_im, s5_log_dt, s5_b_re, s5_b_im, s5_c_re, s5_c_im, s5_d, s5_glu_wo, s5_glu_wg, ln_mix_g, ln_mix_b, xq_w, xk_w, xv_w, xo_w, ln_x_g, ln_x_b, ffn_wg, ffn_wu, ffn_wd, ln_ffn_g, ln_ffn_b, loss_target, m_hyb_w_in, m_dn_conv_w, m_dn_a_log, m_dn_dt_bias, m_dn_norm_g, m_hyb_w_out, m_s5_a_re, m_s5_a_im, m_s5_log_dt, m_s5_b_re, m_s5_b_im, m_s5_c_re, m_s5_c_im, m_s5_d, m_s5_glu_wo, m_s5_glu_wg, m_ln_mix_g, m_ln_mix_b, m_xq_w, m_xk_w, m_xv_w, m_xo_w, m_ln_x_g, m_ln_x_b, m_ffn_wg, m_ffn_wu, m_ffn_wd, m_ln_ffn_g, m_ln_ffn_b, v_hyb_w_in, v_dn_conv_w, v_dn_a_log, v_dn_dt_bias, v_dn_norm_g, v_hyb_w_out, v_s5_a_re, v_s5_a_im, v_s5_log_dt, v_s5_b_re, v_s5_b_im, v_s5_c_re, v_s5_c_im, v_s5_d, v_s5_glu_wo, v_s5_glu_wg, v_ln_mix_g, v_ln_mix_b, v_xq_w, v_xk_w, v_xv_w, v_xo_w, v_ln_x_g, v_ln_x_b, v_ffn_wg, v_ffn_wu, v_ffn_wd, v_ln_ffn_g, v_ln_ffn_b):
    a = dict(locals())
    big = [n for n in WEIGHT_NAMES if n in SHARD_AXIS and n not in GATHER_F32]
    small = [n for n in WEIGHT_NAMES if n not in big]
    chip = 2 * lax.axis_index("x") + lax.axis_index("y")
    for n in FFN_TRANSPOSED:
        for pre in ("", "m_", "v_"):
            a[pre + n] = jnp.swapaxes(a[pre + n], 1, 2)

    mc = lax.axis_index("c")
    view2 = lambda t: t.reshape(-1, t.shape[-1])
    slots = [cast_into_slot("slot_" + n, view2(a[n]), chip, bf16).reshape((N_CHIPS,) + a[n].shape) for n in big]
    gathered = comm_gather_weights("gather_w", slots)
    tiny4 = _unpack_slots(comm_allgather4("gather_w_tiny", _pack([a[n] for n in GATHER_F32], f32, 8)),
                          [a[n].shape for n in GATHER_F32])
    p = {n: a[n] for n in small if n not in GATHER_F32}
    for n, g4 in zip(GATHER_F32, tiny4):
        p[n] = _gathered_to_full(g4, SHARD_AXIS[n])
    for n, g4 in zip(big, gathered):
        p[n] = g4 if n in SHARD_ORDER_GRADS else _gathered_to_full(g4, SHARD_AXIS[n])

    loss, grad_x, grads = local_step(x[0], mem[0], positions, loss_target[0], p)
    loss = lax.psum(loss, ("x", "y", "c"))

    g4s = [grads[n] for n in big]
    recv = comm_sibling_halves("rs_sibling_halves", g4s)
    pairs = []
    for n, g4, r4 in zip(big, g4s, recv):
        lh, cols = g4.shape[1] // 2, g4.shape[-1]
        v4 = g4.reshape(N_CHIPS, 2, -1, cols)
        pairs.append(add_own_half("rs_add_" + n, v4, r4.reshape(N_CHIPS, -1, cols), bf16).reshape((N_CHIPS, lh) + g4.shape[2:]))
    arrived = comm_alltoall4("rs_alltoall", pairs)
    slot3 = lambda t: t.reshape(N_CHIPS, -1, t.shape[-1])
    halves = [sum_chips_into_half("rs_sum_" + n, slot3(pr), slot3(ar), chip, mc) for n, pr, ar in zip(big, pairs, arrived)]
    g_big = {n: t.reshape(a[n].shape) for n, t in zip(big, comm_sibling_join("rs_sibling_join", halves))}

    rpack = _pack([grads[n] for n in small], f32, 8)
    rpair = _add2("ar_add_sibling", rpack, comm_sibling_swap("ar_sibling_swap", rpack))
    g_small = _unpack(sum_slots("ar_sum_chips", comm_allgather4("ar_allgather", rpair)), [grads[n].shape for n in small])
    g_small = {n: (lax.dynamic_index_in_dim(_full_to_shard_major(g, SHARD_AXIS[n]), chip, 0, keepdims=False)
                   if n in SHARD_AXIS else g) for n, g in zip(small, g_small)}

    outs = {}
    for n in big:
        view = lambda t: t.reshape(-1, t.shape[-1])
        d, nm, nv = adamw("adamw_" + n, view(a[n]), view(g_big[n]), view(a["m_" + n]), view(a["v_" + n]))
        outs[n] = (g_big[n],) + tuple(t.reshape(a[n].shape) for t in (d, nm, nv))
    shapes = [a[n].shape for n in small]
    packs = [_pack([a[pre + n] for n in small], f32, 8) for pre in ("", "m_", "v_")]
    upd = adamw("adamw_small", packs[0], _pack([g_small[n] for n in small], f32, 8), packs[1], packs[2])
    for k, n in enumerate(small):
        outs[n] = (g_small[n],) + tuple(_unpack(buf, shapes)[k] for buf in upd)
    for n in FFN_TRANSPOSED:
        outs[n] = tuple(jnp.swapaxes(t, 1, 2) for t in outs[n])
    res = [loss, grad_x[None]]
    for kind in range(4):
        res += [outs[n][kind] for n in WEIGHT_NAMES]
    return tuple(res)


def _unpack_slots(gathered, shapes):
    out, r = [], 0
    for shp in shapes:
        n = math.prod(shp)
        k = _pack_rows(n)
        out.append(gathered[:, r:r + k].reshape(N_CHIPS, -1)[:, :n].reshape((N_CHIPS,) + tuple(shp)))
        r += k
    return out
```

```python
import functools
import math

import jax
import jax.numpy as jnp
from jax import lax
from jax.experimental import pallas as pl
from jax.experimental.pallas import tpu as pltpu

f32 = jnp.float32
bf16 = jnp.bfloat16

D_MODEL = 1024
DEPTH = 4
DN_HEADS = 4
DN_HEAD_DIM = 128
DN_KEY_DIM = 512
DN_QKV_DIM = 1536
DN_CONV = 4
SW_HEADS = 8
SW_HEAD_DIM = 64
SW_DIM = 512
SW_DILATIONS = (1, 4, 16)
SW_BLOCK = 128
ROPE_THETA = 10000.0
S5_GROUP = 16
S5_GROUPS = 64
S5_STATE = 64
X_HEADS = 4
X_HEAD_DIM = 256
FFN_HIDDEN = 2816
ALPHA = (2 * DEPTH) ** 0.25
LN_EPS = 1e-5
RMS_EPS = 1e-6
ADAM_LR, ADAM_B1, ADAM_B2, ADAM_EPS, ADAM_WD, ADAM_STEP = 0.001, 0.9, 0.999, 1e-08, 0.01, 10

BA_PAD = 256
PROJ_COLS = DN_QKV_DIM + DN_KEY_DIM + 3 * SW_DIM + BA_PAD
COL_Z = DN_QKV_DIM
COL_SWQ = COL_Z + DN_KEY_DIM
COL_SWK = COL_SWQ + SW_DIM
COL_SWV = COL_SWK + SW_DIM
COL_BA = COL_SWV + SW_DIM

LANES = 128
SUBLANES = 8
VMEM_LIMIT = 56 * 1024 * 1024
DN_CHUNK = 128
DN_HEADS_PER_STEP = 4


def _cparams(*sem):
    return pltpu.CompilerParams(dimension_semantics=tuple(sem), vmem_limit_bytes=VMEM_LIMIT)


def _dg(x, y, cx, cy):
    return lax.dot_general(x, y, (((cx,), (cy,)), ((), ())), preferred_element_type=f32)


@functools.partial(jax.custom_vjp, nondiff_argnums=(2, 3))
def bdot(a, b, ca, cb):
    return _dg(a.astype(bf16), b.astype(bf16), ca, cb)


def _bdot_fwd(a, b, ca, cb):
    return bdot(a, b, ca, cb), (a, b)


def _bdot_bwd(ca, cb, res, g):
    a, b = res
    g16, a16, b16 = g.astype(bf16), a.astype(bf16), b.astype(bf16)
    da = _dg(g16, b16, 1, 1 - cb) if ca == 1 else _dg(b16, g16, 1 - cb, 1)
    db = _dg(a16, g16, 1 - ca, 0) if cb == 0 else _dg(g16, a16, 0, 1 - ca)
    return da.astype(a.dtype), db.astype(b.dtype)


bdot.defvjp(_bdot_fwd, _bdot_bwd)


def _split_hi_lo(a):
    hi = a.astype(bf16)
    return hi, (a - hi.astype(f32)).astype(bf16)


def _dot3(a, b, ca, cb):
    a_hi, a_lo = _split_hi_lo(a)
    b_hi, b_lo = _split_hi_lo(b)
    return _dg(a_hi, b_hi, ca, cb) + (_dg(a_hi, b_lo, ca, cb) + _dg(a_lo, b_hi, ca, cb))


def hdot(a, b):
    return jnp.dot(a, b, precision=lax.Precision.HIGHEST, preferred_element_type=f32)


def _iota2(shape, dim):
    return lax.broadcasted_iota(jnp.int32, shape, dim)


def _row_spec(r, tm):
    if isinstance(r, tuple):
        arr, width, blk = r
        return arr, pl.BlockSpec((tm, width), lambda i, _b=blk: (i, _b))
    return r, pl.BlockSpec((tm, r.shape[1]), lambda i: (i, 0))


def _par_spec(p):
    return pl.BlockSpec(p.shape, lambda i, _n=p.ndim: (0,) * _n)


def rowmap(name, fn, rows, params, out_cols, tm, out_dtypes=None):
    arrs, specs = zip(*[_row_spec(r, tm) for r in rows])
    s = arrs[0].shape[0]
    n_in = len(rows) + len(params)
    out_dtypes = out_dtypes or [f32] * len(out_cols)

    def body(*refs):
        outs = fn(*[r[...] for r in refs[:n_in]])
        for o_ref, o in zip(refs[n_in:], outs):
            o_ref[...] = o.astype(o_ref.dtype)

    return pl.pallas_call(
        body, grid=(s // tm,),
        in_specs=list(specs) + [_par_spec(p) for p in params],
        out_specs=[pl.BlockSpec((tm, c), lambda i: (i, 0)) for c in out_cols],
        out_shape=[jax.ShapeDtypeStruct((s, c), dt) for c, dt in zip(out_cols, out_dtypes)],
        compiler_params=_cparams("parallel"), name=name)(*arrs, *params)


def rowmap_bwd(name, fn, rows, params, cts, tm, row_mask=None, par_mask=None, row_dtypes=None):
    arrs, specs = zip(*[_row_spec(r, tm) for r in rows])
    s = arrs[0].shape[0]
    ct_groups = [c if isinstance(c, list) else [c] for c in cts]
    ct_arrs, ct_specs = zip(*[_row_spec(a, tm) for grp in ct_groups for a in grp])
    cts = list(ct_arrs)
    nr, npar, nct = len(rows), len(params), len(cts)
    row_mask = row_mask or [True] * nr
    par_mask = par_mask or [True] * npar
    row_idx = [k for k in range(nr) if row_mask[k]]
    par_idx = [k for k in range(npar) if par_mask[k]]
    row_w = [specs[k].block_shape[1] for k in row_idx]

    def body(*refs):
        ins = [r[...] for r in refs[:nr + npar]]
        ct_refs = list(refs[nr + npar:nr + npar + nct])
        ctv = []
        for grp in ct_groups:
            acc = ct_refs.pop(0)[...].astype(f32)
            for _ in grp[1:]:
                acc = acc + ct_refs.pop(0)[...].astype(f32)
            ctv.append(acc)
        ctv = tuple(ctv)
        outs = refs[nr + npar + nct:]
        _, vjp = jax.vjp(fn, *ins)
        grads = vjp(ctv)
        for o_ref, k in zip(outs[:len(row_idx)], row_idx):
            o_ref[...] = grads[k].astype(o_ref.dtype)
        first = pl.program_id(0) == 0
        for o_ref, k in zip(outs[len(row_idx):], par_idx):
            g = grads[nr + k].astype(f32)

            @pl.when(first)
            def _(o_ref=o_ref, g=g):
                o_ref[...] = g

            @pl.when(jnp.logical_not(first))
            def _(o_ref=o_ref, g=g):
                o_ref[...] += g

    res = pl.pallas_call(
        body, grid=(s // tm,),
        in_specs=list(specs) + [_par_spec(p) for p in params]
        + list(ct_specs),
        out_specs=[pl.BlockSpec((tm, w), lambda i: (i, 0)) for w in row_w]
        + [_par_spec(params[k]) for k in par_idx],
        out_shape=[jax.ShapeDtypeStruct((s, w), dt) for w, dt in zip(row_w, row_dtypes or [f32] * len(row_w))]
        + [jax.ShapeDtypeStruct(params[k].shape, f32) for k in par_idx],
        compiler_params=_cparams("arbitrary"), name=name)(*arrs, *params, *cts)
    return list(res[:len(row_idx)]), list(res[len(row_idx):])


def _pick(n, prefs):
    for t in prefs:
        if n % t == 0:
            return t
    return n


MM_CHUNK = 512
MM_WIDE = 2048


def mm_nn(name, a, b, out_dtype=f32):
    m, k = a.shape
    n = b.shape[1]
    tm = _pick(m, ((1024,) if n <= MM_WIDE else ()) + (512, 256, 128))
    cn = _pick(n, (MM_CHUNK, 256, 128))

    def body(a_ref, b_ref, o_ref):
        av = a_ref[...].astype(bf16)
        for c in range(n // cn):
            sl = slice(c * cn, (c + 1) * cn)
            o_ref[:, sl] = _dg(av, b_ref[:, sl].astype(bf16), 1, 0).astype(o_ref.dtype)

    return pl.pallas_call(
        body, grid=(m // tm,),
        in_specs=[pl.BlockSpec((tm, k), lambda i: (i, 0)), pl.BlockSpec((k, n), lambda i: (0, 0))],
        out_specs=pl.BlockSpec((tm, n), lambda i: (i, 0)),
        out_shape=jax.ShapeDtypeStruct((m, n), out_dtype),
        compiler_params=_cparams("parallel"), name=name)(a, b)


def mm_nt(name, a, b, out_dtype=f32):
    m, n = a.shape
    k = b.shape[0]
    tm = _pick(m, ((1024,) if n <= MM_WIDE else ()) + (512, 256, 128))
    ck = _pick(k, (MM_CHUNK, 256, 128))

    def body(a_ref, b_ref, o_ref):
        av = a_ref[...].astype(bf16)
        for c in range(k // ck):
            sl = slice(c * ck, (c + 1) * ck)
            o_ref[:, sl] = _dg(av, b_ref[sl, :].astype(bf16), 1, 1).astype(o_ref.dtype)

    return pl.pallas_call(
        body, grid=(m // tm,),
        in_specs=[pl.BlockSpec((tm, n), lambda i: (i, 0)), pl.BlockSpec((k, n), lambda i: (0, 0))],
        out_specs=pl.BlockSpec((tm, k), lambda i: (i, 0)),
        out_shape=jax.ShapeDtypeStruct((m, k), out_dtype),
        compiler_params=_cparams("parallel"), name=name)(a, b)


def _stacked(buf):
    if buf is None:
        return [], [], {}
    return [buf], [pl.BlockSpec(memory_space=pl.ANY)], None


def mm_tn(name, a, b, out_dtype=f32, b_col0=0, n_cols=None, stack=None):
    s, m = a.shape
    n = n_cols or b.shape[1]
    tn = _pick(n, (256, 128))
    cm = _pick(m, (256, 128))
    col0 = b_col0 // tn
    in_specs = [pl.BlockSpec((s, m), lambda j: (0, 0)), pl.BlockSpec((s, tn), lambda j: (0, j + col0))]

    if stack is None:
        def body(a_ref, b_ref, o_ref):
            bv = b_ref[...].astype(bf16)
            for c in range(m // cm):
                sl = slice(c * cm, (c + 1) * cm)
                o_ref[sl, :] = _dg(a_ref[:, sl].astype(bf16), bv, 0, 0).astype(o_ref.dtype)

        return pl.pallas_call(
            body, grid=(n // tn,), in_specs=in_specs, out_specs=pl.BlockSpec((m, tn), lambda j: (0, j)),
            out_shape=jax.ShapeDtypeStruct((m, n), out_dtype),
            compiler_params=_cparams("parallel"), name=name)(a, b)

    buf, layer, n_layers = stack
    assert cm * N_CHIPS == m
    extra, extra_specs, _ = _stacked(buf)

    def body_stacked(a_ref, b_ref, *rest):
        o_ref = rest[-1]
        bv = b_ref[...].astype(bf16)
        for c in range(N_CHIPS):
            o_ref[c] = _dg(a_ref[:, c * cm:(c + 1) * cm].astype(bf16), bv, 0, 0).astype(o_ref.dtype)

    return pl.pallas_call(
        body_stacked, grid=(n // tn,), in_specs=in_specs + extra_specs,
        out_specs=pl.BlockSpec((N_CHIPS, None, cm, tn), lambda j: (0, layer, 0, j)),
        out_shape=jax.ShapeDtypeStruct((N_CHIPS, n_layers, cm, n), out_dtype),
        input_output_aliases={2: 0} if extra else {},
        compiler_params=_cparams("parallel"), name=name)(a, b, *extra)


def _postnorm_tile(h, sub, g, b):
    z = ALPHA * h + sub
    mu = jnp.mean(z, -1, keepdims=True)
    zc = z - mu
    var = jnp.mean(zc * zc, -1, keepdims=True)
    return (zc * lax.rsqrt(var + LN_EPS) * g + b,)


def _glu_tile(og):
    o, g = og[:, :D_MODEL], og[:, D_MODEL:]
    return (o * jax.nn.sigmoid(g),)


def _xattn_tile(q, kv):
    outs = []
    for h in range(X_HEADS):
        sl = slice(h * X_HEAD_DIM, (h + 1) * X_HEAD_DIM)
        s = bdot(q[:, sl], kv[:, sl], 1, 1) * (X_HEAD_DIM ** -0.5)
        m = lax.stop_gradient(jnp.max(s, -1, keepdims=True))
        p = jnp.exp(s - m)
        p = p / jnp.sum(p, -1, keepdims=True)
        outs.append(bdot(p, kv[:, D_MODEL + h * X_HEAD_DIM:D_MODEL + (h + 1) * X_HEAD_DIM], 1, 0))
    return (jnp.concatenate(outs, -1),)


TM_ROW = 512


def postnorm_fwd(tag, h, sub, g, b):
    return rowmap("postnorm_" + tag, lambda *a: _postnorm_tile(*a) * 2, [h, sub], [g, b], [D_MODEL] * 2, TM_ROW,
                  out_dtypes=[f32, bf16])


def postnorm_bwd(tag, h, sub, g, b, dy):
    (dh, dsub), (dg, db) = rowmap_bwd("postnorm_bwd_" + tag, _postnorm_tile, [h, sub], [g, b], [dy], TM_ROW,
                                      row_dtypes=[f32, bf16])
    return dh, dsub, dg, db


def xattn_fwd(tag, h, mem, wq, wkv, wo):
    q = mm_nn("xq_" + tag, h, wq, out_dtype=bf16)
    kv = mm_nn("xkv_" + tag, mem, wkv)
    ao = rowmap("xattn_" + tag, _xattn_tile, [q], [kv], [D_MODEL], TM_ROW, out_dtypes=[bf16])[0]
    out = mm_nn("xo_" + tag, ao, wo)
    return out, (q, kv, ao)


def xattn_bwd(tag, layer, h, mem, wq, wkv, wo, res, dout, stacks):
    q, kv, ao = res
    sq, sk, sv, so = stacks
    so = mm_tn("xo_dw_" + tag, ao, dout, stack=(so, layer, DEPTH))
    dao = mm_nt("xo_dx_" + tag, dout, wo)
    (dq,), (dkv,) = rowmap_bwd("xattn_bwd_" + tag, _xattn_tile, [q], [kv], [dao], TM_ROW, row_dtypes=[bf16])
    sq = mm_tn("xq_dw_" + tag, h, dq, stack=(sq, layer, DEPTH))
    dh = mm_nt("xq_dx_" + tag, dq, wq)
    sk = mm_tn("xk_dw_" + tag, mem, dkv, n_cols=D_MODEL, stack=(sk, layer, DEPTH))
    sv = mm_tn("xv_dw_" + tag, mem, dkv, b_col0=D_MODEL, n_cols=D_MODEL, stack=(sv, layer, DEPTH))
    return dh, (sq, sk, sv, so)


FFN_SHARD = FFN_HIDDEN // 4
TM_FFN = 512
TM_FFN_WIDE = 1024


def _silu_mul(a, u):
    return jax.nn.silu(a) * u


def ffn_fwd(tag, layer, h, wg, wu, wd):
    s = h.shape[0]
    tm, fs = TM_FFN, FFN_SHARD
    w_in = pl.BlockSpec((None, None, fs, D_MODEL), lambda k, i: (k, layer, 0, 0))
    tu = TM_FFN_WIDE
    act = pl.BlockSpec((None, tu, fs), lambda k, i: (k, i, 0))

    def up_body(h_ref, wg_ref, wu_ref, a_ref, u_ref, hid_ref):
        hv = h_ref[...].astype(bf16)
        a, u = _dg(hv, wg_ref[...], 1, 1), _dg(hv, wu_ref[...], 1, 1)
        a_ref[...], u_ref[...] = a.astype(bf16), u.astype(bf16)
        hid_ref[...] = _silu_mul(a, u).astype(bf16)

    a4, u4, hid4 = pl.pallas_call(
        up_body, grid=(4, s // tu),
        in_specs=[pl.BlockSpec((tu, D_MODEL), lambda k, i: (i, 0)), w_in, w_in],
        out_specs=[act, act, act],
        out_shape=[jax.ShapeDtypeStruct((4, s, fs), bf16)] * 3,
        compiler_params=_cparams("parallel", "parallel"), name="ffn_up_" + tag)(h, wg, wu)

    all_act = pl.BlockSpec((4, tm, fs), lambda i: (0, i, 0))
    all_w = pl.BlockSpec((4, None, fs, D_MODEL), lambda i: (0, layer, 0, 0))

    def down_body(hid_ref, wd_ref, o_ref):
        acc = _dg(hid_ref[0], wd_ref[0], 1, 0)
        for k in range(1, 4):
            acc = acc + _dg(hid_ref[k], wd_ref[k], 1, 0)
        o_ref[...] = acc

    out = pl.pallas_call(
        down_body, grid=(s // tm,), in_specs=[all_act, all_w],
        out_specs=pl.BlockSpec((tm, D_MODEL), lambda i: (i, 0)),
        out_shape=jax.ShapeDtypeStruct((s, D_MODEL), f32),
        compiler_params=_cparams("parallel"), name="ffn_down_" + tag)(hid4, wd)
    return out, (a4, u4, hid4)


def ffn_bwd(tag, layer, h, wg, wu, wd, res, dout, stacks=None):
    a4, u4, hid4 = res
    s = h.shape[0]
    tm, fs = TM_FFN, FFN_SHARD
    tu = TM_FFN_WIDE
    act = pl.BlockSpec((None, tu, fs), lambda k, i: (k, i, 0))

    def dact_body(do_ref, wd_ref, a_ref, u_ref, da_ref, du_ref):
        dhid = _dg(do_ref[...].astype(bf16), wd_ref[...], 1, 1)
        _, vjp = jax.vjp(_silu_mul, a_ref[...].astype(f32), u_ref[...].astype(f32))
        da, du = vjp(dhid)
        da_ref[...], du_ref[...] = da.astype(bf16), du.astype(bf16)

    da4, du4 = pl.pallas_call(
        dact_body, grid=(4, s // tu),
        in_specs=[pl.BlockSpec((tu, D_MODEL), lambda k, i: (i, 0)),
                  pl.BlockSpec((None, None, fs, D_MODEL), lambda k, i: (k, layer, 0, 0)), act, act],
        out_specs=[act, act], out_shape=[jax.ShapeDtypeStruct((4, s, fs), bf16)] * 2,
        compiler_params=_cparams("parallel", "parallel"), name="ffn_dact_" + tag)(dout, wd, a4, u4)

    all_act = pl.BlockSpec((4, tm, fs), lambda i: (0, i, 0))
    all_w = pl.BlockSpec((4, None, fs, D_MODEL), lambda i: (0, layer, 0, 0))

    def dx_body(da_ref, du_ref, wg_ref, wu_ref, o_ref):
        acc = _dg(da_ref[0], wg_ref[0], 1, 0) + _dg(du_ref[0], wu_ref[0], 1, 0)
        for k in range(1, 4):
            acc = acc + (_dg(da_ref[k], wg_ref[k], 1, 0) + _dg(du_ref[k], wu_ref[k], 1, 0))
        o_ref[...] = acc

    dh = pl.pallas_call(
        dx_body, grid=(s // tm,), in_specs=[all_act, all_act, all_w, all_w],
        out_specs=pl.BlockSpec((tm, D_MODEL), lambda i: (i, 0)),
        out_shape=jax.ShapeDtypeStruct((s, D_MODEL), f32),
        compiler_params=_cparams("parallel"), name="ffn_dx_" + tag)(da4, du4, wg, wu)

    tn = 256
    whole = pl.BlockSpec((None, s, fs), lambda k: (k, 0, 0))
    resident = pl.BlockSpec((s, D_MODEL), lambda k: (0, 0), pipeline_mode=pl.Buffered(1))

    def dwin_body(h_ref, da_ref, du_ref, *rest):
        dwg_ref, dwu_ref = rest[-2:]
        da, du = da_ref[...], du_ref[...]
        for c in range(D_MODEL // tn):
            sl = slice(c * tn, (c + 1) * tn)
            hv = h_ref[:, sl].astype(bf16)
            dwg_ref[:, sl] = _dg(da, hv, 0, 0)
            dwu_ref[:, sl] = _dg(du, hv, 0, 0)

    n_layers = wd.shape[1]
    layer_out = pl.BlockSpec((None, None, fs, D_MODEL), lambda k: (k, layer, 0, 0))
    stack_shape = jax.ShapeDtypeStruct((4, n_layers, fs, D_MODEL), f32)
    prev = [] if stacks is None else list(stacks)
    any_spec = [pl.BlockSpec(memory_space=pl.ANY)]

    dwg, dwu = pl.pallas_call(
        dwin_body, grid=(4,),
        in_specs=[resident, whole, whole] + any_spec * len(prev[:2]),
        out_specs=[layer_out] * 2, out_shape=[stack_shape] * 2,
        input_output_aliases={3: 0, 4: 1} if prev else {},
        compiler_params=_cparams("parallel"), name="ffn_dwin_" + tag)(h, da4, du4, *prev[:2])

    def dwd_body(hid_ref, do_ref, *rest):
        hid = hid_ref[...]
        for c in range(D_MODEL // tn):
            sl = slice(c * tn, (c + 1) * tn)
            rest[-1][:, sl] = _dg(hid, do_ref[:, sl].astype(bf16), 0, 0)

    dwd = pl.pallas_call(
        dwd_body, grid=(4,),
        in_specs=[whole, resident] + any_spec * len(prev[2:]),
        out_specs=layer_out, out_shape=stack_shape,
        input_output_aliases={2: 0} if prev else {},
        compiler_params=_cparams("parallel"), name="ffn_dwd_" + tag)(hid4, dout, *prev[2:])
    return dh, (dwg, dwu, dwd)


def loss_head(y, target):
    s, d = y.shape
    tm = TM_ROW

    def body(y_ref, t_ref, part_ref, dy_ref):
        e = y_ref[...] - t_ref[...]
        dy_ref[...] = e * (1.0 / d)
        p = jnp.sum(e * e, 0, keepdims=True) * (0.5 / d)

        @pl.when(pl.program_id(0) == 0)
        def _():
            part_ref[...] = p

        @pl.when(pl.program_id(0) != 0)
        def _():
            part_ref[...] += p

    return pl.pallas_call(
        body, grid=(s // tm,),
        in_specs=[pl.BlockSpec((tm, d), lambda i: (i, 0))] * 2,
        out_specs=[pl.BlockSpec((1, d), lambda i: (0, 0)), pl.BlockSpec((tm, d), lambda i: (i, 0))],
        out_shape=[jax.ShapeDtypeStruct((1, d), f32), jax.ShapeDtypeStruct((s, d), f32)],
        compiler_params=_cparams("arbitrary"), name="loss_head")(y, target)


TM_CONV = 1024


def _conv_rows(xx, w_ref, n_rows):
    a = w_ref[3:4, :] * xx
    for k in (1, 2, 3):
        a = a + w_ref[3 - k:4 - k, :] * pltpu.roll(xx, k, 0)
    return a


def _dn_act(a, is_qk):
    s = jax.nn.silu(a)
    n = s * lax.rsqrt(jnp.sum(s * s, -1, keepdims=True) + RMS_EPS)
    return jnp.where(is_qk, n, s)


def dn_conv_fwd(tag, proj, cw):
    s = proj.shape[0]
    tm, hb = TM_CONV, TM_CONV // SUBLANES

    def body(xh_ref, x_ref, w_ref, o_ref):
        j, t = pl.program_id(0), pl.program_id(1)
        halo = jnp.where(t > 0, xh_ref[...], 0.0)
        xx = jnp.concatenate([halo, x_ref[...]], 0)
        a = _conv_rows(xx, w_ref, tm + SUBLANES)
        o_ref[...] = _dn_act(a, j < 2 * DN_HEADS)[SUBLANES:, :]

    return pl.pallas_call(
        body, grid=(DN_QKV_DIM // LANES, s // tm),
        in_specs=[pl.BlockSpec((SUBLANES, LANES), lambda j, t: (jnp.maximum(t * hb - 1, 0), j)),
                  pl.BlockSpec((tm, LANES), lambda j, t: (t, j)),
                  pl.BlockSpec((DN_CONV, LANES), lambda j, t: (0, j))],
        out_specs=pl.BlockSpec((tm, LANES), lambda j, t: (t, j)),
        out_shape=jax.ShapeDtypeStruct((s, DN_QKV_DIM), f32),
        compiler_params=_cparams("parallel", "parallel"), name="dn_conv_" + tag)(proj, proj, cw)


def dn_conv_bwd(tag, proj, cw, dy):
    s = proj.shape[0]
    tm, hb = TM_CONV, TM_CONV // SUBLANES
    nt = s // tm
    n_ext = tm + 2 * SUBLANES

    def body(xb_ref, x_ref, xa_ref, dy_ref, dya_ref, w_ref, dx_ref, dw_ref):
        j, t = pl.program_id(0), pl.program_id(1)
        xx = jnp.concatenate([jnp.where(t > 0, xb_ref[...], 0.0), x_ref[...],
                              jnp.where(t < nt - 1, xa_ref[...], 0.0)], 0)
        dyy = jnp.concatenate([jnp.zeros((SUBLANES, LANES), f32), dy_ref[...],
                               jnp.where(t < nt - 1, dya_ref[...], 0.0)], 0)
        a = _conv_rows(xx, w_ref, n_ext)
        _, vjp = jax.vjp(lambda v: _dn_act(v, j < 2 * DN_HEADS), a)
        da, = vjp(dyy)
        dx = w_ref[3:4, :] * da
        for k in (1, 2, 3):
            dx = dx + w_ref[3 - k:4 - k, :] * pltpu.roll(da, n_ext - k, 0)
        dx_ref[...] = dx[SUBLANES:SUBLANES + tm, :]
        row = _iota2((n_ext, LANES), 0)
        da_in = jnp.where((row >= SUBLANES) & (row < SUBLANES + tm), da, 0.0)
        r8 = _iota2((SUBLANES, LANES), 0)
        dw = jnp.zeros((SUBLANES, LANES), f32)
        for k in range(DN_CONV):
            xs = xx if k == 0 else pltpu.roll(xx, k, 0)
            dw = dw + jnp.where(r8 == 3 - k, jnp.sum(da_in * xs, 0, keepdims=True), 0.0)

        @pl.when(t == 0)
        def _():
            dw_ref[...] = dw

        @pl.when(t != 0)
        def _():
            dw_ref[...] += dw

    nb8 = s // SUBLANES
    return pl.pallas_call(
        body, grid=(DN_QKV_DIM // LANES, nt),
        in_specs=[pl.BlockSpec((SUBLANES, LANES), lambda j, t: (jnp.maximum(t * hb - 1, 0), j)),
                  pl.BlockSpec((tm, LANES), lambda j, t: (t, j)),
                  pl.BlockSpec((SUBLANES, LANES), lambda j, t: (jnp.minimum((t + 1) * hb, nb8 - 1), j)),
                  pl.BlockSpec((tm, LANES), lambda j, t: (t, j)),
                  pl.BlockSpec((SUBLANES, LANES), lambda j, t: (jnp.minimum((t + 1) * hb, nb8 - 1), j)),
                  pl.BlockSpec((DN_CONV, LANES), lambda j, t: (0, j))],
        out_specs=[pl.BlockSpec((tm, LANES), lambda j, t: (t, j)),
                   pl.BlockSpec((SUBLANES, LANES), lambda j, t: (0, j))],
        out_shape=[jax.ShapeDtypeStruct((s, DN_QKV_DIM), f32), jax.ShapeDtypeStruct((SUBLANES, DN_QKV_DIM), f32)],
        compiler_params=_cparams("parallel", "arbitrary"), name="dn_conv_bwd_" + tag)(proj, proj, proj, dy, dy, cw)


def _gate_tile(ba, eb, ea, alog, dtb):
    beta = jax.nn.sigmoid(hdot(ba, eb))
    g = -jnp.exp(alog) * jax.nn.softplus(hdot(ba, ea) + dtb)
    return beta, g


def _each(fn, *lists):
    return [fn(*args) for args in zip(*lists)]


@functools.partial(jax.custom_vjp, nondiff_argnums=(1,))
def _halves(x, axis):
    h = x.shape[axis] // 2
    return (x[:h], x[h:]) if axis == 0 else (x[:, :h], x[:, h:])


def _halves_fwd(x, axis):
    return _halves(x, axis), None


def _halves_bwd(axis, _, g):
    return (jnp.concatenate(g, axis),)


_halves.defvjp(_halves_fwd, _halves_bwd)


def _tri_inv_unit(lowers):
    c = lowers[0].shape[0]
    r, col = _iota2((c, c), 0), _iota2((c, c), 1)
    eye = jnp.where(r == col, 1.0, 0.0).astype(f32)
    invs = None
    sh = 0
    while (1 << sh) < c:
        same_2b = lax.shift_right_logical(r, sh + 1) == lax.shift_right_logical(col, sh + 1)
        diff_b = lax.shift_right_logical(r, sh) != lax.shift_right_logical(col, sh)
        offs = [jnp.where(same_2b & diff_b, low, 0.0) for low in lowers]
        if invs is None:
            invs = [eye - off for off in offs]
        else:
            part = _each(lambda inv, off: _dot3(inv, off, 1, 0), invs, offs)
            invs = _each(lambda inv, p: inv - _dot3(p, inv, 1, 0), invs, part)
        sh += 1
    return invs


@jax.custom_vjp
def _known_inverse(lower, tinv):
    return tinv


def _known_inverse_fwd(lower, tinv):
    return tinv, tinv


def _known_inverse_bwd(tinv, g):
    tt = tinv.T
    return -hdot(hdot(tt, g), tt), jnp.zeros_like(tinv)


_known_inverse.defvjp(_known_inverse_fwd, _known_inverse_bwd)


def _delta_chunk(q, k, v, gb, betab, state, tinv_known=None):
    c, hd = DN_CHUNK, DN_HEAD_DIM
    r, col = _iota2((c, c), 0), _iota2((c, c), 1)
    causal, strict = r >= col, r > col
    tril = jnp.where(causal, 1.0, 0.0).astype(f32)
    gc = _each(lambda g: hdot(tril, g), gb)
    decay = _each(lambda g: jnp.where(causal, jnp.exp(jnp.where(causal, g - g.T, 0.0)), 0.0), gc)
    qs = _each(lambda t: t * (DN_HEAD_DIM ** -0.5), q)
    kb = _each(lambda a, b: a * b, k, betab)
    kq = _each(lambda a, b, kk: _halves(bdot(jnp.concatenate([a, b], 0), kk, 1, 1), 0), kb, qs, k)
    lower = _each(lambda x, d: jnp.where(strict, x[0], 0.0) * d, kq, decay)
    intra = _each(lambda x, d: x[1] * d, kq, decay)
    tinv = _tri_inv_unit(lower) if tinv_known is None else _each(_known_inverse, lower, tinv_known)
    eg = _each(jnp.exp, gc)
    uw = _each(lambda t, vv, b, kb_, e: _halves(hdot(t, jnp.concatenate([vv * b, kb_ * e], 1)), 1),
               tinv, v, betab, kb, eg)
    gl = _each(lambda g: jnp.sum(jnp.where(r == c - 1, g, 0.0), 0, keepdims=True), gc)
    k_dec = _each(lambda kk, a, g: kk * jnp.exp(a - g), k, gl, gc)
    ws = _each(lambda x, t, e, st: _halves(bdot(jnp.concatenate([x[1], t * e], 0), st, 1, 0), 0), uw, qs, eg, state)
    v_new = _each(lambda x, y: x[0] - y[0], uw, ws)
    out = _each(lambda y, a, vn: y[1] + bdot(a, vn, 1, 0), ws, intra, v_new)
    new_state = _each(lambda st, a, kd, vn: st * jnp.exp(a) + bdot(kd, vn, 0, 0), state, gl, k_dec, v_new)
    return tuple(out), tuple(new_state), tuple(tinv)


def delta_fwd(tag, qkv, gb, betab):
    s = qkv.shape[0]
    c, hd = DN_CHUNK, DN_HEAD_DIM
    n = s // c

    hg, ng = DN_HEADS_PER_STEP, DN_HEADS // DN_HEADS_PER_STEP

    def body(q_ref, k_ref, v_ref, g_ref, b_ref, o_ref, st_ref, ti_ref, state):
        @pl.when(pl.program_id(1) == 0)
        def _():
            state[...] = jnp.zeros_like(state)

        heads = lambda ref: tuple(ref[:, j * hd:(j + 1) * hd] for j in range(hg))
        st = tuple(state[j] for j in range(hg))
        outs, news, tinv = _delta_chunk(heads(q_ref), heads(k_ref), heads(v_ref), heads(g_ref), heads(b_ref), st)
        for j in range(hg):
            st_ref[j] = st[j]
            ti_ref[j] = tinv[j]
            o_ref[:, j * hd:(j + 1) * hd] = outs[j]
            state[j] = news[j]

    blk = lambda off: pl.BlockSpec((c, hg * hd), lambda h, i, _o=off: (i, h + _o))
    per_chunk = pl.BlockSpec((hg, None, hd, hd), lambda h, i: (h, i, 0, 0))
    return pl.pallas_call(
        body, grid=(ng, n),
        in_specs=[blk(0), blk(ng), blk(2 * ng), blk(0), blk(0)],
        out_specs=[blk(0), per_chunk, per_chunk],
        out_shape=[jax.ShapeDtypeStruct((s, DN_KEY_DIM), f32)] + [jax.ShapeDtypeStruct((DN_HEADS, n, hd, hd), f32)] * 2,
        scratch_shapes=[pltpu.VMEM((hg, hd, hd), f32)],
        compiler_params=_cparams("parallel", "arbitrary"), name="delta_" + tag)(qkv, qkv, qkv, gb, betab)


def delta_bwd(tag, qkv, gb, betab, states, tinvs, do):
    s = qkv.shape[0]
    c, hd = DN_CHUNK, DN_HEAD_DIM
    n = s // c

    hg, ng = DN_HEADS_PER_STEP, DN_HEADS // DN_HEADS_PER_STEP

    def body(q_ref, k_ref, v_ref, g_ref, b_ref, st_ref, ti_ref, do_ref, dqkv_ref, dg_ref, db_ref, dstate):
        @pl.when(pl.program_id(1) == 0)
        def _():
            dstate[...] = jnp.zeros_like(dstate)

        heads = lambda ref: tuple(ref[:, j * hd:(j + 1) * hd] for j in range(hg))
        tinv = tuple(ti_ref[j] for j in range(hg))
        _, vjp = jax.vjp(lambda *args: _delta_chunk(*args, tinv_known=tinv)[:2],
                         heads(q_ref), heads(k_ref), heads(v_ref), heads(g_ref), heads(b_ref),
                         tuple(st_ref[j] for j in range(hg)))
        grads = vjp((heads(do_ref), tuple(dstate[j] for j in range(hg))))
        for part, g in enumerate(grads[:3]):
            for j in range(hg):
                dqkv_ref[:, part * DN_KEY_DIM + j * hd:part * DN_KEY_DIM + (j + 1) * hd] = g[j]
        for ref, g in zip((dg_ref, db_ref), grads[3:5]):
            for j in range(hg):
                ref[:, j * hd:(j + 1) * hd] = g[j]
        for j in range(hg):
            dstate[j] = grads[5][j]

    assert ng == 1
    blk = lambda off: pl.BlockSpec((c, hg * hd), lambda h, i, _o=off: (n - 1 - i, h + _o))
    return pl.pallas_call(
        body, grid=(ng, n),
        in_specs=[blk(0), blk(ng), blk(2 * ng), blk(0), blk(0)]
        + [pl.BlockSpec((hg, None, hd, hd), lambda h, i: (h, n - 1 - i, 0, 0))] * 2 + [blk(0)],
        out_specs=[pl.BlockSpec((c, DN_QKV_DIM), lambda h, i: (n - 1 - i, 0)), blk(0), blk(0)],
        out_shape=[jax.ShapeDtypeStruct((s, DN_QKV_DIM), f32)] + [jax.ShapeDtypeStruct((s, DN_KEY_DIM), f32)] * 2,
        scratch_shapes=[pltpu.VMEM((hg, hd, hd), f32)],
        compiler_params=_cparams("parallel", "arbitrary"),
        name="delta_bwd_" + tag)(qkv, qkv, qkv, gb, betab, states, tinvs, do)


def _dn_out_tile(o, z, ng):
    outs = []
    for h in range(DN_HEADS):
        sl = slice(h * DN_HEAD_DIM, (h + 1) * DN_HEAD_DIM)
        oh = o[:, sl]
        nrm = oh * lax.rsqrt(jnp.mean(oh * oh, -1, keepdims=True) + RMS_EPS) * ng[:, sl]
        outs.append(nrm * jax.nn.silu(z[:, sl]))
    return (jnp.concatenate(outs, -1),)


def _head_selectors():
    r, c = _iota2((BA_PAD, DN_KEY_DIM), 0), _iota2((BA_PAD, DN_KEY_DIM), 1) // DN_HEAD_DIM
    return (r == c).astype(f32), (r == c + DN_HEADS).astype(f32)


def dn_mixer_fwd(tag, proj, cw, alog_b, dtb_b, ng_b):
    eb, ea = _head_selectors()
    ba = (proj, BA_PAD, COL_BA // BA_PAD)
    qkv = dn_conv_fwd(tag, proj, cw)
    betab, gb = rowmap("dn_gate_" + tag, _gate_tile, [ba], [eb, ea, alog_b, dtb_b], [DN_KEY_DIM] * 2, TM_ROW)
    o, states, tinvs = delta_fwd(tag, qkv, gb, betab)
    z = (proj, DN_KEY_DIM, COL_Z // DN_KEY_DIM)
    a_out = rowmap("dn_out_" + tag, _dn_out_tile, [o, z], [ng_b], [DN_KEY_DIM], TM_ROW)[0]
    return a_out, (qkv, betab, gb, o, states, tinvs)


def dn_mixer_bwd(tag, proj, cw, alog_b, dtb_b, ng_b, res, da_out):
    qkv, betab, gb, o, states, tinvs = res
    eb, ea = _head_selectors()
    ba = (proj, BA_PAD, COL_BA // BA_PAD)
    z = (proj, DN_KEY_DIM, COL_Z // DN_KEY_DIM)
    (do, dz), (dng,) = rowmap_bwd("dn_out_bwd_" + tag, _dn_out_tile, [o, z], [ng_b], [da_out], TM_ROW)
    dqkv, dgb, dbetab = delta_bwd(tag, qkv, gb, betab, states, tinvs, do)
    dqkv_raw, dcw = dn_conv_bwd(tag, proj, cw, dqkv)
    (dba,), (dalog, ddtb) = rowmap_bwd("dn_gate_bwd_" + tag, _gate_tile, [ba], [eb, ea, alog_b, dtb_b],
                                       [dbetab, dgb], TM_ROW, par_mask=[False, False, True, True])
    return dqkv_raw, dz, dba, dcw[:DN_CONV], dalog, ddtb, dng


def _swap_halves(x):
    n = x.shape[1]
    first = (_iota2((1, n), 1) % SW_HEAD_DIM) < SW_HEAD_DIM // 2
    return jnp.where(first, pltpu.roll(x, n - SW_HEAD_DIM // 2, 1), pltpu.roll(x, SW_HEAD_DIM // 2, 1))


def _rope_apply(x, cos, sin_signed):
    return x * cos + _swap_halves(x) * sin_signed


def _rope_transpose(dy, cos, sin_signed):
    return dy * cos + _swap_halves(dy * sin_signed)


def rope_tables(positions, s):
    half = SW_HEAD_DIM // 2
    inv_freq = ROPE_THETA ** (-jnp.arange(0, SW_HEAD_DIM, 2, dtype=f32) / SW_HEAD_DIM)
    ang = positions.reshape(s, 1).astype(f32) * inv_freq[None, :]
    cos, sin = jnp.cos(ang), jnp.sin(ang)
    cos_t = jnp.tile(jnp.concatenate([cos, cos], 1), (1, SW_HEADS))
    sin_t = jnp.tile(jnp.concatenate([-sin, sin], 1), (1, SW_HEADS))
    assert cos_t.shape == (s, SW_DIM) and half * 2 == SW_HEAD_DIM
    return cos_t, sin_t


def rope_fwd(tag, proj, cos, sin):
    def fn(q, k, v, c, sg):
        return _rope_apply(q, c, sg), _rope_apply(k, c, sg), v

    rows = [(proj, SW_DIM, COL_SWQ // SW_DIM), (proj, SW_DIM, COL_SWK // SW_DIM), (proj, SW_DIM, COL_SWV // SW_DIM), cos, sin]
    return rowmap("rope_" + tag, fn, rows, [], [SW_DIM] * 3, TM_ROW, out_dtypes=[bf16] * 3)


def _swa_block(q, kp, kc, vp, vc, first):
    blk = SW_BLOCK
    kk = jnp.concatenate([kp, kc], 0)
    vv = jnp.concatenate([vp, vc], 0)
    dist = (_iota2((blk, 2 * blk), 0) + blk) - _iota2((blk, 2 * blk), 1)
    kj = _iota2((blk, 2 * blk), 1)
    valid = (dist >= 0) & (dist <= blk) & ((kj >= blk) | jnp.logical_not(first))
    lane_head = _iota2((1, LANES), 1) // SW_HEAD_DIM
    outs, lses = [], []
    for p in range(SW_DIM // LANES):
        sl = slice(p * LANES, (p + 1) * LANES)
        qp, kp_, vp_ = q[:, sl], kk[:, sl], vv[:, sl]
        o_pair = jnp.zeros((blk, LANES), f32)
        l_pair = jnp.zeros((blk, LANES), f32)
        for e in range(LANES // SW_HEAD_DIM):
            msk = lane_head == e
            sc = bdot(jnp.where(msk, qp, 0.0), kp_, 1, 1) * (SW_HEAD_DIM ** -0.5)
            sc = jnp.where(valid, sc, -1e30)
            m = lax.stop_gradient(jnp.max(sc, -1, keepdims=True))
            pe = jnp.exp(sc - m)
            l = jnp.sum(pe, -1, keepdims=True)
            o = bdot(pe, vp_, 1, 0) / l
            o_pair = o_pair + jnp.where(msk, o, 0.0)
            l_pair = l_pair + jnp.where(msk, m + jnp.log(l), 0.0)
        outs.append(o_pair)
        lses.append(l_pair)
    return jnp.concatenate(outs, -1), jnp.concatenate(lses, -1)


def _swa_specs(r):
    cur = pl.BlockSpec((SW_BLOCK, SW_DIM), lambda rho, n: (n, rho))
    prev = pl.BlockSpec((SW_BLOCK, SW_DIM), lambda rho, n: (jnp.maximum(n - 1, 0), rho))
    return cur, prev


def swa_fwd(tag, r, q, k, v):
    s = q.shape[0]
    ln = s // r
    q2, k2, v2 = (t.reshape(ln, r * SW_DIM) for t in (q, k, v))
    cur, prev = _swa_specs(r)

    def body(q_ref, kp_ref, kc_ref, vp_ref, vc_ref, o_ref, l_ref):
        ins = [r[...].astype(f32) for r in (q_ref, kp_ref, kc_ref, vp_ref, vc_ref)]
        o, l = _swa_block(*ins, pl.program_id(1) == 0)
        o_ref[...] = o
        l_ref[...] = l

    o, l = pl.pallas_call(
        body, grid=(r, ln // SW_BLOCK),
        in_specs=[cur, prev, cur, prev, cur], out_specs=[cur, cur],
        out_shape=[jax.ShapeDtypeStruct((ln, r * SW_DIM), f32)] * 2,
        compiler_params=_cparams("parallel", "parallel"), name=f"swa{r}_{tag}")(q2, k2, k2, v2, v2)
    return o.reshape(s, SW_DIM), l.reshape(s, SW_DIM)


def swa_bwd(tag, r, q, k, v, do, dl):
    s = q.shape[0]
    ln = s // r
    q2, k2, v2, do2, dl2 = (t.reshape(ln, r * SW_DIM) for t in (q, k, v, do, dl))
    cur, prev = _swa_specs(r)

    def body(q_ref, kp_ref, kc_ref, vp_ref, vc_ref, do_ref, dl_ref, dq_ref, dka_ref, dkb_ref, dva_ref, dvb_ref):
        first = pl.program_id(1) == 0
        ins = [r[...].astype(f32) for r in (q_ref, kp_ref, kc_ref, vp_ref, vc_ref)]
        _, vjp = jax.vjp(lambda *a: _swa_block(*a, first), *ins)
        dq_ref[...], dka_ref[...], dkb_ref[...], dva_ref[...], dvb_ref[...] = vjp((do_ref[...], dl_ref[...]))

    outs = pl.pallas_call(
        body, grid=(r, ln // SW_BLOCK),
        in_specs=[cur, prev, cur, prev, cur, cur, cur], out_specs=[cur] * 5,
        out_shape=[jax.ShapeDtypeStruct((ln, r * SW_DIM), f32)] * 5,
        compiler_params=_cparams("parallel", "parallel"), name=f"swa{r}_bwd_{tag}")(q2, k2, k2, v2, v2, do2, dl2)
    return [t.reshape(s, SW_DIM) for t in outs]


def _combine_tile(o1, l1, o2, l2, o3, l3):
    m = lax.stop_gradient(jnp.maximum(jnp.maximum(l1, l2), l3))
    e1, e2, e3 = jnp.exp(l1 - m), jnp.exp(l2 - m), jnp.exp(l3 - m)
    return ((o1 * e1 + o2 * e2 + o3 * e3) / (e1 + e2 + e3),)


def swa_merge_bwd(tag, grads, cos, sin):
    s = cos.shape[0]
    tm = SW_BLOCK
    nt = s // tm
    here = pl.BlockSpec((tm, SW_DIM), lambda i: (i, 0))
    arrs, specs = [], []
    for r, g in zip(SW_DILATIONS, grads):
        ahead = pl.BlockSpec((tm, SW_DIM), lambda i, _r=r: (jnp.minimum(i + _r, nt - 1), 0))
        arrs += g
        specs += [here, ahead, here, ahead, here]

    def body(*refs):
        i = pl.program_id(0)
        c_ref, s_ref = refs[15], refs[16]
        dq_ref, dk_ref, dv_ref = refs[17:]
        dq = jnp.zeros((tm, SW_DIM), f32)
        dk = jnp.zeros((tm, SW_DIM), f32)
        dv = jnp.zeros((tm, SW_DIM), f32)
        for b, r in enumerate(SW_DILATIONS):
            gq, gka, gkb, gva, gvb = refs[5 * b:5 * b + 5]
            inside = i + r < nt
            dq = dq + gq[...]
            dk = dk + gkb[...] + jnp.where(inside, gka[...], 0.0)
            dv = dv + gvb[...] + jnp.where(inside, gva[...], 0.0)
        dq_ref[...] = _rope_transpose(dq, c_ref[...], s_ref[...])
        dk_ref[...] = _rope_transpose(dk, c_ref[...], s_ref[...])
        dv_ref[...] = dv

    return pl.pallas_call(
        body, grid=(nt,), in_specs=specs + [here, here], out_specs=[here] * 3,
        out_shape=[jax.ShapeDtypeStruct((s, SW_DIM), f32)] * 3,
        compiler_params=_cparams("parallel"), name="swa_merge_bwd_" + tag)(*arrs, cos, sin)


def swa_mixer_fwd(tag, proj, cos, sin):
    q, k, v = rope_fwd(tag, proj, cos, sin)
    ols = []
    for r in SW_DILATIONS:
        ols += list(swa_fwd(tag, r, q, k, v))
    b_out = rowmap("swa_comb_" + tag, _combine_tile, ols, [], [SW_DIM], TM_ROW)[0]
    return b_out, (q, k, v, ols)


def swa_mixer_bwd(tag, cos, sin, res, db_out):
    q, k, v, ols = res
    dols, _ = rowmap_bwd("swa_comb_bwd_" + tag, _combine_tile, ols, [], [db_out], TM_ROW)
    grads = [swa_bwd(tag, r, q, k, v, dols[2 * b], dols[2 * b + 1]) for b, r in enumerate(SW_DILATIONS)]
    return swa_merge_bwd(tag, grads, cos, sin)


TM_S5 = 256
S5_GPB = LANES // S5_GROUP
S5_NBLK = D_MODEL // LANES
S5_HALF = S5_GPB * S5_STATE
S5_BW = 2 * S5_HALF
S5_WIDTH = S5_NBLK * S5_BW
S5_TABW = S5_NBLK * S5_HALF


def _s5_disc_tile(a_re, a_im, log_dt, b_re, b_im, expand):
    dt = jnp.exp(log_dt)
    mag = jnp.exp(a_re * dt)
    abar_re, abar_im = mag * jnp.cos(a_im * dt), mag * jnp.sin(a_im * dt)
    n_re, n_im = abar_re - 1.0, abar_im
    den = a_re * a_re + a_im * a_im
    c_re = (n_re * a_re + n_im * a_im) / den
    c_im = (n_im * a_re - n_re * a_im) / den
    cx_re, cx_im = hdot(c_re, expand), hdot(c_im, expand)
    return abar_re, abar_im, cx_re * b_re - cx_im * b_im, cx_re * b_im + cx_im * b_re


def _s5_expand():
    return (_iota2((S5_STATE, S5_STATE * S5_GROUP), 1) // S5_GROUP == _iota2((S5_STATE, S5_STATE * S5_GROUP), 0)).astype(f32)


def s5_tables(a_re, a_im, log_dt):
    lanes = lambda v: v.reshape(1, S5_TABW)
    dt = jnp.broadcast_to(log_dt.reshape(S5_GROUPS, 1), (S5_GROUPS, S5_STATE))
    t = TM_S5

    def body(are_ref, aim_ref, ldt_ref, ar_ref, ai_ref, arr_ref, air_ref):
        dtv = jnp.exp(ldt_ref[...])
        lre, lim = are_ref[...] * dtv, aim_ref[...] * dtv
        row = _iota2((t, S5_HALF), 0)
        for asc, o_re, o_im in ((True, ar_ref, ai_ref), (False, arr_ref, air_ref)):
            n = (row + 1 if asc else t - row).astype(f32)
            mag = jnp.exp(n * lre)
            o_re[...] = mag * jnp.cos(n * lim)
            o_im[...] = mag * jnp.sin(n * lim)

    lane = pl.BlockSpec((1, S5_HALF), lambda j: (0, j))
    tab = pl.BlockSpec((t, S5_HALF), lambda j: (0, j))
    return pl.pallas_call(
        body, grid=(S5_NBLK,), in_specs=[lane] * 3, out_specs=[tab] * 4,
        out_shape=[jax.ShapeDtypeStruct((t, S5_TABW), f32)] * 4,
        compiler_params=_cparams("parallel"), name="s5_tables")(lanes(a_re), lanes(a_im), lanes(dt))


def s5_pack_weights(bbar_re, bbar_im, c_re, c_im):
    eye = jnp.eye(S5_GPB, dtype=f32)
    bb = jnp.stack([bbar_re.reshape(S5_GROUPS, S5_STATE, S5_GROUP), bbar_im.reshape(S5_GROUPS, S5_STATE, S5_GROUP)], 1)
    bb = bb.transpose(0, 3, 1, 2).reshape(S5_NBLK, S5_GPB, S5_GROUP, 2, S5_STATE)
    wb = (bb[:, :, :, :, None, :] * eye[None, :, None, None, :, None]).reshape(S5_NBLK, LANES, S5_BW)
    cc = jnp.stack([c_re, -c_im], 1)
    cc = cc.reshape(S5_NBLK, S5_GPB, 2, S5_GROUP, S5_STATE).transpose(0, 2, 1, 4, 3)
    wc = (cc[:, :, :, :, None, :] * eye[None, None, :, None, :, None]).reshape(S5_NBLK, S5_BW, LANES)
    return wb, wc


def s5_unpack_weight_grads(dwb, dwc):
    d6 = dwb.reshape(S5_NBLK, S5_GPB, S5_GROUP, 2, S5_GPB, S5_STATE)
    dbb = jnp.stack([d6[:, gl, :, :, gl, :] for gl in range(S5_GPB)])
    dbb = dbb.transpose(1, 0, 3, 4, 2).reshape(S5_GROUPS, 2, S5_STATE * S5_GROUP)
    c6 = dwc.reshape(S5_NBLK, 2, S5_GPB, S5_STATE, S5_GPB, S5_GROUP)
    dcc = jnp.stack([c6[:, :, gl, :, gl, :] for gl in range(S5_GPB)])
    dcc = dcc.transpose(1, 0, 2, 4, 3).reshape(S5_GROUPS, 2, S5_GROUP, S5_STATE)
    return dbb[:, 0], dbb[:, 1], dcc[:, 0], -dcc[:, 1]


def _s5_step_rows(t):
    d, out = 1, []
    while d < t:
        out.append(d)
        d *= 2
    return out


def s5_core_fwd(tag, u, wb, wc, a1, a2, dskip):
    s = u.shape[0]
    t = TM_S5

    def body(u_ref, wb_ref, wc_ref, ar_ref, ai_ref, d_ref, y_ref, x_ref, carry):
        @pl.when(pl.program_id(1) == 0)
        def _():
            carry[...] = jnp.zeros_like(carry)

        uv = u_ref[...]
        bu = bdot(uv, wb_ref[...], 1, 0)
        row = _iota2((t, LANES), 0)
        for c in range(S5_HALF // LANES):
            re, im = slice(c * LANES, (c + 1) * LANES), slice(S5_HALF + c * LANES, S5_HALF + (c + 1) * LANES)
            xr, xi = bu[:, re], bu[:, im]
            for d in _s5_step_rows(t):
                ar, ai = ar_ref[d - 1:d, re], ai_ref[d - 1:d, re]
                if d % SUBLANES:
                    keep = row >= d
                    sr = jnp.where(keep, pltpu.roll(xr, d, 0), 0.0)
                    si = jnp.where(keep, pltpu.roll(xi, d, 0), 0.0)
                    xr, xi = xr + ar * sr - ai * si, xi + ar * si + ai * sr
                else:
                    sr, si = xr[:t - d], xi[:t - d]
                    xr = jnp.concatenate([xr[:d], xr[d:] + (ar * sr - ai * si)], 0)
                    xi = jnp.concatenate([xi[:d], xi[d:] + (ar * si + ai * sr)], 0)
            cr, ci = carry[:, re], carry[:, im]
            ar, ai = ar_ref[:, re], ai_ref[:, re]
            x_ref[:, re] = xr + ar * cr - ai * ci
            x_ref[:, im] = xi + ar * ci + ai * cr
        carry[...] = x_ref[t - 1:t, :]
        y_ref[...] = bdot(x_ref[...], wc_ref[...], 1, 0) + d_ref[...] * uv

    tab = pl.BlockSpec((t, S5_HALF), lambda j, i: (0, j))
    return pl.pallas_call(
        body, grid=(S5_NBLK, s // t),
        in_specs=[pl.BlockSpec((t, LANES), lambda j, i: (i, j)),
                  pl.BlockSpec((None, LANES, S5_BW), lambda j, i: (j, 0, 0)),
                  pl.BlockSpec((None, S5_BW, LANES), lambda j, i: (j, 0, 0)),
                  tab, tab, pl.BlockSpec((1, LANES), lambda j, i: (0, j))],
        out_specs=[pl.BlockSpec((t, LANES), lambda j, i: (i, j)), pl.BlockSpec((t, S5_BW), lambda j, i: (i, j))],
        out_shape=[jax.ShapeDtypeStruct((s, D_MODEL), f32), jax.ShapeDtypeStruct((s, S5_WIDTH), f32)],
        scratch_shapes=[pltpu.VMEM((1, S5_BW), f32)],
        compiler_params=_cparams("parallel", "arbitrary"), name="s5_core_" + tag)(u, wb, wc, a1, a2, dskip)


def s5_core_bwd(tag, u, x, wb, wc, a1, a2, a1r, a2r, dskip, dy):
    s = u.shape[0]
    t = TM_S5
    nt = s // t
    hb = t // SUBLANES

    def body(u_ref, dy_ref, x_ref, xh_ref, wb_ref, wc_ref, ar_ref, ai_ref, arr_ref, air_ref, d_ref,
             du_ref, dwb_ref, dwc_ref, dd_ref, q1_ref, q2_ref, carry, lam_scr):
        i = pl.program_id(1)
        tt = nt - 1 - i

        @pl.when(i == 0)
        def _():
            carry[...] = jnp.zeros_like(carry)

        uv, dyv, xv = u_ref[...], dy_ref[...], x_ref[...]
        lam = bdot(dyv, wc_ref[...], 1, 1)
        row = _iota2((t, LANES), 0)
        x_last = jnp.where(tt > 0, xh_ref[SUBLANES - 1:SUBLANES, :], 0.0)
        q1s, q2s = [], []
        for c in range(S5_HALF // LANES):
            re, im = slice(c * LANES, (c + 1) * LANES), slice(S5_HALF + c * LANES, S5_HALF + (c + 1) * LANES)
            lr, li = lam[:, re], lam[:, im]
            for d in _s5_step_rows(t):
                ar, ai = ar_ref[d - 1:d, re], ai_ref[d - 1:d, re]
                if d % SUBLANES:
                    keep = row < t - d
                    sr = jnp.where(keep, pltpu.roll(lr, t - d, 0), 0.0)
                    si = jnp.where(keep, pltpu.roll(li, t - d, 0), 0.0)
                    lr, li = lr + ar * sr + ai * si, li + ar * si - ai * sr
                else:
                    sr, si = lr[d:], li[d:]
                    lr = jnp.concatenate([lr[:t - d] + (ar * sr + ai * si), lr[t - d:]], 0)
                    li = jnp.concatenate([li[:t - d] + (ar * si - ai * sr), li[t - d:]], 0)
            cr, ci = carry[:, re], carry[:, im]
            ar, ai = arr_ref[:, re], air_ref[:, re]
            lr, li = lr + ar * cr + ai * ci, li + ar * ci - ai * cr
            lam_scr[:, re] = lr
            lam_scr[:, im] = li
            pr = jnp.where(row == 0, x_last[:, re], pltpu.roll(xv[:, re], 1, 0))
            pi = jnp.where(row == 0, x_last[:, im], pltpu.roll(xv[:, im], 1, 0))
            p1, p2 = lr * pr + li * pi, li * pr - lr * pi
            q1, q2 = p1[:SUBLANES, :], p2[:SUBLANES, :]
            for k in range(1, hb):
                q1 = q1 + p1[k * SUBLANES:(k + 1) * SUBLANES, :]
                q2 = q2 + p2[k * SUBLANES:(k + 1) * SUBLANES, :]
            q1s.append(q1)
            q2s.append(q2)
        carry[...] = lam_scr[0:1, :]
        lam = lam_scr[...]
        du_ref[...] = bdot(lam, wb_ref[...], 1, 1) + d_ref[...] * dyv
        upd = [(dwb_ref, bdot(uv, lam, 0, 0)), (dwc_ref, bdot(xv, dyv, 0, 0)),
               (dd_ref, jnp.sum(dyv * uv, 0, keepdims=True)),
               (q1_ref, jnp.concatenate(q1s, 1)), (q2_ref, jnp.concatenate(q2s, 1))]

        @pl.when(i == 0)
        def _():
            for ref, val in upd:
                ref[...] = val

        @pl.when(i != 0)
        def _():
            for ref, val in upd:
                ref[...] += val

    nb8 = s // SUBLANES
    rev = lambda w: pl.BlockSpec((t, w), lambda j, i: (nt - 1 - i, j))
    tab = pl.BlockSpec((t, S5_HALF), lambda j, i: (0, j))
    return pl.pallas_call(
        body, grid=(S5_NBLK, nt),
        in_specs=[rev(LANES), rev(LANES), rev(S5_BW),
                  pl.BlockSpec((SUBLANES, S5_BW), lambda j, i: (jnp.maximum((nt - 1 - i) * hb - 1, 0), j)),
                  pl.BlockSpec((None, LANES, S5_BW), lambda j, i: (j, 0, 0)),
                  pl.BlockSpec((None, S5_BW, LANES), lambda j, i: (j, 0, 0)),
                  tab, tab, tab, tab, pl.BlockSpec((1, LANES), lambda j, i: (0, j))],
        out_specs=[rev(LANES),
                   pl.BlockSpec((None, LANES, S5_BW), lambda j, i: (j, 0, 0)),
                   pl.BlockSpec((None, S5_BW, LANES), lambda j, i: (j, 0, 0)),
                   pl.BlockSpec((1, LANES), lambda j, i: (0, j)),
                   pl.BlockSpec((SUBLANES, S5_HALF), lambda j, i: (0, j)),
                   pl.BlockSpec((SUBLANES, S5_HALF), lambda j, i: (0, j))],
        out_shape=[jax.ShapeDtypeStruct((s, D_MODEL), f32),
                   jax.ShapeDtypeStruct((S5_NBLK, LANES, S5_BW), f32),
                   jax.ShapeDtypeStruct((S5_NBLK, S5_BW, LANES), f32),
                   jax.ShapeDtypeStruct((1, D_MODEL), f32),
                   jax.ShapeDtypeStruct((SUBLANES, S5_TABW), f32),
                   jax.ShapeDtypeStruct((SUBLANES, S5_TABW), f32)],
        scratch_shapes=[pltpu.VMEM((1, S5_BW), f32), pltpu.VMEM((t, S5_BW), f32)],
        compiler_params=_cparams("parallel", "arbitrary"),
        name="s5_core_bwd_" + tag)(u, dy, x, x, wb, wc, a1, a2, a1r, a2r, dskip)


def _gelu_tile(y):
    return (jax.nn.gelu(y),)


def s5_mixer_fwd(tag, u, prm, w_og):
    a_re, a_im, log_dt, b_re, b_im, c_re, c_im, dskip = prm
    disc_in = [a_re, a_im, log_dt.reshape(S5_GROUPS, 1), b_re.reshape(S5_GROUPS, -1), b_im.reshape(S5_GROUPS, -1)]
    abar_re, abar_im, bbar_re, bbar_im = rowmap("s5_disc_" + tag, _s5_disc_tile, disc_in, [_s5_expand()],
                                                [S5_STATE, S5_STATE, S5_STATE * S5_GROUP, S5_STATE * S5_GROUP], S5_GROUPS)
    del abar_re, abar_im
    a1, a2, a1r, a2r = s5_tables(a_re, a_im, log_dt)
    wb, wc = s5_pack_weights(bbar_re, bbar_im, c_re, c_im)
    wb, wc = wb.astype(bf16), wc.astype(bf16)
    y, x = s5_core_fwd(tag, u, wb, wc, a1, a2, dskip.reshape(1, D_MODEL))
    hid = rowmap("s5_gelu_" + tag, _gelu_tile, [y], [], [D_MODEL], TM_ROW, out_dtypes=[bf16])[0]
    og = mm_nn("s5_og_" + tag, hid, w_og)
    mix = rowmap("s5_glu_" + tag, _glu_tile, [og], [], [D_MODEL], TM_ROW)[0]
    return mix, (disc_in, a1, a2, a1r, a2r, wb, wc, x, y, hid, og)


def s5_mixer_bwd(tag, idx, u, prm, w_og, res, dmix, stacks):
    a_re, a_im, log_dt, b_re, b_im, c_re, c_im, dskip = prm
    disc_in, a1, a2, a1r, a2r, wb, wc, x, y, hid, og = res
    (dog,), _ = rowmap_bwd("s5_glu_bwd_" + tag, _glu_tile, [og], [], [dmix], TM_ROW)
    n_odd = DEPTH // 2
    dw_og = (mm_tn("s5_wo_dw_" + tag, hid, dog, n_cols=D_MODEL, stack=(stacks[0], idx, n_odd)),
             mm_tn("s5_wg_dw_" + tag, hid, dog, b_col0=D_MODEL, n_cols=D_MODEL, stack=(stacks[1], idx, n_odd)))
    dhid = mm_nt("s5_og_dx_" + tag, dog, w_og)
    (dy,), _ = rowmap_bwd("s5_gelu_bwd_" + tag, _gelu_tile, [y], [], [dhid], TM_ROW)
    du, dwb, dwc, ddskip, q1, q2 = s5_core_bwd(tag, u, x, wb, wc, a1, a2, a1r, a2r, dskip.reshape(1, D_MODEL), dy)
    dbbar_re, dbbar_im, dc_re, dc_im = s5_unpack_weight_grads(dwb, dwc)
    dabar_re = q1.sum(0).reshape(S5_GROUPS, S5_STATE)
    dabar_im = q2.sum(0).reshape(S5_GROUPS, S5_STATE)
    grads, _ = rowmap_bwd("s5_disc_bwd_" + tag, _s5_disc_tile, disc_in, [_s5_expand()],
                          [dabar_re, dabar_im, dbbar_re, dbbar_im], S5_GROUPS, par_mask=[False])
    da_re, da_im, dlog_dt, db_re, db_im = grads
    return du, (da_re, da_im, dlog_dt.reshape(S5_GROUPS), db_re.reshape(b_re.shape), db_im.reshape(b_im.shape),
                dc_re, dc_im, ddskip.reshape(D_MODEL)), dw_og


HYB_IN = 3592
_IN_B0, _IN_SW0 = 2048, 2056


IN_SHARD = HYB_IN // 4
SHARD_ORDER_GRADS = ("hyb_w_in", "ffn_wg", "ffn_wu", "ffn_wd")
FFN_TRANSPOSED = ("ffn_wg", "ffn_wu")
BIG_SHARDED = ("hyb_w_in", "hyb_w_out", "s5_glu_wo", "s5_glu_wg", "xq_w", "xk_w", "xv_w", "xo_w", "ffn_wg", "ffn_wu", "ffn_wd")


def _w_in_pieces():
    runs = [(0, _IN_B0, 0), (_IN_B0, _IN_SW0, COL_BA), (_IN_SW0, HYB_IN, _IN_B0)]
    out = []
    for sh in range(4):
        lo, hi = sh * IN_SHARD, (sh + 1) * IN_SHARD
        for r_lo, r_hi, c_lo in runs:
            a, b = max(lo, r_lo), min(hi, r_hi)
            if a < b:
                out.append((sh, a - lo, b - lo, c_lo + a - r_lo))
    return out


def w_in_to_canonical(tag, layer, w4):
    tr = 128

    def body(w_ref, o_ref):
        o_ref[:, COL_BA:] = jnp.zeros((tr, BA_PAD), o_ref.dtype)
        for sh, a, b, c in _w_in_pieces():
            o_ref[:, c:c + b - a] = w_ref[sh, :, a:b]

    return pl.pallas_call(
        body, grid=(D_MODEL // tr,),
        in_specs=[pl.BlockSpec((4, None, tr, IN_SHARD), lambda i: (0, layer, i, 0))],
        out_specs=pl.BlockSpec((tr, PROJ_COLS), lambda i: (i, 0)),
        out_shape=jax.ShapeDtypeStruct((D_MODEL, PROJ_COLS), w4.dtype),
        compiler_params=_cparams("parallel"), name="w_in_canon_" + tag)(w4)


def w_in_grad_to_shards(tag, layer, g, stack, n_layers):
    tr = 128
    extra, extra_specs, _ = _stacked(stack)

    def body(g_ref, *rest):
        o_ref = rest[-1]
        for sh, a, b, c in _w_in_pieces():
            o_ref[sh, :, a:b] = g_ref[:, c:c + b - a]

    return pl.pallas_call(
        body, grid=(D_MODEL // tr,),
        in_specs=[pl.BlockSpec((tr, PROJ_COLS), lambda i: (i, 0))] + extra_specs,
        out_specs=pl.BlockSpec((4, None, tr, IN_SHARD), lambda i: (0, layer, i, 0)),
        out_shape=jax.ShapeDtypeStruct((4, n_layers, D_MODEL, IN_SHARD), f32),
        input_output_aliases={1: 0} if extra else {},
        compiler_params=_cparams("parallel"), name="w_in_grad_shards_" + tag)(g, *extra)


def _add2(name, a, b):
    return rowmap(name, lambda p, q: (p + q,), [a, b], [], [a.shape[1]], _pick(a.shape[0], (256, 128, 64, 32, 16, 8)))[0]


def local_step(x, mem, positions, target, p):
    s = x.shape[0]
    cos, sin = rope_tables(positions, s)
    row = lambda v: v.reshape(1, -1).astype(f32)
    wg4, wu4, wd4 = (p[n].astype(bf16) for n in ("ffn_wg", "ffn_wu", "ffn_wd"))
    h = h16 = x
    tape = []
    for l in range(DEPTH):
        i, tag = l // 2, str(l)
        t = {"h0": h, "h0_16": h16}
        if l % 2 == 0:
            t["w_in"] = w_in_to_canonical(tag, i, p["hyb_w_in"].astype(bf16))
            t["w_out"] = p["hyb_w_out"][i].astype(bf16)
            t["dn_prm"] = (p["dn_conv_w"][i].astype(f32), row(jnp.repeat(p["dn_a_log"][i], DN_HEAD_DIM)),
                           row(jnp.repeat(p["dn_dt_bias"][i], DN_HEAD_DIM)), row(jnp.tile(p["dn_norm_g"][i], DN_HEADS)))
            t["proj"] = mm_nn("hyb_in_" + tag, h16, t["w_in"])
            a_out, t["dn"] = dn_mixer_fwd(tag, t["proj"], *t["dn_prm"])
            b_out, t["swa"] = swa_mixer_fwd(tag, t["proj"], cos, sin)
            t["mixed"] = jnp.concatenate([a_out, b_out], 1)
            mix = mm_nn("hyb_out_" + tag, t["mixed"], t["w_out"])
        else:
            t["s5_prm"] = tuple(p[n][i].astype(f32) for n in
                                ("s5_a_re", "s5_a_im", "s5_log_dt", "s5_b_re", "s5_b_im", "s5_c_re", "s5_c_im", "s5_d"))
            t["w_og"] = jnp.concatenate([p["s5_glu_wo"][i], p["s5_glu_wg"][i]], 1).astype(bf16)
            mix, t["s5"] = s5_mixer_fwd(tag, h, t["s5_prm"], t["w_og"])
        t["mix"] = mix
        t["ln"] = [(row(p[g][l]), row(p[b][l])) for g, b in
                   (("ln_mix_g", "ln_mix_b"), ("ln_x_g", "ln_x_b"), ("ln_ffn_g", "ln_ffn_b"))]
        t["h1"], t["h1_16"] = postnorm_fwd("mix" + tag, h, mix, *t["ln"][0])
        t["wq"], t["wo"] = p["xq_w"][l].astype(bf16), p["xo_w"][l].astype(bf16)
        t["wkv"] = jnp.concatenate([p["xk_w"][l], p["xv_w"][l]], 1).astype(bf16)
        t["xo"], t["xres"] = xattn_fwd(tag, t["h1_16"], mem, t["wq"], t["wkv"], t["wo"])
        t["h2"], t["h2_16"] = postnorm_fwd("x" + tag, t["h1"], t["xo"], *t["ln"][1])
        t["fo"], t["fres"] = ffn_fwd(tag, l, t["h2_16"], wg4, wu4, wd4)
        h, h16 = postnorm_fwd("ffn" + tag, t["h2"], t["fo"], *t["ln"][2])
        tape.append(t)

    part, dh = loss_head(h, target)
    loss = jnp.sum(part)

    g = {n: [None] * v.shape[0] for n, v in p.items() if n not in BIG_SHARDED}
    st = {n: None for n in BIG_SHARDED}
    for l in reversed(range(DEPTH)):
        i, tag, t = l // 2, str(l), tape[l]
        dh2a, dfo, dg, db = postnorm_bwd("ffn" + tag, t["h2"], t["fo"], *t["ln"][2], dh)
        g["ln_ffn_g"][l], g["ln_ffn_b"][l] = dg[0], db[0]
        ffn_names = ("ffn_wg", "ffn_wu", "ffn_wd")
        prev = None if st["ffn_wd"] is None else [st[n] for n in ffn_names]
        dh2b, new = ffn_bwd(tag, l, t["h2_16"], wg4, wu4, wd4, t["fres"], dfo, prev)
        st.update(zip(ffn_names, new))
        dh1a, dxo, dg, db = postnorm_bwd("x" + tag, t["h1"], t["xo"], *t["ln"][1], [dh2a, dh2b])
        g["ln_x_g"][l], g["ln_x_b"][l] = dg[0], db[0]
        x_names = ("xq_w", "xk_w", "xv_w", "xo_w")
        dh1b, new = xattn_bwd(tag, l, t["h1_16"], mem, t["wq"], t["wkv"], t["wo"], t["xres"], dxo, [st[n] for n in x_names])
        st.update(zip(x_names, new))
        dh0a, dmix, dg, db = postnorm_bwd("mix" + tag, t["h0"], t["mix"], *t["ln"][0], [dh1a, dh1b])
        g["ln_mix_g"][l], g["ln_mix_b"][l] = dg[0], db[0]
        if l % 2 == 0:
            st["hyb_w_out"] = mm_tn("hyb_out_dw_" + tag, t["mixed"], dmix, stack=(st["hyb_w_out"], i, DEPTH // 2))
            dmixed = mm_nt("hyb_out_dx_" + tag, dmix, t["w_out"])
            dqkv, dz, dba, dcw, dalog, ddtb, dng = dn_mixer_bwd(tag, t["proj"], *t["dn_prm"], t["dn"], (dmixed, DN_KEY_DIM, 0))
            g["dn_conv_w"][i] = dcw
            g["dn_a_log"][i] = dalog.reshape(DN_HEADS, DN_HEAD_DIM).sum(1)
            g["dn_dt_bias"][i] = ddtb.reshape(DN_HEADS, DN_HEAD_DIM).sum(1)
            g["dn_norm_g"][i] = dng.reshape(DN_HEADS, DN_HEAD_DIM).sum(0)
            dq, dk, dv = swa_mixer_bwd(tag, cos, sin, t["swa"], (dmixed, SW_DIM, 1))
            dproj = jnp.concatenate([dqkv, dz, dq, dk, dv, dba], 1)
            st["hyb_w_in"] = w_in_grad_to_shards(tag, i, mm_tn("hyb_in_dw_" + tag, t["h0_16"], dproj), st["hyb_w_in"], DEPTH // 2)
            dh0b = mm_nt("hyb_in_dx_" + tag, dproj, t["w_in"])
        else:
            dh0b, dprm, (st["s5_glu_wo"], st["s5_glu_wg"]) = s5_mixer_bwd(
                tag, i, t["h0"], t["s5_prm"], t["w_og"], t["s5"], dmix, (st["s5_glu_wo"], st["s5_glu_wg"]))
            for n, v in zip(("s5_a_re", "s5_a_im", "s5_log_dt", "s5_b_re", "s5_b_im", "s5_c_re", "s5_c_im", "s5_d"), dprm):
                g[n][i] = v
        dh = [dh0a, dh0b]
    grad_x = _add2("grad_x", dh[0], dh[1])
    grads = {n: jnp.stack(v) for n, v in g.items()}
    grads.update(st)
    return loss, grad_x, grads


WEIGHT_NAMES = ("hyb_w_in", "dn_conv_w", "dn_a_log", "dn_dt_bias", "dn_norm_g", "hyb_w_out", "s5_a_re", "s5_a_im",
                "s5_log_dt", "s5_b_re", "s5_b_im", "s5_c_re", "s5_c_im", "s5_d", "s5_glu_wo", "s5_glu_wg",
                "ln_mix_g", "ln_mix_b", "xq_w", "xk_w", "xv_w", "xo_w", "ln_x_g", "ln_x_b",
                "ffn_wg", "ffn_wu", "ffn_wd", "ln_ffn_g", "ln_ffn_b")
SHARD_AXIS = {"hyb_w_in": 2, "dn_conv_w": 2, "hyb_w_out": 1, "s5_d": 1, "s5_glu_wo": 1, "s5_glu_wg": 1,
              "xq_w": 1, "xk_w": 1, "xv_w": 1, "xo_w": 1, "ffn_wg": 2, "ffn_wu": 2, "ffn_wd": 1}
GATHER_F32 = ("dn_conv_w", "s5_d")
N_CHIPS = 4
PACK_COLS = 1024
_ANY = pl.BlockSpec(memory_space=pl.ANY)


def _pos():
    return lax.axis_index("x"), lax.axis_index("y"), lax.axis_index("c")


def _chip_peers(mx, my):
    return [(1 - mx, my), (mx, 1 - my), (1 - mx, 1 - my)]


def _rcopy(src, dst, ssem, rsem, dev):
    return pltpu.make_async_remote_copy(src_ref=src, dst_ref=dst, send_sem=ssem, recv_sem=rsem,
                                        device_id=dev, device_id_type=pl.DeviceIdType.MESH)


def comm_allgather4(name, x):
    def body(x_ref, o_ref, ssem, rsem, lsem):
        mx, my, mc = _pos()
        me = 2 * mx + my
        peers = _chip_peers(mx, my)
        loc = pltpu.make_async_copy(x_ref, o_ref.at[me], lsem)
        loc.start()
        sends = [_rcopy(x_ref, o_ref.at[me], ssem.at[k], rsem.at[k], (px, py, mc)) for k, (px, py) in enumerate(peers)]
        for cp in sends:
            cp.start()
        for k, (px, py) in enumerate(peers):
            _rcopy(x_ref, o_ref.at[2 * px + py], ssem.at[k], rsem.at[k], (px, py, mc)).wait_recv()
        for cp in sends:
            cp.wait_send()
        loc.wait()

    return pl.pallas_call(
        body, out_shape=jax.ShapeDtypeStruct((N_CHIPS,) + x.shape, x.dtype), in_specs=[_ANY], out_specs=_ANY,
        scratch_shapes=[pltpu.SemaphoreType.DMA((3,)), pltpu.SemaphoreType.DMA((3,)), pltpu.SemaphoreType.DMA],
        name=name)(x)


def _multi_call(name, body, ins, out_shapes, sems, in_place=False):
    return pl.pallas_call(
        body, out_shape=out_shapes, in_specs=[_ANY] * len(ins), out_specs=[_ANY] * len(out_shapes),
        scratch_shapes=sems, input_output_aliases={w: w for w in range(len(ins))} if in_place else {},
        name=name)(*ins)


def comm_gather_weights(name, slots):
    n = len(slots)

    def body(*refs):
        os_ = refs[n:2 * n]
        ssem, rsem, fssem, frsem = refs[2 * n:]
        mx, my, mc = _pos()
        me = 2 * mx + my
        peers = _chip_peers(mx, my)
        sib = (mx, my, 1 - mc)
        half = [o.shape[1] // 2 for o in os_]
        mine = [pl.ds(mc * h, h) for h in half]
        other = [pl.ds((1 - mc) * h, h) for h in half]
        sends = [_rcopy(os_[w].at[me, mine[w]], os_[w].at[me, mine[w]], ssem.at[w, k], rsem.at[w, k], (px, py, mc))
                 for w in range(n) for k, (px, py) in enumerate(peers)]
        for cp in sends:
            cp.start()
        fwds = []
        for w in range(n):
            for k, (px, py) in enumerate(peers):
                landed = os_[w].at[2 * px + py, mine[w]]
                _rcopy(landed, landed, ssem.at[w, k], rsem.at[w, k], (px, py, mc)).wait_recv()
                fw = _rcopy(landed, landed, fssem.at[w, k], frsem.at[w, k], sib)
                fw.start()
                fwds.append(fw)
        for w in range(n):
            for k, (px, py) in enumerate(peers):
                theirs = os_[w].at[2 * px + py, other[w]]
                _rcopy(theirs, theirs, fssem.at[w, k], frsem.at[w, k], sib).wait_recv()
        for cp in sends + fwds:
            cp.wait_send()

    dma = pltpu.SemaphoreType.DMA
    return _multi_call(name, body, slots, [jax.ShapeDtypeStruct(x.shape, x.dtype) for x in slots],
                       [dma((n, 3)), dma((n, 3)), dma((n, 3)), dma((n, 3))], in_place=True)


def comm_sibling_halves(name, gs):
    n = len(gs)

    def body(*refs):
        xs, os_ = refs[:n], refs[n:2 * n]
        ssem, rsem = refs[2 * n:]
        mx, my, mc = _pos()
        sib = (mx, my, 1 - mc)
        sends = []
        for w in range(n):
            h = xs[w].shape[1] // 2
            for j in range(N_CHIPS):
                sends.append(_rcopy(xs[w].at[j, pl.ds((1 - mc) * h, h)], os_[w].at[j], ssem.at[w, j], rsem.at[w, j], sib))
        for cp in sends:
            cp.start()
        for w in range(n):
            for j in range(N_CHIPS):
                _rcopy(os_[w].at[j], os_[w].at[j], ssem.at[w, j], rsem.at[w, j], sib).wait_recv()
        for cp in sends:
            cp.wait_send()

    dma = pltpu.SemaphoreType.DMA
    return _multi_call(name, body, gs,
                       [jax.ShapeDtypeStruct((N_CHIPS, g.shape[1] // 2) + g.shape[2:], g.dtype) for g in gs],
                       [dma((n, N_CHIPS)), dma((n, N_CHIPS))])


def comm_alltoall4(name, xs):
    n = len(xs)

    def body(*refs):
        xr, os_ = refs[:n], refs[n:2 * n]
        ssem, rsem = refs[2 * n:]
        mx, my, mc = _pos()
        me = 2 * mx + my
        peers = _chip_peers(mx, my)
        sends = [_rcopy(xr[w].at[2 * px + py], os_[w].at[me], ssem.at[w, k], rsem.at[w, k], (px, py, mc))
                 for w in range(n) for k, (px, py) in enumerate(peers)]
        for cp in sends:
            cp.start()
        for w in range(n):
            for k, (px, py) in enumerate(peers):
                dst = os_[w].at[2 * px + py]
                _rcopy(dst, dst, ssem.at[w, k], rsem.at[w, k], (px, py, mc)).wait_recv()
        for cp in sends:
            cp.wait_send()

    dma = pltpu.SemaphoreType.DMA
    return _multi_call(name, body, xs, [jax.ShapeDtypeStruct(x.shape, x.dtype) for x in xs], [dma((n, 3)), dma((n, 3))])


def comm_sibling_join(name, bs):
    n = len(bs)

    def body(*refs):
        os_ = refs[n:2 * n]
        ssem, rsem = refs[2 * n:]
        mx, my, mc = _pos()
        sib = (mx, my, 1 - mc)
        sends = [_rcopy(os_[w].at[mc], os_[w].at[mc], ssem.at[w], rsem.at[w], sib) for w in range(n)]
        for cp in sends:
            cp.start()
        for w in range(n):
            dst = os_[w].at[1 - mc]
            _rcopy(dst, dst, ssem.at[w], rsem.at[w], sib).wait_recv()
        for cp in sends:
            cp.wait_send()

    dma = pltpu.SemaphoreType.DMA
    return _multi_call(name, body, bs, [jax.ShapeDtypeStruct(b.shape, b.dtype) for b in bs], [dma((n,)), dma((n,))],
                       in_place=True)


def comm_sibling_swap(name, x):
    def body(x_ref, o_ref, ssem, rsem):
        mx, my, mc = _pos()
        cp = _rcopy(x_ref, o_ref, ssem, rsem, (mx, my, 1 - mc))
        cp.start()
        cp.wait_recv()
        cp.wait_send()

    return pl.pallas_call(
        body, out_shape=jax.ShapeDtypeStruct(x.shape, x.dtype), in_specs=[_ANY], out_specs=_ANY,
        scratch_shapes=[pltpu.SemaphoreType.DMA, pltpu.SemaphoreType.DMA], name=name)(x)


def _row_tile(r):
    return _pick(r, (256, 128, 64, 32, 16, 8))


def add_own_half(name, g, recv, out_dtype):
    r, c = g.shape[2:]
    tr = _row_tile(r)
    mc = lax.axis_index("c").astype(jnp.int32).reshape(1)

    def body(c_ref, g_ref, r_ref, o_ref):
        o_ref[...] = (g_ref[...] + r_ref[...]).astype(o_ref.dtype)

    grid_spec = pltpu.PrefetchScalarGridSpec(
        num_scalar_prefetch=1, grid=(N_CHIPS, r // tr),
        in_specs=[pl.BlockSpec((None, None, tr, c), lambda j, i, cr: (j, cr[0], i, 0)),
                  pl.BlockSpec((None, tr, c), lambda j, i, cr: (j, i, 0))],
        out_specs=pl.BlockSpec((None, tr, c), lambda j, i, cr: (j, i, 0)))
    return pl.pallas_call(body, grid_spec=grid_spec, out_shape=jax.ShapeDtypeStruct(recv.shape, out_dtype),
                          compiler_params=_cparams("parallel", "parallel"), name=name)(mc, g, recv)


def cast_into_slot(name, w, chip, dtype):
    r, c = w.shape
    tr = _row_tile(r)

    def body(c_ref, w_ref, o_ref):
        o_ref[...] = w_ref[...].astype(o_ref.dtype)

    grid_spec = pltpu.PrefetchScalarGridSpec(
        num_scalar_prefetch=1, grid=(r // tr,),
        in_specs=[pl.BlockSpec((tr, c), lambda i, cr: (i, 0))],
        out_specs=pl.BlockSpec((None, tr, c), lambda i, cr: (cr[0], i, 0)))
    return pl.pallas_call(body, grid_spec=grid_spec, out_shape=jax.ShapeDtypeStruct((N_CHIPS, r, c), dtype),
                          compiler_params=_cparams("parallel"), name=name)(chip.astype(jnp.int32).reshape(1), w)


def sum_chips_into_half(name, own, arrived, chip, mc):
    r, c = own.shape[1:]
    tr = _row_tile(r)

    def body(s0, s1, s2, s3, s4, own_ref, a_ref, b_ref, d_ref, o_ref):
        o_ref[...] = ((own_ref[...].astype(f32) + a_ref[...].astype(f32))
                      + (b_ref[...].astype(f32) + d_ref[...].astype(f32)))

    slot = lambda k: pl.BlockSpec((None, tr, c), lambda i, *sc, _k=k: (sc[_k][0], i, 0))
    grid_spec = pltpu.PrefetchScalarGridSpec(
        num_scalar_prefetch=5, grid=(r // tr,), in_specs=[slot(0), slot(1), slot(2), slot(3)],
        out_specs=pl.BlockSpec((None, tr, c), lambda i, *sc: (sc[4][0], i, 0)))
    mx, my = lax.axis_index("x"), lax.axis_index("y")
    scal = [v.astype(jnp.int32).reshape(1) for v in
            (2 * mx + my, 2 * (1 - mx) + my, 2 * mx + (1 - my), 2 * (1 - mx) + (1 - my), mc)]
    return pl.pallas_call(body, grid_spec=grid_spec, out_shape=jax.ShapeDtypeStruct((2, r, c), f32),
                          compiler_params=_cparams("parallel"), name=name)(*scal, own, arrived, arrived, arrived)


def sum_slots(name, x):
    r, c = x.shape[1:]
    tr = _row_tile(r)

    def body(x_ref, o_ref):
        o_ref[...] = (x_ref[0].astype(f32) + x_ref[1].astype(f32)) + (x_ref[2].astype(f32) + x_ref[3].astype(f32))

    return pl.pallas_call(
        body, grid=(r // tr,), in_specs=[pl.BlockSpec((N_CHIPS, tr, c), lambda i: (0, i, 0))],
        out_specs=pl.BlockSpec((tr, c), lambda i: (i, 0)), out_shape=jax.ShapeDtypeStruct((r, c), f32),
        compiler_params=_cparams("parallel"), name=name)(x)


def adamw(name, w, g, m, v):
    r, c = w.shape
    tr = _row_tile(r)

    def body(w_ref, g_ref, m_ref, v_ref, d_ref, nm_ref, nv_ref):
        gv = g_ref[...]
        nm = ADAM_B1 * m_ref[...] + (1.0 - ADAM_B1) * gv
        nv = ADAM_B2 * v_ref[...] + (1.0 - ADAM_B2) * (gv * gv)
        m_hat = nm / (1.0 - ADAM_B1 ** ADAM_STEP)
        v_hat = nv / (1.0 - ADAM_B2 ** ADAM_STEP)
        d_ref[...] = -ADAM_LR * (m_hat / (jnp.sqrt(v_hat) + ADAM_EPS) + ADAM_WD * w_ref[...])
        nm_ref[...] = nm
        nv_ref[...] = nv

    blk = pl.BlockSpec((tr, c), lambda i: (i, 0))
    return pl.pallas_call(
        body, grid=(r // tr,), in_specs=[blk] * 4, out_specs=[blk] * 3,
        out_shape=[jax.ShapeDtypeStruct((r, c), f32)] * 3,
        compiler_params=_cparams("parallel"), name=name)(w, g, m, v)


def _pack_rows(n):
    return -(-n // PACK_COLS)


def _pack(arrs, dtype, row_multiple):
    segs = []
    for a in arrs:
        flat = a.astype(dtype).reshape(-1)
        k = _pack_rows(flat.shape[0])
        segs.append(jnp.pad(flat, (0, k * PACK_COLS - flat.shape[0])).reshape(k, PACK_COLS))
    rows = sum(s.shape[0] for s in segs)
    pad = -rows % row_multiple
    if pad:
        segs.append(jnp.zeros((pad, PACK_COLS), dtype))
    return jnp.concatenate(segs, 0)


def _unpack(packed, shapes):
    out, r = [], 0
    for shp in shapes:
        n = math.prod(shp)
        k = _pack_rows(n)
        out.append(packed[r:r + k].reshape(-1)[:n].reshape(shp))
        r += k
    return out


def _gathered_to_full(g, axis):
    t = jnp.moveaxis(g, 0, axis)
    return t.reshape(t.shape[:axis] + (t.shape[axis] * t.shape[axis + 1],) + t.shape[axis + 2:])


def _full_to_shard_major(full, axis):
    shp = full.shape
    t = full.reshape(shp[:axis] + (N_CHIPS, shp[axis] // N_CHIPS) + shp[axis + 1:])
    return jnp.moveaxis(t, axis, 0)


def kernel(x, mem, positions, hyb_w_in, dn_conv_w, dn_a_log, dn_dt_bias, dn_norm_g, hyb_w_out, s5_a_re, s5_a_im, s5_log_dt, s5_b_re, s5_b_im, s5_c_re, s5_c_im, s5_d, s5_glu_wo, s5_glu_wg, ln_mix_g, ln_mix_b, xq_w, xk_w, xv_w, xo_w, ln_x_g, ln_x_b, ffn_wg, ffn_wu, ffn_wd, ln_ffn_g, ln_ffn_b, loss_target, m_hyb_w_in, m_dn_conv_w, m_dn_a_log, m_dn_dt_bias, m_dn_norm_g, m_hyb_w_out, m_s5_a_re, m_s5_a_im, m_s5_log_dt, m_s5_b_re, m_s5_b_im, m_s5_c_re, m_s5_c_im, m_s5_d, m_s5_glu_wo, m_s5_glu_wg, m_ln_mix_g, m_ln_mix_b, m_xq_w, m_xk_w, m_xv_w, m_xo_w, m_ln_x_g, m_ln_x_b, m_ffn_wg, m_ffn_wu, m_ffn_wd, m_ln_ffn_g, m_ln_ffn_b, v_hyb_w_in, v_dn_conv_w, v_dn_a_log, v_dn_dt_bias, v_dn_norm_g, v_hyb_w_out, v_s5_a_re, v_s5_a_im, v_s5_log_dt, v_s5_b_re, v_s5_b_im, v_s5_c_re, v_s5_c_im, v_s5_d, v_s5_glu_wo, v_s5_glu_wg, v_ln_mix_g, v_ln_mix_b, v_xq_w, v_xk_w, v_xv_w, v_xo_w, v_ln_x_g, v_ln_x_b, v_ffn_wg, v_ffn_wu, v_ffn_wd, v_ln_ffn_g, v_ln_ffn_b):
    a = dict(locals())
    big = [n for n in WEIGHT_NAMES if n in SHARD_AXIS and n not in GATHER_F32]
    small = [n for n in WEIGHT_NAMES if n not in big]
    chip = 2 * lax.axis_index("x") + lax.axis_index("y")
    for n in FFN_TRANSPOSED:
        for pre in ("", "m_", "v_"):
            a[pre + n] = jnp.swapaxes(a[pre + n], 1, 2)

    mc = lax.axis_index("c")
    view2 = lambda t: t.reshape(-1, t.shape[-1])
    slots = [cast_into_slot("slot_" + n, view2(a[n]), chip, bf16).reshape((N_CHIPS,) + a[n].shape) for n in big]
    gathered = comm_gather_weights("gather_w", slots)
    tiny4 = _unpack_slots(comm_allgather4("gather_w_tiny", _pack([a[n] for n in GATHER_F32], f32, 8)),
                          [a[n].shape for n in GATHER_F32])
    p = {n: a[n] for n in small if n not in GATHER_F32}
    for n, g4 in zip(GATHER_F32, tiny4):
        p[n] = _gathered_to_full(g4, SHARD_AXIS[n])
    for n, g4 in zip(big, gathered):
        p[n] = g4 if n in SHARD_ORDER_GRADS else _gathered_to_full(g4, SHARD_AXIS[n])

    loss, grad_x, grads = local_step(x[0], mem[0], positions, loss_target[0], p)
    loss = lax.psum(loss, ("x", "y", "c"))

    g4s = [grads[n] for n in big]
    recv = comm_sibling_halves("rs_sibling_halves", g4s)
    pairs = []
    for n, g4, r4 in zip(big, g4s, recv):
        lh, cols = g4.shape[1] // 2, g4.shape[-1]
        v4 = g4.reshape(N_CHIPS, 2, -1, cols)
        pairs.append(add_own_half("rs_add_" + n, v4, r4.reshape(N_CHIPS, -1, cols), bf16).reshape((N_CHIPS, lh) + g4.shape[2:]))
    arrived = comm_alltoall4("rs_alltoall", pairs)
    slot3 = lambda t: t.reshape(N_CHIPS, -1, t.shape[-1])
    halves = [sum_chips_into_half("rs_sum_" + n, slot3(pr), slot3(ar), chip, mc) for n, pr, ar in zip(big, pairs, arrived)]
    g_big = {n: t.reshape(a[n].shape) for n, t in zip(big, comm_sibling_join("rs_sibling_join", halves))}

    rpack = _pack([grads[n] for n in small], f32, 8)
    rpair = _add2("ar_add_sibling", rpack, comm_sibling_swap("ar_sibling_swap", rpack))
    g_small = _unpack(sum_slots("ar_sum_chips", comm_allgather4("ar_allgather", rpair)), [grads[n].shape for n in small])
    g_small = {n: (lax.dynamic_index_in_dim(_full_to_shard_major(g, SHARD_AXIS[n]), chip, 0, keepdims=False)
                   if n in SHARD_AXIS else g) for n, g in zip(small, g_small)}

    outs = {}
    for n in big:
        view = lambda t: t.reshape(-1, t.shape[-1])
        d, nm, nv = adamw("adamw_" + n, view(a[n]), view(g_big[n]), view(a["m_" + n]), view(a["v_" + n]))
        outs[n] = (g_big[n],) + tuple(t.reshape(a[n].shape) for t in (d, nm, nv))
    shapes = [a[n].shape for n in small]
    packs = [_pack([a[pre + n] for n in small], f32, 8) for pre in ("", "m_", "v_")]
    upd = adamw("adamw_small", packs[0], _pack([g_small[n] for n in small], f32, 8), packs[1], packs[2])
    for k, n in enumerate(small):
        outs[n] = (g_small[n],) + tuple(_unpack(buf, shapes)[k] for buf in upd)
    for n in FFN_TRANSPOSED:
        outs[n] = tuple(jnp.swapaxes(t, 1, 2) for t in outs[n])
    res = [loss, grad_x[None]]
    for kind in range(4):
        res += [outs[n][kind] for n in WEIGHT_NAMES]
    return tuple(res)


def _unpack_slots(gathered, shapes):
    out, r = [], 0
    for shp in shapes:
        n = math.prod(shp)
        k = _pack_rows(n)
        out.append(gathered[:, r:r + k].reshape(N_CHIPS, -1)[:, :n].reshape((N_CHIPS,) + tuple(shp)))
        r += k
    return out
```

```python
import functools
import math

import jax
import jax.numpy as jnp
from jax import lax
from jax.experimental import pallas as pl
from jax.experimental.pallas import tpu as pltpu

f32 = jnp.float32
bf16 = jnp.bfloat16

D_MODEL = 1024
DEPTH = 4
DN_HEADS = 4
DN_HEAD_DIM = 128
DN_KEY_DIM = 512
DN_QKV_DIM = 1536
DN_CONV = 4
SW_HEADS = 8
SW_HEAD_DIM = 64
SW_DIM = 512
SW_DILATIONS = (1, 4, 16)
SW_BLOCK = 128
ROPE_THETA = 10000.0
S5_GROUP = 16
S5_GROUPS = 64
S5_STATE = 64
X_HEADS = 4
X_HEAD_DIM = 256
FFN_HIDDEN = 2816
ALPHA = (2 * DEPTH) ** 0.25
LN_EPS = 1e-5
RMS_EPS = 1e-6
ADAM_LR, ADAM_B1, ADAM_B2, ADAM_EPS, ADAM_WD, ADAM_STEP = 0.001, 0.9, 0.999, 1e-08, 0.01, 10

BA_PAD = 256
PROJ_COLS = DN_QKV_DIM + DN_KEY_DIM + 3 * SW_DIM + BA_PAD
COL_Z = DN_QKV_DIM
COL_SWQ = COL_Z + DN_KEY_DIM
COL_SWK = COL_SWQ + SW_DIM
COL_SWV = COL_SWK + SW_DIM
COL_BA = COL_SWV + SW_DIM

LANES = 128
SUBLANES = 8
VMEM_LIMIT = 56 * 1024 * 1024
DN_CHUNK = 128
DN_HEADS_PER_STEP = 4


def _cparams(*sem):
    return pltpu.CompilerParams(dimension_semantics=tuple(sem), vmem_limit_bytes=VMEM_LIMIT)


def _dg(x, y, cx, cy):
    return lax.dot_general(x, y, (((cx,), (cy,)), ((), ())), preferred_element_type=f32)


@functools.partial(jax.custom_vjp, nondiff_argnums=(2, 3))
def bdot(a, b, ca, cb):
    return _dg(a.astype(bf16), b.astype(bf16), ca, cb)


def _bdot_fwd(a, b, ca, cb):
    return bdot(a, b, ca, cb), (a, b)


def _bdot_bwd(ca, cb, res, g):
    a, b = res
    g16, a16, b16 = g.astype(bf16), a.astype(bf16), b.astype(bf16)
    da = _dg(g16, b16, 1, 1 - cb) if ca == 1 else _dg(b16, g16, 1 - cb, 1)
    db = _dg(a16, g16, 1 - ca, 0) if cb == 0 else _dg(g16, a16, 0, 1 - ca)
    return da.astype(a.dtype), db.astype(b.dtype)


bdot.defvjp(_bdot_fwd, _bdot_bwd)


def _split_hi_lo(a):
    hi = a.astype(bf16)
    return hi, (a - hi.astype(f32)).astype(bf16)


def _dot3(a, b, ca, cb):
    a_hi, a_lo = _split_hi_lo(a)
    b_hi, b_lo = _split_hi_lo(b)
    return _dg(a_hi, b_hi, ca, cb) + (_dg(a_hi, b_lo, ca, cb) + _dg(a_lo, b_hi, ca, cb))


def hdot(a, b):
    return jnp.dot(a, b, precision=lax.Precision.HIGHEST, preferred_element_type=f32)


def _iota2(shape, dim):
    return lax.broadcasted_iota(jnp.int32, shape, dim)


def _row_spec(r, tm):
    if isinstance(r, tuple):
        arr, width, blk = r
        return arr, pl.BlockSpec((tm, width), lambda i, _b=blk: (i, _b))
    return r, pl.BlockSpec((tm, r.shape[1]), lambda i: (i, 0))


def _par_spec(p):
    return pl.BlockSpec(p.shape, lambda i, _n=p.ndim: (0,) * _n)


def rowmap(name, fn, rows, params, out_cols, tm, out_dtypes=None):
    arrs, specs = zip(*[_row_spec(r, tm) for r in rows])
    s = arrs[0].shape[0]
    n_in = len(rows) + len(params)
    out_dtypes = out_dtypes or [f32] * len(out_cols)

    def body(*refs):
        outs = fn(*[r[...] for r in refs[:n_in]])
        for o_ref, o in zip(refs[n_in:], outs):
            o_ref[...] = o.astype(o_ref.dtype)

    return pl.pallas_call(
        body, grid=(s // tm,),
        in_specs=list(specs) + [_par_spec(p) for p in params],
        out_specs=[pl.BlockSpec((tm, c), lambda i: (i, 0)) for c in out_cols],
        out_shape=[jax.ShapeDtypeStruct((s, c), dt) for c, dt in zip(out_cols, out_dtypes)],
        compiler_params=_cparams("parallel"), name=name)(*arrs, *params)


def rowmap_bwd(name, fn, rows, params, cts, tm, row_mask=None, par_mask=None, row_dtypes=None):
    arrs, specs = zip(*[_row_spec(r, tm) for r in rows])
    s = arrs[0].shape[0]
    ct_groups = [c if isinstance(c, list) else [c] for c in cts]
    ct_arrs, ct_specs = zip(*[_row_spec(a, tm) for grp in ct_groups for a in grp])
    cts = list(ct_arrs)
    nr, npar, nct = len(rows), len(params), len(cts)
    row_mask = row_mask or [True] * nr
    par_mask = par_mask or [True] * npar
    row_idx = [k for k in range(nr) if row_mask[k]]
    par_idx = [k for k in range(npar) if par_mask[k]]
    row_w = [specs[k].block_shape[1] for k in row_idx]

    def body(*refs):
        ins = [r[...] for r in refs[:nr + npar]]
        ct_refs = list(refs[nr + npar:nr + npar + nct])
        ctv = []
        for grp in ct_groups:
            acc = ct_refs.pop(0)[...].astype(f32)
            for _ in grp[1:]:
                acc = acc + ct_refs.pop(0)[...].astype(f32)
            ctv.append(acc)
        ctv = tuple(ctv)
        outs = refs[nr + npar + nct:]
        _, vjp = jax.vjp(fn, *ins)
        grads = vjp(ctv)
        for o_ref, k in zip(outs[:len(row_idx)], row_idx):
            o_ref[...] = grads[k].astype(o_ref.dtype)
        first = pl.program_id(0) == 0
        for o_ref, k in zip(outs[len(row_idx):], par_idx):
            g = grads[nr + k].astype(f32)

            @pl.when(first)
            def _(o_ref=o_ref, g=g):
                o_ref[...] = g

            @pl.when(jnp.logical_not(first))
            def _(o_ref=o_ref, g=g):
                o_ref[...] += g

    res = pl.pallas_call(
        body, grid=(s // tm,),
        in_specs=list(specs) + [_par_spec(p) for p in params]
        + list(ct_specs),
        out_specs=[pl.BlockSpec((tm, w), lambda i: (i, 0)) for w in row_w]
        + [_par_spec(params[k]) for k in par_idx],
        out_shape=[jax.ShapeDtypeStruct((s, w), dt) for w, dt in zip(row_w, row_dtypes or [f32] * len(row_w))]
        + [jax.ShapeDtypeStruct(params[k].shape, f32) for k in par_idx],
        compiler_params=_cparams("arbitrary"), name=name)(*arrs, *params, *cts)
    return list(res[:len(row_idx)]), list(res[len(row_idx):])


def _pick(n, prefs):
    for t in prefs:
        if n % t == 0:
            return t
    return n


MM_CHUNK = 512
MM_WIDE = 2048


def mm_nn(name, a, b, out_dtype=f32):
    m, k = a.shape
    n = b.shape[1]
    tm = _pick(m, ((1024,) if n <= MM_WIDE else ()) + (512, 256, 128))
    cn = _pick(n, (MM_CHUNK, 256, 128))

    def body(a_ref, b_ref, o_ref):
        av = a_ref[...].astype(bf16)
        for c in range(n // cn):
            sl = slice(c * cn, (c + 1) * cn)
            o_ref[:, sl] = _dg(av, b_ref[:, sl].astype(bf16), 1, 0).astype(o_ref.dtype)

    return pl.pallas_call(
        body, grid=(m // tm,),
        in_specs=[pl.BlockSpec((tm, k), lambda i: (i, 0)), pl.BlockSpec((k, n), lambda i: (0, 0))],
        out_specs=pl.BlockSpec((tm, n), lambda i: (i, 0)),
        out_shape=jax.ShapeDtypeStruct((m, n), out_dtype),
        compiler_params=_cparams("parallel"), name=name)(a, b)


def mm_nt(name, a, b, out_dtype=f32):
    m, n = a.shape
    k = b.shape[0]
    tm = _pick(m, ((1024,) if n <= MM_WIDE else ()) + (512, 256, 128))
    ck = _pick(k, (MM_CHUNK, 256, 128))

    def body(a_ref, b_ref, o_ref):
        av = a_ref[...].astype(bf16)
        for c in range(k // ck):
            sl = slice(c * ck, (c + 1) * ck)
            o_ref[:, sl] = _dg(av, b_ref[sl, :].astype(bf16), 1, 1).astype(o_ref.dtype)

    return pl.pallas_call(
        body, grid=(m // tm,),
        in_specs=[pl.BlockSpec((tm, n), lambda i: (i, 0)), pl.BlockSpec((k, n), lambda i: (0, 0))],
        out_specs=pl.BlockSpec((tm, k), lambda i: (i, 0)),
        out_shape=jax.ShapeDtypeStruct((m, k), out_dtype),
        compiler_params=_cparams("parallel"), name=name)(a, b)


def _stacked(buf):
    if buf is None:
        return [], [], {}
    return [buf], [pl.BlockSpec(memory_space=pl.ANY)], None


def mm_tn(name, a, b, out_dtype=f32, b_col0=0, n_cols=None, stack=None):
    s, m = a.shape
    n = n_cols or b.shape[1]
    tn = _pick(n, ((512,) if a.dtype == bf16 and b.dtype == bf16 else ()) + (256, 128))
    cm = _pick(m, (256, 128))
    col0 = b_col0 // tn
    in_specs = [pl.BlockSpec((s, m), lambda j: (0, 0)), pl.BlockSpec((s, tn), lambda j: (0, j + col0))]

    if stack is None:
        def body(a_ref, b_ref, o_ref):
            bv = b_ref[...].astype(bf16)
            for c in range(m // cm):
                sl = slice(c * cm, (c + 1) * cm)
                o_ref[sl, :] = _dg(a_ref[:, sl].astype(bf16), bv, 0, 0).astype(o_ref.dtype)

        return pl.pallas_call(
            body, grid=(n // tn,), in_specs=in_specs, out_specs=pl.BlockSpec((m, tn), lambda j: (0, j)),
            out_shape=jax.ShapeDtypeStruct((m, n), out_dtype),
            compiler_params=_cparams("parallel"), name=name)(a, b)

    buf, layer, n_layers = stack
    assert cm * N_CHIPS == m
    extra, extra_specs, _ = _stacked(buf)

    def body_stacked(a_ref, b_ref, *rest):
        o_ref = rest[-1]
        bv = b_ref[...].astype(bf16)
        for c in range(N_CHIPS):
            o_ref[c] = _dg(a_ref[:, c * cm:(c + 1) * cm].astype(bf16), bv, 0, 0).astype(o_ref.dtype)

    return pl.pallas_call(
        body_stacked, grid=(n // tn,), in_specs=in_specs + extra_specs,
        out_specs=pl.BlockSpec((N_CHIPS, None, cm, tn), lambda j: (0, layer, 0, j)),
        out_shape=jax.ShapeDtypeStruct((N_CHIPS, n_layers, cm, n), out_dtype),
        input_output_aliases={2: 0} if extra else {},
        compiler_params=_cparams("parallel"), name=name)(a, b, *extra)


def _postnorm_tile(h, sub, g, b):
    z = ALPHA * h + sub
    mu = jnp.mean(z, -1, keepdims=True)
    zc = z - mu
    var = jnp.mean(zc * zc, -1, keepdims=True)
    return (zc * lax.rsqrt(var + LN_EPS) * g + b,)


def _glu_tile(og):
    o, g = og[:, :D_MODEL], og[:, D_MODEL:]
    return (o * jax.nn.sigmoid(g),)


def _xattn_tile(q, kv):
    outs = []
    for h in range(X_HEADS):
        sl = slice(h * X_HEAD_DIM, (h + 1) * X_HEAD_DIM)
        s = bdot(q[:, sl], kv[:, sl], 1, 1) * (X_HEAD_DIM ** -0.5)
        m = lax.stop_gradient(jnp.max(s, -1, keepdims=True))
        p = jnp.exp(s - m)
        p = p / jnp.sum(p, -1, keepdims=True)
        outs.append(bdot(p, kv[:, D_MODEL + h * X_HEAD_DIM:D_MODEL + (h + 1) * X_HEAD_DIM], 1, 0))
    return (jnp.concatenate(outs, -1),)


TM_ROW = 512


def postnorm_fwd(tag, h, sub, g, b):
    return rowmap("postnorm_" + tag, lambda *a: _postnorm_tile(*a) * 2, [h, sub], [g, b], [D_MODEL] * 2, TM_ROW,
                  out_dtypes=[f32, bf16])


def postnorm_bwd(tag, h, sub, g, b, dy):
    (dh, dsub), (dg, db) = rowmap_bwd("postnorm_bwd_" + tag, _postnorm_tile, [h, sub], [g, b], [dy], TM_ROW,
                                      row_dtypes=[f32, bf16])
    return dh, dsub, dg, db


def xattn_fwd(tag, h, mem, wq, wkv, wo):
    q = mm_nn("xq_" + tag, h, wq, out_dtype=bf16)
    kv = mm_nn("xkv_" + tag, mem, wkv)
    ao = rowmap("xattn_" + tag, _xattn_tile, [q], [kv], [D_MODEL], TM_ROW, out_dtypes=[bf16])[0]
    out = mm_nn("xo_" + tag, ao, wo)
    return out, (q, kv, ao)


def xattn_bwd(tag, layer, h, mem, wq, wkv, wo, res, dout, stacks):
    q, kv, ao = res
    sq, sk, sv, so = stacks
    so = mm_tn("xo_dw_" + tag, ao, dout, stack=(so, layer, DEPTH))
    dao = mm_nt("xo_dx_" + tag, dout, wo)
    (dq,), (dkv,) = rowmap_bwd("xattn_bwd_" + tag, _xattn_tile, [q], [kv], [dao], TM_ROW, row_dtypes=[bf16])
    sq = mm_tn("xq_dw_" + tag, h, dq, stack=(sq, layer, DEPTH))
    dh = mm_nt("xq_dx_" + tag, dq, wq)
    sk = mm_tn("xk_dw_" + tag, mem, dkv, n_cols=D_MODEL, stack=(sk, layer, DEPTH))
    sv = mm_tn("xv_dw_" + tag, mem, dkv, b_col0=D_MODEL, n_cols=D_MODEL, stack=(sv, layer, DEPTH))
    return dh, (sq, sk, sv, so)


FFN_SHARD = FFN_HIDDEN // 4
TM_FFN = 512
TM_FFN_WIDE = 1024


def _silu_mul(a, u):
    return jax.nn.silu(a) * u


def ffn_fwd(tag, layer, h, wg, wu, wd):
    s = h.shape[0]
    tm, fs = TM_FFN, FFN_SHARD
    w_in = pl.BlockSpec((None, None, fs, D_MODEL), lambda k, i: (k, layer, 0, 0))
    tu = TM_FFN_WIDE
    act = pl.BlockSpec((None, tu, fs), lambda k, i: (k, i, 0))

    def up_body(h_ref, wg_ref, wu_ref, a_ref, u_ref, hid_ref):
        hv = h_ref[...].astype(bf16)
        a, u = _dg(hv, wg_ref[...], 1, 1), _dg(hv, wu_ref[...], 1, 1)
        a_ref[...], u_ref[...] = a.astype(bf16), u.astype(bf16)
        hid_ref[...] = _silu_mul(a, u).astype(bf16)

    a4, u4, hid4 = pl.pallas_call(
        up_body, grid=(4, s // tu),
        in_specs=[pl.BlockSpec((tu, D_MODEL), lambda k, i: (i, 0)), w_in, w_in],
        out_specs=[act, act, act],
        out_shape=[jax.ShapeDtypeStruct((4, s, fs), bf16)] * 3,
        compiler_params=_cparams("parallel", "parallel"), name="ffn_up_" + tag)(h, wg, wu)

    all_act = pl.BlockSpec((4, tm, fs), lambda i: (0, i, 0))
    all_w = pl.BlockSpec((4, None, fs, D_MODEL), lambda i: (0, layer, 0, 0))

    def down_body(hid_ref, wd_ref, o_ref):
        acc = _dg(hid_ref[0], wd_ref[0], 1, 0)
        for k in range(1, 4):
            acc = acc + _dg(hid_ref[k], wd_ref[k], 1, 0)
        o_ref[...] = acc

    out = pl.pallas_call(
        down_body, grid=(s // tm,), in_specs=[all_act, all_w],
        out_specs=pl.BlockSpec((tm, D_MODEL), lambda i: (i, 0)),
        out_shape=jax.ShapeDtypeStruct((s, D_MODEL), f32),
        compiler_params=_cparams("parallel"), name="ffn_down_" + tag)(hid4, wd)
    return out, (a4, u4, hid4)


def ffn_bwd(tag, layer, h, wg, wu, wd, res, dout, stacks=None):
    a4, u4, hid4 = res
    s = h.shape[0]
    tm, fs = TM_FFN, FFN_SHARD
    tu = TM_FFN_WIDE
    act = pl.BlockSpec((None, tu, fs), lambda k, i: (k, i, 0))

    def dact_body(do_ref, wd_ref, a_ref, u_ref, da_ref, du_ref):
        dhid = _dg(do_ref[...].astype(bf16), wd_ref[...], 1, 1)
        _, vjp = jax.vjp(_silu_mul, a_ref[...].astype(f32), u_ref[...].astype(f32))
        da, du = vjp(dhid)
        da_ref[...], du_ref[...] = da.astype(bf16), du.astype(bf16)

    da4, du4 = pl.pallas_call(
        dact_body, grid=(4, s // tu),
        in_specs=[pl.BlockSpec((tu, D_MODEL), lambda k, i: (i, 0)),
                  pl.BlockSpec((None, None, fs, D_MODEL), lambda k, i: (k, layer, 0, 0)), act, act],
        out_specs=[act, act], out_shape=[jax.ShapeDtypeStruct((4, s, fs), bf16)] * 2,
        compiler_params=_cparams("parallel", "parallel"), name="ffn_dact_" + tag)(dout, wd, a4, u4)

    all_act = pl.BlockSpec((4, tm, fs), lambda i: (0, i, 0))
    all_w = pl.BlockSpec((4, None, fs, D_MODEL), lambda i: (0, layer, 0, 0))

    def dx_body(da_ref, du_ref, wg_ref, wu_ref, o_ref):
        acc = _dg(da_ref[0], wg_ref[0], 1, 0) + _dg(du_ref[0], wu_ref[0], 1, 0)
        for k in range(1, 4):
            acc = acc + (_dg(da_ref[k], wg_ref[k], 1, 0) + _dg(du_ref[k], wu_ref[k], 1, 0))
        o_ref[...] = acc

    dh = pl.pallas_call(
        dx_body, grid=(s // tm,), in_specs=[all_act, all_act, all_w, all_w],
        out_specs=pl.BlockSpec((tm, D_MODEL), lambda i: (i, 0)),
        out_shape=jax.ShapeDtypeStruct((s, D_MODEL), f32),
        compiler_params=_cparams("parallel"), name="ffn_dx_" + tag)(da4, du4, wg, wu)

    tn = 256
    whole = pl.BlockSpec((None, s, fs), lambda k: (k, 0, 0))
    resident = pl.BlockSpec((s, D_MODEL), lambda k: (0, 0), pipeline_mode=pl.Buffered(1))

    def dwin_body(h_ref, da_ref, du_ref, *rest):
        dwg_ref, dwu_ref = rest[-2:]
        da, du = da_ref[...], du_ref[...]
        for c in range(D_MODEL // tn):
            sl = slice(c * tn, (c + 1) * tn)
            hv = h_ref[:, sl].astype(bf16)
            dwg_ref[:, sl] = _dg(da, hv, 0, 0)
            dwu_ref[:, sl] = _dg(du, hv, 0, 0)

    n_layers = wd.shape[1]
    layer_out = pl.BlockSpec((None, None, fs, D_MODEL), lambda k: (k, layer, 0, 0))
    stack_shape = jax.ShapeDtypeStruct((4, n_layers, fs, D_MODEL), f32)
    prev = [] if stacks is None else list(stacks)
    any_spec = [pl.BlockSpec(memory_space=pl.ANY)]

    dwg, dwu = pl.pallas_call(
        dwin_body, grid=(4,),
        in_specs=[resident, whole, whole] + any_spec * len(prev[:2]),
        out_specs=[layer_out] * 2, out_shape=[stack_shape] * 2,
        input_output_aliases={3: 0, 4: 1} if prev else {},
        compiler_params=_cparams("parallel"), name="ffn_dwin_" + tag)(h, da4, du4, *prev[:2])

    def dwd_body(hid_ref, do_ref, *rest):
        hid = hid_ref[...]
        for c in range(D_MODEL // tn):
            sl = slice(c * tn, (c + 1) * tn)
            rest[-1][:, sl] = _dg(hid, do_ref[:, sl].astype(bf16), 0, 0)

    dwd = pl.pallas_call(
        dwd_body, grid=(4,),
        in_specs=[whole, resident] + any_spec * len(prev[2:]),
        out_specs=layer_out, out_shape=stack_shape,
        input_output_aliases={2: 0} if prev else {},
        compiler_params=_cparams("parallel"), name="ffn_dwd_" + tag)(hid4, dout, *prev[2:])
    return dh, (dwg, dwu, dwd)


def loss_head(y, target):
    s, d = y.shape
    tm = TM_ROW

    def body(y_ref, t_ref, part_ref, dy_ref):
        e = y_ref[...] - t_ref[...]
        dy_ref[...] = e * (1.0 / d)
        p = jnp.sum(e * e, 0, keepdims=True) * (0.5 / d)

        @pl.when(pl.program_id(0) == 0)
        def _():
            part_ref[...] = p

        @pl.when(pl.program_id(0) != 0)
        def _():
            part_ref[...] += p

    return pl.pallas_call(
        body, grid=(s // tm,),
        in_specs=[pl.BlockSpec((tm, d), lambda i: (i, 0))] * 2,
        out_specs=[pl.BlockSpec((1, d), lambda i: (0, 0)), pl.BlockSpec((tm, d), lambda i: (i, 0))],
        out_shape=[jax.ShapeDtypeStruct((1, d), f32), jax.ShapeDtypeStruct((s, d), f32)],
        compiler_params=_cparams("arbitrary"), name="loss_head")(y, target)


TM_CONV = 1024


def _conv_rows(xx, w_ref, n_rows):
    a = w_ref[3:4, :] * xx
    for k in (1, 2, 3):
        a = a + w_ref[3 - k:4 - k, :] * pltpu.roll(xx, k, 0)
    return a


def _dn_act(a, is_qk):
    s = jax.nn.silu(a)
    n = s * lax.rsqrt(jnp.sum(s * s, -1, keepdims=True) + RMS_EPS)
    return jnp.where(is_qk, n, s)


def dn_conv_fwd(tag, proj, cw):
    s = proj.shape[0]
    tm, hb = TM_CONV, TM_CONV // SUBLANES

    def body(xh_ref, x_ref, w_ref, o_ref):
        j, t = pl.program_id(0), pl.program_id(1)
        halo = jnp.where(t > 0, xh_ref[...], 0.0)
        xx = jnp.concatenate([halo, x_ref[...]], 0)
        a = _conv_rows(xx, w_ref, tm + SUBLANES)
        o_ref[...] = _dn_act(a, j < 2 * DN_HEADS)[SUBLANES:, :]

    return pl.pallas_call(
        body, grid=(DN_QKV_DIM // LANES, s // tm),
        in_specs=[pl.BlockSpec((SUBLANES, LANES), lambda j, t: (jnp.maximum(t * hb - 1, 0), j)),
                  pl.BlockSpec((tm, LANES), lambda j, t: (t, j)),
                  pl.BlockSpec((DN_CONV, LANES), lambda j, t: (0, j))],
        out_specs=pl.BlockSpec((tm, LANES), lambda j, t: (t, j)),
        out_shape=jax.ShapeDtypeStruct((s, DN_QKV_DIM), f32),
        compiler_params=_cparams("parallel", "parallel"), name="dn_conv_" + tag)(proj, proj, cw)


def dn_conv_bwd(tag, proj, cw, dy):
    s = proj.shape[0]
    tm, hb = TM_CONV, TM_CONV // SUBLANES
    nt = s // tm
    n_ext = tm + 2 * SUBLANES

    def body(xb_ref, x_ref, xa_ref, dy_ref, dya_ref, w_ref, dx_ref, dw_ref):
        j, t = pl.program_id(0), pl.program_id(1)
        xx = jnp.concatenate([jnp.where(t > 0, xb_ref[...], 0.0), x_ref[...],
                              jnp.where(t < nt - 1, xa_ref[...], 0.0)], 0)
        dyy = jnp.concatenate([jnp.zeros((SUBLANES, LANES), f32), dy_ref[...],
                               jnp.where(t < nt - 1, dya_ref[...], 0.0)], 0)
        a = _conv_rows(xx, w_ref, n_ext)
        _, vjp = jax.vjp(lambda v: _dn_act(v, j < 2 * DN_HEADS), a)
        da, = vjp(dyy)
        dx = w_ref[3:4, :] * da
        for k in (1, 2, 3):
            dx = dx + w_ref[3 - k:4 - k, :] * pltpu.roll(da, n_ext - k, 0)
        dx_ref[...] = dx[SUBLANES:SUBLANES + tm, :]
        row = _iota2((n_ext, LANES), 0)
        da_in = jnp.where((row >= SUBLANES) & (row < SUBLANES + tm), da, 0.0)
        r8 = _iota2((SUBLANES, LANES), 0)
        dw = jnp.zeros((SUBLANES, LANES), f32)
        for k in range(DN_CONV):
            xs = xx if k == 0 else pltpu.roll(xx, k, 0)
            dw = dw + jnp.where(r8 == 3 - k, jnp.sum(da_in * xs, 0, keepdims=True), 0.0)

        @pl.when(t == 0)
        def _():
            dw_ref[...] = dw

        @pl.when(t != 0)
        def _():
            dw_ref[...] += dw

    nb8 = s // SUBLANES
    return pl.pallas_call(
        body, grid=(DN_QKV_DIM // LANES, nt),
        in_specs=[pl.BlockSpec((SUBLANES, LANES), lambda j, t: (jnp.maximum(t * hb - 1, 0), j)),
                  pl.BlockSpec((tm, LANES), lambda j, t: (t, j)),
                  pl.BlockSpec((SUBLANES, LANES), lambda j, t: (jnp.minimum((t + 1) * hb, nb8 - 1), j)),
                  pl.BlockSpec((tm, LANES), lambda j, t: (t, j)),
                  pl.BlockSpec((SUBLANES, LANES), lambda j, t: (jnp.minimum((t + 1) * hb, nb8 - 1), j)),
                  pl.BlockSpec((DN_CONV, LANES), lambda j, t: (0, j))],
        out_specs=[pl.BlockSpec((tm, LANES), lambda j, t: (t, j)),
                   pl.BlockSpec((SUBLANES, LANES), lambda j, t: (0, j))],
        out_shape=[jax.ShapeDtypeStruct((s, DN_QKV_DIM), f32), jax.ShapeDtypeStruct((SUBLANES, DN_QKV_DIM), f32)],
        compiler_params=_cparams("parallel", "arbitrary"), name="dn_conv_bwd_" + tag)(proj, proj, proj, dy, dy, cw)


def _gate_tile(ba, eb, ea, alog, dtb):
    beta = jax.nn.sigmoid(hdot(ba, eb))
    g = -jnp.exp(alog) * jax.nn.softplus(hdot(ba, ea) + dtb)
    return beta, g


def _each(fn, *lists):
    return [fn(*args) for args in zip(*lists)]


@functools.partial(jax.custom_vjp, nondiff_argnums=(1,))
def _halves(x, axis):
    h = x.shape[axis] // 2
    return (x[:h], x[h:]) if axis == 0 else (x[:, :h], x[:, h:])


def _halves_fwd(x, axis):
    return _halves(x, axis), None


def _halves_bwd(axis, _, g):
    return (jnp.concatenate(g, axis),)


_halves.defvjp(_halves_fwd, _halves_bwd)


def _tri_inv_unit(lowers):
    c = lowers[0].shape[0]
    r, col = _iota2((c, c), 0), _iota2((c, c), 1)
    eye = jnp.where(r == col, 1.0, 0.0).astype(f32)
    invs = None
    sh = 0
    while (1 << sh) < c:
        same_2b = lax.shift_right_logical(r, sh + 1) == lax.shift_right_logical(col, sh + 1)
        diff_b = lax.shift_right_logical(r, sh) != lax.shift_right_logical(col, sh)
        offs = [jnp.where(same_2b & diff_b, low, 0.0) for low in lowers]
        if invs is None:
            invs = [eye - off for off in offs]
        else:
            part = _each(lambda inv, off: _dot3(inv, off, 1, 0), invs, offs)
            invs = _each(lambda inv, p: inv - _dot3(p, inv, 1, 0), invs, part)
        sh += 1
    return invs


@jax.custom_vjp
def _known_inverse(lower, tinv):
    return tinv


def _known_inverse_fwd(lower, tinv):
    return tinv, tinv


def _known_inverse_bwd(tinv, g):
    tt = tinv.T
    return -hdot(hdot(tt, g), tt), jnp.zeros_like(tinv)


_known_inverse.defvjp(_known_inverse_fwd, _known_inverse_bwd)


def _delta_chunk(q, k, v, gb, betab, state, tinv_known=None):
    c, hd = DN_CHUNK, DN_HEAD_DIM
    r, col = _iota2((c, c), 0), _iota2((c, c), 1)
    causal, strict = r >= col, r > col
    tril = jnp.where(causal, 1.0, 0.0).astype(f32)
    gc = _each(lambda g: hdot(tril, g), gb)
    decay = _each(lambda g: jnp.where(causal, jnp.exp(jnp.where(causal, g - g.T, 0.0)), 0.0), gc)
    qs = _each(lambda t: t * (DN_HEAD_DIM ** -0.5), q)
    kb = _each(lambda a, b: a * b, k, betab)
    kq = _each(lambda a, b, kk: _halves(bdot(jnp.concatenate([a, b], 0), kk, 1, 1), 0), kb, qs, k)
    lower = _each(lambda x, d: jnp.where(strict, x[0], 0.0) * d, kq, decay)
    intra = _each(lambda x, d: x[1] * d, kq, decay)
    tinv = _tri_inv_unit(lower) if tinv_known is None else _each(_known_inverse, lower, tinv_known)
    eg = _each(jnp.exp, gc)
    uw = _each(lambda t, vv, b, kb_, e: _halves(hdot(t, jnp.concatenate([vv * b, kb_ * e], 1)), 1),
               tinv, v, betab, kb, eg)
    gl = _each(lambda g: jnp.sum(jnp.where(r == c - 1, g, 0.0), 0, keepdims=True), gc)
    k_dec = _each(lambda kk, a, g: kk * jnp.exp(a - g), k, gl, gc)
    ws = _each(lambda x, t, e, st: _halves(bdot(jnp.concatenate([x[1], t * e], 0), st, 1, 0), 0), uw, qs, eg, state)
    v_new = _each(lambda x, y: x[0] - y[0], uw, ws)
    out = _each(lambda y, a, vn: y[1] + bdot(a, vn, 1, 0), ws, intra, v_new)
    new_state = _each(lambda st, a, kd, vn: st * jnp.exp(a) + bdot(kd, vn, 0, 0), state, gl, k_dec, v_new)
    return tuple(out), tuple(new_state), tuple(tinv)


def delta_fwd(tag, qkv, gb, betab):
    s = qkv.shape[0]
    c, hd = DN_CHUNK, DN_HEAD_DIM
    n = s // c

    hg, ng = DN_HEADS_PER_STEP, DN_HEADS // DN_HEADS_PER_STEP

    def body(q_ref, k_ref, v_ref, g_ref, b_ref, o_ref, st_ref, ti_ref, state):
        @pl.when(pl.program_id(1) == 0)
        def _():
            state[...] = jnp.zeros_like(state)

        heads = lambda ref: tuple(ref[:, j * hd:(j + 1) * hd] for j in range(hg))
        st = tuple(state[j] for j in range(hg))
        outs, news, tinv = _delta_chunk(heads(q_ref), heads(k_ref), heads(v_ref), heads(g_ref), heads(b_ref), st)
        for j in range(hg):
            st_ref[j] = st[j]
            ti_ref[j] = tinv[j]
            o_ref[:, j * hd:(j + 1) * hd] = outs[j]
            state[j] = news[j]

    blk = lambda off: pl.BlockSpec((c, hg * hd), lambda h, i, _o=off: (i, h + _o))
    per_chunk = pl.BlockSpec((hg, None, hd, hd), lambda h, i: (h, i, 0, 0))
    return pl.pallas_call(
        body, grid=(ng, n),
        in_specs=[blk(0), blk(ng), blk(2 * ng), blk(0), blk(0)],
        out_specs=[blk(0), per_chunk, per_chunk],
        out_shape=[jax.ShapeDtypeStruct((s, DN_KEY_DIM), f32)] + [jax.ShapeDtypeStruct((DN_HEADS, n, hd, hd), f32)] * 2,
        scratch_shapes=[pltpu.VMEM((hg, hd, hd), f32)],
        compiler_params=_cparams("parallel", "arbitrary"), name="delta_" + tag)(qkv, qkv, qkv, gb, betab)


def delta_bwd(tag, qkv, gb, betab, states, tinvs, do):
    s = qkv.shape[0]
    c, hd = DN_CHUNK, DN_HEAD_DIM
    n = s // c

    hg, ng = DN_HEADS_PER_STEP, DN_HEADS // DN_HEADS_PER_STEP

    def body(q_ref, k_ref, v_ref, g_ref, b_ref, st_ref, ti_ref, do_ref, dqkv_ref, dg_ref, db_ref, dstate):
        @pl.when(pl.program_id(1) == 0)
        def _():
            dstate[...] = jnp.zeros_like(dstate)

        heads = lambda ref: tuple(ref[:, j * hd:(j + 1) * hd] for j in range(hg))
        tinv = tuple(ti_ref[j] for j in range(hg))
        _, vjp = jax.vjp(lambda *args: _delta_chunk(*args, tinv_known=tinv)[:2],
                         heads(q_ref), heads(k_ref), heads(v_ref), heads(g_ref), heads(b_ref),
                         tuple(st_ref[j] for j in range(hg)))
        grads = vjp((heads(do_ref), tuple(dstate[j] for j in range(hg))))
        for part, g in enumerate(grads[:3]):
            for j in range(hg):
                dqkv_ref[:, part * DN_KEY_DIM + j * hd:part * DN_KEY_DIM + (j + 1) * hd] = g[j]
        for ref, g in zip((dg_ref, db_ref), grads[3:5]):
            for j in range(hg):
                ref[:, j * hd:(j + 1) * hd] = g[j]
        for j in range(hg):
            dstate[j] = grads[5][j]

    assert ng == 1
    blk = lambda off: pl.BlockSpec((c, hg * hd), lambda h, i, _o=off: (n - 1 - i, h + _o))
    return pl.pallas_call(
        body, grid=(ng, n),
        in_specs=[blk(0), blk(ng), blk(2 * ng), blk(0), blk(0)]
        + [pl.BlockSpec((hg, None, hd, hd), lambda h, i: (h, n - 1 - i, 0, 0))] * 2 + [blk(0)],
        out_specs=[pl.BlockSpec((c, DN_QKV_DIM), lambda h, i: (n - 1 - i, 0)), blk(0), blk(0)],
        out_shape=[jax.ShapeDtypeStruct((s, DN_QKV_DIM), f32)] + [jax.ShapeDtypeStruct((s, DN_KEY_DIM), f32)] * 2,
        scratch_shapes=[pltpu.VMEM((hg, hd, hd), f32)],
        compiler_params=_cparams("parallel", "arbitrary"),
        name="delta_bwd_" + tag)(qkv, qkv, qkv, gb, betab, states, tinvs, do)


def _dn_out_tile(o, z, ng):
    outs = []
    for h in range(DN_HEADS):
        sl = slice(h * DN_HEAD_DIM, (h + 1) * DN_HEAD_DIM)
        oh = o[:, sl]
        nrm = oh * lax.rsqrt(jnp.mean(oh * oh, -1, keepdims=True) + RMS_EPS) * ng[:, sl]
        outs.append(nrm * jax.nn.silu(z[:, sl]))
    return (jnp.concatenate(outs, -1),)


def _head_selectors():
    r, c = _iota2((BA_PAD, DN_KEY_DIM), 0), _iota2((BA_PAD, DN_KEY_DIM), 1) // DN_HEAD_DIM
    return (r == c).astype(f32), (r == c + DN_HEADS).astype(f32)


def dn_mixer_fwd(tag, proj, cw, alog_b, dtb_b, ng_b):
    eb, ea = _head_selectors()
    ba = (proj, BA_PAD, COL_BA // BA_PAD)
    qkv = dn_conv_fwd(tag, proj, cw)
    betab, gb = rowmap("dn_gate_" + tag, _gate_tile, [ba], [eb, ea, alog_b, dtb_b], [DN_KEY_DIM] * 2, TM_ROW)
    o, states, tinvs = delta_fwd(tag, qkv, gb, betab)
    z = (proj, DN_KEY_DIM, COL_Z // DN_KEY_DIM)
    a_out = rowmap("dn_out_" + tag, _dn_out_tile, [o, z], [ng_b], [DN_KEY_DIM], TM_ROW)[0]
    return a_out, (qkv, betab, gb, o, states, tinvs)


def dn_mixer_bwd(tag, proj, cw, alog_b, dtb_b, ng_b, res, da_out):
    qkv, betab, gb, o, states, tinvs = res
    eb, ea = _head_selectors()
    ba = (proj, BA_PAD, COL_BA // BA_PAD)
    z = (proj, DN_KEY_DIM, COL_Z // DN_KEY_DIM)
    (do, dz), (dng,) = rowmap_bwd("dn_out_bwd_" + tag, _dn_out_tile, [o, z], [ng_b], [da_out], TM_ROW)
    dqkv, dgb, dbetab = delta_bwd(tag, qkv, gb, betab, states, tinvs, do)
    dqkv_raw, dcw = dn_conv_bwd(tag, proj, cw, dqkv)
    (dba,), (dalog, ddtb) = rowmap_bwd("dn_gate_bwd_" + tag, _gate_tile, [ba], [eb, ea, alog_b, dtb_b],
                                       [dbetab, dgb], TM_ROW, par_mask=[False, False, True, True])
    return dqkv_raw, dz, dba, dcw[:DN_CONV], dalog, ddtb, dng


def _swap_halves(x):
    n = x.shape[1]
    first = (_iota2((1, n), 1) % SW_HEAD_DIM) < SW_HEAD_DIM // 2
    return jnp.where(first, pltpu.roll(x, n - SW_HEAD_DIM // 2, 1), pltpu.roll(x, SW_HEAD_DIM // 2, 1))


def _rope_apply(x, cos, sin_signed):
    return x * cos + _swap_halves(x) * sin_signed


def _rope_transpose(dy, cos, sin_signed):
    return dy * cos + _swap_halves(dy * sin_signed)


def rope_tables(positions, s):
    half = SW_HEAD_DIM // 2
    inv_freq = ROPE_THETA ** (-jnp.arange(0, SW_HEAD_DIM, 2, dtype=f32) / SW_HEAD_DIM)
    ang = positions.reshape(s, 1).astype(f32) * inv_freq[None, :]
    cos, sin = jnp.cos(ang), jnp.sin(ang)
    cos_t = jnp.tile(jnp.concatenate([cos, cos], 1), (1, SW_HEADS))
    sin_t = jnp.tile(jnp.concatenate([-sin, sin], 1), (1, SW_HEADS))
    assert cos_t.shape == (s, SW_DIM) and half * 2 == SW_HEAD_DIM
    return cos_t, sin_t


def rope_fwd(tag, proj, cos, sin):
    def fn(q, k, v, c, sg):
        return _rope_apply(q, c, sg), _rope_apply(k, c, sg), v

    rows = [(proj, SW_DIM, COL_SWQ // SW_DIM), (proj, SW_DIM, COL_SWK // SW_DIM), (proj, SW_DIM, COL_SWV // SW_DIM), cos, sin]
    return rowmap("rope_" + tag, fn, rows, [], [SW_DIM] * 3, TM_ROW, out_dtypes=[bf16] * 3)


def _swa_block(q, kp, kc, vp, vc, first):
    blk = SW_BLOCK
    kk = jnp.concatenate([kp, kc], 0)
    vv = jnp.concatenate([vp, vc], 0)
    dist = (_iota2((blk, 2 * blk), 0) + blk) - _iota2((blk, 2 * blk), 1)
    kj = _iota2((blk, 2 * blk), 1)
    valid = (dist >= 0) & (dist <= blk) & ((kj >= blk) | jnp.logical_not(first))
    lane_head = _iota2((1, LANES), 1) // SW_HEAD_DIM
    outs, lses = [], []
    for p in range(SW_DIM // LANES):
        sl = slice(p * LANES, (p + 1) * LANES)
        qp, kp_, vp_ = q[:, sl], kk[:, sl], vv[:, sl]
        o_pair = jnp.zeros((blk, LANES), f32)
        l_pair = jnp.zeros((blk, LANES), f32)
        for e in range(LANES // SW_HEAD_DIM):
            msk = lane_head == e
            sc = bdot(jnp.where(msk, qp, 0.0), kp_, 1, 1) * (SW_HEAD_DIM ** -0.5)
            sc = jnp.where(valid, sc, -1e30)
            m = lax.stop_gradient(jnp.max(sc, -1, keepdims=True))
            pe = jnp.exp(sc - m)
            l = jnp.sum(pe, -1, keepdims=True)
            o = bdot(pe, vp_, 1, 0) / l
            o_pair = o_pair + jnp.where(msk, o, 0.0)
            l_pair = l_pair + jnp.where(msk, m + jnp.log(l), 0.0)
        outs.append(o_pair)
        lses.append(l_pair)
    return jnp.concatenate(outs, -1), jnp.concatenate(lses, -1)


def _swa_specs(r):
    cur = pl.BlockSpec((SW_BLOCK, SW_DIM), lambda rho, n: (n, rho))
    prev = pl.BlockSpec((SW_BLOCK, SW_DIM), lambda rho, n: (jnp.maximum(n - 1, 0), rho))
    return cur, prev


def swa_fwd(tag, r, q, k, v):
    s = q.shape[0]
    ln = s // r
    q2, k2, v2 = (t.reshape(ln, r * SW_DIM) for t in (q, k, v))
    cur, prev = _swa_specs(r)

    def body(q_ref, kp_ref, kc_ref, vp_ref, vc_ref, o_ref, l_ref):
        ins = [r[...].astype(f32) for r in (q_ref, kp_ref, kc_ref, vp_ref, vc_ref)]
        o, l = _swa_block(*ins, pl.program_id(1) == 0)
        o_ref[...] = o
        l_ref[...] = l

    o, l = pl.pallas_call(
        body, grid=(r, ln // SW_BLOCK),
        in_specs=[cur, prev, cur, prev, cur], out_specs=[cur, cur],
        out_shape=[jax.ShapeDtypeStruct((ln, r * SW_DIM), f32)] * 2,
        compiler_params=_cparams("parallel", "parallel"), name=f"swa{r}_{tag}")(q2, k2, k2, v2, v2)
    return o.reshape(s, SW_DIM), l.reshape(s, SW_DIM)


def swa_bwd(tag, r, q, k, v, do, dl):
    s = q.shape[0]
    ln = s // r
    q2, k2, v2, do2, dl2 = (t.reshape(ln, r * SW_DIM) for t in (q, k, v, do, dl))
    cur, prev = _swa_specs(r)

    def body(q_ref, kp_ref, kc_ref, vp_ref, vc_ref, do_ref, dl_ref, dq_ref, dka_ref, dkb_ref, dva_ref, dvb_ref):
        first = pl.program_id(1) == 0
        ins = [r[...].astype(f32) for r in (q_ref, kp_ref, kc_ref, vp_ref, vc_ref)]
        _, vjp = jax.vjp(lambda *a: _swa_block(*a, first), *ins)
        dq_ref[...], dka_ref[...], dkb_ref[...], dva_ref[...], dvb_ref[...] = vjp((do_ref[...], dl_ref[...]))

    outs = pl.pallas_call(
        body, grid=(r, ln // SW_BLOCK),
        in_specs=[cur, prev, cur, prev, cur, cur, cur], out_specs=[cur] * 5,
        out_shape=[jax.ShapeDtypeStruct((ln, r * SW_DIM), f32)] * 5,
        compiler_params=_cparams("parallel", "parallel"), name=f"swa{r}_bwd_{tag}")(q2, k2, k2, v2, v2, do2, dl2)
    return [t.reshape(s, SW_DIM) for t in outs]


def _combine_tile(o1, l1, o2, l2, o3, l3):
    m = lax.stop_gradient(jnp.maximum(jnp.maximum(l1, l2), l3))
    e1, e2, e3 = jnp.exp(l1 - m), jnp.exp(l2 - m), jnp.exp(l3 - m)
    return ((o1 * e1 + o2 * e2 + o3 * e3) / (e1 + e2 + e3),)


def swa_merge_bwd(tag, grads, cos, sin):
    s = cos.shape[0]
    tm = SW_BLOCK
    nt = s // tm
    here = pl.BlockSpec((tm, SW_DIM), lambda i: (i, 0))
    arrs, specs = [], []
    for r, g in zip(SW_DILATIONS, grads):
        ahead = pl.BlockSpec((tm, SW_DIM), lambda i, _r=r: (jnp.minimum(i + _r, nt - 1), 0))
        arrs += g
        specs += [here, ahead, here, ahead, here]

    def body(*refs):
        i = pl.program_id(0)
        c_ref, s_ref = refs[15], refs[16]
        dq_ref, dk_ref, dv_ref = refs[17:]
        dq = jnp.zeros((tm, SW_DIM), f32)
        dk = jnp.zeros((tm, SW_DIM), f32)
        dv = jnp.zeros((tm, SW_DIM), f32)
        for b, r in enumerate(SW_DILATIONS):
            gq, gka, gkb, gva, gvb = refs[5 * b:5 * b + 5]
            inside = i + r < nt
            dq = dq + gq[...]
            dk = dk + gkb[...] + jnp.where(inside, gka[...], 0.0)
            dv = dv + gvb[...] + jnp.where(inside, gva[...], 0.0)
        dq_ref[...] = _rope_transpose(dq, c_ref[...], s_ref[...])
        dk_ref[...] = _rope_transpose(dk, c_ref[...], s_ref[...])
        dv_ref[...] = dv

    return pl.pallas_call(
        body, grid=(nt,), in_specs=specs + [here, here], out_specs=[here] * 3,
        out_shape=[jax.ShapeDtypeStruct((s, SW_DIM), f32)] * 3,
        compiler_params=_cparams("parallel"), name="swa_merge_bwd_" + tag)(*arrs, cos, sin)


def swa_mixer_fwd(tag, proj, cos, sin):
    q, k, v = rope_fwd(tag, proj, cos, sin)
    ols = []
    for r in SW_DILATIONS:
        ols += list(swa_fwd(tag, r, q, k, v))
    b_out = rowmap("swa_comb_" + tag, _combine_tile, ols, [], [SW_DIM], TM_ROW)[0]
    return b_out, (q, k, v, ols)


def swa_mixer_bwd(tag, cos, sin, res, db_out):
    q, k, v, ols = res
    dols, _ = rowmap_bwd("swa_comb_bwd_" + tag, _combine_tile, ols, [], [db_out], TM_ROW)
    grads = [swa_bwd(tag, r, q, k, v, dols[2 * b], dols[2 * b + 1]) for b, r in enumerate(SW_DILATIONS)]
    return swa_merge_bwd(tag, grads, cos, sin)


TM_S5 = 256
S5_GPB = LANES // S5_GROUP
S5_NBLK = D_MODEL // LANES
S5_HALF = S5_GPB * S5_STATE
S5_BW = 2 * S5_HALF
S5_WIDTH = S5_NBLK * S5_BW
S5_TABW = S5_NBLK * S5_HALF


def _s5_disc_tile(a_re, a_im, log_dt, b_re, b_im, expand):
    dt = jnp.exp(log_dt)
    mag = jnp.exp(a_re * dt)
    abar_re, abar_im = mag * jnp.cos(a_im * dt), mag * jnp.sin(a_im * dt)
    n_re, n_im = abar_re - 1.0, abar_im
    den = a_re * a_re + a_im * a_im
    c_re = (n_re * a_re + n_im * a_im) / den
    c_im = (n_im * a_re - n_re * a_im) / den
    cx_re, cx_im = hdot(c_re, expand), hdot(c_im, expand)
    return abar_re, abar_im, cx_re * b_re - cx_im * b_im, cx_re * b_im + cx_im * b_re


def _s5_expand():
    return (_iota2((S5_STATE, S5_STATE * S5_GROUP), 1) // S5_GROUP == _iota2((S5_STATE, S5_STATE * S5_GROUP), 0)).astype(f32)


def s5_tables(a_re, a_im, log_dt):
    lanes = lambda v: v.reshape(1, S5_TABW)
    dt = jnp.broadcast_to(log_dt.reshape(S5_GROUPS, 1), (S5_GROUPS, S5_STATE))
    t = TM_S5

    def body(are_ref, aim_ref, ldt_ref, ar_ref, ai_ref, arr_ref, air_ref):
        dtv = jnp.exp(ldt_ref[...])
        lre, lim = are_ref[...] * dtv, aim_ref[...] * dtv
        row = _iota2((t, S5_HALF), 0)
        for asc, o_re, o_im in ((True, ar_ref, ai_ref), (False, arr_ref, air_ref)):
            n = (row + 1 if asc else t - row).astype(f32)
            mag = jnp.exp(n * lre)
            o_re[...] = mag * jnp.cos(n * lim)
            o_im[...] = mag * jnp.sin(n * lim)

    lane = pl.BlockSpec((1, S5_HALF), lambda j: (0, j))
    tab = pl.BlockSpec((t, S5_HALF), lambda j: (0, j))
    return pl.pallas_call(
        body, grid=(S5_NBLK,), in_specs=[lane] * 3, out_specs=[tab] * 4,
        out_shape=[jax.ShapeDtypeStruct((t, S5_TABW), f32)] * 4,
        compiler_params=_cparams("parallel"), name="s5_tables")(lanes(a_re), lanes(a_im), lanes(dt))


def s5_pack_weights(bbar_re, bbar_im, c_re, c_im):
    eye = jnp.eye(S5_GPB, dtype=f32)
    bb = jnp.stack([bbar_re.reshape(S5_GROUPS, S5_STATE, S5_GROUP), bbar_im.reshape(S5_GROUPS, S5_STATE, S5_GROUP)], 1)
    bb = bb.transpose(0, 3, 1, 2).reshape(S5_NBLK, S5_GPB, S5_GROUP, 2, S5_STATE)
    wb = (bb[:, :, :, :, None, :] * eye[None, :, None, None, :, None]).reshape(S5_NBLK, LANES, S5_BW)
    cc = jnp.stack([c_re, -c_im], 1)
    cc = cc.reshape(S5_NBLK, S5_GPB, 2, S5_GROUP, S5_STATE).transpose(0, 2, 1, 4, 3)
    wc = (cc[:, :, :, :, None, :] * eye[None, None, :, None, :, None]).reshape(S5_NBLK, S5_BW, LANES)
    return wb, wc


def s5_unpack_weight_grads(dwb, dwc):
    d6 = dwb.reshape(S5_NBLK, S5_GPB, S5_GROUP, 2, S5_GPB, S5_STATE)
    dbb = jnp.stack([d6[:, gl, :, :, gl, :] for gl in range(S5_GPB)])
    dbb = dbb.transpose(1, 0, 3, 4, 2).reshape(S5_GROUPS, 2, S5_STATE * S5_GROUP)
    c6 = dwc.reshape(S5_NBLK, 2, S5_GPB, S5_STATE, S5_GPB, S5_GROUP)
    dcc = jnp.stack([c6[:, :, gl, :, gl, :] for gl in range(S5_GPB)])
    dcc = dcc.transpose(1, 0, 2, 4, 3).reshape(S5_GROUPS, 2, S5_GROUP, S5_STATE)
    return dbb[:, 0], dbb[:, 1], dcc[:, 0], -dcc[:, 1]


def _s5_step_rows(t):
    d, out = 1, []
    while d < t:
        out.append(d)
        d *= 2
    return out


def s5_core_fwd(tag, u, wb, wc, a1, a2, dskip):
    s = u.shape[0]
    t = TM_S5

    def body(u_ref, wb_ref, wc_ref, ar_ref, ai_ref, d_ref, y_ref, x_ref, carry):
        @pl.when(pl.program_id(1) == 0)
        def _():
            carry[...] = jnp.zeros_like(carry)

        uv = u_ref[...]
        bu = bdot(uv, wb_ref[...], 1, 0)
        row = _iota2((t, LANES), 0)
        for c in range(S5_HALF // LANES):
            re, im = slice(c * LANES, (c + 1) * LANES), slice(S5_HALF + c * LANES, S5_HALF + (c + 1) * LANES)
            xr, xi = bu[:, re], bu[:, im]
            for d in _s5_step_rows(t):
                ar, ai = ar_ref[d - 1:d, re], ai_ref[d - 1:d, re]
                if d % SUBLANES:
                    keep = row >= d
                    sr = jnp.where(keep, pltpu.roll(xr, d, 0), 0.0)
                    si = jnp.where(keep, pltpu.roll(xi, d, 0), 0.0)
                    xr, xi = xr + ar * sr - ai * si, xi + ar * si + ai * sr
                else:
                    sr, si = xr[:t - d], xi[:t - d]
                    xr = jnp.concatenate([xr[:d], xr[d:] + (ar * sr - ai * si)], 0)
                    xi = jnp.concatenate([xi[:d], xi[d:] + (ar * si + ai * sr)], 0)
            cr, ci = carry[:, re], carry[:, im]
            ar, ai = ar_ref[:, re], ai_ref[:, re]
            x_ref[:, re] = xr + ar * cr - ai * ci
            x_ref[:, im] = xi + ar * ci + ai * cr
        carry[...] = x_ref[t - 1:t, :]
        y_ref[...] = bdot(x_ref[...], wc_ref[...], 1, 0) + d_ref[...] * uv

    tab = pl.BlockSpec((t, S5_HALF), lambda j, i: (0, j))
    return pl.pallas_call(
        body, grid=(S5_NBLK, s // t),
        in_specs=[pl.BlockSpec((t, LANES), lambda j, i: (i, j)),
                  pl.BlockSpec((None, LANES, S5_BW), lambda j, i: (j, 0, 0)),
                  pl.BlockSpec((None, S5_BW, LANES), lambda j, i: (j, 0, 0)),
                  tab, tab, pl.BlockSpec((1, LANES), lambda j, i: (0, j))],
        out_specs=[pl.BlockSpec((t, LANES), lambda j, i: (i, j)), pl.BlockSpec((t, S5_BW), lambda j, i: (i, j))],
        out_shape=[jax.ShapeDtypeStruct((s, D_MODEL), f32), jax.ShapeDtypeStruct((s, S5_WIDTH), f32)],
        scratch_shapes=[pltpu.VMEM((1, S5_BW), f32)],
        compiler_params=_cparams("parallel", "arbitrary"), name="s5_core_" + tag)(u, wb, wc, a1, a2, dskip)


def s5_core_bwd(tag, u, x, wb, wc, a1, a2, a1r, a2r, dskip, dy):
    s = u.shape[0]
    t = TM_S5
    nt = s // t
    hb = t // SUBLANES

    def body(u_ref, dy_ref, x_ref, xh_ref, wb_ref, wc_ref, ar_ref, ai_ref, arr_ref, air_ref, d_ref,
             du_ref, dwb_ref, dwc_ref, dd_ref, q1_ref, q2_ref, carry, lam_scr):
        i = pl.program_id(1)
        tt = nt - 1 - i

        @pl.when(i == 0)
        def _():
            carry[...] = jnp.zeros_like(carry)

        uv, dyv, xv = u_ref[...], dy_ref[...], x_ref[...]
        lam = bdot(dyv, wc_ref[...], 1, 1)
        row = _iota2((t, LANES), 0)
        x_last = jnp.where(tt > 0, xh_ref[SUBLANES - 1:SUBLANES, :], 0.0)
        q1s, q2s = [], []
        for c in range(S5_HALF // LANES):
            re, im = slice(c * LANES, (c + 1) * LANES), slice(S5_HALF + c * LANES, S5_HALF + (c + 1) * LANES)
            lr, li = lam[:, re], lam[:, im]
            for d in _s5_step_rows(t):
                ar, ai = ar_ref[d - 1:d, re], ai_ref[d - 1:d, re]
                if d % SUBLANES:
                    keep = row < t - d
                    sr = jnp.where(keep, pltpu.roll(lr, t - d, 0), 0.0)
                    si = jnp.where(keep, pltpu.roll(li, t - d, 0), 0.0)
                    lr, li = lr + ar * sr + ai * si, li + ar * si - ai * sr
                else:
                    sr, si = lr[d:], li[d:]
                    lr = jnp.concatenate([lr[:t - d] + (ar * sr + ai * si), lr[t - d:]], 0)
                    li = jnp.concatenate([li[:t - d] + (ar * si - ai * sr), li[t - d:]], 0)
            cr, ci = carry[:, re], carry[:, im]
            ar, ai = arr_ref[:, re], air_ref[:, re]
            lr, li = lr + ar * cr + ai * ci, li + ar * ci - ai * cr
            lam_scr[:, re] = lr
            lam_scr[:, im] = li
            pr = jnp.where(row == 0, x_last[:, re], pltpu.roll(xv[:, re], 1, 0))
            pi = jnp.where(row == 0, x_last[:, im], pltpu.roll(xv[:, im], 1, 0))
            p1, p2 = lr * pr + li * pi, li * pr - lr * pi
            q1, q2 = p1[:SUBLANES, :], p2[:SUBLANES, :]
            for k in range(1, hb):
                q1 = q1 + p1[k * SUBLANES:(k + 1) * SUBLANES, :]
                q2 = q2 + p2[k * SUBLANES:(k + 1) * SUBLANES, :]
            q1s.append(q1)
            q2s.append(q2)
        carry[...] = lam_scr[0:1, :]
        lam = lam_scr[...]
        du_ref[...] = bdot(lam, wb_ref[...], 1, 1) + d_ref[...] * dyv
        upd = [(dwb_ref, bdot(uv, lam, 0, 0)), (dwc_ref, bdot(xv, dyv, 0, 0)),
               (dd_ref, jnp.sum(dyv * uv, 0, keepdims=True)),
               (q1_ref, jnp.concatenate(q1s, 1)), (q2_ref, jnp.concatenate(q2s, 1))]

        @pl.when(i == 0)
        def _():
            for ref, val in upd:
                ref[...] = val

        @pl.when(i != 0)
        def _():
            for ref, val in upd:
                ref[...] += val

    nb8 = s // SUBLANES
    rev = lambda w: pl.BlockSpec((t, w), lambda j, i: (nt - 1 - i, j))
    tab = pl.BlockSpec((t, S5_HALF), lambda j, i: (0, j))
    return pl.pallas_call(
        body, grid=(S5_NBLK, nt),
        in_specs=[rev(LANES), rev(LANES), rev(S5_BW),
                  pl.BlockSpec((SUBLANES, S5_BW), lambda j, i: (jnp.maximum((nt - 1 - i) * hb - 1, 0), j)),
                  pl.BlockSpec((None, LANES, S5_BW), lambda j, i: (j, 0, 0)),
                  pl.BlockSpec((None, S5_BW, LANES), lambda j, i: (j, 0, 0)),
                  tab, tab, tab, tab, pl.BlockSpec((1, LANES), lambda j, i: (0, j))],
        out_specs=[rev(LANES),
                   pl.BlockSpec((None, LANES, S5_BW), lambda j, i: (j, 0, 0)),
                   pl.BlockSpec((None, S5_BW, LANES), lambda j, i: (j, 0, 0)),
                   pl.BlockSpec((1, LANES), lambda j, i: (0, j)),
                   pl.BlockSpec((SUBLANES, S5_HALF), lambda j, i: (0, j)),
                   pl.BlockSpec((SUBLANES, S5_HALF), lambda j, i: (0, j))],
        out_shape=[jax.ShapeDtypeStruct((s, D_MODEL), f32),
                   jax.ShapeDtypeStruct((S5_NBLK, LANES, S5_BW), f32),
                   jax.ShapeDtypeStruct((S5_NBLK, S5_BW, LANES), f32),
                   jax.ShapeDtypeStruct((1, D_MODEL), f32),
                   jax.ShapeDtypeStruct((SUBLANES, S5_TABW), f32),
                   jax.ShapeDtypeStruct((SUBLANES, S5_TABW), f32)],
        scratch_shapes=[pltpu.VMEM((1, S5_BW), f32), pltpu.VMEM((t, S5_BW), f32)],
        compiler_params=_cparams("parallel", "arbitrary"),
        name="s5_core_bwd_" + tag)(u, dy, x, x, wb, wc, a1, a2, a1r, a2r, dskip)


def _gelu_tile(y):
    return (jax.nn.gelu(y),)


def s5_mixer_fwd(tag, u, prm, w_og):
    a_re, a_im, log_dt, b_re, b_im, c_re, c_im, dskip = prm
    disc_in = [a_re, a_im, log_dt.reshape(S5_GROUPS, 1), b_re.reshape(S5_GROUPS, -1), b_im.reshape(S5_GROUPS, -1)]
    abar_re, abar_im, bbar_re, bbar_im = rowmap("s5_disc_" + tag, _s5_disc_tile, disc_in, [_s5_expand()],
                                                [S5_STATE, S5_STATE, S5_STATE * S5_GROUP, S5_STATE * S5_GROUP], S5_GROUPS)
    del abar_re, abar_im
    a1, a2, a1r, a2r = s5_tables(a_re, a_im, log_dt)
    wb, wc = s5_pack_weights(bbar_re, bbar_im, c_re, c_im)
    wb, wc = wb.astype(bf16), wc.astype(bf16)
    y, x = s5_core_fwd(tag, u, wb, wc, a1, a2, dskip.reshape(1, D_MODEL))
    hid = rowmap("s5_gelu_" + tag, _gelu_tile, [y], [], [D_MODEL], TM_ROW, out_dtypes=[bf16])[0]
    og = mm_nn("s5_og_" + tag, hid, w_og)
    mix = rowmap("s5_glu_" + tag, _glu_tile, [og], [], [D_MODEL], TM_ROW)[0]
    return mix, (disc_in, a1, a2, a1r, a2r, wb, wc, x, y, hid, og)


def s5_mixer_bwd(tag, idx, u, prm, w_og, res, dmix, stacks):
    a_re, a_im, log_dt, b_re, b_im, c_re, c_im, dskip = prm
    disc_in, a1, a2, a1r, a2r, wb, wc, x, y, hid, og = res
    (dog,), _ = rowmap_bwd("s5_glu_bwd_" + tag, _glu_tile, [og], [], [dmix], TM_ROW)
    n_odd = DEPTH // 2
    dw_og = (mm_tn("s5_wo_dw_" + tag, hid, dog, n_cols=D_MODEL, stack=(stacks[0], idx, n_odd)),
             mm_tn("s5_wg_dw_" + tag, hid, dog, b_col0=D_MODEL, n_cols=D_MODEL, stack=(stacks[1], idx, n_odd)))
    dhid = mm_nt("s5_og_dx_" + tag, dog, w_og)
    (dy,), _ = rowmap_bwd("s5_gelu_bwd_" + tag, _gelu_tile, [y], [], [dhid], TM_ROW)
    du, dwb, dwc, ddskip, q1, q2 = s5_core_bwd(tag, u, x, wb, wc, a1, a2, a1r, a2r, dskip.reshape(1, D_MODEL), dy)
    dbbar_re, dbbar_im, dc_re, dc_im = s5_unpack_weight_grads(dwb, dwc)
    dabar_re = q1.sum(0).reshape(S5_GROUPS, S5_STATE)
    dabar_im = q2.sum(0).reshape(S5_GROUPS, S5_STATE)
    grads, _ = rowmap_bwd("s5_disc_bwd_" + tag, _s5_disc_tile, disc_in, [_s5_expand()],
                          [dabar_re, dabar_im, dbbar_re, dbbar_im], S5_GROUPS, par_mask=[False])
    da_re, da_im, dlog_dt, db_re, db_im = grads
    return du, (da_re, da_im, dlog_dt.reshape(S5_GROUPS), db_re.reshape(b_re.shape), db_im.reshape(b_im.shape),
                dc_re, dc_im, ddskip.reshape(D_MODEL)), dw_og


HYB_IN = 3592
_IN_B0, _IN_SW0 = 2048, 2056


IN_SHARD = HYB_IN // 4
SHARD_ORDER_GRADS = ("hyb_w_in", "ffn_wg", "ffn_wu", "ffn_wd")
FFN_TRANSPOSED = ("ffn_wg", "ffn_wu")
BIG_SHARDED = ("hyb_w_in", "hyb_w_out", "s5_glu_wo", "s5_glu_wg", "xq_w", "xk_w", "xv_w", "xo_w", "ffn_wg", "ffn_wu", "ffn_wd")


def _w_in_pieces():
    runs = [(0, _IN_B0, 0), (_IN_B0, _IN_SW0, COL_BA), (_IN_SW0, HYB_IN, _IN_B0)]
    out = []
    for sh in range(4):
        lo, hi = sh * IN_SHARD, (sh + 1) * IN_SHARD
        for r_lo, r_hi, c_lo in runs:
            a, b = max(lo, r_lo), min(hi, r_hi)
            if a < b:
                out.append((sh, a - lo, b - lo, c_lo + a - r_lo))
    return out


def w_in_to_canonical(tag, layer, w4):
    tr = 128

    def body(w_ref, o_ref):
        o_ref[:, COL_BA:] = jnp.zeros((tr, BA_PAD), o_ref.dtype)
        for sh, a, b, c in _w_in_pieces():
            o_ref[:, c:c + b - a] = w_ref[sh, :, a:b]

    return pl.pallas_call(
        body, grid=(D_MODEL // tr,),
        in_specs=[pl.BlockSpec((4, None, tr, IN_SHARD), lambda i: (0, layer, i, 0))],
        out_specs=pl.BlockSpec((tr, PROJ_COLS), lambda i: (i, 0)),
        out_shape=jax.ShapeDtypeStruct((D_MODEL, PROJ_COLS), w4.dtype),
        compiler_params=_cparams("parallel"), name="w_in_canon_" + tag)(w4)


def w_in_grad_to_shards(tag, layer, g, stack, n_layers):
    tr = 128
    extra, extra_specs, _ = _stacked(stack)

    def body(g_ref, *rest):
        o_ref = rest[-1]
        for sh, a, b, c in _w_in_pieces():
            o_ref[sh, :, a:b] = g_ref[:, c:c + b - a]

    return pl.pallas_call(
        body, grid=(D_MODEL // tr,),
        in_specs=[pl.BlockSpec((tr, PROJ_COLS), lambda i: (i, 0))] + extra_specs,
        out_specs=pl.BlockSpec((4, None, tr, IN_SHARD), lambda i: (0, layer, i, 0)),
        out_shape=jax.ShapeDtypeStruct((4, n_layers, D_MODEL, IN_SHARD), f32),
        input_output_aliases={1: 0} if extra else {},
        compiler_params=_cparams("parallel"), name="w_in_grad_shards_" + tag)(g, *extra)


def _add2(name, a, b):
    return rowmap(name, lambda p, q: (p + q,), [a, b], [], [a.shape[1]], _pick(a.shape[0], (256, 128, 64, 32, 16, 8)))[0]


def local_step(x, mem, positions, target, p):
    s = x.shape[0]
    cos, sin = rope_tables(positions, s)
    row = lambda v: v.reshape(1, -1).astype(f32)
    wg4, wu4, wd4 = (p[n].astype(bf16) for n in ("ffn_wg", "ffn_wu", "ffn_wd"))
    h = h16 = x
    tape = []
    for l in range(DEPTH):
        i, tag = l // 2, str(l)
        t = {"h0": h, "h0_16": h16}
        if l % 2 == 0:
            t["w_in"] = w_in_to_canonical(tag, i, p["hyb_w_in"].astype(bf16))
            t["w_out"] = p["hyb_w_out"][i].astype(bf16)
            t["dn_prm"] = (p["dn_conv_w"][i].astype(f32), row(jnp.repeat(p["dn_a_log"][i], DN_HEAD_DIM)),
                           row(jnp.repeat(p["dn_dt_bias"][i], DN_HEAD_DIM)), row(jnp.tile(p["dn_norm_g"][i], DN_HEADS)))
            t["proj"] = mm_nn("hyb_in_" + tag, h16, t["w_in"])
            a_out, t["dn"] = dn_mixer_fwd(tag, t["proj"], *t["dn_prm"])
            b_out, t["swa"] = swa_mixer_fwd(tag, t["proj"], cos, sin)
            t["mixed"] = jnp.concatenate([a_out, b_out], 1)
            mix = mm_nn("hyb_out_" + tag, t["mixed"], t["w_out"])
        else:
            t["s5_prm"] = tuple(p[n][i].astype(f32) for n in
                                ("s5_a_re", "s5_a_im", "s5_log_dt", "s5_b_re", "s5_b_im", "s5_c_re", "s5_c_im", "s5_d"))
            t["w_og"] = jnp.concatenate([p["s5_glu_wo"][i], p["s5_glu_wg"][i]], 1).astype(bf16)
            mix, t["s5"] = s5_mixer_fwd(tag, h, t["s5_prm"], t["w_og"])
        t["mix"] = mix
        t["ln"] = [(row(p[g][l]), row(p[b][l])) for g, b in
                   (("ln_mix_g", "ln_mix_b"), ("ln_x_g", "ln_x_b"), ("ln_ffn_g", "ln_ffn_b"))]
        t["h1"], t["h1_16"] = postnorm_fwd("mix" + tag, h, mix, *t["ln"][0])
        t["wq"], t["wo"] = p["xq_w"][l].astype(bf16), p["xo_w"][l].astype(bf16)
        t["wkv"] = jnp.concatenate([p["xk_w"][l], p["xv_w"][l]], 1).astype(bf16)
        t["xo"], t["xres"] = xattn_fwd(tag, t["h1_16"], mem, t["wq"], t["wkv"], t["wo"])
        t["h2"], t["h2_16"] = postnorm_fwd("x" + tag, t["h1"], t["xo"], *t["ln"][1])
        t["fo"], t["fres"] = ffn_fwd(tag, l, t["h2_16"], wg4, wu4, wd4)
        h, h16 = postnorm_fwd("ffn" + tag, t["h2"], t["fo"], *t["ln"][2])
        tape.append(t)

    part, dh = loss_head(h, target)
    loss = jnp.sum(part)

    g = {n: [None] * v.shape[0] for n, v in p.items() if n not in BIG_SHARDED}
    st = {n: None for n in BIG_SHARDED}
    for l in reversed(range(DEPTH)):
        i, tag, t = l // 2, str(l), tape[l]
        dh2a, dfo, dg, db = postnorm_bwd("ffn" + tag, t["h2"], t["fo"], *t["ln"][2], dh)
        g["ln_ffn_g"][l], g["ln_ffn_b"][l] = dg[0], db[0]
        ffn_names = ("ffn_wg", "ffn_wu", "ffn_wd")
        prev = None if st["ffn_wd"] is None else [st[n] for n in ffn_names]
        dh2b, new = ffn_bwd(tag, l, t["h2_16"], wg4, wu4, wd4, t["fres"], dfo, prev)
        st.update(zip(ffn_names, new))
        dh1a, dxo, dg, db = postnorm_bwd("x" + tag, t["h1"], t["xo"], *t["ln"][1], [dh2a, dh2b])
        g["ln_x_g"][l], g["ln_x_b"][l] = dg[0], db[0]
        x_names = ("xq_w", "xk_w", "xv_w", "xo_w")
        dh1b, new = xattn_bwd(tag, l, t["h1_16"], mem, t["wq"], t["wkv"], t["wo"], t["xres"], dxo, [st[n] for n in x_names])
        st.update(zip(x_names, new))
        dh0a, dmix, dg, db = postnorm_bwd("mix" + tag, t["h0"], t["mix"], *t["ln"][0], [dh1a, dh1b])
        g["ln_mix_g"][l], g["ln_mix_b"][l] = dg[0], db[0]
        if l % 2 == 0:
            st["hyb_w_out"] = mm_tn("hyb_out_dw_" + tag, t["mixed"], dmix, stack=(st["hyb_w_out"], i, DEPTH // 2))
            dmixed = mm_nt("hyb_out_dx_" + tag, dmix, t["w_out"])
            dqkv, dz, dba, dcw, dalog, ddtb, dng = dn_mixer_bwd(tag, t["proj"], *t["dn_prm"], t["dn"], (dmixed, DN_KEY_DIM, 0))
            g["dn_conv_w"][i] = dcw
            g["dn_a_log"][i] = dalog.reshape(DN_HEADS, DN_HEAD_DIM).sum(1)
            g["dn_dt_bias"][i] = ddtb.reshape(DN_HEADS, DN_HEAD_DIM).sum(1)
            g["dn_norm_g"][i] = dng.reshape(DN_HEADS, DN_HEAD_DIM).sum(0)
            dq, dk, dv = swa_mixer_bwd(tag, cos, sin, t["swa"], (dmixed, SW_DIM, 1))
            dproj = jnp.concatenate([dqkv, dz, dq, dk, dv, dba], 1)
            st["hyb_w_in"] = w_in_grad_to_shards(tag, i, mm_tn("hyb_in_dw_" + tag, t["h0_16"], dproj), st["hyb_w_in"], DEPTH // 2)
            dh0b = mm_nt("hyb_in_dx_" + tag, dproj, t["w_in"])
        else:
            dh0b, dprm, (st["s5_glu_wo"], st["s5_glu_wg"]) = s5_mixer_bwd(
                tag, i, t["h0"], t["s5_prm"], t["w_og"], t["s5"], dmix, (st["s5_glu_wo"], st["s5_glu_wg"]))
            for n, v in zip(("s5_a_re", "s5_a_im", "s5_log_dt", "s5_b_re", "s5_b_im", "s5_c_re", "s5_c_im", "s5_d"), dprm):
                g[n][i] = v
        dh = [dh0a, dh0b]
    grad_x = _add2("grad_x", dh[0], dh[1])
    grads = {n: jnp.stack(v) for n, v in g.items()}
    grads.update(st)
    return loss, grad_x, grads


WEIGHT_NAMES = ("hyb_w_in", "dn_conv_w", "dn_a_log", "dn_dt_bias", "dn_norm_g", "hyb_w_out", "s5_a_re", "s5_a_im",
                "s5_log_dt", "s5_b_re", "s5_b_im", "s5_c_re", "s5_c_im", "s5_d", "s5_glu_wo", "s5_glu_wg",
                "ln_mix_g", "ln_mix_b", "xq_w", "xk_w", "xv_w", "xo_w", "ln_x_g", "ln_x_b",
                "ffn_wg", "ffn_wu", "ffn_wd", "ln_ffn_g", "ln_ffn_b")
SHARD_AXIS = {"hyb_w_in": 2, "dn_conv_w": 2, "hyb_w_out": 1, "s5_d": 1, "s5_glu_wo": 1, "s5_glu_wg": 1,
              "xq_w": 1, "xk_w": 1, "xv_w": 1, "xo_w": 1, "ffn_wg": 2, "ffn_wu": 2, "ffn_wd": 1}
GATHER_F32 = ("dn_conv_w", "s5_d")
N_CHIPS = 4
PACK_COLS = 1024
_ANY = pl.BlockSpec(memory_space=pl.ANY)


def _pos():
    return lax.axis_index("x"), lax.axis_index("y"), lax.axis_index("c")


def _chip_peers(mx, my):
    return [(1 - mx, my), (mx, 1 - my), (1 - mx, 1 - my)]


def _rcopy(src, dst, ssem, rsem, dev):
    return pltpu.make_async_remote_copy(src_ref=src, dst_ref=dst, send_sem=ssem, recv_sem=rsem,
                                        device_id=dev, device_id_type=pl.DeviceIdType.MESH)


def comm_allgather4(name, x):
    def body(x_ref, o_ref, ssem, rsem, lsem):
        mx, my, mc = _pos()
        me = 2 * mx + my
        peers = _chip_peers(mx, my)
        loc = pltpu.make_async_copy(x_ref, o_ref.at[me], lsem)
        loc.start()
        sends = [_rcopy(x_ref, o_ref.at[me], ssem.at[k], rsem.at[k], (px, py, mc)) for k, (px, py) in enumerate(peers)]
        for cp in sends:
            cp.start()
        for k, (px, py) in enumerate(peers):
            _rcopy(x_ref, o_ref.at[2 * px + py], ssem.at[k], rsem.at[k], (px, py, mc)).wait_recv()
        for cp in sends:
            cp.wait_send()
        loc.wait()

    return pl.pallas_call(
        body, out_shape=jax.ShapeDtypeStruct((N_CHIPS,) + x.shape, x.dtype), in_specs=[_ANY], out_specs=_ANY,
        scratch_shapes=[pltpu.SemaphoreType.DMA((3,)), pltpu.SemaphoreType.DMA((3,)), pltpu.SemaphoreType.DMA],
        name=name)(x)


def _multi_call(name, body, ins, out_shapes, sems, in_place=False):
    return pl.pallas_call(
        body, out_shape=out_shapes, in_specs=[_ANY] * len(ins), out_specs=[_ANY] * len(out_shapes),
        scratch_shapes=sems, input_output_aliases={w: w for w in range(len(ins))} if in_place else {},
        name=name)(*ins)


def comm_gather_weights(name, slots):
    n = len(slots)

    def body(*refs):
        os_ = refs[n:2 * n]
        ssem, rsem, fssem, frsem = refs[2 * n:]
        mx, my, mc = _pos()
        me = 2 * mx + my
        peers = _chip_peers(mx, my)
        sib = (mx, my, 1 - mc)
        half = [o.shape[1] // 2 for o in os_]
        mine = [pl.ds(mc * h, h) for h in half]
        other = [pl.ds((1 - mc) * h, h) for h in half]
        sends = [_rcopy(os_[w].at[me, mine[w]], os_[w].at[me, mine[w]], ssem.at[w, k], rsem.at[w, k], (px, py, mc))
                 for w in range(n) for k, (px, py) in enumerate(peers)]
        for cp in sends:
            cp.start()
        fwds = []
        for w in range(n):
            for k, (px, py) in enumerate(peers):
                landed = os_[w].at[2 * px + py, mine[w]]
                _rcopy(landed, landed, ssem.at[w, k], rsem.at[w, k], (px, py, mc)).wait_recv()
                fw = _rcopy(landed, landed, fssem.at[w, k], frsem.at[w, k], sib)
                fw.start()
                fwds.append(fw)
        for w in range(n):
            for k, (px, py) in enumerate(peers):
                theirs = os_[w].at[2 * px + py, other[w]]
                _rcopy(theirs, theirs, fssem.at[w, k], frsem.at[w, k], sib).wait_recv()
        for cp in sends + fwds:
            cp.wait_send()

    dma = pltpu.SemaphoreType.DMA
    return _multi_call(name, body, slots, [jax.ShapeDtypeStruct(x.shape, x.dtype) for x in slots],
                       [dma((n, 3)), dma((n, 3)), dma((n, 3)), dma((n, 3))], in_place=True)


def comm_sibling_halves(name, gs):
    n = len(gs)

    def body(*refs):
        xs, os_ = refs[:n], refs[n:2 * n]
        ssem, rsem = refs[2 * n:]
        mx, my, mc = _pos()
        sib = (mx, my, 1 - mc)
        sends = []
        for w in range(n):
            h = xs[w].shape[1] // 2
            for j in range(N_CHIPS):
                sends.append(_rcopy(xs[w].at[j, pl.ds((1 - mc) * h, h)], os_[w].at[j], ssem.at[w, j], rsem.at[w, j], sib))
        for cp in sends:
            cp.start()
        for w in range(n):
            for j in range(N_CHIPS):
                _rcopy(os_[w].at[j], os_[w].at[j], ssem.at[w, j], rsem.at[w, j], sib).wait_recv()
        for cp in sends:
            cp.wait_send()

    dma = pltpu.SemaphoreType.DMA
    return _multi_call(name, body, gs,
                       [jax.ShapeDtypeStruct((N_CHIPS, g.shape[1] // 2) + g.shape[2:], g.dtype) for g in gs],
                       [dma((n, N_CHIPS)), dma((n, N_CHIPS))])


def comm_alltoall4(name, xs):
    n = len(xs)

    def body(*refs):
        xr, os_ = refs[:n], refs[n:2 * n]
        ssem, rsem = refs[2 * n:]
        mx, my, mc = _pos()
        me = 2 * mx + my
        peers = _chip_peers(mx, my)
        sends = [_rcopy(xr[w].at[2 * px + py], os_[w].at[me], ssem.at[w, k], rsem.at[w, k], (px, py, mc))
                 for w in range(n) for k, (px, py) in enumerate(peers)]
        for cp in sends:
            cp.start()
        for w in range(n):
            for k, (px, py) in enumerate(peers):
                dst = os_[w].at[2 * px + py]
                _rcopy(dst, dst, ssem.at[w, k], rsem.at[w, k], (px, py, mc)).wait_recv()
        for cp in sends:
            cp.wait_send()

    dma = pltpu.SemaphoreType.DMA
    return _multi_call(name, body, xs, [jax.ShapeDtypeStruct(x.shape, x.dtype) for x in xs], [dma((n, 3)), dma((n, 3))])


def comm_sibling_join(name, bs):
    n = len(bs)

    def body(*refs):
        os_ = refs[n:2 * n]
        ssem, rsem = refs[2 * n:]
        mx, my, mc = _pos()
        sib = (mx, my, 1 - mc)
        sends = [_rcopy(os_[w].at[mc], os_[w].at[mc], ssem.at[w], rsem.at[w], sib) for w in range(n)]
        for cp in sends:
            cp.start()
        for w in range(n):
            dst = os_[w].at[1 - mc]
            _rcopy(dst, dst, ssem.at[w], rsem.at[w], sib).wait_recv()
        for cp in sends:
            cp.wait_send()

    dma = pltpu.SemaphoreType.DMA
    return _multi_call(name, body, bs, [jax.ShapeDtypeStruct(b.shape, b.dtype) for b in bs], [dma((n,)), dma((n,))],
                       in_place=True)


def comm_sibling_swap(name, x):
    def body(x_ref, o_ref, ssem, rsem):
        mx, my, mc = _pos()
        cp = _rcopy(x_ref, o_ref, ssem, rsem, (mx, my, 1 - mc))
        cp.start()
        cp.wait_recv()
        cp.wait_send()

    return pl.pallas_call(
        body, out_shape=jax.ShapeDtypeStruct(x.shape, x.dtype), in_specs=[_ANY], out_specs=_ANY,
        scratch_shapes=[pltpu.SemaphoreType.DMA, pltpu.SemaphoreType.DMA], name=name)(x)


def _row_tile(r):
    return _pick(r, (512, 256, 128, 64, 32, 16, 8))


def add_own_half(name, g, recv, out_dtype):
    r, c = g.shape[2:]
    tr = _row_tile(r)
    mc = lax.axis_index("c").astype(jnp.int32).reshape(1)

    def body(c_ref, g_ref, r_ref, o_ref):
        o_ref[...] = (g_ref[...] + r_ref[...]).astype(o_ref.dtype)

    grid_spec = pltpu.PrefetchScalarGridSpec(
        num_scalar_prefetch=1, grid=(N_CHIPS, r // tr),
        in_specs=[pl.BlockSpec((None, None, tr, c), lambda j, i, cr: (j, cr[0], i, 0)),
                  pl.BlockSpec((None, tr, c), lambda j, i, cr: (j, i, 0))],
        out_specs=pl.BlockSpec((None, tr, c), lambda j, i, cr: (j, i, 0)))
    return pl.pallas_call(body, grid_spec=grid_spec, out_shape=jax.ShapeDtypeStruct(recv.shape, out_dtype),
                          compiler_params=_cparams("parallel", "parallel"), name=name)(mc, g, recv)


def cast_into_slot(name, w, chip, dtype):
    r, c = w.shape
    tr = _row_tile(r)

    def body(c_ref, w_ref, o_ref):
        o_ref[...] = w_ref[...].astype(o_ref.dtype)

    grid_spec = pltpu.PrefetchScalarGridSpec(
        num_scalar_prefetch=1, grid=(r // tr,),
        in_specs=[pl.BlockSpec((tr, c), lambda i, cr: (i, 0))],
        out_specs=pl.BlockSpec((None, tr, c), lambda i, cr: (cr[0], i, 0)))
    return pl.pallas_call(body, grid_spec=grid_spec, out_shape=jax.ShapeDtypeStruct((N_CHIPS, r, c), dtype),
                          compiler_params=_cparams("parallel"), name=name)(chip.astype(jnp.int32).reshape(1), w)


def sum_chips_into_half(name, own, arrived, chip, mc):
    r, c = own.shape[1:]
    tr = _row_tile(r)

    def body(s0, s1, s2, s3, s4, own_ref, a_ref, b_ref, d_ref, o_ref):
        o_ref[...] = ((own_ref[...].astype(f32) + a_ref[...].astype(f32))
                      + (b_ref[...].astype(f32) + d_ref[...].astype(f32)))

    slot = lambda k: pl.BlockSpec((None, tr, c), lambda i, *sc, _k=k: (sc[_k][0], i, 0))
    grid_spec = pltpu.PrefetchScalarGridSpec(
        num_scalar_prefetch=5, grid=(r // tr,), in_specs=[slot(0), slot(1), slot(2), slot(3)],
        out_specs=pl.BlockSpec((None, tr, c), lambda i, *sc: (sc[4][0], i, 0)))
    mx, my = lax.axis_index("x"), lax.axis_index("y")
    scal = [v.astype(jnp.int32).reshape(1) for v in
            (2 * mx + my, 2 * (1 - mx) + my, 2 * mx + (1 - my), 2 * (1 - mx) + (1 - my), mc)]
    return pl.pallas_call(body, grid_spec=grid_spec, out_shape=jax.ShapeDtypeStruct((2, r, c), f32),
                          compiler_params=_cparams("parallel"), name=name)(*scal, own, arrived, arrived, arrived)


def sum_slots(name, x):
    r, c = x.shape[1:]
    tr = _row_tile(r)

    def body(x_ref, o_ref):
        o_ref[...] = (x_ref[0].astype(f32) + x_ref[1].astype(f32)) + (x_ref[2].astype(f32) + x_ref[3].astype(f32))

    return pl.pallas_call(
        body, grid=(r // tr,), in_specs=[pl.BlockSpec((N_CHIPS, tr, c), lambda i: (0, i, 0))],
        out_specs=pl.BlockSpec((tr, c), lambda i: (i, 0)), out_shape=jax.ShapeDtypeStruct((r, c), f32),
        compiler_params=_cparams("parallel"), name=name)(x)


def adamw(name, w, g, m, v):
    r, c = w.shape
    tr = _row_tile(r)

    def body(w_ref, g_ref, m_ref, v_ref, d_ref, nm_ref, nv_ref):
        gv = g_ref[...]
        nm = ADAM_B1 * m_ref[...] + (1.0 - ADAM_B1) * gv
        nv = ADAM_B2 * v_ref[...] + (1.0 - ADAM_B2) * (gv * gv)
        m_hat = nm / (1.0 - ADAM_B1 ** ADAM_STEP)
        v_hat = nv / (1.0 - ADAM_B2 ** ADAM_STEP)
        d_ref[...] = -ADAM_LR * (m_hat / (jnp.sqrt(v_hat) + ADAM_EPS) + ADAM_WD * w_ref[...])
        nm_ref[...] = nm
        nv_ref[...] = nv

    blk = pl.BlockSpec((tr, c), lambda i: (i, 0))
    return pl.pallas_call(
        body, grid=(r // tr,), in_specs=[blk] * 4, out_specs=[blk] * 3,
        out_shape=[jax.ShapeDtypeStruct((r, c), f32)] * 3,
        compiler_params=_cparams("parallel"), name=name)(w, g, m, v)


def _pack_rows(n):
    return -(-n // PACK_COLS)


def _pack(arrs, dtype, row_multiple):
    segs = []
    for a in arrs:
        flat = a.astype(dtype).reshape(-1)
        k = _pack_rows(flat.shape[0])
        segs.append(jnp.pad(flat, (0, k * PACK_COLS - flat.shape[0])).reshape(k, PACK_COLS))
    rows = sum(s.shape[0] for s in segs)
    pad = -rows % row_multiple
    if pad:
        segs.append(jnp.zeros((pad, PACK_COLS), dtype))
    return jnp.concatenate(segs, 0)


def _unpack(packed, shapes):
    out, r = [], 0
    for shp in shapes:
        n = math.prod(shp)
        k = _pack_rows(n)
        out.append(packed[r:r + k].reshape(-1)[:n].reshape(shp))
        r += k
    return out


def _gathered_to_full(g, axis):
    t = jnp.moveaxis(g, 0, axis)
    return t.reshape(t.shape[:axis] + (t.shape[axis] * t.shape[axis + 1],) + t.shape[axis + 2:])


def _full_to_shard_major(full, axis):
    shp = full.shape
    t = full.reshape(shp[:axis] + (N_CHIPS, shp[axis] // N_CHIPS) + shp[axis + 1:])
    return jnp.moveaxis(t, axis, 0)


def kernel(x, mem, positions, hyb_w_in, dn_conv_w, dn_a_log, dn_dt_bias, dn_norm_g, hyb_w_out, s5_a_re, s5_a_im, s5_log_dt, s5_b_re, s5_b_im, s5_c_re, s5_c_im, s5_d, s5_glu_wo, s5_glu_wg, ln_mix_g, ln_mix_b, xq_w, xk_w, xv_w, xo_w, ln_x_g, ln_x_b, ffn_wg, ffn_wu, ffn_wd, ln_ffn_g, ln_ffn_b, loss_target, m_hyb_w_in, m_dn_conv_w, m_dn_a_log, m_dn_dt_bias, m_dn_norm_g, m_hyb_w_out, m_s5_a_re, m_s5_a_im, m_s5_log_dt, m_s5_b_re, m_s5_b_im, m_s5_c_re, m_s5_c_im, m_s5_d, m_s5_glu_wo, m_s5_glu_wg, m_ln_mix_g, m_ln_mix_b, m_xq_w, m_xk_w, m_xv_w, m_xo_w, m_ln_x_g, m_ln_x_b, m_ffn_wg, m_ffn_wu, m_ffn_wd, m_ln_ffn_g, m_ln_ffn_b, v_hyb_w_in, v_dn_conv_w, v_dn_a_log, v_dn_dt_bias, v_dn_norm_g, v_hyb_w_out, v_s5_a_re, v_s5_a_im, v_s5_log_dt, v_s5_b_re, v_s5_b_im, v_s5_c_re, v_s5_c_im, v_s5_d, v_s5_glu_wo, v_s5_glu_wg, v_ln_mix_g, v_ln_mix_b, v_xq_w, v_xk_w, v_xv_w, v_xo_w, v_ln_x_g, v_ln_x_b, v_ffn_wg, v_ffn_wu, v_ffn_wd, v_ln_ffn_g, v_ln_ffn_b):
    a = dict(locals())
    big = [n for n in WEIGHT_NAMES if n in SHARD_AXIS and n not in GATHER_F32]
    small = [n for n in WEIGHT_NAMES if n not in big]
    chip = 2 * lax.axis_index("x") + lax.axis_index("y")
    for n in FFN_TRANSPOSED:
        for pre in ("", "m_", "v_"):
            a[pre + n] = jnp.swapaxes(a[pre + n], 1, 2)

    mc = lax.axis_index("c")
    view2 = lambda t: t.reshape(-1, t.shape[-1])
    slots = [cast_into_slot("slot_" + n, view2(a[n]), chip, bf16).reshape((N_CHIPS,) + a[n].shape) for n in big]
    gathered = comm_gather_weights("gather_w", slots)
    tiny4 = _unpack_slots(comm_allgather4("gather_w_tiny", _pack([a[n] for n in GATHER_F32], f32, 8)),
                          [a[n].shape for n in GATHER_F32])
    p = {n: a[n] for n in small if n not in GATHER_F32}
    for n, g4 in zip(GATHER_F32, tiny4):
        p[n] = _gathered_to_full(g4, SHARD_AXIS[n])
    for n, g4 in zip(big, gathered):
        p[n] = g4 if n in SHARD_ORDER_GRADS else _gathered_to_full(g4, SHARD_AXIS[n])

    loss, grad_x, grads = local_step(x[0], mem[0], positions, loss_target[0], p)
    loss = lax.psum(loss, ("x", "y", "c"))

    g4s = [grads[n] for n in big]
    recv = comm_sibling_halves("rs_sibling_halves", g4s)
    pairs = []
    for n, g4, r4 in zip(big, g4s, recv):
        lh, cols = g4.shape[1] // 2, g4.shape[-1]
        v4 = g4.reshape(N_CHIPS, 2, -1, cols)
        pairs.append(add_own_half("rs_add_" + n, v4, r4.reshape(N_CHIPS, -1, cols), bf16).reshape((N_CHIPS, lh) + g4.shape[2:]))
    arrived = comm_alltoall4("rs_alltoall", pairs)
    slot3 = lambda t: t.reshape(N_CHIPS, -1, t.shape[-1])
    halves = [sum_chips_into_half("rs_sum_" + n, slot3(pr), slot3(ar), chip, mc) for n, pr, ar in zip(big, pairs, arrived)]
    g_big = {n: t.reshape(a[n].shape) for n, t in zip(big, comm_sibling_join("rs_sibling_join", halves))}

    rpack = _pack([grads[n] for n in small], f32, 8)
    rpair = _add2("ar_add_sibling", rpack, comm_sibling_swap("ar_sibling_swap", rpack))
    g_small = _unpack(sum_slots("ar_sum_chips", comm_allgather4("ar_allgather", rpair)), [grads[n].shape for n in small])
    g_small = {n: (lax.dynamic_index_in_dim(_full_to_shard_major(g, SHARD_AXIS[n]), chip, 0, keepdims=False)
                   if n in SHARD_AXIS else g) for n, g in zip(small, g_small)}

    outs = {}
    for n in big:
        view = lambda t: t.reshape(-1, t.shape[-1])
        d, nm, nv = adamw("adamw_" + n, view(a[n]), view(g_big[n]), view(a["m_" + n]), view(a["v_" + n]))
        outs[n] = (g_big[n],) + tuple(t.reshape(a[n].shape) for t in (d, nm, nv))
    shapes = [a[n].shape for n in small]
    packs = [_pack([a[pre + n] for n in small], f32, 8) for pre in ("", "m_", "v_")]
    upd = adamw("adamw_small", packs[0], _pack([g_small[n] for n in small], f32, 8), packs[1], packs[2])
    for k, n in enumerate(small):
        outs[n] = (g_small[n],) + tuple(_unpack(buf, shapes)[k] for buf in upd)
    for n in FFN_TRANSPOSED:
        outs[n] = tuple(jnp.swapaxes(t, 1, 2) for t in outs[n])
    res = [loss, grad_x[None]]
    for kind in range(4):
        res += [outs[n][kind] for n in WEIGHT_NAMES]
    return tuple(res)


def _unpack_slots(gathered, shapes):
    out, r = [], 0
    for shp in shapes:
        n = math.prod(shp)
        k = _pack_rows(n)
        out.append(gathered[:, r:r + k].reshape(N_CHIPS, -1)[:, :n].reshape((N_CHIPS,) + tuple(shp)))
        r += k
    return out
```

```python
import functools
import math

import jax
import jax.numpy as jnp
from jax import lax
from jax.experimental import pallas as pl
from jax.experimental.pallas import tpu as pltpu

f32 = jnp.float32
bf16 = jnp.bfloat16

D_MODEL = 1024
DEPTH = 4
DN_HEADS = 4
DN_HEAD_DIM = 128
DN_KEY_DIM = 512
DN_QKV_DIM = 1536
DN_CONV = 4
SW_HEADS = 8
SW_HEAD_DIM = 64
SW_DIM = 512
SW_DILATIONS = (1, 4, 16)
SW_BLOCK = 128
ROPE_THETA = 10000.0
S5_GROUP = 16
S5_GROUPS = 64
S5_STATE = 64
X_HEADS = 4
X_HEAD_DIM = 256
FFN_HIDDEN = 2816
ALPHA = (2 * DEPTH) ** 0.25
LN_EPS = 1e-5
RMS_EPS = 1e-6
ADAM_LR, ADAM_B1, ADAM_B2, ADAM_EPS, ADAM_WD, ADAM_STEP = 0.001, 0.9, 0.999, 1e-08, 0.01, 10

BA_PAD = 256
PROJ_COLS = DN_QKV_DIM + DN_KEY_DIM + 3 * SW_DIM + BA_PAD
COL_Z = DN_QKV_DIM
COL_SWQ = COL_Z + DN_KEY_DIM
COL_SWK = COL_SWQ + SW_DIM
COL_SWV = COL_SWK + SW_DIM
COL_BA = COL_SWV + SW_DIM

LANES = 128
SUBLANES = 8
VMEM_LIMIT = 56 * 1024 * 1024
DN_CHUNK = 128
DN_HEADS_PER_STEP = 4


def _cparams(*sem):
    return pltpu.CompilerParams(dimension_semantics=tuple(sem), vmem_limit_bytes=VMEM_LIMIT)


def _dg(x, y, cx, cy):
    return lax.dot_general(x, y, (((cx,), (cy,)), ((), ())), preferred_element_type=f32)


@functools.partial(jax.custom_vjp, nondiff_argnums=(2, 3))
def bdot(a, b, ca, cb):
    return _dg(a.astype(bf16), b.astype(bf16), ca, cb)


def _bdot_fwd(a, b, ca, cb):
    return bdot(a, b, ca, cb), (a, b)


def _bdot_bwd(ca, cb, res, g):
    a, b = res
    g16, a16, b16 = g.astype(bf16), a.astype(bf16), b.astype(bf16)
    da = _dg(g16, b16, 1, 1 - cb) if ca == 1 else _dg(b16, g16, 1 - cb, 1)
    db = _dg(a16, g16, 1 - ca, 0) if cb == 0 else _dg(g16, a16, 0, 1 - ca)
    return da.astype(a.dtype), db.astype(b.dtype)


bdot.defvjp(_bdot_fwd, _bdot_bwd)


def _split_hi_lo(a):
    hi = a.astype(bf16)
    return hi, (a - hi.astype(f32)).astype(bf16)


def _dot3(a, b, ca, cb):
    a_hi, a_lo = _split_hi_lo(a)
    b_hi, b_lo = _split_hi_lo(b)
    return _dg(a_hi, b_hi, ca, cb) + (_dg(a_hi, b_lo, ca, cb) + _dg(a_lo, b_hi, ca, cb))


def hdot(a, b):
    return jnp.dot(a, b, precision=lax.Precision.HIGHEST, preferred_element_type=f32)


def _iota2(shape, dim):
    return lax.broadcasted_iota(jnp.int32, shape, dim)


def _row_spec(r, tm):
    if isinstance(r, tuple):
        arr, width, blk = r
        return arr, pl.BlockSpec((tm, width), lambda i, _b=blk: (i, _b))
    return r, pl.BlockSpec((tm, r.shape[1]), lambda i: (i, 0))


def _par_spec(p):
    return pl.BlockSpec(p.shape, lambda i, _n=p.ndim: (0,) * _n)


def rowmap(name, fn, rows, params, out_cols, tm, out_dtypes=None):
    arrs, specs = zip(*[_row_spec(r, tm) for r in rows])
    s = arrs[0].shape[0]
    n_in = len(rows) + len(params)
    out_dtypes = out_dtypes or [f32] * len(out_cols)

    def body(*refs):
        outs = fn(*[r[...] for r in refs[:n_in]])
        for o_ref, o in zip(refs[n_in:], outs):
            o_ref[...] = o.astype(o_ref.dtype)

    return pl.pallas_call(
        body, grid=(s // tm,),
        in_specs=list(specs) + [_par_spec(p) for p in params],
        out_specs=[pl.BlockSpec((tm, c), lambda i: (i, 0)) for c in out_cols],
        out_shape=[jax.ShapeDtypeStruct((s, c), dt) for c, dt in zip(out_cols, out_dtypes)],
        compiler_params=_cparams("parallel"), name=name)(*arrs, *params)


def rowmap_bwd(name, fn, rows, params, cts, tm, row_mask=None, par_mask=None, row_dtypes=None):
    arrs, specs = zip(*[_row_spec(r, tm) for r in rows])
    s = arrs[0].shape[0]
    ct_groups = [c if isinstance(c, list) else [c] for c in cts]
    ct_arrs, ct_specs = zip(*[_row_spec(a, tm) for grp in ct_groups for a in grp])
    cts = list(ct_arrs)
    nr, npar, nct = len(rows), len(params), len(cts)
    row_mask = row_mask or [True] * nr
    par_mask = par_mask or [True] * npar
    row_idx = [k for k in range(nr) if row_mask[k]]
    par_idx = [k for k in range(npar) if par_mask[k]]
    row_w = [specs[k].block_shape[1] for k in row_idx]

    def body(*refs):
        ins = [r[...] for r in refs[:nr + npar]]
        ct_refs = list(refs[nr + npar:nr + npar + nct])
        ctv = []
        for grp in ct_groups:
            acc = ct_refs.pop(0)[...].astype(f32)
            for _ in grp[1:]:
                acc = acc + ct_refs.pop(0)[...].astype(f32)
            ctv.append(acc)
        ctv = tuple(ctv)
        outs = refs[nr + npar + nct:]
        _, vjp = jax.vjp(fn, *ins)
        grads = vjp(ctv)
        for o_ref, k in zip(outs[:len(row_idx)], row_idx):
            o_ref[...] = grads[k].astype(o_ref.dtype)
        first = pl.program_id(0) == 0
        for o_ref, k in zip(outs[len(row_idx):], par_idx):
            g = grads[nr + k].astype(f32)

            @pl.when(first)
            def _(o_ref=o_ref, g=g):
                o_ref[...] = g

            @pl.when(jnp.logical_not(first))
            def _(o_ref=o_ref, g=g):
                o_ref[...] += g

    res = pl.pallas_call(
        body, grid=(s // tm,),
        in_specs=list(specs) + [_par_spec(p) for p in params]
        + list(ct_specs),
        out_specs=[pl.BlockSpec((tm, w), lambda i: (i, 0)) for w in row_w]
        + [_par_spec(params[k]) for k in par_idx],
        out_shape=[jax.ShapeDtypeStruct((s, w), dt) for w, dt in zip(row_w, row_dtypes or [f32] * len(row_w))]
        + [jax.ShapeDtypeStruct(params[k].shape, f32) for k in par_idx],
        compiler_params=_cparams("arbitrary"), name=name)(*arrs, *params, *cts)
    return list(res[:len(row_idx)]), list(res[len(row_idx):])


def _pick(n, prefs):
    for t in prefs:
        if n % t == 0:
            return t
    return n


MM_CHUNK = 512
MM_WIDE = 2048


def mm_nn(name, a, b, out_dtype=f32):
    m, k = a.shape
    n = b.shape[1]
    tm = _pick(m, ((1024,) if n <= MM_WIDE else ()) + (512, 256, 128))
    cn = _pick(n, (MM_CHUNK, 256, 128))

    def body(a_ref, b_ref, o_ref):
        av = a_ref[...].astype(bf16)
        for c in range(n // cn):
            sl = slice(c * cn, (c + 1) * cn)
            o_ref[:, sl] = _dg(av, b_ref[:, sl].astype(bf16), 1, 0).astype(o_ref.dtype)

    return pl.pallas_call(
        body, grid=(m // tm,),
        in_specs=[pl.BlockSpec((tm, k), lambda i: (i, 0)), pl.BlockSpec((k, n), lambda i: (0, 0))],
        out_specs=pl.BlockSpec((tm, n), lambda i: (i, 0)),
        out_shape=jax.ShapeDtypeStruct((m, n), out_dtype),
        compiler_params=_cparams("parallel"), name=name)(a, b)


def mm_nt(name, a, b, out_dtype=f32):
    m, n = a.shape
    k = b.shape[0]
    tm = _pick(m, ((1024,) if n <= MM_WIDE else ()) + (512, 256, 128))
    ck = _pick(k, (MM_CHUNK, 256, 128))

    def body(a_ref, b_ref, o_ref):
        av = a_ref[...].astype(bf16)
        for c in range(k // ck):
            sl = slice(c * ck, (c + 1) * ck)
            o_ref[:, sl] = _dg(av, b_ref[sl, :].astype(bf16), 1, 1).astype(o_ref.dtype)

    return pl.pallas_call(
        body, grid=(m // tm,),
        in_specs=[pl.BlockSpec((tm, n), lambda i: (i, 0)), pl.BlockSpec((k, n), lambda i: (0, 0))],
        out_specs=pl.BlockSpec((tm, k), lambda i: (i, 0)),
        out_shape=jax.ShapeDtypeStruct((m, k), out_dtype),
        compiler_params=_cparams("parallel"), name=name)(a, b)


def _stacked(buf):
    if buf is None:
        return [], [], {}
    return [buf], [pl.BlockSpec(memory_space=pl.ANY)], None


def mm_tn(name, a, b, out_dtype=f32, b_col0=0, n_cols=None, stack=None):
    s, m = a.shape
    n = n_cols or b.shape[1]
    tn = _pick(n, (256, 128))
    cm = _pick(m, (256, 128))
    col0 = b_col0 // tn
    in_specs = [pl.BlockSpec((s, m), lambda j: (0, 0)), pl.BlockSpec((s, tn), lambda j: (0, j + col0))]

    if stack is None:
        def body(a_ref, b_ref, o_ref):
            bv = b_ref[...].astype(bf16)
            for c in range(m // cm):
                sl = slice(c * cm, (c + 1) * cm)
                o_ref[sl, :] = _dg(a_ref[:, sl].astype(bf16), bv, 0, 0).astype(o_ref.dtype)

        return pl.pallas_call(
            body, grid=(n // tn,), in_specs=in_specs, out_specs=pl.BlockSpec((m, tn), lambda j: (0, j)),
            out_shape=jax.ShapeDtypeStruct((m, n), out_dtype),
            compiler_params=_cparams("parallel"), name=name)(a, b)

    buf, layer, n_layers = stack
    assert cm * N_CHIPS == m
    extra, extra_specs, _ = _stacked(buf)

    def body_stacked(a_ref, b_ref, *rest):
        o_ref = rest[-1]
        bv = b_ref[...].astype(bf16)
        for c in range(N_CHIPS):
            o_ref[c] = _dg(a_ref[:, c * cm:(c + 1) * cm].astype(bf16), bv, 0, 0).astype(o_ref.dtype)

    return pl.pallas_call(
        body_stacked, grid=(n // tn,), in_specs=in_specs + extra_specs,
        out_specs=pl.BlockSpec((N_CHIPS, None, cm, tn), lambda j: (0, layer, 0, j)),
        out_shape=jax.ShapeDtypeStruct((N_CHIPS, n_layers, cm, n), out_dtype),
        input_output_aliases={2: 0} if extra else {},
        compiler_params=_cparams("parallel"), name=name)(a, b, *extra)


def _postnorm_tile(h, sub, g, b):
    z = ALPHA * h + sub
    mu = jnp.mean(z, -1, keepdims=True)
    zc = z - mu
    var = jnp.mean(zc * zc, -1, keepdims=True)
    return (zc * lax.rsqrt(var + LN_EPS) * g + b,)


def _glu_tile(og):
    o, g = og[:, :D_MODEL], og[:, D_MODEL:]
    return (o * jax.nn.sigmoid(g),)


def _xattn_tile(q, kv):
    outs = []
    for h in range(X_HEADS):
        sl = slice(h * X_HEAD_DIM, (h + 1) * X_HEAD_DIM)
        s = bdot(q[:, sl], kv[:, sl], 1, 1) * (X_HEAD_DIM ** -0.5)
        m = lax.stop_gradient(jnp.max(s, -1, keepdims=True))
        p = jnp.exp(s - m)
        p = p / jnp.sum(p, -1, keepdims=True)
        outs.append(bdot(p, kv[:, D_MODEL + h * X_HEAD_DIM:D_MODEL + (h + 1) * X_HEAD_DIM], 1, 0))
    return (jnp.concatenate(outs, -1),)


TM_ROW = 512


def postnorm_fwd(tag, h, sub, g, b):
    return rowmap("postnorm_" + tag, lambda *a: _postnorm_tile(*a) * 2, [h, sub], [g, b], [D_MODEL] * 2, TM_ROW,
                  out_dtypes=[f32, bf16])


def postnorm_bwd(tag, h, sub, g, b, dy):
    (dh, dsub), (dg, db) = rowmap_bwd("postnorm_bwd_" + tag, _postnorm_tile, [h, sub], [g, b], [dy], TM_ROW,
                                      row_dtypes=[f32, bf16])
    return dh, dsub, dg, db


def xattn_fwd(tag, h, mem, wq, wkv, wo):
    q = mm_nn("xq_" + tag, h, wq, out_dtype=bf16)
    kv = mm_nn("xkv_" + tag, mem, wkv)
    ao = rowmap("xattn_" + tag, _xattn_tile, [q], [kv], [D_MODEL], TM_ROW, out_dtypes=[bf16])[0]
    out = mm_nn("xo_" + tag, ao, wo)
    return out, (q, kv, ao)


def xattn_bwd(tag, layer, h, mem, wq, wkv, wo, res, dout, stacks):
    q, kv, ao = res
    sq, sk, sv, so = stacks
    so = mm_tn("xo_dw_" + tag, ao, dout, stack=(so, layer, DEPTH))
    dao = mm_nt("xo_dx_" + tag, dout, wo)
    (dq,), (dkv,) = rowmap_bwd("xattn_bwd_" + tag, _xattn_tile, [q], [kv], [dao], TM_ROW, row_dtypes=[bf16])
    sq = mm_tn("xq_dw_" + tag, h, dq, stack=(sq, layer, DEPTH))
    dh = mm_nt("xq_dx_" + tag, dq, wq)
    sk = mm_tn("xk_dw_" + tag, mem, dkv, n_cols=D_MODEL, stack=(sk, layer, DEPTH))
    sv = mm_tn("xv_dw_" + tag, mem, dkv, b_col0=D_MODEL, n_cols=D_MODEL, stack=(sv, layer, DEPTH))
    return dh, (sq, sk, sv, so)


FFN_SHARD = FFN_HIDDEN // 4
TM_FFN = 512
TM_FFN_WIDE = 1024


def _silu_mul(a, u):
    return jax.nn.silu(a) * u


def ffn_fwd(tag, layer, h, wg, wu, wd):
    s = h.shape[0]
    tm, fs = TM_FFN, FFN_SHARD
    w_in = pl.BlockSpec((None, None, fs, D_MODEL), lambda k, i: (k, layer, 0, 0))
    tu = TM_FFN_WIDE
    act = pl.BlockSpec((None, tu, fs), lambda k, i: (k, i, 0))

    def up_body(h_ref, wg_ref, wu_ref, a_ref, u_ref, hid_ref):
        hv = h_ref[...].astype(bf16)
        a, u = _dg(hv, wg_ref[...], 1, 1), _dg(hv, wu_ref[...], 1, 1)
        a_ref[...], u_ref[...] = a.astype(bf16), u.astype(bf16)
        hid_ref[...] = _silu_mul(a, u).astype(bf16)

    a4, u4, hid4 = pl.pallas_call(
        up_body, grid=(4, s // tu),
        in_specs=[pl.BlockSpec((tu, D_MODEL), lambda k, i: (i, 0)), w_in, w_in],
        out_specs=[act, act, act],
        out_shape=[jax.ShapeDtypeStruct((4, s, fs), bf16)] * 3,
        compiler_params=_cparams("parallel", "parallel"), name="ffn_up_" + tag)(h, wg, wu)

    all_act = pl.BlockSpec((4, tm, fs), lambda i: (0, i, 0))
    all_w = pl.BlockSpec((4, None, fs, D_MODEL), lambda i: (0, layer, 0, 0))

    def down_body(hid_ref, wd_ref, o_ref):
        acc = _dg(hid_ref[0], wd_ref[0], 1, 0)
        for k in range(1, 4):
            acc = acc + _dg(hid_ref[k], wd_ref[k], 1, 0)
        o_ref[...] = acc

    out = pl.pallas_call(
        down_body, grid=(s // tm,), in_specs=[all_act, all_w],
        out_specs=pl.BlockSpec((tm, D_MODEL), lambda i: (i, 0)),
        out_shape=jax.ShapeDtypeStruct((s, D_MODEL), f32),
        compiler_params=_cparams("parallel"), name="ffn_down_" + tag)(hid4, wd)
    return out, (a4, u4, hid4)


def ffn_bwd(tag, layer, h, wg, wu, wd, res, dout, stacks=None):
    a4, u4, hid4 = res
    s = h.shape[0]
    tm, fs = TM_FFN, FFN_SHARD
    tu = TM_FFN_WIDE
    act = pl.BlockSpec((None, tu, fs), lambda k, i: (k, i, 0))

    def dact_body(do_ref, wd_ref, a_ref, u_ref, da_ref, du_ref):
        dhid = _dg(do_ref[...].astype(bf16), wd_ref[...], 1, 1)
        _, vjp = jax.vjp(_silu_mul, a_ref[...].astype(f32), u_ref[...].astype(f32))
        da, du = vjp(dhid)
        da_ref[...], du_ref[...] = da.astype(bf16), du.astype(bf16)

    da4, du4 = pl.pallas_call(
        dact_body, grid=(4, s // tu),
        in_specs=[pl.BlockSpec((tu, D_MODEL), lambda k, i: (i, 0)),
                  pl.BlockSpec((None, None, fs, D_MODEL), lambda k, i: (k, layer, 0, 0)), act, act],
        out_specs=[act, act], out_shape=[jax.ShapeDtypeStruct((4, s, fs), bf16)] * 2,
        compiler_params=_cparams("parallel", "parallel"), name="ffn_dact_" + tag)(dout, wd, a4, u4)

    all_act = pl.BlockSpec((4, tm, fs), lambda i: (0, i, 0))
    all_w = pl.BlockSpec((4, None, fs, D_MODEL), lambda i: (0, layer, 0, 0))

    def dx_body(da_ref, du_ref, wg_ref, wu_ref, o_ref):
        acc = _dg(da_ref[0], wg_ref[0], 1, 0) + _dg(du_ref[0], wu_ref[0], 1, 0)
        for k in range(1, 4):
            acc = acc + (_dg(da_ref[k], wg_ref[k], 1, 0) + _dg(du_ref[k], wu_ref[k], 1, 0))
        o_ref[...] = acc

    dh = pl.pallas_call(
        dx_body, grid=(s // tm,), in_specs=[all_act, all_act, all_w, all_w],
        out_specs=pl.BlockSpec((tm, D_MODEL), lambda i: (i, 0)),
        out_shape=jax.ShapeDtypeStruct((s, D_MODEL), f32),
        compiler_params=_cparams("parallel"), name="ffn_dx_" + tag)(da4, du4, wg, wu)

    tn = 256
    whole = pl.BlockSpec((None, s, fs), lambda k: (k, 0, 0))
    resident = pl.BlockSpec((s, D_MODEL), lambda k: (0, 0), pipeline_mode=pl.Buffered(1))

    def dwin_body(h_ref, da_ref, du_ref, *rest):
        dwg_ref, dwu_ref = rest[-2:]
        da, du = da_ref[...], du_ref[...]
        for c in range(D_MODEL // tn):
            sl = slice(c * tn, (c + 1) * tn)
            hv = h_ref[:, sl].astype(bf16)
            dwg_ref[:, sl] = _dg(da, hv, 0, 0)
            dwu_ref[:, sl] = _dg(du, hv, 0, 0)

    n_layers = wd.shape[1]
    layer_out = pl.BlockSpec((None, None, fs, D_MODEL), lambda k: (k, layer, 0, 0))
    stack_shape = jax.ShapeDtypeStruct((4, n_layers, fs, D_MODEL), f32)
    prev = [] if stacks is None else list(stacks)
    any_spec = [pl.BlockSpec(memory_space=pl.ANY)]

    dwg, dwu = pl.pallas_call(
        dwin_body, grid=(4,),
        in_specs=[resident, whole, whole] + any_spec * len(prev[:2]),
        out_specs=[layer_out] * 2, out_shape=[stack_shape] * 2,
        input_output_aliases={3: 0, 4: 1} if prev else {},
        compiler_params=_cparams("parallel"), name="ffn_dwin_" + tag)(h, da4, du4, *prev[:2])

    def dwd_body(hid_ref, do_ref, *rest):
        hid = hid_ref[...]
        for c in range(D_MODEL // tn):
            sl = slice(c * tn, (c + 1) * tn)
            rest[-1][:, sl] = _dg(hid, do_ref[:, sl].astype(bf16), 0, 0)

    dwd = pl.pallas_call(
        dwd_body, grid=(4,),
        in_specs=[whole, resident] + any_spec * len(prev[2:]),
        out_specs=layer_out, out_shape=stack_shape,
        input_output_aliases={2: 0} if prev else {},
        compiler_params=_cparams("parallel"), name="ffn_dwd_" + tag)(hid4, dout, *prev[2:])
    return dh, (dwg, dwu, dwd)


def loss_head(y, target):
    s, d = y.shape
    tm = TM_ROW

    def body(y_ref, t_ref, part_ref, dy_ref):
        e = y_ref[...] - t_ref[...]
        dy_ref[...] = e * (1.0 / d)
        p = jnp.sum(e * e, 0, keepdims=True) * (0.5 / d)

        @pl.when(pl.program_id(0) == 0)
        def _():
            part_ref[...] = p

        @pl.when(pl.program_id(0) != 0)
        def _():
            part_ref[...] += p

    return pl.pallas_call(
        body, grid=(s // tm,),
        in_specs=[pl.BlockSpec((tm, d), lambda i: (i, 0))] * 2,
        out_specs=[pl.BlockSpec((1, d), lambda i: (0, 0)), pl.BlockSpec((tm, d), lambda i: (i, 0))],
        out_shape=[jax.ShapeDtypeStruct((1, d), f32), jax.ShapeDtypeStruct((s, d), f32)],
        compiler_params=_cparams("arbitrary"), name="loss_head")(y, target)


TM_CONV = 1024


def _conv_rows(xx, w_ref, n_rows):
    a = w_ref[3:4, :] * xx
    for k in (1, 2, 3):
        a = a + w_ref[3 - k:4 - k, :] * pltpu.roll(xx, k, 0)
    return a


def _dn_act(a, is_qk):
    s = jax.nn.silu(a)
    n = s * lax.rsqrt(jnp.sum(s * s, -1, keepdims=True) + RMS_EPS)
    return jnp.where(is_qk, n, s)


def dn_conv_fwd(tag, proj, cw):
    s = proj.shape[0]
    tm, hb = TM_CONV, TM_CONV // SUBLANES

    def body(xh_ref, x_ref, w_ref, o_ref):
        j, t = pl.program_id(0), pl.program_id(1)
        halo = jnp.where(t > 0, xh_ref[...], 0.0)
        xx = jnp.concatenate([halo, x_ref[...]], 0)
        a = _conv_rows(xx, w_ref, tm + SUBLANES)
        o_ref[...] = _dn_act(a, j < 2 * DN_HEADS)[SUBLANES:, :]

    return pl.pallas_call(
        body, grid=(DN_QKV_DIM // LANES, s // tm),
        in_specs=[pl.BlockSpec((SUBLANES, LANES), lambda j, t: (jnp.maximum(t * hb - 1, 0), j)),
                  pl.BlockSpec((tm, LANES), lambda j, t: (t, j)),
                  pl.BlockSpec((DN_CONV, LANES), lambda j, t: (0, j))],
        out_specs=pl.BlockSpec((tm, LANES), lambda j, t: (t, j)),
        out_shape=jax.ShapeDtypeStruct((s, DN_QKV_DIM), f32),
        compiler_params=_cparams("parallel", "parallel"), name="dn_conv_" + tag)(proj, proj, cw)


def dn_conv_bwd(tag, proj, cw, dy):
    s = proj.shape[0]
    tm, hb = TM_CONV, TM_CONV // SUBLANES
    nt = s // tm
    n_ext = tm + 2 * SUBLANES

    def body(xb_ref, x_ref, xa_ref, dy_ref, dya_ref, w_ref, dx_ref, dw_ref):
        j, t = pl.program_id(0), pl.program_id(1)
        xx = jnp.concatenate([jnp.where(t > 0, xb_ref[...], 0.0), x_ref[...],
                              jnp.where(t < nt - 1, xa_ref[...], 0.0)], 0)
        dyy = jnp.concatenate([jnp.zeros((SUBLANES, LANES), f32), dy_ref[...],
                               jnp.where(t < nt - 1, dya_ref[...], 0.0)], 0)
        a = _conv_rows(xx, w_ref, n_ext)
        _, vjp = jax.vjp(lambda v: _dn_act(v, j < 2 * DN_HEADS), a)
        da, = vjp(dyy)
        dx = w_ref[3:4, :] * da
        for k in (1, 2, 3):
            dx = dx + w_ref[3 - k:4 - k, :] * pltpu.roll(da, n_ext - k, 0)
        dx_ref[...] = dx[SUBLANES:SUBLANES + tm, :]
        row = _iota2((n_ext, LANES), 0)
        da_in = jnp.where((row >= SUBLANES) & (row < SUBLANES + tm), da, 0.0)
        r8 = _iota2((SUBLANES, LANES), 0)
        dw = jnp.zeros((SUBLANES, LANES), f32)
        for k in range(DN_CONV):
            xs = xx if k == 0 else pltpu.roll(xx, k, 0)
            dw = dw + jnp.where(r8 == 3 - k, jnp.sum(da_in * xs, 0, keepdims=True), 0.0)

        @pl.when(t == 0)
        def _():
            dw_ref[...] = dw

        @pl.when(t != 0)
        def _():
            dw_ref[...] += dw

    nb8 = s // SUBLANES
    return pl.pallas_call(
        body, grid=(DN_QKV_DIM // LANES, nt),
        in_specs=[pl.BlockSpec((SUBLANES, LANES), lambda j, t: (jnp.maximum(t * hb - 1, 0), j)),
                  pl.BlockSpec((tm, LANES), lambda j, t: (t, j)),
                  pl.BlockSpec((SUBLANES, LANES), lambda j, t: (jnp.minimum((t + 1) * hb, nb8 - 1), j)),
                  pl.BlockSpec((tm, LANES), lambda j, t: (t, j)),
                  pl.BlockSpec((SUBLANES, LANES), lambda j, t: (jnp.minimum((t + 1) * hb, nb8 - 1), j)),
                  pl.BlockSpec((DN_CONV, LANES), lambda j, t: (0, j))],
        out_specs=[pl.BlockSpec((tm, LANES), lambda j, t: (t, j)),
                   pl.BlockSpec((SUBLANES, LANES), lambda j, t: (0, j))],
        out_shape=[jax.ShapeDtypeStruct((s, DN_QKV_DIM), f32), jax.ShapeDtypeStruct((SUBLANES, DN_QKV_DIM), f32)],
        compiler_params=_cparams("parallel", "arbitrary"), name="dn_conv_bwd_" + tag)(proj, proj, proj, dy, dy, cw)


def _gate_tile(ba, eb, ea, alog, dtb):
    beta = jax.nn.sigmoid(hdot(ba, eb))
    g = -jnp.exp(alog) * jax.nn.softplus(hdot(ba, ea) + dtb)
    return beta, g


def _each(fn, *lists):
    return [fn(*args) for args in zip(*lists)]


@functools.partial(jax.custom_vjp, nondiff_argnums=(1,))
def _halves(x, axis):
    h = x.shape[axis] // 2
    return (x[:h], x[h:]) if axis == 0 else (x[:, :h], x[:, h:])


def _halves_fwd(x, axis):
    return _halves(x, axis), None


def _halves_bwd(axis, _, g):
    return (jnp.concatenate(g, axis),)


_halves.defvjp(_halves_fwd, _halves_bwd)


def _tri_inv_unit(lowers):
    c = lowers[0].shape[0]
    r, col = _iota2((c, c), 0), _iota2((c, c), 1)
    eye = jnp.where(r == col, 1.0, 0.0).astype(f32)
    invs = None
    sh = 0
    while (1 << sh) < c:
        same_2b = lax.shift_right_logical(r, sh + 1) == lax.shift_right_logical(col, sh + 1)
        diff_b = lax.shift_right_logical(r, sh) != lax.shift_right_logical(col, sh)
        offs = [jnp.where(same_2b & diff_b, low, 0.0) for low in lowers]
        if invs is None:
            invs = [eye - off for off in offs]
        else:
            part = _each(lambda inv, off: _dot3(inv, off, 1, 0), invs, offs)
            invs = _each(lambda inv, p: inv - _dot3(p, inv, 1, 0), invs, part)
        sh += 1
    return invs


@jax.custom_vjp
def _known_inverse(lower, tinv):
    return tinv


def _known_inverse_fwd(lower, tinv):
    return tinv, tinv


def _known_inverse_bwd(tinv, g):
    tt = tinv.T
    return -hdot(hdot(tt, g), tt), jnp.zeros_like(tinv)


_known_inverse.defvjp(_known_inverse_fwd, _known_inverse_bwd)


def _delta_chunk(q, k, v, gb, betab, state, tinv_known=None):
    c, hd = DN_CHUNK, DN_HEAD_DIM
    r, col = _iota2((c, c), 0), _iota2((c, c), 1)
    causal, strict = r >= col, r > col
    tril = jnp.where(causal, 1.0, 0.0).astype(f32)
    gc = _each(lambda g: hdot(tril, g), gb)
    decay = _each(lambda g: jnp.where(causal, jnp.exp(jnp.where(causal, g - g.T, 0.0)), 0.0), gc)
    qs = _each(lambda t: t * (DN_HEAD_DIM ** -0.5), q)
    kb = _each(lambda a, b: a * b, k, betab)
    kq = _each(lambda a, b, kk: _halves(bdot(jnp.concatenate([a, b], 0), kk, 1, 1), 0), kb, qs, k)
    lower = _each(lambda x, d: jnp.where(strict, x[0], 0.0) * d, kq, decay)
    intra = _each(lambda x, d: x[1] * d, kq, decay)
    tinv = _tri_inv_unit(lower) if tinv_known is None else _each(_known_inverse, lower, tinv_known)
    eg = _each(jnp.exp, gc)
    uw = _each(lambda t, vv, b, kb_, e: _halves(hdot(t, jnp.concatenate([vv * b, kb_ * e], 1)), 1),
               tinv, v, betab, kb, eg)
    gl = _each(lambda g: jnp.sum(jnp.where(r == c - 1, g, 0.0), 0, keepdims=True), gc)
    k_dec = _each(lambda kk, a, g: kk * jnp.exp(a - g), k, gl, gc)
    ws = _each(lambda x, t, e, st: _halves(bdot(jnp.concatenate([x[1], t * e], 0), st, 1, 0), 0), uw, qs, eg, state)
    v_new = _each(lambda x, y: x[0] - y[0], uw, ws)
    out = _each(lambda y, a, vn: y[1] + bdot(a, vn, 1, 0), ws, intra, v_new)
    new_state = _each(lambda st, a, kd, vn: st * jnp.exp(a) + bdot(kd, vn, 0, 0), state, gl, k_dec, v_new)
    return tuple(out), tuple(new_state), tuple(tinv)


def delta_fwd(tag, qkv, gb, betab):
    s = qkv.shape[0]
    c, hd = DN_CHUNK, DN_HEAD_DIM
    n = s // c

    hg, ng = DN_HEADS_PER_STEP, DN_HEADS // DN_HEADS_PER_STEP

    def body(q_ref, k_ref, v_ref, g_ref, b_ref, o_ref, st_ref, ti_ref, state):
        @pl.when(pl.program_id(1) == 0)
        def _():
            state[...] = jnp.zeros_like(state)

        heads = lambda ref: tuple(ref[:, j * hd:(j + 1) * hd] for j in range(hg))
        st = tuple(state[j] for j in range(hg))
        outs, news, tinv = _delta_chunk(heads(q_ref), heads(k_ref), heads(v_ref), heads(g_ref), heads(b_ref), st)
        for j in range(hg):
            st_ref[j] = st[j]
            ti_ref[j] = tinv[j]
            o_ref[:, j * hd:(j + 1) * hd] = outs[j]
            state[j] = news[j]

    blk = lambda off: pl.BlockSpec((c, hg * hd), lambda h, i, _o=off: (i, h + _o))
    per_chunk = pl.BlockSpec((hg, None, hd, hd), lambda h, i: (h, i, 0, 0))
    return pl.pallas_call(
        body, grid=(ng, n),
        in_specs=[blk(0), blk(ng), blk(2 * ng), blk(0), blk(0)],
        out_specs=[blk(0), per_chunk, per_chunk],
        out_shape=[jax.ShapeDtypeStruct((s, DN_KEY_DIM), f32)] + [jax.ShapeDtypeStruct((DN_HEADS, n, hd, hd), f32)] * 2,
        scratch_shapes=[pltpu.VMEM((hg, hd, hd), f32)],
        compiler_params=_cparams("parallel", "arbitrary"), name="delta_" + tag)(qkv, qkv, qkv, gb, betab)


def delta_bwd(tag, qkv, gb, betab, states, tinvs, do):
    s = qkv.shape[0]
    c, hd = DN_CHUNK, DN_HEAD_DIM
    n = s // c

    hg, ng = DN_HEADS_PER_STEP, DN_HEADS // DN_HEADS_PER_STEP

    def body(q_ref, k_ref, v_ref, g_ref, b_ref, st_ref, ti_ref, do_ref, dqkv_ref, dg_ref, db_ref, dstate):
        @pl.when(pl.program_id(1) == 0)
        def _():
            dstate[...] = jnp.zeros_like(dstate)

        heads = lambda ref: tuple(ref[:, j * hd:(j + 1) * hd] for j in range(hg))
        tinv = tuple(ti_ref[j] for j in range(hg))
        _, vjp = jax.vjp(lambda *args: _delta_chunk(*args, tinv_known=tinv)[:2],
                         heads(q_ref), heads(k_ref), heads(v_ref), heads(g_ref), heads(b_ref),
                         tuple(st_ref[j] for j in range(hg)))
        grads = vjp((heads(do_ref), tuple(dstate[j] for j in range(hg))))
        for part, g in enumerate(grads[:3]):
            for j in range(hg):
                dqkv_ref[:, part * DN_KEY_DIM + j * hd:part * DN_KEY_DIM + (j + 1) * hd] = g[j]
        for ref, g in zip((dg_ref, db_ref), grads[3:5]):
            for j in range(hg):
                ref[:, j * hd:(j + 1) * hd] = g[j]
        for j in range(hg):
            dstate[j] = grads[5][j]

    assert ng == 1
    blk = lambda off: pl.BlockSpec((c, hg * hd), lambda h, i, _o=off: (n - 1 - i, h + _o))
    return pl.pallas_call(
        body, grid=(ng, n),
        in_specs=[blk(0), blk(ng), blk(2 * ng), blk(0), blk(0)]
        + [pl.BlockSpec((hg, None, hd, hd), lambda h, i: (h, n - 1 - i, 0, 0))] * 2 + [blk(0)],
        out_specs=[pl.BlockSpec((c, DN_QKV_DIM), lambda h, i: (n - 1 - i, 0)), blk(0), blk(0)],
        out_shape=[jax.ShapeDtypeStruct((s, DN_QKV_DIM), f32)] + [jax.ShapeDtypeStruct((s, DN_KEY_DIM), f32)] * 2,
        scratch_shapes=[pltpu.VMEM((hg, hd, hd), f32)],
        compiler_params=_cparams("parallel", "arbitrary"),
        name="delta_bwd_" + tag)(qkv, qkv, qkv, gb, betab, states, tinvs, do)


def _dn_out_tile(o, z, ng):
    outs = []
    for h in range(DN_HEADS):
        sl = slice(h * DN_HEAD_DIM, (h + 1) * DN_HEAD_DIM)
        oh = o[:, sl]
        nrm = oh * lax.rsqrt(jnp.mean(oh * oh, -1, keepdims=True) + RMS_EPS) * ng[:, sl]
        outs.append(nrm * jax.nn.silu(z[:, sl]))
    return (jnp.concatenate(outs, -1),)


def _head_selectors():
    r, c = _iota2((BA_PAD, DN_KEY_DIM), 0), _iota2((BA_PAD, DN_KEY_DIM), 1) // DN_HEAD_DIM
    return (r == c).astype(f32), (r == c + DN_HEADS).astype(f32)


def dn_mixer_fwd(tag, proj, cw, alog_b, dtb_b, ng_b):
    eb, ea = _head_selectors()
    ba = (proj, BA_PAD, COL_BA // BA_PAD)
    qkv = dn_conv_fwd(tag, proj, cw)
    betab, gb = rowmap("dn_gate_" + tag, _gate_tile, [ba], [eb, ea, alog_b, dtb_b], [DN_KEY_DIM] * 2, TM_ROW)
    o, states, tinvs = delta_fwd(tag, qkv, gb, betab)
    z = (proj, DN_KEY_DIM, COL_Z // DN_KEY_DIM)
    a_out = rowmap("dn_out_" + tag, _dn_out_tile, [o, z], [ng_b], [DN_KEY_DIM], TM_ROW)[0]
    return a_out, (qkv, betab, gb, o, states, tinvs)


def dn_mixer_bwd(tag, proj, cw, alog_b, dtb_b, ng_b, res, da_out):
    qkv, betab, gb, o, states, tinvs = res
    eb, ea = _head_selectors()
    ba = (proj, BA_PAD, COL_BA // BA_PAD)
    z = (proj, DN_KEY_DIM, COL_Z // DN_KEY_DIM)
    (do, dz), (dng,) = rowmap_bwd("dn_out_bwd_" + tag, _dn_out_tile, [o, z], [ng_b], [da_out], TM_ROW)
    dqkv, dgb, dbetab = delta_bwd(tag, qkv, gb, betab, states, tinvs, do)
    dqkv_raw, dcw = dn_conv_bwd(tag, proj, cw, dqkv)
    (dba,), (dalog, ddtb) = rowmap_bwd("dn_gate_bwd_" + tag, _gate_tile, [ba], [eb, ea, alog_b, dtb_b],
                                       [dbetab, dgb], TM_ROW, par_mask=[False, False, True, True])
    return dqkv_raw, dz, dba, dcw[:DN_CONV], dalog, ddtb, dng


def _swap_halves(x):
    n = x.shape[1]
    first = (_iota2((1, n), 1) % SW_HEAD_DIM) < SW_HEAD_DIM // 2
    return jnp.where(first, pltpu.roll(x, n - SW_HEAD_DIM // 2, 1), pltpu.roll(x, SW_HEAD_DIM // 2, 1))


def _rope_apply(x, cos, sin_signed):
    return x * cos + _swap_halves(x) * sin_signed


def _rope_transpose(dy, cos, sin_signed):
    return dy * cos + _swap_halves(dy * sin_signed)


def rope_tables(positions, s):
    half = SW_HEAD_DIM // 2
    inv_freq = ROPE_THETA ** (-jnp.arange(0, SW_HEAD_DIM, 2, dtype=f32) / SW_HEAD_DIM)
    ang = positions.reshape(s, 1).astype(f32) * inv_freq[None, :]
    cos, sin = jnp.cos(ang), jnp.sin(ang)
    cos_t = jnp.tile(jnp.concatenate([cos, cos], 1), (1, SW_HEADS))
    sin_t = jnp.tile(jnp.concatenate([-sin, sin], 1), (1, SW_HEADS))
    assert cos_t.shape == (s, SW_DIM) and half * 2 == SW_HEAD_DIM
    return cos_t, sin_t


def rope_fwd(tag, proj, cos, sin):
    def fn(q, k, v, c, sg):
        return _rope_apply(q, c, sg), _rope_apply(k, c, sg), v

    rows = [(proj, SW_DIM, COL_SWQ // SW_DIM), (proj, SW_DIM, COL_SWK // SW_DIM), (proj, SW_DIM, COL_SWV // SW_DIM), cos, sin]
    return rowmap("rope_" + tag, fn, rows, [], [SW_DIM] * 3, TM_ROW, out_dtypes=[bf16] * 3)


def _swa_block(q, kp, kc, vp, vc, first):
    blk = SW_BLOCK
    kk = jnp.concatenate([kp, kc], 0)
    vv = jnp.concatenate([vp, vc], 0)
    dist = (_iota2((blk, 2 * blk), 0) + blk) - _iota2((blk, 2 * blk), 1)
    kj = _iota2((blk, 2 * blk), 1)
    valid = (dist >= 0) & (dist <= blk) & ((kj >= blk) | jnp.logical_not(first))
    lane_head = _iota2((1, LANES), 1) // SW_HEAD_DIM
    outs, lses = [], []
    for p in range(SW_DIM // LANES):
        sl = slice(p * LANES, (p + 1) * LANES)
        qp, kp_, vp_ = q[:, sl], kk[:, sl], vv[:, sl]
        o_pair = jnp.zeros((blk, LANES), f32)
        l_pair = jnp.zeros((blk, LANES), f32)
        for e in range(LANES // SW_HEAD_DIM):
            msk = lane_head == e
            sc = bdot(jnp.where(msk, qp, 0.0), kp_, 1, 1) * (SW_HEAD_DIM ** -0.5)
            sc = jnp.where(valid, sc, -1e30)
            m = lax.stop_gradient(jnp.max(sc, -1, keepdims=True))
            pe = jnp.exp(sc - m)
            l = jnp.sum(pe, -1, keepdims=True)
            o = bdot(pe, vp_, 1, 0) / l
            o_pair = o_pair + jnp.where(msk, o, 0.0)
            l_pair = l_pair + jnp.where(msk, m + jnp.log(l), 0.0)
        outs.append(o_pair)
        lses.append(l_pair)
    return jnp.concatenate(outs, -1), jnp.concatenate(lses, -1)


def _swa_specs(r):
    cur = pl.BlockSpec((SW_BLOCK, SW_DIM), lambda rho, n: (n, rho))
    prev = pl.BlockSpec((SW_BLOCK, SW_DIM), lambda rho, n: (jnp.maximum(n - 1, 0), rho))
    return cur, prev


def swa_fwd(tag, r, q, k, v):
    s = q.shape[0]
    ln = s // r
    q2, k2, v2 = (t.reshape(ln, r * SW_DIM) for t in (q, k, v))
    cur, prev = _swa_specs(r)

    def body(q_ref, kp_ref, kc_ref, vp_ref, vc_ref, o_ref, l_ref):
        ins = [r[...].astype(f32) for r in (q_ref, kp_ref, kc_ref, vp_ref, vc_ref)]
        o, l = _swa_block(*ins, pl.program_id(1) == 0)
        o_ref[...] = o
        l_ref[...] = l

    o, l = pl.pallas_call(
        body, grid=(r, ln // SW_BLOCK),
        in_specs=[cur, prev, cur, prev, cur], out_specs=[cur, cur],
        out_shape=[jax.ShapeDtypeStruct((ln, r * SW_DIM), f32)] * 2,
        compiler_params=_cparams("parallel", "parallel"), name=f"swa{r}_{tag}")(q2, k2, k2, v2, v2)
    return o.reshape(s, SW_DIM), l.reshape(s, SW_DIM)


def swa_bwd(tag, r, q, k, v, do, dl):
    s = q.shape[0]
    ln = s // r
    q2, k2, v2, do2, dl2 = (t.reshape(ln, r * SW_DIM) for t in (q, k, v, do, dl))
    cur, prev = _swa_specs(r)

    def body(q_ref, kp_ref, kc_ref, vp_ref, vc_ref, do_ref, dl_ref, dq_ref, dka_ref, dkb_ref, dva_ref, dvb_ref):
        first = pl.program_id(1) == 0
        ins = [r[...].astype(f32) for r in (q_ref, kp_ref, kc_ref, vp_ref, vc_ref)]
        _, vjp = jax.vjp(lambda *a: _swa_block(*a, first), *ins)
        dq_ref[...], dka_ref[...], dkb_ref[...], dva_ref[...], dvb_ref[...] = vjp((do_ref[...], dl_ref[...]))

    outs = pl.pallas_call(
        body, grid=(r, ln // SW_BLOCK),
        in_specs=[cur, prev, cur, prev, cur, cur, cur], out_specs=[cur] * 5,
        out_shape=[jax.ShapeDtypeStruct((ln, r * SW_DIM), f32)] * 5,
        compiler_params=_cparams("parallel", "parallel"), name=f"swa{r}_bwd_{tag}")(q2, k2, k2, v2, v2, do2, dl2)
    return [t.reshape(s, SW_DIM) for t in outs]


def _combine_tile(o1, l1, o2, l2, o3, l3):
    m = lax.stop_gradient(jnp.maximum(jnp.maximum(l1, l2), l3))
    e1, e2, e3 = jnp.exp(l1 - m), jnp.exp(l2 - m), jnp.exp(l3 - m)
    return ((o1 * e1 + o2 * e2 + o3 * e3) / (e1 + e2 + e3),)


def swa_merge_bwd(tag, grads, cos, sin):
    s = cos.shape[0]
    tm = SW_BLOCK
    nt = s // tm
    here = pl.BlockSpec((tm, SW_DIM), lambda i: (i, 0))
    arrs, specs = [], []
    for r, g in zip(SW_DILATIONS, grads):
        ahead = pl.BlockSpec((tm, SW_DIM), lambda i, _r=r: (jnp.minimum(i + _r, nt - 1), 0))
        arrs += g
        specs += [here, ahead, here, ahead, here]

    def body(*refs):
        i = pl.program_id(0)
        c_ref, s_ref = refs[15], refs[16]
        dq_ref, dk_ref, dv_ref = refs[17:]
        dq = jnp.zeros((tm, SW_DIM), f32)
        dk = jnp.zeros((tm, SW_DIM), f32)
        dv = jnp.zeros((tm, SW_DIM), f32)
        for b, r in enumerate(SW_DILATIONS):
            gq, gka, gkb, gva, gvb = refs[5 * b:5 * b + 5]
            inside = i + r < nt
            dq = dq + gq[...]
            dk = dk + gkb[...] + jnp.where(inside, gka[...], 0.0)
            dv = dv + gvb[...] + jnp.where(inside, gva[...], 0.0)
        dq_ref[...] = _rope_transpose(dq, c_ref[...], s_ref[...])
        dk_ref[...] = _rope_transpose(dk, c_ref[...], s_ref[...])
        dv_ref[...] = dv

    return pl.pallas_call(
        body, grid=(nt,), in_specs=specs + [here, here], out_specs=[here] * 3,
        out_shape=[jax.ShapeDtypeStruct((s, SW_DIM), f32)] * 3,
        compiler_params=_cparams("parallel"), name="swa_merge_bwd_" + tag)(*arrs, cos, sin)


def swa_mixer_fwd(tag, proj, cos, sin):
    q, k, v = rope_fwd(tag, proj, cos, sin)
    ols = []
    for r in SW_DILATIONS:
        ols += list(swa_fwd(tag, r, q, k, v))
    b_out = rowmap("swa_comb_" + tag, _combine_tile, ols, [], [SW_DIM], TM_ROW)[0]
    return b_out, (q, k, v, ols)


def swa_mixer_bwd(tag, cos, sin, res, db_out):
    q, k, v, ols = res
    dols, _ = rowmap_bwd("swa_comb_bwd_" + tag, _combine_tile, ols, [], [db_out], TM_ROW)
    grads = [swa_bwd(tag, r, q, k, v, dols[2 * b], dols[2 * b + 1]) for b, r in enumerate(SW_DILATIONS)]
    return swa_merge_bwd(tag, grads, cos, sin)


TM_S5 = 512
S5_GPB = LANES // S5_GROUP
S5_NBLK = D_MODEL // LANES
S5_HALF = S5_GPB * S5_STATE
S5_BW = 2 * S5_HALF
S5_WIDTH = S5_NBLK * S5_BW
S5_TABW = S5_NBLK * S5_HALF


def _s5_disc_tile(a_re, a_im, log_dt, b_re, b_im, expand):
    dt = jnp.exp(log_dt)
    mag = jnp.exp(a_re * dt)
    abar_re, abar_im = mag * jnp.cos(a_im * dt), mag * jnp.sin(a_im * dt)
    n_re, n_im = abar_re - 1.0, abar_im
    den = a_re * a_re + a_im * a_im
    c_re = (n_re * a_re + n_im * a_im) / den
    c_im = (n_im * a_re - n_re * a_im) / den
    cx_re, cx_im = hdot(c_re, expand), hdot(c_im, expand)
    return abar_re, abar_im, cx_re * b_re - cx_im * b_im, cx_re * b_im + cx_im * b_re


def _s5_expand():
    return (_iota2((S5_STATE, S5_STATE * S5_GROUP), 1) // S5_GROUP == _iota2((S5_STATE, S5_STATE * S5_GROUP), 0)).astype(f32)


def s5_tables(a_re, a_im, log_dt):
    lanes = lambda v: v.reshape(1, S5_TABW)
    dt = jnp.broadcast_to(log_dt.reshape(S5_GROUPS, 1), (S5_GROUPS, S5_STATE))
    t = TM_S5

    def body(are_ref, aim_ref, ldt_ref, ar_ref, ai_ref, arr_ref, air_ref):
        dtv = jnp.exp(ldt_ref[...])
        lre, lim = are_ref[...] * dtv, aim_ref[...] * dtv
        row = _iota2((t, S5_HALF), 0)
        for asc, o_re, o_im in ((True, ar_ref, ai_ref), (False, arr_ref, air_ref)):
            n = (row + 1 if asc else t - row).astype(f32)
            mag = jnp.exp(n * lre)
            o_re[...] = mag * jnp.cos(n * lim)
            o_im[...] = mag * jnp.sin(n * lim)

    lane = pl.BlockSpec((1, S5_HALF), lambda j: (0, j))
    tab = pl.BlockSpec((t, S5_HALF), lambda j: (0, j))
    return pl.pallas_call(
        body, grid=(S5_NBLK,), in_specs=[lane] * 3, out_specs=[tab] * 4,
        out_shape=[jax.ShapeDtypeStruct((t, S5_TABW), f32)] * 4,
        compiler_params=_cparams("parallel"), name="s5_tables")(lanes(a_re), lanes(a_im), lanes(dt))


def s5_pack_weights(bbar_re, bbar_im, c_re, c_im):
    eye = jnp.eye(S5_GPB, dtype=f32)
    bb = jnp.stack([bbar_re.reshape(S5_GROUPS, S5_STATE, S5_GROUP), bbar_im.reshape(S5_GROUPS, S5_STATE, S5_GROUP)], 1)
    bb = bb.transpose(0, 3, 1, 2).reshape(S5_NBLK, S5_GPB, S5_GROUP, 2, S5_STATE)
    wb = (bb[:, :, :, :, None, :] * eye[None, :, None, None, :, None]).reshape(S5_NBLK, LANES, S5_BW)
    cc = jnp.stack([c_re, -c_im], 1)
    cc = cc.reshape(S5_NBLK, S5_GPB, 2, S5_GROUP, S5_STATE).transpose(0, 2, 1, 4, 3)
    wc = (cc[:, :, :, :, None, :] * eye[None, None, :, None, :, None]).reshape(S5_NBLK, S5_BW, LANES)
    return wb, wc


def s5_unpack_weight_grads(dwb, dwc):
    d6 = dwb.reshape(S5_NBLK, S5_GPB, S5_GROUP, 2, S5_GPB, S5_STATE)
    dbb = jnp.stack([d6[:, gl, :, :, gl, :] for gl in range(S5_GPB)])
    dbb = dbb.transpose(1, 0, 3, 4, 2).reshape(S5_GROUPS, 2, S5_STATE * S5_GROUP)
    c6 = dwc.reshape(S5_NBLK, 2, S5_GPB, S5_STATE, S5_GPB, S5_GROUP)
    dcc = jnp.stack([c6[:, :, gl, :, gl, :] for gl in range(S5_GPB)])
    dcc = dcc.transpose(1, 0, 2, 4, 3).reshape(S5_GROUPS, 2, S5_GROUP, S5_STATE)
    return dbb[:, 0], dbb[:, 1], dcc[:, 0], -dcc[:, 1]


def _s5_step_rows(t):
    d, out = 1, []
    while d < t:
        out.append(d)
        d *= 2
    return out


def s5_core_fwd(tag, u, wb, wc, a1, a2, dskip):
    s = u.shape[0]
    t = TM_S5

    def body(u_ref, wb_ref, wc_ref, ar_ref, ai_ref, d_ref, y_ref, x_ref, carry):
        @pl.when(pl.program_id(1) == 0)
        def _():
            carry[...] = jnp.zeros_like(carry)

        uv = u_ref[...]
        bu = bdot(uv, wb_ref[...], 1, 0)
        row = _iota2((t, LANES), 0)
        for c in range(S5_HALF // LANES):
            re, im = slice(c * LANES, (c + 1) * LANES), slice(S5_HALF + c * LANES, S5_HALF + (c + 1) * LANES)
            xr, xi = bu[:, re], bu[:, im]
            for d in _s5_step_rows(t):
                ar, ai = ar_ref[d - 1:d, re], ai_ref[d - 1:d, re]
                if d % SUBLANES:
                    keep = row >= d
                    sr = jnp.where(keep, pltpu.roll(xr, d, 0), 0.0)
                    si = jnp.where(keep, pltpu.roll(xi, d, 0), 0.0)
                    xr, xi = xr + ar * sr - ai * si, xi + ar * si + ai * sr
                else:
                    sr, si = xr[:t - d], xi[:t - d]
                    xr = jnp.concatenate([xr[:d], xr[d:] + (ar * sr - ai * si)], 0)
                    xi = jnp.concatenate([xi[:d], xi[d:] + (ar * si + ai * sr)], 0)
            cr, ci = carry[:, re], carry[:, im]
            ar, ai = ar_ref[:, re], ai_ref[:, re]
            x_ref[:, re] = xr + ar * cr - ai * ci
            x_ref[:, im] = xi + ar * ci + ai * cr
        carry[...] = x_ref[t - 1:t, :]
        y_ref[...] = bdot(x_ref[...], wc_ref[...], 1, 0) + d_ref[...] * uv

    tab = pl.BlockSpec((t, S5_HALF), lambda j, i: (0, j))
    return pl.pallas_call(
        body, grid=(S5_NBLK, s // t),
        in_specs=[pl.BlockSpec((t, LANES), lambda j, i: (i, j)),
                  pl.BlockSpec((None, LANES, S5_BW), lambda j, i: (j, 0, 0)),
                  pl.BlockSpec((None, S5_BW, LANES), lambda j, i: (j, 0, 0)),
                  tab, tab, pl.BlockSpec((1, LANES), lambda j, i: (0, j))],
        out_specs=[pl.BlockSpec((t, LANES), lambda j, i: (i, j)), pl.BlockSpec((t, S5_BW), lambda j, i: (i, j))],
        out_shape=[jax.ShapeDtypeStruct((s, D_MODEL), f32), jax.ShapeDtypeStruct((s, S5_WIDTH), f32)],
        scratch_shapes=[pltpu.VMEM((1, S5_BW), f32)],
        compiler_params=_cparams("parallel", "arbitrary"), name="s5_core_" + tag)(u, wb, wc, a1, a2, dskip)


def s5_core_bwd(tag, u, x, wb, wc, a1, a2, a1r, a2r, dskip, dy):
    s = u.shape[0]
    t = TM_S5
    nt = s // t
    hb = t // SUBLANES

    def body(u_ref, dy_ref, x_ref, xh_ref, wb_ref, wc_ref, ar_ref, ai_ref, arr_ref, air_ref, d_ref,
             du_ref, dwb_ref, dwc_ref, dd_ref, q1_ref, q2_ref, carry, lam_scr):
        i = pl.program_id(1)
        tt = nt - 1 - i

        @pl.when(i == 0)
        def _():
            carry[...] = jnp.zeros_like(carry)

        uv, dyv, xv = u_ref[...], dy_ref[...], x_ref[...]
        lam = bdot(dyv, wc_ref[...], 1, 1)
        row = _iota2((t, LANES), 0)
        x_last = jnp.where(tt > 0, xh_ref[SUBLANES - 1:SUBLANES, :], 0.0)
        q1s, q2s = [], []
        for c in range(S5_HALF // LANES):
            re, im = slice(c * LANES, (c + 1) * LANES), slice(S5_HALF + c * LANES, S5_HALF + (c + 1) * LANES)
            lr, li = lam[:, re], lam[:, im]
            for d in _s5_step_rows(t):
                ar, ai = ar_ref[d - 1:d, re], ai_ref[d - 1:d, re]
                if d % SUBLANES:
                    keep = row < t - d
                    sr = jnp.where(keep, pltpu.roll(lr, t - d, 0), 0.0)
                    si = jnp.where(keep, pltpu.roll(li, t - d, 0), 0.0)
                    lr, li = lr + ar * sr + ai * si, li + ar * si - ai * sr
                else:
                    sr, si = lr[d:], li[d:]
                    lr = jnp.concatenate([lr[:t - d] + (ar * sr + ai * si), lr[t - d:]], 0)
                    li = jnp.concatenate([li[:t - d] + (ar * si - ai * sr), li[t - d:]], 0)
            cr, ci = carry[:, re], carry[:, im]
            ar, ai = arr_ref[:, re], air_ref[:, re]
            lr, li = lr + ar * cr + ai * ci, li + ar * ci - ai * cr
            lam_scr[:, re] = lr
            lam_scr[:, im] = li
            pr = jnp.where(row == 0, x_last[:, re], pltpu.roll(xv[:, re], 1, 0))
            pi = jnp.where(row == 0, x_last[:, im], pltpu.roll(xv[:, im], 1, 0))
            p1, p2 = lr * pr + li * pi, li * pr - lr * pi
            q1, q2 = p1[:SUBLANES, :], p2[:SUBLANES, :]
            for k in range(1, hb):
                q1 = q1 + p1[k * SUBLANES:(k + 1) * SUBLANES, :]
                q2 = q2 + p2[k * SUBLANES:(k + 1) * SUBLANES, :]
            q1s.append(q1)
            q2s.append(q2)
        carry[...] = lam_scr[0:1, :]
        lam = lam_scr[...]
        du_ref[...] = bdot(lam, wb_ref[...], 1, 1) + d_ref[...] * dyv
        upd = [(dwb_ref, bdot(uv, lam, 0, 0)), (dwc_ref, bdot(xv, dyv, 0, 0)),
               (dd_ref, jnp.sum(dyv * uv, 0, keepdims=True)),
               (q1_ref, jnp.concatenate(q1s, 1)), (q2_ref, jnp.concatenate(q2s, 1))]

        @pl.when(i == 0)
        def _():
            for ref, val in upd:
                ref[...] = val

        @pl.when(i != 0)
        def _():
            for ref, val in upd:
                ref[...] += val

    nb8 = s // SUBLANES
    rev = lambda w: pl.BlockSpec((t, w), lambda j, i: (nt - 1 - i, j))
    tab = pl.BlockSpec((t, S5_HALF), lambda j, i: (0, j))
    return pl.pallas_call(
        body, grid=(S5_NBLK, nt),
        in_specs=[rev(LANES), rev(LANES), rev(S5_BW),
                  pl.BlockSpec((SUBLANES, S5_BW), lambda j, i: (jnp.maximum((nt - 1 - i) * hb - 1, 0), j)),
                  pl.BlockSpec((None, LANES, S5_BW), lambda j, i: (j, 0, 0)),
                  pl.BlockSpec((None, S5_BW, LANES), lambda j, i: (j, 0, 0)),
                  tab, tab, tab, tab, pl.BlockSpec((1, LANES), lambda j, i: (0, j))],
        out_specs=[rev(LANES),
                   pl.BlockSpec((None, LANES, S5_BW), lambda j, i: (j, 0, 0)),
                   pl.BlockSpec((None, S5_BW, LANES), lambda j, i: (j, 0, 0)),
                   pl.BlockSpec((1, LANES), lambda j, i: (0, j)),
                   pl.BlockSpec((SUBLANES, S5_HALF), lambda j, i: (0, j)),
                   pl.BlockSpec((SUBLANES, S5_HALF), lambda j, i: (0, j))],
        out_shape=[jax.ShapeDtypeStruct((s, D_MODEL), f32),
                   jax.ShapeDtypeStruct((S5_NBLK, LANES, S5_BW), f32),
                   jax.ShapeDtypeStruct((S5_NBLK, S5_BW, LANES), f32),
                   jax.ShapeDtypeStruct((1, D_MODEL), f32),
                   jax.ShapeDtypeStruct((SUBLANES, S5_TABW), f32),
                   jax.ShapeDtypeStruct((SUBLANES, S5_TABW), f32)],
        scratch_shapes=[pltpu.VMEM((1, S5_BW), f32), pltpu.VMEM((t, S5_BW), f32)],
        compiler_params=_cparams("parallel", "arbitrary"),
        name="s5_core_bwd_" + tag)(u, dy, x, x, wb, wc, a1, a2, a1r, a2r, dskip)


def _gelu_tile(y):
    return (jax.nn.gelu(y),)


def s5_mixer_fwd(tag, u, prm, w_og):
    a_re, a_im, log_dt, b_re, b_im, c_re, c_im, dskip = prm
    disc_in = [a_re, a_im, log_dt.reshape(S5_GROUPS, 1), b_re.reshape(S5_GROUPS, -1), b_im.reshape(S5_GROUPS, -1)]
    abar_re, abar_im, bbar_re, bbar_im = rowmap("s5_disc_" + tag, _s5_disc_tile, disc_in, [_s5_expand()],
                                                [S5_STATE, S5_STATE, S5_STATE * S5_GROUP, S5_STATE * S5_GROUP], S5_GROUPS)
    del abar_re, abar_im
    a1, a2, a1r, a2r = s5_tables(a_re, a_im, log_dt)
    wb, wc = s5_pack_weights(bbar_re, bbar_im, c_re, c_im)
    wb, wc = wb.astype(bf16), wc.astype(bf16)
    y, x = s5_core_fwd(tag, u, wb, wc, a1, a2, dskip.reshape(1, D_MODEL))
    hid = rowmap("s5_gelu_" + tag, _gelu_tile, [y], [], [D_MODEL], TM_ROW, out_dtypes=[bf16])[0]
    og = mm_nn("s5_og_" + tag, hid, w_og)
    mix = rowmap("s5_glu_" + tag, _glu_tile, [og], [], [D_MODEL], TM_ROW)[0]
    return mix, (disc_in, a1, a2, a1r, a2r, wb, wc, x, y, hid, og)


def s5_mixer_bwd(tag, idx, u, prm, w_og, res, dmix, stacks):
    a_re, a_im, log_dt, b_re, b_im, c_re, c_im, dskip = prm
    disc_in, a1, a2, a1r, a2r, wb, wc, x, y, hid, og = res
    (dog,), _ = rowmap_bwd("s5_glu_bwd_" + tag, _glu_tile, [og], [], [dmix], TM_ROW)
    n_odd = DEPTH // 2
    dw_og = (mm_tn("s5_wo_dw_" + tag, hid, dog, n_cols=D_MODEL, stack=(stacks[0], idx, n_odd)),
             mm_tn("s5_wg_dw_" + tag, hid, dog, b_col0=D_MODEL, n_cols=D_MODEL, stack=(stacks[1], idx, n_odd)))
    dhid = mm_nt("s5_og_dx_" + tag, dog, w_og)
    (dy,), _ = rowmap_bwd("s5_gelu_bwd_" + tag, _gelu_tile, [y], [], [dhid], TM_ROW)
    du, dwb, dwc, ddskip, q1, q2 = s5_core_bwd(tag, u, x, wb, wc, a1, a2, a1r, a2r, dskip.reshape(1, D_MODEL), dy)
    dbbar_re, dbbar_im, dc_re, dc_im = s5_unpack_weight_grads(dwb, dwc)
    dabar_re = q1.sum(0).reshape(S5_GROUPS, S5_STATE)
    dabar_im = q2.sum(0).reshape(S5_GROUPS, S5_STATE)
    grads, _ = rowmap_bwd("s5_disc_bwd_" + tag, _s5_disc_tile, disc_in, [_s5_expand()],
                          [dabar_re, dabar_im, dbbar_re, dbbar_im], S5_GROUPS, par_mask=[False])
    da_re, da_im, dlog_dt, db_re, db_im = grads
    return du, (da_re, da_im, dlog_dt.reshape(S5_GROUPS), db_re.reshape(b_re.shape), db_im.reshape(b_im.shape),
                dc_re, dc_im, ddskip.reshape(D_MODEL)), dw_og


HYB_IN = 3592
_IN_B0, _IN_SW0 = 2048, 2056


IN_SHARD = HYB_IN // 4
SHARD_ORDER_GRADS = ("hyb_w_in", "ffn_wg", "ffn_wu", "ffn_wd")
FFN_TRANSPOSED = ("ffn_wg", "ffn_wu")
BIG_SHARDED = ("hyb_w_in", "hyb_w_out", "s5_glu_wo", "s5_glu_wg", "xq_w", "xk_w", "xv_w", "xo_w", "ffn_wg", "ffn_wu", "ffn_wd")


def _w_in_pieces():
    runs = [(0, _IN_B0, 0), (_IN_B0, _IN_SW0, COL_BA), (_IN_SW0, HYB_IN, _IN_B0)]
    out = []
    for sh in range(4):
        lo, hi = sh * IN_SHARD, (sh + 1) * IN_SHARD
        for r_lo, r_hi, c_lo in runs:
            a, b = max(lo, r_lo), min(hi, r_hi)
            if a < b:
                out.append((sh, a - lo, b - lo, c_lo + a - r_lo))
    return out


def w_in_to_canonical(tag, layer, w4):
    tr = 128

    def body(w_ref, o_ref):
        o_ref[:, COL_BA:] = jnp.zeros((tr, BA_PAD), o_ref.dtype)
        for sh, a, b, c in _w_in_pieces():
            o_ref[:, c:c + b - a] = w_ref[sh, :, a:b]

    return pl.pallas_call(
        body, grid=(D_MODEL // tr,),
        in_specs=[pl.BlockSpec((4, None, tr, IN_SHARD), lambda i: (0, layer, i, 0))],
        out_specs=pl.BlockSpec((tr, PROJ_COLS), lambda i: (i, 0)),
        out_shape=jax.ShapeDtypeStruct((D_MODEL, PROJ_COLS), w4.dtype),
        compiler_params=_cparams("parallel"), name="w_in_canon_" + tag)(w4)


def w_in_grad_to_shards(tag, layer, g, stack, n_layers):
    tr = 128
    extra, extra_specs, _ = _stacked(stack)

    def body(g_ref, *rest):
        o_ref = rest[-1]
        for sh, a, b, c in _w_in_pieces():
            o_ref[sh, :, a:b] = g_ref[:, c:c + b - a]

    return pl.pallas_call(
        body, grid=(D_MODEL // tr,),
        in_specs=[pl.BlockSpec((tr, PROJ_COLS), lambda i: (i, 0))] + extra_specs,
        out_specs=pl.BlockSpec((4, None, tr, IN_SHARD), lambda i: (0, layer, i, 0)),
        out_shape=jax.ShapeDtypeStruct((4, n_layers, D_MODEL, IN_SHARD), f32),
        input_output_aliases={1: 0} if extra else {},
        compiler_params=_cparams("parallel"), name="w_in_grad_shards_" + tag)(g, *extra)


def _add2(name, a, b):
    return rowmap(name, lambda p, q: (p + q,), [a, b], [], [a.shape[1]], _pick(a.shape[0], (256, 128, 64, 32, 16, 8)))[0]


def local_step(x, mem, positions, target, p):
    s = x.shape[0]
    cos, sin = rope_tables(positions, s)
    row = lambda v: v.reshape(1, -1).astype(f32)
    wg4, wu4, wd4 = (p[n].astype(bf16) for n in ("ffn_wg", "ffn_wu", "ffn_wd"))
    h = h16 = x
    tape = []
    for l in range(DEPTH):
        i, tag = l // 2, str(l)
        t = {"h0": h, "h0_16": h16}
        if l % 2 == 0:
            t["w_in"] = w_in_to_canonical(tag, i, p["hyb_w_in"].astype(bf16))
            t["w_out"] = p["hyb_w_out"][i].astype(bf16)
            t["dn_prm"] = (p["dn_conv_w"][i].astype(f32), row(jnp.repeat(p["dn_a_log"][i], DN_HEAD_DIM)),
                           row(jnp.repeat(p["dn_dt_bias"][i], DN_HEAD_DIM)), row(jnp.tile(p["dn_norm_g"][i], DN_HEADS)))
            t["proj"] = mm_nn("hyb_in_" + tag, h16, t["w_in"])
            a_out, t["dn"] = dn_mixer_fwd(tag, t["proj"], *t["dn_prm"])
            b_out, t["swa"] = swa_mixer_fwd(tag, t["proj"], cos, sin)
            t["mixed"] = jnp.concatenate([a_out, b_out], 1)
            mix = mm_nn("hyb_out_" + tag, t["mixed"], t["w_out"])
        else:
            t["s5_prm"] = tuple(p[n][i].astype(f32) for n in
                                ("s5_a_re", "s5_a_im", "s5_log_dt", "s5_b_re", "s5_b_im", "s5_c_re", "s5_c_im", "s5_d"))
            t["w_og"] = jnp.concatenate([p["s5_glu_wo"][i], p["s5_glu_wg"][i]], 1).astype(bf16)
            mix, t["s5"] = s5_mixer_fwd(tag, h, t["s5_prm"], t["w_og"])
        t["mix"] = mix
        t["ln"] = [(row(p[g][l]), row(p[b][l])) for g, b in
                   (("ln_mix_g", "ln_mix_b"), ("ln_x_g", "ln_x_b"), ("ln_ffn_g", "ln_ffn_b"))]
        t["h1"], t["h1_16"] = postnorm_fwd("mix" + tag, h, mix, *t["ln"][0])
        t["wq"], t["wo"] = p["xq_w"][l].astype(bf16), p["xo_w"][l].astype(bf16)
        t["wkv"] = jnp.concatenate([p["xk_w"][l], p["xv_w"][l]], 1).astype(bf16)
        t["xo"], t["xres"] = xattn_fwd(tag, t["h1_16"], mem, t["wq"], t["wkv"], t["wo"])
        t["h2"], t["h2_16"] = postnorm_fwd("x" + tag, t["h1"], t["xo"], *t["ln"][1])
        t["fo"], t["fres"] = ffn_fwd(tag, l, t["h2_16"], wg4, wu4, wd4)
        h, h16 = postnorm_fwd("ffn" + tag, t["h2"], t["fo"], *t["ln"][2])
        tape.append(t)

    part, dh = loss_head(h, target)
    loss = jnp.sum(part)

    g = {n: [None] * v.shape[0] for n, v in p.items() if n not in BIG_SHARDED}
    st = {n: None for n in BIG_SHARDED}
    for l in reversed(range(DEPTH)):
        i, tag, t = l // 2, str(l), tape[l]
        dh2a, dfo, dg, db = postnorm_bwd("ffn" + tag, t["h2"], t["fo"], *t["ln"][2], dh)
        g["ln_ffn_g"][l], g["ln_ffn_b"][l] = dg[0], db[0]
        ffn_names = ("ffn_wg", "ffn_wu", "ffn_wd")
        prev = None if st["ffn_wd"] is None else [st[n] for n in ffn_names]
        dh2b, new = ffn_bwd(tag, l, t["h2_16"], wg4, wu4, wd4, t["fres"], dfo, prev)
        st.update(zip(ffn_names, new))
        dh1a, dxo, dg, db = postnorm_bwd("x" + tag, t["h1"], t["xo"], *t["ln"][1], [dh2a, dh2b])
        g["ln_x_g"][l], g["ln_x_b"][l] = dg[0], db[0]
        x_names = ("xq_w", "xk_w", "xv_w", "xo_w")
        dh1b, new = xattn_bwd(tag, l, t["h1_16"], mem, t["wq"], t["wkv"], t["wo"], t["xres"], dxo, [st[n] for n in x_names])
        st.update(zip(x_names, new))
        dh0a, dmix, dg, db = postnorm_bwd("mix" + tag, t["h0"], t["mix"], *t["ln"][0], [dh1a, dh1b])
        g["ln_mix_g"][l], g["ln_mix_b"][l] = dg[0], db[0]
        if l % 2 == 0:
            st["hyb_w_out"] = mm_tn("hyb_out_dw_" + tag, t["mixed"], dmix, stack=(st["hyb_w_out"], i, DEPTH // 2))
            dmixed = mm_nt("hyb_out_dx_" + tag, dmix, t["w_out"])
            dqkv, dz, dba, dcw, dalog, ddtb, dng = dn_mixer_bwd(tag, t["proj"], *t["dn_prm"], t["dn"], (dmixed, DN_KEY_DIM, 0))
            g["dn_conv_w"][i] = dcw
            g["dn_a_log"][i] = dalog.reshape(DN_HEADS, DN_HEAD_DIM).sum(1)
            g["dn_dt_bias"][i] = ddtb.reshape(DN_HEADS, DN_HEAD_DIM).sum(1)
            g["dn_norm_g"][i] = dng.reshape(DN_HEADS, DN_HEAD_DIM).sum(0)
            dq, dk, dv = swa_mixer_bwd(tag, cos, sin, t["swa"], (dmixed, SW_DIM, 1))
            dproj = jnp.concatenate([dqkv, dz, dq, dk, dv, dba], 1)
            st["hyb_w_in"] = w_in_grad_to_shards(tag, i, mm_tn("hyb_in_dw_" + tag, t["h0_16"], dproj), st["hyb_w_in"], DEPTH // 2)
            dh0b = mm_nt("hyb_in_dx_" + tag, dproj, t["w_in"])
        else:
            dh0b, dprm, (st["s5_glu_wo"], st["s5_glu_wg"]) = s5_mixer_bwd(
                tag, i, t["h0"], t["s5_prm"], t["w_og"], t["s5"], dmix, (st["s5_glu_wo"], st["s5_glu_wg"]))
            for n, v in zip(("s5_a_re", "s5_a_im", "s5_log_dt", "s5_b_re", "s5_b_im", "s5_c_re", "s5_c_im", "s5_d"), dprm):
                g[n][i] = v
        dh = [dh0a, dh0b]
    grad_x = _add2("grad_x", dh[0], dh[1])
    grads = {n: jnp.stack(v) for n, v in g.items()}
    grads.update(st)
    return loss, grad_x, grads


WEIGHT_NAMES = ("hyb_w_in", "dn_conv_w", "dn_a_log", "dn_dt_bias", "dn_norm_g", "hyb_w_out", "s5_a_re", "s5_a_im",
                "s5_log_dt", "s5_b_re", "s5_b_im", "s5_c_re", "s5_c_im", "s5_d", "s5_glu_wo", "s5_glu_wg",
                "ln_mix_g", "ln_mix_b", "xq_w", "xk_w", "xv_w", "xo_w", "ln_x_g", "ln_x_b",
                "ffn_wg", "ffn_wu", "ffn_wd", "ln_ffn_g", "ln_ffn_b")
SHARD_AXIS = {"hyb_w_in": 2, "dn_conv_w": 2, "hyb_w_out": 1, "s5_d": 1, "s5_glu_wo": 1, "s5_glu_wg": 1,
              "xq_w": 1, "xk_w": 1, "xv_w": 1, "xo_w": 1, "ffn_wg": 2, "ffn_wu": 2, "ffn_wd": 1}
GATHER_F32 = ("dn_conv_w", "s5_d")
N_CHIPS = 4
PACK_COLS = 1024
_ANY = pl.BlockSpec(memory_space=pl.ANY)


def _pos():
    return lax.axis_index("x"), lax.axis_index("y"), lax.axis_index("c")


def _chip_peers(mx, my):
    return [(1 - mx, my), (mx, 1 - my), (1 - mx, 1 - my)]


def _rcopy(src, dst, ssem, rsem, dev):
    return pltpu.make_async_remote_copy(src_ref=src, dst_ref=dst, send_sem=ssem, recv_sem=rsem,
                                        device_id=dev, device_id_type=pl.DeviceIdType.MESH)


def comm_allgather4(name, x):
    def body(x_ref, o_ref, ssem, rsem, lsem):
        mx, my, mc = _pos()
        me = 2 * mx + my
        peers = _chip_peers(mx, my)
        loc = pltpu.make_async_copy(x_ref, o_ref.at[me], lsem)
        loc.start()
        sends = [_rcopy(x_ref, o_ref.at[me], ssem.at[k], rsem.at[k], (px, py, mc)) for k, (px, py) in enumerate(peers)]
        for cp in sends:
            cp.start()
        for k, (px, py) in enumerate(peers):
            _rcopy(x_ref, o_ref.at[2 * px + py], ssem.at[k], rsem.at[k], (px, py, mc)).wait_recv()
        for cp in sends:
            cp.wait_send()
        loc.wait()

    return pl.pallas_call(
        body, out_shape=jax.ShapeDtypeStruct((N_CHIPS,) + x.shape, x.dtype), in_specs=[_ANY], out_specs=_ANY,
        scratch_shapes=[pltpu.SemaphoreType.DMA((3,)), pltpu.SemaphoreType.DMA((3,)), pltpu.SemaphoreType.DMA],
        name=name)(x)


def _multi_call(name, body, ins, out_shapes, sems, in_place=False):
    return pl.pallas_call(
        body, out_shape=out_shapes, in_specs=[_ANY] * len(ins), out_specs=[_ANY] * len(out_shapes),
        scratch_shapes=sems, input_output_aliases={w: w for w in range(len(ins))} if in_place else {},
        name=name)(*ins)


def comm_gather_weights(name, slots):
    n = len(slots)

    def body(*refs):
        os_ = refs[n:2 * n]
        ssem, rsem, fssem, frsem = refs[2 * n:]
        mx, my, mc = _pos()
        me = 2 * mx + my
        peers = _chip_peers(mx, my)
        sib = (mx, my, 1 - mc)
        half = [o.shape[1] // 2 for o in os_]
        mine = [pl.ds(mc * h, h) for h in half]
        other = [pl.ds((1 - mc) * h, h) for h in half]
        sends = [_rcopy(os_[w].at[me, mine[w]], os_[w].at[me, mine[w]], ssem.at[w, k], rsem.at[w, k], (px, py, mc))
                 for w in range(n) for k, (px, py) in enumerate(peers)]
        for cp in sends:
            cp.start()
        fwds = []
        for w in range(n):
            for k, (px, py) in enumerate(peers):
                landed = os_[w].at[2 * px + py, mine[w]]
                _rcopy(landed, landed, ssem.at[w, k], rsem.at[w, k], (px, py, mc)).wait_recv()
                fw = _rcopy(landed, landed, fssem.at[w, k], frsem.at[w, k], sib)
                fw.start()
                fwds.append(fw)
        for w in range(n):
            for k, (px, py) in enumerate(peers):
                theirs = os_[w].at[2 * px + py, other[w]]
                _rcopy(theirs, theirs, fssem.at[w, k], frsem.at[w, k], sib).wait_recv()
        for cp in sends + fwds:
            cp.wait_send()

    dma = pltpu.SemaphoreType.DMA
    return _multi_call(name, body, slots, [jax.ShapeDtypeStruct(x.shape, x.dtype) for x in slots],
                       [dma((n, 3)), dma((n, 3)), dma((n, 3)), dma((n, 3))], in_place=True)


def comm_sibling_halves(name, gs):
    n = len(gs)

    def body(*refs):
        xs, os_ = refs[:n], refs[n:2 * n]
        ssem, rsem = refs[2 * n:]
        mx, my, mc = _pos()
        sib = (mx, my, 1 - mc)
        sends = []
        for w in range(n):
            h = xs[w].shape[1] // 2
            for j in range(N_CHIPS):
                sends.append(_rcopy(xs[w].at[j, pl.ds((1 - mc) * h, h)], os_[w].at[j], ssem.at[w, j], rsem.at[w, j], sib))
        for cp in sends:
            cp.start()
        for w in range(n):
            for j in range(N_CHIPS):
                _rcopy(os_[w].at[j], os_[w].at[j], ssem.at[w, j], rsem.at[w, j], sib).wait_recv()
        for cp in sends:
            cp.wait_send()

    dma = pltpu.SemaphoreType.DMA
    return _multi_call(name, body, gs,
                       [jax.ShapeDtypeStruct((N_CHIPS, g.shape[1] // 2) + g.shape[2:], g.dtype) for g in gs],
                       [dma((n, N_CHIPS)), dma((n, N_CHIPS))])


def comm_alltoall4(name, xs):
    n = len(xs)

    def body(*refs):
        xr, os_ = refs[:n], refs[n:2 * n]
        ssem, rsem = refs[2 * n:]
        mx, my, mc = _pos()
        me = 2 * mx + my
        peers = _chip_peers(mx, my)
        sends = [_rcopy(xr[w].at[2 * px + py], os_[w].at[me], ssem.at[w, k], rsem.at[w, k], (px, py, mc))
                 for w in range(n) for k, (px, py) in enumerate(peers)]
        for cp in sends:
            cp.start()
        for w in range(n):
            for k, (px, py) in enumerate(peers):
                dst = os_[w].at[2 * px + py]
                _rcopy(dst, dst, ssem.at[w, k], rsem.at[w, k], (px, py, mc)).wait_recv()
        for cp in sends:
            cp.wait_send()

    dma = pltpu.SemaphoreType.DMA
    return _multi_call(name, body, xs, [jax.ShapeDtypeStruct(x.shape, x.dtype) for x in xs], [dma((n, 3)), dma((n, 3))])


def comm_sibling_join(name, bs):
    n = len(bs)

    def body(*refs):
        os_ = refs[n:2 * n]
        ssem, rsem = refs[2 * n:]
        mx, my, mc = _pos()
        sib = (mx, my, 1 - mc)
        sends = [_rcopy(os_[w].at[mc], os_[w].at[mc], ssem.at[w], rsem.at[w], sib) for w in range(n)]
        for cp in sends:
            cp.start()
        for w in range(n):
            dst = os_[w].at[1 - mc]
            _rcopy(dst, dst, ssem.at[w], rsem.at[w], sib).wait_recv()
        for cp in sends:
            cp.wait_send()

    dma = pltpu.SemaphoreType.DMA
    return _multi_call(name, body, bs, [jax.ShapeDtypeStruct(b.shape, b.dtype) for b in bs], [dma((n,)), dma((n,))],
                       in_place=True)


def comm_sibling_swap(name, x):
    def body(x_ref, o_ref, ssem, rsem):
        mx, my, mc = _pos()
        cp = _rcopy(x_ref, o_ref, ssem, rsem, (mx, my, 1 - mc))
        cp.start()
        cp.wait_recv()
        cp.wait_send()

    return pl.pallas_call(
        body, out_shape=jax.ShapeDtypeStruct(x.shape, x.dtype), in_specs=[_ANY], out_specs=_ANY,
        scratch_shapes=[pltpu.SemaphoreType.DMA, pltpu.SemaphoreType.DMA], name=name)(x)


def _row_tile(r):
    return _pick(r, (256, 128, 64, 32, 16, 8))


def add_own_half(name, g, recv, out_dtype):
    r, c = g.shape[2:]
    tr = _row_tile(r)
    mc = lax.axis_index("c").astype(jnp.int32).reshape(1)

    def body(c_ref, g_ref, r_ref, o_ref):
        o_ref[...] = (g_ref[...] + r_ref[...]).astype(o_ref.dtype)

    grid_spec = pltpu.PrefetchScalarGridSpec(
        num_scalar_prefetch=1, grid=(N_CHIPS, r // tr),
        in_specs=[pl.BlockSpec((None, None, tr, c), lambda j, i, cr: (j, cr[0], i, 0)),
                  pl.BlockSpec((None, tr, c), lambda j, i, cr: (j, i, 0))],
        out_specs=pl.BlockSpec((None, tr, c), lambda j, i, cr: (j, i, 0)))
    return pl.pallas_call(body, grid_spec=grid_spec, out_shape=jax.ShapeDtypeStruct(recv.shape, out_dtype),
                          compiler_params=_cparams("parallel", "parallel"), name=name)(mc, g, recv)


def cast_into_slot(name, w, chip, dtype):
    r, c = w.shape
    tr = _row_tile(r)

    def body(c_ref, w_ref, o_ref):
        o_ref[...] = w_ref[...].astype(o_ref.dtype)

    grid_spec = pltpu.PrefetchScalarGridSpec(
        num_scalar_prefetch=1, grid=(r // tr,),
        in_specs=[pl.BlockSpec((tr, c), lambda i, cr: (i, 0))],
        out_specs=pl.BlockSpec((None, tr, c), lambda i, cr: (cr[0], i, 0)))
    return pl.pallas_call(body, grid_spec=grid_spec, out_shape=jax.ShapeDtypeStruct((N_CHIPS, r, c), dtype),
                          compiler_params=_cparams("parallel"), name=name)(chip.astype(jnp.int32).reshape(1), w)


def sum_chips_into_half(name, own, arrived, chip, mc):
    r, c = own.shape[1:]
    tr = _row_tile(r)

    def body(s0, s1, s2, s3, s4, own_ref, a_ref, b_ref, d_ref, o_ref):
        o_ref[...] = ((own_ref[...].astype(f32) + a_ref[...].astype(f32))
                      + (b_ref[...].astype(f32) + d_ref[...].astype(f32)))

    slot = lambda k: pl.BlockSpec((None, tr, c), lambda i, *sc, _k=k: (sc[_k][0], i, 0))
    grid_spec = pltpu.PrefetchScalarGridSpec(
        num_scalar_prefetch=5, grid=(r // tr,), in_specs=[slot(0), slot(1), slot(2), slot(3)],
        out_specs=pl.BlockSpec((None, tr, c), lambda i, *sc: (sc[4][0], i, 0)))
    mx, my = lax.axis_index("x"), lax.axis_index("y")
    scal = [v.astype(jnp.int32).reshape(1) for v in
            (2 * mx + my, 2 * (1 - mx) + my, 2 * mx + (1 - my), 2 * (1 - mx) + (1 - my), mc)]
    return pl.pallas_call(body, grid_spec=grid_spec, out_shape=jax.ShapeDtypeStruct((2, r, c), f32),
                          compiler_params=_cparams("parallel"), name=name)(*scal, own, arrived, arrived, arrived)


def sum_slots(name, x):
    r, c = x.shape[1:]
    tr = _row_tile(r)

    def body(x_ref, o_ref):
        o_ref[...] = (x_ref[0].astype(f32) + x_ref[1].astype(f32)) + (x_ref[2].astype(f32) + x_ref[3].astype(f32))

    return pl.pallas_call(
        body, grid=(r // tr,), in_specs=[pl.BlockSpec((N_CHIPS, tr, c), lambda i: (0, i, 0))],
        out_specs=pl.BlockSpec((tr, c), lambda i: (i, 0)), out_shape=jax.ShapeDtypeStruct((r, c), f32),
        compiler_params=_cparams("parallel"), name=name)(x)


def adamw(name, w, g, m, v):
    r, c = w.shape
    tr = _row_tile(r)

    def body(w_ref, g_ref, m_ref, v_ref, d_ref, nm_ref, nv_ref):
        gv = g_ref[...]
        nm = ADAM_B1 * m_ref[...] + (1.0 - ADAM_B1) * gv
        nv = ADAM_B2 * v_ref[...] + (1.0 - ADAM_B2) * (gv * gv)
        m_hat = nm / (1.0 - ADAM_B1 ** ADAM_STEP)
        v_hat = nv / (1.0 - ADAM_B2 ** ADAM_STEP)
        d_ref[...] = -ADAM_LR * (m_hat / (jnp.sqrt(v_hat) + ADAM_EPS) + ADAM_WD * w_ref[...])
        nm_ref[...] = nm
        nv_ref[...] = nv

    blk = pl.BlockSpec((tr, c), lambda i: (i, 0))
    return pl.pallas_call(
        body, grid=(r // tr,), in_specs=[blk] * 4, out_specs=[blk] * 3,
        out_shape=[jax.ShapeDtypeStruct((r, c), f32)] * 3,
        compiler_params=_cparams("parallel"), name=name)(w, g, m, v)


def _pack_rows(n):
    return -(-n // PACK_COLS)


def _pack(arrs, dtype, row_multiple):
    segs = []
    for a in arrs:
        flat = a.astype(dtype).reshape(-1)
        k = _pack_rows(flat.shape[0])
        segs.append(jnp.pad(flat, (0, k * PACK_COLS - flat.shape[0])).reshape(k, PACK_COLS))
    rows = sum(s.shape[0] for s in segs)
    pad = -rows % row_multiple
    if pad:
        segs.append(jnp.zeros((pad, PACK_COLS), dtype))
    return jnp.concatenate(segs, 0)


def _unpack(packed, shapes):
    out, r = [], 0
    for shp in shapes:
        n = math.prod(shp)
        k = _pack_rows(n)
        out.append(packed[r:r + k].reshape(-1)[:n].reshape(shp))
        r += k
    return out


def _gathered_to_full(g, axis):
    t = jnp.moveaxis(g, 0, axis)
    return t.reshape(t.shape[:axis] + (t.shape[axis] * t.shape[axis + 1],) + t.shape[axis + 2:])


def _full_to_shard_major(full, axis):
    shp = full.shape
    t = full.reshape(shp[:axis] + (N_CHIPS, shp[axis] // N_CHIPS) + shp[axis + 1:])
    return jnp.moveaxis(t, axis, 0)


def kernel(x, mem, positions, hyb_w_in, dn_conv_w, dn_a_log, dn_dt_bias, dn_norm_g, hyb_w_out, s5_a_re, s5_a_im, s5_log_dt, s5_b_re, s5_b_im, s5_c_re, s5_c_im, s5_d, s5_glu_wo, s5_glu_wg, ln_mix_g, ln_mix_b, xq_w, xk_w, xv_w, xo_w, ln_x_g, ln_x_b, ffn_wg, ffn_wu, ffn_wd, ln_ffn_g, ln_ffn_b, loss_target, m_hyb_w_in, m_dn_conv_w, m_dn_a_log, m_dn_dt_bias, m_dn_norm_g, m_hyb_w_out, m_s5_a_re, m_s5_a_im, m_s5_log_dt, m_s5_b_re, m_s5_b_im, m_s5_c_re, m_s5_c_im, m_s5_d, m_s5_glu_wo, m_s5_glu_wg, m_ln_mix_g, m_ln_mix_b, m_xq_w, m_xk_w, m_xv_w, m_xo_w, m_ln_x_g, m_ln_x_b, m_ffn_wg, m_ffn_wu, m_ffn_wd, m_ln_ffn_g, m_ln_ffn_b, v_hyb_w_in, v_dn_conv_w, v_dn_a_log, v_dn_dt_bias, v_dn_norm_g, v_hyb_w_out, v_s5_a_re, v_s5_a_im, v_s5_log_dt, v_s5_b_re, v_s5_b_im, v_s5_c_re, v_s5_c_im, v_s5_d, v_s5_glu_wo, v_s5_glu_wg, v_ln_mix_g, v_ln_mix_b, v_xq_w, v_xk_w, v_xv_w, v_xo_w, v_ln_x_g, v_ln_x_b, v_ffn_wg, v_ffn_wu, v_ffn_wd, v_ln_ffn_g, v_ln_ffn_b):
    a = dict(locals())
    big = [n for n in WEIGHT_NAMES if n in SHARD_AXIS and n not in GATHER_F32]
    small = [n for n in WEIGHT_NAMES if n not in big]
    chip = 2 * lax.axis_index("x") + lax.axis_index("y")
    for n in FFN_TRANSPOSED:
        for pre in ("", "m_", "v_"):
            a[pre + n] = jnp.swapaxes(a[pre + n], 1, 2)

    mc = lax.axis_index("c")
    view2 = lambda t: t.reshape(-1, t.shape[-1])
    slots = [cast_into_slot("slot_" + n, view2(a[n]), chip, bf16).reshape((N_CHIPS,) + a[n].shape) for n in big]
    gathered = comm_gather_weights("gather_w", slots)
    tiny4 = _unpack_slots(comm_allgather4("gather_w_tiny", _pack([a[n] for n in GATHER_F32], f32, 8)),
                          [a[n].shape for n in GATHER_F32])
    p = {n: a[n] for n in small if n not in GATHER_F32}
    for n, g4 in zip(GATHER_F32, tiny4):
        p[n] = _gathered_to_full(g4, SHARD_AXIS[n])
    for n, g4 in zip(big, gathered):
        p[n] = g4 if n in SHARD_ORDER_GRADS else _gathered_to_full(g4, SHARD_AXIS[n])

    loss, grad_x, grads = local_step(x[0], mem[0], positions, loss_target[0], p)
    loss = lax.psum(loss, ("x", "y", "c"))

    g4s = [grads[n] for n in big]
    recv = comm_sibling_halves("rs_sibling_halves", g4s)
    pairs = []
    for n, g4, r4 in zip(big, g4s, recv):
        lh, cols = g4.shape[1] // 2, g4.shape[-1]
        v4 = g4.reshape(N_CHIPS, 2, -1, cols)
        pairs.append(add_own_half("rs_add_" + n, v4, r4.reshape(N_CHIPS, -1, cols), bf16).reshape((N_CHIPS, lh) + g4.shape[2:]))
    arrived = comm_alltoall4("rs_alltoall", pairs)
    slot3 = lambda t: t.reshape(N_CHIPS, -1, t.shape[-1])
    halves = [sum_chips_into_half("rs_sum_" + n, slot3(pr), slot3(ar), chip, mc) for n, pr, ar in zip(big, pairs, arrived)]
    g_big = {n: t.reshape(a[n].shape) for n, t in zip(big, comm_sibling_join("rs_sibling_join", halves))}

    rpack = _pack([grads[n] for n in small], f32, 8)
    rpair = _add2("ar_add_sibling", rpack, comm_sibling_swap("ar_sibling_swap", rpack))
    g_small = _unpack(sum_slots("ar_sum_chips", comm_allgather4("ar_allgather", rpair)), [grads[n].shape for n in small])
    g_small = {n: (lax.dynamic_index_in_dim(_full_to_shard_major(g, SHARD_AXIS[n]), chip, 0, keepdims=False)
                   if n in SHARD_AXIS else g) for n, g in zip(small, g_small)}

    outs = {}
    for n in big:
        view = lambda t: t.reshape(-1, t.shape[-1])
        d, nm, nv = adamw("adamw_" + n, view(a[n]), view(g_big[n]), view(a["m_" + n]), view(a["v_" + n]))
        outs[n] = (g_big[n],) + tuple(t.reshape(a[n].shape) for t in (d, nm, nv))
    shapes = [a[n].shape for n in small]
    packs = [_pack([a[pre + n] for n in small], f32, 8) for pre in ("", "m_", "v_")]
    upd = adamw("adamw_small", packs[0], _pack([g_small[n] for n in small], f32, 8), packs[1], packs[2])
    for k, n in enumerate(small):
        outs[n] = (g_small[n],) + tuple(_unpack(buf, shapes)[k] for buf in upd)
    for n in FFN_TRANSPOSED:
        outs[n] = tuple(jnp.swapaxes(t, 1, 2) for t in outs[n])
    res = [loss, grad_x[None]]
    for kind in range(4):
        res += [outs[n][kind] for n in WEIGHT_NAMES]
    return tuple(res)


def _unpack_slots(gathered, shapes):
    out, r = [], 0
    for shp in shapes:
        n = math.prod(shp)
        k = _pack_rows(n)
        out.append(gathered[:, r:r + k].reshape(N_CHIPS, -1)[:, :n].reshape((N_CHIPS,) + tuple(shp)))
        r += k
    return out
```

```python
import functools
import math

import jax
import jax.numpy as jnp
from jax import lax
from jax.experimental import pallas as pl
from jax.experimental.pallas import tpu as pltpu

f32 = jnp.float32
bf16 = jnp.bfloat16

D_MODEL = 1024
DEPTH = 4
DN_HEADS = 4
DN_HEAD_DIM = 128
DN_KEY_DIM = 512
DN_QKV_DIM = 1536
DN_CONV = 4
SW_HEADS = 8
SW_HEAD_DIM = 64
SW_DIM = 512
SW_DILATIONS = (1, 4, 16)
SW_BLOCK = 128
ROPE_THETA = 10000.0
S5_GROUP = 16
S5_GROUPS = 64
S5_STATE = 64
X_HEADS = 4
X_HEAD_DIM = 256
FFN_HIDDEN = 2816
ALPHA = (2 * DEPTH) ** 0.25
LN_EPS = 1e-5
RMS_EPS = 1e-6
ADAM_LR, ADAM_B1, ADAM_B2, ADAM_EPS, ADAM_WD, ADAM_STEP = 0.001, 0.9, 0.999, 1e-08, 0.01, 10

BA_PAD = 256
PROJ_COLS = DN_QKV_DIM + DN_KEY_DIM + 3 * SW_DIM + BA_PAD
COL_Z = DN_QKV_DIM
COL_SWQ = COL_Z + DN_KEY_DIM
COL_SWK = COL_SWQ + SW_DIM
COL_SWV = COL_SWK + SW_DIM
COL_BA = COL_SWV + SW_DIM

LANES = 128
SUBLANES = 8
VMEM_LIMIT = 56 * 1024 * 1024
DN_CHUNK = 128
DN_HEADS_PER_STEP = 4


def _cparams(*sem):
    return pltpu.CompilerParams(dimension_semantics=tuple(sem), vmem_limit_bytes=VMEM_LIMIT)


def _dg(x, y, cx, cy):
    return lax.dot_general(x, y, (((cx,), (cy,)), ((), ())), preferred_element_type=f32)


@functools.partial(jax.custom_vjp, nondiff_argnums=(2, 3))
def bdot(a, b, ca, cb):
    return _dg(a.astype(bf16), b.astype(bf16), ca, cb)


def _bdot_fwd(a, b, ca, cb):
    return bdot(a, b, ca, cb), (a, b)


def _bdot_bwd(ca, cb, res, g):
    a, b = res
    g16, a16, b16 = g.astype(bf16), a.astype(bf16), b.astype(bf16)
    da = _dg(g16, b16, 1, 1 - cb) if ca == 1 else _dg(b16, g16, 1 - cb, 1)
    db = _dg(a16, g16, 1 - ca, 0) if cb == 0 else _dg(g16, a16, 0, 1 - ca)
    return da.astype(a.dtype), db.astype(b.dtype)


bdot.defvjp(_bdot_fwd, _bdot_bwd)


def _split_hi_lo(a):
    hi = a.astype(bf16)
    return hi, (a - hi.astype(f32)).astype(bf16)


def _dot3(a, b, ca, cb):
    a_hi, a_lo = _split_hi_lo(a)
    b_hi, b_lo = _split_hi_lo(b)
    return _dg(a_hi, b_hi, ca, cb) + (_dg(a_hi, b_lo, ca, cb) + _dg(a_lo, b_hi, ca, cb))


def hdot(a, b):
    return jnp.dot(a, b, precision=lax.Precision.HIGHEST, preferred_element_type=f32)


def _iota2(shape, dim):
    return lax.broadcasted_iota(jnp.int32, shape, dim)


def _row_spec(r, tm):
    if isinstance(r, tuple):
        arr, width, blk = r
        return arr, pl.BlockSpec((tm, width), lambda i, _b=blk: (i, _b))
    return r, pl.BlockSpec((tm, r.shape[1]), lambda i: (i, 0))


def _par_spec(p):
    return pl.BlockSpec(p.shape, lambda i, _n=p.ndim: (0,) * _n)


def rowmap(name, fn, rows, params, out_cols, tm, out_dtypes=None):
    arrs, specs = zip(*[_row_spec(r, tm) for r in rows])
    s = arrs[0].shape[0]
    n_in = len(rows) + len(params)
    out_dtypes = out_dtypes or [f32] * len(out_cols)

    def body(*refs):
        outs = fn(*[r[...] for r in refs[:n_in]])
        for o_ref, o in zip(refs[n_in:], outs):
            o_ref[...] = o.astype(o_ref.dtype)

    return pl.pallas_call(
        body, grid=(s // tm,),
        in_specs=list(specs) + [_par_spec(p) for p in params],
        out_specs=[pl.BlockSpec((tm, c), lambda i: (i, 0)) for c in out_cols],
        out_shape=[jax.ShapeDtypeStruct((s, c), dt) for c, dt in zip(out_cols, out_dtypes)],
        compiler_params=_cparams("parallel"), name=name)(*arrs, *params)


def rowmap_bwd(name, fn, rows, params, cts, tm, row_mask=None, par_mask=None, row_dtypes=None):
    arrs, specs = zip(*[_row_spec(r, tm) for r in rows])
    s = arrs[0].shape[0]
    ct_groups = [c if isinstance(c, list) else [c] for c in cts]
    ct_arrs, ct_specs = zip(*[_row_spec(a, tm) for grp in ct_groups for a in grp])
    cts = list(ct_arrs)
    nr, npar, nct = len(rows), len(params), len(cts)
    row_mask = row_mask or [True] * nr
    par_mask = par_mask or [True] * npar
    row_idx = [k for k in range(nr) if row_mask[k]]
    par_idx = [k for k in range(npar) if par_mask[k]]
    row_w = [specs[k].block_shape[1] for k in row_idx]

    def body(*refs):
        ins = [r[...] for r in refs[:nr + npar]]
        ct_refs = list(refs[nr + npar:nr + npar + nct])
        ctv = []
        for grp in ct_groups:
            acc = ct_refs.pop(0)[...].astype(f32)
            for _ in grp[1:]:
                acc = acc + ct_refs.pop(0)[...].astype(f32)
            ctv.append(acc)
        ctv = tuple(ctv)
        outs = refs[nr + npar + nct:]
        _, vjp = jax.vjp(fn, *ins)
        grads = vjp(ctv)
        for o_ref, k in zip(outs[:len(row_idx)], row_idx):
            o_ref[...] = grads[k].astype(o_ref.dtype)
        first = pl.program_id(0) == 0
        for o_ref, k in zip(outs[len(row_idx):], par_idx):
            g = grads[nr + k].astype(f32)

            @pl.when(first)
            def _(o_ref=o_ref, g=g):
                o_ref[...] = g

            @pl.when(jnp.logical_not(first))
            def _(o_ref=o_ref, g=g):
                o_ref[...] += g

    res = pl.pallas_call(
        body, grid=(s // tm,),
        in_specs=list(specs) + [_par_spec(p) for p in params]
        + list(ct_specs),
        out_specs=[pl.BlockSpec((tm, w), lambda i: (i, 0)) for w in row_w]
        + [_par_spec(params[k]) for k in par_idx],
        out_shape=[jax.ShapeDtypeStruct((s, w), dt) for w, dt in zip(row_w, row_dtypes or [f32] * len(row_w))]
        + [jax.ShapeDtypeStruct(params[k].shape, f32) for k in par_idx],
        compiler_params=_cparams("arbitrary"), name=name)(*arrs, *params, *cts)
    return list(res[:len(row_idx)]), list(res[len(row_idx):])


def _pick(n, prefs):
    for t in prefs:
        if n % t == 0:
            return t
    return n


MM_CHUNK = 512
MM_WIDE = 2048


def mm_nn(name, a, b, out_dtype=f32, postnorm=None):
    m, k = a.shape
    n = b.shape[1]
    tm = _pick(m, ((1024,) if n <= MM_WIDE else ()) + (512, 256, 128))
    cn = _pick(n, (MM_CHUNK, 256, 128))

    if postnorm is not None:
        h, g, beta = postnorm
        tm = min(tm, 512)

        def body_pn(a_ref, b_ref, h_ref, g_ref, be_ref, o_ref, y_ref, y16_ref):
            av = a_ref[...].astype(bf16)
            for c in range(n // cn):
                sl = slice(c * cn, (c + 1) * cn)
                o_ref[:, sl] = _dg(av, b_ref[:, sl].astype(bf16), 1, 0)
            y = _postnorm_tile(h_ref[...], o_ref[...], g_ref[...], be_ref[...])[0]
            y_ref[...] = y
            y16_ref[...] = y.astype(bf16)

        row = pl.BlockSpec((tm, n), lambda i: (i, 0))
        vec = pl.BlockSpec((1, n), lambda i: (0, 0))
        return pl.pallas_call(
            body_pn, grid=(m // tm,),
            in_specs=[pl.BlockSpec((tm, k), lambda i: (i, 0)), pl.BlockSpec((k, n), lambda i: (0, 0)), row, vec, vec],
            out_specs=[row, row, row],
            out_shape=[jax.ShapeDtypeStruct((m, n), f32), jax.ShapeDtypeStruct((m, n), f32), jax.ShapeDtypeStruct((m, n), bf16)],
            compiler_params=_cparams("parallel"), name=name)(a, b, h, g, beta)

    def body(a_ref, b_ref, o_ref):
        av = a_ref[...].astype(bf16)
        for c in range(n // cn):
            sl = slice(c * cn, (c + 1) * cn)
            o_ref[:, sl] = _dg(av, b_ref[:, sl].astype(bf16), 1, 0).astype(o_ref.dtype)

    return pl.pallas_call(
        body, grid=(m // tm,),
        in_specs=[pl.BlockSpec((tm, k), lambda i: (i, 0)), pl.BlockSpec((k, n), lambda i: (0, 0))],
        out_specs=pl.BlockSpec((tm, n), lambda i: (i, 0)),
        out_shape=jax.ShapeDtypeStruct((m, n), out_dtype),
        compiler_params=_cparams("parallel"), name=name)(a, b)


def mm_nt(name, a, b, out_dtype=f32):
    m, n = a.shape
    k = b.shape[0]
    tm = _pick(m, ((1024,) if n <= MM_WIDE else ()) + (512, 256, 128))
    ck = _pick(k, (MM_CHUNK, 256, 128))

    def body(a_ref, b_ref, o_ref):
        av = a_ref[...].astype(bf16)
        for c in range(k // ck):
            sl = slice(c * ck, (c + 1) * ck)
            o_ref[:, sl] = _dg(av, b_ref[sl, :].astype(bf16), 1, 1).astype(o_ref.dtype)

    return pl.pallas_call(
        body, grid=(m // tm,),
        in_specs=[pl.BlockSpec((tm, n), lambda i: (i, 0)), pl.BlockSpec((k, n), lambda i: (0, 0))],
        out_specs=pl.BlockSpec((tm, k), lambda i: (i, 0)),
        out_shape=jax.ShapeDtypeStruct((m, k), out_dtype),
        compiler_params=_cparams("parallel"), name=name)(a, b)


def _stacked(buf):
    if buf is None:
        return [], [], {}
    return [buf], [pl.BlockSpec(memory_space=pl.ANY)], None


def mm_tn(name, a, b, out_dtype=f32, b_col0=0, n_cols=None, stack=None):
    s, m = a.shape
    n = n_cols or b.shape[1]
    tn = _pick(n, (256, 128))
    cm = _pick(m, (256, 128))
    col0 = b_col0 // tn
    in_specs = [pl.BlockSpec((s, m), lambda j: (0, 0)), pl.BlockSpec((s, tn), lambda j: (0, j + col0))]

    if stack is None:
        def body(a_ref, b_ref, o_ref):
            bv = b_ref[...].astype(bf16)
            for c in range(m // cm):
                sl = slice(c * cm, (c + 1) * cm)
                o_ref[sl, :] = _dg(a_ref[:, sl].astype(bf16), bv, 0, 0).astype(o_ref.dtype)

        return pl.pallas_call(
            body, grid=(n // tn,), in_specs=in_specs, out_specs=pl.BlockSpec((m, tn), lambda j: (0, j)),
            out_shape=jax.ShapeDtypeStruct((m, n), out_dtype),
            compiler_params=_cparams("parallel"), name=name)(a, b)

    buf, layer, n_layers = stack
    assert cm * N_CHIPS == m
    extra, extra_specs, _ = _stacked(buf)

    def body_stacked(a_ref, b_ref, *rest):
        o_ref = rest[-1]
        bv = b_ref[...].astype(bf16)
        for c in range(N_CHIPS):
            o_ref[c] = _dg(a_ref[:, c * cm:(c + 1) * cm].astype(bf16), bv, 0, 0).astype(o_ref.dtype)

    return pl.pallas_call(
        body_stacked, grid=(n // tn,), in_specs=in_specs + extra_specs,
        out_specs=pl.BlockSpec((N_CHIPS, None, cm, tn), lambda j: (0, layer, 0, j)),
        out_shape=jax.ShapeDtypeStruct((N_CHIPS, n_layers, cm, n), out_dtype),
        input_output_aliases={2: 0} if extra else {},
        compiler_params=_cparams("parallel"), name=name)(a, b, *extra)


def _postnorm_tile(h, sub, g, b):
    z = ALPHA * h + sub
    mu = jnp.mean(z, -1, keepdims=True)
    zc = z - mu
    var = jnp.mean(zc * zc, -1, keepdims=True)
    return (zc * lax.rsqrt(var + LN_EPS) * g + b,)


def _glu_tile(og):
    o, g = og[:, :D_MODEL], og[:, D_MODEL:]
    return (o * jax.nn.sigmoid(g),)


def _xattn_tile(q, kv):
    outs = []
    for h in range(X_HEADS):
        sl = slice(h * X_HEAD_DIM, (h + 1) * X_HEAD_DIM)
        s = bdot(q[:, sl], kv[:, sl], 1, 1) * (X_HEAD_DIM ** -0.5)
        m = lax.stop_gradient(jnp.max(s, -1, keepdims=True))
        p = jnp.exp(s - m)
        p = p / jnp.sum(p, -1, keepdims=True)
        outs.append(bdot(p, kv[:, D_MODEL + h * X_HEAD_DIM:D_MODEL + (h + 1) * X_HEAD_DIM], 1, 0))
    return (jnp.concatenate(outs, -1),)


TM_ROW = 512


def postnorm_fwd(tag, h, sub, g, b):
    return rowmap("postnorm_" + tag, lambda *a: _postnorm_tile(*a) * 2, [h, sub], [g, b], [D_MODEL] * 2, TM_ROW,
                  out_dtypes=[f32, bf16])


def postnorm_bwd(tag, h, sub, g, b, dy):
    (dh, dsub), (dg, db) = rowmap_bwd("postnorm_bwd_" + tag, _postnorm_tile, [h, sub], [g, b], [dy], TM_ROW,
                                      row_dtypes=[f32, bf16])
    return dh, dsub, dg, db


def xattn_fwd(tag, h, mem, wq, wkv, wo, postnorm):
    q = mm_nn("xq_" + tag, h, wq, out_dtype=bf16)
    kv = mm_nn("xkv_" + tag, mem, wkv)
    ao = rowmap("xattn_" + tag, _xattn_tile, [q], [kv], [D_MODEL], TM_ROW, out_dtypes=[bf16])[0]
    outs = mm_nn("xo_" + tag, ao, wo, postnorm=postnorm)
    return outs, (q, kv, ao)


def xattn_bwd(tag, layer, h, mem, wq, wkv, wo, res, dout, stacks):
    q, kv, ao = res
    sq, sk, sv, so = stacks
    so = mm_tn("xo_dw_" + tag, ao, dout, stack=(so, layer, DEPTH))
    dao = mm_nt("xo_dx_" + tag, dout, wo)
    (dq,), (dkv,) = rowmap_bwd("xattn_bwd_" + tag, _xattn_tile, [q], [kv], [dao], TM_ROW, row_dtypes=[bf16])
    sq = mm_tn("xq_dw_" + tag, h, dq, stack=(sq, layer, DEPTH))
    dh = mm_nt("xq_dx_" + tag, dq, wq)
    sk = mm_tn("xk_dw_" + tag, mem, dkv, n_cols=D_MODEL, stack=(sk, layer, DEPTH))
    sv = mm_tn("xv_dw_" + tag, mem, dkv, b_col0=D_MODEL, n_cols=D_MODEL, stack=(sv, layer, DEPTH))
    return dh, (sq, sk, sv, so)


FFN_SHARD = FFN_HIDDEN // 4
TM_FFN = 512
TM_FFN_WIDE = 1024


def _silu_mul(a, u):
    return jax.nn.silu(a) * u


def ffn_fwd(tag, layer, h, wg, wu, wd, postnorm):
    s = h.shape[0]
    tm, fs = TM_FFN, FFN_SHARD
    w_in = pl.BlockSpec((None, None, fs, D_MODEL), lambda k, i: (k, layer, 0, 0))
    tu = TM_FFN_WIDE
    act = pl.BlockSpec((None, tu, fs), lambda k, i: (k, i, 0))

    def up_body(h_ref, wg_ref, wu_ref, a_ref, u_ref, hid_ref):
        hv = h_ref[...].astype(bf16)
        a, u = _dg(hv, wg_ref[...], 1, 1), _dg(hv, wu_ref[...], 1, 1)
        a_ref[...], u_ref[...] = a.astype(bf16), u.astype(bf16)
        hid_ref[...] = _silu_mul(a, u).astype(bf16)

    a4, u4, hid4 = pl.pallas_call(
        up_body, grid=(4, s // tu),
        in_specs=[pl.BlockSpec((tu, D_MODEL), lambda k, i: (i, 0)), w_in, w_in],
        out_specs=[act, act, act],
        out_shape=[jax.ShapeDtypeStruct((4, s, fs), bf16)] * 3,
        compiler_params=_cparams("parallel", "parallel"), name="ffn_up_" + tag)(h, wg, wu)

    all_act = pl.BlockSpec((4, tm, fs), lambda i: (0, i, 0))
    all_w = pl.BlockSpec((4, None, fs, D_MODEL), lambda i: (0, layer, 0, 0))

    def down_body(hid_ref, wd_ref, h_ref, g_ref, be_ref, o_ref, y_ref, y16_ref):
        acc = _dg(hid_ref[0], wd_ref[0], 1, 0)
        for k in range(1, 4):
            acc = acc + _dg(hid_ref[k], wd_ref[k], 1, 0)
        o_ref[...] = acc
        y = _postnorm_tile(h_ref[...], acc, g_ref[...], be_ref[...])[0]
        y_ref[...] = y
        y16_ref[...] = y.astype(bf16)

    h32, g, beta = postnorm
    row = pl.BlockSpec((tm, D_MODEL), lambda i: (i, 0))
    vec = pl.BlockSpec((1, D_MODEL), lambda i: (0, 0))
    outs = pl.pallas_call(
        down_body, grid=(s // tm,), in_specs=[all_act, all_w, row, vec, vec],
        out_specs=[row, row, row],
        out_shape=[jax.ShapeDtypeStruct((s, D_MODEL), f32)] * 2 + [jax.ShapeDtypeStruct((s, D_MODEL), bf16)],
        compiler_params=_cparams("parallel"), name="ffn_down_" + tag)(hid4, wd, h32, g, beta)
    return outs, (a4, u4, hid4)


def ffn_bwd(tag, layer, h, wg, wu, wd, res, dout, stacks=None):
    a4, u4, hid4 = res
    s = h.shape[0]
    tm, fs = TM_FFN, FFN_SHARD
    tu = TM_FFN_WIDE
    act = pl.BlockSpec((None, tu, fs), lambda k, i: (k, i, 0))

    def dact_body(do_ref, wd_ref, a_ref, u_ref, da_ref, du_ref):
        dhid = _dg(do_ref[...].astype(bf16), wd_ref[...], 1, 1)
        _, vjp = jax.vjp(_silu_mul, a_ref[...].astype(f32), u_ref[...].astype(f32))
        da, du = vjp(dhid)
        da_ref[...], du_ref[...] = da.astype(bf16), du.astype(bf16)

    da4, du4 = pl.pallas_call(
        dact_body, grid=(4, s // tu),
        in_specs=[pl.BlockSpec((tu, D_MODEL), lambda k, i: (i, 0)),
                  pl.BlockSpec((None, None, fs, D_MODEL), lambda k, i: (k, layer, 0, 0)), act, act],
        out_specs=[act, act], out_shape=[jax.ShapeDtypeStruct((4, s, fs), bf16)] * 2,
        compiler_params=_cparams("parallel", "parallel"), name="ffn_dact_" + tag)(dout, wd, a4, u4)

    all_act = pl.BlockSpec((4, tm, fs), lambda i: (0, i, 0))
    all_w = pl.BlockSpec((4, None, fs, D_MODEL), lambda i: (0, layer, 0, 0))

    def dx_body(da_ref, du_ref, wg_ref, wu_ref, o_ref):
        acc = _dg(da_ref[0], wg_ref[0], 1, 0) + _dg(du_ref[0], wu_ref[0], 1, 0)
        for k in range(1, 4):
            acc = acc + (_dg(da_ref[k], wg_ref[k], 1, 0) + _dg(du_ref[k], wu_ref[k], 1, 0))
        o_ref[...] = acc

    dh = pl.pallas_call(
        dx_body, grid=(s // tm,), in_specs=[all_act, all_act, all_w, all_w],
        out_specs=pl.BlockSpec((tm, D_MODEL), lambda i: (i, 0)),
        out_shape=jax.ShapeDtypeStruct((s, D_MODEL), f32),
        compiler_params=_cparams("parallel"), name="ffn_dx_" + tag)(da4, du4, wg, wu)

    tn = 256
    whole = pl.BlockSpec((None, s, fs), lambda k: (k, 0, 0))
    resident = pl.BlockSpec((s, D_MODEL), lambda k: (0, 0), pipeline_mode=pl.Buffered(1))

    def dwin_body(h_ref, da_ref, du_ref, *rest):
        dwg_ref, dwu_ref = rest[-2:]
        da, du = da_ref[...], du_ref[...]
        for c in range(D_MODEL // tn):
            sl = slice(c * tn, (c + 1) * tn)
            hv = h_ref[:, sl].astype(bf16)
            dwg_ref[:, sl] = _dg(da, hv, 0, 0)
            dwu_ref[:, sl] = _dg(du, hv, 0, 0)

    n_layers = wd.shape[1]
    layer_out = pl.BlockSpec((None, None, fs, D_MODEL), lambda k: (k, layer, 0, 0))
    stack_shape = jax.ShapeDtypeStruct((4, n_layers, fs, D_MODEL), f32)
    prev = [] if stacks is None else list(stacks)
    any_spec = [pl.BlockSpec(memory_space=pl.ANY)]

    dwg, dwu = pl.pallas_call(
        dwin_body, grid=(4,),
        in_specs=[resident, whole, whole] + any_spec * len(prev[:2]),
        out_specs=[layer_out] * 2, out_shape=[stack_shape] * 2,
        input_output_aliases={3: 0, 4: 1} if prev else {},
        compiler_params=_cparams("parallel"), name="ffn_dwin_" + tag)(h, da4, du4, *prev[:2])

    def dwd_body(hid_ref, do_ref, *rest):
        hid = hid_ref[...]
        for c in range(D_MODEL // tn):
            sl = slice(c * tn, (c + 1) * tn)
            rest[-1][:, sl] = _dg(hid, do_ref[:, sl].astype(bf16), 0, 0)

    dwd = pl.pallas_call(
        dwd_body, grid=(4,),
        in_specs=[whole, resident] + any_spec * len(prev[2:]),
        out_specs=layer_out, out_shape=stack_shape,
        input_output_aliases={2: 0} if prev else {},
        compiler_params=_cparams("parallel"), name="ffn_dwd_" + tag)(hid4, dout, *prev[2:])
    return dh, (dwg, dwu, dwd)


def loss_head(y, target):
    s, d = y.shape
    tm = TM_ROW

    def body(y_ref, t_ref, part_ref, dy_ref):
        e = y_ref[...] - t_ref[...]
        dy_ref[...] = e * (1.0 / d)
        p = jnp.sum(e * e, 0, keepdims=True) * (0.5 / d)

        @pl.when(pl.program_id(0) == 0)
        def _():
            part_ref[...] = p

        @pl.when(pl.program_id(0) != 0)
        def _():
            part_ref[...] += p

    return pl.pallas_call(
        body, grid=(s // tm,),
        in_specs=[pl.BlockSpec((tm, d), lambda i: (i, 0))] * 2,
        out_specs=[pl.BlockSpec((1, d), lambda i: (0, 0)), pl.BlockSpec((tm, d), lambda i: (i, 0))],
        out_shape=[jax.ShapeDtypeStruct((1, d), f32), jax.ShapeDtypeStruct((s, d), f32)],
        compiler_params=_cparams("arbitrary"), name="loss_head")(y, target)


TM_CONV = 1024


def _conv_rows(xx, w_ref, n_rows):
    a = w_ref[3:4, :] * xx
    for k in (1, 2, 3):
        a = a + w_ref[3 - k:4 - k, :] * pltpu.roll(xx, k, 0)
    return a


def _dn_act(a, is_qk):
    s = jax.nn.silu(a)
    n = s * lax.rsqrt(jnp.sum(s * s, -1, keepdims=True) + RMS_EPS)
    return jnp.where(is_qk, n, s)


def dn_conv_fwd(tag, proj, cw):
    s = proj.shape[0]
    tm, hb = TM_CONV, TM_CONV // SUBLANES

    def body(xh_ref, x_ref, w_ref, o_ref):
        j, t = pl.program_id(0), pl.program_id(1)
        halo = jnp.where(t > 0, xh_ref[...], 0.0)
        xx = jnp.concatenate([halo, x_ref[...]], 0)
        a = _conv_rows(xx, w_ref, tm + SUBLANES)
        o_ref[...] = _dn_act(a, j < 2 * DN_HEADS)[SUBLANES:, :]

    return pl.pallas_call(
        body, grid=(DN_QKV_DIM // LANES, s // tm),
        in_specs=[pl.BlockSpec((SUBLANES, LANES), lambda j, t: (jnp.maximum(t * hb - 1, 0), j)),
                  pl.BlockSpec((tm, LANES), lambda j, t: (t, j)),
                  pl.BlockSpec((DN_CONV, LANES), lambda j, t: (0, j))],
        out_specs=pl.BlockSpec((tm, LANES), lambda j, t: (t, j)),
        out_shape=jax.ShapeDtypeStruct((s, DN_QKV_DIM), f32),
        compiler_params=_cparams("parallel", "parallel"), name="dn_conv_" + tag)(proj, proj, cw)


def dn_conv_bwd(tag, proj, cw, dy):
    s = proj.shape[0]
    tm, hb = TM_CONV, TM_CONV // SUBLANES
    nt = s // tm
    n_ext = tm + 2 * SUBLANES

    def body(xb_ref, x_ref, xa_ref, dy_ref, dya_ref, w_ref, dx_ref, dw_ref):
        j, t = pl.program_id(0), pl.program_id(1)
        xx = jnp.concatenate([jnp.where(t > 0, xb_ref[...], 0.0), x_ref[...],
                              jnp.where(t < nt - 1, xa_ref[...], 0.0)], 0)
        dyy = jnp.concatenate([jnp.zeros((SUBLANES, LANES), f32), dy_ref[...],
                               jnp.where(t < nt - 1, dya_ref[...], 0.0)], 0)
        a = _conv_rows(xx, w_ref, n_ext)
        _, vjp = jax.vjp(lambda v: _dn_act(v, j < 2 * DN_HEADS), a)
        da, = vjp(dyy)
        dx = w_ref[3:4, :] * da
        for k in (1, 2, 3):
            dx = dx + w_ref[3 - k:4 - k, :] * pltpu.roll(da, n_ext - k, 0)
        dx_ref[...] = dx[SUBLANES:SUBLANES + tm, :]
        row = _iota2((n_ext, LANES), 0)
        da_in = jnp.where((row >= SUBLANES) & (row < SUBLANES + tm), da, 0.0)
        r8 = _iota2((SUBLANES, LANES), 0)
        dw = jnp.zeros((SUBLANES, LANES), f32)
        for k in range(DN_CONV):
            xs = xx if k == 0 else pltpu.roll(xx, k, 0)
            dw = dw + jnp.where(r8 == 3 - k, jnp.sum(da_in * xs, 0, keepdims=True), 0.0)

        @pl.when(t == 0)
        def _():
            dw_ref[...] = dw

        @pl.when(t != 0)
        def _():
            dw_ref[...] += dw

    nb8 = s // SUBLANES
    return pl.pallas_call(
        body, grid=(DN_QKV_DIM // LANES, nt),
        in_specs=[pl.BlockSpec((SUBLANES, LANES), lambda j, t: (jnp.maximum(t * hb - 1, 0), j)),
                  pl.BlockSpec((tm, LANES), lambda j, t: (t, j)),
                  pl.BlockSpec((SUBLANES, LANES), lambda j, t: (jnp.minimum((t + 1) * hb, nb8 - 1), j)),
                  pl.BlockSpec((tm, LANES), lambda j, t: (t, j)),
                  pl.BlockSpec((SUBLANES, LANES), lambda j, t: (jnp.minimum((t + 1) * hb, nb8 - 1), j)),
                  pl.BlockSpec((DN_CONV, LANES), lambda j, t: (0, j))],
        out_specs=[pl.BlockSpec((tm, LANES), lambda j, t: (t, j)),
                   pl.BlockSpec((SUBLANES, LANES), lambda j, t: (0, j))],
        out_shape=[jax.ShapeDtypeStruct((s, DN_QKV_DIM), f32), jax.ShapeDtypeStruct((SUBLANES, DN_QKV_DIM), f32)],
        compiler_params=_cparams("parallel", "arbitrary"), name="dn_conv_bwd_" + tag)(proj, proj, proj, dy, dy, cw)


def _gate_tile(ba, eb, ea, alog, dtb):
    beta = jax.nn.sigmoid(hdot(ba, eb))
    g = -jnp.exp(alog) * jax.nn.softplus(hdot(ba, ea) + dtb)
    return beta, g


def _each(fn, *lists):
    return [fn(*args) for args in zip(*lists)]


@functools.partial(jax.custom_vjp, nondiff_argnums=(1,))
def _halves(x, axis):
    h = x.shape[axis] // 2
    return (x[:h], x[h:]) if axis == 0 else (x[:, :h], x[:, h:])


def _halves_fwd(x, axis):
    return _halves(x, axis), None


def _halves_bwd(axis, _, g):
    return (jnp.concatenate(g, axis),)


_halves.defvjp(_halves_fwd, _halves_bwd)


def _tri_inv_unit(lowers):
    c = lowers[0].shape[0]
    r, col = _iota2((c, c), 0), _iota2((c, c), 1)
    eye = jnp.where(r == col, 1.0, 0.0).astype(f32)
    invs = None
    sh = 0
    while (1 << sh) < c:
        same_2b = lax.shift_right_logical(r, sh + 1) == lax.shift_right_logical(col, sh + 1)
        diff_b = lax.shift_right_logical(r, sh) != lax.shift_right_logical(col, sh)
        offs = [jnp.where(same_2b & diff_b, low, 0.0) for low in lowers]
        if invs is None:
            invs = [eye - off for off in offs]
        else:
            part = _each(lambda inv, off: _dot3(inv, off, 1, 0), invs, offs)
            invs = _each(lambda inv, p: inv - _dot3(p, inv, 1, 0), invs, part)
        sh += 1
    return invs


@jax.custom_vjp
def _known_inverse(lower, tinv):
    return tinv


def _known_inverse_fwd(lower, tinv):
    return tinv, tinv


def _known_inverse_bwd(tinv, g):
    tt = tinv.T
    return -hdot(hdot(tt, g), tt), jnp.zeros_like(tinv)


_known_inverse.defvjp(_known_inverse_fwd, _known_inverse_bwd)


def _delta_chunk(q, k, v, gb, betab, state, tinv_known=None):
    c, hd = DN_CHUNK, DN_HEAD_DIM
    r, col = _iota2((c, c), 0), _iota2((c, c), 1)
    causal, strict = r >= col, r > col
    tril = jnp.where(causal, 1.0, 0.0).astype(f32)
    gc = _each(lambda g: hdot(tril, g), gb)
    decay = _each(lambda g: jnp.where(causal, jnp.exp(jnp.where(causal, g - g.T, 0.0)), 0.0), gc)
    qs = _each(lambda t: t * (DN_HEAD_DIM ** -0.5), q)
    kb = _each(lambda a, b: a * b, k, betab)
    kq = _each(lambda a, b, kk: _halves(bdot(jnp.concatenate([a, b], 0), kk, 1, 1), 0), kb, qs, k)
    lower = _each(lambda x, d: jnp.where(strict, x[0], 0.0) * d, kq, decay)
    intra = _each(lambda x, d: x[1] * d, kq, decay)
    tinv = _tri_inv_unit(lower) if tinv_known is None else _each(_known_inverse, lower, tinv_known)
    eg = _each(jnp.exp, gc)
    uw = _each(lambda t, vv, b, kb_, e: _halves(hdot(t, jnp.concatenate([vv * b, kb_ * e], 1)), 1),
               tinv, v, betab, kb, eg)
    gl = _each(lambda g: jnp.sum(jnp.where(r == c - 1, g, 0.0), 0, keepdims=True), gc)
    k_dec = _each(lambda kk, a, g: kk * jnp.exp(a - g), k, gl, gc)
    ws = _each(lambda x, t, e, st: _halves(bdot(jnp.concatenate([x[1], t * e], 0), st, 1, 0), 0), uw, qs, eg, state)
    v_new = _each(lambda x, y: x[0] - y[0], uw, ws)
    out = _each(lambda y, a, vn: y[1] + bdot(a, vn, 1, 0), ws, intra, v_new)
    new_state = _each(lambda st, a, kd, vn: st * jnp.exp(a) + bdot(kd, vn, 0, 0), state, gl, k_dec, v_new)
    return tuple(out), tuple(new_state), tuple(tinv)


def delta_fwd(tag, qkv, gb, betab):
    s = qkv.shape[0]
    c, hd = DN_CHUNK, DN_HEAD_DIM
    n = s // c

    hg, ng = DN_HEADS_PER_STEP, DN_HEADS // DN_HEADS_PER_STEP

    def body(q_ref, k_ref, v_ref, g_ref, b_ref, o_ref, st_ref, ti_ref, state):
        @pl.when(pl.program_id(1) == 0)
        def _():
            state[...] = jnp.zeros_like(state)

        heads = lambda ref: tuple(ref[:, j * hd:(j + 1) * hd] for j in range(hg))
        st = tuple(state[j] for j in range(hg))
        outs, news, tinv = _delta_chunk(heads(q_ref), heads(k_ref), heads(v_ref), heads(g_ref), heads(b_ref), st)
        for j in range(hg):
            st_ref[j] = st[j]
            ti_ref[j] = tinv[j]
            o_ref[:, j * hd:(j + 1) * hd] = outs[j]
            state[j] = news[j]

    blk = lambda off: pl.BlockSpec((c, hg * hd), lambda h, i, _o=off: (i, h + _o))
    per_chunk = pl.BlockSpec((hg, None, hd, hd), lambda h, i: (h, i, 0, 0))
    return pl.pallas_call(
        body, grid=(ng, n),
        in_specs=[blk(0), blk(ng), blk(2 * ng), blk(0), blk(0)],
        out_specs=[blk(0), per_chunk, per_chunk],
        out_shape=[jax.ShapeDtypeStruct((s, DN_KEY_DIM), f32)] + [jax.ShapeDtypeStruct((DN_HEADS, n, hd, hd), f32)] * 2,
        scratch_shapes=[pltpu.VMEM((hg, hd, hd), f32)],
        compiler_params=_cparams("parallel", "arbitrary"), name="delta_" + tag)(qkv, qkv, qkv, gb, betab)


def delta_bwd(tag, qkv, gb, betab, states, tinvs, do):
    s = qkv.shape[0]
    c, hd = DN_CHUNK, DN_HEAD_DIM
    n = s // c

    hg, ng = DN_HEADS_PER_STEP, DN_HEADS // DN_HEADS_PER_STEP

    def body(q_ref, k_ref, v_ref, g_ref, b_ref, st_ref, ti_ref, do_ref, dqkv_ref, dg_ref, db_ref, dstate):
        @pl.when(pl.program_id(1) == 0)
        def _():
            dstate[...] = jnp.zeros_like(dstate)

        heads = lambda ref: tuple(ref[:, j * hd:(j + 1) * hd] for j in range(hg))
        tinv = tuple(ti_ref[j] for j in range(hg))
        _, vjp = jax.vjp(lambda *args: _delta_chunk(*args, tinv_known=tinv)[:2],
                         heads(q_ref), heads(k_ref), heads(v_ref), heads(g_ref), heads(b_ref),
                         tuple(st_ref[j] for j in range(hg)))
        grads = vjp((heads(do_ref), tuple(dstate[j] for j in range(hg))))
        for part, g in enumerate(grads[:3]):
            for j in range(hg):
                dqkv_ref[:, part * DN_KEY_DIM + j * hd:part * DN_KEY_DIM + (j + 1) * hd] = g[j]
        for ref, g in zip((dg_ref, db_ref), grads[3:5]):
            for j in range(hg):
                ref[:, j * hd:(j + 1) * hd] = g[j]
        for j in range(hg):
            dstate[j] = grads[5][j]

    assert ng == 1
    blk = lambda off: pl.BlockSpec((c, hg * hd), lambda h, i, _o=off: (n - 1 - i, h + _o))
    return pl.pallas_call(
        body, grid=(ng, n),
        in_specs=[blk(0), blk(ng), blk(2 * ng), blk(0), blk(0)]
        + [pl.BlockSpec((hg, None, hd, hd), lambda h, i: (h, n - 1 - i, 0, 0))] * 2 + [blk(0)],
        out_specs=[pl.BlockSpec((c, DN_QKV_DIM), lambda h, i: (n - 1 - i, 0)), blk(0), blk(0)],
        out_shape=[jax.ShapeDtypeStruct((s, DN_QKV_DIM), f32)] + [jax.ShapeDtypeStruct((s, DN_KEY_DIM), f32)] * 2,
        scratch_shapes=[pltpu.VMEM((hg, hd, hd), f32)],
        compiler_params=_cparams("parallel", "arbitrary"),
        name="delta_bwd_" + tag)(qkv, qkv, qkv, gb, betab, states, tinvs, do)


def _dn_out_tile(o, z, ng):
    outs = []
    for h in range(DN_HEADS):
        sl = slice(h * DN_HEAD_DIM, (h + 1) * DN_HEAD_DIM)
        oh = o[:, sl]
        nrm = oh * lax.rsqrt(jnp.mean(oh * oh, -1, keepdims=True) + RMS_EPS) * ng[:, sl]
        outs.append(nrm * jax.nn.silu(z[:, sl]))
    return (jnp.concatenate(outs, -1),)


def _head_selectors():
    r, c = _iota2((BA_PAD, DN_KEY_DIM), 0), _iota2((BA_PAD, DN_KEY_DIM), 1) // DN_HEAD_DIM
    return (r == c).astype(f32), (r == c + DN_HEADS).astype(f32)


def dn_mixer_fwd(tag, proj, cw, alog_b, dtb_b, ng_b):
    eb, ea = _head_selectors()
    ba = (proj, BA_PAD, COL_BA // BA_PAD)
    qkv = dn_conv_fwd(tag, proj, cw)
    betab, gb = rowmap("dn_gate_" + tag, _gate_tile, [ba], [eb, ea, alog_b, dtb_b], [DN_KEY_DIM] * 2, TM_ROW)
    o, states, tinvs = delta_fwd(tag, qkv, gb, betab)
    z = (proj, DN_KEY_DIM, COL_Z // DN_KEY_DIM)
    a_out = rowmap("dn_out_" + tag, _dn_out_tile, [o, z], [ng_b], [DN_KEY_DIM], TM_ROW)[0]
    return a_out, (qkv, betab, gb, o, states, tinvs)


def dn_mixer_bwd(tag, proj, cw, alog_b, dtb_b, ng_b, res, da_out):
    qkv, betab, gb, o, states, tinvs = res
    eb, ea = _head_selectors()
    ba = (proj, BA_PAD, COL_BA // BA_PAD)
    z = (proj, DN_KEY_DIM, COL_Z // DN_KEY_DIM)
    (do, dz), (dng,) = rowmap_bwd("dn_out_bwd_" + tag, _dn_out_tile, [o, z], [ng_b], [da_out], TM_ROW)
    dqkv, dgb, dbetab = delta_bwd(tag, qkv, gb, betab, states, tinvs, do)
    dqkv_raw, dcw = dn_conv_bwd(tag, proj, cw, dqkv)
    (dba,), (dalog, ddtb) = rowmap_bwd("dn_gate_bwd_" + tag, _gate_tile, [ba], [eb, ea, alog_b, dtb_b],
                                       [dbetab, dgb], TM_ROW, par_mask=[False, False, True, True])
    return dqkv_raw, dz, dba, dcw[:DN_CONV], dalog, ddtb, dng


def _swap_halves(x):
    n = x.shape[1]
    first = (_iota2((1, n), 1) % SW_HEAD_DIM) < SW_HEAD_DIM // 2
    return jnp.where(first, pltpu.roll(x, n - SW_HEAD_DIM // 2, 1), pltpu.roll(x, SW_HEAD_DIM // 2, 1))


def _rope_apply(x, cos, sin_signed):
    return x * cos + _swap_halves(x) * sin_signed


def _rope_transpose(dy, cos, sin_signed):
    return dy * cos + _swap_halves(dy * sin_signed)


def rope_tables(positions, s):
    half = SW_HEAD_DIM // 2
    inv_freq = ROPE_THETA ** (-jnp.arange(0, SW_HEAD_DIM, 2, dtype=f32) / SW_HEAD_DIM)
    ang = positions.reshape(s, 1).astype(f32) * inv_freq[None, :]
    cos, sin = jnp.cos(ang), jnp.sin(ang)
    cos_t = jnp.tile(jnp.concatenate([cos, cos], 1), (1, SW_HEADS))
    sin_t = jnp.tile(jnp.concatenate([-sin, sin], 1), (1, SW_HEADS))
    assert cos_t.shape == (s, SW_DIM) and half * 2 == SW_HEAD_DIM
    return cos_t, sin_t


def rope_fwd(tag, proj, cos, sin):
    def fn(q, k, v, c, sg):
        return _rope_apply(q, c, sg), _rope_apply(k, c, sg), v

    rows = [(proj, SW_DIM, COL_SWQ // SW_DIM), (proj, SW_DIM, COL_SWK // SW_DIM), (proj, SW_DIM, COL_SWV // SW_DIM), cos, sin]
    return rowmap("rope_" + tag, fn, rows, [], [SW_DIM] * 3, TM_ROW, out_dtypes=[bf16] * 3)


def _swa_block(q, kp, kc, vp, vc, first):
    blk = SW_BLOCK
    kk = jnp.concatenate([kp, kc], 0)
    vv = jnp.concatenate([vp, vc], 0)
    dist = (_iota2((blk, 2 * blk), 0) + blk) - _iota2((blk, 2 * blk), 1)
    kj = _iota2((blk, 2 * blk), 1)
    valid = (dist >= 0) & (dist <= blk) & ((kj >= blk) | jnp.logical_not(first))
    lane_head = _iota2((1, LANES), 1) // SW_HEAD_DIM
    outs, lses = [], []
    for p in range(SW_DIM // LANES):
        sl = slice(p * LANES, (p + 1) * LANES)
        qp, kp_, vp_ = q[:, sl], kk[:, sl], vv[:, sl]
        o_pair = jnp.zeros((blk, LANES), f32)
        l_pair = jnp.zeros((blk, LANES), f32)
        for e in range(LANES // SW_HEAD_DIM):
            msk = lane_head == e
            sc = bdot(jnp.where(msk, qp, 0.0), kp_, 1, 1) * (SW_HEAD_DIM ** -0.5)
            sc = jnp.where(valid, sc, -1e30)
            m = lax.stop_gradient(jnp.max(sc, -1, keepdims=True))
            pe = jnp.exp(sc - m)
            l = jnp.sum(pe, -1, keepdims=True)
            o = bdot(pe, vp_, 1, 0) / l
            o_pair = o_pair + jnp.where(msk, o, 0.0)
            l_pair = l_pair + jnp.where(msk, m + jnp.log(l), 0.0)
        outs.append(o_pair)
        lses.append(l_pair)
    return jnp.concatenate(outs, -1), jnp.concatenate(lses, -1)


def _swa_specs(r):
    cur = pl.BlockSpec((SW_BLOCK, SW_DIM), lambda rho, n: (n, rho))
    prev = pl.BlockSpec((SW_BLOCK, SW_DIM), lambda rho, n: (jnp.maximum(n - 1, 0), rho))
    return cur, prev


def swa_fwd(tag, r, q, k, v):
    s = q.shape[0]
    ln = s // r
    q2, k2, v2 = (t.reshape(ln, r * SW_DIM) for t in (q, k, v))
    cur, prev = _swa_specs(r)

    def body(q_ref, kp_ref, kc_ref, vp_ref, vc_ref, o_ref, l_ref):
        ins = [r[...].astype(f32) for r in (q_ref, kp_ref, kc_ref, vp_ref, vc_ref)]
        o, l = _swa_block(*ins, pl.program_id(1) == 0)
        o_ref[...] = o
        l_ref[...] = l

    o, l = pl.pallas_call(
        body, grid=(r, ln // SW_BLOCK),
        in_specs=[cur, prev, cur, prev, cur], out_specs=[cur, cur],
        out_shape=[jax.ShapeDtypeStruct((ln, r * SW_DIM), f32)] * 2,
        compiler_params=_cparams("parallel", "parallel"), name=f"swa{r}_{tag}")(q2, k2, k2, v2, v2)
    return o.reshape(s, SW_DIM), l.reshape(s, SW_DIM)


def swa_bwd(tag, r, q, k, v, do, dl):
    s = q.shape[0]
    ln = s // r
    q2, k2, v2, do2, dl2 = (t.reshape(ln, r * SW_DIM) for t in (q, k, v, do, dl))
    cur, prev = _swa_specs(r)

    def body(q_ref, kp_ref, kc_ref, vp_ref, vc_ref, do_ref, dl_ref, dq_ref, dka_ref, dkb_ref, dva_ref, dvb_ref):
        first = pl.program_id(1) == 0
        ins = [r[...].astype(f32) for r in (q_ref, kp_ref, kc_ref, vp_ref, vc_ref)]
        _, vjp = jax.vjp(lambda *a: _swa_block(*a, first), *ins)
        dq_ref[...], dka_ref[...], dkb_ref[...], dva_ref[...], dvb_ref[...] = vjp((do_ref[...], dl_ref[...]))

    outs = pl.pallas_call(
        body, grid=(r, ln // SW_BLOCK),
        in_specs=[cur, prev, cur, prev, cur, cur, cur], out_specs=[cur] * 5,
        out_shape=[jax.ShapeDtypeStruct((ln, r * SW_DIM), f32)] * 5,
        compiler_params=_cparams("parallel", "parallel"), name=f"swa{r}_bwd_{tag}")(q2, k2, k2, v2, v2, do2, dl2)
    return [t.reshape(s, SW_DIM) for t in outs]


def _combine_tile(o1, l1, o2, l2, o3, l3):
    m = lax.stop_gradient(jnp.maximum(jnp.maximum(l1, l2), l3))
    e1, e2, e3 = jnp.exp(l1 - m), jnp.exp(l2 - m), jnp.exp(l3 - m)
    return ((o1 * e1 + o2 * e2 + o3 * e3) / (e1 + e2 + e3),)


def swa_merge_bwd(tag, grads, cos, sin):
    s = cos.shape[0]
    tm = SW_BLOCK
    nt = s // tm
    here = pl.BlockSpec((tm, SW_DIM), lambda i: (i, 0))
    arrs, specs = [], []
    for r, g in zip(SW_DILATIONS, grads):
        ahead = pl.BlockSpec((tm, SW_DIM), lambda i, _r=r: (jnp.minimum(i + _r, nt - 1), 0))
        arrs += g
        specs += [here, ahead, here, ahead, here]

    def body(*refs):
        i = pl.program_id(0)
        c_ref, s_ref = refs[15], refs[16]
        dq_ref, dk_ref, dv_ref = refs[17:]
        dq = jnp.zeros((tm, SW_DIM), f32)
        dk = jnp.zeros((tm, SW_DIM), f32)
        dv = jnp.zeros((tm, SW_DIM), f32)
        for b, r in enumerate(SW_DILATIONS):
            gq, gka, gkb, gva, gvb = refs[5 * b:5 * b + 5]
            inside = i + r < nt
            dq = dq + gq[...]
            dk = dk + gkb[...] + jnp.where(inside, gka[...], 0.0)
            dv = dv + gvb[...] + jnp.where(inside, gva[...], 0.0)
        dq_ref[...] = _rope_transpose(dq, c_ref[...], s_ref[...])
        dk_ref[...] = _rope_transpose(dk, c_ref[...], s_ref[...])
        dv_ref[...] = dv

    return pl.pallas_call(
        body, grid=(nt,), in_specs=specs + [here, here], out_specs=[here] * 3,
        out_shape=[jax.ShapeDtypeStruct((s, SW_DIM), f32)] * 3,
        compiler_params=_cparams("parallel"), name="swa_merge_bwd_" + tag)(*arrs, cos, sin)


def swa_mixer_fwd(tag, proj, cos, sin):
    q, k, v = rope_fwd(tag, proj, cos, sin)
    ols = []
    for r in SW_DILATIONS:
        ols += list(swa_fwd(tag, r, q, k, v))
    b_out = rowmap("swa_comb_" + tag, _combine_tile, ols, [], [SW_DIM], TM_ROW)[0]
    return b_out, (q, k, v, ols)


def swa_mixer_bwd(tag, cos, sin, res, db_out):
    q, k, v, ols = res
    dols, _ = rowmap_bwd("swa_comb_bwd_" + tag, _combine_tile, ols, [], [db_out], TM_ROW)
    grads = [swa_bwd(tag, r, q, k, v, dols[2 * b], dols[2 * b + 1]) for b, r in enumerate(SW_DILATIONS)]
    return swa_merge_bwd(tag, grads, cos, sin)


TM_S5 = 256
S5_GPB = LANES // S5_GROUP
S5_NBLK = D_MODEL // LANES
S5_HALF = S5_GPB * S5_STATE
S5_BW = 2 * S5_HALF
S5_WIDTH = S5_NBLK * S5_BW
S5_TABW = S5_NBLK * S5_HALF


def _s5_disc_tile(a_re, a_im, log_dt, b_re, b_im, expand):
    dt = jnp.exp(log_dt)
    mag = jnp.exp(a_re * dt)
    abar_re, abar_im = mag * jnp.cos(a_im * dt), mag * jnp.sin(a_im * dt)
    n_re, n_im = abar_re - 1.0, abar_im
    den = a_re * a_re + a_im * a_im
    c_re = (n_re * a_re + n_im * a_im) / den
    c_im = (n_im * a_re - n_re * a_im) / den
    cx_re, cx_im = hdot(c_re, expand), hdot(c_im, expand)
    return abar_re, abar_im, cx_re * b_re - cx_im * b_im, cx_re * b_im + cx_im * b_re


def _s5_expand():
    return (_iota2((S5_STATE, S5_STATE * S5_GROUP), 1) // S5_GROUP == _iota2((S5_STATE, S5_STATE * S5_GROUP), 0)).astype(f32)


def s5_tables(a_re, a_im, log_dt):
    lanes = lambda v: v.reshape(1, S5_TABW)
    dt = jnp.broadcast_to(log_dt.reshape(S5_GROUPS, 1), (S5_GROUPS, S5_STATE))
    t = TM_S5

    def body(are_ref, aim_ref, ldt_ref, ar_ref, ai_ref, arr_ref, air_ref):
        dtv = jnp.exp(ldt_ref[...])
        lre, lim = are_ref[...] * dtv, aim_ref[...] * dtv
        row = _iota2((t, S5_HALF), 0)
        for asc, o_re, o_im in ((True, ar_ref, ai_ref), (False, arr_ref, air_ref)):
            n = (row + 1 if asc else t - row).astype(f32)
            mag = jnp.exp(n * lre)
            o_re[...] = mag * jnp.cos(n * lim)
            o_im[...] = mag * jnp.sin(n * lim)

    lane = pl.BlockSpec((1, S5_HALF), lambda j: (0, j))
    tab = pl.BlockSpec((t, S5_HALF), lambda j: (0, j))
    return pl.pallas_call(
        body, grid=(S5_NBLK,), in_specs=[lane] * 3, out_specs=[tab] * 4,
        out_shape=[jax.ShapeDtypeStruct((t, S5_TABW), f32)] * 4,
        compiler_params=_cparams("parallel"), name="s5_tables")(lanes(a_re), lanes(a_im), lanes(dt))


def s5_pack_weights(bbar_re, bbar_im, c_re, c_im):
    eye = jnp.eye(S5_GPB, dtype=f32)
    bb = jnp.stack([bbar_re.reshape(S5_GROUPS, S5_STATE, S5_GROUP), bbar_im.reshape(S5_GROUPS, S5_STATE, S5_GROUP)], 1)
    bb = bb.transpose(0, 3, 1, 2).reshape(S5_NBLK, S5_GPB, S5_GROUP, 2, S5_STATE)
    wb = (bb[:, :, :, :, None, :] * eye[None, :, None, None, :, None]).reshape(S5_NBLK, LANES, S5_BW)
    cc = jnp.stack([c_re, -c_im], 1)
    cc = cc.reshape(S5_NBLK, S5_GPB, 2, S5_GROUP, S5_STATE).transpose(0, 2, 1, 4, 3)
    wc = (cc[:, :, :, :, None, :] * eye[None, None, :, None, :, None]).reshape(S5_NBLK, S5_BW, LANES)
    return wb, wc


def s5_unpack_weight_grads(dwb, dwc):
    d6 = dwb.reshape(S5_NBLK, S5_GPB, S5_GROUP, 2, S5_GPB, S5_STATE)
    dbb = jnp.stack([d6[:, gl, :, :, gl, :] for gl in range(S5_GPB)])
    dbb = dbb.transpose(1, 0, 3, 4, 2).reshape(S5_GROUPS, 2, S5_STATE * S5_GROUP)
    c6 = dwc.reshape(S5_NBLK, 2, S5_GPB, S5_STATE, S5_GPB, S5_GROUP)
    dcc = jnp.stack([c6[:, :, gl, :, gl, :] for gl in range(S5_GPB)])
    dcc = dcc.transpose(1, 0, 2, 4, 3).reshape(S5_GROUPS, 2, S5_GROUP, S5_STATE)
    return dbb[:, 0], dbb[:, 1], dcc[:, 0], -dcc[:, 1]


def _s5_step_rows(t):
    d, out = 1, []
    while d < t:
        out.append(d)
        d *= 2
    return out


def s5_core_fwd(tag, u, wb, wc, a1, a2, dskip):
    s = u.shape[0]
    t = TM_S5

    def body(u_ref, wb_ref, wc_ref, ar_ref, ai_ref, d_ref, y_ref, x_ref, carry):
        @pl.when(pl.program_id(1) == 0)
        def _():
            carry[...] = jnp.zeros_like(carry)

        uv = u_ref[...]
        bu = bdot(uv, wb_ref[...], 1, 0)
        row = _iota2((t, LANES), 0)
        for c in range(S5_HALF // LANES):
            re, im = slice(c * LANES, (c + 1) * LANES), slice(S5_HALF + c * LANES, S5_HALF + (c + 1) * LANES)
            xr, xi = bu[:, re], bu[:, im]
            for d in _s5_step_rows(t):
                ar, ai = ar_ref[d - 1:d, re], ai_ref[d - 1:d, re]
                if d % SUBLANES:
                    keep = row >= d
                    sr = jnp.where(keep, pltpu.roll(xr, d, 0), 0.0)
                    si = jnp.where(keep, pltpu.roll(xi, d, 0), 0.0)
                    xr, xi = xr + ar * sr - ai * si, xi + ar * si + ai * sr
                else:
                    sr, si = xr[:t - d], xi[:t - d]
                    xr = jnp.concatenate([xr[:d], xr[d:] + (ar * sr - ai * si)], 0)
                    xi = jnp.concatenate([xi[:d], xi[d:] + (ar * si + ai * sr)], 0)
            cr, ci = carry[:, re], carry[:, im]
            ar, ai = ar_ref[:, re], ai_ref[:, re]
            x_ref[:, re] = xr + ar * cr - ai * ci
            x_ref[:, im] = xi + ar * ci + ai * cr
        carry[...] = x_ref[t - 1:t, :]
        y_ref[...] = bdot(x_ref[...], wc_ref[...], 1, 0) + d_ref[...] * uv

    tab = pl.BlockSpec((t, S5_HALF), lambda j, i: (0, j))
    return pl.pallas_call(
        body, grid=(S5_NBLK, s // t),
        in_specs=[pl.BlockSpec((t, LANES), lambda j, i: (i, j)),
                  pl.BlockSpec((None, LANES, S5_BW), lambda j, i: (j, 0, 0)),
                  pl.BlockSpec((None, S5_BW, LANES), lambda j, i: (j, 0, 0)),
                  tab, tab, pl.BlockSpec((1, LANES), lambda j, i: (0, j))],
        out_specs=[pl.BlockSpec((t, LANES), lambda j, i: (i, j)), pl.BlockSpec((t, S5_BW), lambda j, i: (i, j))],
        out_shape=[jax.ShapeDtypeStruct((s, D_MODEL), f32), jax.ShapeDtypeStruct((s, S5_WIDTH), f32)],
        scratch_shapes=[pltpu.VMEM((1, S5_BW), f32)],
        compiler_params=_cparams("parallel", "arbitrary"), name="s5_core_" + tag)(u, wb, wc, a1, a2, dskip)


def s5_core_bwd(tag, u, x, wb, wc, a1, a2, a1r, a2r, dskip, dy):
    s = u.shape[0]
    t = TM_S5
    nt = s // t
    hb = t // SUBLANES

    def body(u_ref, dy_ref, x_ref, xh_ref, wb_ref, wc_ref, ar_ref, ai_ref, arr_ref, air_ref, d_ref,
             du_ref, dwb_ref, dwc_ref, dd_ref, q1_ref, q2_ref, carry, lam_scr):
        i = pl.program_id(1)
        tt = nt - 1 - i

        @pl.when(i == 0)
        def _():
            carry[...] = jnp.zeros_like(carry)

        uv, dyv, xv = u_ref[...], dy_ref[...], x_ref[...]
        lam = bdot(dyv, wc_ref[...], 1, 1)
        row = _iota2((t, LANES), 0)
        x_last = jnp.where(tt > 0, xh_ref[SUBLANES - 1:SUBLANES, :], 0.0)
        q1s, q2s = [], []
        for c in range(S5_HALF // LANES):
            re, im = slice(c * LANES, (c + 1) * LANES), slice(S5_HALF + c * LANES, S5_HALF + (c + 1) * LANES)
            lr, li = lam[:, re], lam[:, im]
            for d in _s5_step_rows(t):
                ar, ai = ar_ref[d - 1:d, re], ai_ref[d - 1:d, re]
                if d % SUBLANES:
                    keep = row < t - d
                    sr = jnp.where(keep, pltpu.roll(lr, t - d, 0), 0.0)
                    si = jnp.where(keep, pltpu.roll(li, t - d, 0), 0.0)
                    lr, li = lr + ar * sr + ai * si, li + ar * si - ai * sr
                else:
                    sr, si = lr[d:], li[d:]
                    lr = jnp.concatenate([lr[:t - d] + (ar * sr + ai * si), lr[t - d:]], 0)
                    li = jnp.concatenate([li[:t - d] + (ar * si - ai * sr), li[t - d:]], 0)
            cr, ci = carry[:, re], carry[:, im]
            ar, ai = arr_ref[:, re], air_ref[:, re]
            lr, li = lr + ar * cr + ai * ci, li + ar * ci - ai * cr
            lam_scr[:, re] = lr
            lam_scr[:, im] = li
            pr = jnp.where(row == 0, x_last[:, re], pltpu.roll(xv[:, re], 1, 0))
            pi = jnp.where(row == 0, x_last[:, im], pltpu.roll(xv[:, im], 1, 0))
            p1, p2 = lr * pr + li * pi, li * pr - lr * pi
            q1, q2 = p1[:SUBLANES, :], p2[:SUBLANES, :]
            for k in range(1, hb):
                q1 = q1 + p1[k * SUBLANES:(k + 1) * SUBLANES, :]
                q2 = q2 + p2[k * SUBLANES:(k + 1) * SUBLANES, :]
            q1s.append(q1)
            q2s.append(q2)
        carry[...] = lam_scr[0:1, :]
        lam = lam_scr[...]
        du_ref[...] = bdot(lam, wb_ref[...], 1, 1) + d_ref[...] * dyv
        upd = [(dwb_ref, bdot(uv, lam, 0, 0)), (dwc_ref, bdot(xv, dyv, 0, 0)),
               (dd_ref, jnp.sum(dyv * uv, 0, keepdims=True)),
               (q1_ref, jnp.concatenate(q1s, 1)), (q2_ref, jnp.concatenate(q2s, 1))]

        @pl.when(i == 0)
        def _():
            for ref, val in upd:
                ref[...] = val

        @pl.when(i != 0)
        def _():
            for ref, val in upd:
                ref[...] += val

    nb8 = s // SUBLANES
    rev = lambda w: pl.BlockSpec((t, w), lambda j, i: (nt - 1 - i, j))
    tab = pl.BlockSpec((t, S5_HALF), lambda j, i: (0, j))
    return pl.pallas_call(
        body, grid=(S5_NBLK, nt),
        in_specs=[rev(LANES), rev(LANES), rev(S5_BW),
                  pl.BlockSpec((SUBLANES, S5_BW), lambda j, i: (jnp.maximum((nt - 1 - i) * hb - 1, 0), j)),
                  pl.BlockSpec((None, LANES, S5_BW), lambda j, i: (j, 0, 0)),
                  pl.BlockSpec((None, S5_BW, LANES), lambda j, i: (j, 0, 0)),
                  tab, tab, tab, tab, pl.BlockSpec((1, LANES), lambda j, i: (0, j))],
        out_specs=[rev(LANES),
                   pl.BlockSpec((None, LANES, S5_BW), lambda j, i: (j, 0, 0)),
                   pl.BlockSpec((None, S5_BW, LANES), lambda j, i: (j, 0, 0)),
                   pl.BlockSpec((1, LANES), lambda j, i: (0, j)),
                   pl.BlockSpec((SUBLANES, S5_HALF), lambda j, i: (0, j)),
                   pl.BlockSpec((SUBLANES, S5_HALF), lambda j, i: (0, j))],
        out_shape=[jax.ShapeDtypeStruct((s, D_MODEL), f32),
                   jax.ShapeDtypeStruct((S5_NBLK, LANES, S5_BW), f32),
                   jax.ShapeDtypeStruct((S5_NBLK, S5_BW, LANES), f32),
                   jax.ShapeDtypeStruct((1, D_MODEL), f32),
                   jax.ShapeDtypeStruct((SUBLANES, S5_TABW), f32),
                   jax.ShapeDtypeStruct((SUBLANES, S5_TABW), f32)],
        scratch_shapes=[pltpu.VMEM((1, S5_BW), f32), pltpu.VMEM((t, S5_BW), f32)],
        compiler_params=_cparams("parallel", "arbitrary"),
        name="s5_core_bwd_" + tag)(u, dy, x, x, wb, wc, a1, a2, a1r, a2r, dskip)


def _gelu_tile(y):
    return (jax.nn.gelu(y),)


def s5_mixer_fwd(tag, u, prm, w_og):
    a_re, a_im, log_dt, b_re, b_im, c_re, c_im, dskip = prm
    disc_in = [a_re, a_im, log_dt.reshape(S5_GROUPS, 1), b_re.reshape(S5_GROUPS, -1), b_im.reshape(S5_GROUPS, -1)]
    abar_re, abar_im, bbar_re, bbar_im = rowmap("s5_disc_" + tag, _s5_disc_tile, disc_in, [_s5_expand()],
                                                [S5_STATE, S5_STATE, S5_STATE * S5_GROUP, S5_STATE * S5_GROUP], S5_GROUPS)
    del abar_re, abar_im
    a1, a2, a1r, a2r = s5_tables(a_re, a_im, log_dt)
    wb, wc = s5_pack_weights(bbar_re, bbar_im, c_re, c_im)
    wb, wc = wb.astype(bf16), wc.astype(bf16)
    y, x = s5_core_fwd(tag, u, wb, wc, a1, a2, dskip.reshape(1, D_MODEL))
    hid = rowmap("s5_gelu_" + tag, _gelu_tile, [y], [], [D_MODEL], TM_ROW, out_dtypes=[bf16])[0]
    og = mm_nn("s5_og_" + tag, hid, w_og)
    mix = rowmap("s5_glu_" + tag, _glu_tile, [og], [], [D_MODEL], TM_ROW)[0]
    return mix, (disc_in, a1, a2, a1r, a2r, wb, wc, x, y, hid, og)


def s5_mixer_bwd(tag, idx, u, prm, w_og, res, dmix, stacks):
    a_re, a_im, log_dt, b_re, b_im, c_re, c_im, dskip = prm
    disc_in, a1, a2, a1r, a2r, wb, wc, x, y, hid, og = res
    (dog,), _ = rowmap_bwd("s5_glu_bwd_" + tag, _glu_tile, [og], [], [dmix], TM_ROW)
    n_odd = DEPTH // 2
    dw_og = (mm_tn("s5_wo_dw_" + tag, hid, dog, n_cols=D_MODEL, stack=(stacks[0], idx, n_odd)),
             mm_tn("s5_wg_dw_" + tag, hid, dog, b_col0=D_MODEL, n_cols=D_MODEL, stack=(stacks[1], idx, n_odd)))
    dhid = mm_nt("s5_og_dx_" + tag, dog, w_og)
    (dy,), _ = rowmap_bwd("s5_gelu_bwd_" + tag, _gelu_tile, [y], [], [dhid], TM_ROW)
    du, dwb, dwc, ddskip, q1, q2 = s5_core_bwd(tag, u, x, wb, wc, a1, a2, a1r, a2r, dskip.reshape(1, D_MODEL), dy)
    dbbar_re, dbbar_im, dc_re, dc_im = s5_unpack_weight_grads(dwb, dwc)
    dabar_re = q1.sum(0).reshape(S5_GROUPS, S5_STATE)
    dabar_im = q2.sum(0).reshape(S5_GROUPS, S5_STATE)
    grads, _ = rowmap_bwd("s5_disc_bwd_" + tag, _s5_disc_tile, disc_in, [_s5_expand()],
                          [dabar_re, dabar_im, dbbar_re, dbbar_im], S5_GROUPS, par_mask=[False])
    da_re, da_im, dlog_dt, db_re, db_im = grads
    return du, (da_re, da_im, dlog_dt.reshape(S5_GROUPS), db_re.reshape(b_re.shape), db_im.reshape(b_im.shape),
                dc_re, dc_im, ddskip.reshape(D_MODEL)), dw_og


HYB_IN = 3592
_IN_B0, _IN_SW0 = 2048, 2056


IN_SHARD = HYB_IN // 4
SHARD_ORDER_GRADS = ("hyb_w_in", "ffn_wg", "ffn_wu", "ffn_wd")
FFN_TRANSPOSED = ("ffn_wg", "ffn_wu")
BIG_SHARDED = ("hyb_w_in", "hyb_w_out", "s5_glu_wo", "s5_glu_wg", "xq_w", "xk_w", "xv_w", "xo_w", "ffn_wg", "ffn_wu", "ffn_wd")


def _w_in_pieces():
    runs = [(0, _IN_B0, 0), (_IN_B0, _IN_SW0, COL_BA), (_IN_SW0, HYB_IN, _IN_B0)]
    out = []
    for sh in range(4):
        lo, hi = sh * IN_SHARD, (sh + 1) * IN_SHARD
        for r_lo, r_hi, c_lo in runs:
            a, b = max(lo, r_lo), min(hi, r_hi)
            if a < b:
                out.append((sh, a - lo, b - lo, c_lo + a - r_lo))
    return out


def w_in_to_canonical(tag, layer, w4):
    tr = 128

    def body(w_ref, o_ref):
        o_ref[:, COL_BA:] = jnp.zeros((tr, BA_PAD), o_ref.dtype)
        for sh, a, b, c in _w_in_pieces():
            o_ref[:, c:c + b - a] = w_ref[sh, :, a:b]

    return pl.pallas_call(
        body, grid=(D_MODEL // tr,),
        in_specs=[pl.BlockSpec((4, None, tr, IN_SHARD), lambda i: (0, layer, i, 0))],
        out_specs=pl.BlockSpec((tr, PROJ_COLS), lambda i: (i, 0)),
        out_shape=jax.ShapeDtypeStruct((D_MODEL, PROJ_COLS), w4.dtype),
        compiler_params=_cparams("parallel"), name="w_in_canon_" + tag)(w4)


def w_in_grad_to_shards(tag, layer, g, stack, n_layers):
    tr = 128
    extra, extra_specs, _ = _stacked(stack)

    def body(g_ref, *rest):
        o_ref = rest[-1]
        for sh, a, b, c in _w_in_pieces():
            o_ref[sh, :, a:b] = g_ref[:, c:c + b - a]

    return pl.pallas_call(
        body, grid=(D_MODEL // tr,),
        in_specs=[pl.BlockSpec((tr, PROJ_COLS), lambda i: (i, 0))] + extra_specs,
        out_specs=pl.BlockSpec((4, None, tr, IN_SHARD), lambda i: (0, layer, i, 0)),
        out_shape=jax.ShapeDtypeStruct((4, n_layers, D_MODEL, IN_SHARD), f32),
        input_output_aliases={1: 0} if extra else {},
        compiler_params=_cparams("parallel"), name="w_in_grad_shards_" + tag)(g, *extra)


def _add2(name, a, b):
    return rowmap(name, lambda p, q: (p + q,), [a, b], [], [a.shape[1]], _pick(a.shape[0], (256, 128, 64, 32, 16, 8)))[0]


def local_step(x, mem, positions, target, p):
    s = x.shape[0]
    cos, sin = rope_tables(positions, s)
    row = lambda v: v.reshape(1, -1).astype(f32)
    wg4, wu4, wd4 = (p[n].astype(bf16) for n in ("ffn_wg", "ffn_wu", "ffn_wd"))
    h = h16 = x
    tape = []
    for l in range(DEPTH):
        i, tag = l // 2, str(l)
        t = {"h0": h, "h0_16": h16}
        if l % 2 == 0:
            t["w_in"] = w_in_to_canonical(tag, i, p["hyb_w_in"].astype(bf16))
            t["w_out"] = p["hyb_w_out"][i].astype(bf16)
            t["dn_prm"] = (p["dn_conv_w"][i].astype(f32), row(jnp.repeat(p["dn_a_log"][i], DN_HEAD_DIM)),
                           row(jnp.repeat(p["dn_dt_bias"][i], DN_HEAD_DIM)), row(jnp.tile(p["dn_norm_g"][i], DN_HEADS)))
            t["proj"] = mm_nn("hyb_in_" + tag, h16, t["w_in"])
            a_out, t["dn"] = dn_mixer_fwd(tag, t["proj"], *t["dn_prm"])
            b_out, t["swa"] = swa_mixer_fwd(tag, t["proj"], cos, sin)
            t["mixed"] = jnp.concatenate([a_out, b_out], 1)
            mix = mm_nn("hyb_out_" + tag, t["mixed"], t["w_out"])
        else:
            t["s5_prm"] = tuple(p[n][i].astype(f32) for n in
                                ("s5_a_re", "s5_a_im", "s5_log_dt", "s5_b_re", "s5_b_im", "s5_c_re", "s5_c_im", "s5_d"))
            t["w_og"] = jnp.concatenate([p["s5_glu_wo"][i], p["s5_glu_wg"][i]], 1).astype(bf16)
            mix, t["s5"] = s5_mixer_fwd(tag, h, t["s5_prm"], t["w_og"])
        t["mix"] = mix
        t["ln"] = [(row(p[g][l]), row(p[b][l])) for g, b in
                   (("ln_mix_g", "ln_mix_b"), ("ln_x_g", "ln_x_b"), ("ln_ffn_g", "ln_ffn_b"))]
        t["h1"], t["h1_16"] = postnorm_fwd("mix" + tag, h, mix, *t["ln"][0])
        t["wq"], t["wo"] = p["xq_w"][l].astype(bf16), p["xo_w"][l].astype(bf16)
        t["wkv"] = jnp.concatenate([p["xk_w"][l], p["xv_w"][l]], 1).astype(bf16)
        (t["xo"], t["h2"], t["h2_16"]), t["xres"] = xattn_fwd(tag, t["h1_16"], mem, t["wq"], t["wkv"], t["wo"],
                                                              (t["h1"], *t["ln"][1]))
        (t["fo"], h, h16), t["fres"] = ffn_fwd(tag, l, t["h2_16"], wg4, wu4, wd4, (t["h2"], *t["ln"][2]))
        tape.append(t)

    part, dh = loss_head(h, target)
    loss = jnp.sum(part)

    g = {n: [None] * v.shape[0] for n, v in p.items() if n not in BIG_SHARDED}
    st = {n: None for n in BIG_SHARDED}
    for l in reversed(range(DEPTH)):
        i, tag, t = l // 2, str(l), tape[l]
        dh2a, dfo, dg, db = postnorm_bwd("ffn" + tag, t["h2"], t["fo"], *t["ln"][2], dh)
        g["ln_ffn_g"][l], g["ln_ffn_b"][l] = dg[0], db[0]
        ffn_names = ("ffn_wg", "ffn_wu", "ffn_wd")
        prev = None if st["ffn_wd"] is None else [st[n] for n in ffn_names]
        dh2b, new = ffn_bwd(tag, l, t["h2_16"], wg4, wu4, wd4, t["fres"], dfo, prev)
        st.update(zip(ffn_names, new))
        dh1a, dxo, dg, db = postnorm_bwd("x" + tag, t["h1"], t["xo"], *t["ln"][1], [dh2a, dh2b])
        g["ln_x_g"][l], g["ln_x_b"][l] = dg[0], db[0]
        x_names = ("xq_w", "xk_w", "xv_w", "xo_w")
        dh1b, new = xattn_bwd(tag, l, t["h1_16"], mem, t["wq"], t["wkv"], t["wo"], t["xres"], dxo, [st[n] for n in x_names])
        st.update(zip(x_names, new))
        dh0a, dmix, dg, db = postnorm_bwd("mix" + tag, t["h0"], t["mix"], *t["ln"][0], [dh1a, dh1b])
        g["ln_mix_g"][l], g["ln_mix_b"][l] = dg[0], db[0]
        if l % 2 == 0:
            st["hyb_w_out"] = mm_tn("hyb_out_dw_" + tag, t["mixed"], dmix, stack=(st["hyb_w_out"], i, DEPTH // 2))
            dmixed = mm_nt("hyb_out_dx_" + tag, dmix, t["w_out"])
            dqkv, dz, dba, dcw, dalog, ddtb, dng = dn_mixer_bwd(tag, t["proj"], *t["dn_prm"], t["dn"], (dmixed, DN_KEY_DIM, 0))
            g["dn_conv_w"][i] = dcw
            g["dn_a_log"][i] = dalog.reshape(DN_HEADS, DN_HEAD_DIM).sum(1)
            g["dn_dt_bias"][i] = ddtb.reshape(DN_HEADS, DN_HEAD_DIM).sum(1)
            g["dn_norm_g"][i] = dng.reshape(DN_HEADS, DN_HEAD_DIM).sum(0)
            dq, dk, dv = swa_mixer_bwd(tag, cos, sin, t["swa"], (dmixed, SW_DIM, 1))
            dproj = jnp.concatenate([dqkv, dz, dq, dk, dv, dba], 1)
            st["hyb_w_in"] = w_in_grad_to_shards(tag, i, mm_tn("hyb_in_dw_" + tag, t["h0_16"], dproj), st["hyb_w_in"], DEPTH // 2)
            dh0b = mm_nt("hyb_in_dx_" + tag, dproj, t["w_in"])
        else:
            dh0b, dprm, (st["s5_glu_wo"], st["s5_glu_wg"]) = s5_mixer_bwd(
                tag, i, t["h0"], t["s5_prm"], t["w_og"], t["s5"], dmix, (st["s5_glu_wo"], st["s5_glu_wg"]))
            for n, v in zip(("s5_a_re", "s5_a_im", "s5_log_dt", "s5_b_re", "s5_b_im", "s5_c_re", "s5_c_im", "s5_d"), dprm):
                g[n][i] = v
        dh = [dh0a, dh0b]
    grad_x = _add2("grad_x", dh[0], dh[1])
    grads = {n: jnp.stack(v) for n, v in g.items()}
    grads.update(st)
    return loss, grad_x, grads


WEIGHT_NAMES = ("hyb_w_in", "dn_conv_w", "dn_a_log", "dn_dt_bias", "dn_norm_g", "hyb_w_out", "s5_a_re", "s5_a_im",
                "s5_log_dt", "s5_b_re", "s5_b_im", "s5_c_re", "s5_c_im", "s5_d", "s5_glu_wo", "s5_glu_wg",
                "ln_mix_g", "ln_mix_b", "xq_w", "xk_w", "xv_w", "xo_w", "ln_x_g", "ln_x_b",
                "ffn_wg", "ffn_wu", "ffn_wd", "ln_ffn_g", "ln_ffn_b")
SHARD_AXIS = {"hyb_w_in": 2, "dn_conv_w": 2, "hyb_w_out": 1, "s5_d": 1, "s5_glu_wo": 1, "s5_glu_wg": 1,
              "xq_w": 1, "xk_w": 1, "xv_w": 1, "xo_w": 1, "ffn_wg": 2, "ffn_wu": 2, "ffn_wd": 1}
GATHER_F32 = ("dn_conv_w", "s5_d")
N_CHIPS = 4
PACK_COLS = 1024
_ANY = pl.BlockSpec(memory_space=pl.ANY)


def _pos():
    return lax.axis_index("x"), lax.axis_index("y"), lax.axis_index("c")


def _chip_peers(mx, my):
    return [(1 - mx, my), (mx, 1 - my), (1 - mx, 1 - my)]


def _rcopy(src, dst, ssem, rsem, dev):
    return pltpu.make_async_remote_copy(src_ref=src, dst_ref=dst, send_sem=ssem, recv_sem=rsem,
                                        device_id=dev, device_id_type=pl.DeviceIdType.MESH)


def comm_allgather4(name, x):
    def body(x_ref, o_ref, ssem, rsem, lsem):
        mx, my, mc = _pos()
        me = 2 * mx + my
        peers = _chip_peers(mx, my)
        loc = pltpu.make_async_copy(x_ref, o_ref.at[me], lsem)
        loc.start()
        sends = [_rcopy(x_ref, o_ref.at[me], ssem.at[k], rsem.at[k], (px, py, mc)) for k, (px, py) in enumerate(peers)]
        for cp in sends:
            cp.start()
        for k, (px, py) in enumerate(peers):
            _rcopy(x_ref, o_ref.at[2 * px + py], ssem.at[k], rsem.at[k], (px, py, mc)).wait_recv()
        for cp in sends:
            cp.wait_send()
        loc.wait()

    return pl.pallas_call(
        body, out_shape=jax.ShapeDtypeStruct((N_CHIPS,) + x.shape, x.dtype), in_specs=[_ANY], out_specs=_ANY,
        scratch_shapes=[pltpu.SemaphoreType.DMA((3,)), pltpu.SemaphoreType.DMA((3,)), pltpu.SemaphoreType.DMA],
        name=name)(x)


def _multi_call(name, body, ins, out_shapes, sems, in_place=False):
    return pl.pallas_call(
        body, out_shape=out_shapes, in_specs=[_ANY] * len(ins), out_specs=[_ANY] * len(out_shapes),
        scratch_shapes=sems, input_output_aliases={w: w for w in range(len(ins))} if in_place else {},
        name=name)(*ins)


def comm_gather_weights(name, slots):
    n = len(slots)

    def body(*refs):
        os_ = refs[n:2 * n]
        ssem, rsem, fssem, frsem = refs[2 * n:]
        mx, my, mc = _pos()
        me = 2 * mx + my
        peers = _chip_peers(mx, my)
        sib = (mx, my, 1 - mc)
        half = [o.shape[1] // 2 for o in os_]
        mine = [pl.ds(mc * h, h) for h in half]
        other = [pl.ds((1 - mc) * h, h) for h in half]
        sends = [_rcopy(os_[w].at[me, mine[w]], os_[w].at[me, mine[w]], ssem.at[w, k], rsem.at[w, k], (px, py, mc))
                 for w in range(n) for k, (px, py) in enumerate(peers)]
        for cp in sends:
            cp.start()
        fwds = []
        for w in range(n):
            for k, (px, py) in enumerate(peers):
                landed = os_[w].at[2 * px + py, mine[w]]
                _rcopy(landed, landed, ssem.at[w, k], rsem.at[w, k], (px, py, mc)).wait_recv()
                fw = _rcopy(landed, landed, fssem.at[w, k], frsem.at[w, k], sib)
                fw.start()
                fwds.append(fw)
        for w in range(n):
            for k, (px, py) in enumerate(peers):
                theirs = os_[w].at[2 * px + py, other[w]]
                _rcopy(theirs, theirs, fssem.at[w, k], frsem.at[w, k], sib).wait_recv()
        for cp in sends + fwds:
            cp.wait_send()

    dma = pltpu.SemaphoreType.DMA
    return _multi_call(name, body, slots, [jax.ShapeDtypeStruct(x.shape, x.dtype) for x in slots],
                       [dma((n, 3)), dma((n, 3)), dma((n, 3)), dma((n, 3))], in_place=True)


def comm_sibling_halves(name, gs):
    n = len(gs)

    def body(*refs):
        xs, os_ = refs[:n], refs[n:2 * n]
        ssem, rsem = refs[2 * n:]
        mx, my, mc = _pos()
        sib = (mx, my, 1 - mc)
        sends = []
        for w in range(n):
            h = xs[w].shape[1] // 2
            for j in range(N_CHIPS):
                sends.append(_rcopy(xs[w].at[j, pl.ds((1 - mc) * h, h)], os_[w].at[j], ssem.at[w, j], rsem.at[w, j], sib))
        for cp in sends:
            cp.start()
        for w in range(n):
            for j in range(N_CHIPS):
                _rcopy(os_[w].at[j], os_[w].at[j], ssem.at[w, j], rsem.at[w, j], sib).wait_recv()
        for cp in sends:
            cp.wait_send()

    dma = pltpu.SemaphoreType.DMA
    return _multi_call(name, body, gs,
                       [jax.ShapeDtypeStruct((N_CHIPS, g.shape[1] // 2) + g.shape[2:], g.dtype) for g in gs],
                       [dma((n, N_CHIPS)), dma((n, N_CHIPS))])


def comm_alltoall4(name, xs):
    n = len(xs)

    def body(*refs):
        xr, os_ = refs[:n], refs[n:2 * n]
        ssem, rsem = refs[2 * n:]
        mx, my, mc = _pos()
        me = 2 * mx + my
        peers = _chip_peers(mx, my)
        sends = [_rcopy(xr[w].at[2 * px + py], os_[w].at[me], ssem.at[w, k], rsem.at[w, k], (px, py, mc))
                 for w in range(n) for k, (px, py) in enumerate(peers)]
        for cp in sends:
            cp.start()
        for w in range(n):
            for k, (px, py) in enumerate(peers):
                dst = os_[w].at[2 * px + py]
                _rcopy(dst, dst, ssem.at[w, k], rsem.at[w, k], (px, py, mc)).wait_recv()
        for cp in sends:
            cp.wait_send()

    dma = pltpu.SemaphoreType.DMA
    return _multi_call(name, body, xs, [jax.ShapeDtypeStruct(x.shape, x.dtype) for x in xs], [dma((n, 3)), dma((n, 3))])


def comm_sibling_join(name, bs):
    n = len(bs)

    def body(*refs):
        os_ = refs[n:2 * n]
        ssem, rsem = refs[2 * n:]
        mx, my, mc = _pos()
        sib = (mx, my, 1 - mc)
        sends = [_rcopy(os_[w].at[mc], os_[w].at[mc], ssem.at[w], rsem.at[w], sib) for w in range(n)]
        for cp in sends:
            cp.start()
        for w in range(n):
            dst = os_[w].at[1 - mc]
            _rcopy(dst, dst, ssem.at[w], rsem.at[w], sib).wait_recv()
        for cp in sends:
            cp.wait_send()

    dma = pltpu.SemaphoreType.DMA
    return _multi_call(name, body, bs, [jax.ShapeDtypeStruct(b.shape, b.dtype) for b in bs], [dma((n,)), dma((n,))],
                       in_place=True)


def comm_sibling_swap(name, x):
    def body(x_ref, o_ref, ssem, rsem):
        mx, my, mc = _pos()
        cp = _rcopy(x_ref, o_ref, ssem, rsem, (mx, my, 1 - mc))
        cp.start()
        cp.wait_recv()
        cp.wait_send()

    return pl.pallas_call(
        body, out_shape=jax.ShapeDtypeStruct(x.shape, x.dtype), in_specs=[_ANY], out_specs=_ANY,
        scratch_shapes=[pltpu.SemaphoreType.DMA, pltpu.SemaphoreType.DMA], name=name)(x)


def _row_tile(r):
    return _pick(r, (256, 128, 64, 32, 16, 8))


def add_own_half(name, g, recv, out_dtype):
    r, c = g.shape[2:]
    tr = _row_tile(r)
    mc = lax.axis_index("c").astype(jnp.int32).reshape(1)

    def body(c_ref, g_ref, r_ref, o_ref):
        o_ref[...] = (g_ref[...] + r_ref[...]).astype(o_ref.dtype)

    grid_spec = pltpu.PrefetchScalarGridSpec(
        num_scalar_prefetch=1, grid=(N_CHIPS, r // tr),
        in_specs=[pl.BlockSpec((None, None, tr, c), lambda j, i, cr: (j, cr[0], i, 0)),
                  pl.BlockSpec((None, tr, c), lambda j, i, cr: (j, i, 0))],
        out_specs=pl.BlockSpec((None, tr, c), lambda j, i, cr: (j, i, 0)))
    return pl.pallas_call(body, grid_spec=grid_spec, out_shape=jax.ShapeDtypeStruct(recv.shape, out_dtype),
                          compiler_params=_cparams("parallel", "parallel"), name=name)(mc, g, recv)


def cast_into_slot(name, w, chip, dtype):
    r, c = w.shape
    tr = _row_tile(r)

    def body(c_ref, w_ref, o_ref):
        o_ref[...] = w_ref[...].astype(o_ref.dtype)

    grid_spec = pltpu.PrefetchScalarGridSpec(
        num_scalar_prefetch=1, grid=(r // tr,),
        in_specs=[pl.BlockSpec((tr, c), lambda i, cr: (i, 0))],
        out_specs=pl.BlockSpec((None, tr, c), lambda i, cr: (cr[0], i, 0)))
    return pl.pallas_call(body, grid_spec=grid_spec, out_shape=jax.ShapeDtypeStruct((N_CHIPS, r, c), dtype),
                          compiler_params=_cparams("parallel"), name=name)(chip.astype(jnp.int32).reshape(1), w)


def sum_chips_into_half(name, own, arrived, chip, mc):
    r, c = own.shape[1:]
    tr = _row_tile(r)

    def body(s0, s1, s2, s3, s4, own_ref, a_ref, b_ref, d_ref, o_ref):
        o_ref[...] = ((own_ref[...].astype(f32) + a_ref[...].astype(f32))
                      + (b_ref[...].astype(f32) + d_ref[...].astype(f32)))

    slot = lambda k: pl.BlockSpec((None, tr, c), lambda i, *sc, _k=k: (sc[_k][0], i, 0))
    grid_spec = pltpu.PrefetchScalarGridSpec(
        num_scalar_prefetch=5, grid=(r // tr,), in_specs=[slot(0), slot(1), slot(2), slot(3)],
        out_specs=pl.BlockSpec((None, tr, c), lambda i, *sc: (sc[4][0], i, 0)))
    mx, my = lax.axis_index("x"), lax.axis_index("y")
    scal = [v.astype(jnp.int32).reshape(1) for v in
            (2 * mx + my, 2 * (1 - mx) + my, 2 * mx + (1 - my), 2 * (1 - mx) + (1 - my), mc)]
    return pl.pallas_call(body, grid_spec=grid_spec, out_shape=jax.ShapeDtypeStruct((2, r, c), f32),
                          compiler_params=_cparams("parallel"), name=name)(*scal, own, arrived, arrived, arrived)


def sum_slots(name, x):
    r, c = x.shape[1:]
    tr = _row_tile(r)

    def body(x_ref, o_ref):
        o_ref[...] = (x_ref[0].astype(f32) + x_ref[1].astype(f32)) + (x_ref[2].astype(f32) + x_ref[3].astype(f32))

    return pl.pallas_call(
        body, grid=(r // tr,), in_specs=[pl.BlockSpec((N_CHIPS, tr, c), lambda i: (0, i, 0))],
        out_specs=pl.BlockSpec((tr, c), lambda i: (i, 0)), out_shape=jax.ShapeDtypeStruct((r, c), f32),
        compiler_params=_cparams("parallel"), name=name)(x)


def adamw(name, w, g, m, v):
    r, c = w.shape
    tr = _row_tile(r)

    def body(w_ref, g_ref, m_ref, v_ref, d_ref, nm_ref, nv_ref):
        gv = g_ref[...]
        nm = ADAM_B1 * m_ref[...] + (1.0 - ADAM_B1) * gv
        nv = ADAM_B2 * v_ref[...] + (1.0 - ADAM_B2) * (gv * gv)
        m_hat = nm / (1.0 - ADAM_B1 ** ADAM_STEP)
        v_hat = nv / (1.0 - ADAM_B2 ** ADAM_STEP)
        d_ref[...] = -ADAM_LR * (m_hat / (jnp.sqrt(v_hat) + ADAM_EPS) + ADAM_WD * w_ref[...])
        nm_ref[...] = nm
        nv_ref[...] = nv

    blk = pl.BlockSpec((tr, c), lambda i: (i, 0))
    return pl.pallas_call(
        body, grid=(r // tr,), in_specs=[blk] * 4, out_specs=[blk] * 3,
        out_shape=[jax.ShapeDtypeStruct((r, c), f32)] * 3,
        compiler_params=_cparams("parallel"), name=name)(w, g, m, v)


def _pack_rows(n):
    return -(-n // PACK_COLS)


def _pack(arrs, dtype, row_multiple):
    segs = []
    for a in arrs:
        flat = a.astype(dtype).reshape(-1)
        k = _pack_rows(flat.shape[0])
        segs.append(jnp.pad(flat, (0, k * PACK_COLS - flat.shape[0])).reshape(k, PACK_COLS))
    rows = sum(s.shape[0] for s in segs)
    pad = -rows % row_multiple
    if pad:
        segs.append(jnp.zeros((pad, PACK_COLS), dtype))
    return jnp.concatenate(segs, 0)


def _unpack(packed, shapes):
    out, r = [], 0
    for shp in shapes:
        n = math.prod(shp)
        k = _pack_rows(n)
        out.append(packed[r:r + k].reshape(-1)[:n].reshape(shp))
        r += k
    return out


def _gathered_to_full(g, axis):
    t = jnp.moveaxis(g, 0, axis)
    return t.reshape(t.shape[:axis] + (t.shape[axis] * t.shape[axis + 1],) + t.shape[axis + 2:])


def _full_to_shard_major(full, axis):
    shp = full.shape
    t = full.reshape(shp[:axis] + (N_CHIPS, shp[axis] // N_CHIPS) + shp[axis + 1:])
    return jnp.moveaxis(t, axis, 0)


def kernel(x, mem, positions, hyb_w_in, dn_conv_w, dn_a_log, dn_dt_bias, dn_norm_g, hyb_w_out, s5_a_re, s5_a_im, s5_log_dt, s5_b_re, s5_b_im, s5_c_re, s5_c_im, s5_d, s5_glu_wo, s5_glu_wg, ln_mix_g, ln_mix_b, xq_w, xk_w, xv_w, xo_w, ln_x_g, ln_x_b, ffn_wg, ffn_wu, ffn_wd, ln_ffn_g, ln_ffn_b, loss_target, m_hyb_w_in, m_dn_conv_w, m_dn_a_log, m_dn_dt_bias, m_dn_norm_g, m_hyb_w_out, m_s5_a_re, m_s5_a_im, m_s5_log_dt, m_s5_b_re, m_s5_b_im, m_s5_c_re, m_s5_c_im, m_s5_d, m_s5_glu_wo, m_s5_glu_wg, m_ln_mix_g, m_ln_mix_b, m_xq_w, m_xk_w, m_xv_w, m_xo_w, m_ln_x_g, m_ln_x_b, m_ffn_wg, m_ffn_wu, m_ffn_wd, m_ln_ffn_g, m_ln_ffn_b, v_hyb_w_in, v_dn_conv_w, v_dn_a_log, v_dn_dt_bias, v_dn_norm_g, v_hyb_w_out, v_s5_a_re, v_s5_a_im, v_s5_log_dt, v_s5_b_re, v_s5_b_im, v_s5_c_re, v_s5_c_im, v_s5_d, v_s5_glu_wo, v_s5_glu_wg, v_ln_mix_g, v_ln_mix_b, v_xq_w, v_xk_w, v_xv_w, v_xo_w, v_ln_x_g, v_ln_x_b, v_ffn_wg, v_ffn_wu, v_ffn_wd, v_ln_ffn_g, v_ln_ffn_b):
    a = dict(locals())
    big = [n for n in WEIGHT_NAMES if n in SHARD_AXIS and n not in GATHER_F32]
    small = [n for n in WEIGHT_NAMES if n not in big]
    chip = 2 * lax.axis_index("x") + lax.axis_index("y")
    for n in FFN_TRANSPOSED:
        for pre in ("", "m_", "v_"):
            a[pre + n] = jnp.swapaxes(a[pre + n], 1, 2)

    mc = lax.axis_index("c")
    view2 = lambda t: t.reshape(-1, t.shape[-1])
    slots = [cast_into_slot("slot_" + n, view2(a[n]), chip, bf16).reshape((N_CHIPS,) + a[n].shape) for n in big]
    gathered = comm_gather_weights("gather_w", slots)
    tiny4 = _unpack_slots(comm_allgather4("gather_w_tiny", _pack([a[n] for n in GATHER_F32], f32, 8)),
                          [a[n].shape for n in GATHER_F32])
    p = {n: a[n] for n in small if n not in GATHER_F32}
    for n, g4 in zip(GATHER_F32, tiny4):
        p[n] = _gathered_to_full(g4, SHARD_AXIS[n])
    for n, g4 in zip(big, gathered):
        p[n] = g4 if n in SHARD_ORDER_GRADS else _gathered_to_full(g4, SHARD_AXIS[n])

    loss, grad_x, grads = local_step(x[0], mem[0], positions, loss_target[0], p)
    loss = lax.psum(loss, ("x", "y", "c"))

    g4s = [grads[n] for n in big]
    recv = comm_sibling_halves("rs_sibling_halves", g4s)
    pairs = []
    for n, g4, r4 in zip(big, g4s, recv):
        lh, cols = g4.shape[1] // 2, g4.shape[-1]
        v4 = g4.reshape(N_CHIPS, 2, -1, cols)
        pairs.append(add_own_half("rs_add_" + n, v4, r4.reshape(N_CHIPS, -1, cols), bf16).reshape((N_CHIPS, lh) + g4.shape[2:]))
    arrived = comm_alltoall4("rs_alltoall", pairs)
    slot3 = lambda t: t.reshape(N_CHIPS, -1, t.shape[-1])
    halves = [sum_chips_into_half("rs_sum_" + n, slot3(pr), slot3(ar), chip, mc) for n, pr, ar in zip(big, pairs, arrived)]
    g_big = {n: t.reshape(a[n].shape) for n, t in zip(big, comm_sibling_join("rs_sibling_join", halves))}

    rpack = _pack([grads[n] for n in small], f32, 8)
    rpair = _add2("ar_add_sibling", rpack, comm_sibling_swap("ar_sibling_swap", rpack))
    g_small = _unpack(sum_slots("ar_sum_chips", comm_allgather4("ar_allgather", rpair)), [grads[n].shape for n in small])
    g_small = {n: (lax.dynamic_index_in_dim(_full_to_shard_major(g, SHARD_AXIS[n]), chip, 0, keepdims=False)
                   if n in SHARD_AXIS else g) for n, g in zip(small, g_small)}

    outs = {}
    for n in big:
        view = lambda t: t.reshape(-1, t.shape[-1])
        d, nm, nv = adamw("adamw_" + n, view(a[n]), view(g_big[n]), view(a["m_" + n]), view(a["v_" + n]))
        outs[n] = (g_big[n],) + tuple(t.reshape(a[n].shape) for t in (d, nm, nv))
    shapes = [a[n].shape for n in small]
    packs = [_pack([a[pre + n] for n in small], f32, 8) for pre in ("", "m_", "v_")]
    upd = adamw("adamw_small", packs[0], _pack([g_small[n] for n in small], f32, 8), packs[1], packs[2])
    for k, n in enumerate(small):
        outs[n] = (g_small[n],) + tuple(_unpack(buf, shapes)[k] for buf in upd)
    for n in FFN_TRANSPOSED:
        outs[n] = tuple(jnp.swapaxes(t, 1, 2) for t in outs[n])
    res = [loss, grad_x[None]]
    for kind in range(4):
        res += [outs[n][kind] for n in WEIGHT_NAMES]
    return tuple(res)


def _unpack_slots(gathered, shapes):
    out, r = [], 0
    for shp in shapes:
        n = math.prod(shp)
        k = _pack_rows(n)
        out.append(gathered[:, r:r + k].reshape(N_CHIPS, -1)[:, :n].reshape((N_CHIPS,) + tuple(shp)))
        r += k
    return out
```

```python
import functools
import math

import jax
import jax.numpy as jnp
from jax import lax
from jax.experimental import pallas as pl
from jax.experimental.pallas import tpu as pltpu

f32 = jnp.float32
bf16 = jnp.bfloat16

D_MODEL = 1024
DEPTH = 4
DN_HEADS = 4
DN_HEAD_DIM = 128
DN_KEY_DIM = 512
DN_QKV_DIM = 1536
DN_CONV = 4
SW_HEADS = 8
SW_HEAD_DIM = 64
SW_DIM = 512
SW_DILATIONS = (1, 4, 16)
SW_BLOCK = 128
ROPE_THETA = 10000.0
S5_GROUP = 16
S5_GROUPS = 64
S5_STATE = 64
X_HEADS = 4
X_HEAD_DIM = 256
FFN_HIDDEN = 2816
ALPHA = (2 * DEPTH) ** 0.25
LN_EPS = 1e-5
RMS_EPS = 1e-6
ADAM_LR, ADAM_B1, ADAM_B2, ADAM_EPS, ADAM_WD, ADAM_STEP = 0.001, 0.9, 0.999, 1e-08, 0.01, 10

BA_PAD = 256
PROJ_COLS = DN_QKV_DIM + DN_KEY_DIM + 3 * SW_DIM + BA_PAD
COL_Z = DN_QKV_DIM
COL_SWQ = COL_Z + DN_KEY_DIM
COL_SWK = COL_SWQ + SW_DIM
COL_SWV = COL_SWK + SW_DIM
COL_BA = COL_SWV + SW_DIM

LANES = 128
SUBLANES = 8
VMEM_LIMIT = 56 * 1024 * 1024
DN_CHUNK = 128
DN_HEADS_PER_STEP = 4


def _cparams(*sem):
    return pltpu.CompilerParams(dimension_semantics=tuple(sem), vmem_limit_bytes=VMEM_LIMIT)


def _dg(x, y, cx, cy):
    return lax.dot_general(x, y, (((cx,), (cy,)), ((), ())), preferred_element_type=f32)


@functools.partial(jax.custom_vjp, nondiff_argnums=(2, 3))
def bdot(a, b, ca, cb):
    return _dg(a.astype(bf16), b.astype(bf16), ca, cb)


def _bdot_fwd(a, b, ca, cb):
    return bdot(a, b, ca, cb), (a, b)


def _bdot_bwd(ca, cb, res, g):
    a, b = res
    g16, a16, b16 = g.astype(bf16), a.astype(bf16), b.astype(bf16)
    da = _dg(g16, b16, 1, 1 - cb) if ca == 1 else _dg(b16, g16, 1 - cb, 1)
    db = _dg(a16, g16, 1 - ca, 0) if cb == 0 else _dg(g16, a16, 0, 1 - ca)
    return da.astype(a.dtype), db.astype(b.dtype)


bdot.defvjp(_bdot_fwd, _bdot_bwd)


def _split_hi_lo(a):
    hi = a.astype(bf16)
    return hi, (a - hi.astype(f32)).astype(bf16)


def _dot3(a, b, ca, cb):
    a_hi, a_lo = _split_hi_lo(a)
    b_hi, b_lo = _split_hi_lo(b)
    return _dg(a_hi, b_hi, ca, cb) + (_dg(a_hi, b_lo, ca, cb) + _dg(a_lo, b_hi, ca, cb))


def hdot(a, b):
    return jnp.dot(a, b, precision=lax.Precision.HIGHEST, preferred_element_type=f32)


def _iota2(shape, dim):
    return lax.broadcasted_iota(jnp.int32, shape, dim)


def _row_spec(r, tm):
    if isinstance(r, tuple):
        arr, width, blk = r
        return arr, pl.BlockSpec((tm, width), lambda i, _b=blk: (i, _b))
    return r, pl.BlockSpec((tm, r.shape[1]), lambda i: (i, 0))


def _par_spec(p):
    return pl.BlockSpec(p.shape, lambda i, _n=p.ndim: (0,) * _n)


def rowmap(name, fn, rows, params, out_cols, tm, out_dtypes=None):
    arrs, specs = zip(*[_row_spec(r, tm) for r in rows])
    s = arrs[0].shape[0]
    n_in = len(rows) + len(params)
    out_dtypes = out_dtypes or [f32] * len(out_cols)

    def body(*refs):
        outs = fn(*[r[...] for r in refs[:n_in]])
        for o_ref, o in zip(refs[n_in:], outs):
            o_ref[...] = o.astype(o_ref.dtype)

    return pl.pallas_call(
        body, grid=(s // tm,),
        in_specs=list(specs) + [_par_spec(p) for p in params],
        out_specs=[pl.BlockSpec((tm, c), lambda i: (i, 0)) for c in out_cols],
        out_shape=[jax.ShapeDtypeStruct((s, c), dt) for c, dt in zip(out_cols, out_dtypes)],
        compiler_params=_cparams("parallel"), name=name)(*arrs, *params)


def rowmap_bwd(name, fn, rows, params, cts, tm, row_mask=None, par_mask=None, row_dtypes=None):
    arrs, specs = zip(*[_row_spec(r, tm) for r in rows])
    s = arrs[0].shape[0]
    ct_groups = [c if isinstance(c, list) else [c] for c in cts]
    ct_arrs, ct_specs = zip(*[_row_spec(a, tm) for grp in ct_groups for a in grp])
    cts = list(ct_arrs)
    nr, npar, nct = len(rows), len(params), len(cts)
    row_mask = row_mask or [True] * nr
    par_mask = par_mask or [True] * npar
    row_idx = [k for k in range(nr) if row_mask[k]]
    par_idx = [k for k in range(npar) if par_mask[k]]
    row_w = [specs[k].block_shape[1] for k in row_idx]

    def body(*refs):
        ins = [r[...] for r in refs[:nr + npar]]
        ct_refs = list(refs[nr + npar:nr + npar + nct])
        ctv = []
        for grp in ct_groups:
            acc = ct_refs.pop(0)[...].astype(f32)
            for _ in grp[1:]:
                acc = acc + ct_refs.pop(0)[...].astype(f32)
            ctv.append(acc)
        ctv = tuple(ctv)
        outs = refs[nr + npar + nct:]
        _, vjp = jax.vjp(fn, *ins)
        grads = vjp(ctv)
        for o_ref, k in zip(outs[:len(row_idx)], row_idx):
            o_ref[...] = grads[k].astype(o_ref.dtype)
        first = pl.program_id(0) == 0
        for o_ref, k in zip(outs[len(row_idx):], par_idx):
            g = grads[nr + k].astype(f32)

            @pl.when(first)
            def _(o_ref=o_ref, g=g):
                o_ref[...] = g

            @pl.when(jnp.logical_not(first))
            def _(o_ref=o_ref, g=g):
                o_ref[...] += g

    res = pl.pallas_call(
        body, grid=(s // tm,),
        in_specs=list(specs) + [_par_spec(p) for p in params]
        + list(ct_specs),
        out_specs=[pl.BlockSpec((tm, w), lambda i: (i, 0)) for w in row_w]
        + [_par_spec(params[k]) for k in par_idx],
        out_shape=[jax.ShapeDtypeStruct((s, w), dt) for w, dt in zip(row_w, row_dtypes or [f32] * len(row_w))]
        + [jax.ShapeDtypeStruct(params[k].shape, f32) for k in par_idx],
        compiler_params=_cparams("arbitrary"), name=name)(*arrs, *params, *cts)
    return list(res[:len(row_idx)]), list(res[len(row_idx):])


def _pick(n, prefs):
    for t in prefs:
        if n % t == 0:
            return t
    return n


MM_CHUNK = 512
MM_WIDE = 2048


def mm_nn(name, a, b, out_dtype=f32, postnorm=None):
    m, k = a.shape
    n = b.shape[1]
    tm = _pick(m, ((1024,) if n <= MM_WIDE else ()) + (512, 256, 128))
    cn = _pick(n, (MM_CHUNK, 256, 128))

    if postnorm is not None:
        h, g, beta = postnorm
        tm = min(tm, 512)

        def body_pn(a_ref, b_ref, h_ref, g_ref, be_ref, o_ref, y_ref, y16_ref):
            av = a_ref[...].astype(bf16)
            for c in range(n // cn):
                sl = slice(c * cn, (c + 1) * cn)
                o_ref[:, sl] = _dg(av, b_ref[:, sl].astype(bf16), 1, 0)
            y = _postnorm_tile(h_ref[...], o_ref[...], g_ref[...], be_ref[...])[0]
            y_ref[...] = y
            y16_ref[...] = y.astype(bf16)

        row = pl.BlockSpec((tm, n), lambda i: (i, 0))
        vec = pl.BlockSpec((1, n), lambda i: (0, 0))
        return pl.pallas_call(
            body_pn, grid=(m // tm,),
            in_specs=[pl.BlockSpec((tm, k), lambda i: (i, 0)), pl.BlockSpec((k, n), lambda i: (0, 0)), row, vec, vec],
            out_specs=[row, row, row],
            out_shape=[jax.ShapeDtypeStruct((m, n), f32), jax.ShapeDtypeStruct((m, n), f32), jax.ShapeDtypeStruct((m, n), bf16)],
            compiler_params=_cparams("parallel"), name=name)(a, b, h, g, beta)

    def body(a_ref, b_ref, o_ref):
        av = a_ref[...].astype(bf16)
        for c in range(n // cn):
            sl = slice(c * cn, (c + 1) * cn)
            o_ref[:, sl] = _dg(av, b_ref[:, sl].astype(bf16), 1, 0).astype(o_ref.dtype)

    return pl.pallas_call(
        body, grid=(m // tm,),
        in_specs=[pl.BlockSpec((tm, k), lambda i: (i, 0)), pl.BlockSpec((k, n), lambda i: (0, 0))],
        out_specs=pl.BlockSpec((tm, n), lambda i: (i, 0)),
        out_shape=jax.ShapeDtypeStruct((m, n), out_dtype),
        compiler_params=_cparams("parallel"), name=name)(a, b)


def mm_nt(name, a, b, out_dtype=f32):
    m, n = a.shape
    k = b.shape[0]
    tm = _pick(m, ((1024,) if n <= MM_WIDE else ()) + (512, 256, 128))
    ck = _pick(k, (MM_CHUNK, 256, 128))

    def body(a_ref, b_ref, o_ref):
        av = a_ref[...].astype(bf16)
        for c in range(k // ck):
            sl = slice(c * ck, (c + 1) * ck)
            o_ref[:, sl] = _dg(av, b_ref[sl, :].astype(bf16), 1, 1).astype(o_ref.dtype)

    return pl.pallas_call(
        body, grid=(m // tm,),
        in_specs=[pl.BlockSpec((tm, n), lambda i: (i, 0)), pl.BlockSpec((k, n), lambda i: (0, 0))],
        out_specs=pl.BlockSpec((tm, k), lambda i: (i, 0)),
        out_shape=jax.ShapeDtypeStruct((m, k), out_dtype),
        compiler_params=_cparams("parallel"), name=name)(a, b)


def _stacked(buf):
    if buf is None:
        return [], [], {}
    return [buf], [pl.BlockSpec(memory_space=pl.ANY)], None


def mm_tn(name, a, b, out_dtype=f32, b_col0=0, n_cols=None, stack=None):
    s, m = a.shape
    n = n_cols or b.shape[1]
    tn = _pick(n, (256, 128))
    cm = _pick(m, (256, 128))
    col0 = b_col0 // tn
    in_specs = [pl.BlockSpec((s, m), lambda j: (0, 0)), pl.BlockSpec((s, tn), lambda j: (0, j + col0))]

    if stack is None:
        def body(a_ref, b_ref, o_ref):
            bv = b_ref[...].astype(bf16)
            for c in range(m // cm):
                sl = slice(c * cm, (c + 1) * cm)
                o_ref[sl, :] = _dg(a_ref[:, sl].astype(bf16), bv, 0, 0).astype(o_ref.dtype)

        return pl.pallas_call(
            body, grid=(n // tn,), in_specs=in_specs, out_specs=pl.BlockSpec((m, tn), lambda j: (0, j)),
            out_shape=jax.ShapeDtypeStruct((m, n), out_dtype),
            compiler_params=_cparams("parallel"), name=name)(a, b)

    buf, layer, n_layers = stack
    assert cm * N_CHIPS == m
    extra, extra_specs, _ = _stacked(buf)

    def body_stacked(a_ref, b_ref, *rest):
        o_ref = rest[-1]
        bv = b_ref[...].astype(bf16)
        for c in range(N_CHIPS):
            o_ref[c] = _dg(a_ref[:, c * cm:(c + 1) * cm].astype(bf16), bv, 0, 0).astype(o_ref.dtype)

    return pl.pallas_call(
        body_stacked, grid=(n // tn,), in_specs=in_specs + extra_specs,
        out_specs=pl.BlockSpec((N_CHIPS, None, cm, tn), lambda j: (0, layer, 0, j)),
        out_shape=jax.ShapeDtypeStruct((N_CHIPS, n_layers, cm, n), out_dtype),
        input_output_aliases={2: 0} if extra else {},
        compiler_params=_cparams("parallel"), name=name)(a, b, *extra)


def _postnorm_tile(h, sub, g, b):
    z = ALPHA * h + sub
    mu = jnp.mean(z, -1, keepdims=True)
    zc = z - mu
    var = jnp.mean(zc * zc, -1, keepdims=True)
    return (zc * lax.rsqrt(var + LN_EPS) * g + b,)


def _glu_tile(og):
    o, g = og[:, :D_MODEL], og[:, D_MODEL:]
    return (o * jax.nn.sigmoid(g),)


def _xattn_tile(q, kv):
    outs = []
    for h in range(X_HEADS):
        sl = slice(h * X_HEAD_DIM, (h + 1) * X_HEAD_DIM)
        s = bdot(q[:, sl], kv[:, sl], 1, 1) * (X_HEAD_DIM ** -0.5)
        m = lax.stop_gradient(jnp.max(s, -1, keepdims=True))
        p = jnp.exp(s - m)
        p = p / jnp.sum(p, -1, keepdims=True)
        outs.append(bdot(p, kv[:, D_MODEL + h * X_HEAD_DIM:D_MODEL + (h + 1) * X_HEAD_DIM], 1, 0))
    return (jnp.concatenate(outs, -1),)


TM_ROW = 512


def postnorm_fwd(tag, h, sub, g, b):
    return rowmap("postnorm_" + tag, lambda *a: _postnorm_tile(*a) * 2, [h, sub], [g, b], [D_MODEL] * 2, TM_ROW,
                  out_dtypes=[f32, bf16])


def postnorm_bwd(tag, h, sub, g, b, dy):
    (dh, dsub), (dg, db) = rowmap_bwd("postnorm_bwd_" + tag, _postnorm_tile, [h, sub], [g, b], [dy], TM_ROW,
                                      row_dtypes=[f32, bf16])
    return dh, dsub, dg, db


def xattn_fwd(tag, h, mem, wq, wkv, wo, postnorm):
    q = mm_nn("xq_" + tag, h, wq, out_dtype=bf16)
    kv = mm_nn("xkv_" + tag, mem, wkv)
    ao = rowmap("xattn_" + tag, _xattn_tile, [q], [kv], [D_MODEL], TM_ROW, out_dtypes=[bf16])[0]
    outs = mm_nn("xo_" + tag, ao, wo, postnorm=postnorm)
    return outs, (q, kv, ao)


def xattn_bwd(tag, layer, h, mem, wq, wkv, wo, res, dout, stacks):
    q, kv, ao = res
    sq, sk, sv, so = stacks
    so = mm_tn("xo_dw_" + tag, ao, dout, stack=(so, layer, DEPTH))
    dao = mm_nt("xo_dx_" + tag, dout, wo)
    (dq,), (dkv,) = rowmap_bwd("xattn_bwd_" + tag, _xattn_tile, [q], [kv], [dao], TM_ROW, row_dtypes=[bf16])
    sq = mm_tn("xq_dw_" + tag, h, dq, stack=(sq, layer, DEPTH))
    dh = mm_nt("xq_dx_" + tag, dq, wq)
    sk = mm_tn("xk_dw_" + tag, mem, dkv, n_cols=D_MODEL, stack=(sk, layer, DEPTH))
    sv = mm_tn("xv_dw_" + tag, mem, dkv, b_col0=D_MODEL, n_cols=D_MODEL, stack=(sv, layer, DEPTH))
    return dh, (sq, sk, sv, so)


FFN_SHARD = FFN_HIDDEN // 4
TM_FFN = 512
TM_FFN_WIDE = 1024


def _silu_mul(a, u):
    return jax.nn.silu(a) * u


def ffn_fwd(tag, layer, h, wg, wu, wd, postnorm):
    s = h.shape[0]
    tm, fs = TM_FFN, FFN_SHARD
    w_in = pl.BlockSpec((None, None, fs, D_MODEL), lambda k, i: (k, layer, 0, 0))
    tu = TM_FFN_WIDE
    act = pl.BlockSpec((None, tu, fs), lambda k, i: (k, i, 0))

    def up_body(h_ref, wg_ref, wu_ref, a_ref, u_ref, hid_ref):
        hv = h_ref[...].astype(bf16)
        a, u = _dg(hv, wg_ref[...], 1, 1), _dg(hv, wu_ref[...], 1, 1)
        a_ref[...], u_ref[...] = a.astype(bf16), u.astype(bf16)
        hid_ref[...] = _silu_mul(a, u).astype(bf16)

    a4, u4, hid4 = pl.pallas_call(
        up_body, grid=(4, s // tu),
        in_specs=[pl.BlockSpec((tu, D_MODEL), lambda k, i: (i, 0)), w_in, w_in],
        out_specs=[act, act, act],
        out_shape=[jax.ShapeDtypeStruct((4, s, fs), bf16)] * 3,
        compiler_params=_cparams("parallel", "parallel"), name="ffn_up_" + tag)(h, wg, wu)

    all_act = pl.BlockSpec((4, tm, fs), lambda i: (0, i, 0))
    all_w = pl.BlockSpec((4, None, fs, D_MODEL), lambda i: (0, layer, 0, 0))

    def down_body(hid_ref, wd_ref, h_ref, g_ref, be_ref, o_ref, y_ref, y16_ref):
        acc = _dg(hid_ref[0], wd_ref[0], 1, 0)
        for k in range(1, 4):
            acc = acc + _dg(hid_ref[k], wd_ref[k], 1, 0)
        o_ref[...] = acc
        y = _postnorm_tile(h_ref[...], acc, g_ref[...], be_ref[...])[0]
        y_ref[...] = y
        y16_ref[...] = y.astype(bf16)

    h32, g, beta = postnorm
    row = pl.BlockSpec((tm, D_MODEL), lambda i: (i, 0))
    vec = pl.BlockSpec((1, D_MODEL), lambda i: (0, 0))
    outs = pl.pallas_call(
        down_body, grid=(s // tm,), in_specs=[all_act, all_w, row, vec, vec],
        out_specs=[row, row, row],
        out_shape=[jax.ShapeDtypeStruct((s, D_MODEL), f32)] * 2 + [jax.ShapeDtypeStruct((s, D_MODEL), bf16)],
        compiler_params=_cparams("parallel"), name="ffn_down_" + tag)(hid4, wd, h32, g, beta)
    return outs, (a4, u4, hid4)


def ffn_bwd(tag, layer, h, wg, wu, wd, res, dout, stacks=None):
    a4, u4, hid4 = res
    s = h.shape[0]
    tm, fs = TM_FFN, FFN_SHARD
    tu = TM_FFN_WIDE
    act = pl.BlockSpec((None, tu, fs), lambda k, i: (k, i, 0))

    def dact_body(do_ref, wd_ref, a_ref, u_ref, da_ref, du_ref):
        dhid = _dg(do_ref[...].astype(bf16), wd_ref[...], 1, 1)
        _, vjp = jax.vjp(_silu_mul, a_ref[...].astype(f32), u_ref[...].astype(f32))
        da, du = vjp(dhid)
        da_ref[...], du_ref[...] = da.astype(bf16), du.astype(bf16)

    da4, du4 = pl.pallas_call(
        dact_body, grid=(4, s // tu),
        in_specs=[pl.BlockSpec((tu, D_MODEL), lambda k, i: (i, 0)),
                  pl.BlockSpec((None, None, fs, D_MODEL), lambda k, i: (k, layer, 0, 0)), act, act],
        out_specs=[act, act], out_shape=[jax.ShapeDtypeStruct((4, s, fs), bf16)] * 2,
        compiler_params=_cparams("parallel", "parallel"), name="ffn_dact_" + tag)(dout, wd, a4, u4)

    all_act = pl.BlockSpec((4, tm, fs), lambda i: (0, i, 0))
    all_w = pl.BlockSpec((4, None, fs, D_MODEL), lambda i: (0, layer, 0, 0))

    def dx_body(da_ref, du_ref, wg_ref, wu_ref, o_ref):
        acc = _dg(da_ref[0], wg_ref[0], 1, 0) + _dg(du_ref[0], wu_ref[0], 1, 0)
        for k in range(1, 4):
            acc = acc + (_dg(da_ref[k], wg_ref[k], 1, 0) + _dg(du_ref[k], wu_ref[k], 1, 0))
        o_ref[...] = acc

    dh = pl.pallas_call(
        dx_body, grid=(s // tm,), in_specs=[all_act, all_act, all_w, all_w],
        out_specs=pl.BlockSpec((tm, D_MODEL), lambda i: (i, 0)),
        out_shape=jax.ShapeDtypeStruct((s, D_MODEL), f32),
        compiler_params=_cparams("parallel"), name="ffn_dx_" + tag)(da4, du4, wg, wu)

    tn = 256
    whole = pl.BlockSpec((None, s, fs), lambda k: (k, 0, 0))
    resident = pl.BlockSpec((s, D_MODEL), lambda k: (0, 0), pipeline_mode=pl.Buffered(1))

    def dwin_body(h_ref, da_ref, du_ref, *rest):
        dwg_ref, dwu_ref = rest[-2:]
        da, du = da_ref[...], du_ref[...]
        for c in range(D_MODEL // tn):
            sl = slice(c * tn, (c + 1) * tn)
            hv = h_ref[:, sl].astype(bf16)
            dwg_ref[:, sl] = _dg(da, hv, 0, 0)
            dwu_ref[:, sl] = _dg(du, hv, 0, 0)

    n_layers = wd.shape[1]
    layer_out = pl.BlockSpec((None, None, fs, D_MODEL), lambda k: (k, layer, 0, 0))
    stack_shape = jax.ShapeDtypeStruct((4, n_layers, fs, D_MODEL), f32)
    prev = [] if stacks is None else list(stacks)
    any_spec = [pl.BlockSpec(memory_space=pl.ANY)]

    dwg, dwu = pl.pallas_call(
        dwin_body, grid=(4,),
        in_specs=[resident, whole, whole] + any_spec * len(prev[:2]),
        out_specs=[layer_out] * 2, out_shape=[stack_shape] * 2,
        input_output_aliases={3: 0, 4: 1} if prev else {},
        compiler_params=_cparams("parallel"), name="ffn_dwin_" + tag)(h, da4, du4, *prev[:2])

    def dwd_body(hid_ref, do_ref, *rest):
        hid = hid_ref[...]
        for c in range(D_MODEL // tn):
            sl = slice(c * tn, (c + 1) * tn)
            rest[-1][:, sl] = _dg(hid, do_ref[:, sl].astype(bf16), 0, 0)

    dwd = pl.pallas_call(
        dwd_body, grid=(4,),
        in_specs=[whole, resident] + any_spec * len(prev[2:]),
        out_specs=layer_out, out_shape=stack_shape,
        input_output_aliases={2: 0} if prev else {},
        compiler_params=_cparams("parallel"), name="ffn_dwd_" + tag)(hid4, dout, *prev[2:])
    return dh, (dwg, dwu, dwd)


def loss_head(y, target):
    s, d = y.shape
    tm = TM_ROW

    def body(y_ref, t_ref, part_ref, dy_ref):
        e = y_ref[...] - t_ref[...]
        dy_ref[...] = e * (1.0 / d)
        p = jnp.sum(e * e, 0, keepdims=True) * (0.5 / d)

        @pl.when(pl.program_id(0) == 0)
        def _():
            part_ref[...] = p

        @pl.when(pl.program_id(0) != 0)
        def _():
            part_ref[...] += p

    return pl.pallas_call(
        body, grid=(s // tm,),
        in_specs=[pl.BlockSpec((tm, d), lambda i: (i, 0))] * 2,
        out_specs=[pl.BlockSpec((1, d), lambda i: (0, 0)), pl.BlockSpec((tm, d), lambda i: (i, 0))],
        out_shape=[jax.ShapeDtypeStruct((1, d), f32), jax.ShapeDtypeStruct((s, d), f32)],
        compiler_params=_cparams("arbitrary"), name="loss_head")(y, target)


TM_CONV = 1024


def _conv_rows(xx, w_ref, n_rows):
    a = w_ref[3:4, :] * xx
    for k in (1, 2, 3):
        a = a + w_ref[3 - k:4 - k, :] * pltpu.roll(xx, k, 0)
    return a


def _dn_act(a, is_qk):
    s = jax.nn.silu(a)
    n = s * lax.rsqrt(jnp.sum(s * s, -1, keepdims=True) + RMS_EPS)
    return jnp.where(is_qk, n, s)


def dn_conv_fwd(tag, proj, cw):
    s = proj.shape[0]
    tm, hb = TM_CONV, TM_CONV // SUBLANES

    def body(xh_ref, x_ref, w_ref, o_ref):
        j, t = pl.program_id(0), pl.program_id(1)
        halo = jnp.where(t > 0, xh_ref[...], 0.0)
        xx = jnp.concatenate([halo, x_ref[...]], 0)
        a = _conv_rows(xx, w_ref, tm + SUBLANES)
        o_ref[...] = _dn_act(a, j < 2 * DN_HEADS)[SUBLANES:, :]

    return pl.pallas_call(
        body, grid=(DN_QKV_DIM // LANES, s // tm),
        in_specs=[pl.BlockSpec((SUBLANES, LANES), lambda j, t: (jnp.maximum(t * hb - 1, 0), j)),
                  pl.BlockSpec((tm, LANES), lambda j, t: (t, j)),
                  pl.BlockSpec((DN_CONV, LANES), lambda j, t: (0, j))],
        out_specs=pl.BlockSpec((tm, LANES), lambda j, t: (t, j)),
        out_shape=jax.ShapeDtypeStruct((s, DN_QKV_DIM), f32),
        compiler_params=_cparams("parallel", "parallel"), name="dn_conv_" + tag)(proj, proj, cw)


def dn_conv_bwd(tag, proj, cw, dy):
    s = proj.shape[0]
    tm, hb = TM_CONV, TM_CONV // SUBLANES
    nt = s // tm
    n_ext = tm + 2 * SUBLANES

    def body(xb_ref, x_ref, xa_ref, dy_ref, dya_ref, w_ref, dx_ref, dw_ref):
        j, t = pl.program_id(0), pl.program_id(1)
        xx = jnp.concatenate([jnp.where(t > 0, xb_ref[...], 0.0), x_ref[...],
                              jnp.where(t < nt - 1, xa_ref[...], 0.0)], 0)
        dyy = jnp.concatenate([jnp.zeros((SUBLANES, LANES), f32), dy_ref[...],
                               jnp.where(t < nt - 1, dya_ref[...], 0.0)], 0)
        a = _conv_rows(xx, w_ref, n_ext)
        _, vjp = jax.vjp(lambda v: _dn_act(v, j < 2 * DN_HEADS), a)
        da, = vjp(dyy)
        dx = w_ref[3:4, :] * da
        for k in (1, 2, 3):
            dx = dx + w_ref[3 - k:4 - k, :] * pltpu.roll(da, n_ext - k, 0)
        dx_ref[...] = dx[SUBLANES:SUBLANES + tm, :]
        row = _iota2((n_ext, LANES), 0)
        da_in = jnp.where((row >= SUBLANES) & (row < SUBLANES + tm), da, 0.0)
        r8 = _iota2((SUBLANES, LANES), 0)
        dw = jnp.zeros((SUBLANES, LANES), f32)
        for k in range(DN_CONV):
            xs = xx if k == 0 else pltpu.roll(xx, k, 0)
            dw = dw + jnp.where(r8 == 3 - k, jnp.sum(da_in * xs, 0, keepdims=True), 0.0)

        @pl.when(t == 0)
        def _():
            dw_ref[...] = dw

        @pl.when(t != 0)
        def _():
            dw_ref[...] += dw

    nb8 = s // SUBLANES
    return pl.pallas_call(
        body, grid=(DN_QKV_DIM // LANES, nt),
        in_specs=[pl.BlockSpec((SUBLANES, LANES), lambda j, t: (jnp.maximum(t * hb - 1, 0), j)),
                  pl.BlockSpec((tm, LANES), lambda j, t: (t, j)),
                  pl.BlockSpec((SUBLANES, LANES), lambda j, t: (jnp.minimum((t + 1) * hb, nb8 - 1), j)),
                  pl.BlockSpec((tm, LANES), lambda j, t: (t, j)),
                  pl.BlockSpec((SUBLANES, LANES), lambda j, t: (jnp.minimum((t + 1) * hb, nb8 - 1), j)),
                  pl.BlockSpec((DN_CONV, LANES), lambda j, t: (0, j))],
        out_specs=[pl.BlockSpec((tm, LANES), lambda j, t: (t, j)),
                   pl.BlockSpec((SUBLANES, LANES), lambda j, t: (0, j))],
        out_shape=[jax.ShapeDtypeStruct((s, DN_QKV_DIM), f32), jax.ShapeDtypeStruct((SUBLANES, DN_QKV_DIM), f32)],
        compiler_params=_cparams("parallel", "arbitrary"), name="dn_conv_bwd_" + tag)(proj, proj, proj, dy, dy, cw)


def _gate_tile(ba, eb, ea, alog, dtb):
    beta = jax.nn.sigmoid(hdot(ba, eb))
    g = -jnp.exp(alog) * jax.nn.softplus(hdot(ba, ea) + dtb)
    return beta, g


def _each(fn, *lists):
    return [fn(*args) for args in zip(*lists)]


@functools.partial(jax.custom_vjp, nondiff_argnums=(1,))
def _halves(x, axis):
    h = x.shape[axis] // 2
    return (x[:h], x[h:]) if axis == 0 else (x[:, :h], x[:, h:])


def _halves_fwd(x, axis):
    return _halves(x, axis), None


def _halves_bwd(axis, _, g):
    return (jnp.concatenate(g, axis),)


_halves.defvjp(_halves_fwd, _halves_bwd)


def _tri_inv_unit(lowers):
    c = lowers[0].shape[0]
    r, col = _iota2((c, c), 0), _iota2((c, c), 1)
    eye = jnp.where(r == col, 1.0, 0.0).astype(f32)
    invs = None
    sh = 0
    while (1 << sh) < c:
        same_2b = lax.shift_right_logical(r, sh + 1) == lax.shift_right_logical(col, sh + 1)
        diff_b = lax.shift_right_logical(r, sh) != lax.shift_right_logical(col, sh)
        offs = [jnp.where(same_2b & diff_b, low, 0.0) for low in lowers]
        if invs is None:
            invs = [eye - off for off in offs]
        else:
            part = _each(lambda inv, off: _dot3(inv, off, 1, 0), invs, offs)
            invs = _each(lambda inv, p: inv - _dot3(p, inv, 1, 0), invs, part)
        sh += 1
    return invs


@jax.custom_vjp
def _known_inverse(lower, tinv):
    return tinv


def _known_inverse_fwd(lower, tinv):
    return tinv, tinv


def _known_inverse_bwd(tinv, g):
    tt = tinv.T
    return -hdot(hdot(tt, g), tt), jnp.zeros_like(tinv)


_known_inverse.defvjp(_known_inverse_fwd, _known_inverse_bwd)


def _delta_chunk(q, k, v, gb, betab, state, tinv_known=None):
    c, hd = DN_CHUNK, DN_HEAD_DIM
    r, col = _iota2((c, c), 0), _iota2((c, c), 1)
    causal, strict = r >= col, r > col
    tril = jnp.where(causal, 1.0, 0.0).astype(f32)
    gc = _each(lambda g: hdot(tril, g), gb)
    decay = _each(lambda g: jnp.where(causal, jnp.exp(jnp.where(causal, g - g.T, 0.0)), 0.0), gc)
    qs = _each(lambda t: t * (DN_HEAD_DIM ** -0.5), q)
    kb = _each(lambda a, b: a * b, k, betab)
    kq = _each(lambda a, b, kk: _halves(bdot(jnp.concatenate([a, b], 0), kk, 1, 1), 0), kb, qs, k)
    lower = _each(lambda x, d: jnp.where(strict, x[0], 0.0) * d, kq, decay)
    intra = _each(lambda x, d: x[1] * d, kq, decay)
    tinv = _tri_inv_unit(lower) if tinv_known is None else _each(_known_inverse, lower, tinv_known)
    eg = _each(jnp.exp, gc)
    uw = _each(lambda t, vv, b, kb_, e: _halves(hdot(t, jnp.concatenate([vv * b, kb_ * e], 1)), 1),
               tinv, v, betab, kb, eg)
    gl = _each(lambda g: jnp.sum(jnp.where(r == c - 1, g, 0.0), 0, keepdims=True), gc)
    k_dec = _each(lambda kk, a, g: kk * jnp.exp(a - g), k, gl, gc)
    ws = _each(lambda x, t, e, st: _halves(bdot(jnp.concatenate([x[1], t * e], 0), st, 1, 0), 0), uw, qs, eg, state)
    v_new = _each(lambda x, y: x[0] - y[0], uw, ws)
    out = _each(lambda y, a, vn: y[1] + bdot(a, vn, 1, 0), ws, intra, v_new)
    new_state = _each(lambda st, a, kd, vn: st * jnp.exp(a) + bdot(kd, vn, 0, 0), state, gl, k_dec, v_new)
    return tuple(out), tuple(new_state), tuple(tinv)


def delta_fwd(tag, qkv, gb, betab):
    s = qkv.shape[0]
    c, hd = DN_CHUNK, DN_HEAD_DIM
    n = s // c

    hg, ng = DN_HEADS_PER_STEP, DN_HEADS // DN_HEADS_PER_STEP

    def body(q_ref, k_ref, v_ref, g_ref, b_ref, o_ref, st_ref, ti_ref, state):
        @pl.when(pl.program_id(1) == 0)
        def _():
            state[...] = jnp.zeros_like(state)

        heads = lambda ref: tuple(ref[:, j * hd:(j + 1) * hd] for j in range(hg))
        st = tuple(state[j] for j in range(hg))
        outs, news, tinv = _delta_chunk(heads(q_ref), heads(k_ref), heads(v_ref), heads(g_ref), heads(b_ref), st)
        for j in range(hg):
            st_ref[j] = st[j]
            ti_ref[j] = tinv[j]
            o_ref[:, j * hd:(j + 1) * hd] = outs[j]
            state[j] = news[j]

    blk = lambda off: pl.BlockSpec((c, hg * hd), lambda h, i, _o=off: (i, h + _o))
    per_chunk = pl.BlockSpec((hg, None, hd, hd), lambda h, i: (h, i, 0, 0))
    return pl.pallas_call(
        body, grid=(ng, n),
        in_specs=[blk(0), blk(ng), blk(2 * ng), blk(0), blk(0)],
        out_specs=[blk(0), per_chunk, per_chunk],
        out_shape=[jax.ShapeDtypeStruct((s, DN_KEY_DIM), f32)] + [jax.ShapeDtypeStruct((DN_HEADS, n, hd, hd), f32)] * 2,
        scratch_shapes=[pltpu.VMEM((hg, hd, hd), f32)],
        compiler_params=_cparams("parallel", "arbitrary"), name="delta_" + tag)(qkv, qkv, qkv, gb, betab)


def delta_bwd(tag, qkv, gb, betab, states, tinvs, do):
    s = qkv.shape[0]
    c, hd = DN_CHUNK, DN_HEAD_DIM
    n = s // c

    hg, ng = DN_HEADS_PER_STEP, DN_HEADS // DN_HEADS_PER_STEP

    def body(q_ref, k_ref, v_ref, g_ref, b_ref, st_ref, ti_ref, do_ref, dqkv_ref, dg_ref, db_ref, dstate):
        @pl.when(pl.program_id(1) == 0)
        def _():
            dstate[...] = jnp.zeros_like(dstate)

        heads = lambda ref: tuple(ref[:, j * hd:(j + 1) * hd] for j in range(hg))
        tinv = tuple(ti_ref[j] for j in range(hg))
        _, vjp = jax.vjp(lambda *args: _delta_chunk(*args, tinv_known=tinv)[:2],
                         heads(q_ref), heads(k_ref), heads(v_ref), heads(g_ref), heads(b_ref),
                         tuple(st_ref[j] for j in range(hg)))
        grads = vjp((heads(do_ref), tuple(dstate[j] for j in range(hg))))
        for part, g in enumerate(grads[:3]):
            for j in range(hg):
                dqkv_ref[:, part * DN_KEY_DIM + j * hd:part * DN_KEY_DIM + (j + 1) * hd] = g[j]
        for ref, g in zip((dg_ref, db_ref), grads[3:5]):
            for j in range(hg):
                ref[:, j * hd:(j + 1) * hd] = g[j]
        for j in range(hg):
            dstate[j] = grads[5][j]

    assert ng == 1
    blk = lambda off: pl.BlockSpec((c, hg * hd), lambda h, i, _o=off: (n - 1 - i, h + _o))
    return pl.pallas_call(
        body, grid=(ng, n),
        in_specs=[blk(0), blk(ng), blk(2 * ng), blk(0), blk(0)]
        + [pl.BlockSpec((hg, None, hd, hd), lambda h, i: (h, n - 1 - i, 0, 0))] * 2 + [blk(0)],
        out_specs=[pl.BlockSpec((c, DN_QKV_DIM), lambda h, i: (n - 1 - i, 0)), blk(0), blk(0)],
        out_shape=[jax.ShapeDtypeStruct((s, DN_QKV_DIM), f32)] + [jax.ShapeDtypeStruct((s, DN_KEY_DIM), f32)] * 2,
        scratch_shapes=[pltpu.VMEM((hg, hd, hd), f32)],
        compiler_params=_cparams("parallel", "arbitrary"),
        name="delta_bwd_" + tag)(qkv, qkv, qkv, gb, betab, states, tinvs, do)


def _dn_out_tile(o, z, ng):
    outs = []
    for h in range(DN_HEADS):
        sl = slice(h * DN_HEAD_DIM, (h + 1) * DN_HEAD_DIM)
        oh = o[:, sl]
        nrm = oh * lax.rsqrt(jnp.mean(oh * oh, -1, keepdims=True) + RMS_EPS) * ng[:, sl]
        outs.append(nrm * jax.nn.silu(z[:, sl]))
    return (jnp.concatenate(outs, -1),)


def _head_selectors():
    r, c = _iota2((BA_PAD, DN_KEY_DIM), 0), _iota2((BA_PAD, DN_KEY_DIM), 1) // DN_HEAD_DIM
    return (r == c).astype(f32), (r == c + DN_HEADS).astype(f32)


def dn_mixer_fwd(tag, proj, cw, alog_b, dtb_b, ng_b):
    eb, ea = _head_selectors()
    ba = (proj, BA_PAD, COL_BA // BA_PAD)
    qkv = dn_conv_fwd(tag, proj, cw)
    betab, gb = rowmap("dn_gate_" + tag, _gate_tile, [ba], [eb, ea, alog_b, dtb_b], [DN_KEY_DIM] * 2, TM_ROW)
    o, states, tinvs = delta_fwd(tag, qkv, gb, betab)
    z = (proj, DN_KEY_DIM, COL_Z // DN_KEY_DIM)
    a_out = rowmap("dn_out_" + tag, _dn_out_tile, [o, z], [ng_b], [DN_KEY_DIM], TM_ROW)[0]
    return a_out, (qkv, betab, gb, o, states, tinvs)


def dn_mixer_bwd(tag, proj, cw, alog_b, dtb_b, ng_b, res, da_out):
    qkv, betab, gb, o, states, tinvs = res
    eb, ea = _head_selectors()
    ba = (proj, BA_PAD, COL_BA // BA_PAD)
    z = (proj, DN_KEY_DIM, COL_Z // DN_KEY_DIM)
    (do, dz), (dng,) = rowmap_bwd("dn_out_bwd_" + tag, _dn_out_tile, [o, z], [ng_b], [da_out], TM_ROW)
    dqkv, dgb, dbetab = delta_bwd(tag, qkv, gb, betab, states, tinvs, do)
    dqkv_raw, dcw = dn_conv_bwd(tag, proj, cw, dqkv)
    (dba,), (dalog, ddtb) = rowmap_bwd("dn_gate_bwd_" + tag, _gate_tile, [ba], [eb, ea, alog_b, dtb_b],
                                       [dbetab, dgb], TM_ROW, par_mask=[False, False, True, True])
    return dqkv_raw, dz, dba, dcw[:DN_CONV], dalog, ddtb, dng


def _swap_halves(x):
    n = x.shape[1]
    first = (_iota2((1, n), 1) % SW_HEAD_DIM) < SW_HEAD_DIM // 2
    return jnp.where(first, pltpu.roll(x, n - SW_HEAD_DIM // 2, 1), pltpu.roll(x, SW_HEAD_DIM // 2, 1))


def _rope_apply(x, cos, sin_signed):
    return x * cos + _swap_halves(x) * sin_signed


def _rope_transpose(dy, cos, sin_signed):
    return dy * cos + _swap_halves(dy * sin_signed)


def rope_tables(positions, s):
    half = SW_HEAD_DIM // 2
    inv_freq = ROPE_THETA ** (-jnp.arange(0, SW_HEAD_DIM, 2, dtype=f32) / SW_HEAD_DIM)
    ang = positions.reshape(s, 1).astype(f32) * inv_freq[None, :]
    cos, sin = jnp.cos(ang), jnp.sin(ang)
    cos_t = jnp.tile(jnp.concatenate([cos, cos], 1), (1, SW_HEADS))
    sin_t = jnp.tile(jnp.concatenate([-sin, sin], 1), (1, SW_HEADS))
    assert cos_t.shape == (s, SW_DIM) and half * 2 == SW_HEAD_DIM
    return cos_t, sin_t


def rope_fwd(tag, proj, cos, sin):
    def fn(q, k, v, c, sg):
        return _rope_apply(q, c, sg), _rope_apply(k, c, sg), v

    rows = [(proj, SW_DIM, COL_SWQ // SW_DIM), (proj, SW_DIM, COL_SWK // SW_DIM), (proj, SW_DIM, COL_SWV // SW_DIM), cos, sin]
    return rowmap("rope_" + tag, fn, rows, [], [SW_DIM] * 3, TM_ROW, out_dtypes=[bf16] * 3)


def _swa_block(q, kp, kc, vp, vc, first):
    blk = SW_BLOCK
    kk = jnp.concatenate([kp, kc], 0)
    vv = jnp.concatenate([vp, vc], 0)
    dist = (_iota2((blk, 2 * blk), 0) + blk) - _iota2((blk, 2 * blk), 1)
    kj = _iota2((blk, 2 * blk), 1)
    valid = (dist >= 0) & (dist <= blk) & ((kj >= blk) | jnp.logical_not(first))
    lane_head = _iota2((1, LANES), 1) // SW_HEAD_DIM
    outs, lses = [], []
    for p in range(SW_DIM // LANES):
        sl = slice(p * LANES, (p + 1) * LANES)
        qp, kp_, vp_ = q[:, sl], kk[:, sl], vv[:, sl]
        o_pair = jnp.zeros((blk, LANES), f32)
        l_pair = jnp.zeros((blk, LANES), f32)
        for e in range(LANES // SW_HEAD_DIM):
            msk = lane_head == e
            sc = bdot(jnp.where(msk, qp, 0.0), kp_, 1, 1) * (SW_HEAD_DIM ** -0.5)
            sc = jnp.where(valid, sc, -1e30)
            m = lax.stop_gradient(jnp.max(sc, -1, keepdims=True))
            pe = jnp.exp(sc - m)
            l = jnp.sum(pe, -1, keepdims=True)
            o = bdot(pe, vp_, 1, 0) / l
            o_pair = o_pair + jnp.where(msk, o, 0.0)
            l_pair = l_pair + jnp.where(msk, m + jnp.log(l), 0.0)
        outs.append(o_pair)
        lses.append(l_pair)
    return jnp.concatenate(outs, -1), jnp.concatenate(lses, -1)


def _swa_specs(r):
    cur = pl.BlockSpec((SW_BLOCK, SW_DIM), lambda rho, n: (n, rho))
    prev = pl.BlockSpec((SW_BLOCK, SW_DIM), lambda rho, n: (jnp.maximum(n - 1, 0), rho))
    return cur, prev


def swa_fwd(tag, r, q, k, v):
    s = q.shape[0]
    ln = s // r
    q2, k2, v2 = (t.reshape(ln, r * SW_DIM) for t in (q, k, v))
    cur, prev = _swa_specs(r)

    def body(q_ref, kp_ref, kc_ref, vp_ref, vc_ref, o_ref, l_ref):
        ins = [r[...].astype(f32) for r in (q_ref, kp_ref, kc_ref, vp_ref, vc_ref)]
        o, l = _swa_block(*ins, pl.program_id(1) == 0)
        o_ref[...] = o
        l_ref[...] = l

    o, l = pl.pallas_call(
        body, grid=(r, ln // SW_BLOCK),
        in_specs=[cur, prev, cur, prev, cur], out_specs=[cur, cur],
        out_shape=[jax.ShapeDtypeStruct((ln, r * SW_DIM), f32)] * 2,
        compiler_params=_cparams("parallel", "parallel"), name=f"swa{r}_{tag}")(q2, k2, k2, v2, v2)
    return o.reshape(s, SW_DIM), l.reshape(s, SW_DIM)


def swa_bwd(tag, r, q, k, v, do, dl):
    s = q.shape[0]
    ln = s // r
    q2, k2, v2, do2, dl2 = (t.reshape(ln, r * SW_DIM) for t in (q, k, v, do, dl))
    cur, prev = _swa_specs(r)

    def body(q_ref, kp_ref, kc_ref, vp_ref, vc_ref, do_ref, dl_ref, dq_ref, dka_ref, dkb_ref, dva_ref, dvb_ref):
        first = pl.program_id(1) == 0
        ins = [r[...].astype(f32) for r in (q_ref, kp_ref, kc_ref, vp_ref, vc_ref)]
        _, vjp = jax.vjp(lambda *a: _swa_block(*a, first), *ins)
        dq_ref[...], dka_ref[...], dkb_ref[...], dva_ref[...], dvb_ref[...] = vjp((do_ref[...], dl_ref[...]))

    outs = pl.pallas_call(
        body, grid=(r, ln // SW_BLOCK),
        in_specs=[cur, prev, cur, prev, cur, cur, cur], out_specs=[cur] * 5,
        out_shape=[jax.ShapeDtypeStruct((ln, r * SW_DIM), f32)] * 5,
        compiler_params=_cparams("parallel", "parallel"), name=f"swa{r}_bwd_{tag}")(q2, k2, k2, v2, v2, do2, dl2)
    return [t.reshape(s, SW_DIM) for t in outs]


def _combine_tile(o1, l1, o2, l2, o3, l3):
    m = lax.stop_gradient(jnp.maximum(jnp.maximum(l1, l2), l3))
    e1, e2, e3 = jnp.exp(l1 - m), jnp.exp(l2 - m), jnp.exp(l3 - m)
    return ((o1 * e1 + o2 * e2 + o3 * e3) / (e1 + e2 + e3),)


def swa_merge_bwd(tag, grads, cos, sin):
    s = cos.shape[0]
    tm = SW_BLOCK
    nt = s // tm
    here = pl.BlockSpec((tm, SW_DIM), lambda i: (i, 0))
    arrs, specs = [], []
    for r, g in zip(SW_DILATIONS, grads):
        ahead = pl.BlockSpec((tm, SW_DIM), lambda i, _r=r: (jnp.minimum(i + _r, nt - 1), 0))
        arrs += g
        specs += [here, ahead, here, ahead, here]

    def body(*refs):
        i = pl.program_id(0)
        c_ref, s_ref = refs[15], refs[16]
        dq_ref, dk_ref, dv_ref = refs[17:]
        dq = jnp.zeros((tm, SW_DIM), f32)
        dk = jnp.zeros((tm, SW_DIM), f32)
        dv = jnp.zeros((tm, SW_DIM), f32)
        for b, r in enumerate(SW_DILATIONS):
            gq, gka, gkb, gva, gvb = refs[5 * b:5 * b + 5]
            inside = i + r < nt
            dq = dq + gq[...]
            dk = dk + gkb[...] + jnp.where(inside, gka[...], 0.0)
            dv = dv + gvb[...] + jnp.where(inside, gva[...], 0.0)
        dq_ref[...] = _rope_transpose(dq, c_ref[...], s_ref[...])
        dk_ref[...] = _rope_transpose(dk, c_ref[...], s_ref[...])
        dv_ref[...] = dv

    return pl.pallas_call(
        body, grid=(nt,), in_specs=specs + [here, here], out_specs=[here] * 3,
        out_shape=[jax.ShapeDtypeStruct((s, SW_DIM), f32)] * 3,
        compiler_params=_cparams("parallel"), name="swa_merge_bwd_" + tag)(*arrs, cos, sin)


def swa_mixer_fwd(tag, proj, cos, sin):
    q, k, v = rope_fwd(tag, proj, cos, sin)
    ols = []
    for r in SW_DILATIONS:
        ols += list(swa_fwd(tag, r, q, k, v))
    b_out = rowmap("swa_comb_" + tag, _combine_tile, ols, [], [SW_DIM], TM_ROW)[0]
    return b_out, (q, k, v, ols)


def swa_mixer_bwd(tag, cos, sin, res, db_out):
    q, k, v, ols = res
    dols, _ = rowmap_bwd("swa_comb_bwd_" + tag, _combine_tile, ols, [], [db_out], TM_ROW)
    grads = [swa_bwd(tag, r, q, k, v, dols[2 * b], dols[2 * b + 1]) for b, r in enumerate(SW_DILATIONS)]
    return swa_merge_bwd(tag, grads, cos, sin)


TM_S5 = 256
S5_GPB = LANES // S5_GROUP
S5_NBLK = D_MODEL // LANES
S5_HALF = S5_GPB * S5_STATE
S5_BW = 2 * S5_HALF
S5_WIDTH = S5_NBLK * S5_BW
S5_TABW = S5_NBLK * S5_HALF


def _s5_disc_tile(a_re, a_im, log_dt, b_re, b_im, expand):
    dt = jnp.exp(log_dt)
    mag = jnp.exp(a_re * dt)
    abar_re, abar_im = mag * jnp.cos(a_im * dt), mag * jnp.sin(a_im * dt)
    n_re, n_im = abar_re - 1.0, abar_im
    den = a_re * a_re + a_im * a_im
    c_re = (n_re * a_re + n_im * a_im) / den
    c_im = (n_im * a_re - n_re * a_im) / den
    cx_re, cx_im = hdot(c_re, expand), hdot(c_im, expand)
    return abar_re, abar_im, cx_re * b_re - cx_im * b_im, cx_re * b_im + cx_im * b_re


def _s5_expand():
    return (_iota2((S5_STATE, S5_STATE * S5_GROUP), 1) // S5_GROUP == _iota2((S5_STATE, S5_STATE * S5_GROUP), 0)).astype(f32)


def s5_tables(a_re, a_im, log_dt):
    lanes = lambda v: v.reshape(1, S5_TABW)
    dt = jnp.broadcast_to(log_dt.reshape(S5_GROUPS, 1), (S5_GROUPS, S5_STATE))
    t = TM_S5

    def body(are_ref, aim_ref, ldt_ref, ar_ref, ai_ref, arr_ref, air_ref):
        dtv = jnp.exp(ldt_ref[...])
        lre, lim = are_ref[...] * dtv, aim_ref[...] * dtv
        row = _iota2((t, S5_HALF), 0)
        for asc, o_re, o_im in ((True, ar_ref, ai_ref), (False, arr_ref, air_ref)):
            n = (row + 1 if asc else t - row).astype(f32)
            mag = jnp.exp(n * lre)
            o_re[...] = mag * jnp.cos(n * lim)
            o_im[...] = mag * jnp.sin(n * lim)

    lane = pl.BlockSpec((1, S5_HALF), lambda j: (0, j))
    tab = pl.BlockSpec((t, S5_HALF), lambda j: (0, j))
    return pl.pallas_call(
        body, grid=(S5_NBLK,), in_specs=[lane] * 3, out_specs=[tab] * 4,
        out_shape=[jax.ShapeDtypeStruct((t, S5_TABW), f32)] * 4,
        compiler_params=_cparams("parallel"), name="s5_tables")(lanes(a_re), lanes(a_im), lanes(dt))


def s5_pack_weights(bbar_re, bbar_im, c_re, c_im):
    eye = jnp.eye(S5_GPB, dtype=f32)
    bb = jnp.stack([bbar_re.reshape(S5_GROUPS, S5_STATE, S5_GROUP), bbar_im.reshape(S5_GROUPS, S5_STATE, S5_GROUP)], 1)
    bb = bb.transpose(0, 3, 1, 2).reshape(S5_NBLK, S5_GPB, S5_GROUP, 2, S5_STATE)
    wb = (bb[:, :, :, :, None, :] * eye[None, :, None, None, :, None]).reshape(S5_NBLK, LANES, S5_BW)
    cc = jnp.stack([c_re, -c_im], 1)
    cc = cc.reshape(S5_NBLK, S5_GPB, 2, S5_GROUP, S5_STATE).transpose(0, 2, 1, 4, 3)
    wc = (cc[:, :, :, :, None, :] * eye[None, None, :, None, :, None]).reshape(S5_NBLK, S5_BW, LANES)
    return wb, wc


def s5_unpack_weight_grads(dwb, dwc):
    d6 = dwb.reshape(S5_NBLK, S5_GPB, S5_GROUP, 2, S5_GPB, S5_STATE)
    dbb = jnp.stack([d6[:, gl, :, :, gl, :] for gl in range(S5_GPB)])
    dbb = dbb.transpose(1, 0, 3, 4, 2).reshape(S5_GROUPS, 2, S5_STATE * S5_GROUP)
    c6 = dwc.reshape(S5_NBLK, 2, S5_GPB, S5_STATE, S5_GPB, S5_GROUP)
    dcc = jnp.stack([c6[:, :, gl, :, gl, :] for gl in range(S5_GPB)])
    dcc = dcc.transpose(1, 0, 2, 4, 3).reshape(S5_GROUPS, 2, S5_GROUP, S5_STATE)
    return dbb[:, 0], dbb[:, 1], dcc[:, 0], -dcc[:, 1]


def _s5_step_rows(t):
    d, out = 1, []
    while d < t:
        out.append(d)
        d *= 2
    return out


def s5_core_fwd(tag, u, wb, wc, a1, a2, dskip):
    s = u.shape[0]
    t = TM_S5

    def body(u_ref, wb_ref, wc_ref, ar_ref, ai_ref, d_ref, y_ref, x_ref, carry):
        @pl.when(pl.program_id(1) == 0)
        def _():
            carry[...] = jnp.zeros_like(carry)

        uv = u_ref[...]
        bu = bdot(uv, wb_ref[...], 1, 0)
        row = _iota2((t, LANES), 0)
        for c in range(S5_HALF // LANES):
            re, im = slice(c * LANES, (c + 1) * LANES), slice(S5_HALF + c * LANES, S5_HALF + (c + 1) * LANES)
            xr, xi = bu[:, re], bu[:, im]
            for d in _s5_step_rows(t):
                ar, ai = ar_ref[d - 1:d, re], ai_ref[d - 1:d, re]
                if d % SUBLANES:
                    keep = row >= d
                    sr = jnp.where(keep, pltpu.roll(xr, d, 0), 0.0)
                    si = jnp.where(keep, pltpu.roll(xi, d, 0), 0.0)
                    xr, xi = xr + ar * sr - ai * si, xi + ar * si + ai * sr
                else:
                    sr, si = xr[:t - d], xi[:t - d]
                    xr = jnp.concatenate([xr[:d], xr[d:] + (ar * sr - ai * si)], 0)
                    xi = jnp.concatenate([xi[:d], xi[d:] + (ar * si + ai * sr)], 0)
            cr, ci = carry[:, re], carry[:, im]
            ar, ai = ar_ref[:, re], ai_ref[:, re]
            x_ref[:, re] = xr + ar * cr - ai * ci
            x_ref[:, im] = xi + ar * ci + ai * cr
        carry[...] = x_ref[t - 1:t, :]
        y_ref[...] = bdot(x_ref[...], wc_ref[...], 1, 0) + d_ref[...] * uv

    tab = pl.BlockSpec((t, S5_HALF), lambda j, i: (0, j))
    return pl.pallas_call(
        body, grid=(S5_NBLK, s // t),
        in_specs=[pl.BlockSpec((t, LANES), lambda j, i: (i, j)),
                  pl.BlockSpec((None, LANES, S5_BW), lambda j, i: (j, 0, 0)),
                  pl.BlockSpec((None, S5_BW, LANES), lambda j, i: (j, 0, 0)),
                  tab, tab, pl.BlockSpec((1, LANES), lambda j, i: (0, j))],
        out_specs=[pl.BlockSpec((t, LANES), lambda j, i: (i, j)), pl.BlockSpec((t, S5_BW), lambda j, i: (i, j))],
        out_shape=[jax.ShapeDtypeStruct((s, D_MODEL), f32), jax.ShapeDtypeStruct((s, S5_WIDTH), f32)],
        scratch_shapes=[pltpu.VMEM((1, S5_BW), f32)],
        compiler_params=_cparams("parallel", "arbitrary"), name="s5_core_" + tag)(u, wb, wc, a1, a2, dskip)


def s5_core_bwd(tag, u, x, wb, wc, a1, a2, a1r, a2r, dskip, dy):
    s = u.shape[0]
    t = TM_S5
    nt = s // t
    hb = t // SUBLANES

    def body(u_ref, dy_ref, x_ref, xh_ref, wb_ref, wc_ref, ar_ref, ai_ref, arr_ref, air_ref, d_ref,
             du_ref, dwb_ref, dwc_ref, dd_ref, q1_ref, q2_ref, carry, lam_scr):
        i = pl.program_id(1)
        tt = nt - 1 - i

        @pl.when(i == 0)
        def _():
            carry[...] = jnp.zeros_like(carry)

        uv, dyv, xv = u_ref[...], dy_ref[...], x_ref[...]
        lam = bdot(dyv, wc_ref[...], 1, 1)
        row = _iota2((t, LANES), 0)
        x_last = jnp.where(tt > 0, xh_ref[SUBLANES - 1:SUBLANES, :], 0.0)
        q1s, q2s = [], []
        for c in range(S5_HALF // LANES):
            re, im = slice(c * LANES, (c + 1) * LANES), slice(S5_HALF + c * LANES, S5_HALF + (c + 1) * LANES)
            lr, li = lam[:, re], lam[:, im]
            for d in _s5_step_rows(t):
                ar, ai = ar_ref[d - 1:d, re], ai_ref[d - 1:d, re]
                if d % SUBLANES:
                    keep = row < t - d
                    sr = jnp.where(keep, pltpu.roll(lr, t - d, 0), 0.0)
                    si = jnp.where(keep, pltpu.roll(li, t - d, 0), 0.0)
                    lr, li = lr + ar * sr + ai * si, li + ar * si - ai * sr
                else:
                    sr, si = lr[d:], li[d:]
                    lr = jnp.concatenate([lr[:t - d] + (ar * sr + ai * si), lr[t - d:]], 0)
                    li = jnp.concatenate([li[:t - d] + (ar * si - ai * sr), li[t - d:]], 0)
            cr, ci = carry[:, re], carry[:, im]
            ar, ai = arr_ref[:, re], air_ref[:, re]
            lr, li = lr + ar * cr + ai * ci, li + ar * ci - ai * cr
            lam_scr[:, re] = lr
            lam_scr[:, im] = li
            pr = jnp.where(row == 0, x_last[:, re], pltpu.roll(xv[:, re], 1, 0))
            pi = jnp.where(row == 0, x_last[:, im], pltpu.roll(xv[:, im], 1, 0))
            p1, p2 = lr * pr + li * pi, li * pr - lr * pi
            q1, q2 = p1[:SUBLANES, :], p2[:SUBLANES, :]
            for k in range(1, hb):
                q1 = q1 + p1[k * SUBLANES:(k + 1) * SUBLANES, :]
                q2 = q2 + p2[k * SUBLANES:(k + 1) * SUBLANES, :]
            q1s.append(q1)
            q2s.append(q2)
        carry[...] = lam_scr[0:1, :]
        lam = lam_scr[...]
        du_ref[...] = bdot(lam, wb_ref[...], 1, 1) + d_ref[...] * dyv
        upd = [(dwb_ref, bdot(uv, lam, 0, 0)), (dwc_ref, bdot(xv, dyv, 0, 0)),
               (dd_ref, jnp.sum(dyv * uv, 0, keepdims=True)),
               (q1_ref, jnp.concatenate(q1s, 1)), (q2_ref, jnp.concatenate(q2s, 1))]

        @pl.when(i == 0)
        def _():
            for ref, val in upd:
                ref[...] = val

        @pl.when(i != 0)
        def _():
            for ref, val in upd:
                ref[...] += val

    nb8 = s // SUBLANES
    rev = lambda w: pl.BlockSpec((t, w), lambda j, i: (nt - 1 - i, j))
    tab = pl.BlockSpec((t, S5_HALF), lambda j, i: (0, j))
    return pl.pallas_call(
        body, grid=(S5_NBLK, nt),
        in_specs=[rev(LANES), rev(LANES), rev(S5_BW),
                  pl.BlockSpec((SUBLANES, S5_BW), lambda j, i: (jnp.maximum((nt - 1 - i) * hb - 1, 0), j)),
                  pl.BlockSpec((None, LANES, S5_BW), lambda j, i: (j, 0, 0)),
                  pl.BlockSpec((None, S5_BW, LANES), lambda j, i: (j, 0, 0)),
                  tab, tab, tab, tab, pl.BlockSpec((1, LANES), lambda j, i: (0, j))],
        out_specs=[rev(LANES),
                   pl.BlockSpec((None, LANES, S5_BW), lambda j, i: (j, 0, 0)),
                   pl.BlockSpec((None, S5_BW, LANES), lambda j, i: (j, 0, 0)),
                   pl.BlockSpec((1, LANES), lambda j, i: (0, j)),
                   pl.BlockSpec((SUBLANES, S5_HALF), lambda j, i: (0, j)),
                   pl.BlockSpec((SUBLANES, S5_HALF), lambda j, i: (0, j))],
        out_shape=[jax.ShapeDtypeStruct((s, D_MODEL), f32),
                   jax.ShapeDtypeStruct((S5_NBLK, LANES, S5_BW), f32),
                   jax.ShapeDtypeStruct((S5_NBLK, S5_BW, LANES), f32),
                   jax.ShapeDtypeStruct((1, D_MODEL), f32),
                   jax.ShapeDtypeStruct((SUBLANES, S5_TABW), f32),
                   jax.ShapeDtypeStruct((SUBLANES, S5_TABW), f32)],
        scratch_shapes=[pltpu.VMEM((1, S5_BW), f32), pltpu.VMEM((t, S5_BW), f32)],
        compiler_params=_cparams("parallel", "arbitrary"),
        name="s5_core_bwd_" + tag)(u, dy, x, x, wb, wc, a1, a2, a1r, a2r, dskip)


def _gelu_tile(y):
    return (jax.nn.gelu(y),)


def s5_mixer_fwd(tag, u, prm, w_og):
    a_re, a_im, log_dt, b_re, b_im, c_re, c_im, dskip = prm
    disc_in = [a_re, a_im, log_dt.reshape(S5_GROUPS, 1), b_re.reshape(S5_GROUPS, -1), b_im.reshape(S5_GROUPS, -1)]
    abar_re, abar_im, bbar_re, bbar_im = rowmap("s5_disc_" + tag, _s5_disc_tile, disc_in, [_s5_expand()],
                                                [S5_STATE, S5_STATE, S5_STATE * S5_GROUP, S5_STATE * S5_GROUP], S5_GROUPS)
    del abar_re, abar_im
    a1, a2, a1r, a2r = s5_tables(a_re, a_im, log_dt)
    wb, wc = s5_pack_weights(bbar_re, bbar_im, c_re, c_im)
    wb, wc = wb.astype(bf16), wc.astype(bf16)
    y, x = s5_core_fwd(tag, u, wb, wc, a1, a2, dskip.reshape(1, D_MODEL))
    hid = rowmap("s5_gelu_" + tag, _gelu_tile, [y], [], [D_MODEL], TM_ROW, out_dtypes=[bf16])[0]
    og = mm_nn("s5_og_" + tag, hid, w_og)
    mix = rowmap("s5_glu_" + tag, _glu_tile, [og], [], [D_MODEL], TM_ROW)[0]
    return mix, (disc_in, a1, a2, a1r, a2r, wb, wc, x, y, hid, og)


def s5_mixer_bwd(tag, idx, u, prm, w_og, res, dmix, stacks):
    a_re, a_im, log_dt, b_re, b_im, c_re, c_im, dskip = prm
    disc_in, a1, a2, a1r, a2r, wb, wc, x, y, hid, og = res
    (dog,), _ = rowmap_bwd("s5_glu_bwd_" + tag, _glu_tile, [og], [], [dmix], TM_ROW)
    n_odd = DEPTH // 2
    dw_og = (mm_tn("s5_wo_dw_" + tag, hid, dog, n_cols=D_MODEL, stack=(stacks[0], idx, n_odd)),
             mm_tn("s5_wg_dw_" + tag, hid, dog, b_col0=D_MODEL, n_cols=D_MODEL, stack=(stacks[1], idx, n_odd)))
    dhid = mm_nt("s5_og_dx_" + tag, dog, w_og)
    (dy,), _ = rowmap_bwd("s5_gelu_bwd_" + tag, _gelu_tile, [y], [], [dhid], TM_ROW)
    du, dwb, dwc, ddskip, q1, q2 = s5_core_bwd(tag, u, x, wb, wc, a1, a2, a1r, a2r, dskip.reshape(1, D_MODEL), dy)
    dbbar_re, dbbar_im, dc_re, dc_im = s5_unpack_weight_grads(dwb, dwc)
    dabar_re = q1.sum(0).reshape(S5_GROUPS, S5_STATE)
    dabar_im = q2.sum(0).reshape(S5_GROUPS, S5_STATE)
    grads, _ = rowmap_bwd("s5_disc_bwd_" + tag, _s5_disc_tile, disc_in, [_s5_expand()],
                          [dabar_re, dabar_im, dbbar_re, dbbar_im], S5_GROUPS, par_mask=[False])
    da_re, da_im, dlog_dt, db_re, db_im = grads
    return du, (da_re, da_im, dlog_dt.reshape(S5_GROUPS), db_re.reshape(b_re.shape), db_im.reshape(b_im.shape),
                dc_re, dc_im, ddskip.reshape(D_MODEL)), dw_og


HYB_IN = 3592
_IN_B0, _IN_SW0 = 2048, 2056


IN_SHARD = HYB_IN // 4
SHARD_ORDER_GRADS = ("hyb_w_in", "ffn_wg", "ffn_wu", "ffn_wd")
FFN_TRANSPOSED = ("ffn_wg", "ffn_wu")
BIG_SHARDED = ("hyb_w_in", "hyb_w_out", "s5_glu_wo", "s5_glu_wg", "xq_w", "xk_w", "xv_w", "xo_w", "ffn_wg", "ffn_wu", "ffn_wd")


def _w_in_pieces():
    runs = [(0, _IN_B0, 0), (_IN_B0, _IN_SW0, COL_BA), (_IN_SW0, HYB_IN, _IN_B0)]
    out = []
    for sh in range(4):
        lo, hi = sh * IN_SHARD, (sh + 1) * IN_SHARD
        for r_lo, r_hi, c_lo in runs:
            a, b = max(lo, r_lo), min(hi, r_hi)
            if a < b:
                out.append((sh, a - lo, b - lo, c_lo + a - r_lo))
    return out


def w_in_to_canonical(tag, layer, w4):
    tr = 128

    def body(w_ref, o_ref):
        o_ref[:, COL_BA:] = jnp.zeros((tr, BA_PAD), o_ref.dtype)
        for sh, a, b, c in _w_in_pieces():
            o_ref[:, c:c + b - a] = w_ref[sh, :, a:b]

    return pl.pallas_call(
        body, grid=(D_MODEL // tr,),
        in_specs=[pl.BlockSpec((4, None, tr, IN_SHARD), lambda i: (0, layer, i, 0))],
        out_specs=pl.BlockSpec((tr, PROJ_COLS), lambda i: (i, 0)),
        out_shape=jax.ShapeDtypeStruct((D_MODEL, PROJ_COLS), w4.dtype),
        compiler_params=_cparams("parallel"), name="w_in_canon_" + tag)(w4)


def w_in_grad_to_shards(tag, layer, g, stack, n_layers):
    tr = 128
    extra, extra_specs, _ = _stacked(stack)

    def body(g_ref, *rest):
        o_ref = rest[-1]
        for sh, a, b, c in _w_in_pieces():
            o_ref[sh, :, a:b] = g_ref[:, c:c + b - a]

    return pl.pallas_call(
        body, grid=(D_MODEL // tr,),
        in_specs=[pl.BlockSpec((tr, PROJ_COLS), lambda i: (i, 0))] + extra_specs,
        out_specs=pl.BlockSpec((4, None, tr, IN_SHARD), lambda i: (0, layer, i, 0)),
        out_shape=jax.ShapeDtypeStruct((4, n_layers, D_MODEL, IN_SHARD), f32),
        input_output_aliases={1: 0} if extra else {},
        compiler_params=_cparams("parallel"), name="w_in_grad_shards_" + tag)(g, *extra)


def _add2(name, a, b):
    return rowmap(name, lambda p, q: (p + q,), [a, b], [], [a.shape[1]], _pick(a.shape[0], (256, 128, 64, 32, 16, 8)))[0]


def local_step(x, mem, positions, target, p):
    s = x.shape[0]
    cos, sin = rope_tables(positions, s)
    row = lambda v: v.reshape(1, -1).astype(f32)
    wg4, wu4, wd4 = (p[n].astype(bf16) for n in ("ffn_wg", "ffn_wu", "ffn_wd"))
    h = h16 = x
    tape = []
    for l in range(DEPTH):
        i, tag = l // 2, str(l)
        t = {"h0": h, "h0_16": h16}
        t["ln"] = [(row(p[g][l]), row(p[b][l])) for g, b in
                   (("ln_mix_g", "ln_mix_b"), ("ln_x_g", "ln_x_b"), ("ln_ffn_g", "ln_ffn_b"))]
        if l % 2 == 0:
            t["w_in"] = w_in_to_canonical(tag, i, p["hyb_w_in"].astype(bf16))
            t["w_out"] = p["hyb_w_out"][i].astype(bf16)
            t["dn_prm"] = (p["dn_conv_w"][i].astype(f32), row(jnp.repeat(p["dn_a_log"][i], DN_HEAD_DIM)),
                           row(jnp.repeat(p["dn_dt_bias"][i], DN_HEAD_DIM)), row(jnp.tile(p["dn_norm_g"][i], DN_HEADS)))
            t["proj"] = mm_nn("hyb_in_" + tag, h16, t["w_in"])
            a_out, t["dn"] = dn_mixer_fwd(tag, t["proj"], *t["dn_prm"])
            b_out, t["swa"] = swa_mixer_fwd(tag, t["proj"], cos, sin)
            t["mixed"] = jnp.concatenate([a_out, b_out], 1)
            mix, t["h1"], t["h1_16"] = mm_nn("hyb_out_" + tag, t["mixed"], t["w_out"], postnorm=(h, *t["ln"][0]))
        else:
            t["s5_prm"] = tuple(p[n][i].astype(f32) for n in
                                ("s5_a_re", "s5_a_im", "s5_log_dt", "s5_b_re", "s5_b_im", "s5_c_re", "s5_c_im", "s5_d"))
            t["w_og"] = jnp.concatenate([p["s5_glu_wo"][i], p["s5_glu_wg"][i]], 1).astype(bf16)
            mix, t["s5"] = s5_mixer_fwd(tag, h, t["s5_prm"], t["w_og"])
            t["h1"], t["h1_16"] = postnorm_fwd("mix" + tag, h, mix, *t["ln"][0])
        t["mix"] = mix
        t["wq"], t["wo"] = p["xq_w"][l].astype(bf16), p["xo_w"][l].astype(bf16)
        t["wkv"] = jnp.concatenate([p["xk_w"][l], p["xv_w"][l]], 1).astype(bf16)
        (t["xo"], t["h2"], t["h2_16"]), t["xres"] = xattn_fwd(tag, t["h1_16"], mem, t["wq"], t["wkv"], t["wo"],
                                                              (t["h1"], *t["ln"][1]))
        (t["fo"], h, h16), t["fres"] = ffn_fwd(tag, l, t["h2_16"], wg4, wu4, wd4, (t["h2"], *t["ln"][2]))
        tape.append(t)

    part, dh = loss_head(h, target)
    loss = jnp.sum(part)

    g = {n: [None] * v.shape[0] for n, v in p.items() if n not in BIG_SHARDED}
    st = {n: None for n in BIG_SHARDED}
    for l in reversed(range(DEPTH)):
        i, tag, t = l // 2, str(l), tape[l]
        dh2a, dfo, dg, db = postnorm_bwd("ffn" + tag, t["h2"], t["fo"], *t["ln"][2], dh)
        g["ln_ffn_g"][l], g["ln_ffn_b"][l] = dg[0], db[0]
        ffn_names = ("ffn_wg", "ffn_wu", "ffn_wd")
        prev = None if st["ffn_wd"] is None else [st[n] for n in ffn_names]
        dh2b, new = ffn_bwd(tag, l, t["h2_16"], wg4, wu4, wd4, t["fres"], dfo, prev)
        st.update(zip(ffn_names, new))
        dh1a, dxo, dg, db = postnorm_bwd("x" + tag, t["h1"], t["xo"], *t["ln"][1], [dh2a, dh2b])
        g["ln_x_g"][l], g["ln_x_b"][l] = dg[0], db[0]
        x_names = ("xq_w", "xk_w", "xv_w", "xo_w")
        dh1b, new = xattn_bwd(tag, l, t["h1_16"], mem, t["wq"], t["wkv"], t["wo"], t["xres"], dxo, [st[n] for n in x_names])
        st.update(zip(x_names, new))
        dh0a, dmix, dg, db = postnorm_bwd("mix" + tag, t["h0"], t["mix"], *t["ln"][0], [dh1a, dh1b])
        g["ln_mix_g"][l], g["ln_mix_b"][l] = dg[0], db[0]
        if l % 2 == 0:
            st["hyb_w_out"] = mm_tn("hyb_out_dw_" + tag, t["mixed"], dmix, stack=(st["hyb_w_out"], i, DEPTH // 2))
            dmixed = mm_nt("hyb_out_dx_" + tag, dmix, t["w_out"])
            dqkv, dz, dba, dcw, dalog, ddtb, dng = dn_mixer_bwd(tag, t["proj"], *t["dn_prm"], t["dn"], (dmixed, DN_KEY_DIM, 0))
            g["dn_conv_w"][i] = dcw
            g["dn_a_log"][i] = dalog.reshape(DN_HEADS, DN_HEAD_DIM).sum(1)
            g["dn_dt_bias"][i] = ddtb.reshape(DN_HEADS, DN_HEAD_DIM).sum(1)
            g["dn_norm_g"][i] = dng.reshape(DN_HEADS, DN_HEAD_DIM).sum(0)
            dq, dk, dv = swa_mixer_bwd(tag, cos, sin, t["swa"], (dmixed, SW_DIM, 1))
            dproj = jnp.concatenate([dqkv, dz, dq, dk, dv, dba], 1)
            st["hyb_w_in"] = w_in_grad_to_shards(tag, i, mm_tn("hyb_in_dw_" + tag, t["h0_16"], dproj), st["hyb_w_in"], DEPTH // 2)
            dh0b = mm_nt("hyb_in_dx_" + tag, dproj, t["w_in"])
        else:
            dh0b, dprm, (st["s5_glu_wo"], st["s5_glu_wg"]) = s5_mixer_bwd(
                tag, i, t["h0"], t["s5_prm"], t["w_og"], t["s5"], dmix, (st["s5_glu_wo"], st["s5_glu_wg"]))
            for n, v in zip(("s5_a_re", "s5_a_im", "s5_log_dt", "s5_b_re", "s5_b_im", "s5_c_re", "s5_c_im", "s5_d"), dprm):
                g[n][i] = v
        dh = [dh0a, dh0b]
    grad_x = _add2("grad_x", dh[0], dh[1])
    grads = {n: jnp.stack(v) for n, v in g.items()}
    grads.update(st)
    return loss, grad_x, grads


WEIGHT_NAMES = ("hyb_w_in", "dn_conv_w", "dn_a_log", "dn_dt_bias", "dn_norm_g", "hyb_w_out", "s5_a_re", "s5_a_im",
                "s5_log_dt", "s5_b_re", "s5_b_im", "s5_c_re", "s5_c_im", "s5_d", "s5_glu_wo", "s5_glu_wg",
                "ln_mix_g", "ln_mix_b", "xq_w", "xk_w", "xv_w", "xo_w", "ln_x_g", "ln_x_b",
                "ffn_wg", "ffn_wu", "ffn_wd", "ln_ffn_g", "ln_ffn_b")
SHARD_AXIS = {"hyb_w_in": 2, "dn_conv_w": 2, "hyb_w_out": 1, "s5_d": 1, "s5_glu_wo": 1, "s5_glu_wg": 1,
              "xq_w": 1, "xk_w": 1, "xv_w": 1, "xo_w": 1, "ffn_wg": 2, "ffn_wu": 2, "ffn_wd": 1}
GATHER_F32 = ("dn_conv_w", "s5_d")
N_CHIPS = 4
PACK_COLS = 1024
_ANY = pl.BlockSpec(memory_space=pl.ANY)


def _pos():
    return lax.axis_index("x"), lax.axis_index("y"), lax.axis_index("c")


def _chip_peers(mx, my):
    return [(1 - mx, my), (mx, 1 - my), (1 - mx, 1 - my)]


def _rcopy(src, dst, ssem, rsem, dev):
    return pltpu.make_async_remote_copy(src_ref=src, dst_ref=dst, send_sem=ssem, recv_sem=rsem,
                                        device_id=dev, device_id_type=pl.DeviceIdType.MESH)


def comm_allgather4(name, x):
    def body(x_ref, o_ref, ssem, rsem, lsem):
        mx, my, mc = _pos()
        me = 2 * mx + my
        peers = _chip_peers(mx, my)
        loc = pltpu.make_async_copy(x_ref, o_ref.at[me], lsem)
        loc.start()
        sends = [_rcopy(x_ref, o_ref.at[me], ssem.at[k], rsem.at[k], (px, py, mc)) for k, (px, py) in enumerate(peers)]
        for cp in sends:
            cp.start()
        for k, (px, py) in enumerate(peers):
            _rcopy(x_ref, o_ref.at[2 * px + py], ssem.at[k], rsem.at[k], (px, py, mc)).wait_recv()
        for cp in sends:
            cp.wait_send()
        loc.wait()

    return pl.pallas_call(
        body, out_shape=jax.ShapeDtypeStruct((N_CHIPS,) + x.shape, x.dtype), in_specs=[_ANY], out_specs=_ANY,
        scratch_shapes=[pltpu.SemaphoreType.DMA((3,)), pltpu.SemaphoreType.DMA((3,)), pltpu.SemaphoreType.DMA],
        name=name)(x)


def _multi_call(name, body, ins, out_shapes, sems, in_place=False):
    return pl.pallas_call(
        body, out_shape=out_shapes, in_specs=[_ANY] * len(ins), out_specs=[_ANY] * len(out_shapes),
        scratch_shapes=sems, input_output_aliases={w: w for w in range(len(ins))} if in_place else {},
        name=name)(*ins)


def comm_gather_weights(name, slots):
    n = len(slots)

    def body(*refs):
        os_ = refs[n:2 * n]
        ssem, rsem, fssem, frsem = refs[2 * n:]
        mx, my, mc = _pos()
        me = 2 * mx + my
        peers = _chip_peers(mx, my)
        sib = (mx, my, 1 - mc)
        half = [o.shape[1] // 2 for o in os_]
        mine = [pl.ds(mc * h, h) for h in half]
        other = [pl.ds((1 - mc) * h, h) for h in half]
        sends = [_rcopy(os_[w].at[me, mine[w]], os_[w].at[me, mine[w]], ssem.at[w, k], rsem.at[w, k], (px, py, mc))
                 for w in range(n) for k, (px, py) in enumerate(peers)]
        for cp in sends:
            cp.start()
        fwds = []
        for w in range(n):
            for k, (px, py) in enumerate(peers):
                landed = os_[w].at[2 * px + py, mine[w]]
                _rcopy(landed, landed, ssem.at[w, k], rsem.at[w, k], (px, py, mc)).wait_recv()
                fw = _rcopy(landed, landed, fssem.at[w, k], frsem.at[w, k], sib)
                fw.start()
                fwds.append(fw)
        for w in range(n):
            for k, (px, py) in enumerate(peers):
                theirs = os_[w].at[2 * px + py, other[w]]
                _rcopy(theirs, theirs, fssem.at[w, k], frsem.at[w, k], sib).wait_recv()
        for cp in sends + fwds:
            cp.wait_send()

    dma = pltpu.SemaphoreType.DMA
    return _multi_call(name, body, slots, [jax.ShapeDtypeStruct(x.shape, x.dtype) for x in slots],
                       [dma((n, 3)), dma((n, 3)), dma((n, 3)), dma((n, 3))], in_place=True)


def comm_sibling_halves(name, gs):
    n = len(gs)

    def body(*refs):
        xs, os_ = refs[:n], refs[n:2 * n]
        ssem, rsem = refs[2 * n:]
        mx, my, mc = _pos()
        sib = (mx, my, 1 - mc)
        sends = []
        for w in range(n):
            h = xs[w].shape[1] // 2
            for j in range(N_CHIPS):
                sends.append(_rcopy(xs[w].at[j, pl.ds((1 - mc) * h, h)], os_[w].at[j], ssem.at[w, j], rsem.at[w, j], sib))
        for cp in sends:
            cp.start()
        for w in range(n):
            for j in range(N_CHIPS):
                _rcopy(os_[w].at[j], os_[w].at[j], ssem.at[w, j], rsem.at[w, j], sib).wait_recv()
        for cp in sends:
            cp.wait_send()

    dma = pltpu.SemaphoreType.DMA
    return _multi_call(name, body, gs,
                       [jax.ShapeDtypeStruct((N_CHIPS, g.shape[1] // 2) + g.shape[2:], g.dtype) for g in gs],
                       [dma((n, N_CHIPS)), dma((n, N_CHIPS))])


def comm_alltoall4(name, xs):
    n = len(xs)

    def body(*refs):
        xr, os_ = refs[:n], refs[n:2 * n]
        ssem, rsem = refs[2 * n:]
        mx, my, mc = _pos()
        me = 2 * mx + my
        peers = _chip_peers(mx, my)
        sends = [_rcopy(xr[w].at[2 * px + py], os_[w].at[me], ssem.at[w, k], rsem.at[w, k], (px, py, mc))
                 for w in range(n) for k, (px, py) in enumerate(peers)]
        for cp in sends:
            cp.start()
        for w in range(n):
            for k, (px, py) in enumerate(peers):
                dst = os_[w].at[2 * px + py]
                _rcopy(dst, dst, ssem.at[w, k], rsem.at[w, k], (px, py, mc)).wait_recv()
        for cp in sends:
            cp.wait_send()

    dma = pltpu.SemaphoreType.DMA
    return _multi_call(name, body, xs, [jax.ShapeDtypeStruct(x.shape, x.dtype) for x in xs], [dma((n, 3)), dma((n, 3))])


def comm_sibling_join(name, bs):
    n = len(bs)

    def body(*refs):
        os_ = refs[n:2 * n]
        ssem, rsem = refs[2 * n:]
        mx, my, mc = _pos()
        sib = (mx, my, 1 - mc)
        sends = [_rcopy(os_[w].at[mc], os_[w].at[mc], ssem.at[w], rsem.at[w], sib) for w in range(n)]
        for cp in sends:
            cp.start()
        for w in range(n):
            dst = os_[w].at[1 - mc]
            _rcopy(dst, dst, ssem.at[w], rsem.at[w], sib).wait_recv()
        for cp in sends:
            cp.wait_send()

    dma = pltpu.SemaphoreType.DMA
    return _multi_call(name, body, bs, [jax.ShapeDtypeStruct(b.shape, b.dtype) for b in bs], [dma((n,)), dma((n,))],
                       in_place=True)


def comm_sibling_swap(name, x):
    def body(x_ref, o_ref, ssem, rsem):
        mx, my, mc = _pos()
        cp = _rcopy(x_ref, o_ref, ssem, rsem, (mx, my, 1 - mc))
        cp.start()
        cp.wait_recv()
        cp.wait_send()

    return pl.pallas_call(
        body, out_shape=jax.ShapeDtypeStruct(x.shape, x.dtype), in_specs=[_ANY], out_specs=_ANY,
        scratch_shapes=[pltpu.SemaphoreType.DMA, pltpu.SemaphoreType.DMA], name=name)(x)


def _row_tile(r):
    return _pick(r, (256, 128, 64, 32, 16, 8))


def add_own_half(name, g, recv, out_dtype):
    r, c = g.shape[2:]
    tr = _row_tile(r)
    mc = lax.axis_index("c").astype(jnp.int32).reshape(1)

    def body(c_ref, g_ref, r_ref, o_ref):
        o_ref[...] = (g_ref[...] + r_ref[...]).astype(o_ref.dtype)

    grid_spec = pltpu.PrefetchScalarGridSpec(
        num_scalar_prefetch=1, grid=(N_CHIPS, r // tr),
        in_specs=[pl.BlockSpec((None, None, tr, c), lambda j, i, cr: (j, cr[0], i, 0)),
                  pl.BlockSpec((None, tr, c), lambda j, i, cr: (j, i, 0))],
        out_specs=pl.BlockSpec((None, tr, c), lambda j, i, cr: (j, i, 0)))
    return pl.pallas_call(body, grid_spec=grid_spec, out_shape=jax.ShapeDtypeStruct(recv.shape, out_dtype),
                          compiler_params=_cparams("parallel", "parallel"), name=name)(mc, g, recv)


def cast_into_slot(name, w, chip, dtype):
    r, c = w.shape
    tr = _row_tile(r)

    def body(c_ref, w_ref, o_ref):
        o_ref[...] = w_ref[...].astype(o_ref.dtype)

    grid_spec = pltpu.PrefetchScalarGridSpec(
        num_scalar_prefetch=1, grid=(r // tr,),
        in_specs=[pl.BlockSpec((tr, c), lambda i, cr: (i, 0))],
        out_specs=pl.BlockSpec((None, tr, c), lambda i, cr: (cr[0], i, 0)))
    return pl.pallas_call(body, grid_spec=grid_spec, out_shape=jax.ShapeDtypeStruct((N_CHIPS, r, c), dtype),
                          compiler_params=_cparams("parallel"), name=name)(chip.astype(jnp.int32).reshape(1), w)


def sum_chips_into_half(name, own, arrived, chip, mc):
    r, c = own.shape[1:]
    tr = _row_tile(r)

    def body(s0, s1, s2, s3, s4, own_ref, a_ref, b_ref, d_ref, o_ref):
        o_ref[...] = ((own_ref[...].astype(f32) + a_ref[...].astype(f32))
                      + (b_ref[...].astype(f32) + d_ref[...].astype(f32)))

    slot = lambda k: pl.BlockSpec((None, tr, c), lambda i, *sc, _k=k: (sc[_k][0], i, 0))
    grid_spec = pltpu.PrefetchScalarGridSpec(
        num_scalar_prefetch=5, grid=(r // tr,), in_specs=[slot(0), slot(1), slot(2), slot(3)],
        out_specs=pl.BlockSpec((None, tr, c), lambda i, *sc: (sc[4][0], i, 0)))
    mx, my = lax.axis_index("x"), lax.axis_index("y")
    scal = [v.astype(jnp.int32).reshape(1) for v in
            (2 * mx + my, 2 * (1 - mx) + my, 2 * mx + (1 - my), 2 * (1 - mx) + (1 - my), mc)]
    return pl.pallas_call(body, grid_spec=grid_spec, out_shape=jax.ShapeDtypeStruct((2, r, c), f32),
                          compiler_params=_cparams("parallel"), name=name)(*scal, own, arrived, arrived, arrived)


def sum_slots(name, x):
    r, c = x.shape[1:]
    tr = _row_tile(r)

    def body(x_ref, o_ref):
        o_ref[...] = (x_ref[0].astype(f32) + x_ref[1].astype(f32)) + (x_ref[2].astype(f32) + x_ref[3].astype(f32))

    return pl.pallas_call(
        body, grid=(r // tr,), in_specs=[pl.BlockSpec((N_CHIPS, tr, c), lambda i: (0, i, 0))],
        out_specs=pl.BlockSpec((tr, c), lambda i: (i, 0)), out_shape=jax.ShapeDtypeStruct((r, c), f32),
        compiler_params=_cparams("parallel"), name=name)(x)


def adamw(name, w, g, m, v):
    r, c = w.shape
    tr = _row_tile(r)

    def body(w_ref, g_ref, m_ref, v_ref, d_ref, nm_ref, nv_ref):
        gv = g_ref[...]
        nm = ADAM_B1 * m_ref[...] + (1.0 - ADAM_B1) * gv
        nv = ADAM_B2 * v_ref[...] + (1.0 - ADAM_B2) * (gv * gv)
        m_hat = nm / (1.0 - ADAM_B1 ** ADAM_STEP)
        v_hat = nv / (1.0 - ADAM_B2 ** ADAM_STEP)
        d_ref[...] = -ADAM_LR * (m_hat / (jnp.sqrt(v_hat) + ADAM_EPS) + ADAM_WD * w_ref[...])
        nm_ref[...] = nm
        nv_ref[...] = nv

    blk = pl.BlockSpec((tr, c), lambda i: (i, 0))
    return pl.pallas_call(
        body, grid=(r // tr,), in_specs=[blk] * 4, out_specs=[blk] * 3,
        out_shape=[jax.ShapeDtypeStruct((r, c), f32)] * 3,
        compiler_params=_cparams("parallel"), name=name)(w, g, m, v)


def _pack_rows(n):
    return -(-n // PACK_COLS)


def _pack(arrs, dtype, row_multiple):
    segs = []
    for a in arrs:
        flat = a.astype(dtype).reshape(-1)
        k = _pack_rows(flat.shape[0])
        segs.append(jnp.pad(flat, (0, k * PACK_COLS - flat.shape[0])).reshape(k, PACK_COLS))
    rows = sum(s.shape[0] for s in segs)
    pad = -rows % row_multiple
    if pad:
        segs.append(jnp.zeros((pad, PACK_COLS), dtype))
    return jnp.concatenate(segs, 0)


def _unpack(packed, shapes):
    out, r = [], 0
    for shp in shapes:
        n = math.prod(shp)
        k = _pack_rows(n)
        out.append(packed[r:r + k].reshape(-1)[:n].reshape(shp))
        r += k
    return out


def _gathered_to_full(g, axis):
    t = jnp.moveaxis(g, 0, axis)
    return t.reshape(t.shape[:axis] + (t.shape[axis] * t.shape[axis + 1],) + t.shape[axis + 2:])


def _full_to_shard_major(full, axis):
    shp = full.shape
    t = full.reshape(shp[:axis] + (N_CHIPS, shp[axis] // N_CHIPS) + shp[axis + 1:])
    return jnp.moveaxis(t, axis, 0)


def kernel(x, mem, positions, hyb_w_in, dn_conv_w, dn_a_log, dn_dt_bias, dn_norm_g, hyb_w_out, s5_a_re, s5_a_im, s5_log_dt, s5_b_re, s5_b_im, s5_c_re, s5_c_im, s5_d, s5_glu_wo, s5_glu_wg, ln_mix_g, ln_mix_b, xq_w, xk_w, xv_w, xo_w, ln_x_g, ln_x_b, ffn_wg, ffn_wu, ffn_wd, ln_ffn_g, ln_ffn_b, loss_target, m_hyb_w_in, m_dn_conv_w, m_dn_a_log, m_dn_dt_bias, m_dn_norm_g, m_hyb_w_out, m_s5_a_re, m_s5_a_im, m_s5_log_dt, m_s5_b_re, m_s5_b_im, m_s5_c_re, m_s5_c_im, m_s5_d, m_s5_glu_wo, m_s5_glu_wg, m_ln_mix_g, m_ln_mix_b, m_xq_w, m_xk_w, m_xv_w, m_xo_w, m_ln_x_g, m_ln_x_b, m_ffn_wg, m_ffn_wu, m_ffn_wd, m_ln_ffn_g, m_ln_ffn_b, v_hyb_w_in, v_dn_conv_w, v_dn_a_log, v_dn_dt_bias, v_dn_norm_g, v_hyb_w_out, v_s5_a_re, v_s5_a_im, v_s5_log_dt, v_s5_b_re, v_s5_b_im, v_s5_c_re, v_s5_c_im, v_s5_d, v_s5_glu_wo, v_s5_glu_wg, v_ln_mix_g, v_ln_mix_b, v_xq_w, v_xk_w, v_xv_w, v_xo_w, v_ln_x_g, v_ln_x_b, v_ffn_wg, v_ffn_wu, v_ffn_wd, v_ln_ffn_g, v_ln_ffn_b):
    a = dict(locals())
    big = [n for n in WEIGHT_NAMES if n in SHARD_AXIS and n not in GATHER_F32]
    small = [n for n in WEIGHT_NAMES if n not in big]
    chip = 2 * lax.axis_index("x") + lax.axis_index("y")
    for n in FFN_TRANSPOSED:
        for pre in ("", "m_", "v_"):
            a[pre + n] = jnp.swapaxes(a[pre + n], 1, 2)

    mc = lax.axis_index("c")
    view2 = lambda t: t.reshape(-1, t.shape[-1])
    slots = [cast_into_slot("slot_" + n, view2(a[n]), chip, bf16).reshape((N_CHIPS,) + a[n].shape) for n in big]
    gathered = comm_gather_weights("gather_w", slots)
    tiny4 = _unpack_slots(comm_allgather4("gather_w_tiny", _pack([a[n] for n in GATHER_F32], f32, 8)),
                          [a[n].shape for n in GATHER_F32])
    p = {n: a[n] for n in small if n not in GATHER_F32}
    for n, g4 in zip(GATHER_F32, tiny4):
        p[n] = _gathered_to_full(g4, SHARD_AXIS[n])
    for n, g4 in zip(big, gathered):
        p[n] = g4 if n in SHARD_ORDER_GRADS else _gathered_to_full(g4, SHARD_AXIS[n])

    loss, grad_x, grads = local_step(x[0], mem[0], positions, loss_target[0], p)
    loss = lax.psum(loss, ("x", "y", "c"))

    g4s = [grads[n] for n in big]
    recv = comm_sibling_halves("rs_sibling_halves", g4s)
    pairs = []
    for n, g4, r4 in zip(big, g4s, recv):
        lh, cols = g4.shape[1] // 2, g4.shape[-1]
        v4 = g4.reshape(N_CHIPS, 2, -1, cols)
        pairs.append(add_own_half("rs_add_" + n, v4, r4.reshape(N_CHIPS, -1, cols), bf16).reshape((N_CHIPS, lh) + g4.shape[2:]))
    arrived = comm_alltoall4("rs_alltoall", pairs)
    slot3 = lambda t: t.reshape(N_CHIPS, -1, t.shape[-1])
    halves = [sum_chips_into_half("rs_sum_" + n, slot3(pr), slot3(ar), chip, mc) for n, pr, ar in zip(big, pairs, arrived)]
    g_big = {n: t.reshape(a[n].shape) for n, t in zip(big, comm_sibling_join("rs_sibling_join", halves))}

    rpack = _pack([grads[n] for n in small], f32, 8)
    rpair = _add2("ar_add_sibling", rpack, comm_sibling_swap("ar_sibling_swap", rpack))
    g_small = _unpack(sum_slots("ar_sum_chips", comm_allgather4("ar_allgather", rpair)), [grads[n].shape for n in small])
    g_small = {n: (lax.dynamic_index_in_dim(_full_to_shard_major(g, SHARD_AXIS[n]), chip, 0, keepdims=False)
                   if n in SHARD_AXIS else g) for n, g in zip(small, g_small)}

    outs = {}
    for n in big:
        view = lambda t: t.reshape(-1, t.shape[-1])
        d, nm, nv = adamw("adamw_" + n, view(a[n]), view(g_big[n]), view(a["m_" + n]), view(a["v_" + n]))
        outs[n] = (g_big[n],) + tuple(t.reshape(a[n].shape) for t in (d, nm, nv))
    shapes = [a[n].shape for n in small]
    packs = [_pack([a[pre + n] for n in small], f32, 8) for pre in ("", "m_", "v_")]
    upd = adamw("adamw_small", packs[0], _pack([g_small[n] for n in small], f32, 8), packs[1], packs[2])
    for k, n in enumerate(small):
        outs[n] = (g_small[n],) + tuple(_unpack(buf, shapes)[k] for buf in upd)
    for n in FFN_TRANSPOSED:
        outs[n] = tuple(jnp.swapaxes(t, 1, 2) for t in outs[n])
    res = [loss, grad_x[None]]
    for kind in range(4):
        res += [outs[n][kind] for n in WEIGHT_NAMES]
    return tuple(res)


def _unpack_slots(gathered, shapes):
    out, r = [], 0
    for shp in shapes:
        n = math.prod(shp)
        k = _pack_rows(n)
        out.append(gathered[:, r:r + k].reshape(N_CHIPS, -1)[:, :n].reshape((N_CHIPS,) + tuple(shp)))
        r += k
    return out
```
